```python
import jax, jax.numpy as jnp
from jax import lax
import numpy as np

D_MODEL = 2048
BATCH = 8
SEQ = 2048
DEPTH = 1

CHUNK = 64
N_PAST_CHUNKS = 8
BAND = (N_PAST_CHUNKS + 1) * CHUNK
ATTN_WIDTH = D_MODEL // 2
ATTN_HEAD_DIM = 64
ATTN_HEADS = ATTN_WIDTH // ATTN_HEAD_DIM
MAX_REL = 4 * CHUNK
REC_WIDTH = D_MODEL - ATTN_WIDTH
REC_HEAD_DIM = 128
REC_HEADS = REC_WIDTH // REC_HEAD_DIM
MIX_WIDTH = ATTN_WIDTH + REC_WIDTH
IN_PROJ_WIDTH = 3 * ATTN_WIDTH + 4 * REC_WIDTH
D_FF = ((8 * D_MODEL // 3 + 255) // 256) * 256
ALPHA = (2 * DEPTH) ** 0.25
BETA = (8 * DEPTH) ** -0.25
EPS = 1e-5
N_MOD = 6

kernel_name = "hybrid_chunkattn_hgrn2_deepnorm_adaln"


def _layernorm(x, g=None, b=None):
    xf = x.astype(jnp.float32)
    mu = jnp.mean(xf, axis=-1, keepdims=True)
    var = jnp.mean(jnp.square(xf - mu), axis=-1, keepdims=True)
    y = (xf - mu) * lax.rsqrt(var + EPS)
    if g is not None:
        y = y * g.astype(jnp.float32) + b.astype(jnp.float32)
    return y.astype(x.dtype)


def _rmsnorm(x, g):
    xf = x.astype(jnp.float32)
    y = xf * lax.rsqrt(jnp.mean(jnp.square(xf), axis=-1, keepdims=True) + EPS)
    return y * g.astype(jnp.float32)


def _chunk_attention(q, k, v, rel_bias):
    B, T, H, Dh = q.shape
    n_chunks = T // CHUNK
    pad = N_PAST_CHUNKS * CHUNK
    k_pad = jnp.pad(k, ((0, 0), (pad, 0), (0, 0), (0, 0)))
    v_pad = jnp.pad(v, ((0, 0), (pad, 0), (0, 0), (0, 0)))
    rel = jnp.arange(CHUNK)[:, None] + pad - jnp.arange(BAND)[None, :]
    idx = jnp.clip(rel, -MAX_REL, MAX_REL) + MAX_REL
    bias = rel_bias[:, idx].astype(jnp.float32)
    q_chunks = q.reshape(B, n_chunks, CHUNK, H, Dh).transpose(1, 0, 2, 3, 4)
    scale = Dh ** -0.5
    band_pos = jnp.arange(BAND)

    def one_chunk(args):
        n, qc = args
        kb = lax.dynamic_slice_in_dim(k_pad, n * CHUNK, BAND, axis=1)
        vb = lax.dynamic_slice_in_dim(v_pad, n * CHUNK, BAND, axis=1)
        s = jnp.einsum('bthd,bjhd->bhtj', qc, kb).astype(jnp.float32) * scale + bias
        valid = band_pos >= (N_PAST_CHUNKS - n) * CHUNK
        s = jnp.where(valid, s, -jnp.inf)
        p = jax.nn.softmax(s, axis=-1).astype(vb.dtype)
        return jnp.einsum('bhtj,bjhd->bthd', p, vb)

    out = lax.map(one_chunk, (jnp.arange(n_chunks), q_chunks))
    return out.transpose(1, 0, 2, 3, 4).reshape(B, T, H, Dh)


def _hgrn2(q, f_logit, i, lower_bound):
    B, T, H, Dk = q.shape
    lb = lower_bound.reshape(H, Dk).astype(jnp.float32)
    f = lb + (1.0 - lb) * jax.nn.sigmoid(f_logit.astype(jnp.float32))
    log_f = jnp.log(f)
    k = 1.0 - f
    q = jax.nn.silu(q.astype(jnp.float32))
    i = i.astype(jnp.float32)
    Dv = i.shape[-1]
    n_chunks = T // CHUNK

    def to_chunks(a):
        return a.reshape(B, n_chunks, CHUNK, H, a.shape[-1]).transpose(1, 0, 3, 2, 4)

    causal = jnp.tril(jnp.ones((CHUNK, CHUNK), dtype=bool))[:, :, None]

    def step(S, inp):
        qc, kc, ic, gc = inp
        b = jnp.cumsum(gc, axis=2)
        diff = b[:, :, :, None, :] - b[:, :, None, :, :]
        decay = jnp.exp(jnp.where(causal, diff, -jnp.inf))
        scores = jnp.einsum('bhtd,bhtsd,bhsd->bhts', qc, decay, kc)
        o = (jnp.einsum('bhts,bhse->bhte', scores, ic)
             + jnp.einsum('bhtd,bhde->bhte', qc * jnp.exp(b), S))
        b_last = b[:, :, -1:, :]
        S = (jnp.exp(b_last[:, :, 0, :, None]) * S
             + jnp.einsum('bhsd,bhse->bhde', kc * jnp.exp(b_last - b), ic))
        return S, o

    S0 = jnp.zeros((B, H, Dk, Dv), jnp.float32)
    _, o = lax.scan(step, S0, (to_chunks(q), to_chunks(k), to_chunks(i), to_chunks(log_f)))
    return o.transpose(1, 0, 3, 2, 4).reshape(B, T, H, Dv)


def _token_mixer(h, w_in, rel_bias, attn_gain, lower_bound, gnorm_gain, w_o):
    B, T, _ = h.shape
    proj = h @ w_in
    splits = [ATTN_WIDTH, 2 * ATTN_WIDTH, 3 * ATTN_WIDTH,
              3 * ATTN_WIDTH + REC_WIDTH, 3 * ATTN_WIDTH + 2 * REC_WIDTH,
              3 * ATTN_WIDTH + 3 * REC_WIDTH]
    q_a, k_a, v_a, q_b, f_b, i_b, g_b = jnp.split(proj, splits, axis=-1)
    heads_a = lambda a: a.reshape(B, T, ATTN_HEADS, ATTN_HEAD_DIM)
    heads_b = lambda a: a.reshape(B, T, REC_HEADS, REC_HEAD_DIM)
    o_a = _chunk_attention(heads_a(q_a), heads_a(k_a), heads_a(v_a), rel_bias)
    o_a = _rmsnorm(o_a, attn_gain.reshape(ATTN_HEADS, ATTN_HEAD_DIM)).reshape(B, T, ATTN_WIDTH)
    o_b = _hgrn2(heads_b(q_b), heads_b(f_b), heads_b(i_b), lower_bound)
    o_b = _rmsnorm(o_b, gnorm_gain).reshape(B, T, REC_WIDTH)
    o_b = o_b * jax.nn.silu(g_b.astype(jnp.float32))
    out = jnp.concatenate([o_a, o_b], axis=-1).astype(h.dtype)
    return out @ w_o


def _swiglu(h, w_ffn_in, w_ffn_out):
    gate, up = jnp.split(h @ w_ffn_in, 2, axis=-1)
    return (jax.nn.silu(gate) * up) @ w_ffn_out


def _fwd_setup_inputs(seed: int = 0) -> dict:
    key = jax.random.key(seed)
    ks = jax.random.split(key, 20)
    f32 = jnp.float32
    nrm = lambda k, shape, s: jax.random.normal(k, shape, f32) * s
    return {
        "x": nrm(ks[0], (BATCH, SEQ, D_MODEL), 1.0),
        "c": nrm(ks[1], (BATCH, D_MODEL), 1.0),
        "w_ada": nrm(ks[2], (DEPTH, D_MODEL, N_MOD * D_MODEL), 0.5 * D_MODEL ** -0.5),
        "b_ada": nrm(ks[3], (DEPTH, N_MOD * D_MODEL), 0.01),
        "w_in": nrm(ks[4], (DEPTH, D_MODEL, IN_PROJ_WIDTH), D_MODEL ** -0.5),
        "rel_bias": nrm(ks[5], (DEPTH, ATTN_HEADS, 2 * MAX_REL + 1), 0.1),
        "attn_norm_g": 1.0 + nrm(ks[6], (DEPTH, ATTN_WIDTH), 0.02),
        "lb_logits": nrm(ks[7], (DEPTH + 1, REC_WIDTH), 0.1),
        "gnorm_g": 1.0 + nrm(ks[8], (DEPTH, REC_HEAD_DIM), 0.02),
        "w_o": nrm(ks[9], (DEPTH, MIX_WIDTH, D_MODEL), BETA * MIX_WIDTH ** -0.5),
        "ln1_g": 1.0 + nrm(ks[10], (DEPTH, D_MODEL), 0.02),
        "ln1_b": nrm(ks[11], (DEPTH, D_MODEL), 0.01),
        "w_ffn_in": nrm(ks[12], (DEPTH, D_MODEL, 2 * D_FF), D_MODEL ** -0.5),
        "w_ffn_out": nrm(ks[13], (DEPTH, D_FF, D_MODEL), BETA * D_FF ** -0.5),
        "ln2_g": 1.0 + nrm(ks[14], (DEPTH, D_MODEL), 0.02),
        "ln2_b": nrm(ks[15], (DEPTH, D_MODEL), 0.01),
    }


def _fwd_reference(x, c, w_ada, b_ada, w_in, rel_bias, attn_norm_g, lb_logits, gnorm_g, w_o,
              ln1_g, ln1_b, w_ffn_in, w_ffn_out, ln2_g, ln2_b):
    lower_bounds = jnp.cumsum(jax.nn.softmax(lb_logits.astype(jnp.float32), axis=0), axis=0)
    c_act = jax.nn.silu(c)
    for layer in range(DEPTH):
        mod = c_act @ w_ada[layer] + b_ada[layer]
        shift1, scale1, gate1, shift2, scale2, gate2 = [m[:, None, :] for m in jnp.split(mod, N_MOD, axis=-1)]
        h = _layernorm(x) * (1.0 + scale1) + shift1
        mix = _token_mixer(h, w_in[layer], rel_bias[layer], attn_norm_g[layer],
                           lower_bounds[layer], gnorm_g[layer], w_o[layer])
        x = _layernorm(ALPHA * x + gate1 * mix, ln1_g[layer], ln1_b[layer])
        h = _layernorm(x) * (1.0 + scale2) + shift2
        x = _layernorm(ALPHA * x + gate2 * _swiglu(h, w_ffn_in[layer], w_ffn_out[layer]),
                       ln2_g[layer], ln2_b[layer])
    return x


import jax as _jax
import jax.numpy as _jnp

TWIN_FORMAT = 'train_step'
FWD_PARAMS = ['x', 'c', 'w_ada', 'b_ada', 'w_in', 'rel_bias', 'attn_norm_g', 'lb_logits', 'gnorm_g', 'w_o', 'ln1_g', 'ln1_b', 'w_ffn_in', 'w_ffn_out', 'ln2_g', 'ln2_b']
TWIN_WEIGHTS = ['w_ada', 'b_ada', 'w_in', 'rel_bias', 'attn_norm_g', 'lb_logits', 'gnorm_g', 'w_o', 'ln1_g', 'ln1_b', 'w_ffn_in', 'w_ffn_out', 'ln2_g', 'ln2_b']
TWIN_DIFF_INPUT = 'x'
TWIN_INPUTS = ['x', 'c', 'w_ada', 'b_ada', 'w_in', 'rel_bias', 'attn_norm_g', 'lb_logits', 'gnorm_g', 'w_o', 'ln1_g', 'ln1_b', 'w_ffn_in', 'w_ffn_out', 'ln2_g', 'ln2_b', 'loss_target', 'm_w_ada', 'm_b_ada', 'm_w_in', 'm_rel_bias', 'm_attn_norm_g', 'm_lb_logits', 'm_gnorm_g', 'm_w_o', 'm_ln1_g', 'm_ln1_b', 'm_w_ffn_in', 'm_w_ffn_out', 'm_ln2_g', 'm_ln2_b', 'v_w_ada', 'v_b_ada', 'v_w_in', 'v_rel_bias', 'v_attn_norm_g', 'v_lb_logits', 'v_gnorm_g', 'v_w_o', 'v_ln1_g', 'v_ln1_b', 'v_w_ffn_in', 'v_w_ffn_out', 'v_ln2_g', 'v_ln2_b']
TWIN_OUTPUTS = ['loss', 'grad_x', 'grad_w_ada', 'grad_b_ada', 'grad_w_in', 'grad_rel_bias', 'grad_attn_norm_g', 'grad_lb_logits', 'grad_gnorm_g', 'grad_w_o', 'grad_ln1_g', 'grad_ln1_b', 'grad_w_ffn_in', 'grad_w_ffn_out', 'grad_ln2_g', 'grad_ln2_b', 'delta_w_ada', 'delta_b_ada', 'delta_w_in', 'delta_rel_bias', 'delta_attn_norm_g', 'delta_lb_logits', 'delta_gnorm_g', 'delta_w_o', 'delta_ln1_g', 'delta_ln1_b', 'delta_w_ffn_in', 'delta_w_ffn_out', 'delta_ln2_g', 'delta_ln2_b', 'new_m_w_ada', 'new_m_b_ada', 'new_m_w_in', 'new_m_rel_bias', 'new_m_attn_norm_g', 'new_m_lb_logits', 'new_m_gnorm_g', 'new_m_w_o', 'new_m_ln1_g', 'new_m_ln1_b', 'new_m_w_ffn_in', 'new_m_w_ffn_out', 'new_m_ln2_g', 'new_m_ln2_b', 'new_v_w_ada', 'new_v_b_ada', 'new_v_w_in', 'new_v_rel_bias', 'new_v_attn_norm_g', 'new_v_lb_logits', 'new_v_gnorm_g', 'new_v_w_o', 'new_v_ln1_g', 'new_v_ln1_b', 'new_v_w_ffn_in', 'new_v_w_ffn_out', 'new_v_ln2_g', 'new_v_ln2_b']
TWIN_LEAF_KINDS = {'loss': 'loss', 'grad_x': 'grad_x', 'grad_w_ada': 'grad_w', 'grad_b_ada': 'grad_w', 'grad_w_in': 'grad_w', 'grad_rel_bias': 'grad_w', 'grad_attn_norm_g': 'grad_w', 'grad_lb_logits': 'grad_w', 'grad_gnorm_g': 'grad_w', 'grad_w_o': 'grad_w', 'grad_ln1_g': 'grad_w', 'grad_ln1_b': 'grad_w', 'grad_w_ffn_in': 'grad_w', 'grad_w_ffn_out': 'grad_w', 'grad_ln2_g': 'grad_w', 'grad_ln2_b': 'grad_w', 'delta_w_ada': 'delta_w', 'delta_b_ada': 'delta_w', 'delta_w_in': 'delta_w', 'delta_rel_bias': 'delta_w', 'delta_attn_norm_g': 'delta_w', 'delta_lb_logits': 'delta_w', 'delta_gnorm_g': 'delta_w', 'delta_w_o': 'delta_w', 'delta_ln1_g': 'delta_w', 'delta_ln1_b': 'delta_w', 'delta_w_ffn_in': 'delta_w', 'delta_w_ffn_out': 'delta_w', 'delta_ln2_g': 'delta_w', 'delta_ln2_b': 'delta_w', 'new_m_w_ada': 'new_m', 'new_m_b_ada': 'new_m', 'new_m_w_in': 'new_m', 'new_m_rel_bias': 'new_m', 'new_m_attn_norm_g': 'new_m', 'new_m_lb_logits': 'new_m', 'new_m_gnorm_g': 'new_m', 'new_m_w_o': 'new_m', 'new_m_ln1_g': 'new_m', 'new_m_ln1_b': 'new_m', 'new_m_w_ffn_in': 'new_m', 'new_m_w_ffn_out': 'new_m', 'new_m_ln2_g': 'new_m', 'new_m_ln2_b': 'new_m', 'new_v_w_ada': 'new_v', 'new_v_b_ada': 'new_v', 'new_v_w_in': 'new_v', 'new_v_rel_bias': 'new_v', 'new_v_attn_norm_g': 'new_v', 'new_v_lb_logits': 'new_v', 'new_v_gnorm_g': 'new_v', 'new_v_w_o': 'new_v', 'new_v_ln1_g': 'new_v', 'new_v_ln1_b': 'new_v', 'new_v_w_ffn_in': 'new_v', 'new_v_w_ffn_out': 'new_v', 'new_v_ln2_g': 'new_v', 'new_v_ln2_b': 'new_v'}


def _forward(args):
    return _fwd_reference(*[args[k] for k in FWD_PARAMS])


def _output_shape():
    out = _jax.eval_shape(lambda: _forward(_fwd_setup_inputs(0)))
    return out.shape, out.dtype

N_MICROBATCH = 1
ADAM_LR = 0.001
ADAM_B1 = 0.9
ADAM_B2 = 0.999
ADAM_EPS = 1e-08
ADAM_WD = 0.01
ADAM_STEP = 10
PER_EXAMPLE_BATCH_AXIS = {'x': 0, 'c': 0, 'loss_target': 0}
SHARED_INPUTS = []
_WEIGHT_DTYPES = {'w_ada': _jnp.float32, 'b_ada': _jnp.float32, 'w_in': _jnp.float32, 'rel_bias': _jnp.float32, 'attn_norm_g': _jnp.float32, 'lb_logits': _jnp.float32, 'gnorm_g': _jnp.float32, 'w_o': _jnp.float32, 'ln1_g': _jnp.float32, 'ln1_b': _jnp.float32, 'w_ffn_in': _jnp.float32, 'w_ffn_out': _jnp.float32, 'ln2_g': _jnp.float32, 'ln2_b': _jnp.float32}
MOMENT_SCALE = {'w_ada': 1.140574e-02, 'b_ada': 1.879042e-02, 'w_in': 5.505261e-03, 'rel_bias': 1.444267e-03, 'attn_norm_g': 1.008920e-02, 'lb_logits': 4.618047e-04, 'gnorm_g': 2.011170e-02, 'w_o': 1.457766e-02, 'ln1_g': 2.902163e-01, 'ln1_b': 8.655859e-02, 'w_ffn_in': 4.159217e-03, 'w_ffn_out': 1.140842e-02, 'ln2_g': 7.998986e+00, 'ln2_b': 3.978661e-01}


def _to_microbatches(a, axis):
    t = _jnp.moveaxis(a, axis, 0)
    t = t.reshape((N_MICROBATCH, t.shape[0] // N_MICROBATCH) + t.shape[1:])
    return _jnp.moveaxis(t, 1, axis + 1)


def setup_inputs(seed: int = 0) -> dict:
    inp = _fwd_setup_inputs(seed)
    key = _jax.random.fold_in(_jax.random.key(seed), 7919)
    shape, _ = _output_shape()
    out = dict(inp)
    out["loss_target"] = _jax.random.normal(_jax.random.fold_in(key, 0), shape, _jnp.float32)
    for i, name in enumerate(TWIN_WEIGHTS):
        w = inp[name].astype(_jnp.float32)
        if MOMENT_SCALE is None:
            s = _jnp.sqrt(_jnp.mean(_jnp.square(w)) + 1e-30)
        else:
            s = MOMENT_SCALE[name]
        km, kv = _jax.random.split(_jax.random.fold_in(key, i + 1))
        out[name] = w
        out["m_" + name] = s * _jax.random.normal(km, w.shape, _jnp.float32)
        out["v_" + name] = (s * s) * _jax.random.uniform(kv, w.shape, _jnp.float32, 0.5, 1.5)
    if N_MICROBATCH > 1:
        for name, axis in PER_EXAMPLE_BATCH_AXIS.items():
            out[name] = _to_microbatches(out[name], axis)
    return {'x': out['x'], 'c': out['c'], 'w_ada': out['w_ada'], 'b_ada': out['b_ada'], 'w_in': out['w_in'], 'rel_bias': out['rel_bias'], 'attn_norm_g': out['attn_norm_g'], 'lb_logits': out['lb_logits'], 'gnorm_g': out['gnorm_g'], 'w_o': out['w_o'], 'ln1_g': out['ln1_g'], 'ln1_b': out['ln1_b'], 'w_ffn_in': out['w_ffn_in'], 'w_ffn_out': out['w_ffn_out'], 'ln2_g': out['ln2_g'], 'ln2_b': out['ln2_b'], 'loss_target': out['loss_target'], 'm_w_ada': out['m_w_ada'], 'm_b_ada': out['m_b_ada'], 'm_w_in': out['m_w_in'], 'm_rel_bias': out['m_rel_bias'], 'm_attn_norm_g': out['m_attn_norm_g'], 'm_lb_logits': out['m_lb_logits'], 'm_gnorm_g': out['m_gnorm_g'], 'm_w_o': out['m_w_o'], 'm_ln1_g': out['m_ln1_g'], 'm_ln1_b': out['m_ln1_b'], 'm_w_ffn_in': out['m_w_ffn_in'], 'm_w_ffn_out': out['m_w_ffn_out'], 'm_ln2_g': out['m_ln2_g'], 'm_ln2_b': out['m_ln2_b'], 'v_w_ada': out['v_w_ada'], 'v_b_ada': out['v_b_ada'], 'v_w_in': out['v_w_in'], 'v_rel_bias': out['v_rel_bias'], 'v_attn_norm_g': out['v_attn_norm_g'], 'v_lb_logits': out['v_lb_logits'], 'v_gnorm_g': out['v_gnorm_g'], 'v_w_o': out['v_w_o'], 'v_ln1_g': out['v_ln1_g'], 'v_ln1_b': out['v_ln1_b'], 'v_w_ffn_in': out['v_w_ffn_in'], 'v_w_ffn_out': out['v_w_ffn_out'], 'v_ln2_g': out['v_ln2_g'], 'v_ln2_b': out['v_ln2_b']}


def _loss(weights, diff, rest, loss_target):
    with _jax.named_scope("forward"):
        args = {**rest, TWIN_DIFF_INPUT: diff, **{k: w.astype(_WEIGHT_DTYPES[k]) for k, w in weights.items()}}
        y = _forward(args)
    with _jax.named_scope("loss_head"):
        err = _jnp.square(y.astype(_jnp.float32) - loss_target)
        return 0.5 * _jnp.sum(_jnp.mean(err, axis=-1)) if err.ndim else 0.5 * err


def _adamw(w, g, m, v):
    m = ADAM_B1 * m + (1.0 - ADAM_B1) * g
    v = ADAM_B2 * v + (1.0 - ADAM_B2) * _jnp.square(g)
    m_hat = m / (1.0 - ADAM_B1 ** ADAM_STEP)
    v_hat = v / (1.0 - ADAM_B2 ** ADAM_STEP)
    delta = -ADAM_LR * (m_hat / (_jnp.sqrt(v_hat) + ADAM_EPS) + ADAM_WD * w)
    return delta, m, v


def reference(x, c, w_ada, b_ada, w_in, rel_bias, attn_norm_g, lb_logits, gnorm_g, w_o, ln1_g, ln1_b, w_ffn_in, w_ffn_out, ln2_g, ln2_b, loss_target, m_w_ada, m_b_ada, m_w_in, m_rel_bias, m_attn_norm_g, m_lb_logits, m_gnorm_g, m_w_o, m_ln1_g, m_ln1_b, m_w_ffn_in, m_w_ffn_out, m_ln2_g, m_ln2_b, v_w_ada, v_b_ada, v_w_in, v_rel_bias, v_attn_norm_g, v_lb_logits, v_gnorm_g, v_w_o, v_ln1_g, v_ln1_b, v_w_ffn_in, v_w_ffn_out, v_ln2_g, v_ln2_b):
    given = dict(x=x, c=c, w_ada=w_ada, b_ada=b_ada, w_in=w_in, rel_bias=rel_bias, attn_norm_g=attn_norm_g, lb_logits=lb_logits, gnorm_g=gnorm_g, w_o=w_o, ln1_g=ln1_g, ln1_b=ln1_b, w_ffn_in=w_ffn_in, w_ffn_out=w_ffn_out, ln2_g=ln2_g, ln2_b=ln2_b, loss_target=loss_target, m_w_ada=m_w_ada, m_b_ada=m_b_ada, m_w_in=m_w_in, m_rel_bias=m_rel_bias, m_attn_norm_g=m_attn_norm_g, m_lb_logits=m_lb_logits, m_gnorm_g=m_gnorm_g, m_w_o=m_w_o, m_ln1_g=m_ln1_g, m_ln1_b=m_ln1_b, m_w_ffn_in=m_w_ffn_in, m_w_ffn_out=m_w_ffn_out, m_ln2_g=m_ln2_g, m_ln2_b=m_ln2_b, v_w_ada=v_w_ada, v_b_ada=v_b_ada, v_w_in=v_w_in, v_rel_bias=v_rel_bias, v_attn_norm_g=v_attn_norm_g, v_lb_logits=v_lb_logits, v_gnorm_g=v_gnorm_g, v_w_o=v_w_o, v_ln1_g=v_ln1_g, v_ln1_b=v_ln1_b, v_w_ffn_in=v_w_ffn_in, v_w_ffn_out=v_w_ffn_out, v_ln2_g=v_ln2_g, v_ln2_b=v_ln2_b)
    weights = {n: given[n] for n in TWIN_WEIGHTS}
    shared = {n: given[n] for n in SHARED_INPUTS}
    per_example = {n: given[n] for n in ['x', 'c']}
    grad_fn = _jax.value_and_grad(_loss, argnums=(0, 1))

    def one_microbatch(ex, loss_target):
        ex = dict(ex)
        diff = ex.pop(TWIN_DIFF_INPUT)
        return grad_fn(weights, diff, {**shared, **ex}, loss_target)

    if N_MICROBATCH == 1:
        loss, (grad_w, grad_x) = one_microbatch(per_example, given["loss_target"])
    else:
        def body(carry, xs):
            loss_sum, grad_sum = carry
            l_k, (gw_k, gx_k) = one_microbatch(xs[0], xs[1])
            with _jax.named_scope("update"):
                return (loss_sum + l_k, _jax.tree.map(_jnp.add, grad_sum, gw_k)), gx_k

        init = (_jnp.zeros((), _jnp.float32), _jax.tree.map(_jnp.zeros_like, weights))
        (loss, grad_w), grad_x = _jax.lax.scan(body, init, (per_example, given["loss_target"]))
    with _jax.named_scope("update"):
        delta_w, new_m, new_v = {}, {}, {}
        for n in TWIN_WEIGHTS:
            delta_w[n], new_m[n], new_v[n] = _adamw(weights[n], grad_w[n], given["m_" + n], given["v_" + n])
    return (loss, grad_x, *[grad_w[n] for n in TWIN_WEIGHTS], *[delta_w[n] for n in TWIN_WEIGHTS],
            *[new_m[n] for n in TWIN_WEIGHTS], *[new_v[n] for n in TWIN_WEIGHTS])
```

```python
import functools

import jax
import jax.numpy as jnp
from jax import lax
from jax.experimental import pallas as pl
from jax.experimental.pallas import tpu as pltpu

F32 = jnp.float32
BF16 = jnp.bfloat16
MESH = pl.DeviceIdType.MESH
HIGHEST = lax.Precision.HIGHEST

N_DEV = 8
CHUNK = 64
N_PAST = 8
KPAD = (N_PAST + 1) * CHUNK
BAND = (N_PAST + 2) * CHUNK
ATTN_HEAD_DIM = 64
REC_HEAD_DIM = 128
SUB = 16
LANE = 128
EPS = 1e-5
ALPHA = 2.0 ** 0.25
ADAM_LR, ADAM_B1, ADAM_B2, ADAM_EPS, ADAM_WD, ADAM_STEP = 0.001, 0.9, 0.999, 1e-08, 0.01, 10
NEG = -1e30
VMEM_LIMIT = 56 * 1024 * 1024


def _sds(shape, dtype):
    return jax.ShapeDtypeStruct(tuple(shape), dtype)


def _tile(n, pref, mult):
    best = None
    for t in range(mult, min(n, pref) + 1, mult):
        if n % t == 0:
            best = t
    return n if best is None else best


def _params(sem=None, big=False):
    kw = {}
    if sem is not None:
        kw["dimension_semantics"] = sem
    if big:
        kw["vmem_limit_bytes"] = VMEM_LIMIT
    return pltpu.CompilerParams(**kw)


def _sigmoid(v):
    return 1.0 / (1.0 + jnp.exp(-v))


def _dot(a, b, dims, precision=None):
    return lax.dot_general(a, b, (dims, ((), ())), preferred_element_type=F32, precision=precision)


NN = ((1,), (0,))
NT = ((1,), (1,))
TN = ((0,), (0,))


def _ln(v):
    mu = jnp.mean(v, axis=-1, keepdims=True)
    d = v - mu
    rstd = lax.rsqrt(jnp.mean(d * d, axis=-1, keepdims=True) + EPS)
    return d * rstd, rstd


def _ln_bwd(dxh, xh, rstd):
    return rstd * (dxh - jnp.mean(dxh, axis=-1, keepdims=True) - xh * jnp.mean(dxh * xh, axis=-1, keepdims=True))


def _colsum(v):
    return jnp.sum(v, axis=0, keepdims=True)


def _ln_mod(x2, mod6):
    T, D = x2.shape
    tm = _tile(T, 256, 8)

    def body(x_ref, mod_ref, o_ref):
        xh, _ = _ln(x_ref[...])
        o_ref[...] = (xh * (1.0 + mod_ref[1:2, :]) + mod_ref[0:1, :]).astype(BF16)

    return pl.pallas_call(
        body, grid=(T // tm,), name="ln_mod",
        in_specs=[pl.BlockSpec((tm, D), lambda i: (i, 0)), pl.BlockSpec((6, D), lambda i: (0, 0))],
        out_specs=pl.BlockSpec((tm, D), lambda i: (i, 0)),
        out_shape=_sds((T, D), BF16), compiler_params=_params(("parallel",)),
    )(x2, mod6)


def _mid_fwd(x2, mix, mod6, ln1_g, ln1_b):
    T, D = x2.shape
    tm = _tile(T, 256, 8)

    def body(x_ref, mix_ref, mod_ref, g_ref, b_ref, x1_ref, h2_ref):
        zh, _ = _ln(ALPHA * x_ref[...] + mod_ref[2:3, :] * mix_ref[...])
        x1 = zh * g_ref[...] + b_ref[...]
        x1_ref[...] = x1
        xh, _ = _ln(x1)
        h2_ref[...] = (xh * (1.0 + mod_ref[4:5, :]) + mod_ref[3:4, :]).astype(BF16)

    row = pl.BlockSpec((tm, D), lambda i: (i, 0))
    vec = pl.BlockSpec((1, D), lambda i: (0, 0))
    return pl.pallas_call(
        body, grid=(T // tm,), name="mid_fwd",
        in_specs=[row, row, pl.BlockSpec((6, D), lambda i: (0, 0)), vec, vec],
        out_specs=[row, row],
        out_shape=[_sds((T, D), F32), _sds((T, D), BF16)], compiler_params=_params(("parallel",)),
    )(x2, mix, mod6, ln1_g, ln1_b)


def _final(x1, ff, mod6, ln2_g, ln2_b, tgt):
    T, D = x1.shape
    tm = _tile(T, 256, 8)

    def body(x1_ref, ff_ref, mod_ref, g_ref, b_ref, t_ref, dff_ref, dx1_ref, vec_ref):
        @pl.when(pl.program_id(0) == 0)
        def _():
            vec_ref[...] = jnp.zeros_like(vec_ref)

        ff_v = ff_ref[...]
        gate2 = mod_ref[5:6, :]
        zh, rstd = _ln(ALPHA * x1_ref[...] + gate2 * ff_v)
        err = zh * g_ref[...] + b_ref[...] - t_ref[...]
        dy = err * (1.0 / D)
        dz = _ln_bwd(dy * g_ref[...], zh, rstd)
        dff_ref[...] = (gate2 * dz).astype(BF16)
        dx1_ref[...] = ALPHA * dz
        vec_ref[0:1, :] += _colsum(dy * zh)
        vec_ref[1:2, :] += _colsum(dy)
        vec_ref[2:3, :] += _colsum(dz * ff_v)
        vec_ref[3:4, :] += _colsum(err * err) * (0.5 / D)

    row = pl.BlockSpec((tm, D), lambda i: (i, 0))
    vec = pl.BlockSpec((1, D), lambda i: (0, 0))
    return pl.pallas_call(
        body, grid=(T // tm,), name="final_fwd_bwd",
        in_specs=[row, row, pl.BlockSpec((6, D), lambda i: (0, 0)), vec, vec, row],
        out_specs=[row, row, pl.BlockSpec((8, D), lambda i: (0, 0))],
        out_shape=[_sds((T, D), BF16), _sds((T, D), F32), _sds((8, D), F32)],
        compiler_params=_params(("arbitrary",)),
    )(x1, ff, mod6, ln2_g, ln2_b, tgt)


def _mid_bwd(x2, mix, x1, dx1a, dh2, mod6, ln1_g):
    T, D = x2.shape
    tm = _tile(T, 256, 8)

    def body(x_ref, mix_ref, x1_ref, dx1a_ref, dh2_ref, mod_ref, g_ref, dmix_ref, dxa_ref, vec_ref):
        @pl.when(pl.program_id(0) == 0)
        def _():
            vec_ref[...] = jnp.zeros_like(vec_ref)

        dh2 = dh2_ref[...]
        xh, rstd = _ln(x1_ref[...])
        dx1 = dx1a_ref[...] + _ln_bwd(dh2 * (1.0 + mod_ref[4:5, :]), xh, rstd)
        mix_v = mix_ref[...]
        gate1 = mod_ref[2:3, :]
        zh, rstdz = _ln(ALPHA * x_ref[...] + gate1 * mix_v)
        dz = _ln_bwd(dx1 * g_ref[...], zh, rstdz)
        dmix_ref[...] = (gate1 * dz).astype(BF16)
        dxa_ref[...] = ALPHA * dz
        vec_ref[0:1, :] += _colsum(dh2 * xh)
        vec_ref[1:2, :] += _colsum(dh2)
        vec_ref[2:3, :] += _colsum(dx1 * zh)
        vec_ref[3:4, :] += _colsum(dx1)
        vec_ref[4:5, :] += _colsum(dz * mix_v)

    row = pl.BlockSpec((tm, D), lambda i: (i, 0))
    vec = pl.BlockSpec((1, D), lambda i: (0, 0))
    return pl.pallas_call(
        body, grid=(T // tm,), name="mid_bwd",
        in_specs=[row, row, row, row, row, pl.BlockSpec((6, D), lambda i: (0, 0)), vec],
        out_specs=[row, row, pl.BlockSpec((8, D), lambda i: (0, 0))],
        out_shape=[_sds((T, D), BF16), _sds((T, D), F32), _sds((8, D), F32)],
        compiler_params=_params(("arbitrary",)),
    )(x2, mix, x1, dx1a, dh2, mod6, ln1_g)


def _first_bwd(x2, dh1, dxa, mod6):
    T, D = x2.shape
    tm = _tile(T, 256, 8)

    def body(x_ref, dh1_ref, dxa_ref, mod_ref, gx_ref, vec_ref):
        @pl.when(pl.program_id(0) == 0)
        def _():
            vec_ref[...] = jnp.zeros_like(vec_ref)

        dh1 = dh1_ref[...]
        xh, rstd = _ln(x_ref[...])
        gx_ref[...] = dxa_ref[...] + _ln_bwd(dh1 * (1.0 + mod_ref[1:2, :]), xh, rstd)
        vec_ref[0:1, :] += _colsum(dh1 * xh)
        vec_ref[1:2, :] += _colsum(dh1)

    row = pl.BlockSpec((tm, D), lambda i: (i, 0))
    return pl.pallas_call(
        body, grid=(T // tm,), name="first_bwd",
        in_specs=[row, row, row, pl.BlockSpec((6, D), lambda i: (0, 0))],
        out_specs=[row, pl.BlockSpec((8, D), lambda i: (0, 0))],
        out_shape=[_sds((T, D), F32), _sds((8, D), F32)],
        compiler_params=_params(("arbitrary",)),
    )(x2, dh1, dxa, mod6)


def _slot(j):
    return (j % 2) * 4 + j // 2


def _mm_gathered(a, wg, name):
    M, K = a.shape
    _, _, ns = wg.shape
    tm = _tile(M, 512, 16)

    def body(a_ref, w_ref, o_ref):
        o_ref[...] = _dot(a_ref[...], w_ref[...], NN)

    return pl.pallas_call(
        body, grid=(N_DEV, M // tm), name=name,
        in_specs=[pl.BlockSpec((tm, K), lambda j, i: (i, 0)), pl.BlockSpec((None, K, ns), lambda j, i: (j, 0, 0))],
        out_specs=pl.BlockSpec((tm, ns), lambda j, i: (i, j)),
        out_shape=_sds((M, N_DEV * ns), F32), compiler_params=_params(("parallel", "parallel"), big=True),
    )(a, wg)


def _mm_nn(a, b, name):
    M, K = a.shape
    _, N = b.shape
    tm, tn, tk = _tile(M, 512, 16), _tile(N, 1024, LANE), _tile(K, 2048, LANE)

    def body(a_ref, b_ref, o_ref):
        @pl.when(pl.program_id(2) == 0)
        def _():
            o_ref[...] = jnp.zeros_like(o_ref)

        o_ref[...] += _dot(a_ref[...], b_ref[...], NN)

    return pl.pallas_call(
        body, grid=(M // tm, N // tn, K // tk), name=name,
        in_specs=[pl.BlockSpec((tm, tk), lambda i, j, k: (i, k)), pl.BlockSpec((tk, tn), lambda i, j, k: (k, j))],
        out_specs=pl.BlockSpec((tm, tn), lambda i, j, k: (i, j)),
        out_shape=_sds((M, N), F32), compiler_params=_params(("parallel", "parallel", "arbitrary"), big=True),
    )(a, b)


def _mm_nt(a, b, name):
    M, K = a.shape
    N, _ = b.shape
    tm, tn = _tile(M, 512, 16), _tile(N, 1024, LANE)

    def body(a_ref, b_ref, o_ref):
        o_ref[...] = _dot(a_ref[...], b_ref[...], NT)

    return pl.pallas_call(
        body, grid=(M // tm, N // tn), name=name,
        in_specs=[pl.BlockSpec((tm, K), lambda i, j: (i, 0)), pl.BlockSpec((tn, K), lambda i, j: (j, 0))],
        out_specs=pl.BlockSpec((tm, tn), lambda i, j: (i, j)),
        out_shape=_sds((M, N), F32), compiler_params=_params(("parallel", "parallel"), big=True),
    )(a, b)


def _mm_swiglu(h2, wg):
    M, K = h2.shape
    _, _, ns = wg.shape
    half = N_DEV // 2
    tm = _tile(M, 256, 16)

    def body(a_ref, wgate_ref, wup_ref, gu_ref, act_ref):
        a = a_ref[...]
        g = _dot(a, wgate_ref[...], NN)
        u = _dot(a, wup_ref[...], NN)
        gu_ref[0] = g
        gu_ref[1] = u
        act_ref[...] = (g * _sigmoid(g) * u).astype(BF16)

    return pl.pallas_call(
        body, grid=(half, M // tm), name="ffn_in_swiglu",
        in_specs=[pl.BlockSpec((tm, K), lambda j, i: (i, 0)),
                  pl.BlockSpec((None, K, ns), lambda j, i: (j, 0, 0)),
                  pl.BlockSpec((None, K, ns), lambda j, i: (j + half, 0, 0))],
        out_specs=[pl.BlockSpec((2, tm, ns), lambda j, i: (0, i, j)), pl.BlockSpec((tm, ns), lambda j, i: (i, j))],
        out_shape=[_sds((2, M, half * ns), F32), _sds((M, half * ns), BF16)],
        compiler_params=_params(("parallel", "parallel"), big=True),
    )(h2, wg, wg)


def _mm_swiglu_bwd(dff, w2, gu):
    M, K = dff.shape
    F = w2.shape[0]
    tm, tn = _tile(M, 256, 16), _tile(F, 1408, LANE)

    def body(a_ref, b_ref, gu_ref, du_ref):
        da = _dot(a_ref[...], b_ref[...], NT)
        g = gu_ref[0]
        u = gu_ref[1]
        sg = _sigmoid(g)
        du_ref[0] = (da * u * (sg * (1.0 + g * (1.0 - sg)))).astype(BF16)
        du_ref[1] = (da * (g * sg)).astype(BF16)

    return pl.pallas_call(
        body, grid=(M // tm, F // tn), name="ffn_out_bwd_swiglu",
        in_specs=[pl.BlockSpec((tm, K), lambda i, j: (i, 0)), pl.BlockSpec((tn, K), lambda i, j: (j, 0)),
                  pl.BlockSpec((2, tm, tn), lambda i, j: (0, i, j))],
        out_specs=pl.BlockSpec((2, tm, tn), lambda i, j: (0, i, j)),
        out_shape=_sds((2, M, F), BF16), compiler_params=_params(("parallel", "parallel"), big=True),
    )(dff, w2, gu)


def _mm_tn_rows(a, b, rs, name):
    M, Ka = a.shape
    _, N = b.shape
    tm = _tile(M, 512, 16)

    def body(a_ref, b_ref, o_ref, acc_ref):
        m = pl.program_id(1)

        @pl.when(m == 0)
        def _():
            acc_ref[...] = jnp.zeros_like(acc_ref)

        acc_ref[...] += _dot(a_ref[...], b_ref[...], TN)

        @pl.when(m == pl.num_programs(1) - 1)
        def _():
            o_ref[0, 0] = acc_ref[0:rs, :].astype(BF16)
            o_ref[1, 0] = acc_ref[rs:2 * rs, :].astype(BF16)

    return pl.pallas_call(
        body, grid=(N_DEV // 2, M // tm), name=name,
        in_specs=[pl.BlockSpec((tm, 2 * rs), lambda ch, m: (m, ch)), pl.BlockSpec((tm, N), lambda ch, m: (m, 0))],
        out_specs=pl.BlockSpec((2, 1, rs, N), lambda ch, m: (0, ch, 0, 0)),
        out_shape=_sds((2, N_DEV // 2, rs, N), BF16),
        scratch_shapes=[pltpu.VMEM((2 * rs, N), F32)],
        compiler_params=_params(("parallel", "arbitrary"), big=True),
    )(a, b)


def _mm_gathered_nt(a, a_spec, wg, M, tm, name):
    _, K, ns = wg.shape

    def body(a_ref, w_ref, o_ref):
        @pl.when(pl.program_id(1) == 0)
        def _():
            o_ref[...] = jnp.zeros_like(o_ref)

        o_ref[...] += _dot(a_ref[...], w_ref[...], NT)

    return pl.pallas_call(
        body, grid=(M // tm, N_DEV), name=name,
        in_specs=[a_spec(tm, ns), pl.BlockSpec((None, K, ns), lambda i, j: (j, 0, 0))],
        out_specs=pl.BlockSpec((tm, K), lambda i, j: (i, 0)),
        out_shape=_sds((M, K), F32), compiler_params=_params(("parallel", "arbitrary"), big=True),
    )(a, wg)


def _mm_tn_gathered(h, a, a_spec, ns, tm, name):
    M, K = h.shape

    def body(h_ref, a_ref, o_ref, acc_ref):
        m = pl.program_id(1)

        @pl.when(m == 0)
        def _():
            acc_ref[...] = jnp.zeros_like(acc_ref)

        acc_ref[...] += _dot(h_ref[...], a_ref[...], TN)

        @pl.when(m == pl.num_programs(1) - 1)
        def _():
            o_ref[...] = acc_ref[...].astype(BF16)

    return pl.pallas_call(
        body, grid=(N_DEV, M // tm), name=name,
        in_specs=[pl.BlockSpec((tm, K), lambda j, m: (m, 0)), a_spec(tm, ns)],
        out_specs=pl.BlockSpec((None, K, ns), lambda j, m: (_slot(j), 0, 0)),
        out_shape=_sds((N_DEV, K, ns), BF16),
        scratch_shapes=[pltpu.VMEM((K, ns), F32)],
        compiler_params=_params(("parallel", "arbitrary"), big=True),
    )(h, a)


def _bias_onehot(rbp, max_rel):
    r = lax.broadcasted_iota(jnp.int32, (rbp, BAND), 0)
    m = lax.broadcasted_iota(jnp.int32, (rbp, BAND), 1)
    return (r == jnp.minimum(KPAD - m, max_rel) + max_rel).astype(F32)


def _attn_setup(i, hp, k_ref, v_ref, gv_ref, kpad, vpad, bias):
    ls = slice(i * ATTN_HEAD_DIM, (i + 1) * ATTN_HEAD_DIM)
    kpad[i, 0:KPAD, :] = jnp.zeros((KPAD, ATTN_HEAD_DIM), BF16)
    vpad[i, 0:KPAD, :] = jnp.zeros((KPAD, ATTN_HEAD_DIM), BF16)
    kpad[i, KPAD:, :] = k_ref[:, ls].astype(BF16)
    vpad[i, KPAD:, :] = v_ref[:, ls].astype(BF16)
    gvrow = gv_ref[pl.ds(hp * 2 + i, 1), :]
    bias[i] = pltpu.roll(jnp.broadcast_to(gvrow, (CHUNK, BAND)), 0, 1, stride=1, stride_axis=0)


def _attn_probs(n, i, q_ref, kpad, vpad, bias, col):
    ls = slice(i * ATTN_HEAD_DIM, (i + 1) * ATTN_HEAD_DIM)
    r0 = pl.multiple_of(n * CHUNK, CHUNK)
    q = q_ref[pl.ds(r0, CHUNK), ls].astype(BF16)
    kb = kpad[i, pl.ds(r0, BAND), :]
    vb = vpad[i, pl.ds(r0, BAND), :]
    s = _dot(q, kb, NT) * (ATTN_HEAD_DIM ** -0.5) + bias[i]
    valid = col >= jnp.maximum(CHUNK, (N_PAST + 1 - n) * CHUNK)
    s = jnp.where(valid, s, NEG)
    p = jnp.exp(s - jnp.max(s, axis=-1, keepdims=True))
    pn = p / jnp.sum(p, axis=-1, keepdims=True)
    return r0, ls, q, kb, vb, pn


def _attn_fwd(proj, gv, ga, AW):
    T = proj.shape[0]
    HP = AW // LANE

    def body(q_ref, k_ref, v_ref, gv_ref, ga_ref, o_ref, kpad, vpad, bias):
        hp = pl.program_id(0)
        for i in range(2):
            _attn_setup(i, hp, k_ref, v_ref, gv_ref, kpad, vpad, bias)
        col = lax.broadcasted_iota(jnp.int32, (CHUNK, BAND), 1)

        def chunk(n, carry):
            for i in range(2):
                r0, ls, _, _, vb, pn = _attn_probs(n, i, q_ref, kpad, vpad, bias, col)
                o = _dot(pn.astype(BF16), vb, NN)
                r = lax.rsqrt(jnp.mean(o * o, axis=-1, keepdims=True) + EPS)
                o_ref[pl.ds(r0, CHUNK), ls] = (o * r * ga_ref[0:1, ls]).astype(BF16)
            return carry

        lax.fori_loop(0, T // CHUNK, chunk, 0)

    blk = lambda off: pl.BlockSpec((T, LANE), lambda hp: (0, off + hp))
    return pl.pallas_call(
        body, grid=(HP,), name="attn_fwd",
        in_specs=[blk(0), blk(HP), blk(2 * HP), pl.BlockSpec(gv.shape, lambda hp: (0, 0)),
                  pl.BlockSpec((1, LANE), lambda hp: (0, hp))],
        out_specs=pl.BlockSpec((T, LANE), lambda hp: (0, hp)),
        out_shape=_sds((T, AW), BF16),
        scratch_shapes=[pltpu.VMEM((2, T + KPAD, ATTN_HEAD_DIM), BF16), pltpu.VMEM((2, T + KPAD, ATTN_HEAD_DIM), BF16),
                        pltpu.VMEM((2, CHUNK, BAND), F32)],
        compiler_params=_params(("parallel",), big=True),
    )(proj, proj, proj, gv, ga)


def _attn_bwd(proj, dmixin, gv, ga, AW):
    T = proj.shape[0]
    HP = AW // LANE
    scale = ATTN_HEAD_DIM ** -0.5

    def body(q_ref, k_ref, v_ref, dn_ref, gv_ref, ga_ref, dq_ref, dk_ref, dv_ref, dgv_ref, dga_ref,
             kpad, vpad, dkacc, dvacc, bias, dbias):
        hp = pl.program_id(0)
        for i in range(2):
            _attn_setup(i, hp, k_ref, v_ref, gv_ref, kpad, vpad, bias)
        dkacc[...] = jnp.zeros_like(dkacc)
        dvacc[...] = jnp.zeros_like(dvacc)
        dbias[...] = jnp.zeros_like(dbias)
        dga_ref[...] = jnp.zeros_like(dga_ref)
        col = lax.broadcasted_iota(jnp.int32, (CHUNK, BAND), 1)

        def chunk(n, carry):
            for i in range(2):
                r0, ls, q, kb, vb, pn = _attn_probs(n, i, q_ref, kpad, vpad, bias, col)
                pn_b = pn.astype(BF16)
                o = _dot(pn_b, vb, NN)
                r = lax.rsqrt(jnp.mean(o * o, axis=-1, keepdims=True) + EPS)
                dn = dn_ref[pl.ds(r0, CHUNK), ls]
                dga_ref[i:i + 1, :] += _colsum(dn * o * r)
                a = dn * ga_ref[0:1, ls]
                do = r * (a - o * (r * r) * jnp.mean(a * o, axis=-1, keepdims=True))
                do_b = do.astype(BF16)
                dp = _dot(do_b, vb, NT)
                dvacc[i, pl.ds(r0, BAND), :] += _dot(pn_b, do_b, TN)
                ds = pn * (dp - jnp.sum(pn * dp, axis=-1, keepdims=True))
                dbias[i] += ds
                ds_b = ds.astype(BF16)
                dq_ref[pl.ds(r0, CHUNK), ls] = (_dot(ds_b, kb, NN) * scale).astype(BF16)
                dkacc[i, pl.ds(r0, BAND), :] += _dot(ds_b, q, TN) * scale
            return carry

        lax.fori_loop(0, T // CHUNK, chunk, 0)

        rr = lax.broadcasted_iota(jnp.int32, (CHUNK, CHUNK), 0)
        cc = lax.broadcasted_iota(jnp.int32, (CHUNK, CHUNK), 1)
        flip = (rr + cc == CHUNK - 1).astype(F32)
        for i in range(2):
            ls = slice(i * ATTN_HEAD_DIM, (i + 1) * ATTN_HEAD_DIM)
            dk_ref[:, ls] = dkacc[i, KPAD:, :].astype(BF16)
            dv_ref[:, ls] = dvacc[i, KPAD:, :].astype(BF16)
            rev = _dot(flip, dbias[i], NN, HIGHEST)
            dgv_ref[i:i + 1, :] = _colsum(pltpu.roll(rev, BAND - (CHUNK - 1), 1, stride=1, stride_axis=0))

    blk = lambda off: pl.BlockSpec((T, LANE), lambda hp: (0, off + hp))
    accs = lambda dt: pltpu.VMEM((2, T + KPAD, ATTN_HEAD_DIM), dt)
    return pl.pallas_call(
        body, grid=(HP,), name="attn_bwd",
        in_specs=[blk(0), blk(HP), blk(2 * HP), blk(0), pl.BlockSpec(gv.shape, lambda hp: (0, 0)),
                  pl.BlockSpec((1, LANE), lambda hp: (0, hp))],
        out_specs=[blk(0), blk(0), blk(0), pl.BlockSpec((None, 2, BAND), lambda hp: (hp, 0, 0)),
                   pl.BlockSpec((None, 2, ATTN_HEAD_DIM), lambda hp: (hp, 0, 0))],
        out_shape=[_sds((T, AW), BF16), _sds((T, AW), BF16), _sds((T, AW), BF16),
                   _sds((HP, 2, BAND), F32), _sds((HP, 2, ATTN_HEAD_DIM), F32)],
        scratch_shapes=[accs(BF16), accs(BF16), accs(F32), accs(F32),
                        pltpu.VMEM((2, CHUNK, BAND), F32), pltpu.VMEM((2, CHUNK, BAND), F32)],
        compiler_params=_params(("parallel",), big=True),
    )(proj, proj, proj, dmixin, gv, ga)


def _ltri():
    r = lax.broadcasted_iota(jnp.int32, (CHUNK, CHUNK), 0)
    c = lax.broadcasted_iota(jnp.int32, (CHUNK, CHUNK), 1)
    return (c <= r).astype(F32)


def _hgrn_gates(n, q_ref, f_ref, lb_ref, ltri):
    r0 = pl.multiple_of(n * CHUNK, CHUNK)
    rows = pl.ds(r0, CHUNK)
    lb = lb_ref[...]
    qb = q_ref[rows, :]
    sg = _sigmoid(f_ref[rows, :])
    f = lb + (1.0 - lb) * sg
    sq = _sigmoid(qb)
    b = _dot(ltri, jnp.log(f), NN, HIGHEST)
    return rows, lb, qb, sg, f, 1.0 - f, sq, qb * sq, b


def _hgrn_fwd(proj, lb, gn, AW, RW):
    T = proj.shape[0]
    RH, NC, NSUB = RW // LANE, T // CHUNK, CHUNK // SUB
    base = 3 * AW // LANE

    def body(q_ref, f_ref, i_ref, g_ref, lb_ref, gn_ref, mix_ref, o_ref, stall_ref, st, bs, kks, ics):
        st[...] = jnp.zeros_like(st)
        ltri = _ltri()
        rowi = lax.broadcasted_iota(jnp.int32, (SUB, 1), 0)

        def chunk(n, carry):
            rows, _, _, _, _, kk, _, qs, b = _hgrn_gates(n, q_ref, f_ref, lb_ref, ltri)
            ic = i_ref[rows, :]
            stv = st[...]
            stall_ref[n] = stv
            bs[...] = b
            kks[...] = kk
            ics[...] = ic
            o = _dot((qs * jnp.exp(b)).astype(BF16), stv.astype(BF16), NT)
            ic_b = ic.astype(BF16)
            pieces = []
            for blk in range(NSUB):
                s0 = blk * SUB
                bI, qI = b[s0:s0 + SUB], qs[s0:s0 + SUB]
                if blk == 0:
                    oI = jnp.zeros((SUB, LANE), F32)
                else:
                    ref = bs[s0 - 1:s0, :]
                    qt = (qI * jnp.exp(bI - ref)).astype(BF16)
                    kt = (kk[0:s0] * jnp.exp(ref - b[0:s0])).astype(BF16)
                    oI = _dot(_dot(qt, kt, NT).astype(BF16), ic_b[0:s0], NN)
                for s in range(SUB):
                    sr = s0 + s
                    e = jnp.exp(jnp.minimum(bI - bs[sr:sr + 1, :], 0.0))
                    a = jnp.sum(qI * kks[sr:sr + 1, :] * e, axis=-1, keepdims=True)
                    oI = oI + jnp.where(rowi >= s, a, 0.0) * ics[sr:sr + 1, :]
                pieces.append(oI)
            o = o + jnp.concatenate(pieces, axis=0)
            bl = bs[CHUNK - 1:CHUNK, :]
            kd = (kk * jnp.exp(bl - b)).astype(BF16)
            st[...] = stv * jnp.exp(bl) + _dot(ic_b, kd, TN)
            o_ref[rows, :] = o
            r = lax.rsqrt(jnp.mean(o * o, axis=-1, keepdims=True) + EPS)
            gb = g_ref[rows, :]
            mix_ref[rows, :] = (o * r * gn_ref[...] * (gb * _sigmoid(gb))).astype(BF16)
            return carry

        lax.fori_loop(0, NC, chunk, 0)

    blk_in = lambda off: pl.BlockSpec((T, LANE), lambda h: (0, base + off + h))
    col = pl.BlockSpec((T, LANE), lambda h: (0, h))
    tile = pltpu.VMEM((CHUNK, LANE), F32)
    return pl.pallas_call(
        body, grid=(RH,), name="hgrn_fwd",
        in_specs=[blk_in(0), blk_in(RH), blk_in(2 * RH), blk_in(3 * RH), pl.BlockSpec((1, LANE), lambda h: (0, h)),
                  pl.BlockSpec((1, LANE), lambda h: (0, 0))],
        out_specs=[col, col, pl.BlockSpec((None, NC, LANE, LANE), lambda h: (h, 0, 0, 0))],
        out_shape=[_sds((T, RW), BF16), _sds((T, RW), F32), _sds((RH, NC, LANE, LANE), F32)],
        scratch_shapes=[pltpu.VMEM((LANE, LANE), F32), tile, tile, tile],
        compiler_params=_params(("parallel",), big=True),
    )(proj, proj, proj, proj, lb, gn)


def _hgrn_bwd(proj, dmixin, o_b, st_all, lb, gn, AW, RW):
    T = proj.shape[0]
    RH, NC, NSUB = RW // LANE, T // CHUNK, CHUNK // SUB
    base = 3 * AW // LANE

    def body(q_ref, f_ref, i_ref, g_ref, o_ref, dn_ref, stall_ref, lb_ref, gn_ref,
             dq_ref, df_ref, di_ref, dg_ref, dlb_ref, dgn_ref, dst, bs, kks, ics, dos, p2, dic):
        dst[...] = jnp.zeros_like(dst)
        dlb_ref[...] = jnp.zeros_like(dlb_ref)
        dgn_ref[...] = jnp.zeros_like(dgn_ref)
        ltri = _ltri()
        rowi = lax.broadcasted_iota(jnp.int32, (SUB, 1), 0)
        last = lax.broadcasted_iota(jnp.int32, (CHUNK, 1), 0) == CHUNK - 1

        def chunk(k, carry):
            n = NC - 1 - k
            rows, lbv, qb, sg, f, kk, sq, qs, b = _hgrn_gates(n, q_ref, f_ref, lb_ref, ltri)
            ic = i_ref[rows, :]
            stv = stall_ref[n]
            dstv = dst[...]
            o = o_ref[rows, :]
            dn = dn_ref[rows, :]
            gb = g_ref[rows, :]
            sgb = _sigmoid(gb)
            r = lax.rsqrt(jnp.mean(o * o, axis=-1, keepdims=True) + EPS)
            gnv = gn_ref[...]
            dg_ref[rows, :] = (dn * (o * r * gnv) * (sgb * (1.0 + gb * (1.0 - sgb)))).astype(BF16)
            dy = dn * (gb * sgb)
            dgn_ref[...] += _colsum(dy * o * r)
            a_ = dy * gnv
            do = r * (a_ - o * (r * r) * jnp.mean(a_ * o, axis=-1, keepdims=True))
            do_b = do.astype(BF16)
            bs[...] = b
            kks[...] = kk
            ics[...] = ic
            dos[...] = do
            ic_b = ic.astype(BF16)
            eb = jnp.exp(b)
            bl = bs[CHUNK - 1:CHUNK, :]
            ebl = jnp.exp(bl)
            dec = jnp.exp(bl - b)
            kd = (kk * dec).astype(BF16)
            dst_b = dstv.astype(BF16)
            dqs = _dot(do_b, stv.astype(BF16), NN) * eb
            dkk2 = _dot(ic_b, dst_b, NN) * dec
            dic[...] = _dot(kd, dst_b, NT)
            dbl = ebl * _colsum(stv * dstv) + _colsum(kk * dkk2)
            dst[...] = dstv * ebl + _dot(do_b, (qs * eb).astype(BF16), TN)
            p2[...] = jnp.zeros_like(p2)
            p1_pieces = []
            for blk in range(NSUB):
                s0 = blk * SUB
                bI, qI, doI = b[s0:s0 + SUB], qs[s0:s0 + SUB], do[s0:s0 + SUB]
                if blk == 0:
                    p1 = jnp.zeros((SUB, LANE), F32)
                else:
                    ref = bs[s0 - 1:s0, :]
                    eq = jnp.exp(bI - ref)
                    ek = jnp.exp(ref - b[0:s0])
                    qt = (qI * eq).astype(BF16)
                    kt = (kk[0:s0] * ek).astype(BF16)
                    doI_b = doI.astype(BF16)
                    dic[0:s0, :] += _dot(_dot(qt, kt, NT).astype(BF16), doI_b, TN)
                    da = _dot(doI_b, ic_b[0:s0], NT).astype(BF16)
                    p1 = _dot(da, kt, NN) * eq
                    p2[0:s0, :] += _dot(da, qt, TN) * ek
                for s in range(SUB):
                    sr = s0 + s
                    keep = rowi >= s
                    kk_s = kks[sr:sr + 1, :]
                    e = jnp.exp(jnp.minimum(bI - bs[sr:sr + 1, :], 0.0))
                    w = qI * e
                    a = jnp.where(keep, jnp.sum(w * kk_s, axis=-1, keepdims=True), 0.0)
                    da_s = jnp.where(keep, jnp.sum(doI * ics[sr:sr + 1, :], axis=-1, keepdims=True), 0.0)
                    p1 = p1 + da_s * kk_s * e
                    p2[sr:sr + 1, :] += _colsum(da_s * w)
                    dic[sr:sr + 1, :] += _colsum(a * doI)
                p1_pieces.append(p1)
            dqs = dqs + jnp.concatenate(p1_pieces, axis=0)
            dkk = dkk2 + p2[...]
            db = qs * dqs - kk * dkk + jnp.where(last, dbl, 0.0)
            dgl = _dot(ltri, db, TN, HIGHEST)
            dfv = dgl / f - dkk
            df_ref[rows, :] = (dfv * (1.0 - lbv) * sg * (1.0 - sg)).astype(BF16)
            dlb_ref[...] += _colsum(dfv * (1.0 - sg))
            dq_ref[rows, :] = (dqs * (sq * (1.0 + qb * (1.0 - sq)))).astype(BF16)
            di_ref[rows, :] = dic[...].astype(BF16)
            return carry

        lax.fori_loop(0, NC, chunk, 0)

    blk_in = lambda off: pl.BlockSpec((T, LANE), lambda h: (0, base + off + h))
    col = pl.BlockSpec((T, LANE), lambda h: (0, h))
    tile = pltpu.VMEM((CHUNK, LANE), F32)
    return pl.pallas_call(
        body, grid=(RH,), name="hgrn_bwd",
        in_specs=[blk_in(0), blk_in(RH), blk_in(2 * RH), blk_in(3 * RH), col,
                  pl.BlockSpec((T, LANE), lambda h: (0, AW // LANE + h)),
                  pl.BlockSpec((None, NC, LANE, LANE), lambda h: (h, 0, 0, 0)),
                  pl.BlockSpec((1, LANE), lambda h: (0, h)), pl.BlockSpec((1, LANE), lambda h: (0, 0))],
        out_specs=[col, col, col, col, pl.BlockSpec((1, LANE), lambda h: (0, h)),
                   pl.BlockSpec((None, 1, LANE), lambda h: (h, 0, 0))],
        out_shape=[_sds((T, RW), BF16)] * 4 + [_sds((1, RW), F32), _sds((RH, 1, LANE), F32)],
        scratch_shapes=[pltpu.VMEM((LANE, LANE), F32), tile, tile, tile, tile, tile, tile],
        compiler_params=_params(("parallel",), big=True),
    )(proj, proj, proj, proj, o_b, dmixin, st_all, lb, gn)


def _prep(c, lb_logits, rb_pad, max_rel):
    D, RW = c.shape[-1], lb_logits.shape[-1]
    H, rbp = rb_pad.shape

    def body(c_ref, l_ref, rb_ref, cact_ref, lb_ref, gv_ref):
        cv = c_ref[...]
        cact_ref[...] = cv * _sigmoid(cv)
        lb_ref[...] = _sigmoid(l_ref[0:1, :] - l_ref[1:2, :])
        gv_ref[...] = _dot(rb_ref[...], _bias_onehot(rbp, max_rel), NN, HIGHEST)

    return pl.pallas_call(
        body, name="prep", out_shape=[_sds((1, D), F32), _sds((1, RW), F32), _sds((H, BAND), F32)],
    )(c, lb_logits, rb_pad)


def _mod_part(c_all, w_ada_s, b_ada_s):
    B, D = c_all.shape
    ns = w_ada_s.shape[1]
    tn = _tile(ns, 768, LANE)

    def body(c_ref, w_ref, b_ref, o_ref):
        o_ref[...] = _dot(c_ref[...], w_ref[...], NN) + b_ref[...]

    return pl.pallas_call(
        body, grid=(ns // tn,), name="mod_part",
        in_specs=[pl.BlockSpec((B, D), lambda j: (0, 0)), pl.BlockSpec((D, tn), lambda j: (0, j)),
                  pl.BlockSpec((1, tn), lambda j: (0, j))],
        out_specs=pl.BlockSpec((B, tn), lambda j: (0, j)),
        out_shape=_sds((B, ns), F32), compiler_params=_params(("parallel",)),
    )(c_all, w_ada_s, b_ada_s)


def _adam(w, g, m, v):
    m = ADAM_B1 * m + (1.0 - ADAM_B1) * g
    v = ADAM_B2 * v + (1.0 - ADAM_B2) * (g * g)
    m_hat = m / (1.0 - ADAM_B1 ** ADAM_STEP)
    v_hat = v / (1.0 - ADAM_B2 ** ADAM_STEP)
    return -ADAM_LR * (m_hat / (jnp.sqrt(v_hat) + ADAM_EPS) + ADAM_WD * w), m, v


def _adam_ada(c_all, dmod_s, w, m, v):
    B, D = c_all.shape
    ns = w.shape[1]
    tr, tn = _tile(D, 512, LANE), _tile(ns, 768, LANE)

    def body(c_ref, d_ref, w_ref, m_ref, v_ref, g_out, dw_out, m_out, v_out):
        g = _dot(c_ref[...], d_ref[...], TN)
        g_out[...] = g
        dw_out[...], m_out[...], v_out[...] = _adam(w_ref[...], g, m_ref[...], v_ref[...])

    big = pl.BlockSpec((tr, tn), lambda i, j: (i, j))
    return pl.pallas_call(
        body, grid=(D // tr, ns // tn), name="adam_w_ada",
        in_specs=[pl.BlockSpec((B, tr), lambda i, j: (0, i)), pl.BlockSpec((B, tn), lambda i, j: (0, j)),
                  big, big, big],
        out_specs=[big] * 4, out_shape=[_sds((D, ns), F32)] * 4,
        compiler_params=_params(("parallel", "parallel")),
    )(c_all, dmod_s, w, m, v)


def _adam_shard(parts, w, m, v, name):
    R, C = w.shape
    tr = _tile(R, 256, 16)

    def body(p_ref, w_ref, m_ref, v_ref, g_out, dw_out, m_out, v_out):
        g = p_ref[0].astype(F32)
        for k in range(1, N_DEV // 2):
            g = g + p_ref[k].astype(F32)
        g_out[...] = g
        dw_out[...], m_out[...], v_out[...] = _adam(w_ref[...], g, m_ref[...], v_ref[...])

    big = pl.BlockSpec((tr, C), lambda i: (i, 0))
    return pl.pallas_call(
        body, grid=(R // tr,), name=name,
        in_specs=[pl.BlockSpec((N_DEV // 2, tr, C), lambda i: (0, i, 0)), big, big, big],
        out_specs=[big] * 4, out_shape=[_sds((R, C), F32)] * 4,
        compiler_params=_params(("parallel",), big=True),
    )(parts, w, m, v)


def _pair_sum(g8, land, core, name):
    _, NCHIP, R, C = g8.shape
    tr = _tile(R, 256, 16)

    def body(core_ref, g_ref, l_ref, o_ref):
        o_ref[...] = (g_ref[...].astype(F32) + l_ref[...].astype(F32)).astype(BF16)

    return pl.pallas_call(
        body, name=name,
        grid_spec=pltpu.PrefetchScalarGridSpec(
            num_scalar_prefetch=1, grid=(NCHIP, R // tr),
            in_specs=[pl.BlockSpec((None, None, tr, C), lambda k, i, core_ref: (core_ref[0], k, i, 0)),
                      pl.BlockSpec((None, tr, C), lambda k, i, core_ref: (k, i, 0))],
            out_specs=pl.BlockSpec((None, tr, C), lambda k, i, core_ref: (k, i, 0))),
        out_shape=_sds((NCHIP, R, C), BF16), compiler_params=_params(("parallel", "parallel")),
    )(core, g8, land)


SMALL = ("b_ada", "rel_bias", "attn_norm_g", "lb_logits", "gnorm_g", "ln1_g", "ln1_b", "ln2_g", "ln2_b")


def _small_update(parts, loss_parts, lbv, ws, ms, vs, max_rel):
    n = len(SMALL)

    def body(*refs):
        part_refs = dict(zip(SMALL, refs[:n]))
        loss_in, lb_ref = refs[n], refs[n + 1]
        w_refs, m_refs, v_refs = refs[n + 2:2 * n + 2], refs[2 * n + 2:3 * n + 2], refs[3 * n + 2:4 * n + 2]
        outs = refs[4 * n + 2:]

        def total(ref):
            tot = ref[0]
            for k in range(1, N_DEV):
                tot = tot + ref[k]
            return tot

        outs[0][...] = jnp.sum(total(loss_in), axis=-1, keepdims=True)
        for idx, name in enumerate(SMALL):
            g = total(part_refs[name])
            if name == "rel_bias":
                g = _dot(g, _bias_onehot(w_refs[idx].shape[1], max_rel), NT, HIGHEST)
            elif name == "lb_logits":
                lb = lb_ref[...]
                sign = (1 - 2 * lax.broadcasted_iota(jnp.int32, (2, 1), 0)).astype(F32)
                g = sign * (g * lb * (1.0 - lb))
            elif name == "gnorm_g":
                g = _colsum(g)
            dw, mm, vv = _adam(w_refs[idx][...], g, m_refs[idx][...], v_refs[idx][...])
            outs[1 + 4 * idx][...] = g
            outs[2 + 4 * idx][...] = dw
            outs[3 + 4 * idx][...] = mm
            outs[4 + 4 * idx][...] = vv

    out_shape = [_sds((1, 1), F32)]
    for w in ws:
        out_shape += [_sds(w.shape, F32)] * 4
    return pl.pallas_call(body, name="small_update", out_shape=out_shape, compiler_params=_params(big=True))(
        *[parts[k] for k in SMALL], loss_parts, lbv, *ws, *ms, *vs)


def _place():
    x, y, c = lax.axis_index("x"), lax.axis_index("y"), lax.axis_index("c")
    return x, y, c, [(1 - x, y), (x, 1 - y), (1 - x, 1 - y)]


def _all_gather(shard, name):
    HBM = pl.BlockSpec(memory_space=pl.ANY)

    def body(x_ref, out_ref, send_sems, recv_sems, local_sem):
        x, y, c, chips = _place()
        me, sibling = (x, y, c), (x, y, 1 - c)

        def slot(px, py, pc):
            return out_ref.at[4 * px + 2 * py + pc]

        def copy(k, block, to, src=None):
            return pltpu.make_async_remote_copy(
                src_ref=slot(*block) if src is None else src, dst_ref=slot(*block),
                send_sem=send_sems.at[k], recv_sem=recv_sems.at[k], device_id=to, device_id_type=MESH)

        mine = pltpu.make_async_copy(x_ref, slot(*me), local_sem)
        mine.start()
        first = [copy(0, me, sibling, src=x_ref)]
        first += [copy(1 + j, me, (*chip, c), src=x_ref) for j, chip in enumerate(chips)]
        for cp in first:
            cp.start()
        passed = [copy(4 + j, (*chip, c), sibling) for j, chip in enumerate(chips)]
        for j, chip in enumerate(chips):
            copy(1 + j, (*chip, c), me).wait_recv()
            passed[j].start()
        copy(0, sibling, me).wait_recv()
        for j, chip in enumerate(chips):
            copy(4 + j, (*chip, 1 - c), me).wait_recv()
        for cp in first + passed:
            cp.wait_send()
        mine.wait()

    return pl.pallas_call(
        body, name=name, out_shape=_sds((N_DEV,) + shard.shape, shard.dtype),
        in_specs=[HBM], out_specs=HBM,
        scratch_shapes=[pltpu.SemaphoreType.DMA((7,)), pltpu.SemaphoreType.DMA((7,)), pltpu.SemaphoreType.DMA(())],
    )(shard)


def _rs_pair(g8, name):
    HBM = pl.BlockSpec(memory_space=pl.ANY)

    def body(g_ref, land_ref, send_sem, recv_sem):
        x, y, c, _ = _place()
        cp = pltpu.make_async_remote_copy(
            src_ref=g_ref.at[1 - c], dst_ref=land_ref, send_sem=send_sem, recv_sem=recv_sem,
            device_id=(x, y, 1 - c), device_id_type=MESH)
        cp.start()
        cp.wait()

    return pl.pallas_call(
        body, name=name, out_shape=_sds(g8.shape[1:], g8.dtype), in_specs=[HBM], out_specs=HBM,
        scratch_shapes=[pltpu.SemaphoreType.DMA(()), pltpu.SemaphoreType.DMA(())],
    )(g8)


def _rs_chips(p4, name):
    HBM = pl.BlockSpec(memory_space=pl.ANY)

    def body(p_ref, land_ref, send_sems, recv_sems, local_sem):
        x, y, c, chips = _place()
        mine = 2 * x + y
        own = pltpu.make_async_copy(p_ref.at[mine], land_ref.at[mine], local_sem)
        own.start()

        def copy(k, chip):
            return pltpu.make_async_remote_copy(
                src_ref=p_ref.at[2 * chip[0] + chip[1]], dst_ref=land_ref.at[mine],
                send_sem=send_sems.at[k], recv_sem=recv_sems.at[k], device_id=(*chip, c), device_id_type=MESH)

        def arrival(k, chip):
            return pltpu.make_async_remote_copy(
                src_ref=p_ref.at[mine], dst_ref=land_ref.at[2 * chip[0] + chip[1]],
                send_sem=send_sems.at[k], recv_sem=recv_sems.at[k], device_id=(*chip, c), device_id_type=MESH)

        sends = [copy(k, chip) for k, chip in enumerate(chips)]
        for cp in sends:
            cp.start()
        for k, chip in enumerate(chips):
            arrival(k, chip).wait_recv()
        for cp in sends:
            cp.wait_send()
        own.wait()

    return pl.pallas_call(
        body, name=name, out_shape=_sds(p4.shape, p4.dtype), in_specs=[HBM], out_specs=HBM,
        scratch_shapes=[pltpu.SemaphoreType.DMA((3,)), pltpu.SemaphoreType.DMA((3,)), pltpu.SemaphoreType.DMA(())],
    )(p4)


def _reduce_scatter(g8, core, tag):
    land = _rs_pair(g8, "rs_pair_" + tag)
    return _rs_chips(_pair_sum(g8, land, core, "rs_pair_sum_" + tag), "rs_chips_" + tag)


BIG = ("w_in", "w_o", "w_ffn_in", "w_ffn_out")
ORDER = ("w_ada", "b_ada", "w_in", "rel_bias", "attn_norm_g", "lb_logits", "gnorm_g", "w_o", "ln1_g", "ln1_b",
         "w_ffn_in", "w_ffn_out", "ln2_g", "ln2_b")


def kernel(x, c, w_ada, b_ada, w_in, rel_bias, attn_norm_g, lb_logits, gnorm_g, w_o, ln1_g, ln1_b, w_ffn_in, w_ffn_out, ln2_g, ln2_b, loss_target, m_w_ada, m_b_ada, m_w_in, m_rel_bias, m_attn_norm_g, m_lb_logits, m_gnorm_g, m_w_o, m_ln1_g, m_ln1_b, m_w_ffn_in, m_w_ffn_out, m_ln2_g, m_ln2_b, v_w_ada, v_b_ada, v_w_in, v_rel_bias, v_attn_norm_g, v_lb_logits, v_gnorm_g, v_w_o, v_ln1_g, v_ln1_b, v_w_ffn_in, v_w_ffn_out, v_ln2_g, v_ln2_b):
    W = dict(w_ada=w_ada, b_ada=b_ada, w_in=w_in, rel_bias=rel_bias, attn_norm_g=attn_norm_g, lb_logits=lb_logits,
             gnorm_g=gnorm_g, w_o=w_o, ln1_g=ln1_g, ln1_b=ln1_b, w_ffn_in=w_ffn_in, w_ffn_out=w_ffn_out,
             ln2_g=ln2_g, ln2_b=ln2_b)
    M = dict(w_ada=m_w_ada, b_ada=m_b_ada, w_in=m_w_in, rel_bias=m_rel_bias, attn_norm_g=m_attn_norm_g,
             lb_logits=m_lb_logits, gnorm_g=m_gnorm_g, w_o=m_w_o, ln1_g=m_ln1_g, ln1_b=m_ln1_b,
             w_ffn_in=m_w_ffn_in, w_ffn_out=m_w_ffn_out, ln2_g=m_ln2_g, ln2_b=m_ln2_b)
    V = dict(w_ada=v_w_ada, b_ada=v_b_ada, w_in=v_w_in, rel_bias=v_rel_bias, attn_norm_g=v_attn_norm_g,
             lb_logits=v_lb_logits, gnorm_g=v_gnorm_g, w_o=v_w_o, ln1_g=v_ln1_g, ln1_b=v_ln1_b,
             w_ffn_in=v_w_ffn_in, w_ffn_out=v_w_ffn_out, ln2_g=v_ln2_g, ln2_b=v_ln2_b)

    x2, tgt = x[0], loss_target[0]
    T, D = x2.shape
    AW, RW = attn_norm_g.shape[-1], lb_logits.shape[-1]
    MIX = AW + RW
    H, RH = AW // ATTN_HEAD_DIM, RW // LANE
    RB = rel_bias.shape[-1]
    max_rel = (RB - 1) // 2
    rbp = -(-RB // LANE) * LANE
    F = w_ffn_out.shape[1] * N_DEV
    half = N_DEV // 2
    xi, yi, ci = lax.axis_index("x"), lax.axis_index("y"), lax.axis_index("c")
    me = 4 * xi + 2 * yi + ci
    core = jnp.reshape(ci, (1,)).astype(jnp.int32)
    pad_rb = lambda a: jnp.pad(a[0], ((0, 0), (0, rbp - RB)))

    c_act, lbv, gv = _prep(c, lb_logits, pad_rb(rel_bias), max_rel)
    c_all = _all_gather(c_act, "ag_c").reshape(N_DEV, D)
    ns_ada = w_ada.shape[-1]
    mod_part = _mod_part(c_all, w_ada[0], lax.dynamic_slice_in_dim(b_ada, me * ns_ada, ns_ada, axis=1))
    mod_all = _all_gather(mod_part, "ag_mod")
    mod6 = lax.dynamic_index_in_dim(mod_all, me, axis=1, keepdims=False).reshape(6, D)

    wg_in = _all_gather(w_in[0].astype(BF16), "ag_w_in")
    wg_o = _all_gather(w_o[0].astype(BF16), "ag_w_o").reshape(MIX, D)
    wg_f1 = _all_gather(w_ffn_in[0].astype(BF16), "ag_w_ffn_in")
    wg_f2 = _all_gather(w_ffn_out[0].astype(BF16), "ag_w_ffn_out").reshape(F, D)

    h1 = _ln_mod(x2, mod6)
    proj = _mm_gathered(h1, wg_in, "in_proj")
    mix_a = _attn_fwd(proj, gv, attn_norm_g, AW)
    mix_b, o_b, st_all = _hgrn_fwd(proj, lbv, gnorm_g, AW, RW)
    mixin = jnp.concatenate([mix_a, mix_b], axis=1)
    mix = _mm_nn(mixin, wg_o, "out_proj")
    x1, h2 = _mid_fwd(x2, mix, mod6, ln1_g, ln1_b)
    gu, act = _mm_swiglu(h2, wg_f1)
    ff = _mm_nn(act, wg_f2, "ffn_out")
    dff, dx1a, vec_a = _final(x1, ff, mod6, ln2_g, ln2_b, tgt)

    du = _mm_swiglu_bwd(dff, wg_f2, gu)
    gw_f2 = _mm_tn_rows(act, dff, F // N_DEV, "grad_w_ffn_out")
    tm = _tile(T, 512, 16)
    du_ij = lambda tm_, ns: pl.BlockSpec((None, tm_, ns), lambda i, j: (j // half, i, j % half))
    du_jm = lambda tm_, ns: pl.BlockSpec((None, tm_, ns), lambda j, m: (j // half, m, j % half))
    dh2 = _mm_gathered_nt(du, du_ij, wg_f1, T, tm, "ffn_in_bwd")
    gw_f1 = _mm_tn_gathered(h2, du, du_jm, wg_f1.shape[-1], tm, "grad_w_ffn_in")
    dmix, dxa, vec_b = _mid_bwd(x2, mix, x1, dx1a, dh2, mod6, ln1_g)
    dmixin = _mm_nt(dmix, wg_o, "out_proj_bwd")
    gw_o = _mm_tn_rows(mixin, dmix, MIX // N_DEV, "grad_w_o")
    dq, dk, dv, dgv, dga = _attn_bwd(proj, dmixin, gv, attn_norm_g, AW)
    dqb, dfl, dib, dgb, dlb, dgn = _hgrn_bwd(proj, dmixin, o_b, st_all, lbv, gnorm_g, AW, RW)
    dproj = jnp.concatenate([dq, dk, dv, dqb, dfl, dib, dgb], axis=1)
    p_ij = lambda tm_, ns: pl.BlockSpec((tm_, ns), lambda i, j: (i, j))
    p_jm = lambda tm_, ns: pl.BlockSpec((tm_, ns), lambda j, m: (m, j))
    dh1 = _mm_gathered_nt(dproj, p_ij, wg_in, T, tm, "in_proj_bwd")
    gw_in = _mm_tn_gathered(h1, dproj, p_jm, wg_in.shape[-1], tm, "grad_w_in")
    grad_x, vec_c = _first_bwd(x2, dh1, dxa, mod6)

    dmod = jnp.concatenate([vec_c[1:2], vec_c[0:1], vec_b[4:5], vec_b[1:2], vec_b[0:1], vec_a[2:3]], axis=0)
    pieces = dict(b_ada=dmod, rel_bias=dgv, attn_norm_g=dga, lb_logits=dlb, gnorm_g=dgn, ln1_g=vec_b[2:3],
                  ln1_b=vec_b[3:4], ln2_g=vec_a[0:1], ln2_b=vec_a[1:2], loss=vec_a[3:4])
    widths = dict(b_ada=(1, 6 * D), rel_bias=(H, BAND), attn_norm_g=(1, AW), lb_logits=(1, RW), gnorm_g=(RH, LANE),
                  ln1_g=(1, D), ln1_b=(1, D), ln2_g=(1, D), ln2_b=(1, D), loss=(1, D))
    packed = jnp.concatenate([pieces[k].reshape(-1, LANE) for k in widths], axis=0)
    gathered = _all_gather(packed, "ag_small")
    parts, r0 = {}, 0
    for k, (rows, width) in widths.items():
        nr = rows * width // LANE
        parts[k] = gathered[:, r0:r0 + nr, :].reshape(N_DEV, rows, width)
        r0 += nr
    prep_small = lambda d, k: pad_rb(d[k]) if k == "rel_bias" else d[k]
    small = _small_update(parts, parts["loss"], lbv, [prep_small(W, k) for k in SMALL],
                          [prep_small(M, k) for k in SMALL], [prep_small(V, k) for k in SMALL], max_rel)
    loss = small[0].reshape(())
    res = {}
    for idx, k in enumerate(SMALL):
        four = small[1 + 4 * idx:5 + 4 * idx]
        if k == "rel_bias":
            four = [a[:, :RB][None] for a in four]
        res[k] = list(four)

    dmod_s = lax.dynamic_slice_in_dim(parts["b_ada"].reshape(N_DEV, 6 * D), me * ns_ada, ns_ada, axis=1)
    res["w_ada"] = [a[None] for a in _adam_ada(c_all, dmod_s, w_ada[0], m_w_ada[0], v_w_ada[0])]
    full = dict(w_in=gw_in, w_o=gw_o, w_ffn_in=gw_f1, w_ffn_out=gw_f2)
    for k in BIG:
        R, C = W[k].shape[1:]
        sums = _reduce_scatter(full[k].reshape(2, half, R, C), core, k)
        res[k] = [a[None] for a in _adam_shard(sums, W[k][0], M[k][0], V[k][0], "adam_" + k)]

    out = [loss, grad_x[None]]
    for field in range(4):
        out += [res[k][field] for k in ORDER]
    return tuple(out)
```

```python
import functools

import jax
import jax.numpy as jnp
from jax import lax
from jax.experimental import pallas as pl
from jax.experimental.pallas import tpu as pltpu

F32 = jnp.float32
BF16 = jnp.bfloat16
MESH = pl.DeviceIdType.MESH
HIGHEST = lax.Precision.HIGHEST

N_DEV = 8
CHUNK = 64
N_PAST = 8
KPAD = (N_PAST + 1) * CHUNK
BAND = (N_PAST + 2) * CHUNK
ATTN_HEAD_DIM = 64
REC_HEAD_DIM = 128
SUB = 16
LANE = 128
EPS = 1e-5
ALPHA = 2.0 ** 0.25
ADAM_LR, ADAM_B1, ADAM_B2, ADAM_EPS, ADAM_WD, ADAM_STEP = 0.001, 0.9, 0.999, 1e-08, 0.01, 10
NEG = -1e30
VMEM_LIMIT = 56 * 1024 * 1024


def _sds(shape, dtype):
    return jax.ShapeDtypeStruct(tuple(shape), dtype)


def _tile(n, pref, mult):
    best = None
    for t in range(mult, min(n, pref) + 1, mult):
        if n % t == 0:
            best = t
    return n if best is None else best


def _params(sem=None, big=False):
    kw = {}
    if sem is not None:
        kw["dimension_semantics"] = sem
    if big:
        kw["vmem_limit_bytes"] = VMEM_LIMIT
    return pltpu.CompilerParams(**kw)


def _sigmoid(v):
    return 1.0 / (1.0 + jnp.exp(-v))


def _dot(a, b, dims, precision=None):
    return lax.dot_general(a, b, (dims, ((), ())), preferred_element_type=F32, precision=precision)


NN = ((1,), (0,))
NT = ((1,), (1,))
TN = ((0,), (0,))


def _ln(v):
    mu = jnp.mean(v, axis=-1, keepdims=True)
    d = v - mu
    rstd = lax.rsqrt(jnp.mean(d * d, axis=-1, keepdims=True) + EPS)
    return d * rstd, rstd


def _ln_bwd(dxh, xh, rstd):
    return rstd * (dxh - jnp.mean(dxh, axis=-1, keepdims=True) - xh * jnp.mean(dxh * xh, axis=-1, keepdims=True))


def _colsum(v):
    return jnp.sum(v, axis=0, keepdims=True)


def _ln_mod(x2, mod6):
    T, D = x2.shape
    tm = _tile(T, 256, 8)

    def body(x_ref, mod_ref, o_ref):
        xh, _ = _ln(x_ref[...])
        o_ref[...] = (xh * (1.0 + mod_ref[1:2, :]) + mod_ref[0:1, :]).astype(BF16)

    return pl.pallas_call(
        body, grid=(T // tm,), name="ln_mod",
        in_specs=[pl.BlockSpec((tm, D), lambda i: (i, 0)), pl.BlockSpec((6, D), lambda i: (0, 0))],
        out_specs=pl.BlockSpec((tm, D), lambda i: (i, 0)),
        out_shape=_sds((T, D), BF16), compiler_params=_params(("parallel",)),
    )(x2, mod6)


def _mid_fwd(x2, mix, mod6, ln1_g, ln1_b):
    T, D = x2.shape
    tm = _tile(T, 256, 8)

    def body(x_ref, mix_ref, mod_ref, g_ref, b_ref, x1_ref, h2_ref):
        zh, _ = _ln(ALPHA * x_ref[...] + mod_ref[2:3, :] * mix_ref[...])
        x1 = zh * g_ref[...] + b_ref[...]
        x1_ref[...] = x1
        xh, _ = _ln(x1)
        h2_ref[...] = (xh * (1.0 + mod_ref[4:5, :]) + mod_ref[3:4, :]).astype(BF16)

    row = pl.BlockSpec((tm, D), lambda i: (i, 0))
    vec = pl.BlockSpec((1, D), lambda i: (0, 0))
    return pl.pallas_call(
        body, grid=(T // tm,), name="mid_fwd",
        in_specs=[row, row, pl.BlockSpec((6, D), lambda i: (0, 0)), vec, vec],
        out_specs=[row, row],
        out_shape=[_sds((T, D), F32), _sds((T, D), BF16)], compiler_params=_params(("parallel",)),
    )(x2, mix, mod6, ln1_g, ln1_b)


def _final(x1, ff, mod6, ln2_g, ln2_b, tgt):
    T, D = x1.shape
    tm = _tile(T, 256, 8)

    def body(x1_ref, ff_ref, mod_ref, g_ref, b_ref, t_ref, dff_ref, dx1_ref, vec_ref):
        @pl.when(pl.program_id(0) == 0)
        def _():
            vec_ref[...] = jnp.zeros_like(vec_ref)

        ff_v = ff_ref[...]
        gate2 = mod_ref[5:6, :]
        zh, rstd = _ln(ALPHA * x1_ref[...] + gate2 * ff_v)
        err = zh * g_ref[...] + b_ref[...] - t_ref[...]
        dy = err * (1.0 / D)
        dz = _ln_bwd(dy * g_ref[...], zh, rstd)
        dff_ref[...] = (gate2 * dz).astype(BF16)
        dx1_ref[...] = ALPHA * dz
        vec_ref[0:1, :] += _colsum(dy * zh)
        vec_ref[1:2, :] += _colsum(dy)
        vec_ref[2:3, :] += _colsum(dz * ff_v)
        vec_ref[3:4, :] += _colsum(err * err) * (0.5 / D)

    row = pl.BlockSpec((tm, D), lambda i: (i, 0))
    vec = pl.BlockSpec((1, D), lambda i: (0, 0))
    return pl.pallas_call(
        body, grid=(T // tm,), name="final_fwd_bwd",
        in_specs=[row, row, pl.BlockSpec((6, D), lambda i: (0, 0)), vec, vec, row],
        out_specs=[row, row, pl.BlockSpec((8, D), lambda i: (0, 0))],
        out_shape=[_sds((T, D), BF16), _sds((T, D), F32), _sds((8, D), F32)],
        compiler_params=_params(("arbitrary",)),
    )(x1, ff, mod6, ln2_g, ln2_b, tgt)


def _mid_bwd(x2, mix, x1, dx1a, dh2, mod6, ln1_g):
    T, D = x2.shape
    tm = _tile(T, 256, 8)

    def body(x_ref, mix_ref, x1_ref, dx1a_ref, dh2_ref, mod_ref, g_ref, dmix_ref, dxa_ref, vec_ref):
        @pl.when(pl.program_id(0) == 0)
        def _():
            vec_ref[...] = jnp.zeros_like(vec_ref)

        dh2 = dh2_ref[...]
        xh, rstd = _ln(x1_ref[...])
        dx1 = dx1a_ref[...] + _ln_bwd(dh2 * (1.0 + mod_ref[4:5, :]), xh, rstd)
        mix_v = mix_ref[...]
        gate1 = mod_ref[2:3, :]
        zh, rstdz = _ln(ALPHA * x_ref[...] + gate1 * mix_v)
        dz = _ln_bwd(dx1 * g_ref[...], zh, rstdz)
        dmix_ref[...] = (gate1 * dz).astype(BF16)
        dxa_ref[...] = ALPHA * dz
        vec_ref[0:1, :] += _colsum(dh2 * xh)
        vec_ref[1:2, :] += _colsum(dh2)
        vec_ref[2:3, :] += _colsum(dx1 * zh)
        vec_ref[3:4, :] += _colsum(dx1)
        vec_ref[4:5, :] += _colsum(dz * mix_v)

    row = pl.BlockSpec((tm, D), lambda i: (i, 0))
    vec = pl.BlockSpec((1, D), lambda i: (0, 0))
    return pl.pallas_call(
        body, grid=(T // tm,), name="mid_bwd",
        in_specs=[row, row, row, row, row, pl.BlockSpec((6, D), lambda i: (0, 0)), vec],
        out_specs=[row, row, pl.BlockSpec((8, D), lambda i: (0, 0))],
        out_shape=[_sds((T, D), BF16), _sds((T, D), F32), _sds((8, D), F32)],
        compiler_params=_params(("arbitrary",)),
    )(x2, mix, x1, dx1a, dh2, mod6, ln1_g)


def _first_bwd(x2, dh1, dxa, mod6):
    T, D = x2.shape
    tm = _tile(T, 256, 8)

    def body(x_ref, dh1_ref, dxa_ref, mod_ref, gx_ref, vec_ref):
        @pl.when(pl.program_id(0) == 0)
        def _():
            vec_ref[...] = jnp.zeros_like(vec_ref)

        dh1 = dh1_ref[...]
        xh, rstd = _ln(x_ref[...])
        gx_ref[...] = dxa_ref[...] + _ln_bwd(dh1 * (1.0 + mod_ref[1:2, :]), xh, rstd)
        vec_ref[0:1, :] += _colsum(dh1 * xh)
        vec_ref[1:2, :] += _colsum(dh1)

    row = pl.BlockSpec((tm, D), lambda i: (i, 0))
    return pl.pallas_call(
        body, grid=(T // tm,), name="first_bwd",
        in_specs=[row, row, row, pl.BlockSpec((6, D), lambda i: (0, 0))],
        out_specs=[row, pl.BlockSpec((8, D), lambda i: (0, 0))],
        out_shape=[_sds((T, D), F32), _sds((8, D), F32)],
        compiler_params=_params(("arbitrary",)),
    )(x2, dh1, dxa, mod6)


def _slot(j):
    return (j % 2) * 4 + j // 2


def _mm_gathered(a, wg, name):
    M, K = a.shape
    _, _, ns = wg.shape
    tm = _tile(M, 512, 16)

    def body(a_ref, w_ref, o_ref):
        o_ref[...] = _dot(a_ref[...], w_ref[...], NN)

    return pl.pallas_call(
        body, grid=(N_DEV, M // tm), name=name,
        in_specs=[pl.BlockSpec((tm, K), lambda j, i: (i, 0)), pl.BlockSpec((None, K, ns), lambda j, i: (j, 0, 0))],
        out_specs=pl.BlockSpec((tm, ns), lambda j, i: (i, j)),
        out_shape=_sds((M, N_DEV * ns), F32), compiler_params=_params(("parallel", "parallel"), big=True),
    )(a, wg)


def _mm_nn(a, b, name):
    M, K = a.shape
    _, N = b.shape
    tm, tn, tk = _tile(M, 512, 16), _tile(N, 1024, LANE), _tile(K, 2048, LANE)

    def body(a_ref, b_ref, o_ref):
        @pl.when(pl.program_id(2) == 0)
        def _():
            o_ref[...] = jnp.zeros_like(o_ref)

        o_ref[...] += _dot(a_ref[...], b_ref[...], NN)

    return pl.pallas_call(
        body, grid=(M // tm, N // tn, K // tk), name=name,
        in_specs=[pl.BlockSpec((tm, tk), lambda i, j, k: (i, k)), pl.BlockSpec((tk, tn), lambda i, j, k: (k, j))],
        out_specs=pl.BlockSpec((tm, tn), lambda i, j, k: (i, j)),
        out_shape=_sds((M, N), F32), compiler_params=_params(("parallel", "parallel", "arbitrary"), big=True),
    )(a, b)


def _mm_nt(a, b, name):
    M, K = a.shape
    N, _ = b.shape
    tm, tn = _tile(M, 512, 16), _tile(N, 1024, LANE)

    def body(a_ref, b_ref, o_ref):
        o_ref[...] = _dot(a_ref[...], b_ref[...], NT)

    return pl.pallas_call(
        body, grid=(M // tm, N // tn), name=name,
        in_specs=[pl.BlockSpec((tm, K), lambda i, j: (i, 0)), pl.BlockSpec((tn, K), lambda i, j: (j, 0))],
        out_specs=pl.BlockSpec((tm, tn), lambda i, j: (i, j)),
        out_shape=_sds((M, N), F32), compiler_params=_params(("parallel", "parallel"), big=True),
    )(a, b)


def _mm_swiglu(h2, wg):
    M, K = h2.shape
    _, _, ns = wg.shape
    half = N_DEV // 2
    tm = _tile(M, 256, 16)

    def body(a_ref, wgate_ref, wup_ref, gu_ref, act_ref):
        a = a_ref[...]
        g = _dot(a, wgate_ref[...], NN)
        u = _dot(a, wup_ref[...], NN)
        gu_ref[0] = g
        gu_ref[1] = u
        act_ref[...] = (g * _sigmoid(g) * u).astype(BF16)

    return pl.pallas_call(
        body, grid=(half, M // tm), name="ffn_in_swiglu",
        in_specs=[pl.BlockSpec((tm, K), lambda j, i: (i, 0)),
                  pl.BlockSpec((None, K, ns), lambda j, i: (j, 0, 0)),
                  pl.BlockSpec((None, K, ns), lambda j, i: (j + half, 0, 0))],
        out_specs=[pl.BlockSpec((2, tm, ns), lambda j, i: (0, i, j)), pl.BlockSpec((tm, ns), lambda j, i: (i, j))],
        out_shape=[_sds((2, M, half * ns), F32), _sds((M, half * ns), BF16)],
        compiler_params=_params(("parallel", "parallel"), big=True),
    )(h2, wg, wg)


def _mm_swiglu_bwd(dff, w2, gu):
    M, K = dff.shape
    F = w2.shape[0]
    tm, tn = _tile(M, 256, 16), _tile(F, 1408, LANE)

    def body(a_ref, b_ref, gu_ref, du_ref):
        da = _dot(a_ref[...], b_ref[...], NT)
        g = gu_ref[0]
        u = gu_ref[1]
        sg = _sigmoid(g)
        du_ref[0] = (da * u * (sg * (1.0 + g * (1.0 - sg)))).astype(BF16)
        du_ref[1] = (da * (g * sg)).astype(BF16)

    return pl.pallas_call(
        body, grid=(M // tm, F // tn), name="ffn_out_bwd_swiglu",
        in_specs=[pl.BlockSpec((tm, K), lambda i, j: (i, 0)), pl.BlockSpec((tn, K), lambda i, j: (j, 0)),
                  pl.BlockSpec((2, tm, tn), lambda i, j: (0, i, j))],
        out_specs=pl.BlockSpec((2, tm, tn), lambda i, j: (0, i, j)),
        out_shape=_sds((2, M, F), BF16), compiler_params=_params(("parallel", "parallel"), big=True),
    )(dff, w2, gu)


ORDER_ONLY = pl.BlockSpec(memory_space=pl.ANY)


def _mm_tn_rows(dep, a, b, rs, name):
    M, Ka = a.shape
    _, N = b.shape
    tm = _tile(M, 512, 16)

    def body(_, a_ref, b_ref, o_ref, acc_ref):
        m = pl.program_id(1)

        @pl.when(m == 0)
        def _():
            acc_ref[...] = jnp.zeros_like(acc_ref)

        acc_ref[...] += _dot(a_ref[...], b_ref[...], TN)

        @pl.when(m == pl.num_programs(1) - 1)
        def _():
            o_ref[0, 0] = acc_ref[0:rs, :].astype(BF16)
            o_ref[1, 0] = acc_ref[rs:2 * rs, :].astype(BF16)

    return pl.pallas_call(
        body, grid=(N_DEV // 2, M // tm), name=name,
        in_specs=[ORDER_ONLY, pl.BlockSpec((tm, 2 * rs), lambda ch, m: (m, ch)),
                  pl.BlockSpec((tm, N), lambda ch, m: (m, 0))],
        out_specs=pl.BlockSpec((2, 1, rs, N), lambda ch, m: (0, ch, 0, 0)),
        out_shape=_sds((2, N_DEV // 2, rs, N), BF16),
        scratch_shapes=[pltpu.VMEM((2 * rs, N), F32)],
        compiler_params=_params(("parallel", "arbitrary"), big=True),
    )(dep, a, b)


def _mm_gathered_nt(dep, a, a_spec, wg, M, tm, name):
    _, K, ns = wg.shape

    def body(_, a_ref, w_ref, o_ref):
        @pl.when(pl.program_id(1) == 0)
        def _():
            o_ref[...] = jnp.zeros_like(o_ref)

        o_ref[...] += _dot(a_ref[...], w_ref[...], NT)

    return pl.pallas_call(
        body, grid=(M // tm, N_DEV), name=name,
        in_specs=[ORDER_ONLY, a_spec(tm, ns), pl.BlockSpec((None, K, ns), lambda i, j: (j, 0, 0))],
        out_specs=pl.BlockSpec((tm, K), lambda i, j: (i, 0)),
        out_shape=_sds((M, K), F32), compiler_params=_params(("parallel", "arbitrary"), big=True),
    )(dep, a, wg)


def _mm_tn_gathered(dep, h, a, a_spec, ns, tm, name):
    M, K = h.shape

    def body(_, h_ref, a_ref, o_ref, acc_ref):
        m = pl.program_id(1)

        @pl.when(m == 0)
        def _():
            acc_ref[...] = jnp.zeros_like(acc_ref)

        acc_ref[...] += _dot(h_ref[...], a_ref[...], TN)

        @pl.when(m == pl.num_programs(1) - 1)
        def _():
            o_ref[...] = acc_ref[...].astype(BF16)

    return pl.pallas_call(
        body, grid=(N_DEV, M // tm), name=name,
        in_specs=[ORDER_ONLY, pl.BlockSpec((tm, K), lambda j, m: (m, 0)), a_spec(tm, ns)],
        out_specs=pl.BlockSpec((None, K, ns), lambda j, m: (_slot(j), 0, 0)),
        out_shape=_sds((N_DEV, K, ns), BF16),
        scratch_shapes=[pltpu.VMEM((K, ns), F32)],
        compiler_params=_params(("parallel", "arbitrary"), big=True),
    )(dep, h, a)


def _bias_onehot(rbp, max_rel):
    r = lax.broadcasted_iota(jnp.int32, (rbp, BAND), 0)
    m = lax.broadcasted_iota(jnp.int32, (rbp, BAND), 1)
    return (r == jnp.minimum(KPAD - m, max_rel) + max_rel).astype(F32)


def _attn_setup(i, hp, k_ref, v_ref, gv_ref, kpad, vpad, bias):
    ls = slice(i * ATTN_HEAD_DIM, (i + 1) * ATTN_HEAD_DIM)
    kpad[i, 0:KPAD, :] = jnp.zeros((KPAD, ATTN_HEAD_DIM), BF16)
    vpad[i, 0:KPAD, :] = jnp.zeros((KPAD, ATTN_HEAD_DIM), BF16)
    kpad[i, KPAD:, :] = k_ref[:, ls].astype(BF16)
    vpad[i, KPAD:, :] = v_ref[:, ls].astype(BF16)
    gvrow = gv_ref[pl.ds(hp * 2 + i, 1), :]
    bias[i] = pltpu.roll(jnp.broadcast_to(gvrow, (CHUNK, BAND)), 0, 1, stride=1, stride_axis=0)


def _attn_probs(n, i, q_ref, kpad, vpad, bias, col):
    ls = slice(i * ATTN_HEAD_DIM, (i + 1) * ATTN_HEAD_DIM)
    r0 = pl.multiple_of(n * CHUNK, CHUNK)
    q = q_ref[pl.ds(r0, CHUNK), ls].astype(BF16)
    kb = kpad[i, pl.ds(r0, BAND), :]
    vb = vpad[i, pl.ds(r0, BAND), :]
    s = _dot(q, kb, NT) * (ATTN_HEAD_DIM ** -0.5) + bias[i]
    valid = col >= jnp.maximum(CHUNK, (N_PAST + 1 - n) * CHUNK)
    s = jnp.where(valid, s, NEG)
    p = jnp.exp(s - jnp.max(s, axis=-1, keepdims=True))
    pn = p / jnp.sum(p, axis=-1, keepdims=True)
    return r0, ls, q, kb, vb, pn


def _attn_fwd(proj, gv, ga, AW):
    T = proj.shape[0]
    HP = AW // LANE

    def body(q_ref, k_ref, v_ref, gv_ref, ga_ref, o_ref, kpad, vpad, bias):
        hp = pl.program_id(0)
        for i in range(2):
            _attn_setup(i, hp, k_ref, v_ref, gv_ref, kpad, vpad, bias)
        col = lax.broadcasted_iota(jnp.int32, (CHUNK, BAND), 1)

        def chunk(n, carry):
            for i in range(2):
                r0, ls, _, _, vb, pn = _attn_probs(n, i, q_ref, kpad, vpad, bias, col)
                o = _dot(pn.astype(BF16), vb, NN)
                r = lax.rsqrt(jnp.mean(o * o, axis=-1, keepdims=True) + EPS)
                o_ref[pl.ds(r0, CHUNK), ls] = (o * r * ga_ref[0:1, ls]).astype(BF16)
            return carry

        lax.fori_loop(0, T // CHUNK, chunk, 0)

    blk = lambda off: pl.BlockSpec((T, LANE), lambda hp: (0, off + hp))
    return pl.pallas_call(
        body, grid=(HP,), name="attn_fwd",
        in_specs=[blk(0), blk(HP), blk(2 * HP), pl.BlockSpec(gv.shape, lambda hp: (0, 0)),
                  pl.BlockSpec((1, LANE), lambda hp: (0, hp))],
        out_specs=pl.BlockSpec((T, LANE), lambda hp: (0, hp)),
        out_shape=_sds((T, AW), BF16),
        scratch_shapes=[pltpu.VMEM((2, T + KPAD, ATTN_HEAD_DIM), BF16), pltpu.VMEM((2, T + KPAD, ATTN_HEAD_DIM), BF16),
                        pltpu.VMEM((2, CHUNK, BAND), F32)],
        compiler_params=_params(("parallel",), big=True),
    )(proj, proj, proj, gv, ga)


def _attn_bwd(proj, dmixin, gv, ga, AW):
    T = proj.shape[0]
    HP = AW // LANE
    scale = ATTN_HEAD_DIM ** -0.5

    def body(q_ref, k_ref, v_ref, dn_ref, gv_ref, ga_ref, dq_ref, dk_ref, dv_ref, dgv_ref, dga_ref,
             kpad, vpad, dkacc, dvacc, bias, dbias):
        hp = pl.program_id(0)
        for i in range(2):
            _attn_setup(i, hp, k_ref, v_ref, gv_ref, kpad, vpad, bias)
        dkacc[...] = jnp.zeros_like(dkacc)
        dvacc[...] = jnp.zeros_like(dvacc)
        dbias[...] = jnp.zeros_like(dbias)
        dga_ref[...] = jnp.zeros_like(dga_ref)
        col = lax.broadcasted_iota(jnp.int32, (CHUNK, BAND), 1)

        def chunk(n, carry):
            for i in range(2):
                r0, ls, q, kb, vb, pn = _attn_probs(n, i, q_ref, kpad, vpad, bias, col)
                pn_b = pn.astype(BF16)
                o = _dot(pn_b, vb, NN)
                r = lax.rsqrt(jnp.mean(o * o, axis=-1, keepdims=True) + EPS)
                dn = dn_ref[pl.ds(r0, CHUNK), ls]
                dga_ref[i:i + 1, :] += _colsum(dn * o * r)
                a = dn * ga_ref[0:1, ls]
                do = r * (a - o * (r * r) * jnp.mean(a * o, axis=-1, keepdims=True))
                do_b = do.astype(BF16)
                dp = _dot(do_b, vb, NT)
                dvacc[i, pl.ds(r0, BAND), :] += _dot(pn_b, do_b, TN)
                ds = pn * (dp - jnp.sum(pn * dp, axis=-1, keepdims=True))
                dbias[i] += ds
                ds_b = ds.astype(BF16)
                dq_ref[pl.ds(r0, CHUNK), ls] = (_dot(ds_b, kb, NN) * scale).astype(BF16)
                dkacc[i, pl.ds(r0, BAND), :] += _dot(ds_b, q, TN) * scale
            return carry

        lax.fori_loop(0, T // CHUNK, chunk, 0)

        rr = lax.broadcasted_iota(jnp.int32, (CHUNK, CHUNK), 0)
        cc = lax.broadcasted_iota(jnp.int32, (CHUNK, CHUNK), 1)
        flip = (rr + cc == CHUNK - 1).astype(F32)
        for i in range(2):
            ls = slice(i * ATTN_HEAD_DIM, (i + 1) * ATTN_HEAD_DIM)
            dk_ref[:, ls] = dkacc[i, KPAD:, :].astype(BF16)
            dv_ref[:, ls] = dvacc[i, KPAD:, :].astype(BF16)
            rev = _dot(flip, dbias[i], NN, HIGHEST)
            dgv_ref[i:i + 1, :] = _colsum(pltpu.roll(rev, BAND - (CHUNK - 1), 1, stride=1, stride_axis=0))

    blk = lambda off: pl.BlockSpec((T, LANE), lambda hp: (0, off + hp))
    accs = lambda dt: pltpu.VMEM((2, T + KPAD, ATTN_HEAD_DIM), dt)
    return pl.pallas_call(
        body, grid=(HP,), name="attn_bwd",
        in_specs=[blk(0), blk(HP), blk(2 * HP), blk(0), pl.BlockSpec(gv.shape, lambda hp: (0, 0)),
                  pl.BlockSpec((1, LANE), lambda hp: (0, hp))],
        out_specs=[blk(0), blk(0), blk(0), pl.BlockSpec((None, 2, BAND), lambda hp: (hp, 0, 0)),
                   pl.BlockSpec((None, 2, ATTN_HEAD_DIM), lambda hp: (hp, 0, 0))],
        out_shape=[_sds((T, AW), BF16), _sds((T, AW), BF16), _sds((T, AW), BF16),
                   _sds((HP, 2, BAND), F32), _sds((HP, 2, ATTN_HEAD_DIM), F32)],
        scratch_shapes=[accs(BF16), accs(BF16), accs(F32), accs(F32),
                        pltpu.VMEM((2, CHUNK, BAND), F32), pltpu.VMEM((2, CHUNK, BAND), F32)],
        compiler_params=_params(("parallel",), big=True),
    )(proj, proj, proj, dmixin, gv, ga)


def _ltri():
    r = lax.broadcasted_iota(jnp.int32, (CHUNK, CHUNK), 0)
    c = lax.broadcasted_iota(jnp.int32, (CHUNK, CHUNK), 1)
    return (c <= r).astype(F32)


def _hgrn_gates(n, q_ref, f_ref, lb_ref, ltri):
    r0 = pl.multiple_of(n * CHUNK, CHUNK)
    rows = pl.ds(r0, CHUNK)
    lb = lb_ref[...]
    qb = q_ref[rows, :]
    sg = _sigmoid(f_ref[rows, :])
    f = lb + (1.0 - lb) * sg
    sq = _sigmoid(qb)
    b = _dot(ltri, jnp.log(f), NN, HIGHEST)
    return rows, lb, qb, sg, f, 1.0 - f, sq, qb * sq, b


def _hgrn_fwd(proj, lb, gn, AW, RW):
    T = proj.shape[0]
    RH, NC, NSUB = RW // LANE, T // CHUNK, CHUNK // SUB
    base = 3 * AW // LANE

    def body(q_ref, f_ref, i_ref, g_ref, lb_ref, gn_ref, mix_ref, o_ref, stall_ref, st, bs, kks, ics):
        st[...] = jnp.zeros_like(st)
        ltri = _ltri()
        rowi = lax.broadcasted_iota(jnp.int32, (SUB, 1), 0)

        def chunk(n, carry):
            rows, _, _, _, _, kk, _, qs, b = _hgrn_gates(n, q_ref, f_ref, lb_ref, ltri)
            ic = i_ref[rows, :]
            stv = st[...]
            stall_ref[n] = stv
            bs[...] = b
            kks[...] = kk
            ics[...] = ic
            o = _dot((qs * jnp.exp(b)).astype(BF16), stv.astype(BF16), NT)
            ic_b = ic.astype(BF16)
            pieces = []
            for blk in range(NSUB):
                s0 = blk * SUB
                bI, qI = b[s0:s0 + SUB], qs[s0:s0 + SUB]
                if blk == 0:
                    oI = jnp.zeros((SUB, LANE), F32)
                else:
                    ref = bs[s0 - 1:s0, :]
                    qt = (qI * jnp.exp(bI - ref)).astype(BF16)
                    kt = (kk[0:s0] * jnp.exp(ref - b[0:s0])).astype(BF16)
                    oI = _dot(_dot(qt, kt, NT).astype(BF16), ic_b[0:s0], NN)
                for s in range(SUB):
                    sr = s0 + s
                    e = jnp.exp(jnp.minimum(bI - bs[sr:sr + 1, :], 0.0))
                    a = jnp.sum(qI * kks[sr:sr + 1, :] * e, axis=-1, keepdims=True)
                    oI = oI + jnp.where(rowi >= s, a, 0.0) * ics[sr:sr + 1, :]
                pieces.append(oI)
            o = o + jnp.concatenate(pieces, axis=0)
            bl = bs[CHUNK - 1:CHUNK, :]
            kd = (kk * jnp.exp(bl - b)).astype(BF16)
            st[...] = stv * jnp.exp(bl) + _dot(ic_b, kd, TN)
            o_ref[rows, :] = o
            r = lax.rsqrt(jnp.mean(o * o, axis=-1, keepdims=True) + EPS)
            gb = g_ref[rows, :]
            mix_ref[rows, :] = (o * r * gn_ref[...] * (gb * _sigmoid(gb))).astype(BF16)
            return carry

        lax.fori_loop(0, NC, chunk, 0)

    blk_in = lambda off: pl.BlockSpec((T, LANE), lambda h: (0, base + off + h))
    col = pl.BlockSpec((T, LANE), lambda h: (0, h))
    tile = pltpu.VMEM((CHUNK, LANE), F32)
    return pl.pallas_call(
        body, grid=(RH,), name="hgrn_fwd",
        in_specs=[blk_in(0), blk_in(RH), blk_in(2 * RH), blk_in(3 * RH), pl.BlockSpec((1, LANE), lambda h: (0, h)),
                  pl.BlockSpec((1, LANE), lambda h: (0, 0))],
        out_specs=[col, col, pl.BlockSpec((None, NC, LANE, LANE), lambda h: (h, 0, 0, 0))],
        out_shape=[_sds((T, RW), BF16), _sds((T, RW), F32), _sds((RH, NC, LANE, LANE), F32)],
        scratch_shapes=[pltpu.VMEM((LANE, LANE), F32), tile, tile, tile],
        compiler_params=_params(("parallel",), big=True),
    )(proj, proj, proj, proj, lb, gn)


def _hgrn_bwd(proj, dmixin, o_b, st_all, lb, gn, AW, RW):
    T = proj.shape[0]
    RH, NC, NSUB = RW // LANE, T // CHUNK, CHUNK // SUB
    base = 3 * AW // LANE

    def body(q_ref, f_ref, i_ref, g_ref, o_ref, dn_ref, stall_ref, lb_ref, gn_ref,
             dq_ref, df_ref, di_ref, dg_ref, dlb_ref, dgn_ref, dst, bs, kks, ics, dos, p2, dic):
        dst[...] = jnp.zeros_like(dst)
        dlb_ref[...] = jnp.zeros_like(dlb_ref)
        dgn_ref[...] = jnp.zeros_like(dgn_ref)
        ltri = _ltri()
        rowi = lax.broadcasted_iota(jnp.int32, (SUB, 1), 0)
        last = lax.broadcasted_iota(jnp.int32, (CHUNK, 1), 0) == CHUNK - 1

        def chunk(k, carry):
            n = NC - 1 - k
            rows, lbv, qb, sg, f, kk, sq, qs, b = _hgrn_gates(n, q_ref, f_ref, lb_ref, ltri)
            ic = i_ref[rows, :]
            stv = stall_ref[n]
            dstv = dst[...]
            o = o_ref[rows, :]
            dn = dn_ref[rows, :]
            gb = g_ref[rows, :]
            sgb = _sigmoid(gb)
            r = lax.rsqrt(jnp.mean(o * o, axis=-1, keepdims=True) + EPS)
            gnv = gn_ref[...]
            dg_ref[rows, :] = (dn * (o * r * gnv) * (sgb * (1.0 + gb * (1.0 - sgb)))).astype(BF16)
            dy = dn * (gb * sgb)
            dgn_ref[...] += _colsum(dy * o * r)
            a_ = dy * gnv
            do = r * (a_ - o * (r * r) * jnp.mean(a_ * o, axis=-1, keepdims=True))
            do_b = do.astype(BF16)
            bs[...] = b
            kks[...] = kk
            ics[...] = ic
            dos[...] = do
            ic_b = ic.astype(BF16)
            eb = jnp.exp(b)
            bl = bs[CHUNK - 1:CHUNK, :]
            ebl = jnp.exp(bl)
            dec = jnp.exp(bl - b)
            kd = (kk * dec).astype(BF16)
            dst_b = dstv.astype(BF16)
            dqs = _dot(do_b, stv.astype(BF16), NN) * eb
            dkk2 = _dot(ic_b, dst_b, NN) * dec
            dic[...] = _dot(kd, dst_b, NT)
            dbl = ebl * _colsum(stv * dstv) + _colsum(kk * dkk2)
            dst[...] = dstv * ebl + _dot(do_b, (qs * eb).astype(BF16), TN)
            p2[...] = jnp.zeros_like(p2)
            p1_pieces = []
            for blk in range(NSUB):
                s0 = blk * SUB
                bI, qI, doI = b[s0:s0 + SUB], qs[s0:s0 + SUB], do[s0:s0 + SUB]
                if blk == 0:
                    p1 = jnp.zeros((SUB, LANE), F32)
                else:
                    ref = bs[s0 - 1:s0, :]
                    eq = jnp.exp(bI - ref)
                    ek = jnp.exp(ref - b[0:s0])
                    qt = (qI * eq).astype(BF16)
                    kt = (kk[0:s0] * ek).astype(BF16)
                    doI_b = doI.astype(BF16)
                    dic[0:s0, :] += _dot(_dot(qt, kt, NT).astype(BF16), doI_b, TN)
                    da = _dot(doI_b, ic_b[0:s0], NT).astype(BF16)
                    p1 = _dot(da, kt, NN) * eq
                    p2[0:s0, :] += _dot(da, qt, TN) * ek
                for s in range(SUB):
                    sr = s0 + s
                    keep = rowi >= s
                    kk_s = kks[sr:sr + 1, :]
                    e = jnp.exp(jnp.minimum(bI - bs[sr:sr + 1, :], 0.0))
                    w = qI * e
                    a = jnp.where(keep, jnp.sum(w * kk_s, axis=-1, keepdims=True), 0.0)
                    da_s = jnp.where(keep, jnp.sum(doI * ics[sr:sr + 1, :], axis=-1, keepdims=True), 0.0)
                    p1 = p1 + da_s * kk_s * e
                    p2[sr:sr + 1, :] += _colsum(da_s * w)
                    dic[sr:sr + 1, :] += _colsum(a * doI)
                p1_pieces.append(p1)
            dqs = dqs + jnp.concatenate(p1_pieces, axis=0)
            dkk = dkk2 + p2[...]
            db = qs * dqs - kk * dkk + jnp.where(last, dbl, 0.0)
            dgl = _dot(ltri, db, TN, HIGHEST)
            dfv = dgl / f - dkk
            df_ref[rows, :] = (dfv * (1.0 - lbv) * sg * (1.0 - sg)).astype(BF16)
            dlb_ref[...] += _colsum(dfv * (1.0 - sg))
            dq_ref[rows, :] = (dqs * (sq * (1.0 + qb * (1.0 - sq)))).astype(BF16)
            di_ref[rows, :] = dic[...].astype(BF16)
            return carry

        lax.fori_loop(0, NC, chunk, 0)

    blk_in = lambda off: pl.BlockSpec((T, LANE), lambda h: (0, base + off + h))
    col = pl.BlockSpec((T, LANE), lambda h: (0, h))
    tile = pltpu.VMEM((CHUNK, LANE), F32)
    return pl.pallas_call(
        body, grid=(RH,), name="hgrn_bwd",
        in_specs=[blk_in(0), blk_in(RH), blk_in(2 * RH), blk_in(3 * RH), col,
                  pl.BlockSpec((T, LANE), lambda h: (0, AW // LANE + h)),
                  pl.BlockSpec((None, NC, LANE, LANE), lambda h: (h, 0, 0, 0)),
                  pl.BlockSpec((1, LANE), lambda h: (0, h)), pl.BlockSpec((1, LANE), lambda h: (0, 0))],
        out_specs=[col, col, col, col, pl.BlockSpec((1, LANE), lambda h: (0, h)),
                   pl.BlockSpec((None, 1, LANE), lambda h: (h, 0, 0))],
        out_shape=[_sds((T, RW), BF16)] * 4 + [_sds((1, RW), F32), _sds((RH, 1, LANE), F32)],
        scratch_shapes=[pltpu.VMEM((LANE, LANE), F32), tile, tile, tile, tile, tile, tile],
        compiler_params=_params(("parallel",), big=True),
    )(proj, proj, proj, proj, o_b, dmixin, st_all, lb, gn)


def _prep(c, lb_logits, rb_pad, max_rel):
    D, RW = c.shape[-1], lb_logits.shape[-1]
    H, rbp = rb_pad.shape

    def body(c_ref, l_ref, rb_ref, cact_ref, lb_ref, gv_ref):
        cv = c_ref[...]
        cact_ref[...] = cv * _sigmoid(cv)
        lb_ref[...] = _sigmoid(l_ref[0:1, :] - l_ref[1:2, :])
        gv_ref[...] = _dot(rb_ref[...], _bias_onehot(rbp, max_rel), NN, HIGHEST)

    return pl.pallas_call(
        body, name="prep", out_shape=[_sds((1, D), F32), _sds((1, RW), F32), _sds((H, BAND), F32)],
    )(c, lb_logits, rb_pad)


def _mod_part(c_all, w_ada_s, b_ada_s):
    B, D = c_all.shape
    ns = w_ada_s.shape[1]
    tn = _tile(ns, 768, LANE)

    def body(c_ref, w_ref, b_ref, o_ref):
        o_ref[...] = _dot(c_ref[...], w_ref[...], NN) + b_ref[...]

    return pl.pallas_call(
        body, grid=(ns // tn,), name="mod_part",
        in_specs=[pl.BlockSpec((B, D), lambda j: (0, 0)), pl.BlockSpec((D, tn), lambda j: (0, j)),
                  pl.BlockSpec((1, tn), lambda j: (0, j))],
        out_specs=pl.BlockSpec((B, tn), lambda j: (0, j)),
        out_shape=_sds((B, ns), F32), compiler_params=_params(("parallel",)),
    )(c_all, w_ada_s, b_ada_s)


def _adam(w, g, m, v):
    m = ADAM_B1 * m + (1.0 - ADAM_B1) * g
    v = ADAM_B2 * v + (1.0 - ADAM_B2) * (g * g)
    m_hat = m / (1.0 - ADAM_B1 ** ADAM_STEP)
    v_hat = v / (1.0 - ADAM_B2 ** ADAM_STEP)
    return -ADAM_LR * (m_hat / (jnp.sqrt(v_hat) + ADAM_EPS) + ADAM_WD * w), m, v


def _adam_ada(c_all, dmod_s, w, m, v):
    B, D = c_all.shape
    ns = w.shape[1]
    tr, tn = _tile(D, 512, LANE), _tile(ns, 768, LANE)

    def body(c_ref, d_ref, w_ref, m_ref, v_ref, g_out, dw_out, m_out, v_out):
        g = _dot(c_ref[...], d_ref[...], TN)
        g_out[...] = g
        dw_out[...], m_out[...], v_out[...] = _adam(w_ref[...], g, m_ref[...], v_ref[...])

    big = pl.BlockSpec((tr, tn), lambda i, j: (i, j))
    return pl.pallas_call(
        body, grid=(D // tr, ns // tn), name="adam_w_ada",
        in_specs=[pl.BlockSpec((B, tr), lambda i, j: (0, i)), pl.BlockSpec((B, tn), lambda i, j: (0, j)),
                  big, big, big],
        out_specs=[big] * 4, out_shape=[_sds((D, ns), F32)] * 4,
        compiler_params=_params(("parallel", "parallel")),
    )(c_all, dmod_s, w, m, v)


def _adam_shard(parts, w, m, v, name):
    R, C = w.shape
    tr = _tile(R, 256, 16)

    def body(p_ref, w_ref, m_ref, v_ref, g_out, dw_out, m_out, v_out):
        g = p_ref[0].astype(F32)
        for k in range(1, N_DEV // 2):
            g = g + p_ref[k].astype(F32)
        g_out[...] = g
        dw_out[...], m_out[...], v_out[...] = _adam(w_ref[...], g, m_ref[...], v_ref[...])

    big = pl.BlockSpec((tr, C), lambda i: (i, 0))
    return pl.pallas_call(
        body, grid=(R // tr,), name=name,
        in_specs=[pl.BlockSpec((N_DEV // 2, tr, C), lambda i: (0, i, 0)), big, big, big],
        out_specs=[big] * 4, out_shape=[_sds((R, C), F32)] * 4,
        compiler_params=_params(("parallel",), big=True),
    )(parts, w, m, v)


def _pair_sum(g8, land, core, name):
    _, NCHIP, R, C = g8.shape
    tr = _tile(R, 256, 16)

    def body(core_ref, g_ref, l_ref, o_ref):
        o_ref[...] = (g_ref[...].astype(F32) + l_ref[...].astype(F32)).astype(BF16)

    return pl.pallas_call(
        body, name=name,
        grid_spec=pltpu.PrefetchScalarGridSpec(
            num_scalar_prefetch=1, grid=(NCHIP, R // tr),
            in_specs=[pl.BlockSpec((None, None, tr, C), lambda k, i, core_ref: (core_ref[0], k, i, 0)),
                      pl.BlockSpec((None, tr, C), lambda k, i, core_ref: (k, i, 0))],
            out_specs=pl.BlockSpec((None, tr, C), lambda k, i, core_ref: (k, i, 0))),
        out_shape=_sds((NCHIP, R, C), BF16), compiler_params=_params(("parallel", "parallel")),
    )(core, g8, land)


SMALL = ("b_ada", "rel_bias", "attn_norm_g", "lb_logits", "gnorm_g", "ln1_g", "ln1_b", "ln2_g", "ln2_b")


def _small_update(parts, loss_parts, lbv, ws, ms, vs, max_rel):
    n = len(SMALL)

    def body(*refs):
        part_refs = dict(zip(SMALL, refs[:n]))
        loss_in, lb_ref = refs[n], refs[n + 1]
        w_refs, m_refs, v_refs = refs[n + 2:2 * n + 2], refs[2 * n + 2:3 * n + 2], refs[3 * n + 2:4 * n + 2]
        outs = refs[4 * n + 2:]

        def total(ref):
            tot = ref[0]
            for k in range(1, N_DEV):
                tot = tot + ref[k]
            return tot

        outs[0][...] = jnp.sum(total(loss_in), axis=-1, keepdims=True)
        for idx, name in enumerate(SMALL):
            g = total(part_refs[name])
            if name == "rel_bias":
                g = _dot(g, _bias_onehot(w_refs[idx].shape[1], max_rel), NT, HIGHEST)
            elif name == "lb_logits":
                lb = lb_ref[...]
                sign = (1 - 2 * lax.broadcasted_iota(jnp.int32, (2, 1), 0)).astype(F32)
                g = sign * (g * lb * (1.0 - lb))
            elif name == "gnorm_g":
                g = _colsum(g)
            dw, mm, vv = _adam(w_refs[idx][...], g, m_refs[idx][...], v_refs[idx][...])
            outs[1 + 4 * idx][...] = g
            outs[2 + 4 * idx][...] = dw
            outs[3 + 4 * idx][...] = mm
            outs[4 + 4 * idx][...] = vv

    out_shape = [_sds((1, 1), F32)]
    for w in ws:
        out_shape += [_sds(w.shape, F32)] * 4
    return pl.pallas_call(body, name="small_update", out_shape=out_shape, compiler_params=_params(big=True))(
        *[parts[k] for k in SMALL], loss_parts, lbv, *ws, *ms, *vs)


def _place():
    x, y, c = lax.axis_index("x"), lax.axis_index("y"), lax.axis_index("c")
    return x, y, c, [(1 - x, y), (x, 1 - y), (1 - x, 1 - y)]


def _all_gather(shard, name):
    HBM = pl.BlockSpec(memory_space=pl.ANY)

    def body(x_ref, out_ref, send_sems, recv_sems, local_sem):
        x, y, c, chips = _place()
        me, sibling = (x, y, c), (x, y, 1 - c)

        def slot(px, py, pc):
            return out_ref.at[4 * px + 2 * py + pc]

        def copy(k, block, to, src=None):
            return pltpu.make_async_remote_copy(
                src_ref=slot(*block) if src is None else src, dst_ref=slot(*block),
                send_sem=send_sems.at[k], recv_sem=recv_sems.at[k], device_id=to, device_id_type=MESH)

        mine = pltpu.make_async_copy(x_ref, slot(*me), local_sem)
        mine.start()
        first = [copy(0, me, sibling, src=x_ref)]
        first += [copy(1 + j, me, (*chip, c), src=x_ref) for j, chip in enumerate(chips)]
        for cp in first:
            cp.start()
        passed = [copy(4 + j, (*chip, c), sibling) for j, chip in enumerate(chips)]
        for j, chip in enumerate(chips):
            copy(1 + j, (*chip, c), me).wait_recv()
            passed[j].start()
        copy(0, sibling, me).wait_recv()
        for j, chip in enumerate(chips):
            copy(4 + j, (*chip, 1 - c), me).wait_recv()
        for cp in first + passed:
            cp.wait_send()
        mine.wait()

    return pl.pallas_call(
        body, name=name, out_shape=_sds((N_DEV,) + shard.shape, shard.dtype),
        in_specs=[HBM], out_specs=HBM,
        scratch_shapes=[pltpu.SemaphoreType.DMA((7,)), pltpu.SemaphoreType.DMA((7,)), pltpu.SemaphoreType.DMA(())],
    )(shard)


SEM_SPEC = pl.BlockSpec(memory_space=pltpu.SEMAPHORE)
HBM_SPEC = pl.BlockSpec(memory_space=pltpu.HBM)
EFFECT = pltpu.SideEffectType.DATAFLOW_SIDE_EFFECTING


def _remote(src, dst, send_sems, recv_sems, k, dev):
    return pltpu.make_async_remote_copy(src_ref=src, dst_ref=dst, send_sem=send_sems.at[k], recv_sem=recv_sems.at[k],
                                        device_id=dev, device_id_type=MESH)


def _copy_start(name, bufs, plan, n):
    nb = len(bufs)

    def body(*refs):
        send_sems, recv_sems = refs[nb], refs[nb + 1]
        for k, (src, dst, dev) in enumerate(plan(*refs[:nb])):
            _remote(src, dst, send_sems, recv_sems, k, dev).start()
        refs[-1][...] = jnp.zeros_like(refs[-1])

    out = pl.pallas_call(
        body, name=name,
        out_shape=(pltpu.SemaphoreType.DMA((n,)), pltpu.SemaphoreType.DMA((n,)),
                   *[pltpu.HBM(b.shape, b.dtype) for b in bufs], _sds((8, LANE), F32)),
        in_specs=[HBM_SPEC] * nb,
        out_specs=(SEM_SPEC, SEM_SPEC, *[HBM_SPEC] * nb, pl.BlockSpec(memory_space=pltpu.VMEM)),
        input_output_aliases={i: 2 + i for i in range(nb)},
        compiler_params=pltpu.CompilerParams(has_side_effects=EFFECT),
    )(*[pltpu.with_memory_space_constraint(b, pltpu.HBM) for b in bufs])
    return (out[0], out[1]), list(out[2:2 + nb]), out[-1]


def _copy_wait(name, sems, bufs, plan, after):
    nb = len(bufs)

    def body(*refs):
        send_sems, recv_sems = refs[nb], refs[nb + 1]
        for k, (src, dst, dev) in enumerate(plan(*refs[:nb])):
            cp = _remote(src, dst, send_sems, recv_sems, k, dev)
            cp.wait_send()
            cp.wait_recv()

    out = pl.pallas_call(
        body, name=name, out_shape=tuple(pltpu.HBM(b.shape, b.dtype) for b in bufs),
        in_specs=[HBM_SPEC] * nb + [SEM_SPEC, SEM_SPEC, pl.BlockSpec(memory_space=pl.ANY)],
        out_specs=tuple([HBM_SPEC] * nb), input_output_aliases={i: i for i in range(nb)},
        compiler_params=pltpu.CompilerParams(has_side_effects=EFFECT),
    )(*bufs, sems[0], sems[1], after)
    return list(out)


def _ag_plan_chips(shard_ref, out_ref):
    x, y, c, chips = _place()
    mine = out_ref.at[4 * x + 2 * y + c]
    return [(shard_ref, mine, (x, y, 1 - c))] + [(shard_ref, mine, (*chip, c)) for chip in chips]


def _ag_plan_pass(out_ref):
    x, y, c, chips = _place()
    slots = [out_ref.at[4 * chip[0] + 2 * chip[1] + c] for chip in chips]
    return [(s, s, (x, y, 1 - c)) for s in slots]


def _rs_plan_pair(g_ref, land_ref):
    x, y, c, _ = _place()
    return [(g_ref.at[1 - c], land_ref, (x, y, 1 - c))]


def _rs_plan_chips(p_ref, land_ref):
    x, y, c, chips = _place()
    return [(p_ref.at[2 * chip[0] + chip[1]], land_ref.at[2 * x + y], (*chip, c)) for chip in chips]


class _Gather:
    def __init__(self, shard, me, tag):
        self.tag = tag
        out = lax.dynamic_update_slice(lax.empty((N_DEV,) + shard.shape, shard.dtype), shard[None],
                                       (me,) + (0,) * shard.ndim)
        self.sems, (self.shard, self.out), self.token = _copy_start("ag_start_" + tag, [shard, out], _ag_plan_chips, 4)

    def arrived_from_chips(self, after):
        _, out = _copy_wait("ag_wait_" + self.tag, self.sems, [self.shard, self.out], _ag_plan_chips, after)
        self.sems, (self.out,), _ = _copy_start("ag_pass_" + self.tag, [out], _ag_plan_pass, 3)

    def passed_on(self, after):
        return _copy_wait("ag_pass_wait_" + self.tag, self.sems, [self.out], _ag_plan_pass, after)[0]


class _ReduceScatter:
    def __init__(self, g8, tag):
        self.tag = tag
        land = lax.empty(g8.shape[1:], g8.dtype)
        self.sems, self.bufs, self.token = _copy_start("rs_pair_start_" + tag, [g8, land], _rs_plan_pair, 1)

    def pair_done(self, core, chip, after):
        g8, land = _copy_wait("rs_pair_wait_" + self.tag, self.sems, self.bufs, _rs_plan_pair, after)
        p4 = _pair_sum(g8, land, core, "rs_pair_sum_" + self.tag)
        own = lax.dynamic_slice_in_dim(p4, chip, 1, axis=0)
        land2 = lax.dynamic_update_slice(lax.empty(p4.shape, p4.dtype), own, (chip, 0, 0))
        self.sems, self.bufs, self.token = _copy_start("rs_chips_start_" + self.tag, [p4, land2], _rs_plan_chips, 3)

    def sums(self, after):
        return _copy_wait("rs_chips_wait_" + self.tag, self.sems, self.bufs, _rs_plan_chips, after)[1]


BIG = ("w_in", "w_o", "w_ffn_in", "w_ffn_out")
ORDER = ("w_ada", "b_ada", "w_in", "rel_bias", "attn_norm_g", "lb_logits", "gnorm_g", "w_o", "ln1_g", "ln1_b",
         "w_ffn_in", "w_ffn_out", "ln2_g", "ln2_b")


def kernel(x, c, w_ada, b_ada, w_in, rel_bias, attn_norm_g, lb_logits, gnorm_g, w_o, ln1_g, ln1_b, w_ffn_in, w_ffn_out, ln2_g, ln2_b, loss_target, m_w_ada, m_b_ada, m_w_in, m_rel_bias, m_attn_norm_g, m_lb_logits, m_gnorm_g, m_w_o, m_ln1_g, m_ln1_b, m_w_ffn_in, m_w_ffn_out, m_ln2_g, m_ln2_b, v_w_ada, v_b_ada, v_w_in, v_rel_bias, v_attn_norm_g, v_lb_logits, v_gnorm_g, v_w_o, v_ln1_g, v_ln1_b, v_w_ffn_in, v_w_ffn_out, v_ln2_g, v_ln2_b):
    W = dict(w_ada=w_ada, b_ada=b_ada, w_in=w_in, rel_bias=rel_bias, attn_norm_g=attn_norm_g, lb_logits=lb_logits,
             gnorm_g=gnorm_g, w_o=w_o, ln1_g=ln1_g, ln1_b=ln1_b, w_ffn_in=w_ffn_in, w_ffn_out=w_ffn_out,
             ln2_g=ln2_g, ln2_b=ln2_b)
    M = dict(w_ada=m_w_ada, b_ada=m_b_ada, w_in=m_w_in, rel_bias=m_rel_bias, attn_norm_g=m_attn_norm_g,
             lb_logits=m_lb_logits, gnorm_g=m_gnorm_g, w_o=m_w_o, ln1_g=m_ln1_g, ln1_b=m_ln1_b,
             w_ffn_in=m_w_ffn_in, w_ffn_out=m_w_ffn_out, ln2_g=m_ln2_g, ln2_b=m_ln2_b)
    V = dict(w_ada=v_w_ada, b_ada=v_b_ada, w_in=v_w_in, rel_bias=v_rel_bias, attn_norm_g=v_attn_norm_g,
             lb_logits=v_lb_logits, gnorm_g=v_gnorm_g, w_o=v_w_o, ln1_g=v_ln1_g, ln1_b=v_ln1_b,
             w_ffn_in=v_w_ffn_in, w_ffn_out=v_w_ffn_out, ln2_g=v_ln2_g, ln2_b=v_ln2_b)

    x2, tgt = x[0], loss_target[0]
    T, D = x2.shape
    AW, RW = attn_norm_g.shape[-1], lb_logits.shape[-1]
    MIX = AW + RW
    H, RH = AW // ATTN_HEAD_DIM, RW // LANE
    RB = rel_bias.shape[-1]
    max_rel = (RB - 1) // 2
    rbp = -(-RB // LANE) * LANE
    F = w_ffn_out.shape[1] * N_DEV
    half = N_DEV // 2
    xi, yi, ci = lax.axis_index("x"), lax.axis_index("y"), lax.axis_index("c")
    me = 4 * xi + 2 * yi + ci
    core = jnp.reshape(ci, (1,)).astype(jnp.int32)
    pad_rb = lambda a: jnp.pad(a[0], ((0, 0), (0, rbp - RB)))

    chip = 2 * xi + yi
    ag_in = _Gather(w_in[0].astype(BF16), me, "w_in")
    ag_o = _Gather(w_o[0].astype(BF16) + ag_in.token[0, 0].astype(BF16), me, "w_o")
    ag_f1 = _Gather(w_ffn_in[0].astype(BF16) + ag_o.token[0, 0].astype(BF16), me, "w_ffn_in")
    ag_f2 = _Gather(w_ffn_out[0].astype(BF16) + ag_f1.token[0, 0].astype(BF16), me, "w_ffn_out")

    c_act, lbv, gv = _prep(c + ag_f2.token[0:1, 0:1], lb_logits, pad_rb(rel_bias), max_rel)
    c_all = _all_gather(c_act, "ag_c").reshape(N_DEV, D)
    ns_ada = w_ada.shape[-1]
    mod_part = _mod_part(c_all, w_ada[0], lax.dynamic_slice_in_dim(b_ada, me * ns_ada, ns_ada, axis=1))
    mod_all = _all_gather(mod_part, "ag_mod")
    mod6 = lax.dynamic_index_in_dim(mod_all, me, axis=1, keepdims=False).reshape(6, D)

    h1 = _ln_mod(x2, mod6)
    ag_in.arrived_from_chips(h1)
    wg_in = ag_in.passed_on(h1)
    proj = _mm_gathered(h1, wg_in, "in_proj")
    ag_o.arrived_from_chips(proj)
    mix_a = _attn_fwd(proj, gv, attn_norm_g, AW)
    wg_o = ag_o.passed_on(mix_a).reshape(MIX, D)
    mix_b, o_b, st_all = _hgrn_fwd(proj, lbv, gnorm_g, AW, RW)
    ag_f1.arrived_from_chips(mix_b)
    mixin = jnp.concatenate([mix_a, mix_b], axis=1)
    mix = _mm_nn(mixin, wg_o, "out_proj")
    x1, h2 = _mid_fwd(x2, mix, mod6, ln1_g, ln1_b)
    wg_f1 = ag_f1.passed_on(h2)
    gu, act = _mm_swiglu(h2, wg_f1)
    ag_f2.arrived_from_chips(act)
    wg_f2 = ag_f2.passed_on(act).reshape(F, D)
    ff = _mm_nn(act, wg_f2, "ffn_out")
    dff, dx1a, vec_a = _final(x1, ff, mod6, ln2_g, ln2_b, tgt)

    du = _mm_swiglu_bwd(dff, wg_f2, gu)
    rs_f2 = _ReduceScatter(_mm_tn_rows(dff, act, dff, F // N_DEV, "grad_w_ffn_out"), "w_ffn_out")
    tm = _tile(T, 512, 16)
    du_ij = lambda tm_, ns: pl.BlockSpec((None, tm_, ns), lambda i, j: (j // half, i, j % half))
    du_jm = lambda tm_, ns: pl.BlockSpec((None, tm_, ns), lambda j, m: (j // half, m, j % half))
    dh2 = _mm_gathered_nt(rs_f2.token, du, du_ij, wg_f1, T, tm, "ffn_in_bwd")
    rs_f2.pair_done(core, chip, dh2)
    gw_f1 = _mm_tn_gathered(rs_f2.token, h2, du, du_jm, wg_f1.shape[-1], tm, "grad_w_ffn_in")
    rs_f1 = _ReduceScatter(gw_f1.reshape(2, half, D, -1), "w_ffn_in")
    dmix, dxa, vec_b = _mid_bwd(x2, mix, x1, dx1a, dh2, mod6 + rs_f1.token[0, 0], ln1_g)
    dmixin = _mm_nt(dmix, wg_o, "out_proj_bwd")
    rs_f1.pair_done(core, chip, dmixin)
    rs_o = _ReduceScatter(_mm_tn_rows(rs_f1.token, mixin, dmix, MIX // N_DEV, "grad_w_o"), "w_o")
    dq, dk, dv, dgv, dga = _attn_bwd(proj, dmixin, gv + rs_o.token[0, 0], attn_norm_g, AW)
    rs_o.pair_done(core, chip, dq)
    dqb, dfl, dib, dgb, dlb, dgn = _hgrn_bwd(proj, dmixin, o_b, st_all, lbv + rs_o.token[0, 0], gnorm_g, AW, RW)
    dproj = jnp.concatenate([dq, dk, dv, dqb, dfl, dib, dgb], axis=1)
    p_ij = lambda tm_, ns: pl.BlockSpec((tm_, ns), lambda i, j: (i, j))
    p_jm = lambda tm_, ns: pl.BlockSpec((tm_, ns), lambda j, m: (m, j))
    gw_in = _mm_tn_gathered(rs_o.token, h1, dproj, p_jm, wg_in.shape[-1], tm, "grad_w_in")
    rs_in = _ReduceScatter(gw_in.reshape(2, half, D, -1), "w_in")
    dh1 = _mm_gathered_nt(rs_in.token, dproj, p_ij, wg_in, T, tm, "in_proj_bwd")
    rs_in.pair_done(core, chip, dh1)
    grad_x, vec_c = _first_bwd(x2, dh1, dxa, mod6 + rs_in.token[0, 0])

    dmod = jnp.concatenate([vec_c[1:2], vec_c[0:1], vec_b[4:5], vec_b[1:2], vec_b[0:1], vec_a[2:3]], axis=0)
    pieces = dict(b_ada=dmod, rel_bias=dgv, attn_norm_g=dga, lb_logits=dlb, gnorm_g=dgn, ln1_g=vec_b[2:3],
                  ln1_b=vec_b[3:4], ln2_g=vec_a[0:1], ln2_b=vec_a[1:2], loss=vec_a[3:4])
    widths = dict(b_ada=(1, 6 * D), rel_bias=(H, BAND), attn_norm_g=(1, AW), lb_logits=(1, RW), gnorm_g=(RH, LANE),
                  ln1_g=(1, D), ln1_b=(1, D), ln2_g=(1, D), ln2_b=(1, D), loss=(1, D))
    packed = jnp.concatenate([pieces[k].reshape(-1, LANE) for k in widths], axis=0)
    gathered = _all_gather(packed, "ag_small")
    parts, r0 = {}, 0
    for k, (rows, width) in widths.items():
        nr = rows * width // LANE
        parts[k] = gathered[:, r0:r0 + nr, :].reshape(N_DEV, rows, width)
        r0 += nr
    prep_small = lambda d, k: pad_rb(d[k]) if k == "rel_bias" else d[k]
    small = _small_update(parts, parts["loss"], lbv, [prep_small(W, k) for k in SMALL],
                          [prep_small(M, k) for k in SMALL], [prep_small(V, k) for k in SMALL], max_rel)
    loss = small[0].reshape(())
    res = {}
    for idx, k in enumerate(SMALL):
        four = small[1 + 4 * idx:5 + 4 * idx]
        if k == "rel_bias":
            four = [a[:, :RB][None] for a in four]
        res[k] = list(four)

    dmod_s = lax.dynamic_slice_in_dim(parts["b_ada"].reshape(N_DEV, 6 * D), me * ns_ada, ns_ada, axis=1)
    res["w_ada"] = [a[None] for a in _adam_ada(c_all, dmod_s, w_ada[0], m_w_ada[0], v_w_ada[0])]
    after = res["w_ada"][0]
    for k, rs in (("w_ffn_out", rs_f2), ("w_ffn_in", rs_f1), ("w_o", rs_o), ("w_in", rs_in)):
        four = _adam_shard(rs.sums(after), W[k][0], M[k][0], V[k][0], "adam_" + k)
        res[k] = [a[None] for a in four]
        after = four[0]

    out = [loss, grad_x[None]]
    for field in range(4):
        out += [res[k][field] for k in ORDER]
    return tuple(out)
```

```python
import functools

import jax
import jax.numpy as jnp
from jax import lax
from jax.experimental import pallas as pl
from jax.experimental.pallas import tpu as pltpu

F32 = jnp.float32
BF16 = jnp.bfloat16
MESH = pl.DeviceIdType.MESH
HIGHEST = lax.Precision.HIGHEST

N_DEV = 8
CHUNK = 64
N_PAST = 8
KPAD = (N_PAST + 1) * CHUNK
BAND = (N_PAST + 2) * CHUNK
ATTN_HEAD_DIM = 64
REC_HEAD_DIM = 128
SUB = 16
LANE = 128
EPS = 1e-5
ALPHA = 2.0 ** 0.25
ADAM_LR, ADAM_B1, ADAM_B2, ADAM_EPS, ADAM_WD, ADAM_STEP = 0.001, 0.9, 0.999, 1e-08, 0.01, 10
NEG = -1e30
VMEM_LIMIT = 56 * 1024 * 1024


def _sds(shape, dtype):
    return jax.ShapeDtypeStruct(tuple(shape), dtype)


def _tile(n, pref, mult):
    best = None
    for t in range(mult, min(n, pref) + 1, mult):
        if n % t == 0:
            best = t
    return n if best is None else best


def _params(sem=None, big=False):
    kw = {}
    if sem is not None:
        kw["dimension_semantics"] = sem
    if big:
        kw["vmem_limit_bytes"] = VMEM_LIMIT
    return pltpu.CompilerParams(**kw)


def _sigmoid(v):
    return 1.0 / (1.0 + jnp.exp(-v))


def _dot(a, b, dims, precision=None):
    return lax.dot_general(a, b, (dims, ((), ())), preferred_element_type=F32, precision=precision)


NN = ((1,), (0,))
NT = ((1,), (1,))
TN = ((0,), (0,))


def _ln(v):
    mu = jnp.mean(v, axis=-1, keepdims=True)
    d = v - mu
    rstd = lax.rsqrt(jnp.mean(d * d, axis=-1, keepdims=True) + EPS)
    return d * rstd, rstd


def _ln_bwd(dxh, xh, rstd):
    return rstd * (dxh - jnp.mean(dxh, axis=-1, keepdims=True) - xh * jnp.mean(dxh * xh, axis=-1, keepdims=True))


def _colsum(v):
    return jnp.sum(v, axis=0, keepdims=True)


def _ln_mod(x2, mod6):
    T, D = x2.shape
    tm = _tile(T, 256, 8)

    def body(x_ref, mod_ref, o_ref):
        xh, _ = _ln(x_ref[...])
        o_ref[...] = (xh * (1.0 + mod_ref[1:2, :]) + mod_ref[0:1, :]).astype(BF16)

    return pl.pallas_call(
        body, grid=(T // tm,), name="ln_mod",
        in_specs=[pl.BlockSpec((tm, D), lambda i: (i, 0)), pl.BlockSpec((6, D), lambda i: (0, 0))],
        out_specs=pl.BlockSpec((tm, D), lambda i: (i, 0)),
        out_shape=_sds((T, D), BF16), compiler_params=_params(("parallel",)),
    )(x2, mod6)


def _mid_fwd(x2, mix, mod6, ln1_g, ln1_b):
    T, D = x2.shape
    tm = _tile(T, 256, 8)

    def body(x_ref, mix_ref, mod_ref, g_ref, b_ref, x1_ref, h2_ref):
        zh, _ = _ln(ALPHA * x_ref[...] + mod_ref[2:3, :] * mix_ref[...])
        x1 = zh * g_ref[...] + b_ref[...]
        x1_ref[...] = x1
        xh, _ = _ln(x1)
        h2_ref[...] = (xh * (1.0 + mod_ref[4:5, :]) + mod_ref[3:4, :]).astype(BF16)

    row = pl.BlockSpec((tm, D), lambda i: (i, 0))
    vec = pl.BlockSpec((1, D), lambda i: (0, 0))
    return pl.pallas_call(
        body, grid=(T // tm,), name="mid_fwd",
        in_specs=[row, row, pl.BlockSpec((6, D), lambda i: (0, 0)), vec, vec],
        out_specs=[row, row],
        out_shape=[_sds((T, D), F32), _sds((T, D), BF16)], compiler_params=_params(("parallel",)),
    )(x2, mix, mod6, ln1_g, ln1_b)


def _final(x1, ff, mod6, ln2_g, ln2_b, tgt):
    T, D = x1.shape
    tm = _tile(T, 256, 8)

    def body(x1_ref, ff_ref, mod_ref, g_ref, b_ref, t_ref, dff_ref, dx1_ref, vec_ref):
        @pl.when(pl.program_id(0) == 0)
        def _():
            vec_ref[...] = jnp.zeros_like(vec_ref)

        ff_v = ff_ref[...]
        gate2 = mod_ref[5:6, :]
        zh, rstd = _ln(ALPHA * x1_ref[...] + gate2 * ff_v)
        err = zh * g_ref[...] + b_ref[...] - t_ref[...]
        dy = err * (1.0 / D)
        dz = _ln_bwd(dy * g_ref[...], zh, rstd)
        dff_ref[...] = (gate2 * dz).astype(BF16)
        dx1_ref[...] = ALPHA * dz
        vec_ref[0:1, :] += _colsum(dy * zh)
        vec_ref[1:2, :] += _colsum(dy)
        vec_ref[2:3, :] += _colsum(dz * ff_v)
        vec_ref[3:4, :] += _colsum(err * err) * (0.5 / D)

    row = pl.BlockSpec((tm, D), lambda i: (i, 0))
    vec = pl.BlockSpec((1, D), lambda i: (0, 0))
    return pl.pallas_call(
        body, grid=(T // tm,), name="final_fwd_bwd",
        in_specs=[row, row, pl.BlockSpec((6, D), lambda i: (0, 0)), vec, vec, row],
        out_specs=[row, row, pl.BlockSpec((8, D), lambda i: (0, 0))],
        out_shape=[_sds((T, D), BF16), _sds((T, D), F32), _sds((8, D), F32)],
        compiler_params=_params(("arbitrary",)),
    )(x1, ff, mod6, ln2_g, ln2_b, tgt)


def _mid_bwd(x2, mix, x1, dx1a, dh2, mod6, ln1_g):
    T, D = x2.shape
    tm = _tile(T, 256, 8)

    def body(x_ref, mix_ref, x1_ref, dx1a_ref, dh2_ref, mod_ref, g_ref, dmix_ref, dxa_ref, vec_ref):
        @pl.when(pl.program_id(0) == 0)
        def _():
            vec_ref[...] = jnp.zeros_like(vec_ref)

        dh2 = dh2_ref[...]
        xh, rstd = _ln(x1_ref[...])
        dx1 = dx1a_ref[...] + _ln_bwd(dh2 * (1.0 + mod_ref[4:5, :]), xh, rstd)
        mix_v = mix_ref[...]
        gate1 = mod_ref[2:3, :]
        zh, rstdz = _ln(ALPHA * x_ref[...] + gate1 * mix_v)
        dz = _ln_bwd(dx1 * g_ref[...], zh, rstdz)
        dmix_ref[...] = (gate1 * dz).astype(BF16)
        dxa_ref[...] = ALPHA * dz
        vec_ref[0:1, :] += _colsum(dh2 * xh)
        vec_ref[1:2, :] += _colsum(dh2)
        vec_ref[2:3, :] += _colsum(dx1 * zh)
        vec_ref[3:4, :] += _colsum(dx1)
        vec_ref[4:5, :] += _colsum(dz * mix_v)

    row = pl.BlockSpec((tm, D), lambda i: (i, 0))
    vec = pl.BlockSpec((1, D), lambda i: (0, 0))
    return pl.pallas_call(
        body, grid=(T // tm,), name="mid_bwd",
        in_specs=[row, row, row, row, row, pl.BlockSpec((6, D), lambda i: (0, 0)), vec],
        out_specs=[row, row, pl.BlockSpec((8, D), lambda i: (0, 0))],
        out_shape=[_sds((T, D), BF16), _sds((T, D), F32), _sds((8, D), F32)],
        compiler_params=_params(("arbitrary",)),
    )(x2, mix, x1, dx1a, dh2, mod6, ln1_g)


def _first_bwd(x2, dh1, dxa, mod6):
    T, D = x2.shape
    tm = _tile(T, 256, 8)

    def body(x_ref, dh1_ref, dxa_ref, mod_ref, gx_ref, vec_ref):
        @pl.when(pl.program_id(0) == 0)
        def _():
            vec_ref[...] = jnp.zeros_like(vec_ref)

        dh1 = dh1_ref[...]
        xh, rstd = _ln(x_ref[...])
        gx_ref[...] = dxa_ref[...] + _ln_bwd(dh1 * (1.0 + mod_ref[1:2, :]), xh, rstd)
        vec_ref[0:1, :] += _colsum(dh1 * xh)
        vec_ref[1:2, :] += _colsum(dh1)

    row = pl.BlockSpec((tm, D), lambda i: (i, 0))
    return pl.pallas_call(
        body, grid=(T // tm,), name="first_bwd",
        in_specs=[row, row, row, pl.BlockSpec((6, D), lambda i: (0, 0))],
        out_specs=[row, pl.BlockSpec((8, D), lambda i: (0, 0))],
        out_shape=[_sds((T, D), F32), _sds((8, D), F32)],
        compiler_params=_params(("arbitrary",)),
    )(x2, dh1, dxa, mod6)


def _slot(j):
    return (j % 2) * 4 + j // 2


def _mm_gathered(a, wg, name):
    M, K = a.shape
    _, _, ns = wg.shape
    tm = _tile(M, 512, 16)

    def body(a_ref, w_ref, o_ref):
        o_ref[...] = _dot(a_ref[...], w_ref[...], NN)

    return pl.pallas_call(
        body, grid=(N_DEV, M // tm), name=name,
        in_specs=[pl.BlockSpec((tm, K), lambda j, i: (i, 0)), pl.BlockSpec((None, K, ns), lambda j, i: (j, 0, 0))],
        out_specs=pl.BlockSpec((tm, ns), lambda j, i: (i, j)),
        out_shape=_sds((M, N_DEV * ns), F32), compiler_params=_params(("parallel", "parallel"), big=True),
    )(a, wg)


def _mm_nn(a, b, name):
    M, K = a.shape
    _, N = b.shape
    tm, tn, tk = _tile(M, 512, 16), _tile(N, 1024, LANE), _tile(K, 2048, LANE)

    def body(a_ref, b_ref, o_ref):
        @pl.when(pl.program_id(2) == 0)
        def _():
            o_ref[...] = jnp.zeros_like(o_ref)

        o_ref[...] += _dot(a_ref[...], b_ref[...], NN)

    return pl.pallas_call(
        body, grid=(M // tm, N // tn, K // tk), name=name,
        in_specs=[pl.BlockSpec((tm, tk), lambda i, j, k: (i, k)), pl.BlockSpec((tk, tn), lambda i, j, k: (k, j))],
        out_specs=pl.BlockSpec((tm, tn), lambda i, j, k: (i, j)),
        out_shape=_sds((M, N), F32), compiler_params=_params(("parallel", "parallel", "arbitrary"), big=True),
    )(a, b)


def _mm_nt(a, b, name):
    M, K = a.shape
    N, _ = b.shape
    tm, tn = _tile(M, 512, 16), _tile(N, 1024, LANE)

    def body(a_ref, b_ref, o_ref):
        o_ref[...] = _dot(a_ref[...], b_ref[...], NT)

    return pl.pallas_call(
        body, grid=(M // tm, N // tn), name=name,
        in_specs=[pl.BlockSpec((tm, K), lambda i, j: (i, 0)), pl.BlockSpec((tn, K), lambda i, j: (j, 0))],
        out_specs=pl.BlockSpec((tm, tn), lambda i, j: (i, j)),
        out_shape=_sds((M, N), F32), compiler_params=_params(("parallel", "parallel"), big=True),
    )(a, b)


def _mm_swiglu(h2, wg):
    M, K = h2.shape
    _, _, ns = wg.shape
    half = N_DEV // 2
    tm = _tile(M, 256, 16)

    def body(a_ref, wgate_ref, wup_ref, gu_ref, act_ref):
        a = a_ref[...]
        g = _dot(a, wgate_ref[...], NN)
        u = _dot(a, wup_ref[...], NN)
        gu_ref[0] = g
        gu_ref[1] = u
        act_ref[...] = (g * _sigmoid(g) * u).astype(BF16)

    return pl.pallas_call(
        body, grid=(half, M // tm), name="ffn_in_swiglu",
        in_specs=[pl.BlockSpec((tm, K), lambda j, i: (i, 0)),
                  pl.BlockSpec((None, K, ns), lambda j, i: (j, 0, 0)),
                  pl.BlockSpec((None, K, ns), lambda j, i: (j + half, 0, 0))],
        out_specs=[pl.BlockSpec((2, tm, ns), lambda j, i: (0, i, j)), pl.BlockSpec((tm, ns), lambda j, i: (i, j))],
        out_shape=[_sds((2, M, half * ns), F32), _sds((M, half * ns), BF16)],
        compiler_params=_params(("parallel", "parallel"), big=True),
    )(h2, wg, wg)


def _mm_swiglu_bwd(dff, w2, gu):
    M, K = dff.shape
    F = w2.shape[0]
    tm, tn = _tile(M, 256, 16), _tile(F, 1408, LANE)

    def body(a_ref, b_ref, gu_ref, du_ref):
        da = _dot(a_ref[...], b_ref[...], NT)
        g = gu_ref[0]
        u = gu_ref[1]
        sg = _sigmoid(g)
        du_ref[0] = (da * u * (sg * (1.0 + g * (1.0 - sg)))).astype(BF16)
        du_ref[1] = (da * (g * sg)).astype(BF16)

    return pl.pallas_call(
        body, grid=(M // tm, F // tn), name="ffn_out_bwd_swiglu",
        in_specs=[pl.BlockSpec((tm, K), lambda i, j: (i, 0)), pl.BlockSpec((tn, K), lambda i, j: (j, 0)),
                  pl.BlockSpec((2, tm, tn), lambda i, j: (0, i, j))],
        out_specs=pl.BlockSpec((2, tm, tn), lambda i, j: (0, i, j)),
        out_shape=_sds((2, M, F), BF16), compiler_params=_params(("parallel", "parallel"), big=True),
    )(dff, w2, gu)


ORDER_ONLY = pl.BlockSpec(memory_space=pl.ANY)


def _mm_tn_rows(dep, a, b, rs, name):
    M, Ka = a.shape
    _, N = b.shape
    tm = _tile(M, 512, 16)

    def body(_, a_ref, b_ref, o_ref, acc_ref):
        m = pl.program_id(1)

        @pl.when(m == 0)
        def _():
            acc_ref[...] = jnp.zeros_like(acc_ref)

        acc_ref[...] += _dot(a_ref[...], b_ref[...], TN)

        @pl.when(m == pl.num_programs(1) - 1)
        def _():
            o_ref[0, 0] = acc_ref[0:rs, :].astype(BF16)
            o_ref[1, 0] = acc_ref[rs:2 * rs, :].astype(BF16)

    return pl.pallas_call(
        body, grid=(N_DEV // 2, M // tm), name=name,
        in_specs=[ORDER_ONLY, pl.BlockSpec((tm, 2 * rs), lambda ch, m: (m, ch)),
                  pl.BlockSpec((tm, N), lambda ch, m: (m, 0))],
        out_specs=pl.BlockSpec((2, 1, rs, N), lambda ch, m: (0, ch, 0, 0)),
        out_shape=_sds((2, N_DEV // 2, rs, N), BF16),
        scratch_shapes=[pltpu.VMEM((2 * rs, N), F32)],
        compiler_params=_params(("parallel", "arbitrary"), big=True),
    )(dep, a, b)


def _mm_gathered_nt(dep, a, a_spec, wg, M, tm, name):
    _, K, ns = wg.shape

    def body(_, a_ref, w_ref, o_ref):
        @pl.when(pl.program_id(1) == 0)
        def _():
            o_ref[...] = jnp.zeros_like(o_ref)

        o_ref[...] += _dot(a_ref[...], w_ref[...], NT)

    return pl.pallas_call(
        body, grid=(M // tm, N_DEV), name=name,
        in_specs=[ORDER_ONLY, a_spec(tm, ns), pl.BlockSpec((None, K, ns), lambda i, j: (j, 0, 0))],
        out_specs=pl.BlockSpec((tm, K), lambda i, j: (i, 0)),
        out_shape=_sds((M, K), F32), compiler_params=_params(("parallel", "arbitrary"), big=True),
    )(dep, a, wg)


def _mm_tn_gathered(dep, h, a, a_spec, ns, tm, name):
    M, K = h.shape

    def body(_, h_ref, a_ref, o_ref, acc_ref):
        m = pl.program_id(1)

        @pl.when(m == 0)
        def _():
            acc_ref[...] = jnp.zeros_like(acc_ref)

        acc_ref[...] += _dot(h_ref[...], a_ref[...], TN)

        @pl.when(m == pl.num_programs(1) - 1)
        def _():
            o_ref[...] = acc_ref[...].astype(BF16)

    return pl.pallas_call(
        body, grid=(N_DEV, M // tm), name=name,
        in_specs=[ORDER_ONLY, pl.BlockSpec((tm, K), lambda j, m: (m, 0)), a_spec(tm, ns)],
        out_specs=pl.BlockSpec((None, K, ns), lambda j, m: (_slot(j), 0, 0)),
        out_shape=_sds((N_DEV, K, ns), BF16),
        scratch_shapes=[pltpu.VMEM((K, ns), F32)],
        compiler_params=_params(("parallel", "arbitrary"), big=True),
    )(dep, h, a)


def _bias_onehot(rbp, max_rel):
    r = lax.broadcasted_iota(jnp.int32, (rbp, BAND), 0)
    m = lax.broadcasted_iota(jnp.int32, (rbp, BAND), 1)
    return (r == jnp.minimum(KPAD - m, max_rel) + max_rel).astype(F32)


def _attn_setup(i, hp, k_ref, v_ref, gv_ref, kpad, vpad, bias):
    ls = slice(i * ATTN_HEAD_DIM, (i + 1) * ATTN_HEAD_DIM)
    kpad[i, 0:KPAD, :] = jnp.zeros((KPAD, ATTN_HEAD_DIM), BF16)
    vpad[i, 0:KPAD, :] = jnp.zeros((KPAD, ATTN_HEAD_DIM), BF16)
    kpad[i, KPAD:, :] = k_ref[:, ls].astype(BF16)
    vpad[i, KPAD:, :] = v_ref[:, ls].astype(BF16)
    gvrow = gv_ref[pl.ds(hp * 2 + i, 1), :]
    bias[i] = pltpu.roll(jnp.broadcast_to(gvrow, (CHUNK, BAND)), 0, 1, stride=1, stride_axis=0)


def _attn_probs(n, i, q_ref, kpad, vpad, bias, col):
    ls = slice(i * ATTN_HEAD_DIM, (i + 1) * ATTN_HEAD_DIM)
    r0 = pl.multiple_of(n * CHUNK, CHUNK)
    q = q_ref[pl.ds(r0, CHUNK), ls].astype(BF16)
    kb = kpad[i, pl.ds(r0, BAND), :]
    vb = vpad[i, pl.ds(r0, BAND), :]
    s = _dot(q, kb, NT) * (ATTN_HEAD_DIM ** -0.5) + bias[i]
    valid = col >= jnp.maximum(CHUNK, (N_PAST + 1 - n) * CHUNK)
    s = jnp.where(valid, s, NEG)
    p = jnp.exp(s - jnp.max(s, axis=-1, keepdims=True))
    pn = p / jnp.sum(p, axis=-1, keepdims=True)
    return r0, ls, q, kb, vb, pn


def _attn_fwd(proj, gv, ga, AW):
    T = proj.shape[0]
    HP = AW // LANE

    def body(q_ref, k_ref, v_ref, gv_ref, ga_ref, o_ref, kpad, vpad, bias):
        hp = pl.program_id(0)
        for i in range(2):
            _attn_setup(i, hp, k_ref, v_ref, gv_ref, kpad, vpad, bias)
        col = lax.broadcasted_iota(jnp.int32, (CHUNK, BAND), 1)

        def chunk(n, carry):
            for i in range(2):
                r0, ls, _, _, vb, pn = _attn_probs(n, i, q_ref, kpad, vpad, bias, col)
                o = _dot(pn.astype(BF16), vb, NN)
                r = lax.rsqrt(jnp.mean(o * o, axis=-1, keepdims=True) + EPS)
                o_ref[pl.ds(r0, CHUNK), ls] = (o * r * ga_ref[0:1, ls]).astype(BF16)
            return carry

        lax.fori_loop(0, T // CHUNK, chunk, 0)

    blk = lambda off: pl.BlockSpec((T, LANE), lambda hp: (0, off + hp))
    return pl.pallas_call(
        body, grid=(HP,), name="attn_fwd",
        in_specs=[blk(0), blk(HP), blk(2 * HP), pl.BlockSpec(gv.shape, lambda hp: (0, 0)),
                  pl.BlockSpec((1, LANE), lambda hp: (0, hp))],
        out_specs=pl.BlockSpec((T, LANE), lambda hp: (0, hp)),
        out_shape=_sds((T, AW), BF16),
        scratch_shapes=[pltpu.VMEM((2, T + KPAD, ATTN_HEAD_DIM), BF16), pltpu.VMEM((2, T + KPAD, ATTN_HEAD_DIM), BF16),
                        pltpu.VMEM((2, CHUNK, BAND), F32)],
        compiler_params=_params(("parallel",), big=True),
    )(proj, proj, proj, gv, ga)


def _attn_bwd(proj, dmixin, gv, ga, AW):
    T = proj.shape[0]
    HP = AW // LANE
    scale = ATTN_HEAD_DIM ** -0.5

    def body(q_ref, k_ref, v_ref, dn_ref, gv_ref, ga_ref, dq_ref, dk_ref, dv_ref, dgv_ref, dga_ref,
             kpad, vpad, dkacc, dvacc, bias, dbias):
        hp = pl.program_id(0)
        for i in range(2):
            _attn_setup(i, hp, k_ref, v_ref, gv_ref, kpad, vpad, bias)
        dkacc[...] = jnp.zeros_like(dkacc)
        dvacc[...] = jnp.zeros_like(dvacc)
        dbias[...] = jnp.zeros_like(dbias)
        dga_ref[...] = jnp.zeros_like(dga_ref)
        col = lax.broadcasted_iota(jnp.int32, (CHUNK, BAND), 1)

        def chunk(n, carry):
            for i in range(2):
                r0, ls, q, kb, vb, pn = _attn_probs(n, i, q_ref, kpad, vpad, bias, col)
                pn_b = pn.astype(BF16)
                o = _dot(pn_b, vb, NN)
                r = lax.rsqrt(jnp.mean(o * o, axis=-1, keepdims=True) + EPS)
                dn = dn_ref[pl.ds(r0, CHUNK), ls]
                dga_ref[i:i + 1, :] += _colsum(dn * o * r)
                a = dn * ga_ref[0:1, ls]
                do = r * (a - o * (r * r) * jnp.mean(a * o, axis=-1, keepdims=True))
                do_b = do.astype(BF16)
                dp = _dot(do_b, vb, NT)
                dvacc[i, pl.ds(r0, BAND), :] += _dot(pn_b, do_b, TN)
                ds = pn * (dp - jnp.sum(pn * dp, axis=-1, keepdims=True))
                dbias[i] += ds
                ds_b = ds.astype(BF16)
                dq_ref[pl.ds(r0, CHUNK), ls] = (_dot(ds_b, kb, NN) * scale).astype(BF16)
                dkacc[i, pl.ds(r0, BAND), :] += _dot(ds_b, q, TN) * scale
            return carry

        lax.fori_loop(0, T // CHUNK, chunk, 0)

        rr = lax.broadcasted_iota(jnp.int32, (CHUNK, CHUNK), 0)
        cc = lax.broadcasted_iota(jnp.int32, (CHUNK, CHUNK), 1)
        flip = (rr + cc == CHUNK - 1).astype(F32)
        for i in range(2):
            ls = slice(i * ATTN_HEAD_DIM, (i + 1) * ATTN_HEAD_DIM)
            dk_ref[:, ls] = dkacc[i, KPAD:, :].astype(BF16)
            dv_ref[:, ls] = dvacc[i, KPAD:, :].astype(BF16)
            rev = _dot(flip, dbias[i], NN, HIGHEST)
            dgv_ref[i:i + 1, :] = _colsum(pltpu.roll(rev, BAND - (CHUNK - 1), 1, stride=1, stride_axis=0))

    blk = lambda off: pl.BlockSpec((T, LANE), lambda hp: (0, off + hp))
    accs = lambda dt: pltpu.VMEM((2, T + KPAD, ATTN_HEAD_DIM), dt)
    return pl.pallas_call(
        body, grid=(HP,), name="attn_bwd",
        in_specs=[blk(0), blk(HP), blk(2 * HP), blk(0), pl.BlockSpec(gv.shape, lambda hp: (0, 0)),
                  pl.BlockSpec((1, LANE), lambda hp: (0, hp))],
        out_specs=[blk(0), blk(0), blk(0), pl.BlockSpec((None, 2, BAND), lambda hp: (hp, 0, 0)),
                   pl.BlockSpec((None, 2, ATTN_HEAD_DIM), lambda hp: (hp, 0, 0))],
        out_shape=[_sds((T, AW), BF16), _sds((T, AW), BF16), _sds((T, AW), BF16),
                   _sds((HP, 2, BAND), F32), _sds((HP, 2, ATTN_HEAD_DIM), F32)],
        scratch_shapes=[accs(BF16), accs(BF16), accs(F32), accs(F32),
                        pltpu.VMEM((2, CHUNK, BAND), F32), pltpu.VMEM((2, CHUNK, BAND), F32)],
        compiler_params=_params(("parallel",), big=True),
    )(proj, proj, proj, dmixin, gv, ga)


def _ltri():
    r = lax.broadcasted_iota(jnp.int32, (CHUNK, CHUNK), 0)
    c = lax.broadcasted_iota(jnp.int32, (CHUNK, CHUNK), 1)
    return (c <= r).astype(F32)


def _hgrn_gates(n, q_ref, f_ref, lb_ref, ltri):
    r0 = pl.multiple_of(n * CHUNK, CHUNK)
    rows = pl.ds(r0, CHUNK)
    lb = lb_ref[...]
    qb = q_ref[rows, :]
    sg = _sigmoid(f_ref[rows, :])
    f = lb + (1.0 - lb) * sg
    sq = _sigmoid(qb)
    b = _dot(ltri, jnp.log(f), NN, HIGHEST)
    return rows, lb, qb, sg, f, 1.0 - f, sq, qb * sq, b


def _hgrn_fwd(proj, lb, gn, AW, RW):
    T = proj.shape[0]
    RH, NC, NSUB = RW // LANE, T // CHUNK, CHUNK // SUB
    base = 3 * AW // LANE

    def body(q_ref, f_ref, i_ref, g_ref, lb_ref, gn_ref, mix_ref, o_ref, stall_ref, st, bs, kks, ics):
        st[...] = jnp.zeros_like(st)
        ltri = _ltri()
        rowi = lax.broadcasted_iota(jnp.int32, (SUB, 1), 0)

        def chunk(n, carry):
            rows, _, _, _, _, kk, _, qs, b = _hgrn_gates(n, q_ref, f_ref, lb_ref, ltri)
            ic = i_ref[rows, :]
            stv = st[...]
            stall_ref[n] = stv
            bs[...] = b
            kks[...] = kk
            ics[...] = ic
            o = _dot((qs * jnp.exp(b)).astype(BF16), stv.astype(BF16), NT)
            ic_b = ic.astype(BF16)
            pieces = []
            for blk in range(NSUB):
                s0 = blk * SUB
                bI, qI = b[s0:s0 + SUB], qs[s0:s0 + SUB]
                if blk == 0:
                    oI = jnp.zeros((SUB, LANE), F32)
                else:
                    ref = bs[s0 - 1:s0, :]
                    qt = (qI * jnp.exp(bI - ref)).astype(BF16)
                    kt = (kk[0:s0] * jnp.exp(ref - b[0:s0])).astype(BF16)
                    oI = _dot(_dot(qt, kt, NT).astype(BF16), ic_b[0:s0], NN)
                for s in range(SUB):
                    sr = s0 + s
                    e = jnp.exp(jnp.minimum(bI - bs[sr:sr + 1, :], 0.0))
                    a = jnp.sum(qI * kks[sr:sr + 1, :] * e, axis=-1, keepdims=True)
                    oI = oI + jnp.where(rowi >= s, a, 0.0) * ics[sr:sr + 1, :]
                pieces.append(oI)
            o = o + jnp.concatenate(pieces, axis=0)
            bl = bs[CHUNK - 1:CHUNK, :]
            kd = (kk * jnp.exp(bl - b)).astype(BF16)
            st[...] = stv * jnp.exp(bl) + _dot(ic_b, kd, TN)
            o_ref[rows, :] = o
            r = lax.rsqrt(jnp.mean(o * o, axis=-1, keepdims=True) + EPS)
            gb = g_ref[rows, :]
            mix_ref[rows, :] = (o * r * gn_ref[...] * (gb * _sigmoid(gb))).astype(BF16)
            return carry

        lax.fori_loop(0, NC, chunk, 0)

    blk_in = lambda off: pl.BlockSpec((T, LANE), lambda h: (0, base + off + h))
    col = pl.BlockSpec((T, LANE), lambda h: (0, h))
    tile = pltpu.VMEM((CHUNK, LANE), F32)
    return pl.pallas_call(
        body, grid=(RH,), name="hgrn_fwd",
        in_specs=[blk_in(0), blk_in(RH), blk_in(2 * RH), blk_in(3 * RH), pl.BlockSpec((1, LANE), lambda h: (0, h)),
                  pl.BlockSpec((1, LANE), lambda h: (0, 0))],
        out_specs=[col, col, pl.BlockSpec((None, NC, LANE, LANE), lambda h: (h, 0, 0, 0))],
        out_shape=[_sds((T, RW), BF16), _sds((T, RW), F32), _sds((RH, NC, LANE, LANE), F32)],
        scratch_shapes=[pltpu.VMEM((LANE, LANE), F32), tile, tile, tile],
        compiler_params=_params(("parallel",), big=True),
    )(proj, proj, proj, proj, lb, gn)


def _hgrn_bwd(proj, dmixin, o_b, st_all, lb, gn, AW, RW):
    T = proj.shape[0]
    RH, NC, NSUB = RW // LANE, T // CHUNK, CHUNK // SUB
    base = 3 * AW // LANE

    def body(q_ref, f_ref, i_ref, g_ref, o_ref, dn_ref, stall_ref, lb_ref, gn_ref,
             dq_ref, df_ref, di_ref, dg_ref, dlb_ref, dgn_ref, dst, bs, kks, ics, dos, p2, dic):
        dst[...] = jnp.zeros_like(dst)
        dlb_ref[...] = jnp.zeros_like(dlb_ref)
        dgn_ref[...] = jnp.zeros_like(dgn_ref)
        ltri = _ltri()
        rowi = lax.broadcasted_iota(jnp.int32, (SUB, 1), 0)
        last = lax.broadcasted_iota(jnp.int32, (CHUNK, 1), 0) == CHUNK - 1

        def chunk(k, carry):
            n = NC - 1 - k
            rows, lbv, qb, sg, f, kk, sq, qs, b = _hgrn_gates(n, q_ref, f_ref, lb_ref, ltri)
            ic = i_ref[rows, :]
            stv = stall_ref[n]
            dstv = dst[...]
            o = o_ref[rows, :]
            dn = dn_ref[rows, :]
            gb = g_ref[rows, :]
            sgb = _sigmoid(gb)
            r = lax.rsqrt(jnp.mean(o * o, axis=-1, keepdims=True) + EPS)
            gnv = gn_ref[...]
            dg_ref[rows, :] = (dn * (o * r * gnv) * (sgb * (1.0 + gb * (1.0 - sgb)))).astype(BF16)
            dy = dn * (gb * sgb)
            dgn_ref[...] += _colsum(dy * o * r)
            a_ = dy * gnv
            do = r * (a_ - o * (r * r) * jnp.mean(a_ * o, axis=-1, keepdims=True))
            do_b = do.astype(BF16)
            bs[...] = b
            kks[...] = kk
            ics[...] = ic
            dos[...] = do
            ic_b = ic.astype(BF16)
            eb = jnp.exp(b)
            bl = bs[CHUNK - 1:CHUNK, :]
            ebl = jnp.exp(bl)
            dec = jnp.exp(bl - b)
            kd = (kk * dec).astype(BF16)
            dst_b = dstv.astype(BF16)
            dqs = _dot(do_b, stv.astype(BF16), NN) * eb
            dkk2 = _dot(ic_b, dst_b, NN) * dec
            dic[...] = _dot(kd, dst_b, NT)
            dbl = ebl * _colsum(stv * dstv) + _colsum(kk * dkk2)
            dst[...] = dstv * ebl + _dot(do_b, (qs * eb).astype(BF16), TN)
            p2[...] = jnp.zeros_like(p2)
            p1_pieces = []
            for blk in range(NSUB):
                s0 = blk * SUB
                bI, qI, doI = b[s0:s0 + SUB], qs[s0:s0 + SUB], do[s0:s0 + SUB]
                if blk == 0:
                    p1 = jnp.zeros((SUB, LANE), F32)
                else:
                    ref = bs[s0 - 1:s0, :]
                    eq = jnp.exp(bI - ref)
                    ek = jnp.exp(ref - b[0:s0])
                    qt = (qI * eq).astype(BF16)
                    kt = (kk[0:s0] * ek).astype(BF16)
                    doI_b = doI.astype(BF16)
                    dic[0:s0, :] += _dot(_dot(qt, kt, NT).astype(BF16), doI_b, TN)
                    da = _dot(doI_b, ic_b[0:s0], NT).astype(BF16)
                    p1 = _dot(da, kt, NN) * eq
                    p2[0:s0, :] += _dot(da, qt, TN) * ek
                for s in range(SUB):
                    sr = s0 + s
                    keep = rowi >= s
                    kk_s = kks[sr:sr + 1, :]
                    e = jnp.exp(jnp.minimum(bI - bs[sr:sr + 1, :], 0.0))
                    w = qI * e
                    a = jnp.where(keep, jnp.sum(w * kk_s, axis=-1, keepdims=True), 0.0)
                    da_s = jnp.where(keep, jnp.sum(doI * ics[sr:sr + 1, :], axis=-1, keepdims=True), 0.0)
                    p1 = p1 + da_s * kk_s * e
                    p2[sr:sr + 1, :] += _colsum(da_s * w)
                    dic[sr:sr + 1, :] += _colsum(a * doI)
                p1_pieces.append(p1)
            dqs = dqs + jnp.concatenate(p1_pieces, axis=0)
            dkk = dkk2 + p2[...]
            db = qs * dqs - kk * dkk + jnp.where(last, dbl, 0.0)
            dgl = _dot(ltri, db, TN, HIGHEST)
            dfv = dgl / f - dkk
            df_ref[rows, :] = (dfv * (1.0 - lbv) * sg * (1.0 - sg)).astype(BF16)
            dlb_ref[...] += _colsum(dfv * (1.0 - sg))
            dq_ref[rows, :] = (dqs * (sq * (1.0 + qb * (1.0 - sq)))).astype(BF16)
            di_ref[rows, :] = dic[...].astype(BF16)
            return carry

        lax.fori_loop(0, NC, chunk, 0)

    blk_in = lambda off: pl.BlockSpec((T, LANE), lambda h: (0, base + off + h))
    col = pl.BlockSpec((T, LANE), lambda h: (0, h))
    tile = pltpu.VMEM((CHUNK, LANE), F32)
    return pl.pallas_call(
        body, grid=(RH,), name="hgrn_bwd",
        in_specs=[blk_in(0), blk_in(RH), blk_in(2 * RH), blk_in(3 * RH), col,
                  pl.BlockSpec((T, LANE), lambda h: (0, AW // LANE + h)),
                  pl.BlockSpec((None, NC, LANE, LANE), lambda h: (h, 0, 0, 0)),
                  pl.BlockSpec((1, LANE), lambda h: (0, h)), pl.BlockSpec((1, LANE), lambda h: (0, 0))],
        out_specs=[col, col, col, col, pl.BlockSpec((1, LANE), lambda h: (0, h)),
                   pl.BlockSpec((None, 1, LANE), lambda h: (h, 0, 0))],
        out_shape=[_sds((T, RW), BF16)] * 4 + [_sds((1, RW), F32), _sds((RH, 1, LANE), F32)],
        scratch_shapes=[pltpu.VMEM((LANE, LANE), F32), tile, tile, tile, tile, tile, tile],
        compiler_params=_params(("parallel",), big=True),
    )(proj, proj, proj, proj, o_b, dmixin, st_all, lb, gn)


def _prep(c, lb_logits, rb_pad, max_rel):
    D, RW = c.shape[-1], lb_logits.shape[-1]
    H, rbp = rb_pad.shape

    def body(c_ref, l_ref, rb_ref, cact_ref, lb_ref, gv_ref):
        cv = c_ref[...]
        cact_ref[...] = cv * _sigmoid(cv)
        lb_ref[...] = _sigmoid(l_ref[0:1, :] - l_ref[1:2, :])
        gv_ref[...] = _dot(rb_ref[...], _bias_onehot(rbp, max_rel), NN, HIGHEST)

    return pl.pallas_call(
        body, name="prep", out_shape=[_sds((1, D), F32), _sds((1, RW), F32), _sds((H, BAND), F32)],
    )(c, lb_logits, rb_pad)


def _mod_part(c_all, w_ada_s, b_ada_s):
    B, D = c_all.shape
    ns = w_ada_s.shape[1]
    tn = _tile(ns, 768, LANE)

    def body(c_ref, w_ref, b_ref, o_ref):
        o_ref[...] = _dot(c_ref[...], w_ref[...], NN) + b_ref[...]

    return pl.pallas_call(
        body, grid=(ns // tn,), name="mod_part",
        in_specs=[pl.BlockSpec((B, D), lambda j: (0, 0)), pl.BlockSpec((D, tn), lambda j: (0, j)),
                  pl.BlockSpec((1, tn), lambda j: (0, j))],
        out_specs=pl.BlockSpec((B, tn), lambda j: (0, j)),
        out_shape=_sds((B, ns), F32), compiler_params=_params(("parallel",)),
    )(c_all, w_ada_s, b_ada_s)


def _adam(w, g, m, v):
    m = ADAM_B1 * m + (1.0 - ADAM_B1) * g
    v = ADAM_B2 * v + (1.0 - ADAM_B2) * (g * g)
    m_hat = m / (1.0 - ADAM_B1 ** ADAM_STEP)
    v_hat = v / (1.0 - ADAM_B2 ** ADAM_STEP)
    return -ADAM_LR * (m_hat / (jnp.sqrt(v_hat) + ADAM_EPS) + ADAM_WD * w), m, v


def _adam_ada(c_all, dmod_s, w, m, v):
    B, D = c_all.shape
    ns = w.shape[1]
    tr, tn = _tile(D, 512, LANE), _tile(ns, 768, LANE)

    def body(c_ref, d_ref, w_ref, m_ref, v_ref, g_out, dw_out, m_out, v_out):
        g = _dot(c_ref[...], d_ref[...], TN)
        g_out[...] = g
        dw_out[...], m_out[...], v_out[...] = _adam(w_ref[...], g, m_ref[...], v_ref[...])

    big = pl.BlockSpec((tr, tn), lambda i, j: (i, j))
    return pl.pallas_call(
        body, grid=(D // tr, ns // tn), name="adam_w_ada",
        in_specs=[pl.BlockSpec((B, tr), lambda i, j: (0, i)), pl.BlockSpec((B, tn), lambda i, j: (0, j)),
                  big, big, big],
        out_specs=[big] * 4, out_shape=[_sds((D, ns), F32)] * 4,
        compiler_params=_params(("parallel", "parallel")),
    )(c_all, dmod_s, w, m, v)


def _adam_shard(parts, w, m, v, name):
    R, C = w.shape
    tr = _tile(R, 256, 16)

    def body(p_ref, w_ref, m_ref, v_ref, g_out, dw_out, m_out, v_out):
        g = p_ref[0].astype(F32)
        for k in range(1, N_DEV // 2):
            g = g + p_ref[k].astype(F32)
        g_out[...] = g
        dw_out[...], m_out[...], v_out[...] = _adam(w_ref[...], g, m_ref[...], v_ref[...])

    big = pl.BlockSpec((tr, C), lambda i: (i, 0))
    return pl.pallas_call(
        body, grid=(R // tr,), name=name,
        in_specs=[pl.BlockSpec((N_DEV // 2, tr, C), lambda i: (0, i, 0)), big, big, big],
        out_specs=[big] * 4, out_shape=[_sds((R, C), F32)] * 4,
        compiler_params=_params(("parallel",), big=True),
    )(parts, w, m, v)


def _pair_sum(g8, land, core, name):
    _, NCHIP, R, C = g8.shape
    tr = _tile(R, 256, 16)

    def body(core_ref, g_ref, l_ref, o_ref):
        o_ref[...] = (g_ref[...].astype(F32) + l_ref[...].astype(F32)).astype(BF16)

    return pl.pallas_call(
        body, name=name,
        grid_spec=pltpu.PrefetchScalarGridSpec(
            num_scalar_prefetch=1, grid=(NCHIP, R // tr),
            in_specs=[pl.BlockSpec((None, None, tr, C), lambda k, i, core_ref: (core_ref[0], k, i, 0)),
                      pl.BlockSpec((None, tr, C), lambda k, i, core_ref: (k, i, 0))],
            out_specs=pl.BlockSpec((None, tr, C), lambda k, i, core_ref: (k, i, 0))),
        out_shape=_sds((NCHIP, R, C), BF16), compiler_params=_params(("parallel", "parallel")),
    )(core, g8, land)


SMALL = ("b_ada", "rel_bias", "attn_norm_g", "lb_logits", "gnorm_g", "ln1_g", "ln1_b", "ln2_g", "ln2_b")


def _small_update(parts, loss_parts, lbv, ws, ms, vs, max_rel):
    n = len(SMALL)

    def body(*refs):
        part_refs = dict(zip(SMALL, refs[:n]))
        loss_in, lb_ref = refs[n], refs[n + 1]
        w_refs, m_refs, v_refs = refs[n + 2:2 * n + 2], refs[2 * n + 2:3 * n + 2], refs[3 * n + 2:4 * n + 2]
        outs = refs[4 * n + 2:]

        def total(ref):
            tot = ref[0]
            for k in range(1, N_DEV):
                tot = tot + ref[k]
            return tot

        outs[0][...] = jnp.sum(total(loss_in), axis=-1, keepdims=True)
        for idx, name in enumerate(SMALL):
            g = total(part_refs[name])
            if name == "rel_bias":
                g = _dot(g, _bias_onehot(w_refs[idx].shape[1], max_rel), NT, HIGHEST)
            elif name == "lb_logits":
                lb = lb_ref[...]
                sign = (1 - 2 * lax.broadcasted_iota(jnp.int32, (2, 1), 0)).astype(F32)
                g = sign * (g * lb * (1.0 - lb))
            elif name == "gnorm_g":
                g = _colsum(g)
            dw, mm, vv = _adam(w_refs[idx][...], g, m_refs[idx][...], v_refs[idx][...])
            outs[1 + 4 * idx][...] = g
            outs[2 + 4 * idx][...] = dw
            outs[3 + 4 * idx][...] = mm
            outs[4 + 4 * idx][...] = vv

    out_shape = [_sds((1, 1), F32)]
    for w in ws:
        out_shape += [_sds(w.shape, F32)] * 4
    return pl.pallas_call(body, name="small_update", out_shape=out_shape, compiler_params=_params(big=True))(
        *[parts[k] for k in SMALL], loss_parts, lbv, *ws, *ms, *vs)


def _place():
    x, y, c = lax.axis_index("x"), lax.axis_index("y"), lax.axis_index("c")
    return x, y, c, [(1 - x, y), (x, 1 - y), (1 - x, 1 - y)]


def _all_gather(shard, name):
    HBM = pl.BlockSpec(memory_space=pl.ANY)

    def body(x_ref, out_ref, send_sems, recv_sems, local_sem):
        x, y, c, chips = _place()
        me, sibling = (x, y, c), (x, y, 1 - c)

        def slot(px, py, pc):
            return out_ref.at[4 * px + 2 * py + pc]

        def copy(k, block, to, src=None):
            return pltpu.make_async_remote_copy(
                src_ref=slot(*block) if src is None else src, dst_ref=slot(*block),
                send_sem=send_sems.at[k], recv_sem=recv_sems.at[k], device_id=to, device_id_type=MESH)

        mine = pltpu.make_async_copy(x_ref, slot(*me), local_sem)
        mine.start()
        first = [copy(0, me, sibling, src=x_ref)]
        first += [copy(1 + j, me, (*chip, c), src=x_ref) for j, chip in enumerate(chips)]
        for cp in first:
            cp.start()
        passed = [copy(4 + j, (*chip, c), sibling) for j, chip in enumerate(chips)]
        for j, chip in enumerate(chips):
            copy(1 + j, (*chip, c), me).wait_recv()
            passed[j].start()
        copy(0, sibling, me).wait_recv()
        for j, chip in enumerate(chips):
            copy(4 + j, (*chip, 1 - c), me).wait_recv()
        for cp in first + passed:
            cp.wait_send()
        mine.wait()

    return pl.pallas_call(
        body, name=name, out_shape=_sds((N_DEV,) + shard.shape, shard.dtype),
        in_specs=[HBM], out_specs=HBM,
        scratch_shapes=[pltpu.SemaphoreType.DMA((7,)), pltpu.SemaphoreType.DMA((7,)), pltpu.SemaphoreType.DMA(())],
    )(shard)


SEM_SPEC = pl.BlockSpec(memory_space=pltpu.SEMAPHORE)
HBM_SPEC = pl.BlockSpec(memory_space=pltpu.HBM)
EFFECT = pltpu.SideEffectType.DATAFLOW_SIDE_EFFECTING


def _remote(src, dst, send_sems, recv_sems, k, dev):
    return pltpu.make_async_remote_copy(src_ref=src, dst_ref=dst, send_sem=send_sems.at[k], recv_sem=recv_sems.at[k],
                                        device_id=dev, device_id_type=MESH)


def _copy_start(name, bufs, plan, n, after):
    nb = len(bufs)

    def body(*refs):
        send_sems, recv_sems = refs[nb + 1], refs[nb + 2]
        for k, (src, dst, dev) in enumerate(plan(*refs[:nb])):
            _remote(src, dst, send_sems, recv_sems, k, dev).start()
        refs[-1][...] = jnp.zeros_like(refs[-1])

    out = pl.pallas_call(
        body, name=name,
        out_shape=(pltpu.SemaphoreType.DMA((n,)), pltpu.SemaphoreType.DMA((n,)),
                   *[pltpu.HBM(b.shape, b.dtype) for b in bufs], _sds((8, LANE), F32)),
        in_specs=[HBM_SPEC] * nb + [ORDER_ONLY],
        out_specs=(SEM_SPEC, SEM_SPEC, *[HBM_SPEC] * nb, pl.BlockSpec(memory_space=pltpu.VMEM)),
        input_output_aliases={i: 2 + i for i in range(nb)},
        compiler_params=pltpu.CompilerParams(has_side_effects=EFFECT),
    )(*[pltpu.with_memory_space_constraint(b, pltpu.HBM) for b in bufs], after)
    return (out[0], out[1]), list(out[2:2 + nb]), out[-1]


def _copy_wait(name, sems, bufs, plan, after):
    nb = len(bufs)

    def body(*refs):
        send_sems, recv_sems = refs[nb], refs[nb + 1]
        for k, (src, dst, dev) in enumerate(plan(*refs[:nb])):
            cp = _remote(src, dst, send_sems, recv_sems, k, dev)
            cp.wait_send()
            cp.wait_recv()

    out = pl.pallas_call(
        body, name=name, out_shape=tuple(pltpu.HBM(b.shape, b.dtype) for b in bufs),
        in_specs=[HBM_SPEC] * nb + [SEM_SPEC, SEM_SPEC, pl.BlockSpec(memory_space=pl.ANY)],
        out_specs=tuple([HBM_SPEC] * nb), input_output_aliases={i: i for i in range(nb)},
        compiler_params=pltpu.CompilerParams(has_side_effects=EFFECT),
    )(*bufs, sems[0], sems[1], after)
    return list(out)


def _ag_plan_chips(shard_ref, out_ref):
    x, y, c, chips = _place()
    mine = out_ref.at[4 * x + 2 * y + c]
    return [(shard_ref, mine, (x, y, 1 - c))] + [(shard_ref, mine, (*chip, c)) for chip in chips]


def _ag_plan_pass(out_ref):
    x, y, c, chips = _place()
    slots = [out_ref.at[4 * chip[0] + 2 * chip[1] + c] for chip in chips]
    return [(s, s, (x, y, 1 - c)) for s in slots]


def _rs_plan_pair(g_ref, land_ref):
    x, y, c, _ = _place()
    return [(g_ref.at[1 - c], land_ref, (x, y, 1 - c))]


def _rs_plan_chips(p_ref, land_ref):
    x, y, c, chips = _place()
    return [(p_ref.at[2 * chip[0] + chip[1]], land_ref.at[2 * x + y], (*chip, c)) for chip in chips]


class _Gather:
    def __init__(self, shard, me, tag, after):
        self.tag = tag
        out = lax.dynamic_update_slice(lax.empty((N_DEV,) + shard.shape, shard.dtype), shard[None],
                                       (me,) + (0,) * shard.ndim)
        self.sems, (self.shard, self.out), self.token = _copy_start(
            "ag_start_" + tag, [shard, out], _ag_plan_chips, 4, after)

    def arrived_from_chips(self, after):
        _, out = _copy_wait("ag_wait_" + self.tag, self.sems, [self.shard, self.out], _ag_plan_chips, after)
        self.sems, (self.out,), _ = _copy_start("ag_pass_" + self.tag, [out], _ag_plan_pass, 3, after)

    def passed_on(self, after):
        return _copy_wait("ag_pass_wait_" + self.tag, self.sems, [self.out], _ag_plan_pass, after)[0]


class _ReduceScatter:
    def __init__(self, g8, tag):
        self.tag = tag
        land = lax.empty(g8.shape[1:], g8.dtype)
        self.sems, self.bufs, self.token = _copy_start("rs_pair_start_" + tag, [g8, land], _rs_plan_pair, 1, g8)

    def pair_done(self, core, chip, after, start_after=None):
        g8, land = _copy_wait("rs_pair_wait_" + self.tag, self.sems, self.bufs, _rs_plan_pair, after)
        p4 = _pair_sum(g8, land, core, "rs_pair_sum_" + self.tag)
        own = lax.dynamic_slice_in_dim(p4, chip, 1, axis=0)
        land2 = lax.dynamic_update_slice(lax.empty(p4.shape, p4.dtype), own, (chip, 0, 0))
        self.sems, self.bufs, self.token = _copy_start(
            "rs_chips_start_" + self.tag, [p4, land2], _rs_plan_chips, 3, p4 if start_after is None else start_after)

    def sums(self, after):
        return _copy_wait("rs_chips_wait_" + self.tag, self.sems, self.bufs, _rs_plan_chips, after)[1]


BIG = ("w_in", "w_o", "w_ffn_in", "w_ffn_out")
ORDER = ("w_ada", "b_ada", "w_in", "rel_bias", "attn_norm_g", "lb_logits", "gnorm_g", "w_o", "ln1_g", "ln1_b",
         "w_ffn_in", "w_ffn_out", "ln2_g", "ln2_b")


def kernel(x, c, w_ada, b_ada, w_in, rel_bias, attn_norm_g, lb_logits, gnorm_g, w_o, ln1_g, ln1_b, w_ffn_in, w_ffn_out, ln2_g, ln2_b, loss_target, m_w_ada, m_b_ada, m_w_in, m_rel_bias, m_attn_norm_g, m_lb_logits, m_gnorm_g, m_w_o, m_ln1_g, m_ln1_b, m_w_ffn_in, m_w_ffn_out, m_ln2_g, m_ln2_b, v_w_ada, v_b_ada, v_w_in, v_rel_bias, v_attn_norm_g, v_lb_logits, v_gnorm_g, v_w_o, v_ln1_g, v_ln1_b, v_w_ffn_in, v_w_ffn_out, v_ln2_g, v_ln2_b):
    W = dict(w_ada=w_ada, b_ada=b_ada, w_in=w_in, rel_bias=rel_bias, attn_norm_g=attn_norm_g, lb_logits=lb_logits,
             gnorm_g=gnorm_g, w_o=w_o, ln1_g=ln1_g, ln1_b=ln1_b, w_ffn_in=w_ffn_in, w_ffn_out=w_ffn_out,
             ln2_g=ln2_g, ln2_b=ln2_b)
    M = dict(w_ada=m_w_ada, b_ada=m_b_ada, w_in=m_w_in, rel_bias=m_rel_bias, attn_norm_g=m_attn_norm_g,
             lb_logits=m_lb_logits, gnorm_g=m_gnorm_g, w_o=m_w_o, ln1_g=m_ln1_g, ln1_b=m_ln1_b,
             w_ffn_in=m_w_ffn_in, w_ffn_out=m_w_ffn_out, ln2_g=m_ln2_g, ln2_b=m_ln2_b)
    V = dict(w_ada=v_w_ada, b_ada=v_b_ada, w_in=v_w_in, rel_bias=v_rel_bias, attn_norm_g=v_attn_norm_g,
             lb_logits=v_lb_logits, gnorm_g=v_gnorm_g, w_o=v_w_o, ln1_g=v_ln1_g, ln1_b=v_ln1_b,
             w_ffn_in=v_w_ffn_in, w_ffn_out=v_w_ffn_out, ln2_g=v_ln2_g, ln2_b=v_ln2_b)

    x2, tgt = x[0], loss_target[0]
    T, D = x2.shape
    AW, RW = attn_norm_g.shape[-1], lb_logits.shape[-1]
    MIX = AW + RW
    H, RH = AW // ATTN_HEAD_DIM, RW // LANE
    RB = rel_bias.shape[-1]
    max_rel = (RB - 1) // 2
    rbp = -(-RB // LANE) * LANE
    F = w_ffn_out.shape[1] * N_DEV
    half = N_DEV // 2
    xi, yi, ci = lax.axis_index("x"), lax.axis_index("y"), lax.axis_index("c")
    me = 4 * xi + 2 * yi + ci
    core = jnp.reshape(ci, (1,)).astype(jnp.int32)
    pad_rb = lambda a: jnp.pad(a[0], ((0, 0), (0, rbp - RB)))

    chip = 2 * xi + yi

    c_act, lbv, gv = _prep(c, lb_logits, pad_rb(rel_bias), max_rel)
    c_all = _all_gather(c_act, "ag_c").reshape(N_DEV, D)
    ns_ada = w_ada.shape[-1]
    mod_part = _mod_part(c_all, w_ada[0], lax.dynamic_slice_in_dim(b_ada, me * ns_ada, ns_ada, axis=1))
    mod_all = _all_gather(mod_part, "ag_mod")
    mod6 = lax.dynamic_index_in_dim(mod_all, me, axis=1, keepdims=False).reshape(6, D)

    ag_in = _Gather(w_in[0].astype(BF16), me, "w_in", mod_all)
    ag_o = _Gather(w_o[0].astype(BF16), me, "w_o", ag_in.token)
    ag_f1 = _Gather(w_ffn_in[0].astype(BF16), me, "w_ffn_in", ag_o.token)
    ag_f2 = _Gather(w_ffn_out[0].astype(BF16), me, "w_ffn_out", ag_f1.token)

    h1 = _ln_mod(x2, mod6 + ag_f2.token[0, 0])
    ag_in.arrived_from_chips(h1)
    wg_in = ag_in.passed_on(h1)
    proj = _mm_gathered(h1, wg_in, "in_proj")
    ag_o.arrived_from_chips(proj)
    mix_a = _attn_fwd(proj, gv, attn_norm_g, AW)
    wg_o = ag_o.passed_on(mix_a).reshape(MIX, D)
    mix_b, o_b, st_all = _hgrn_fwd(proj, lbv, gnorm_g, AW, RW)
    ag_f1.arrived_from_chips(mix_b)
    mixin = jnp.concatenate([mix_a, mix_b], axis=1)
    mix = _mm_nn(mixin, wg_o, "out_proj")
    x1, h2 = _mid_fwd(x2, mix, mod6, ln1_g, ln1_b)
    wg_f1 = ag_f1.passed_on(h2)
    gu, act = _mm_swiglu(h2, wg_f1)
    ag_f2.arrived_from_chips(act)
    wg_f2 = ag_f2.passed_on(act).reshape(F, D)
    ff = _mm_nn(act, wg_f2, "ffn_out")
    dff, dx1a, vec_a = _final(x1, ff, mod6, ln2_g, ln2_b, tgt)

    du = _mm_swiglu_bwd(dff, wg_f2, gu)
    rs_f2 = _ReduceScatter(_mm_tn_rows(dff, act, dff, F // N_DEV, "grad_w_ffn_out"), "w_ffn_out")
    tm = _tile(T, 512, 16)
    du_ij = lambda tm_, ns: pl.BlockSpec((None, tm_, ns), lambda i, j: (j // half, i, j % half))
    du_jm = lambda tm_, ns: pl.BlockSpec((None, tm_, ns), lambda j, m: (j // half, m, j % half))
    dh2 = _mm_gathered_nt(rs_f2.token, du, du_ij, wg_f1, T, tm, "ffn_in_bwd")
    rs_f2.pair_done(core, chip, dh2)
    gw_f1 = _mm_tn_gathered(rs_f2.token, h2, du, du_jm, wg_f1.shape[-1], tm, "grad_w_ffn_in")
    rs_f1 = _ReduceScatter(gw_f1.reshape(2, half, D, -1), "w_ffn_in")
    dmix, dxa, vec_b = _mid_bwd(x2, mix, x1, dx1a, dh2, mod6 + rs_f1.token[0, 0], ln1_g)
    dmixin = _mm_nt(dmix, wg_o, "out_proj_bwd")
    rs_f1.pair_done(core, chip, dmixin)
    rs_o = _ReduceScatter(_mm_tn_rows(rs_f1.token, mixin, dmix, MIX // N_DEV, "grad_w_o"), "w_o")
    dq, dk, dv, dgv, dga = _attn_bwd(proj, dmixin, gv + rs_o.token[0, 0], attn_norm_g, AW)
    rs_o.pair_done(core, chip, dq)
    dqb, dfl, dib, dgb, dlb, dgn = _hgrn_bwd(proj, dmixin, o_b, st_all, lbv + rs_o.token[0, 0], gnorm_g, AW, RW)
    dproj = jnp.concatenate([dq, dk, dv, dqb, dfl, dib, dgb], axis=1)
    p_ij = lambda tm_, ns: pl.BlockSpec((tm_, ns), lambda i, j: (i, j))
    p_jm = lambda tm_, ns: pl.BlockSpec((tm_, ns), lambda j, m: (m, j))
    gw_in = _mm_tn_gathered(rs_o.token, h1, dproj, p_jm, wg_in.shape[-1], tm, "grad_w_in")
    rs_in = _ReduceScatter(gw_in.reshape(2, half, D, -1), "w_in")
    dh1 = _mm_gathered_nt(rs_in.token, dproj, p_ij, wg_in, T, tm, "in_proj_bwd")
    grad_x, vec_c = _first_bwd(x2, dh1, dxa, mod6)

    dmod = jnp.concatenate([vec_c[1:2], vec_c[0:1], vec_b[4:5], vec_b[1:2], vec_b[0:1], vec_a[2:3]], axis=0)
    pieces = dict(b_ada=dmod, rel_bias=dgv, attn_norm_g=dga, lb_logits=dlb, gnorm_g=dgn, ln1_g=vec_b[2:3],
                  ln1_b=vec_b[3:4], ln2_g=vec_a[0:1], ln2_b=vec_a[1:2], loss=vec_a[3:4])
    widths = dict(b_ada=(1, 6 * D), rel_bias=(H, BAND), attn_norm_g=(1, AW), lb_logits=(1, RW), gnorm_g=(RH, LANE),
                  ln1_g=(1, D), ln1_b=(1, D), ln2_g=(1, D), ln2_b=(1, D), loss=(1, D))
    packed = jnp.concatenate([pieces[k].reshape(-1, LANE) for k in widths], axis=0)
    gathered = _all_gather(packed, "ag_small")
    rs_in.pair_done(core, chip, dh1, start_after=gathered)
    parts, r0 = {}, 0
    for k, (rows, width) in widths.items():
        nr = rows * width // LANE
        parts[k] = gathered[:, r0:r0 + nr, :].reshape(N_DEV, rows, width)
        r0 += nr
    prep_small = lambda d, k: pad_rb(d[k]) if k == "rel_bias" else d[k]
    small = _small_update(parts, parts["loss"], lbv, [prep_small(W, k) for k in SMALL],
                          [prep_small(M, k) for k in SMALL], [prep_small(V, k) for k in SMALL], max_rel)
    loss = small[0].reshape(())
    res = {}
    for idx, k in enumerate(SMALL):
        four = small[1 + 4 * idx:5 + 4 * idx]
        if k == "rel_bias":
            four = [a[:, :RB][None] for a in four]
        res[k] = list(four)

    dmod_s = lax.dynamic_slice_in_dim(parts["b_ada"].reshape(N_DEV, 6 * D), me * ns_ada, ns_ada, axis=1)
    res["w_ada"] = [a[None] for a in _adam_ada(c_all, dmod_s, w_ada[0], m_w_ada[0], v_w_ada[0])]
    after = res["w_ada"][0]
    for k, rs in (("w_ffn_out", rs_f2), ("w_ffn_in", rs_f1), ("w_o", rs_o), ("w_in", rs_in)):
        four = _adam_shard(rs.sums(after), W[k][0], M[k][0], V[k][0], "adam_" + k)
        res[k] = [a[None] for a in four]
        after = four[0]

    out = [loss, grad_x[None]]
    for field in range(4):
        out += [res[k][field] for k in ORDER]
    return tuple(out)
```

```python
import functools

import jax
import jax.numpy as jnp
from jax import lax
from jax.experimental import pallas as pl
from jax.experimental.pallas import tpu as pltpu

F32 = jnp.float32
BF16 = jnp.bfloat16
MESH = pl.DeviceIdType.MESH
HIGHEST = lax.Precision.HIGHEST

N_DEV = 8
CHUNK = 64
N_PAST = 8
QBLK = 4 * CHUNK
KPAD = N_PAST * CHUNK
WIN = KPAD + QBLK
TAB = 1024
ATTN_HEAD_DIM = 64
REC_HEAD_DIM = 128
SUB = 16
LANE = 128
EPS = 1e-5
ALPHA = 2.0 ** 0.25
ADAM_LR, ADAM_B1, ADAM_B2, ADAM_EPS, ADAM_WD, ADAM_STEP = 0.001, 0.9, 0.999, 1e-08, 0.01, 10
NEG = -1e30
VMEM_LIMIT = 56 * 1024 * 1024


def _sds(shape, dtype):
    return jax.ShapeDtypeStruct(tuple(shape), dtype)


def _tile(n, pref, mult):
    best = None
    for t in range(mult, min(n, pref) + 1, mult):
        if n % t == 0:
            best = t
    return n if best is None else best


def _params(sem=None, big=False):
    kw = {}
    if sem is not None:
        kw["dimension_semantics"] = sem
    if big:
        kw["vmem_limit_bytes"] = VMEM_LIMIT
    return pltpu.CompilerParams(**kw)


def _sigmoid(v):
    return 1.0 / (1.0 + jnp.exp(-v))


def _dot(a, b, dims, precision=None):
    return lax.dot_general(a, b, (dims, ((), ())), preferred_element_type=F32, precision=precision)


NN = ((1,), (0,))
NT = ((1,), (1,))
TN = ((0,), (0,))


def _ln(v):
    mu = jnp.mean(v, axis=-1, keepdims=True)
    d = v - mu
    rstd = lax.rsqrt(jnp.mean(d * d, axis=-1, keepdims=True) + EPS)
    return d * rstd, rstd


def _ln_bwd(dxh, xh, rstd):
    return rstd * (dxh - jnp.mean(dxh, axis=-1, keepdims=True) - xh * jnp.mean(dxh * xh, axis=-1, keepdims=True))


def _colsum(v):
    return jnp.sum(v, axis=0, keepdims=True)


def _ln_mod(x2, mod6):
    T, D = x2.shape
    tm = _tile(T, 256, 8)

    def body(x_ref, mod_ref, o_ref):
        xh, _ = _ln(x_ref[...])
        o_ref[...] = (xh * (1.0 + mod_ref[1:2, :]) + mod_ref[0:1, :]).astype(BF16)

    return pl.pallas_call(
        body, grid=(T // tm,), name="ln_mod",
        in_specs=[pl.BlockSpec((tm, D), lambda i: (i, 0)), pl.BlockSpec((6, D), lambda i: (0, 0))],
        out_specs=pl.BlockSpec((tm, D), lambda i: (i, 0)),
        out_shape=_sds((T, D), BF16), compiler_params=_params(("parallel",)),
    )(x2, mod6)


def _mid_fwd(x2, mix, mod6, ln1_g, ln1_b):
    T, D = x2.shape
    tm = _tile(T, 256, 8)

    def body(x_ref, mix_ref, mod_ref, g_ref, b_ref, x1_ref, h2_ref):
        zh, _ = _ln(ALPHA * x_ref[...] + mod_ref[2:3, :] * mix_ref[...])
        x1 = zh * g_ref[...] + b_ref[...]
        x1_ref[...] = x1
        xh, _ = _ln(x1)
        h2_ref[...] = (xh * (1.0 + mod_ref[4:5, :]) + mod_ref[3:4, :]).astype(BF16)

    row = pl.BlockSpec((tm, D), lambda i: (i, 0))
    vec = pl.BlockSpec((1, D), lambda i: (0, 0))
    return pl.pallas_call(
        body, grid=(T // tm,), name="mid_fwd",
        in_specs=[row, row, pl.BlockSpec((6, D), lambda i: (0, 0)), vec, vec],
        out_specs=[row, row],
        out_shape=[_sds((T, D), F32), _sds((T, D), BF16)], compiler_params=_params(("parallel",)),
    )(x2, mix, mod6, ln1_g, ln1_b)


def _final(x1, ff, mod6, ln2_g, ln2_b, tgt):
    T, D = x1.shape
    tm = _tile(T, 256, 8)

    def body(x1_ref, ff_ref, mod_ref, g_ref, b_ref, t_ref, dff_ref, dx1_ref, vec_ref):
        @pl.when(pl.program_id(0) == 0)
        def _():
            vec_ref[...] = jnp.zeros_like(vec_ref)

        ff_v = ff_ref[...]
        gate2 = mod_ref[5:6, :]
        zh, rstd = _ln(ALPHA * x1_ref[...] + gate2 * ff_v)
        err = zh * g_ref[...] + b_ref[...] - t_ref[...]
        dy = err * (1.0 / D)
        dz = _ln_bwd(dy * g_ref[...], zh, rstd)
        dff_ref[...] = (gate2 * dz).astype(BF16)
        dx1_ref[...] = ALPHA * dz
        vec_ref[0:1, :] += _colsum(dy * zh)
        vec_ref[1:2, :] += _colsum(dy)
        vec_ref[2:3, :] += _colsum(dz * ff_v)
        vec_ref[3:4, :] += _colsum(err * err) * (0.5 / D)

    row = pl.BlockSpec((tm, D), lambda i: (i, 0))
    vec = pl.BlockSpec((1, D), lambda i: (0, 0))
    return pl.pallas_call(
        body, grid=(T // tm,), name="final_fwd_bwd",
        in_specs=[row, row, pl.BlockSpec((6, D), lambda i: (0, 0)), vec, vec, row],
        out_specs=[row, row, pl.BlockSpec((8, D), lambda i: (0, 0))],
        out_shape=[_sds((T, D), BF16), _sds((T, D), F32), _sds((8, D), F32)],
        compiler_params=_params(("arbitrary",)),
    )(x1, ff, mod6, ln2_g, ln2_b, tgt)


def _mid_bwd(x2, mix, x1, dx1a, dh2, mod6, ln1_g):
    T, D = x2.shape
    tm = _tile(T, 256, 8)

    def body(x_ref, mix_ref, x1_ref, dx1a_ref, dh2_ref, mod_ref, g_ref, dmix_ref, dxa_ref, vec_ref):
        @pl.when(pl.program_id(0) == 0)
        def _():
            vec_ref[...] = jnp.zeros_like(vec_ref)

        dh2 = dh2_ref[...]
        xh, rstd = _ln(x1_ref[...])
        dx1 = dx1a_ref[...] + _ln_bwd(dh2 * (1.0 + mod_ref[4:5, :]), xh, rstd)
        mix_v = mix_ref[...]
        gate1 = mod_ref[2:3, :]
        zh, rstdz = _ln(ALPHA * x_ref[...] + gate1 * mix_v)
        dz = _ln_bwd(dx1 * g_ref[...], zh, rstdz)
        dmix_ref[...] = (gate1 * dz).astype(BF16)
        dxa_ref[...] = ALPHA * dz
        vec_ref[0:1, :] += _colsum(dh2 * xh)
        vec_ref[1:2, :] += _colsum(dh2)
        vec_ref[2:3, :] += _colsum(dx1 * zh)
        vec_ref[3:4, :] += _colsum(dx1)
        vec_ref[4:5, :] += _colsum(dz * mix_v)

    row = pl.BlockSpec((tm, D), lambda i: (i, 0))
    vec = pl.BlockSpec((1, D), lambda i: (0, 0))
    return pl.pallas_call(
        body, grid=(T // tm,), name="mid_bwd",
        in_specs=[row, row, row, row, row, pl.BlockSpec((6, D), lambda i: (0, 0)), vec],
        out_specs=[row, row, pl.BlockSpec((8, D), lambda i: (0, 0))],
        out_shape=[_sds((T, D), BF16), _sds((T, D), F32), _sds((8, D), F32)],
        compiler_params=_params(("arbitrary",)),
    )(x2, mix, x1, dx1a, dh2, mod6, ln1_g)


def _first_bwd(x2, dh1, dxa, mod6):
    T, D = x2.shape
    tm = _tile(T, 256, 8)

    def body(x_ref, dh1_ref, dxa_ref, mod_ref, gx_ref, vec_ref):
        @pl.when(pl.program_id(0) == 0)
        def _():
            vec_ref[...] = jnp.zeros_like(vec_ref)

        dh1 = dh1_ref[...]
        xh, rstd = _ln(x_ref[...])
        gx_ref[...] = dxa_ref[...] + _ln_bwd(dh1 * (1.0 + mod_ref[1:2, :]), xh, rstd)
        vec_ref[0:1, :] += _colsum(dh1 * xh)
        vec_ref[1:2, :] += _colsum(dh1)

    row = pl.BlockSpec((tm, D), lambda i: (i, 0))
    return pl.pallas_call(
        body, grid=(T // tm,), name="first_bwd",
        in_specs=[row, row, row, pl.BlockSpec((6, D), lambda i: (0, 0))],
        out_specs=[row, pl.BlockSpec((8, D), lambda i: (0, 0))],
        out_shape=[_sds((T, D), F32), _sds((8, D), F32)],
        compiler_params=_params(("arbitrary",)),
    )(x2, dh1, dxa, mod6)


def _slot(j):
    return (j % 2) * 4 + j // 2


def _mm_gathered(a, wg, name):
    M, K = a.shape
    _, _, ns = wg.shape
    tm = _tile(M, 512, 16)

    def body(a_ref, w_ref, o_ref):
        o_ref[...] = _dot(a_ref[...], w_ref[...], NN)

    return pl.pallas_call(
        body, grid=(N_DEV, M // tm), name=name,
        in_specs=[pl.BlockSpec((tm, K), lambda j, i: (i, 0)), pl.BlockSpec((None, K, ns), lambda j, i: (j, 0, 0))],
        out_specs=pl.BlockSpec((tm, ns), lambda j, i: (i, j)),
        out_shape=_sds((M, N_DEV * ns), F32), compiler_params=_params(("parallel", "parallel"), big=True),
    )(a, wg)


def _mm_nn(a, b, name):
    M, K = a.shape
    _, N = b.shape
    tm, tn, tk = _tile(M, 512, 16), _tile(N, 1024, LANE), _tile(K, 2048, LANE)

    def body(a_ref, b_ref, o_ref):
        @pl.when(pl.program_id(2) == 0)
        def _():
            o_ref[...] = jnp.zeros_like(o_ref)

        o_ref[...] += _dot(a_ref[...], b_ref[...], NN)

    return pl.pallas_call(
        body, grid=(M // tm, N // tn, K // tk), name=name,
        in_specs=[pl.BlockSpec((tm, tk), lambda i, j, k: (i, k)), pl.BlockSpec((tk, tn), lambda i, j, k: (k, j))],
        out_specs=pl.BlockSpec((tm, tn), lambda i, j, k: (i, j)),
        out_shape=_sds((M, N), F32), compiler_params=_params(("parallel", "parallel", "arbitrary"), big=True),
    )(a, b)


def _mm_nt(a, b, name):
    M, K = a.shape
    N, _ = b.shape
    tm, tn = _tile(M, 512, 16), _tile(N, 1024, LANE)

    def body(a_ref, b_ref, o_ref):
        o_ref[...] = _dot(a_ref[...], b_ref[...], NT)

    return pl.pallas_call(
        body, grid=(M // tm, N // tn), name=name,
        in_specs=[pl.BlockSpec((tm, K), lambda i, j: (i, 0)), pl.BlockSpec((tn, K), lambda i, j: (j, 0))],
        out_specs=pl.BlockSpec((tm, tn), lambda i, j: (i, j)),
        out_shape=_sds((M, N), F32), compiler_params=_params(("parallel", "parallel"), big=True),
    )(a, b)


def _mm_swiglu(h2, wg):
    M, K = h2.shape
    _, _, ns = wg.shape
    half = N_DEV // 2
    tm = _tile(M, 256, 16)

    def body(a_ref, wgate_ref, wup_ref, gu_ref, act_ref):
        a = a_ref[...]
        g = _dot(a, wgate_ref[...], NN)
        u = _dot(a, wup_ref[...], NN)
        gu_ref[0] = g
        gu_ref[1] = u
        act_ref[...] = (g * _sigmoid(g) * u).astype(BF16)

    return pl.pallas_call(
        body, grid=(half, M // tm), name="ffn_in_swiglu",
        in_specs=[pl.BlockSpec((tm, K), lambda j, i: (i, 0)),
                  pl.BlockSpec((None, K, ns), lambda j, i: (j, 0, 0)),
                  pl.BlockSpec((None, K, ns), lambda j, i: (j + half, 0, 0))],
        out_specs=[pl.BlockSpec((2, tm, ns), lambda j, i: (0, i, j)), pl.BlockSpec((tm, ns), lambda j, i: (i, j))],
        out_shape=[_sds((2, M, half * ns), F32), _sds((M, half * ns), BF16)],
        compiler_params=_params(("parallel", "parallel"), big=True),
    )(h2, wg, wg)


def _mm_swiglu_bwd(dff, w2, gu):
    M, K = dff.shape
    F = w2.shape[0]
    tm, tn = _tile(M, 256, 16), _tile(F, 1408, LANE)

    def body(a_ref, b_ref, gu_ref, du_ref):
        da = _dot(a_ref[...], b_ref[...], NT)
        g = gu_ref[0]
        u = gu_ref[1]
        sg = _sigmoid(g)
        du_ref[0] = (da * u * (sg * (1.0 + g * (1.0 - sg)))).astype(BF16)
        du_ref[1] = (da * (g * sg)).astype(BF16)

    return pl.pallas_call(
        body, grid=(M // tm, F // tn), name="ffn_out_bwd_swiglu",
        in_specs=[pl.BlockSpec((tm, K), lambda i, j: (i, 0)), pl.BlockSpec((tn, K), lambda i, j: (j, 0)),
                  pl.BlockSpec((2, tm, tn), lambda i, j: (0, i, j))],
        out_specs=pl.BlockSpec((2, tm, tn), lambda i, j: (0, i, j)),
        out_shape=_sds((2, M, F), BF16), compiler_params=_params(("parallel", "parallel"), big=True),
    )(dff, w2, gu)


ORDER_ONLY = pl.BlockSpec(memory_space=pl.ANY)


def _mm_tn_rows(dep, a, b, rs, name):
    M, Ka = a.shape
    _, N = b.shape
    tm = _tile(M, 512, 16)

    def body(_, a_ref, b_ref, o_ref, acc_ref):
        m = pl.program_id(1)

        @pl.when(m == 0)
        def _():
            acc_ref[...] = jnp.zeros_like(acc_ref)

        acc_ref[...] += _dot(a_ref[...], b_ref[...], TN)

        @pl.when(m == pl.num_programs(1) - 1)
        def _():
            o_ref[0, 0] = acc_ref[0:rs, :].astype(BF16)
            o_ref[1, 0] = acc_ref[rs:2 * rs, :].astype(BF16)

    return pl.pallas_call(
        body, grid=(N_DEV // 2, M // tm), name=name,
        in_specs=[ORDER_ONLY, pl.BlockSpec((tm, 2 * rs), lambda ch, m: (m, ch)),
                  pl.BlockSpec((tm, N), lambda ch, m: (m, 0))],
        out_specs=pl.BlockSpec((2, 1, rs, N), lambda ch, m: (0, ch, 0, 0)),
        out_shape=_sds((2, N_DEV // 2, rs, N), BF16),
        scratch_shapes=[pltpu.VMEM((2 * rs, N), F32)],
        compiler_params=_params(("parallel", "arbitrary"), big=True),
    )(dep, a, b)


def _mm_gathered_nt(dep, a, a_spec, wg, M, tm, name):
    _, K, ns = wg.shape

    def body(_, a_ref, w_ref, o_ref):
        @pl.when(pl.program_id(1) == 0)
        def _():
            o_ref[...] = jnp.zeros_like(o_ref)

        o_ref[...] += _dot(a_ref[...], w_ref[...], NT)

    return pl.pallas_call(
        body, grid=(M // tm, N_DEV), name=name,
        in_specs=[ORDER_ONLY, a_spec(tm, ns), pl.BlockSpec((None, K, ns), lambda i, j: (j, 0, 0))],
        out_specs=pl.BlockSpec((tm, K), lambda i, j: (i, 0)),
        out_shape=_sds((M, K), F32), compiler_params=_params(("parallel", "arbitrary"), big=True),
    )(dep, a, wg)


def _mm_tn_gathered(dep, h, a, a_spec, ns, tm, name):
    M, K = h.shape

    def body(_, h_ref, a_ref, o_ref, acc_ref):
        m = pl.program_id(1)

        @pl.when(m == 0)
        def _():
            acc_ref[...] = jnp.zeros_like(acc_ref)

        acc_ref[...] += _dot(h_ref[...], a_ref[...], TN)

        @pl.when(m == pl.num_programs(1) - 1)
        def _():
            o_ref[...] = acc_ref[...].astype(BF16)

    return pl.pallas_call(
        body, grid=(N_DEV, M // tm), name=name,
        in_specs=[ORDER_ONLY, pl.BlockSpec((tm, K), lambda j, m: (m, 0)), a_spec(tm, ns)],
        out_specs=pl.BlockSpec((None, K, ns), lambda j, m: (_slot(j), 0, 0)),
        out_shape=_sds((N_DEV, K, ns), BF16),
        scratch_shapes=[pltpu.VMEM((K, ns), F32)],
        compiler_params=_params(("parallel", "arbitrary"), big=True),
    )(dep, h, a)


def _bias_onehot(rbp, max_rel):
    r = lax.broadcasted_iota(jnp.int32, (rbp, TAB), 0)
    m = lax.broadcasted_iota(jnp.int32, (rbp, TAB), 1)
    dist = KPAD - jnp.where(m < WIN, m, m - TAB)
    return (r == jnp.clip(dist, -max_rel, max_rel) + max_rel).astype(F32)


def _attn_setup(i, hp, k_ref, v_ref, gv_ref, kpad, vpad, bias):
    ls = slice(i * ATTN_HEAD_DIM, (i + 1) * ATTN_HEAD_DIM)
    kpad[i, 0:KPAD, :] = jnp.zeros((KPAD, ATTN_HEAD_DIM), BF16)
    vpad[i, 0:KPAD, :] = jnp.zeros((KPAD, ATTN_HEAD_DIM), BF16)
    kpad[i, KPAD:, :] = k_ref[:, ls].astype(BF16)
    vpad[i, KPAD:, :] = v_ref[:, ls].astype(BF16)
    gvrow = gv_ref[pl.ds(hp * 2 + i, 1), :]
    tab = pltpu.roll(jnp.broadcast_to(gvrow, (QBLK, TAB)), 0, 1, stride=1, stride_axis=0)
    row = lax.broadcasted_iota(jnp.int32, (QBLK, WIN), 0)
    col = lax.broadcasted_iota(jnp.int32, (QBLK, WIN), 1)
    first = jnp.bitwise_and(row, -CHUNK)
    seen = jnp.logical_and(col >= first, col < first + (N_PAST + 1) * CHUNK)
    bias[i] = jnp.where(seen, tab[:, 0:WIN], NEG)


def _attn_probs(b, i, q_ref, kpad, vpad, bias, col):
    ls = slice(i * ATTN_HEAD_DIM, (i + 1) * ATTN_HEAD_DIM)
    r0 = pl.multiple_of(b * QBLK, QBLK)
    q = q_ref[pl.ds(r0, QBLK), ls].astype(BF16)
    kw = kpad[i, pl.ds(r0, WIN), :]
    vw = vpad[i, pl.ds(r0, WIN), :]
    s = _dot(q, kw, NT) * (ATTN_HEAD_DIM ** -0.5) + bias[i]
    s = jnp.where(col >= KPAD - r0, s, NEG)
    p = jnp.exp(s - jnp.max(s, axis=-1, keepdims=True))
    pn = p / jnp.sum(p, axis=-1, keepdims=True)
    return r0, ls, q, kw, vw, pn


def _attn_fwd(proj, gv, ga, AW):
    T = proj.shape[0]
    HP = AW // LANE

    def body(q_ref, k_ref, v_ref, gv_ref, ga_ref, o_ref, kpad, vpad, bias):
        hp = pl.program_id(0)
        for i in range(2):
            _attn_setup(i, hp, k_ref, v_ref, gv_ref, kpad, vpad, bias)
        col = lax.broadcasted_iota(jnp.int32, (QBLK, WIN), 1)

        def block(b, carry):
            for i in range(2):
                r0, ls, _, _, vw, pn = _attn_probs(b, i, q_ref, kpad, vpad, bias, col)
                o = _dot(pn.astype(BF16), vw, NN)
                r = lax.rsqrt(jnp.mean(o * o, axis=-1, keepdims=True) + EPS)
                o_ref[pl.ds(r0, QBLK), ls] = (o * r * ga_ref[0:1, ls]).astype(BF16)
            return carry

        lax.fori_loop(0, T // QBLK, block, 0)

    blk = lambda off: pl.BlockSpec((T, LANE), lambda hp: (0, off + hp))
    return pl.pallas_call(
        body, grid=(HP,), name="attn_fwd",
        in_specs=[blk(0), blk(HP), blk(2 * HP), pl.BlockSpec(gv.shape, lambda hp: (0, 0)),
                  pl.BlockSpec((1, LANE), lambda hp: (0, hp))],
        out_specs=pl.BlockSpec((T, LANE), lambda hp: (0, hp)),
        out_shape=_sds((T, AW), BF16),
        scratch_shapes=[pltpu.VMEM((2, T + KPAD, ATTN_HEAD_DIM), BF16), pltpu.VMEM((2, T + KPAD, ATTN_HEAD_DIM), BF16),
                        pltpu.VMEM((2, QBLK, WIN), F32)],
        compiler_params=_params(("parallel",), big=True),
    )(proj, proj, proj, gv, ga)


def _attn_bwd(proj, dmixin, gv, ga, AW):
    T = proj.shape[0]
    HP = AW // LANE
    scale = ATTN_HEAD_DIM ** -0.5

    def body(q_ref, k_ref, v_ref, dn_ref, gv_ref, ga_ref, dq_ref, dk_ref, dv_ref, dgv_ref, dga_ref,
             kpad, vpad, dkacc, dvacc, bias, dbias):
        hp = pl.program_id(0)
        for i in range(2):
            _attn_setup(i, hp, k_ref, v_ref, gv_ref, kpad, vpad, bias)
        dkacc[...] = jnp.zeros_like(dkacc)
        dvacc[...] = jnp.zeros_like(dvacc)
        dbias[...] = jnp.zeros_like(dbias)
        dga_ref[...] = jnp.zeros_like(dga_ref)
        col = lax.broadcasted_iota(jnp.int32, (QBLK, WIN), 1)

        def block(b, carry):
            for i in range(2):
                r0, ls, q, kw, vw, pn = _attn_probs(b, i, q_ref, kpad, vpad, bias, col)
                pn_b = pn.astype(BF16)
                o = _dot(pn_b, vw, NN)
                r = lax.rsqrt(jnp.mean(o * o, axis=-1, keepdims=True) + EPS)
                dn = dn_ref[pl.ds(r0, QBLK), ls]
                dga_ref[i:i + 1, :] += _colsum(dn * o * r)
                a = dn * ga_ref[0:1, ls]
                do = r * (a - o * (r * r) * jnp.mean(a * o, axis=-1, keepdims=True))
                do_b = do.astype(BF16)
                dp = _dot(do_b, vw, NT)
                dvacc[i, pl.ds(r0, WIN), :] += _dot(pn_b, do_b, TN)
                ds = pn * (dp - jnp.sum(pn * dp, axis=-1, keepdims=True))
                dbias[i] += ds
                ds_b = ds.astype(BF16)
                dq_ref[pl.ds(r0, QBLK), ls] = (_dot(ds_b, kw, NN) * scale).astype(BF16)
                dkacc[i, pl.ds(r0, WIN), :] += _dot(ds_b, q, TN) * scale
            return carry

        lax.fori_loop(0, T // QBLK, block, 0)

        rr = lax.broadcasted_iota(jnp.int32, (QBLK, QBLK), 0)
        cc = lax.broadcasted_iota(jnp.int32, (QBLK, QBLK), 1)
        flip = (rr + cc == QBLK - 1).astype(BF16)
        for i in range(2):
            ls = slice(i * ATTN_HEAD_DIM, (i + 1) * ATTN_HEAD_DIM)
            dk_ref[:, ls] = dkacc[i, KPAD:, :].astype(BF16)
            dv_ref[:, ls] = dvacc[i, KPAD:, :].astype(BF16)
            full = jnp.concatenate([dbias[i], jnp.zeros((QBLK, TAB - WIN), F32)], axis=1)
            hi = full.astype(BF16)
            lo = (full - hi.astype(F32)).astype(BF16)
            rev = _dot(flip, hi, NN) + _dot(flip, lo, NN)
            dgv_ref[i:i + 1, :] = _colsum(pltpu.roll(rev, TAB - (QBLK - 1), 1, stride=1, stride_axis=0))

    blk = lambda off: pl.BlockSpec((T, LANE), lambda hp: (0, off + hp))
    accs = lambda dt: pltpu.VMEM((2, T + KPAD, ATTN_HEAD_DIM), dt)
    return pl.pallas_call(
        body, grid=(HP,), name="attn_bwd",
        in_specs=[blk(0), blk(HP), blk(2 * HP), blk(0), pl.BlockSpec(gv.shape, lambda hp: (0, 0)),
                  pl.BlockSpec((1, LANE), lambda hp: (0, hp))],
        out_specs=[blk(0), blk(0), blk(0), pl.BlockSpec((None, 2, TAB), lambda hp: (hp, 0, 0)),
                   pl.BlockSpec((None, 2, ATTN_HEAD_DIM), lambda hp: (hp, 0, 0))],
        out_shape=[_sds((T, AW), BF16), _sds((T, AW), BF16), _sds((T, AW), BF16),
                   _sds((HP, 2, TAB), F32), _sds((HP, 2, ATTN_HEAD_DIM), F32)],
        scratch_shapes=[accs(BF16), accs(BF16), accs(F32), accs(F32),
                        pltpu.VMEM((2, QBLK, WIN), F32), pltpu.VMEM((2, QBLK, WIN), F32)],
        compiler_params=_params(("parallel",), big=True),
    )(proj, proj, proj, dmixin, gv, ga)


def _ltri():
    r = lax.broadcasted_iota(jnp.int32, (CHUNK, CHUNK), 0)
    c = lax.broadcasted_iota(jnp.int32, (CHUNK, CHUNK), 1)
    return (c <= r).astype(F32)


def _hgrn_gates(n, q_ref, f_ref, lb_ref, ltri):
    r0 = pl.multiple_of(n * CHUNK, CHUNK)
    rows = pl.ds(r0, CHUNK)
    lb = lb_ref[...]
    qb = q_ref[rows, :]
    sg = _sigmoid(f_ref[rows, :])
    f = lb + (1.0 - lb) * sg
    sq = _sigmoid(qb)
    b = _dot(ltri, jnp.log(f), NN, HIGHEST)
    return rows, lb, qb, sg, f, 1.0 - f, sq, qb * sq, b


def _hgrn_fwd(proj, lb, gn, AW, RW):
    T = proj.shape[0]
    RH, NC, NSUB = RW // LANE, T // CHUNK, CHUNK // SUB
    base = 3 * AW // LANE

    def body(q_ref, f_ref, i_ref, g_ref, lb_ref, gn_ref, mix_ref, o_ref, stall_ref, st, bs, kks, ics):
        st[...] = jnp.zeros_like(st)
        ltri = _ltri()
        rowi = lax.broadcasted_iota(jnp.int32, (SUB, 1), 0)

        def chunk(n, carry):
            rows, _, _, _, _, kk, _, qs, b = _hgrn_gates(n, q_ref, f_ref, lb_ref, ltri)
            ic = i_ref[rows, :]
            stv = st[...]
            stall_ref[n] = stv
            bs[...] = b
            kks[...] = kk
            ics[...] = ic
            o = _dot((qs * jnp.exp(b)).astype(BF16), stv.astype(BF16), NT)
            ic_b = ic.astype(BF16)
            pieces = []
            for blk in range(NSUB):
                s0 = blk * SUB
                bI, qI = b[s0:s0 + SUB], qs[s0:s0 + SUB]
                if blk == 0:
                    oI = jnp.zeros((SUB, LANE), F32)
                else:
                    ref = bs[s0 - 1:s0, :]
                    qt = (qI * jnp.exp(bI - ref)).astype(BF16)
                    kt = (kk[0:s0] * jnp.exp(ref - b[0:s0])).astype(BF16)
                    oI = _dot(_dot(qt, kt, NT).astype(BF16), ic_b[0:s0], NN)
                for s in range(SUB):
                    sr = s0 + s
                    e = jnp.exp(jnp.minimum(bI - bs[sr:sr + 1, :], 0.0))
                    a = jnp.sum(qI * kks[sr:sr + 1, :] * e, axis=-1, keepdims=True)
                    oI = oI + jnp.where(rowi >= s, a, 0.0) * ics[sr:sr + 1, :]
                pieces.append(oI)
            o = o + jnp.concatenate(pieces, axis=0)
            bl = bs[CHUNK - 1:CHUNK, :]
            kd = (kk * jnp.exp(bl - b)).astype(BF16)
            st[...] = stv * jnp.exp(bl) + _dot(ic_b, kd, TN)
            o_ref[rows, :] = o
            r = lax.rsqrt(jnp.mean(o * o, axis=-1, keepdims=True) + EPS)
            gb = g_ref[rows, :]
            mix_ref[rows, :] = (o * r * gn_ref[...] * (gb * _sigmoid(gb))).astype(BF16)
            return carry

        lax.fori_loop(0, NC, chunk, 0)

    blk_in = lambda off: pl.BlockSpec((T, LANE), lambda h: (0, base + off + h))
    col = pl.BlockSpec((T, LANE), lambda h: (0, h))
    tile = pltpu.VMEM((CHUNK, LANE), F32)
    return pl.pallas_call(
        body, grid=(RH,), name="hgrn_fwd",
        in_specs=[blk_in(0), blk_in(RH), blk_in(2 * RH), blk_in(3 * RH), pl.BlockSpec((1, LANE), lambda h: (0, h)),
                  pl.BlockSpec((1, LANE), lambda h: (0, 0))],
        out_specs=[col, col, pl.BlockSpec((None, NC, LANE, LANE), lambda h: (h, 0, 0, 0))],
        out_shape=[_sds((T, RW), BF16), _sds((T, RW), F32), _sds((RH, NC, LANE, LANE), F32)],
        scratch_shapes=[pltpu.VMEM((LANE, LANE), F32), tile, tile, tile],
        compiler_params=_params(("parallel",), big=True),
    )(proj, proj, proj, proj, lb, gn)


def _hgrn_bwd(proj, dmixin, o_b, st_all, lb, gn, AW, RW):
    T = proj.shape[0]
    RH, NC, NSUB = RW // LANE, T // CHUNK, CHUNK // SUB
    base = 3 * AW // LANE

    def body(q_ref, f_ref, i_ref, g_ref, o_ref, dn_ref, stall_ref, lb_ref, gn_ref,
             dq_ref, df_ref, di_ref, dg_ref, dlb_ref, dgn_ref, dst, bs, kks, ics, dos, p2, dic):
        dst[...] = jnp.zeros_like(dst)
        dlb_ref[...] = jnp.zeros_like(dlb_ref)
        dgn_ref[...] = jnp.zeros_like(dgn_ref)
        ltri = _ltri()
        rowi = lax.broadcasted_iota(jnp.int32, (SUB, 1), 0)
        last = lax.broadcasted_iota(jnp.int32, (CHUNK, 1), 0) == CHUNK - 1

        def chunk(k, carry):
            n = NC - 1 - k
            rows, lbv, qb, sg, f, kk, sq, qs, b = _hgrn_gates(n, q_ref, f_ref, lb_ref, ltri)
            ic = i_ref[rows, :]
            stv = stall_ref[n]
            dstv = dst[...]
            o = o_ref[rows, :]
            dn = dn_ref[rows, :]
            gb = g_ref[rows, :]
            sgb = _sigmoid(gb)
            r = lax.rsqrt(jnp.mean(o * o, axis=-1, keepdims=True) + EPS)
            gnv = gn_ref[...]
            dg_ref[rows, :] = (dn * (o * r * gnv) * (sgb * (1.0 + gb * (1.0 - sgb)))).astype(BF16)
            dy = dn * (gb * sgb)
            dgn_ref[...] += _colsum(dy * o * r)
            a_ = dy * gnv
            do = r * (a_ - o * (r * r) * jnp.mean(a_ * o, axis=-1, keepdims=True))
            do_b = do.astype(BF16)
            bs[...] = b
            kks[...] = kk
            ics[...] = ic
            dos[...] = do
            ic_b = ic.astype(BF16)
            eb = jnp.exp(b)
            bl = bs[CHUNK - 1:CHUNK, :]
            ebl = jnp.exp(bl)
            dec = jnp.exp(bl - b)
            kd = (kk * dec).astype(BF16)
            dst_b = dstv.astype(BF16)
            dqs = _dot(do_b, stv.astype(BF16), NN) * eb
            dkk2 = _dot(ic_b, dst_b, NN) * dec
            dic[...] = _dot(kd, dst_b, NT)
            dbl = ebl * _colsum(stv * dstv) + _colsum(kk * dkk2)
            dst[...] = dstv * ebl + _dot(do_b, (qs * eb).astype(BF16), TN)
            p2[...] = jnp.zeros_like(p2)
            p1_pieces = []
            for blk in range(NSUB):
                s0 = blk * SUB
                bI, qI, doI = b[s0:s0 + SUB], qs[s0:s0 + SUB], do[s0:s0 + SUB]
                if blk == 0:
                    p1 = jnp.zeros((SUB, LANE), F32)
                else:
                    ref = bs[s0 - 1:s0, :]
                    eq = jnp.exp(bI - ref)
                    ek = jnp.exp(ref - b[0:s0])
                    qt = (qI * eq).astype(BF16)
                    kt = (kk[0:s0] * ek).astype(BF16)
                    doI_b = doI.astype(BF16)
                    dic[0:s0, :] += _dot(_dot(qt, kt, NT).astype(BF16), doI_b, TN)
                    da = _dot(doI_b, ic_b[0:s0], NT).astype(BF16)
                    p1 = _dot(da, kt, NN) * eq
                    p2[0:s0, :] += _dot(da, qt, TN) * ek
                for s in range(SUB):
                    sr = s0 + s
                    keep = rowi >= s
                    kk_s = kks[sr:sr + 1, :]
                    e = jnp.exp(jnp.minimum(bI - bs[sr:sr + 1, :], 0.0))
                    w = qI * e
                    a = jnp.where(keep, jnp.sum(w * kk_s, axis=-1, keepdims=True), 0.0)
                    da_s = jnp.where(keep, jnp.sum(doI * ics[sr:sr + 1, :], axis=-1, keepdims=True), 0.0)
                    p1 = p1 + da_s * kk_s * e
                    p2[sr:sr + 1, :] += _colsum(da_s * w)
                    dic[sr:sr + 1, :] += _colsum(a * doI)
                p1_pieces.append(p1)
            dqs = dqs + jnp.concatenate(p1_pieces, axis=0)
            dkk = dkk2 + p2[...]
            db = qs * dqs - kk * dkk + jnp.where(last, dbl, 0.0)
            dgl = _dot(ltri, db, TN, HIGHEST)
            dfv = dgl / f - dkk
            df_ref[rows, :] = (dfv * (1.0 - lbv) * sg * (1.0 - sg)).astype(BF16)
            dlb_ref[...] += _colsum(dfv * (1.0 - sg))
            dq_ref[rows, :] = (dqs * (sq * (1.0 + qb * (1.0 - sq)))).astype(BF16)
            di_ref[rows, :] = dic[...].astype(BF16)
            return carry

        lax.fori_loop(0, NC, chunk, 0)

    blk_in = lambda off: pl.BlockSpec((T, LANE), lambda h: (0, base + off + h))
    col = pl.BlockSpec((T, LANE), lambda h: (0, h))
    tile = pltpu.VMEM((CHUNK, LANE), F32)
    return pl.pallas_call(
        body, grid=(RH,), name="hgrn_bwd",
        in_specs=[blk_in(0), blk_in(RH), blk_in(2 * RH), blk_in(3 * RH), col,
                  pl.BlockSpec((T, LANE), lambda h: (0, AW // LANE + h)),
                  pl.BlockSpec((None, NC, LANE, LANE), lambda h: (h, 0, 0, 0)),
                  pl.BlockSpec((1, LANE), lambda h: (0, h)), pl.BlockSpec((1, LANE), lambda h: (0, 0))],
        out_specs=[col, col, col, col, pl.BlockSpec((1, LANE), lambda h: (0, h)),
                   pl.BlockSpec((None, 1, LANE), lambda h: (h, 0, 0))],
        out_shape=[_sds((T, RW), BF16)] * 4 + [_sds((1, RW), F32), _sds((RH, 1, LANE), F32)],
        scratch_shapes=[pltpu.VMEM((LANE, LANE), F32), tile, tile, tile, tile, tile, tile],
        compiler_params=_params(("parallel",), big=True),
    )(proj, proj, proj, proj, o_b, dmixin, st_all, lb, gn)


def _prep(c, lb_logits, rb_pad, max_rel):
    D, RW = c.shape[-1], lb_logits.shape[-1]
    H, rbp = rb_pad.shape

    def body(c_ref, l_ref, rb_ref, cact_ref, lb_ref, gv_ref):
        cv = c_ref[...]
        cact_ref[...] = cv * _sigmoid(cv)
        lb_ref[...] = _sigmoid(l_ref[0:1, :] - l_ref[1:2, :])
        gv_ref[...] = _dot(rb_ref[...], _bias_onehot(rbp, max_rel), NN, HIGHEST)

    return pl.pallas_call(
        body, name="prep", out_shape=[_sds((1, D), F32), _sds((1, RW), F32), _sds((H, TAB), F32)],
    )(c, lb_logits, rb_pad)


def _mod_part(c_all, w_ada_s, b_ada_s):
    B, D = c_all.shape
    ns = w_ada_s.shape[1]
    tn = _tile(ns, 768, LANE)

    def body(c_ref, w_ref, b_ref, o_ref):
        o_ref[...] = _dot(c_ref[...], w_ref[...], NN) + b_ref[...]

    return pl.pallas_call(
        body, grid=(ns // tn,), name="mod_part",
        in_specs=[pl.BlockSpec((B, D), lambda j: (0, 0)), pl.BlockSpec((D, tn), lambda j: (0, j)),
                  pl.BlockSpec((1, tn), lambda j: (0, j))],
        out_specs=pl.BlockSpec((B, tn), lambda j: (0, j)),
        out_shape=_sds((B, ns), F32), compiler_params=_params(("parallel",)),
    )(c_all, w_ada_s, b_ada_s)


def _adam(w, g, m, v):
    m = ADAM_B1 * m + (1.0 - ADAM_B1) * g
    v = ADAM_B2 * v + (1.0 - ADAM_B2) * (g * g)
    m_hat = m / (1.0 - ADAM_B1 ** ADAM_STEP)
    v_hat = v / (1.0 - ADAM_B2 ** ADAM_STEP)
    return -ADAM_LR * (m_hat / (jnp.sqrt(v_hat) + ADAM_EPS) + ADAM_WD * w), m, v


def _adam_ada(c_all, dmod_s, w, m, v):
    B, D = c_all.shape
    ns = w.shape[1]
    tr, tn = _tile(D, 512, LANE), _tile(ns, 768, LANE)

    def body(c_ref, d_ref, w_ref, m_ref, v_ref, g_out, dw_out, m_out, v_out):
        g = _dot(c_ref[...], d_ref[...], TN)
        g_out[...] = g
        dw_out[...], m_out[...], v_out[...] = _adam(w_ref[...], g, m_ref[...], v_ref[...])

    big = pl.BlockSpec((tr, tn), lambda i, j: (i, j))
    return pl.pallas_call(
        body, grid=(D // tr, ns // tn), name="adam_w_ada",
        in_specs=[pl.BlockSpec((B, tr), lambda i, j: (0, i)), pl.BlockSpec((B, tn), lambda i, j: (0, j)),
                  big, big, big],
        out_specs=[big] * 4, out_shape=[_sds((D, ns), F32)] * 4,
        compiler_params=_params(("parallel", "parallel")),
    )(c_all, dmod_s, w, m, v)


def _adam_shard(parts, w, m, v, name):
    R, C = w.shape
    tr = _tile(R, 256, 16)

    def body(p_ref, w_ref, m_ref, v_ref, g_out, dw_out, m_out, v_out):
        g = p_ref[0].astype(F32)
        for k in range(1, N_DEV // 2):
            g = g + p_ref[k].astype(F32)
        g_out[...] = g
        dw_out[...], m_out[...], v_out[...] = _adam(w_ref[...], g, m_ref[...], v_ref[...])

    big = pl.BlockSpec((tr, C), lambda i: (i, 0))
    return pl.pallas_call(
        body, grid=(R // tr,), name=name,
        in_specs=[pl.BlockSpec((N_DEV // 2, tr, C), lambda i: (0, i, 0)), big, big, big],
        out_specs=[big] * 4, out_shape=[_sds((R, C), F32)] * 4,
        compiler_params=_params(("parallel",), big=True),
    )(parts, w, m, v)


def _pair_sum(g8, land, core, name):
    _, NCHIP, R, C = g8.shape
    tr = _tile(R, 256, 16)

    def body(core_ref, g_ref, l_ref, o_ref):
        o_ref[...] = (g_ref[...].astype(F32) + l_ref[...].astype(F32)).astype(BF16)

    return pl.pallas_call(
        body, name=name,
        grid_spec=pltpu.PrefetchScalarGridSpec(
            num_scalar_prefetch=1, grid=(NCHIP, R // tr),
            in_specs=[pl.BlockSpec((None, None, tr, C), lambda k, i, core_ref: (core_ref[0], k, i, 0)),
                      pl.BlockSpec((None, tr, C), lambda k, i, core_ref: (k, i, 0))],
            out_specs=pl.BlockSpec((None, tr, C), lambda k, i, core_ref: (k, i, 0))),
        out_shape=_sds((NCHIP, R, C), BF16), compiler_params=_params(("parallel", "parallel")),
    )(core, g8, land)


SMALL = ("b_ada", "rel_bias", "attn_norm_g", "lb_logits", "gnorm_g", "ln1_g", "ln1_b", "ln2_g", "ln2_b")


def _small_update(parts, loss_parts, lbv, ws, ms, vs, max_rel):
    n = len(SMALL)

    def body(*refs):
        part_refs = dict(zip(SMALL, refs[:n]))
        loss_in, lb_ref = refs[n], refs[n + 1]
        w_refs, m_refs, v_refs = refs[n + 2:2 * n + 2], refs[2 * n + 2:3 * n + 2], refs[3 * n + 2:4 * n + 2]
        outs = refs[4 * n + 2:]

        def total(ref):
            tot = ref[0]
            for k in range(1, N_DEV):
                tot = tot + ref[k]
            return tot

        outs[0][...] = jnp.sum(total(loss_in), axis=-1, keepdims=True)
        for idx, name in enumerate(SMALL):
            g = total(part_refs[name])
            if name == "rel_bias":
                g = _dot(g, _bias_onehot(w_refs[idx].shape[1], max_rel), NT, HIGHEST)
            elif name == "lb_logits":
                lb = lb_ref[...]
                sign = (1 - 2 * lax.broadcasted_iota(jnp.int32, (2, 1), 0)).astype(F32)
                g = sign * (g * lb * (1.0 - lb))
            elif name == "gnorm_g":
                g = _colsum(g)
            dw, mm, vv = _adam(w_refs[idx][...], g, m_refs[idx][...], v_refs[idx][...])
            outs[1 + 4 * idx][...] = g
            outs[2 + 4 * idx][...] = dw
            outs[3 + 4 * idx][...] = mm
            outs[4 + 4 * idx][...] = vv

    out_shape = [_sds((1, 1), F32)]
    for w in ws:
        out_shape += [_sds(w.shape, F32)] * 4
    return pl.pallas_call(body, name="small_update", out_shape=out_shape, compiler_params=_params(big=True))(
        *[parts[k] for k in SMALL], loss_parts, lbv, *ws, *ms, *vs)


def _place():
    x, y, c = lax.axis_index("x"), lax.axis_index("y"), lax.axis_index("c")
    return x, y, c, [(1 - x, y), (x, 1 - y), (1 - x, 1 - y)]


def _all_gather(shard, name):
    HBM = pl.BlockSpec(memory_space=pl.ANY)

    def body(x_ref, out_ref, send_sems, recv_sems, local_sem):
        x, y, c, chips = _place()
        me, sibling = (x, y, c), (x, y, 1 - c)

        def slot(px, py, pc):
            return out_ref.at[4 * px + 2 * py + pc]

        def copy(k, block, to, src=None):
            return pltpu.make_async_remote_copy(
                src_ref=slot(*block) if src is None else src, dst_ref=slot(*block),
                send_sem=send_sems.at[k], recv_sem=recv_sems.at[k], device_id=to, device_id_type=MESH)

        mine = pltpu.make_async_copy(x_ref, slot(*me), local_sem)
        mine.start()
        first = [copy(0, me, sibling, src=x_ref)]
        first += [copy(1 + j, me, (*chip, c), src=x_ref) for j, chip in enumerate(chips)]
        for cp in first:
            cp.start()
        passed = [copy(4 + j, (*chip, c), sibling) for j, chip in enumerate(chips)]
        for j, chip in enumerate(chips):
            copy(1 + j, (*chip, c), me).wait_recv()
            passed[j].start()
        copy(0, sibling, me).wait_recv()
        for j, chip in enumerate(chips):
            copy(4 + j, (*chip, 1 - c), me).wait_recv()
        for cp in first + passed:
            cp.wait_send()
        mine.wait()

    return pl.pallas_call(
        body, name=name, out_shape=_sds((N_DEV,) + shard.shape, shard.dtype),
        in_specs=[HBM], out_specs=HBM,
        scratch_shapes=[pltpu.SemaphoreType.DMA((7,)), pltpu.SemaphoreType.DMA((7,)), pltpu.SemaphoreType.DMA(())],
    )(shard)


SEM_SPEC = pl.BlockSpec(memory_space=pltpu.SEMAPHORE)
HBM_SPEC = pl.BlockSpec(memory_space=pltpu.HBM)
EFFECT = pltpu.SideEffectType.DATAFLOW_SIDE_EFFECTING


def _remote(src, dst, send_sems, recv_sems, k, dev):
    return pltpu.make_async_remote_copy(src_ref=src, dst_ref=dst, send_sem=send_sems.at[k], recv_sem=recv_sems.at[k],
                                        device_id=dev, device_id_type=MESH)


def _copy_start(name, bufs, plan, n, after):
    nb = len(bufs)

    def body(*refs):
        send_sems, recv_sems = refs[nb + 1], refs[nb + 2]
        for k, (src, dst, dev) in enumerate(plan(*refs[:nb])):
            _remote(src, dst, send_sems, recv_sems, k, dev).start()
        refs[-1][...] = jnp.zeros_like(refs[-1])

    out = pl.pallas_call(
        body, name=name,
        out_shape=(pltpu.SemaphoreType.DMA((n,)), pltpu.SemaphoreType.DMA((n,)),
                   *[pltpu.HBM(b.shape, b.dtype) for b in bufs], _sds((8, LANE), F32)),
        in_specs=[HBM_SPEC] * nb + [ORDER_ONLY],
        out_specs=(SEM_SPEC, SEM_SPEC, *[HBM_SPEC] * nb, pl.BlockSpec(memory_space=pltpu.VMEM)),
        input_output_aliases={i: 2 + i for i in range(nb)},
        compiler_params=pltpu.CompilerParams(has_side_effects=EFFECT),
    )(*[pltpu.with_memory_space_constraint(b, pltpu.HBM) for b in bufs], after)
    return (out[0], out[1]), list(out[2:2 + nb]), out[-1]


def _copy_wait(name, sems, bufs, plan, after):
    nb = len(bufs)

    def body(*refs):
        send_sems, recv_sems = refs[nb], refs[nb + 1]
        for k, (src, dst, dev) in enumerate(plan(*refs[:nb])):
            cp = _remote(src, dst, send_sems, recv_sems, k, dev)
            cp.wait_send()
            cp.wait_recv()

    out = pl.pallas_call(
        body, name=name, out_shape=tuple(pltpu.HBM(b.shape, b.dtype) for b in bufs),
        in_specs=[HBM_SPEC] * nb + [SEM_SPEC, SEM_SPEC, pl.BlockSpec(memory_space=pl.ANY)],
        out_specs=tuple([HBM_SPEC] * nb), input_output_aliases={i: i for i in range(nb)},
        compiler_params=pltpu.CompilerParams(has_side_effects=EFFECT),
    )(*bufs, sems[0], sems[1], after)
    return list(out)


def _ag_plan_chips(shard_ref, out_ref):
    x, y, c, chips = _place()
    mine = out_ref.at[4 * x + 2 * y + c]
    return [(shard_ref, mine, (x, y, 1 - c))] + [(shard_ref, mine, (*chip, c)) for chip in chips]


def _ag_plan_pass(out_ref):
    x, y, c, chips = _place()
    slots = [out_ref.at[4 * chip[0] + 2 * chip[1] + c] for chip in chips]
    return [(s, s, (x, y, 1 - c)) for s in slots]


def _rs_plan_pair(g_ref, land_ref):
    x, y, c, _ = _place()
    return [(g_ref.at[1 - c], land_ref, (x, y, 1 - c))]


def _rs_plan_chips(p_ref, land_ref):
    x, y, c, chips = _place()
    return [(p_ref.at[2 * chip[0] + chip[1]], land_ref.at[2 * x + y], (*chip, c)) for chip in chips]


class _Gather:
    def __init__(self, shard, me, tag, after):
        self.tag = tag
        out = lax.dynamic_update_slice(lax.empty((N_DEV,) + shard.shape, shard.dtype), shard[None],
                                       (me,) + (0,) * shard.ndim)
        self.sems, (self.shard, self.out), self.token = _copy_start(
            "ag_start_" + tag, [shard, out], _ag_plan_chips, 4, after)

    def arrived_from_chips(self, after):
        _, out = _copy_wait("ag_wait_" + self.tag, self.sems, [self.shard, self.out], _ag_plan_chips, after)
        self.sems, (self.out,), _ = _copy_start("ag_pass_" + self.tag, [out], _ag_plan_pass, 3, after)

    def passed_on(self, after):
        return _copy_wait("ag_pass_wait_" + self.tag, self.sems, [self.out], _ag_plan_pass, after)[0]


class _ReduceScatter:
    def __init__(self, g8, tag):
        self.tag = tag
        land = lax.empty(g8.shape[1:], g8.dtype)
        self.sems, self.bufs, self.token = _copy_start("rs_pair_start_" + tag, [g8, land], _rs_plan_pair, 1, g8)

    def pair_done(self, core, chip, after, start_after=None):
        g8, land = _copy_wait("rs_pair_wait_" + self.tag, self.sems, self.bufs, _rs_plan_pair, after)
        p4 = _pair_sum(g8, land, core, "rs_pair_sum_" + self.tag)
        own = lax.dynamic_slice_in_dim(p4, chip, 1, axis=0)
        land2 = lax.dynamic_update_slice(lax.empty(p4.shape, p4.dtype), own, (chip, 0, 0))
        self.sems, self.bufs, self.token = _copy_start(
            "rs_chips_start_" + self.tag, [p4, land2], _rs_plan_chips, 3, p4 if start_after is None else start_after)

    def sums(self, after):
        return _copy_wait("rs_chips_wait_" + self.tag, self.sems, self.bufs, _rs_plan_chips, after)[1]


BIG = ("w_in", "w_o", "w_ffn_in", "w_ffn_out")
ORDER = ("w_ada", "b_ada", "w_in", "rel_bias", "attn_norm_g", "lb_logits", "gnorm_g", "w_o", "ln1_g", "ln1_b",
         "w_ffn_in", "w_ffn_out", "ln2_g", "ln2_b")


def kernel(x, c, w_ada, b_ada, w_in, rel_bias, attn_norm_g, lb_logits, gnorm_g, w_o, ln1_g, ln1_b, w_ffn_in, w_ffn_out, ln2_g, ln2_b, loss_target, m_w_ada, m_b_ada, m_w_in, m_rel_bias, m_attn_norm_g, m_lb_logits, m_gnorm_g, m_w_o, m_ln1_g, m_ln1_b, m_w_ffn_in, m_w_ffn_out, m_ln2_g, m_ln2_b, v_w_ada, v_b_ada, v_w_in, v_rel_bias, v_attn_norm_g, v_lb_logits, v_gnorm_g, v_w_o, v_ln1_g, v_ln1_b, v_w_ffn_in, v_w_ffn_out, v_ln2_g, v_ln2_b):
    W = dict(w_ada=w_ada, b_ada=b_ada, w_in=w_in, rel_bias=rel_bias, attn_norm_g=attn_norm_g, lb_logits=lb_logits,
             gnorm_g=gnorm_g, w_o=w_o, ln1_g=ln1_g, ln1_b=ln1_b, w_ffn_in=w_ffn_in, w_ffn_out=w_ffn_out,
             ln2_g=ln2_g, ln2_b=ln2_b)
    M = dict(w_ada=m_w_ada, b_ada=m_b_ada, w_in=m_w_in, rel_bias=m_rel_bias, attn_norm_g=m_attn_norm_g,
             lb_logits=m_lb_logits, gnorm_g=m_gnorm_g, w_o=m_w_o, ln1_g=m_ln1_g, ln1_b=m_ln1_b,
             w_ffn_in=m_w_ffn_in, w_ffn_out=m_w_ffn_out, ln2_g=m_ln2_g, ln2_b=m_ln2_b)
    V = dict(w_ada=v_w_ada, b_ada=v_b_ada, w_in=v_w_in, rel_bias=v_rel_bias, attn_norm_g=v_attn_norm_g,
             lb_logits=v_lb_logits, gnorm_g=v_gnorm_g, w_o=v_w_o, ln1_g=v_ln1_g, ln1_b=v_ln1_b,
             w_ffn_in=v_w_ffn_in, w_ffn_out=v_w_ffn_out, ln2_g=v_ln2_g, ln2_b=v_ln2_b)

    x2, tgt = x[0], loss_target[0]
    T, D = x2.shape
    AW, RW = attn_norm_g.shape[-1], lb_logits.shape[-1]
    MIX = AW + RW
    H, RH = AW // ATTN_HEAD_DIM, RW // LANE
    RB = rel_bias.shape[-1]
    max_rel = (RB - 1) // 2
    rbp = -(-RB // LANE) * LANE
    F = w_ffn_out.shape[1] * N_DEV
    half = N_DEV // 2
    xi, yi, ci = lax.axis_index("x"), lax.axis_index("y"), lax.axis_index("c")
    me = 4 * xi + 2 * yi + ci
    core = jnp.reshape(ci, (1,)).astype(jnp.int32)
    pad_rb = lambda a: jnp.pad(a[0], ((0, 0), (0, rbp - RB)))

    chip = 2 * xi + yi

    c_act, lbv, gv = _prep(c, lb_logits, pad_rb(rel_bias), max_rel)
    c_all = _all_gather(c_act, "ag_c").reshape(N_DEV, D)
    ns_ada = w_ada.shape[-1]
    mod_part = _mod_part(c_all, w_ada[0], lax.dynamic_slice_in_dim(b_ada, me * ns_ada, ns_ada, axis=1))
    mod_all = _all_gather(mod_part, "ag_mod")
    mod6 = lax.dynamic_index_in_dim(mod_all, me, axis=1, keepdims=False).reshape(6, D)

    ag_in = _Gather(w_in[0].astype(BF16), me, "w_in", mod_all)
    ag_o = _Gather(w_o[0].astype(BF16), me, "w_o", ag_in.token)
    ag_f1 = _Gather(w_ffn_in[0].astype(BF16), me, "w_ffn_in", ag_o.token)
    ag_f2 = _Gather(w_ffn_out[0].astype(BF16), me, "w_ffn_out", ag_f1.token)

    h1 = _ln_mod(x2, mod6 + ag_f2.token[0, 0])
    ag_in.arrived_from_chips(h1)
    wg_in = ag_in.passed_on(h1)
    proj = _mm_gathered(h1, wg_in, "in_proj")
    ag_o.arrived_from_chips(proj)
    mix_a = _attn_fwd(proj, gv, attn_norm_g, AW)
    wg_o = ag_o.passed_on(mix_a).reshape(MIX, D)
    mix_b, o_b, st_all = _hgrn_fwd(proj, lbv, gnorm_g, AW, RW)
    ag_f1.arrived_from_chips(mix_b)
    mixin = jnp.concatenate([mix_a, mix_b], axis=1)
    mix = _mm_nn(mixin, wg_o, "out_proj")
    x1, h2 = _mid_fwd(x2, mix, mod6, ln1_g, ln1_b)
    wg_f1 = ag_f1.passed_on(h2)
    gu, act = _mm_swiglu(h2, wg_f1)
    ag_f2.arrived_from_chips(act)
    wg_f2 = ag_f2.passed_on(act).reshape(F, D)
    ff = _mm_nn(act, wg_f2, "ffn_out")
    dff, dx1a, vec_a = _final(x1, ff, mod6, ln2_g, ln2_b, tgt)

    du = _mm_swiglu_bwd(dff, wg_f2, gu)
    rs_f2 = _ReduceScatter(_mm_tn_rows(dff, act, dff, F // N_DEV, "grad_w_ffn_out"), "w_ffn_out")
    tm = _tile(T, 512, 16)
    du_ij = lambda tm_, ns: pl.BlockSpec((None, tm_, ns), lambda i, j: (j // half, i, j % half))
    du_jm = lambda tm_, ns: pl.BlockSpec((None, tm_, ns), lambda j, m: (j // half, m, j % half))
    dh2 = _mm_gathered_nt(rs_f2.token, du, du_ij, wg_f1, T, tm, "ffn_in_bwd")
    rs_f2.pair_done(core, chip, dh2)
    gw_f1 = _mm_tn_gathered(rs_f2.token, h2, du, du_jm, wg_f1.shape[-1], tm, "grad_w_ffn_in")
    rs_f1 = _ReduceScatter(gw_f1.reshape(2, half, D, -1), "w_ffn_in")
    dmix, dxa, vec_b = _mid_bwd(x2, mix, x1, dx1a, dh2, mod6 + rs_f1.token[0, 0], ln1_g)
    dmixin = _mm_nt(dmix, wg_o, "out_proj_bwd")
    rs_f1.pair_done(core, chip, dmixin)
    rs_o = _ReduceScatter(_mm_tn_rows(rs_f1.token, mixin, dmix, MIX // N_DEV, "grad_w_o"), "w_o")
    dq, dk, dv, dgv, dga = _attn_bwd(proj, dmixin, gv + rs_o.token[0, 0], attn_norm_g, AW)
    rs_o.pair_done(core, chip, dq)
    dqb, dfl, dib, dgb, dlb, dgn = _hgrn_bwd(proj, dmixin, o_b, st_all, lbv + rs_o.token[0, 0], gnorm_g, AW, RW)
    dproj = jnp.concatenate([dq, dk, dv, dqb, dfl, dib, dgb], axis=1)
    p_ij = lambda tm_, ns: pl.BlockSpec((tm_, ns), lambda i, j: (i, j))
    p_jm = lambda tm_, ns: pl.BlockSpec((tm_, ns), lambda j, m: (m, j))
    gw_in = _mm_tn_gathered(rs_o.token, h1, dproj, p_jm, wg_in.shape[-1], tm, "grad_w_in")
    rs_in = _ReduceScatter(gw_in.reshape(2, half, D, -1), "w_in")
    dh1 = _mm_gathered_nt(rs_in.token, dproj, p_ij, wg_in, T, tm, "in_proj_bwd")
    grad_x, vec_c = _first_bwd(x2, dh1, dxa, mod6)

    dmod = jnp.concatenate([vec_c[1:2], vec_c[0:1], vec_b[4:5], vec_b[1:2], vec_b[0:1], vec_a[2:3]], axis=0)
    pieces = dict(b_ada=dmod, rel_bias=dgv, attn_norm_g=dga, lb_logits=dlb, gnorm_g=dgn, ln1_g=vec_b[2:3],
                  ln1_b=vec_b[3:4], ln2_g=vec_a[0:1], ln2_b=vec_a[1:2], loss=vec_a[3:4])
    widths = dict(b_ada=(1, 6 * D), rel_bias=(H, TAB), attn_norm_g=(1, AW), lb_logits=(1, RW), gnorm_g=(RH, LANE),
                  ln1_g=(1, D), ln1_b=(1, D), ln2_g=(1, D), ln2_b=(1, D), loss=(1, D))
    packed = jnp.concatenate([pieces[k].reshape(-1, LANE) for k in widths], axis=0)
    gathered = _all_gather(packed, "ag_small")
    rs_in.pair_done(core, chip, dh1, start_after=gathered)
    parts, r0 = {}, 0
    for k, (rows, width) in widths.items():
        nr = rows * width // LANE
        parts[k] = gathered[:, r0:r0 + nr, :].reshape(N_DEV, rows, width)
        r0 += nr
    prep_small = lambda d, k: pad_rb(d[k]) if k == "rel_bias" else d[k]
    small = _small_update(parts, parts["loss"], lbv, [prep_small(W, k) for k in SMALL],
                          [prep_small(M, k) for k in SMALL], [prep_small(V, k) for k in SMALL], max_rel)
    loss = small[0].reshape(())
    res = {}
    for idx, k in enumerate(SMALL):
        four = small[1 + 4 * idx:5 + 4 * idx]
        if k == "rel_bias":
            four = [a[:, :RB][None] for a in four]
        res[k] = list(four)

    dmod_s = lax.dynamic_slice_in_dim(parts["b_ada"].reshape(N_DEV, 6 * D), me * ns_ada, ns_ada, axis=1)
    res["w_ada"] = [a[None] for a in _adam_ada(c_all, dmod_s, w_ada[0], m_w_ada[0], v_w_ada[0])]
    after = res["w_ada"][0]
    for k, rs in (("w_ffn_out", rs_f2), ("w_ffn_in", rs_f1), ("w_o", rs_o), ("w_in", rs_in)):
        four = _adam_shard(rs.sums(after), W[k][0], M[k][0], V[k][0], "adam_" + k)
        res[k] = [a[None] for a in four]
        after = four[0]

    out = [loss, grad_x[None]]
    for field in range(4):
        out += [res[k][field] for k in ORDER]
    return tuple(out)
```

```python
import functools

import jax
import jax.numpy as jnp
from jax import lax
from jax.experimental import pallas as pl
from jax.experimental.pallas import tpu as pltpu

F32 = jnp.float32
BF16 = jnp.bfloat16
MESH = pl.DeviceIdType.MESH
HIGHEST = lax.Precision.HIGHEST

N_DEV = 8
CHUNK = 64
N_PAST = 8
QBLK = 4 * CHUNK
KPAD = N_PAST * CHUNK
WIN = KPAD + QBLK
TAB = 1024
ATTN_HEAD_DIM = 64
REC_HEAD_DIM = 128
SUB = 16
LANE = 128
EPS = 1e-5
ALPHA = 2.0 ** 0.25
ADAM_LR, ADAM_B1, ADAM_B2, ADAM_EPS, ADAM_WD, ADAM_STEP = 0.001, 0.9, 0.999, 1e-08, 0.01, 10
NEG = -1e30
VMEM_LIMIT = 56 * 1024 * 1024


def _sds(shape, dtype):
    return jax.ShapeDtypeStruct(tuple(shape), dtype)


def _tile(n, pref, mult):
    best = None
    for t in range(mult, min(n, pref) + 1, mult):
        if n % t == 0:
            best = t
    return n if best is None else best


def _params(sem=None, big=False):
    kw = {}
    if sem is not None:
        kw["dimension_semantics"] = sem
    if big:
        kw["vmem_limit_bytes"] = VMEM_LIMIT
    return pltpu.CompilerParams(**kw)


def _sigmoid(v):
    return 1.0 / (1.0 + jnp.exp(-v))


def _dot(a, b, dims, precision=None):
    return lax.dot_general(a, b, (dims, ((), ())), preferred_element_type=F32, precision=precision)


NN = ((1,), (0,))
NT = ((1,), (1,))
TN = ((0,), (0,))


def _ln(v):
    mu = jnp.mean(v, axis=-1, keepdims=True)
    d = v - mu
    rstd = lax.rsqrt(jnp.mean(d * d, axis=-1, keepdims=True) + EPS)
    return d * rstd, rstd


def _ln_bwd(dxh, xh, rstd):
    return rstd * (dxh - jnp.mean(dxh, axis=-1, keepdims=True) - xh * jnp.mean(dxh * xh, axis=-1, keepdims=True))


def _colsum(v):
    return jnp.sum(v, axis=0, keepdims=True)


def _ln_mod(x2, mod6):
    T, D = x2.shape
    tm = _tile(T, 256, 8)

    def body(x_ref, mod_ref, o_ref):
        xh, _ = _ln(x_ref[...])
        o_ref[...] = (xh * (1.0 + mod_ref[1:2, :]) + mod_ref[0:1, :]).astype(BF16)

    return pl.pallas_call(
        body, grid=(T // tm,), name="ln_mod",
        in_specs=[pl.BlockSpec((tm, D), lambda i: (i, 0)), pl.BlockSpec((6, D), lambda i: (0, 0))],
        out_specs=pl.BlockSpec((tm, D), lambda i: (i, 0)),
        out_shape=_sds((T, D), BF16), compiler_params=_params(("parallel",)),
    )(x2, mod6)


def _mid_fwd(x2, mix, mod6, ln1_g, ln1_b):
    T, D = x2.shape
    tm = _tile(T, 256, 8)

    def body(x_ref, mix_ref, mod_ref, g_ref, b_ref, x1_ref, h2_ref):
        zh, _ = _ln(ALPHA * x_ref[...] + mod_ref[2:3, :] * mix_ref[...])
        x1 = zh * g_ref[...] + b_ref[...]
        x1_ref[...] = x1
        xh, _ = _ln(x1)
        h2_ref[...] = (xh * (1.0 + mod_ref[4:5, :]) + mod_ref[3:4, :]).astype(BF16)

    row = pl.BlockSpec((tm, D), lambda i: (i, 0))
    vec = pl.BlockSpec((1, D), lambda i: (0, 0))
    return pl.pallas_call(
        body, grid=(T // tm,), name="mid_fwd",
        in_specs=[row, row, pl.BlockSpec((6, D), lambda i: (0, 0)), vec, vec],
        out_specs=[row, row],
        out_shape=[_sds((T, D), F32), _sds((T, D), BF16)], compiler_params=_params(("parallel",)),
    )(x2, mix, mod6, ln1_g, ln1_b)


def _final(x1, ff, mod6, ln2_g, ln2_b, tgt):
    T, D = x1.shape
    tm = _tile(T, 256, 8)

    def body(x1_ref, ff_ref, mod_ref, g_ref, b_ref, t_ref, dff_ref, dx1_ref, vec_ref):
        @pl.when(pl.program_id(0) == 0)
        def _():
            vec_ref[...] = jnp.zeros_like(vec_ref)

        ff_v = ff_ref[...]
        gate2 = mod_ref[5:6, :]
        zh, rstd = _ln(ALPHA * x1_ref[...] + gate2 * ff_v)
        err = zh * g_ref[...] + b_ref[...] - t_ref[...]
        dy = err * (1.0 / D)
        dz = _ln_bwd(dy * g_ref[...], zh, rstd)
        dff_ref[...] = (gate2 * dz).astype(BF16)
        dx1_ref[...] = ALPHA * dz
        vec_ref[0:1, :] += _colsum(dy * zh)
        vec_ref[1:2, :] += _colsum(dy)
        vec_ref[2:3, :] += _colsum(dz * ff_v)
        vec_ref[3:4, :] += _colsum(err * err) * (0.5 / D)

    row = pl.BlockSpec((tm, D), lambda i: (i, 0))
    vec = pl.BlockSpec((1, D), lambda i: (0, 0))
    return pl.pallas_call(
        body, grid=(T // tm,), name="final_fwd_bwd",
        in_specs=[row, row, pl.BlockSpec((6, D), lambda i: (0, 0)), vec, vec, row],
        out_specs=[row, row, pl.BlockSpec((8, D), lambda i: (0, 0))],
        out_shape=[_sds((T, D), BF16), _sds((T, D), F32), _sds((8, D), F32)],
        compiler_params=_params(("arbitrary",)),
    )(x1, ff, mod6, ln2_g, ln2_b, tgt)


def _mid_bwd(x2, mix, x1, dx1a, dh2, mod6, ln1_g):
    T, D = x2.shape
    tm = _tile(T, 256, 8)

    def body(x_ref, mix_ref, x1_ref, dx1a_ref, dh2_ref, mod_ref, g_ref, dmix_ref, dxa_ref, vec_ref):
        @pl.when(pl.program_id(0) == 0)
        def _():
            vec_ref[...] = jnp.zeros_like(vec_ref)

        dh2 = dh2_ref[...]
        xh, rstd = _ln(x1_ref[...])
        dx1 = dx1a_ref[...] + _ln_bwd(dh2 * (1.0 + mod_ref[4:5, :]), xh, rstd)
        mix_v = mix_ref[...]
        gate1 = mod_ref[2:3, :]
        zh, rstdz = _ln(ALPHA * x_ref[...] + gate1 * mix_v)
        dz = _ln_bwd(dx1 * g_ref[...], zh, rstdz)
        dmix_ref[...] = (gate1 * dz).astype(BF16)
        dxa_ref[...] = ALPHA * dz
        vec_ref[0:1, :] += _colsum(dh2 * xh)
        vec_ref[1:2, :] += _colsum(dh2)
        vec_ref[2:3, :] += _colsum(dx1 * zh)
        vec_ref[3:4, :] += _colsum(dx1)
        vec_ref[4:5, :] += _colsum(dz * mix_v)

    row = pl.BlockSpec((tm, D), lambda i: (i, 0))
    vec = pl.BlockSpec((1, D), lambda i: (0, 0))
    return pl.pallas_call(
        body, grid=(T // tm,), name="mid_bwd",
        in_specs=[row, row, row, row, row, pl.BlockSpec((6, D), lambda i: (0, 0)), vec],
        out_specs=[row, row, pl.BlockSpec((8, D), lambda i: (0, 0))],
        out_shape=[_sds((T, D), BF16), _sds((T, D), F32), _sds((8, D), F32)],
        compiler_params=_params(("arbitrary",)),
    )(x2, mix, x1, dx1a, dh2, mod6, ln1_g)


def _first_bwd(x2, dh1, dxa, mod6):
    T, D = x2.shape
    tm = _tile(T, 256, 8)

    def body(x_ref, dh1_ref, dxa_ref, mod_ref, gx_ref, vec_ref):
        @pl.when(pl.program_id(0) == 0)
        def _():
            vec_ref[...] = jnp.zeros_like(vec_ref)

        dh1 = dh1_ref[...]
        xh, rstd = _ln(x_ref[...])
        gx_ref[...] = dxa_ref[...] + _ln_bwd(dh1 * (1.0 + mod_ref[1:2, :]), xh, rstd)
        vec_ref[0:1, :] += _colsum(dh1 * xh)
        vec_ref[1:2, :] += _colsum(dh1)

    row = pl.BlockSpec((tm, D), lambda i: (i, 0))
    return pl.pallas_call(
        body, grid=(T // tm,), name="first_bwd",
        in_specs=[row, row, row, pl.BlockSpec((6, D), lambda i: (0, 0))],
        out_specs=[row, pl.BlockSpec((8, D), lambda i: (0, 0))],
        out_shape=[_sds((T, D), F32), _sds((8, D), F32)],
        compiler_params=_params(("arbitrary",)),
    )(x2, dh1, dxa, mod6)


def _slot(j):
    return (j % 2) * 4 + j // 2


def _mm_gathered(a, wg, name):
    M, K = a.shape
    _, _, ns = wg.shape
    tm = _tile(M, 512, 16)

    def body(a_ref, w_ref, o_ref):
        o_ref[...] = _dot(a_ref[...], w_ref[...], NN)

    return pl.pallas_call(
        body, grid=(N_DEV, M // tm), name=name,
        in_specs=[pl.BlockSpec((tm, K), lambda j, i: (i, 0)), pl.BlockSpec((None, K, ns), lambda j, i: (j, 0, 0))],
        out_specs=pl.BlockSpec((tm, ns), lambda j, i: (i, j)),
        out_shape=_sds((M, N_DEV * ns), F32), compiler_params=_params(("parallel", "parallel"), big=True),
    )(a, wg)


def _mm_nn(a, b, name):
    M, K = a.shape
    _, N = b.shape
    tm, tn, tk = _tile(M, 512, 16), _tile(N, 1024, LANE), _tile(K, 2048, LANE)

    def body(a_ref, b_ref, o_ref):
        @pl.when(pl.program_id(2) == 0)
        def _():
            o_ref[...] = jnp.zeros_like(o_ref)

        o_ref[...] += _dot(a_ref[...], b_ref[...], NN)

    return pl.pallas_call(
        body, grid=(M // tm, N // tn, K // tk), name=name,
        in_specs=[pl.BlockSpec((tm, tk), lambda i, j, k: (i, k)), pl.BlockSpec((tk, tn), lambda i, j, k: (k, j))],
        out_specs=pl.BlockSpec((tm, tn), lambda i, j, k: (i, j)),
        out_shape=_sds((M, N), F32), compiler_params=_params(("parallel", "parallel", "arbitrary"), big=True),
    )(a, b)


def _mm_nt(a, b, name):
    M, K = a.shape
    N, _ = b.shape
    tm, tn = _tile(M, 512, 16), _tile(N, 1024, LANE)

    def body(a_ref, b_ref, o_ref):
        o_ref[...] = _dot(a_ref[...], b_ref[...], NT)

    return pl.pallas_call(
        body, grid=(M // tm, N // tn), name=name,
        in_specs=[pl.BlockSpec((tm, K), lambda i, j: (i, 0)), pl.BlockSpec((tn, K), lambda i, j: (j, 0))],
        out_specs=pl.BlockSpec((tm, tn), lambda i, j: (i, j)),
        out_shape=_sds((M, N), F32), compiler_params=_params(("parallel", "parallel"), big=True),
    )(a, b)


def _mm_swiglu(h2, wg):
    M, K = h2.shape
    _, _, ns = wg.shape
    half = N_DEV // 2
    tm = _tile(M, 256, 16)

    def body(a_ref, wgate_ref, wup_ref, gu_ref, act_ref):
        a = a_ref[...]
        g = _dot(a, wgate_ref[...], NN)
        u = _dot(a, wup_ref[...], NN)
        gu_ref[0] = g
        gu_ref[1] = u
        act_ref[...] = (g * _sigmoid(g) * u).astype(BF16)

    return pl.pallas_call(
        body, grid=(half, M // tm), name="ffn_in_swiglu",
        in_specs=[pl.BlockSpec((tm, K), lambda j, i: (i, 0)),
                  pl.BlockSpec((None, K, ns), lambda j, i: (j, 0, 0)),
                  pl.BlockSpec((None, K, ns), lambda j, i: (j + half, 0, 0))],
        out_specs=[pl.BlockSpec((2, tm, ns), lambda j, i: (0, i, j)), pl.BlockSpec((tm, ns), lambda j, i: (i, j))],
        out_shape=[_sds((2, M, half * ns), F32), _sds((M, half * ns), BF16)],
        compiler_params=_params(("parallel", "parallel"), big=True),
    )(h2, wg, wg)


def _mm_swiglu_bwd(dff, w2, gu):
    M, K = dff.shape
    F = w2.shape[0]
    tm, tn = _tile(M, 512, 16), _tile(F, 1408, LANE)

    def body(a_ref, b_ref, gu_ref, du_ref):
        da = _dot(a_ref[...], b_ref[...], NT)
        g = gu_ref[0]
        u = gu_ref[1]
        sg = _sigmoid(g)
        du_ref[0] = (da * u * (sg * (1.0 + g * (1.0 - sg)))).astype(BF16)
        du_ref[1] = (da * (g * sg)).astype(BF16)

    return pl.pallas_call(
        body, grid=(F // tn, M // tm), name="ffn_out_bwd_swiglu",
        in_specs=[pl.BlockSpec((tm, K), lambda j, i: (i, 0)), pl.BlockSpec((tn, K), lambda j, i: (j, 0)),
                  pl.BlockSpec((2, tm, tn), lambda j, i: (0, i, j))],
        out_specs=pl.BlockSpec((2, tm, tn), lambda j, i: (0, i, j)),
        out_shape=_sds((2, M, F), BF16), compiler_params=_params(("parallel", "parallel"), big=True),
    )(dff, w2, gu)


ORDER_ONLY = pl.BlockSpec(memory_space=pl.ANY)


def _mm_tn_rows(dep, a, b, rs, name):
    M, Ka = a.shape
    _, N = b.shape
    tm = _tile(M, 1024, 16)

    def body(_, a_ref, b_ref, o_ref, acc_ref):
        m = pl.program_id(1)

        @pl.when(m == 0)
        def _():
            acc_ref[...] = jnp.zeros_like(acc_ref)

        acc_ref[...] += _dot(a_ref[...], b_ref[...], TN)

        @pl.when(m == pl.num_programs(1) - 1)
        def _():
            o_ref[0, 0] = acc_ref[0:rs, :].astype(BF16)
            o_ref[1, 0] = acc_ref[rs:2 * rs, :].astype(BF16)

    return pl.pallas_call(
        body, grid=(N_DEV // 2, M // tm), name=name,
        in_specs=[ORDER_ONLY, pl.BlockSpec((tm, 2 * rs), lambda ch, m: (m, ch)),
                  pl.BlockSpec((tm, N), lambda ch, m: (m, 0))],
        out_specs=pl.BlockSpec((2, 1, rs, N), lambda ch, m: (0, ch, 0, 0)),
        out_shape=_sds((2, N_DEV // 2, rs, N), BF16),
        scratch_shapes=[pltpu.VMEM((2 * rs, N), F32)],
        compiler_params=_params(("parallel", "arbitrary"), big=True),
    )(dep, a, b)


def _mm_gathered_nt(dep, a, a_spec, wg, M, tm, name):
    _, K, ns = wg.shape

    def body(_, a_ref, w_ref, o_ref):
        @pl.when(pl.program_id(1) == 0)
        def _():
            o_ref[...] = jnp.zeros_like(o_ref)

        o_ref[...] += _dot(a_ref[...], w_ref[...], NT)

    return pl.pallas_call(
        body, grid=(M // tm, N_DEV), name=name,
        in_specs=[ORDER_ONLY, a_spec(tm, ns), pl.BlockSpec((None, K, ns), lambda i, j: (j, 0, 0))],
        out_specs=pl.BlockSpec((tm, K), lambda i, j: (i, 0)),
        out_shape=_sds((M, K), F32), compiler_params=_params(("parallel", "arbitrary"), big=True),
    )(dep, a, wg)


def _mm_tn_gathered(dep, h, a, a_spec, ns, tm, name):
    M, K = h.shape

    def body(_, h_ref, a_ref, o_ref, acc_ref):
        m = pl.program_id(1)

        @pl.when(m == 0)
        def _():
            acc_ref[...] = jnp.zeros_like(acc_ref)

        acc_ref[...] += _dot(h_ref[...], a_ref[...], TN)

        @pl.when(m == pl.num_programs(1) - 1)
        def _():
            o_ref[...] = acc_ref[...].astype(BF16)

    return pl.pallas_call(
        body, grid=(N_DEV, M // tm), name=name,
        in_specs=[ORDER_ONLY, pl.BlockSpec((tm, K), lambda j, m: (m, 0)), a_spec(tm, ns)],
        out_specs=pl.BlockSpec((None, K, ns), lambda j, m: (_slot(j), 0, 0)),
        out_shape=_sds((N_DEV, K, ns), BF16),
        scratch_shapes=[pltpu.VMEM((K, ns), F32)],
        compiler_params=_params(("parallel", "arbitrary"), big=True),
    )(dep, h, a)


def _bias_onehot(rbp, max_rel):
    r = lax.broadcasted_iota(jnp.int32, (rbp, TAB), 0)
    m = lax.broadcasted_iota(jnp.int32, (rbp, TAB), 1)
    dist = KPAD - jnp.where(m < WIN, m, m - TAB)
    return (r == jnp.clip(dist, -max_rel, max_rel) + max_rel).astype(F32)


def _attn_setup(i, hp, k_ref, v_ref, gv_ref, kpad, vpad, bias):
    ls = slice(i * ATTN_HEAD_DIM, (i + 1) * ATTN_HEAD_DIM)
    kpad[i, 0:KPAD, :] = jnp.zeros((KPAD, ATTN_HEAD_DIM), BF16)
    vpad[i, 0:KPAD, :] = jnp.zeros((KPAD, ATTN_HEAD_DIM), BF16)
    kpad[i, KPAD:, :] = k_ref[:, ls].astype(BF16)
    vpad[i, KPAD:, :] = v_ref[:, ls].astype(BF16)
    gvrow = gv_ref[pl.ds(hp * 2 + i, 1), :]
    tab = pltpu.roll(jnp.broadcast_to(gvrow, (QBLK, TAB)), 0, 1, stride=1, stride_axis=0)
    row = lax.broadcasted_iota(jnp.int32, (QBLK, WIN), 0)
    col = lax.broadcasted_iota(jnp.int32, (QBLK, WIN), 1)
    first = jnp.bitwise_and(row, -CHUNK)
    seen = jnp.logical_and(col >= first, col < first + (N_PAST + 1) * CHUNK)
    bias[i] = jnp.where(seen, tab[:, 0:WIN], NEG)


def _attn_probs(b, i, q_ref, kpad, vpad, bias, col):
    ls = slice(i * ATTN_HEAD_DIM, (i + 1) * ATTN_HEAD_DIM)
    r0 = pl.multiple_of(b * QBLK, QBLK)
    q = q_ref[pl.ds(r0, QBLK), ls].astype(BF16)
    kw = kpad[i, pl.ds(r0, WIN), :]
    vw = vpad[i, pl.ds(r0, WIN), :]
    s = _dot(q, kw, NT) * (ATTN_HEAD_DIM ** -0.5) + bias[i]
    s = jnp.where(col >= KPAD - r0, s, NEG)
    p = jnp.exp(s - jnp.max(s, axis=-1, keepdims=True))
    pn = p / jnp.sum(p, axis=-1, keepdims=True)
    return r0, ls, q, kw, vw, pn


def _attn_fwd(proj, gv, ga, AW):
    T = proj.shape[0]
    HP = AW // LANE

    def body(q_ref, k_ref, v_ref, gv_ref, ga_ref, o_ref, kpad, vpad, bias):
        hp = pl.program_id(0)
        for i in range(2):
            _attn_setup(i, hp, k_ref, v_ref, gv_ref, kpad, vpad, bias)
        col = lax.broadcasted_iota(jnp.int32, (QBLK, WIN), 1)

        def block(b, carry):
            for i in range(2):
                r0, ls, _, _, vw, pn = _attn_probs(b, i, q_ref, kpad, vpad, bias, col)
                o = _dot(pn.astype(BF16), vw, NN)
                r = lax.rsqrt(jnp.mean(o * o, axis=-1, keepdims=True) + EPS)
                o_ref[pl.ds(r0, QBLK), ls] = (o * r * ga_ref[0:1, ls]).astype(BF16)
            return carry

        lax.fori_loop(0, T // QBLK, block, 0)

    blk = lambda off: pl.BlockSpec((T, LANE), lambda hp: (0, off + hp))
    return pl.pallas_call(
        body, grid=(HP,), name="attn_fwd",
        in_specs=[blk(0), blk(HP), blk(2 * HP), pl.BlockSpec(gv.shape, lambda hp: (0, 0)),
                  pl.BlockSpec((1, LANE), lambda hp: (0, hp))],
        out_specs=pl.BlockSpec((T, LANE), lambda hp: (0, hp)),
        out_shape=_sds((T, AW), BF16),
        scratch_shapes=[pltpu.VMEM((2, T + KPAD, ATTN_HEAD_DIM), BF16), pltpu.VMEM((2, T + KPAD, ATTN_HEAD_DIM), BF16),
                        pltpu.VMEM((2, QBLK, WIN), F32)],
        compiler_params=_params(("parallel",), big=True),
    )(proj, proj, proj, gv, ga)


def _attn_bwd(proj, dmixin, gv, ga, AW):
    T = proj.shape[0]
    HP = AW // LANE
    scale = ATTN_HEAD_DIM ** -0.5

    def body(q_ref, k_ref, v_ref, dn_ref, gv_ref, ga_ref, dq_ref, dk_ref, dv_ref, dgv_ref, dga_ref,
             kpad, vpad, dkacc, dvacc, bias, dbias):
        hp = pl.program_id(0)
        for i in range(2):
            _attn_setup(i, hp, k_ref, v_ref, gv_ref, kpad, vpad, bias)
        dkacc[...] = jnp.zeros_like(dkacc)
        dvacc[...] = jnp.zeros_like(dvacc)
        dbias[...] = jnp.zeros_like(dbias)
        dga_ref[...] = jnp.zeros_like(dga_ref)
        col = lax.broadcasted_iota(jnp.int32, (QBLK, WIN), 1)

        def block(b, carry):
            for i in range(2):
                r0, ls, q, kw, vw, pn = _attn_probs(b, i, q_ref, kpad, vpad, bias, col)
                pn_b = pn.astype(BF16)
                o = _dot(pn_b, vw, NN)
                r = lax.rsqrt(jnp.mean(o * o, axis=-1, keepdims=True) + EPS)
                dn = dn_ref[pl.ds(r0, QBLK), ls]
                dga_ref[i:i + 1, :] += _colsum(dn * o * r)
                a = dn * ga_ref[0:1, ls]
                do = r * (a - o * (r * r) * jnp.mean(a * o, axis=-1, keepdims=True))
                do_b = do.astype(BF16)
                dp = _dot(do_b, vw, NT)
                dvacc[i, pl.ds(r0, WIN), :] += _dot(pn_b, do_b, TN)
                ds = pn * (dp - jnp.sum(pn * dp, axis=-1, keepdims=True))
                dbias[i] += ds
                ds_b = ds.astype(BF16)
                dq_ref[pl.ds(r0, QBLK), ls] = (_dot(ds_b, kw, NN) * scale).astype(BF16)
                dkacc[i, pl.ds(r0, WIN), :] += _dot(ds_b, q, TN) * scale
            return carry

        lax.fori_loop(0, T // QBLK, block, 0)

        rr = lax.broadcasted_iota(jnp.int32, (QBLK, QBLK), 0)
        cc = lax.broadcasted_iota(jnp.int32, (QBLK, QBLK), 1)
        flip = (rr + cc == QBLK - 1).astype(BF16)
        for i in range(2):
            ls = slice(i * ATTN_HEAD_DIM, (i + 1) * ATTN_HEAD_DIM)
            dk_ref[:, ls] = dkacc[i, KPAD:, :].astype(BF16)
            dv_ref[:, ls] = dvacc[i, KPAD:, :].astype(BF16)
            full = jnp.concatenate([dbias[i], jnp.zeros((QBLK, TAB - WIN), F32)], axis=1)
            hi = full.astype(BF16)
            lo = (full - hi.astype(F32)).astype(BF16)
            rev = _dot(flip, hi, NN) + _dot(flip, lo, NN)
            dgv_ref[i:i + 1, :] = _colsum(pltpu.roll(rev, TAB - (QBLK - 1), 1, stride=1, stride_axis=0))

    blk = lambda off: pl.BlockSpec((T, LANE), lambda hp: (0, off + hp))
    accs = lambda dt: pltpu.VMEM((2, T + KPAD, ATTN_HEAD_DIM), dt)
    return pl.pallas_call(
        body, grid=(HP,), name="attn_bwd",
        in_specs=[blk(0), blk(HP), blk(2 * HP), blk(0), pl.BlockSpec(gv.shape, lambda hp: (0, 0)),
                  pl.BlockSpec((1, LANE), lambda hp: (0, hp))],
        out_specs=[blk(0), blk(0), blk(0), pl.BlockSpec((None, 2, TAB), lambda hp: (hp, 0, 0)),
                   pl.BlockSpec((None, 2, ATTN_HEAD_DIM), lambda hp: (hp, 0, 0))],
        out_shape=[_sds((T, AW), BF16), _sds((T, AW), BF16), _sds((T, AW), BF16),
                   _sds((HP, 2, TAB), F32), _sds((HP, 2, ATTN_HEAD_DIM), F32)],
        scratch_shapes=[accs(BF16), accs(BF16), accs(F32), accs(F32),
                        pltpu.VMEM((2, QBLK, WIN), F32), pltpu.VMEM((2, QBLK, WIN), F32)],
        compiler_params=_params(("parallel",), big=True),
    )(proj, proj, proj, dmixin, gv, ga)


def _ltri():
    r = lax.broadcasted_iota(jnp.int32, (CHUNK, CHUNK), 0)
    c = lax.broadcasted_iota(jnp.int32, (CHUNK, CHUNK), 1)
    return (c <= r).astype(F32)


HEADS_PER_STEP = 2


def _hgrn_gates(n, ls, q_ref, f_ref, lb_ref, ltri):
    r0 = pl.multiple_of(n * CHUNK, CHUNK)
    rows = pl.ds(r0, CHUNK)
    lb = lb_ref[:, ls]
    qb = q_ref[rows, ls]
    sg = _sigmoid(f_ref[rows, ls])
    f = lb + (1.0 - lb) * sg
    sq = _sigmoid(qb)
    b = _dot(ltri, jnp.log(f), NN, HIGHEST)
    return rows, lb, qb, sg, f, 1.0 - f, sq, qb * sq, b


def _hgrn_specs(T, RW, AW):
    HG = HEADS_PER_STEP
    W = HG * LANE
    base = 3 * AW // W
    blk_in = lambda off: pl.BlockSpec((T, W), lambda g: (0, base + off + g))
    col = pl.BlockSpec((T, W), lambda g: (0, g))
    return HG, W, RW // W, blk_in, col


def _hgrn_fwd(proj, lb, gn, AW, RW):
    T = proj.shape[0]
    RH, NC, NSUB = RW // LANE, T // CHUNK, CHUNK // SUB
    HG, W, NG, blk_in, col = _hgrn_specs(T, RW, AW)

    def body(q_ref, f_ref, i_ref, g_ref, lb_ref, gn_ref, mix_ref, o_ref, stall_ref, st_all, bs_all, kks_all, ics_all):
        st_all[...] = jnp.zeros_like(st_all)
        ltri = _ltri()
        rowi = lax.broadcasted_iota(jnp.int32, (SUB, 1), 0)

        def one_head(h, n):
            ls = slice(h * LANE, (h + 1) * LANE)
            st, bs, kks, ics = st_all.at[h], bs_all.at[h], kks_all.at[h], ics_all.at[h]
            rows, _, _, _, _, kk, _, qs, b = _hgrn_gates(n, ls, q_ref, f_ref, lb_ref, ltri)
            ic = i_ref[rows, ls]
            stv = st[...]
            stall_ref[h, n] = stv
            bs[...] = b
            kks[...] = kk
            ics[...] = ic
            o = _dot((qs * jnp.exp(b)).astype(BF16), stv.astype(BF16), NT)
            ic_b = ic.astype(BF16)
            pieces = []
            for blk in range(NSUB):
                s0 = blk * SUB
                bI, qI = b[s0:s0 + SUB], qs[s0:s0 + SUB]
                if blk == 0:
                    oI = jnp.zeros((SUB, LANE), F32)
                else:
                    ref = bs[s0 - 1:s0, :]
                    qt = (qI * jnp.exp(bI - ref)).astype(BF16)
                    kt = (kk[0:s0] * jnp.exp(ref - b[0:s0])).astype(BF16)
                    oI = _dot(_dot(qt, kt, NT).astype(BF16), ic_b[0:s0], NN)
                for s in range(SUB):
                    sr = s0 + s
                    e = jnp.exp(jnp.minimum(bI - bs[sr:sr + 1, :], 0.0))
                    a = jnp.sum(qI * kks[sr:sr + 1, :] * e, axis=-1, keepdims=True)
                    oI = oI + jnp.where(rowi >= s, a, 0.0) * ics[sr:sr + 1, :]
                pieces.append(oI)
            o = o + jnp.concatenate(pieces, axis=0)
            bl = bs[CHUNK - 1:CHUNK, :]
            kd = (kk * jnp.exp(bl - b)).astype(BF16)
            st[...] = stv * jnp.exp(bl) + _dot(ic_b, kd, TN)
            o_ref[rows, ls] = o
            r = lax.rsqrt(jnp.mean(o * o, axis=-1, keepdims=True) + EPS)
            gb = g_ref[rows, ls]
            mix_ref[rows, ls] = (o * r * gn_ref[...] * (gb * _sigmoid(gb))).astype(BF16)

        def chunk(n, carry):
            for h in range(HG):
                one_head(h, n)
            return carry

        lax.fori_loop(0, NC, chunk, 0)

    tile = pltpu.VMEM((HG, CHUNK, LANE), F32)
    return pl.pallas_call(
        body, grid=(NG,), name="hgrn_fwd",
        in_specs=[blk_in(0), blk_in(NG), blk_in(2 * NG), blk_in(3 * NG), pl.BlockSpec((1, W), lambda g: (0, g)),
                  pl.BlockSpec((1, LANE), lambda g: (0, 0))],
        out_specs=[col, col, pl.BlockSpec((HG, NC, LANE, LANE), lambda g: (g, 0, 0, 0))],
        out_shape=[_sds((T, RW), BF16), _sds((T, RW), F32), _sds((RH, NC, LANE, LANE), F32)],
        scratch_shapes=[pltpu.VMEM((HG, LANE, LANE), F32), tile, tile, tile],
        compiler_params=_params(("parallel",), big=True),
    )(proj, proj, proj, proj, lb, gn)


def _hgrn_bwd(proj, dmixin, o_b, st_all, lb, gn, AW, RW):
    T = proj.shape[0]
    RH, NC, NSUB = RW // LANE, T // CHUNK, CHUNK // SUB
    HG, W, NG, blk_in, col = _hgrn_specs(T, RW, AW)

    def body(q_ref, f_ref, i_ref, g_ref, o_ref, dn_ref, stall_ref, lb_ref, gn_ref,
             dq_ref, df_ref, di_ref, dg_ref, dlb_ref, dgn_ref, dst_all, bs_all, kks_all, ics_all, p2_all, dic_all):
        dst_all[...] = jnp.zeros_like(dst_all)
        dlb_ref[...] = jnp.zeros_like(dlb_ref)
        dgn_ref[...] = jnp.zeros_like(dgn_ref)
        ltri = _ltri()
        rowi = lax.broadcasted_iota(jnp.int32, (SUB, 1), 0)
        last = lax.broadcasted_iota(jnp.int32, (CHUNK, 1), 0) == CHUNK - 1

        def one_head(h, n):
            ls = slice(h * LANE, (h + 1) * LANE)
            dst, bs, kks, ics = dst_all.at[h], bs_all.at[h], kks_all.at[h], ics_all.at[h]
            p2, dic = p2_all.at[h], dic_all.at[h]
            rows, lbv, qb, sg, f, kk, sq, qs, b = _hgrn_gates(n, ls, q_ref, f_ref, lb_ref, ltri)
            ic = i_ref[rows, ls]
            stv = stall_ref[h, n]
            dstv = dst[...]
            o = o_ref[rows, ls]
            dn = dn_ref[rows, ls]
            gb = g_ref[rows, ls]
            sgb = _sigmoid(gb)
            r = lax.rsqrt(jnp.mean(o * o, axis=-1, keepdims=True) + EPS)
            gnv = gn_ref[...]
            dg_ref[rows, ls] = (dn * (o * r * gnv) * (sgb * (1.0 + gb * (1.0 - sgb)))).astype(BF16)
            dy = dn * (gb * sgb)
            dgn_ref[h] += _colsum(dy * o * r)
            a_ = dy * gnv
            do = r * (a_ - o * (r * r) * jnp.mean(a_ * o, axis=-1, keepdims=True))
            do_b = do.astype(BF16)
            bs[...] = b
            kks[...] = kk
            ics[...] = ic
            ic_b = ic.astype(BF16)
            eb = jnp.exp(b)
            bl = bs[CHUNK - 1:CHUNK, :]
            ebl = jnp.exp(bl)
            dec = jnp.exp(bl - b)
            kd = (kk * dec).astype(BF16)
            dst_b = dstv.astype(BF16)
            dqs = _dot(do_b, stv.astype(BF16), NN) * eb
            dkk2 = _dot(ic_b, dst_b, NN) * dec
            dic[...] = _dot(kd, dst_b, NT)
            dbl = ebl * _colsum(stv * dstv) + _colsum(kk * dkk2)
            dst[...] = dstv * ebl + _dot(do_b, (qs * eb).astype(BF16), TN)
            p2[...] = jnp.zeros_like(p2)
            p1_pieces = []
            for blk in range(NSUB):
                s0 = blk * SUB
                bI, qI, doI = b[s0:s0 + SUB], qs[s0:s0 + SUB], do[s0:s0 + SUB]
                if blk == 0:
                    p1 = jnp.zeros((SUB, LANE), F32)
                else:
                    ref = bs[s0 - 1:s0, :]
                    eq = jnp.exp(bI - ref)
                    ek = jnp.exp(ref - b[0:s0])
                    qt = (qI * eq).astype(BF16)
                    kt = (kk[0:s0] * ek).astype(BF16)
                    doI_b = doI.astype(BF16)
                    dic[0:s0, :] += _dot(_dot(qt, kt, NT).astype(BF16), doI_b, TN)
                    da = _dot(doI_b, ic_b[0:s0], NT).astype(BF16)
                    p1 = _dot(da, kt, NN) * eq
                    p2[0:s0, :] += _dot(da, qt, TN) * ek
                for s in range(SUB):
                    sr = s0 + s
                    keep = rowi >= s
                    kk_s = kks[sr:sr + 1, :]
                    e = jnp.exp(jnp.minimum(bI - bs[sr:sr + 1, :], 0.0))
                    w = qI * e
                    a = jnp.where(keep, jnp.sum(w * kk_s, axis=-1, keepdims=True), 0.0)
                    da_s = jnp.where(keep, jnp.sum(doI * ics[sr:sr + 1, :], axis=-1, keepdims=True), 0.0)
                    p1 = p1 + da_s * kk_s * e
                    p2[sr:sr + 1, :] += _colsum(da_s * w)
                    dic[sr:sr + 1, :] += _colsum(a * doI)
                p1_pieces.append(p1)
            dqs = dqs + jnp.concatenate(p1_pieces, axis=0)
            dkk = dkk2 + p2[...]
            db = qs * dqs - kk * dkk + jnp.where(last, dbl, 0.0)
            dgl = _dot(ltri, db, TN, HIGHEST)
            dfv = dgl / f - dkk
            df_ref[rows, ls] = (dfv * (1.0 - lbv) * sg * (1.0 - sg)).astype(BF16)
            dlb_ref[:, ls] += _colsum(dfv * (1.0 - sg))
            dq_ref[rows, ls] = (dqs * (sq * (1.0 + qb * (1.0 - sq)))).astype(BF16)
            di_ref[rows, ls] = dic[...].astype(BF16)

        def chunk(k, carry):
            for h in range(HG):
                one_head(h, NC - 1 - k)
            return carry

        lax.fori_loop(0, NC, chunk, 0)

    tile = pltpu.VMEM((HG, CHUNK, LANE), F32)
    return pl.pallas_call(
        body, grid=(NG,), name="hgrn_bwd",
        in_specs=[blk_in(0), blk_in(NG), blk_in(2 * NG), blk_in(3 * NG), col,
                  pl.BlockSpec((T, W), lambda g: (0, AW // W + g)),
                  pl.BlockSpec((HG, NC, LANE, LANE), lambda g: (g, 0, 0, 0)),
                  pl.BlockSpec((1, W), lambda g: (0, g)), pl.BlockSpec((1, LANE), lambda g: (0, 0))],
        out_specs=[col, col, col, col, pl.BlockSpec((1, W), lambda g: (0, g)),
                   pl.BlockSpec((HG, 1, LANE), lambda g: (g, 0, 0))],
        out_shape=[_sds((T, RW), BF16)] * 4 + [_sds((1, RW), F32), _sds((RH, 1, LANE), F32)],
        scratch_shapes=[pltpu.VMEM((HG, LANE, LANE), F32), tile, tile, tile, tile, tile],
        compiler_params=_params(("parallel",), big=True),
    )(proj, proj, proj, proj, o_b, dmixin, st_all, lb, gn)


def _prep(c, lb_logits, rb_pad, max_rel):
    D, RW = c.shape[-1], lb_logits.shape[-1]
    H, rbp = rb_pad.shape

    def body(c_ref, l_ref, rb_ref, cact_ref, lb_ref, gv_ref):
        cv = c_ref[...]
        cact_ref[...] = cv * _sigmoid(cv)
        lb_ref[...] = _sigmoid(l_ref[0:1, :] - l_ref[1:2, :])
        gv_ref[...] = _dot(rb_ref[...], _bias_onehot(rbp, max_rel), NN, HIGHEST)

    return pl.pallas_call(
        body, name="prep", out_shape=[_sds((1, D), F32), _sds((1, RW), F32), _sds((H, TAB), F32)],
    )(c, lb_logits, rb_pad)


def _mod_part(c_all, w_ada_s, b_ada_s):
    B, D = c_all.shape
    ns = w_ada_s.shape[1]
    tn = _tile(ns, 768, LANE)

    def body(c_ref, w_ref, b_ref, o_ref):
        o_ref[...] = _dot(c_ref[...], w_ref[...], NN) + b_ref[...]

    return pl.pallas_call(
        body, grid=(ns // tn,), name="mod_part",
        in_specs=[pl.BlockSpec((B, D), lambda j: (0, 0)), pl.BlockSpec((D, tn), lambda j: (0, j)),
                  pl.BlockSpec((1, tn), lambda j: (0, j))],
        out_specs=pl.BlockSpec((B, tn), lambda j: (0, j)),
        out_shape=_sds((B, ns), F32), compiler_params=_params(("parallel",)),
    )(c_all, w_ada_s, b_ada_s)


def _adam(w, g, m, v):
    m = ADAM_B1 * m + (1.0 - ADAM_B1) * g
    v = ADAM_B2 * v + (1.0 - ADAM_B2) * (g * g)
    m_hat = m / (1.0 - ADAM_B1 ** ADAM_STEP)
    v_hat = v / (1.0 - ADAM_B2 ** ADAM_STEP)
    return -ADAM_LR * (m_hat / (jnp.sqrt(v_hat) + ADAM_EPS) + ADAM_WD * w), m, v


def _adam_ada(c_all, dmod_s, w, m, v):
    B, D = c_all.shape
    ns = w.shape[1]
    tr, tn = _tile(D, 512, LANE), _tile(ns, 768, LANE)

    def body(c_ref, d_ref, w_ref, m_ref, v_ref, g_out, dw_out, m_out, v_out):
        g = _dot(c_ref[...], d_ref[...], TN)
        g_out[...] = g
        dw_out[...], m_out[...], v_out[...] = _adam(w_ref[...], g, m_ref[...], v_ref[...])

    big = pl.BlockSpec((tr, tn), lambda i, j: (i, j))
    return pl.pallas_call(
        body, grid=(D // tr, ns // tn), name="adam_w_ada",
        in_specs=[pl.BlockSpec((B, tr), lambda i, j: (0, i)), pl.BlockSpec((B, tn), lambda i, j: (0, j)),
                  big, big, big],
        out_specs=[big] * 4, out_shape=[_sds((D, ns), F32)] * 4,
        compiler_params=_params(("parallel", "parallel")),
    )(c_all, dmod_s, w, m, v)


def _adam_shard(parts, w, m, v, name):
    R, C = w.shape
    tr = _tile(R, 256, 16)

    def body(p_ref, w_ref, m_ref, v_ref, g_out, dw_out, m_out, v_out):
        g = p_ref[0].astype(F32)
        for k in range(1, N_DEV // 2):
            g = g + p_ref[k].astype(F32)
        g_out[...] = g
        dw_out[...], m_out[...], v_out[...] = _adam(w_ref[...], g, m_ref[...], v_ref[...])

    big = pl.BlockSpec((tr, C), lambda i: (i, 0))
    return pl.pallas_call(
        body, grid=(R // tr,), name=name,
        in_specs=[pl.BlockSpec((N_DEV // 2, tr, C), lambda i: (0, i, 0)), big, big, big],
        out_specs=[big] * 4, out_shape=[_sds((R, C), F32)] * 4,
        compiler_params=_params(("parallel",), big=True),
    )(parts, w, m, v)


def _pair_sum(g8, land, core, name):
    _, NCHIP, R, C = g8.shape
    tr = _tile(R, 256, 16)

    def body(core_ref, g_ref, l_ref, o_ref):
        o_ref[...] = (g_ref[...].astype(F32) + l_ref[...].astype(F32)).astype(BF16)

    return pl.pallas_call(
        body, name=name,
        grid_spec=pltpu.PrefetchScalarGridSpec(
            num_scalar_prefetch=1, grid=(NCHIP, R // tr),
            in_specs=[pl.BlockSpec((None, None, tr, C), lambda k, i, core_ref: (core_ref[0], k, i, 0)),
                      pl.BlockSpec((None, tr, C), lambda k, i, core_ref: (k, i, 0))],
            out_specs=pl.BlockSpec((None, tr, C), lambda k, i, core_ref: (k, i, 0))),
        out_shape=_sds((NCHIP, R, C), BF16), compiler_params=_params(("parallel", "parallel")),
    )(core, g8, land)


SMALL = ("b_ada", "rel_bias", "attn_norm_g", "lb_logits", "gnorm_g", "ln1_g", "ln1_b", "ln2_g", "ln2_b")


def _small_update(parts, loss_parts, lbv, ws, ms, vs, max_rel):
    n = len(SMALL)

    def body(*refs):
        part_refs = dict(zip(SMALL, refs[:n]))
        loss_in, lb_ref = refs[n], refs[n + 1]
        w_refs, m_refs, v_refs = refs[n + 2:2 * n + 2], refs[2 * n + 2:3 * n + 2], refs[3 * n + 2:4 * n + 2]
        outs = refs[4 * n + 2:]

        def total(ref):
            tot = ref[0]
            for k in range(1, N_DEV):
                tot = tot + ref[k]
            return tot

        outs[0][...] = jnp.sum(total(loss_in), axis=-1, keepdims=True)
        for idx, name in enumerate(SMALL):
            g = total(part_refs[name])
            if name == "rel_bias":
                g = _dot(g, _bias_onehot(w_refs[idx].shape[1], max_rel), NT, HIGHEST)
            elif name == "lb_logits":
                lb = lb_ref[...]
                sign = (1 - 2 * lax.broadcasted_iota(jnp.int32, (2, 1), 0)).astype(F32)
                g = sign * (g * lb * (1.0 - lb))
            elif name == "gnorm_g":
                g = _colsum(g)
            dw, mm, vv = _adam(w_refs[idx][...], g, m_refs[idx][...], v_refs[idx][...])
            outs[1 + 4 * idx][...] = g
            outs[2 + 4 * idx][...] = dw
            outs[3 + 4 * idx][...] = mm
            outs[4 + 4 * idx][...] = vv

    out_shape = [_sds((1, 1), F32)]
    for w in ws:
        out_shape += [_sds(w.shape, F32)] * 4
    return pl.pallas_call(body, name="small_update", out_shape=out_shape, compiler_params=_params(big=True))(
        *[parts[k] for k in SMALL], loss_parts, lbv, *ws, *ms, *vs)


def _place():
    x, y, c = lax.axis_index("x"), lax.axis_index("y"), lax.axis_index("c")
    return x, y, c, [(1 - x, y), (x, 1 - y), (1 - x, 1 - y)]


def _all_gather(shard, name):
    HBM = pl.BlockSpec(memory_space=pl.ANY)

    def body(x_ref, out_ref, send_sems, recv_sems, local_sem):
        x, y, c, chips = _place()
        me, sibling = (x, y, c), (x, y, 1 - c)

        def slot(px, py, pc):
            return out_ref.at[4 * px + 2 * py + pc]

        def copy(k, block, to, src=None):
            return pltpu.make_async_remote_copy(
                src_ref=slot(*block) if src is None else src, dst_ref=slot(*block),
                send_sem=send_sems.at[k], recv_sem=recv_sems.at[k], device_id=to, device_id_type=MESH)

        mine = pltpu.make_async_copy(x_ref, slot(*me), local_sem)
        mine.start()
        first = [copy(0, me, sibling, src=x_ref)]
        first += [copy(1 + j, me, (*chip, c), src=x_ref) for j, chip in enumerate(chips)]
        for cp in first:
            cp.start()
        passed = [copy(4 + j, (*chip, c), sibling) for j, chip in enumerate(chips)]
        for j, chip in enumerate(chips):
            copy(1 + j, (*chip, c), me).wait_recv()
            passed[j].start()
        copy(0, sibling, me).wait_recv()
        for j, chip in enumerate(chips):
            copy(4 + j, (*chip, 1 - c), me).wait_recv()
        for cp in first + passed:
            cp.wait_send()
        mine.wait()

    return pl.pallas_call(
        body, name=name, out_shape=_sds((N_DEV,) + shard.shape, shard.dtype),
        in_specs=[HBM], out_specs=HBM,
        scratch_shapes=[pltpu.SemaphoreType.DMA((7,)), pltpu.SemaphoreType.DMA((7,)), pltpu.SemaphoreType.DMA(())],
    )(shard)


SEM_SPEC = pl.BlockSpec(memory_space=pltpu.SEMAPHORE)
HBM_SPEC = pl.BlockSpec(memory_space=pltpu.HBM)
EFFECT = pltpu.SideEffectType.DATAFLOW_SIDE_EFFECTING


def _remote(src, dst, send_sems, recv_sems, k, dev):
    return pltpu.make_async_remote_copy(src_ref=src, dst_ref=dst, send_sem=send_sems.at[k], recv_sem=recv_sems.at[k],
                                        device_id=dev, device_id_type=MESH)


def _copy_start(name, bufs, plan, n, after):
    nb = len(bufs)

    def body(*refs):
        send_sems, recv_sems = refs[nb + 1], refs[nb + 2]
        for k, (src, dst, dev) in enumerate(plan(*refs[:nb])):
            _remote(src, dst, send_sems, recv_sems, k, dev).start()
        refs[-1][...] = jnp.zeros_like(refs[-1])

    out = pl.pallas_call(
        body, name=name,
        out_shape=(pltpu.SemaphoreType.DMA((n,)), pltpu.SemaphoreType.DMA((n,)),
                   *[pltpu.HBM(b.shape, b.dtype) for b in bufs], _sds((8, LANE), F32)),
        in_specs=[HBM_SPEC] * nb + [ORDER_ONLY],
        out_specs=(SEM_SPEC, SEM_SPEC, *[HBM_SPEC] * nb, pl.BlockSpec(memory_space=pltpu.VMEM)),
        input_output_aliases={i: 2 + i for i in range(nb)},
        compiler_params=pltpu.CompilerParams(has_side_effects=EFFECT),
    )(*[pltpu.with_memory_space_constraint(b, pltpu.HBM) for b in bufs], after)
    return (out[0], out[1]), list(out[2:2 + nb]), out[-1]


def _copy_wait(name, sems, bufs, plan, after):
    nb = len(bufs)

    def body(*refs):
        send_sems, recv_sems = refs[nb], refs[nb + 1]
        for k, (src, dst, dev) in enumerate(plan(*refs[:nb])):
            cp = _remote(src, dst, send_sems, recv_sems, k, dev)
            cp.wait_send()
            cp.wait_recv()

    out = pl.pallas_call(
        body, name=name, out_shape=tuple(pltpu.HBM(b.shape, b.dtype) for b in bufs),
        in_specs=[HBM_SPEC] * nb + [SEM_SPEC, SEM_SPEC, pl.BlockSpec(memory_space=pl.ANY)],
        out_specs=tuple([HBM_SPEC] * nb), input_output_aliases={i: i for i in range(nb)},
        compiler_params=pltpu.CompilerParams(has_side_effects=EFFECT),
    )(*bufs, sems[0], sems[1], after)
    return list(out)


def _ag_plan_chips(shard_ref, out_ref):
    x, y, c, chips = _place()
    mine = out_ref.at[4 * x + 2 * y + c]
    return [(shard_ref, mine, (x, y, 1 - c))] + [(shard_ref, mine, (*chip, c)) for chip in chips]


def _ag_plan_pass(out_ref):
    x, y, c, chips = _place()
    slots = [out_ref.at[4 * chip[0] + 2 * chip[1] + c] for chip in chips]
    return [(s, s, (x, y, 1 - c)) for s in slots]


def _rs_plan_pair(g_ref, land_ref):
    x, y, c, _ = _place()
    return [(g_ref.at[1 - c], land_ref, (x, y, 1 - c))]


def _rs_plan_chips(p_ref, land_ref):
    x, y, c, chips = _place()
    return [(p_ref.at[2 * chip[0] + chip[1]], land_ref.at[2 * x + y], (*chip, c)) for chip in chips]


class _Gather:
    def __init__(self, shard, me, tag, after):
        self.tag = tag
        out = lax.dynamic_update_slice(lax.empty((N_DEV,) + shard.shape, shard.dtype), shard[None],
                                       (me,) + (0,) * shard.ndim)
        self.sems, (self.shard, self.out), self.token = _copy_start(
            "ag_start_" + tag, [shard, out], _ag_plan_chips, 4, after)

    def arrived_from_chips(self, after):
        _, out = _copy_wait("ag_wait_" + self.tag, self.sems, [self.shard, self.out], _ag_plan_chips, after)
        self.sems, (self.out,), _ = _copy_start("ag_pass_" + self.tag, [out], _ag_plan_pass, 3, after)

    def passed_on(self, after):
        return _copy_wait("ag_pass_wait_" + self.tag, self.sems, [self.out], _ag_plan_pass, after)[0]


class _ReduceScatter:
    def __init__(self, g8, tag):
        self.tag = tag
        land = lax.empty(g8.shape[1:], g8.dtype)
        self.sems, self.bufs, self.token = _copy_start(
            "rs_pair_start_" + tag, [g8, land], _rs_plan_pair, 1, jnp.zeros((1,), F32))

    def pair_done(self, core, chip, after, start_after=None):
        g8, land = _copy_wait("rs_pair_wait_" + self.tag, self.sems, self.bufs, _rs_plan_pair, after)
        p4 = _pair_sum(g8, land, core, "rs_pair_sum_" + self.tag)
        own = lax.dynamic_slice_in_dim(p4, chip, 1, axis=0)
        land2 = lax.dynamic_update_slice(lax.empty(p4.shape, p4.dtype), own, (chip, 0, 0))
        self.sems, self.bufs, self.token = _copy_start(
            "rs_chips_start_" + self.tag, [p4, land2], _rs_plan_chips, 3,
            jnp.zeros((1,), F32) if start_after is None else start_after)

    def sums(self, after):
        return _copy_wait("rs_chips_wait_" + self.tag, self.sems, self.bufs, _rs_plan_chips, after)[1]


BIG = ("w_in", "w_o", "w_ffn_in", "w_ffn_out")
ORDER = ("w_ada", "b_ada", "w_in", "rel_bias", "attn_norm_g", "lb_logits", "gnorm_g", "w_o", "ln1_g", "ln1_b",
         "w_ffn_in", "w_ffn_out", "ln2_g", "ln2_b")


def kernel(x, c, w_ada, b_ada, w_in, rel_bias, attn_norm_g, lb_logits, gnorm_g, w_o, ln1_g, ln1_b, w_ffn_in, w_ffn_out, ln2_g, ln2_b, loss_target, m_w_ada, m_b_ada, m_w_in, m_rel_bias, m_attn_norm_g, m_lb_logits, m_gnorm_g, m_w_o, m_ln1_g, m_ln1_b, m_w_ffn_in, m_w_ffn_out, m_ln2_g, m_ln2_b, v_w_ada, v_b_ada, v_w_in, v_rel_bias, v_attn_norm_g, v_lb_logits, v_gnorm_g, v_w_o, v_ln1_g, v_ln1_b, v_w_ffn_in, v_w_ffn_out, v_ln2_g, v_ln2_b):
    W = dict(w_ada=w_ada, b_ada=b_ada, w_in=w_in, rel_bias=rel_bias, attn_norm_g=attn_norm_g, lb_logits=lb_logits,
             gnorm_g=gnorm_g, w_o=w_o, ln1_g=ln1_g, ln1_b=ln1_b, w_ffn_in=w_ffn_in, w_ffn_out=w_ffn_out,
             ln2_g=ln2_g, ln2_b=ln2_b)
    M = dict(w_ada=m_w_ada, b_ada=m_b_ada, w_in=m_w_in, rel_bias=m_rel_bias, attn_norm_g=m_attn_norm_g,
             lb_logits=m_lb_logits, gnorm_g=m_gnorm_g, w_o=m_w_o, ln1_g=m_ln1_g, ln1_b=m_ln1_b,
             w_ffn_in=m_w_ffn_in, w_ffn_out=m_w_ffn_out, ln2_g=m_ln2_g, ln2_b=m_ln2_b)
    V = dict(w_ada=v_w_ada, b_ada=v_b_ada, w_in=v_w_in, rel_bias=v_rel_bias, attn_norm_g=v_attn_norm_g,
             lb_logits=v_lb_logits, gnorm_g=v_gnorm_g, w_o=v_w_o, ln1_g=v_ln1_g, ln1_b=v_ln1_b,
             w_ffn_in=v_w_ffn_in, w_ffn_out=v_w_ffn_out, ln2_g=v_ln2_g, ln2_b=v_ln2_b)

    x2, tgt = x[0], loss_target[0]
    T, D = x2.shape
    AW, RW = attn_norm_g.shape[-1], lb_logits.shape[-1]
    MIX = AW + RW
    H, RH = AW // ATTN_HEAD_DIM, RW // LANE
    RB = rel_bias.shape[-1]
    max_rel = (RB - 1) // 2
    rbp = -(-RB // LANE) * LANE
    F = w_ffn_out.shape[1] * N_DEV
    half = N_DEV // 2
    xi, yi, ci = lax.axis_index("x"), lax.axis_index("y"), lax.axis_index("c")
    me = 4 * xi + 2 * yi + ci
    core = jnp.reshape(ci, (1,)).astype(jnp.int32)
    pad_rb = lambda a: jnp.pad(a[0], ((0, 0), (0, rbp - RB)))

    chip = 2 * xi + yi

    c_act, lbv, gv = _prep(c, lb_logits, pad_rb(rel_bias), max_rel)
    c_all = _all_gather(c_act, "ag_c").reshape(N_DEV, D)
    ns_ada = w_ada.shape[-1]
    mod_part = _mod_part(c_all, w_ada[0], lax.dynamic_slice_in_dim(b_ada, me * ns_ada, ns_ada, axis=1))
    mod_all = _all_gather(mod_part, "ag_mod")
    mod6 = lax.dynamic_index_in_dim(mod_all, me, axis=1, keepdims=False).reshape(6, D)

    ag_in = _Gather(w_in[0].astype(BF16), me, "w_in", mod_all)
    ag_o = _Gather(w_o[0].astype(BF16), me, "w_o", ag_in.token)
    ag_f1 = _Gather(w_ffn_in[0].astype(BF16), me, "w_ffn_in", ag_o.token)
    ag_f2 = _Gather(w_ffn_out[0].astype(BF16), me, "w_ffn_out", ag_f1.token)

    h1 = _ln_mod(x2, mod6 + ag_f2.token[0, 0])
    ag_in.arrived_from_chips(h1)
    wg_in = ag_in.passed_on(h1)
    proj = _mm_gathered(h1, wg_in, "in_proj")
    ag_o.arrived_from_chips(proj)
    mix_a = _attn_fwd(proj, gv, attn_norm_g, AW)
    wg_o = ag_o.passed_on(mix_a).reshape(MIX, D)
    mix_b, o_b, st_all = _hgrn_fwd(proj, lbv, gnorm_g, AW, RW)
    ag_f1.arrived_from_chips(mix_b)
    mixin = jnp.concatenate([mix_a, mix_b], axis=1)
    mix = _mm_nn(mixin, wg_o, "out_proj")
    x1, h2 = _mid_fwd(x2, mix, mod6, ln1_g, ln1_b)
    wg_f1 = ag_f1.passed_on(h2)
    gu, act = _mm_swiglu(h2, wg_f1)
    ag_f2.arrived_from_chips(act)
    wg_f2 = ag_f2.passed_on(act).reshape(F, D)
    ff = _mm_nn(act, wg_f2, "ffn_out")
    dff, dx1a, vec_a = _final(x1, ff, mod6, ln2_g, ln2_b, tgt)

    du = _mm_swiglu_bwd(dff, wg_f2, gu)
    rs_f2 = _ReduceScatter(_mm_tn_rows(dff, act, dff, F // N_DEV, "grad_w_ffn_out"), "w_ffn_out")
    tm = _tile(T, 512, 16)
    du_ij = lambda tm_, ns: pl.BlockSpec((None, tm_, ns), lambda i, j: (j // half, i, j % half))
    du_jm = lambda tm_, ns: pl.BlockSpec((None, tm_, ns), lambda j, m: (j // half, m, j % half))
    dh2 = _mm_gathered_nt(rs_f2.token, du, du_ij, wg_f1, T, tm, "ffn_in_bwd")
    rs_f2.pair_done(core, chip, dh2)
    tm_red = _tile(T, 1024, 16)
    gw_f1 = _mm_tn_gathered(rs_f2.token, h2, du, du_jm, wg_f1.shape[-1], tm_red, "grad_w_ffn_in")
    rs_f1 = _ReduceScatter(gw_f1.reshape(2, half, D, -1), "w_ffn_in")
    dmix, dxa, vec_b = _mid_bwd(x2, mix, x1, dx1a, dh2, mod6 + rs_f1.token[0, 0], ln1_g)
    dmixin = _mm_nt(dmix, wg_o, "out_proj_bwd")
    rs_f1.pair_done(core, chip, dmixin)
    rs_o = _ReduceScatter(_mm_tn_rows(rs_f1.token, mixin, dmix, MIX // N_DEV, "grad_w_o"), "w_o")
    dq, dk, dv, dgv, dga = _attn_bwd(proj, dmixin, gv + rs_o.token[0, 0], attn_norm_g, AW)
    rs_o.pair_done(core, chip, dq)
    dqb, dfl, dib, dgb, dlb, dgn = _hgrn_bwd(proj, dmixin, o_b, st_all, lbv + rs_o.token[0, 0], gnorm_g, AW, RW)
    dproj = jnp.concatenate([dq, dk, dv, dqb, dfl, dib, dgb], axis=1)
    p_ij = lambda tm_, ns: pl.BlockSpec((tm_, ns), lambda i, j: (i, j))
    p_jm = lambda tm_, ns: pl.BlockSpec((tm_, ns), lambda j, m: (m, j))
    gw_in = _mm_tn_gathered(rs_o.token, h1, dproj, p_jm, wg_in.shape[-1], tm_red, "grad_w_in")
    rs_in = _ReduceScatter(gw_in.reshape(2, half, D, -1), "w_in")
    dh1 = _mm_gathered_nt(rs_in.token, dproj, p_ij, wg_in, T, tm, "in_proj_bwd")
    grad_x, vec_c = _first_bwd(x2, dh1, dxa, mod6)

    dmod = jnp.concatenate([vec_c[1:2], vec_c[0:1], vec_b[4:5], vec_b[1:2], vec_b[0:1], vec_a[2:3]], axis=0)
    pieces = dict(b_ada=dmod, rel_bias=dgv, attn_norm_g=dga, lb_logits=dlb, gnorm_g=dgn, ln1_g=vec_b[2:3],
                  ln1_b=vec_b[3:4], ln2_g=vec_a[0:1], ln2_b=vec_a[1:2], loss=vec_a[3:4])
    widths = dict(b_ada=(1, 6 * D), rel_bias=(H, TAB), attn_norm_g=(1, AW), lb_logits=(1, RW), gnorm_g=(RH, LANE),
                  ln1_g=(1, D), ln1_b=(1, D), ln2_g=(1, D), ln2_b=(1, D), loss=(1, D))
    packed = jnp.concatenate([pieces[k].reshape(-1, LANE) for k in widths], axis=0)
    gathered = _all_gather(packed, "ag_small")
    rs_in.pair_done(core, chip, dh1, start_after=gathered)
    parts, r0 = {}, 0
    for k, (rows, width) in widths.items():
        nr = rows * width // LANE
        parts[k] = gathered[:, r0:r0 + nr, :].reshape(N_DEV, rows, width)
        r0 += nr
    prep_small = lambda d, k: pad_rb(d[k]) if k == "rel_bias" else d[k]
    small = _small_update(parts, parts["loss"], lbv, [prep_small(W, k) for k in SMALL],
                          [prep_small(M, k) for k in SMALL], [prep_small(V, k) for k in SMALL], max_rel)
    loss = small[0].reshape(())
    res = {}
    for idx, k in enumerate(SMALL):
        four = small[1 + 4 * idx:5 + 4 * idx]
        if k == "rel_bias":
            four = [a[:, :RB][None] for a in four]
        res[k] = list(four)

    dmod_s = lax.dynamic_slice_in_dim(parts["b_ada"].reshape(N_DEV, 6 * D), me * ns_ada, ns_ada, axis=1)
    dmod_s = dmod_s + rs_in.token[0, 0]
    res["w_ada"] = [a[None] for a in _adam_ada(c_all, dmod_s, w_ada[0], m_w_ada[0], v_w_ada[0])]
    after = res["w_ada"][0]
    for k, rs in (("w_ffn_out", rs_f2), ("w_ffn_in", rs_f1), ("w_o", rs_o), ("w_in", rs_in)):
        four = _adam_shard(rs.sums(after), W[k][0], M[k][0], V[k][0], "adam_" + k)
        res[k] = [a[None] for a in four]
        after = four[0]

    out = [loss, grad_x[None]]
    for field in range(4):
        out += [res[k][field] for k in ORDER]
    return tuple(out)
```

```python
import functools

import jax
import jax.numpy as jnp
from jax import lax
from jax.experimental import pallas as pl
from jax.experimental.pallas import tpu as pltpu

F32 = jnp.float32
BF16 = jnp.bfloat16
MESH = pl.DeviceIdType.MESH
HIGHEST = lax.Precision.HIGHEST

N_DEV = 8
CHUNK = 64
N_PAST = 8
QBLK = 4 * CHUNK
KPAD = N_PAST * CHUNK
WIN = KPAD + QBLK
TAB = 1024
ATTN_HEAD_DIM = 64
REC_HEAD_DIM = 128
SUB = 16
ROWS = 8
LANE = 128
EPS = 1e-5
ALPHA = 2.0 ** 0.25
ADAM_LR, ADAM_B1, ADAM_B2, ADAM_EPS, ADAM_WD, ADAM_STEP = 0.001, 0.9, 0.999, 1e-08, 0.01, 10
NEG = -1e30
VMEM_LIMIT = 56 * 1024 * 1024


def _sds(shape, dtype):
    return jax.ShapeDtypeStruct(tuple(shape), dtype)


def _tile(n, pref, mult):
    best = None
    for t in range(mult, min(n, pref) + 1, mult):
        if n % t == 0:
            best = t
    return n if best is None else best


def _params(sem=None, big=False):
    kw = {}
    if sem is not None:
        kw["dimension_semantics"] = sem
    if big:
        kw["vmem_limit_bytes"] = VMEM_LIMIT
    return pltpu.CompilerParams(**kw)


def _sigmoid(v):
    return 1.0 / (1.0 + jnp.exp(-v))


def _dot(a, b, dims, precision=None):
    return lax.dot_general(a, b, (dims, ((), ())), preferred_element_type=F32, precision=precision)


NN = ((1,), (0,))
NT = ((1,), (1,))
TN = ((0,), (0,))


def _ln(v):
    mu = jnp.mean(v, axis=-1, keepdims=True)
    d = v - mu
    rstd = lax.rsqrt(jnp.mean(d * d, axis=-1, keepdims=True) + EPS)
    return d * rstd, rstd


def _ln_bwd(dxh, xh, rstd):
    return rstd * (dxh - jnp.mean(dxh, axis=-1, keepdims=True) - xh * jnp.mean(dxh * xh, axis=-1, keepdims=True))


def _colsum(v):
    return jnp.sum(v, axis=0, keepdims=True)


def _ln_mod(x2, mod6):
    T, D = x2.shape
    tm = _tile(T, 256, 8)

    def body(x_ref, mod_ref, o_ref):
        xh, _ = _ln(x_ref[...])
        o_ref[...] = (xh * (1.0 + mod_ref[1:2, :]) + mod_ref[0:1, :]).astype(BF16)

    return pl.pallas_call(
        body, grid=(T // tm,), name="ln_mod",
        in_specs=[pl.BlockSpec((tm, D), lambda i: (i, 0)), pl.BlockSpec((6, D), lambda i: (0, 0))],
        out_specs=pl.BlockSpec((tm, D), lambda i: (i, 0)),
        out_shape=_sds((T, D), BF16), compiler_params=_params(("parallel",)),
    )(x2, mod6)


def _mid_fwd(x2, mix, mod6, ln1_g, ln1_b):
    T, D = x2.shape
    tm = _tile(T, 256, 8)

    def body(x_ref, mix_ref, mod_ref, g_ref, b_ref, x1_ref, h2_ref):
        zh, _ = _ln(ALPHA * x_ref[...] + mod_ref[2:3, :] * mix_ref[...])
        x1 = zh * g_ref[...] + b_ref[...]
        x1_ref[...] = x1
        xh, _ = _ln(x1)
        h2_ref[...] = (xh * (1.0 + mod_ref[4:5, :]) + mod_ref[3:4, :]).astype(BF16)

    row = pl.BlockSpec((tm, D), lambda i: (i, 0))
    vec = pl.BlockSpec((1, D), lambda i: (0, 0))
    return pl.pallas_call(
        body, grid=(T // tm,), name="mid_fwd",
        in_specs=[row, row, pl.BlockSpec((6, D), lambda i: (0, 0)), vec, vec],
        out_specs=[row, row],
        out_shape=[_sds((T, D), F32), _sds((T, D), BF16)], compiler_params=_params(("parallel",)),
    )(x2, mix, mod6, ln1_g, ln1_b)


def _final(x1, ff, mod6, ln2_g, ln2_b, tgt):
    T, D = x1.shape
    tm = _tile(T, 256, 8)

    def body(x1_ref, ff_ref, mod_ref, g_ref, b_ref, t_ref, dff_ref, dx1_ref, vec_ref):
        @pl.when(pl.program_id(0) == 0)
        def _():
            vec_ref[...] = jnp.zeros_like(vec_ref)

        ff_v = ff_ref[...]
        gate2 = mod_ref[5:6, :]
        zh, rstd = _ln(ALPHA * x1_ref[...] + gate2 * ff_v)
        err = zh * g_ref[...] + b_ref[...] - t_ref[...]
        dy = err * (1.0 / D)
        dz = _ln_bwd(dy * g_ref[...], zh, rstd)
        dff_ref[...] = (gate2 * dz).astype(BF16)
        dx1_ref[...] = ALPHA * dz
        vec_ref[0:1, :] += _colsum(dy * zh)
        vec_ref[1:2, :] += _colsum(dy)
        vec_ref[2:3, :] += _colsum(dz * ff_v)
        vec_ref[3:4, :] += _colsum(err * err) * (0.5 / D)

    row = pl.BlockSpec((tm, D), lambda i: (i, 0))
    vec = pl.BlockSpec((1, D), lambda i: (0, 0))
    return pl.pallas_call(
        body, grid=(T // tm,), name="final_fwd_bwd",
        in_specs=[row, row, pl.BlockSpec((6, D), lambda i: (0, 0)), vec, vec, row],
        out_specs=[row, row, pl.BlockSpec((8, D), lambda i: (0, 0))],
        out_shape=[_sds((T, D), BF16), _sds((T, D), F32), _sds((8, D), F32)],
        compiler_params=_params(("arbitrary",)),
    )(x1, ff, mod6, ln2_g, ln2_b, tgt)


def _mid_bwd(x2, mix, x1, dx1a, dh2, mod6, ln1_g):
    T, D = x2.shape
    tm = _tile(T, 256, 8)

    def body(x_ref, mix_ref, x1_ref, dx1a_ref, dh2_ref, mod_ref, g_ref, dmix_ref, dxa_ref, vec_ref):
        @pl.when(pl.program_id(0) == 0)
        def _():
            vec_ref[...] = jnp.zeros_like(vec_ref)

        dh2 = dh2_ref[...]
        xh, rstd = _ln(x1_ref[...])
        dx1 = dx1a_ref[...] + _ln_bwd(dh2 * (1.0 + mod_ref[4:5, :]), xh, rstd)
        mix_v = mix_ref[...]
        gate1 = mod_ref[2:3, :]
        zh, rstdz = _ln(ALPHA * x_ref[...] + gate1 * mix_v)
        dz = _ln_bwd(dx1 * g_ref[...], zh, rstdz)
        dmix_ref[...] = (gate1 * dz).astype(BF16)
        dxa_ref[...] = ALPHA * dz
        vec_ref[0:1, :] += _colsum(dh2 * xh)
        vec_ref[1:2, :] += _colsum(dh2)
        vec_ref[2:3, :] += _colsum(dx1 * zh)
        vec_ref[3:4, :] += _colsum(dx1)
        vec_ref[4:5, :] += _colsum(dz * mix_v)

    row = pl.BlockSpec((tm, D), lambda i: (i, 0))
    vec = pl.BlockSpec((1, D), lambda i: (0, 0))
    return pl.pallas_call(
        body, grid=(T // tm,), name="mid_bwd",
        in_specs=[row, row, row, row, row, pl.BlockSpec((6, D), lambda i: (0, 0)), vec],
        out_specs=[row, row, pl.BlockSpec((8, D), lambda i: (0, 0))],
        out_shape=[_sds((T, D), BF16), _sds((T, D), F32), _sds((8, D), F32)],
        compiler_params=_params(("arbitrary",)),
    )(x2, mix, x1, dx1a, dh2, mod6, ln1_g)


def _first_bwd(x2, dh1, dxa, mod6):
    T, D = x2.shape
    tm = _tile(T, 256, 8)

    def body(x_ref, dh1_ref, dxa_ref, mod_ref, gx_ref, vec_ref):
        @pl.when(pl.program_id(0) == 0)
        def _():
            vec_ref[...] = jnp.zeros_like(vec_ref)

        dh1 = dh1_ref[...]
        xh, rstd = _ln(x_ref[...])
        gx_ref[...] = dxa_ref[...] + _ln_bwd(dh1 * (1.0 + mod_ref[1:2, :]), xh, rstd)
        vec_ref[0:1, :] += _colsum(dh1 * xh)
        vec_ref[1:2, :] += _colsum(dh1)

    row = pl.BlockSpec((tm, D), lambda i: (i, 0))
    return pl.pallas_call(
        body, grid=(T // tm,), name="first_bwd",
        in_specs=[row, row, row, pl.BlockSpec((6, D), lambda i: (0, 0))],
        out_specs=[row, pl.BlockSpec((8, D), lambda i: (0, 0))],
        out_shape=[_sds((T, D), F32), _sds((8, D), F32)],
        compiler_params=_params(("arbitrary",)),
    )(x2, dh1, dxa, mod6)


def _slot(j):
    return (j % 2) * 4 + j // 2


def _mm_gathered(a, wg, name):
    M, K = a.shape
    _, _, ns = wg.shape
    tm = _tile(M, 512, 16)

    def body(a_ref, w_ref, o_ref):
        o_ref[...] = _dot(a_ref[...], w_ref[...], NN)

    return pl.pallas_call(
        body, grid=(N_DEV, M // tm), name=name,
        in_specs=[pl.BlockSpec((tm, K), lambda j, i: (i, 0)), pl.BlockSpec((None, K, ns), lambda j, i: (j, 0, 0))],
        out_specs=pl.BlockSpec((tm, ns), lambda j, i: (i, j)),
        out_shape=_sds((M, N_DEV * ns), F32), compiler_params=_params(("parallel", "parallel"), big=True),
    )(a, wg)


def _mm_nn(a, b, name):
    M, K = a.shape
    _, N = b.shape
    tm, tn, tk = _tile(M, 512, 16), _tile(N, 1024, LANE), _tile(K, 2048, LANE)

    def body(a_ref, b_ref, o_ref):
        @pl.when(pl.program_id(2) == 0)
        def _():
            o_ref[...] = jnp.zeros_like(o_ref)

        o_ref[...] += _dot(a_ref[...], b_ref[...], NN)

    return pl.pallas_call(
        body, grid=(M // tm, N // tn, K // tk), name=name,
        in_specs=[pl.BlockSpec((tm, tk), lambda i, j, k: (i, k)), pl.BlockSpec((tk, tn), lambda i, j, k: (k, j))],
        out_specs=pl.BlockSpec((tm, tn), lambda i, j, k: (i, j)),
        out_shape=_sds((M, N), F32), compiler_params=_params(("parallel", "parallel", "arbitrary"), big=True),
    )(a, b)


def _mm_nt(a, b, name):
    M, K = a.shape
    N, _ = b.shape
    tm, tn = _tile(M, 512, 16), _tile(N, 1024, LANE)

    def body(a_ref, b_ref, o_ref):
        o_ref[...] = _dot(a_ref[...], b_ref[...], NT)

    return pl.pallas_call(
        body, grid=(M // tm, N // tn), name=name,
        in_specs=[pl.BlockSpec((tm, K), lambda i, j: (i, 0)), pl.BlockSpec((tn, K), lambda i, j: (j, 0))],
        out_specs=pl.BlockSpec((tm, tn), lambda i, j: (i, j)),
        out_shape=_sds((M, N), F32), compiler_params=_params(("parallel", "parallel"), big=True),
    )(a, b)


def _mm_swiglu(h2, wg):
    M, K = h2.shape
    _, _, ns = wg.shape
    half = N_DEV // 2
    tm = _tile(M, 256, 16)

    def body(a_ref, wgate_ref, wup_ref, gu_ref, act_ref):
        a = a_ref[...]
        g = _dot(a, wgate_ref[...], NN)
        u = _dot(a, wup_ref[...], NN)
        gu_ref[0] = g
        gu_ref[1] = u
        act_ref[...] = (g * _sigmoid(g) * u).astype(BF16)

    return pl.pallas_call(
        body, grid=(half, M // tm), name="ffn_in_swiglu",
        in_specs=[pl.BlockSpec((tm, K), lambda j, i: (i, 0)),
                  pl.BlockSpec((None, K, ns), lambda j, i: (j, 0, 0)),
                  pl.BlockSpec((None, K, ns), lambda j, i: (j + half, 0, 0))],
        out_specs=[pl.BlockSpec((2, tm, ns), lambda j, i: (0, i, j)), pl.BlockSpec((tm, ns), lambda j, i: (i, j))],
        out_shape=[_sds((2, M, half * ns), F32), _sds((M, half * ns), BF16)],
        compiler_params=_params(("parallel", "parallel"), big=True),
    )(h2, wg, wg)


def _mm_swiglu_bwd(dff, w2, gu):
    M, K = dff.shape
    F = w2.shape[0]
    tm, tn = _tile(M, 512, 16), _tile(F, 1408, LANE)

    def body(a_ref, b_ref, gu_ref, du_ref):
        da = _dot(a_ref[...], b_ref[...], NT)
        g = gu_ref[0]
        u = gu_ref[1]
        sg = _sigmoid(g)
        du_ref[0] = (da * u * (sg * (1.0 + g * (1.0 - sg)))).astype(BF16)
        du_ref[1] = (da * (g * sg)).astype(BF16)

    return pl.pallas_call(
        body, grid=(F // tn, M // tm), name="ffn_out_bwd_swiglu",
        in_specs=[pl.BlockSpec((tm, K), lambda j, i: (i, 0)), pl.BlockSpec((tn, K), lambda j, i: (j, 0)),
                  pl.BlockSpec((2, tm, tn), lambda j, i: (0, i, j))],
        out_specs=pl.BlockSpec((2, tm, tn), lambda j, i: (0, i, j)),
        out_shape=_sds((2, M, F), BF16), compiler_params=_params(("parallel", "parallel"), big=True),
    )(dff, w2, gu)


ORDER_ONLY = pl.BlockSpec(memory_space=pl.ANY)


def _mm_tn_rows(dep, a, b, rs, name):
    M, Ka = a.shape
    _, N = b.shape
    tm = _tile(M, 1024, 16)

    def body(_, a_ref, b_ref, o_ref, acc_ref):
        m = pl.program_id(1)

        @pl.when(m == 0)
        def _():
            acc_ref[...] = jnp.zeros_like(acc_ref)

        acc_ref[...] += _dot(a_ref[...], b_ref[...], TN)

        @pl.when(m == pl.num_programs(1) - 1)
        def _():
            o_ref[0, 0] = acc_ref[0:rs, :].astype(BF16)
            o_ref[1, 0] = acc_ref[rs:2 * rs, :].astype(BF16)

    return pl.pallas_call(
        body, grid=(N_DEV // 2, M // tm), name=name,
        in_specs=[ORDER_ONLY, pl.BlockSpec((tm, 2 * rs), lambda ch, m: (m, ch)),
                  pl.BlockSpec((tm, N), lambda ch, m: (m, 0))],
        out_specs=pl.BlockSpec((2, 1, rs, N), lambda ch, m: (0, ch, 0, 0)),
        out_shape=_sds((2, N_DEV // 2, rs, N), BF16),
        scratch_shapes=[pltpu.VMEM((2 * rs, N), F32)],
        compiler_params=_params(("parallel", "arbitrary"), big=True),
    )(dep, a, b)


def _mm_gathered_nt(dep, a, a_spec, wg, M, tm, name):
    _, K, ns = wg.shape

    def body(_, a_ref, w_ref, o_ref):
        @pl.when(pl.program_id(1) == 0)
        def _():
            o_ref[...] = jnp.zeros_like(o_ref)

        o_ref[...] += _dot(a_ref[...], w_ref[...], NT)

    return pl.pallas_call(
        body, grid=(M // tm, N_DEV), name=name,
        in_specs=[ORDER_ONLY, a_spec(tm, ns), pl.BlockSpec((None, K, ns), lambda i, j: (j, 0, 0))],
        out_specs=pl.BlockSpec((tm, K), lambda i, j: (i, 0)),
        out_shape=_sds((M, K), F32), compiler_params=_params(("parallel", "arbitrary"), big=True),
    )(dep, a, wg)


def _mm_tn_gathered(dep, h, a, a_spec, ns, tm, name):
    M, K = h.shape

    def body(_, h_ref, a_ref, o_ref, acc_ref):
        m = pl.program_id(1)

        @pl.when(m == 0)
        def _():
            acc_ref[...] = jnp.zeros_like(acc_ref)

        acc_ref[...] += _dot(h_ref[...], a_ref[...], TN)

        @pl.when(m == pl.num_programs(1) - 1)
        def _():
            o_ref[...] = acc_ref[...].astype(BF16)

    return pl.pallas_call(
        body, grid=(N_DEV, M // tm), name=name,
        in_specs=[ORDER_ONLY, pl.BlockSpec((tm, K), lambda j, m: (m, 0)), a_spec(tm, ns)],
        out_specs=pl.BlockSpec((None, K, ns), lambda j, m: (_slot(j), 0, 0)),
        out_shape=_sds((N_DEV, K, ns), BF16),
        scratch_shapes=[pltpu.VMEM((K, ns), F32)],
        compiler_params=_params(("parallel", "arbitrary"), big=True),
    )(dep, h, a)


def _bias_onehot(rbp, max_rel):
    r = lax.broadcasted_iota(jnp.int32, (rbp, TAB), 0)
    m = lax.broadcasted_iota(jnp.int32, (rbp, TAB), 1)
    dist = KPAD - jnp.where(m < WIN, m, m - TAB)
    return (r == jnp.clip(dist, -max_rel, max_rel) + max_rel).astype(F32)


def _attn_setup(i, hp, k_ref, v_ref, gv_ref, kpad, vpad, bias):
    ls = slice(i * ATTN_HEAD_DIM, (i + 1) * ATTN_HEAD_DIM)
    kpad[i][0:KPAD, :] = jnp.zeros((KPAD, ATTN_HEAD_DIM), BF16)
    vpad[i][0:KPAD, :] = jnp.zeros((KPAD, ATTN_HEAD_DIM), BF16)
    kpad[i][KPAD:, :] = k_ref[:, ls].astype(BF16)
    vpad[i][KPAD:, :] = v_ref[:, ls].astype(BF16)
    gvrow = gv_ref[pl.ds(hp * 2 + i, 1), :]
    tab = pltpu.roll(jnp.broadcast_to(gvrow, (QBLK, TAB)), 0, 1, stride=1, stride_axis=0)
    row = lax.broadcasted_iota(jnp.int32, (QBLK, WIN), 0)
    col = lax.broadcasted_iota(jnp.int32, (QBLK, WIN), 1)
    first = jnp.bitwise_and(row, -CHUNK)
    seen = jnp.logical_and(col >= first, col < first + (N_PAST + 1) * CHUNK)
    bias[i][...] = jnp.where(seen, tab[:, 0:WIN], NEG)


def _attn_probs(b, i, q_ref, kpad, vpad, bias, col):
    ls = slice(i * ATTN_HEAD_DIM, (i + 1) * ATTN_HEAD_DIM)
    r0 = pl.multiple_of(b * QBLK, QBLK)
    q = q_ref[pl.ds(r0, QBLK), ls].astype(BF16)
    kw = kpad[i][pl.ds(r0, WIN), :]
    vw = vpad[i][pl.ds(r0, WIN), :]
    s = _dot(q, kw, NT) * (ATTN_HEAD_DIM ** -0.5) + bias[i][...]
    s = jnp.where(col >= KPAD - r0, s, NEG)
    p = jnp.exp(s - jnp.max(s, axis=-1, keepdims=True))
    pn = p / jnp.sum(p, axis=-1, keepdims=True)
    return r0, ls, q, kw, vw, pn


def _attn_fwd(proj, gv, ga, AW):
    T = proj.shape[0]
    HP = AW // LANE

    def body(q_ref, k_ref, v_ref, gv_ref, ga_ref, o_ref, *scratch):
        kpad, vpad, bias = scratch[0:2], scratch[2:4], scratch[4:6]
        hp = pl.program_id(0)
        for i in range(2):
            _attn_setup(i, hp, k_ref, v_ref, gv_ref, kpad, vpad, bias)
        col = lax.broadcasted_iota(jnp.int32, (QBLK, WIN), 1)

        def block(b, carry):
            for i in range(2):
                r0, ls, _, _, vw, pn = _attn_probs(b, i, q_ref, kpad, vpad, bias, col)
                o = _dot(pn.astype(BF16), vw, NN)
                r = lax.rsqrt(jnp.mean(o * o, axis=-1, keepdims=True) + EPS)
                o_ref[pl.ds(r0, QBLK), ls] = (o * r * ga_ref[0:1, ls]).astype(BF16)
            return carry

        lax.fori_loop(0, T // QBLK, block, 0)

    blk = lambda off: pl.BlockSpec((T, LANE), lambda hp: (0, off + hp))
    return pl.pallas_call(
        body, grid=(HP,), name="attn_fwd",
        in_specs=[blk(0), blk(HP), blk(2 * HP), pl.BlockSpec(gv.shape, lambda hp: (0, 0)),
                  pl.BlockSpec((1, LANE), lambda hp: (0, hp))],
        out_specs=pl.BlockSpec((T, LANE), lambda hp: (0, hp)),
        out_shape=_sds((T, AW), BF16),
        scratch_shapes=[pltpu.VMEM((T + KPAD, ATTN_HEAD_DIM), BF16)] * 4 + [pltpu.VMEM((QBLK, WIN), F32)] * 2,
        compiler_params=_params(("parallel",), big=True),
    )(proj, proj, proj, gv, ga)


def _attn_bwd(proj, dmixin, gv, ga, AW):
    T = proj.shape[0]
    HP = AW // LANE
    scale = ATTN_HEAD_DIM ** -0.5

    def body(q_ref, k_ref, v_ref, dn_ref, gv_ref, ga_ref, dq_ref, dk_ref, dv_ref, dgv_ref, dga_ref, *scratch):
        kpad, vpad, dkacc, dvacc = scratch[0:2], scratch[2:4], scratch[4:6], scratch[6:8]
        bias, dbias = scratch[8:10], scratch[10:12]
        hp = pl.program_id(0)
        for i in range(2):
            _attn_setup(i, hp, k_ref, v_ref, gv_ref, kpad, vpad, bias)
            dkacc[i][...] = jnp.zeros_like(dkacc[i])
            dvacc[i][...] = jnp.zeros_like(dvacc[i])
            dbias[i][...] = jnp.zeros_like(dbias[i])
        dga_ref[...] = jnp.zeros_like(dga_ref)
        col = lax.broadcasted_iota(jnp.int32, (QBLK, WIN), 1)

        def block(b, carry):
            for i in range(2):
                r0, ls, q, kw, vw, pn = _attn_probs(b, i, q_ref, kpad, vpad, bias, col)
                pn_b = pn.astype(BF16)
                o = _dot(pn_b, vw, NN)
                r = lax.rsqrt(jnp.mean(o * o, axis=-1, keepdims=True) + EPS)
                dn = dn_ref[pl.ds(r0, QBLK), ls]
                dga_ref[i:i + 1, :] += _colsum(dn * o * r)
                a = dn * ga_ref[0:1, ls]
                do = r * (a - o * (r * r) * jnp.mean(a * o, axis=-1, keepdims=True))
                do_b = do.astype(BF16)
                dp = _dot(do_b, vw, NT)
                dvacc[i][pl.ds(r0, WIN), :] += _dot(pn_b, do_b, TN)
                ds = pn * (dp - jnp.sum(pn * dp, axis=-1, keepdims=True))
                dbias[i][...] += ds
                ds_b = ds.astype(BF16)
                dq_ref[pl.ds(r0, QBLK), ls] = (_dot(ds_b, kw, NN) * scale).astype(BF16)
                dkacc[i][pl.ds(r0, WIN), :] += _dot(ds_b, q, TN) * scale
            return carry

        lax.fori_loop(0, T // QBLK, block, 0)

        rr = lax.broadcasted_iota(jnp.int32, (QBLK, QBLK), 0)
        cc = lax.broadcasted_iota(jnp.int32, (QBLK, QBLK), 1)
        flip = (rr + cc == QBLK - 1).astype(BF16)
        for i in range(2):
            ls = slice(i * ATTN_HEAD_DIM, (i + 1) * ATTN_HEAD_DIM)
            dk_ref[:, ls] = dkacc[i][KPAD:, :].astype(BF16)
            dv_ref[:, ls] = dvacc[i][KPAD:, :].astype(BF16)
            full = jnp.concatenate([dbias[i][...], jnp.zeros((QBLK, TAB - WIN), F32)], axis=1)
            hi = full.astype(BF16)
            lo = (full - hi.astype(F32)).astype(BF16)
            rev = _dot(flip, hi, NN) + _dot(flip, lo, NN)
            dgv_ref[i:i + 1, :] = _colsum(pltpu.roll(rev, TAB - (QBLK - 1), 1, stride=1, stride_axis=0))

    blk = lambda off: pl.BlockSpec((T, LANE), lambda hp: (0, off + hp))
    accs = lambda dt: [pltpu.VMEM((T + KPAD, ATTN_HEAD_DIM), dt)] * 2
    return pl.pallas_call(
        body, grid=(HP,), name="attn_bwd",
        in_specs=[blk(0), blk(HP), blk(2 * HP), blk(0), pl.BlockSpec(gv.shape, lambda hp: (0, 0)),
                  pl.BlockSpec((1, LANE), lambda hp: (0, hp))],
        out_specs=[blk(0), blk(0), blk(0), pl.BlockSpec((None, 2, TAB), lambda hp: (hp, 0, 0)),
                   pl.BlockSpec((None, 2, ATTN_HEAD_DIM), lambda hp: (hp, 0, 0))],
        out_shape=[_sds((T, AW), BF16), _sds((T, AW), BF16), _sds((T, AW), BF16),
                   _sds((HP, 2, TAB), F32), _sds((HP, 2, ATTN_HEAD_DIM), F32)],
        scratch_shapes=accs(BF16) + accs(BF16) + accs(F32) + accs(F32) + [pltpu.VMEM((QBLK, WIN), F32)] * 4,
        compiler_params=_params(("parallel",), big=True),
    )(proj, proj, proj, dmixin, gv, ga)


def _ltri():
    r = lax.broadcasted_iota(jnp.int32, (CHUNK, CHUNK), 0)
    c = lax.broadcasted_iota(jnp.int32, (CHUNK, CHUNK), 1)
    return (c <= r).astype(BF16)


def _tri_dot(tri, v, dims):
    hi = v.astype(BF16)
    lo = (v - hi.astype(F32)).astype(BF16)
    return _dot(tri, hi, dims) + _dot(tri, lo, dims)


HEADS_PER_STEP = 2


def _hgrn_gates(n, ls, q_ref, f_ref, lb_ref, ltri):
    r0 = pl.multiple_of(n * CHUNK, CHUNK)
    rows = pl.ds(r0, CHUNK)
    lb = lb_ref[:, ls]
    qb = q_ref[rows, ls]
    sg = _sigmoid(f_ref[rows, ls])
    f = lb + (1.0 - lb) * sg
    sq = _sigmoid(qb)
    b = _tri_dot(ltri, jnp.log(f), NN)
    return rows, lb, qb, sg, f, 1.0 - f, sq, qb * sq, b


def _hgrn_specs(T, RW, AW):
    HG = HEADS_PER_STEP
    W = HG * LANE
    base = 3 * AW // W
    blk_in = lambda off: pl.BlockSpec((T, W), lambda g: (0, base + off + g))
    col = pl.BlockSpec((T, W), lambda g: (0, g))
    return HG, W, RW // W, blk_in, col


def _hgrn_fwd(proj, lb, gn, AW, RW):
    T = proj.shape[0]
    RH, NC, NSUB = RW // LANE, T // CHUNK, CHUNK // SUB
    HG, W, NG, blk_in, col = _hgrn_specs(T, RW, AW)

    def body(q_ref, f_ref, i_ref, g_ref, lb_ref, gn_ref, mix_ref, o_ref, stall_ref, st_all, bs_all, kks_all, ics_all):
        st_all[...] = jnp.zeros_like(st_all)
        ltri = _ltri()
        rowi = lax.broadcasted_iota(jnp.int32, (SUB, 1), 0)

        def one_head(h, n):
            ls = slice(h * LANE, (h + 1) * LANE)
            st, bs, kks, ics = st_all.at[h], bs_all.at[h], kks_all.at[h], ics_all.at[h]
            rows, _, _, _, _, kk, _, qs, b = _hgrn_gates(n, ls, q_ref, f_ref, lb_ref, ltri)
            ic = i_ref[rows, ls]
            stv = st[...]
            stall_ref[h, n] = stv
            bs[...] = b
            kks[...] = kk
            ics[...] = ic
            o = _dot((qs * jnp.exp(b)).astype(BF16), stv.astype(BF16), NT)
            ic_b = ic.astype(BF16)
            pieces = []
            for blk in range(NSUB):
                s0 = blk * SUB
                bI, qI = b[s0:s0 + SUB], qs[s0:s0 + SUB]
                if blk == 0:
                    oI = jnp.zeros((SUB, LANE), F32)
                else:
                    ref = bs[s0 - 1:s0, :]
                    qt = (qI * jnp.exp(bI - ref)).astype(BF16)
                    kt = (kk[0:s0] * jnp.exp(ref - b[0:s0])).astype(BF16)
                    oI = _dot(_dot(qt, kt, NT).astype(BF16), ic_b[0:s0], NN)
                acc = [oI[g * ROWS:(g + 1) * ROWS] for g in range(SUB // ROWS)]
                for s in range(SUB):
                    sr = s0 + s
                    g0 = s // ROWS
                    lo = g0 * ROWS
                    e = jnp.exp(jnp.minimum(bI[lo:] - bs[sr:sr + 1, :], 0.0))
                    a = jnp.sum(qI[lo:] * kks[sr:sr + 1, :] * e, axis=-1, keepdims=True)
                    add = jnp.where(rowi[lo:] >= s, a, 0.0) * ics[sr:sr + 1, :]
                    for g in range(g0, SUB // ROWS):
                        acc[g] = acc[g] + add[(g - g0) * ROWS:(g - g0 + 1) * ROWS]
                pieces.extend(acc)
            o = o + jnp.concatenate(pieces, axis=0)
            bl = bs[CHUNK - 1:CHUNK, :]
            kd = (kk * jnp.exp(bl - b)).astype(BF16)
            st[...] = stv * jnp.exp(bl) + _dot(ic_b, kd, TN)
            o_ref[rows, ls] = o
            r = lax.rsqrt(jnp.mean(o * o, axis=-1, keepdims=True) + EPS)
            gb = g_ref[rows, ls]
            mix_ref[rows, ls] = (o * r * gn_ref[...] * (gb * _sigmoid(gb))).astype(BF16)

        def chunk(n, carry):
            for h in range(HG):
                one_head(h, n)
            return carry

        lax.fori_loop(0, NC, chunk, 0)

    tile = pltpu.VMEM((HG, CHUNK, LANE), F32)
    return pl.pallas_call(
        body, grid=(NG,), name="hgrn_fwd",
        in_specs=[blk_in(0), blk_in(NG), blk_in(2 * NG), blk_in(3 * NG), pl.BlockSpec((1, W), lambda g: (0, g)),
                  pl.BlockSpec((1, LANE), lambda g: (0, 0))],
        out_specs=[col, col, pl.BlockSpec((HG, NC, LANE, LANE), lambda g: (g, 0, 0, 0))],
        out_shape=[_sds((T, RW), BF16), _sds((T, RW), F32), _sds((RH, NC, LANE, LANE), F32)],
        scratch_shapes=[pltpu.VMEM((HG, LANE, LANE), F32), tile, tile, tile],
        compiler_params=_params(("parallel",), big=True),
    )(proj, proj, proj, proj, lb, gn)


def _hgrn_bwd(proj, dmixin, o_b, st_all, lb, gn, AW, RW):
    T = proj.shape[0]
    RH, NC, NSUB = RW // LANE, T // CHUNK, CHUNK // SUB
    HG, W, NG, blk_in, col = _hgrn_specs(T, RW, AW)

    def body(q_ref, f_ref, i_ref, g_ref, o_ref, dn_ref, stall_ref, lb_ref, gn_ref,
             dq_ref, df_ref, di_ref, dg_ref, dlb_ref, dgn_ref, dst_all, bs_all, kks_all, ics_all, p2_all, dic_all):
        dst_all[...] = jnp.zeros_like(dst_all)
        dlb_ref[...] = jnp.zeros_like(dlb_ref)
        dgn_ref[...] = jnp.zeros_like(dgn_ref)
        ltri = _ltri()
        rowi = lax.broadcasted_iota(jnp.int32, (SUB, 1), 0)
        last = lax.broadcasted_iota(jnp.int32, (CHUNK, 1), 0) == CHUNK - 1

        def one_head(h, n):
            ls = slice(h * LANE, (h + 1) * LANE)
            dst, bs, kks, ics = dst_all.at[h], bs_all.at[h], kks_all.at[h], ics_all.at[h]
            p2, dic = p2_all.at[h], dic_all.at[h]
            rows, lbv, qb, sg, f, kk, sq, qs, b = _hgrn_gates(n, ls, q_ref, f_ref, lb_ref, ltri)
            ic = i_ref[rows, ls]
            stv = stall_ref[h, n]
            dstv = dst[...]
            o = o_ref[rows, ls]
            dn = dn_ref[rows, ls]
            gb = g_ref[rows, ls]
            sgb = _sigmoid(gb)
            r = lax.rsqrt(jnp.mean(o * o, axis=-1, keepdims=True) + EPS)
            gnv = gn_ref[...]
            dg_ref[rows, ls] = (dn * (o * r * gnv) * (sgb * (1.0 + gb * (1.0 - sgb)))).astype(BF16)
            dy = dn * (gb * sgb)
            dgn_ref[h] += _colsum(dy * o * r)
            a_ = dy * gnv
            do = r * (a_ - o * (r * r) * jnp.mean(a_ * o, axis=-1, keepdims=True))
            do_b = do.astype(BF16)
            bs[...] = b
            kks[...] = kk
            ics[...] = ic
            ic_b = ic.astype(BF16)
            eb = jnp.exp(b)
            bl = bs[CHUNK - 1:CHUNK, :]
            ebl = jnp.exp(bl)
            dec = jnp.exp(bl - b)
            kd = (kk * dec).astype(BF16)
            dst_b = dstv.astype(BF16)
            dqs = _dot(do_b, stv.astype(BF16), NN) * eb
            dkk2 = _dot(ic_b, dst_b, NN) * dec
            dic[...] = _dot(kd, dst_b, NT)
            dbl = ebl * _colsum(stv * dstv) + _colsum(kk * dkk2)
            dst[...] = dstv * ebl + _dot(do_b, (qs * eb).astype(BF16), TN)
            p2[...] = jnp.zeros_like(p2)
            p1_pieces = []
            for blk in range(NSUB):
                s0 = blk * SUB
                bI, qI, doI = b[s0:s0 + SUB], qs[s0:s0 + SUB], do[s0:s0 + SUB]
                if blk == 0:
                    p1 = jnp.zeros((SUB, LANE), F32)
                else:
                    ref = bs[s0 - 1:s0, :]
                    eq = jnp.exp(bI - ref)
                    ek = jnp.exp(ref - b[0:s0])
                    qt = (qI * eq).astype(BF16)
                    kt = (kk[0:s0] * ek).astype(BF16)
                    doI_b = doI.astype(BF16)
                    dic[0:s0, :] += _dot(_dot(qt, kt, NT).astype(BF16), doI_b, TN)
                    da = _dot(doI_b, ic_b[0:s0], NT).astype(BF16)
                    p1 = _dot(da, kt, NN) * eq
                    p2[0:s0, :] += _dot(da, qt, TN) * ek
                acc = [p1[g * ROWS:(g + 1) * ROWS] for g in range(SUB // ROWS)]
                for s in range(SUB):
                    sr = s0 + s
                    g0 = s // ROWS
                    lo = g0 * ROWS
                    keep = rowi[lo:] >= s
                    kk_s = kks[sr:sr + 1, :]
                    e = jnp.exp(jnp.minimum(bI[lo:] - bs[sr:sr + 1, :], 0.0))
                    w = qI[lo:] * e
                    a = jnp.where(keep, jnp.sum(w * kk_s, axis=-1, keepdims=True), 0.0)
                    da_s = jnp.where(keep, jnp.sum(doI[lo:] * ics[sr:sr + 1, :], axis=-1, keepdims=True), 0.0)
                    add = da_s * kk_s * e
                    for g in range(g0, SUB // ROWS):
                        acc[g] = acc[g] + add[(g - g0) * ROWS:(g - g0 + 1) * ROWS]
                    p2[sr:sr + 1, :] += _colsum(da_s * w)
                    dic[sr:sr + 1, :] += _colsum(a * doI[lo:])
                p1_pieces.extend(acc)
            dqs = dqs + jnp.concatenate(p1_pieces, axis=0)
            dkk = dkk2 + p2[...]
            db = qs * dqs - kk * dkk + jnp.where(last, dbl, 0.0)
            dgl = _tri_dot(ltri, db, TN)
            dfv = dgl / f - dkk
            df_ref[rows, ls] = (dfv * (1.0 - lbv) * sg * (1.0 - sg)).astype(BF16)
            dlb_ref[:, ls] += _colsum(dfv * (1.0 - sg))
            dq_ref[rows, ls] = (dqs * (sq * (1.0 + qb * (1.0 - sq)))).astype(BF16)
            di_ref[rows, ls] = dic[...].astype(BF16)

        def chunk(k, carry):
            for h in range(HG):
                one_head(h, NC - 1 - k)
            return carry

        lax.fori_loop(0, NC, chunk, 0)

    tile = pltpu.VMEM((HG, CHUNK, LANE), F32)
    return pl.pallas_call(
        body, grid=(NG,), name="hgrn_bwd",
        in_specs=[blk_in(0), blk_in(NG), blk_in(2 * NG), blk_in(3 * NG), col,
                  pl.BlockSpec((T, W), lambda g: (0, AW // W + g)),
                  pl.BlockSpec((HG, NC, LANE, LANE), lambda g: (g, 0, 0, 0)),
                  pl.BlockSpec((1, W), lambda g: (0, g)), pl.BlockSpec((1, LANE), lambda g: (0, 0))],
        out_specs=[col, col, col, col, pl.BlockSpec((1, W), lambda g: (0, g)),
                   pl.BlockSpec((HG, 1, LANE), lambda g: (g, 0, 0))],
        out_shape=[_sds((T, RW), BF16)] * 4 + [_sds((1, RW), F32), _sds((RH, 1, LANE), F32)],
        scratch_shapes=[pltpu.VMEM((HG, LANE, LANE), F32), tile, tile, tile, tile, tile],
        compiler_params=_params(("parallel",), big=True),
    )(proj, proj, proj, proj, o_b, dmixin, st_all, lb, gn)


def _prep(c, lb_logits, rb_pad, max_rel):
    D, RW = c.shape[-1], lb_logits.shape[-1]
    H, rbp = rb_pad.shape

    def body(c_ref, l_ref, rb_ref, cact_ref, lb_ref, gv_ref):
        cv = c_ref[...]
        cact_ref[...] = cv * _sigmoid(cv)
        lb_ref[...] = _sigmoid(l_ref[0:1, :] - l_ref[1:2, :])
        gv_ref[...] = _dot(rb_ref[...], _bias_onehot(rbp, max_rel), NN, HIGHEST)

    return pl.pallas_call(
        body, name="prep", out_shape=[_sds((1, D), F32), _sds((1, RW), F32), _sds((H, TAB), F32)],
    )(c, lb_logits, rb_pad)


def _mod_part(c_all, w_ada_s, b_ada_s):
    B, D = c_all.shape
    ns = w_ada_s.shape[1]
    tn = _tile(ns, 768, LANE)

    def body(c_ref, w_ref, b_ref, o_ref):
        o_ref[...] = _dot(c_ref[...], w_ref[...], NN) + b_ref[...]

    return pl.pallas_call(
        body, grid=(ns // tn,), name="mod_part",
        in_specs=[pl.BlockSpec((B, D), lambda j: (0, 0)), pl.BlockSpec((D, tn), lambda j: (0, j)),
                  pl.BlockSpec((1, tn), lambda j: (0, j))],
        out_specs=pl.BlockSpec((B, tn), lambda j: (0, j)),
        out_shape=_sds((B, ns), F32), compiler_params=_params(("parallel",)),
    )(c_all, w_ada_s, b_ada_s)


def _adam(w, g, m, v):
    m = ADAM_B1 * m + (1.0 - ADAM_B1) * g
    v = ADAM_B2 * v + (1.0 - ADAM_B2) * (g * g)
    m_hat = m / (1.0 - ADAM_B1 ** ADAM_STEP)
    v_hat = v / (1.0 - ADAM_B2 ** ADAM_STEP)
    return -ADAM_LR * (m_hat / (jnp.sqrt(v_hat) + ADAM_EPS) + ADAM_WD * w), m, v


def _adam_ada(c_all, dmod_s, w, m, v):
    B, D = c_all.shape
    ns = w.shape[1]
    tr, tn = _tile(D, 512, LANE), _tile(ns, 768, LANE)

    def body(c_ref, d_ref, w_ref, m_ref, v_ref, g_out, dw_out, m_out, v_out):
        g = _dot(c_ref[...], d_ref[...], TN)
        g_out[...] = g
        dw_out[...], m_out[...], v_out[...] = _adam(w_ref[...], g, m_ref[...], v_ref[...])

    big = pl.BlockSpec((tr, tn), lambda i, j: (i, j))
    return pl.pallas_call(
        body, grid=(D // tr, ns // tn), name="adam_w_ada",
        in_specs=[pl.BlockSpec((B, tr), lambda i, j: (0, i)), pl.BlockSpec((B, tn), lambda i, j: (0, j)),
                  big, big, big],
        out_specs=[big] * 4, out_shape=[_sds((D, ns), F32)] * 4,
        compiler_params=_params(("parallel", "parallel")),
    )(c_all, dmod_s, w, m, v)


def _adam_shard(parts, w, m, v, name):
    R, C = w.shape
    tr = _tile(R, 256, 16)

    def body(p_ref, w_ref, m_ref, v_ref, g_out, dw_out, m_out, v_out):
        g = p_ref[0].astype(F32)
        for k in range(1, N_DEV // 2):
            g = g + p_ref[k].astype(F32)
        g_out[...] = g
        dw_out[...], m_out[...], v_out[...] = _adam(w_ref[...], g, m_ref[...], v_ref[...])

    big = pl.BlockSpec((tr, C), lambda i: (i, 0))
    return pl.pallas_call(
        body, grid=(R // tr,), name=name,
        in_specs=[pl.BlockSpec((N_DEV // 2, tr, C), lambda i: (0, i, 0)), big, big, big],
        out_specs=[big] * 4, out_shape=[_sds((R, C), F32)] * 4,
        compiler_params=_params(("parallel",), big=True),
    )(parts, w, m, v)


def _pair_sum(g8, land, core, name):
    _, NCHIP, R, C = g8.shape
    tr = _tile(R, 256, 16)

    def body(core_ref, g_ref, l_ref, o_ref):
        o_ref[...] = (g_ref[...].astype(F32) + l_ref[...].astype(F32)).astype(BF16)

    return pl.pallas_call(
        body, name=name,
        grid_spec=pltpu.PrefetchScalarGridSpec(
            num_scalar_prefetch=1, grid=(NCHIP, R // tr),
            in_specs=[pl.BlockSpec((None, None, tr, C), lambda k, i, core_ref: (core_ref[0], k, i, 0)),
                      pl.BlockSpec((None, tr, C), lambda k, i, core_ref: (k, i, 0))],
            out_specs=pl.BlockSpec((None, tr, C), lambda k, i, core_ref: (k, i, 0))),
        out_shape=_sds((NCHIP, R, C), BF16), compiler_params=_params(("parallel", "parallel")),
    )(core, g8, land)


SMALL = ("b_ada", "rel_bias", "attn_norm_g", "lb_logits", "gnorm_g", "ln1_g", "ln1_b", "ln2_g", "ln2_b")


def _small_update(parts, loss_parts, lbv, ws, ms, vs, max_rel):
    n = len(SMALL)

    def body(*refs):
        part_refs = dict(zip(SMALL, refs[:n]))
        loss_in, lb_ref = refs[n], refs[n + 1]
        w_refs, m_refs, v_refs = refs[n + 2:2 * n + 2], refs[2 * n + 2:3 * n + 2], refs[3 * n + 2:4 * n + 2]
        outs = refs[4 * n + 2:]

        def total(ref):
            tot = ref[0]
            for k in range(1, N_DEV):
                tot = tot + ref[k]
            return tot

        outs[0][...] = jnp.sum(total(loss_in), axis=-1, keepdims=True)
        for idx, name in enumerate(SMALL):
            g = total(part_refs[name])
            if name == "rel_bias":
                g = _dot(g, _bias_onehot(w_refs[idx].shape[1], max_rel), NT, HIGHEST)
            elif name == "lb_logits":
                lb = lb_ref[...]
                sign = (1 - 2 * lax.broadcasted_iota(jnp.int32, (2, 1), 0)).astype(F32)
                g = sign * (g * lb * (1.0 - lb))
            elif name == "gnorm_g":
                g = _colsum(g)
            dw, mm, vv = _adam(w_refs[idx][...], g, m_refs[idx][...], v_refs[idx][...])
            outs[1 + 4 * idx][...] = g
            outs[2 + 4 * idx][...] = dw
            outs[3 + 4 * idx][...] = mm
            outs[4 + 4 * idx][...] = vv

    out_shape = [_sds((1, 1), F32)]
    for w in ws:
        out_shape += [_sds(w.shape, F32)] * 4
    return pl.pallas_call(body, name="small_update", out_shape=out_shape, compiler_params=_params(big=True))(
        *[parts[k] for k in SMALL], loss_parts, lbv, *ws, *ms, *vs)


def _place():
    x, y, c = lax.axis_index("x"), lax.axis_index("y"), lax.axis_index("c")
    return x, y, c, [(1 - x, y), (x, 1 - y), (1 - x, 1 - y)]


def _all_gather(shard, name):
    HBM = pl.BlockSpec(memory_space=pl.ANY)

    def body(x_ref, out_ref, send_sems, recv_sems, local_sem):
        x, y, c, chips = _place()
        me, sibling = (x, y, c), (x, y, 1 - c)

        def slot(px, py, pc):
            return out_ref.at[4 * px + 2 * py + pc]

        def copy(k, block, to, src=None):
            return pltpu.make_async_remote_copy(
                src_ref=slot(*block) if src is None else src, dst_ref=slot(*block),
                send_sem=send_sems.at[k], recv_sem=recv_sems.at[k], device_id=to, device_id_type=MESH)

        mine = pltpu.make_async_copy(x_ref, slot(*me), local_sem)
        mine.start()
        first = [copy(0, me, sibling, src=x_ref)]
        first += [copy(1 + j, me, (*chip, c), src=x_ref) for j, chip in enumerate(chips)]
        for cp in first:
            cp.start()
        passed = [copy(4 + j, (*chip, c), sibling) for j, chip in enumerate(chips)]
        for j, chip in enumerate(chips):
            copy(1 + j, (*chip, c), me).wait_recv()
            passed[j].start()
        copy(0, sibling, me).wait_recv()
        for j, chip in enumerate(chips):
            copy(4 + j, (*chip, 1 - c), me).wait_recv()
        for cp in first + passed:
            cp.wait_send()
        mine.wait()

    return pl.pallas_call(
        body, name=name, out_shape=_sds((N_DEV,) + shard.shape, shard.dtype),
        in_specs=[HBM], out_specs=HBM,
        scratch_shapes=[pltpu.SemaphoreType.DMA((7,)), pltpu.SemaphoreType.DMA((7,)), pltpu.SemaphoreType.DMA(())],
    )(shard)


SEM_SPEC = pl.BlockSpec(memory_space=pltpu.SEMAPHORE)
HBM_SPEC = pl.BlockSpec(memory_space=pltpu.HBM)
EFFECT = pltpu.SideEffectType.DATAFLOW_SIDE_EFFECTING


def _remote(src, dst, send_sems, recv_sems, k, dev):
    return pltpu.make_async_remote_copy(src_ref=src, dst_ref=dst, send_sem=send_sems.at[k], recv_sem=recv_sems.at[k],
                                        device_id=dev, device_id_type=MESH)


def _copy_start(name, bufs, plan, n, after):
    nb = len(bufs)

    def body(*refs):
        send_sems, recv_sems = refs[nb + 1], refs[nb + 2]
        for k, (src, dst, dev) in enumerate(plan(*refs[:nb])):
            _remote(src, dst, send_sems, recv_sems, k, dev).start()
        refs[-1][...] = jnp.zeros_like(refs[-1])

    out = pl.pallas_call(
        body, name=name,
        out_shape=(pltpu.SemaphoreType.DMA((n,)), pltpu.SemaphoreType.DMA((n,)),
                   *[pltpu.HBM(b.shape, b.dtype) for b in bufs], _sds((8, LANE), F32)),
        in_specs=[HBM_SPEC] * nb + [ORDER_ONLY],
        out_specs=(SEM_SPEC, SEM_SPEC, *[HBM_SPEC] * nb, pl.BlockSpec(memory_space=pltpu.VMEM)),
        input_output_aliases={i: 2 + i for i in range(nb)},
        compiler_params=pltpu.CompilerParams(has_side_effects=EFFECT),
    )(*[pltpu.with_memory_space_constraint(b, pltpu.HBM) for b in bufs], after)
    return (out[0], out[1]), list(out[2:2 + nb]), out[-1]


def _copy_wait(name, sems, bufs, plan, after):
    nb = len(bufs)

    def body(*refs):
        send_sems, recv_sems = refs[nb], refs[nb + 1]
        for k, (src, dst, dev) in enumerate(plan(*refs[:nb])):
            cp = _remote(src, dst, send_sems, recv_sems, k, dev)
            cp.wait_send()
            cp.wait_recv()

    out = pl.pallas_call(
        body, name=name, out_shape=tuple(pltpu.HBM(b.shape, b.dtype) for b in bufs),
        in_specs=[HBM_SPEC] * nb + [SEM_SPEC, SEM_SPEC, pl.BlockSpec(memory_space=pl.ANY)],
        out_specs=tuple([HBM_SPEC] * nb), input_output_aliases={i: i for i in range(nb)},
        compiler_params=pltpu.CompilerParams(has_side_effects=EFFECT),
    )(*bufs, sems[0], sems[1], after)
    return list(out)


def _ag_plan_chips(shard_ref, out_ref):
    x, y, c, chips = _place()
    mine = out_ref.at[4 * x + 2 * y + c]
    return [(shard_ref, mine, (x, y, 1 - c))] + [(shard_ref, mine, (*chip, c)) for chip in chips]


def _ag_plan_pass(out_ref):
    x, y, c, chips = _place()
    slots = [out_ref.at[4 * chip[0] + 2 * chip[1] + c] for chip in chips]
    return [(s, s, (x, y, 1 - c)) for s in slots]


def _rs_plan_pair(g_ref, land_ref):
    x, y, c, _ = _place()
    return [(g_ref.at[1 - c], land_ref, (x, y, 1 - c))]


def _rs_plan_chips(p_ref, land_ref):
    x, y, c, chips = _place()
    return [(p_ref.at[2 * chip[0] + chip[1]], land_ref.at[2 * x + y], (*chip, c)) for chip in chips]


class _Gather:
    def __init__(self, shard, me, tag, after):
        self.tag = tag
        out = lax.dynamic_update_slice(lax.empty((N_DEV,) + shard.shape, shard.dtype), shard[None],
                                       (me,) + (0,) * shard.ndim)
        self.sems, (self.shard, self.out), self.token = _copy_start(
            "ag_start_" + tag, [shard, out], _ag_plan_chips, 4, after)

    def arrived_from_chips(self, after):
        _, out = _copy_wait("ag_wait_" + self.tag, self.sems, [self.shard, self.out], _ag_plan_chips, after)
        self.sems, (self.out,), _ = _copy_start("ag_pass_" + self.tag, [out], _ag_plan_pass, 3, after)

    def passed_on(self, after):
        return _copy_wait("ag_pass_wait_" + self.tag, self.sems, [self.out], _ag_plan_pass, after)[0]


class _ReduceScatter:
    def __init__(self, g8, tag):
        self.tag = tag
        land = lax.empty(g8.shape[1:], g8.dtype)
        self.sems, self.bufs, self.token = _copy_start(
            "rs_pair_start_" + tag, [g8, land], _rs_plan_pair, 1, jnp.zeros((1,), F32))

    def pair_done(self, core, chip, after, start_after=None):
        g8, land = _copy_wait("rs_pair_wait_" + self.tag, self.sems, self.bufs, _rs_plan_pair, after)
        p4 = _pair_sum(g8, land, core, "rs_pair_sum_" + self.tag)
        own = lax.dynamic_slice_in_dim(p4, chip, 1, axis=0)
        land2 = lax.dynamic_update_slice(lax.empty(p4.shape, p4.dtype), own, (chip, 0, 0))
        self.sems, self.bufs, self.token = _copy_start(
            "rs_chips_start_" + self.tag, [p4, land2], _rs_plan_chips, 3,
            jnp.zeros((1,), F32) if start_after is None else start_after)

    def sums(self, after):
        return _copy_wait("rs_chips_wait_" + self.tag, self.sems, self.bufs, _rs_plan_chips, after)[1]


BIG = ("w_in", "w_o", "w_ffn_in", "w_ffn_out")
ORDER = ("w_ada", "b_ada", "w_in", "rel_bias", "attn_norm_g", "lb_logits", "gnorm_g", "w_o", "ln1_g", "ln1_b",
         "w_ffn_in", "w_ffn_out", "ln2_g", "ln2_b")


def kernel(x, c, w_ada, b_ada, w_in, rel_bias, attn_norm_g, lb_logits, gnorm_g, w_o, ln1_g, ln1_b, w_ffn_in, w_ffn_out, ln2_g, ln2_b, loss_target, m_w_ada, m_b_ada, m_w_in, m_rel_bias, m_attn_norm_g, m_lb_logits, m_gnorm_g, m_w_o, m_ln1_g, m_ln1_b, m_w_ffn_in, m_w_ffn_out, m_ln2_g, m_ln2_b, v_w_ada, v_b_ada, v_w_in, v_rel_bias, v_attn_norm_g, v_lb_logits, v_gnorm_g, v_w_o, v_ln1_g, v_ln1_b, v_w_ffn_in, v_w_ffn_out, v_ln2_g, v_ln2_b):
    W = dict(w_ada=w_ada, b_ada=b_ada, w_in=w_in, rel_bias=rel_bias, attn_norm_g=attn_norm_g, lb_logits=lb_logits,
             gnorm_g=gnorm_g, w_o=w_o, ln1_g=ln1_g, ln1_b=ln1_b, w_ffn_in=w_ffn_in, w_ffn_out=w_ffn_out,
             ln2_g=ln2_g, ln2_b=ln2_b)
    M = dict(w_ada=m_w_ada, b_ada=m_b_ada, w_in=m_w_in, rel_bias=m_rel_bias, attn_norm_g=m_attn_norm_g,
             lb_logits=m_lb_logits, gnorm_g=m_gnorm_g, w_o=m_w_o, ln1_g=m_ln1_g, ln1_b=m_ln1_b,
             w_ffn_in=m_w_ffn_in, w_ffn_out=m_w_ffn_out, ln2_g=m_ln2_g, ln2_b=m_ln2_b)
    V = dict(w_ada=v_w_ada, b_ada=v_b_ada, w_in=v_w_in, rel_bias=v_rel_bias, attn_norm_g=v_attn_norm_g,
             lb_logits=v_lb_logits, gnorm_g=v_gnorm_g, w_o=v_w_o, ln1_g=v_ln1_g, ln1_b=v_ln1_b,
             w_ffn_in=v_w_ffn_in, w_ffn_out=v_w_ffn_out, ln2_g=v_ln2_g, ln2_b=v_ln2_b)

    x2, tgt = x[0], loss_target[0]
    T, D = x2.shape
    AW, RW = attn_norm_g.shape[-1], lb_logits.shape[-1]
    MIX = AW + RW
    H, RH = AW // ATTN_HEAD_DIM, RW // LANE
    RB = rel_bias.shape[-1]
    max_rel = (RB - 1) // 2
    rbp = -(-RB // LANE) * LANE
    F = w_ffn_out.shape[1] * N_DEV
    half = N_DEV // 2
    xi, yi, ci = lax.axis_index("x"), lax.axis_index("y"), lax.axis_index("c")
    me = 4 * xi + 2 * yi + ci
    core = jnp.reshape(ci, (1,)).astype(jnp.int32)
    pad_rb = lambda a: jnp.pad(a[0], ((0, 0), (0, rbp - RB)))

    chip = 2 * xi + yi

    c_act, lbv, gv = _prep(c, lb_logits, pad_rb(rel_bias), max_rel)
    c_all = _all_gather(c_act, "ag_c").reshape(N_DEV, D)
    ns_ada = w_ada.shape[-1]
    mod_part = _mod_part(c_all, w_ada[0], lax.dynamic_slice_in_dim(b_ada, me * ns_ada, ns_ada, axis=1))
    mod_all = _all_gather(mod_part, "ag_mod")
    mod6 = lax.dynamic_index_in_dim(mod_all, me, axis=1, keepdims=False).reshape(6, D)

    ag_in = _Gather(w_in[0].astype(BF16), me, "w_in", mod_all)
    ag_o = _Gather(w_o[0].astype(BF16), me, "w_o", ag_in.token)
    ag_f1 = _Gather(w_ffn_in[0].astype(BF16), me, "w_ffn_in", ag_o.token)
    ag_f2 = _Gather(w_ffn_out[0].astype(BF16), me, "w_ffn_out", ag_f1.token)

    h1 = _ln_mod(x2, mod6 + ag_f2.token[0, 0])
    ag_in.arrived_from_chips(h1)
    wg_in = ag_in.passed_on(h1)
    proj = _mm_gathered(h1, wg_in, "in_proj")
    ag_o.arrived_from_chips(proj)
    mix_a = _attn_fwd(proj, gv, attn_norm_g, AW)
    wg_o = ag_o.passed_on(mix_a).reshape(MIX, D)
    mix_b, o_b, st_all = _hgrn_fwd(proj, lbv, gnorm_g, AW, RW)
    ag_f1.arrived_from_chips(mix_b)
    mixin = jnp.concatenate([mix_a, mix_b], axis=1)
    mix = _mm_nn(mixin, wg_o, "out_proj")
    x1, h2 = _mid_fwd(x2, mix, mod6, ln1_g, ln1_b)
    wg_f1 = ag_f1.passed_on(h2)
    gu, act = _mm_swiglu(h2, wg_f1)
    ag_f2.arrived_from_chips(act)
    wg_f2 = ag_f2.passed_on(act).reshape(F, D)
    ff = _mm_nn(act, wg_f2, "ffn_out")
    dff, dx1a, vec_a = _final(x1, ff, mod6, ln2_g, ln2_b, tgt)

    du = _mm_swiglu_bwd(dff, wg_f2, gu)
    rs_f2 = _ReduceScatter(_mm_tn_rows(dff, act, dff, F // N_DEV, "grad_w_ffn_out"), "w_ffn_out")
    tm = _tile(T, 512, 16)
    du_ij = lambda tm_, ns: pl.BlockSpec((None, tm_, ns), lambda i, j: (j // half, i, j % half))
    du_jm = lambda tm_, ns: pl.BlockSpec((None, tm_, ns), lambda j, m: (j // half, m, j % half))
    dh2 = _mm_gathered_nt(rs_f2.token, du, du_ij, wg_f1, T, tm, "ffn_in_bwd")
    rs_f2.pair_done(core, chip, dh2)
    tm_red = _tile(T, 1024, 16)
    gw_f1 = _mm_tn_gathered(rs_f2.token, h2, du, du_jm, wg_f1.shape[-1], tm_red, "grad_w_ffn_in")
    rs_f1 = _ReduceScatter(gw_f1.reshape(2, half, D, -1), "w_ffn_in")
    dmix, dxa, vec_b = _mid_bwd(x2, mix, x1, dx1a, dh2, mod6 + rs_f1.token[0, 0], ln1_g)
    dmixin = _mm_nt(dmix, wg_o, "out_proj_bwd")
    rs_f1.pair_done(core, chip, dmixin)
    rs_o = _ReduceScatter(_mm_tn_rows(rs_f1.token, mixin, dmix, MIX // N_DEV, "grad_w_o"), "w_o")
    dq, dk, dv, dgv, dga = _attn_bwd(proj, dmixin, gv + rs_o.token[0, 0], attn_norm_g, AW)
    rs_o.pair_done(core, chip, dq)
    dqb, dfl, dib, dgb, dlb, dgn = _hgrn_bwd(proj, dmixin, o_b, st_all, lbv + rs_o.token[0, 0], gnorm_g, AW, RW)
    dproj = jnp.concatenate([dq, dk, dv, dqb, dfl, dib, dgb], axis=1)
    p_ij = lambda tm_, ns: pl.BlockSpec((tm_, ns), lambda i, j: (i, j))
    p_jm = lambda tm_, ns: pl.BlockSpec((tm_, ns), lambda j, m: (m, j))
    gw_in = _mm_tn_gathered(rs_o.token, h1, dproj, p_jm, wg_in.shape[-1], tm_red, "grad_w_in")
    rs_in = _ReduceScatter(gw_in.reshape(2, half, D, -1), "w_in")
    dh1 = _mm_gathered_nt(rs_in.token, dproj, p_ij, wg_in, T, tm, "in_proj_bwd")
    grad_x, vec_c = _first_bwd(x2, dh1, dxa, mod6)

    dmod = jnp.concatenate([vec_c[1:2], vec_c[0:1], vec_b[4:5], vec_b[1:2], vec_b[0:1], vec_a[2:3]], axis=0)
    pieces = dict(b_ada=dmod, rel_bias=dgv, attn_norm_g=dga, lb_logits=dlb, gnorm_g=dgn, ln1_g=vec_b[2:3],
                  ln1_b=vec_b[3:4], ln2_g=vec_a[0:1], ln2_b=vec_a[1:2], loss=vec_a[3:4])
    widths = dict(b_ada=(1, 6 * D), rel_bias=(H, TAB), attn_norm_g=(1, AW), lb_logits=(1, RW), gnorm_g=(RH, LANE),
                  ln1_g=(1, D), ln1_b=(1, D), ln2_g=(1, D), ln2_b=(1, D), loss=(1, D))
    packed = jnp.concatenate([pieces[k].reshape(-1, LANE) for k in widths], axis=0)
    gathered = _all_gather(packed, "ag_small")
    rs_in.pair_done(core, chip, dh1, start_after=gathered)
    parts, r0 = {}, 0
    for k, (rows, width) in widths.items():
        nr = rows * width // LANE
        parts[k] = gathered[:, r0:r0 + nr, :].reshape(N_DEV, rows, width)
        r0 += nr
    prep_small = lambda d, k: pad_rb(d[k]) if k == "rel_bias" else d[k]
    small = _small_update(parts, parts["loss"], lbv, [prep_small(W, k) for k in SMALL],
                          [prep_small(M, k) for k in SMALL], [prep_small(V, k) for k in SMALL], max_rel)
    loss = small[0].reshape(())
    res = {}
    for idx, k in enumerate(SMALL):
        four = small[1 + 4 * idx:5 + 4 * idx]
        if k == "rel_bias":
            four = [a[:, :RB][None] for a in four]
        res[k] = list(four)

    dmod_s = lax.dynamic_slice_in_dim(parts["b_ada"].reshape(N_DEV, 6 * D), me * ns_ada, ns_ada, axis=1)
    dmod_s = dmod_s + rs_in.token[0, 0]
    res["w_ada"] = [a[None] for a in _adam_ada(c_all, dmod_s, w_ada[0], m_w_ada[0], v_w_ada[0])]
    after = res["w_ada"][0]
    for k, rs in (("w_ffn_out", rs_f2), ("w_ffn_in", rs_f1), ("w_o", rs_o), ("w_in", rs_in)):
        four = _adam_shard(rs.sums(after), W[k][0], M[k][0], V[k][0], "adam_" + k)
        res[k] = [a[None] for a in four]
        after = four[0]

    out = [loss, grad_x[None]]
    for field in range(4):
        out += [res[k][field] for k in ORDER]
    return tuple(out)
```

```python
import functools

import jax
import jax.numpy as jnp
from jax import lax
from jax.experimental import pallas as pl
from jax.experimental.pallas import tpu as pltpu

F32 = jnp.float32
BF16 = jnp.bfloat16
MESH = pl.DeviceIdType.MESH
HIGHEST = lax.Precision.HIGHEST

N_DEV = 8
CHUNK = 64
N_PAST = 8
QBLK = 4 * CHUNK
KPAD = N_PAST * CHUNK
WIN = KPAD + QBLK
TAB = 1024
ATTN_HEAD_DIM = 64
REC_HEAD_DIM = 128
SUB = 16
ROWS = 8
LANE = 128
EPS = 1e-5
ALPHA = 2.0 ** 0.25
ADAM_LR, ADAM_B1, ADAM_B2, ADAM_EPS, ADAM_WD, ADAM_STEP = 0.001, 0.9, 0.999, 1e-08, 0.01, 10
NEG = -1e30
VMEM_LIMIT = 56 * 1024 * 1024


def _sds(shape, dtype):
    return jax.ShapeDtypeStruct(tuple(shape), dtype)


def _tile(n, pref, mult):
    best = None
    for t in range(mult, min(n, pref) + 1, mult):
        if n % t == 0:
            best = t
    return n if best is None else best


def _params(sem=None, big=False):
    kw = {}
    if sem is not None:
        kw["dimension_semantics"] = sem
    if big:
        kw["vmem_limit_bytes"] = VMEM_LIMIT
    return pltpu.CompilerParams(**kw)


def _sigmoid(v):
    return 1.0 / (1.0 + jnp.exp(-v))


def _dot(a, b, dims, precision=None):
    return lax.dot_general(a, b, (dims, ((), ())), preferred_element_type=F32, precision=precision)


NN = ((1,), (0,))
NT = ((1,), (1,))
TN = ((0,), (0,))


def _ln(v):
    mu = jnp.mean(v, axis=-1, keepdims=True)
    d = v - mu
    rstd = lax.rsqrt(jnp.mean(d * d, axis=-1, keepdims=True) + EPS)
    return d * rstd, rstd


def _ln_bwd(dxh, xh, rstd):
    return rstd * (dxh - jnp.mean(dxh, axis=-1, keepdims=True) - xh * jnp.mean(dxh * xh, axis=-1, keepdims=True))


def _colsum(v):
    return jnp.sum(v, axis=0, keepdims=True)


def _ln_mod(x2, mod6):
    T, D = x2.shape
    tm = _tile(T, 256, 8)

    def body(x_ref, mod_ref, o_ref):
        xh, _ = _ln(x_ref[...])
        o_ref[...] = (xh * (1.0 + mod_ref[1:2, :]) + mod_ref[0:1, :]).astype(BF16)

    return pl.pallas_call(
        body, grid=(T // tm,), name="ln_mod",
        in_specs=[pl.BlockSpec((tm, D), lambda i: (i, 0)), pl.BlockSpec((6, D), lambda i: (0, 0))],
        out_specs=pl.BlockSpec((tm, D), lambda i: (i, 0)),
        out_shape=_sds((T, D), BF16), compiler_params=_params(("parallel",)),
    )(x2, mod6)


def _mid_fwd(x2, mix, mod6, ln1_g, ln1_b):
    T, D = x2.shape
    tm = _tile(T, 256, 8)

    def body(x_ref, mix_ref, mod_ref, g_ref, b_ref, x1_ref, h2_ref):
        zh, _ = _ln(ALPHA * x_ref[...] + mod_ref[2:3, :] * mix_ref[...])
        x1 = zh * g_ref[...] + b_ref[...]
        x1_ref[...] = x1
        xh, _ = _ln(x1)
        h2_ref[...] = (xh * (1.0 + mod_ref[4:5, :]) + mod_ref[3:4, :]).astype(BF16)

    row = pl.BlockSpec((tm, D), lambda i: (i, 0))
    vec = pl.BlockSpec((1, D), lambda i: (0, 0))
    return pl.pallas_call(
        body, grid=(T // tm,), name="mid_fwd",
        in_specs=[row, row, pl.BlockSpec((6, D), lambda i: (0, 0)), vec, vec],
        out_specs=[row, row],
        out_shape=[_sds((T, D), F32), _sds((T, D), BF16)], compiler_params=_params(("parallel",)),
    )(x2, mix, mod6, ln1_g, ln1_b)


def _final(x1, ff, mod6, ln2_g, ln2_b, tgt):
    T, D = x1.shape
    tm = _tile(T, 256, 8)

    def body(x1_ref, ff_ref, mod_ref, g_ref, b_ref, t_ref, dff_ref, dx1_ref, vec_ref):
        @pl.when(pl.program_id(0) == 0)
        def _():
            vec_ref[...] = jnp.zeros_like(vec_ref)

        ff_v = ff_ref[...]
        gate2 = mod_ref[5:6, :]
        zh, rstd = _ln(ALPHA * x1_ref[...] + gate2 * ff_v)
        err = zh * g_ref[...] + b_ref[...] - t_ref[...]
        dy = err * (1.0 / D)
        dz = _ln_bwd(dy * g_ref[...], zh, rstd)
        dff_ref[...] = (gate2 * dz).astype(BF16)
        dx1_ref[...] = ALPHA * dz
        vec_ref[0:1, :] += _colsum(dy * zh)
        vec_ref[1:2, :] += _colsum(dy)
        vec_ref[2:3, :] += _colsum(dz * ff_v)
        vec_ref[3:4, :] += _colsum(err * err) * (0.5 / D)

    row = pl.BlockSpec((tm, D), lambda i: (i, 0))
    vec = pl.BlockSpec((1, D), lambda i: (0, 0))
    return pl.pallas_call(
        body, grid=(T // tm,), name="final_fwd_bwd",
        in_specs=[row, row, pl.BlockSpec((6, D), lambda i: (0, 0)), vec, vec, row],
        out_specs=[row, row, pl.BlockSpec((8, D), lambda i: (0, 0))],
        out_shape=[_sds((T, D), BF16), _sds((T, D), F32), _sds((8, D), F32)],
        compiler_params=_params(("arbitrary",)),
    )(x1, ff, mod6, ln2_g, ln2_b, tgt)


def _mid_bwd(x2, mix, x1, dx1a, dh2, mod6, ln1_g):
    T, D = x2.shape
    tm = _tile(T, 256, 8)

    def body(x_ref, mix_ref, x1_ref, dx1a_ref, dh2_ref, mod_ref, g_ref, dmix_ref, dxa_ref, vec_ref):
        @pl.when(pl.program_id(0) == 0)
        def _():
            vec_ref[...] = jnp.zeros_like(vec_ref)

        dh2 = dh2_ref[...]
        xh, rstd = _ln(x1_ref[...])
        dx1 = dx1a_ref[...] + _ln_bwd(dh2 * (1.0 + mod_ref[4:5, :]), xh, rstd)
        mix_v = mix_ref[...]
        gate1 = mod_ref[2:3, :]
        zh, rstdz = _ln(ALPHA * x_ref[...] + gate1 * mix_v)
        dz = _ln_bwd(dx1 * g_ref[...], zh, rstdz)
        dmix_ref[...] = (gate1 * dz).astype(BF16)
        dxa_ref[...] = ALPHA * dz
        vec_ref[0:1, :] += _colsum(dh2 * xh)
        vec_ref[1:2, :] += _colsum(dh2)
        vec_ref[2:3, :] += _colsum(dx1 * zh)
        vec_ref[3:4, :] += _colsum(dx1)
        vec_ref[4:5, :] += _colsum(dz * mix_v)

    row = pl.BlockSpec((tm, D), lambda i: (i, 0))
    vec = pl.BlockSpec((1, D), lambda i: (0, 0))
    return pl.pallas_call(
        body, grid=(T // tm,), name="mid_bwd",
        in_specs=[row, row, row, row, row, pl.BlockSpec((6, D), lambda i: (0, 0)), vec],
        out_specs=[row, row, pl.BlockSpec((8, D), lambda i: (0, 0))],
        out_shape=[_sds((T, D), BF16), _sds((T, D), F32), _sds((8, D), F32)],
        compiler_params=_params(("arbitrary",)),
    )(x2, mix, x1, dx1a, dh2, mod6, ln1_g)


def _first_bwd(x2, dh1, dxa, mod6):
    T, D = x2.shape
    tm = _tile(T, 256, 8)

    def body(x_ref, dh1_ref, dxa_ref, mod_ref, gx_ref, vec_ref):
        @pl.when(pl.program_id(0) == 0)
        def _():
            vec_ref[...] = jnp.zeros_like(vec_ref)

        dh1 = dh1_ref[...]
        xh, rstd = _ln(x_ref[...])
        gx_ref[...] = dxa_ref[...] + _ln_bwd(dh1 * (1.0 + mod_ref[1:2, :]), xh, rstd)
        vec_ref[0:1, :] += _colsum(dh1 * xh)
        vec_ref[1:2, :] += _colsum(dh1)

    row = pl.BlockSpec((tm, D), lambda i: (i, 0))
    return pl.pallas_call(
        body, grid=(T // tm,), name="first_bwd",
        in_specs=[row, row, row, pl.BlockSpec((6, D), lambda i: (0, 0))],
        out_specs=[row, pl.BlockSpec((8, D), lambda i: (0, 0))],
        out_shape=[_sds((T, D), F32), _sds((8, D), F32)],
        compiler_params=_params(("arbitrary",)),
    )(x2, dh1, dxa, mod6)


def _slot(j):
    return (j % 2) * 4 + j // 2


def _mm_gathered(a, wg, name):
    M, K = a.shape
    _, _, ns = wg.shape
    tm = _tile(M, 512, 16)

    def body(a_ref, w_ref, o_ref):
        o_ref[...] = _dot(a_ref[...], w_ref[...], NN)

    return pl.pallas_call(
        body, grid=(N_DEV, M // tm), name=name,
        in_specs=[pl.BlockSpec((tm, K), lambda j, i: (i, 0)), pl.BlockSpec((None, K, ns), lambda j, i: (j, 0, 0))],
        out_specs=pl.BlockSpec((tm, ns), lambda j, i: (i, j)),
        out_shape=_sds((M, N_DEV * ns), F32), compiler_params=_params(("parallel", "parallel"), big=True),
    )(a, wg)


def _mm_nn(a, b, name):
    M, K = a.shape
    _, N = b.shape
    tm, tn, tk = _tile(M, 512, 16), _tile(N, 1024, LANE), _tile(K, 2048, LANE)

    def body(a_ref, b_ref, o_ref):
        @pl.when(pl.program_id(2) == 0)
        def _():
            o_ref[...] = jnp.zeros_like(o_ref)

        o_ref[...] += _dot(a_ref[...], b_ref[...], NN)

    return pl.pallas_call(
        body, grid=(M // tm, N // tn, K // tk), name=name,
        in_specs=[pl.BlockSpec((tm, tk), lambda i, j, k: (i, k)), pl.BlockSpec((tk, tn), lambda i, j, k: (k, j))],
        out_specs=pl.BlockSpec((tm, tn), lambda i, j, k: (i, j)),
        out_shape=_sds((M, N), F32), compiler_params=_params(("parallel", "parallel", "arbitrary"), big=True),
    )(a, b)


def _mm_nt(a, b, name):
    M, K = a.shape
    N, _ = b.shape
    tm, tn = _tile(M, 512, 16), _tile(N, 1024, LANE)

    def body(a_ref, b_ref, o_ref):
        o_ref[...] = _dot(a_ref[...], b_ref[...], NT)

    return pl.pallas_call(
        body, grid=(M // tm, N // tn), name=name,
        in_specs=[pl.BlockSpec((tm, K), lambda i, j: (i, 0)), pl.BlockSpec((tn, K), lambda i, j: (j, 0))],
        out_specs=pl.BlockSpec((tm, tn), lambda i, j: (i, j)),
        out_shape=_sds((M, N), F32), compiler_params=_params(("parallel", "parallel"), big=True),
    )(a, b)


def _mm_swiglu(h2, wg):
    M, K = h2.shape
    _, _, ns = wg.shape
    half = N_DEV // 2
    tm = _tile(M, 256, 16)

    def body(a_ref, wgate_ref, wup_ref, gu_ref, act_ref):
        a = a_ref[...]
        g = _dot(a, wgate_ref[...], NN)
        u = _dot(a, wup_ref[...], NN)
        gu_ref[0] = g
        gu_ref[1] = u
        act_ref[...] = (g * _sigmoid(g) * u).astype(BF16)

    return pl.pallas_call(
        body, grid=(half, M // tm), name="ffn_in_swiglu",
        in_specs=[pl.BlockSpec((tm, K), lambda j, i: (i, 0)),
                  pl.BlockSpec((None, K, ns), lambda j, i: (j, 0, 0)),
                  pl.BlockSpec((None, K, ns), lambda j, i: (j + half, 0, 0))],
        out_specs=[pl.BlockSpec((2, tm, ns), lambda j, i: (0, i, j)), pl.BlockSpec((tm, ns), lambda j, i: (i, j))],
        out_shape=[_sds((2, M, half * ns), F32), _sds((M, half * ns), BF16)],
        compiler_params=_params(("parallel", "parallel"), big=True),
    )(h2, wg, wg)


def _mm_swiglu_bwd(dff, w2, gu):
    M, K = dff.shape
    F = w2.shape[0]
    tm, tn = _tile(M, 512, 16), _tile(F, 1408, LANE)

    def body(a_ref, b_ref, gu_ref, du_ref):
        da = _dot(a_ref[...], b_ref[...], NT)
        g = gu_ref[0]
        u = gu_ref[1]
        sg = _sigmoid(g)
        du_ref[0] = (da * u * (sg * (1.0 + g * (1.0 - sg)))).astype(BF16)
        du_ref[1] = (da * (g * sg)).astype(BF16)

    return pl.pallas_call(
        body, grid=(F // tn, M // tm), name="ffn_out_bwd_swiglu",
        in_specs=[pl.BlockSpec((tm, K), lambda j, i: (i, 0)), pl.BlockSpec((tn, K), lambda j, i: (j, 0)),
                  pl.BlockSpec((2, tm, tn), lambda j, i: (0, i, j))],
        out_specs=pl.BlockSpec((2, tm, tn), lambda j, i: (0, i, j)),
        out_shape=_sds((2, M, F), BF16), compiler_params=_params(("parallel", "parallel"), big=True),
    )(dff, w2, gu)


ORDER_ONLY = pl.BlockSpec(memory_space=pl.ANY)


def _mm_tn_rows(dep, a, b, rs, name):
    M, Ka = a.shape
    _, N = b.shape
    tm = _tile(M, 1024, 16)

    def body(_, a_ref, b_ref, o_ref, acc_ref):
        m = pl.program_id(1)

        @pl.when(m == 0)
        def _():
            acc_ref[...] = jnp.zeros_like(acc_ref)

        acc_ref[...] += _dot(a_ref[...], b_ref[...], TN)

        @pl.when(m == pl.num_programs(1) - 1)
        def _():
            o_ref[0, 0] = acc_ref[0:rs, :].astype(BF16)
            o_ref[1, 0] = acc_ref[rs:2 * rs, :].astype(BF16)

    return pl.pallas_call(
        body, grid=(N_DEV // 2, M // tm), name=name,
        in_specs=[ORDER_ONLY, pl.BlockSpec((tm, 2 * rs), lambda ch, m: (m, ch)),
                  pl.BlockSpec((tm, N), lambda ch, m: (m, 0))],
        out_specs=pl.BlockSpec((2, 1, rs, N), lambda ch, m: (0, ch, 0, 0)),
        out_shape=_sds((2, N_DEV // 2, rs, N), BF16),
        scratch_shapes=[pltpu.VMEM((2 * rs, N), F32)],
        compiler_params=_params(("parallel", "arbitrary"), big=True),
    )(dep, a, b)


def _mm_gathered_nt(dep, a, a_spec, wg, M, tm, name):
    _, K, ns = wg.shape

    def body(_, a_ref, w_ref, o_ref):
        @pl.when(pl.program_id(1) == 0)
        def _():
            o_ref[...] = jnp.zeros_like(o_ref)

        o_ref[...] += _dot(a_ref[...], w_ref[...], NT)

    return pl.pallas_call(
        body, grid=(M // tm, N_DEV), name=name,
        in_specs=[ORDER_ONLY, a_spec(tm, ns), pl.BlockSpec((None, K, ns), lambda i, j: (j, 0, 0))],
        out_specs=pl.BlockSpec((tm, K), lambda i, j: (i, 0)),
        out_shape=_sds((M, K), F32), compiler_params=_params(("parallel", "arbitrary"), big=True),
    )(dep, a, wg)


def _mm_tn_gathered(dep, h, a, a_spec, ns, tm, name):
    M, K = h.shape

    def body(_, h_ref, a_ref, o_ref, acc_ref):
        m = pl.program_id(1)

        @pl.when(m == 0)
        def _():
            acc_ref[...] = jnp.zeros_like(acc_ref)

        acc_ref[...] += _dot(h_ref[...], a_ref[...], TN)

        @pl.when(m == pl.num_programs(1) - 1)
        def _():
            o_ref[...] = acc_ref[...].astype(BF16)

    return pl.pallas_call(
        body, grid=(N_DEV, M // tm), name=name,
        in_specs=[ORDER_ONLY, pl.BlockSpec((tm, K), lambda j, m: (m, 0)), a_spec(tm, ns)],
        out_specs=pl.BlockSpec((None, K, ns), lambda j, m: (_slot(j), 0, 0)),
        out_shape=_sds((N_DEV, K, ns), BF16),
        scratch_shapes=[pltpu.VMEM((K, ns), F32)],
        compiler_params=_params(("parallel", "arbitrary"), big=True),
    )(dep, h, a)


def _bias_onehot(rbp, max_rel):
    r = lax.broadcasted_iota(jnp.int32, (rbp, TAB), 0)
    m = lax.broadcasted_iota(jnp.int32, (rbp, TAB), 1)
    dist = KPAD - jnp.where(m < WIN, m, m - TAB)
    return (r == jnp.clip(dist, -max_rel, max_rel) + max_rel).astype(F32)


def _attn_setup(i, hp, k_ref, v_ref, gv_ref, kpad, vpad, bias):
    ls = slice(i * ATTN_HEAD_DIM, (i + 1) * ATTN_HEAD_DIM)
    kpad[i][0:KPAD, :] = jnp.zeros((KPAD, ATTN_HEAD_DIM), BF16)
    vpad[i][0:KPAD, :] = jnp.zeros((KPAD, ATTN_HEAD_DIM), BF16)
    kpad[i][KPAD:, :] = k_ref[:, ls].astype(BF16)
    vpad[i][KPAD:, :] = v_ref[:, ls].astype(BF16)
    gvrow = gv_ref[pl.ds(hp * 2 + i, 1), :]
    tab = pltpu.roll(jnp.broadcast_to(gvrow, (QBLK, TAB)), 0, 1, stride=1, stride_axis=0)
    row = lax.broadcasted_iota(jnp.int32, (QBLK, WIN), 0)
    col = lax.broadcasted_iota(jnp.int32, (QBLK, WIN), 1)
    first = jnp.bitwise_and(row, -CHUNK)
    seen = jnp.logical_and(col >= first, col < first + (N_PAST + 1) * CHUNK)
    bias[i][...] = jnp.where(seen, tab[:, 0:WIN], NEG)


def _attn_probs(b, i, q_ref, kpad, vpad, bias, col):
    ls = slice(i * ATTN_HEAD_DIM, (i + 1) * ATTN_HEAD_DIM)
    r0 = pl.multiple_of(b * QBLK, QBLK)
    q = q_ref[pl.ds(r0, QBLK), ls].astype(BF16)
    kw = kpad[i][pl.ds(r0, WIN), :]
    vw = vpad[i][pl.ds(r0, WIN), :]
    s = _dot(q, kw, NT) * (ATTN_HEAD_DIM ** -0.5) + bias[i][...]
    s = jnp.where(col >= KPAD - r0, s, NEG)
    p = jnp.exp(s - jnp.max(s, axis=-1, keepdims=True))
    pn = p / jnp.sum(p, axis=-1, keepdims=True)
    return r0, ls, q, kw, vw, pn


def _attn_fwd(proj, gv, ga, AW):
    T = proj.shape[0]
    HP = AW // LANE

    def body(q_ref, k_ref, v_ref, gv_ref, ga_ref, o_ref, *scratch):
        kpad, vpad, bias = scratch[0:2], scratch[2:4], scratch[4:6]
        hp = pl.program_id(0)
        for i in range(2):
            _attn_setup(i, hp, k_ref, v_ref, gv_ref, kpad, vpad, bias)
        col = lax.broadcasted_iota(jnp.int32, (QBLK, WIN), 1)

        def block(b, carry):
            outs = []
            for i in range(2):
                r0, ls, _, _, vw, pn = _attn_probs(b, i, q_ref, kpad, vpad, bias, col)
                o = _dot(pn.astype(BF16), vw, NN)
                r = lax.rsqrt(jnp.mean(o * o, axis=-1, keepdims=True) + EPS)
                outs.append(o * r * ga_ref[0:1, ls])
            o_ref[pl.ds(r0, QBLK), :] = jnp.concatenate(outs, axis=1).astype(BF16)
            return carry

        lax.fori_loop(0, T // QBLK, block, 0)

    blk = lambda off: pl.BlockSpec((T, LANE), lambda hp: (0, off + hp))
    return pl.pallas_call(
        body, grid=(HP,), name="attn_fwd",
        in_specs=[blk(0), blk(HP), blk(2 * HP), pl.BlockSpec(gv.shape, lambda hp: (0, 0)),
                  pl.BlockSpec((1, LANE), lambda hp: (0, hp))],
        out_specs=pl.BlockSpec((T, LANE), lambda hp: (0, hp)),
        out_shape=_sds((T, AW), BF16),
        scratch_shapes=[pltpu.VMEM((T + KPAD, ATTN_HEAD_DIM), BF16)] * 4 + [pltpu.VMEM((QBLK, WIN), F32)] * 2,
        compiler_params=_params(("parallel",), big=True),
    )(proj, proj, proj, gv, ga)


def _attn_bwd(proj, dmixin, gv, ga, AW):
    T = proj.shape[0]
    HP = AW // LANE
    scale = ATTN_HEAD_DIM ** -0.5

    def body(q_ref, k_ref, v_ref, dn_ref, gv_ref, ga_ref, dq_ref, dk_ref, dv_ref, dgv_ref, dga_ref, *scratch):
        kpad, vpad, dkacc, dvacc = scratch[0:2], scratch[2:4], scratch[4:6], scratch[6:8]
        bias, dbias = scratch[8:10], scratch[10:12]
        hp = pl.program_id(0)
        for i in range(2):
            _attn_setup(i, hp, k_ref, v_ref, gv_ref, kpad, vpad, bias)
            dkacc[i][...] = jnp.zeros_like(dkacc[i])
            dvacc[i][...] = jnp.zeros_like(dvacc[i])
            dbias[i][...] = jnp.zeros_like(dbias[i])
        dga_ref[...] = jnp.zeros_like(dga_ref)
        col = lax.broadcasted_iota(jnp.int32, (QBLK, WIN), 1)

        def block(b, carry):
            for i in range(2):
                r0, ls, q, kw, vw, pn = _attn_probs(b, i, q_ref, kpad, vpad, bias, col)
                pn_b = pn.astype(BF16)
                o = _dot(pn_b, vw, NN)
                r = lax.rsqrt(jnp.mean(o * o, axis=-1, keepdims=True) + EPS)
                dn = dn_ref[pl.ds(r0, QBLK), ls]
                dga_ref[i:i + 1, :] += _colsum(dn * o * r)
                a = dn * ga_ref[0:1, ls]
                do = r * (a - o * (r * r) * jnp.mean(a * o, axis=-1, keepdims=True))
                do_b = do.astype(BF16)
                dp = _dot(do_b, vw, NT)
                dvacc[i][pl.ds(r0, WIN), :] += _dot(pn_b, do_b, TN)
                ds = pn * (dp - jnp.sum(pn * dp, axis=-1, keepdims=True))
                dbias[i][...] += ds
                ds_b = ds.astype(BF16)
                dq_ref[pl.ds(r0, QBLK), ls] = (_dot(ds_b, kw, NN) * scale).astype(BF16)
                dkacc[i][pl.ds(r0, WIN), :] += _dot(ds_b, q, TN) * scale
            return carry

        lax.fori_loop(0, T // QBLK, block, 0)

        rr = lax.broadcasted_iota(jnp.int32, (QBLK, QBLK), 0)
        cc = lax.broadcasted_iota(jnp.int32, (QBLK, QBLK), 1)
        flip = (rr + cc == QBLK - 1).astype(BF16)
        for i in range(2):
            ls = slice(i * ATTN_HEAD_DIM, (i + 1) * ATTN_HEAD_DIM)
            dk_ref[:, ls] = dkacc[i][KPAD:, :].astype(BF16)
            dv_ref[:, ls] = dvacc[i][KPAD:, :].astype(BF16)
            full = jnp.concatenate([dbias[i][...], jnp.zeros((QBLK, TAB - WIN), F32)], axis=1)
            hi = full.astype(BF16)
            lo = (full - hi.astype(F32)).astype(BF16)
            rev = _dot(flip, hi, NN) + _dot(flip, lo, NN)
            dgv_ref[i:i + 1, :] = _colsum(pltpu.roll(rev, TAB - (QBLK - 1), 1, stride=1, stride_axis=0))

    blk = lambda off: pl.BlockSpec((T, LANE), lambda hp: (0, off + hp))
    accs = lambda dt: [pltpu.VMEM((T + KPAD, ATTN_HEAD_DIM), dt)] * 2
    return pl.pallas_call(
        body, grid=(HP,), name="attn_bwd",
        in_specs=[blk(0), blk(HP), blk(2 * HP), blk(0), pl.BlockSpec(gv.shape, lambda hp: (0, 0)),
                  pl.BlockSpec((1, LANE), lambda hp: (0, hp))],
        out_specs=[blk(0), blk(0), blk(0), pl.BlockSpec((None, 2, TAB), lambda hp: (hp, 0, 0)),
                   pl.BlockSpec((None, 2, ATTN_HEAD_DIM), lambda hp: (hp, 0, 0))],
        out_shape=[_sds((T, AW), BF16), _sds((T, AW), BF16), _sds((T, AW), BF16),
                   _sds((HP, 2, TAB), F32), _sds((HP, 2, ATTN_HEAD_DIM), F32)],
        scratch_shapes=accs(BF16) + accs(BF16) + accs(F32) + accs(F32) + [pltpu.VMEM((QBLK, WIN), F32)] * 4,
        compiler_params=_params(("parallel",), big=True),
    )(proj, proj, proj, dmixin, gv, ga)


def _ltri():
    r = lax.broadcasted_iota(jnp.int32, (CHUNK, CHUNK), 0)
    c = lax.broadcasted_iota(jnp.int32, (CHUNK, CHUNK), 1)
    return (c <= r).astype(BF16)


def _tri_dot(tri, v, dims):
    hi = v.astype(BF16)
    lo = (v - hi.astype(F32)).astype(BF16)
    return _dot(tri, hi, dims) + _dot(tri, lo, dims)


HEADS_PER_STEP = 2


def _hgrn_gates(n, ls, q_ref, f_ref, lb_ref, ltri):
    r0 = pl.multiple_of(n * CHUNK, CHUNK)
    rows = pl.ds(r0, CHUNK)
    lb = lb_ref[:, ls]
    qb = q_ref[rows, ls]
    sg = _sigmoid(f_ref[rows, ls])
    f = lb + (1.0 - lb) * sg
    sq = _sigmoid(qb)
    b = _tri_dot(ltri, jnp.log(f), NN)
    return rows, lb, qb, sg, f, 1.0 - f, sq, qb * sq, b


def _hgrn_specs(T, RW, AW):
    HG = HEADS_PER_STEP
    W = HG * LANE
    base = 3 * AW // W
    blk_in = lambda off: pl.BlockSpec((T, W), lambda g: (0, base + off + g))
    col = pl.BlockSpec((T, W), lambda g: (0, g))
    return HG, W, RW // W, blk_in, col


def _hgrn_fwd(proj, lb, gn, AW, RW):
    T = proj.shape[0]
    RH, NC, NSUB = RW // LANE, T // CHUNK, CHUNK // SUB
    HG, W, NG, blk_in, col = _hgrn_specs(T, RW, AW)

    def body(q_ref, f_ref, i_ref, g_ref, lb_ref, gn_ref, mix_ref, o_ref, stall_ref, st_all, bs_all, kks_all, ics_all):
        st_all[...] = jnp.zeros_like(st_all)
        ltri = _ltri()
        rowi = lax.broadcasted_iota(jnp.int32, (SUB, 1), 0)

        def one_head(h, n):
            ls = slice(h * LANE, (h + 1) * LANE)
            st, bs, kks, ics = st_all.at[h], bs_all.at[h], kks_all.at[h], ics_all.at[h]
            rows, _, _, _, _, kk, _, qs, b = _hgrn_gates(n, ls, q_ref, f_ref, lb_ref, ltri)
            ic = i_ref[rows, ls]
            stv = st[...]
            stall_ref[h, n] = stv
            bs[...] = b
            kks[...] = kk
            ics[...] = ic
            o = _dot((qs * jnp.exp(b)).astype(BF16), stv.astype(BF16), NT)
            ic_b = ic.astype(BF16)
            pieces = []
            for blk in range(NSUB):
                s0 = blk * SUB
                bI, qI = b[s0:s0 + SUB], qs[s0:s0 + SUB]
                if blk == 0:
                    oI = jnp.zeros((SUB, LANE), F32)
                else:
                    ref = bs[s0 - 1:s0, :]
                    qt = (qI * jnp.exp(bI - ref)).astype(BF16)
                    kt = (kk[0:s0] * jnp.exp(ref - b[0:s0])).astype(BF16)
                    oI = _dot(_dot(qt, kt, NT).astype(BF16), ic_b[0:s0], NN)
                acc = [oI[g * ROWS:(g + 1) * ROWS] for g in range(SUB // ROWS)]
                for s in range(SUB):
                    sr = s0 + s
                    g0 = s // ROWS
                    lo = g0 * ROWS
                    e = jnp.exp(jnp.minimum(bI[lo:] - bs[sr:sr + 1, :], 0.0))
                    a = jnp.sum(qI[lo:] * kks[sr:sr + 1, :] * e, axis=-1, keepdims=True)
                    add = jnp.where(rowi[lo:] >= s, a, 0.0) * ics[sr:sr + 1, :]
                    for g in range(g0, SUB // ROWS):
                        acc[g] = acc[g] + add[(g - g0) * ROWS:(g - g0 + 1) * ROWS]
                pieces.extend(acc)
            o = o + jnp.concatenate(pieces, axis=0)
            bl = bs[CHUNK - 1:CHUNK, :]
            kd = (kk * jnp.exp(bl - b)).astype(BF16)
            st[...] = stv * jnp.exp(bl) + _dot(ic_b, kd, TN)
            o_ref[rows, ls] = o
            r = lax.rsqrt(jnp.mean(o * o, axis=-1, keepdims=True) + EPS)
            gb = g_ref[rows, ls]
            mix_ref[rows, ls] = (o * r * gn_ref[...] * (gb * _sigmoid(gb))).astype(BF16)

        def chunk(n, carry):
            for h in range(HG):
                one_head(h, n)
            return carry

        lax.fori_loop(0, NC, chunk, 0)

    tile = pltpu.VMEM((HG, CHUNK, LANE), F32)
    return pl.pallas_call(
        body, grid=(NG,), name="hgrn_fwd",
        in_specs=[blk_in(0), blk_in(NG), blk_in(2 * NG), blk_in(3 * NG), pl.BlockSpec((1, W), lambda g: (0, g)),
                  pl.BlockSpec((1, LANE), lambda g: (0, 0))],
        out_specs=[col, col, pl.BlockSpec((HG, NC, LANE, LANE), lambda g: (g, 0, 0, 0))],
        out_shape=[_sds((T, RW), BF16), _sds((T, RW), F32), _sds((RH, NC, LANE, LANE), F32)],
        scratch_shapes=[pltpu.VMEM((HG, LANE, LANE), F32), tile, tile, tile],
        compiler_params=_params(("parallel",), big=True),
    )(proj, proj, proj, proj, lb, gn)


def _hgrn_bwd(proj, dmixin, o_b, st_all, lb, gn, AW, RW):
    T = proj.shape[0]
    RH, NC, NSUB = RW // LANE, T // CHUNK, CHUNK // SUB
    HG, W, NG, blk_in, col = _hgrn_specs(T, RW, AW)

    def body(q_ref, f_ref, i_ref, g_ref, o_ref, dn_ref, stall_ref, lb_ref, gn_ref,
             dq_ref, df_ref, di_ref, dg_ref, dlb_ref, dgn_ref, dst_all, bs_all, kks_all, ics_all, p2_all, dic_all):
        dst_all[...] = jnp.zeros_like(dst_all)
        dlb_ref[...] = jnp.zeros_like(dlb_ref)
        dgn_ref[...] = jnp.zeros_like(dgn_ref)
        ltri = _ltri()
        rowi = lax.broadcasted_iota(jnp.int32, (SUB, 1), 0)
        last = lax.broadcasted_iota(jnp.int32, (CHUNK, 1), 0) == CHUNK - 1

        def one_head(h, n):
            ls = slice(h * LANE, (h + 1) * LANE)
            dst, bs, kks, ics = dst_all.at[h], bs_all.at[h], kks_all.at[h], ics_all.at[h]
            p2, dic = p2_all.at[h], dic_all.at[h]
            rows, lbv, qb, sg, f, kk, sq, qs, b = _hgrn_gates(n, ls, q_ref, f_ref, lb_ref, ltri)
            ic = i_ref[rows, ls]
            stv = stall_ref[h, n]
            dstv = dst[...]
            o = o_ref[rows, ls]
            dn = dn_ref[rows, ls]
            gb = g_ref[rows, ls]
            sgb = _sigmoid(gb)
            r = lax.rsqrt(jnp.mean(o * o, axis=-1, keepdims=True) + EPS)
            gnv = gn_ref[...]
            dg_ref[rows, ls] = (dn * (o * r * gnv) * (sgb * (1.0 + gb * (1.0 - sgb)))).astype(BF16)
            dy = dn * (gb * sgb)
            dgn_ref[h] += _colsum(dy * o * r)
            a_ = dy * gnv
            do = r * (a_ - o * (r * r) * jnp.mean(a_ * o, axis=-1, keepdims=True))
            do_b = do.astype(BF16)
            bs[...] = b
            kks[...] = kk
            ics[...] = ic
            ic_b = ic.astype(BF16)
            eb = jnp.exp(b)
            bl = bs[CHUNK - 1:CHUNK, :]
            ebl = jnp.exp(bl)
            dec = jnp.exp(bl - b)
            kd = (kk * dec).astype(BF16)
            dst_b = dstv.astype(BF16)
            dqs = _dot(do_b, stv.astype(BF16), NN) * eb
            dkk2 = _dot(ic_b, dst_b, NN) * dec
            dic[...] = _dot(kd, dst_b, NT)
            dbl = ebl * _colsum(stv * dstv) + _colsum(kk * dkk2)
            dst[...] = dstv * ebl + _dot(do_b, (qs * eb).astype(BF16), TN)
            p2[...] = jnp.zeros_like(p2)
            p1_pieces = []
            for blk in range(NSUB):
                s0 = blk * SUB
                bI, qI, doI = b[s0:s0 + SUB], qs[s0:s0 + SUB], do[s0:s0 + SUB]
                if blk == 0:
                    p1 = jnp.zeros((SUB, LANE), F32)
                else:
                    ref = bs[s0 - 1:s0, :]
                    eq = jnp.exp(bI - ref)
                    ek = jnp.exp(ref - b[0:s0])
                    qt = (qI * eq).astype(BF16)
                    kt = (kk[0:s0] * ek).astype(BF16)
                    doI_b = doI.astype(BF16)
                    dic[0:s0, :] += _dot(_dot(qt, kt, NT).astype(BF16), doI_b, TN)
                    da = _dot(doI_b, ic_b[0:s0], NT).astype(BF16)
                    p1 = _dot(da, kt, NN) * eq
                    p2[0:s0, :] += _dot(da, qt, TN) * ek
                acc = [p1[g * ROWS:(g + 1) * ROWS] for g in range(SUB // ROWS)]
                for s in range(SUB):
                    sr = s0 + s
                    g0 = s // ROWS
                    lo = g0 * ROWS
                    keep = rowi[lo:] >= s
                    kk_s = kks[sr:sr + 1, :]
                    e = jnp.exp(jnp.minimum(bI[lo:] - bs[sr:sr + 1, :], 0.0))
                    w = qI[lo:] * e
                    a = jnp.where(keep, jnp.sum(w * kk_s, axis=-1, keepdims=True), 0.0)
                    da_s = jnp.where(keep, jnp.sum(doI[lo:] * ics[sr:sr + 1, :], axis=-1, keepdims=True), 0.0)
                    add = da_s * kk_s * e
                    for g in range(g0, SUB // ROWS):
                        acc[g] = acc[g] + add[(g - g0) * ROWS:(g - g0 + 1) * ROWS]
                    p2[sr:sr + 1, :] += _colsum(da_s * w)
                    dic[sr:sr + 1, :] += _colsum(a * doI[lo:])
                p1_pieces.extend(acc)
            dqs = dqs + jnp.concatenate(p1_pieces, axis=0)
            dkk = dkk2 + p2[...]
            db = qs * dqs - kk * dkk + jnp.where(last, dbl, 0.0)
            dgl = _tri_dot(ltri, db, TN)
            dfv = dgl / f - dkk
            df_ref[rows, ls] = (dfv * (1.0 - lbv) * sg * (1.0 - sg)).astype(BF16)
            dlb_ref[:, ls] += _colsum(dfv * (1.0 - sg))
            dq_ref[rows, ls] = (dqs * (sq * (1.0 + qb * (1.0 - sq)))).astype(BF16)
            di_ref[rows, ls] = dic[...].astype(BF16)

        def chunk(k, carry):
            for h in range(HG):
                one_head(h, NC - 1 - k)
            return carry

        lax.fori_loop(0, NC, chunk, 0)

    tile = pltpu.VMEM((HG, CHUNK, LANE), F32)
    return pl.pallas_call(
        body, grid=(NG,), name="hgrn_bwd",
        in_specs=[blk_in(0), blk_in(NG), blk_in(2 * NG), blk_in(3 * NG), col,
                  pl.BlockSpec((T, W), lambda g: (0, AW // W + g)),
                  pl.BlockSpec((HG, NC, LANE, LANE), lambda g: (g, 0, 0, 0)),
                  pl.BlockSpec((1, W), lambda g: (0, g)), pl.BlockSpec((1, LANE), lambda g: (0, 0))],
        out_specs=[col, col, col, col, pl.BlockSpec((1, W), lambda g: (0, g)),
                   pl.BlockSpec((HG, 1, LANE), lambda g: (g, 0, 0))],
        out_shape=[_sds((T, RW), BF16)] * 4 + [_sds((1, RW), F32), _sds((RH, 1, LANE), F32)],
        scratch_shapes=[pltpu.VMEM((HG, LANE, LANE), F32), tile, tile, tile, tile, tile],
        compiler_params=_params(("parallel",), big=True),
    )(proj, proj, proj, proj, o_b, dmixin, st_all, lb, gn)


def _prep(c, lb_logits, rb_pad, max_rel):
    D, RW = c.shape[-1], lb_logits.shape[-1]
    H, rbp = rb_pad.shape

    def body(c_ref, l_ref, rb_ref, cact_ref, lb_ref, gv_ref):
        cv = c_ref[...]
        cact_ref[...] = cv * _sigmoid(cv)
        lb_ref[...] = _sigmoid(l_ref[0:1, :] - l_ref[1:2, :])
        gv_ref[...] = _dot(rb_ref[...], _bias_onehot(rbp, max_rel), NN, HIGHEST)

    return pl.pallas_call(
        body, name="prep", out_shape=[_sds((1, D), F32), _sds((1, RW), F32), _sds((H, TAB), F32)],
    )(c, lb_logits, rb_pad)


def _mod_part(c_all, w_ada_s, b_ada_s):
    B, D = c_all.shape
    ns = w_ada_s.shape[1]
    tn = _tile(ns, 768, LANE)

    def body(c_ref, w_ref, b_ref, o_ref):
        o_ref[...] = _dot(c_ref[...], w_ref[...], NN) + b_ref[...]

    return pl.pallas_call(
        body, grid=(ns // tn,), name="mod_part",
        in_specs=[pl.BlockSpec((B, D), lambda j: (0, 0)), pl.BlockSpec((D, tn), lambda j: (0, j)),
                  pl.BlockSpec((1, tn), lambda j: (0, j))],
        out_specs=pl.BlockSpec((B, tn), lambda j: (0, j)),
        out_shape=_sds((B, ns), F32), compiler_params=_params(("parallel",)),
    )(c_all, w_ada_s, b_ada_s)


def _adam(w, g, m, v):
    m = ADAM_B1 * m + (1.0 - ADAM_B1) * g
    v = ADAM_B2 * v + (1.0 - ADAM_B2) * (g * g)
    m_hat = m / (1.0 - ADAM_B1 ** ADAM_STEP)
    v_hat = v / (1.0 - ADAM_B2 ** ADAM_STEP)
    return -ADAM_LR * (m_hat / (jnp.sqrt(v_hat) + ADAM_EPS) + ADAM_WD * w), m, v


def _adam_ada(c_all, dmod_s, w, m, v):
    B, D = c_all.shape
    ns = w.shape[1]
    tr, tn = _tile(D, 512, LANE), _tile(ns, 768, LANE)

    def body(c_ref, d_ref, w_ref, m_ref, v_ref, g_out, dw_out, m_out, v_out):
        g = _dot(c_ref[...], d_ref[...], TN)
        g_out[...] = g
        dw_out[...], m_out[...], v_out[...] = _adam(w_ref[...], g, m_ref[...], v_ref[...])

    big = pl.BlockSpec((tr, tn), lambda i, j: (i, j))
    return pl.pallas_call(
        body, grid=(D // tr, ns // tn), name="adam_w_ada",
        in_specs=[pl.BlockSpec((B, tr), lambda i, j: (0, i)), pl.BlockSpec((B, tn), lambda i, j: (0, j)),
                  big, big, big],
        out_specs=[big] * 4, out_shape=[_sds((D, ns), F32)] * 4,
        compiler_params=_params(("parallel", "parallel")),
    )(c_all, dmod_s, w, m, v)


def _adam_shard(parts, w, m, v, name):
    R, C = w.shape
    tr = _tile(R, 256, 16)

    def body(p_ref, w_ref, m_ref, v_ref, g_out, dw_out, m_out, v_out):
        g = p_ref[0].astype(F32)
        for k in range(1, N_DEV // 2):
            g = g + p_ref[k].astype(F32)
        g_out[...] = g
        dw_out[...], m_out[...], v_out[...] = _adam(w_ref[...], g, m_ref[...], v_ref[...])

    big = pl.BlockSpec((tr, C), lambda i: (i, 0))
    return pl.pallas_call(
        body, grid=(R // tr,), name=name,
        in_specs=[pl.BlockSpec((N_DEV // 2, tr, C), lambda i: (0, i, 0)), big, big, big],
        out_specs=[big] * 4, out_shape=[_sds((R, C), F32)] * 4,
        compiler_params=_params(("parallel",), big=True),
    )(parts, w, m, v)


def _pair_sum(g8, land, core, name):
    _, NCHIP, R, C = g8.shape
    tr = _tile(R, 1024, 16)

    def body(core_ref, g_ref, l_ref, o_ref):
        o_ref[...] = (g_ref[...].astype(F32) + l_ref[...].astype(F32)).astype(BF16)

    return pl.pallas_call(
        body, name=name,
        grid_spec=pltpu.PrefetchScalarGridSpec(
            num_scalar_prefetch=1, grid=(NCHIP, R // tr),
            in_specs=[pl.BlockSpec((None, None, tr, C), lambda k, i, core_ref: (core_ref[0], k, i, 0)),
                      pl.BlockSpec((None, tr, C), lambda k, i, core_ref: (k, i, 0))],
            out_specs=pl.BlockSpec((None, tr, C), lambda k, i, core_ref: (k, i, 0))),
        out_shape=_sds((NCHIP, R, C), BF16), compiler_params=_params(("parallel", "parallel")),
    )(core, g8, land)


SMALL = ("b_ada", "rel_bias", "attn_norm_g", "lb_logits", "gnorm_g", "ln1_g", "ln1_b", "ln2_g", "ln2_b")


def _small_update(parts, loss_parts, lbv, ws, ms, vs, max_rel):
    n = len(SMALL)

    def body(*refs):
        part_refs = dict(zip(SMALL, refs[:n]))
        loss_in, lb_ref = refs[n], refs[n + 1]
        w_refs, m_refs, v_refs = refs[n + 2:2 * n + 2], refs[2 * n + 2:3 * n + 2], refs[3 * n + 2:4 * n + 2]
        outs = refs[4 * n + 2:]

        def total(ref):
            tot = ref[0]
            for k in range(1, N_DEV):
                tot = tot + ref[k]
            return tot

        outs[0][...] = jnp.sum(total(loss_in), axis=-1, keepdims=True)
        for idx, name in enumerate(SMALL):
            g = total(part_refs[name])
            if name == "rel_bias":
                g = _dot(g, _bias_onehot(w_refs[idx].shape[1], max_rel), NT, HIGHEST)
            elif name == "lb_logits":
                lb = lb_ref[...]
                sign = (1 - 2 * lax.broadcasted_iota(jnp.int32, (2, 1), 0)).astype(F32)
                g = sign * (g * lb * (1.0 - lb))
            elif name == "gnorm_g":
                g = _colsum(g)
            dw, mm, vv = _adam(w_refs[idx][...], g, m_refs[idx][...], v_refs[idx][...])
            outs[1 + 4 * idx][...] = g
            outs[2 + 4 * idx][...] = dw
            outs[3 + 4 * idx][...] = mm
            outs[4 + 4 * idx][...] = vv

    out_shape = [_sds((1, 1), F32)]
    for w in ws:
        out_shape += [_sds(w.shape, F32)] * 4
    return pl.pallas_call(body, name="small_update", out_shape=out_shape, compiler_params=_params(big=True))(
        *[parts[k] for k in SMALL], loss_parts, lbv, *ws, *ms, *vs)


def _place():
    x, y, c = lax.axis_index("x"), lax.axis_index("y"), lax.axis_index("c")
    return x, y, c, [(1 - x, y), (x, 1 - y), (1 - x, 1 - y)]


def _all_gather(shard, name):
    HBM = pl.BlockSpec(memory_space=pl.ANY)

    def body(x_ref, out_ref, send_sems, recv_sems, local_sem):
        x, y, c, chips = _place()
        me, sibling = (x, y, c), (x, y, 1 - c)

        def slot(px, py, pc):
            return out_ref.at[4 * px + 2 * py + pc]

        def copy(k, block, to, src=None):
            return pltpu.make_async_remote_copy(
                src_ref=slot(*block) if src is None else src, dst_ref=slot(*block),
                send_sem=send_sems.at[k], recv_sem=recv_sems.at[k], device_id=to, device_id_type=MESH)

        mine = pltpu.make_async_copy(x_ref, slot(*me), local_sem)
        mine.start()
        first = [copy(0, me, sibling, src=x_ref)]
        first += [copy(1 + j, me, (*chip, c), src=x_ref) for j, chip in enumerate(chips)]
        for cp in first:
            cp.start()
        passed = [copy(4 + j, (*chip, c), sibling) for j, chip in enumerate(chips)]
        for j, chip in enumerate(chips):
            copy(1 + j, (*chip, c), me).wait_recv()
            passed[j].start()
        copy(0, sibling, me).wait_recv()
        for j, chip in enumerate(chips):
            copy(4 + j, (*chip, 1 - c), me).wait_recv()
        for cp in first + passed:
            cp.wait_send()
        mine.wait()

    return pl.pallas_call(
        body, name=name, out_shape=_sds((N_DEV,) + shard.shape, shard.dtype),
        in_specs=[HBM], out_specs=HBM,
        scratch_shapes=[pltpu.SemaphoreType.DMA((7,)), pltpu.SemaphoreType.DMA((7,)), pltpu.SemaphoreType.DMA(())],
    )(shard)


SEM_SPEC = pl.BlockSpec(memory_space=pltpu.SEMAPHORE)
HBM_SPEC = pl.BlockSpec(memory_space=pltpu.HBM)
EFFECT = pltpu.SideEffectType.DATAFLOW_SIDE_EFFECTING


def _remote(src, dst, send_sems, recv_sems, k, dev):
    return pltpu.make_async_remote_copy(src_ref=src, dst_ref=dst, send_sem=send_sems.at[k], recv_sem=recv_sems.at[k],
                                        device_id=dev, device_id_type=MESH)


def _copy_start(name, bufs, plan, n, after):
    nb = len(bufs)

    def body(*refs):
        send_sems, recv_sems = refs[nb + 1], refs[nb + 2]
        for k, (src, dst, dev) in enumerate(plan(*refs[:nb])):
            _remote(src, dst, send_sems, recv_sems, k, dev).start()
        refs[-1][...] = jnp.zeros_like(refs[-1])

    out = pl.pallas_call(
        body, name=name,
        out_shape=(pltpu.SemaphoreType.DMA((n,)), pltpu.SemaphoreType.DMA((n,)),
                   *[pltpu.HBM(b.shape, b.dtype) for b in bufs], _sds((8, LANE), F32)),
        in_specs=[HBM_SPEC] * nb + [ORDER_ONLY],
        out_specs=(SEM_SPEC, SEM_SPEC, *[HBM_SPEC] * nb, pl.BlockSpec(memory_space=pltpu.VMEM)),
        input_output_aliases={i: 2 + i for i in range(nb)},
        compiler_params=pltpu.CompilerParams(has_side_effects=EFFECT),
    )(*[pltpu.with_memory_space_constraint(b, pltpu.HBM) for b in bufs], after)
    return (out[0], out[1]), list(out[2:2 + nb]), out[-1]


def _copy_wait(name, sems, bufs, plan, after):
    nb = len(bufs)

    def body(*refs):
        send_sems, recv_sems = refs[nb], refs[nb + 1]
        for k, (src, dst, dev) in enumerate(plan(*refs[:nb])):
            cp = _remote(src, dst, send_sems, recv_sems, k, dev)
            cp.wait_send()
            cp.wait_recv()

    out = pl.pallas_call(
        body, name=name, out_shape=tuple(pltpu.HBM(b.shape, b.dtype) for b in bufs),
        in_specs=[HBM_SPEC] * nb + [SEM_SPEC, SEM_SPEC, pl.BlockSpec(memory_space=pl.ANY)],
        out_specs=tuple([HBM_SPEC] * nb), input_output_aliases={i: i for i in range(nb)},
        compiler_params=pltpu.CompilerParams(has_side_effects=EFFECT),
    )(*bufs, sems[0], sems[1], after)
    return list(out)


def _ag_plan_chips(shard_ref, out_ref):
    x, y, c, chips = _place()
    mine = out_ref.at[4 * x + 2 * y + c]
    return [(shard_ref, mine, (x, y, 1 - c))] + [(shard_ref, mine, (*chip, c)) for chip in chips]


def _ag_plan_pass(out_ref):
    x, y, c, chips = _place()
    slots = [out_ref.at[4 * chip[0] + 2 * chip[1] + c] for chip in chips]
    return [(s, s, (x, y, 1 - c)) for s in slots]


def _rs_plan_pair(g_ref, land_ref):
    x, y, c, _ = _place()
    return [(g_ref.at[1 - c], land_ref, (x, y, 1 - c))]


def _rs_plan_chips(p_ref, land_ref):
    x, y, c, chips = _place()
    return [(p_ref.at[2 * chip[0] + chip[1]], land_ref.at[2 * x + y], (*chip, c)) for chip in chips]


class _Gather:
    def __init__(self, shard, me, tag, after):
        self.tag = tag
        out = lax.dynamic_update_slice(lax.empty((N_DEV,) + shard.shape, shard.dtype), shard[None],
                                       (me,) + (0,) * shard.ndim)
        self.sems, (self.shard, self.out), self.token = _copy_start(
            "ag_start_" + tag, [shard, out], _ag_plan_chips, 4, after)

    def arrived_from_chips(self, after):
        _, out = _copy_wait("ag_wait_" + self.tag, self.sems, [self.shard, self.out], _ag_plan_chips, after)
        self.sems, (self.out,), _ = _copy_start("ag_pass_" + self.tag, [out], _ag_plan_pass, 3, after)

    def passed_on(self, after):
        return _copy_wait("ag_pass_wait_" + self.tag, self.sems, [self.out], _ag_plan_pass, after)[0]


class _ReduceScatter:
    def __init__(self, g8, tag):
        self.tag = tag
        land = lax.empty(g8.shape[1:], g8.dtype)
        self.sems, self.bufs, self.token = _copy_start(
            "rs_pair_start_" + tag, [g8, land], _rs_plan_pair, 1, jnp.zeros((1,), F32))

    def pair_done(self, core, chip, after, start_after=None):
        g8, land = _copy_wait("rs_pair_wait_" + self.tag, self.sems, self.bufs, _rs_plan_pair, after)
        p4 = _pair_sum(g8, land, core, "rs_pair_sum_" + self.tag)
        own = lax.dynamic_slice_in_dim(p4, chip, 1, axis=0)
        land2 = lax.dynamic_update_slice(lax.empty(p4.shape, p4.dtype), own, (chip, 0, 0))
        self.sems, self.bufs, self.token = _copy_start(
            "rs_chips_start_" + self.tag, [p4, land2], _rs_plan_chips, 3,
            jnp.zeros((1,), F32) if start_after is None else start_after)

    def sums(self, after):
        return _copy_wait("rs_chips_wait_" + self.tag, self.sems, self.bufs, _rs_plan_chips, after)[1]


BIG = ("w_in", "w_o", "w_ffn_in", "w_ffn_out")
ORDER = ("w_ada", "b_ada", "w_in", "rel_bias", "attn_norm_g", "lb_logits", "gnorm_g", "w_o", "ln1_g", "ln1_b",
         "w_ffn_in", "w_ffn_out", "ln2_g", "ln2_b")


def kernel(x, c, w_ada, b_ada, w_in, rel_bias, attn_norm_g, lb_logits, gnorm_g, w_o, ln1_g, ln1_b, w_ffn_in, w_ffn_out, ln2_g, ln2_b, loss_target, m_w_ada, m_b_ada, m_w_in, m_rel_bias, m_attn_norm_g, m_lb_logits, m_gnorm_g, m_w_o, m_ln1_g, m_ln1_b, m_w_ffn_in, m_w_ffn_out, m_ln2_g, m_ln2_b, v_w_ada, v_b_ada, v_w_in, v_rel_bias, v_attn_norm_g, v_lb_logits, v_gnorm_g, v_w_o, v_ln1_g, v_ln1_b, v_w_ffn_in, v_w_ffn_out, v_ln2_g, v_ln2_b):
    W = dict(w_ada=w_ada, b_ada=b_ada, w_in=w_in, rel_bias=rel_bias, attn_norm_g=attn_norm_g, lb_logits=lb_logits,
             gnorm_g=gnorm_g, w_o=w_o, ln1_g=ln1_g, ln1_b=ln1_b, w_ffn_in=w_ffn_in, w_ffn_out=w_ffn_out,
             ln2_g=ln2_g, ln2_b=ln2_b)
    M = dict(w_ada=m_w_ada, b_ada=m_b_ada, w_in=m_w_in, rel_bias=m_rel_bias, attn_norm_g=m_attn_norm_g,
             lb_logits=m_lb_logits, gnorm_g=m_gnorm_g, w_o=m_w_o, ln1_g=m_ln1_g, ln1_b=m_ln1_b,
             w_ffn_in=m_w_ffn_in, w_ffn_out=m_w_ffn_out, ln2_g=m_ln2_g, ln2_b=m_ln2_b)
    V = dict(w_ada=v_w_ada, b_ada=v_b_ada, w_in=v_w_in, rel_bias=v_rel_bias, attn_norm_g=v_attn_norm_g,
             lb_logits=v_lb_logits, gnorm_g=v_gnorm_g, w_o=v_w_o, ln1_g=v_ln1_g, ln1_b=v_ln1_b,
             w_ffn_in=v_w_ffn_in, w_ffn_out=v_w_ffn_out, ln2_g=v_ln2_g, ln2_b=v_ln2_b)

    x2, tgt = x[0], loss_target[0]
    T, D = x2.shape
    AW, RW = attn_norm_g.shape[-1], lb_logits.shape[-1]
    MIX = AW + RW
    H, RH = AW // ATTN_HEAD_DIM, RW // LANE
    RB = rel_bias.shape[-1]
    max_rel = (RB - 1) // 2
    rbp = -(-RB // LANE) * LANE
    F = w_ffn_out.shape[1] * N_DEV
    half = N_DEV // 2
    xi, yi, ci = lax.axis_index("x"), lax.axis_index("y"), lax.axis_index("c")
    me = 4 * xi + 2 * yi + ci
    core = jnp.reshape(ci, (1,)).astype(jnp.int32)
    pad_rb = lambda a: jnp.pad(a[0], ((0, 0), (0, rbp - RB)))

    chip = 2 * xi + yi

    ag_in = _Gather(w_in[0].astype(BF16), me, "w_in", core)

    c_act, lbv, gv = _prep(c + ag_in.token[0:1, 0:1], lb_logits, pad_rb(rel_bias), max_rel)
    c_all = _all_gather(c_act, "ag_c").reshape(N_DEV, D)
    ns_ada = w_ada.shape[-1]
    mod_part = _mod_part(c_all, w_ada[0], lax.dynamic_slice_in_dim(b_ada, me * ns_ada, ns_ada, axis=1))
    mod_all = _all_gather(mod_part, "ag_mod")
    mod6 = lax.dynamic_index_in_dim(mod_all, me, axis=1, keepdims=False).reshape(6, D)

    ag_o = _Gather(w_o[0].astype(BF16), me, "w_o", mod_all)
    ag_f1 = _Gather(w_ffn_in[0].astype(BF16), me, "w_ffn_in", ag_o.token)
    ag_f2 = _Gather(w_ffn_out[0].astype(BF16), me, "w_ffn_out", ag_f1.token)

    h1 = _ln_mod(x2, mod6 + ag_f2.token[0, 0])
    ag_in.arrived_from_chips(h1)
    wg_in = ag_in.passed_on(h1)
    proj = _mm_gathered(h1, wg_in, "in_proj")
    ag_o.arrived_from_chips(proj)
    mix_a = _attn_fwd(proj, gv, attn_norm_g, AW)
    wg_o = ag_o.passed_on(mix_a).reshape(MIX, D)
    mix_b, o_b, st_all = _hgrn_fwd(proj, lbv, gnorm_g, AW, RW)
    ag_f1.arrived_from_chips(mix_b)
    mixin = jnp.concatenate([mix_a, mix_b], axis=1)
    mix = _mm_nn(mixin, wg_o, "out_proj")
    x1, h2 = _mid_fwd(x2, mix, mod6, ln1_g, ln1_b)
    wg_f1 = ag_f1.passed_on(h2)
    gu, act = _mm_swiglu(h2, wg_f1)
    ag_f2.arrived_from_chips(act)
    wg_f2 = ag_f2.passed_on(act).reshape(F, D)
    ff = _mm_nn(act, wg_f2, "ffn_out")
    dff, dx1a, vec_a = _final(x1, ff, mod6, ln2_g, ln2_b, tgt)

    du = _mm_swiglu_bwd(dff, wg_f2, gu)
    rs_f2 = _ReduceScatter(_mm_tn_rows(dff, act, dff, F // N_DEV, "grad_w_ffn_out"), "w_ffn_out")
    tm = _tile(T, 512, 16)
    du_ij = lambda tm_, ns: pl.BlockSpec((None, tm_, ns), lambda i, j: (j // half, i, j % half))
    du_jm = lambda tm_, ns: pl.BlockSpec((None, tm_, ns), lambda j, m: (j // half, m, j % half))
    dh2 = _mm_gathered_nt(rs_f2.token, du, du_ij, wg_f1, T, tm, "ffn_in_bwd")
    rs_f2.pair_done(core, chip, dh2)
    tm_red = _tile(T, 1024, 16)
    gw_f1 = _mm_tn_gathered(rs_f2.token, h2, du, du_jm, wg_f1.shape[-1], tm_red, "grad_w_ffn_in")
    rs_f1 = _ReduceScatter(gw_f1.reshape(2, half, D, -1), "w_ffn_in")
    dmix, dxa, vec_b = _mid_bwd(x2, mix, x1, dx1a, dh2, mod6 + rs_f1.token[0, 0], ln1_g)
    dmixin = _mm_nt(dmix, wg_o, "out_proj_bwd")
    rs_f1.pair_done(core, chip, dmixin)
    rs_o = _ReduceScatter(_mm_tn_rows(rs_f1.token, mixin, dmix, MIX // N_DEV, "grad_w_o"), "w_o")
    dq, dk, dv, dgv, dga = _attn_bwd(proj, dmixin, gv + rs_o.token[0, 0], attn_norm_g, AW)
    rs_o.pair_done(core, chip, dq)
    dqb, dfl, dib, dgb, dlb, dgn = _hgrn_bwd(proj, dmixin, o_b, st_all, lbv + rs_o.token[0, 0], gnorm_g, AW, RW)
    dproj = jnp.concatenate([dq, dk, dv, dqb, dfl, dib, dgb], axis=1)
    p_ij = lambda tm_, ns: pl.BlockSpec((tm_, ns), lambda i, j: (i, j))
    p_jm = lambda tm_, ns: pl.BlockSpec((tm_, ns), lambda j, m: (m, j))
    gw_in = _mm_tn_gathered(rs_o.token, h1, dproj, p_jm, wg_in.shape[-1], tm_red, "grad_w_in")
    rs_in = _ReduceScatter(gw_in.reshape(2, half, D, -1), "w_in")
    dh1 = _mm_gathered_nt(rs_in.token, dproj, p_ij, wg_in, T, tm, "in_proj_bwd")
    grad_x, vec_c = _first_bwd(x2, dh1, dxa, mod6)

    dmod = jnp.concatenate([vec_c[1:2], vec_c[0:1], vec_b[4:5], vec_b[1:2], vec_b[0:1], vec_a[2:3]], axis=0)
    pieces = dict(b_ada=dmod, rel_bias=dgv, attn_norm_g=dga, lb_logits=dlb, gnorm_g=dgn, ln1_g=vec_b[2:3],
                  ln1_b=vec_b[3:4], ln2_g=vec_a[0:1], ln2_b=vec_a[1:2], loss=vec_a[3:4])
    widths = dict(b_ada=(1, 6 * D), rel_bias=(H, TAB), attn_norm_g=(1, AW), lb_logits=(1, RW), gnorm_g=(RH, LANE),
                  ln1_g=(1, D), ln1_b=(1, D), ln2_g=(1, D), ln2_b=(1, D), loss=(1, D))
    packed = jnp.concatenate([pieces[k].reshape(-1, LANE) for k in widths], axis=0)
    gathered = _all_gather(packed, "ag_small")
    rs_in.pair_done(core, chip, dh1, start_after=gathered)
    parts, r0 = {}, 0
    for k, (rows, width) in widths.items():
        nr = rows * width // LANE
        parts[k] = gathered[:, r0:r0 + nr, :].reshape(N_DEV, rows, width)
        r0 += nr
    prep_small = lambda d, k: pad_rb(d[k]) if k == "rel_bias" else d[k]
    small = _small_update(parts, parts["loss"], lbv, [prep_small(W, k) for k in SMALL],
                          [prep_small(M, k) for k in SMALL], [prep_small(V, k) for k in SMALL], max_rel)
    loss = small[0].reshape(())
    res = {}
    for idx, k in enumerate(SMALL):
        four = small[1 + 4 * idx:5 + 4 * idx]
        if k == "rel_bias":
            four = [a[:, :RB][None] for a in four]
        res[k] = list(four)

    dmod_s = lax.dynamic_slice_in_dim(parts["b_ada"].reshape(N_DEV, 6 * D), me * ns_ada, ns_ada, axis=1)
    dmod_s = dmod_s + rs_in.token[0, 0]
    res["w_ada"] = [a[None] for a in _adam_ada(c_all, dmod_s, w_ada[0], m_w_ada[0], v_w_ada[0])]
    after = res["w_ada"][0]
    for k, rs in (("w_ffn_out", rs_f2), ("w_ffn_in", rs_f1), ("w_o", rs_o), ("w_in", rs_in)):
        four = _adam_shard(rs.sums(after), W[k][0], M[k][0], V[k][0], "adam_" + k)
        res[k] = [a[None] for a in four]
        after = four[0]

    out = [loss, grad_x[None]]
    for field in range(4):
        out += [res[k][field] for k in ORDER]
    return tuple(out)
```

```python
import functools

import jax
import jax.numpy as jnp
from jax import lax
from jax.experimental import pallas as pl
from jax.experimental.pallas import tpu as pltpu

F32 = jnp.float32
BF16 = jnp.bfloat16
MESH = pl.DeviceIdType.MESH
HIGHEST = lax.Precision.HIGHEST

N_DEV = 8
CHUNK = 64
N_PAST = 8
QBLK = 4 * CHUNK
KPAD = N_PAST * CHUNK
WIN = KPAD + QBLK
TAB = 1024
ATTN_HEAD_DIM = 64
REC_HEAD_DIM = 128
SUB = 16
ROWS = 8
LANE = 128
EPS = 1e-5
ALPHA = 2.0 ** 0.25
ADAM_LR, ADAM_B1, ADAM_B2, ADAM_EPS, ADAM_WD, ADAM_STEP = 0.001, 0.9, 0.999, 1e-08, 0.01, 10
NEG = -1e30
VMEM_LIMIT = 56 * 1024 * 1024


def _sds(shape, dtype):
    return jax.ShapeDtypeStruct(tuple(shape), dtype)


def _tile(n, pref, mult):
    best = None
    for t in range(mult, min(n, pref) + 1, mult):
        if n % t == 0:
            best = t
    return n if best is None else best


def _params(sem=None, big=False):
    kw = {}
    if sem is not None:
        kw["dimension_semantics"] = sem
    if big:
        kw["vmem_limit_bytes"] = VMEM_LIMIT
    return pltpu.CompilerParams(**kw)


def _sigmoid(v):
    return 1.0 / (1.0 + jnp.exp(-v))


def _dot(a, b, dims, precision=None):
    return lax.dot_general(a, b, (dims, ((), ())), preferred_element_type=F32, precision=precision)


NN = ((1,), (0,))
NT = ((1,), (1,))
TN = ((0,), (0,))


def _ln(v):
    mu = jnp.mean(v, axis=-1, keepdims=True)
    d = v - mu
    rstd = lax.rsqrt(jnp.mean(d * d, axis=-1, keepdims=True) + EPS)
    return d * rstd, rstd


def _ln_bwd(dxh, xh, rstd):
    return rstd * (dxh - jnp.mean(dxh, axis=-1, keepdims=True) - xh * jnp.mean(dxh * xh, axis=-1, keepdims=True))


def _colsum(v):
    return jnp.sum(v, axis=0, keepdims=True)


def _ln_mod(x2, mod6):
    T, D = x2.shape
    tm = _tile(T, 256, 8)

    def body(x_ref, mod_ref, o_ref):
        xh, _ = _ln(x_ref[...])
        o_ref[...] = (xh * (1.0 + mod_ref[1:2, :]) + mod_ref[0:1, :]).astype(BF16)

    return pl.pallas_call(
        body, grid=(T // tm,), name="ln_mod",
        in_specs=[pl.BlockSpec((tm, D), lambda i: (i, 0)), pl.BlockSpec((6, D), lambda i: (0, 0))],
        out_specs=pl.BlockSpec((tm, D), lambda i: (i, 0)),
        out_shape=_sds((T, D), BF16), compiler_params=_params(("parallel",)),
    )(x2, mod6)


def _mid_fwd(x2, mix, mod6, ln1_g, ln1_b):
    T, D = x2.shape
    tm = _tile(T, 256, 8)

    def body(x_ref, mix_ref, mod_ref, g_ref, b_ref, x1_ref, h2_ref):
        zh, _ = _ln(ALPHA * x_ref[...] + mod_ref[2:3, :] * mix_ref[...])
        x1 = zh * g_ref[...] + b_ref[...]
        x1_ref[...] = x1
        xh, _ = _ln(x1)
        h2_ref[...] = (xh * (1.0 + mod_ref[4:5, :]) + mod_ref[3:4, :]).astype(BF16)

    row = pl.BlockSpec((tm, D), lambda i: (i, 0))
    vec = pl.BlockSpec((1, D), lambda i: (0, 0))
    return pl.pallas_call(
        body, grid=(T // tm,), name="mid_fwd",
        in_specs=[row, row, pl.BlockSpec((6, D), lambda i: (0, 0)), vec, vec],
        out_specs=[row, row],
        out_shape=[_sds((T, D), F32), _sds((T, D), BF16)], compiler_params=_params(("parallel",)),
    )(x2, mix, mod6, ln1_g, ln1_b)


def _final(x1, ff, mod6, ln2_g, ln2_b, tgt):
    T, D = x1.shape
    tm = _tile(T, 256, 8)

    def body(x1_ref, ff_ref, mod_ref, g_ref, b_ref, t_ref, dff_ref, dx1_ref, vec_ref):
        @pl.when(pl.program_id(0) == 0)
        def _():
            vec_ref[...] = jnp.zeros_like(vec_ref)

        ff_v = ff_ref[...]
        gate2 = mod_ref[5:6, :]
        zh, rstd = _ln(ALPHA * x1_ref[...] + gate2 * ff_v)
        err = zh * g_ref[...] + b_ref[...] - t_ref[...]
        dy = err * (1.0 / D)
        dz = _ln_bwd(dy * g_ref[...], zh, rstd)
        dff_ref[...] = (gate2 * dz).astype(BF16)
        dx1_ref[...] = ALPHA * dz
        vec_ref[0:1, :] += _colsum(dy * zh)
        vec_ref[1:2, :] += _colsum(dy)
        vec_ref[2:3, :] += _colsum(dz * ff_v)
        vec_ref[3:4, :] += _colsum(err * err) * (0.5 / D)

    row = pl.BlockSpec((tm, D), lambda i: (i, 0))
    vec = pl.BlockSpec((1, D), lambda i: (0, 0))
    return pl.pallas_call(
        body, grid=(T // tm,), name="final_fwd_bwd",
        in_specs=[row, row, pl.BlockSpec((6, D), lambda i: (0, 0)), vec, vec, row],
        out_specs=[row, row, pl.BlockSpec((8, D), lambda i: (0, 0))],
        out_shape=[_sds((T, D), BF16), _sds((T, D), F32), _sds((8, D), F32)],
        compiler_params=_params(("arbitrary",)),
    )(x1, ff, mod6, ln2_g, ln2_b, tgt)


def _mid_bwd(x2, mix, x1, dx1a, dh2, mod6, ln1_g):
    T, D = x2.shape
    tm = _tile(T, 256, 8)

    def body(x_ref, mix_ref, x1_ref, dx1a_ref, dh2_ref, mod_ref, g_ref, dmix_ref, dxa_ref, vec_ref):
        @pl.when(pl.program_id(0) == 0)
        def _():
            vec_ref[...] = jnp.zeros_like(vec_ref)

        dh2 = dh2_ref[...]
        xh, rstd = _ln(x1_ref[...])
        dx1 = dx1a_ref[...] + _ln_bwd(dh2 * (1.0 + mod_ref[4:5, :]), xh, rstd)
        mix_v = mix_ref[...]
        gate1 = mod_ref[2:3, :]
        zh, rstdz = _ln(ALPHA * x_ref[...] + gate1 * mix_v)
        dz = _ln_bwd(dx1 * g_ref[...], zh, rstdz)
        dmix_ref[...] = (gate1 * dz).astype(BF16)
        dxa_ref[...] = ALPHA * dz
        vec_ref[0:1, :] += _colsum(dh2 * xh)
        vec_ref[1:2, :] += _colsum(dh2)
        vec_ref[2:3, :] += _colsum(dx1 * zh)
        vec_ref[3:4, :] += _colsum(dx1)
        vec_ref[4:5, :] += _colsum(dz * mix_v)

    row = pl.BlockSpec((tm, D), lambda i: (i, 0))
    vec = pl.BlockSpec((1, D), lambda i: (0, 0))
    return pl.pallas_call(
        body, grid=(T // tm,), name="mid_bwd",
        in_specs=[row, row, row, row, row, pl.BlockSpec((6, D), lambda i: (0, 0)), vec],
        out_specs=[row, row, pl.BlockSpec((8, D), lambda i: (0, 0))],
        out_shape=[_sds((T, D), BF16), _sds((T, D), F32), _sds((8, D), F32)],
        compiler_params=_params(("arbitrary",)),
    )(x2, mix, x1, dx1a, dh2, mod6, ln1_g)


def _first_bwd(x2, dh1, dxa, mod6):
    T, D = x2.shape
    tm = _tile(T, 256, 8)

    def body(x_ref, dh1_ref, dxa_ref, mod_ref, gx_ref, vec_ref):
        @pl.when(pl.program_id(0) == 0)
        def _():
            vec_ref[...] = jnp.zeros_like(vec_ref)

        dh1 = dh1_ref[...]
        xh, rstd = _ln(x_ref[...])
        gx_ref[...] = dxa_ref[...] + _ln_bwd(dh1 * (1.0 + mod_ref[1:2, :]), xh, rstd)
        vec_ref[0:1, :] += _colsum(dh1 * xh)
        vec_ref[1:2, :] += _colsum(dh1)

    row = pl.BlockSpec((tm, D), lambda i: (i, 0))
    return pl.pallas_call(
        body, grid=(T // tm,), name="first_bwd",
        in_specs=[row, row, row, pl.BlockSpec((6, D), lambda i: (0, 0))],
        out_specs=[row, pl.BlockSpec((8, D), lambda i: (0, 0))],
        out_shape=[_sds((T, D), F32), _sds((8, D), F32)],
        compiler_params=_params(("arbitrary",)),
    )(x2, dh1, dxa, mod6)


def _slot(j):
    return (j % 2) * 4 + j // 2


def _mm_gathered(a, wg, name):
    M, K = a.shape
    _, _, ns = wg.shape
    tm = _tile(M, 512, 16)

    def body(a_ref, w_ref, o_ref):
        o_ref[...] = _dot(a_ref[...], w_ref[...], NN)

    return pl.pallas_call(
        body, grid=(N_DEV, M // tm), name=name,
        in_specs=[pl.BlockSpec((tm, K), lambda j, i: (i, 0)), pl.BlockSpec((None, K, ns), lambda j, i: (j, 0, 0))],
        out_specs=pl.BlockSpec((tm, ns), lambda j, i: (i, j)),
        out_shape=_sds((M, N_DEV * ns), F32), compiler_params=_params(("parallel", "parallel"), big=True),
    )(a, wg)


def _mm_nn(a, b, name):
    M, K = a.shape
    _, N = b.shape
    tm, tn, tk = _tile(M, 512, 16), _tile(N, 1024, LANE), _tile(K, 2048, LANE)

    def body(a_ref, b_ref, o_ref):
        @pl.when(pl.program_id(2) == 0)
        def _():
            o_ref[...] = jnp.zeros_like(o_ref)

        o_ref[...] += _dot(a_ref[...], b_ref[...], NN)

    return pl.pallas_call(
        body, grid=(M // tm, N // tn, K // tk), name=name,
        in_specs=[pl.BlockSpec((tm, tk), lambda i, j, k: (i, k)), pl.BlockSpec((tk, tn), lambda i, j, k: (k, j))],
        out_specs=pl.BlockSpec((tm, tn), lambda i, j, k: (i, j)),
        out_shape=_sds((M, N), F32), compiler_params=_params(("parallel", "parallel", "arbitrary"), big=True),
    )(a, b)


def _mm_nt(a, b, name):
    M, K = a.shape
    N, _ = b.shape
    tm, tn = _tile(M, 512, 16), _tile(N, 1024, LANE)

    def body(a_ref, b_ref, o_ref):
        o_ref[...] = _dot(a_ref[...], b_ref[...], NT)

    return pl.pallas_call(
        body, grid=(M // tm, N // tn), name=name,
        in_specs=[pl.BlockSpec((tm, K), lambda i, j: (i, 0)), pl.BlockSpec((tn, K), lambda i, j: (j, 0))],
        out_specs=pl.BlockSpec((tm, tn), lambda i, j: (i, j)),
        out_shape=_sds((M, N), F32), compiler_params=_params(("parallel", "parallel"), big=True),
    )(a, b)


def _mm_swiglu(h2, wg):
    M, K = h2.shape
    _, _, ns = wg.shape
    half = N_DEV // 2
    tm = _tile(M, 256, 16)

    def body(a_ref, wgate_ref, wup_ref, gu_ref, act_ref):
        a = a_ref[...]
        g = _dot(a, wgate_ref[...], NN)
        u = _dot(a, wup_ref[...], NN)
        gu_ref[0] = g
        gu_ref[1] = u
        act_ref[...] = (g * _sigmoid(g) * u).astype(BF16)

    return pl.pallas_call(
        body, grid=(half, M // tm), name="ffn_in_swiglu",
        in_specs=[pl.BlockSpec((tm, K), lambda j, i: (i, 0)),
                  pl.BlockSpec((None, K, ns), lambda j, i: (j, 0, 0)),
                  pl.BlockSpec((None, K, ns), lambda j, i: (j + half, 0, 0))],
        out_specs=[pl.BlockSpec((2, tm, ns), lambda j, i: (0, i, j)), pl.BlockSpec((tm, ns), lambda j, i: (i, j))],
        out_shape=[_sds((2, M, half * ns), F32), _sds((M, half * ns), BF16)],
        compiler_params=_params(("parallel", "parallel"), big=True),
    )(h2, wg, wg)


def _mm_swiglu_bwd(dff, w2, gu):
    M, K = dff.shape
    F = w2.shape[0]
    tm, tn = _tile(M, 512, 16), _tile(F, 1408, LANE)

    def body(a_ref, b_ref, gu_ref, du_ref):
        da = _dot(a_ref[...], b_ref[...], NT)
        g = gu_ref[0]
        u = gu_ref[1]
        sg = _sigmoid(g)
        du_ref[0] = (da * u * (sg * (1.0 + g * (1.0 - sg)))).astype(BF16)
        du_ref[1] = (da * (g * sg)).astype(BF16)

    return pl.pallas_call(
        body, grid=(F // tn, M // tm), name="ffn_out_bwd_swiglu",
        in_specs=[pl.BlockSpec((tm, K), lambda j, i: (i, 0)), pl.BlockSpec((tn, K), lambda j, i: (j, 0)),
                  pl.BlockSpec((2, tm, tn), lambda j, i: (0, i, j))],
        out_specs=pl.BlockSpec((2, tm, tn), lambda j, i: (0, i, j)),
        out_shape=_sds((2, M, F), BF16), compiler_params=_params(("parallel", "parallel"), big=True),
    )(dff, w2, gu)


ORDER_ONLY = pl.BlockSpec(memory_space=pl.ANY)


def _mm_tn_rows(dep, a, b, rs, name):
    M, Ka = a.shape
    _, N = b.shape
    tm = _tile(M, 1024, 16)

    def body(_, a_ref, b_ref, o_ref, acc_ref):
        m = pl.program_id(1)

        @pl.when(m == 0)
        def _():
            acc_ref[...] = jnp.zeros_like(acc_ref)

        acc_ref[...] += _dot(a_ref[...], b_ref[...], TN)

        @pl.when(m == pl.num_programs(1) - 1)
        def _():
            o_ref[0, 0] = acc_ref[0:rs, :].astype(BF16)
            o_ref[1, 0] = acc_ref[rs:2 * rs, :].astype(BF16)

    return pl.pallas_call(
        body, grid=(N_DEV // 2, M // tm), name=name,
        in_specs=[ORDER_ONLY, pl.BlockSpec((tm, 2 * rs), lambda ch, m: (m, ch)),
                  pl.BlockSpec((tm, N), lambda ch, m: (m, 0))],
        out_specs=pl.BlockSpec((2, 1, rs, N), lambda ch, m: (0, ch, 0, 0)),
        out_shape=_sds((2, N_DEV // 2, rs, N), BF16),
        scratch_shapes=[pltpu.VMEM((2 * rs, N), F32)],
        compiler_params=_params(("parallel", "arbitrary"), big=True),
    )(dep, a, b)


def _mm_gathered_nt(dep, a, a_spec, wg, M, tm, name):
    _, K, ns = wg.shape

    def body(_, a_ref, w_ref, o_ref):
        @pl.when(pl.program_id(1) == 0)
        def _():
            o_ref[...] = jnp.zeros_like(o_ref)

        o_ref[...] += _dot(a_ref[...], w_ref[...], NT)

    return pl.pallas_call(
        body, grid=(M // tm, N_DEV), name=name,
        in_specs=[ORDER_ONLY, a_spec(tm, ns), pl.BlockSpec((None, K, ns), lambda i, j: (j, 0, 0))],
        out_specs=pl.BlockSpec((tm, K), lambda i, j: (i, 0)),
        out_shape=_sds((M, K), F32), compiler_params=_params(("parallel", "arbitrary"), big=True),
    )(dep, a, wg)


def _mm_tn_gathered(dep, h, a, a_spec, ns, tm, name):
    M, K = h.shape

    def body(_, h_ref, a_ref, o_ref, acc_ref):
        m = pl.program_id(1)

        @pl.when(m == 0)
        def _():
            acc_ref[...] = jnp.zeros_like(acc_ref)

        acc_ref[...] += _dot(h_ref[...], a_ref[...], TN)

        @pl.when(m == pl.num_programs(1) - 1)
        def _():
            o_ref[...] = acc_ref[...].astype(BF16)

    return pl.pallas_call(
        body, grid=(N_DEV, M // tm), name=name,
        in_specs=[ORDER_ONLY, pl.BlockSpec((tm, K), lambda j, m: (m, 0)), a_spec(tm, ns)],
        out_specs=pl.BlockSpec((None, K, ns), lambda j, m: (_slot(j), 0, 0)),
        out_shape=_sds((N_DEV, K, ns), BF16),
        scratch_shapes=[pltpu.VMEM((K, ns), F32)],
        compiler_params=_params(("parallel", "arbitrary"), big=True),
    )(dep, h, a)


def _bias_onehot(rbp, max_rel):
    r = lax.broadcasted_iota(jnp.int32, (rbp, TAB), 0)
    m = lax.broadcasted_iota(jnp.int32, (rbp, TAB), 1)
    dist = KPAD - jnp.where(m < WIN, m, m - TAB)
    return (r == jnp.clip(dist, -max_rel, max_rel) + max_rel).astype(F32)


def _attn_setup(i, hp, k_ref, v_ref, gv_ref, kpad, vpad, bias):
    ls = slice(i * ATTN_HEAD_DIM, (i + 1) * ATTN_HEAD_DIM)
    kpad[i][0:KPAD, :] = jnp.zeros((KPAD, ATTN_HEAD_DIM), BF16)
    vpad[i][0:KPAD, :] = jnp.zeros((KPAD, ATTN_HEAD_DIM), BF16)
    kpad[i][KPAD:, :] = k_ref[:, ls].astype(BF16)
    vpad[i][KPAD:, :] = v_ref[:, ls].astype(BF16)
    gvrow = gv_ref[pl.ds(hp * 2 + i, 1), :]
    tab = pltpu.roll(jnp.broadcast_to(gvrow, (QBLK, TAB)), 0, 1, stride=1, stride_axis=0)
    row = lax.broadcasted_iota(jnp.int32, (QBLK, WIN), 0)
    col = lax.broadcasted_iota(jnp.int32, (QBLK, WIN), 1)
    first = jnp.bitwise_and(row, -CHUNK)
    seen = jnp.logical_and(col >= first, col < first + (N_PAST + 1) * CHUNK)
    bias[i][...] = jnp.where(seen, tab[:, 0:WIN], NEG)


def _attn_probs(b, q_ref, kpad, vpad, bias, col):
    pair = range(2)
    ls = [slice(i * ATTN_HEAD_DIM, (i + 1) * ATTN_HEAD_DIM) for i in pair]
    r0 = pl.multiple_of(b * QBLK, QBLK)
    q = [q_ref[pl.ds(r0, QBLK), ls[i]].astype(BF16) for i in pair]
    kw = [kpad[i][pl.ds(r0, WIN), :] for i in pair]
    vw = [vpad[i][pl.ds(r0, WIN), :] for i in pair]
    s = [_dot(q[i], kw[i], NT) * (ATTN_HEAD_DIM ** -0.5) + bias[i][...] for i in pair]
    s = [jnp.where(col >= KPAD - r0, s[i], NEG) for i in pair]
    p = [jnp.exp(s[i] - jnp.max(s[i], axis=-1, keepdims=True)) for i in pair]
    pn = [p[i] / jnp.sum(p[i], axis=-1, keepdims=True) for i in pair]
    return r0, ls, q, kw, vw, pn


def _attn_fwd(proj, gv, ga, AW):
    T = proj.shape[0]
    HP = AW // LANE

    def body(q_ref, k_ref, v_ref, gv_ref, ga_ref, o_ref, *scratch):
        kpad, vpad, bias = scratch[0:2], scratch[2:4], scratch[4:6]
        hp = pl.program_id(0)
        for i in range(2):
            _attn_setup(i, hp, k_ref, v_ref, gv_ref, kpad, vpad, bias)
        col = lax.broadcasted_iota(jnp.int32, (QBLK, WIN), 1)

        def block(b, carry):
            pair = range(2)
            r0, ls, _, _, vw, pn = _attn_probs(b, q_ref, kpad, vpad, bias, col)
            o = [_dot(pn[i].astype(BF16), vw[i], NN) for i in pair]
            r = [lax.rsqrt(jnp.mean(o[i] * o[i], axis=-1, keepdims=True) + EPS) for i in pair]
            outs = [o[i] * r[i] * ga_ref[0:1, ls[i]] for i in pair]
            o_ref[pl.ds(r0, QBLK), :] = jnp.concatenate(outs, axis=1).astype(BF16)
            return carry

        lax.fori_loop(0, T // QBLK, block, 0)

    blk = lambda off: pl.BlockSpec((T, LANE), lambda hp: (0, off + hp))
    return pl.pallas_call(
        body, grid=(HP,), name="attn_fwd",
        in_specs=[blk(0), blk(HP), blk(2 * HP), pl.BlockSpec(gv.shape, lambda hp: (0, 0)),
                  pl.BlockSpec((1, LANE), lambda hp: (0, hp))],
        out_specs=pl.BlockSpec((T, LANE), lambda hp: (0, hp)),
        out_shape=_sds((T, AW), BF16),
        scratch_shapes=[pltpu.VMEM((T + KPAD, ATTN_HEAD_DIM), BF16)] * 4 + [pltpu.VMEM((QBLK, WIN), F32)] * 2,
        compiler_params=_params(("parallel",), big=True),
    )(proj, proj, proj, gv, ga)


def _attn_bwd(proj, dmixin, gv, ga, AW):
    T = proj.shape[0]
    HP = AW // LANE
    scale = ATTN_HEAD_DIM ** -0.5

    def body(q_ref, k_ref, v_ref, dn_ref, gv_ref, ga_ref, dq_ref, dk_ref, dv_ref, dgv_ref, dga_ref, *scratch):
        kpad, vpad, dkacc, dvacc = scratch[0:2], scratch[2:4], scratch[4:6], scratch[6:8]
        bias, dbias = scratch[8:10], scratch[10:12]
        hp = pl.program_id(0)
        for i in range(2):
            _attn_setup(i, hp, k_ref, v_ref, gv_ref, kpad, vpad, bias)
            dkacc[i][...] = jnp.zeros_like(dkacc[i])
            dvacc[i][...] = jnp.zeros_like(dvacc[i])
            dbias[i][...] = jnp.zeros_like(dbias[i])
        dga_ref[...] = jnp.zeros_like(dga_ref)
        col = lax.broadcasted_iota(jnp.int32, (QBLK, WIN), 1)

        def block(b, carry):
            pair = range(2)
            r0, lss, qs, kws, vws, pns = _attn_probs(b, q_ref, kpad, vpad, bias, col)
            pn_b = [pns[i].astype(BF16) for i in pair]
            o = [_dot(pn_b[i], vws[i], NN) for i in pair]
            r = [lax.rsqrt(jnp.mean(o[i] * o[i], axis=-1, keepdims=True) + EPS) for i in pair]
            dn = [dn_ref[pl.ds(r0, QBLK), lss[i]] for i in pair]
            for i in pair:
                dga_ref[i:i + 1, :] += _colsum(dn[i] * o[i] * r[i])
            a = [dn[i] * ga_ref[0:1, lss[i]] for i in pair]
            do_b = [(r[i] * (a[i] - o[i] * (r[i] * r[i]) * jnp.mean(a[i] * o[i], axis=-1, keepdims=True))).astype(BF16)
                    for i in pair]
            dp = [_dot(do_b[i], vws[i], NT) for i in pair]
            for i in pair:
                dvacc[i][pl.ds(r0, WIN), :] += _dot(pn_b[i], do_b[i], TN)
            ds = [pns[i] * (dp[i] - jnp.sum(pns[i] * dp[i], axis=-1, keepdims=True)) for i in pair]
            for i in pair:
                dbias[i][...] += ds[i]
            ds_b = [ds[i].astype(BF16) for i in pair]
            dq = [_dot(ds_b[i], kws[i], NN) * scale for i in pair]
            dq_ref[pl.ds(r0, QBLK), :] = jnp.concatenate(dq, axis=1).astype(BF16)
            for i in pair:
                dkacc[i][pl.ds(r0, WIN), :] += _dot(ds_b[i], qs[i], TN) * scale
            return carry

        lax.fori_loop(0, T // QBLK, block, 0)

        rr = lax.broadcasted_iota(jnp.int32, (QBLK, QBLK), 0)
        cc = lax.broadcasted_iota(jnp.int32, (QBLK, QBLK), 1)
        flip = (rr + cc == QBLK - 1).astype(BF16)
        for i in range(2):
            ls = slice(i * ATTN_HEAD_DIM, (i + 1) * ATTN_HEAD_DIM)
            dk_ref[:, ls] = dkacc[i][KPAD:, :].astype(BF16)
            dv_ref[:, ls] = dvacc[i][KPAD:, :].astype(BF16)
            full = jnp.concatenate([dbias[i][...], jnp.zeros((QBLK, TAB - WIN), F32)], axis=1)
            hi = full.astype(BF16)
            lo = (full - hi.astype(F32)).astype(BF16)
            rev = _dot(flip, hi, NN) + _dot(flip, lo, NN)
            dgv_ref[i:i + 1, :] = _colsum(pltpu.roll(rev, TAB - (QBLK - 1), 1, stride=1, stride_axis=0))

    blk = lambda off: pl.BlockSpec((T, LANE), lambda hp: (0, off + hp))
    accs = lambda dt: [pltpu.VMEM((T + KPAD, ATTN_HEAD_DIM), dt)] * 2
    return pl.pallas_call(
        body, grid=(HP,), name="attn_bwd",
        in_specs=[blk(0), blk(HP), blk(2 * HP), blk(0), pl.BlockSpec(gv.shape, lambda hp: (0, 0)),
                  pl.BlockSpec((1, LANE), lambda hp: (0, hp))],
        out_specs=[blk(0), blk(0), blk(0), pl.BlockSpec((None, 2, TAB), lambda hp: (hp, 0, 0)),
                   pl.BlockSpec((None, 2, ATTN_HEAD_DIM), lambda hp: (hp, 0, 0))],
        out_shape=[_sds((T, AW), BF16), _sds((T, AW), BF16), _sds((T, AW), BF16),
                   _sds((HP, 2, TAB), F32), _sds((HP, 2, ATTN_HEAD_DIM), F32)],
        scratch_shapes=accs(BF16) + accs(BF16) + accs(F32) + accs(F32) + [pltpu.VMEM((QBLK, WIN), F32)] * 4,
        compiler_params=_params(("parallel",), big=True),
    )(proj, proj, proj, dmixin, gv, ga)


def _ltri():
    r = lax.broadcasted_iota(jnp.int32, (CHUNK, CHUNK), 0)
    c = lax.broadcasted_iota(jnp.int32, (CHUNK, CHUNK), 1)
    return (c <= r).astype(BF16)


def _tri_dot(tri, v, dims):
    hi = v.astype(BF16)
    lo = (v - hi.astype(F32)).astype(BF16)
    return _dot(tri, hi, dims) + _dot(tri, lo, dims)


HEADS_PER_STEP = 2


def _alternate(stages):
    live = list(stages)
    while live:
        for g in list(live):
            if next(g, StopIteration) is StopIteration:
                live.remove(g)


def _hgrn_gates(n, ls, q_ref, f_ref, lb_ref, ltri):
    r0 = pl.multiple_of(n * CHUNK, CHUNK)
    rows = pl.ds(r0, CHUNK)
    lb = lb_ref[:, ls]
    qb = q_ref[rows, ls]
    sg = _sigmoid(f_ref[rows, ls])
    f = lb + (1.0 - lb) * sg
    sq = _sigmoid(qb)
    b = _tri_dot(ltri, jnp.log(f), NN)
    return rows, lb, qb, sg, f, 1.0 - f, sq, qb * sq, b


def _hgrn_specs(T, RW, AW):
    HG = HEADS_PER_STEP
    W = HG * LANE
    base = 3 * AW // W
    blk_in = lambda off: pl.BlockSpec((T, W), lambda g: (0, base + off + g))
    col = pl.BlockSpec((T, W), lambda g: (0, g))
    return HG, W, RW // W, blk_in, col


def _hgrn_fwd(proj, lb, gn, AW, RW):
    T = proj.shape[0]
    RH, NC, NSUB = RW // LANE, T // CHUNK, CHUNK // SUB
    HG, W, NG, blk_in, col = _hgrn_specs(T, RW, AW)

    def body(q_ref, f_ref, i_ref, g_ref, lb_ref, gn_ref, mix_ref, o_ref, stall_ref, st_all, bs_all, kks_all, ics_all):
        st_all[...] = jnp.zeros_like(st_all)
        ltri = _ltri()
        rowi = lax.broadcasted_iota(jnp.int32, (SUB, 1), 0)

        def one_head(h, n):
            ls = slice(h * LANE, (h + 1) * LANE)
            st, bs, kks, ics = st_all.at[h], bs_all.at[h], kks_all.at[h], ics_all.at[h]
            rows, _, _, _, _, kk, _, qs, b = _hgrn_gates(n, ls, q_ref, f_ref, lb_ref, ltri)
            ic = i_ref[rows, ls]
            stv = st[...]
            stall_ref[h, n] = stv
            bs[...] = b
            kks[...] = kk
            ics[...] = ic
            yield
            o = _dot((qs * jnp.exp(b)).astype(BF16), stv.astype(BF16), NT)
            yield
            ic_b = ic.astype(BF16)
            pieces = []
            for blk in range(NSUB):
                s0 = blk * SUB
                bI, qI = b[s0:s0 + SUB], qs[s0:s0 + SUB]
                if blk == 0:
                    oI = jnp.zeros((SUB, LANE), F32)
                else:
                    ref = bs[s0 - 1:s0, :]
                    qt = (qI * jnp.exp(bI - ref)).astype(BF16)
                    kt = (kk[0:s0] * jnp.exp(ref - b[0:s0])).astype(BF16)
                    oI = _dot(_dot(qt, kt, NT).astype(BF16), ic_b[0:s0], NN)
                    yield
                acc = [oI[g * ROWS:(g + 1) * ROWS] for g in range(SUB // ROWS)]
                for s in range(SUB):
                    sr = s0 + s
                    g0 = s // ROWS
                    lo = g0 * ROWS
                    e = jnp.exp(jnp.minimum(bI[lo:] - bs[sr:sr + 1, :], 0.0))
                    a = jnp.sum(qI[lo:] * kks[sr:sr + 1, :] * e, axis=-1, keepdims=True)
                    add = jnp.where(rowi[lo:] >= s, a, 0.0) * ics[sr:sr + 1, :]
                    for g in range(g0, SUB // ROWS):
                        acc[g] = acc[g] + add[(g - g0) * ROWS:(g - g0 + 1) * ROWS]
                    yield
                pieces.extend(acc)
            o = o + jnp.concatenate(pieces, axis=0)
            bl = bs[CHUNK - 1:CHUNK, :]
            kd = (kk * jnp.exp(bl - b)).astype(BF16)
            st[...] = stv * jnp.exp(bl) + _dot(ic_b, kd, TN)
            yield
            o_ref[rows, ls] = o
            r = lax.rsqrt(jnp.mean(o * o, axis=-1, keepdims=True) + EPS)
            gb = g_ref[rows, ls]
            mix_ref[rows, ls] = (o * r * gn_ref[...] * (gb * _sigmoid(gb))).astype(BF16)

        def chunk(n, carry):
            _alternate([one_head(h, n) for h in range(HG)])
            return carry

        lax.fori_loop(0, NC, chunk, 0)

    tile = pltpu.VMEM((HG, CHUNK, LANE), F32)
    return pl.pallas_call(
        body, grid=(NG,), name="hgrn_fwd",
        in_specs=[blk_in(0), blk_in(NG), blk_in(2 * NG), blk_in(3 * NG), pl.BlockSpec((1, W), lambda g: (0, g)),
                  pl.BlockSpec((1, LANE), lambda g: (0, 0))],
        out_specs=[col, col, pl.BlockSpec((HG, NC, LANE, LANE), lambda g: (g, 0, 0, 0))],
        out_shape=[_sds((T, RW), BF16), _sds((T, RW), F32), _sds((RH, NC, LANE, LANE), F32)],
        scratch_shapes=[pltpu.VMEM((HG, LANE, LANE), F32), tile, tile, tile],
        compiler_params=_params(("parallel",), big=True),
    )(proj, proj, proj, proj, lb, gn)


def _hgrn_bwd(proj, dmixin, o_b, st_all, lb, gn, AW, RW):
    T = proj.shape[0]
    RH, NC, NSUB = RW // LANE, T // CHUNK, CHUNK // SUB
    HG, W, NG, blk_in, col = _hgrn_specs(T, RW, AW)

    def body(q_ref, f_ref, i_ref, g_ref, o_ref, dn_ref, stall_ref, lb_ref, gn_ref,
             dq_ref, df_ref, di_ref, dg_ref, dlb_ref, dgn_ref, dst_all, bs_all, kks_all, ics_all, p2_all, dic_all):
        dst_all[...] = jnp.zeros_like(dst_all)
        dlb_ref[...] = jnp.zeros_like(dlb_ref)
        dgn_ref[...] = jnp.zeros_like(dgn_ref)
        ltri = _ltri()
        rowi = lax.broadcasted_iota(jnp.int32, (SUB, 1), 0)
        last = lax.broadcasted_iota(jnp.int32, (CHUNK, 1), 0) == CHUNK - 1

        def one_head(h, n):
            ls = slice(h * LANE, (h + 1) * LANE)
            dst, bs, kks, ics = dst_all.at[h], bs_all.at[h], kks_all.at[h], ics_all.at[h]
            p2, dic = p2_all.at[h], dic_all.at[h]
            rows, lbv, qb, sg, f, kk, sq, qs, b = _hgrn_gates(n, ls, q_ref, f_ref, lb_ref, ltri)
            ic = i_ref[rows, ls]
            stv = stall_ref[h, n]
            dstv = dst[...]
            o = o_ref[rows, ls]
            dn = dn_ref[rows, ls]
            gb = g_ref[rows, ls]
            sgb = _sigmoid(gb)
            r = lax.rsqrt(jnp.mean(o * o, axis=-1, keepdims=True) + EPS)
            gnv = gn_ref[...]
            dg_ref[rows, ls] = (dn * (o * r * gnv) * (sgb * (1.0 + gb * (1.0 - sgb)))).astype(BF16)
            dy = dn * (gb * sgb)
            dgn_ref[h] += _colsum(dy * o * r)
            a_ = dy * gnv
            do = r * (a_ - o * (r * r) * jnp.mean(a_ * o, axis=-1, keepdims=True))
            do_b = do.astype(BF16)
            bs[...] = b
            kks[...] = kk
            ics[...] = ic
            yield
            ic_b = ic.astype(BF16)
            eb = jnp.exp(b)
            bl = bs[CHUNK - 1:CHUNK, :]
            ebl = jnp.exp(bl)
            dec = jnp.exp(bl - b)
            kd = (kk * dec).astype(BF16)
            dst_b = dstv.astype(BF16)
            dqs = _dot(do_b, stv.astype(BF16), NN) * eb
            dkk2 = _dot(ic_b, dst_b, NN) * dec
            dic[...] = _dot(kd, dst_b, NT)
            dbl = ebl * _colsum(stv * dstv) + _colsum(kk * dkk2)
            dst[...] = dstv * ebl + _dot(do_b, (qs * eb).astype(BF16), TN)
            yield
            p2[...] = jnp.zeros_like(p2)
            p1_pieces = []
            for blk in range(NSUB):
                s0 = blk * SUB
                bI, qI, doI = b[s0:s0 + SUB], qs[s0:s0 + SUB], do[s0:s0 + SUB]
                if blk == 0:
                    p1 = jnp.zeros((SUB, LANE), F32)
                else:
                    ref = bs[s0 - 1:s0, :]
                    eq = jnp.exp(bI - ref)
                    ek = jnp.exp(ref - b[0:s0])
                    qt = (qI * eq).astype(BF16)
                    kt = (kk[0:s0] * ek).astype(BF16)
                    doI_b = doI.astype(BF16)
                    dic[0:s0, :] += _dot(_dot(qt, kt, NT).astype(BF16), doI_b, TN)
                    da = _dot(doI_b, ic_b[0:s0], NT).astype(BF16)
                    p1 = _dot(da, kt, NN) * eq
                    p2[0:s0, :] += _dot(da, qt, TN) * ek
                    yield
                acc = [p1[g * ROWS:(g + 1) * ROWS] for g in range(SUB // ROWS)]
                for s in range(SUB):
                    sr = s0 + s
                    g0 = s // ROWS
                    lo = g0 * ROWS
                    keep = rowi[lo:] >= s
                    kk_s = kks[sr:sr + 1, :]
                    e = jnp.exp(jnp.minimum(bI[lo:] - bs[sr:sr + 1, :], 0.0))
                    w = qI[lo:] * e
                    a = jnp.where(keep, jnp.sum(w * kk_s, axis=-1, keepdims=True), 0.0)
                    da_s = jnp.where(keep, jnp.sum(doI[lo:] * ics[sr:sr + 1, :], axis=-1, keepdims=True), 0.0)
                    add = da_s * kk_s * e
                    for g in range(g0, SUB // ROWS):
                        acc[g] = acc[g] + add[(g - g0) * ROWS:(g - g0 + 1) * ROWS]
                    p2[sr:sr + 1, :] += _colsum(da_s * w)
                    dic[sr:sr + 1, :] += _colsum(a * doI[lo:])
                    yield
                p1_pieces.extend(acc)
            dqs = dqs + jnp.concatenate(p1_pieces, axis=0)
            dkk = dkk2 + p2[...]
            db = qs * dqs - kk * dkk + jnp.where(last, dbl, 0.0)
            dgl = _tri_dot(ltri, db, TN)
            yield
            dfv = dgl / f - dkk
            df_ref[rows, ls] = (dfv * (1.0 - lbv) * sg * (1.0 - sg)).astype(BF16)
            dlb_ref[:, ls] += _colsum(dfv * (1.0 - sg))
            dq_ref[rows, ls] = (dqs * (sq * (1.0 + qb * (1.0 - sq)))).astype(BF16)
            di_ref[rows, ls] = dic[...].astype(BF16)

        def chunk(k, carry):
            _alternate([one_head(h, NC - 1 - k) for h in range(HG)])
            return carry

        lax.fori_loop(0, NC, chunk, 0)

    tile = pltpu.VMEM((HG, CHUNK, LANE), F32)
    return pl.pallas_call(
        body, grid=(NG,), name="hgrn_bwd",
        in_specs=[blk_in(0), blk_in(NG), blk_in(2 * NG), blk_in(3 * NG), col,
                  pl.BlockSpec((T, W), lambda g: (0, AW // W + g)),
                  pl.BlockSpec((HG, NC, LANE, LANE), lambda g: (g, 0, 0, 0)),
                  pl.BlockSpec((1, W), lambda g: (0, g)), pl.BlockSpec((1, LANE), lambda g: (0, 0))],
        out_specs=[col, col, col, col, pl.BlockSpec((1, W), lambda g: (0, g)),
                   pl.BlockSpec((HG, 1, LANE), lambda g: (g, 0, 0))],
        out_shape=[_sds((T, RW), BF16)] * 4 + [_sds((1, RW), F32), _sds((RH, 1, LANE), F32)],
        scratch_shapes=[pltpu.VMEM((HG, LANE, LANE), F32), tile, tile, tile, tile, tile],
        compiler_params=_params(("parallel",), big=True),
    )(proj, proj, proj, proj, o_b, dmixin, st_all, lb, gn)


def _prep(c, lb_logits, rb_pad, max_rel):
    D, RW = c.shape[-1], lb_logits.shape[-1]
    H, rbp = rb_pad.shape

    def body(c_ref, l_ref, rb_ref, cact_ref, lb_ref, gv_ref):
        cv = c_ref[...]
        cact_ref[...] = cv * _sigmoid(cv)
        lb_ref[...] = _sigmoid(l_ref[0:1, :] - l_ref[1:2, :])
        gv_ref[...] = _dot(rb_ref[...], _bias_onehot(rbp, max_rel), NN, HIGHEST)

    return pl.pallas_call(
        body, name="prep", out_shape=[_sds((1, D), F32), _sds((1, RW), F32), _sds((H, TAB), F32)],
    )(c, lb_logits, rb_pad)


def _mod_part(c_all, w_ada_s, b_ada_s):
    B, D = c_all.shape
    ns = w_ada_s.shape[1]
    tn = _tile(ns, 768, LANE)

    def body(c_ref, w_ref, b_ref, o_ref):
        o_ref[...] = _dot(c_ref[...], w_ref[...], NN) + b_ref[...]

    return pl.pallas_call(
        body, grid=(ns // tn,), name="mod_part",
        in_specs=[pl.BlockSpec((B, D), lambda j: (0, 0)), pl.BlockSpec((D, tn), lambda j: (0, j)),
                  pl.BlockSpec((1, tn), lambda j: (0, j))],
        out_specs=pl.BlockSpec((B, tn), lambda j: (0, j)),
        out_shape=_sds((B, ns), F32), compiler_params=_params(("parallel",)),
    )(c_all, w_ada_s, b_ada_s)


def _adam(w, g, m, v):
    m = ADAM_B1 * m + (1.0 - ADAM_B1) * g
    v = ADAM_B2 * v + (1.0 - ADAM_B2) * (g * g)
    m_hat = m / (1.0 - ADAM_B1 ** ADAM_STEP)
    v_hat = v / (1.0 - ADAM_B2 ** ADAM_STEP)
    return -ADAM_LR * (m_hat / (jnp.sqrt(v_hat) + ADAM_EPS) + ADAM_WD * w), m, v


def _adam_ada(c_all, dmod_s, w, m, v):
    B, D = c_all.shape
    ns = w.shape[1]
    tr, tn = _tile(D, 512, LANE), _tile(ns, 768, LANE)

    def body(c_ref, d_ref, w_ref, m_ref, v_ref, g_out, dw_out, m_out, v_out):
        g = _dot(c_ref[...], d_ref[...], TN)
        g_out[...] = g
        dw_out[...], m_out[...], v_out[...] = _adam(w_ref[...], g, m_ref[...], v_ref[...])

    big = pl.BlockSpec((tr, tn), lambda i, j: (i, j))
    return pl.pallas_call(
        body, grid=(D // tr, ns // tn), name="adam_w_ada",
        in_specs=[pl.BlockSpec((B, tr), lambda i, j: (0, i)), pl.BlockSpec((B, tn), lambda i, j: (0, j)),
                  big, big, big],
        out_specs=[big] * 4, out_shape=[_sds((D, ns), F32)] * 4,
        compiler_params=_params(("parallel", "parallel")),
    )(c_all, dmod_s, w, m, v)


def _adam_shard(parts, w, m, v, name):
    R, C = w.shape
    tr = _tile(R, 256, 16)

    def body(p_ref, w_ref, m_ref, v_ref, g_out, dw_out, m_out, v_out):
        g = p_ref[0].astype(F32)
        for k in range(1, N_DEV // 2):
            g = g + p_ref[k].astype(F32)
        g_out[...] = g
        dw_out[...], m_out[...], v_out[...] = _adam(w_ref[...], g, m_ref[...], v_ref[...])

    big = pl.BlockSpec((tr, C), lambda i: (i, 0))
    return pl.pallas_call(
        body, grid=(R // tr,), name=name,
        in_specs=[pl.BlockSpec((N_DEV // 2, tr, C), lambda i: (0, i, 0)), big, big, big],
        out_specs=[big] * 4, out_shape=[_sds((R, C), F32)] * 4,
        compiler_params=_params(("parallel",), big=True),
    )(parts, w, m, v)


def _pair_sum(g8, land, core, name):
    _, NCHIP, R, C = g8.shape
    tr = _tile(R, 1024, 16)

    def body(core_ref, g_ref, l_ref, o_ref):
        o_ref[...] = (g_ref[...].astype(F32) + l_ref[...].astype(F32)).astype(BF16)

    return pl.pallas_call(
        body, name=name,
        grid_spec=pltpu.PrefetchScalarGridSpec(
            num_scalar_prefetch=1, grid=(NCHIP, R // tr),
            in_specs=[pl.BlockSpec((None, None, tr, C), lambda k, i, core_ref: (core_ref[0], k, i, 0)),
                      pl.BlockSpec((None, tr, C), lambda k, i, core_ref: (k, i, 0))],
            out_specs=pl.BlockSpec((None, tr, C), lambda k, i, core_ref: (k, i, 0))),
        out_shape=_sds((NCHIP, R, C), BF16), compiler_params=_params(("parallel", "parallel")),
    )(core, g8, land)


SMALL = ("b_ada", "rel_bias", "attn_norm_g", "lb_logits", "gnorm_g", "ln1_g", "ln1_b", "ln2_g", "ln2_b")


def _small_update(parts, loss_parts, lbv, ws, ms, vs, max_rel):
    n = len(SMALL)

    def body(*refs):
        part_refs = dict(zip(SMALL, refs[:n]))
        loss_in, lb_ref = refs[n], refs[n + 1]
        w_refs, m_refs, v_refs = refs[n + 2:2 * n + 2], refs[2 * n + 2:3 * n + 2], refs[3 * n + 2:4 * n + 2]
        outs = refs[4 * n + 2:]

        def total(ref):
            tot = ref[0]
            for k in range(1, N_DEV):
                tot = tot + ref[k]
            return tot

        outs[0][...] = jnp.sum(total(loss_in), axis=-1, keepdims=True)
        for idx, name in enumerate(SMALL):
            g = total(part_refs[name])
            if name == "rel_bias":
                g = _dot(g, _bias_onehot(w_refs[idx].shape[1], max_rel), NT, HIGHEST)
            elif name == "lb_logits":
                lb = lb_ref[...]
                sign = (1 - 2 * lax.broadcasted_iota(jnp.int32, (2, 1), 0)).astype(F32)
                g = sign * (g * lb * (1.0 - lb))
            elif name == "gnorm_g":
                g = _colsum(g)
            dw, mm, vv = _adam(w_refs[idx][...], g, m_refs[idx][...], v_refs[idx][...])
            outs[1 + 4 * idx][...] = g
            outs[2 + 4 * idx][...] = dw
            outs[3 + 4 * idx][...] = mm
            outs[4 + 4 * idx][...] = vv

    out_shape = [_sds((1, 1), F32)]
    for w in ws:
        out_shape += [_sds(w.shape, F32)] * 4
    return pl.pallas_call(body, name="small_update", out_shape=out_shape, compiler_params=_params(big=True))(
        *[parts[k] for k in SMALL], loss_parts, lbv, *ws, *ms, *vs)


def _place():
    x, y, c = lax.axis_index("x"), lax.axis_index("y"), lax.axis_index("c")
    return x, y, c, [(1 - x, y), (x, 1 - y), (1 - x, 1 - y)]


def _all_gather(shard, name):
    HBM = pl.BlockSpec(memory_space=pl.ANY)

    def body(x_ref, out_ref, send_sems, recv_sems, local_sem):
        x, y, c, chips = _place()
        me, sibling = (x, y, c), (x, y, 1 - c)

        def slot(px, py, pc):
            return out_ref.at[4 * px + 2 * py + pc]

        def copy(k, block, to, src=None):
            return pltpu.make_async_remote_copy(
                src_ref=slot(*block) if src is None else src, dst_ref=slot(*block),
                send_sem=send_sems.at[k], recv_sem=recv_sems.at[k], device_id=to, device_id_type=MESH)

        mine = pltpu.make_async_copy(x_ref, slot(*me), local_sem)
        mine.start()
        first = [copy(0, me, sibling, src=x_ref)]
        first += [copy(1 + j, me, (*chip, c), src=x_ref) for j, chip in enumerate(chips)]
        for cp in first:
            cp.start()
        passed = [copy(4 + j, (*chip, c), sibling) for j, chip in enumerate(chips)]
        for j, chip in enumerate(chips):
            copy(1 + j, (*chip, c), me).wait_recv()
            passed[j].start()
        copy(0, sibling, me).wait_recv()
        for j, chip in enumerate(chips):
            copy(4 + j, (*chip, 1 - c), me).wait_recv()
        for cp in first + passed:
            cp.wait_send()
        mine.wait()

    return pl.pallas_call(
        body, name=name, out_shape=_sds((N_DEV,) + shard.shape, shard.dtype),
        in_specs=[HBM], out_specs=HBM,
        scratch_shapes=[pltpu.SemaphoreType.DMA((7,)), pltpu.SemaphoreType.DMA((7,)), pltpu.SemaphoreType.DMA(())],
    )(shard)


SEM_SPEC = pl.BlockSpec(memory_space=pltpu.SEMAPHORE)
HBM_SPEC = pl.BlockSpec(memory_space=pltpu.HBM)
EFFECT = pltpu.SideEffectType.DATAFLOW_SIDE_EFFECTING


def _remote(src, dst, send_sems, recv_sems, k, dev):
    return pltpu.make_async_remote_copy(src_ref=src, dst_ref=dst, send_sem=send_sems.at[k], recv_sem=recv_sems.at[k],
                                        device_id=dev, device_id_type=MESH)


def _copy_start(name, bufs, plan, n, after):
    nb = len(bufs)

    def body(*refs):
        send_sems, recv_sems = refs[nb + 1], refs[nb + 2]
        for k, (src, dst, dev) in enumerate(plan(*refs[:nb])):
            _remote(src, dst, send_sems, recv_sems, k, dev).start()
        refs[-1][...] = jnp.zeros_like(refs[-1])

    out = pl.pallas_call(
        body, name=name,
        out_shape=(pltpu.SemaphoreType.DMA((n,)), pltpu.SemaphoreType.DMA((n,)),
                   *[pltpu.HBM(b.shape, b.dtype) for b in bufs], _sds((8, LANE), F32)),
        in_specs=[HBM_SPEC] * nb + [ORDER_ONLY],
        out_specs=(SEM_SPEC, SEM_SPEC, *[HBM_SPEC] * nb, pl.BlockSpec(memory_space=pltpu.VMEM)),
        input_output_aliases={i: 2 + i for i in range(nb)},
        compiler_params=pltpu.CompilerParams(has_side_effects=EFFECT),
    )(*[pltpu.with_memory_space_constraint(b, pltpu.HBM) for b in bufs], after)
    return (out[0], out[1]), list(out[2:2 + nb]), out[-1]


def _copy_wait(name, sems, bufs, plan, after):
    nb = len(bufs)

    def body(*refs):
        send_sems, recv_sems = refs[nb], refs[nb + 1]
        for k, (src, dst, dev) in enumerate(plan(*refs[:nb])):
            cp = _remote(src, dst, send_sems, recv_sems, k, dev)
            cp.wait_send()
            cp.wait_recv()

    out = pl.pallas_call(
        body, name=name, out_shape=tuple(pltpu.HBM(b.shape, b.dtype) for b in bufs),
        in_specs=[HBM_SPEC] * nb + [SEM_SPEC, SEM_SPEC, pl.BlockSpec(memory_space=pl.ANY)],
        out_specs=tuple([HBM_SPEC] * nb), input_output_aliases={i: i for i in range(nb)},
        compiler_params=pltpu.CompilerParams(has_side_effects=EFFECT),
    )(*bufs, sems[0], sems[1], after)
    return list(out)


def _ag_plan_chips(shard_ref, out_ref):
    x, y, c, chips = _place()
    mine = out_ref.at[4 * x + 2 * y + c]
    return [(shard_ref, mine, (x, y, 1 - c))] + [(shard_ref, mine, (*chip, c)) for chip in chips]


def _ag_plan_pass(out_ref):
    x, y, c, chips = _place()
    slots = [out_ref.at[4 * chip[0] + 2 * chip[1] + c] for chip in chips]
    return [(s, s, (x, y, 1 - c)) for s in slots]


def _rs_plan_pair(g_ref, land_ref):
    x, y, c, _ = _place()
    return [(g_ref.at[1 - c], land_ref, (x, y, 1 - c))]


def _rs_plan_chips(p_ref, land_ref):
    x, y, c, chips = _place()
    return [(p_ref.at[2 * chip[0] + chip[1]], land_ref.at[2 * x + y], (*chip, c)) for chip in chips]


class _Gather:
    def __init__(self, shard, me, tag, after):
        self.tag = tag
        out = lax.dynamic_update_slice(lax.empty((N_DEV,) + shard.shape, shard.dtype), shard[None],
                                       (me,) + (0,) * shard.ndim)
        self.sems, (self.shard, self.out), self.token = _copy_start(
            "ag_start_" + tag, [shard, out], _ag_plan_chips, 4, after)

    def arrived_from_chips(self, after):
        _, out = _copy_wait("ag_wait_" + self.tag, self.sems, [self.shard, self.out], _ag_plan_chips, after)
        self.sems, (self.out,), _ = _copy_start("ag_pass_" + self.tag, [out], _ag_plan_pass, 3, after)

    def passed_on(self, after):
        return _copy_wait("ag_pass_wait_" + self.tag, self.sems, [self.out], _ag_plan_pass, after)[0]


class _ReduceScatter:
    def __init__(self, g8, tag):
        self.tag = tag
        land = lax.empty(g8.shape[1:], g8.dtype)
        self.sems, self.bufs, self.token = _copy_start(
            "rs_pair_start_" + tag, [g8, land], _rs_plan_pair, 1, jnp.zeros((1,), F32))

    def pair_done(self, core, chip, after, start_after=None):
        g8, land = _copy_wait("rs_pair_wait_" + self.tag, self.sems, self.bufs, _rs_plan_pair, after)
        p4 = _pair_sum(g8, land, core, "rs_pair_sum_" + self.tag)
        own = lax.dynamic_slice_in_dim(p4, chip, 1, axis=0)
        land2 = lax.dynamic_update_slice(lax.empty(p4.shape, p4.dtype), own, (chip, 0, 0))
        self.sems, self.bufs, self.token = _copy_start(
            "rs_chips_start_" + self.tag, [p4, land2], _rs_plan_chips, 3,
            jnp.zeros((1,), F32) if start_after is None else start_after)

    def sums(self, after):
        return _copy_wait("rs_chips_wait_" + self.tag, self.sems, self.bufs, _rs_plan_chips, after)[1]


BIG = ("w_in", "w_o", "w_ffn_in", "w_ffn_out")
ORDER = ("w_ada", "b_ada", "w_in", "rel_bias", "attn_norm_g", "lb_logits", "gnorm_g", "w_o", "ln1_g", "ln1_b",
         "w_ffn_in", "w_ffn_out", "ln2_g", "ln2_b")


def kernel(x, c, w_ada, b_ada, w_in, rel_bias, attn_norm_g, lb_logits, gnorm_g, w_o, ln1_g, ln1_b, w_ffn_in, w_ffn_out, ln2_g, ln2_b, loss_target, m_w_ada, m_b_ada, m_w_in, m_rel_bias, m_attn_norm_g, m_lb_logits, m_gnorm_g, m_w_o, m_ln1_g, m_ln1_b, m_w_ffn_in, m_w_ffn_out, m_ln2_g, m_ln2_b, v_w_ada, v_b_ada, v_w_in, v_rel_bias, v_attn_norm_g, v_lb_logits, v_gnorm_g, v_w_o, v_ln1_g, v_ln1_b, v_w_ffn_in, v_w_ffn_out, v_ln2_g, v_ln2_b):
    W = dict(w_ada=w_ada, b_ada=b_ada, w_in=w_in, rel_bias=rel_bias, attn_norm_g=attn_norm_g, lb_logits=lb_logits,
             gnorm_g=gnorm_g, w_o=w_o, ln1_g=ln1_g, ln1_b=ln1_b, w_ffn_in=w_ffn_in, w_ffn_out=w_ffn_out,
             ln2_g=ln2_g, ln2_b=ln2_b)
    M = dict(w_ada=m_w_ada, b_ada=m_b_ada, w_in=m_w_in, rel_bias=m_rel_bias, attn_norm_g=m_attn_norm_g,
             lb_logits=m_lb_logits, gnorm_g=m_gnorm_g, w_o=m_w_o, ln1_g=m_ln1_g, ln1_b=m_ln1_b,
             w_ffn_in=m_w_ffn_in, w_ffn_out=m_w_ffn_out, ln2_g=m_ln2_g, ln2_b=m_ln2_b)
    V = dict(w_ada=v_w_ada, b_ada=v_b_ada, w_in=v_w_in, rel_bias=v_rel_bias, attn_norm_g=v_attn_norm_g,
             lb_logits=v_lb_logits, gnorm_g=v_gnorm_g, w_o=v_w_o, ln1_g=v_ln1_g, ln1_b=v_ln1_b,
             w_ffn_in=v_w_ffn_in, w_ffn_out=v_w_ffn_out, ln2_g=v_ln2_g, ln2_b=v_ln2_b)

    x2, tgt = x[0], loss_target[0]
    T, D = x2.shape
    AW, RW = attn_norm_g.shape[-1], lb_logits.shape[-1]
    MIX = AW + RW
    H, RH = AW // ATTN_HEAD_DIM, RW // LANE
    RB = rel_bias.shape[-1]
    max_rel = (RB - 1) // 2
    rbp = -(-RB // LANE) * LANE
    F = w_ffn_out.shape[1] * N_DEV
    half = N_DEV // 2
    xi, yi, ci = lax.axis_index("x"), lax.axis_index("y"), lax.axis_index("c")
    me = 4 * xi + 2 * yi + ci
    core = jnp.reshape(ci, (1,)).astype(jnp.int32)
    pad_rb = lambda a: jnp.pad(a[0], ((0, 0), (0, rbp - RB)))

    chip = 2 * xi + yi

    c_act, lbv, gv = _prep(c, lb_logits, pad_rb(rel_bias), max_rel)
    c_all = _all_gather(c_act, "ag_c").reshape(N_DEV, D)
    ns_ada = w_ada.shape[-1]
    mod_part = _mod_part(c_all, w_ada[0], lax.dynamic_slice_in_dim(b_ada, me * ns_ada, ns_ada, axis=1))
    mod_all = _all_gather(mod_part, "ag_mod")
    mod6 = lax.dynamic_index_in_dim(mod_all, me, axis=1, keepdims=False).reshape(6, D)

    ag_in = _Gather(w_in[0].astype(BF16), me, "w_in", mod_all)
    ag_o = _Gather(w_o[0].astype(BF16), me, "w_o", ag_in.token)
    ag_f1 = _Gather(w_ffn_in[0].astype(BF16), me, "w_ffn_in", ag_o.token)
    ag_f2 = _Gather(w_ffn_out[0].astype(BF16), me, "w_ffn_out", ag_f1.token)

    h1 = _ln_mod(x2, mod6 + ag_f2.token[0, 0])
    ag_in.arrived_from_chips(h1)
    wg_in = ag_in.passed_on(h1)
    proj = _mm_gathered(h1, wg_in, "in_proj")
    ag_o.arrived_from_chips(proj)
    mix_a = _attn_fwd(proj, gv, attn_norm_g, AW)
    wg_o = ag_o.passed_on(mix_a).reshape(MIX, D)
    mix_b, o_b, st_all = _hgrn_fwd(proj, lbv, gnorm_g, AW, RW)
    ag_f1.arrived_from_chips(mix_b)
    mixin = jnp.concatenate([mix_a, mix_b], axis=1)
    mix = _mm_nn(mixin, wg_o, "out_proj")
    x1, h2 = _mid_fwd(x2, mix, mod6, ln1_g, ln1_b)
    wg_f1 = ag_f1.passed_on(h2)
    gu, act = _mm_swiglu(h2, wg_f1)
    ag_f2.arrived_from_chips(act)
    wg_f2 = ag_f2.passed_on(act).reshape(F, D)
    ff = _mm_nn(act, wg_f2, "ffn_out")
    dff, dx1a, vec_a = _final(x1, ff, mod6, ln2_g, ln2_b, tgt)

    du = _mm_swiglu_bwd(dff, wg_f2, gu)
    rs_f2 = _ReduceScatter(_mm_tn_rows(dff, act, dff, F // N_DEV, "grad_w_ffn_out"), "w_ffn_out")
    tm = _tile(T, 512, 16)
    du_ij = lambda tm_, ns: pl.BlockSpec((None, tm_, ns), lambda i, j: (j // half, i, j % half))
    du_jm = lambda tm_, ns: pl.BlockSpec((None, tm_, ns), lambda j, m: (j // half, m, j % half))
    dh2 = _mm_gathered_nt(rs_f2.token, du, du_ij, wg_f1, T, tm, "ffn_in_bwd")
    rs_f2.pair_done(core, chip, dh2)
    tm_red = _tile(T, 1024, 16)
    gw_f1 = _mm_tn_gathered(rs_f2.token, h2, du, du_jm, wg_f1.shape[-1], tm_red, "grad_w_ffn_in")
    rs_f1 = _ReduceScatter(gw_f1.reshape(2, half, D, -1), "w_ffn_in")
    dmix, dxa, vec_b = _mid_bwd(x2, mix, x1, dx1a, dh2, mod6 + rs_f1.token[0, 0], ln1_g)
    dmixin = _mm_nt(dmix, wg_o, "out_proj_bwd")
    rs_f1.pair_done(core, chip, dmixin)
    rs_o = _ReduceScatter(_mm_tn_rows(rs_f1.token, mixin, dmix, MIX // N_DEV, "grad_w_o"), "w_o")
    dq, dk, dv, dgv, dga = _attn_bwd(proj, dmixin, gv + rs_o.token[0, 0], attn_norm_g, AW)
    rs_o.pair_done(core, chip, dq)
    dqb, dfl, dib, dgb, dlb, dgn = _hgrn_bwd(proj, dmixin, o_b, st_all, lbv + rs_o.token[0, 0], gnorm_g, AW, RW)
    dproj = jnp.concatenate([dq, dk, dv, dqb, dfl, dib, dgb], axis=1)
    p_ij = lambda tm_, ns: pl.BlockSpec((tm_, ns), lambda i, j: (i, j))
    p_jm = lambda tm_, ns: pl.BlockSpec((tm_, ns), lambda j, m: (m, j))
    gw_in = _mm_tn_gathered(rs_o.token, h1, dproj, p_jm, wg_in.shape[-1], tm_red, "grad_w_in")
    rs_in = _ReduceScatter(gw_in.reshape(2, half, D, -1), "w_in")
    dh1 = _mm_gathered_nt(rs_in.token, dproj, p_ij, wg_in, T, tm, "in_proj_bwd")
    grad_x, vec_c = _first_bwd(x2, dh1, dxa, mod6)

    dmod = jnp.concatenate([vec_c[1:2], vec_c[0:1], vec_b[4:5], vec_b[1:2], vec_b[0:1], vec_a[2:3]], axis=0)
    pieces = dict(b_ada=dmod, rel_bias=dgv, attn_norm_g=dga, lb_logits=dlb, gnorm_g=dgn, ln1_g=vec_b[2:3],
                  ln1_b=vec_b[3:4], ln2_g=vec_a[0:1], ln2_b=vec_a[1:2], loss=vec_a[3:4])
    widths = dict(b_ada=(1, 6 * D), rel_bias=(H, TAB), attn_norm_g=(1, AW), lb_logits=(1, RW), gnorm_g=(RH, LANE),
                  ln1_g=(1, D), ln1_b=(1, D), ln2_g=(1, D), ln2_b=(1, D), loss=(1, D))
    packed = jnp.concatenate([pieces[k].reshape(-1, LANE) for k in widths], axis=0)
    gathered = _all_gather(packed, "ag_small")
    rs_in.pair_done(core, chip, dh1, start_after=gathered)
    parts, r0 = {}, 0
    for k, (rows, width) in widths.items():
        nr = rows * width // LANE
        parts[k] = gathered[:, r0:r0 + nr, :].reshape(N_DEV, rows, width)
        r0 += nr
    prep_small = lambda d, k: pad_rb(d[k]) if k == "rel_bias" else d[k]
    small = _small_update(parts, parts["loss"], lbv, [prep_small(W, k) for k in SMALL],
                          [prep_small(M, k) for k in SMALL], [prep_small(V, k) for k in SMALL], max_rel)
    loss = small[0].reshape(())
    res = {}
    for idx, k in enumerate(SMALL):
        four = small[1 + 4 * idx:5 + 4 * idx]
        if k == "rel_bias":
            four = [a[:, :RB][None] for a in four]
        res[k] = list(four)

    dmod_s = lax.dynamic_slice_in_dim(parts["b_ada"].reshape(N_DEV, 6 * D), me * ns_ada, ns_ada, axis=1)
    dmod_s = dmod_s + rs_in.token[0, 0]
    res["w_ada"] = [a[None] for a in _adam_ada(c_all, dmod_s, w_ada[0], m_w_ada[0], v_w_ada[0])]
    after = res["w_ada"][0]
    for k, rs in (("w_ffn_out", rs_f2), ("w_ffn_in", rs_f1), ("w_o", rs_o), ("w_in", rs_in)):
        four = _adam_shard(rs.sums(after), W[k][0], M[k][0], V[k][0], "adam_" + k)
        res[k] = [a[None] for a in four]
        after = four[0]

    out = [loss, grad_x[None]]
    for field in range(4):
        out += [res[k][field] for k in ORDER]
    return tuple(out)
```

```python
import functools

import jax
import jax.numpy as jnp
from jax import lax
from jax.experimental import pallas as pl
from jax.experimental.pallas import tpu as pltpu

F32 = jnp.float32
BF16 = jnp.bfloat16
MESH = pl.DeviceIdType.MESH
HIGHEST = lax.Precision.HIGHEST

N_DEV = 8
CHUNK = 64
N_PAST = 8
QBLK = 4 * CHUNK
KPAD = N_PAST * CHUNK
WIN = KPAD + QBLK
TAB = 1024
ATTN_HEAD_DIM = 64
ATTN_HEADS_PER_STEP = 4
REC_HEAD_DIM = 128
SUB = 16
ROWS = 8
LANE = 128
EPS = 1e-5
ALPHA = 2.0 ** 0.25
ADAM_LR, ADAM_B1, ADAM_B2, ADAM_EPS, ADAM_WD, ADAM_STEP = 0.001, 0.9, 0.999, 1e-08, 0.01, 10
NEG = -1e30
VMEM_LIMIT = 56 * 1024 * 1024


def _sds(shape, dtype):
    return jax.ShapeDtypeStruct(tuple(shape), dtype)


def _tile(n, pref, mult):
    best = None
    for t in range(mult, min(n, pref) + 1, mult):
        if n % t == 0:
            best = t
    return n if best is None else best


def _params(sem=None, big=False):
    kw = {}
    if sem is not None:
        kw["dimension_semantics"] = sem
    if big:
        kw["vmem_limit_bytes"] = VMEM_LIMIT
    return pltpu.CompilerParams(**kw)


def _sigmoid(v):
    return 1.0 / (1.0 + jnp.exp(-v))


def _dot(a, b, dims, precision=None):
    return lax.dot_general(a, b, (dims, ((), ())), preferred_element_type=F32, precision=precision)


NN = ((1,), (0,))
NT = ((1,), (1,))
TN = ((0,), (0,))


def _ln(v):
    mu = jnp.mean(v, axis=-1, keepdims=True)
    d = v - mu
    rstd = lax.rsqrt(jnp.mean(d * d, axis=-1, keepdims=True) + EPS)
    return d * rstd, rstd


def _ln_bwd(dxh, xh, rstd):
    return rstd * (dxh - jnp.mean(dxh, axis=-1, keepdims=True) - xh * jnp.mean(dxh * xh, axis=-1, keepdims=True))


def _colsum(v):
    return jnp.sum(v, axis=0, keepdims=True)


def _ln_mod(x2, mod6):
    T, D = x2.shape
    tm = _tile(T, 256, 8)

    def body(x_ref, mod_ref, o_ref):
        xh, _ = _ln(x_ref[...])
        o_ref[...] = (xh * (1.0 + mod_ref[1:2, :]) + mod_ref[0:1, :]).astype(BF16)

    return pl.pallas_call(
        body, grid=(T // tm,), name="ln_mod",
        in_specs=[pl.BlockSpec((tm, D), lambda i: (i, 0)), pl.BlockSpec((6, D), lambda i: (0, 0))],
        out_specs=pl.BlockSpec((tm, D), lambda i: (i, 0)),
        out_shape=_sds((T, D), BF16), compiler_params=_params(("parallel",)),
    )(x2, mod6)


def _mid_fwd(x2, mix, mod6, ln1_g, ln1_b):
    T, D = x2.shape
    tm = _tile(T, 256, 8)

    def body(x_ref, mix_ref, mod_ref, g_ref, b_ref, x1_ref, h2_ref):
        zh, _ = _ln(ALPHA * x_ref[...] + mod_ref[2:3, :] * mix_ref[...])
        x1 = zh * g_ref[...] + b_ref[...]
        x1_ref[...] = x1
        xh, _ = _ln(x1)
        h2_ref[...] = (xh * (1.0 + mod_ref[4:5, :]) + mod_ref[3:4, :]).astype(BF16)

    row = pl.BlockSpec((tm, D), lambda i: (i, 0))
    vec = pl.BlockSpec((1, D), lambda i: (0, 0))
    return pl.pallas_call(
        body, grid=(T // tm,), name="mid_fwd",
        in_specs=[row, row, pl.BlockSpec((6, D), lambda i: (0, 0)), vec, vec],
        out_specs=[row, row],
        out_shape=[_sds((T, D), F32), _sds((T, D), BF16)], compiler_params=_params(("parallel",)),
    )(x2, mix, mod6, ln1_g, ln1_b)


def _final(x1, ff, mod6, ln2_g, ln2_b, tgt):
    T, D = x1.shape
    tm = _tile(T, 256, 8)

    def body(x1_ref, ff_ref, mod_ref, g_ref, b_ref, t_ref, dff_ref, dx1_ref, vec_ref):
        @pl.when(pl.program_id(0) == 0)
        def _():
            vec_ref[...] = jnp.zeros_like(vec_ref)

        ff_v = ff_ref[...]
        gate2 = mod_ref[5:6, :]
        zh, rstd = _ln(ALPHA * x1_ref[...] + gate2 * ff_v)
        err = zh * g_ref[...] + b_ref[...] - t_ref[...]
        dy = err * (1.0 / D)
        dz = _ln_bwd(dy * g_ref[...], zh, rstd)
        dff_ref[...] = (gate2 * dz).astype(BF16)
        dx1_ref[...] = ALPHA * dz
        vec_ref[0:1, :] += _colsum(dy * zh)
        vec_ref[1:2, :] += _colsum(dy)
        vec_ref[2:3, :] += _colsum(dz * ff_v)
        vec_ref[3:4, :] += _colsum(err * err) * (0.5 / D)

    row = pl.BlockSpec((tm, D), lambda i: (i, 0))
    vec = pl.BlockSpec((1, D), lambda i: (0, 0))
    return pl.pallas_call(
        body, grid=(T // tm,), name="final_fwd_bwd",
        in_specs=[row, row, pl.BlockSpec((6, D), lambda i: (0, 0)), vec, vec, row],
        out_specs=[row, row, pl.BlockSpec((8, D), lambda i: (0, 0))],
        out_shape=[_sds((T, D), BF16), _sds((T, D), F32), _sds((8, D), F32)],
        compiler_params=_params(("arbitrary",)),
    )(x1, ff, mod6, ln2_g, ln2_b, tgt)


def _mid_bwd(x2, mix, x1, dx1a, dh2, mod6, ln1_g):
    T, D = x2.shape
    tm = _tile(T, 256, 8)

    def body(x_ref, mix_ref, x1_ref, dx1a_ref, dh2_ref, mod_ref, g_ref, dmix_ref, dxa_ref, vec_ref):
        @pl.when(pl.program_id(0) == 0)
        def _():
            vec_ref[...] = jnp.zeros_like(vec_ref)

        dh2 = dh2_ref[...]
        xh, rstd = _ln(x1_ref[...])
        dx1 = dx1a_ref[...] + _ln_bwd(dh2 * (1.0 + mod_ref[4:5, :]), xh, rstd)
        mix_v = mix_ref[...]
        gate1 = mod_ref[2:3, :]
        zh, rstdz = _ln(ALPHA * x_ref[...] + gate1 * mix_v)
        dz = _ln_bwd(dx1 * g_ref[...], zh, rstdz)
        dmix_ref[...] = (gate1 * dz).astype(BF16)
        dxa_ref[...] = ALPHA * dz
        vec_ref[0:1, :] += _colsum(dh2 * xh)
        vec_ref[1:2, :] += _colsum(dh2)
        vec_ref[2:3, :] += _colsum(dx1 * zh)
        vec_ref[3:4, :] += _colsum(dx1)
        vec_ref[4:5, :] += _colsum(dz * mix_v)

    row = pl.BlockSpec((tm, D), lambda i: (i, 0))
    vec = pl.BlockSpec((1, D), lambda i: (0, 0))
    return pl.pallas_call(
        body, grid=(T // tm,), name="mid_bwd",
        in_specs=[row, row, row, row, row, pl.BlockSpec((6, D), lambda i: (0, 0)), vec],
        out_specs=[row, row, pl.BlockSpec((8, D), lambda i: (0, 0))],
        out_shape=[_sds((T, D), BF16), _sds((T, D), F32), _sds((8, D), F32)],
        compiler_params=_params(("arbitrary",)),
    )(x2, mix, x1, dx1a, dh2, mod6, ln1_g)


def _first_bwd(x2, dh1, dxa, mod6):
    T, D = x2.shape
    tm = _tile(T, 256, 8)

    def body(x_ref, dh1_ref, dxa_ref, mod_ref, gx_ref, vec_ref):
        @pl.when(pl.program_id(0) == 0)
        def _():
            vec_ref[...] = jnp.zeros_like(vec_ref)

        dh1 = dh1_ref[...]
        xh, rstd = _ln(x_ref[...])
        gx_ref[...] = dxa_ref[...] + _ln_bwd(dh1 * (1.0 + mod_ref[1:2, :]), xh, rstd)
        vec_ref[0:1, :] += _colsum(dh1 * xh)
        vec_ref[1:2, :] += _colsum(dh1)

    row = pl.BlockSpec((tm, D), lambda i: (i, 0))
    return pl.pallas_call(
        body, grid=(T // tm,), name="first_bwd",
        in_specs=[row, row, row, pl.BlockSpec((6, D), lambda i: (0, 0))],
        out_specs=[row, pl.BlockSpec((8, D), lambda i: (0, 0))],
        out_shape=[_sds((T, D), F32), _sds((8, D), F32)],
        compiler_params=_params(("arbitrary",)),
    )(x2, dh1, dxa, mod6)


def _slot(j):
    return (j % 2) * 4 + j // 2


def _mm_gathered(a, wg, name):
    M, K = a.shape
    _, _, ns = wg.shape
    tm = _tile(M, 512, 16)

    def body(a_ref, w_ref, o_ref):
        o_ref[...] = _dot(a_ref[...], w_ref[...], NN)

    return pl.pallas_call(
        body, grid=(N_DEV, M // tm), name=name,
        in_specs=[pl.BlockSpec((tm, K), lambda j, i: (i, 0)), pl.BlockSpec((None, K, ns), lambda j, i: (j, 0, 0))],
        out_specs=pl.BlockSpec((tm, ns), lambda j, i: (i, j)),
        out_shape=_sds((M, N_DEV * ns), F32), compiler_params=_params(("parallel", "parallel"), big=True),
    )(a, wg)


def _mm_nn(a, b, name):
    M, K = a.shape
    _, N = b.shape
    tm, tn, tk = _tile(M, 512, 16), _tile(N, 1024, LANE), _tile(K, 2048, LANE)

    def body(a_ref, b_ref, o_ref):
        @pl.when(pl.program_id(2) == 0)
        def _():
            o_ref[...] = jnp.zeros_like(o_ref)

        o_ref[...] += _dot(a_ref[...], b_ref[...], NN)

    return pl.pallas_call(
        body, grid=(M // tm, N // tn, K // tk), name=name,
        in_specs=[pl.BlockSpec((tm, tk), lambda i, j, k: (i, k)), pl.BlockSpec((tk, tn), lambda i, j, k: (k, j))],
        out_specs=pl.BlockSpec((tm, tn), lambda i, j, k: (i, j)),
        out_shape=_sds((M, N), F32), compiler_params=_params(("parallel", "parallel", "arbitrary"), big=True),
    )(a, b)


def _mm_nt(a, b, name):
    M, K = a.shape
    N, _ = b.shape
    tm, tn = _tile(M, 512, 16), _tile(N, 1024, LANE)

    def body(a_ref, b_ref, o_ref):
        o_ref[...] = _dot(a_ref[...], b_ref[...], NT)

    return pl.pallas_call(
        body, grid=(M // tm, N // tn), name=name,
        in_specs=[pl.BlockSpec((tm, K), lambda i, j: (i, 0)), pl.BlockSpec((tn, K), lambda i, j: (j, 0))],
        out_specs=pl.BlockSpec((tm, tn), lambda i, j: (i, j)),
        out_shape=_sds((M, N), F32), compiler_params=_params(("parallel", "parallel"), big=True),
    )(a, b)


def _mm_swiglu(h2, wg):
    M, K = h2.shape
    _, _, ns = wg.shape
    half = N_DEV // 2
    tm = _tile(M, 256, 16)

    def body(a_ref, wgate_ref, wup_ref, gu_ref, act_ref):
        a = a_ref[...]
        g = _dot(a, wgate_ref[...], NN)
        u = _dot(a, wup_ref[...], NN)
        gu_ref[0] = g
        gu_ref[1] = u
        act_ref[...] = (g * _sigmoid(g) * u).astype(BF16)

    return pl.pallas_call(
        body, grid=(half, M // tm), name="ffn_in_swiglu",
        in_specs=[pl.BlockSpec((tm, K), lambda j, i: (i, 0)),
                  pl.BlockSpec((None, K, ns), lambda j, i: (j, 0, 0)),
                  pl.BlockSpec((None, K, ns), lambda j, i: (j + half, 0, 0))],
        out_specs=[pl.BlockSpec((2, tm, ns), lambda j, i: (0, i, j)), pl.BlockSpec((tm, ns), lambda j, i: (i, j))],
        out_shape=[_sds((2, M, half * ns), F32), _sds((M, half * ns), BF16)],
        compiler_params=_params(("parallel", "parallel"), big=True),
    )(h2, wg, wg)


def _mm_swiglu_bwd(dff, w2, gu):
    M, K = dff.shape
    F = w2.shape[0]
    tm, tn = _tile(M, 512, 16), _tile(F, 1408, LANE)

    def body(a_ref, b_ref, gu_ref, du_ref):
        da = _dot(a_ref[...], b_ref[...], NT)
        g = gu_ref[0]
        u = gu_ref[1]
        sg = _sigmoid(g)
        du_ref[0] = (da * u * (sg * (1.0 + g * (1.0 - sg)))).astype(BF16)
        du_ref[1] = (da * (g * sg)).astype(BF16)

    return pl.pallas_call(
        body, grid=(F // tn, M // tm), name="ffn_out_bwd_swiglu",
        in_specs=[pl.BlockSpec((tm, K), lambda j, i: (i, 0)), pl.BlockSpec((tn, K), lambda j, i: (j, 0)),
                  pl.BlockSpec((2, tm, tn), lambda j, i: (0, i, j))],
        out_specs=pl.BlockSpec((2, tm, tn), lambda j, i: (0, i, j)),
        out_shape=_sds((2, M, F), BF16), compiler_params=_params(("parallel", "parallel"), big=True),
    )(dff, w2, gu)


ORDER_ONLY = pl.BlockSpec(memory_space=pl.ANY)


def _mm_tn_rows(dep, a, b, rs, name):
    M, Ka = a.shape
    _, N = b.shape
    tm = _tile(M, 1024, 16)

    def body(_, a_ref, b_ref, o_ref, acc_ref):
        m = pl.program_id(1)

        @pl.when(m == 0)
        def _():
            acc_ref[...] = jnp.zeros_like(acc_ref)

        acc_ref[...] += _dot(a_ref[...], b_ref[...], TN)

        @pl.when(m == pl.num_programs(1) - 1)
        def _():
            o_ref[0, 0] = acc_ref[0:rs, :].astype(BF16)
            o_ref[1, 0] = acc_ref[rs:2 * rs, :].astype(BF16)

    return pl.pallas_call(
        body, grid=(N_DEV // 2, M // tm), name=name,
        in_specs=[ORDER_ONLY, pl.BlockSpec((tm, 2 * rs), lambda ch, m: (m, ch)),
                  pl.BlockSpec((tm, N), lambda ch, m: (m, 0))],
        out_specs=pl.BlockSpec((2, 1, rs, N), lambda ch, m: (0, ch, 0, 0)),
        out_shape=_sds((2, N_DEV // 2, rs, N), BF16),
        scratch_shapes=[pltpu.VMEM((2 * rs, N), F32)],
        compiler_params=_params(("parallel", "arbitrary"), big=True),
    )(dep, a, b)


def _mm_gathered_nt(dep, a, a_spec, wg, M, tm, name):
    _, K, ns = wg.shape

    def body(_, a_ref, w_ref, o_ref):
        @pl.when(pl.program_id(1) == 0)
        def _():
            o_ref[...] = jnp.zeros_like(o_ref)

        o_ref[...] += _dot(a_ref[...], w_ref[...], NT)

    return pl.pallas_call(
        body, grid=(M // tm, N_DEV), name=name,
        in_specs=[ORDER_ONLY, a_spec(tm, ns), pl.BlockSpec((None, K, ns), lambda i, j: (j, 0, 0))],
        out_specs=pl.BlockSpec((tm, K), lambda i, j: (i, 0)),
        out_shape=_sds((M, K), F32), compiler_params=_params(("parallel", "arbitrary"), big=True),
    )(dep, a, wg)


def _mm_tn_gathered(dep, h, a, a_spec, ns, tm, name):
    M, K = h.shape

    def body(_, h_ref, a_ref, o_ref, acc_ref):
        m = pl.program_id(1)

        @pl.when(m == 0)
        def _():
            acc_ref[...] = jnp.zeros_like(acc_ref)

        acc_ref[...] += _dot(h_ref[...], a_ref[...], TN)

        @pl.when(m == pl.num_programs(1) - 1)
        def _():
            o_ref[...] = acc_ref[...].astype(BF16)

    return pl.pallas_call(
        body, grid=(N_DEV, M // tm), name=name,
        in_specs=[ORDER_ONLY, pl.BlockSpec((tm, K), lambda j, m: (m, 0)), a_spec(tm, ns)],
        out_specs=pl.BlockSpec((None, K, ns), lambda j, m: (_slot(j), 0, 0)),
        out_shape=_sds((N_DEV, K, ns), BF16),
        scratch_shapes=[pltpu.VMEM((K, ns), F32)],
        compiler_params=_params(("parallel", "arbitrary"), big=True),
    )(dep, h, a)


def _bias_onehot(rbp, max_rel):
    r = lax.broadcasted_iota(jnp.int32, (rbp, TAB), 0)
    m = lax.broadcasted_iota(jnp.int32, (rbp, TAB), 1)
    dist = KPAD - jnp.where(m < WIN, m, m - TAB)
    return (r == jnp.clip(dist, -max_rel, max_rel) + max_rel).astype(F32)


def _attn_setup(i, hp, k_ref, v_ref, gv_ref, kpad, vpad, bias):
    ls = slice(i * ATTN_HEAD_DIM, (i + 1) * ATTN_HEAD_DIM)
    kpad[i][0:KPAD, :] = jnp.zeros((KPAD, ATTN_HEAD_DIM), BF16)
    vpad[i][0:KPAD, :] = jnp.zeros((KPAD, ATTN_HEAD_DIM), BF16)
    kpad[i][KPAD:, :] = k_ref[:, ls].astype(BF16)
    vpad[i][KPAD:, :] = v_ref[:, ls].astype(BF16)
    gvrow = gv_ref[pl.ds(hp * ATTN_HEADS_PER_STEP + i, 1), :]
    tab = pltpu.roll(jnp.broadcast_to(gvrow, (QBLK, TAB)), 0, 1, stride=1, stride_axis=0)
    row = lax.broadcasted_iota(jnp.int32, (QBLK, WIN), 0)
    col = lax.broadcasted_iota(jnp.int32, (QBLK, WIN), 1)
    first = jnp.bitwise_and(row, -CHUNK)
    seen = jnp.logical_and(col >= first, col < first + (N_PAST + 1) * CHUNK)
    bias[i][...] = jnp.where(seen, tab[:, 0:WIN], NEG)


def _attn_probs(b, q_ref, kpad, vpad, bias, col):
    pair = range(ATTN_HEADS_PER_STEP)
    ls = [slice(i * ATTN_HEAD_DIM, (i + 1) * ATTN_HEAD_DIM) for i in pair]
    r0 = pl.multiple_of(b * QBLK, QBLK)
    q = [q_ref[pl.ds(r0, QBLK), ls[i]].astype(BF16) for i in pair]
    kw = [kpad[i][pl.ds(r0, WIN), :] for i in pair]
    vw = [vpad[i][pl.ds(r0, WIN), :] for i in pair]
    s = [_dot(q[i], kw[i], NT) * (ATTN_HEAD_DIM ** -0.5) + bias[i][...] for i in pair]
    s = [jnp.where(col >= KPAD - r0, s[i], NEG) for i in pair]
    p = [jnp.exp(s[i] - jnp.max(s[i], axis=-1, keepdims=True)) for i in pair]
    pn = [p[i] / jnp.sum(p[i], axis=-1, keepdims=True) for i in pair]
    return r0, ls, q, kw, vw, pn


def _attn_fwd(proj, gv, ga, AW):
    T = proj.shape[0]
    AH = ATTN_HEADS_PER_STEP
    W = AH * ATTN_HEAD_DIM
    HP = AW // W

    def body(q_ref, k_ref, v_ref, gv_ref, ga_ref, o_ref, *scratch):
        kpad, vpad, bias = (scratch[k * AH:(k + 1) * AH] for k in range(3))
        hp = pl.program_id(0)
        for i in range(AH):
            _attn_setup(i, hp, k_ref, v_ref, gv_ref, kpad, vpad, bias)
        col = lax.broadcasted_iota(jnp.int32, (QBLK, WIN), 1)

        def block(b, carry):
            pair = range(AH)
            r0, ls, _, _, vw, pn = _attn_probs(b, q_ref, kpad, vpad, bias, col)
            o = [_dot(pn[i].astype(BF16), vw[i], NN) for i in pair]
            r = [lax.rsqrt(jnp.mean(o[i] * o[i], axis=-1, keepdims=True) + EPS) for i in pair]
            outs = [o[i] * r[i] * ga_ref[0:1, ls[i]] for i in pair]
            o_ref[pl.ds(r0, QBLK), :] = jnp.concatenate(outs, axis=1).astype(BF16)
            return carry

        lax.fori_loop(0, T // QBLK, block, 0)

    blk = lambda off: pl.BlockSpec((T, W), lambda hp: (0, off + hp))
    return pl.pallas_call(
        body, grid=(HP,), name="attn_fwd",
        in_specs=[blk(0), blk(HP), blk(2 * HP), pl.BlockSpec(gv.shape, lambda hp: (0, 0)),
                  pl.BlockSpec((1, W), lambda hp: (0, hp))],
        out_specs=pl.BlockSpec((T, W), lambda hp: (0, hp)),
        out_shape=_sds((T, AW), BF16),
        scratch_shapes=[pltpu.VMEM((T + KPAD, ATTN_HEAD_DIM), BF16)] * (2 * AH) + [pltpu.VMEM((QBLK, WIN), F32)] * AH,
        compiler_params=_params(("parallel",), big=True),
    )(proj, proj, proj, gv, ga)


def _attn_bwd(proj, dmixin, gv, ga, AW):
    T = proj.shape[0]
    AH = ATTN_HEADS_PER_STEP
    W = AH * ATTN_HEAD_DIM
    HP = AW // W
    scale = ATTN_HEAD_DIM ** -0.5

    def body(q_ref, k_ref, v_ref, dn_ref, gv_ref, ga_ref, dq_ref, dk_ref, dv_ref, dgv_ref, dga_ref, *scratch):
        kpad, vpad, dkacc, dvacc, bias, dbias = (scratch[k * AH:(k + 1) * AH] for k in range(6))
        hp = pl.program_id(0)
        for i in range(AH):
            _attn_setup(i, hp, k_ref, v_ref, gv_ref, kpad, vpad, bias)
            dkacc[i][...] = jnp.zeros_like(dkacc[i])
            dvacc[i][...] = jnp.zeros_like(dvacc[i])
            dbias[i][...] = jnp.zeros_like(dbias[i])
        dga_ref[...] = jnp.zeros_like(dga_ref)
        col = lax.broadcasted_iota(jnp.int32, (QBLK, WIN), 1)

        def block(b, carry):
            pair = range(AH)
            r0, lss, qs, kws, vws, pns = _attn_probs(b, q_ref, kpad, vpad, bias, col)
            pn_b = [pns[i].astype(BF16) for i in pair]
            o = [_dot(pn_b[i], vws[i], NN) for i in pair]
            r = [lax.rsqrt(jnp.mean(o[i] * o[i], axis=-1, keepdims=True) + EPS) for i in pair]
            dn = [dn_ref[pl.ds(r0, QBLK), lss[i]] for i in pair]
            for i in pair:
                dga_ref[i:i + 1, :] += _colsum(dn[i] * o[i] * r[i])
            a = [dn[i] * ga_ref[0:1, lss[i]] for i in pair]
            do_b = [(r[i] * (a[i] - o[i] * (r[i] * r[i]) * jnp.mean(a[i] * o[i], axis=-1, keepdims=True))).astype(BF16)
                    for i in pair]
            dp = [_dot(do_b[i], vws[i], NT) for i in pair]
            for i in pair:
                dvacc[i][pl.ds(r0, WIN), :] += _dot(pn_b[i], do_b[i], TN)
            ds = [pns[i] * (dp[i] - jnp.sum(pns[i] * dp[i], axis=-1, keepdims=True)) for i in pair]
            for i in pair:
                dbias[i][...] += ds[i]
            ds_b = [ds[i].astype(BF16) for i in pair]
            dq = [_dot(ds_b[i], kws[i], NN) * scale for i in pair]
            dq_ref[pl.ds(r0, QBLK), :] = jnp.concatenate(dq, axis=1).astype(BF16)
            for i in pair:
                dkacc[i][pl.ds(r0, WIN), :] += _dot(ds_b[i], qs[i], TN) * scale
            return carry

        lax.fori_loop(0, T // QBLK, block, 0)

        rr = lax.broadcasted_iota(jnp.int32, (QBLK, QBLK), 0)
        cc = lax.broadcasted_iota(jnp.int32, (QBLK, QBLK), 1)
        flip = (rr + cc == QBLK - 1).astype(BF16)
        for i in range(AH):
            ls = slice(i * ATTN_HEAD_DIM, (i + 1) * ATTN_HEAD_DIM)
            dk_ref[:, ls] = dkacc[i][KPAD:, :].astype(BF16)
            dv_ref[:, ls] = dvacc[i][KPAD:, :].astype(BF16)
            full = jnp.concatenate([dbias[i][...], jnp.zeros((QBLK, TAB - WIN), F32)], axis=1)
            hi = full.astype(BF16)
            lo = (full - hi.astype(F32)).astype(BF16)
            rev = _dot(flip, hi, NN) + _dot(flip, lo, NN)
            dgv_ref[i:i + 1, :] = _colsum(pltpu.roll(rev, TAB - (QBLK - 1), 1, stride=1, stride_axis=0))

    blk = lambda off: pl.BlockSpec((T, W), lambda hp: (0, off + hp))
    accs = lambda dt: [pltpu.VMEM((T + KPAD, ATTN_HEAD_DIM), dt)] * AH
    return pl.pallas_call(
        body, grid=(HP,), name="attn_bwd",
        in_specs=[blk(0), blk(HP), blk(2 * HP), blk(0), pl.BlockSpec(gv.shape, lambda hp: (0, 0)),
                  pl.BlockSpec((1, W), lambda hp: (0, hp))],
        out_specs=[blk(0), blk(0), blk(0), pl.BlockSpec((None, AH, TAB), lambda hp: (hp, 0, 0)),
                   pl.BlockSpec((None, AH, ATTN_HEAD_DIM), lambda hp: (hp, 0, 0))],
        out_shape=[_sds((T, AW), BF16), _sds((T, AW), BF16), _sds((T, AW), BF16),
                   _sds((HP, AH, TAB), F32), _sds((HP, AH, ATTN_HEAD_DIM), F32)],
        scratch_shapes=accs(BF16) + accs(BF16) + accs(F32) + accs(F32) + [pltpu.VMEM((QBLK, WIN), F32)] * (2 * AH),
        compiler_params=_params(("parallel",), big=True),
    )(proj, proj, proj, dmixin, gv, ga)


def _ltri():
    r = lax.broadcasted_iota(jnp.int32, (CHUNK, CHUNK), 0)
    c = lax.broadcasted_iota(jnp.int32, (CHUNK, CHUNK), 1)
    return (c <= r).astype(BF16)


def _tri_dot(tri, v, dims):
    hi = v.astype(BF16)
    lo = (v - hi.astype(F32)).astype(BF16)
    return _dot(tri, hi, dims) + _dot(tri, lo, dims)


HEADS_PER_STEP = 2


def _alternate(stages):
    live = list(stages)
    while live:
        for g in list(live):
            if next(g, StopIteration) is StopIteration:
                live.remove(g)


def _hgrn_gates(n, ls, q_ref, f_ref, lb_ref, ltri):
    r0 = pl.multiple_of(n * CHUNK, CHUNK)
    rows = pl.ds(r0, CHUNK)
    lb = lb_ref[:, ls]
    qb = q_ref[rows, ls]
    sg = _sigmoid(f_ref[rows, ls])
    f = lb + (1.0 - lb) * sg
    sq = _sigmoid(qb)
    b = _tri_dot(ltri, jnp.log(f), NN)
    return rows, lb, qb, sg, f, 1.0 - f, sq, qb * sq, b


def _hgrn_specs(T, RW, AW):
    HG = HEADS_PER_STEP
    W = HG * LANE
    base = 3 * AW // W
    blk_in = lambda off: pl.BlockSpec((T, W), lambda g: (0, base + off + g))
    col = pl.BlockSpec((T, W), lambda g: (0, g))
    return HG, W, RW // W, blk_in, col


def _hgrn_fwd(proj, lb, gn, AW, RW):
    T = proj.shape[0]
    RH, NC, NSUB = RW // LANE, T // CHUNK, CHUNK // SUB
    HG, W, NG, blk_in, col = _hgrn_specs(T, RW, AW)

    def body(q_ref, f_ref, i_ref, g_ref, lb_ref, gn_ref, mix_ref, o_ref, stall_ref, st_all, bs_all, kks_all, ics_all):
        st_all[...] = jnp.zeros_like(st_all)
        ltri = _ltri()
        rowi = lax.broadcasted_iota(jnp.int32, (SUB, 1), 0)

        def one_head(h, n):
            ls = slice(h * LANE, (h + 1) * LANE)
            st, bs, kks, ics = st_all.at[h], bs_all.at[h], kks_all.at[h], ics_all.at[h]
            rows, _, _, _, _, kk, _, qs, b = _hgrn_gates(n, ls, q_ref, f_ref, lb_ref, ltri)
            ic = i_ref[rows, ls]
            stv = st[...]
            stall_ref[h, n] = stv
            bs[...] = b
            kks[...] = kk
            ics[...] = ic
            yield
            o = _dot((qs * jnp.exp(b)).astype(BF16), stv.astype(BF16), NT)
            yield
            ic_b = ic.astype(BF16)
            pieces = []
            for blk in range(NSUB):
                s0 = blk * SUB
                bI, qI = b[s0:s0 + SUB], qs[s0:s0 + SUB]
                if blk == 0:
                    oI = jnp.zeros((SUB, LANE), F32)
                else:
                    ref = bs[s0 - 1:s0, :]
                    qt = (qI * jnp.exp(bI - ref)).astype(BF16)
                    kt = (kk[0:s0] * jnp.exp(ref - b[0:s0])).astype(BF16)
                    oI = _dot(_dot(qt, kt, NT).astype(BF16), ic_b[0:s0], NN)
                    yield
                acc = [oI[g * ROWS:(g + 1) * ROWS] for g in range(SUB // ROWS)]
                for s in range(SUB):
                    sr = s0 + s
                    g0 = s // ROWS
                    lo = g0 * ROWS
                    e = jnp.exp(jnp.minimum(bI[lo:] - bs[sr:sr + 1, :], 0.0))
                    a = jnp.sum(qI[lo:] * kks[sr:sr + 1, :] * e, axis=-1, keepdims=True)
                    add = jnp.where(rowi[lo:] >= s, a, 0.0) * ics[sr:sr + 1, :]
                    for g in range(g0, SUB // ROWS):
                        acc[g] = acc[g] + add[(g - g0) * ROWS:(g - g0 + 1) * ROWS]
                    yield
                pieces.extend(acc)
            o = o + jnp.concatenate(pieces, axis=0)
            bl = bs[CHUNK - 1:CHUNK, :]
            kd = (kk * jnp.exp(bl - b)).astype(BF16)
            st[...] = stv * jnp.exp(bl) + _dot(ic_b, kd, TN)
            yield
            o_ref[rows, ls] = o
            r = lax.rsqrt(jnp.mean(o * o, axis=-1, keepdims=True) + EPS)
            gb = g_ref[rows, ls]
            mix_ref[rows, ls] = (o * r * gn_ref[...] * (gb * _sigmoid(gb))).astype(BF16)

        def chunk(n, carry):
            _alternate([one_head(h, n) for h in range(HG)])
            return carry

        lax.fori_loop(0, NC, chunk, 0)

    tile = pltpu.VMEM((HG, CHUNK, LANE), F32)
    return pl.pallas_call(
        body, grid=(NG,), name="hgrn_fwd",
        in_specs=[blk_in(0), blk_in(NG), blk_in(2 * NG), blk_in(3 * NG), pl.BlockSpec((1, W), lambda g: (0, g)),
                  pl.BlockSpec((1, LANE), lambda g: (0, 0))],
        out_specs=[col, col, pl.BlockSpec((HG, NC, LANE, LANE), lambda g: (g, 0, 0, 0))],
        out_shape=[_sds((T, RW), BF16), _sds((T, RW), F32), _sds((RH, NC, LANE, LANE), F32)],
        scratch_shapes=[pltpu.VMEM((HG, LANE, LANE), F32), tile, tile, tile],
        compiler_params=_params(("parallel",), big=True),
    )(proj, proj, proj, proj, lb, gn)


def _hgrn_bwd(proj, dmixin, o_b, st_all, lb, gn, AW, RW):
    T = proj.shape[0]
    RH, NC, NSUB = RW // LANE, T // CHUNK, CHUNK // SUB
    HG, W, NG, blk_in, col = _hgrn_specs(T, RW, AW)

    def body(q_ref, f_ref, i_ref, g_ref, o_ref, dn_ref, stall_ref, lb_ref, gn_ref,
             dq_ref, df_ref, di_ref, dg_ref, dlb_ref, dgn_ref, dst_all, bs_all, kks_all, ics_all, p2_all, dic_all):
        dst_all[...] = jnp.zeros_like(dst_all)
        dlb_ref[...] = jnp.zeros_like(dlb_ref)
        dgn_ref[...] = jnp.zeros_like(dgn_ref)
        ltri = _ltri()
        rowi = lax.broadcasted_iota(jnp.int32, (SUB, 1), 0)
        last = lax.broadcasted_iota(jnp.int32, (CHUNK, 1), 0) == CHUNK - 1

        def one_head(h, n):
            ls = slice(h * LANE, (h + 1) * LANE)
            dst, bs, kks, ics = dst_all.at[h], bs_all.at[h], kks_all.at[h], ics_all.at[h]
            p2, dic = p2_all.at[h], dic_all.at[h]
            rows, lbv, qb, sg, f, kk, sq, qs, b = _hgrn_gates(n, ls, q_ref, f_ref, lb_ref, ltri)
            ic = i_ref[rows, ls]
            stv = stall_ref[h, n]
            dstv = dst[...]
            o = o_ref[rows, ls]
            dn = dn_ref[rows, ls]
            gb = g_ref[rows, ls]
            sgb = _sigmoid(gb)
            r = lax.rsqrt(jnp.mean(o * o, axis=-1, keepdims=True) + EPS)
            gnv = gn_ref[...]
            dg_ref[rows, ls] = (dn * (o * r * gnv) * (sgb * (1.0 + gb * (1.0 - sgb)))).astype(BF16)
            dy = dn * (gb * sgb)
            dgn_ref[h] += _colsum(dy * o * r)
            a_ = dy * gnv
            do = r * (a_ - o * (r * r) * jnp.mean(a_ * o, axis=-1, keepdims=True))
            do_b = do.astype(BF16)
            bs[...] = b
            kks[...] = kk
            ics[...] = ic
            yield
            ic_b = ic.astype(BF16)
            eb = jnp.exp(b)
            bl = bs[CHUNK - 1:CHUNK, :]
            ebl = jnp.exp(bl)
            dec = jnp.exp(bl - b)
            kd = (kk * dec).astype(BF16)
            dst_b = dstv.astype(BF16)
            dqs = _dot(do_b, stv.astype(BF16), NN) * eb
            dkk2 = _dot(ic_b, dst_b, NN) * dec
            dic[...] = _dot(kd, dst_b, NT)
            dbl = ebl * _colsum(stv * dstv) + _colsum(kk * dkk2)
            dst[...] = dstv * ebl + _dot(do_b, (qs * eb).astype(BF16), TN)
            yield
            p2[...] = jnp.zeros_like(p2)
            p1_pieces = []
            for blk in range(NSUB):
                s0 = blk * SUB
                bI, qI, doI = b[s0:s0 + SUB], qs[s0:s0 + SUB], do[s0:s0 + SUB]
                if blk == 0:
                    p1 = jnp.zeros((SUB, LANE), F32)
                else:
                    ref = bs[s0 - 1:s0, :]
                    eq = jnp.exp(bI - ref)
                    ek = jnp.exp(ref - b[0:s0])
                    qt = (qI * eq).astype(BF16)
                    kt = (kk[0:s0] * ek).astype(BF16)
                    doI_b = doI.astype(BF16)
                    dic[0:s0, :] += _dot(_dot(qt, kt, NT).astype(BF16), doI_b, TN)
                    da = _dot(doI_b, ic_b[0:s0], NT).astype(BF16)
                    p1 = _dot(da, kt, NN) * eq
                    p2[0:s0, :] += _dot(da, qt, TN) * ek
                    yield
                acc = [p1[g * ROWS:(g + 1) * ROWS] for g in range(SUB // ROWS)]
                for s in range(SUB):
                    sr = s0 + s
                    g0 = s // ROWS
                    lo = g0 * ROWS
                    keep = rowi[lo:] >= s
                    kk_s = kks[sr:sr + 1, :]
                    e = jnp.exp(jnp.minimum(bI[lo:] - bs[sr:sr + 1, :], 0.0))
                    w = qI[lo:] * e
                    a = jnp.where(keep, jnp.sum(w * kk_s, axis=-1, keepdims=True), 0.0)
                    da_s = jnp.where(keep, jnp.sum(doI[lo:] * ics[sr:sr + 1, :], axis=-1, keepdims=True), 0.0)
                    add = da_s * kk_s * e
                    for g in range(g0, SUB // ROWS):
                        acc[g] = acc[g] + add[(g - g0) * ROWS:(g - g0 + 1) * ROWS]
                    p2[sr:sr + 1, :] += _colsum(da_s * w)
                    dic[sr:sr + 1, :] += _colsum(a * doI[lo:])
                    yield
                p1_pieces.extend(acc)
            dqs = dqs + jnp.concatenate(p1_pieces, axis=0)
            dkk = dkk2 + p2[...]
            db = qs * dqs - kk * dkk + jnp.where(last, dbl, 0.0)
            dgl = _tri_dot(ltri, db, TN)
            yield
            dfv = dgl / f - dkk
            df_ref[rows, ls] = (dfv * (1.0 - lbv) * sg * (1.0 - sg)).astype(BF16)
            dlb_ref[:, ls] += _colsum(dfv * (1.0 - sg))
            dq_ref[rows, ls] = (dqs * (sq * (1.0 + qb * (1.0 - sq)))).astype(BF16)
            di_ref[rows, ls] = dic[...].astype(BF16)

        def chunk(k, carry):
            _alternate([one_head(h, NC - 1 - k) for h in range(HG)])
            return carry

        lax.fori_loop(0, NC, chunk, 0)

    tile = pltpu.VMEM((HG, CHUNK, LANE), F32)
    return pl.pallas_call(
        body, grid=(NG,), name="hgrn_bwd",
        in_specs=[blk_in(0), blk_in(NG), blk_in(2 * NG), blk_in(3 * NG), col,
                  pl.BlockSpec((T, W), lambda g: (0, AW // W + g)),
                  pl.BlockSpec((HG, NC, LANE, LANE), lambda g: (g, 0, 0, 0)),
                  pl.BlockSpec((1, W), lambda g: (0, g)), pl.BlockSpec((1, LANE), lambda g: (0, 0))],
        out_specs=[col, col, col, col, pl.BlockSpec((1, W), lambda g: (0, g)),
                   pl.BlockSpec((HG, 1, LANE), lambda g: (g, 0, 0))],
        out_shape=[_sds((T, RW), BF16)] * 4 + [_sds((1, RW), F32), _sds((RH, 1, LANE), F32)],
        scratch_shapes=[pltpu.VMEM((HG, LANE, LANE), F32), tile, tile, tile, tile, tile],
        compiler_params=_params(("parallel",), big=True),
    )(proj, proj, proj, proj, o_b, dmixin, st_all, lb, gn)


def _prep(c, lb_logits, rb_pad, max_rel):
    D, RW = c.shape[-1], lb_logits.shape[-1]
    H, rbp = rb_pad.shape

    def body(c_ref, l_ref, rb_ref, cact_ref, lb_ref, gv_ref):
        cv = c_ref[...]
        cact_ref[...] = cv * _sigmoid(cv)
        lb_ref[...] = _sigmoid(l_ref[0:1, :] - l_ref[1:2, :])
        gv_ref[...] = _dot(rb_ref[...], _bias_onehot(rbp, max_rel), NN, HIGHEST)

    return pl.pallas_call(
        body, name="prep", out_shape=[_sds((1, D), F32), _sds((1, RW), F32), _sds((H, TAB), F32)],
    )(c, lb_logits, rb_pad)


def _mod_part(c_all, w_ada_s, b_ada_s):
    B, D = c_all.shape
    ns = w_ada_s.shape[1]
    tn = _tile(ns, 768, LANE)

    def body(c_ref, w_ref, b_ref, o_ref):
        o_ref[...] = _dot(c_ref[...], w_ref[...], NN) + b_ref[...]

    return pl.pallas_call(
        body, grid=(ns // tn,), name="mod_part",
        in_specs=[pl.BlockSpec((B, D), lambda j: (0, 0)), pl.BlockSpec((D, tn), lambda j: (0, j)),
                  pl.BlockSpec((1, tn), lambda j: (0, j))],
        out_specs=pl.BlockSpec((B, tn), lambda j: (0, j)),
        out_shape=_sds((B, ns), F32), compiler_params=_params(("parallel",)),
    )(c_all, w_ada_s, b_ada_s)


def _adam(w, g, m, v):
    m = ADAM_B1 * m + (1.0 - ADAM_B1) * g
    v = ADAM_B2 * v + (1.0 - ADAM_B2) * (g * g)
    m_hat = m * (1.0 / (1.0 - ADAM_B1 ** ADAM_STEP))
    v_hat = v * (1.0 / (1.0 - ADAM_B2 ** ADAM_STEP))
    return -ADAM_LR * (m_hat / (jnp.sqrt(v_hat) + ADAM_EPS) + ADAM_WD * w), m, v


def _adam_ada(c_all, dmod_s, w, m, v):
    B, D = c_all.shape
    ns = w.shape[1]
    tr, tn = _tile(D, 512, LANE), _tile(ns, 768, LANE)

    def body(c_ref, d_ref, w_ref, m_ref, v_ref, g_out, dw_out, m_out, v_out):
        g = _dot(c_ref[...], d_ref[...], TN)
        g_out[...] = g
        dw_out[...], m_out[...], v_out[...] = _adam(w_ref[...], g, m_ref[...], v_ref[...])

    big = pl.BlockSpec((tr, tn), lambda i, j: (i, j))
    return pl.pallas_call(
        body, grid=(D // tr, ns // tn), name="adam_w_ada",
        in_specs=[pl.BlockSpec((B, tr), lambda i, j: (0, i)), pl.BlockSpec((B, tn), lambda i, j: (0, j)),
                  big, big, big],
        out_specs=[big] * 4, out_shape=[_sds((D, ns), F32)] * 4,
        compiler_params=_params(("parallel", "parallel")),
    )(c_all, dmod_s, w, m, v)


def _adam_shard(parts, w, m, v, name):
    R, C = w.shape
    tr = _tile(R, 256, 16)

    def body(p_ref, w_ref, m_ref, v_ref, g_out, dw_out, m_out, v_out):
        g = p_ref[0].astype(F32)
        for k in range(1, N_DEV // 2):
            g = g + p_ref[k].astype(F32)
        g_out[...] = g
        dw_out[...], m_out[...], v_out[...] = _adam(w_ref[...], g, m_ref[...], v_ref[...])

    big = pl.BlockSpec((tr, C), lambda i: (i, 0))
    return pl.pallas_call(
        body, grid=(R // tr,), name=name,
        in_specs=[pl.BlockSpec((N_DEV // 2, tr, C), lambda i: (0, i, 0)), big, big, big],
        out_specs=[big] * 4, out_shape=[_sds((R, C), F32)] * 4,
        compiler_params=_params(("parallel",), big=True),
    )(parts, w, m, v)


def _pair_sum(g8, land, core, name):
    _, NCHIP, R, C = g8.shape
    tr = _tile(R, 1024, 16)

    def body(core_ref, g_ref, l_ref, o_ref):
        o_ref[...] = g_ref[...] + l_ref[...]

    return pl.pallas_call(
        body, name=name,
        grid_spec=pltpu.PrefetchScalarGridSpec(
            num_scalar_prefetch=1, grid=(NCHIP, R // tr),
            in_specs=[pl.BlockSpec((None, None, tr, C), lambda k, i, core_ref: (core_ref[0], k, i, 0)),
                      pl.BlockSpec((None, tr, C), lambda k, i, core_ref: (k, i, 0))],
            out_specs=pl.BlockSpec((None, tr, C), lambda k, i, core_ref: (k, i, 0))),
        out_shape=_sds((NCHIP, R, C), BF16), compiler_params=_params(("parallel", "parallel")),
    )(core, g8, land)


SMALL = ("b_ada", "rel_bias", "attn_norm_g", "lb_logits", "gnorm_g", "ln1_g", "ln1_b", "ln2_g", "ln2_b")


def _small_update(parts, loss_parts, lbv, ws, ms, vs, max_rel):
    n = len(SMALL)

    def body(*refs):
        part_refs = dict(zip(SMALL, refs[:n]))
        loss_in, lb_ref = refs[n], refs[n + 1]
        w_refs, m_refs, v_refs = refs[n + 2:2 * n + 2], refs[2 * n + 2:3 * n + 2], refs[3 * n + 2:4 * n + 2]
        outs = refs[4 * n + 2:]

        def total(ref):
            tot = ref[0]
            for k in range(1, N_DEV):
                tot = tot + ref[k]
            return tot

        outs[0][...] = jnp.sum(total(loss_in), axis=-1, keepdims=True)
        for idx, name in enumerate(SMALL):
            g = total(part_refs[name])
            if name == "rel_bias":
                g = _dot(g, _bias_onehot(w_refs[idx].shape[1], max_rel), NT, HIGHEST)
            elif name == "lb_logits":
                lb = lb_ref[...]
                sign = (1 - 2 * lax.broadcasted_iota(jnp.int32, (2, 1), 0)).astype(F32)
                g = sign * (g * lb * (1.0 - lb))
            elif name == "gnorm_g":
                g = _colsum(g)
            dw, mm, vv = _adam(w_refs[idx][...], g, m_refs[idx][...], v_refs[idx][...])
            outs[1 + 4 * idx][...] = g
            outs[2 + 4 * idx][...] = dw
            outs[3 + 4 * idx][...] = mm
            outs[4 + 4 * idx][...] = vv

    out_shape = [_sds((1, 1), F32)]
    for w in ws:
        out_shape += [_sds(w.shape, F32)] * 4
    return pl.pallas_call(body, name="small_update", out_shape=out_shape, compiler_params=_params(big=True))(
        *[parts[k] for k in SMALL], loss_parts, lbv, *ws, *ms, *vs)


def _place():
    x, y, c = lax.axis_index("x"), lax.axis_index("y"), lax.axis_index("c")
    return x, y, c, [(1 - x, y), (x, 1 - y), (1 - x, 1 - y)]


def _all_gather(shard, name):
    HBM = pl.BlockSpec(memory_space=pl.ANY)

    def body(x_ref, out_ref, send_sems, recv_sems, local_sem):
        x, y, c, chips = _place()
        me, sibling = (x, y, c), (x, y, 1 - c)

        def slot(px, py, pc):
            return out_ref.at[4 * px + 2 * py + pc]

        def copy(k, block, to, src=None):
            return pltpu.make_async_remote_copy(
                src_ref=slot(*block) if src is None else src, dst_ref=slot(*block),
                send_sem=send_sems.at[k], recv_sem=recv_sems.at[k], device_id=to, device_id_type=MESH)

        mine = pltpu.make_async_copy(x_ref, slot(*me), local_sem)
        mine.start()
        first = [copy(0, me, sibling, src=x_ref)]
        first += [copy(1 + j, me, (*chip, c), src=x_ref) for j, chip in enumerate(chips)]
        for cp in first:
            cp.start()
        passed = [copy(4 + j, (*chip, c), sibling) for j, chip in enumerate(chips)]
        for j, chip in enumerate(chips):
            copy(1 + j, (*chip, c), me).wait_recv()
            passed[j].start()
        copy(0, sibling, me).wait_recv()
        for j, chip in enumerate(chips):
            copy(4 + j, (*chip, 1 - c), me).wait_recv()
        for cp in first + passed:
            cp.wait_send()
        mine.wait()

    return pl.pallas_call(
        body, name=name, out_shape=_sds((N_DEV,) + shard.shape, shard.dtype),
        in_specs=[HBM], out_specs=HBM,
        scratch_shapes=[pltpu.SemaphoreType.DMA((7,)), pltpu.SemaphoreType.DMA((7,)), pltpu.SemaphoreType.DMA(())],
    )(shard)


SEM_SPEC = pl.BlockSpec(memory_space=pltpu.SEMAPHORE)
HBM_SPEC = pl.BlockSpec(memory_space=pltpu.HBM)
EFFECT = pltpu.SideEffectType.DATAFLOW_SIDE_EFFECTING


def _remote(src, dst, send_sems, recv_sems, k, dev):
    return pltpu.make_async_remote_copy(src_ref=src, dst_ref=dst, send_sem=send_sems.at[k], recv_sem=recv_sems.at[k],
                                        device_id=dev, device_id_type=MESH)


def _copy_start(name, bufs, plan, n, after):
    nb = len(bufs)

    def body(*refs):
        send_sems, recv_sems = refs[nb + 1], refs[nb + 2]
        for k, (src, dst, dev) in enumerate(plan(*refs[:nb])):
            _remote(src, dst, send_sems, recv_sems, k, dev).start()
        refs[-1][...] = jnp.zeros_like(refs[-1])

    out = pl.pallas_call(
        body, name=name,
        out_shape=(pltpu.SemaphoreType.DMA((n,)), pltpu.SemaphoreType.DMA((n,)),
                   *[pltpu.HBM(b.shape, b.dtype) for b in bufs], _sds((8, LANE), F32)),
        in_specs=[HBM_SPEC] * nb + [ORDER_ONLY],
        out_specs=(SEM_SPEC, SEM_SPEC, *[HBM_SPEC] * nb, pl.BlockSpec(memory_space=pltpu.VMEM)),
        input_output_aliases={i: 2 + i for i in range(nb)},
        compiler_params=pltpu.CompilerParams(has_side_effects=EFFECT),
    )(*[pltpu.with_memory_space_constraint(b, pltpu.HBM) for b in bufs], after)
    return (out[0], out[1]), list(out[2:2 + nb]), out[-1]


def _copy_wait(name, sems, bufs, plan, after):
    nb = len(bufs)

    def body(*refs):
        send_sems, recv_sems = refs[nb], refs[nb + 1]
        for k, (src, dst, dev) in enumerate(plan(*refs[:nb])):
            cp = _remote(src, dst, send_sems, recv_sems, k, dev)
            cp.wait_send()
            cp.wait_recv()

    out = pl.pallas_call(
        body, name=name, out_shape=tuple(pltpu.HBM(b.shape, b.dtype) for b in bufs),
        in_specs=[HBM_SPEC] * nb + [SEM_SPEC, SEM_SPEC, pl.BlockSpec(memory_space=pl.ANY)],
        out_specs=tuple([HBM_SPEC] * nb), input_output_aliases={i: i for i in range(nb)},
        compiler_params=pltpu.CompilerParams(has_side_effects=EFFECT),
    )(*bufs, sems[0], sems[1], after)
    return list(out)


def _ag_plan_chips(shard_ref, out_ref):
    x, y, c, chips = _place()
    mine = out_ref.at[4 * x + 2 * y + c]
    return [(shard_ref, mine, (x, y, 1 - c))] + [(shard_ref, mine, (*chip, c)) for chip in chips]


def _ag_plan_pass(out_ref):
    x, y, c, chips = _place()
    slots = [out_ref.at[4 * chip[0] + 2 * chip[1] + c] for chip in chips]
    return [(s, s, (x, y, 1 - c)) for s in slots]


def _rs_plan_pair(g_ref, land_ref):
    x, y, c, _ = _place()
    return [(g_ref.at[1 - c], land_ref, (x, y, 1 - c))]


def _rs_plan_chips(p_ref, land_ref):
    x, y, c, chips = _place()
    return [(p_ref.at[2 * chip[0] + chip[1]], land_ref.at[2 * x + y], (*chip, c)) for chip in chips]


class _Gather:
    def __init__(self, shard, me, tag, after):
        self.tag = tag
        out = lax.dynamic_update_slice(lax.empty((N_DEV,) + shard.shape, shard.dtype), shard[None],
                                       (me,) + (0,) * shard.ndim)
        self.sems, (self.shard, self.out), self.token = _copy_start(
            "ag_start_" + tag, [shard, out], _ag_plan_chips, 4, after)

    def arrived_from_chips(self, after):
        _, out = _copy_wait("ag_wait_" + self.tag, self.sems, [self.shard, self.out], _ag_plan_chips, after)
        self.sems, (self.out,), _ = _copy_start("ag_pass_" + self.tag, [out], _ag_plan_pass, 3, after)

    def passed_on(self, after):
        return _copy_wait("ag_pass_wait_" + self.tag, self.sems, [self.out], _ag_plan_pass, after)[0]


class _ReduceScatter:
    def __init__(self, g8, tag):
        self.tag = tag
        land = lax.empty(g8.shape[1:], g8.dtype)
        self.sems, self.bufs, self.token = _copy_start(
            "rs_pair_start_" + tag, [g8, land], _rs_plan_pair, 1, jnp.zeros((1,), F32))

    def pair_done(self, core, chip, after, start_after=None):
        g8, land = _copy_wait("rs_pair_wait_" + self.tag, self.sems, self.bufs, _rs_plan_pair, after)
        p4 = _pair_sum(g8, land, core, "rs_pair_sum_" + self.tag)
        own = lax.dynamic_slice_in_dim(p4, chip, 1, axis=0)
        land2 = lax.dynamic_update_slice(lax.empty(p4.shape, p4.dtype), own, (chip, 0, 0))
        self.sems, self.bufs, self.token = _copy_start(
            "rs_chips_start_" + self.tag, [p4, land2], _rs_plan_chips, 3,
            jnp.zeros((1,), F32) if start_after is None else start_after)

    def sums(self, after):
        return _copy_wait("rs_chips_wait_" + self.tag, self.sems, self.bufs, _rs_plan_chips, after)[1]


BIG = ("w_in", "w_o", "w_ffn_in", "w_ffn_out")
ORDER = ("w_ada", "b_ada", "w_in", "rel_bias", "attn_norm_g", "lb_logits", "gnorm_g", "w_o", "ln1_g", "ln1_b",
         "w_ffn_in", "w_ffn_out", "ln2_g", "ln2_b")


def kernel(x, c, w_ada, b_ada, w_in, rel_bias, attn_norm_g, lb_logits, gnorm_g, w_o, ln1_g, ln1_b, w_ffn_in, w_ffn_out, ln2_g, ln2_b, loss_target, m_w_ada, m_b_ada, m_w_in, m_rel_bias, m_attn_norm_g, m_lb_logits, m_gnorm_g, m_w_o, m_ln1_g, m_ln1_b, m_w_ffn_in, m_w_ffn_out, m_ln2_g, m_ln2_b, v_w_ada, v_b_ada, v_w_in, v_rel_bias, v_attn_norm_g, v_lb_logits, v_gnorm_g, v_w_o, v_ln1_g, v_ln1_b, v_w_ffn_in, v_w_ffn_out, v_ln2_g, v_ln2_b):
    W = dict(w_ada=w_ada, b_ada=b_ada, w_in=w_in, rel_bias=rel_bias, attn_norm_g=attn_norm_g, lb_logits=lb_logits,
             gnorm_g=gnorm_g, w_o=w_o, ln1_g=ln1_g, ln1_b=ln1_b, w_ffn_in=w_ffn_in, w_ffn_out=w_ffn_out,
             ln2_g=ln2_g, ln2_b=ln2_b)
    M = dict(w_ada=m_w_ada, b_ada=m_b_ada, w_in=m_w_in, rel_bias=m_rel_bias, attn_norm_g=m_attn_norm_g,
             lb_logits=m_lb_logits, gnorm_g=m_gnorm_g, w_o=m_w_o, ln1_g=m_ln1_g, ln1_b=m_ln1_b,
             w_ffn_in=m_w_ffn_in, w_ffn_out=m_w_ffn_out, ln2_g=m_ln2_g, ln2_b=m_ln2_b)
    V = dict(w_ada=v_w_ada, b_ada=v_b_ada, w_in=v_w_in, rel_bias=v_rel_bias, attn_norm_g=v_attn_norm_g,
             lb_logits=v_lb_logits, gnorm_g=v_gnorm_g, w_o=v_w_o, ln1_g=v_ln1_g, ln1_b=v_ln1_b,
             w_ffn_in=v_w_ffn_in, w_ffn_out=v_w_ffn_out, ln2_g=v_ln2_g, ln2_b=v_ln2_b)

    x2, tgt = x[0], loss_target[0]
    T, D = x2.shape
    AW, RW = attn_norm_g.shape[-1], lb_logits.shape[-1]
    MIX = AW + RW
    H, RH = AW // ATTN_HEAD_DIM, RW // LANE
    RB = rel_bias.shape[-1]
    max_rel = (RB - 1) // 2
    rbp = -(-RB // LANE) * LANE
    F = w_ffn_out.shape[1] * N_DEV
    half = N_DEV // 2
    xi, yi, ci = lax.axis_index("x"), lax.axis_index("y"), lax.axis_index("c")
    me = 4 * xi + 2 * yi + ci
    core = jnp.reshape(ci, (1,)).astype(jnp.int32)
    pad_rb = lambda a: jnp.pad(a[0], ((0, 0), (0, rbp - RB)))

    chip = 2 * xi + yi

    c_act, lbv, gv = _prep(c, lb_logits, pad_rb(rel_bias), max_rel)
    c_all = _all_gather(c_act, "ag_c").reshape(N_DEV, D)
    ns_ada = w_ada.shape[-1]
    mod_part = _mod_part(c_all, w_ada[0], lax.dynamic_slice_in_dim(b_ada, me * ns_ada, ns_ada, axis=1))
    mod_all = _all_gather(mod_part, "ag_mod")
    mod6 = lax.dynamic_index_in_dim(mod_all, me, axis=1, keepdims=False).reshape(6, D)

    ag_in = _Gather(w_in[0].astype(BF16), me, "w_in", mod_all)
    ag_o = _Gather(w_o[0].astype(BF16), me, "w_o", ag_in.token)
    ag_f1 = _Gather(w_ffn_in[0].astype(BF16), me, "w_ffn_in", ag_o.token)
    ag_f2 = _Gather(w_ffn_out[0].astype(BF16), me, "w_ffn_out", ag_f1.token)

    h1 = _ln_mod(x2, mod6 + ag_f2.token[0, 0])
    ag_in.arrived_from_chips(h1)
    wg_in = ag_in.passed_on(h1)
    proj = _mm_gathered(h1, wg_in, "in_proj")
    ag_o.arrived_from_chips(proj)
    mix_a = _attn_fwd(proj, gv, attn_norm_g, AW)
    wg_o = ag_o.passed_on(mix_a).reshape(MIX, D)
    mix_b, o_b, st_all = _hgrn_fwd(proj, lbv, gnorm_g, AW, RW)
    ag_f1.arrived_from_chips(mix_b)
    mixin = jnp.concatenate([mix_a, mix_b], axis=1)
    mix = _mm_nn(mixin, wg_o, "out_proj")
    x1, h2 = _mid_fwd(x2, mix, mod6, ln1_g, ln1_b)
    wg_f1 = ag_f1.passed_on(h2)
    gu, act = _mm_swiglu(h2, wg_f1)
    ag_f2.arrived_from_chips(act)
    wg_f2 = ag_f2.passed_on(act).reshape(F, D)
    ff = _mm_nn(act, wg_f2, "ffn_out")
    dff, dx1a, vec_a = _final(x1, ff, mod6, ln2_g, ln2_b, tgt)

    du = _mm_swiglu_bwd(dff, wg_f2, gu)
    rs_f2 = _ReduceScatter(_mm_tn_rows(dff, act, dff, F // N_DEV, "grad_w_ffn_out"), "w_ffn_out")
    tm = _tile(T, 512, 16)
    du_ij = lambda tm_, ns: pl.BlockSpec((None, tm_, ns), lambda i, j: (j // half, i, j % half))
    du_jm = lambda tm_, ns: pl.BlockSpec((None, tm_, ns), lambda j, m: (j // half, m, j % half))
    dh2 = _mm_gathered_nt(rs_f2.token, du, du_ij, wg_f1, T, tm, "ffn_in_bwd")
    rs_f2.pair_done(core, chip, dh2)
    tm_red = _tile(T, 1024, 16)
    gw_f1 = _mm_tn_gathered(rs_f2.token, h2, du, du_jm, wg_f1.shape[-1], tm_red, "grad_w_ffn_in")
    rs_f1 = _ReduceScatter(gw_f1.reshape(2, half, D, -1), "w_ffn_in")
    dmix, dxa, vec_b = _mid_bwd(x2, mix, x1, dx1a, dh2, mod6 + rs_f1.token[0, 0], ln1_g)
    dmixin = _mm_nt(dmix, wg_o, "out_proj_bwd")
    rs_f1.pair_done(core, chip, dmixin)
    rs_o = _ReduceScatter(_mm_tn_rows(rs_f1.token, mixin, dmix, MIX // N_DEV, "grad_w_o"), "w_o")
    dq, dk, dv, dgv, dga = _attn_bwd(proj, dmixin, gv + rs_o.token[0, 0], attn_norm_g, AW)
    rs_o.pair_done(core, chip, dq)
    dqb, dfl, dib, dgb, dlb, dgn = _hgrn_bwd(proj, dmixin, o_b, st_all, lbv + rs_o.token[0, 0], gnorm_g, AW, RW)
    dproj = jnp.concatenate([dq, dk, dv, dqb, dfl, dib, dgb], axis=1)
    p_ij = lambda tm_, ns: pl.BlockSpec((tm_, ns), lambda i, j: (i, j))
    p_jm = lambda tm_, ns: pl.BlockSpec((tm_, ns), lambda j, m: (m, j))
    gw_in = _mm_tn_gathered(rs_o.token, h1, dproj, p_jm, wg_in.shape[-1], tm_red, "grad_w_in")
    rs_in = _ReduceScatter(gw_in.reshape(2, half, D, -1), "w_in")
    dh1 = _mm_gathered_nt(rs_in.token, dproj, p_ij, wg_in, T, tm, "in_proj_bwd")
    grad_x, vec_c = _first_bwd(x2, dh1, dxa, mod6)

    dmod = jnp.concatenate([vec_c[1:2], vec_c[0:1], vec_b[4:5], vec_b[1:2], vec_b[0:1], vec_a[2:3]], axis=0)
    pieces = dict(b_ada=dmod, rel_bias=dgv, attn_norm_g=dga, lb_logits=dlb, gnorm_g=dgn, ln1_g=vec_b[2:3],
                  ln1_b=vec_b[3:4], ln2_g=vec_a[0:1], ln2_b=vec_a[1:2], loss=vec_a[3:4])
    widths = dict(b_ada=(1, 6 * D), rel_bias=(H, TAB), attn_norm_g=(1, AW), lb_logits=(1, RW), gnorm_g=(RH, LANE),
                  ln1_g=(1, D), ln1_b=(1, D), ln2_g=(1, D), ln2_b=(1, D), loss=(1, D))
    packed = jnp.concatenate([pieces[k].reshape(-1, LANE) for k in widths], axis=0)
    gathered = _all_gather(packed, "ag_small")
    rs_in.pair_done(core, chip, dh1, start_after=gathered)
    parts, r0 = {}, 0
    for k, (rows, width) in widths.items():
        nr = rows * width // LANE
        parts[k] = gathered[:, r0:r0 + nr, :].reshape(N_DEV, rows, width)
        r0 += nr
    prep_small = lambda d, k: pad_rb(d[k]) if k == "rel_bias" else d[k]
    small = _small_update(parts, parts["loss"], lbv, [prep_small(W, k) for k in SMALL],
                          [prep_small(M, k) for k in SMALL], [prep_small(V, k) for k in SMALL], max_rel)
    loss = small[0].reshape(())
    res = {}
    for idx, k in enumerate(SMALL):
        four = small[1 + 4 * idx:5 + 4 * idx]
        if k == "rel_bias":
            four = [a[:, :RB][None] for a in four]
        res[k] = list(four)

    dmod_s = lax.dynamic_slice_in_dim(parts["b_ada"].reshape(N_DEV, 6 * D), me * ns_ada, ns_ada, axis=1)
    dmod_s = dmod_s + rs_in.token[0, 0]
    res["w_ada"] = [a[None] for a in _adam_ada(c_all, dmod_s, w_ada[0], m_w_ada[0], v_w_ada[0])]
    after = res["w_ada"][0]
    for k, rs in (("w_ffn_out", rs_f2), ("w_ffn_in", rs_f1), ("w_o", rs_o), ("w_in", rs_in)):
        four = _adam_shard(rs.sums(after), W[k][0], M[k][0], V[k][0], "adam_" + k)
        res[k] = [a[None] for a in four]
        after = four[0]

    out = [loss, grad_x[None]]
    for field in range(4):
        out += [res[k][field] for k in ORDER]
    return tuple(out)
```

```python
import functools

import jax
import jax.numpy as jnp
from jax import lax
from jax.experimental import pallas as pl
from jax.experimental.pallas import tpu as pltpu

F32 = jnp.float32
BF16 = jnp.bfloat16
MESH = pl.DeviceIdType.MESH
HIGHEST = lax.Precision.HIGHEST

N_DEV = 8
CHUNK = 64
N_PAST = 8
QBLK = 4 * CHUNK
KPAD = N_PAST * CHUNK
WIN = KPAD + QBLK
TAB = 1024
ATTN_HEAD_DIM = 64
ATTN_HEADS_PER_STEP = 4
REC_HEAD_DIM = 128
SUB = 16
ROWS = 8
LANE = 128
EPS = 1e-5
ALPHA = 2.0 ** 0.25
ADAM_LR, ADAM_B1, ADAM_B2, ADAM_EPS, ADAM_WD, ADAM_STEP = 0.001, 0.9, 0.999, 1e-08, 0.01, 10
NEG = -1e30
VMEM_LIMIT = 56 * 1024 * 1024


def _sds(shape, dtype):
    return jax.ShapeDtypeStruct(tuple(shape), dtype)


def _tile(n, pref, mult):
    best = None
    for t in range(mult, min(n, pref) + 1, mult):
        if n % t == 0:
            best = t
    return n if best is None else best


def _params(sem=None, big=False):
    kw = {}
    if sem is not None:
        kw["dimension_semantics"] = sem
    if big:
        kw["vmem_limit_bytes"] = VMEM_LIMIT
    return pltpu.CompilerParams(**kw)


def _sigmoid(v):
    return 1.0 / (1.0 + jnp.exp(-v))


def _dot(a, b, dims, precision=None):
    return lax.dot_general(a, b, (dims, ((), ())), preferred_element_type=F32, precision=precision)


NN = ((1,), (0,))
NT = ((1,), (1,))
TN = ((0,), (0,))


def _ln(v):
    mu = jnp.mean(v, axis=-1, keepdims=True)
    d = v - mu
    rstd = lax.rsqrt(jnp.mean(d * d, axis=-1, keepdims=True) + EPS)
    return d * rstd, rstd


def _ln_bwd(dxh, xh, rstd):
    return rstd * (dxh - jnp.mean(dxh, axis=-1, keepdims=True) - xh * jnp.mean(dxh * xh, axis=-1, keepdims=True))


def _colsum(v):
    return jnp.sum(v, axis=0, keepdims=True)


def _ln_mod(x2, mod6):
    T, D = x2.shape
    tm = _tile(T, 256, 8)

    def body(x_ref, mod_ref, o_ref):
        xh, _ = _ln(x_ref[...])
        o_ref[...] = (xh * (1.0 + mod_ref[1:2, :]) + mod_ref[0:1, :]).astype(BF16)

    return pl.pallas_call(
        body, grid=(T // tm,), name="ln_mod",
        in_specs=[pl.BlockSpec((tm, D), lambda i: (i, 0)), pl.BlockSpec((6, D), lambda i: (0, 0))],
        out_specs=pl.BlockSpec((tm, D), lambda i: (i, 0)),
        out_shape=_sds((T, D), BF16), compiler_params=_params(("parallel",)),
    )(x2, mod6)


def _mid_fwd(x2, mix, mod6, ln1_g, ln1_b):
    T, D = x2.shape
    tm = _tile(T, 256, 8)

    def body(x_ref, mix_ref, mod_ref, g_ref, b_ref, x1_ref, h2_ref):
        zh, _ = _ln(ALPHA * x_ref[...] + mod_ref[2:3, :] * mix_ref[...])
        x1 = zh * g_ref[...] + b_ref[...]
        x1_ref[...] = x1
        xh, _ = _ln(x1)
        h2_ref[...] = (xh * (1.0 + mod_ref[4:5, :]) + mod_ref[3:4, :]).astype(BF16)

    row = pl.BlockSpec((tm, D), lambda i: (i, 0))
    vec = pl.BlockSpec((1, D), lambda i: (0, 0))
    return pl.pallas_call(
        body, grid=(T // tm,), name="mid_fwd",
        in_specs=[row, row, pl.BlockSpec((6, D), lambda i: (0, 0)), vec, vec],
        out_specs=[row, row],
        out_shape=[_sds((T, D), F32), _sds((T, D), BF16)], compiler_params=_params(("parallel",)),
    )(x2, mix, mod6, ln1_g, ln1_b)


def _final(x1, ff, mod6, ln2_g, ln2_b, tgt):
    T, D = x1.shape
    tm = _tile(T, 256, 8)

    def body(x1_ref, ff_ref, mod_ref, g_ref, b_ref, t_ref, dff_ref, dx1_ref, vec_ref):
        @pl.when(pl.program_id(0) == 0)
        def _():
            vec_ref[...] = jnp.zeros_like(vec_ref)

        ff_v = ff_ref[...]
        gate2 = mod_ref[5:6, :]
        zh, rstd = _ln(ALPHA * x1_ref[...] + gate2 * ff_v)
        err = zh * g_ref[...] + b_ref[...] - t_ref[...]
        dy = err * (1.0 / D)
        dz = _ln_bwd(dy * g_ref[...], zh, rstd)
        dff_ref[...] = (gate2 * dz).astype(BF16)
        dx1_ref[...] = ALPHA * dz
        vec_ref[0:1, :] += _colsum(dy * zh)
        vec_ref[1:2, :] += _colsum(dy)
        vec_ref[2:3, :] += _colsum(dz * ff_v)
        vec_ref[3:4, :] += _colsum(err * err) * (0.5 / D)

    row = pl.BlockSpec((tm, D), lambda i: (i, 0))
    vec = pl.BlockSpec((1, D), lambda i: (0, 0))
    return pl.pallas_call(
        body, grid=(T // tm,), name="final_fwd_bwd",
        in_specs=[row, row, pl.BlockSpec((6, D), lambda i: (0, 0)), vec, vec, row],
        out_specs=[row, row, pl.BlockSpec((8, D), lambda i: (0, 0))],
        out_shape=[_sds((T, D), BF16), _sds((T, D), F32), _sds((8, D), F32)],
        compiler_params=_params(("arbitrary",)),
    )(x1, ff, mod6, ln2_g, ln2_b, tgt)


def _mid_bwd(x2, mix, x1, dx1a, dh2, mod6, ln1_g):
    T, D = x2.shape
    tm = _tile(T, 256, 8)

    def body(x_ref, mix_ref, x1_ref, dx1a_ref, dh2_ref, mod_ref, g_ref, dmix_ref, dxa_ref, vec_ref):
        @pl.when(pl.program_id(0) == 0)
        def _():
            vec_ref[...] = jnp.zeros_like(vec_ref)

        dh2 = dh2_ref[...]
        xh, rstd = _ln(x1_ref[...])
        dx1 = dx1a_ref[...] + _ln_bwd(dh2 * (1.0 + mod_ref[4:5, :]), xh, rstd)
        mix_v = mix_ref[...]
        gate1 = mod_ref[2:3, :]
        zh, rstdz = _ln(ALPHA * x_ref[...] + gate1 * mix_v)
        dz = _ln_bwd(dx1 * g_ref[...], zh, rstdz)
        dmix_ref[...] = (gate1 * dz).astype(BF16)
        dxa_ref[...] = ALPHA * dz
        vec_ref[0:1, :] += _colsum(dh2 * xh)
        vec_ref[1:2, :] += _colsum(dh2)
        vec_ref[2:3, :] += _colsum(dx1 * zh)
        vec_ref[3:4, :] += _colsum(dx1)
        vec_ref[4:5, :] += _colsum(dz * mix_v)

    row = pl.BlockSpec((tm, D), lambda i: (i, 0))
    vec = pl.BlockSpec((1, D), lambda i: (0, 0))
    return pl.pallas_call(
        body, grid=(T // tm,), name="mid_bwd",
        in_specs=[row, row, row, row, row, pl.BlockSpec((6, D), lambda i: (0, 0)), vec],
        out_specs=[row, row, pl.BlockSpec((8, D), lambda i: (0, 0))],
        out_shape=[_sds((T, D), BF16), _sds((T, D), F32), _sds((8, D), F32)],
        compiler_params=_params(("arbitrary",)),
    )(x2, mix, x1, dx1a, dh2, mod6, ln1_g)


def _first_bwd(x2, dh1, dxa, mod6):
    T, D = x2.shape
    tm = _tile(T, 256, 8)

    def body(x_ref, dh1_ref, dxa_ref, mod_ref, gx_ref, vec_ref):
        @pl.when(pl.program_id(0) == 0)
        def _():
            vec_ref[...] = jnp.zeros_like(vec_ref)

        dh1 = dh1_ref[...]
        xh, rstd = _ln(x_ref[...])
        gx_ref[...] = dxa_ref[...] + _ln_bwd(dh1 * (1.0 + mod_ref[1:2, :]), xh, rstd)
        vec_ref[0:1, :] += _colsum(dh1 * xh)
        vec_ref[1:2, :] += _colsum(dh1)

    row = pl.BlockSpec((tm, D), lambda i: (i, 0))
    return pl.pallas_call(
        body, grid=(T // tm,), name="first_bwd",
        in_specs=[row, row, row, pl.BlockSpec((6, D), lambda i: (0, 0))],
        out_specs=[row, pl.BlockSpec((8, D), lambda i: (0, 0))],
        out_shape=[_sds((T, D), F32), _sds((8, D), F32)],
        compiler_params=_params(("arbitrary",)),
    )(x2, dh1, dxa, mod6)


def _slot(j):
    return (j % 2) * 4 + j // 2


def _mm_gathered(a, wg, shards, out, name):
    M, K = a.shape
    _, _, ns = wg.shape
    tm = _tile(M, 512, 16)

    def body(shards_ref, a_ref, w_ref, prev_ref, o_ref):
        o_ref[...] = _dot(a_ref[...], w_ref[...], NN)

    return pl.pallas_call(
        body, name=name,
        grid_spec=pltpu.PrefetchScalarGridSpec(
            num_scalar_prefetch=1, grid=(shards.shape[0], M // tm),
            in_specs=[pl.BlockSpec((tm, K), lambda j, i, s: (i, 0)),
                      pl.BlockSpec((None, K, ns), lambda j, i, s: (s[j], 0, 0)), ORDER_ONLY],
            out_specs=pl.BlockSpec((tm, ns), lambda j, i, s: (i, s[j]))),
        out_shape=_sds((M, N_DEV * ns), F32), input_output_aliases={3: 0},
        compiler_params=_params(("parallel", "parallel"), big=True),
    )(shards, a, wg, out)


def _mm_nn(a, b, name):
    M, K = a.shape
    _, N = b.shape
    tm, tn, tk = _tile(M, 512, 16), _tile(N, 1024, LANE), _tile(K, 2048, LANE)

    def body(a_ref, b_ref, o_ref):
        @pl.when(pl.program_id(2) == 0)
        def _():
            o_ref[...] = jnp.zeros_like(o_ref)

        o_ref[...] += _dot(a_ref[...], b_ref[...], NN)

    return pl.pallas_call(
        body, grid=(M // tm, N // tn, K // tk), name=name,
        in_specs=[pl.BlockSpec((tm, tk), lambda i, j, k: (i, k)), pl.BlockSpec((tk, tn), lambda i, j, k: (k, j))],
        out_specs=pl.BlockSpec((tm, tn), lambda i, j, k: (i, j)),
        out_shape=_sds((M, N), F32), compiler_params=_params(("parallel", "parallel", "arbitrary"), big=True),
    )(a, b)


def _mm_nt(a, b, name):
    M, K = a.shape
    N, _ = b.shape
    tm, tn = _tile(M, 512, 16), _tile(N, 1024, LANE)

    def body(a_ref, b_ref, o_ref):
        o_ref[...] = _dot(a_ref[...], b_ref[...], NT)

    return pl.pallas_call(
        body, grid=(M // tm, N // tn), name=name,
        in_specs=[pl.BlockSpec((tm, K), lambda i, j: (i, 0)), pl.BlockSpec((tn, K), lambda i, j: (j, 0))],
        out_specs=pl.BlockSpec((tm, tn), lambda i, j: (i, j)),
        out_shape=_sds((M, N), F32), compiler_params=_params(("parallel", "parallel"), big=True),
    )(a, b)


def _mm_swiglu(h2, wg):
    M, K = h2.shape
    _, _, ns = wg.shape
    half = N_DEV // 2
    tm = _tile(M, 256, 16)

    def body(a_ref, wgate_ref, wup_ref, gu_ref, act_ref):
        a = a_ref[...]
        g = _dot(a, wgate_ref[...], NN)
        u = _dot(a, wup_ref[...], NN)
        gu_ref[0] = g
        gu_ref[1] = u
        act_ref[...] = (g * _sigmoid(g) * u).astype(BF16)

    return pl.pallas_call(
        body, grid=(half, M // tm), name="ffn_in_swiglu",
        in_specs=[pl.BlockSpec((tm, K), lambda j, i: (i, 0)),
                  pl.BlockSpec((None, K, ns), lambda j, i: (j, 0, 0)),
                  pl.BlockSpec((None, K, ns), lambda j, i: (j + half, 0, 0))],
        out_specs=[pl.BlockSpec((2, tm, ns), lambda j, i: (0, i, j)), pl.BlockSpec((tm, ns), lambda j, i: (i, j))],
        out_shape=[_sds((2, M, half * ns), F32), _sds((M, half * ns), BF16)],
        compiler_params=_params(("parallel", "parallel"), big=True),
    )(h2, wg, wg)


def _mm_swiglu_bwd(dff, w2, gu):
    M, K = dff.shape
    F = w2.shape[0]
    tm, tn = _tile(M, 512, 16), _tile(F, 1408, LANE)

    def body(a_ref, b_ref, gu_ref, du_ref):
        da = _dot(a_ref[...], b_ref[...], NT)
        g = gu_ref[0]
        u = gu_ref[1]
        sg = _sigmoid(g)
        du_ref[0] = (da * u * (sg * (1.0 + g * (1.0 - sg)))).astype(BF16)
        du_ref[1] = (da * (g * sg)).astype(BF16)

    return pl.pallas_call(
        body, grid=(F // tn, M // tm), name="ffn_out_bwd_swiglu",
        in_specs=[pl.BlockSpec((tm, K), lambda j, i: (i, 0)), pl.BlockSpec((tn, K), lambda j, i: (j, 0)),
                  pl.BlockSpec((2, tm, tn), lambda j, i: (0, i, j))],
        out_specs=pl.BlockSpec((2, tm, tn), lambda j, i: (0, i, j)),
        out_shape=_sds((2, M, F), BF16), compiler_params=_params(("parallel", "parallel"), big=True),
    )(dff, w2, gu)


ORDER_ONLY = pl.BlockSpec(memory_space=pl.ANY)


def _mm_tn_rows(dep, a, b, rs, name):
    M, Ka = a.shape
    _, N = b.shape
    tm = _tile(M, 1024, 16)

    def body(_, a_ref, b_ref, o_ref, acc_ref):
        m = pl.program_id(1)

        @pl.when(m == 0)
        def _():
            acc_ref[...] = jnp.zeros_like(acc_ref)

        acc_ref[...] += _dot(a_ref[...], b_ref[...], TN)

        @pl.when(m == pl.num_programs(1) - 1)
        def _():
            o_ref[0, 0] = acc_ref[0:rs, :].astype(BF16)
            o_ref[1, 0] = acc_ref[rs:2 * rs, :].astype(BF16)

    return pl.pallas_call(
        body, grid=(N_DEV // 2, M // tm), name=name,
        in_specs=[ORDER_ONLY, pl.BlockSpec((tm, 2 * rs), lambda ch, m: (m, ch)),
                  pl.BlockSpec((tm, N), lambda ch, m: (m, 0))],
        out_specs=pl.BlockSpec((2, 1, rs, N), lambda ch, m: (0, ch, 0, 0)),
        out_shape=_sds((2, N_DEV // 2, rs, N), BF16),
        scratch_shapes=[pltpu.VMEM((2 * rs, N), F32)],
        compiler_params=_params(("parallel", "arbitrary"), big=True),
    )(dep, a, b)


def _mm_gathered_nt(dep, a, a_spec, wg, M, tm, name):
    _, K, ns = wg.shape

    def body(_, a_ref, w_ref, o_ref):
        @pl.when(pl.program_id(1) == 0)
        def _():
            o_ref[...] = jnp.zeros_like(o_ref)

        o_ref[...] += _dot(a_ref[...], w_ref[...], NT)

    return pl.pallas_call(
        body, grid=(M // tm, N_DEV), name=name,
        in_specs=[ORDER_ONLY, a_spec(tm, ns), pl.BlockSpec((None, K, ns), lambda i, j: (j, 0, 0))],
        out_specs=pl.BlockSpec((tm, K), lambda i, j: (i, 0)),
        out_shape=_sds((M, K), F32), compiler_params=_params(("parallel", "arbitrary"), big=True),
    )(dep, a, wg)


def _mm_tn_gathered(dep, h, a, a_spec, ns, tm, name):
    M, K = h.shape

    def body(_, h_ref, a_ref, o_ref, acc_ref):
        m = pl.program_id(1)

        @pl.when(m == 0)
        def _():
            acc_ref[...] = jnp.zeros_like(acc_ref)

        acc_ref[...] += _dot(h_ref[...], a_ref[...], TN)

        @pl.when(m == pl.num_programs(1) - 1)
        def _():
            o_ref[...] = acc_ref[...].astype(BF16)

    return pl.pallas_call(
        body, grid=(N_DEV, M // tm), name=name,
        in_specs=[ORDER_ONLY, pl.BlockSpec((tm, K), lambda j, m: (m, 0)), a_spec(tm, ns)],
        out_specs=pl.BlockSpec((None, K, ns), lambda j, m: (_slot(j), 0, 0)),
        out_shape=_sds((N_DEV, K, ns), BF16),
        scratch_shapes=[pltpu.VMEM((K, ns), F32)],
        compiler_params=_params(("parallel", "arbitrary"), big=True),
    )(dep, h, a)


def _bias_onehot(rbp, max_rel):
    r = lax.broadcasted_iota(jnp.int32, (rbp, TAB), 0)
    m = lax.broadcasted_iota(jnp.int32, (rbp, TAB), 1)
    dist = KPAD - jnp.where(m < WIN, m, m - TAB)
    return (r == jnp.clip(dist, -max_rel, max_rel) + max_rel).astype(F32)


def _attn_setup(i, hp, k_ref, v_ref, gv_ref, kpad, vpad, bias):
    ls = slice(i * ATTN_HEAD_DIM, (i + 1) * ATTN_HEAD_DIM)
    kpad[i][0:KPAD, :] = jnp.zeros((KPAD, ATTN_HEAD_DIM), BF16)
    vpad[i][0:KPAD, :] = jnp.zeros((KPAD, ATTN_HEAD_DIM), BF16)
    kpad[i][KPAD:, :] = k_ref[:, ls].astype(BF16)
    vpad[i][KPAD:, :] = v_ref[:, ls].astype(BF16)
    gvrow = gv_ref[pl.ds(hp * ATTN_HEADS_PER_STEP + i, 1), :]
    tab = pltpu.roll(jnp.broadcast_to(gvrow, (QBLK, TAB)), 0, 1, stride=1, stride_axis=0)
    row = lax.broadcasted_iota(jnp.int32, (QBLK, WIN), 0)
    col = lax.broadcasted_iota(jnp.int32, (QBLK, WIN), 1)
    first = jnp.bitwise_and(row, -CHUNK)
    seen = jnp.logical_and(col >= first, col < first + (N_PAST + 1) * CHUNK)
    bias[i][...] = jnp.where(seen, tab[:, 0:WIN], NEG)


def _attn_probs(b, q_ref, kpad, vpad, bias, col):
    pair = range(ATTN_HEADS_PER_STEP)
    ls = [slice(i * ATTN_HEAD_DIM, (i + 1) * ATTN_HEAD_DIM) for i in pair]
    r0 = pl.multiple_of(b * QBLK, QBLK)
    q = [q_ref[pl.ds(r0, QBLK), ls[i]].astype(BF16) for i in pair]
    kw = [kpad[i][pl.ds(r0, WIN), :] for i in pair]
    vw = [vpad[i][pl.ds(r0, WIN), :] for i in pair]
    s = [_dot(q[i], kw[i], NT) * (ATTN_HEAD_DIM ** -0.5) + bias[i][...] for i in pair]
    s = [jnp.where(col >= KPAD - r0, s[i], NEG) for i in pair]
    p = [jnp.exp(s[i] - jnp.max(s[i], axis=-1, keepdims=True)) for i in pair]
    pn = [p[i] / jnp.sum(p[i], axis=-1, keepdims=True) for i in pair]
    return r0, ls, q, kw, vw, pn


def _attn_fwd(proj, gv, ga, AW):
    T = proj.shape[0]
    AH = ATTN_HEADS_PER_STEP
    W = AH * ATTN_HEAD_DIM
    HP = AW // W

    def body(q_ref, k_ref, v_ref, gv_ref, ga_ref, o_ref, *scratch):
        kpad, vpad, bias = (scratch[k * AH:(k + 1) * AH] for k in range(3))
        hp = pl.program_id(0)
        for i in range(AH):
            _attn_setup(i, hp, k_ref, v_ref, gv_ref, kpad, vpad, bias)
        col = lax.broadcasted_iota(jnp.int32, (QBLK, WIN), 1)

        def block(b, carry):
            pair = range(AH)
            r0, ls, _, _, vw, pn = _attn_probs(b, q_ref, kpad, vpad, bias, col)
            o = [_dot(pn[i].astype(BF16), vw[i], NN) for i in pair]
            r = [lax.rsqrt(jnp.mean(o[i] * o[i], axis=-1, keepdims=True) + EPS) for i in pair]
            outs = [o[i] * r[i] * ga_ref[0:1, ls[i]] for i in pair]
            o_ref[pl.ds(r0, QBLK), :] = jnp.concatenate(outs, axis=1).astype(BF16)
            return carry

        lax.fori_loop(0, T // QBLK, block, 0)

    blk = lambda off: pl.BlockSpec((T, W), lambda hp: (0, off + hp))
    return pl.pallas_call(
        body, grid=(HP,), name="attn_fwd",
        in_specs=[blk(0), blk(HP), blk(2 * HP), pl.BlockSpec(gv.shape, lambda hp: (0, 0)),
                  pl.BlockSpec((1, W), lambda hp: (0, hp))],
        out_specs=pl.BlockSpec((T, W), lambda hp: (0, hp)),
        out_shape=_sds((T, AW), BF16),
        scratch_shapes=[pltpu.VMEM((T + KPAD, ATTN_HEAD_DIM), BF16)] * (2 * AH) + [pltpu.VMEM((QBLK, WIN), F32)] * AH,
        compiler_params=_params(("parallel",), big=True),
    )(proj, proj, proj, gv, ga)


def _attn_bwd(proj, dmixin, gv, ga, AW):
    T = proj.shape[0]
    AH = ATTN_HEADS_PER_STEP
    W = AH * ATTN_HEAD_DIM
    HP = AW // W
    scale = ATTN_HEAD_DIM ** -0.5

    def body(q_ref, k_ref, v_ref, dn_ref, gv_ref, ga_ref, dq_ref, dk_ref, dv_ref, dgv_ref, dga_ref, *scratch):
        kpad, vpad, dkacc, dvacc, bias, dbias = (scratch[k * AH:(k + 1) * AH] for k in range(6))
        hp = pl.program_id(0)
        for i in range(AH):
            _attn_setup(i, hp, k_ref, v_ref, gv_ref, kpad, vpad, bias)
            dkacc[i][...] = jnp.zeros_like(dkacc[i])
            dvacc[i][...] = jnp.zeros_like(dvacc[i])
            dbias[i][...] = jnp.zeros_like(dbias[i])
        dga_ref[...] = jnp.zeros_like(dga_ref)
        col = lax.broadcasted_iota(jnp.int32, (QBLK, WIN), 1)

        def block(b, carry):
            pair = range(AH)
            r0, lss, qs, kws, vws, pns = _attn_probs(b, q_ref, kpad, vpad, bias, col)
            pn_b = [pns[i].astype(BF16) for i in pair]
            o = [_dot(pn_b[i], vws[i], NN) for i in pair]
            r = [lax.rsqrt(jnp.mean(o[i] * o[i], axis=-1, keepdims=True) + EPS) for i in pair]
            dn = [dn_ref[pl.ds(r0, QBLK), lss[i]] for i in pair]
            for i in pair:
                dga_ref[i:i + 1, :] += _colsum(dn[i] * o[i] * r[i])
            a = [dn[i] * ga_ref[0:1, lss[i]] for i in pair]
            do_b = [(r[i] * (a[i] - o[i] * (r[i] * r[i]) * jnp.mean(a[i] * o[i], axis=-1, keepdims=True))).astype(BF16)
                    for i in pair]
            dp = [_dot(do_b[i], vws[i], NT) for i in pair]
            for i in pair:
                dvacc[i][pl.ds(r0, WIN), :] += _dot(pn_b[i], do_b[i], TN)
            ds = [pns[i] * (dp[i] - jnp.sum(pns[i] * dp[i], axis=-1, keepdims=True)) for i in pair]
            for i in pair:
                dbias[i][...] += ds[i]
            ds_b = [ds[i].astype(BF16) for i in pair]
            dq = [_dot(ds_b[i], kws[i], NN) * scale for i in pair]
            dq_ref[pl.ds(r0, QBLK), :] = jnp.concatenate(dq, axis=1).astype(BF16)
            for i in pair:
                dkacc[i][pl.ds(r0, WIN), :] += _dot(ds_b[i], qs[i], TN) * scale
            return carry

        lax.fori_loop(0, T // QBLK, block, 0)

        rr = lax.broadcasted_iota(jnp.int32, (QBLK, QBLK), 0)
        cc = lax.broadcasted_iota(jnp.int32, (QBLK, QBLK), 1)
        flip = (rr + cc == QBLK - 1).astype(BF16)
        for i in range(AH):
            ls = slice(i * ATTN_HEAD_DIM, (i + 1) * ATTN_HEAD_DIM)
            dk_ref[:, ls] = dkacc[i][KPAD:, :].astype(BF16)
            dv_ref[:, ls] = dvacc[i][KPAD:, :].astype(BF16)
            full = jnp.concatenate([dbias[i][...], jnp.zeros((QBLK, TAB - WIN), F32)], axis=1)
            hi = full.astype(BF16)
            lo = (full - hi.astype(F32)).astype(BF16)
            rev = _dot(flip, hi, NN) + _dot(flip, lo, NN)
            dgv_ref[i:i + 1, :] = _colsum(pltpu.roll(rev, TAB - (QBLK - 1), 1, stride=1, stride_axis=0))

    blk = lambda off: pl.BlockSpec((T, W), lambda hp: (0, off + hp))
    accs = lambda dt: [pltpu.VMEM((T + KPAD, ATTN_HEAD_DIM), dt)] * AH
    return pl.pallas_call(
        body, grid=(HP,), name="attn_bwd",
        in_specs=[blk(0), blk(HP), blk(2 * HP), blk(0), pl.BlockSpec(gv.shape, lambda hp: (0, 0)),
                  pl.BlockSpec((1, W), lambda hp: (0, hp))],
        out_specs=[blk(0), blk(0), blk(0), pl.BlockSpec((None, AH, TAB), lambda hp: (hp, 0, 0)),
                   pl.BlockSpec((None, AH, ATTN_HEAD_DIM), lambda hp: (hp, 0, 0))],
        out_shape=[_sds((T, AW), BF16), _sds((T, AW), BF16), _sds((T, AW), BF16),
                   _sds((HP, AH, TAB), F32), _sds((HP, AH, ATTN_HEAD_DIM), F32)],
        scratch_shapes=accs(BF16) + accs(BF16) + accs(F32) + accs(F32) + [pltpu.VMEM((QBLK, WIN), F32)] * (2 * AH),
        compiler_params=_params(("parallel",), big=True),
    )(proj, proj, proj, dmixin, gv, ga)


def _ltri():
    r = lax.broadcasted_iota(jnp.int32, (CHUNK, CHUNK), 0)
    c = lax.broadcasted_iota(jnp.int32, (CHUNK, CHUNK), 1)
    return (c <= r).astype(BF16)


def _tri_dot(tri, v, dims):
    hi = v.astype(BF16)
    lo = (v - hi.astype(F32)).astype(BF16)
    return _dot(tri, hi, dims) + _dot(tri, lo, dims)


HEADS_PER_STEP = 2


def _alternate(stages):
    live = list(stages)
    while live:
        for g in list(live):
            if next(g, StopIteration) is StopIteration:
                live.remove(g)


def _hgrn_gates(n, ls, q_ref, f_ref, lb_ref, ltri):
    r0 = pl.multiple_of(n * CHUNK, CHUNK)
    rows = pl.ds(r0, CHUNK)
    lb = lb_ref[:, ls]
    qb = q_ref[rows, ls]
    sg = _sigmoid(f_ref[rows, ls])
    f = lb + (1.0 - lb) * sg
    sq = _sigmoid(qb)
    b = _tri_dot(ltri, jnp.log(f), NN)
    return rows, lb, qb, sg, f, 1.0 - f, sq, qb * sq, b


def _hgrn_specs(T, RW, AW):
    HG = HEADS_PER_STEP
    W = HG * LANE
    base = 3 * AW // W
    blk_in = lambda off: pl.BlockSpec((T, W), lambda g: (0, base + off + g))
    col = pl.BlockSpec((T, W), lambda g: (0, g))
    return HG, W, RW // W, blk_in, col


def _hgrn_fwd(proj, lb, gn, AW, RW):
    T = proj.shape[0]
    RH, NC, NSUB = RW // LANE, T // CHUNK, CHUNK // SUB
    HG, W, NG, blk_in, col = _hgrn_specs(T, RW, AW)

    def body(q_ref, f_ref, i_ref, g_ref, lb_ref, gn_ref, mix_ref, o_ref, stall_ref, st_all, bs_all, kks_all, ics_all):
        st_all[...] = jnp.zeros_like(st_all)
        ltri = _ltri()
        rowi = lax.broadcasted_iota(jnp.int32, (SUB, 1), 0)

        def one_head(h, n):
            ls = slice(h * LANE, (h + 1) * LANE)
            st, bs, kks, ics = st_all.at[h], bs_all.at[h], kks_all.at[h], ics_all.at[h]
            rows, _, _, _, _, kk, _, qs, b = _hgrn_gates(n, ls, q_ref, f_ref, lb_ref, ltri)
            ic = i_ref[rows, ls]
            stv = st[...]
            stall_ref[h, n] = stv
            bs[...] = b
            kks[...] = kk
            ics[...] = ic
            yield
            o = _dot((qs * jnp.exp(b)).astype(BF16), stv.astype(BF16), NT)
            yield
            ic_b = ic.astype(BF16)
            pieces = []
            for blk in range(NSUB):
                s0 = blk * SUB
                bI, qI = b[s0:s0 + SUB], qs[s0:s0 + SUB]
                if blk == 0:
                    oI = jnp.zeros((SUB, LANE), F32)
                else:
                    ref = bs[s0 - 1:s0, :]
                    qt = (qI * jnp.exp(bI - ref)).astype(BF16)
                    kt = (kk[0:s0] * jnp.exp(ref - b[0:s0])).astype(BF16)
                    oI = _dot(_dot(qt, kt, NT).astype(BF16), ic_b[0:s0], NN)
                    yield
                acc = [oI[g * ROWS:(g + 1) * ROWS] for g in range(SUB // ROWS)]
                for s in range(SUB):
                    sr = s0 + s
                    g0 = s // ROWS
                    lo = g0 * ROWS
                    e = jnp.exp(jnp.minimum(bI[lo:] - bs[sr:sr + 1, :], 0.0))
                    a = jnp.sum(qI[lo:] * kks[sr:sr + 1, :] * e, axis=-1, keepdims=True)
                    add = jnp.where(rowi[lo:] >= s, a, 0.0) * ics[sr:sr + 1, :]
                    for g in range(g0, SUB // ROWS):
                        acc[g] = acc[g] + add[(g - g0) * ROWS:(g - g0 + 1) * ROWS]
                    yield
                pieces.extend(acc)
            o = o + jnp.concatenate(pieces, axis=0)
            bl = bs[CHUNK - 1:CHUNK, :]
            kd = (kk * jnp.exp(bl - b)).astype(BF16)
            st[...] = stv * jnp.exp(bl) + _dot(ic_b, kd, TN)
            yield
            o_ref[rows, ls] = o
            r = lax.rsqrt(jnp.mean(o * o, axis=-1, keepdims=True) + EPS)
            gb = g_ref[rows, ls]
            mix_ref[rows, ls] = (o * r * gn_ref[...] * (gb * _sigmoid(gb))).astype(BF16)

        def chunk(n, carry):
            _alternate([one_head(h, n) for h in range(HG)])
            return carry

        lax.fori_loop(0, NC, chunk, 0)

    tile = pltpu.VMEM((HG, CHUNK, LANE), F32)
    return pl.pallas_call(
        body, grid=(NG,), name="hgrn_fwd",
        in_specs=[blk_in(0), blk_in(NG), blk_in(2 * NG), blk_in(3 * NG), pl.BlockSpec((1, W), lambda g: (0, g)),
                  pl.BlockSpec((1, LANE), lambda g: (0, 0))],
        out_specs=[col, col, pl.BlockSpec((HG, NC, LANE, LANE), lambda g: (g, 0, 0, 0))],
        out_shape=[_sds((T, RW), BF16), _sds((T, RW), F32), _sds((RH, NC, LANE, LANE), F32)],
        scratch_shapes=[pltpu.VMEM((HG, LANE, LANE), F32), tile, tile, tile],
        compiler_params=_params(("parallel",), big=True),
    )(proj, proj, proj, proj, lb, gn)


def _hgrn_bwd(proj, dmixin, o_b, st_all, lb, gn, AW, RW):
    T = proj.shape[0]
    RH, NC, NSUB = RW // LANE, T // CHUNK, CHUNK // SUB
    HG, W, NG, blk_in, col = _hgrn_specs(T, RW, AW)

    def body(q_ref, f_ref, i_ref, g_ref, o_ref, dn_ref, stall_ref, lb_ref, gn_ref,
             dq_ref, df_ref, di_ref, dg_ref, dlb_ref, dgn_ref, dst_all, bs_all, kks_all, ics_all, p2_all, dic_all):
        dst_all[...] = jnp.zeros_like(dst_all)
        dlb_ref[...] = jnp.zeros_like(dlb_ref)
        dgn_ref[...] = jnp.zeros_like(dgn_ref)
        ltri = _ltri()
        rowi = lax.broadcasted_iota(jnp.int32, (SUB, 1), 0)
        last = lax.broadcasted_iota(jnp.int32, (CHUNK, 1), 0) == CHUNK - 1

        def one_head(h, n):
            ls = slice(h * LANE, (h + 1) * LANE)
            dst, bs, kks, ics = dst_all.at[h], bs_all.at[h], kks_all.at[h], ics_all.at[h]
            p2, dic = p2_all.at[h], dic_all.at[h]
            rows, lbv, qb, sg, f, kk, sq, qs, b = _hgrn_gates(n, ls, q_ref, f_ref, lb_ref, ltri)
            ic = i_ref[rows, ls]
            stv = stall_ref[h, n]
            dstv = dst[...]
            o = o_ref[rows, ls]
            dn = dn_ref[rows, ls]
            gb = g_ref[rows, ls]
            sgb = _sigmoid(gb)
            r = lax.rsqrt(jnp.mean(o * o, axis=-1, keepdims=True) + EPS)
            gnv = gn_ref[...]
            dg_ref[rows, ls] = (dn * (o * r * gnv) * (sgb * (1.0 + gb * (1.0 - sgb)))).astype(BF16)
            dy = dn * (gb * sgb)
            dgn_ref[h] += _colsum(dy * o * r)
            a_ = dy * gnv
            do = r * (a_ - o * (r * r) * jnp.mean(a_ * o, axis=-1, keepdims=True))
            do_b = do.astype(BF16)
            bs[...] = b
            kks[...] = kk
            ics[...] = ic
            yield
            ic_b = ic.astype(BF16)
            eb = jnp.exp(b)
            bl = bs[CHUNK - 1:CHUNK, :]
            ebl = jnp.exp(bl)
            dec = jnp.exp(bl - b)
            kd = (kk * dec).astype(BF16)
            dst_b = dstv.astype(BF16)
            dqs = _dot(do_b, stv.astype(BF16), NN) * eb
            dkk2 = _dot(ic_b, dst_b, NN) * dec
            dic[...] = _dot(kd, dst_b, NT)
            dbl = ebl * _colsum(stv * dstv) + _colsum(kk * dkk2)
            dst[...] = dstv * ebl + _dot(do_b, (qs * eb).astype(BF16), TN)
            yield
            p2[...] = jnp.zeros_like(p2)
            p1_pieces = []
            for blk in range(NSUB):
                s0 = blk * SUB
                bI, qI, doI = b[s0:s0 + SUB], qs[s0:s0 + SUB], do[s0:s0 + SUB]
                if blk == 0:
                    p1 = jnp.zeros((SUB, LANE), F32)
                else:
                    ref = bs[s0 - 1:s0, :]
                    eq = jnp.exp(bI - ref)
                    ek = jnp.exp(ref - b[0:s0])
                    qt = (qI * eq).astype(BF16)
                    kt = (kk[0:s0] * ek).astype(BF16)
                    doI_b = doI.astype(BF16)
                    dic[0:s0, :] += _dot(_dot(qt, kt, NT).astype(BF16), doI_b, TN)
                    da = _dot(doI_b, ic_b[0:s0], NT).astype(BF16)
                    p1 = _dot(da, kt, NN) * eq
                    p2[0:s0, :] += _dot(da, qt, TN) * ek
                    yield
                acc = [p1[g * ROWS:(g + 1) * ROWS] for g in range(SUB // ROWS)]
                for s in range(SUB):
                    sr = s0 + s
                    g0 = s // ROWS
                    lo = g0 * ROWS
                    keep = rowi[lo:] >= s
                    kk_s = kks[sr:sr + 1, :]
                    e = jnp.exp(jnp.minimum(bI[lo:] - bs[sr:sr + 1, :], 0.0))
                    w = qI[lo:] * e
                    a = jnp.where(keep, jnp.sum(w * kk_s, axis=-1, keepdims=True), 0.0)
                    da_s = jnp.where(keep, jnp.sum(doI[lo:] * ics[sr:sr + 1, :], axis=-1, keepdims=True), 0.0)
                    add = da_s * kk_s * e
                    for g in range(g0, SUB // ROWS):
                        acc[g] = acc[g] + add[(g - g0) * ROWS:(g - g0 + 1) * ROWS]
                    p2[sr:sr + 1, :] += _colsum(da_s * w)
                    dic[sr:sr + 1, :] += _colsum(a * doI[lo:])
                    yield
                p1_pieces.extend(acc)
            dqs = dqs + jnp.concatenate(p1_pieces, axis=0)
            dkk = dkk2 + p2[...]
            db = qs * dqs - kk * dkk + jnp.where(last, dbl, 0.0)
            dgl = _tri_dot(ltri, db, TN)
            yield
            dfv = dgl / f - dkk
            df_ref[rows, ls] = (dfv * (1.0 - lbv) * sg * (1.0 - sg)).astype(BF16)
            dlb_ref[:, ls] += _colsum(dfv * (1.0 - sg))
            dq_ref[rows, ls] = (dqs * (sq * (1.0 + qb * (1.0 - sq)))).astype(BF16)
            di_ref[rows, ls] = dic[...].astype(BF16)

        def chunk(k, carry):
            _alternate([one_head(h, NC - 1 - k) for h in range(HG)])
            return carry

        lax.fori_loop(0, NC, chunk, 0)

    tile = pltpu.VMEM((HG, CHUNK, LANE), F32)
    return pl.pallas_call(
        body, grid=(NG,), name="hgrn_bwd",
        in_specs=[blk_in(0), blk_in(NG), blk_in(2 * NG), blk_in(3 * NG), col,
                  pl.BlockSpec((T, W), lambda g: (0, AW // W + g)),
                  pl.BlockSpec((HG, NC, LANE, LANE), lambda g: (g, 0, 0, 0)),
                  pl.BlockSpec((1, W), lambda g: (0, g)), pl.BlockSpec((1, LANE), lambda g: (0, 0))],
        out_specs=[col, col, col, col, pl.BlockSpec((1, W), lambda g: (0, g)),
                   pl.BlockSpec((HG, 1, LANE), lambda g: (g, 0, 0))],
        out_shape=[_sds((T, RW), BF16)] * 4 + [_sds((1, RW), F32), _sds((RH, 1, LANE), F32)],
        scratch_shapes=[pltpu.VMEM((HG, LANE, LANE), F32), tile, tile, tile, tile, tile],
        compiler_params=_params(("parallel",), big=True),
    )(proj, proj, proj, proj, o_b, dmixin, st_all, lb, gn)


def _prep(c, lb_logits, rb_pad, max_rel):
    D, RW = c.shape[-1], lb_logits.shape[-1]
    H, rbp = rb_pad.shape

    def body(c_ref, l_ref, rb_ref, cact_ref, lb_ref, gv_ref):
        cv = c_ref[...]
        cact_ref[...] = cv * _sigmoid(cv)
        lb_ref[...] = _sigmoid(l_ref[0:1, :] - l_ref[1:2, :])
        gv_ref[...] = _dot(rb_ref[...], _bias_onehot(rbp, max_rel), NN, HIGHEST)

    return pl.pallas_call(
        body, name="prep", out_shape=[_sds((1, D), F32), _sds((1, RW), F32), _sds((H, TAB), F32)],
    )(c, lb_logits, rb_pad)


def _mod_part(c_all, w_ada_s, b_ada_s):
    B, D = c_all.shape
    ns = w_ada_s.shape[1]
    tn = _tile(ns, 768, LANE)

    def body(c_ref, w_ref, b_ref, o_ref):
        o_ref[...] = _dot(c_ref[...], w_ref[...], NN) + b_ref[...]

    return pl.pallas_call(
        body, grid=(ns // tn,), name="mod_part",
        in_specs=[pl.BlockSpec((B, D), lambda j: (0, 0)), pl.BlockSpec((D, tn), lambda j: (0, j)),
                  pl.BlockSpec((1, tn), lambda j: (0, j))],
        out_specs=pl.BlockSpec((B, tn), lambda j: (0, j)),
        out_shape=_sds((B, ns), F32), compiler_params=_params(("parallel",)),
    )(c_all, w_ada_s, b_ada_s)


def _adam(w, g, m, v):
    m = ADAM_B1 * m + (1.0 - ADAM_B1) * g
    v = ADAM_B2 * v + (1.0 - ADAM_B2) * (g * g)
    m_hat = m * (1.0 / (1.0 - ADAM_B1 ** ADAM_STEP))
    v_hat = v * (1.0 / (1.0 - ADAM_B2 ** ADAM_STEP))
    return -ADAM_LR * (m_hat / (jnp.sqrt(v_hat) + ADAM_EPS) + ADAM_WD * w), m, v


def _adam_ada(c_all, dmod_s, w, m, v):
    B, D = c_all.shape
    ns = w.shape[1]
    tr, tn = _tile(D, 512, LANE), _tile(ns, 768, LANE)

    def body(c_ref, d_ref, w_ref, m_ref, v_ref, g_out, dw_out, m_out, v_out):
        g = _dot(c_ref[...], d_ref[...], TN)
        g_out[...] = g
        dw_out[...], m_out[...], v_out[...] = _adam(w_ref[...], g, m_ref[...], v_ref[...])

    big = pl.BlockSpec((tr, tn), lambda i, j: (i, j))
    return pl.pallas_call(
        body, grid=(D // tr, ns // tn), name="adam_w_ada",
        in_specs=[pl.BlockSpec((B, tr), lambda i, j: (0, i)), pl.BlockSpec((B, tn), lambda i, j: (0, j)),
                  big, big, big],
        out_specs=[big] * 4, out_shape=[_sds((D, ns), F32)] * 4,
        compiler_params=_params(("parallel", "parallel")),
    )(c_all, dmod_s, w, m, v)


def _adam_shard(parts, w, m, v, name):
    R, C = w.shape
    tr = _tile(R, 256, 16)

    def body(p_ref, w_ref, m_ref, v_ref, g_out, dw_out, m_out, v_out):
        g = p_ref[0].astype(F32)
        for k in range(1, N_DEV // 2):
            g = g + p_ref[k].astype(F32)
        g_out[...] = g
        dw_out[...], m_out[...], v_out[...] = _adam(w_ref[...], g, m_ref[...], v_ref[...])

    big = pl.BlockSpec((tr, C), lambda i: (i, 0))
    return pl.pallas_call(
        body, grid=(R // tr,), name=name,
        in_specs=[pl.BlockSpec((N_DEV // 2, tr, C), lambda i: (0, i, 0)), big, big, big],
        out_specs=[big] * 4, out_shape=[_sds((R, C), F32)] * 4,
        compiler_params=_params(("parallel",), big=True),
    )(parts, w, m, v)


def _pair_sum(g8, land, core, name):
    _, NCHIP, R, C = g8.shape
    tr = _tile(R, 1024, 16)

    def body(core_ref, g_ref, l_ref, o_ref):
        o_ref[...] = g_ref[...] + l_ref[...]

    return pl.pallas_call(
        body, name=name,
        grid_spec=pltpu.PrefetchScalarGridSpec(
            num_scalar_prefetch=1, grid=(NCHIP, R // tr),
            in_specs=[pl.BlockSpec((None, None, tr, C), lambda k, i, core_ref: (core_ref[0], k, i, 0)),
                      pl.BlockSpec((None, tr, C), lambda k, i, core_ref: (k, i, 0))],
            out_specs=pl.BlockSpec((None, tr, C), lambda k, i, core_ref: (k, i, 0))),
        out_shape=_sds((NCHIP, R, C), BF16), compiler_params=_params(("parallel", "parallel")),
    )(core, g8, land)


SMALL = ("b_ada", "rel_bias", "attn_norm_g", "lb_logits", "gnorm_g", "ln1_g", "ln1_b", "ln2_g", "ln2_b")


def _small_update(parts, loss_parts, lbv, ws, ms, vs, max_rel):
    n = len(SMALL)

    def body(*refs):
        part_refs = dict(zip(SMALL, refs[:n]))
        loss_in, lb_ref = refs[n], refs[n + 1]
        w_refs, m_refs, v_refs = refs[n + 2:2 * n + 2], refs[2 * n + 2:3 * n + 2], refs[3 * n + 2:4 * n + 2]
        outs = refs[4 * n + 2:]

        def total(ref):
            tot = ref[0]
            for k in range(1, N_DEV):
                tot = tot + ref[k]
            return tot

        outs[0][...] = jnp.sum(total(loss_in), axis=-1, keepdims=True)
        for idx, name in enumerate(SMALL):
            g = total(part_refs[name])
            if name == "rel_bias":
                g = _dot(g, _bias_onehot(w_refs[idx].shape[1], max_rel), NT, HIGHEST)
            elif name == "lb_logits":
                lb = lb_ref[...]
                sign = (1 - 2 * lax.broadcasted_iota(jnp.int32, (2, 1), 0)).astype(F32)
                g = sign * (g * lb * (1.0 - lb))
            elif name == "gnorm_g":
                g = _colsum(g)
            dw, mm, vv = _adam(w_refs[idx][...], g, m_refs[idx][...], v_refs[idx][...])
            outs[1 + 4 * idx][...] = g
            outs[2 + 4 * idx][...] = dw
            outs[3 + 4 * idx][...] = mm
            outs[4 + 4 * idx][...] = vv

    out_shape = [_sds((1, 1), F32)]
    for w in ws:
        out_shape += [_sds(w.shape, F32)] * 4
    return pl.pallas_call(body, name="small_update", out_shape=out_shape, compiler_params=_params(big=True))(
        *[parts[k] for k in SMALL], loss_parts, lbv, *ws, *ms, *vs)


def _place():
    x, y, c = lax.axis_index("x"), lax.axis_index("y"), lax.axis_index("c")
    return x, y, c, [(1 - x, y), (x, 1 - y), (1 - x, 1 - y)]


def _all_gather(shard, name):
    HBM = pl.BlockSpec(memory_space=pl.ANY)

    def body(x_ref, out_ref, send_sems, recv_sems, local_sem):
        x, y, c, chips = _place()
        me, sibling = (x, y, c), (x, y, 1 - c)

        def slot(px, py, pc):
            return out_ref.at[4 * px + 2 * py + pc]

        def copy(k, block, to, src=None):
            return pltpu.make_async_remote_copy(
                src_ref=slot(*block) if src is None else src, dst_ref=slot(*block),
                send_sem=send_sems.at[k], recv_sem=recv_sems.at[k], device_id=to, device_id_type=MESH)

        mine = pltpu.make_async_copy(x_ref, slot(*me), local_sem)
        mine.start()
        first = [copy(0, me, sibling, src=x_ref)]
        first += [copy(1 + j, me, (*chip, c), src=x_ref) for j, chip in enumerate(chips)]
        for cp in first:
            cp.start()
        passed = [copy(4 + j, (*chip, c), sibling) for j, chip in enumerate(chips)]
        for j, chip in enumerate(chips):
            copy(1 + j, (*chip, c), me).wait_recv()
            passed[j].start()
        copy(0, sibling, me).wait_recv()
        for j, chip in enumerate(chips):
            copy(4 + j, (*chip, 1 - c), me).wait_recv()
        for cp in first + passed:
            cp.wait_send()
        mine.wait()

    return pl.pallas_call(
        body, name=name, out_shape=_sds((N_DEV,) + shard.shape, shard.dtype),
        in_specs=[HBM], out_specs=HBM,
        scratch_shapes=[pltpu.SemaphoreType.DMA((7,)), pltpu.SemaphoreType.DMA((7,)), pltpu.SemaphoreType.DMA(())],
    )(shard)


SEM_SPEC = pl.BlockSpec(memory_space=pltpu.SEMAPHORE)
HBM_SPEC = pl.BlockSpec(memory_space=pltpu.HBM)
EFFECT = pltpu.SideEffectType.DATAFLOW_SIDE_EFFECTING


def _remote(src, dst, send_sems, recv_sems, k, dev):
    return pltpu.make_async_remote_copy(src_ref=src, dst_ref=dst, send_sem=send_sems.at[k], recv_sem=recv_sems.at[k],
                                        device_id=dev, device_id_type=MESH)


def _copy_start(name, bufs, plan, n, after):
    nb = len(bufs)

    def body(*refs):
        send_sems, recv_sems = refs[nb + 1], refs[nb + 2]
        for k, (src, dst, dev) in enumerate(plan(*refs[:nb])):
            _remote(src, dst, send_sems, recv_sems, k, dev).start()
        refs[-1][...] = jnp.zeros_like(refs[-1])

    out = pl.pallas_call(
        body, name=name,
        out_shape=(pltpu.SemaphoreType.DMA((n,)), pltpu.SemaphoreType.DMA((n,)),
                   *[pltpu.HBM(b.shape, b.dtype) for b in bufs], _sds((8, LANE), F32)),
        in_specs=[HBM_SPEC] * nb + [ORDER_ONLY],
        out_specs=(SEM_SPEC, SEM_SPEC, *[HBM_SPEC] * nb, pl.BlockSpec(memory_space=pltpu.VMEM)),
        input_output_aliases={i: 2 + i for i in range(nb)},
        compiler_params=pltpu.CompilerParams(has_side_effects=EFFECT),
    )(*[pltpu.with_memory_space_constraint(b, pltpu.HBM) for b in bufs], after)
    return (out[0], out[1]), list(out[2:2 + nb]), out[-1]


def _copy_wait(name, sems, bufs, plan, after, only=None):
    nb = len(bufs)

    def body(*refs):
        send_sems, recv_sems = refs[nb], refs[nb + 1]
        for k, (src, dst, dev) in enumerate(plan(*refs[:nb])):
            if only is not None and k not in only:
                continue
            cp = _remote(src, dst, send_sems, recv_sems, k, dev)
            cp.wait_send()
            cp.wait_recv()

    out = pl.pallas_call(
        body, name=name, out_shape=tuple(pltpu.HBM(b.shape, b.dtype) for b in bufs),
        in_specs=[HBM_SPEC] * nb + [SEM_SPEC, SEM_SPEC, pl.BlockSpec(memory_space=pl.ANY)],
        out_specs=tuple([HBM_SPEC] * nb), input_output_aliases={i: i for i in range(nb)},
        compiler_params=pltpu.CompilerParams(has_side_effects=EFFECT),
    )(*bufs, sems[0], sems[1], after)
    return list(out)


def _ag_plan_chips(shard_ref, out_ref):
    x, y, c, chips = _place()
    mine = out_ref.at[4 * x + 2 * y + c]
    return [(shard_ref, mine, (x, y, 1 - c))] + [(shard_ref, mine, (*chip, c)) for chip in chips]


def _ag_plan_pass(out_ref):
    x, y, c, chips = _place()
    slots = [out_ref.at[4 * chip[0] + 2 * chip[1] + c] for chip in chips]
    return [(s, s, (x, y, 1 - c)) for s in slots]


def _rs_plan_pair(g_ref, land_ref):
    x, y, c, _ = _place()
    return [(g_ref.at[1 - c], land_ref, (x, y, 1 - c))]


def _rs_plan_chips(p_ref, land_ref):
    x, y, c, chips = _place()
    return [(p_ref.at[2 * chip[0] + chip[1]], land_ref.at[2 * x + y], (*chip, c)) for chip in chips]


class _Gather:
    def __init__(self, shard, me, tag, after):
        self.tag = tag
        out = lax.dynamic_update_slice(lax.empty((N_DEV,) + shard.shape, shard.dtype), shard[None],
                                       (me,) + (0,) * shard.ndim)
        self.sems, (self.shard, self.out), self.token = _copy_start(
            "ag_start_" + tag, [shard, out], _ag_plan_chips, 4, after)

    def arrived_from_sibling(self, after):
        self.shard, self.out = _copy_wait("ag_wait_sibling_" + self.tag, self.sems, [self.shard, self.out],
                                          _ag_plan_chips, after, only=(0,))
        self.first_done = True
        return self.out

    def arrived_from_chips(self, after):
        rest = (1, 2, 3) if getattr(self, "first_done", False) else None
        _, out = _copy_wait("ag_wait_" + self.tag, self.sems, [self.shard, self.out], _ag_plan_chips, after, rest)
        self.sems, (self.out,), _ = _copy_start("ag_pass_" + self.tag, [out], _ag_plan_pass, 3, after)
        return self.out

    def passed_on(self, after):
        return _copy_wait("ag_pass_wait_" + self.tag, self.sems, [self.out], _ag_plan_pass, after)[0]


class _ReduceScatter:
    def __init__(self, g8, tag):
        self.tag = tag
        land = lax.empty(g8.shape[1:], g8.dtype)
        self.sems, self.bufs, self.token = _copy_start(
            "rs_pair_start_" + tag, [g8, land], _rs_plan_pair, 1, jnp.zeros((1,), F32))

    def pair_done(self, core, chip, after, start_after=None):
        g8, land = _copy_wait("rs_pair_wait_" + self.tag, self.sems, self.bufs, _rs_plan_pair, after)
        p4 = _pair_sum(g8, land, core, "rs_pair_sum_" + self.tag)
        own = lax.dynamic_slice_in_dim(p4, chip, 1, axis=0)
        land2 = lax.dynamic_update_slice(lax.empty(p4.shape, p4.dtype), own, (chip, 0, 0))
        self.sems, self.bufs, self.token = _copy_start(
            "rs_chips_start_" + self.tag, [p4, land2], _rs_plan_chips, 3,
            jnp.zeros((1,), F32) if start_after is None else start_after)

    def sums(self, after):
        return _copy_wait("rs_chips_wait_" + self.tag, self.sems, self.bufs, _rs_plan_chips, after)[1]


BIG = ("w_in", "w_o", "w_ffn_in", "w_ffn_out")
ORDER = ("w_ada", "b_ada", "w_in", "rel_bias", "attn_norm_g", "lb_logits", "gnorm_g", "w_o", "ln1_g", "ln1_b",
         "w_ffn_in", "w_ffn_out", "ln2_g", "ln2_b")


def kernel(x, c, w_ada, b_ada, w_in, rel_bias, attn_norm_g, lb_logits, gnorm_g, w_o, ln1_g, ln1_b, w_ffn_in, w_ffn_out, ln2_g, ln2_b, loss_target, m_w_ada, m_b_ada, m_w_in, m_rel_bias, m_attn_norm_g, m_lb_logits, m_gnorm_g, m_w_o, m_ln1_g, m_ln1_b, m_w_ffn_in, m_w_ffn_out, m_ln2_g, m_ln2_b, v_w_ada, v_b_ada, v_w_in, v_rel_bias, v_attn_norm_g, v_lb_logits, v_gnorm_g, v_w_o, v_ln1_g, v_ln1_b, v_w_ffn_in, v_w_ffn_out, v_ln2_g, v_ln2_b):
    W = dict(w_ada=w_ada, b_ada=b_ada, w_in=w_in, rel_bias=rel_bias, attn_norm_g=attn_norm_g, lb_logits=lb_logits,
             gnorm_g=gnorm_g, w_o=w_o, ln1_g=ln1_g, ln1_b=ln1_b, w_ffn_in=w_ffn_in, w_ffn_out=w_ffn_out,
             ln2_g=ln2_g, ln2_b=ln2_b)
    M = dict(w_ada=m_w_ada, b_ada=m_b_ada, w_in=m_w_in, rel_bias=m_rel_bias, attn_norm_g=m_attn_norm_g,
             lb_logits=m_lb_logits, gnorm_g=m_gnorm_g, w_o=m_w_o, ln1_g=m_ln1_g, ln1_b=m_ln1_b,
             w_ffn_in=m_w_ffn_in, w_ffn_out=m_w_ffn_out, ln2_g=m_ln2_g, ln2_b=m_ln2_b)
    V = dict(w_ada=v_w_ada, b_ada=v_b_ada, w_in=v_w_in, rel_bias=v_rel_bias, attn_norm_g=v_attn_norm_g,
             lb_logits=v_lb_logits, gnorm_g=v_gnorm_g, w_o=v_w_o, ln1_g=v_ln1_g, ln1_b=v_ln1_b,
             w_ffn_in=v_w_ffn_in, w_ffn_out=v_w_ffn_out, ln2_g=v_ln2_g, ln2_b=v_ln2_b)

    x2, tgt = x[0], loss_target[0]
    T, D = x2.shape
    AW, RW = attn_norm_g.shape[-1], lb_logits.shape[-1]
    MIX = AW + RW
    H, RH = AW // ATTN_HEAD_DIM, RW // LANE
    RB = rel_bias.shape[-1]
    max_rel = (RB - 1) // 2
    rbp = -(-RB // LANE) * LANE
    F = w_ffn_out.shape[1] * N_DEV
    half = N_DEV // 2
    xi, yi, ci = lax.axis_index("x"), lax.axis_index("y"), lax.axis_index("c")
    me = 4 * xi + 2 * yi + ci
    core = jnp.reshape(ci, (1,)).astype(jnp.int32)
    pad_rb = lambda a: jnp.pad(a[0], ((0, 0), (0, rbp - RB)))

    chip = 2 * xi + yi

    c_act, lbv, gv = _prep(c, lb_logits, pad_rb(rel_bias), max_rel)
    c_all = _all_gather(c_act, "ag_c").reshape(N_DEV, D)
    ns_ada = w_ada.shape[-1]
    mod_part = _mod_part(c_all, w_ada[0], lax.dynamic_slice_in_dim(b_ada, me * ns_ada, ns_ada, axis=1))
    mod_all = _all_gather(mod_part, "ag_mod")
    mod6 = lax.dynamic_index_in_dim(mod_all, me, axis=1, keepdims=False).reshape(6, D)

    ag_in = _Gather(w_in[0].astype(BF16), me, "w_in", mod_all)
    ag_o = _Gather(w_o[0].astype(BF16), me, "w_o", ag_in.token)
    ag_f1 = _Gather(w_ffn_in[0].astype(BF16), me, "w_ffn_in", ag_o.token)
    ag_f2 = _Gather(w_ffn_out[0].astype(BF16), me, "w_ffn_out", ag_f1.token)

    h1 = _ln_mod(x2, mod6 + ag_f2.token[0, 0])
    ids = lambda pairs: jnp.stack([4 * px + 2 * py + pc for px, py, pc in pairs]).astype(jnp.int32)
    others = [(1 - xi, yi), (xi, 1 - yi), (1 - xi, 1 - yi)]
    proj = lax.empty((T, w_in.shape[-1] * N_DEV), F32)
    proj = _mm_gathered(h1, ag_in.arrived_from_sibling(h1), ids([(xi, yi, ci), (xi, yi, 1 - ci)]), proj, "in_proj_a")
    proj = _mm_gathered(h1, ag_in.arrived_from_chips(proj), ids([(*ch, ci) for ch in others]), proj, "in_proj_b")
    wg_in = ag_in.passed_on(proj)
    proj = _mm_gathered(h1, wg_in, ids([(*ch, 1 - ci) for ch in others]), proj, "in_proj_c")
    ag_o.arrived_from_chips(proj)
    mix_a = _attn_fwd(proj, gv, attn_norm_g, AW)
    wg_o = ag_o.passed_on(mix_a).reshape(MIX, D)
    mix_b, o_b, st_all = _hgrn_fwd(proj, lbv, gnorm_g, AW, RW)
    mixin = jnp.concatenate([mix_a, mix_b], axis=1)
    mix = _mm_nn(mixin, wg_o, "out_proj")
    ag_f1.arrived_from_chips(mix)
    x1, h2 = _mid_fwd(x2, mix, mod6, ln1_g, ln1_b)
    wg_f1 = ag_f1.passed_on(h2)
    gu, act = _mm_swiglu(h2, wg_f1)
    ag_f2.arrived_from_chips(act)
    wg_f2 = ag_f2.passed_on(act).reshape(F, D)
    ff = _mm_nn(act, wg_f2, "ffn_out")
    dff, dx1a, vec_a = _final(x1, ff, mod6, ln2_g, ln2_b, tgt)

    du = _mm_swiglu_bwd(dff, wg_f2, gu)
    rs_f2 = _ReduceScatter(_mm_tn_rows(dff, act, dff, F // N_DEV, "grad_w_ffn_out"), "w_ffn_out")
    tm = _tile(T, 512, 16)
    du_ij = lambda tm_, ns: pl.BlockSpec((None, tm_, ns), lambda i, j: (j // half, i, j % half))
    du_jm = lambda tm_, ns: pl.BlockSpec((None, tm_, ns), lambda j, m: (j // half, m, j % half))
    dh2 = _mm_gathered_nt(rs_f2.token, du, du_ij, wg_f1, T, tm, "ffn_in_bwd")
    rs_f2.pair_done(core, chip, dh2)
    tm_red = _tile(T, 1024, 16)
    gw_f1 = _mm_tn_gathered(rs_f2.token, h2, du, du_jm, wg_f1.shape[-1], tm_red, "grad_w_ffn_in")
    rs_f1 = _ReduceScatter(gw_f1.reshape(2, half, D, -1), "w_ffn_in")
    dmix, dxa, vec_b = _mid_bwd(x2, mix, x1, dx1a, dh2, mod6 + rs_f1.token[0, 0], ln1_g)
    dmixin = _mm_nt(dmix, wg_o, "out_proj_bwd")
    rs_f1.pair_done(core, chip, dmixin)
    rs_o = _ReduceScatter(_mm_tn_rows(rs_f1.token, mixin, dmix, MIX // N_DEV, "grad_w_o"), "w_o")
    dq, dk, dv, dgv, dga = _attn_bwd(proj, dmixin, gv + rs_o.token[0, 0], attn_norm_g, AW)
    rs_o.pair_done(core, chip, dq)
    dqb, dfl, dib, dgb, dlb, dgn = _hgrn_bwd(proj, dmixin, o_b, st_all, lbv + rs_o.token[0, 0], gnorm_g, AW, RW)
    dproj = jnp.concatenate([dq, dk, dv, dqb, dfl, dib, dgb], axis=1)
    p_ij = lambda tm_, ns: pl.BlockSpec((tm_, ns), lambda i, j: (i, j))
    p_jm = lambda tm_, ns: pl.BlockSpec((tm_, ns), lambda j, m: (m, j))
    gw_in = _mm_tn_gathered(rs_o.token, h1, dproj, p_jm, wg_in.shape[-1], tm_red, "grad_w_in")
    rs_in = _ReduceScatter(gw_in.reshape(2, half, D, -1), "w_in")
    dh1 = _mm_gathered_nt(rs_in.token, dproj, p_ij, wg_in, T, tm, "in_proj_bwd")
    grad_x, vec_c = _first_bwd(x2, dh1, dxa, mod6)

    dmod = jnp.concatenate([vec_c[1:2], vec_c[0:1], vec_b[4:5], vec_b[1:2], vec_b[0:1], vec_a[2:3]], axis=0)
    pieces = dict(b_ada=dmod, rel_bias=dgv, attn_norm_g=dga, lb_logits=dlb, gnorm_g=dgn, ln1_g=vec_b[2:3],
                  ln1_b=vec_b[3:4], ln2_g=vec_a[0:1], ln2_b=vec_a[1:2], loss=vec_a[3:4])
    widths = dict(b_ada=(1, 6 * D), rel_bias=(H, TAB), attn_norm_g=(1, AW), lb_logits=(1, RW), gnorm_g=(RH, LANE),
                  ln1_g=(1, D), ln1_b=(1, D), ln2_g=(1, D), ln2_b=(1, D), loss=(1, D))
    packed = jnp.concatenate([pieces[k].reshape(-1, LANE) for k in widths], axis=0)
    gathered = _all_gather(packed, "ag_small")
    rs_in.pair_done(core, chip, dh1, start_after=gathered)
    parts, r0 = {}, 0
    for k, (rows, width) in widths.items():
        nr = rows * width // LANE
        parts[k] = gathered[:, r0:r0 + nr, :].reshape(N_DEV, rows, width)
        r0 += nr
    prep_small = lambda d, k: pad_rb(d[k]) if k == "rel_bias" else d[k]
    small = _small_update(parts, parts["loss"], lbv, [prep_small(W, k) for k in SMALL],
                          [prep_small(M, k) for k in SMALL], [prep_small(V, k) for k in SMALL], max_rel)
    loss = small[0].reshape(())
    res = {}
    for idx, k in enumerate(SMALL):
        four = small[1 + 4 * idx:5 + 4 * idx]
        if k == "rel_bias":
            four = [a[:, :RB][None] for a in four]
        res[k] = list(four)

    dmod_s = lax.dynamic_slice_in_dim(parts["b_ada"].reshape(N_DEV, 6 * D), me * ns_ada, ns_ada, axis=1)
    dmod_s = dmod_s + rs_in.token[0, 0]
    res["w_ada"] = [a[None] for a in _adam_ada(c_all, dmod_s, w_ada[0], m_w_ada[0], v_w_ada[0])]
    after = res["w_ada"][0]
    for k, rs in (("w_ffn_out", rs_f2), ("w_ffn_in", rs_f1), ("w_o", rs_o), ("w_in", rs_in)):
        four = _adam_shard(rs.sums(after), W[k][0], M[k][0], V[k][0], "adam_" + k)
        res[k] = [a[None] for a in four]
        after = four[0]

    out = [loss, grad_x[None]]
    for field in range(4):
        out += [res[k][field] for k in ORDER]
    return tuple(out)
```

```python
import functools

import jax
import jax.numpy as jnp
from jax import lax
from jax.experimental import pallas as pl
from jax.experimental.pallas import tpu as pltpu

F32 = jnp.float32
BF16 = jnp.bfloat16
MESH = pl.DeviceIdType.MESH
HIGHEST = lax.Precision.HIGHEST

N_DEV = 8
CHUNK = 64
N_PAST = 8
QBLK = 4 * CHUNK
KPAD = N_PAST * CHUNK
WIN = KPAD + QBLK
TAB = 1024
ATTN_HEAD_DIM = 64
ATTN_HEADS_PER_STEP = 4
REC_HEAD_DIM = 128
SUB = 16
ROWS = 8
LANE = 128
EPS = 1e-5
ALPHA = 2.0 ** 0.25
ADAM_LR, ADAM_B1, ADAM_B2, ADAM_EPS, ADAM_WD, ADAM_STEP = 0.001, 0.9, 0.999, 1e-08, 0.01, 10
NEG = -1e30
VMEM_LIMIT = 56 * 1024 * 1024


def _sds(shape, dtype):
    return jax.ShapeDtypeStruct(tuple(shape), dtype)


def _tile(n, pref, mult):
    best = None
    for t in range(mult, min(n, pref) + 1, mult):
        if n % t == 0:
            best = t
    return n if best is None else best


def _params(sem=None, big=False):
    kw = {}
    if sem is not None:
        kw["dimension_semantics"] = sem
    if big:
        kw["vmem_limit_bytes"] = VMEM_LIMIT
    return pltpu.CompilerParams(**kw)


def _sigmoid(v):
    return 1.0 / (1.0 + jnp.exp(-v))


def _dot(a, b, dims, precision=None):
    return lax.dot_general(a, b, (dims, ((), ())), preferred_element_type=F32, precision=precision)


NN = ((1,), (0,))
NT = ((1,), (1,))
TN = ((0,), (0,))


def _ln(v):
    mu = jnp.mean(v, axis=-1, keepdims=True)
    d = v - mu
    rstd = lax.rsqrt(jnp.mean(d * d, axis=-1, keepdims=True) + EPS)
    return d * rstd, rstd


def _ln_bwd(dxh, xh, rstd):
    return rstd * (dxh - jnp.mean(dxh, axis=-1, keepdims=True) - xh * jnp.mean(dxh * xh, axis=-1, keepdims=True))


def _colsum(v):
    return jnp.sum(v, axis=0, keepdims=True)


def _ln_mod(x2, mod6):
    T, D = x2.shape
    tm = _tile(T, 256, 8)

    def body(x_ref, mod_ref, o_ref):
        xh, _ = _ln(x_ref[...])
        o_ref[...] = (xh * (1.0 + mod_ref[1:2, :]) + mod_ref[0:1, :]).astype(BF16)

    return pl.pallas_call(
        body, grid=(T // tm,), name="ln_mod",
        in_specs=[pl.BlockSpec((tm, D), lambda i: (i, 0)), pl.BlockSpec((6, D), lambda i: (0, 0))],
        out_specs=pl.BlockSpec((tm, D), lambda i: (i, 0)),
        out_shape=_sds((T, D), BF16), compiler_params=_params(("parallel",)),
    )(x2, mod6)


def _mid_fwd(x2, mix, mod6, ln1_g, ln1_b):
    T, D = x2.shape
    tm = _tile(T, 256, 8)

    def body(x_ref, mix_ref, mod_ref, g_ref, b_ref, x1_ref, h2_ref):
        zh, _ = _ln(ALPHA * x_ref[...] + mod_ref[2:3, :] * mix_ref[...])
        x1 = zh * g_ref[...] + b_ref[...]
        x1_ref[...] = x1
        xh, _ = _ln(x1)
        h2_ref[...] = (xh * (1.0 + mod_ref[4:5, :]) + mod_ref[3:4, :]).astype(BF16)

    row = pl.BlockSpec((tm, D), lambda i: (i, 0))
    vec = pl.BlockSpec((1, D), lambda i: (0, 0))
    return pl.pallas_call(
        body, grid=(T // tm,), name="mid_fwd",
        in_specs=[row, row, pl.BlockSpec((6, D), lambda i: (0, 0)), vec, vec],
        out_specs=[row, row],
        out_shape=[_sds((T, D), F32), _sds((T, D), BF16)], compiler_params=_params(("parallel",)),
    )(x2, mix, mod6, ln1_g, ln1_b)


def _final(x1, ff, mod6, ln2_g, ln2_b, tgt):
    T, D = x1.shape
    tm = _tile(T, 256, 8)

    def body(x1_ref, ff_ref, mod_ref, g_ref, b_ref, t_ref, dff_ref, dx1_ref, vec_ref):
        @pl.when(pl.program_id(0) == 0)
        def _():
            vec_ref[...] = jnp.zeros_like(vec_ref)

        ff_v = ff_ref[...]
        gate2 = mod_ref[5:6, :]
        zh, rstd = _ln(ALPHA * x1_ref[...] + gate2 * ff_v)
        err = zh * g_ref[...] + b_ref[...] - t_ref[...]
        dy = err * (1.0 / D)
        dz = _ln_bwd(dy * g_ref[...], zh, rstd)
        dff_ref[...] = (gate2 * dz).astype(BF16)
        dx1_ref[...] = ALPHA * dz
        vec_ref[0:1, :] += _colsum(dy * zh)
        vec_ref[1:2, :] += _colsum(dy)
        vec_ref[2:3, :] += _colsum(dz * ff_v)
        vec_ref[3:4, :] += _colsum(err * err) * (0.5 / D)

    row = pl.BlockSpec((tm, D), lambda i: (i, 0))
    vec = pl.BlockSpec((1, D), lambda i: (0, 0))
    return pl.pallas_call(
        body, grid=(T // tm,), name="final_fwd_bwd",
        in_specs=[row, row, pl.BlockSpec((6, D), lambda i: (0, 0)), vec, vec, row],
        out_specs=[row, row, pl.BlockSpec((8, D), lambda i: (0, 0))],
        out_shape=[_sds((T, D), BF16), _sds((T, D), F32), _sds((8, D), F32)],
        compiler_params=_params(("arbitrary",)),
    )(x1, ff, mod6, ln2_g, ln2_b, tgt)


def _mid_bwd(x2, mix, x1, dx1a, dh2, mod6, ln1_g):
    T, D = x2.shape
    tm = _tile(T, 256, 8)

    def body(x_ref, mix_ref, x1_ref, dx1a_ref, dh2_ref, mod_ref, g_ref, dmix_ref, dxa_ref, vec_ref):
        @pl.when(pl.program_id(0) == 0)
        def _():
            vec_ref[...] = jnp.zeros_like(vec_ref)

        dh2 = dh2_ref[...]
        xh, rstd = _ln(x1_ref[...])
        dx1 = dx1a_ref[...] + _ln_bwd(dh2 * (1.0 + mod_ref[4:5, :]), xh, rstd)
        mix_v = mix_ref[...]
        gate1 = mod_ref[2:3, :]
        zh, rstdz = _ln(ALPHA * x_ref[...] + gate1 * mix_v)
        dz = _ln_bwd(dx1 * g_ref[...], zh, rstdz)
        dmix_ref[...] = (gate1 * dz).astype(BF16)
        dxa_ref[...] = ALPHA * dz
        vec_ref[0:1, :] += _colsum(dh2 * xh)
        vec_ref[1:2, :] += _colsum(dh2)
        vec_ref[2:3, :] += _colsum(dx1 * zh)
        vec_ref[3:4, :] += _colsum(dx1)
        vec_ref[4:5, :] += _colsum(dz * mix_v)

    row = pl.BlockSpec((tm, D), lambda i: (i, 0))
    vec = pl.BlockSpec((1, D), lambda i: (0, 0))
    return pl.pallas_call(
        body, grid=(T // tm,), name="mid_bwd",
        in_specs=[row, row, row, row, row, pl.BlockSpec((6, D), lambda i: (0, 0)), vec],
        out_specs=[row, row, pl.BlockSpec((8, D), lambda i: (0, 0))],
        out_shape=[_sds((T, D), BF16), _sds((T, D), F32), _sds((8, D), F32)],
        compiler_params=_params(("arbitrary",)),
    )(x2, mix, x1, dx1a, dh2, mod6, ln1_g)


def _first_bwd(x2, dh1, dxa, mod6):
    T, D = x2.shape
    tm = _tile(T, 256, 8)

    def body(x_ref, dh1_ref, dxa_ref, mod_ref, gx_ref, vec_ref):
        @pl.when(pl.program_id(0) == 0)
        def _():
            vec_ref[...] = jnp.zeros_like(vec_ref)

        dh1 = dh1_ref[...]
        xh, rstd = _ln(x_ref[...])
        gx_ref[...] = dxa_ref[...] + _ln_bwd(dh1 * (1.0 + mod_ref[1:2, :]), xh, rstd)
        vec_ref[0:1, :] += _colsum(dh1 * xh)
        vec_ref[1:2, :] += _colsum(dh1)

    row = pl.BlockSpec((tm, D), lambda i: (i, 0))
    return pl.pallas_call(
        body, grid=(T // tm,), name="first_bwd",
        in_specs=[row, row, row, pl.BlockSpec((6, D), lambda i: (0, 0))],
        out_specs=[row, pl.BlockSpec((8, D), lambda i: (0, 0))],
        out_shape=[_sds((T, D), F32), _sds((8, D), F32)],
        compiler_params=_params(("arbitrary",)),
    )(x2, dh1, dxa, mod6)


def _slot(j):
    return (j % 2) * 4 + j // 2


def _mm_gathered(a, wg, shards, out, name):
    M, K = a.shape
    _, _, ns = wg.shape
    tm = _tile(M, 512, 16)

    def body(shards_ref, a_ref, w_ref, prev_ref, o_ref):
        o_ref[...] = _dot(a_ref[...], w_ref[...], NN)

    return pl.pallas_call(
        body, name=name,
        grid_spec=pltpu.PrefetchScalarGridSpec(
            num_scalar_prefetch=1, grid=(shards.shape[0], M // tm),
            in_specs=[pl.BlockSpec((tm, K), lambda j, i, s: (i, 0)),
                      pl.BlockSpec((None, K, ns), lambda j, i, s: (s[j], 0, 0)), ORDER_ONLY],
            out_specs=pl.BlockSpec((tm, ns), lambda j, i, s: (i, s[j]))),
        out_shape=_sds((M, N_DEV * ns), F32), input_output_aliases={3: 0},
        compiler_params=_params(("parallel", "parallel"), big=True),
    )(shards, a, wg, out)


def _mm_nn(a, b, name):
    M, K = a.shape
    _, N = b.shape
    tm, tn, tk = _tile(M, 512, 16), _tile(N, 1024, LANE), _tile(K, 2048, LANE)

    def body(a_ref, b_ref, o_ref):
        @pl.when(pl.program_id(2) == 0)
        def _():
            o_ref[...] = jnp.zeros_like(o_ref)

        o_ref[...] += _dot(a_ref[...], b_ref[...], NN)

    return pl.pallas_call(
        body, grid=(M // tm, N // tn, K // tk), name=name,
        in_specs=[pl.BlockSpec((tm, tk), lambda i, j, k: (i, k)), pl.BlockSpec((tk, tn), lambda i, j, k: (k, j))],
        out_specs=pl.BlockSpec((tm, tn), lambda i, j, k: (i, j)),
        out_shape=_sds((M, N), F32), compiler_params=_params(("parallel", "parallel", "arbitrary"), big=True),
    )(a, b)


def _mm_nt(a, b, name):
    M, K = a.shape
    N, _ = b.shape
    tm, tn = _tile(M, 512, 16), _tile(N, 1024, LANE)

    def body(a_ref, b_ref, o_ref):
        o_ref[...] = _dot(a_ref[...], b_ref[...], NT)

    return pl.pallas_call(
        body, grid=(M // tm, N // tn), name=name,
        in_specs=[pl.BlockSpec((tm, K), lambda i, j: (i, 0)), pl.BlockSpec((tn, K), lambda i, j: (j, 0))],
        out_specs=pl.BlockSpec((tm, tn), lambda i, j: (i, j)),
        out_shape=_sds((M, N), F32), compiler_params=_params(("parallel", "parallel"), big=True),
    )(a, b)


def _mm_swiglu(h2, wg):
    M, K = h2.shape
    _, _, ns = wg.shape
    half = N_DEV // 2
    tm = _tile(M, 256, 16)

    def body(a_ref, wgate_ref, wup_ref, gu_ref, act_ref):
        a = a_ref[...]
        g = _dot(a, wgate_ref[...], NN)
        u = _dot(a, wup_ref[...], NN)
        gu_ref[0] = g
        gu_ref[1] = u
        act_ref[...] = (g * _sigmoid(g) * u).astype(BF16)

    return pl.pallas_call(
        body, grid=(half, M // tm), name="ffn_in_swiglu",
        in_specs=[pl.BlockSpec((tm, K), lambda j, i: (i, 0)),
                  pl.BlockSpec((None, K, ns), lambda j, i: (j, 0, 0)),
                  pl.BlockSpec((None, K, ns), lambda j, i: (j + half, 0, 0))],
        out_specs=[pl.BlockSpec((2, tm, ns), lambda j, i: (0, i, j)), pl.BlockSpec((tm, ns), lambda j, i: (i, j))],
        out_shape=[_sds((2, M, half * ns), F32), _sds((M, half * ns), BF16)],
        compiler_params=_params(("parallel", "parallel"), big=True),
    )(h2, wg, wg)


def _mm_swiglu_bwd(dff, w2, gu):
    M, K = dff.shape
    F = w2.shape[0]
    tm, tn = _tile(M, 512, 16), _tile(F, 1408, LANE)

    def body(a_ref, b_ref, gu_ref, du_ref):
        da = _dot(a_ref[...], b_ref[...], NT)
        g = gu_ref[0]
        u = gu_ref[1]
        sg = _sigmoid(g)
        du_ref[0] = (da * u * (sg * (1.0 + g * (1.0 - sg)))).astype(BF16)
        du_ref[1] = (da * (g * sg)).astype(BF16)

    return pl.pallas_call(
        body, grid=(F // tn, M // tm), name="ffn_out_bwd_swiglu",
        in_specs=[pl.BlockSpec((tm, K), lambda j, i: (i, 0)), pl.BlockSpec((tn, K), lambda j, i: (j, 0)),
                  pl.BlockSpec((2, tm, tn), lambda j, i: (0, i, j))],
        out_specs=pl.BlockSpec((2, tm, tn), lambda j, i: (0, i, j)),
        out_shape=_sds((2, M, F), BF16), compiler_params=_params(("parallel", "parallel"), big=True),
    )(dff, w2, gu)


ORDER_ONLY = pl.BlockSpec(memory_space=pl.ANY)


def _mm_tn_rows(dep, a, b, rs, name):
    M, Ka = a.shape
    _, N = b.shape
    tm = _tile(M, 1024, 16)

    def body(_, a_ref, b_ref, o_ref, acc_ref):
        m = pl.program_id(1)

        @pl.when(m == 0)
        def _():
            acc_ref[...] = jnp.zeros_like(acc_ref)

        acc_ref[...] += _dot(a_ref[...], b_ref[...], TN)

        @pl.when(m == pl.num_programs(1) - 1)
        def _():
            o_ref[0, 0] = acc_ref[0:rs, :].astype(BF16)
            o_ref[1, 0] = acc_ref[rs:2 * rs, :].astype(BF16)

    return pl.pallas_call(
        body, grid=(N_DEV // 2, M // tm), name=name,
        in_specs=[ORDER_ONLY, pl.BlockSpec((tm, 2 * rs), lambda ch, m: (m, ch)),
                  pl.BlockSpec((tm, N), lambda ch, m: (m, 0))],
        out_specs=pl.BlockSpec((2, 1, rs, N), lambda ch, m: (0, ch, 0, 0)),
        out_shape=_sds((2, N_DEV // 2, rs, N), BF16),
        scratch_shapes=[pltpu.VMEM((2 * rs, N), F32)],
        compiler_params=_params(("parallel", "arbitrary"), big=True),
    )(dep, a, b)


def _mm_gathered_nt(dep, a, a_spec, wg, M, tm, name):
    _, K, ns = wg.shape

    def body(_, a_ref, w_ref, o_ref):
        @pl.when(pl.program_id(1) == 0)
        def _():
            o_ref[...] = jnp.zeros_like(o_ref)

        o_ref[...] += _dot(a_ref[...], w_ref[...], NT)

    return pl.pallas_call(
        body, grid=(M // tm, N_DEV), name=name,
        in_specs=[ORDER_ONLY, a_spec(tm, ns), pl.BlockSpec((None, K, ns), lambda i, j: (j, 0, 0))],
        out_specs=pl.BlockSpec((tm, K), lambda i, j: (i, 0)),
        out_shape=_sds((M, K), F32), compiler_params=_params(("parallel", "arbitrary"), big=True),
    )(dep, a, wg)


def _mm_tn_gathered(dep, h, a, a_spec, ns, tm, name):
    M, K = h.shape

    def body(_, h_ref, a_ref, o_ref, acc_ref):
        m = pl.program_id(1)

        @pl.when(m == 0)
        def _():
            acc_ref[...] = jnp.zeros_like(acc_ref)

        acc_ref[...] += _dot(h_ref[...], a_ref[...], TN)

        @pl.when(m == pl.num_programs(1) - 1)
        def _():
            o_ref[...] = acc_ref[...].astype(BF16)

    return pl.pallas_call(
        body, grid=(N_DEV, M // tm), name=name,
        in_specs=[ORDER_ONLY, pl.BlockSpec((tm, K), lambda j, m: (m, 0)), a_spec(tm, ns)],
        out_specs=pl.BlockSpec((None, K, ns), lambda j, m: (_slot(j), 0, 0)),
        out_shape=_sds((N_DEV, K, ns), BF16),
        scratch_shapes=[pltpu.VMEM((K, ns), F32)],
        compiler_params=_params(("parallel", "arbitrary"), big=True),
    )(dep, h, a)


def _bias_onehot(rbp, max_rel):
    r = lax.broadcasted_iota(jnp.int32, (rbp, TAB), 0)
    m = lax.broadcasted_iota(jnp.int32, (rbp, TAB), 1)
    dist = KPAD - jnp.where(m < WIN, m, m - TAB)
    return (r == jnp.clip(dist, -max_rel, max_rel) + max_rel).astype(F32)


def _attn_setup(i, hp, k_ref, v_ref, gv_ref, kpad, vpad, bias):
    ls = slice(i * ATTN_HEAD_DIM, (i + 1) * ATTN_HEAD_DIM)
    kpad[i][0:KPAD, :] = jnp.zeros((KPAD, ATTN_HEAD_DIM), BF16)
    vpad[i][0:KPAD, :] = jnp.zeros((KPAD, ATTN_HEAD_DIM), BF16)
    kpad[i][KPAD:, :] = k_ref[:, ls].astype(BF16)
    vpad[i][KPAD:, :] = v_ref[:, ls].astype(BF16)
    gvrow = gv_ref[pl.ds(hp * ATTN_HEADS_PER_STEP + i, 1), :]
    tab = pltpu.roll(jnp.broadcast_to(gvrow, (QBLK, TAB)), 0, 1, stride=1, stride_axis=0)
    row = lax.broadcasted_iota(jnp.int32, (QBLK, WIN), 0)
    col = lax.broadcasted_iota(jnp.int32, (QBLK, WIN), 1)
    first = jnp.bitwise_and(row, -CHUNK)
    seen = jnp.logical_and(col >= first, col < first + (N_PAST + 1) * CHUNK)
    bias[i][...] = jnp.where(seen, tab[:, 0:WIN], NEG)


def _attn_probs(b, q_ref, kpad, vpad, bias, col):
    pair = range(ATTN_HEADS_PER_STEP)
    ls = [slice(i * ATTN_HEAD_DIM, (i + 1) * ATTN_HEAD_DIM) for i in pair]
    r0 = pl.multiple_of(b * QBLK, QBLK)
    q = [q_ref[pl.ds(r0, QBLK), ls[i]].astype(BF16) for i in pair]
    kw = [kpad[i][pl.ds(r0, WIN), :] for i in pair]
    vw = [vpad[i][pl.ds(r0, WIN), :] for i in pair]
    s = [_dot(q[i], kw[i], NT) * (ATTN_HEAD_DIM ** -0.5) + bias[i][...] for i in pair]
    s = [jnp.where(col >= KPAD - r0, s[i], NEG) for i in pair]
    p = [jnp.exp(s[i] - jnp.max(s[i], axis=-1, keepdims=True)) for i in pair]
    pn = [p[i] / jnp.sum(p[i], axis=-1, keepdims=True) for i in pair]
    return r0, ls, q, kw, vw, pn


def _attn_fwd(proj, gv, ga, AW):
    T = proj.shape[0]
    AH = ATTN_HEADS_PER_STEP
    W = AH * ATTN_HEAD_DIM
    HP = AW // W

    def body(q_ref, k_ref, v_ref, gv_ref, ga_ref, o_ref, *scratch):
        kpad, vpad, bias = (scratch[k * AH:(k + 1) * AH] for k in range(3))
        hp = pl.program_id(0)
        for i in range(AH):
            _attn_setup(i, hp, k_ref, v_ref, gv_ref, kpad, vpad, bias)
        col = lax.broadcasted_iota(jnp.int32, (QBLK, WIN), 1)

        def block(b, carry):
            pair = range(AH)
            r0, ls, _, _, vw, pn = _attn_probs(b, q_ref, kpad, vpad, bias, col)
            o = [_dot(pn[i].astype(BF16), vw[i], NN) for i in pair]
            r = [lax.rsqrt(jnp.mean(o[i] * o[i], axis=-1, keepdims=True) + EPS) for i in pair]
            outs = [o[i] * r[i] * ga_ref[0:1, ls[i]] for i in pair]
            o_ref[pl.ds(r0, QBLK), :] = jnp.concatenate(outs, axis=1).astype(BF16)
            return carry

        lax.fori_loop(0, T // QBLK, block, 0)

    blk = lambda off: pl.BlockSpec((T, W), lambda hp: (0, off + hp))
    return pl.pallas_call(
        body, grid=(HP,), name="attn_fwd",
        in_specs=[blk(0), blk(HP), blk(2 * HP), pl.BlockSpec(gv.shape, lambda hp: (0, 0)),
                  pl.BlockSpec((1, W), lambda hp: (0, hp))],
        out_specs=pl.BlockSpec((T, W), lambda hp: (0, hp)),
        out_shape=_sds((T, AW), BF16),
        scratch_shapes=[pltpu.VMEM((T + KPAD, ATTN_HEAD_DIM), BF16)] * (2 * AH) + [pltpu.VMEM((QBLK, WIN), F32)] * AH,
        compiler_params=_params(("parallel",), big=True),
    )(proj, proj, proj, gv, ga)


def _attn_bwd(proj, dmixin, gv, ga, AW):
    T = proj.shape[0]
    AH = ATTN_HEADS_PER_STEP
    W = AH * ATTN_HEAD_DIM
    HP = AW // W
    scale = ATTN_HEAD_DIM ** -0.5

    def body(q_ref, k_ref, v_ref, dn_ref, gv_ref, ga_ref, dq_ref, dk_ref, dv_ref, dgv_ref, dga_ref, *scratch):
        kpad, vpad, dkacc, dvacc, bias, dbias = (scratch[k * AH:(k + 1) * AH] for k in range(6))
        hp = pl.program_id(0)
        for i in range(AH):
            _attn_setup(i, hp, k_ref, v_ref, gv_ref, kpad, vpad, bias)
            dkacc[i][...] = jnp.zeros_like(dkacc[i])
            dvacc[i][...] = jnp.zeros_like(dvacc[i])
            dbias[i][...] = jnp.zeros_like(dbias[i])
        dga_ref[...] = jnp.zeros_like(dga_ref)
        col = lax.broadcasted_iota(jnp.int32, (QBLK, WIN), 1)

        def block(b, carry):
            pair = range(AH)
            r0, lss, qs, kws, vws, pns = _attn_probs(b, q_ref, kpad, vpad, bias, col)
            pn_b = [pns[i].astype(BF16) for i in pair]
            o = [_dot(pn_b[i], vws[i], NN) for i in pair]
            r = [lax.rsqrt(jnp.mean(o[i] * o[i], axis=-1, keepdims=True) + EPS) for i in pair]
            dn = [dn_ref[pl.ds(r0, QBLK), lss[i]] for i in pair]
            for i in pair:
                dga_ref[i:i + 1, :] += _colsum(dn[i] * o[i] * r[i])
            a = [dn[i] * ga_ref[0:1, lss[i]] for i in pair]
            do_b = [(r[i] * (a[i] - o[i] * (r[i] * r[i]) * jnp.mean(a[i] * o[i], axis=-1, keepdims=True))).astype(BF16)
                    for i in pair]
            dp = [_dot(do_b[i], vws[i], NT) for i in pair]
            for i in pair:
                dvacc[i][pl.ds(r0, WIN), :] += _dot(pn_b[i], do_b[i], TN)
            ds = [pns[i] * (dp[i] - jnp.sum(pns[i] * dp[i], axis=-1, keepdims=True)) for i in pair]
            for i in pair:
                dbias[i][...] += ds[i]
            ds_b = [ds[i].astype(BF16) for i in pair]
            dq = [_dot(ds_b[i], kws[i], NN) * scale for i in pair]
            dq_ref[pl.ds(r0, QBLK), :] = jnp.concatenate(dq, axis=1).astype(BF16)
            for i in pair:
                dkacc[i][pl.ds(r0, WIN), :] += _dot(ds_b[i], qs[i], TN) * scale
            return carry

        lax.fori_loop(0, T // QBLK, block, 0)

        rr = lax.broadcasted_iota(jnp.int32, (QBLK, QBLK), 0)
        cc = lax.broadcasted_iota(jnp.int32, (QBLK, QBLK), 1)
        flip = (rr + cc == QBLK - 1).astype(BF16)
        for i in range(AH):
            ls = slice(i * ATTN_HEAD_DIM, (i + 1) * ATTN_HEAD_DIM)
            dk_ref[:, ls] = dkacc[i][KPAD:, :].astype(BF16)
            dv_ref[:, ls] = dvacc[i][KPAD:, :].astype(BF16)
            full = jnp.concatenate([dbias[i][...], jnp.zeros((QBLK, TAB - WIN), F32)], axis=1)
            hi = full.astype(BF16)
            lo = (full - hi.astype(F32)).astype(BF16)
            rev = _dot(flip, hi, NN) + _dot(flip, lo, NN)
            dgv_ref[i:i + 1, :] = _colsum(pltpu.roll(rev, TAB - (QBLK - 1), 1, stride=1, stride_axis=0))

    blk = lambda off: pl.BlockSpec((T, W), lambda hp: (0, off + hp))
    accs = lambda dt: [pltpu.VMEM((T + KPAD, ATTN_HEAD_DIM), dt)] * AH
    return pl.pallas_call(
        body, grid=(HP,), name="attn_bwd",
        in_specs=[blk(0), blk(HP), blk(2 * HP), blk(0), pl.BlockSpec(gv.shape, lambda hp: (0, 0)),
                  pl.BlockSpec((1, W), lambda hp: (0, hp))],
        out_specs=[blk(0), blk(0), blk(0), pl.BlockSpec((None, AH, TAB), lambda hp: (hp, 0, 0)),
                   pl.BlockSpec((None, AH, ATTN_HEAD_DIM), lambda hp: (hp, 0, 0))],
        out_shape=[_sds((T, AW), BF16), _sds((T, AW), BF16), _sds((T, AW), BF16),
                   _sds((HP, AH, TAB), F32), _sds((HP, AH, ATTN_HEAD_DIM), F32)],
        scratch_shapes=accs(BF16) + accs(BF16) + accs(F32) + accs(F32) + [pltpu.VMEM((QBLK, WIN), F32)] * (2 * AH),
        compiler_params=_params(("parallel",), big=True),
    )(proj, proj, proj, dmixin, gv, ga)


def _ltri():
    r = lax.broadcasted_iota(jnp.int32, (CHUNK, CHUNK), 0)
    c = lax.broadcasted_iota(jnp.int32, (CHUNK, CHUNK), 1)
    return (c <= r).astype(BF16)


def _tri_dot(tri, v, dims):
    hi = v.astype(BF16)
    lo = (v - hi.astype(F32)).astype(BF16)
    return _dot(tri, hi, dims) + _dot(tri, lo, dims)


HEADS_PER_STEP = 2


def _alternate(stages):
    live = list(stages)
    while live:
        for g in list(live):
            if next(g, StopIteration) is StopIteration:
                live.remove(g)


def _hgrn_gates(n, ls, q_ref, f_ref, lb_ref, ltri):
    r0 = pl.multiple_of(n * CHUNK, CHUNK)
    rows = pl.ds(r0, CHUNK)
    lb = lb_ref[:, ls]
    qb = q_ref[rows, ls]
    sg = _sigmoid(f_ref[rows, ls])
    f = lb + (1.0 - lb) * sg
    sq = _sigmoid(qb)
    b = _tri_dot(ltri, jnp.log(f), NN)
    return rows, lb, qb, sg, f, 1.0 - f, sq, qb * sq, b


def _hgrn_specs(T, RW, AW):
    HG = HEADS_PER_STEP
    W = HG * LANE
    base = 3 * AW // W
    blk_in = lambda off: pl.BlockSpec((T, W), lambda g: (0, base + off + g))
    col = pl.BlockSpec((T, W), lambda g: (0, g))
    return HG, W, RW // W, blk_in, col


def _hgrn_fwd(proj, lb, gn, AW, RW):
    T = proj.shape[0]
    RH, NC, NSUB = RW // LANE, T // CHUNK, CHUNK // SUB
    HG, W, NG, blk_in, col = _hgrn_specs(T, RW, AW)

    def body(q_ref, f_ref, i_ref, g_ref, lb_ref, gn_ref, mix_ref, o_ref, stall_ref, st_all, bs_all, kks_all, ics_all):
        st_all[...] = jnp.zeros_like(st_all)
        ltri = _ltri()
        rowi = lax.broadcasted_iota(jnp.int32, (SUB, 1), 0)

        def one_head(h, n):
            ls = slice(h * LANE, (h + 1) * LANE)
            st, bs, kks, ics = st_all.at[h], bs_all.at[h], kks_all.at[h], ics_all.at[h]
            rows, _, _, _, _, kk, _, qs, b = _hgrn_gates(n, ls, q_ref, f_ref, lb_ref, ltri)
            ic = i_ref[rows, ls]
            stv = st[...]
            stall_ref[h, n] = stv
            bs[...] = b
            kks[...] = kk
            ics[...] = ic
            yield
            o = _dot((qs * jnp.exp(b)).astype(BF16), stv.astype(BF16), NT)
            yield
            ic_b = ic.astype(BF16)
            pieces = []
            for blk in range(NSUB):
                s0 = blk * SUB
                bI, qI = b[s0:s0 + SUB], qs[s0:s0 + SUB]
                if blk == 0:
                    oI = jnp.zeros((SUB, LANE), F32)
                else:
                    ref = bs[s0 - 1:s0, :]
                    qt = (qI * jnp.exp(bI - ref)).astype(BF16)
                    kt = (kk[0:s0] * jnp.exp(ref - b[0:s0])).astype(BF16)
                    oI = _dot(_dot(qt, kt, NT).astype(BF16), ic_b[0:s0], NN)
                    yield
                acc = [oI[g * ROWS:(g + 1) * ROWS] for g in range(SUB // ROWS)]
                for s in range(SUB):
                    sr = s0 + s
                    g0 = s // ROWS
                    lo = g0 * ROWS
                    e = jnp.exp(jnp.minimum(bI[lo:] - bs[sr:sr + 1, :], 0.0))
                    a = jnp.sum(qI[lo:] * kks[sr:sr + 1, :] * e, axis=-1, keepdims=True)
                    add = jnp.where(rowi[lo:] >= s, a, 0.0) * ics[sr:sr + 1, :]
                    for g in range(g0, SUB // ROWS):
                        acc[g] = acc[g] + add[(g - g0) * ROWS:(g - g0 + 1) * ROWS]
                    yield
                pieces.extend(acc)
            o = o + jnp.concatenate(pieces, axis=0)
            bl = bs[CHUNK - 1:CHUNK, :]
            kd = (kk * jnp.exp(bl - b)).astype(BF16)
            st[...] = stv * jnp.exp(bl) + _dot(ic_b, kd, TN)
            yield
            o_ref[rows, ls] = o
            r = lax.rsqrt(jnp.mean(o * o, axis=-1, keepdims=True) + EPS)
            gb = g_ref[rows, ls]
            mix_ref[rows, ls] = (o * r * gn_ref[...] * (gb * _sigmoid(gb))).astype(BF16)

        def chunk(n, carry):
            _alternate([one_head(h, n) for h in range(HG)])
            return carry

        lax.fori_loop(0, NC, chunk, 0)

    tile = pltpu.VMEM((HG, CHUNK, LANE), F32)
    return pl.pallas_call(
        body, grid=(NG,), name="hgrn_fwd",
        in_specs=[blk_in(0), blk_in(NG), blk_in(2 * NG), blk_in(3 * NG), pl.BlockSpec((1, W), lambda g: (0, g)),
                  pl.BlockSpec((1, LANE), lambda g: (0, 0))],
        out_specs=[col, col, pl.BlockSpec((HG, NC, LANE, LANE), lambda g: (g, 0, 0, 0))],
        out_shape=[_sds((T, RW), BF16), _sds((T, RW), F32), _sds((RH, NC, LANE, LANE), F32)],
        scratch_shapes=[pltpu.VMEM((HG, LANE, LANE), F32), tile, tile, tile],
        compiler_params=_params(("parallel",), big=True),
    )(proj, proj, proj, proj, lb, gn)


def _hgrn_bwd(proj, dmixin, o_b, st_all, lb, gn, AW, RW):
    T = proj.shape[0]
    RH, NC, NSUB = RW // LANE, T // CHUNK, CHUNK // SUB
    HG, W, NG, blk_in, col = _hgrn_specs(T, RW, AW)

    def body(q_ref, f_ref, i_ref, g_ref, o_ref, dn_ref, stall_ref, lb_ref, gn_ref,
             dq_ref, df_ref, di_ref, dg_ref, dlb_ref, dgn_ref, dst_all, bs_all, kks_all, ics_all, p2_all, dic_all):
        dst_all[...] = jnp.zeros_like(dst_all)
        dlb_ref[...] = jnp.zeros_like(dlb_ref)
        dgn_ref[...] = jnp.zeros_like(dgn_ref)
        ltri = _ltri()
        rowi = lax.broadcasted_iota(jnp.int32, (SUB, 1), 0)
        last = lax.broadcasted_iota(jnp.int32, (CHUNK, 1), 0) == CHUNK - 1

        def one_head(h, n):
            ls = slice(h * LANE, (h + 1) * LANE)
            dst, bs, kks, ics = dst_all.at[h], bs_all.at[h], kks_all.at[h], ics_all.at[h]
            p2, dic = p2_all.at[h], dic_all.at[h]
            rows, lbv, qb, sg, f, kk, sq, qs, b = _hgrn_gates(n, ls, q_ref, f_ref, lb_ref, ltri)
            ic = i_ref[rows, ls]
            stv = stall_ref[h, n]
            dstv = dst[...]
            o = o_ref[rows, ls]
            dn = dn_ref[rows, ls]
            gb = g_ref[rows, ls]
            sgb = _sigmoid(gb)
            r = lax.rsqrt(jnp.mean(o * o, axis=-1, keepdims=True) + EPS)
            gnv = gn_ref[...]
            dg_ref[rows, ls] = (dn * (o * r * gnv) * (sgb * (1.0 + gb * (1.0 - sgb)))).astype(BF16)
            dy = dn * (gb * sgb)
            dgn_ref[h] += _colsum(dy * o * r)
            a_ = dy * gnv
            do = r * (a_ - o * (r * r) * jnp.mean(a_ * o, axis=-1, keepdims=True))
            do_b = do.astype(BF16)
            bs[...] = b
            kks[...] = kk
            ics[...] = ic
            yield
            ic_b = ic.astype(BF16)
            eb = jnp.exp(b)
            bl = bs[CHUNK - 1:CHUNK, :]
            ebl = jnp.exp(bl)
            dec = jnp.exp(bl - b)
            kd = (kk * dec).astype(BF16)
            dst_b = dstv.astype(BF16)
            dqs = _dot(do_b, stv.astype(BF16), NN) * eb
            dkk2 = _dot(ic_b, dst_b, NN) * dec
            dic[...] = _dot(kd, dst_b, NT)
            dbl = ebl * _colsum(stv * dstv) + _colsum(kk * dkk2)
            dst[...] = dstv * ebl + _dot(do_b, (qs * eb).astype(BF16), TN)
            yield
            p2[...] = jnp.zeros_like(p2)
            p1_pieces = []
            for blk in range(NSUB):
                s0 = blk * SUB
                bI, qI, doI = b[s0:s0 + SUB], qs[s0:s0 + SUB], do[s0:s0 + SUB]
                if blk == 0:
                    p1 = jnp.zeros((SUB, LANE), F32)
                else:
                    ref = bs[s0 - 1:s0, :]
                    eq = jnp.exp(bI - ref)
                    ek = jnp.exp(ref - b[0:s0])
                    qt = (qI * eq).astype(BF16)
                    kt = (kk[0:s0] * ek).astype(BF16)
                    doI_b = doI.astype(BF16)
                    dic[0:s0, :] += _dot(_dot(qt, kt, NT).astype(BF16), doI_b, TN)
                    da = _dot(doI_b, ic_b[0:s0], NT).astype(BF16)
                    p1 = _dot(da, kt, NN) * eq
                    p2[0:s0, :] += _dot(da, qt, TN) * ek
                    yield
                acc = [p1[g * ROWS:(g + 1) * ROWS] for g in range(SUB // ROWS)]
                for s in range(SUB):
                    sr = s0 + s
                    g0 = s // ROWS
                    lo = g0 * ROWS
                    keep = rowi[lo:] >= s
                    kk_s = kks[sr:sr + 1, :]
                    e = jnp.exp(jnp.minimum(bI[lo:] - bs[sr:sr + 1, :], 0.0))
                    w = qI[lo:] * e
                    a = jnp.where(keep, jnp.sum(w * kk_s, axis=-1, keepdims=True), 0.0)
                    da_s = jnp.where(keep, jnp.sum(doI[lo:] * ics[sr:sr + 1, :], axis=-1, keepdims=True), 0.0)
                    add = da_s * kk_s * e
                    for g in range(g0, SUB // ROWS):
                        acc[g] = acc[g] + add[(g - g0) * ROWS:(g - g0 + 1) * ROWS]
                    p2[sr:sr + 1, :] += _colsum(da_s * w)
                    dic[sr:sr + 1, :] += _colsum(a * doI[lo:])
                    yield
                p1_pieces.extend(acc)
            dqs = dqs + jnp.concatenate(p1_pieces, axis=0)
            dkk = dkk2 + p2[...]
            db = qs * dqs - kk * dkk + jnp.where(last, dbl, 0.0)
            dgl = _tri_dot(ltri, db, TN)
            yield
            dfv = dgl / f - dkk
            df_ref[rows, ls] = (dfv * (1.0 - lbv) * sg * (1.0 - sg)).astype(BF16)
            dlb_ref[:, ls] += _colsum(dfv * (1.0 - sg))
            dq_ref[rows, ls] = (dqs * (sq * (1.0 + qb * (1.0 - sq)))).astype(BF16)
            di_ref[rows, ls] = dic[...].astype(BF16)

        def chunk(k, carry):
            _alternate([one_head(h, NC - 1 - k) for h in range(HG)])
            return carry

        lax.fori_loop(0, NC, chunk, 0)

    tile = pltpu.VMEM((HG, CHUNK, LANE), F32)
    return pl.pallas_call(
        body, grid=(NG,), name="hgrn_bwd",
        in_specs=[blk_in(0), blk_in(NG), blk_in(2 * NG), blk_in(3 * NG), col,
                  pl.BlockSpec((T, W), lambda g: (0, AW // W + g)),
                  pl.BlockSpec((HG, NC, LANE, LANE), lambda g: (g, 0, 0, 0)),
                  pl.BlockSpec((1, W), lambda g: (0, g)), pl.BlockSpec((1, LANE), lambda g: (0, 0))],
        out_specs=[col, col, col, col, pl.BlockSpec((1, W), lambda g: (0, g)),
                   pl.BlockSpec((HG, 1, LANE), lambda g: (g, 0, 0))],
        out_shape=[_sds((T, RW), BF16)] * 4 + [_sds((1, RW), F32), _sds((RH, 1, LANE), F32)],
        scratch_shapes=[pltpu.VMEM((HG, LANE, LANE), F32), tile, tile, tile, tile, tile],
        compiler_params=_params(("parallel",), big=True),
    )(proj, proj, proj, proj, o_b, dmixin, st_all, lb, gn)


def _prep(c, lb_logits, rb_pad, max_rel):
    D, RW = c.shape[-1], lb_logits.shape[-1]
    H, rbp = rb_pad.shape

    def body(c_ref, l_ref, rb_ref, cact_ref, lb_ref, gv_ref):
        cv = c_ref[...]
        cact_ref[...] = cv * _sigmoid(cv)
        lb_ref[...] = _sigmoid(l_ref[0:1, :] - l_ref[1:2, :])
        gv_ref[...] = _dot(rb_ref[...], _bias_onehot(rbp, max_rel), NN, HIGHEST)

    return pl.pallas_call(
        body, name="prep", out_shape=[_sds((1, D), F32), _sds((1, RW), F32), _sds((H, TAB), F32)],
    )(c, lb_logits, rb_pad)


def _mod_part(c_all, w_ada_s, b_ada_s):
    B, D = c_all.shape
    ns = w_ada_s.shape[1]
    tn = _tile(ns, 768, LANE)

    def body(c_ref, w_ref, b_ref, o_ref):
        o_ref[...] = _dot(c_ref[...], w_ref[...], NN) + b_ref[...]

    return pl.pallas_call(
        body, grid=(ns // tn,), name="mod_part",
        in_specs=[pl.BlockSpec((B, D), lambda j: (0, 0)), pl.BlockSpec((D, tn), lambda j: (0, j)),
                  pl.BlockSpec((1, tn), lambda j: (0, j))],
        out_specs=pl.BlockSpec((B, tn), lambda j: (0, j)),
        out_shape=_sds((B, ns), F32), compiler_params=_params(("parallel",)),
    )(c_all, w_ada_s, b_ada_s)


def _adam(w, g, m, v):
    m = ADAM_B1 * m + (1.0 - ADAM_B1) * g
    v = ADAM_B2 * v + (1.0 - ADAM_B2) * (g * g)
    m_hat = m * (1.0 / (1.0 - ADAM_B1 ** ADAM_STEP))
    v_hat = v * (1.0 / (1.0 - ADAM_B2 ** ADAM_STEP))
    return -ADAM_LR * (m_hat / (jnp.sqrt(v_hat) + ADAM_EPS) + ADAM_WD * w), m, v


def _adam_ada(c_all, dmod_s, w, m, v):
    B, D = c_all.shape
    ns = w.shape[1]
    tr, tn = _tile(D, 512, LANE), _tile(ns, 768, LANE)

    def body(c_ref, d_ref, w_ref, m_ref, v_ref, g_out, dw_out, m_out, v_out):
        g = _dot(c_ref[...], d_ref[...], TN)
        g_out[...] = g
        dw_out[...], m_out[...], v_out[...] = _adam(w_ref[...], g, m_ref[...], v_ref[...])

    big = pl.BlockSpec((tr, tn), lambda i, j: (i, j))
    return pl.pallas_call(
        body, grid=(D // tr, ns // tn), name="adam_w_ada",
        in_specs=[pl.BlockSpec((B, tr), lambda i, j: (0, i)), pl.BlockSpec((B, tn), lambda i, j: (0, j)),
                  big, big, big],
        out_specs=[big] * 4, out_shape=[_sds((D, ns), F32)] * 4,
        compiler_params=_params(("parallel", "parallel")),
    )(c_all, dmod_s, w, m, v)


def _adam_shard(parts, w, m, v, name):
    R, C = w.shape
    tr = _tile(R, 256, 16)

    def body(p_ref, w_ref, m_ref, v_ref, g_out, dw_out, m_out, v_out):
        g = p_ref[0].astype(F32)
        for k in range(1, N_DEV // 2):
            g = g + p_ref[k].astype(F32)
        g_out[...] = g
        dw_out[...], m_out[...], v_out[...] = _adam(w_ref[...], g, m_ref[...], v_ref[...])

    big = pl.BlockSpec((tr, C), lambda i: (i, 0))
    return pl.pallas_call(
        body, grid=(R // tr,), name=name,
        in_specs=[pl.BlockSpec((N_DEV // 2, tr, C), lambda i: (0, i, 0)), big, big, big],
        out_specs=[big] * 4, out_shape=[_sds((R, C), F32)] * 4,
        compiler_params=_params(("parallel",), big=True),
    )(parts, w, m, v)


def _pair_sum(g8, land, core, name):
    _, NCHIP, R, C = g8.shape
    tr = _tile(R, 1024, 16)

    def body(core_ref, g_ref, l_ref, o_ref):
        o_ref[...] = g_ref[...] + l_ref[...]

    return pl.pallas_call(
        body, name=name,
        grid_spec=pltpu.PrefetchScalarGridSpec(
            num_scalar_prefetch=1, grid=(NCHIP, R // tr),
            in_specs=[pl.BlockSpec((None, None, tr, C), lambda k, i, core_ref: (core_ref[0], k, i, 0)),
                      pl.BlockSpec((None, tr, C), lambda k, i, core_ref: (k, i, 0))],
            out_specs=pl.BlockSpec((None, tr, C), lambda k, i, core_ref: (k, i, 0))),
        out_shape=_sds((NCHIP, R, C), BF16), compiler_params=_params(("parallel", "parallel")),
    )(core, g8, land)


SMALL = ("b_ada", "rel_bias", "attn_norm_g", "lb_logits", "gnorm_g", "ln1_g", "ln1_b", "ln2_g", "ln2_b")


def _small_update(parts, loss_parts, lbv, ws, ms, vs, max_rel):
    n = len(SMALL)

    def body(*refs):
        part_refs = dict(zip(SMALL, refs[:n]))
        loss_in, lb_ref = refs[n], refs[n + 1]
        w_refs, m_refs, v_refs = refs[n + 2:2 * n + 2], refs[2 * n + 2:3 * n + 2], refs[3 * n + 2:4 * n + 2]
        outs = refs[4 * n + 2:]

        def total(ref):
            tot = ref[0]
            for k in range(1, N_DEV):
                tot = tot + ref[k]
            return tot

        outs[0][...] = jnp.sum(total(loss_in), axis=-1, keepdims=True)
        for idx, name in enumerate(SMALL):
            g = total(part_refs[name])
            if name == "rel_bias":
                g = _dot(g, _bias_onehot(w_refs[idx].shape[1], max_rel), NT, HIGHEST)
            elif name == "lb_logits":
                lb = lb_ref[...]
                sign = (1 - 2 * lax.broadcasted_iota(jnp.int32, (2, 1), 0)).astype(F32)
                g = sign * (g * lb * (1.0 - lb))
            elif name == "gnorm_g":
                g = _colsum(g)
            dw, mm, vv = _adam(w_refs[idx][...], g, m_refs[idx][...], v_refs[idx][...])
            outs[1 + 4 * idx][...] = g
            outs[2 + 4 * idx][...] = dw
            outs[3 + 4 * idx][...] = mm
            outs[4 + 4 * idx][...] = vv

    out_shape = [_sds((1, 1), F32)]
    for w in ws:
        out_shape += [_sds(w.shape, F32)] * 4
    return pl.pallas_call(body, name="small_update", out_shape=out_shape, compiler_params=_params(big=True))(
        *[parts[k] for k in SMALL], loss_parts, lbv, *ws, *ms, *vs)


def _place():
    x, y, c = lax.axis_index("x"), lax.axis_index("y"), lax.axis_index("c")
    return x, y, c, [(1 - x, y), (x, 1 - y), (1 - x, 1 - y)]


def _all_gather(shard, name):
    HBM = pl.BlockSpec(memory_space=pl.ANY)

    def body(x_ref, out_ref, send_sems, recv_sems, local_sem):
        x, y, c, chips = _place()
        me, sibling = (x, y, c), (x, y, 1 - c)

        def slot(px, py, pc):
            return out_ref.at[4 * px + 2 * py + pc]

        def copy(k, block, to, src=None):
            return pltpu.make_async_remote_copy(
                src_ref=slot(*block) if src is None else src, dst_ref=slot(*block),
                send_sem=send_sems.at[k], recv_sem=recv_sems.at[k], device_id=to, device_id_type=MESH)

        mine = pltpu.make_async_copy(x_ref, slot(*me), local_sem)
        mine.start()
        first = [copy(0, me, sibling, src=x_ref)]
        first += [copy(1 + j, me, (*chip, c), src=x_ref) for j, chip in enumerate(chips)]
        for cp in first:
            cp.start()
        passed = [copy(4 + j, (*chip, c), sibling) for j, chip in enumerate(chips)]
        for j, chip in enumerate(chips):
            copy(1 + j, (*chip, c), me).wait_recv()
            passed[j].start()
        copy(0, sibling, me).wait_recv()
        for j, chip in enumerate(chips):
            copy(4 + j, (*chip, 1 - c), me).wait_recv()
        for cp in first + passed:
            cp.wait_send()
        mine.wait()

    return pl.pallas_call(
        body, name=name, out_shape=_sds((N_DEV,) + shard.shape, shard.dtype),
        in_specs=[HBM], out_specs=HBM,
        scratch_shapes=[pltpu.SemaphoreType.DMA((7,)), pltpu.SemaphoreType.DMA((7,)), pltpu.SemaphoreType.DMA(())],
    )(shard)


SEM_SPEC = pl.BlockSpec(memory_space=pltpu.SEMAPHORE)
HBM_SPEC = pl.BlockSpec(memory_space=pltpu.HBM)
EFFECT = pltpu.SideEffectType.DATAFLOW_SIDE_EFFECTING


def _remote(src, dst, send_sems, recv_sems, k, dev):
    return pltpu.make_async_remote_copy(src_ref=src, dst_ref=dst, send_sem=send_sems.at[k], recv_sem=recv_sems.at[k],
                                        device_id=dev, device_id_type=MESH)


def _copy_start(name, bufs, plan, n, after, only=None):
    nb = len(bufs)

    def body(*refs):
        send_sems, recv_sems = refs[nb + 1], refs[nb + 2]
        for k, (src, dst, dev) in enumerate(plan(*refs[:nb])):
            if only is not None and k not in only:
                continue
            _remote(src, dst, send_sems, recv_sems, k, dev).start()
        refs[-1][...] = jnp.zeros_like(refs[-1])

    out = pl.pallas_call(
        body, name=name,
        out_shape=(pltpu.SemaphoreType.DMA((n,)), pltpu.SemaphoreType.DMA((n,)),
                   *[pltpu.HBM(b.shape, b.dtype) for b in bufs], _sds((8, LANE), F32)),
        in_specs=[HBM_SPEC] * nb + [ORDER_ONLY],
        out_specs=(SEM_SPEC, SEM_SPEC, *[HBM_SPEC] * nb, pl.BlockSpec(memory_space=pltpu.VMEM)),
        input_output_aliases={i: 2 + i for i in range(nb)},
        compiler_params=pltpu.CompilerParams(has_side_effects=EFFECT),
    )(*[pltpu.with_memory_space_constraint(b, pltpu.HBM) for b in bufs], after)
    return (out[0], out[1]), list(out[2:2 + nb]), out[-1]


def _copy_wait(name, sems, bufs, plan, after, only=None):
    nb = len(bufs)

    def body(*refs):
        send_sems, recv_sems = refs[nb], refs[nb + 1]
        for k, (src, dst, dev) in enumerate(plan(*refs[:nb])):
            if only is not None and k not in only:
                continue
            cp = _remote(src, dst, send_sems, recv_sems, k, dev)
            cp.wait_send()
            cp.wait_recv()

    out = pl.pallas_call(
        body, name=name, out_shape=tuple(pltpu.HBM(b.shape, b.dtype) for b in bufs),
        in_specs=[HBM_SPEC] * nb + [SEM_SPEC, SEM_SPEC, pl.BlockSpec(memory_space=pl.ANY)],
        out_specs=tuple([HBM_SPEC] * nb), input_output_aliases={i: i for i in range(nb)},
        compiler_params=pltpu.CompilerParams(has_side_effects=EFFECT),
    )(*bufs, sems[0], sems[1], after)
    return list(out)


def _ag_plan_chips(shard_ref, out_ref):
    x, y, c, chips = _place()
    mine = out_ref.at[4 * x + 2 * y + c]
    return [(shard_ref, mine, (x, y, 1 - c))] + [(shard_ref, mine, (*chip, c)) for chip in chips]


def _ag_plan_pass(out_ref):
    x, y, c, chips = _place()
    slots = [out_ref.at[4 * chip[0] + 2 * chip[1] + c] for chip in chips]
    return [(s, s, (x, y, 1 - c)) for s in slots]


def _rs_plan_pair(g_ref, land_ref):
    x, y, c, _ = _place()
    return [(g_ref.at[1 - c], land_ref, (x, y, 1 - c))]


def _rs_plan_chips(p_ref, land_ref):
    x, y, c, chips = _place()
    return [(p_ref.at[2 * chip[0] + chip[1]], land_ref.at[2 * x + y], (*chip, c)) for chip in chips]


class _Gather:
    def __init__(self, shard, me, tag, after):
        self.tag = tag
        out = lax.dynamic_update_slice(lax.empty((N_DEV,) + shard.shape, shard.dtype), shard[None],
                                       (me,) + (0,) * shard.ndim)
        self.sems, (self.shard, self.out), self.token = _copy_start(
            "ag_start_" + tag, [shard, out], _ag_plan_chips, 4, after)
        self.groups = []

    def arrived(self, after, copies):
        name = "ag_wait_%s_%s" % (self.tag, "".join(map(str, copies)))
        self.shard, self.out = _copy_wait(name, self.sems, [self.shard, self.out], _ag_plan_chips, after, copies)
        return self.out

    def pass_on(self, after, blocks):
        name = "ag_pass_%s_%s" % (self.tag, "".join(map(str, blocks)))
        sems, (self.out,), _ = _copy_start(name, [self.out], _ag_plan_pass, 3, after, blocks)
        self.groups.append((sems, blocks))
        return self.out

    def passed(self, after, group):
        sems, blocks = self.groups[group]
        name = "ag_pass_wait_%s_%s" % (self.tag, "".join(map(str, blocks)))
        self.out = _copy_wait(name, sems, [self.out], _ag_plan_pass, after, blocks)[0]
        return self.out

    def arrived_from_chips(self, after):
        self.arrived(after, (0, 1, 2, 3))
        return self.pass_on(after, (0, 1, 2))

    def passed_on(self, after):
        return self.passed(after, 0)


class _ReduceScatter:
    def __init__(self, g8, tag):
        self.tag = tag
        land = lax.empty(g8.shape[1:], g8.dtype)
        self.sems, self.bufs, self.token = _copy_start(
            "rs_pair_start_" + tag, [g8, land], _rs_plan_pair, 1, jnp.zeros((1,), F32))

    def pair_done(self, core, chip, after, start_after=None):
        g8, land = _copy_wait("rs_pair_wait_" + self.tag, self.sems, self.bufs, _rs_plan_pair, after)
        p4 = _pair_sum(g8, land, core, "rs_pair_sum_" + self.tag)
        own = lax.dynamic_slice_in_dim(p4, chip, 1, axis=0)
        land2 = lax.dynamic_update_slice(lax.empty(p4.shape, p4.dtype), own, (chip, 0, 0))
        self.sems, self.bufs, self.token = _copy_start(
            "rs_chips_start_" + self.tag, [p4, land2], _rs_plan_chips, 3,
            jnp.zeros((1,), F32) if start_after is None else start_after)

    def sums(self, after):
        return _copy_wait("rs_chips_wait_" + self.tag, self.sems, self.bufs, _rs_plan_chips, after)[1]


BIG = ("w_in", "w_o", "w_ffn_in", "w_ffn_out")
ORDER = ("w_ada", "b_ada", "w_in", "rel_bias", "attn_norm_g", "lb_logits", "gnorm_g", "w_o", "ln1_g", "ln1_b",
         "w_ffn_in", "w_ffn_out", "ln2_g", "ln2_b")


def kernel(x, c, w_ada, b_ada, w_in, rel_bias, attn_norm_g, lb_logits, gnorm_g, w_o, ln1_g, ln1_b, w_ffn_in, w_ffn_out, ln2_g, ln2_b, loss_target, m_w_ada, m_b_ada, m_w_in, m_rel_bias, m_attn_norm_g, m_lb_logits, m_gnorm_g, m_w_o, m_ln1_g, m_ln1_b, m_w_ffn_in, m_w_ffn_out, m_ln2_g, m_ln2_b, v_w_ada, v_b_ada, v_w_in, v_rel_bias, v_attn_norm_g, v_lb_logits, v_gnorm_g, v_w_o, v_ln1_g, v_ln1_b, v_w_ffn_in, v_w_ffn_out, v_ln2_g, v_ln2_b):
    W = dict(w_ada=w_ada, b_ada=b_ada, w_in=w_in, rel_bias=rel_bias, attn_norm_g=attn_norm_g, lb_logits=lb_logits,
             gnorm_g=gnorm_g, w_o=w_o, ln1_g=ln1_g, ln1_b=ln1_b, w_ffn_in=w_ffn_in, w_ffn_out=w_ffn_out,
             ln2_g=ln2_g, ln2_b=ln2_b)
    M = dict(w_ada=m_w_ada, b_ada=m_b_ada, w_in=m_w_in, rel_bias=m_rel_bias, attn_norm_g=m_attn_norm_g,
             lb_logits=m_lb_logits, gnorm_g=m_gnorm_g, w_o=m_w_o, ln1_g=m_ln1_g, ln1_b=m_ln1_b,
             w_ffn_in=m_w_ffn_in, w_ffn_out=m_w_ffn_out, ln2_g=m_ln2_g, ln2_b=m_ln2_b)
    V = dict(w_ada=v_w_ada, b_ada=v_b_ada, w_in=v_w_in, rel_bias=v_rel_bias, attn_norm_g=v_attn_norm_g,
             lb_logits=v_lb_logits, gnorm_g=v_gnorm_g, w_o=v_w_o, ln1_g=v_ln1_g, ln1_b=v_ln1_b,
             w_ffn_in=v_w_ffn_in, w_ffn_out=v_w_ffn_out, ln2_g=v_ln2_g, ln2_b=v_ln2_b)

    x2, tgt = x[0], loss_target[0]
    T, D = x2.shape
    AW, RW = attn_norm_g.shape[-1], lb_logits.shape[-1]
    MIX = AW + RW
    H, RH = AW // ATTN_HEAD_DIM, RW // LANE
    RB = rel_bias.shape[-1]
    max_rel = (RB - 1) // 2
    rbp = -(-RB // LANE) * LANE
    F = w_ffn_out.shape[1] * N_DEV
    half = N_DEV // 2
    xi, yi, ci = lax.axis_index("x"), lax.axis_index("y"), lax.axis_index("c")
    me = 4 * xi + 2 * yi + ci
    core = jnp.reshape(ci, (1,)).astype(jnp.int32)
    pad_rb = lambda a: jnp.pad(a[0], ((0, 0), (0, rbp - RB)))

    chip = 2 * xi + yi

    c_act, lbv, gv = _prep(c, lb_logits, pad_rb(rel_bias), max_rel)
    c_all = _all_gather(c_act, "ag_c").reshape(N_DEV, D)
    ns_ada = w_ada.shape[-1]
    mod_part = _mod_part(c_all, w_ada[0], lax.dynamic_slice_in_dim(b_ada, me * ns_ada, ns_ada, axis=1))
    mod_all = _all_gather(mod_part, "ag_mod")
    mod6 = lax.dynamic_index_in_dim(mod_all, me, axis=1, keepdims=False).reshape(6, D)

    ag_in = _Gather(w_in[0].astype(BF16), me, "w_in", mod_all)
    ag_o = _Gather(w_o[0].astype(BF16), me, "w_o", ag_in.token)
    ag_f1 = _Gather(w_ffn_in[0].astype(BF16), me, "w_ffn_in", ag_o.token)
    ag_f2 = _Gather(w_ffn_out[0].astype(BF16), me, "w_ffn_out", ag_f1.token)

    h1 = _ln_mod(x2, mod6 + ag_f2.token[0, 0])
    ids = lambda pairs: jnp.stack([4 * px + 2 * py + pc for px, py, pc in pairs]).astype(jnp.int32)
    near, far = [(1 - xi, yi), (xi, 1 - yi)], [(1 - xi, 1 - yi)]
    proj = lax.empty((T, w_in.shape[-1] * N_DEV), F32)
    proj = _mm_gathered(h1, ag_in.arrived(h1, (0,)), ids([(xi, yi, ci), (xi, yi, 1 - ci)]), proj, "in_proj_a")
    ag_in.arrived(proj, (1, 2))
    proj = _mm_gathered(h1, ag_in.pass_on(proj, (0, 1)), ids([(*ch, ci) for ch in near]), proj, "in_proj_b")
    proj = _mm_gathered(h1, ag_in.passed(proj, 0), ids([(*ch, 1 - ci) for ch in near]), proj, "in_proj_c")
    ag_in.arrived(proj, (3,))
    proj = _mm_gathered(h1, ag_in.pass_on(proj, (2,)), ids([(*ch, ci) for ch in far]), proj, "in_proj_d")
    wg_in = ag_in.passed(proj, 1)
    proj = _mm_gathered(h1, wg_in, ids([(*ch, 1 - ci) for ch in far]), proj, "in_proj_e")
    ag_o.arrived_from_chips(proj)
    mix_a = _attn_fwd(proj, gv, attn_norm_g, AW)
    wg_o = ag_o.passed_on(mix_a).reshape(MIX, D)
    mix_b, o_b, st_all = _hgrn_fwd(proj, lbv, gnorm_g, AW, RW)
    mixin = jnp.concatenate([mix_a, mix_b], axis=1)
    mix = _mm_nn(mixin, wg_o, "out_proj")
    ag_f1.arrived_from_chips(mix)
    x1, h2 = _mid_fwd(x2, mix, mod6, ln1_g, ln1_b)
    wg_f1 = ag_f1.passed_on(h2)
    gu, act = _mm_swiglu(h2, wg_f1)
    ag_f2.arrived_from_chips(act)
    wg_f2 = ag_f2.passed_on(act).reshape(F, D)
    ff = _mm_nn(act, wg_f2, "ffn_out")
    dff, dx1a, vec_a = _final(x1, ff, mod6, ln2_g, ln2_b, tgt)

    du = _mm_swiglu_bwd(dff, wg_f2, gu)
    rs_f2 = _ReduceScatter(_mm_tn_rows(dff, act, dff, F // N_DEV, "grad_w_ffn_out"), "w_ffn_out")
    tm = _tile(T, 512, 16)
    du_ij = lambda tm_, ns: pl.BlockSpec((None, tm_, ns), lambda i, j: (j // half, i, j % half))
    du_jm = lambda tm_, ns: pl.BlockSpec((None, tm_, ns), lambda j, m: (j // half, m, j % half))
    dh2 = _mm_gathered_nt(rs_f2.token, du, du_ij, wg_f1, T, tm, "ffn_in_bwd")
    rs_f2.pair_done(core, chip, dh2)
    tm_red = _tile(T, 1024, 16)
    gw_f1 = _mm_tn_gathered(rs_f2.token, h2, du, du_jm, wg_f1.shape[-1], tm_red, "grad_w_ffn_in")
    rs_f1 = _ReduceScatter(gw_f1.reshape(2, half, D, -1), "w_ffn_in")
    dmix, dxa, vec_b = _mid_bwd(x2, mix, x1, dx1a, dh2, mod6 + rs_f1.token[0, 0], ln1_g)
    dmixin = _mm_nt(dmix, wg_o, "out_proj_bwd")
    rs_f1.pair_done(core, chip, dmixin)
    rs_o = _ReduceScatter(_mm_tn_rows(rs_f1.token, mixin, dmix, MIX // N_DEV, "grad_w_o"), "w_o")
    dq, dk, dv, dgv, dga = _attn_bwd(proj, dmixin, gv + rs_o.token[0, 0], attn_norm_g, AW)
    rs_o.pair_done(core, chip, dq)
    dqb, dfl, dib, dgb, dlb, dgn = _hgrn_bwd(proj, dmixin, o_b, st_all, lbv + rs_o.token[0, 0], gnorm_g, AW, RW)
    dproj = jnp.concatenate([dq, dk, dv, dqb, dfl, dib, dgb], axis=1)
    p_ij = lambda tm_, ns: pl.BlockSpec((tm_, ns), lambda i, j: (i, j))
    p_jm = lambda tm_, ns: pl.BlockSpec((tm_, ns), lambda j, m: (m, j))
    gw_in = _mm_tn_gathered(rs_o.token, h1, dproj, p_jm, wg_in.shape[-1], tm_red, "grad_w_in")
    rs_in = _ReduceScatter(gw_in.reshape(2, half, D, -1), "w_in")
    dh1 = _mm_gathered_nt(rs_in.token, dproj, p_ij, wg_in, T, tm, "in_proj_bwd")
    grad_x, vec_c = _first_bwd(x2, dh1, dxa, mod6)

    dmod = jnp.concatenate([vec_c[1:2], vec_c[0:1], vec_b[4:5], vec_b[1:2], vec_b[0:1], vec_a[2:3]], axis=0)
    pieces = dict(b_ada=dmod, rel_bias=dgv, attn_norm_g=dga, lb_logits=dlb, gnorm_g=dgn, ln1_g=vec_b[2:3],
                  ln1_b=vec_b[3:4], ln2_g=vec_a[0:1], ln2_b=vec_a[1:2], loss=vec_a[3:4])
    widths = dict(b_ada=(1, 6 * D), rel_bias=(H, TAB), attn_norm_g=(1, AW), lb_logits=(1, RW), gnorm_g=(RH, LANE),
                  ln1_g=(1, D), ln1_b=(1, D), ln2_g=(1, D), ln2_b=(1, D), loss=(1, D))
    packed = jnp.concatenate([pieces[k].reshape(-1, LANE) for k in widths], axis=0)
    gathered = _all_gather(packed, "ag_small")
    rs_in.pair_done(core, chip, dh1, start_after=gathered)
    parts, r0 = {}, 0
    for k, (rows, width) in widths.items():
        nr = rows * width // LANE
        parts[k] = gathered[:, r0:r0 + nr, :].reshape(N_DEV, rows, width)
        r0 += nr
    prep_small = lambda d, k: pad_rb(d[k]) if k == "rel_bias" else d[k]
    small = _small_update(parts, parts["loss"], lbv, [prep_small(W, k) for k in SMALL],
                          [prep_small(M, k) for k in SMALL], [prep_small(V, k) for k in SMALL], max_rel)
    loss = small[0].reshape(())
    res = {}
    for idx, k in enumerate(SMALL):
        four = small[1 + 4 * idx:5 + 4 * idx]
        if k == "rel_bias":
            four = [a[:, :RB][None] for a in four]
        res[k] = list(four)

    dmod_s = lax.dynamic_slice_in_dim(parts["b_ada"].reshape(N_DEV, 6 * D), me * ns_ada, ns_ada, axis=1)
    dmod_s = dmod_s + rs_in.token[0, 0]
    res["w_ada"] = [a[None] for a in _adam_ada(c_all, dmod_s, w_ada[0], m_w_ada[0], v_w_ada[0])]
    after = res["w_ada"][0]
    for k, rs in (("w_ffn_out", rs_f2), ("w_ffn_in", rs_f1), ("w_o", rs_o), ("w_in", rs_in)):
        four = _adam_shard(rs.sums(after), W[k][0], M[k][0], V[k][0], "adam_" + k)
        res[k] = [a[None] for a in four]
        after = four[0]

    out = [loss, grad_x[None]]
    for field in range(4):
        out += [res[k][field] for k in ORDER]
    return tuple(out)
```

```python
import functools

import jax
import jax.numpy as jnp
from jax import lax
from jax.experimental import pallas as pl
from jax.experimental.pallas import tpu as pltpu

F32 = jnp.float32
BF16 = jnp.bfloat16
MESH = pl.DeviceIdType.MESH
HIGHEST = lax.Precision.HIGHEST

N_DEV = 8
CHUNK = 64
N_PAST = 8
QBLK = 4 * CHUNK
KPAD = N_PAST * CHUNK
WIN = KPAD + QBLK
TAB = 1024
ATTN_HEAD_DIM = 64
ATTN_HEADS_PER_STEP = 4
REC_HEAD_DIM = 128
SUB = 16
ROWS = 8
LANE = 128
EPS = 1e-5
ALPHA = 2.0 ** 0.25
ADAM_LR, ADAM_B1, ADAM_B2, ADAM_EPS, ADAM_WD, ADAM_STEP = 0.001, 0.9, 0.999, 1e-08, 0.01, 10
NEG = -1e30
VMEM_LIMIT = 56 * 1024 * 1024


def _sds(shape, dtype):
    return jax.ShapeDtypeStruct(tuple(shape), dtype)


def _tile(n, pref, mult):
    best = None
    for t in range(mult, min(n, pref) + 1, mult):
        if n % t == 0:
            best = t
    return n if best is None else best


def _params(sem=None, big=False):
    kw = {}
    if sem is not None:
        kw["dimension_semantics"] = sem
    if big:
        kw["vmem_limit_bytes"] = VMEM_LIMIT
    return pltpu.CompilerParams(**kw)


def _sigmoid(v):
    return 1.0 / (1.0 + jnp.exp(-v))


def _dot(a, b, dims, precision=None):
    return lax.dot_general(a, b, (dims, ((), ())), preferred_element_type=F32, precision=precision)


NN = ((1,), (0,))
NT = ((1,), (1,))
TN = ((0,), (0,))


def _ln(v):
    mu = jnp.mean(v, axis=-1, keepdims=True)
    d = v - mu
    rstd = lax.rsqrt(jnp.mean(d * d, axis=-1, keepdims=True) + EPS)
    return d * rstd, rstd


def _ln_bwd(dxh, xh, rstd):
    return rstd * (dxh - jnp.mean(dxh, axis=-1, keepdims=True) - xh * jnp.mean(dxh * xh, axis=-1, keepdims=True))


def _colsum(v):
    return jnp.sum(v, axis=0, keepdims=True)


def _ln_mod(x2, mod6):
    T, D = x2.shape
    tm = _tile(T, 256, 8)

    def body(x_ref, mod_ref, o_ref):
        xh, _ = _ln(x_ref[...])
        o_ref[...] = (xh * (1.0 + mod_ref[1:2, :]) + mod_ref[0:1, :]).astype(BF16)

    return pl.pallas_call(
        body, grid=(T // tm,), name="ln_mod",
        in_specs=[pl.BlockSpec((tm, D), lambda i: (i, 0)), pl.BlockSpec((6, D), lambda i: (0, 0))],
        out_specs=pl.BlockSpec((tm, D), lambda i: (i, 0)),
        out_shape=_sds((T, D), BF16), compiler_params=_params(("parallel",)),
    )(x2, mod6)


def _mid_fwd(x2, mix, mod6, ln1_g, ln1_b):
    T, D = x2.shape
    tm = _tile(T, 256, 8)

    def body(x_ref, mix_ref, mod_ref, g_ref, b_ref, x1_ref, h2_ref):
        zh, _ = _ln(ALPHA * x_ref[...] + mod_ref[2:3, :] * mix_ref[...])
        x1 = zh * g_ref[...] + b_ref[...]
        x1_ref[...] = x1
        xh, _ = _ln(x1)
        h2_ref[...] = (xh * (1.0 + mod_ref[4:5, :]) + mod_ref[3:4, :]).astype(BF16)

    row = pl.BlockSpec((tm, D), lambda i: (i, 0))
    vec = pl.BlockSpec((1, D), lambda i: (0, 0))
    return pl.pallas_call(
        body, grid=(T // tm,), name="mid_fwd",
        in_specs=[row, row, pl.BlockSpec((6, D), lambda i: (0, 0)), vec, vec],
        out_specs=[row, row],
        out_shape=[_sds((T, D), F32), _sds((T, D), BF16)], compiler_params=_params(("parallel",)),
    )(x2, mix, mod6, ln1_g, ln1_b)


def _final(x1, ff, mod6, ln2_g, ln2_b, tgt):
    T, D = x1.shape
    tm = _tile(T, 256, 8)

    def body(x1_ref, ff_ref, mod_ref, g_ref, b_ref, t_ref, dff_ref, dx1_ref, vec_ref):
        @pl.when(pl.program_id(0) == 0)
        def _():
            vec_ref[...] = jnp.zeros_like(vec_ref)

        ff_v = ff_ref[...]
        gate2 = mod_ref[5:6, :]
        zh, rstd = _ln(ALPHA * x1_ref[...] + gate2 * ff_v)
        err = zh * g_ref[...] + b_ref[...] - t_ref[...]
        dy = err * (1.0 / D)
        dz = _ln_bwd(dy * g_ref[...], zh, rstd)
        dff_ref[...] = (gate2 * dz).astype(BF16)
        dx1_ref[...] = ALPHA * dz
        vec_ref[0:1, :] += _colsum(dy * zh)
        vec_ref[1:2, :] += _colsum(dy)
        vec_ref[2:3, :] += _colsum(dz * ff_v)
        vec_ref[3:4, :] += _colsum(err * err) * (0.5 / D)

    row = pl.BlockSpec((tm, D), lambda i: (i, 0))
    vec = pl.BlockSpec((1, D), lambda i: (0, 0))
    return pl.pallas_call(
        body, grid=(T // tm,), name="final_fwd_bwd",
        in_specs=[row, row, pl.BlockSpec((6, D), lambda i: (0, 0)), vec, vec, row],
        out_specs=[row, row, pl.BlockSpec((8, D), lambda i: (0, 0))],
        out_shape=[_sds((T, D), BF16), _sds((T, D), F32), _sds((8, D), F32)],
        compiler_params=_params(("arbitrary",)),
    )(x1, ff, mod6, ln2_g, ln2_b, tgt)


def _mid_bwd(x2, mix, x1, dx1a, dh2, mod6, ln1_g):
    T, D = x2.shape
    tm = _tile(T, 256, 8)

    def body(x_ref, mix_ref, x1_ref, dx1a_ref, dh2_ref, mod_ref, g_ref, dmix_ref, dxa_ref, vec_ref):
        @pl.when(pl.program_id(0) == 0)
        def _():
            vec_ref[...] = jnp.zeros_like(vec_ref)

        dh2 = dh2_ref[...]
        xh, rstd = _ln(x1_ref[...])
        dx1 = dx1a_ref[...] + _ln_bwd(dh2 * (1.0 + mod_ref[4:5, :]), xh, rstd)
        mix_v = mix_ref[...]
        gate1 = mod_ref[2:3, :]
        zh, rstdz = _ln(ALPHA * x_ref[...] + gate1 * mix_v)
        dz = _ln_bwd(dx1 * g_ref[...], zh, rstdz)
        dmix_ref[...] = (gate1 * dz).astype(BF16)
        dxa_ref[...] = ALPHA * dz
        vec_ref[0:1, :] += _colsum(dh2 * xh)
        vec_ref[1:2, :] += _colsum(dh2)
        vec_ref[2:3, :] += _colsum(dx1 * zh)
        vec_ref[3:4, :] += _colsum(dx1)
        vec_ref[4:5, :] += _colsum(dz * mix_v)

    row = pl.BlockSpec((tm, D), lambda i: (i, 0))
    vec = pl.BlockSpec((1, D), lambda i: (0, 0))
    return pl.pallas_call(
        body, grid=(T // tm,), name="mid_bwd",
        in_specs=[row, row, row, row, row, pl.BlockSpec((6, D), lambda i: (0, 0)), vec],
        out_specs=[row, row, pl.BlockSpec((8, D), lambda i: (0, 0))],
        out_shape=[_sds((T, D), BF16), _sds((T, D), F32), _sds((8, D), F32)],
        compiler_params=_params(("arbitrary",)),
    )(x2, mix, x1, dx1a, dh2, mod6, ln1_g)


def _first_bwd(x2, dh1, dxa, mod6):
    T, D = x2.shape
    tm = _tile(T, 256, 8)

    def body(x_ref, dh1_ref, dxa_ref, mod_ref, gx_ref, vec_ref):
        @pl.when(pl.program_id(0) == 0)
        def _():
            vec_ref[...] = jnp.zeros_like(vec_ref)

        dh1 = dh1_ref[...]
        xh, rstd = _ln(x_ref[...])
        gx_ref[...] = dxa_ref[...] + _ln_bwd(dh1 * (1.0 + mod_ref[1:2, :]), xh, rstd)
        vec_ref[0:1, :] += _colsum(dh1 * xh)
        vec_ref[1:2, :] += _colsum(dh1)

    row = pl.BlockSpec((tm, D), lambda i: (i, 0))
    return pl.pallas_call(
        body, grid=(T // tm,), name="first_bwd",
        in_specs=[row, row, row, pl.BlockSpec((6, D), lambda i: (0, 0))],
        out_specs=[row, pl.BlockSpec((8, D), lambda i: (0, 0))],
        out_shape=[_sds((T, D), F32), _sds((8, D), F32)],
        compiler_params=_params(("arbitrary",)),
    )(x2, dh1, dxa, mod6)


def _slot(j):
    return (j % 2) * 4 + j // 2


def _mm_gathered(a, wg, shards, out, name):
    M, K = a.shape
    _, _, ns = wg.shape
    tm = _tile(M, 512, 16)

    def body(shards_ref, a_ref, w_ref, prev_ref, o_ref):
        o_ref[...] = _dot(a_ref[...], w_ref[...], NN)

    return pl.pallas_call(
        body, name=name,
        grid_spec=pltpu.PrefetchScalarGridSpec(
            num_scalar_prefetch=1, grid=(shards.shape[0], M // tm),
            in_specs=[pl.BlockSpec((tm, K), lambda j, i, s: (i, 0)),
                      pl.BlockSpec((None, K, ns), lambda j, i, s: (s[j], 0, 0)), ORDER_ONLY],
            out_specs=pl.BlockSpec((tm, ns), lambda j, i, s: (i, s[j]))),
        out_shape=_sds((M, N_DEV * ns), F32), input_output_aliases={3: 0},
        compiler_params=_params(("parallel", "parallel"), big=True),
    )(shards, a, wg, out)


def _mm_nn(a, b, name):
    M, K = a.shape
    _, N = b.shape
    tm, tn, tk = _tile(M, 512, 16), _tile(N, 1024, LANE), _tile(K, 2048, LANE)

    def body(a_ref, b_ref, o_ref):
        @pl.when(pl.program_id(2) == 0)
        def _():
            o_ref[...] = jnp.zeros_like(o_ref)

        o_ref[...] += _dot(a_ref[...], b_ref[...], NN)

    return pl.pallas_call(
        body, grid=(M // tm, N // tn, K // tk), name=name,
        in_specs=[pl.BlockSpec((tm, tk), lambda i, j, k: (i, k)), pl.BlockSpec((tk, tn), lambda i, j, k: (k, j))],
        out_specs=pl.BlockSpec((tm, tn), lambda i, j, k: (i, j)),
        out_shape=_sds((M, N), F32), compiler_params=_params(("parallel", "parallel", "arbitrary"), big=True),
    )(a, b)


def _mm_nt(a, b, name):
    M, K = a.shape
    N, _ = b.shape
    tm, tn = _tile(M, 512, 16), _tile(N, 1024, LANE)

    def body(a_ref, b_ref, o_ref):
        o_ref[...] = _dot(a_ref[...], b_ref[...], NT)

    return pl.pallas_call(
        body, grid=(M // tm, N // tn), name=name,
        in_specs=[pl.BlockSpec((tm, K), lambda i, j: (i, 0)), pl.BlockSpec((tn, K), lambda i, j: (j, 0))],
        out_specs=pl.BlockSpec((tm, tn), lambda i, j: (i, j)),
        out_shape=_sds((M, N), F32), compiler_params=_params(("parallel", "parallel"), big=True),
    )(a, b)


def _mm_swiglu(h2, wg):
    M, K = h2.shape
    _, _, ns = wg.shape
    half = N_DEV // 2
    tm = _tile(M, 256, 16)

    def body(a_ref, wgate_ref, wup_ref, gu_ref, act_ref):
        a = a_ref[...]
        g = _dot(a, wgate_ref[...], NN)
        u = _dot(a, wup_ref[...], NN)
        gu_ref[0] = g
        gu_ref[1] = u
        act_ref[...] = (g * _sigmoid(g) * u).astype(BF16)

    return pl.pallas_call(
        body, grid=(half, M // tm), name="ffn_in_swiglu",
        in_specs=[pl.BlockSpec((tm, K), lambda j, i: (i, 0)),
                  pl.BlockSpec((None, K, ns), lambda j, i: (j, 0, 0)),
                  pl.BlockSpec((None, K, ns), lambda j, i: (j + half, 0, 0))],
        out_specs=[pl.BlockSpec((2, tm, ns), lambda j, i: (0, i, j)), pl.BlockSpec((tm, ns), lambda j, i: (i, j))],
        out_shape=[_sds((2, M, half * ns), F32), _sds((M, half * ns), BF16)],
        compiler_params=_params(("parallel", "parallel"), big=True),
    )(h2, wg, wg)


def _mm_swiglu_bwd(dff, w2, gu):
    M, K = dff.shape
    F = w2.shape[0]
    tm, tn = _tile(M, 512, 16), _tile(F, 1408, LANE)

    def body(a_ref, b_ref, gu_ref, du_ref):
        da = _dot(a_ref[...], b_ref[...], NT)
        g = gu_ref[0]
        u = gu_ref[1]
        sg = _sigmoid(g)
        du_ref[0] = (da * u * (sg * (1.0 + g * (1.0 - sg)))).astype(BF16)
        du_ref[1] = (da * (g * sg)).astype(BF16)

    return pl.pallas_call(
        body, grid=(F // tn, M // tm), name="ffn_out_bwd_swiglu",
        in_specs=[pl.BlockSpec((tm, K), lambda j, i: (i, 0)), pl.BlockSpec((tn, K), lambda j, i: (j, 0)),
                  pl.BlockSpec((2, tm, tn), lambda j, i: (0, i, j))],
        out_specs=pl.BlockSpec((2, tm, tn), lambda j, i: (0, i, j)),
        out_shape=_sds((2, M, F), BF16), compiler_params=_params(("parallel", "parallel"), big=True),
    )(dff, w2, gu)


ORDER_ONLY = pl.BlockSpec(memory_space=pl.ANY)


def _mm_tn_rows(dep, a, b, rs, name):
    M, Ka = a.shape
    _, N = b.shape
    tm = _tile(M, 1024, 16)

    def body(_, a_ref, b_ref, o_ref, acc_ref):
        m = pl.program_id(1)

        @pl.when(m == 0)
        def _():
            acc_ref[...] = jnp.zeros_like(acc_ref)

        acc_ref[...] += _dot(a_ref[...], b_ref[...], TN)

        @pl.when(m == pl.num_programs(1) - 1)
        def _():
            o_ref[0, 0] = acc_ref[0:rs, :].astype(BF16)
            o_ref[1, 0] = acc_ref[rs:2 * rs, :].astype(BF16)

    return pl.pallas_call(
        body, grid=(N_DEV // 2, M // tm), name=name,
        in_specs=[ORDER_ONLY, pl.BlockSpec((tm, 2 * rs), lambda ch, m: (m, ch)),
                  pl.BlockSpec((tm, N), lambda ch, m: (m, 0))],
        out_specs=pl.BlockSpec((2, 1, rs, N), lambda ch, m: (0, ch, 0, 0)),
        out_shape=_sds((2, N_DEV // 2, rs, N), BF16),
        scratch_shapes=[pltpu.VMEM((2 * rs, N), F32)],
        compiler_params=_params(("parallel", "arbitrary"), big=True),
    )(dep, a, b)


def _mm_gathered_nt(dep, a, a_spec, wg, M, tm, name, first=0, count=None, out=None):
    _, K, ns = wg.shape
    count = M // tm if count is None else count
    out = lax.empty((M, K), F32) if out is None else out

    def body(_, a_ref, w_ref, prev_ref, o_ref):
        @pl.when(pl.program_id(1) == 0)
        def _():
            o_ref[...] = jnp.zeros_like(o_ref)

        o_ref[...] += _dot(a_ref[...], w_ref[...], NT)

    return pl.pallas_call(
        body, grid=(count, N_DEV), name=name,
        in_specs=[ORDER_ONLY, a_spec(tm, ns, first), pl.BlockSpec((None, K, ns), lambda i, j: (j, 0, 0)), ORDER_ONLY],
        out_specs=pl.BlockSpec((tm, K), lambda i, j: (i + first, 0)),
        out_shape=_sds((M, K), F32), input_output_aliases={3: 0},
        compiler_params=_params(("parallel", "arbitrary"), big=True),
    )(dep, a, wg, out)


def _mm_tn_gathered(dep, h, a, a_spec, ns, tm, name):
    M, K = h.shape

    def body(_, h_ref, a_ref, o_ref, acc_ref):
        m = pl.program_id(1)

        @pl.when(m == 0)
        def _():
            acc_ref[...] = jnp.zeros_like(acc_ref)

        acc_ref[...] += _dot(h_ref[...], a_ref[...], TN)

        @pl.when(m == pl.num_programs(1) - 1)
        def _():
            o_ref[...] = acc_ref[...].astype(BF16)

    return pl.pallas_call(
        body, grid=(N_DEV, M // tm), name=name,
        in_specs=[ORDER_ONLY, pl.BlockSpec((tm, K), lambda j, m: (m, 0)), a_spec(tm, ns)],
        out_specs=pl.BlockSpec((None, K, ns), lambda j, m: (_slot(j), 0, 0)),
        out_shape=_sds((N_DEV, K, ns), BF16),
        scratch_shapes=[pltpu.VMEM((K, ns), F32)],
        compiler_params=_params(("parallel", "arbitrary"), big=True),
    )(dep, h, a)


def _bias_onehot(rbp, max_rel):
    r = lax.broadcasted_iota(jnp.int32, (rbp, TAB), 0)
    m = lax.broadcasted_iota(jnp.int32, (rbp, TAB), 1)
    dist = KPAD - jnp.where(m < WIN, m, m - TAB)
    return (r == jnp.clip(dist, -max_rel, max_rel) + max_rel).astype(F32)


def _attn_setup(i, hp, k_ref, v_ref, gv_ref, kpad, vpad, bias):
    ls = slice(i * ATTN_HEAD_DIM, (i + 1) * ATTN_HEAD_DIM)
    kpad[i][0:KPAD, :] = jnp.zeros((KPAD, ATTN_HEAD_DIM), BF16)
    vpad[i][0:KPAD, :] = jnp.zeros((KPAD, ATTN_HEAD_DIM), BF16)
    kpad[i][KPAD:, :] = k_ref[:, ls].astype(BF16)
    vpad[i][KPAD:, :] = v_ref[:, ls].astype(BF16)
    gvrow = gv_ref[pl.ds(hp * ATTN_HEADS_PER_STEP + i, 1), :]
    tab = pltpu.roll(jnp.broadcast_to(gvrow, (QBLK, TAB)), 0, 1, stride=1, stride_axis=0)
    row = lax.broadcasted_iota(jnp.int32, (QBLK, WIN), 0)
    col = lax.broadcasted_iota(jnp.int32, (QBLK, WIN), 1)
    first = jnp.bitwise_and(row, -CHUNK)
    seen = jnp.logical_and(col >= first, col < first + (N_PAST + 1) * CHUNK)
    bias[i][...] = jnp.where(seen, tab[:, 0:WIN], NEG)


def _attn_probs(b, q_ref, kpad, vpad, bias, col):
    pair = range(ATTN_HEADS_PER_STEP)
    ls = [slice(i * ATTN_HEAD_DIM, (i + 1) * ATTN_HEAD_DIM) for i in pair]
    r0 = pl.multiple_of(b * QBLK, QBLK)
    q = [q_ref[pl.ds(r0, QBLK), ls[i]].astype(BF16) for i in pair]
    kw = [kpad[i][pl.ds(r0, WIN), :] for i in pair]
    vw = [vpad[i][pl.ds(r0, WIN), :] for i in pair]
    s = [_dot(q[i], kw[i], NT) * (ATTN_HEAD_DIM ** -0.5) + bias[i][...] for i in pair]
    s = [jnp.where(col >= KPAD - r0, s[i], NEG) for i in pair]
    p = [jnp.exp(s[i] - jnp.max(s[i], axis=-1, keepdims=True)) for i in pair]
    pn = [p[i] / jnp.sum(p[i], axis=-1, keepdims=True) for i in pair]
    return r0, ls, q, kw, vw, pn


def _attn_fwd(proj, gv, ga, AW):
    T = proj.shape[0]
    AH = ATTN_HEADS_PER_STEP
    W = AH * ATTN_HEAD_DIM
    HP = AW // W

    def body(q_ref, k_ref, v_ref, gv_ref, ga_ref, o_ref, *scratch):
        kpad, vpad, bias = (scratch[k * AH:(k + 1) * AH] for k in range(3))
        hp = pl.program_id(0)
        for i in range(AH):
            _attn_setup(i, hp, k_ref, v_ref, gv_ref, kpad, vpad, bias)
        col = lax.broadcasted_iota(jnp.int32, (QBLK, WIN), 1)

        def block(b, carry):
            pair = range(AH)
            r0, ls, _, _, vw, pn = _attn_probs(b, q_ref, kpad, vpad, bias, col)
            o = [_dot(pn[i].astype(BF16), vw[i], NN) for i in pair]
            r = [lax.rsqrt(jnp.mean(o[i] * o[i], axis=-1, keepdims=True) + EPS) for i in pair]
            outs = [o[i] * r[i] * ga_ref[0:1, ls[i]] for i in pair]
            o_ref[pl.ds(r0, QBLK), :] = jnp.concatenate(outs, axis=1).astype(BF16)
            return carry

        lax.fori_loop(0, T // QBLK, block, 0)

    blk = lambda off: pl.BlockSpec((T, W), lambda hp: (0, off + hp))
    return pl.pallas_call(
        body, grid=(HP,), name="attn_fwd",
        in_specs=[blk(0), blk(HP), blk(2 * HP), pl.BlockSpec(gv.shape, lambda hp: (0, 0)),
                  pl.BlockSpec((1, W), lambda hp: (0, hp))],
        out_specs=pl.BlockSpec((T, W), lambda hp: (0, hp)),
        out_shape=_sds((T, AW), BF16),
        scratch_shapes=[pltpu.VMEM((T + KPAD, ATTN_HEAD_DIM), BF16)] * (2 * AH) + [pltpu.VMEM((QBLK, WIN), F32)] * AH,
        compiler_params=_params(("parallel",), big=True),
    )(proj, proj, proj, gv, ga)


def _attn_bwd(proj, dmixin, gv, ga, AW):
    T = proj.shape[0]
    AH = ATTN_HEADS_PER_STEP
    W = AH * ATTN_HEAD_DIM
    HP = AW // W
    scale = ATTN_HEAD_DIM ** -0.5

    def body(q_ref, k_ref, v_ref, dn_ref, gv_ref, ga_ref, dq_ref, dk_ref, dv_ref, dgv_ref, dga_ref, *scratch):
        kpad, vpad, dkacc, dvacc, bias, dbias = (scratch[k * AH:(k + 1) * AH] for k in range(6))
        hp = pl.program_id(0)
        for i in range(AH):
            _attn_setup(i, hp, k_ref, v_ref, gv_ref, kpad, vpad, bias)
            dkacc[i][...] = jnp.zeros_like(dkacc[i])
            dvacc[i][...] = jnp.zeros_like(dvacc[i])
            dbias[i][...] = jnp.zeros_like(dbias[i])
        dga_ref[...] = jnp.zeros_like(dga_ref)
        col = lax.broadcasted_iota(jnp.int32, (QBLK, WIN), 1)

        def block(b, carry):
            pair = range(AH)
            r0, lss, qs, kws, vws, pns = _attn_probs(b, q_ref, kpad, vpad, bias, col)
            pn_b = [pns[i].astype(BF16) for i in pair]
            o = [_dot(pn_b[i], vws[i], NN) for i in pair]
            r = [lax.rsqrt(jnp.mean(o[i] * o[i], axis=-1, keepdims=True) + EPS) for i in pair]
            dn = [dn_ref[pl.ds(r0, QBLK), lss[i]] for i in pair]
            for i in pair:
                dga_ref[i:i + 1, :] += _colsum(dn[i] * o[i] * r[i])
            a = [dn[i] * ga_ref[0:1, lss[i]] for i in pair]
            do_b = [(r[i] * (a[i] - o[i] * (r[i] * r[i]) * jnp.mean(a[i] * o[i], axis=-1, keepdims=True))).astype(BF16)
                    for i in pair]
            dp = [_dot(do_b[i], vws[i], NT) for i in pair]
            for i in pair:
                dvacc[i][pl.ds(r0, WIN), :] += _dot(pn_b[i], do_b[i], TN)
            ds = [pns[i] * (dp[i] - jnp.sum(pns[i] * dp[i], axis=-1, keepdims=True)) for i in pair]
            for i in pair:
                dbias[i][...] += ds[i]
            ds_b = [ds[i].astype(BF16) for i in pair]
            dq = [_dot(ds_b[i], kws[i], NN) * scale for i in pair]
            dq_ref[pl.ds(r0, QBLK), :] = jnp.concatenate(dq, axis=1).astype(BF16)
            for i in pair:
                dkacc[i][pl.ds(r0, WIN), :] += _dot(ds_b[i], qs[i], TN) * scale
            return carry

        lax.fori_loop(0, T // QBLK, block, 0)

        rr = lax.broadcasted_iota(jnp.int32, (QBLK, QBLK), 0)
        cc = lax.broadcasted_iota(jnp.int32, (QBLK, QBLK), 1)
        flip = (rr + cc == QBLK - 1).astype(BF16)
        for i in range(AH):
            ls = slice(i * ATTN_HEAD_DIM, (i + 1) * ATTN_HEAD_DIM)
            dk_ref[:, ls] = dkacc[i][KPAD:, :].astype(BF16)
            dv_ref[:, ls] = dvacc[i][KPAD:, :].astype(BF16)
            full = jnp.concatenate([dbias[i][...], jnp.zeros((QBLK, TAB - WIN), F32)], axis=1)
            hi = full.astype(BF16)
            lo = (full - hi.astype(F32)).astype(BF16)
            rev = _dot(flip, hi, NN) + _dot(flip, lo, NN)
            dgv_ref[i:i + 1, :] = _colsum(pltpu.roll(rev, TAB - (QBLK - 1), 1, stride=1, stride_axis=0))

    blk = lambda off: pl.BlockSpec((T, W), lambda hp: (0, off + hp))
    accs = lambda dt: [pltpu.VMEM((T + KPAD, ATTN_HEAD_DIM), dt)] * AH
    return pl.pallas_call(
        body, grid=(HP,), name="attn_bwd",
        in_specs=[blk(0), blk(HP), blk(2 * HP), blk(0), pl.BlockSpec(gv.shape, lambda hp: (0, 0)),
                  pl.BlockSpec((1, W), lambda hp: (0, hp))],
        out_specs=[blk(0), blk(0), blk(0), pl.BlockSpec((None, AH, TAB), lambda hp: (hp, 0, 0)),
                   pl.BlockSpec((None, AH, ATTN_HEAD_DIM), lambda hp: (hp, 0, 0))],
        out_shape=[_sds((T, AW), BF16), _sds((T, AW), BF16), _sds((T, AW), BF16),
                   _sds((HP, AH, TAB), F32), _sds((HP, AH, ATTN_HEAD_DIM), F32)],
        scratch_shapes=accs(BF16) + accs(BF16) + accs(F32) + accs(F32) + [pltpu.VMEM((QBLK, WIN), F32)] * (2 * AH),
        compiler_params=_params(("parallel",), big=True),
    )(proj, proj, proj, dmixin, gv, ga)


def _ltri():
    r = lax.broadcasted_iota(jnp.int32, (CHUNK, CHUNK), 0)
    c = lax.broadcasted_iota(jnp.int32, (CHUNK, CHUNK), 1)
    return (c <= r).astype(BF16)


def _tri_dot(tri, v, dims):
    hi = v.astype(BF16)
    lo = (v - hi.astype(F32)).astype(BF16)
    return _dot(tri, hi, dims) + _dot(tri, lo, dims)


HEADS_PER_STEP = 2


def _alternate(stages):
    live = list(stages)
    while live:
        for g in list(live):
            if next(g, StopIteration) is StopIteration:
                live.remove(g)


def _hgrn_gates(n, ls, q_ref, f_ref, lb_ref, ltri):
    r0 = pl.multiple_of(n * CHUNK, CHUNK)
    rows = pl.ds(r0, CHUNK)
    lb = lb_ref[:, ls]
    qb = q_ref[rows, ls]
    sg = _sigmoid(f_ref[rows, ls])
    f = lb + (1.0 - lb) * sg
    sq = _sigmoid(qb)
    b = _tri_dot(ltri, jnp.log(f), NN)
    return rows, lb, qb, sg, f, 1.0 - f, sq, qb * sq, b


def _hgrn_specs(T, RW, AW):
    HG = HEADS_PER_STEP
    W = HG * LANE
    base = 3 * AW // W
    blk_in = lambda off: pl.BlockSpec((T, W), lambda g: (0, base + off + g))
    col = pl.BlockSpec((T, W), lambda g: (0, g))
    return HG, W, RW // W, blk_in, col


def _hgrn_fwd(proj, lb, gn, AW, RW):
    T = proj.shape[0]
    RH, NC, NSUB = RW // LANE, T // CHUNK, CHUNK // SUB
    HG, W, NG, blk_in, col = _hgrn_specs(T, RW, AW)

    def body(q_ref, f_ref, i_ref, g_ref, lb_ref, gn_ref, mix_ref, o_ref, stall_ref, st_all, bs_all, kks_all, ics_all):
        st_all[...] = jnp.zeros_like(st_all)
        ltri = _ltri()
        rowi = lax.broadcasted_iota(jnp.int32, (SUB, 1), 0)

        def one_head(h, n):
            ls = slice(h * LANE, (h + 1) * LANE)
            st, bs, kks, ics = st_all.at[h], bs_all.at[h], kks_all.at[h], ics_all.at[h]
            rows, _, _, _, _, kk, _, qs, b = _hgrn_gates(n, ls, q_ref, f_ref, lb_ref, ltri)
            ic = i_ref[rows, ls]
            stv = st[...]
            stall_ref[h, n] = stv
            bs[...] = b
            kks[...] = kk
            ics[...] = ic
            yield
            o = _dot((qs * jnp.exp(b)).astype(BF16), stv.astype(BF16), NT)
            yield
            ic_b = ic.astype(BF16)
            pieces = []
            for blk in range(NSUB):
                s0 = blk * SUB
                bI, qI = b[s0:s0 + SUB], qs[s0:s0 + SUB]
                if blk == 0:
                    oI = jnp.zeros((SUB, LANE), F32)
                else:
                    ref = bs[s0 - 1:s0, :]
                    qt = (qI * jnp.exp(bI - ref)).astype(BF16)
                    kt = (kk[0:s0] * jnp.exp(ref - b[0:s0])).astype(BF16)
                    oI = _dot(_dot(qt, kt, NT).astype(BF16), ic_b[0:s0], NN)
                    yield
                acc = [oI[g * ROWS:(g + 1) * ROWS] for g in range(SUB // ROWS)]
                for s in range(SUB):
                    sr = s0 + s
                    g0 = s // ROWS
                    lo = g0 * ROWS
                    e = jnp.exp(jnp.minimum(bI[lo:] - bs[sr:sr + 1, :], 0.0))
                    a = jnp.sum(qI[lo:] * kks[sr:sr + 1, :] * e, axis=-1, keepdims=True)
                    add = jnp.where(rowi[lo:] >= s, a, 0.0) * ics[sr:sr + 1, :]
                    for g in range(g0, SUB // ROWS):
                        acc[g] = acc[g] + add[(g - g0) * ROWS:(g - g0 + 1) * ROWS]
                    yield
                pieces.extend(acc)
            o = o + jnp.concatenate(pieces, axis=0)
            bl = bs[CHUNK - 1:CHUNK, :]
            kd = (kk * jnp.exp(bl - b)).astype(BF16)
            st[...] = stv * jnp.exp(bl) + _dot(ic_b, kd, TN)
            yield
            o_ref[rows, ls] = o
            r = lax.rsqrt(jnp.mean(o * o, axis=-1, keepdims=True) + EPS)
            gb = g_ref[rows, ls]
            mix_ref[rows, ls] = (o * r * gn_ref[...] * (gb * _sigmoid(gb))).astype(BF16)

        def chunk(n, carry):
            _alternate([one_head(h, n) for h in range(HG)])
            return carry

        lax.fori_loop(0, NC, chunk, 0)

    tile = pltpu.VMEM((HG, CHUNK, LANE), F32)
    return pl.pallas_call(
        body, grid=(NG,), name="hgrn_fwd",
        in_specs=[blk_in(0), blk_in(NG), blk_in(2 * NG), blk_in(3 * NG), pl.BlockSpec((1, W), lambda g: (0, g)),
                  pl.BlockSpec((1, LANE), lambda g: (0, 0))],
        out_specs=[col, col, pl.BlockSpec((HG, NC, LANE, LANE), lambda g: (g, 0, 0, 0))],
        out_shape=[_sds((T, RW), BF16), _sds((T, RW), F32), _sds((RH, NC, LANE, LANE), F32)],
        scratch_shapes=[pltpu.VMEM((HG, LANE, LANE), F32), tile, tile, tile],
        compiler_params=_params(("parallel",), big=True),
    )(proj, proj, proj, proj, lb, gn)


def _hgrn_bwd(proj, dmixin, o_b, st_all, lb, gn, AW, RW):
    T = proj.shape[0]
    RH, NC, NSUB = RW // LANE, T // CHUNK, CHUNK // SUB
    HG, W, NG, blk_in, col = _hgrn_specs(T, RW, AW)

    def body(q_ref, f_ref, i_ref, g_ref, o_ref, dn_ref, stall_ref, lb_ref, gn_ref,
             dq_ref, df_ref, di_ref, dg_ref, dlb_ref, dgn_ref, dst_all, bs_all, kks_all, ics_all, p2_all, dic_all):
        dst_all[...] = jnp.zeros_like(dst_all)
        dlb_ref[...] = jnp.zeros_like(dlb_ref)
        dgn_ref[...] = jnp.zeros_like(dgn_ref)
        ltri = _ltri()
        rowi = lax.broadcasted_iota(jnp.int32, (SUB, 1), 0)
        last = lax.broadcasted_iota(jnp.int32, (CHUNK, 1), 0) == CHUNK - 1

        def one_head(h, n):
            ls = slice(h * LANE, (h + 1) * LANE)
            dst, bs, kks, ics = dst_all.at[h], bs_all.at[h], kks_all.at[h], ics_all.at[h]
            p2, dic = p2_all.at[h], dic_all.at[h]
            rows, lbv, qb, sg, f, kk, sq, qs, b = _hgrn_gates(n, ls, q_ref, f_ref, lb_ref, ltri)
            ic = i_ref[rows, ls]
            stv = stall_ref[h, n]
            dstv = dst[...]
            o = o_ref[rows, ls]
            dn = dn_ref[rows, ls]
            gb = g_ref[rows, ls]
            sgb = _sigmoid(gb)
            r = lax.rsqrt(jnp.mean(o * o, axis=-1, keepdims=True) + EPS)
            gnv = gn_ref[...]
            dg_ref[rows, ls] = (dn * (o * r * gnv) * (sgb * (1.0 + gb * (1.0 - sgb)))).astype(BF16)
            dy = dn * (gb * sgb)
            dgn_ref[h] += _colsum(dy * o * r)
            a_ = dy * gnv
            do = r * (a_ - o * (r * r) * jnp.mean(a_ * o, axis=-1, keepdims=True))
            do_b = do.astype(BF16)
            bs[...] = b
            kks[...] = kk
            ics[...] = ic
            yield
            ic_b = ic.astype(BF16)
            eb = jnp.exp(b)
            bl = bs[CHUNK - 1:CHUNK, :]
            ebl = jnp.exp(bl)
            dec = jnp.exp(bl - b)
            kd = (kk * dec).astype(BF16)
            dst_b = dstv.astype(BF16)
            dqs = _dot(do_b, stv.astype(BF16), NN) * eb
            dkk2 = _dot(ic_b, dst_b, NN) * dec
            dic[...] = _dot(kd, dst_b, NT)
            dbl = ebl * _colsum(stv * dstv) + _colsum(kk * dkk2)
            dst[...] = dstv * ebl + _dot(do_b, (qs * eb).astype(BF16), TN)
            yield
            p2[...] = jnp.zeros_like(p2)
            p1_pieces = []
            for blk in range(NSUB):
                s0 = blk * SUB
                bI, qI, doI = b[s0:s0 + SUB], qs[s0:s0 + SUB], do[s0:s0 + SUB]
                if blk == 0:
                    p1 = jnp.zeros((SUB, LANE), F32)
                else:
                    ref = bs[s0 - 1:s0, :]
                    eq = jnp.exp(bI - ref)
                    ek = jnp.exp(ref - b[0:s0])
                    qt = (qI * eq).astype(BF16)
                    kt = (kk[0:s0] * ek).astype(BF16)
                    doI_b = doI.astype(BF16)
                    dic[0:s0, :] += _dot(_dot(qt, kt, NT).astype(BF16), doI_b, TN)
                    da = _dot(doI_b, ic_b[0:s0], NT).astype(BF16)
                    p1 = _dot(da, kt, NN) * eq
                    p2[0:s0, :] += _dot(da, qt, TN) * ek
                    yield
                acc = [p1[g * ROWS:(g + 1) * ROWS] for g in range(SUB // ROWS)]
                for s in range(SUB):
                    sr = s0 + s
                    g0 = s // ROWS
                    lo = g0 * ROWS
                    keep = rowi[lo:] >= s
                    kk_s = kks[sr:sr + 1, :]
                    e = jnp.exp(jnp.minimum(bI[lo:] - bs[sr:sr + 1, :], 0.0))
                    w = qI[lo:] * e
                    a = jnp.where(keep, jnp.sum(w * kk_s, axis=-1, keepdims=True), 0.0)
                    da_s = jnp.where(keep, jnp.sum(doI[lo:] * ics[sr:sr + 1, :], axis=-1, keepdims=True), 0.0)
                    add = da_s * kk_s * e
                    for g in range(g0, SUB // ROWS):
                        acc[g] = acc[g] + add[(g - g0) * ROWS:(g - g0 + 1) * ROWS]
                    p2[sr:sr + 1, :] += _colsum(da_s * w)
                    dic[sr:sr + 1, :] += _colsum(a * doI[lo:])
                    yield
                p1_pieces.extend(acc)
            dqs = dqs + jnp.concatenate(p1_pieces, axis=0)
            dkk = dkk2 + p2[...]
            db = qs * dqs - kk * dkk + jnp.where(last, dbl, 0.0)
            dgl = _tri_dot(ltri, db, TN)
            yield
            dfv = dgl / f - dkk
            df_ref[rows, ls] = (dfv * (1.0 - lbv) * sg * (1.0 - sg)).astype(BF16)
            dlb_ref[:, ls] += _colsum(dfv * (1.0 - sg))
            dq_ref[rows, ls] = (dqs * (sq * (1.0 + qb * (1.0 - sq)))).astype(BF16)
            di_ref[rows, ls] = dic[...].astype(BF16)

        def chunk(k, carry):
            _alternate([one_head(h, NC - 1 - k) for h in range(HG)])
            return carry

        lax.fori_loop(0, NC, chunk, 0)

    tile = pltpu.VMEM((HG, CHUNK, LANE), F32)
    return pl.pallas_call(
        body, grid=(NG,), name="hgrn_bwd",
        in_specs=[blk_in(0), blk_in(NG), blk_in(2 * NG), blk_in(3 * NG), col,
                  pl.BlockSpec((T, W), lambda g: (0, AW // W + g)),
                  pl.BlockSpec((HG, NC, LANE, LANE), lambda g: (g, 0, 0, 0)),
                  pl.BlockSpec((1, W), lambda g: (0, g)), pl.BlockSpec((1, LANE), lambda g: (0, 0))],
        out_specs=[col, col, col, col, pl.BlockSpec((1, W), lambda g: (0, g)),
                   pl.BlockSpec((HG, 1, LANE), lambda g: (g, 0, 0))],
        out_shape=[_sds((T, RW), BF16)] * 4 + [_sds((1, RW), F32), _sds((RH, 1, LANE), F32)],
        scratch_shapes=[pltpu.VMEM((HG, LANE, LANE), F32), tile, tile, tile, tile, tile],
        compiler_params=_params(("parallel",), big=True),
    )(proj, proj, proj, proj, o_b, dmixin, st_all, lb, gn)


def _prep(c, lb_logits, rb_pad, max_rel):
    D, RW = c.shape[-1], lb_logits.shape[-1]
    H, rbp = rb_pad.shape

    def body(c_ref, l_ref, rb_ref, cact_ref, lb_ref, gv_ref):
        cv = c_ref[...]
        cact_ref[...] = cv * _sigmoid(cv)
        lb_ref[...] = _sigmoid(l_ref[0:1, :] - l_ref[1:2, :])
        gv_ref[...] = _dot(rb_ref[...], _bias_onehot(rbp, max_rel), NN, HIGHEST)

    return pl.pallas_call(
        body, name="prep", out_shape=[_sds((1, D), F32), _sds((1, RW), F32), _sds((H, TAB), F32)],
    )(c, lb_logits, rb_pad)


def _mod_part(c_all, w_ada_s, b_ada_s):
    B, D = c_all.shape
    ns = w_ada_s.shape[1]
    tn = _tile(ns, 768, LANE)

    def body(c_ref, w_ref, b_ref, o_ref):
        o_ref[...] = _dot(c_ref[...], w_ref[...], NN) + b_ref[...]

    return pl.pallas_call(
        body, grid=(ns // tn,), name="mod_part",
        in_specs=[pl.BlockSpec((B, D), lambda j: (0, 0)), pl.BlockSpec((D, tn), lambda j: (0, j)),
                  pl.BlockSpec((1, tn), lambda j: (0, j))],
        out_specs=pl.BlockSpec((B, tn), lambda j: (0, j)),
        out_shape=_sds((B, ns), F32), compiler_params=_params(("parallel",)),
    )(c_all, w_ada_s, b_ada_s)


def _adam(w, g, m, v):
    m = ADAM_B1 * m + (1.0 - ADAM_B1) * g
    v = ADAM_B2 * v + (1.0 - ADAM_B2) * (g * g)
    m_hat = m * (1.0 / (1.0 - ADAM_B1 ** ADAM_STEP))
    v_hat = v * (1.0 / (1.0 - ADAM_B2 ** ADAM_STEP))
    return -ADAM_LR * (m_hat / (jnp.sqrt(v_hat) + ADAM_EPS) + ADAM_WD * w), m, v


def _adam_ada(c_all, dmod_s, w, m, v):
    B, D = c_all.shape
    ns = w.shape[1]
    tr, tn = _tile(D, 512, LANE), _tile(ns, 768, LANE)

    def body(c_ref, d_ref, w_ref, m_ref, v_ref, g_out, dw_out, m_out, v_out):
        g = _dot(c_ref[...], d_ref[...], TN)
        g_out[...] = g
        dw_out[...], m_out[...], v_out[...] = _adam(w_ref[...], g, m_ref[...], v_ref[...])

    big = pl.BlockSpec((tr, tn), lambda i, j: (i, j))
    return pl.pallas_call(
        body, grid=(D // tr, ns // tn), name="adam_w_ada",
        in_specs=[pl.BlockSpec((B, tr), lambda i, j: (0, i)), pl.BlockSpec((B, tn), lambda i, j: (0, j)),
                  big, big, big],
        out_specs=[big] * 4, out_shape=[_sds((D, ns), F32)] * 4,
        compiler_params=_params(("parallel", "parallel")),
    )(c_all, dmod_s, w, m, v)


def _adam_shard(parts, w, m, v, name):
    R, C = w.shape
    tr = _tile(R, 256, 16)

    def body(p_ref, w_ref, m_ref, v_ref, g_out, dw_out, m_out, v_out):
        g = p_ref[0].astype(F32)
        for k in range(1, N_DEV // 2):
            g = g + p_ref[k].astype(F32)
        g_out[...] = g
        dw_out[...], m_out[...], v_out[...] = _adam(w_ref[...], g, m_ref[...], v_ref[...])

    big = pl.BlockSpec((tr, C), lambda i: (i, 0))
    return pl.pallas_call(
        body, grid=(R // tr,), name=name,
        in_specs=[pl.BlockSpec((N_DEV // 2, tr, C), lambda i: (0, i, 0)), big, big, big],
        out_specs=[big] * 4, out_shape=[_sds((R, C), F32)] * 4,
        compiler_params=_params(("parallel",), big=True),
    )(parts, w, m, v)


def _pair_sum(g8, land, core, name):
    _, NCHIP, R, C = g8.shape
    tr = _tile(R, 1024, 16)

    def body(core_ref, g_ref, l_ref, o_ref):
        o_ref[...] = g_ref[...] + l_ref[...]

    return pl.pallas_call(
        body, name=name,
        grid_spec=pltpu.PrefetchScalarGridSpec(
            num_scalar_prefetch=1, grid=(NCHIP, R // tr),
            in_specs=[pl.BlockSpec((None, None, tr, C), lambda k, i, core_ref: (core_ref[0], k, i, 0)),
                      pl.BlockSpec((None, tr, C), lambda k, i, core_ref: (k, i, 0))],
            out_specs=pl.BlockSpec((None, tr, C), lambda k, i, core_ref: (k, i, 0))),
        out_shape=_sds((NCHIP, R, C), BF16), compiler_params=_params(("parallel", "parallel")),
    )(core, g8, land)


SMALL = ("b_ada", "rel_bias", "attn_norm_g", "lb_logits", "gnorm_g", "ln1_g", "ln1_b", "ln2_g", "ln2_b")


def _small_update(parts, loss_parts, lbv, ws, ms, vs, max_rel):
    n = len(SMALL)

    def body(*refs):
        part_refs = dict(zip(SMALL, refs[:n]))
        loss_in, lb_ref = refs[n], refs[n + 1]
        w_refs, m_refs, v_refs = refs[n + 2:2 * n + 2], refs[2 * n + 2:3 * n + 2], refs[3 * n + 2:4 * n + 2]
        outs = refs[4 * n + 2:]

        def total(ref):
            tot = ref[0]
            for k in range(1, N_DEV):
                tot = tot + ref[k]
            return tot

        outs[0][...] = jnp.sum(total(loss_in), axis=-1, keepdims=True)
        for idx, name in enumerate(SMALL):
            g = total(part_refs[name])
            if name == "rel_bias":
                g = _dot(g, _bias_onehot(w_refs[idx].shape[1], max_rel), NT, HIGHEST)
            elif name == "lb_logits":
                lb = lb_ref[...]
                sign = (1 - 2 * lax.broadcasted_iota(jnp.int32, (2, 1), 0)).astype(F32)
                g = sign * (g * lb * (1.0 - lb))
            elif name == "gnorm_g":
                g = _colsum(g)
            dw, mm, vv = _adam(w_refs[idx][...], g, m_refs[idx][...], v_refs[idx][...])
            outs[1 + 4 * idx][...] = g
            outs[2 + 4 * idx][...] = dw
            outs[3 + 4 * idx][...] = mm
            outs[4 + 4 * idx][...] = vv

    out_shape = [_sds((1, 1), F32)]
    for w in ws:
        out_shape += [_sds(w.shape, F32)] * 4
    return pl.pallas_call(body, name="small_update", out_shape=out_shape, compiler_params=_params(big=True))(
        *[parts[k] for k in SMALL], loss_parts, lbv, *ws, *ms, *vs)


def _place():
    x, y, c = lax.axis_index("x"), lax.axis_index("y"), lax.axis_index("c")
    return x, y, c, [(1 - x, y), (x, 1 - y), (1 - x, 1 - y)]


def _all_gather(shard, name):
    HBM = pl.BlockSpec(memory_space=pl.ANY)

    def body(x_ref, out_ref, send_sems, recv_sems, local_sem):
        x, y, c, chips = _place()
        me, sibling = (x, y, c), (x, y, 1 - c)

        def slot(px, py, pc):
            return out_ref.at[4 * px + 2 * py + pc]

        def copy(k, block, to, src=None):
            return pltpu.make_async_remote_copy(
                src_ref=slot(*block) if src is None else src, dst_ref=slot(*block),
                send_sem=send_sems.at[k], recv_sem=recv_sems.at[k], device_id=to, device_id_type=MESH)

        mine = pltpu.make_async_copy(x_ref, slot(*me), local_sem)
        mine.start()
        first = [copy(0, me, sibling, src=x_ref)]
        first += [copy(1 + j, me, (*chip, c), src=x_ref) for j, chip in enumerate(chips)]
        for cp in first:
            cp.start()
        passed = [copy(4 + j, (*chip, c), sibling) for j, chip in enumerate(chips)]
        for j, chip in enumerate(chips):
            copy(1 + j, (*chip, c), me).wait_recv()
            passed[j].start()
        copy(0, sibling, me).wait_recv()
        for j, chip in enumerate(chips):
            copy(4 + j, (*chip, 1 - c), me).wait_recv()
        for cp in first + passed:
            cp.wait_send()
        mine.wait()

    return pl.pallas_call(
        body, name=name, out_shape=_sds((N_DEV,) + shard.shape, shard.dtype),
        in_specs=[HBM], out_specs=HBM,
        scratch_shapes=[pltpu.SemaphoreType.DMA((7,)), pltpu.SemaphoreType.DMA((7,)), pltpu.SemaphoreType.DMA(())],
    )(shard)


SEM_SPEC = pl.BlockSpec(memory_space=pltpu.SEMAPHORE)
HBM_SPEC = pl.BlockSpec(memory_space=pltpu.HBM)
EFFECT = pltpu.SideEffectType.DATAFLOW_SIDE_EFFECTING


def _remote(src, dst, send_sems, recv_sems, k, dev):
    return pltpu.make_async_remote_copy(src_ref=src, dst_ref=dst, send_sem=send_sems.at[k], recv_sem=recv_sems.at[k],
                                        device_id=dev, device_id_type=MESH)


def _copy_start(name, bufs, plan, n, after, only=None):
    nb = len(bufs)

    def body(*refs):
        send_sems, recv_sems = refs[nb + 1], refs[nb + 2]
        for k, (src, dst, dev) in enumerate(plan(*refs[:nb])):
            if only is not None and k not in only:
                continue
            _remote(src, dst, send_sems, recv_sems, k, dev).start()
        refs[-1][...] = jnp.zeros_like(refs[-1])

    out = pl.pallas_call(
        body, name=name,
        out_shape=(pltpu.SemaphoreType.DMA((n,)), pltpu.SemaphoreType.DMA((n,)),
                   *[pltpu.HBM(b.shape, b.dtype) for b in bufs], _sds((8, LANE), F32)),
        in_specs=[HBM_SPEC] * nb + [ORDER_ONLY],
        out_specs=(SEM_SPEC, SEM_SPEC, *[HBM_SPEC] * nb, pl.BlockSpec(memory_space=pltpu.VMEM)),
        input_output_aliases={i: 2 + i for i in range(nb)},
        compiler_params=pltpu.CompilerParams(has_side_effects=EFFECT),
    )(*[pltpu.with_memory_space_constraint(b, pltpu.HBM) for b in bufs], after)
    return (out[0], out[1]), list(out[2:2 + nb]), out[-1]


def _copy_wait(name, sems, bufs, plan, after, only=None):
    nb = len(bufs)

    def body(*refs):
        send_sems, recv_sems = refs[nb], refs[nb + 1]
        for k, (src, dst, dev) in enumerate(plan(*refs[:nb])):
            if only is not None and k not in only:
                continue
            cp = _remote(src, dst, send_sems, recv_sems, k, dev)
            cp.wait_send()
            cp.wait_recv()

    out = pl.pallas_call(
        body, name=name, out_shape=tuple(pltpu.HBM(b.shape, b.dtype) for b in bufs),
        in_specs=[HBM_SPEC] * nb + [SEM_SPEC, SEM_SPEC, pl.BlockSpec(memory_space=pl.ANY)],
        out_specs=tuple([HBM_SPEC] * nb), input_output_aliases={i: i for i in range(nb)},
        compiler_params=pltpu.CompilerParams(has_side_effects=EFFECT),
    )(*bufs, sems[0], sems[1], after)
    return list(out)


def _ag_plan_chips(shard_ref, out_ref):
    x, y, c, chips = _place()
    mine = out_ref.at[4 * x + 2 * y + c]
    return [(shard_ref, mine, (x, y, 1 - c))] + [(shard_ref, mine, (*chip, c)) for chip in chips]


def _ag_plan_pass(out_ref):
    x, y, c, chips = _place()
    slots = [out_ref.at[4 * chip[0] + 2 * chip[1] + c] for chip in chips]
    return [(s, s, (x, y, 1 - c)) for s in slots]


def _rs_plan_pair(g_ref, land_ref):
    x, y, c, _ = _place()
    return [(g_ref.at[1 - c], land_ref, (x, y, 1 - c))]


def _rs_plan_chips(p_ref, land_ref):
    x, y, c, chips = _place()
    return [(p_ref.at[2 * chip[0] + chip[1]], land_ref.at[2 * x + y], (*chip, c)) for chip in chips]


class _Gather:
    def __init__(self, shard, me, tag, after):
        self.tag = tag
        out = lax.dynamic_update_slice(lax.empty((N_DEV,) + shard.shape, shard.dtype), shard[None],
                                       (me,) + (0,) * shard.ndim)
        self.sems, (self.shard, self.out), self.token = _copy_start(
            "ag_start_" + tag, [shard, out], _ag_plan_chips, 4, after)
        self.groups = []

    def arrived(self, after, copies):
        name = "ag_wait_%s_%s" % (self.tag, "".join(map(str, copies)))
        self.shard, self.out = _copy_wait(name, self.sems, [self.shard, self.out], _ag_plan_chips, after, copies)
        return self.out

    def pass_on(self, after, blocks):
        name = "ag_pass_%s_%s" % (self.tag, "".join(map(str, blocks)))
        sems, (self.out,), _ = _copy_start(name, [self.out], _ag_plan_pass, 3, after, blocks)
        self.groups.append((sems, blocks))
        return self.out

    def passed(self, after, group):
        sems, blocks = self.groups[group]
        name = "ag_pass_wait_%s_%s" % (self.tag, "".join(map(str, blocks)))
        self.out = _copy_wait(name, sems, [self.out], _ag_plan_pass, after, blocks)[0]
        return self.out

    def arrived_from_chips(self, after):
        self.arrived(after, (0, 1, 2, 3))
        return self.pass_on(after, (0, 1, 2))

    def passed_on(self, after):
        return self.passed(after, 0)


def _ag_plan_direct(src_ref, out_ref):
    x, y, c, chips = _place()
    mine = out_ref.at[4 * x + 2 * y + c]
    peers = [(x, y, 1 - c)] + [(*chip, pc) for chip in chips for pc in (c, 1 - c)]
    return [(src_ref, mine, peer) for peer in peers]


class _SmallGather:
    def __init__(self, block, me, tag):
        self.tag = tag
        out = lax.dynamic_update_slice(lax.empty((N_DEV,) + block.shape, block.dtype), block[None],
                                       (me,) + (0,) * block.ndim)
        self.sems, self.bufs, self.token = _copy_start(
            "ag_direct_start_" + tag, [block, out], _ag_plan_direct, N_DEV - 1, jnp.zeros((1,), F32))

    def done(self, after):
        return _copy_wait("ag_direct_wait_" + self.tag, self.sems, self.bufs, _ag_plan_direct, after)[1]


class _ReduceScatter:
    def __init__(self, g8, tag):
        self.tag = tag
        land = lax.empty(g8.shape[1:], g8.dtype)
        self.sems, self.bufs, self.token = _copy_start(
            "rs_pair_start_" + tag, [g8, land], _rs_plan_pair, 1, jnp.zeros((1,), F32))

    def pair_done(self, core, chip, after, start_after=None):
        g8, land = _copy_wait("rs_pair_wait_" + self.tag, self.sems, self.bufs, _rs_plan_pair, after)
        p4 = _pair_sum(g8, land, core, "rs_pair_sum_" + self.tag)
        own = lax.dynamic_slice_in_dim(p4, chip, 1, axis=0)
        land2 = lax.dynamic_update_slice(lax.empty(p4.shape, p4.dtype), own, (chip, 0, 0))
        self.sems, self.bufs, self.token = _copy_start(
            "rs_chips_start_" + self.tag, [p4, land2], _rs_plan_chips, 3,
            jnp.zeros((1,), F32) if start_after is None else start_after)

    def sums(self, after):
        return _copy_wait("rs_chips_wait_" + self.tag, self.sems, self.bufs, _rs_plan_chips, after)[1]


BIG = ("w_in", "w_o", "w_ffn_in", "w_ffn_out")
ORDER = ("w_ada", "b_ada", "w_in", "rel_bias", "attn_norm_g", "lb_logits", "gnorm_g", "w_o", "ln1_g", "ln1_b",
         "w_ffn_in", "w_ffn_out", "ln2_g", "ln2_b")


def kernel(x, c, w_ada, b_ada, w_in, rel_bias, attn_norm_g, lb_logits, gnorm_g, w_o, ln1_g, ln1_b, w_ffn_in, w_ffn_out, ln2_g, ln2_b, loss_target, m_w_ada, m_b_ada, m_w_in, m_rel_bias, m_attn_norm_g, m_lb_logits, m_gnorm_g, m_w_o, m_ln1_g, m_ln1_b, m_w_ffn_in, m_w_ffn_out, m_ln2_g, m_ln2_b, v_w_ada, v_b_ada, v_w_in, v_rel_bias, v_attn_norm_g, v_lb_logits, v_gnorm_g, v_w_o, v_ln1_g, v_ln1_b, v_w_ffn_in, v_w_ffn_out, v_ln2_g, v_ln2_b):
    W = dict(w_ada=w_ada, b_ada=b_ada, w_in=w_in, rel_bias=rel_bias, attn_norm_g=attn_norm_g, lb_logits=lb_logits,
             gnorm_g=gnorm_g, w_o=w_o, ln1_g=ln1_g, ln1_b=ln1_b, w_ffn_in=w_ffn_in, w_ffn_out=w_ffn_out,
             ln2_g=ln2_g, ln2_b=ln2_b)
    M = dict(w_ada=m_w_ada, b_ada=m_b_ada, w_in=m_w_in, rel_bias=m_rel_bias, attn_norm_g=m_attn_norm_g,
             lb_logits=m_lb_logits, gnorm_g=m_gnorm_g, w_o=m_w_o, ln1_g=m_ln1_g, ln1_b=m_ln1_b,
             w_ffn_in=m_w_ffn_in, w_ffn_out=m_w_ffn_out, ln2_g=m_ln2_g, ln2_b=m_ln2_b)
    V = dict(w_ada=v_w_ada, b_ada=v_b_ada, w_in=v_w_in, rel_bias=v_rel_bias, attn_norm_g=v_attn_norm_g,
             lb_logits=v_lb_logits, gnorm_g=v_gnorm_g, w_o=v_w_o, ln1_g=v_ln1_g, ln1_b=v_ln1_b,
             w_ffn_in=v_w_ffn_in, w_ffn_out=v_w_ffn_out, ln2_g=v_ln2_g, ln2_b=v_ln2_b)

    x2, tgt = x[0], loss_target[0]
    T, D = x2.shape
    AW, RW = attn_norm_g.shape[-1], lb_logits.shape[-1]
    MIX = AW + RW
    H, RH = AW // ATTN_HEAD_DIM, RW // LANE
    RB = rel_bias.shape[-1]
    max_rel = (RB - 1) // 2
    rbp = -(-RB // LANE) * LANE
    F = w_ffn_out.shape[1] * N_DEV
    half = N_DEV // 2
    xi, yi, ci = lax.axis_index("x"), lax.axis_index("y"), lax.axis_index("c")
    me = 4 * xi + 2 * yi + ci
    core = jnp.reshape(ci, (1,)).astype(jnp.int32)
    pad_rb = lambda a: jnp.pad(a[0], ((0, 0), (0, rbp - RB)))

    chip = 2 * xi + yi

    c_act, lbv, gv = _prep(c, lb_logits, pad_rb(rel_bias), max_rel)
    c_all = _all_gather(c_act, "ag_c").reshape(N_DEV, D)
    ns_ada = w_ada.shape[-1]
    mod_part = _mod_part(c_all, w_ada[0], lax.dynamic_slice_in_dim(b_ada, me * ns_ada, ns_ada, axis=1))
    mod_all = _all_gather(mod_part, "ag_mod")
    mod6 = lax.dynamic_index_in_dim(mod_all, me, axis=1, keepdims=False).reshape(6, D)

    ag_in = _Gather(w_in[0].astype(BF16), me, "w_in", mod_all)
    ag_o = _Gather(w_o[0].astype(BF16), me, "w_o", ag_in.token)
    ag_f1 = _Gather(w_ffn_in[0].astype(BF16), me, "w_ffn_in", ag_o.token)
    ag_f2 = _Gather(w_ffn_out[0].astype(BF16), me, "w_ffn_out", ag_f1.token)

    h1 = _ln_mod(x2, mod6 + ag_f2.token[0, 0])
    ids = lambda pairs: jnp.stack([4 * px + 2 * py + pc for px, py, pc in pairs]).astype(jnp.int32)
    others = [(1 - xi, yi), (xi, 1 - yi), (1 - xi, 1 - yi)]
    proj = lax.empty((T, w_in.shape[-1] * N_DEV), F32)
    proj = _mm_gathered(h1, ag_in.arrived(h1, (0,)), ids([(xi, yi, ci), (xi, yi, 1 - ci)]), proj, "in_proj_a")
    ag_in.arrived(proj, (1, 2, 3))
    proj = _mm_gathered(h1, ag_in.pass_on(proj, (0, 1, 2)), ids([(*ch, ci) for ch in others]), proj, "in_proj_b")
    wg_in = ag_in.passed(proj, 0)
    proj = _mm_gathered(h1, wg_in, ids([(*ch, 1 - ci) for ch in others]), proj, "in_proj_c")
    ag_o.arrived_from_chips(proj)
    mix_a = _attn_fwd(proj, gv, attn_norm_g, AW)
    wg_o = ag_o.passed_on(mix_a).reshape(MIX, D)
    mix_b, o_b, st_all = _hgrn_fwd(proj, lbv, gnorm_g, AW, RW)
    mixin = jnp.concatenate([mix_a, mix_b], axis=1)
    mix = _mm_nn(mixin, wg_o, "out_proj")
    ag_f1.arrived_from_chips(mix)
    x1, h2 = _mid_fwd(x2, mix, mod6, ln1_g, ln1_b)
    wg_f1 = ag_f1.passed_on(h2)
    gu, act = _mm_swiglu(h2, wg_f1)
    ag_f2.arrived_from_chips(act)
    wg_f2 = ag_f2.passed_on(act).reshape(F, D)
    ff = _mm_nn(act, wg_f2, "ffn_out")
    dff, dx1a, vec_a = _final(x1, ff, mod6, ln2_g, ln2_b, tgt)

    du = _mm_swiglu_bwd(dff, wg_f2, gu)
    rs_f2 = _ReduceScatter(_mm_tn_rows(dff, act, dff, F // N_DEV, "grad_w_ffn_out"), "w_ffn_out")
    tm = _tile(T, 512, 16)
    du_ij = lambda tm_, ns, first: pl.BlockSpec((None, tm_, ns), lambda i, j: (j // half, i + first, j % half))
    du_jm = lambda tm_, ns: pl.BlockSpec((None, tm_, ns), lambda j, m: (j // half, m, j % half))
    dh2 = _mm_gathered_nt(rs_f2.token, du, du_ij, wg_f1, T, tm, "ffn_in_bwd")
    rs_f2.pair_done(core, chip, dh2)
    tm_red = _tile(T, 1024, 16)
    gw_f1 = _mm_tn_gathered(rs_f2.token, h2, du, du_jm, wg_f1.shape[-1], tm_red, "grad_w_ffn_in")
    rs_f1 = _ReduceScatter(gw_f1.reshape(2, half, D, -1), "w_ffn_in")
    dmix, dxa, vec_b = _mid_bwd(x2, mix, x1, dx1a, dh2, mod6 + rs_f1.token[0, 0], ln1_g)
    dmixin = _mm_nt(dmix, wg_o, "out_proj_bwd")
    rs_f1.pair_done(core, chip, dmixin)
    rs_o = _ReduceScatter(_mm_tn_rows(rs_f1.token, mixin, dmix, MIX // N_DEV, "grad_w_o"), "w_o")
    dq, dk, dv, dgv, dga = _attn_bwd(proj, dmixin, gv + rs_o.token[0, 0], attn_norm_g, AW)
    rs_o.pair_done(core, chip, dq)
    dqb, dfl, dib, dgb, dlb, dgn = _hgrn_bwd(proj, dmixin, o_b, st_all, lbv + rs_o.token[0, 0], gnorm_g, AW, RW)
    dproj = jnp.concatenate([dq, dk, dv, dqb, dfl, dib, dgb], axis=1)
    p_ij = lambda tm_, ns, first: pl.BlockSpec((tm_, ns), lambda i, j: (i + first, j))
    p_jm = lambda tm_, ns: pl.BlockSpec((tm_, ns), lambda j, m: (m, j))
    gw_in = _mm_tn_gathered(rs_o.token, h1, dproj, p_jm, wg_in.shape[-1], tm_red, "grad_w_in")
    rs_in = _ReduceScatter(gw_in.reshape(2, half, D, -1), "w_in")
    n_tiles = T // tm
    dh1 = _mm_gathered_nt(rs_in.token, dproj, p_ij, wg_in, T, tm, "in_proj_bwd_a", 0, n_tiles // 2)
    rs_in.pair_done(core, chip, dh1)
    dh1 = _mm_gathered_nt(rs_in.token, dproj, p_ij, wg_in, T, tm, "in_proj_bwd_b", n_tiles // 2,
                          n_tiles - n_tiles // 2, dh1)
    grad_x, vec_c = _first_bwd(x2, dh1, dxa, mod6)

    dmod = jnp.concatenate([vec_c[1:2], vec_c[0:1], vec_b[4:5], vec_b[1:2], vec_b[0:1], vec_a[2:3]], axis=0)
    pieces = dict(b_ada=dmod, rel_bias=dgv, attn_norm_g=dga, lb_logits=dlb, gnorm_g=dgn, ln1_g=vec_b[2:3],
                  ln1_b=vec_b[3:4], ln2_g=vec_a[0:1], ln2_b=vec_a[1:2], loss=vec_a[3:4])
    widths = dict(b_ada=(1, 6 * D), rel_bias=(H, TAB), attn_norm_g=(1, AW), lb_logits=(1, RW), gnorm_g=(RH, LANE),
                  ln1_g=(1, D), ln1_b=(1, D), ln2_g=(1, D), ln2_b=(1, D), loss=(1, D))
    packed = jnp.concatenate([pieces[k].reshape(-1, LANE) for k in widths], axis=0)
    small_ag = _SmallGather(packed, me, "small")
    after, res_big = small_ag.token, {}
    for k, rs in (("w_ffn_out", rs_f2), ("w_ffn_in", rs_f1), ("w_o", rs_o), ("w_in", rs_in)):
        four = _adam_shard(rs.sums(after), W[k][0], M[k][0], V[k][0], "adam_" + k)
        res_big[k] = [a[None] for a in four]
        after = four[0]
    gathered = small_ag.done(after)
    parts, r0 = {}, 0
    for k, (rows, width) in widths.items():
        nr = rows * width // LANE
        parts[k] = gathered[:, r0:r0 + nr, :].reshape(N_DEV, rows, width)
        r0 += nr
    prep_small = lambda d, k: pad_rb(d[k]) if k == "rel_bias" else d[k]
    small = _small_update(parts, parts["loss"], lbv, [prep_small(W, k) for k in SMALL],
                          [prep_small(M, k) for k in SMALL], [prep_small(V, k) for k in SMALL], max_rel)
    loss = small[0].reshape(())
    res = {}
    for idx, k in enumerate(SMALL):
        four = small[1 + 4 * idx:5 + 4 * idx]
        if k == "rel_bias":
            four = [a[:, :RB][None] for a in four]
        res[k] = list(four)

    res.update(res_big)
    dmod_s = lax.dynamic_slice_in_dim(parts["b_ada"].reshape(N_DEV, 6 * D), me * ns_ada, ns_ada, axis=1)
    res["w_ada"] = [a[None] for a in _adam_ada(c_all, dmod_s, w_ada[0], m_w_ada[0], v_w_ada[0])]

    out = [loss, grad_x[None]]
    for field in range(4):
        out += [res[k][field] for k in ORDER]
    return tuple(out)
```

```python
import functools

import jax
import jax.numpy as jnp
from jax import lax
from jax.experimental import pallas as pl
from jax.experimental.pallas import tpu as pltpu

F32 = jnp.float32
BF16 = jnp.bfloat16
MESH = pl.DeviceIdType.MESH
HIGHEST = lax.Precision.HIGHEST

N_DEV = 8
CHUNK = 64
N_PAST = 8
QBLK = 4 * CHUNK
KPAD = N_PAST * CHUNK
WIN = KPAD + QBLK
TAB = 1024
ATTN_HEAD_DIM = 64
ATTN_HEADS_PER_STEP = 4
REC_HEAD_DIM = 128
SUB = 16
ROWS = 8
LANE = 128
EPS = 1e-5
ALPHA = 2.0 ** 0.25
ADAM_LR, ADAM_B1, ADAM_B2, ADAM_EPS, ADAM_WD, ADAM_STEP = 0.001, 0.9, 0.999, 1e-08, 0.01, 10
NEG = -1e30
VMEM_LIMIT = 56 * 1024 * 1024


def _sds(shape, dtype):
    return jax.ShapeDtypeStruct(tuple(shape), dtype)


def _tile(n, pref, mult):
    best = None
    for t in range(mult, min(n, pref) + 1, mult):
        if n % t == 0:
            best = t
    return n if best is None else best


def _params(sem=None, big=False):
    kw = {}
    if sem is not None:
        kw["dimension_semantics"] = sem
    if big:
        kw["vmem_limit_bytes"] = VMEM_LIMIT
    return pltpu.CompilerParams(**kw)


def _sigmoid(v):
    return 1.0 / (1.0 + jnp.exp(-v))


def _dot(a, b, dims, precision=None):
    return lax.dot_general(a, b, (dims, ((), ())), preferred_element_type=F32, precision=precision)


NN = ((1,), (0,))
NT = ((1,), (1,))
TN = ((0,), (0,))


def _ln(v):
    mu = jnp.mean(v, axis=-1, keepdims=True)
    d = v - mu
    rstd = lax.rsqrt(jnp.mean(d * d, axis=-1, keepdims=True) + EPS)
    return d * rstd, rstd


def _ln_bwd(dxh, xh, rstd):
    return rstd * (dxh - jnp.mean(dxh, axis=-1, keepdims=True) - xh * jnp.mean(dxh * xh, axis=-1, keepdims=True))


def _colsum(v):
    return jnp.sum(v, axis=0, keepdims=True)


def _ln_mod(x2, mod6):
    T, D = x2.shape
    tm = _tile(T, 256, 8)

    def body(x_ref, mod_ref, o_ref):
        xh, _ = _ln(x_ref[...])
        o_ref[...] = (xh * (1.0 + mod_ref[1:2, :]) + mod_ref[0:1, :]).astype(BF16)

    return pl.pallas_call(
        body, grid=(T // tm,), name="ln_mod",
        in_specs=[pl.BlockSpec((tm, D), lambda i: (i, 0)), pl.BlockSpec((6, D), lambda i: (0, 0))],
        out_specs=pl.BlockSpec((tm, D), lambda i: (i, 0)),
        out_shape=_sds((T, D), BF16), compiler_params=_params(("parallel",)),
    )(x2, mod6)


def _mid_fwd(x2, mix, mod6, ln1_g, ln1_b):
    T, D = x2.shape
    tm = _tile(T, 256, 8)

    def body(x_ref, mix_ref, mod_ref, g_ref, b_ref, x1_ref, h2_ref):
        zh, _ = _ln(ALPHA * x_ref[...] + mod_ref[2:3, :] * mix_ref[...])
        x1 = zh * g_ref[...] + b_ref[...]
        x1_ref[...] = x1
        xh, _ = _ln(x1)
        h2_ref[...] = (xh * (1.0 + mod_ref[4:5, :]) + mod_ref[3:4, :]).astype(BF16)

    row = pl.BlockSpec((tm, D), lambda i: (i, 0))
    vec = pl.BlockSpec((1, D), lambda i: (0, 0))
    return pl.pallas_call(
        body, grid=(T // tm,), name="mid_fwd",
        in_specs=[row, row, pl.BlockSpec((6, D), lambda i: (0, 0)), vec, vec],
        out_specs=[row, row],
        out_shape=[_sds((T, D), F32), _sds((T, D), BF16)], compiler_params=_params(("parallel",)),
    )(x2, mix, mod6, ln1_g, ln1_b)


def _final(x1, ff, mod6, ln2_g, ln2_b, tgt):
    T, D = x1.shape
    tm = _tile(T, 256, 8)

    def body(x1_ref, ff_ref, mod_ref, g_ref, b_ref, t_ref, dff_ref, dx1_ref, vec_ref):
        @pl.when(pl.program_id(0) == 0)
        def _():
            vec_ref[...] = jnp.zeros_like(vec_ref)

        ff_v = ff_ref[...]
        gate2 = mod_ref[5:6, :]
        zh, rstd = _ln(ALPHA * x1_ref[...] + gate2 * ff_v)
        err = zh * g_ref[...] + b_ref[...] - t_ref[...]
        dy = err * (1.0 / D)
        dz = _ln_bwd(dy * g_ref[...], zh, rstd)
        dff_ref[...] = (gate2 * dz).astype(BF16)
        dx1_ref[...] = ALPHA * dz
        vec_ref[0:1, :] += _colsum(dy * zh)
        vec_ref[1:2, :] += _colsum(dy)
        vec_ref[2:3, :] += _colsum(dz * ff_v)
        vec_ref[3:4, :] += _colsum(err * err) * (0.5 / D)

    row = pl.BlockSpec((tm, D), lambda i: (i, 0))
    vec = pl.BlockSpec((1, D), lambda i: (0, 0))
    return pl.pallas_call(
        body, grid=(T // tm,), name="final_fwd_bwd",
        in_specs=[row, row, pl.BlockSpec((6, D), lambda i: (0, 0)), vec, vec, row],
        out_specs=[row, row, pl.BlockSpec((8, D), lambda i: (0, 0))],
        out_shape=[_sds((T, D), BF16), _sds((T, D), F32), _sds((8, D), F32)],
        compiler_params=_params(("arbitrary",)),
    )(x1, ff, mod6, ln2_g, ln2_b, tgt)


def _mid_bwd(x2, mix, x1, dx1a, dh2, mod6, ln1_g):
    T, D = x2.shape
    tm = _tile(T, 256, 8)

    def body(x_ref, mix_ref, x1_ref, dx1a_ref, dh2_ref, mod_ref, g_ref, dmix_ref, dxa_ref, vec_ref):
        @pl.when(pl.program_id(0) == 0)
        def _():
            vec_ref[...] = jnp.zeros_like(vec_ref)

        dh2 = dh2_ref[...]
        xh, rstd = _ln(x1_ref[...])
        dx1 = dx1a_ref[...] + _ln_bwd(dh2 * (1.0 + mod_ref[4:5, :]), xh, rstd)
        mix_v = mix_ref[...]
        gate1 = mod_ref[2:3, :]
        zh, rstdz = _ln(ALPHA * x_ref[...] + gate1 * mix_v)
        dz = _ln_bwd(dx1 * g_ref[...], zh, rstdz)
        dmix_ref[...] = (gate1 * dz).astype(BF16)
        dxa_ref[...] = ALPHA * dz
        vec_ref[0:1, :] += _colsum(dh2 * xh)
        vec_ref[1:2, :] += _colsum(dh2)
        vec_ref[2:3, :] += _colsum(dx1 * zh)
        vec_ref[3:4, :] += _colsum(dx1)
        vec_ref[4:5, :] += _colsum(dz * mix_v)

    row = pl.BlockSpec((tm, D), lambda i: (i, 0))
    vec = pl.BlockSpec((1, D), lambda i: (0, 0))
    return pl.pallas_call(
        body, grid=(T // tm,), name="mid_bwd",
        in_specs=[row, row, row, row, row, pl.BlockSpec((6, D), lambda i: (0, 0)), vec],
        out_specs=[row, row, pl.BlockSpec((8, D), lambda i: (0, 0))],
        out_shape=[_sds((T, D), BF16), _sds((T, D), F32), _sds((8, D), F32)],
        compiler_params=_params(("arbitrary",)),
    )(x2, mix, x1, dx1a, dh2, mod6, ln1_g)


def _first_bwd(x2, dh1, dxa, mod6):
    T, D = x2.shape
    tm = _tile(T, 256, 8)

    def body(x_ref, dh1_ref, dxa_ref, mod_ref, gx_ref, vec_ref):
        @pl.when(pl.program_id(0) == 0)
        def _():
            vec_ref[...] = jnp.zeros_like(vec_ref)

        dh1 = dh1_ref[...]
        xh, rstd = _ln(x_ref[...])
        gx_ref[...] = dxa_ref[...] + _ln_bwd(dh1 * (1.0 + mod_ref[1:2, :]), xh, rstd)
        vec_ref[0:1, :] += _colsum(dh1 * xh)
        vec_ref[1:2, :] += _colsum(dh1)

    row = pl.BlockSpec((tm, D), lambda i: (i, 0))
    return pl.pallas_call(
        body, grid=(T // tm,), name="first_bwd",
        in_specs=[row, row, row, pl.BlockSpec((6, D), lambda i: (0, 0))],
        out_specs=[row, pl.BlockSpec((8, D), lambda i: (0, 0))],
        out_shape=[_sds((T, D), F32), _sds((8, D), F32)],
        compiler_params=_params(("arbitrary",)),
    )(x2, dh1, dxa, mod6)


def _slot(j):
    return (j % 2) * 4 + j // 2


def _mm_gathered(a, wg, shards, out, name):
    M, K = a.shape
    _, _, ns = wg.shape
    tm = _tile(M, 512, 16)

    def body(shards_ref, a_ref, w_ref, prev_ref, o_ref):
        o_ref[...] = _dot(a_ref[...], w_ref[...], NN)

    return pl.pallas_call(
        body, name=name,
        grid_spec=pltpu.PrefetchScalarGridSpec(
            num_scalar_prefetch=1, grid=(shards.shape[0], M // tm),
            in_specs=[pl.BlockSpec((tm, K), lambda j, i, s: (i, 0)),
                      pl.BlockSpec((None, K, ns), lambda j, i, s: (s[j], 0, 0)), ORDER_ONLY],
            out_specs=pl.BlockSpec((tm, ns), lambda j, i, s: (i, s[j]))),
        out_shape=_sds((M, N_DEV * ns), F32), input_output_aliases={3: 0},
        compiler_params=_params(("parallel", "parallel"), big=True),
    )(shards, a, wg, out)


def _mm_nn(a, b, name):
    M, K = a.shape
    _, N = b.shape
    tm, tn, tk = _tile(M, 512, 16), _tile(N, 1024, LANE), _tile(K, 2048, LANE)

    def body(a_ref, b_ref, o_ref):
        @pl.when(pl.program_id(2) == 0)
        def _():
            o_ref[...] = jnp.zeros_like(o_ref)

        o_ref[...] += _dot(a_ref[...], b_ref[...], NN)

    return pl.pallas_call(
        body, grid=(M // tm, N // tn, K // tk), name=name,
        in_specs=[pl.BlockSpec((tm, tk), lambda i, j, k: (i, k)), pl.BlockSpec((tk, tn), lambda i, j, k: (k, j))],
        out_specs=pl.BlockSpec((tm, tn), lambda i, j, k: (i, j)),
        out_shape=_sds((M, N), F32), compiler_params=_params(("parallel", "parallel", "arbitrary"), big=True),
    )(a, b)


def _mm_nt(a, b, name):
    M, K = a.shape
    N, _ = b.shape
    tm, tn = _tile(M, 512, 16), _tile(N, 1024, LANE)

    def body(a_ref, b_ref, o_ref):
        o_ref[...] = _dot(a_ref[...], b_ref[...], NT)

    return pl.pallas_call(
        body, grid=(M // tm, N // tn), name=name,
        in_specs=[pl.BlockSpec((tm, K), lambda i, j: (i, 0)), pl.BlockSpec((tn, K), lambda i, j: (j, 0))],
        out_specs=pl.BlockSpec((tm, tn), lambda i, j: (i, j)),
        out_shape=_sds((M, N), F32), compiler_params=_params(("parallel", "parallel"), big=True),
    )(a, b)


def _mm_swiglu(h2, wg):
    M, K = h2.shape
    _, _, ns = wg.shape
    half = N_DEV // 2
    tm = _tile(M, 256, 16)

    def body(a_ref, wgate_ref, wup_ref, gu_ref, act_ref):
        a = a_ref[...]
        g = _dot(a, wgate_ref[...], NN)
        u = _dot(a, wup_ref[...], NN)
        gu_ref[0] = g
        gu_ref[1] = u
        act_ref[...] = (g * _sigmoid(g) * u).astype(BF16)

    return pl.pallas_call(
        body, grid=(half, M // tm), name="ffn_in_swiglu",
        in_specs=[pl.BlockSpec((tm, K), lambda j, i: (i, 0)),
                  pl.BlockSpec((None, K, ns), lambda j, i: (j, 0, 0)),
                  pl.BlockSpec((None, K, ns), lambda j, i: (j + half, 0, 0))],
        out_specs=[pl.BlockSpec((2, tm, ns), lambda j, i: (0, i, j)), pl.BlockSpec((tm, ns), lambda j, i: (i, j))],
        out_shape=[_sds((2, M, half * ns), F32), _sds((M, half * ns), BF16)],
        compiler_params=_params(("parallel", "parallel"), big=True),
    )(h2, wg, wg)


def _mm_swiglu_bwd(dff, w2, gu):
    M, K = dff.shape
    F = w2.shape[0]
    tm, tn = _tile(M, 512, 16), _tile(F, 1408, LANE)

    def body(a_ref, b_ref, gu_ref, du_ref):
        da = _dot(a_ref[...], b_ref[...], NT)
        g = gu_ref[0]
        u = gu_ref[1]
        sg = _sigmoid(g)
        du_ref[0] = (da * u * (sg * (1.0 + g * (1.0 - sg)))).astype(BF16)
        du_ref[1] = (da * (g * sg)).astype(BF16)

    return pl.pallas_call(
        body, grid=(F // tn, M // tm), name="ffn_out_bwd_swiglu",
        in_specs=[pl.BlockSpec((tm, K), lambda j, i: (i, 0)), pl.BlockSpec((tn, K), lambda j, i: (j, 0)),
                  pl.BlockSpec((2, tm, tn), lambda j, i: (0, i, j))],
        out_specs=pl.BlockSpec((2, tm, tn), lambda j, i: (0, i, j)),
        out_shape=_sds((2, M, F), BF16), compiler_params=_params(("parallel", "parallel"), big=True),
    )(dff, w2, gu)


ORDER_ONLY = pl.BlockSpec(memory_space=pl.ANY)


def _mm_tn_rows(dep, a, b, rs, name):
    M, Ka = a.shape
    _, N = b.shape
    tm = _tile(M, 1024, 16)

    def body(_, a_ref, b_ref, o_ref, acc_ref):
        m = pl.program_id(1)

        @pl.when(m == 0)
        def _():
            acc_ref[...] = jnp.zeros_like(acc_ref)

        acc_ref[...] += _dot(a_ref[...], b_ref[...], TN)

        @pl.when(m == pl.num_programs(1) - 1)
        def _():
            o_ref[0, 0] = acc_ref[0:rs, :].astype(BF16)
            o_ref[1, 0] = acc_ref[rs:2 * rs, :].astype(BF16)

    return pl.pallas_call(
        body, grid=(N_DEV // 2, M // tm), name=name,
        in_specs=[ORDER_ONLY, pl.BlockSpec((tm, 2 * rs), lambda ch, m: (m, ch)),
                  pl.BlockSpec((tm, N), lambda ch, m: (m, 0))],
        out_specs=pl.BlockSpec((2, 1, rs, N), lambda ch, m: (0, ch, 0, 0)),
        out_shape=_sds((2, N_DEV // 2, rs, N), BF16),
        scratch_shapes=[pltpu.VMEM((2 * rs, N), F32)],
        compiler_params=_params(("parallel", "arbitrary"), big=True),
    )(dep, a, b)


def _mm_gathered_nt(dep, a, a_spec, wg, M, tm, name, first=0, count=None, out=None):
    _, K, ns = wg.shape
    count = M // tm if count is None else count
    out = lax.empty((M, K), F32) if out is None else out

    def body(_, a_ref, w_ref, prev_ref, o_ref):
        @pl.when(pl.program_id(1) == 0)
        def _():
            o_ref[...] = jnp.zeros_like(o_ref)

        o_ref[...] += _dot(a_ref[:, 0:ns], w_ref[0], NT) + _dot(a_ref[:, ns:2 * ns], w_ref[1], NT)

    return pl.pallas_call(
        body, grid=(count, N_DEV // 2), name=name,
        in_specs=[ORDER_ONLY, a_spec(tm, 2 * ns, first), pl.BlockSpec((2, K, ns), lambda i, p: (p, 0, 0)), ORDER_ONLY],
        out_specs=pl.BlockSpec((tm, K), lambda i, j: (i + first, 0)),
        out_shape=_sds((M, K), F32), input_output_aliases={3: 0},
        compiler_params=_params(("parallel", "arbitrary"), big=True),
    )(dep, a, wg, out)


def _mm_tn_gathered(dep, h, a, a_spec, ns, tm, name):
    M, K = h.shape

    def body(_, h_ref, a_ref, o_ref, acc_ref):
        m = pl.program_id(1)

        @pl.when(m == 0)
        def _():
            acc_ref[...] = jnp.zeros_like(acc_ref)

        acc_ref[...] += _dot(h_ref[...], a_ref[...], TN)

        @pl.when(m == pl.num_programs(1) - 1)
        def _():
            o_ref[...] = acc_ref[...].astype(BF16)

    return pl.pallas_call(
        body, grid=(N_DEV, M // tm), name=name,
        in_specs=[ORDER_ONLY, pl.BlockSpec((tm, K), lambda j, m: (m, 0)), a_spec(tm, ns)],
        out_specs=pl.BlockSpec((None, K, ns), lambda j, m: (_slot(j), 0, 0)),
        out_shape=_sds((N_DEV, K, ns), BF16),
        scratch_shapes=[pltpu.VMEM((K, ns), F32)],
        compiler_params=_params(("parallel", "arbitrary"), big=True),
    )(dep, h, a)


def _bias_onehot(rbp, max_rel):
    r = lax.broadcasted_iota(jnp.int32, (rbp, TAB), 0)
    m = lax.broadcasted_iota(jnp.int32, (rbp, TAB), 1)
    dist = KPAD - jnp.where(m < WIN, m, m - TAB)
    return (r == jnp.clip(dist, -max_rel, max_rel) + max_rel).astype(F32)


def _attn_setup(i, hp, k_ref, v_ref, gv_ref, kpad, vpad, bias):
    ls = slice(i * ATTN_HEAD_DIM, (i + 1) * ATTN_HEAD_DIM)
    kpad[i][0:KPAD, :] = jnp.zeros((KPAD, ATTN_HEAD_DIM), BF16)
    vpad[i][0:KPAD, :] = jnp.zeros((KPAD, ATTN_HEAD_DIM), BF16)
    kpad[i][KPAD:, :] = k_ref[:, ls].astype(BF16)
    vpad[i][KPAD:, :] = v_ref[:, ls].astype(BF16)
    gvrow = gv_ref[pl.ds(hp * ATTN_HEADS_PER_STEP + i, 1), :]
    tab = pltpu.roll(jnp.broadcast_to(gvrow, (QBLK, TAB)), 0, 1, stride=1, stride_axis=0)
    row = lax.broadcasted_iota(jnp.int32, (QBLK, WIN), 0)
    col = lax.broadcasted_iota(jnp.int32, (QBLK, WIN), 1)
    first = jnp.bitwise_and(row, -CHUNK)
    seen = jnp.logical_and(col >= first, col < first + (N_PAST + 1) * CHUNK)
    bias[i][...] = jnp.where(seen, tab[:, 0:WIN], NEG)


def _attn_probs(b, q_ref, kpad, vpad, bias, col):
    pair = range(ATTN_HEADS_PER_STEP)
    ls = [slice(i * ATTN_HEAD_DIM, (i + 1) * ATTN_HEAD_DIM) for i in pair]
    r0 = pl.multiple_of(b * QBLK, QBLK)
    q = [q_ref[pl.ds(r0, QBLK), ls[i]].astype(BF16) for i in pair]
    kw = [kpad[i][pl.ds(r0, WIN), :] for i in pair]
    vw = [vpad[i][pl.ds(r0, WIN), :] for i in pair]
    s = [_dot(q[i], kw[i], NT) * (ATTN_HEAD_DIM ** -0.5) + bias[i][...] for i in pair]
    s = [jnp.where(col >= KPAD - r0, s[i], NEG) for i in pair]
    p = [jnp.exp(s[i] - jnp.max(s[i], axis=-1, keepdims=True)) for i in pair]
    pn = [p[i] / jnp.sum(p[i], axis=-1, keepdims=True) for i in pair]
    return r0, ls, q, kw, vw, pn


def _attn_fwd(proj, gv, ga, AW):
    T = proj.shape[0]
    AH = ATTN_HEADS_PER_STEP
    W = AH * ATTN_HEAD_DIM
    HP = AW // W

    def body(q_ref, k_ref, v_ref, gv_ref, ga_ref, o_ref, *scratch):
        kpad, vpad, bias = (scratch[k * AH:(k + 1) * AH] for k in range(3))
        hp = pl.program_id(0)
        for i in range(AH):
            _attn_setup(i, hp, k_ref, v_ref, gv_ref, kpad, vpad, bias)
        col = lax.broadcasted_iota(jnp.int32, (QBLK, WIN), 1)

        def block(b, carry):
            pair = range(AH)
            r0, ls, _, _, vw, pn = _attn_probs(b, q_ref, kpad, vpad, bias, col)
            o = [_dot(pn[i].astype(BF16), vw[i], NN) for i in pair]
            r = [lax.rsqrt(jnp.mean(o[i] * o[i], axis=-1, keepdims=True) + EPS) for i in pair]
            outs = [o[i] * r[i] * ga_ref[0:1, ls[i]] for i in pair]
            o_ref[pl.ds(r0, QBLK), :] = jnp.concatenate(outs, axis=1).astype(BF16)
            return carry

        lax.fori_loop(0, T // QBLK, block, 0)

    blk = lambda off: pl.BlockSpec((T, W), lambda hp: (0, off + hp))
    return pl.pallas_call(
        body, grid=(HP,), name="attn_fwd",
        in_specs=[blk(0), blk(HP), blk(2 * HP), pl.BlockSpec(gv.shape, lambda hp: (0, 0)),
                  pl.BlockSpec((1, W), lambda hp: (0, hp))],
        out_specs=pl.BlockSpec((T, W), lambda hp: (0, hp)),
        out_shape=_sds((T, AW), BF16),
        scratch_shapes=[pltpu.VMEM((T + KPAD, ATTN_HEAD_DIM), BF16)] * (2 * AH) + [pltpu.VMEM((QBLK, WIN), F32)] * AH,
        compiler_params=_params(("parallel",), big=True),
    )(proj, proj, proj, gv, ga)


def _attn_bwd(proj, dmixin, gv, ga, AW):
    T = proj.shape[0]
    AH = ATTN_HEADS_PER_STEP
    W = AH * ATTN_HEAD_DIM
    HP = AW // W
    scale = ATTN_HEAD_DIM ** -0.5

    def body(q_ref, k_ref, v_ref, dn_ref, gv_ref, ga_ref, dq_ref, dk_ref, dv_ref, dgv_ref, dga_ref, *scratch):
        kpad, vpad, dkacc, dvacc, bias, dbias = (scratch[k * AH:(k + 1) * AH] for k in range(6))
        hp = pl.program_id(0)
        for i in range(AH):
            _attn_setup(i, hp, k_ref, v_ref, gv_ref, kpad, vpad, bias)
            dkacc[i][...] = jnp.zeros_like(dkacc[i])
            dvacc[i][...] = jnp.zeros_like(dvacc[i])
            dbias[i][...] = jnp.zeros_like(dbias[i])
        dga_ref[...] = jnp.zeros_like(dga_ref)
        col = lax.broadcasted_iota(jnp.int32, (QBLK, WIN), 1)

        def block(b, carry):
            pair = range(AH)
            r0, lss, qs, kws, vws, pns = _attn_probs(b, q_ref, kpad, vpad, bias, col)
            pn_b = [pns[i].astype(BF16) for i in pair]
            o = [_dot(pn_b[i], vws[i], NN) for i in pair]
            r = [lax.rsqrt(jnp.mean(o[i] * o[i], axis=-1, keepdims=True) + EPS) for i in pair]
            dn = [dn_ref[pl.ds(r0, QBLK), lss[i]] for i in pair]
            for i in pair:
                dga_ref[i:i + 1, :] += _colsum(dn[i] * o[i] * r[i])
            a = [dn[i] * ga_ref[0:1, lss[i]] for i in pair]
            do_b = [(r[i] * (a[i] - o[i] * (r[i] * r[i]) * jnp.mean(a[i] * o[i], axis=-1, keepdims=True))).astype(BF16)
                    for i in pair]
            dp = [_dot(do_b[i], vws[i], NT) for i in pair]
            for i in pair:
                dvacc[i][pl.ds(r0, WIN), :] += _dot(pn_b[i], do_b[i], TN)
            ds = [pns[i] * (dp[i] - jnp.sum(pns[i] * dp[i], axis=-1, keepdims=True)) for i in pair]
            for i in pair:
                dbias[i][...] += ds[i]
            ds_b = [ds[i].astype(BF16) for i in pair]
            dq = [_dot(ds_b[i], kws[i], NN) * scale for i in pair]
            dq_ref[pl.ds(r0, QBLK), :] = jnp.concatenate(dq, axis=1).astype(BF16)
            for i in pair:
                dkacc[i][pl.ds(r0, WIN), :] += _dot(ds_b[i], qs[i], TN) * scale
            return carry

        lax.fori_loop(0, T // QBLK, block, 0)

        rr = lax.broadcasted_iota(jnp.int32, (QBLK, QBLK), 0)
        cc = lax.broadcasted_iota(jnp.int32, (QBLK, QBLK), 1)
        flip = (rr + cc == QBLK - 1).astype(BF16)
        for i in range(AH):
            ls = slice(i * ATTN_HEAD_DIM, (i + 1) * ATTN_HEAD_DIM)
            dk_ref[:, ls] = dkacc[i][KPAD:, :].astype(BF16)
            dv_ref[:, ls] = dvacc[i][KPAD:, :].astype(BF16)
            full = jnp.concatenate([dbias[i][...], jnp.zeros((QBLK, TAB - WIN), F32)], axis=1)
            hi = full.astype(BF16)
            lo = (full - hi.astype(F32)).astype(BF16)
            rev = _dot(flip, hi, NN) + _dot(flip, lo, NN)
            dgv_ref[i:i + 1, :] = _colsum(pltpu.roll(rev, TAB - (QBLK - 1), 1, stride=1, stride_axis=0))

    blk = lambda off: pl.BlockSpec((T, W), lambda hp: (0, off + hp))
    accs = lambda dt: [pltpu.VMEM((T + KPAD, ATTN_HEAD_DIM), dt)] * AH
    return pl.pallas_call(
        body, grid=(HP,), name="attn_bwd",
        in_specs=[blk(0), blk(HP), blk(2 * HP), blk(0), pl.BlockSpec(gv.shape, lambda hp: (0, 0)),
                  pl.BlockSpec((1, W), lambda hp: (0, hp))],
        out_specs=[blk(0), blk(0), blk(0), pl.BlockSpec((None, AH, TAB), lambda hp: (hp, 0, 0)),
                   pl.BlockSpec((None, AH, ATTN_HEAD_DIM), lambda hp: (hp, 0, 0))],
        out_shape=[_sds((T, AW), BF16), _sds((T, AW), BF16), _sds((T, AW), BF16),
                   _sds((HP, AH, TAB), F32), _sds((HP, AH, ATTN_HEAD_DIM), F32)],
        scratch_shapes=accs(BF16) + accs(BF16) + accs(F32) + accs(F32) + [pltpu.VMEM((QBLK, WIN), F32)] * (2 * AH),
        compiler_params=_params(("parallel",), big=True),
    )(proj, proj, proj, dmixin, gv, ga)


def _ltri():
    r = lax.broadcasted_iota(jnp.int32, (CHUNK, CHUNK), 0)
    c = lax.broadcasted_iota(jnp.int32, (CHUNK, CHUNK), 1)
    return (c <= r).astype(BF16)


def _tri_dot(tri, v, dims):
    hi = v.astype(BF16)
    lo = (v - hi.astype(F32)).astype(BF16)
    return _dot(tri, hi, dims) + _dot(tri, lo, dims)


HEADS_PER_STEP = 2


def _alternate(stages):
    live = list(stages)
    while live:
        for g in list(live):
            if next(g, StopIteration) is StopIteration:
                live.remove(g)


def _hgrn_gates(n, ls, q_ref, f_ref, lb_ref, ltri):
    r0 = pl.multiple_of(n * CHUNK, CHUNK)
    rows = pl.ds(r0, CHUNK)
    lb = lb_ref[:, ls]
    qb = q_ref[rows, ls]
    sg = _sigmoid(f_ref[rows, ls])
    f = lb + (1.0 - lb) * sg
    sq = _sigmoid(qb)
    b = _tri_dot(ltri, jnp.log(f), NN)
    return rows, lb, qb, sg, f, 1.0 - f, sq, qb * sq, b


def _hgrn_specs(T, RW, AW):
    HG = HEADS_PER_STEP
    W = HG * LANE
    base = 3 * AW // W
    blk_in = lambda off: pl.BlockSpec((T, W), lambda g: (0, base + off + g))
    col = pl.BlockSpec((T, W), lambda g: (0, g))
    return HG, W, RW // W, blk_in, col


def _hgrn_fwd(proj, lb, gn, AW, RW):
    T = proj.shape[0]
    RH, NC, NSUB = RW // LANE, T // CHUNK, CHUNK // SUB
    HG, W, NG, blk_in, col = _hgrn_specs(T, RW, AW)

    def body(q_ref, f_ref, i_ref, g_ref, lb_ref, gn_ref, mix_ref, o_ref, stall_ref, st_all, bs_all, kks_all, ics_all):
        st_all[...] = jnp.zeros_like(st_all)
        ltri = _ltri()
        rowi = lax.broadcasted_iota(jnp.int32, (SUB, 1), 0)

        def one_head(h, n):
            ls = slice(h * LANE, (h + 1) * LANE)
            st, bs, kks, ics = st_all.at[h], bs_all.at[h], kks_all.at[h], ics_all.at[h]
            rows, _, _, _, _, kk, _, qs, b = _hgrn_gates(n, ls, q_ref, f_ref, lb_ref, ltri)
            ic = i_ref[rows, ls]
            stv = st[...]
            stall_ref[h, n] = stv
            bs[...] = b
            kks[...] = kk
            ics[...] = ic
            yield
            o = _dot((qs * jnp.exp(b)).astype(BF16), stv.astype(BF16), NT)
            yield
            ic_b = ic.astype(BF16)
            pieces = []
            for blk in range(NSUB):
                s0 = blk * SUB
                bI, qI = b[s0:s0 + SUB], qs[s0:s0 + SUB]
                if blk == 0:
                    oI = jnp.zeros((SUB, LANE), F32)
                else:
                    ref = bs[s0 - 1:s0, :]
                    qt = (qI * jnp.exp(bI - ref)).astype(BF16)
                    kt = (kk[0:s0] * jnp.exp(ref - b[0:s0])).astype(BF16)
                    oI = _dot(_dot(qt, kt, NT).astype(BF16), ic_b[0:s0], NN)
                    yield
                acc = [oI[g * ROWS:(g + 1) * ROWS] for g in range(SUB // ROWS)]
                for s in range(SUB):
                    sr = s0 + s
                    g0 = s // ROWS
                    lo = g0 * ROWS
                    e = jnp.exp(jnp.minimum(bI[lo:] - bs[sr:sr + 1, :], 0.0))
                    a = jnp.sum(qI[lo:] * kks[sr:sr + 1, :] * e, axis=-1, keepdims=True)
                    add = jnp.where(rowi[lo:] >= s, a, 0.0) * ics[sr:sr + 1, :]
                    for g in range(g0, SUB // ROWS):
                        acc[g] = acc[g] + add[(g - g0) * ROWS:(g - g0 + 1) * ROWS]
                    yield
                pieces.extend(acc)
            o = o + jnp.concatenate(pieces, axis=0)
            bl = bs[CHUNK - 1:CHUNK, :]
            kd = (kk * jnp.exp(bl - b)).astype(BF16)
            st[...] = stv * jnp.exp(bl) + _dot(ic_b, kd, TN)
            yield
            o_ref[rows, ls] = o
            r = lax.rsqrt(jnp.mean(o * o, axis=-1, keepdims=True) + EPS)
            gb = g_ref[rows, ls]
            mix_ref[rows, ls] = (o * r * gn_ref[...] * (gb * _sigmoid(gb))).astype(BF16)

        def chunk(n, carry):
            _alternate([one_head(h, n) for h in range(HG)])
            return carry

        lax.fori_loop(0, NC, chunk, 0)

    tile = pltpu.VMEM((HG, CHUNK, LANE), F32)
    return pl.pallas_call(
        body, grid=(NG,), name="hgrn_fwd",
        in_specs=[blk_in(0), blk_in(NG), blk_in(2 * NG), blk_in(3 * NG), pl.BlockSpec((1, W), lambda g: (0, g)),
                  pl.BlockSpec((1, LANE), lambda g: (0, 0))],
        out_specs=[col, col, pl.BlockSpec((HG, NC, LANE, LANE), lambda g: (g, 0, 0, 0))],
        out_shape=[_sds((T, RW), BF16), _sds((T, RW), F32), _sds((RH, NC, LANE, LANE), F32)],
        scratch_shapes=[pltpu.VMEM((HG, LANE, LANE), F32), tile, tile, tile],
        compiler_params=_params(("parallel",), big=True),
    )(proj, proj, proj, proj, lb, gn)


def _hgrn_bwd(proj, dmixin, o_b, st_all, lb, gn, AW, RW):
    T = proj.shape[0]
    RH, NC, NSUB = RW // LANE, T // CHUNK, CHUNK // SUB
    HG, W, NG, blk_in, col = _hgrn_specs(T, RW, AW)

    def body(q_ref, f_ref, i_ref, g_ref, o_ref, dn_ref, stall_ref, lb_ref, gn_ref,
             dq_ref, df_ref, di_ref, dg_ref, dlb_ref, dgn_ref, dst_all, bs_all, qss_all, dos_all, p2_all, dic_all,
             p1_all):
        dst_all[...] = jnp.zeros_like(dst_all)
        dlb_ref[...] = jnp.zeros_like(dlb_ref)
        dgn_ref[...] = jnp.zeros_like(dgn_ref)
        ltri = _ltri()
        rowi = lax.broadcasted_iota(jnp.int32, (SUB, 1), 0)
        last = lax.broadcasted_iota(jnp.int32, (CHUNK, 1), 0) == CHUNK - 1

        def one_head(h, n):
            ls = slice(h * LANE, (h + 1) * LANE)
            dst, bs, qss, dos = dst_all.at[h], bs_all.at[h], qss_all.at[h], dos_all.at[h]
            p2, dic, p1s = p2_all.at[h], dic_all.at[h], p1_all.at[h]
            rows, lbv, qb, sg, f, kk, sq, qs, b = _hgrn_gates(n, ls, q_ref, f_ref, lb_ref, ltri)
            ic = i_ref[rows, ls]
            stv = stall_ref[h, n]
            dstv = dst[...]
            o = o_ref[rows, ls]
            dn = dn_ref[rows, ls]
            gb = g_ref[rows, ls]
            sgb = _sigmoid(gb)
            r = lax.rsqrt(jnp.mean(o * o, axis=-1, keepdims=True) + EPS)
            gnv = gn_ref[...]
            dg_ref[rows, ls] = (dn * (o * r * gnv) * (sgb * (1.0 + gb * (1.0 - sgb)))).astype(BF16)
            dy = dn * (gb * sgb)
            dgn_ref[h] += _colsum(dy * o * r)
            a_ = dy * gnv
            do = r * (a_ - o * (r * r) * jnp.mean(a_ * o, axis=-1, keepdims=True))
            do_b = do.astype(BF16)
            bs[...] = b
            qss[...] = qs
            dos[...] = do
            yield
            ic_b = ic.astype(BF16)
            eb = jnp.exp(b)
            bl = bs[CHUNK - 1:CHUNK, :]
            ebl = jnp.exp(bl)
            dec = jnp.exp(bl - b)
            kd = (kk * dec).astype(BF16)
            dst_b = dstv.astype(BF16)
            dqs = _dot(do_b, stv.astype(BF16), NN) * eb
            dkk2 = _dot(ic_b, dst_b, NN) * dec
            dic[...] = _dot(kd, dst_b, NT)
            dbl = ebl * _colsum(stv * dstv) + _colsum(kk * dkk2)
            dst[...] = dstv * ebl + _dot(do_b, (qs * eb).astype(BF16), TN)
            yield
            p2[...] = jnp.zeros_like(p2)
            p1_pieces = []
            for blk in range(NSUB):
                s0 = blk * SUB
                bI, qI, doI = b[s0:s0 + SUB], qs[s0:s0 + SUB], do[s0:s0 + SUB]
                if blk == 0:
                    p1 = jnp.zeros((SUB, LANE), F32)
                else:
                    ref = bs[s0 - 1:s0, :]
                    eq = jnp.exp(bI - ref)
                    ek = jnp.exp(ref - b[0:s0])
                    qt = (qI * eq).astype(BF16)
                    kt = (kk[0:s0] * ek).astype(BF16)
                    doI_b = doI.astype(BF16)
                    dic[0:s0, :] += _dot(_dot(qt, kt, NT).astype(BF16), doI_b, TN)
                    da = _dot(doI_b, ic_b[0:s0], NT).astype(BF16)
                    p1 = _dot(da, kt, NN) * eq
                    p2[0:s0, :] += _dot(da, qt, TN) * ek
                    yield
                p1_pieces.append(p1)
                kkI, icI = kk[s0:s0 + SUB], ic[s0:s0 + SUB]
                p2acc = [jnp.zeros((ROWS, LANE), F32) for _ in range(SUB // ROWS)]
                diacc = [jnp.zeros((ROWS, LANE), F32) for _ in range(SUB // ROWS)]
                for t in range(SUB):
                    tr = s0 + t
                    ng = t // ROWS + 1
                    hi = ng * ROWS
                    keep = rowi[:hi] <= t
                    do_t = dos[tr:tr + 1, :]
                    e = jnp.exp(jnp.minimum(bs[tr:tr + 1, :] - bI[:hi], 0.0))
                    qe = qss[tr:tr + 1, :] * e
                    a = jnp.where(keep, jnp.sum(kkI[:hi] * qe, axis=-1, keepdims=True), 0.0)
                    da = jnp.where(keep, jnp.sum(icI[:hi] * do_t, axis=-1, keepdims=True), 0.0)
                    dp2, ddi = da * qe, a * do_t
                    for g in range(ng):
                        p2acc[g] = p2acc[g] + dp2[g * ROWS:(g + 1) * ROWS]
                        diacc[g] = diacc[g] + ddi[g * ROWS:(g + 1) * ROWS]
                    p1s[tr:tr + 1, :] = _colsum(da * kkI[:hi] * e)
                    yield
                p2[s0:s0 + SUB, :] += jnp.concatenate(p2acc, axis=0)
                dic[s0:s0 + SUB, :] += jnp.concatenate(diacc, axis=0)
            dqs = dqs + jnp.concatenate(p1_pieces, axis=0) + p1s[...]
            dkk = dkk2 + p2[...]
            db = qs * dqs - kk * dkk + jnp.where(last, dbl, 0.0)
            dgl = _tri_dot(ltri, db, TN)
            yield
            dfv = dgl / f - dkk
            df_ref[rows, ls] = (dfv * (1.0 - lbv) * sg * (1.0 - sg)).astype(BF16)
            dlb_ref[:, ls] += _colsum(dfv * (1.0 - sg))
            dq_ref[rows, ls] = (dqs * (sq * (1.0 + qb * (1.0 - sq)))).astype(BF16)
            di_ref[rows, ls] = dic[...].astype(BF16)

        def chunk(k, carry):
            _alternate([one_head(h, NC - 1 - k) for h in range(HG)])
            return carry

        lax.fori_loop(0, NC, chunk, 0)

    tile = pltpu.VMEM((HG, CHUNK, LANE), F32)
    return pl.pallas_call(
        body, grid=(NG,), name="hgrn_bwd",
        in_specs=[blk_in(0), blk_in(NG), blk_in(2 * NG), blk_in(3 * NG), col,
                  pl.BlockSpec((T, W), lambda g: (0, AW // W + g)),
                  pl.BlockSpec((HG, NC, LANE, LANE), lambda g: (g, 0, 0, 0)),
                  pl.BlockSpec((1, W), lambda g: (0, g)), pl.BlockSpec((1, LANE), lambda g: (0, 0))],
        out_specs=[col, col, col, col, pl.BlockSpec((1, W), lambda g: (0, g)),
                   pl.BlockSpec((HG, 1, LANE), lambda g: (g, 0, 0))],
        out_shape=[_sds((T, RW), BF16)] * 4 + [_sds((1, RW), F32), _sds((RH, 1, LANE), F32)],
        scratch_shapes=[pltpu.VMEM((HG, LANE, LANE), F32), tile, tile, tile, tile, tile, tile],
        compiler_params=_params(("parallel",), big=True),
    )(proj, proj, proj, proj, o_b, dmixin, st_all, lb, gn)


def _prep(c, lb_logits, rb_pad, max_rel):
    D, RW = c.shape[-1], lb_logits.shape[-1]
    H, rbp = rb_pad.shape

    def body(c_ref, l_ref, rb_ref, cact_ref, lb_ref, gv_ref):
        cv = c_ref[...]
        cact_ref[...] = cv * _sigmoid(cv)
        lb_ref[...] = _sigmoid(l_ref[0:1, :] - l_ref[1:2, :])
        gv_ref[...] = _dot(rb_ref[...], _bias_onehot(rbp, max_rel), NN, HIGHEST)

    return pl.pallas_call(
        body, name="prep", out_shape=[_sds((1, D), F32), _sds((1, RW), F32), _sds((H, TAB), F32)],
    )(c, lb_logits, rb_pad)


def _mod_part(c_all, w_ada_s, b_ada_s):
    B, D = c_all.shape
    ns = w_ada_s.shape[1]
    tn = _tile(ns, 768, LANE)

    def body(c_ref, w_ref, b_ref, o_ref):
        o_ref[...] = _dot(c_ref[...], w_ref[...], NN) + b_ref[...]

    return pl.pallas_call(
        body, grid=(ns // tn,), name="mod_part",
        in_specs=[pl.BlockSpec((B, D), lambda j: (0, 0)), pl.BlockSpec((D, tn), lambda j: (0, j)),
                  pl.BlockSpec((1, tn), lambda j: (0, j))],
        out_specs=pl.BlockSpec((B, tn), lambda j: (0, j)),
        out_shape=_sds((B, ns), F32), compiler_params=_params(("parallel",)),
    )(c_all, w_ada_s, b_ada_s)


def _adam(w, g, m, v):
    m = ADAM_B1 * m + (1.0 - ADAM_B1) * g
    v = ADAM_B2 * v + (1.0 - ADAM_B2) * (g * g)
    m_hat = m * (1.0 / (1.0 - ADAM_B1 ** ADAM_STEP))
    v_hat = v * (1.0 / (1.0 - ADAM_B2 ** ADAM_STEP))
    return -ADAM_LR * (m_hat / (jnp.sqrt(v_hat) + ADAM_EPS) + ADAM_WD * w), m, v


def _adam_ada(c_all, dmod_s, w, m, v):
    B, D = c_all.shape
    ns = w.shape[1]
    tr, tn = _tile(D, 512, LANE), _tile(ns, 768, LANE)

    def body(c_ref, d_ref, w_ref, m_ref, v_ref, g_out, dw_out, m_out, v_out):
        g = _dot(c_ref[...], d_ref[...], TN)
        g_out[...] = g
        dw_out[...], m_out[...], v_out[...] = _adam(w_ref[...], g, m_ref[...], v_ref[...])

    big = pl.BlockSpec((tr, tn), lambda i, j: (i, j))
    return pl.pallas_call(
        body, grid=(D // tr, ns // tn), name="adam_w_ada",
        in_specs=[pl.BlockSpec((B, tr), lambda i, j: (0, i)), pl.BlockSpec((B, tn), lambda i, j: (0, j)),
                  big, big, big],
        out_specs=[big] * 4, out_shape=[_sds((D, ns), F32)] * 4,
        compiler_params=_params(("parallel", "parallel")),
    )(c_all, dmod_s, w, m, v)


def _adam_shard(parts, w, m, v, name):
    R, C = w.shape
    tr = _tile(R, 256, 16)

    def body(p_ref, w_ref, m_ref, v_ref, g_out, dw_out, m_out, v_out):
        g = p_ref[0].astype(F32)
        for k in range(1, N_DEV // 2):
            g = g + p_ref[k].astype(F32)
        g_out[...] = g
        dw_out[...], m_out[...], v_out[...] = _adam(w_ref[...], g, m_ref[...], v_ref[...])

    big = pl.BlockSpec((tr, C), lambda i: (i, 0))
    return pl.pallas_call(
        body, grid=(R // tr,), name=name,
        in_specs=[pl.BlockSpec((N_DEV // 2, tr, C), lambda i: (0, i, 0)), big, big, big],
        out_specs=[big] * 4, out_shape=[_sds((R, C), F32)] * 4,
        compiler_params=_params(("parallel",), big=True),
    )(parts, w, m, v)


def _pair_sum(g8, land, core, name):
    _, NCHIP, R, C = g8.shape
    tr = _tile(R, 1024, 16)

    def body(core_ref, g_ref, l_ref, o_ref):
        o_ref[...] = g_ref[...] + l_ref[...]

    return pl.pallas_call(
        body, name=name,
        grid_spec=pltpu.PrefetchScalarGridSpec(
            num_scalar_prefetch=1, grid=(NCHIP, R // tr),
            in_specs=[pl.BlockSpec((None, None, tr, C), lambda k, i, core_ref: (core_ref[0], k, i, 0)),
                      pl.BlockSpec((None, tr, C), lambda k, i, core_ref: (k, i, 0))],
            out_specs=pl.BlockSpec((None, tr, C), lambda k, i, core_ref: (k, i, 0))),
        out_shape=_sds((NCHIP, R, C), BF16), compiler_params=_params(("parallel", "parallel")),
    )(core, g8, land)


SMALL = ("b_ada", "rel_bias", "attn_norm_g", "lb_logits", "gnorm_g", "ln1_g", "ln1_b", "ln2_g", "ln2_b")


def _small_update(parts, loss_parts, lbv, ws, ms, vs, max_rel):
    n = len(SMALL)

    def body(*refs):
        part_refs = dict(zip(SMALL, refs[:n]))
        loss_in, lb_ref = refs[n], refs[n + 1]
        w_refs, m_refs, v_refs = refs[n + 2:2 * n + 2], refs[2 * n + 2:3 * n + 2], refs[3 * n + 2:4 * n + 2]
        outs = refs[4 * n + 2:]

        def total(ref):
            tot = ref[0]
            for k in range(1, N_DEV):
                tot = tot + ref[k]
            return tot

        outs[0][...] = jnp.sum(total(loss_in), axis=-1, keepdims=True)
        for idx, name in enumerate(SMALL):
            g = total(part_refs[name])
            if name == "rel_bias":
                g = _dot(g, _bias_onehot(w_refs[idx].shape[1], max_rel), NT, HIGHEST)
            elif name == "lb_logits":
                lb = lb_ref[...]
                sign = (1 - 2 * lax.broadcasted_iota(jnp.int32, (2, 1), 0)).astype(F32)
                g = sign * (g * lb * (1.0 - lb))
            elif name == "gnorm_g":
                g = _colsum(g)
            dw, mm, vv = _adam(w_refs[idx][...], g, m_refs[idx][...], v_refs[idx][...])
            outs[1 + 4 * idx][...] = g
            outs[2 + 4 * idx][...] = dw
            outs[3 + 4 * idx][...] = mm
            outs[4 + 4 * idx][...] = vv

    out_shape = [_sds((1, 1), F32)]
    for w in ws:
        out_shape += [_sds(w.shape, F32)] * 4
    return pl.pallas_call(body, name="small_update", out_shape=out_shape, compiler_params=_params(big=True))(
        *[parts[k] for k in SMALL], loss_parts, lbv, *ws, *ms, *vs)


def _place():
    x, y, c = lax.axis_index("x"), lax.axis_index("y"), lax.axis_index("c")
    return x, y, c, [(1 - x, y), (x, 1 - y), (1 - x, 1 - y)]


def _all_gather(shard, name):
    HBM = pl.BlockSpec(memory_space=pl.ANY)

    def body(x_ref, out_ref, send_sems, recv_sems, local_sem):
        x, y, c, chips = _place()
        me, sibling = (x, y, c), (x, y, 1 - c)

        def slot(px, py, pc):
            return out_ref.at[4 * px + 2 * py + pc]

        def copy(k, block, to, src=None):
            return pltpu.make_async_remote_copy(
                src_ref=slot(*block) if src is None else src, dst_ref=slot(*block),
                send_sem=send_sems.at[k], recv_sem=recv_sems.at[k], device_id=to, device_id_type=MESH)

        mine = pltpu.make_async_copy(x_ref, slot(*me), local_sem)
        mine.start()
        first = [copy(0, me, sibling, src=x_ref)]
        first += [copy(1 + j, me, (*chip, c), src=x_ref) for j, chip in enumerate(chips)]
        for cp in first:
            cp.start()
        passed = [copy(4 + j, (*chip, c), sibling) for j, chip in enumerate(chips)]
        for j, chip in enumerate(chips):
            copy(1 + j, (*chip, c), me).wait_recv()
            passed[j].start()
        copy(0, sibling, me).wait_recv()
        for j, chip in enumerate(chips):
            copy(4 + j, (*chip, 1 - c), me).wait_recv()
        for cp in first + passed:
            cp.wait_send()
        mine.wait()

    return pl.pallas_call(
        body, name=name, out_shape=_sds((N_DEV,) + shard.shape, shard.dtype),
        in_specs=[HBM], out_specs=HBM,
        scratch_shapes=[pltpu.SemaphoreType.DMA((7,)), pltpu.SemaphoreType.DMA((7,)), pltpu.SemaphoreType.DMA(())],
    )(shard)


SEM_SPEC = pl.BlockSpec(memory_space=pltpu.SEMAPHORE)
HBM_SPEC = pl.BlockSpec(memory_space=pltpu.HBM)
EFFECT = pltpu.SideEffectType.DATAFLOW_SIDE_EFFECTING


def _remote(src, dst, send_sems, recv_sems, k, dev):
    return pltpu.make_async_remote_copy(src_ref=src, dst_ref=dst, send_sem=send_sems.at[k], recv_sem=recv_sems.at[k],
                                        device_id=dev, device_id_type=MESH)


def _copy_start(name, bufs, plan, n, after, only=None):
    nb = len(bufs)

    def body(*refs):
        send_sems, recv_sems = refs[nb + 1], refs[nb + 2]
        for k, (src, dst, dev) in enumerate(plan(*refs[:nb])):
            if only is not None and k not in only:
                continue
            _remote(src, dst, send_sems, recv_sems, k, dev).start()
        refs[-1][...] = jnp.zeros_like(refs[-1])

    out = pl.pallas_call(
        body, name=name,
        out_shape=(pltpu.SemaphoreType.DMA((n,)), pltpu.SemaphoreType.DMA((n,)),
                   *[pltpu.HBM(b.shape, b.dtype) for b in bufs], _sds((8, LANE), F32)),
        in_specs=[HBM_SPEC] * nb + [ORDER_ONLY],
        out_specs=(SEM_SPEC, SEM_SPEC, *[HBM_SPEC] * nb, pl.BlockSpec(memory_space=pltpu.VMEM)),
        input_output_aliases={i: 2 + i for i in range(nb)},
        compiler_params=pltpu.CompilerParams(has_side_effects=EFFECT),
    )(*[pltpu.with_memory_space_constraint(b, pltpu.HBM) for b in bufs], after)
    return (out[0], out[1]), list(out[2:2 + nb]), out[-1]


def _copy_wait(name, sems, bufs, plan, after, only=None):
    nb = len(bufs)

    def body(*refs):
        send_sems, recv_sems = refs[nb], refs[nb + 1]
        for k, (src, dst, dev) in enumerate(plan(*refs[:nb])):
            if only is not None and k not in only:
                continue
            cp = _remote(src, dst, send_sems, recv_sems, k, dev)
            cp.wait_send()
            cp.wait_recv()

    out = pl.pallas_call(
        body, name=name, out_shape=tuple(pltpu.HBM(b.shape, b.dtype) for b in bufs),
        in_specs=[HBM_SPEC] * nb + [SEM_SPEC, SEM_SPEC, pl.BlockSpec(memory_space=pl.ANY)],
        out_specs=tuple([HBM_SPEC] * nb), input_output_aliases={i: i for i in range(nb)},
        compiler_params=pltpu.CompilerParams(has_side_effects=EFFECT),
    )(*bufs, sems[0], sems[1], after)
    return list(out)


def _ag_plan_chips(shard_ref, out_ref):
    x, y, c, chips = _place()
    mine = out_ref.at[4 * x + 2 * y + c]
    return [(shard_ref, mine, (x, y, 1 - c))] + [(shard_ref, mine, (*chip, c)) for chip in chips]


def _ag_plan_pass(out_ref):
    x, y, c, chips = _place()
    slots = [out_ref.at[4 * chip[0] + 2 * chip[1] + c] for chip in chips]
    return [(s, s, (x, y, 1 - c)) for s in slots]


def _rs_plan_pair(g_ref, land_ref):
    x, y, c, _ = _place()
    return [(g_ref.at[1 - c], land_ref, (x, y, 1 - c))]


def _rs_plan_chips(p_ref, land_ref):
    x, y, c, chips = _place()
    return [(p_ref.at[2 * chip[0] + chip[1]], land_ref.at[2 * x + y], (*chip, c)) for chip in chips]


class _Gather:
    def __init__(self, shard, me, tag, after):
        self.tag = tag
        out = lax.dynamic_update_slice(lax.empty((N_DEV,) + shard.shape, shard.dtype), shard[None],
                                       (me,) + (0,) * shard.ndim)
        self.sems, (self.shard, self.out), self.token = _copy_start(
            "ag_start_" + tag, [shard, out], _ag_plan_chips, 4, after)
        self.groups = []

    def arrived(self, after, copies):
        name = "ag_wait_%s_%s" % (self.tag, "".join(map(str, copies)))
        self.shard, self.out = _copy_wait(name, self.sems, [self.shard, self.out], _ag_plan_chips, after, copies)
        return self.out

    def pass_on(self, after, blocks):
        name = "ag_pass_%s_%s" % (self.tag, "".join(map(str, blocks)))
        sems, (self.out,), _ = _copy_start(name, [self.out], _ag_plan_pass, 3, after, blocks)
        self.groups.append((sems, blocks))
        return self.out

    def passed(self, after, group):
        sems, blocks = self.groups[group]
        name = "ag_pass_wait_%s_%s" % (self.tag, "".join(map(str, blocks)))
        self.out = _copy_wait(name, sems, [self.out], _ag_plan_pass, after, blocks)[0]
        return self.out

    def arrived_from_chips(self, after):
        self.arrived(after, (0, 1, 2, 3))
        return self.pass_on(after, (0, 1, 2))

    def passed_on(self, after):
        return self.passed(after, 0)


def _ag_plan_direct(src_ref, out_ref):
    x, y, c, chips = _place()
    mine = out_ref.at[4 * x + 2 * y + c]
    peers = [(x, y, 1 - c)] + [(*chip, pc) for chip in chips for pc in (c, 1 - c)]
    return [(src_ref, mine, peer) for peer in peers]


class _SmallGather:
    def __init__(self, block, me, tag):
        self.tag = tag
        out = lax.dynamic_update_slice(lax.empty((N_DEV,) + block.shape, block.dtype), block[None],
                                       (me,) + (0,) * block.ndim)
        self.sems, self.bufs, self.token = _copy_start(
            "ag_direct_start_" + tag, [block, out], _ag_plan_direct, N_DEV - 1, jnp.zeros((1,), F32))

    def done(self, after):
        return _copy_wait("ag_direct_wait_" + self.tag, self.sems, self.bufs, _ag_plan_direct, after)[1]


class _ReduceScatter:
    def __init__(self, g8, tag):
        self.tag = tag
        land = lax.empty(g8.shape[1:], g8.dtype)
        self.sems, self.bufs, self.token = _copy_start(
            "rs_pair_start_" + tag, [g8, land], _rs_plan_pair, 1, jnp.zeros((1,), F32))

    def pair_done(self, core, chip, after, start_after=None):
        g8, land = _copy_wait("rs_pair_wait_" + self.tag, self.sems, self.bufs, _rs_plan_pair, after)
        p4 = _pair_sum(g8, land, core, "rs_pair_sum_" + self.tag)
        own = lax.dynamic_slice_in_dim(p4, chip, 1, axis=0)
        land2 = lax.dynamic_update_slice(lax.empty(p4.shape, p4.dtype), own, (chip, 0, 0))
        self.sems, self.bufs, self.token = _copy_start(
            "rs_chips_start_" + self.tag, [p4, land2], _rs_plan_chips, 3,
            jnp.zeros((1,), F32) if start_after is None else start_after)

    def sums(self, after):
        return _copy_wait("rs_chips_wait_" + self.tag, self.sems, self.bufs, _rs_plan_chips, after)[1]


BIG = ("w_in", "w_o", "w_ffn_in", "w_ffn_out")
ORDER = ("w_ada", "b_ada", "w_in", "rel_bias", "attn_norm_g", "lb_logits", "gnorm_g", "w_o", "ln1_g", "ln1_b",
         "w_ffn_in", "w_ffn_out", "ln2_g", "ln2_b")


def kernel(x, c, w_ada, b_ada, w_in, rel_bias, attn_norm_g, lb_logits, gnorm_g, w_o, ln1_g, ln1_b, w_ffn_in, w_ffn_out, ln2_g, ln2_b, loss_target, m_w_ada, m_b_ada, m_w_in, m_rel_bias, m_attn_norm_g, m_lb_logits, m_gnorm_g, m_w_o, m_ln1_g, m_ln1_b, m_w_ffn_in, m_w_ffn_out, m_ln2_g, m_ln2_b, v_w_ada, v_b_ada, v_w_in, v_rel_bias, v_attn_norm_g, v_lb_logits, v_gnorm_g, v_w_o, v_ln1_g, v_ln1_b, v_w_ffn_in, v_w_ffn_out, v_ln2_g, v_ln2_b):
    W = dict(w_ada=w_ada, b_ada=b_ada, w_in=w_in, rel_bias=rel_bias, attn_norm_g=attn_norm_g, lb_logits=lb_logits,
             gnorm_g=gnorm_g, w_o=w_o, ln1_g=ln1_g, ln1_b=ln1_b, w_ffn_in=w_ffn_in, w_ffn_out=w_ffn_out,
             ln2_g=ln2_g, ln2_b=ln2_b)
    M = dict(w_ada=m_w_ada, b_ada=m_b_ada, w_in=m_w_in, rel_bias=m_rel_bias, attn_norm_g=m_attn_norm_g,
             lb_logits=m_lb_logits, gnorm_g=m_gnorm_g, w_o=m_w_o, ln1_g=m_ln1_g, ln1_b=m_ln1_b,
             w_ffn_in=m_w_ffn_in, w_ffn_out=m_w_ffn_out, ln2_g=m_ln2_g, ln2_b=m_ln2_b)
    V = dict(w_ada=v_w_ada, b_ada=v_b_ada, w_in=v_w_in, rel_bias=v_rel_bias, attn_norm_g=v_attn_norm_g,
             lb_logits=v_lb_logits, gnorm_g=v_gnorm_g, w_o=v_w_o, ln1_g=v_ln1_g, ln1_b=v_ln1_b,
             w_ffn_in=v_w_ffn_in, w_ffn_out=v_w_ffn_out, ln2_g=v_ln2_g, ln2_b=v_ln2_b)

    x2, tgt = x[0], loss_target[0]
    T, D = x2.shape
    AW, RW = attn_norm_g.shape[-1], lb_logits.shape[-1]
    MIX = AW + RW
    H, RH = AW // ATTN_HEAD_DIM, RW // LANE
    RB = rel_bias.shape[-1]
    max_rel = (RB - 1) // 2
    rbp = -(-RB // LANE) * LANE
    F = w_ffn_out.shape[1] * N_DEV
    half = N_DEV // 2
    xi, yi, ci = lax.axis_index("x"), lax.axis_index("y"), lax.axis_index("c")
    me = 4 * xi + 2 * yi + ci
    core = jnp.reshape(ci, (1,)).astype(jnp.int32)
    pad_rb = lambda a: jnp.pad(a[0], ((0, 0), (0, rbp - RB)))

    chip = 2 * xi + yi

    c_act, lbv, gv = _prep(c, lb_logits, pad_rb(rel_bias), max_rel)
    c_all = _all_gather(c_act, "ag_c").reshape(N_DEV, D)
    ns_ada = w_ada.shape[-1]
    mod_part = _mod_part(c_all, w_ada[0], lax.dynamic_slice_in_dim(b_ada, me * ns_ada, ns_ada, axis=1))
    mod_all = _all_gather(mod_part, "ag_mod")
    mod6 = lax.dynamic_index_in_dim(mod_all, me, axis=1, keepdims=False).reshape(6, D)

    ag_in = _Gather(w_in[0].astype(BF16), me, "w_in", mod_all)
    ag_o = _Gather(w_o[0].astype(BF16), me, "w_o", ag_in.token)
    ag_f1 = _Gather(w_ffn_in[0].astype(BF16), me, "w_ffn_in", ag_o.token)
    ag_f2 = _Gather(w_ffn_out[0].astype(BF16), me, "w_ffn_out", ag_f1.token)

    h1 = _ln_mod(x2, mod6 + ag_f2.token[0, 0])
    ids = lambda pairs: jnp.stack([4 * px + 2 * py + pc for px, py, pc in pairs]).astype(jnp.int32)
    others = [(1 - xi, yi), (xi, 1 - yi), (1 - xi, 1 - yi)]
    proj = lax.empty((T, w_in.shape[-1] * N_DEV), F32)
    proj = _mm_gathered(h1, ag_in.arrived(h1, (0,)), ids([(xi, yi, ci), (xi, yi, 1 - ci)]), proj, "in_proj_a")
    ag_in.arrived(proj, (1, 2, 3))
    proj = _mm_gathered(h1, ag_in.pass_on(proj, (0, 1, 2)), ids([(*ch, ci) for ch in others]), proj, "in_proj_b")
    wg_in = ag_in.passed(proj, 0)
    proj = _mm_gathered(h1, wg_in, ids([(*ch, 1 - ci) for ch in others]), proj, "in_proj_c")
    ag_o.arrived_from_chips(proj)
    mix_a = _attn_fwd(proj, gv, attn_norm_g, AW)
    wg_o = ag_o.passed_on(mix_a).reshape(MIX, D)
    mix_b, o_b, st_all = _hgrn_fwd(proj, lbv, gnorm_g, AW, RW)
    mixin = jnp.concatenate([mix_a, mix_b], axis=1)
    mix = _mm_nn(mixin, wg_o, "out_proj")
    ag_f1.arrived_from_chips(mix)
    x1, h2 = _mid_fwd(x2, mix, mod6, ln1_g, ln1_b)
    wg_f1 = ag_f1.passed_on(h2)
    gu, act = _mm_swiglu(h2, wg_f1)
    ag_f2.arrived_from_chips(act)
    wg_f2 = ag_f2.passed_on(act).reshape(F, D)
    ff = _mm_nn(act, wg_f2, "ffn_out")
    dff, dx1a, vec_a = _final(x1, ff, mod6, ln2_g, ln2_b, tgt)

    du = _mm_swiglu_bwd(dff, wg_f2, gu)
    rs_f2 = _ReduceScatter(_mm_tn_rows(dff, act, dff, F // N_DEV, "grad_w_ffn_out"), "w_ffn_out")
    tm = _tile(T, 512, 16)
    du_ij = lambda tm_, w, first: pl.BlockSpec((None, tm_, w), lambda i, p: (p // (half // 2), i + first, p % (half // 2)))
    du_jm = lambda tm_, ns: pl.BlockSpec((None, tm_, ns), lambda j, m: (j // half, m, j % half))
    dh2 = _mm_gathered_nt(rs_f2.token, du, du_ij, wg_f1, T, tm, "ffn_in_bwd")
    rs_f2.pair_done(core, chip, dh2)
    tm_red = _tile(T, 1024, 16)
    gw_f1 = _mm_tn_gathered(rs_f2.token, h2, du, du_jm, wg_f1.shape[-1], tm_red, "grad_w_ffn_in")
    rs_f1 = _ReduceScatter(gw_f1.reshape(2, half, D, -1), "w_ffn_in")
    dmix, dxa, vec_b = _mid_bwd(x2, mix, x1, dx1a, dh2, mod6 + rs_f1.token[0, 0], ln1_g)
    dmixin = _mm_nt(dmix, wg_o, "out_proj_bwd")
    rs_f1.pair_done(core, chip, dmixin)
    rs_o = _ReduceScatter(_mm_tn_rows(rs_f1.token, mixin, dmix, MIX // N_DEV, "grad_w_o"), "w_o")
    dq, dk, dv, dgv, dga = _attn_bwd(proj, dmixin, gv + rs_o.token[0, 0], attn_norm_g, AW)
    rs_o.pair_done(core, chip, dq)
    dqb, dfl, dib, dgb, dlb, dgn = _hgrn_bwd(proj, dmixin, o_b, st_all, lbv + rs_o.token[0, 0], gnorm_g, AW, RW)
    dproj = jnp.concatenate([dq, dk, dv, dqb, dfl, dib, dgb], axis=1)
    p_ij = lambda tm_, w, first: pl.BlockSpec((tm_, w), lambda i, p: (i + first, p))
    p_jm = lambda tm_, ns: pl.BlockSpec((tm_, ns), lambda j, m: (m, j))
    gw_in = _mm_tn_gathered(rs_o.token, h1, dproj, p_jm, wg_in.shape[-1], tm_red, "grad_w_in")
    rs_in = _ReduceScatter(gw_in.reshape(2, half, D, -1), "w_in")
    n_tiles = T // tm
    dh1 = _mm_gathered_nt(rs_in.token, dproj, p_ij, wg_in, T, tm, "in_proj_bwd_a", 0, n_tiles // 2)
    rs_in.pair_done(core, chip, dh1)
    dh1 = _mm_gathered_nt(rs_in.token, dproj, p_ij, wg_in, T, tm, "in_proj_bwd_b", n_tiles // 2,
                          n_tiles - n_tiles // 2, dh1)
    grad_x, vec_c = _first_bwd(x2, dh1, dxa, mod6)

    dmod = jnp.concatenate([vec_c[1:2], vec_c[0:1], vec_b[4:5], vec_b[1:2], vec_b[0:1], vec_a[2:3]], axis=0)
    pieces = dict(b_ada=dmod, rel_bias=dgv, attn_norm_g=dga, lb_logits=dlb, gnorm_g=dgn, ln1_g=vec_b[2:3],
                  ln1_b=vec_b[3:4], ln2_g=vec_a[0:1], ln2_b=vec_a[1:2], loss=vec_a[3:4])
    widths = dict(b_ada=(1, 6 * D), rel_bias=(H, TAB), attn_norm_g=(1, AW), lb_logits=(1, RW), gnorm_g=(RH, LANE),
                  ln1_g=(1, D), ln1_b=(1, D), ln2_g=(1, D), ln2_b=(1, D), loss=(1, D))
    packed = jnp.concatenate([pieces[k].reshape(-1, LANE) for k in widths], axis=0)
    small_ag = _SmallGather(packed, me, "small")
    after, res_big = small_ag.token, {}
    for k, rs in (("w_ffn_out", rs_f2), ("w_ffn_in", rs_f1), ("w_o", rs_o), ("w_in", rs_in)):
        four = _adam_shard(rs.sums(after), W[k][0], M[k][0], V[k][0], "adam_" + k)
        res_big[k] = [a[None] for a in four]
        after = four[0]
    gathered = small_ag.done(after)
    parts, r0 = {}, 0
    for k, (rows, width) in widths.items():
        nr = rows * width // LANE
        parts[k] = gathered[:, r0:r0 + nr, :].reshape(N_DEV, rows, width)
        r0 += nr
    prep_small = lambda d, k: pad_rb(d[k]) if k == "rel_bias" else d[k]
    small = _small_update(parts, parts["loss"], lbv, [prep_small(W, k) for k in SMALL],
                          [prep_small(M, k) for k in SMALL], [prep_small(V, k) for k in SMALL], max_rel)
    loss = small[0].reshape(())
    res = {}
    for idx, k in enumerate(SMALL):
        four = small[1 + 4 * idx:5 + 4 * idx]
        if k == "rel_bias":
            four = [a[:, :RB][None] for a in four]
        res[k] = list(four)

    res.update(res_big)
    dmod_s = lax.dynamic_slice_in_dim(parts["b_ada"].reshape(N_DEV, 6 * D), me * ns_ada, ns_ada, axis=1)
    res["w_ada"] = [a[None] for a in _adam_ada(c_all, dmod_s, w_ada[0], m_w_ada[0], v_w_ada[0])]

    out = [loss, grad_x[None]]
    for field in range(4):
        out += [res[k][field] for k in ORDER]
    return tuple(out)
```

```python
import functools

import jax
import jax.numpy as jnp
from jax import lax
from jax.experimental import pallas as pl
from jax.experimental.pallas import tpu as pltpu

F32 = jnp.float32
BF16 = jnp.bfloat16
MESH = pl.DeviceIdType.MESH
HIGHEST = lax.Precision.HIGHEST

N_DEV = 8
CHUNK = 64
N_PAST = 8
QBLK = 4 * CHUNK
KPAD = N_PAST * CHUNK
WIN = KPAD + QBLK
TAB = 1024
ATTN_HEAD_DIM = 64
ATTN_HEADS_PER_STEP = 4
REC_HEAD_DIM = 128
SUB = 32
ROWS = 8
LANE = 128
EPS = 1e-5
ALPHA = 2.0 ** 0.25
ADAM_LR, ADAM_B1, ADAM_B2, ADAM_EPS, ADAM_WD, ADAM_STEP = 0.001, 0.9, 0.999, 1e-08, 0.01, 10
NEG = -1e30
VMEM_LIMIT = 56 * 1024 * 1024


def _sds(shape, dtype):
    return jax.ShapeDtypeStruct(tuple(shape), dtype)


def _tile(n, pref, mult):
    best = None
    for t in range(mult, min(n, pref) + 1, mult):
        if n % t == 0:
            best = t
    return n if best is None else best


def _params(sem=None, big=False):
    kw = {}
    if sem is not None:
        kw["dimension_semantics"] = sem
    if big:
        kw["vmem_limit_bytes"] = VMEM_LIMIT
    return pltpu.CompilerParams(**kw)


def _sigmoid(v):
    return 1.0 / (1.0 + jnp.exp(-v))


def _dot(a, b, dims, precision=None):
    return lax.dot_general(a, b, (dims, ((), ())), preferred_element_type=F32, precision=precision)


NN = ((1,), (0,))
NT = ((1,), (1,))
TN = ((0,), (0,))


def _ln(v):
    mu = jnp.mean(v, axis=-1, keepdims=True)
    d = v - mu
    rstd = lax.rsqrt(jnp.mean(d * d, axis=-1, keepdims=True) + EPS)
    return d * rstd, rstd


def _ln_bwd(dxh, xh, rstd):
    return rstd * (dxh - jnp.mean(dxh, axis=-1, keepdims=True) - xh * jnp.mean(dxh * xh, axis=-1, keepdims=True))


def _colsum(v):
    return jnp.sum(v, axis=0, keepdims=True)


def _ln_mod(x2, mod6):
    T, D = x2.shape
    tm = _tile(T, 256, 8)

    def body(x_ref, mod_ref, o_ref):
        xh, _ = _ln(x_ref[...])
        o_ref[...] = (xh * (1.0 + mod_ref[1:2, :]) + mod_ref[0:1, :]).astype(BF16)

    return pl.pallas_call(
        body, grid=(T // tm,), name="ln_mod",
        in_specs=[pl.BlockSpec((tm, D), lambda i: (i, 0)), pl.BlockSpec((6, D), lambda i: (0, 0))],
        out_specs=pl.BlockSpec((tm, D), lambda i: (i, 0)),
        out_shape=_sds((T, D), BF16), compiler_params=_params(("parallel",)),
    )(x2, mod6)


def _mid_fwd(x2, mix, mod6, ln1_g, ln1_b):
    T, D = x2.shape
    tm = _tile(T, 256, 8)

    def body(x_ref, mix_ref, mod_ref, g_ref, b_ref, x1_ref, h2_ref):
        zh, _ = _ln(ALPHA * x_ref[...] + mod_ref[2:3, :] * mix_ref[...])
        x1 = zh * g_ref[...] + b_ref[...]
        x1_ref[...] = x1
        xh, _ = _ln(x1)
        h2_ref[...] = (xh * (1.0 + mod_ref[4:5, :]) + mod_ref[3:4, :]).astype(BF16)

    row = pl.BlockSpec((tm, D), lambda i: (i, 0))
    vec = pl.BlockSpec((1, D), lambda i: (0, 0))
    return pl.pallas_call(
        body, grid=(T // tm,), name="mid_fwd",
        in_specs=[row, row, pl.BlockSpec((6, D), lambda i: (0, 0)), vec, vec],
        out_specs=[row, row],
        out_shape=[_sds((T, D), F32), _sds((T, D), BF16)], compiler_params=_params(("parallel",)),
    )(x2, mix, mod6, ln1_g, ln1_b)


def _final(x1, ff, mod6, ln2_g, ln2_b, tgt):
    T, D = x1.shape
    tm = _tile(T, 256, 8)

    def body(x1_ref, ff_ref, mod_ref, g_ref, b_ref, t_ref, dff_ref, dx1_ref, vec_ref):
        @pl.when(pl.program_id(0) == 0)
        def _():
            vec_ref[...] = jnp.zeros_like(vec_ref)

        ff_v = ff_ref[...]
        gate2 = mod_ref[5:6, :]
        zh, rstd = _ln(ALPHA * x1_ref[...] + gate2 * ff_v)
        err = zh * g_ref[...] + b_ref[...] - t_ref[...]
        dy = err * (1.0 / D)
        dz = _ln_bwd(dy * g_ref[...], zh, rstd)
        dff_ref[...] = (gate2 * dz).astype(BF16)
        dx1_ref[...] = ALPHA * dz
        vec_ref[0:1, :] += _colsum(dy * zh)
        vec_ref[1:2, :] += _colsum(dy)
        vec_ref[2:3, :] += _colsum(dz * ff_v)
        vec_ref[3:4, :] += _colsum(err * err) * (0.5 / D)

    row = pl.BlockSpec((tm, D), lambda i: (i, 0))
    vec = pl.BlockSpec((1, D), lambda i: (0, 0))
    return pl.pallas_call(
        body, grid=(T // tm,), name="final_fwd_bwd",
        in_specs=[row, row, pl.BlockSpec((6, D), lambda i: (0, 0)), vec, vec, row],
        out_specs=[row, row, pl.BlockSpec((8, D), lambda i: (0, 0))],
        out_shape=[_sds((T, D), BF16), _sds((T, D), F32), _sds((8, D), F32)],
        compiler_params=_params(("arbitrary",)),
    )(x1, ff, mod6, ln2_g, ln2_b, tgt)


def _mid_bwd(x2, mix, x1, dx1a, dh2, mod6, ln1_g):
    T, D = x2.shape
    tm = _tile(T, 256, 8)

    def body(x_ref, mix_ref, x1_ref, dx1a_ref, dh2_ref, mod_ref, g_ref, dmix_ref, dxa_ref, vec_ref):
        @pl.when(pl.program_id(0) == 0)
        def _():
            vec_ref[...] = jnp.zeros_like(vec_ref)

        dh2 = dh2_ref[...]
        xh, rstd = _ln(x1_ref[...])
        dx1 = dx1a_ref[...] + _ln_bwd(dh2 * (1.0 + mod_ref[4:5, :]), xh, rstd)
        mix_v = mix_ref[...]
        gate1 = mod_ref[2:3, :]
        zh, rstdz = _ln(ALPHA * x_ref[...] + gate1 * mix_v)
        dz = _ln_bwd(dx1 * g_ref[...], zh, rstdz)
        dmix_ref[...] = (gate1 * dz).astype(BF16)
        dxa_ref[...] = ALPHA * dz
        vec_ref[0:1, :] += _colsum(dh2 * xh)
        vec_ref[1:2, :] += _colsum(dh2)
        vec_ref[2:3, :] += _colsum(dx1 * zh)
        vec_ref[3:4, :] += _colsum(dx1)
        vec_ref[4:5, :] += _colsum(dz * mix_v)

    row = pl.BlockSpec((tm, D), lambda i: (i, 0))
    vec = pl.BlockSpec((1, D), lambda i: (0, 0))
    return pl.pallas_call(
        body, grid=(T // tm,), name="mid_bwd",
        in_specs=[row, row, row, row, row, pl.BlockSpec((6, D), lambda i: (0, 0)), vec],
        out_specs=[row, row, pl.BlockSpec((8, D), lambda i: (0, 0))],
        out_shape=[_sds((T, D), BF16), _sds((T, D), F32), _sds((8, D), F32)],
        compiler_params=_params(("arbitrary",)),
    )(x2, mix, x1, dx1a, dh2, mod6, ln1_g)


def _first_bwd(x2, dh1, dxa, mod6):
    T, D = x2.shape
    tm = _tile(T, 256, 8)

    def body(x_ref, dh1_ref, dxa_ref, mod_ref, gx_ref, vec_ref):
        @pl.when(pl.program_id(0) == 0)
        def _():
            vec_ref[...] = jnp.zeros_like(vec_ref)

        dh1 = dh1_ref[...]
        xh, rstd = _ln(x_ref[...])
        gx_ref[...] = dxa_ref[...] + _ln_bwd(dh1 * (1.0 + mod_ref[1:2, :]), xh, rstd)
        vec_ref[0:1, :] += _colsum(dh1 * xh)
        vec_ref[1:2, :] += _colsum(dh1)

    row = pl.BlockSpec((tm, D), lambda i: (i, 0))
    return pl.pallas_call(
        body, grid=(T // tm,), name="first_bwd",
        in_specs=[row, row, row, pl.BlockSpec((6, D), lambda i: (0, 0))],
        out_specs=[row, pl.BlockSpec((8, D), lambda i: (0, 0))],
        out_shape=[_sds((T, D), F32), _sds((8, D), F32)],
        compiler_params=_params(("arbitrary",)),
    )(x2, dh1, dxa, mod6)


def _slot(j):
    return (j % 2) * 4 + j // 2


def _mm_gathered(a, wg, shards, out, name):
    M, K = a.shape
    _, _, ns = wg.shape
    tm = _tile(M, 512, 16)

    def body(shards_ref, a_ref, w_ref, prev_ref, o_ref):
        o_ref[...] = _dot(a_ref[...], w_ref[...], NN)

    return pl.pallas_call(
        body, name=name,
        grid_spec=pltpu.PrefetchScalarGridSpec(
            num_scalar_prefetch=1, grid=(shards.shape[0], M // tm),
            in_specs=[pl.BlockSpec((tm, K), lambda j, i, s: (i, 0)),
                      pl.BlockSpec((None, K, ns), lambda j, i, s: (s[j], 0, 0)), ORDER_ONLY],
            out_specs=pl.BlockSpec((tm, ns), lambda j, i, s: (i, s[j]))),
        out_shape=_sds((M, N_DEV * ns), F32), input_output_aliases={3: 0},
        compiler_params=_params(("parallel", "parallel"), big=True),
    )(shards, a, wg, out)


def _mm_nn(a, b, name):
    M, K = a.shape
    _, N = b.shape
    tm, tn, tk = _tile(M, 512, 16), _tile(N, 1024, LANE), _tile(K, 2048, LANE)

    def body(a_ref, b_ref, o_ref):
        @pl.when(pl.program_id(2) == 0)
        def _():
            o_ref[...] = jnp.zeros_like(o_ref)

        o_ref[...] += _dot(a_ref[...], b_ref[...], NN)

    return pl.pallas_call(
        body, grid=(M // tm, N // tn, K // tk), name=name,
        in_specs=[pl.BlockSpec((tm, tk), lambda i, j, k: (i, k)), pl.BlockSpec((tk, tn), lambda i, j, k: (k, j))],
        out_specs=pl.BlockSpec((tm, tn), lambda i, j, k: (i, j)),
        out_shape=_sds((M, N), F32), compiler_params=_params(("parallel", "parallel", "arbitrary"), big=True),
    )(a, b)


def _mm_nt(a, b, name):
    M, K = a.shape
    N, _ = b.shape
    tm, tn = _tile(M, 512, 16), _tile(N, 1024, LANE)

    def body(a_ref, b_ref, o_ref):
        o_ref[...] = _dot(a_ref[...], b_ref[...], NT)

    return pl.pallas_call(
        body, grid=(M // tm, N // tn), name=name,
        in_specs=[pl.BlockSpec((tm, K), lambda i, j: (i, 0)), pl.BlockSpec((tn, K), lambda i, j: (j, 0))],
        out_specs=pl.BlockSpec((tm, tn), lambda i, j: (i, j)),
        out_shape=_sds((M, N), F32), compiler_params=_params(("parallel", "parallel"), big=True),
    )(a, b)


def _mm_swiglu(h2, wg):
    M, K = h2.shape
    _, _, ns = wg.shape
    half = N_DEV // 2
    tm = _tile(M, 256, 16)

    def body(a_ref, wgate_ref, wup_ref, gu_ref, act_ref):
        a = a_ref[...]
        g = _dot(a, wgate_ref[...], NN)
        u = _dot(a, wup_ref[...], NN)
        gu_ref[0] = g
        gu_ref[1] = u
        act_ref[...] = (g * _sigmoid(g) * u).astype(BF16)

    return pl.pallas_call(
        body, grid=(half, M // tm), name="ffn_in_swiglu",
        in_specs=[pl.BlockSpec((tm, K), lambda j, i: (i, 0)),
                  pl.BlockSpec((None, K, ns), lambda j, i: (j, 0, 0)),
                  pl.BlockSpec((None, K, ns), lambda j, i: (j + half, 0, 0))],
        out_specs=[pl.BlockSpec((2, tm, ns), lambda j, i: (0, i, j)), pl.BlockSpec((tm, ns), lambda j, i: (i, j))],
        out_shape=[_sds((2, M, half * ns), F32), _sds((M, half * ns), BF16)],
        compiler_params=_params(("parallel", "parallel"), big=True),
    )(h2, wg, wg)


def _mm_swiglu_bwd(dff, w2, gu):
    M, K = dff.shape
    F = w2.shape[0]
    tm, tn = _tile(M, 512, 16), _tile(F, 1408, LANE)

    def body(a_ref, b_ref, gu_ref, du_ref):
        da = _dot(a_ref[...], b_ref[...], NT)
        g = gu_ref[0]
        u = gu_ref[1]
        sg = _sigmoid(g)
        du_ref[0] = (da * u * (sg * (1.0 + g * (1.0 - sg)))).astype(BF16)
        du_ref[1] = (da * (g * sg)).astype(BF16)

    return pl.pallas_call(
        body, grid=(F // tn, M // tm), name="ffn_out_bwd_swiglu",
        in_specs=[pl.BlockSpec((tm, K), lambda j, i: (i, 0)), pl.BlockSpec((tn, K), lambda j, i: (j, 0)),
                  pl.BlockSpec((2, tm, tn), lambda j, i: (0, i, j))],
        out_specs=pl.BlockSpec((2, tm, tn), lambda j, i: (0, i, j)),
        out_shape=_sds((2, M, F), BF16), compiler_params=_params(("parallel", "parallel"), big=True),
    )(dff, w2, gu)


ORDER_ONLY = pl.BlockSpec(memory_space=pl.ANY)


def _mm_tn_rows(dep, a, b, rs, name):
    M, Ka = a.shape
    _, N = b.shape
    tm = _tile(M, 1024, 16)

    def body(_, a_ref, b_ref, o_ref, acc_ref):
        m = pl.program_id(1)

        @pl.when(m == 0)
        def _():
            acc_ref[...] = jnp.zeros_like(acc_ref)

        acc_ref[...] += _dot(a_ref[...], b_ref[...], TN)

        @pl.when(m == pl.num_programs(1) - 1)
        def _():
            o_ref[0, 0] = acc_ref[0:rs, :].astype(BF16)
            o_ref[1, 0] = acc_ref[rs:2 * rs, :].astype(BF16)

    return pl.pallas_call(
        body, grid=(N_DEV // 2, M // tm), name=name,
        in_specs=[ORDER_ONLY, pl.BlockSpec((tm, 2 * rs), lambda ch, m: (m, ch)),
                  pl.BlockSpec((tm, N), lambda ch, m: (m, 0))],
        out_specs=pl.BlockSpec((2, 1, rs, N), lambda ch, m: (0, ch, 0, 0)),
        out_shape=_sds((2, N_DEV // 2, rs, N), BF16),
        scratch_shapes=[pltpu.VMEM((2 * rs, N), F32)],
        compiler_params=_params(("parallel", "arbitrary"), big=True),
    )(dep, a, b)


def _mm_gathered_nt(dep, a, a_spec, wg, M, tm, name, first=0, count=None, out=None):
    _, K, ns = wg.shape
    count = M // tm if count is None else count
    out = lax.empty((M, K), F32) if out is None else out

    def body(_, a_ref, w_ref, prev_ref, o_ref):
        @pl.when(pl.program_id(1) == 0)
        def _():
            o_ref[...] = jnp.zeros_like(o_ref)

        o_ref[...] += _dot(a_ref[:, 0:ns], w_ref[0], NT) + _dot(a_ref[:, ns:2 * ns], w_ref[1], NT)

    return pl.pallas_call(
        body, grid=(count, N_DEV // 2), name=name,
        in_specs=[ORDER_ONLY, a_spec(tm, 2 * ns, first), pl.BlockSpec((2, K, ns), lambda i, p: (p, 0, 0)), ORDER_ONLY],
        out_specs=pl.BlockSpec((tm, K), lambda i, j: (i + first, 0)),
        out_shape=_sds((M, K), F32), input_output_aliases={3: 0},
        compiler_params=_params(("parallel", "arbitrary"), big=True),
    )(dep, a, wg, out)


def _mm_tn_gathered(dep, h, a, a_spec, ns, tm, name):
    M, K = h.shape

    def body(_, h_ref, a_ref, o_ref, acc_ref):
        m = pl.program_id(1)

        @pl.when(m == 0)
        def _():
            acc_ref[...] = jnp.zeros_like(acc_ref)

        acc_ref[...] += _dot(h_ref[...], a_ref[...], TN)

        @pl.when(m == pl.num_programs(1) - 1)
        def _():
            o_ref[...] = acc_ref[...].astype(BF16)

    return pl.pallas_call(
        body, grid=(N_DEV, M // tm), name=name,
        in_specs=[ORDER_ONLY, pl.BlockSpec((tm, K), lambda j, m: (m, 0)), a_spec(tm, ns)],
        out_specs=pl.BlockSpec((None, K, ns), lambda j, m: (_slot(j), 0, 0)),
        out_shape=_sds((N_DEV, K, ns), BF16),
        scratch_shapes=[pltpu.VMEM((K, ns), F32)],
        compiler_params=_params(("parallel", "arbitrary"), big=True),
    )(dep, h, a)


def _bias_onehot(rbp, max_rel):
    r = lax.broadcasted_iota(jnp.int32, (rbp, TAB), 0)
    m = lax.broadcasted_iota(jnp.int32, (rbp, TAB), 1)
    dist = KPAD - jnp.where(m < WIN, m, m - TAB)
    return (r == jnp.clip(dist, -max_rel, max_rel) + max_rel).astype(F32)


def _attn_setup(i, hp, k_ref, v_ref, gv_ref, kpad, vpad, bias):
    ls = slice(i * ATTN_HEAD_DIM, (i + 1) * ATTN_HEAD_DIM)
    kpad[i][0:KPAD, :] = jnp.zeros((KPAD, ATTN_HEAD_DIM), BF16)
    vpad[i][0:KPAD, :] = jnp.zeros((KPAD, ATTN_HEAD_DIM), BF16)
    kpad[i][KPAD:, :] = k_ref[:, ls].astype(BF16)
    vpad[i][KPAD:, :] = v_ref[:, ls].astype(BF16)
    gvrow = gv_ref[pl.ds(hp * ATTN_HEADS_PER_STEP + i, 1), :]
    tab = pltpu.roll(jnp.broadcast_to(gvrow, (QBLK, TAB)), 0, 1, stride=1, stride_axis=0)
    row = lax.broadcasted_iota(jnp.int32, (QBLK, WIN), 0)
    col = lax.broadcasted_iota(jnp.int32, (QBLK, WIN), 1)
    first = jnp.bitwise_and(row, -CHUNK)
    seen = jnp.logical_and(col >= first, col < first + (N_PAST + 1) * CHUNK)
    bias[i][...] = jnp.where(seen, tab[:, 0:WIN], NEG)


def _attn_probs(b, q_ref, kpad, vpad, bias, col):
    pair = range(ATTN_HEADS_PER_STEP)
    ls = [slice(i * ATTN_HEAD_DIM, (i + 1) * ATTN_HEAD_DIM) for i in pair]
    r0 = pl.multiple_of(b * QBLK, QBLK)
    q = [q_ref[pl.ds(r0, QBLK), ls[i]].astype(BF16) for i in pair]
    kw = [kpad[i][pl.ds(r0, WIN), :] for i in pair]
    vw = [vpad[i][pl.ds(r0, WIN), :] for i in pair]
    s = [_dot(q[i], kw[i], NT) * (ATTN_HEAD_DIM ** -0.5) + bias[i][...] for i in pair]
    s = [jnp.where(col >= KPAD - r0, s[i], NEG) for i in pair]
    p = [jnp.exp(s[i] - jnp.max(s[i], axis=-1, keepdims=True)) for i in pair]
    pn = [p[i] / jnp.sum(p[i], axis=-1, keepdims=True) for i in pair]
    return r0, ls, q, kw, vw, pn


def _attn_fwd(proj, gv, ga, AW):
    T = proj.shape[0]
    AH = ATTN_HEADS_PER_STEP
    W = AH * ATTN_HEAD_DIM
    HP = AW // W

    def body(q_ref, k_ref, v_ref, gv_ref, ga_ref, o_ref, *scratch):
        kpad, vpad, bias = (scratch[k * AH:(k + 1) * AH] for k in range(3))
        hp = pl.program_id(0)
        for i in range(AH):
            _attn_setup(i, hp, k_ref, v_ref, gv_ref, kpad, vpad, bias)
        col = lax.broadcasted_iota(jnp.int32, (QBLK, WIN), 1)

        def block(b, carry):
            pair = range(AH)
            r0, ls, _, _, vw, pn = _attn_probs(b, q_ref, kpad, vpad, bias, col)
            o = [_dot(pn[i].astype(BF16), vw[i], NN) for i in pair]
            r = [lax.rsqrt(jnp.mean(o[i] * o[i], axis=-1, keepdims=True) + EPS) for i in pair]
            outs = [o[i] * r[i] * ga_ref[0:1, ls[i]] for i in pair]
            o_ref[pl.ds(r0, QBLK), :] = jnp.concatenate(outs, axis=1).astype(BF16)
            return carry

        lax.fori_loop(0, T // QBLK, block, 0)

    blk = lambda off: pl.BlockSpec((T, W), lambda hp: (0, off + hp))
    return pl.pallas_call(
        body, grid=(HP,), name="attn_fwd",
        in_specs=[blk(0), blk(HP), blk(2 * HP), pl.BlockSpec(gv.shape, lambda hp: (0, 0)),
                  pl.BlockSpec((1, W), lambda hp: (0, hp))],
        out_specs=pl.BlockSpec((T, W), lambda hp: (0, hp)),
        out_shape=_sds((T, AW), BF16),
        scratch_shapes=[pltpu.VMEM((T + KPAD, ATTN_HEAD_DIM), BF16)] * (2 * AH) + [pltpu.VMEM((QBLK, WIN), F32)] * AH,
        compiler_params=_params(("parallel",), big=True),
    )(proj, proj, proj, gv, ga)


def _attn_bwd(proj, dmixin, gv, ga, AW):
    T = proj.shape[0]
    AH = ATTN_HEADS_PER_STEP
    W = AH * ATTN_HEAD_DIM
    HP = AW // W
    scale = ATTN_HEAD_DIM ** -0.5

    def body(q_ref, k_ref, v_ref, dn_ref, gv_ref, ga_ref, dq_ref, dk_ref, dv_ref, dgv_ref, dga_ref, *scratch):
        kpad, vpad, dkacc, dvacc, bias, dbias = (scratch[k * AH:(k + 1) * AH] for k in range(6))
        hp = pl.program_id(0)
        for i in range(AH):
            _attn_setup(i, hp, k_ref, v_ref, gv_ref, kpad, vpad, bias)
            dkacc[i][...] = jnp.zeros_like(dkacc[i])
            dvacc[i][...] = jnp.zeros_like(dvacc[i])
            dbias[i][...] = jnp.zeros_like(dbias[i])
        dga_ref[...] = jnp.zeros_like(dga_ref)
        col = lax.broadcasted_iota(jnp.int32, (QBLK, WIN), 1)

        def block(b, carry):
            pair = range(AH)
            r0, lss, qs, kws, vws, pns = _attn_probs(b, q_ref, kpad, vpad, bias, col)
            pn_b = [pns[i].astype(BF16) for i in pair]
            o = [_dot(pn_b[i], vws[i], NN) for i in pair]
            r = [lax.rsqrt(jnp.mean(o[i] * o[i], axis=-1, keepdims=True) + EPS) for i in pair]
            dn = [dn_ref[pl.ds(r0, QBLK), lss[i]] for i in pair]
            for i in pair:
                dga_ref[i:i + 1, :] += _colsum(dn[i] * o[i] * r[i])
            a = [dn[i] * ga_ref[0:1, lss[i]] for i in pair]
            do_b = [(r[i] * (a[i] - o[i] * (r[i] * r[i]) * jnp.mean(a[i] * o[i], axis=-1, keepdims=True))).astype(BF16)
                    for i in pair]
            dp = [_dot(do_b[i], vws[i], NT) for i in pair]
            for i in pair:
                dvacc[i][pl.ds(r0, WIN), :] += _dot(pn_b[i], do_b[i], TN)
            ds = [pns[i] * (dp[i] - jnp.sum(pns[i] * dp[i], axis=-1, keepdims=True)) for i in pair]
            for i in pair:
                dbias[i][...] += ds[i]
            ds_b = [ds[i].astype(BF16) for i in pair]
            dq = [_dot(ds_b[i], kws[i], NN) * scale for i in pair]
            dq_ref[pl.ds(r0, QBLK), :] = jnp.concatenate(dq, axis=1).astype(BF16)
            for i in pair:
                dkacc[i][pl.ds(r0, WIN), :] += _dot(ds_b[i], qs[i], TN) * scale
            return carry

        lax.fori_loop(0, T // QBLK, block, 0)

        rr = lax.broadcasted_iota(jnp.int32, (QBLK, QBLK), 0)
        cc = lax.broadcasted_iota(jnp.int32, (QBLK, QBLK), 1)
        flip = (rr + cc == QBLK - 1).astype(BF16)
        for i in range(AH):
            ls = slice(i * ATTN_HEAD_DIM, (i + 1) * ATTN_HEAD_DIM)
            dk_ref[:, ls] = dkacc[i][KPAD:, :].astype(BF16)
            dv_ref[:, ls] = dvacc[i][KPAD:, :].astype(BF16)
            full = jnp.concatenate([dbias[i][...], jnp.zeros((QBLK, TAB - WIN), F32)], axis=1)
            hi = full.astype(BF16)
            lo = (full - hi.astype(F32)).astype(BF16)
            rev = _dot(flip, hi, NN) + _dot(flip, lo, NN)
            dgv_ref[i:i + 1, :] = _colsum(pltpu.roll(rev, TAB - (QBLK - 1), 1, stride=1, stride_axis=0))

    blk = lambda off: pl.BlockSpec((T, W), lambda hp: (0, off + hp))
    accs = lambda dt: [pltpu.VMEM((T + KPAD, ATTN_HEAD_DIM), dt)] * AH
    return pl.pallas_call(
        body, grid=(HP,), name="attn_bwd",
        in_specs=[blk(0), blk(HP), blk(2 * HP), blk(0), pl.BlockSpec(gv.shape, lambda hp: (0, 0)),
                  pl.BlockSpec((1, W), lambda hp: (0, hp))],
        out_specs=[blk(0), blk(0), blk(0), pl.BlockSpec((None, AH, TAB), lambda hp: (hp, 0, 0)),
                   pl.BlockSpec((None, AH, ATTN_HEAD_DIM), lambda hp: (hp, 0, 0))],
        out_shape=[_sds((T, AW), BF16), _sds((T, AW), BF16), _sds((T, AW), BF16),
                   _sds((HP, AH, TAB), F32), _sds((HP, AH, ATTN_HEAD_DIM), F32)],
        scratch_shapes=accs(BF16) + accs(BF16) + accs(F32) + accs(F32) + [pltpu.VMEM((QBLK, WIN), F32)] * (2 * AH),
        compiler_params=_params(("parallel",), big=True),
    )(proj, proj, proj, dmixin, gv, ga)


def _ltri():
    r = lax.broadcasted_iota(jnp.int32, (CHUNK, CHUNK), 0)
    c = lax.broadcasted_iota(jnp.int32, (CHUNK, CHUNK), 1)
    return (c <= r).astype(BF16)


def _tri_dot(tri, v, dims):
    hi = v.astype(BF16)
    lo = (v - hi.astype(F32)).astype(BF16)
    return _dot(tri, hi, dims) + _dot(tri, lo, dims)


HEADS_PER_STEP = 2


def _alternate(stages):
    live = list(stages)
    while live:
        for g in list(live):
            if next(g, StopIteration) is StopIteration:
                live.remove(g)


def _hgrn_gates(n, ls, q_ref, f_ref, lb_ref, ltri):
    r0 = pl.multiple_of(n * CHUNK, CHUNK)
    rows = pl.ds(r0, CHUNK)
    lb = lb_ref[:, ls]
    qb = q_ref[rows, ls]
    sg = _sigmoid(f_ref[rows, ls])
    f = lb + (1.0 - lb) * sg
    sq = _sigmoid(qb)
    b = _tri_dot(ltri, jnp.log(f), NN)
    return rows, lb, qb, sg, f, 1.0 - f, sq, qb * sq, b


def _hgrn_specs(T, RW, AW):
    HG = HEADS_PER_STEP
    W = HG * LANE
    base = 3 * AW // W
    blk_in = lambda off: pl.BlockSpec((T, W), lambda g: (0, base + off + g))
    col = pl.BlockSpec((T, W), lambda g: (0, g))
    return HG, W, RW // W, blk_in, col


def _hgrn_fwd(proj, lb, gn, AW, RW):
    T = proj.shape[0]
    RH, NC, NSUB = RW // LANE, T // CHUNK, CHUNK // SUB
    HG, W, NG, blk_in, col = _hgrn_specs(T, RW, AW)

    def body(q_ref, f_ref, i_ref, g_ref, lb_ref, gn_ref, mix_ref, o_ref, stall_ref, st_all, bs_all, kks_all, ics_all):
        st_all[...] = jnp.zeros_like(st_all)
        ltri = _ltri()
        rowi = lax.broadcasted_iota(jnp.int32, (SUB, 1), 0)

        def one_head(h, n):
            ls = slice(h * LANE, (h + 1) * LANE)
            st, bs, kks, ics = st_all.at[h], bs_all.at[h], kks_all.at[h], ics_all.at[h]
            rows, _, _, _, _, kk, _, qs, b = _hgrn_gates(n, ls, q_ref, f_ref, lb_ref, ltri)
            ic = i_ref[rows, ls]
            stv = st[...]
            stall_ref[h, n] = stv
            bs[...] = b
            kks[...] = kk
            ics[...] = ic
            yield
            o = _dot((qs * jnp.exp(b)).astype(BF16), stv.astype(BF16), NT)
            yield
            ic_b = ic.astype(BF16)
            pieces = []
            for blk in range(NSUB):
                s0 = blk * SUB
                bI, qI = b[s0:s0 + SUB], qs[s0:s0 + SUB]
                if blk == 0:
                    oI = jnp.zeros((SUB, LANE), F32)
                else:
                    ref = bs[s0 - 1:s0, :]
                    qt = (qI * jnp.exp(bI - ref)).astype(BF16)
                    kt = (kk[0:s0] * jnp.exp(ref - b[0:s0])).astype(BF16)
                    oI = _dot(_dot(qt, kt, NT).astype(BF16), ic_b[0:s0], NN)
                    yield
                acc = [oI[g * ROWS:(g + 1) * ROWS] for g in range(SUB // ROWS)]
                for s in range(SUB):
                    sr = s0 + s
                    g0 = s // ROWS
                    lo = g0 * ROWS
                    e = jnp.exp(jnp.minimum(bI[lo:] - bs[sr:sr + 1, :], 0.0))
                    a = jnp.sum(qI[lo:] * kks[sr:sr + 1, :] * e, axis=-1, keepdims=True)
                    add = jnp.where(rowi[lo:] >= s, a, 0.0) * ics[sr:sr + 1, :]
                    for g in range(g0, SUB // ROWS):
                        acc[g] = acc[g] + add[(g - g0) * ROWS:(g - g0 + 1) * ROWS]
                    yield
                pieces.extend(acc)
            o = o + jnp.concatenate(pieces, axis=0)
            bl = bs[CHUNK - 1:CHUNK, :]
            kd = (kk * jnp.exp(bl - b)).astype(BF16)
            st[...] = stv * jnp.exp(bl) + _dot(ic_b, kd, TN)
            yield
            o_ref[rows, ls] = o
            r = lax.rsqrt(jnp.mean(o * o, axis=-1, keepdims=True) + EPS)
            gb = g_ref[rows, ls]
            mix_ref[rows, ls] = (o * r * gn_ref[...] * (gb * _sigmoid(gb))).astype(BF16)

        def chunk(n, carry):
            _alternate([one_head(h, n) for h in range(HG)])
            return carry

        lax.fori_loop(0, NC, chunk, 0)

    tile = pltpu.VMEM((HG, CHUNK, LANE), F32)
    return pl.pallas_call(
        body, grid=(NG,), name="hgrn_fwd",
        in_specs=[blk_in(0), blk_in(NG), blk_in(2 * NG), blk_in(3 * NG), pl.BlockSpec((1, W), lambda g: (0, g)),
                  pl.BlockSpec((1, LANE), lambda g: (0, 0))],
        out_specs=[col, col, pl.BlockSpec((HG, NC, LANE, LANE), lambda g: (g, 0, 0, 0))],
        out_shape=[_sds((T, RW), BF16), _sds((T, RW), F32), _sds((RH, NC, LANE, LANE), F32)],
        scratch_shapes=[pltpu.VMEM((HG, LANE, LANE), F32), tile, tile, tile],
        compiler_params=_params(("parallel",), big=True),
    )(proj, proj, proj, proj, lb, gn)


def _hgrn_bwd(proj, dmixin, o_b, st_all, lb, gn, AW, RW):
    T = proj.shape[0]
    RH, NC, NSUB = RW // LANE, T // CHUNK, CHUNK // SUB
    HG, W, NG, blk_in, col = _hgrn_specs(T, RW, AW)

    def body(q_ref, f_ref, i_ref, g_ref, o_ref, dn_ref, stall_ref, lb_ref, gn_ref,
             dq_ref, df_ref, di_ref, dg_ref, dlb_ref, dgn_ref, dst_all, bs_all, qss_all, dos_all, p2_all, dic_all,
             p1_all):
        dst_all[...] = jnp.zeros_like(dst_all)
        dlb_ref[...] = jnp.zeros_like(dlb_ref)
        dgn_ref[...] = jnp.zeros_like(dgn_ref)
        ltri = _ltri()
        rowi = lax.broadcasted_iota(jnp.int32, (SUB, 1), 0)
        last = lax.broadcasted_iota(jnp.int32, (CHUNK, 1), 0) == CHUNK - 1

        def one_head(h, n):
            ls = slice(h * LANE, (h + 1) * LANE)
            dst, bs, qss, dos = dst_all.at[h], bs_all.at[h], qss_all.at[h], dos_all.at[h]
            p2, dic, p1s = p2_all.at[h], dic_all.at[h], p1_all.at[h]
            rows, lbv, qb, sg, f, kk, sq, qs, b = _hgrn_gates(n, ls, q_ref, f_ref, lb_ref, ltri)
            ic = i_ref[rows, ls]
            stv = stall_ref[h, n]
            dstv = dst[...]
            o = o_ref[rows, ls]
            dn = dn_ref[rows, ls]
            gb = g_ref[rows, ls]
            sgb = _sigmoid(gb)
            r = lax.rsqrt(jnp.mean(o * o, axis=-1, keepdims=True) + EPS)
            gnv = gn_ref[...]
            dg_ref[rows, ls] = (dn * (o * r * gnv) * (sgb * (1.0 + gb * (1.0 - sgb)))).astype(BF16)
            dy = dn * (gb * sgb)
            dgn_ref[h] += _colsum(dy * o * r)
            a_ = dy * gnv
            do = r * (a_ - o * (r * r) * jnp.mean(a_ * o, axis=-1, keepdims=True))
            do_b = do.astype(BF16)
            bs[...] = b
            qss[...] = qs
            dos[...] = do
            yield
            ic_b = ic.astype(BF16)
            eb = jnp.exp(b)
            bl = bs[CHUNK - 1:CHUNK, :]
            ebl = jnp.exp(bl)
            dec = jnp.exp(bl - b)
            kd = (kk * dec).astype(BF16)
            dst_b = dstv.astype(BF16)
            dqs = _dot(do_b, stv.astype(BF16), NN) * eb
            dkk2 = _dot(ic_b, dst_b, NN) * dec
            dic[...] = _dot(kd, dst_b, NT)
            dbl = ebl * _colsum(stv * dstv) + _colsum(kk * dkk2)
            dst[...] = dstv * ebl + _dot(do_b, (qs * eb).astype(BF16), TN)
            yield
            p2[...] = jnp.zeros_like(p2)
            p1_pieces = []
            for blk in range(NSUB):
                s0 = blk * SUB
                bI, qI, doI = b[s0:s0 + SUB], qs[s0:s0 + SUB], do[s0:s0 + SUB]
                if blk == 0:
                    p1 = jnp.zeros((SUB, LANE), F32)
                else:
                    ref = bs[s0 - 1:s0, :]
                    eq = jnp.exp(bI - ref)
                    ek = jnp.exp(ref - b[0:s0])
                    qt = (qI * eq).astype(BF16)
                    kt = (kk[0:s0] * ek).astype(BF16)
                    doI_b = doI.astype(BF16)
                    dic[0:s0, :] += _dot(_dot(qt, kt, NT).astype(BF16), doI_b, TN)
                    da = _dot(doI_b, ic_b[0:s0], NT).astype(BF16)
                    p1 = _dot(da, kt, NN) * eq
                    p2[0:s0, :] += _dot(da, qt, TN) * ek
                    yield
                p1_pieces.append(p1)
                kkI, icI = kk[s0:s0 + SUB], ic[s0:s0 + SUB]
                p2acc = [jnp.zeros((ROWS, LANE), F32) for _ in range(SUB // ROWS)]
                diacc = [jnp.zeros((ROWS, LANE), F32) for _ in range(SUB // ROWS)]
                for t in range(SUB):
                    tr = s0 + t
                    ng = t // ROWS + 1
                    hi = ng * ROWS
                    keep = rowi[:hi] <= t
                    do_t = dos[tr:tr + 1, :]
                    e = jnp.exp(jnp.minimum(bs[tr:tr + 1, :] - bI[:hi], 0.0))
                    qe = qss[tr:tr + 1, :] * e
                    a = jnp.where(keep, jnp.sum(kkI[:hi] * qe, axis=-1, keepdims=True), 0.0)
                    da = jnp.where(keep, jnp.sum(icI[:hi] * do_t, axis=-1, keepdims=True), 0.0)
                    dp2, ddi = da * qe, a * do_t
                    for g in range(ng):
                        p2acc[g] = p2acc[g] + dp2[g * ROWS:(g + 1) * ROWS]
                        diacc[g] = diacc[g] + ddi[g * ROWS:(g + 1) * ROWS]
                    p1s[tr:tr + 1, :] = _colsum(da * kkI[:hi] * e)
                    yield
                p2[s0:s0 + SUB, :] += jnp.concatenate(p2acc, axis=0)
                dic[s0:s0 + SUB, :] += jnp.concatenate(diacc, axis=0)
            dqs = dqs + jnp.concatenate(p1_pieces, axis=0) + p1s[...]
            dkk = dkk2 + p2[...]
            db = qs * dqs - kk * dkk + jnp.where(last, dbl, 0.0)
            dgl = _tri_dot(ltri, db, TN)
            yield
            dfv = dgl / f - dkk
            df_ref[rows, ls] = (dfv * (1.0 - lbv) * sg * (1.0 - sg)).astype(BF16)
            dlb_ref[:, ls] += _colsum(dfv * (1.0 - sg))
            dq_ref[rows, ls] = (dqs * (sq * (1.0 + qb * (1.0 - sq)))).astype(BF16)
            di_ref[rows, ls] = dic[...].astype(BF16)

        def chunk(k, carry):
            _alternate([one_head(h, NC - 1 - k) for h in range(HG)])
            return carry

        lax.fori_loop(0, NC, chunk, 0)

    tile = pltpu.VMEM((HG, CHUNK, LANE), F32)
    return pl.pallas_call(
        body, grid=(NG,), name="hgrn_bwd",
        in_specs=[blk_in(0), blk_in(NG), blk_in(2 * NG), blk_in(3 * NG), col,
                  pl.BlockSpec((T, W), lambda g: (0, AW // W + g)),
                  pl.BlockSpec((HG, NC, LANE, LANE), lambda g: (g, 0, 0, 0)),
                  pl.BlockSpec((1, W), lambda g: (0, g)), pl.BlockSpec((1, LANE), lambda g: (0, 0))],
        out_specs=[col, col, col, col, pl.BlockSpec((1, W), lambda g: (0, g)),
                   pl.BlockSpec((HG, 1, LANE), lambda g: (g, 0, 0))],
        out_shape=[_sds((T, RW), BF16)] * 4 + [_sds((1, RW), F32), _sds((RH, 1, LANE), F32)],
        scratch_shapes=[pltpu.VMEM((HG, LANE, LANE), F32), tile, tile, tile, tile, tile, tile],
        compiler_params=_params(("parallel",), big=True),
    )(proj, proj, proj, proj, o_b, dmixin, st_all, lb, gn)


def _prep(c, lb_logits, rb_pad, max_rel):
    D, RW = c.shape[-1], lb_logits.shape[-1]
    H, rbp = rb_pad.shape

    def body(c_ref, l_ref, rb_ref, cact_ref, lb_ref, gv_ref):
        cv = c_ref[...]
        cact_ref[...] = cv * _sigmoid(cv)
        lb_ref[...] = _sigmoid(l_ref[0:1, :] - l_ref[1:2, :])
        gv_ref[...] = _dot(rb_ref[...], _bias_onehot(rbp, max_rel), NN, HIGHEST)

    return pl.pallas_call(
        body, name="prep", out_shape=[_sds((1, D), F32), _sds((1, RW), F32), _sds((H, TAB), F32)],
    )(c, lb_logits, rb_pad)


def _mod_part(c_all, w_ada_s, b_ada_s):
    B, D = c_all.shape
    ns = w_ada_s.shape[1]
    tn = _tile(ns, 768, LANE)

    def body(c_ref, w_ref, b_ref, o_ref):
        o_ref[...] = _dot(c_ref[...], w_ref[...], NN) + b_ref[...]

    return pl.pallas_call(
        body, grid=(ns // tn,), name="mod_part",
        in_specs=[pl.BlockSpec((B, D), lambda j: (0, 0)), pl.BlockSpec((D, tn), lambda j: (0, j)),
                  pl.BlockSpec((1, tn), lambda j: (0, j))],
        out_specs=pl.BlockSpec((B, tn), lambda j: (0, j)),
        out_shape=_sds((B, ns), F32), compiler_params=_params(("parallel",)),
    )(c_all, w_ada_s, b_ada_s)


def _adam(w, g, m, v):
    m = ADAM_B1 * m + (1.0 - ADAM_B1) * g
    v = ADAM_B2 * v + (1.0 - ADAM_B2) * (g * g)
    m_hat = m * (1.0 / (1.0 - ADAM_B1 ** ADAM_STEP))
    v_hat = v * (1.0 / (1.0 - ADAM_B2 ** ADAM_STEP))
    return -ADAM_LR * (m_hat / (jnp.sqrt(v_hat) + ADAM_EPS) + ADAM_WD * w), m, v


def _adam_ada(c_all, dmod_s, w, m, v):
    B, D = c_all.shape
    ns = w.shape[1]
    tr, tn = _tile(D, 512, LANE), _tile(ns, 768, LANE)

    def body(c_ref, d_ref, w_ref, m_ref, v_ref, g_out, dw_out, m_out, v_out):
        g = _dot(c_ref[...], d_ref[...], TN)
        g_out[...] = g
        dw_out[...], m_out[...], v_out[...] = _adam(w_ref[...], g, m_ref[...], v_ref[...])

    big = pl.BlockSpec((tr, tn), lambda i, j: (i, j))
    return pl.pallas_call(
        body, grid=(D // tr, ns // tn), name="adam_w_ada",
        in_specs=[pl.BlockSpec((B, tr), lambda i, j: (0, i)), pl.BlockSpec((B, tn), lambda i, j: (0, j)),
                  big, big, big],
        out_specs=[big] * 4, out_shape=[_sds((D, ns), F32)] * 4,
        compiler_params=_params(("parallel", "parallel")),
    )(c_all, dmod_s, w, m, v)


def _adam_shard(parts, w, m, v, name):
    R, C = w.shape
    tr = _tile(R, 256, 16)

    def body(p_ref, w_ref, m_ref, v_ref, g_out, dw_out, m_out, v_out):
        g = p_ref[0].astype(F32)
        for k in range(1, N_DEV // 2):
            g = g + p_ref[k].astype(F32)
        g_out[...] = g
        dw_out[...], m_out[...], v_out[...] = _adam(w_ref[...], g, m_ref[...], v_ref[...])

    big = pl.BlockSpec((tr, C), lambda i: (i, 0))
    return pl.pallas_call(
        body, grid=(R // tr,), name=name,
        in_specs=[pl.BlockSpec((N_DEV // 2, tr, C), lambda i: (0, i, 0)), big, big, big],
        out_specs=[big] * 4, out_shape=[_sds((R, C), F32)] * 4,
        compiler_params=_params(("parallel",), big=True),
    )(parts, w, m, v)


def _pair_sum(g8, land, core, name):
    _, NCHIP, R, C = g8.shape
    tr = _tile(R, 1024, 16)

    def body(core_ref, g_ref, l_ref, o_ref):
        o_ref[...] = g_ref[...] + l_ref[...]

    return pl.pallas_call(
        body, name=name,
        grid_spec=pltpu.PrefetchScalarGridSpec(
            num_scalar_prefetch=1, grid=(NCHIP, R // tr),
            in_specs=[pl.BlockSpec((None, None, tr, C), lambda k, i, core_ref: (core_ref[0], k, i, 0)),
                      pl.BlockSpec((None, tr, C), lambda k, i, core_ref: (k, i, 0))],
            out_specs=pl.BlockSpec((None, tr, C), lambda k, i, core_ref: (k, i, 0))),
        out_shape=_sds((NCHIP, R, C), BF16), compiler_params=_params(("parallel", "parallel")),
    )(core, g8, land)


SMALL = ("b_ada", "rel_bias", "attn_norm_g", "lb_logits", "gnorm_g", "ln1_g", "ln1_b", "ln2_g", "ln2_b")


def _small_update(parts, loss_parts, lbv, ws, ms, vs, max_rel):
    n = len(SMALL)

    def body(*refs):
        part_refs = dict(zip(SMALL, refs[:n]))
        loss_in, lb_ref = refs[n], refs[n + 1]
        w_refs, m_refs, v_refs = refs[n + 2:2 * n + 2], refs[2 * n + 2:3 * n + 2], refs[3 * n + 2:4 * n + 2]
        outs = refs[4 * n + 2:]

        def total(ref):
            tot = ref[0]
            for k in range(1, N_DEV):
                tot = tot + ref[k]
            return tot

        outs[0][...] = jnp.sum(total(loss_in), axis=-1, keepdims=True)
        for idx, name in enumerate(SMALL):
            g = total(part_refs[name])
            if name == "rel_bias":
                g = _dot(g, _bias_onehot(w_refs[idx].shape[1], max_rel), NT, HIGHEST)
            elif name == "lb_logits":
                lb = lb_ref[...]
                sign = (1 - 2 * lax.broadcasted_iota(jnp.int32, (2, 1), 0)).astype(F32)
                g = sign * (g * lb * (1.0 - lb))
            elif name == "gnorm_g":
                g = _colsum(g)
            dw, mm, vv = _adam(w_refs[idx][...], g, m_refs[idx][...], v_refs[idx][...])
            outs[1 + 4 * idx][...] = g
            outs[2 + 4 * idx][...] = dw
            outs[3 + 4 * idx][...] = mm
            outs[4 + 4 * idx][...] = vv

    out_shape = [_sds((1, 1), F32)]
    for w in ws:
        out_shape += [_sds(w.shape, F32)] * 4
    return pl.pallas_call(body, name="small_update", out_shape=out_shape, compiler_params=_params(big=True))(
        *[parts[k] for k in SMALL], loss_parts, lbv, *ws, *ms, *vs)


def _place():
    x, y, c = lax.axis_index("x"), lax.axis_index("y"), lax.axis_index("c")
    return x, y, c, [(1 - x, y), (x, 1 - y), (1 - x, 1 - y)]


def _all_gather(shard, name):
    HBM = pl.BlockSpec(memory_space=pl.ANY)

    def body(x_ref, out_ref, send_sems, recv_sems, local_sem):
        x, y, c, chips = _place()
        me, sibling = (x, y, c), (x, y, 1 - c)

        def slot(px, py, pc):
            return out_ref.at[4 * px + 2 * py + pc]

        def copy(k, block, to, src=None):
            return pltpu.make_async_remote_copy(
                src_ref=slot(*block) if src is None else src, dst_ref=slot(*block),
                send_sem=send_sems.at[k], recv_sem=recv_sems.at[k], device_id=to, device_id_type=MESH)

        mine = pltpu.make_async_copy(x_ref, slot(*me), local_sem)
        mine.start()
        first = [copy(0, me, sibling, src=x_ref)]
        first += [copy(1 + j, me, (*chip, c), src=x_ref) for j, chip in enumerate(chips)]
        for cp in first:
            cp.start()
        passed = [copy(4 + j, (*chip, c), sibling) for j, chip in enumerate(chips)]
        for j, chip in enumerate(chips):
            copy(1 + j, (*chip, c), me).wait_recv()
            passed[j].start()
        copy(0, sibling, me).wait_recv()
        for j, chip in enumerate(chips):
            copy(4 + j, (*chip, 1 - c), me).wait_recv()
        for cp in first + passed:
            cp.wait_send()
        mine.wait()

    return pl.pallas_call(
        body, name=name, out_shape=_sds((N_DEV,) + shard.shape, shard.dtype),
        in_specs=[HBM], out_specs=HBM,
        scratch_shapes=[pltpu.SemaphoreType.DMA((7,)), pltpu.SemaphoreType.DMA((7,)), pltpu.SemaphoreType.DMA(())],
    )(shard)


SEM_SPEC = pl.BlockSpec(memory_space=pltpu.SEMAPHORE)
HBM_SPEC = pl.BlockSpec(memory_space=pltpu.HBM)
EFFECT = pltpu.SideEffectType.DATAFLOW_SIDE_EFFECTING


def _remote(src, dst, send_sems, recv_sems, k, dev):
    return pltpu.make_async_remote_copy(src_ref=src, dst_ref=dst, send_sem=send_sems.at[k], recv_sem=recv_sems.at[k],
                                        device_id=dev, device_id_type=MESH)


def _copy_start(name, bufs, plan, n, after, only=None):
    nb = len(bufs)

    def body(*refs):
        send_sems, recv_sems = refs[nb + 1], refs[nb + 2]
        for k, (src, dst, dev) in enumerate(plan(*refs[:nb])):
            if only is not None and k not in only:
                continue
            _remote(src, dst, send_sems, recv_sems, k, dev).start()
        refs[-1][...] = jnp.zeros_like(refs[-1])

    out = pl.pallas_call(
        body, name=name,
        out_shape=(pltpu.SemaphoreType.DMA((n,)), pltpu.SemaphoreType.DMA((n,)),
                   *[pltpu.HBM(b.shape, b.dtype) for b in bufs], _sds((8, LANE), F32)),
        in_specs=[HBM_SPEC] * nb + [ORDER_ONLY],
        out_specs=(SEM_SPEC, SEM_SPEC, *[HBM_SPEC] * nb, pl.BlockSpec(memory_space=pltpu.VMEM)),
        input_output_aliases={i: 2 + i for i in range(nb)},
        compiler_params=pltpu.CompilerParams(has_side_effects=EFFECT),
    )(*[pltpu.with_memory_space_constraint(b, pltpu.HBM) for b in bufs], after)
    return (out[0], out[1]), list(out[2:2 + nb]), out[-1]


def _copy_wait(name, sems, bufs, plan, after, only=None):
    nb = len(bufs)

    def body(*refs):
        send_sems, recv_sems = refs[nb], refs[nb + 1]
        for k, (src, dst, dev) in enumerate(plan(*refs[:nb])):
            if only is not None and k not in only:
                continue
            cp = _remote(src, dst, send_sems, recv_sems, k, dev)
            cp.wait_send()
            cp.wait_recv()

    out = pl.pallas_call(
        body, name=name, out_shape=tuple(pltpu.HBM(b.shape, b.dtype) for b in bufs),
        in_specs=[HBM_SPEC] * nb + [SEM_SPEC, SEM_SPEC, pl.BlockSpec(memory_space=pl.ANY)],
        out_specs=tuple([HBM_SPEC] * nb), input_output_aliases={i: i for i in range(nb)},
        compiler_params=pltpu.CompilerParams(has_side_effects=EFFECT),
    )(*bufs, sems[0], sems[1], after)
    return list(out)


def _ag_plan_chips(shard_ref, out_ref):
    x, y, c, chips = _place()
    mine = out_ref.at[4 * x + 2 * y + c]
    return [(shard_ref, mine, (x, y, 1 - c))] + [(shard_ref, mine, (*chip, c)) for chip in chips]


def _ag_plan_pass(out_ref):
    x, y, c, chips = _place()
    slots = [out_ref.at[4 * chip[0] + 2 * chip[1] + c] for chip in chips]
    return [(s, s, (x, y, 1 - c)) for s in slots]


def _rs_plan_pair(g_ref, land_ref):
    x, y, c, _ = _place()
    return [(g_ref.at[1 - c], land_ref, (x, y, 1 - c))]


def _rs_plan_chips(p_ref, land_ref):
    x, y, c, chips = _place()
    return [(p_ref.at[2 * chip[0] + chip[1]], land_ref.at[2 * x + y], (*chip, c)) for chip in chips]


class _Gather:
    def __init__(self, shard, me, tag, after):
        self.tag = tag
        out = lax.dynamic_update_slice(lax.empty((N_DEV,) + shard.shape, shard.dtype), shard[None],
                                       (me,) + (0,) * shard.ndim)
        self.sems, (self.shard, self.out), self.token = _copy_start(
            "ag_start_" + tag, [shard, out], _ag_plan_chips, 4, after)
        self.groups = []

    def arrived(self, after, copies):
        name = "ag_wait_%s_%s" % (self.tag, "".join(map(str, copies)))
        self.shard, self.out = _copy_wait(name, self.sems, [self.shard, self.out], _ag_plan_chips, after, copies)
        return self.out

    def pass_on(self, after, blocks):
        name = "ag_pass_%s_%s" % (self.tag, "".join(map(str, blocks)))
        sems, (self.out,), _ = _copy_start(name, [self.out], _ag_plan_pass, 3, after, blocks)
        self.groups.append((sems, blocks))
        return self.out

    def passed(self, after, group):
        sems, blocks = self.groups[group]
        name = "ag_pass_wait_%s_%s" % (self.tag, "".join(map(str, blocks)))
        self.out = _copy_wait(name, sems, [self.out], _ag_plan_pass, after, blocks)[0]
        return self.out

    def arrived_from_chips(self, after):
        self.arrived(after, (0, 1, 2, 3))
        return self.pass_on(after, (0, 1, 2))

    def passed_on(self, after):
        return self.passed(after, 0)


def _ag_plan_direct(src_ref, out_ref):
    x, y, c, chips = _place()
    mine = out_ref.at[4 * x + 2 * y + c]
    peers = [(x, y, 1 - c)] + [(*chip, pc) for chip in chips for pc in (c, 1 - c)]
    return [(src_ref, mine, peer) for peer in peers]


class _SmallGather:
    def __init__(self, block, me, tag):
        self.tag = tag
        out = lax.dynamic_update_slice(lax.empty((N_DEV,) + block.shape, block.dtype), block[None],
                                       (me,) + (0,) * block.ndim)
        self.sems, self.bufs, self.token = _copy_start(
            "ag_direct_start_" + tag, [block, out], _ag_plan_direct, N_DEV - 1, jnp.zeros((1,), F32))

    def done(self, after):
        return _copy_wait("ag_direct_wait_" + self.tag, self.sems, self.bufs, _ag_plan_direct, after)[1]


class _ReduceScatter:
    def __init__(self, g8, tag):
        self.tag = tag
        land = lax.empty(g8.shape[1:], g8.dtype)
        self.sems, self.bufs, self.token = _copy_start(
            "rs_pair_start_" + tag, [g8, land], _rs_plan_pair, 1, jnp.zeros((1,), F32))

    def pair_done(self, core, chip, after, start_after=None):
        g8, land = _copy_wait("rs_pair_wait_" + self.tag, self.sems, self.bufs, _rs_plan_pair, after)
        p4 = _pair_sum(g8, land, core, "rs_pair_sum_" + self.tag)
        own = lax.dynamic_slice_in_dim(p4, chip, 1, axis=0)
        land2 = lax.dynamic_update_slice(lax.empty(p4.shape, p4.dtype), own, (chip, 0, 0))
        self.sems, self.bufs, self.token = _copy_start(
            "rs_chips_start_" + self.tag, [p4, land2], _rs_plan_chips, 3,
            jnp.zeros((1,), F32) if start_after is None else start_after)

    def sums(self, after):
        return _copy_wait("rs_chips_wait_" + self.tag, self.sems, self.bufs, _rs_plan_chips, after)[1]


BIG = ("w_in", "w_o", "w_ffn_in", "w_ffn_out")
ORDER = ("w_ada", "b_ada", "w_in", "rel_bias", "attn_norm_g", "lb_logits", "gnorm_g", "w_o", "ln1_g", "ln1_b",
         "w_ffn_in", "w_ffn_out", "ln2_g", "ln2_b")


def kernel(x, c, w_ada, b_ada, w_in, rel_bias, attn_norm_g, lb_logits, gnorm_g, w_o, ln1_g, ln1_b, w_ffn_in, w_ffn_out, ln2_g, ln2_b, loss_target, m_w_ada, m_b_ada, m_w_in, m_rel_bias, m_attn_norm_g, m_lb_logits, m_gnorm_g, m_w_o, m_ln1_g, m_ln1_b, m_w_ffn_in, m_w_ffn_out, m_ln2_g, m_ln2_b, v_w_ada, v_b_ada, v_w_in, v_rel_bias, v_attn_norm_g, v_lb_logits, v_gnorm_g, v_w_o, v_ln1_g, v_ln1_b, v_w_ffn_in, v_w_ffn_out, v_ln2_g, v_ln2_b):
    W = dict(w_ada=w_ada, b_ada=b_ada, w_in=w_in, rel_bias=rel_bias, attn_norm_g=attn_norm_g, lb_logits=lb_logits,
             gnorm_g=gnorm_g, w_o=w_o, ln1_g=ln1_g, ln1_b=ln1_b, w_ffn_in=w_ffn_in, w_ffn_out=w_ffn_out,
             ln2_g=ln2_g, ln2_b=ln2_b)
    M = dict(w_ada=m_w_ada, b_ada=m_b_ada, w_in=m_w_in, rel_bias=m_rel_bias, attn_norm_g=m_attn_norm_g,
             lb_logits=m_lb_logits, gnorm_g=m_gnorm_g, w_o=m_w_o, ln1_g=m_ln1_g, ln1_b=m_ln1_b,
             w_ffn_in=m_w_ffn_in, w_ffn_out=m_w_ffn_out, ln2_g=m_ln2_g, ln2_b=m_ln2_b)
    V = dict(w_ada=v_w_ada, b_ada=v_b_ada, w_in=v_w_in, rel_bias=v_rel_bias, attn_norm_g=v_attn_norm_g,
             lb_logits=v_lb_logits, gnorm_g=v_gnorm_g, w_o=v_w_o, ln1_g=v_ln1_g, ln1_b=v_ln1_b,
             w_ffn_in=v_w_ffn_in, w_ffn_out=v_w_ffn_out, ln2_g=v_ln2_g, ln2_b=v_ln2_b)

    x2, tgt = x[0], loss_target[0]
    T, D = x2.shape
    AW, RW = attn_norm_g.shape[-1], lb_logits.shape[-1]
    MIX = AW + RW
    H, RH = AW // ATTN_HEAD_DIM, RW // LANE
    RB = rel_bias.shape[-1]
    max_rel = (RB - 1) // 2
    rbp = -(-RB // LANE) * LANE
    F = w_ffn_out.shape[1] * N_DEV
    half = N_DEV // 2
    xi, yi, ci = lax.axis_index("x"), lax.axis_index("y"), lax.axis_index("c")
    me = 4 * xi + 2 * yi + ci
    core = jnp.reshape(ci, (1,)).astype(jnp.int32)
    pad_rb = lambda a: jnp.pad(a[0], ((0, 0), (0, rbp - RB)))

    chip = 2 * xi + yi

    c_act, lbv, gv = _prep(c, lb_logits, pad_rb(rel_bias), max_rel)
    c_all = _all_gather(c_act, "ag_c").reshape(N_DEV, D)
    ns_ada = w_ada.shape[-1]
    mod_part = _mod_part(c_all, w_ada[0], lax.dynamic_slice_in_dim(b_ada, me * ns_ada, ns_ada, axis=1))
    mod_all = _all_gather(mod_part, "ag_mod")
    mod6 = lax.dynamic_index_in_dim(mod_all, me, axis=1, keepdims=False).reshape(6, D)

    ag_in = _Gather(w_in[0].astype(BF16), me, "w_in", mod_all)
    ag_o = _Gather(w_o[0].astype(BF16), me, "w_o", ag_in.token)
    ag_f1 = _Gather(w_ffn_in[0].astype(BF16), me, "w_ffn_in", ag_o.token)
    ag_f2 = _Gather(w_ffn_out[0].astype(BF16), me, "w_ffn_out", ag_f1.token)

    h1 = _ln_mod(x2, mod6 + ag_f2.token[0, 0])
    ids = lambda pairs: jnp.stack([4 * px + 2 * py + pc for px, py, pc in pairs]).astype(jnp.int32)
    others = [(1 - xi, yi), (xi, 1 - yi), (1 - xi, 1 - yi)]
    proj = lax.empty((T, w_in.shape[-1] * N_DEV), F32)
    proj = _mm_gathered(h1, ag_in.arrived(h1, (0,)), ids([(xi, yi, ci), (xi, yi, 1 - ci)]), proj, "in_proj_a")
    ag_in.arrived(proj, (1, 2, 3))
    proj = _mm_gathered(h1, ag_in.pass_on(proj, (0, 1, 2)), ids([(*ch, ci) for ch in others]), proj, "in_proj_b")
    wg_in = ag_in.passed(proj, 0)
    proj = _mm_gathered(h1, wg_in, ids([(*ch, 1 - ci) for ch in others]), proj, "in_proj_c")
    ag_o.arrived_from_chips(proj)
    mix_a = _attn_fwd(proj, gv, attn_norm_g, AW)
    wg_o = ag_o.passed_on(mix_a).reshape(MIX, D)
    mix_b, o_b, st_all = _hgrn_fwd(proj, lbv, gnorm_g, AW, RW)
    mixin = jnp.concatenate([mix_a, mix_b], axis=1)
    mix = _mm_nn(mixin, wg_o, "out_proj")
    ag_f1.arrived_from_chips(mix)
    x1, h2 = _mid_fwd(x2, mix, mod6, ln1_g, ln1_b)
    wg_f1 = ag_f1.passed_on(h2)
    gu, act = _mm_swiglu(h2, wg_f1)
    ag_f2.arrived_from_chips(act)
    wg_f2 = ag_f2.passed_on(act).reshape(F, D)
    ff = _mm_nn(act, wg_f2, "ffn_out")
    dff, dx1a, vec_a = _final(x1, ff, mod6, ln2_g, ln2_b, tgt)

    du = _mm_swiglu_bwd(dff, wg_f2, gu)
    rs_f2 = _ReduceScatter(_mm_tn_rows(dff, act, dff, F // N_DEV, "grad_w_ffn_out"), "w_ffn_out")
    tm = _tile(T, 512, 16)
    du_ij = lambda tm_, w, first: pl.BlockSpec((None, tm_, w), lambda i, p: (p // (half // 2), i + first, p % (half // 2)))
    du_jm = lambda tm_, ns: pl.BlockSpec((None, tm_, ns), lambda j, m: (j // half, m, j % half))
    dh2 = _mm_gathered_nt(rs_f2.token, du, du_ij, wg_f1, T, tm, "ffn_in_bwd")
    rs_f2.pair_done(core, chip, dh2)
    tm_red = _tile(T, 1024, 16)
    gw_f1 = _mm_tn_gathered(rs_f2.token, h2, du, du_jm, wg_f1.shape[-1], tm_red, "grad_w_ffn_in")
    rs_f1 = _ReduceScatter(gw_f1.reshape(2, half, D, -1), "w_ffn_in")
    dmix, dxa, vec_b = _mid_bwd(x2, mix, x1, dx1a, dh2, mod6 + rs_f1.token[0, 0], ln1_g)
    dmixin = _mm_nt(dmix, wg_o, "out_proj_bwd")
    rs_f1.pair_done(core, chip, dmixin)
    rs_o = _ReduceScatter(_mm_tn_rows(rs_f1.token, mixin, dmix, MIX // N_DEV, "grad_w_o"), "w_o")
    dq, dk, dv, dgv, dga = _attn_bwd(proj, dmixin, gv + rs_o.token[0, 0], attn_norm_g, AW)
    rs_o.pair_done(core, chip, dq)
    dqb, dfl, dib, dgb, dlb, dgn = _hgrn_bwd(proj, dmixin, o_b, st_all, lbv + rs_o.token[0, 0], gnorm_g, AW, RW)
    dproj = jnp.concatenate([dq, dk, dv, dqb, dfl, dib, dgb], axis=1)
    p_ij = lambda tm_, w, first: pl.BlockSpec((tm_, w), lambda i, p: (i + first, p))
    p_jm = lambda tm_, ns: pl.BlockSpec((tm_, ns), lambda j, m: (m, j))
    gw_in = _mm_tn_gathered(rs_o.token, h1, dproj, p_jm, wg_in.shape[-1], tm_red, "grad_w_in")
    rs_in = _ReduceScatter(gw_in.reshape(2, half, D, -1), "w_in")
    n_tiles = T // tm
    dh1 = _mm_gathered_nt(rs_in.token, dproj, p_ij, wg_in, T, tm, "in_proj_bwd_a", 0, n_tiles // 2)
    rs_in.pair_done(core, chip, dh1)
    dh1 = _mm_gathered_nt(rs_in.token, dproj, p_ij, wg_in, T, tm, "in_proj_bwd_b", n_tiles // 2,
                          n_tiles - n_tiles // 2, dh1)
    grad_x, vec_c = _first_bwd(x2, dh1, dxa, mod6)

    dmod = jnp.concatenate([vec_c[1:2], vec_c[0:1], vec_b[4:5], vec_b[1:2], vec_b[0:1], vec_a[2:3]], axis=0)
    pieces = dict(b_ada=dmod, rel_bias=dgv, attn_norm_g=dga, lb_logits=dlb, gnorm_g=dgn, ln1_g=vec_b[2:3],
                  ln1_b=vec_b[3:4], ln2_g=vec_a[0:1], ln2_b=vec_a[1:2], loss=vec_a[3:4])
    widths = dict(b_ada=(1, 6 * D), rel_bias=(H, TAB), attn_norm_g=(1, AW), lb_logits=(1, RW), gnorm_g=(RH, LANE),
                  ln1_g=(1, D), ln1_b=(1, D), ln2_g=(1, D), ln2_b=(1, D), loss=(1, D))
    packed = jnp.concatenate([pieces[k].reshape(-1, LANE) for k in widths], axis=0)
    small_ag = _SmallGather(packed, me, "small")
    after, res_big = small_ag.token, {}
    for k, rs in (("w_ffn_out", rs_f2), ("w_ffn_in", rs_f1), ("w_o", rs_o), ("w_in", rs_in)):
        four = _adam_shard(rs.sums(after), W[k][0], M[k][0], V[k][0], "adam_" + k)
        res_big[k] = [a[None] for a in four]
        after = four[0]
    gathered = small_ag.done(after)
    parts, r0 = {}, 0
    for k, (rows, width) in widths.items():
        nr = rows * width // LANE
        parts[k] = gathered[:, r0:r0 + nr, :].reshape(N_DEV, rows, width)
        r0 += nr
    prep_small = lambda d, k: pad_rb(d[k]) if k == "rel_bias" else d[k]
    small = _small_update(parts, parts["loss"], lbv, [prep_small(W, k) for k in SMALL],
                          [prep_small(M, k) for k in SMALL], [prep_small(V, k) for k in SMALL], max_rel)
    loss = small[0].reshape(())
    res = {}
    for idx, k in enumerate(SMALL):
        four = small[1 + 4 * idx:5 + 4 * idx]
        if k == "rel_bias":
            four = [a[:, :RB][None] for a in four]
        res[k] = list(four)

    res.update(res_big)
    dmod_s = lax.dynamic_slice_in_dim(parts["b_ada"].reshape(N_DEV, 6 * D), me * ns_ada, ns_ada, axis=1)
    res["w_ada"] = [a[None] for a in _adam_ada(c_all, dmod_s, w_ada[0], m_w_ada[0], v_w_ada[0])]

    out = [loss, grad_x[None]]
    for field in range(4):
        out += [res[k][field] for k in ORDER]
    return tuple(out)
```

```python
import jax
import jax.numpy as jnp
from jax import lax
from jax.experimental import pallas as pl
from jax.experimental.pallas import tpu as pltpu

F32 = jnp.float32
BF16 = jnp.bfloat16
MESH = pl.DeviceIdType.MESH
HIGHEST = lax.Precision.HIGHEST

N_DEV = 8
CHUNK = 64
N_PAST = 8
QBLK = 4 * CHUNK
KPAD = N_PAST * CHUNK
WIN = KPAD + QBLK
TAB = 1024
ATTN_HEAD_DIM = 64
ATTN_HEADS_PER_STEP = 4
SUB = 32
ROWS = 8
LANE = 128
EPS = 1e-5
ALPHA = 2.0 ** 0.25
ADAM_LR, ADAM_B1, ADAM_B2, ADAM_EPS, ADAM_WD, ADAM_STEP = 0.001, 0.9, 0.999, 1e-08, 0.01, 10
NEG = -1e30
VMEM_LIMIT = 56 * 1024 * 1024


def _sds(shape, dtype):
    return jax.ShapeDtypeStruct(tuple(shape), dtype)


def _tile(n, pref, mult):
    best = None
    for t in range(mult, min(n, pref) + 1, mult):
        if n % t == 0:
            best = t
    return n if best is None else best


def _params(sem=None, big=False):
    kw = {}
    if sem is not None:
        kw["dimension_semantics"] = sem
    if big:
        kw["vmem_limit_bytes"] = VMEM_LIMIT
    return pltpu.CompilerParams(**kw)


def _sigmoid(v):
    return 1.0 / (1.0 + jnp.exp(-v))


def _dot(a, b, dims, precision=None):
    return lax.dot_general(a, b, (dims, ((), ())), preferred_element_type=F32, precision=precision)


NN = ((1,), (0,))
NT = ((1,), (1,))
TN = ((0,), (0,))


def _ln(v):
    mu = jnp.mean(v, axis=-1, keepdims=True)
    d = v - mu
    rstd = lax.rsqrt(jnp.mean(d * d, axis=-1, keepdims=True) + EPS)
    return d * rstd, rstd


def _ln_bwd(dxh, xh, rstd):
    return rstd * (dxh - jnp.mean(dxh, axis=-1, keepdims=True) - xh * jnp.mean(dxh * xh, axis=-1, keepdims=True))


def _colsum(v):
    return jnp.sum(v, axis=0, keepdims=True)


def _ln_mod(x2, mod6):
    T, D = x2.shape
    tm = _tile(T, 256, 8)

    def body(x_ref, mod_ref, o_ref):
        xh, _ = _ln(x_ref[...])
        o_ref[...] = (xh * (1.0 + mod_ref[1:2, :]) + mod_ref[0:1, :]).astype(BF16)

    return pl.pallas_call(
        body, grid=(T // tm,), name="ln_mod",
        in_specs=[pl.BlockSpec((tm, D), lambda i: (i, 0)), pl.BlockSpec((6, D), lambda i: (0, 0))],
        out_specs=pl.BlockSpec((tm, D), lambda i: (i, 0)),
        out_shape=_sds((T, D), BF16), compiler_params=_params(("parallel",)),
    )(x2, mod6)


def _mid_fwd(x2, mix, mod6, ln1_g, ln1_b):
    T, D = x2.shape
    tm = _tile(T, 256, 8)

    def body(x_ref, mix_ref, mod_ref, g_ref, b_ref, x1_ref, h2_ref):
        zh, _ = _ln(ALPHA * x_ref[...] + mod_ref[2:3, :] * mix_ref[...])
        x1 = zh * g_ref[...] + b_ref[...]
        x1_ref[...] = x1
        xh, _ = _ln(x1)
        h2_ref[...] = (xh * (1.0 + mod_ref[4:5, :]) + mod_ref[3:4, :]).astype(BF16)

    row = pl.BlockSpec((tm, D), lambda i: (i, 0))
    vec = pl.BlockSpec((1, D), lambda i: (0, 0))
    return pl.pallas_call(
        body, grid=(T // tm,), name="mid_fwd",
        in_specs=[row, row, pl.BlockSpec((6, D), lambda i: (0, 0)), vec, vec],
        out_specs=[row, row],
        out_shape=[_sds((T, D), F32), _sds((T, D), BF16)], compiler_params=_params(("parallel",)),
    )(x2, mix, mod6, ln1_g, ln1_b)


def _final(x1, ff, mod6, ln2_g, ln2_b, tgt):
    T, D = x1.shape
    tm = _tile(T, 256, 8)

    def body(x1_ref, ff_ref, mod_ref, g_ref, b_ref, t_ref, dff_ref, dx1_ref, vec_ref):
        @pl.when(pl.program_id(0) == 0)
        def _():
            vec_ref[...] = jnp.zeros_like(vec_ref)

        ff_v = ff_ref[...]
        gate2 = mod_ref[5:6, :]
        zh, rstd = _ln(ALPHA * x1_ref[...] + gate2 * ff_v)
        err = zh * g_ref[...] + b_ref[...] - t_ref[...]
        dy = err * (1.0 / D)
        dz = _ln_bwd(dy * g_ref[...], zh, rstd)
        dff_ref[...] = (gate2 * dz).astype(BF16)
        dx1_ref[...] = ALPHA * dz
        vec_ref[0:1, :] += _colsum(dy * zh)
        vec_ref[1:2, :] += _colsum(dy)
        vec_ref[2:3, :] += _colsum(dz * ff_v)
        vec_ref[3:4, :] += _colsum(err * err) * (0.5 / D)

    row = pl.BlockSpec((tm, D), lambda i: (i, 0))
    vec = pl.BlockSpec((1, D), lambda i: (0, 0))
    return pl.pallas_call(
        body, grid=(T // tm,), name="final_fwd_bwd",
        in_specs=[row, row, pl.BlockSpec((6, D), lambda i: (0, 0)), vec, vec, row],
        out_specs=[row, row, pl.BlockSpec((8, D), lambda i: (0, 0))],
        out_shape=[_sds((T, D), BF16), _sds((T, D), F32), _sds((8, D), F32)],
        compiler_params=_params(("arbitrary",)),
    )(x1, ff, mod6, ln2_g, ln2_b, tgt)


def _mid_bwd(x2, mix, x1, dx1a, dh2, mod6, ln1_g):
    T, D = x2.shape
    tm = _tile(T, 256, 8)

    def body(x_ref, mix_ref, x1_ref, dx1a_ref, dh2_ref, mod_ref, g_ref, dmix_ref, dxa_ref, vec_ref):
        @pl.when(pl.program_id(0) == 0)
        def _():
            vec_ref[...] = jnp.zeros_like(vec_ref)

        dh2 = dh2_ref[...]
        xh, rstd = _ln(x1_ref[...])
        dx1 = dx1a_ref[...] + _ln_bwd(dh2 * (1.0 + mod_ref[4:5, :]), xh, rstd)
        mix_v = mix_ref[...]
        gate1 = mod_ref[2:3, :]
        zh, rstdz = _ln(ALPHA * x_ref[...] + gate1 * mix_v)
        dz = _ln_bwd(dx1 * g_ref[...], zh, rstdz)
        dmix_ref[...] = (gate1 * dz).astype(BF16)
        dxa_ref[...] = ALPHA * dz
        vec_ref[0:1, :] += _colsum(dh2 * xh)
        vec_ref[1:2, :] += _colsum(dh2)
        vec_ref[2:3, :] += _colsum(dx1 * zh)
        vec_ref[3:4, :] += _colsum(dx1)
        vec_ref[4:5, :] += _colsum(dz * mix_v)

    row = pl.BlockSpec((tm, D), lambda i: (i, 0))
    vec = pl.BlockSpec((1, D), lambda i: (0, 0))
    return pl.pallas_call(
        body, grid=(T // tm,), name="mid_bwd",
        in_specs=[row, row, row, row, row, pl.BlockSpec((6, D), lambda i: (0, 0)), vec],
        out_specs=[row, row, pl.BlockSpec((8, D), lambda i: (0, 0))],
        out_shape=[_sds((T, D), BF16), _sds((T, D), F32), _sds((8, D), F32)],
        compiler_params=_params(("arbitrary",)),
    )(x2, mix, x1, dx1a, dh2, mod6, ln1_g)


def _first_bwd(x2, dh1, dxa, mod6):
    T, D = x2.shape
    tm = _tile(T, 256, 8)

    def body(x_ref, dh1_ref, dxa_ref, mod_ref, gx_ref, vec_ref):
        @pl.when(pl.program_id(0) == 0)
        def _():
            vec_ref[...] = jnp.zeros_like(vec_ref)

        dh1 = dh1_ref[...]
        xh, rstd = _ln(x_ref[...])
        gx_ref[...] = dxa_ref[...] + _ln_bwd(dh1 * (1.0 + mod_ref[1:2, :]), xh, rstd)
        vec_ref[0:1, :] += _colsum(dh1 * xh)
        vec_ref[1:2, :] += _colsum(dh1)

    row = pl.BlockSpec((tm, D), lambda i: (i, 0))
    return pl.pallas_call(
        body, grid=(T // tm,), name="first_bwd",
        in_specs=[row, row, row, pl.BlockSpec((6, D), lambda i: (0, 0))],
        out_specs=[row, pl.BlockSpec((8, D), lambda i: (0, 0))],
        out_shape=[_sds((T, D), F32), _sds((8, D), F32)],
        compiler_params=_params(("arbitrary",)),
    )(x2, dh1, dxa, mod6)


def _slot(j):
    return (j % 2) * 4 + j // 2


def _mm_gathered(a, wg, shards, out, name):
    M, K = a.shape
    _, _, ns = wg.shape
    tm = _tile(M, 512, 16)

    def body(shards_ref, a_ref, w_ref, prev_ref, o_ref):
        o_ref[...] = _dot(a_ref[...], w_ref[...], NN)

    return pl.pallas_call(
        body, name=name,
        grid_spec=pltpu.PrefetchScalarGridSpec(
            num_scalar_prefetch=1, grid=(shards.shape[0], M // tm),
            in_specs=[pl.BlockSpec((tm, K), lambda j, i, s: (i, 0)),
                      pl.BlockSpec((None, K, ns), lambda j, i, s: (s[j], 0, 0)), ORDER_ONLY],
            out_specs=pl.BlockSpec((tm, ns), lambda j, i, s: (i, s[j]))),
        out_shape=_sds((M, N_DEV * ns), F32), input_output_aliases={3: 0},
        compiler_params=_params(("parallel", "parallel"), big=True),
    )(shards, a, wg, out)


def _mm_nn(a, b, name):
    M, K = a.shape
    _, N = b.shape
    tm, tn = _tile(M, 512, 16), _tile(N, 1024, LANE)

    def body(a_ref, b_ref, o_ref):
        o_ref[...] = _dot(a_ref[...], b_ref[...], NN)

    return pl.pallas_call(
        body, grid=(N // tn, M // tm), name=name,
        in_specs=[pl.BlockSpec((tm, K), lambda j, i: (i, 0)), pl.BlockSpec((K, tn), lambda j, i: (0, j))],
        out_specs=pl.BlockSpec((tm, tn), lambda j, i: (i, j)),
        out_shape=_sds((M, N), F32), compiler_params=_params(("parallel", "parallel"), big=True),
    )(a, b)


def _mm_nt(a, b, name):
    M, K = a.shape
    N, _ = b.shape
    tm, tn = _tile(M, 512, 16), _tile(N, 1024, LANE)

    def body(a_ref, b_ref, o_ref):
        o_ref[...] = _dot(a_ref[...], b_ref[...], NT)

    return pl.pallas_call(
        body, grid=(M // tm, N // tn), name=name,
        in_specs=[pl.BlockSpec((tm, K), lambda i, j: (i, 0)), pl.BlockSpec((tn, K), lambda i, j: (j, 0))],
        out_specs=pl.BlockSpec((tm, tn), lambda i, j: (i, j)),
        out_shape=_sds((M, N), F32), compiler_params=_params(("parallel", "parallel"), big=True),
    )(a, b)


def _mm_swiglu(h2, wg):
    M, K = h2.shape
    _, _, ns = wg.shape
    half = N_DEV // 2
    tm = _tile(M, 256, 16)

    def body(a_ref, wgate_ref, wup_ref, gu_ref, act_ref):
        a = a_ref[...]
        g = _dot(a, wgate_ref[...], NN)
        u = _dot(a, wup_ref[...], NN)
        gu_ref[0] = g
        gu_ref[1] = u
        act_ref[...] = (g * _sigmoid(g) * u).astype(BF16)

    return pl.pallas_call(
        body, grid=(half, M // tm), name="ffn_in_swiglu",
        in_specs=[pl.BlockSpec((tm, K), lambda j, i: (i, 0)),
                  pl.BlockSpec((None, K, ns), lambda j, i: (j, 0, 0)),
                  pl.BlockSpec((None, K, ns), lambda j, i: (j + half, 0, 0))],
        out_specs=[pl.BlockSpec((2, tm, ns), lambda j, i: (0, i, j)), pl.BlockSpec((tm, ns), lambda j, i: (i, j))],
        out_shape=[_sds((2, M, half * ns), F32), _sds((M, half * ns), BF16)],
        compiler_params=_params(("parallel", "parallel"), big=True),
    )(h2, wg, wg)


def _mm_swiglu_bwd(dff, w2, gu):
    M, K = dff.shape
    F = w2.shape[0]
    tm, tn = _tile(M, 512, 16), _tile(F, 1408, LANE)

    def body(a_ref, b_ref, gu_ref, du_ref):
        da = _dot(a_ref[...], b_ref[...], NT)
        g = gu_ref[0]
        u = gu_ref[1]
        sg = _sigmoid(g)
        du_ref[0] = (da * u * (sg * (1.0 + g * (1.0 - sg)))).astype(BF16)
        du_ref[1] = (da * (g * sg)).astype(BF16)

    return pl.pallas_call(
        body, grid=(F // tn, M // tm), name="ffn_out_bwd_swiglu",
        in_specs=[pl.BlockSpec((tm, K), lambda j, i: (i, 0)), pl.BlockSpec((tn, K), lambda j, i: (j, 0)),
                  pl.BlockSpec((2, tm, tn), lambda j, i: (0, i, j))],
        out_specs=pl.BlockSpec((2, tm, tn), lambda j, i: (0, i, j)),
        out_shape=_sds((2, M, F), BF16), compiler_params=_params(("parallel", "parallel"), big=True),
    )(dff, w2, gu)


ORDER_ONLY = pl.BlockSpec(memory_space=pl.ANY)


def _mm_tn_rows(dep, a, b, rs, name):
    M, Ka = a.shape
    _, N = b.shape

    def body(_, a_ref, b_ref, o_ref):
        g = _dot(a_ref[...], b_ref[...], TN)
        o_ref[0, 0] = g[0:rs, :].astype(BF16)
        o_ref[1, 0] = g[rs:2 * rs, :].astype(BF16)

    return pl.pallas_call(
        body, grid=(N_DEV // 2,), name=name,
        in_specs=[ORDER_ONLY, pl.BlockSpec((M, 2 * rs), lambda ch: (0, ch)), pl.BlockSpec((M, N), lambda ch: (0, 0))],
        out_specs=pl.BlockSpec((2, 1, rs, N), lambda ch: (0, ch, 0, 0)),
        out_shape=_sds((2, N_DEV // 2, rs, N), BF16),
        compiler_params=_params(("parallel",), big=True),
    )(dep, a, b)


def _mm_gathered_nt(dep, a, a_spec, wg, M, tm, name, first=0, count=None, out=None):
    _, K, ns = wg.shape
    count = M // tm if count is None else count
    out = lax.empty((M, K), F32) if out is None else out

    def body(_, a_ref, w_ref, prev_ref, o_ref):
        @pl.when(pl.program_id(1) == 0)
        def _():
            o_ref[...] = jnp.zeros_like(o_ref)

        o_ref[...] += _dot(a_ref[:, 0:ns], w_ref[0], NT) + _dot(a_ref[:, ns:2 * ns], w_ref[1], NT)

    return pl.pallas_call(
        body, grid=(count, N_DEV // 2), name=name,
        in_specs=[ORDER_ONLY, a_spec(tm, 2 * ns, first), pl.BlockSpec((2, K, ns), lambda i, p: (p, 0, 0)), ORDER_ONLY],
        out_specs=pl.BlockSpec((tm, K), lambda i, j: (i + first, 0)),
        out_shape=_sds((M, K), F32), input_output_aliases={3: 0},
        compiler_params=_params(("parallel", "arbitrary"), big=True),
    )(dep, a, wg, out)


def _mm_tn_gathered(dep, h, a, a_spec, ns, name):
    M, K = h.shape

    def body(_, h_ref, a_ref, o_ref):
        o_ref[...] = _dot(h_ref[...], a_ref[...], TN).astype(BF16)

    return pl.pallas_call(
        body, grid=(N_DEV,), name=name,
        in_specs=[ORDER_ONLY, pl.BlockSpec((M, K), lambda j: (0, 0)), a_spec(M, ns)],
        out_specs=pl.BlockSpec((None, K, ns), lambda j: (_slot(j), 0, 0)),
        out_shape=_sds((N_DEV, K, ns), BF16),
        compiler_params=_params(("parallel",), big=True),
    )(dep, h, a)


def _bias_onehot(rbp, max_rel):
    r = lax.broadcasted_iota(jnp.int32, (rbp, TAB), 0)
    m = lax.broadcasted_iota(jnp.int32, (rbp, TAB), 1)
    dist = KPAD - jnp.where(m < WIN, m, m - TAB)
    return (r == jnp.clip(dist, -max_rel, max_rel) + max_rel).astype(F32)


def _attn_setup(i, hp, k_ref, v_ref, gv_ref, kpad, vpad, bias):
    ls = slice(i * ATTN_HEAD_DIM, (i + 1) * ATTN_HEAD_DIM)
    kpad[i][0:KPAD, :] = jnp.zeros((KPAD, ATTN_HEAD_DIM), BF16)
    vpad[i][0:KPAD, :] = jnp.zeros((KPAD, ATTN_HEAD_DIM), BF16)
    kpad[i][KPAD:, :] = k_ref[:, ls].astype(BF16)
    vpad[i][KPAD:, :] = v_ref[:, ls].astype(BF16)
    gvrow = gv_ref[pl.ds(hp * ATTN_HEADS_PER_STEP + i, 1), :]
    tab = pltpu.roll(jnp.broadcast_to(gvrow, (QBLK, TAB)), 0, 1, stride=1, stride_axis=0)
    row = lax.broadcasted_iota(jnp.int32, (QBLK, WIN), 0)
    col = lax.broadcasted_iota(jnp.int32, (QBLK, WIN), 1)
    first = jnp.bitwise_and(row, -CHUNK)
    seen = jnp.logical_and(col >= first, col < first + (N_PAST + 1) * CHUNK)
    bias[i][...] = jnp.where(seen, tab[:, 0:WIN], NEG)


def _attn_probs(b, q_ref, kpad, vpad, bias, col):
    pair = range(ATTN_HEADS_PER_STEP)
    ls = [slice(i * ATTN_HEAD_DIM, (i + 1) * ATTN_HEAD_DIM) for i in pair]
    r0 = pl.multiple_of(b * QBLK, QBLK)
    q = [q_ref[pl.ds(r0, QBLK), ls[i]].astype(BF16) for i in pair]
    kw = [kpad[i][pl.ds(r0, WIN), :] for i in pair]
    vw = [vpad[i][pl.ds(r0, WIN), :] for i in pair]
    s = [_dot(q[i], kw[i], NT) * (ATTN_HEAD_DIM ** -0.5) + bias[i][...] for i in pair]
    s = [jnp.where(col >= KPAD - r0, s[i], NEG) for i in pair]
    p = [jnp.exp(s[i] - jnp.max(s[i], axis=-1, keepdims=True)) for i in pair]
    pn = [p[i] / jnp.sum(p[i], axis=-1, keepdims=True) for i in pair]
    return r0, ls, q, kw, vw, pn


def _attn_fwd(proj, gv, ga, AW):
    T = proj.shape[0]
    AH = ATTN_HEADS_PER_STEP
    W = AH * ATTN_HEAD_DIM
    HP = AW // W

    def body(q_ref, k_ref, v_ref, gv_ref, ga_ref, o_ref, *scratch):
        kpad, vpad, bias = (scratch[k * AH:(k + 1) * AH] for k in range(3))
        hp = pl.program_id(0)
        for i in range(AH):
            _attn_setup(i, hp, k_ref, v_ref, gv_ref, kpad, vpad, bias)
        col = lax.broadcasted_iota(jnp.int32, (QBLK, WIN), 1)

        def block(b, carry):
            pair = range(AH)
            r0, ls, _, _, vw, pn = _attn_probs(b, q_ref, kpad, vpad, bias, col)
            o = [_dot(pn[i].astype(BF16), vw[i], NN) for i in pair]
            r = [lax.rsqrt(jnp.mean(o[i] * o[i], axis=-1, keepdims=True) + EPS) for i in pair]
            outs = [o[i] * r[i] * ga_ref[0:1, ls[i]] for i in pair]
            o_ref[pl.ds(r0, QBLK), :] = jnp.concatenate(outs, axis=1).astype(BF16)
            return carry

        lax.fori_loop(0, T // QBLK, block, 0)

    blk = lambda off: pl.BlockSpec((T, W), lambda hp: (0, off + hp))
    return pl.pallas_call(
        body, grid=(HP,), name="attn_fwd",
        in_specs=[blk(0), blk(HP), blk(2 * HP), pl.BlockSpec(gv.shape, lambda hp: (0, 0)),
                  pl.BlockSpec((1, W), lambda hp: (0, hp))],
        out_specs=pl.BlockSpec((T, W), lambda hp: (0, hp)),
        out_shape=_sds((T, AW), BF16),
        scratch_shapes=[pltpu.VMEM((T + KPAD, ATTN_HEAD_DIM), BF16)] * (2 * AH) + [pltpu.VMEM((QBLK, WIN), F32)] * AH,
        compiler_params=_params(("parallel",), big=True),
    )(proj, proj, proj, gv, ga)


def _attn_bwd(proj, dmixin, gv, ga, AW):
    T = proj.shape[0]
    AH = ATTN_HEADS_PER_STEP
    W = AH * ATTN_HEAD_DIM
    HP = AW // W
    scale = ATTN_HEAD_DIM ** -0.5

    def body(q_ref, k_ref, v_ref, dn_ref, gv_ref, ga_ref, dq_ref, dk_ref, dv_ref, dgv_ref, dga_ref, *scratch):
        kpad, vpad, dkacc, dvacc, bias, dbias = (scratch[k * AH:(k + 1) * AH] for k in range(6))
        hp = pl.program_id(0)
        for i in range(AH):
            _attn_setup(i, hp, k_ref, v_ref, gv_ref, kpad, vpad, bias)
            dkacc[i][...] = jnp.zeros_like(dkacc[i])
            dvacc[i][...] = jnp.zeros_like(dvacc[i])
            dbias[i][...] = jnp.zeros_like(dbias[i])
        dga_ref[...] = jnp.zeros_like(dga_ref)
        col = lax.broadcasted_iota(jnp.int32, (QBLK, WIN), 1)

        def block(b, carry):
            pair = range(AH)
            r0, lss, qs, kws, vws, pns = _attn_probs(b, q_ref, kpad, vpad, bias, col)
            pn_b = [pns[i].astype(BF16) for i in pair]
            o = [_dot(pn_b[i], vws[i], NN) for i in pair]
            r = [lax.rsqrt(jnp.mean(o[i] * o[i], axis=-1, keepdims=True) + EPS) for i in pair]
            dn = [dn_ref[pl.ds(r0, QBLK), lss[i]] for i in pair]
            for i in pair:
                dga_ref[i:i + 1, :] += _colsum(dn[i] * o[i] * r[i])
            a = [dn[i] * ga_ref[0:1, lss[i]] for i in pair]
            do_b = [(r[i] * (a[i] - o[i] * (r[i] * r[i]) * jnp.mean(a[i] * o[i], axis=-1, keepdims=True))).astype(BF16)
                    for i in pair]
            dp = [_dot(do_b[i], vws[i], NT) for i in pair]
            for i in pair:
                dvacc[i][pl.ds(r0, WIN), :] += _dot(pn_b[i], do_b[i], TN)
            ds = [pns[i] * (dp[i] - jnp.sum(pns[i] * dp[i], axis=-1, keepdims=True)) for i in pair]
            for i in pair:
                dbias[i][...] += ds[i]
            ds_b = [ds[i].astype(BF16) for i in pair]
            dq = [_dot(ds_b[i], kws[i], NN) * scale for i in pair]
            dq_ref[pl.ds(r0, QBLK), :] = jnp.concatenate(dq, axis=1).astype(BF16)
            for i in pair:
                dkacc[i][pl.ds(r0, WIN), :] += _dot(ds_b[i], qs[i], TN) * scale
            return carry

        lax.fori_loop(0, T // QBLK, block, 0)

        rr = lax.broadcasted_iota(jnp.int32, (QBLK, QBLK), 0)
        cc = lax.broadcasted_iota(jnp.int32, (QBLK, QBLK), 1)
        flip = (rr + cc == QBLK - 1).astype(BF16)
        for i in range(AH):
            ls = slice(i * ATTN_HEAD_DIM, (i + 1) * ATTN_HEAD_DIM)
            dk_ref[:, ls] = dkacc[i][KPAD:, :].astype(BF16)
            dv_ref[:, ls] = dvacc[i][KPAD:, :].astype(BF16)
            full = jnp.concatenate([dbias[i][...], jnp.zeros((QBLK, TAB - WIN), F32)], axis=1)
            hi = full.astype(BF16)
            lo = (full - hi.astype(F32)).astype(BF16)
            rev = _dot(flip, hi, NN) + _dot(flip, lo, NN)
            dgv_ref[i:i + 1, :] = _colsum(pltpu.roll(rev, TAB - (QBLK - 1), 1, stride=1, stride_axis=0))

    blk = lambda off: pl.BlockSpec((T, W), lambda hp: (0, off + hp))
    accs = lambda dt: [pltpu.VMEM((T + KPAD, ATTN_HEAD_DIM), dt)] * AH
    return pl.pallas_call(
        body, grid=(HP,), name="attn_bwd",
        in_specs=[blk(0), blk(HP), blk(2 * HP), blk(0), pl.BlockSpec(gv.shape, lambda hp: (0, 0)),
                  pl.BlockSpec((1, W), lambda hp: (0, hp))],
        out_specs=[blk(0), blk(0), blk(0), pl.BlockSpec((None, AH, TAB), lambda hp: (hp, 0, 0)),
                   pl.BlockSpec((None, AH, ATTN_HEAD_DIM), lambda hp: (hp, 0, 0))],
        out_shape=[_sds((T, AW), BF16), _sds((T, AW), BF16), _sds((T, AW), BF16),
                   _sds((HP, AH, TAB), F32), _sds((HP, AH, ATTN_HEAD_DIM), F32)],
        scratch_shapes=accs(BF16) + accs(BF16) + accs(F32) + accs(F32) + [pltpu.VMEM((QBLK, WIN), F32)] * (2 * AH),
        compiler_params=_params(("parallel",), big=True),
    )(proj, proj, proj, dmixin, gv, ga)


def _ltri():
    r = lax.broadcasted_iota(jnp.int32, (CHUNK, CHUNK), 0)
    c = lax.broadcasted_iota(jnp.int32, (CHUNK, CHUNK), 1)
    return (c <= r).astype(BF16)


def _tri_dot(tri, v, dims):
    hi = v.astype(BF16)
    lo = (v - hi.astype(F32)).astype(BF16)
    return _dot(tri, hi, dims) + _dot(tri, lo, dims)


HEADS_PER_STEP = 2


def _alternate(stages):
    live = list(stages)
    while live:
        for g in list(live):
            if next(g, StopIteration) is StopIteration:
                live.remove(g)


def _hgrn_gates(n, ls, q_ref, f_ref, lb_ref, ltri):
    r0 = pl.multiple_of(n * CHUNK, CHUNK)
    rows = pl.ds(r0, CHUNK)
    lb = lb_ref[:, ls]
    qb = q_ref[rows, ls]
    sg = _sigmoid(f_ref[rows, ls])
    f = lb + (1.0 - lb) * sg
    sq = _sigmoid(qb)
    b = _tri_dot(ltri, jnp.log(f), NN)
    return rows, lb, qb, sg, f, 1.0 - f, sq, qb * sq, b


def _hgrn_specs(T, RW, AW):
    HG = HEADS_PER_STEP
    W = HG * LANE
    base = 3 * AW // W
    blk_in = lambda off: pl.BlockSpec((T, W), lambda g: (0, base + off + g))
    col = pl.BlockSpec((T, W), lambda g: (0, g))
    return HG, W, RW // W, blk_in, col


def _hgrn_fwd(proj, lb, gn, AW, RW):
    T = proj.shape[0]
    RH, NC, NSUB = RW // LANE, T // CHUNK, CHUNK // SUB
    HG, W, NG, blk_in, col = _hgrn_specs(T, RW, AW)

    def body(q_ref, f_ref, i_ref, g_ref, lb_ref, gn_ref, mix_ref, o_ref, stall_ref, st_all, bs_all, kks_all, ics_all):
        st_all[...] = jnp.zeros_like(st_all)
        ltri = _ltri()
        rowi = lax.broadcasted_iota(jnp.int32, (SUB, 1), 0)

        def one_head(h, n):
            ls = slice(h * LANE, (h + 1) * LANE)
            st, bs, kks, ics = st_all.at[h], bs_all.at[h], kks_all.at[h], ics_all.at[h]
            rows, _, _, _, _, kk, _, qs, b = _hgrn_gates(n, ls, q_ref, f_ref, lb_ref, ltri)
            ic = i_ref[rows, ls]
            stv = st[...]
            stall_ref[h, n] = stv
            bs[...] = b
            kks[...] = kk
            ics[...] = ic
            yield
            o = _dot((qs * jnp.exp(b)).astype(BF16), stv.astype(BF16), NT)
            yield
            ic_b = ic.astype(BF16)
            pieces = []
            for blk in range(NSUB):
                s0 = blk * SUB
                bI, qI = b[s0:s0 + SUB], qs[s0:s0 + SUB]
                if blk == 0:
                    oI = jnp.zeros((SUB, LANE), F32)
                else:
                    ref = bs[s0 - 1:s0, :]
                    qt = (qI * jnp.exp(bI - ref)).astype(BF16)
                    kt = (kk[0:s0] * jnp.exp(ref - b[0:s0])).astype(BF16)
                    oI = _dot(_dot(qt, kt, NT).astype(BF16), ic_b[0:s0], NN)
                    yield
                acc = [oI[g * ROWS:(g + 1) * ROWS] for g in range(SUB // ROWS)]
                for s in range(SUB):
                    sr = s0 + s
                    g0 = s // ROWS
                    lo = g0 * ROWS
                    e = jnp.exp(jnp.minimum(bI[lo:] - bs[sr:sr + 1, :], 0.0))
                    a = jnp.sum(qI[lo:] * kks[sr:sr + 1, :] * e, axis=-1, keepdims=True)
                    add = jnp.where(rowi[lo:] >= s, a, 0.0) * ics[sr:sr + 1, :]
                    for g in range(g0, SUB // ROWS):
                        acc[g] = acc[g] + add[(g - g0) * ROWS:(g - g0 + 1) * ROWS]
                    yield
                pieces.extend(acc)
            o = o + jnp.concatenate(pieces, axis=0)
            bl = bs[CHUNK - 1:CHUNK, :]
            kd = (kk * jnp.exp(bl - b)).astype(BF16)
            st[...] = stv * jnp.exp(bl) + _dot(ic_b, kd, TN)
            yield
            o_ref[rows, ls] = o
            r = lax.rsqrt(jnp.mean(o * o, axis=-1, keepdims=True) + EPS)
            gb = g_ref[rows, ls]
            mix_ref[rows, ls] = (o * r * gn_ref[...] * (gb * _sigmoid(gb))).astype(BF16)

        def chunk(n, carry):
            _alternate([one_head(h, n) for h in range(HG)])
            return carry

        lax.fori_loop(0, NC, chunk, 0)

    tile = pltpu.VMEM((HG, CHUNK, LANE), F32)
    return pl.pallas_call(
        body, grid=(NG,), name="hgrn_fwd",
        in_specs=[blk_in(0), blk_in(NG), blk_in(2 * NG), blk_in(3 * NG), pl.BlockSpec((1, W), lambda g: (0, g)),
                  pl.BlockSpec((1, LANE), lambda g: (0, 0))],
        out_specs=[col, col, pl.BlockSpec((HG, NC, LANE, LANE), lambda g: (g, 0, 0, 0))],
        out_shape=[_sds((T, RW), BF16), _sds((T, RW), F32), _sds((RH, NC, LANE, LANE), F32)],
        scratch_shapes=[pltpu.VMEM((HG, LANE, LANE), F32), tile, tile, tile],
        compiler_params=_params(("parallel",), big=True),
    )(proj, proj, proj, proj, lb, gn)


def _hgrn_bwd(proj, dmixin, o_b, st_all, lb, gn, AW, RW):
    T = proj.shape[0]
    RH, NC, NSUB = RW // LANE, T // CHUNK, CHUNK // SUB
    HG, W, NG, blk_in, col = _hgrn_specs(T, RW, AW)

    def body(q_ref, f_ref, i_ref, g_ref, o_ref, dn_ref, stall_ref, lb_ref, gn_ref,
             dq_ref, df_ref, di_ref, dg_ref, dlb_ref, dgn_ref, dst_all, bs_all, qss_all, dos_all, p2_all, dic_all,
             p1_all):
        dst_all[...] = jnp.zeros_like(dst_all)
        dlb_ref[...] = jnp.zeros_like(dlb_ref)
        dgn_ref[...] = jnp.zeros_like(dgn_ref)
        ltri = _ltri()
        rowi = lax.broadcasted_iota(jnp.int32, (SUB, 1), 0)
        last = lax.broadcasted_iota(jnp.int32, (CHUNK, 1), 0) == CHUNK - 1

        def one_head(h, n):
            ls = slice(h * LANE, (h + 1) * LANE)
            dst, bs, qss, dos = dst_all.at[h], bs_all.at[h], qss_all.at[h], dos_all.at[h]
            p2, dic, p1s = p2_all.at[h], dic_all.at[h], p1_all.at[h]
            rows, lbv, qb, sg, f, kk, sq, qs, b = _hgrn_gates(n, ls, q_ref, f_ref, lb_ref, ltri)
            ic = i_ref[rows, ls]
            stv = stall_ref[h, n]
            dstv = dst[...]
            o = o_ref[rows, ls]
            dn = dn_ref[rows, ls]
            gb = g_ref[rows, ls]
            sgb = _sigmoid(gb)
            r = lax.rsqrt(jnp.mean(o * o, axis=-1, keepdims=True) + EPS)
            gnv = gn_ref[...]
            dg_ref[rows, ls] = (dn * (o * r * gnv) * (sgb * (1.0 + gb * (1.0 - sgb)))).astype(BF16)
            dy = dn * (gb * sgb)
            dgn_ref[h] += _colsum(dy * o * r)
            a_ = dy * gnv
            do = r * (a_ - o * (r * r) * jnp.mean(a_ * o, axis=-1, keepdims=True))
            do_b = do.astype(BF16)
            bs[...] = b
            qss[...] = qs
            dos[...] = do
            yield
            ic_b = ic.astype(BF16)
            eb = jnp.exp(b)
            bl = bs[CHUNK - 1:CHUNK, :]
            ebl = jnp.exp(bl)
            dec = jnp.exp(bl - b)
            kd = (kk * dec).astype(BF16)
            dst_b = dstv.astype(BF16)
            dqs = _dot(do_b, stv.astype(BF16), NN) * eb
            dkk2 = _dot(ic_b, dst_b, NN) * dec
            dic[...] = _dot(kd, dst_b, NT)
            dbl = ebl * _colsum(stv * dstv) + _colsum(kk * dkk2)
            dst[...] = dstv * ebl + _dot(do_b, (qs * eb).astype(BF16), TN)
            yield
            p2[...] = jnp.zeros_like(p2)
            p1_pieces = []
            for blk in range(NSUB):
                s0 = blk * SUB
                bI, qI, doI = b[s0:s0 + SUB], qs[s0:s0 + SUB], do[s0:s0 + SUB]
                if blk == 0:
                    p1 = jnp.zeros((SUB, LANE), F32)
                else:
                    ref = bs[s0 - 1:s0, :]
                    eq = jnp.exp(bI - ref)
                    ek = jnp.exp(ref - b[0:s0])
                    qt = (qI * eq).astype(BF16)
                    kt = (kk[0:s0] * ek).astype(BF16)
                    doI_b = doI.astype(BF16)
                    dic[0:s0, :] += _dot(_dot(qt, kt, NT).astype(BF16), doI_b, TN)
                    da = _dot(doI_b, ic_b[0:s0], NT).astype(BF16)
                    p1 = _dot(da, kt, NN) * eq
                    p2[0:s0, :] += _dot(da, qt, TN) * ek
                    yield
                p1_pieces.append(p1)
                kkI, icI = kk[s0:s0 + SUB], ic[s0:s0 + SUB]
                p2acc = [jnp.zeros((ROWS, LANE), F32) for _ in range(SUB // ROWS)]
                diacc = [jnp.zeros((ROWS, LANE), F32) for _ in range(SUB // ROWS)]
                for t in range(SUB):
                    tr = s0 + t
                    ng = t // ROWS + 1
                    hi = ng * ROWS
                    keep = rowi[:hi] <= t
                    do_t = dos[tr:tr + 1, :]
                    e = jnp.exp(jnp.minimum(bs[tr:tr + 1, :] - bI[:hi], 0.0))
                    qe = qss[tr:tr + 1, :] * e
                    a = jnp.where(keep, jnp.sum(kkI[:hi] * qe, axis=-1, keepdims=True), 0.0)
                    da = jnp.where(keep, jnp.sum(icI[:hi] * do_t, axis=-1, keepdims=True), 0.0)
                    dp2, ddi = da * qe, a * do_t
                    for g in range(ng):
                        p2acc[g] = p2acc[g] + dp2[g * ROWS:(g + 1) * ROWS]
                        diacc[g] = diacc[g] + ddi[g * ROWS:(g + 1) * ROWS]
                    p1s[tr:tr + 1, :] = _colsum(da * kkI[:hi] * e)
                    yield
                p2[s0:s0 + SUB, :] += jnp.concatenate(p2acc, axis=0)
                dic[s0:s0 + SUB, :] += jnp.concatenate(diacc, axis=0)
            dqs = dqs + jnp.concatenate(p1_pieces, axis=0) + p1s[...]
            dkk = dkk2 + p2[...]
            db = qs * dqs - kk * dkk + jnp.where(last, dbl, 0.0)
            dgl = _tri_dot(ltri, db, TN)
            yield
            dfv = dgl / f - dkk
            df_ref[rows, ls] = (dfv * (1.0 - lbv) * sg * (1.0 - sg)).astype(BF16)
            dlb_ref[:, ls] += _colsum(dfv * (1.0 - sg))
            dq_ref[rows, ls] = (dqs * (sq * (1.0 + qb * (1.0 - sq)))).astype(BF16)
            di_ref[rows, ls] = dic[...].astype(BF16)

        def chunk(k, carry):
            _alternate([one_head(h, NC - 1 - k) for h in range(HG)])
            return carry

        lax.fori_loop(0, NC, chunk, 0)

    tile = pltpu.VMEM((HG, CHUNK, LANE), F32)
    return pl.pallas_call(
        body, grid=(NG,), name="hgrn_bwd",
        in_specs=[blk_in(0), blk_in(NG), blk_in(2 * NG), blk_in(3 * NG), col,
                  pl.BlockSpec((T, W), lambda g: (0, AW // W + g)),
                  pl.BlockSpec((HG, NC, LANE, LANE), lambda g: (g, 0, 0, 0)),
                  pl.BlockSpec((1, W), lambda g: (0, g)), pl.BlockSpec((1, LANE), lambda g: (0, 0))],
        out_specs=[col, col, col, col, pl.BlockSpec((1, W), lambda g: (0, g)),
                   pl.BlockSpec((HG, 1, LANE), lambda g: (g, 0, 0))],
        out_shape=[_sds((T, RW), BF16)] * 4 + [_sds((1, RW), F32), _sds((RH, 1, LANE), F32)],
        scratch_shapes=[pltpu.VMEM((HG, LANE, LANE), F32), tile, tile, tile, tile, tile, tile],
        compiler_params=_params(("parallel",), big=True),
    )(proj, proj, proj, proj, o_b, dmixin, st_all, lb, gn)


def _prep(c, lb_logits, rb_pad, max_rel):
    D, RW = c.shape[-1], lb_logits.shape[-1]
    H, rbp = rb_pad.shape

    def body(c_ref, l_ref, rb_ref, cact_ref, lb_ref, gv_ref):
        cv = c_ref[...]
        cact_ref[...] = cv * _sigmoid(cv)
        lb_ref[...] = _sigmoid(l_ref[0:1, :] - l_ref[1:2, :])
        gv_ref[...] = _dot(rb_ref[...], _bias_onehot(rbp, max_rel), NN, HIGHEST)

    return pl.pallas_call(
        body, name="prep", out_shape=[_sds((1, D), F32), _sds((1, RW), F32), _sds((H, TAB), F32)],
    )(c, lb_logits, rb_pad)


def _mod_part(c_all, w_ada_s, b_ada_s):
    B, D = c_all.shape
    ns = w_ada_s.shape[1]
    tn = _tile(ns, 768, LANE)

    def body(c_ref, w_ref, b_ref, o_ref):
        o_ref[...] = _dot(c_ref[...], w_ref[...], NN) + b_ref[...]

    return pl.pallas_call(
        body, grid=(ns // tn,), name="mod_part",
        in_specs=[pl.BlockSpec((B, D), lambda j: (0, 0)), pl.BlockSpec((D, tn), lambda j: (0, j)),
                  pl.BlockSpec((1, tn), lambda j: (0, j))],
        out_specs=pl.BlockSpec((B, tn), lambda j: (0, j)),
        out_shape=_sds((B, ns), F32), compiler_params=_params(("parallel",)),
    )(c_all, w_ada_s, b_ada_s)


def _adam(w, g, m, v):
    m = ADAM_B1 * m + (1.0 - ADAM_B1) * g
    v = ADAM_B2 * v + (1.0 - ADAM_B2) * (g * g)
    m_hat = m * (1.0 / (1.0 - ADAM_B1 ** ADAM_STEP))
    v_hat = v * (1.0 / (1.0 - ADAM_B2 ** ADAM_STEP))
    return -ADAM_LR * (m_hat / (jnp.sqrt(v_hat) + ADAM_EPS) + ADAM_WD * w), m, v


def _adam_ada(c_all, dmod_s, w, m, v):
    B, D = c_all.shape
    ns = w.shape[1]
    tr, tn = _tile(D, 512, LANE), _tile(ns, 768, LANE)

    def body(c_ref, d_ref, w_ref, m_ref, v_ref, g_out, dw_out, m_out, v_out):
        g = _dot(c_ref[...], d_ref[...], TN)
        g_out[...] = g
        dw_out[...], m_out[...], v_out[...] = _adam(w_ref[...], g, m_ref[...], v_ref[...])

    big = pl.BlockSpec((tr, tn), lambda i, j: (i, j))
    return pl.pallas_call(
        body, grid=(D // tr, ns // tn), name="adam_w_ada",
        in_specs=[pl.BlockSpec((B, tr), lambda i, j: (0, i)), pl.BlockSpec((B, tn), lambda i, j: (0, j)),
                  big, big, big],
        out_specs=[big] * 4, out_shape=[_sds((D, ns), F32)] * 4,
        compiler_params=_params(("parallel", "parallel")),
    )(c_all, dmod_s, w, m, v)


def _adam_shard(parts, w, m, v, name):
    R, C = w.shape
    tr = _tile(R, 256, 16)

    def body(p_ref, w_ref, m_ref, v_ref, g_out, dw_out, m_out, v_out):
        g = p_ref[0].astype(F32)
        for k in range(1, N_DEV // 2):
            g = g + p_ref[k].astype(F32)
        g_out[...] = g
        dw_out[...], m_out[...], v_out[...] = _adam(w_ref[...], g, m_ref[...], v_ref[...])

    big = pl.BlockSpec((tr, C), lambda i: (i, 0))
    return pl.pallas_call(
        body, grid=(R // tr,), name=name,
        in_specs=[pl.BlockSpec((N_DEV // 2, tr, C), lambda i: (0, i, 0)), big, big, big],
        out_specs=[big] * 4, out_shape=[_sds((R, C), F32)] * 4,
        compiler_params=_params(("parallel",), big=True),
    )(parts, w, m, v)


def _pair_sum(g8, land, core, name):
    _, NCHIP, R, C = g8.shape
    tr = _tile(R, 1024, 16)

    def body(core_ref, g_ref, l_ref, o_ref):
        o_ref[...] = g_ref[...] + l_ref[...]

    return pl.pallas_call(
        body, name=name,
        grid_spec=pltpu.PrefetchScalarGridSpec(
            num_scalar_prefetch=1, grid=(NCHIP, R // tr),
            in_specs=[pl.BlockSpec((None, None, tr, C), lambda k, i, core_ref: (core_ref[0], k, i, 0)),
                      pl.BlockSpec((None, tr, C), lambda k, i, core_ref: (k, i, 0))],
            out_specs=pl.BlockSpec((None, tr, C), lambda k, i, core_ref: (k, i, 0))),
        out_shape=_sds((NCHIP, R, C), BF16), compiler_params=_params(("parallel", "parallel")),
    )(core, g8, land)


SMALL = ("b_ada", "rel_bias", "attn_norm_g", "lb_logits", "gnorm_g", "ln1_g", "ln1_b", "ln2_g", "ln2_b")


def _small_update(parts, loss_parts, lbv, ws, ms, vs, max_rel):
    n = len(SMALL)

    def body(*refs):
        part_refs = dict(zip(SMALL, refs[:n]))
        loss_in, lb_ref = refs[n], refs[n + 1]
        w_refs, m_refs, v_refs = refs[n + 2:2 * n + 2], refs[2 * n + 2:3 * n + 2], refs[3 * n + 2:4 * n + 2]
        outs = refs[4 * n + 2:]

        def total(ref):
            tot = ref[0]
            for k in range(1, N_DEV):
                tot = tot + ref[k]
            return tot

        outs[0][...] = jnp.sum(total(loss_in), axis=-1, keepdims=True)
        for idx, name in enumerate(SMALL):
            g = total(part_refs[name])
            if name == "rel_bias":
                g = _dot(g, _bias_onehot(w_refs[idx].shape[1], max_rel), NT, HIGHEST)
            elif name == "lb_logits":
                lb = lb_ref[...]
                sign = (1 - 2 * lax.broadcasted_iota(jnp.int32, (2, 1), 0)).astype(F32)
                g = sign * (g * lb * (1.0 - lb))
            elif name == "gnorm_g":
                g = _colsum(g)
            dw, mm, vv = _adam(w_refs[idx][...], g, m_refs[idx][...], v_refs[idx][...])
            outs[1 + 4 * idx][...] = g
            outs[2 + 4 * idx][...] = dw
            outs[3 + 4 * idx][...] = mm
            outs[4 + 4 * idx][...] = vv

    out_shape = [_sds((1, 1), F32)]
    for w in ws:
        out_shape += [_sds(w.shape, F32)] * 4
    return pl.pallas_call(body, name="small_update", out_shape=out_shape, compiler_params=_params(big=True))(
        *[parts[k] for k in SMALL], loss_parts, lbv, *ws, *ms, *vs)


def _place():
    x, y, c = lax.axis_index("x"), lax.axis_index("y"), lax.axis_index("c")
    return x, y, c, [(1 - x, y), (x, 1 - y), (1 - x, 1 - y)]


def _all_gather(shard, name):
    HBM = pl.BlockSpec(memory_space=pl.ANY)

    def body(x_ref, out_ref, send_sems, recv_sems, local_sem):
        x, y, c, chips = _place()
        me, sibling = (x, y, c), (x, y, 1 - c)

        def slot(px, py, pc):
            return out_ref.at[4 * px + 2 * py + pc]

        def copy(k, block, to, src=None):
            return pltpu.make_async_remote_copy(
                src_ref=slot(*block) if src is None else src, dst_ref=slot(*block),
                send_sem=send_sems.at[k], recv_sem=recv_sems.at[k], device_id=to, device_id_type=MESH)

        mine = pltpu.make_async_copy(x_ref, slot(*me), local_sem)
        mine.start()
        first = [copy(0, me, sibling, src=x_ref)]
        first += [copy(1 + j, me, (*chip, c), src=x_ref) for j, chip in enumerate(chips)]
        for cp in first:
            cp.start()
        passed = [copy(4 + j, (*chip, c), sibling) for j, chip in enumerate(chips)]
        for j, chip in enumerate(chips):
            copy(1 + j, (*chip, c), me).wait_recv()
            passed[j].start()
        copy(0, sibling, me).wait_recv()
        for j, chip in enumerate(chips):
            copy(4 + j, (*chip, 1 - c), me).wait_recv()
        for cp in first + passed:
            cp.wait_send()
        mine.wait()

    return pl.pallas_call(
        body, name=name, out_shape=_sds((N_DEV,) + shard.shape, shard.dtype),
        in_specs=[HBM], out_specs=HBM,
        scratch_shapes=[pltpu.SemaphoreType.DMA((7,)), pltpu.SemaphoreType.DMA((7,)), pltpu.SemaphoreType.DMA(())],
    )(shard)


SEM_SPEC = pl.BlockSpec(memory_space=pltpu.SEMAPHORE)
HBM_SPEC = pl.BlockSpec(memory_space=pltpu.HBM)
EFFECT = pltpu.SideEffectType.DATAFLOW_SIDE_EFFECTING


def _remote(src, dst, send_sems, recv_sems, k, dev):
    return pltpu.make_async_remote_copy(src_ref=src, dst_ref=dst, send_sem=send_sems.at[k], recv_sem=recv_sems.at[k],
                                        device_id=dev, device_id_type=MESH)


def _copy_start(name, bufs, plan, n, after, only=None):
    nb = len(bufs)

    def body(*refs):
        send_sems, recv_sems = refs[nb + 1], refs[nb + 2]
        for k, (src, dst, dev) in enumerate(plan(*refs[:nb])):
            if only is not None and k not in only:
                continue
            _remote(src, dst, send_sems, recv_sems, k, dev).start()
        refs[-1][...] = jnp.zeros_like(refs[-1])

    out = pl.pallas_call(
        body, name=name,
        out_shape=(pltpu.SemaphoreType.DMA((n,)), pltpu.SemaphoreType.DMA((n,)),
                   *[pltpu.HBM(b.shape, b.dtype) for b in bufs], _sds((8, LANE), F32)),
        in_specs=[HBM_SPEC] * nb + [ORDER_ONLY],
        out_specs=(SEM_SPEC, SEM_SPEC, *[HBM_SPEC] * nb, pl.BlockSpec(memory_space=pltpu.VMEM)),
        input_output_aliases={i: 2 + i for i in range(nb)},
        compiler_params=pltpu.CompilerParams(has_side_effects=EFFECT),
    )(*[pltpu.with_memory_space_constraint(b, pltpu.HBM) for b in bufs], after)
    return (out[0], out[1]), list(out[2:2 + nb]), out[-1]


def _copy_wait(name, sems, bufs, plan, after, only=None):
    nb = len(bufs)

    def body(*refs):
        send_sems, recv_sems = refs[nb], refs[nb + 1]
        for k, (src, dst, dev) in enumerate(plan(*refs[:nb])):
            if only is not None and k not in only:
                continue
            cp = _remote(src, dst, send_sems, recv_sems, k, dev)
            cp.wait_send()
            cp.wait_recv()

    out = pl.pallas_call(
        body, name=name, out_shape=tuple(pltpu.HBM(b.shape, b.dtype) for b in bufs),
        in_specs=[HBM_SPEC] * nb + [SEM_SPEC, SEM_SPEC, pl.BlockSpec(memory_space=pl.ANY)],
        out_specs=tuple([HBM_SPEC] * nb), input_output_aliases={i: i for i in range(nb)},
        compiler_params=pltpu.CompilerParams(has_side_effects=EFFECT),
    )(*bufs, sems[0], sems[1], after)
    return list(out)


def _ag_plan_chips(shard_ref, out_ref):
    x, y, c, chips = _place()
    mine = out_ref.at[4 * x + 2 * y + c]
    return [(shard_ref, mine, (x, y, 1 - c))] + [(shard_ref, mine, (*chip, c)) for chip in chips]


def _ag_plan_pass(out_ref):
    x, y, c, chips = _place()
    slots = [out_ref.at[4 * chip[0] + 2 * chip[1] + c] for chip in chips]
    return [(s, s, (x, y, 1 - c)) for s in slots]


def _rs_plan_pair(g_ref, land_ref):
    x, y, c, _ = _place()
    return [(g_ref.at[1 - c], land_ref, (x, y, 1 - c))]


def _rs_plan_chips(p_ref, land_ref):
    x, y, c, chips = _place()
    return [(p_ref.at[2 * chip[0] + chip[1]], land_ref.at[2 * x + y], (*chip, c)) for chip in chips]


class _Gather:
    def __init__(self, shard, me, tag, after):
        self.tag = tag
        out = lax.dynamic_update_slice(lax.empty((N_DEV,) + shard.shape, shard.dtype), shard[None],
                                       (me,) + (0,) * shard.ndim)
        self.sems, (self.shard, self.out), self.token = _copy_start(
            "ag_start_" + tag, [shard, out], _ag_plan_chips, 4, after)
        self.groups = []

    def arrived(self, after, copies):
        name = "ag_wait_%s_%s" % (self.tag, "".join(map(str, copies)))
        self.shard, self.out = _copy_wait(name, self.sems, [self.shard, self.out], _ag_plan_chips, after, copies)
        return self.out

    def pass_on(self, after, blocks):
        name = "ag_pass_%s_%s" % (self.tag, "".join(map(str, blocks)))
        sems, (self.out,), _ = _copy_start(name, [self.out], _ag_plan_pass, 3, after, blocks)
        self.groups.append((sems, blocks))
        return self.out

    def passed(self, after, group):
        sems, blocks = self.groups[group]
        name = "ag_pass_wait_%s_%s" % (self.tag, "".join(map(str, blocks)))
        self.out = _copy_wait(name, sems, [self.out], _ag_plan_pass, after, blocks)[0]
        return self.out

    def arrived_from_chips(self, after):
        self.arrived(after, (0, 1, 2, 3))
        return self.pass_on(after, (0, 1, 2))

    def passed_on(self, after):
        return self.passed(after, 0)


def _ag_plan_direct(src_ref, out_ref):
    x, y, c, chips = _place()
    mine = out_ref.at[4 * x + 2 * y + c]
    peers = [(x, y, 1 - c)] + [(*chip, pc) for chip in chips for pc in (c, 1 - c)]
    return [(src_ref, mine, peer) for peer in peers]


class _SmallGather:
    def __init__(self, block, me, tag):
        self.tag = tag
        out = lax.dynamic_update_slice(lax.empty((N_DEV,) + block.shape, block.dtype), block[None],
                                       (me,) + (0,) * block.ndim)
        self.sems, self.bufs, self.token = _copy_start(
            "ag_direct_start_" + tag, [block, out], _ag_plan_direct, N_DEV - 1, jnp.zeros((1,), F32))

    def done(self, after):
        return _copy_wait("ag_direct_wait_" + self.tag, self.sems, self.bufs, _ag_plan_direct, after)[1]


class _ReduceScatter:
    def __init__(self, g8, tag):
        self.tag = tag
        land = lax.empty(g8.shape[1:], g8.dtype)
        self.sems, self.bufs, self.token = _copy_start(
            "rs_pair_start_" + tag, [g8, land], _rs_plan_pair, 1, jnp.zeros((1,), F32))

    def pair_done(self, core, chip, after):
        g8, land = _copy_wait("rs_pair_wait_" + self.tag, self.sems, self.bufs, _rs_plan_pair, after)
        p4 = _pair_sum(g8, land, core, "rs_pair_sum_" + self.tag)
        own = lax.dynamic_slice_in_dim(p4, chip, 1, axis=0)
        land2 = lax.dynamic_update_slice(lax.empty(p4.shape, p4.dtype), own, (chip, 0, 0))
        self.sems, self.bufs, self.token = _copy_start(
            "rs_chips_start_" + self.tag, [p4, land2], _rs_plan_chips, 3, jnp.zeros((1,), F32))

    def sums(self, after):
        return _copy_wait("rs_chips_wait_" + self.tag, self.sems, self.bufs, _rs_plan_chips, after)[1]


ORDER = ("w_ada", "b_ada", "w_in", "rel_bias", "attn_norm_g", "lb_logits", "gnorm_g", "w_o", "ln1_g", "ln1_b",
         "w_ffn_in", "w_ffn_out", "ln2_g", "ln2_b")


def kernel(x, c, w_ada, b_ada, w_in, rel_bias, attn_norm_g, lb_logits, gnorm_g, w_o, ln1_g, ln1_b, w_ffn_in, w_ffn_out, ln2_g, ln2_b, loss_target, m_w_ada, m_b_ada, m_w_in, m_rel_bias, m_attn_norm_g, m_lb_logits, m_gnorm_g, m_w_o, m_ln1_g, m_ln1_b, m_w_ffn_in, m_w_ffn_out, m_ln2_g, m_ln2_b, v_w_ada, v_b_ada, v_w_in, v_rel_bias, v_attn_norm_g, v_lb_logits, v_gnorm_g, v_w_o, v_ln1_g, v_ln1_b, v_w_ffn_in, v_w_ffn_out, v_ln2_g, v_ln2_b):
    W = dict(w_ada=w_ada, b_ada=b_ada, w_in=w_in, rel_bias=rel_bias, attn_norm_g=attn_norm_g, lb_logits=lb_logits,
             gnorm_g=gnorm_g, w_o=w_o, ln1_g=ln1_g, ln1_b=ln1_b, w_ffn_in=w_ffn_in, w_ffn_out=w_ffn_out,
             ln2_g=ln2_g, ln2_b=ln2_b)
    M = dict(w_ada=m_w_ada, b_ada=m_b_ada, w_in=m_w_in, rel_bias=m_rel_bias, attn_norm_g=m_attn_norm_g,
             lb_logits=m_lb_logits, gnorm_g=m_gnorm_g, w_o=m_w_o, ln1_g=m_ln1_g, ln1_b=m_ln1_b,
             w_ffn_in=m_w_ffn_in, w_ffn_out=m_w_ffn_out, ln2_g=m_ln2_g, ln2_b=m_ln2_b)
    V = dict(w_ada=v_w_ada, b_ada=v_b_ada, w_in=v_w_in, rel_bias=v_rel_bias, attn_norm_g=v_attn_norm_g,
             lb_logits=v_lb_logits, gnorm_g=v_gnorm_g, w_o=v_w_o, ln1_g=v_ln1_g, ln1_b=v_ln1_b,
             w_ffn_in=v_w_ffn_in, w_ffn_out=v_w_ffn_out, ln2_g=v_ln2_g, ln2_b=v_ln2_b)

    x2, tgt = x[0], loss_target[0]
    T, D = x2.shape
    AW, RW = attn_norm_g.shape[-1], lb_logits.shape[-1]
    MIX = AW + RW
    H, RH = AW // ATTN_HEAD_DIM, RW // LANE
    RB = rel_bias.shape[-1]
    max_rel = (RB - 1) // 2
    rbp = -(-RB // LANE) * LANE
    F = w_ffn_out.shape[1] * N_DEV
    half = N_DEV // 2
    xi, yi, ci = lax.axis_index("x"), lax.axis_index("y"), lax.axis_index("c")
    me = 4 * xi + 2 * yi + ci
    core = jnp.reshape(ci, (1,)).astype(jnp.int32)
    pad_rb = lambda a: jnp.pad(a[0], ((0, 0), (0, rbp - RB)))

    chip = 2 * xi + yi

    c_act, lbv, gv = _prep(c, lb_logits, pad_rb(rel_bias), max_rel)
    c_all = _all_gather(c_act, "ag_c").reshape(N_DEV, D)
    ns_ada = w_ada.shape[-1]
    mod_part = _mod_part(c_all, w_ada[0], lax.dynamic_slice_in_dim(b_ada, me * ns_ada, ns_ada, axis=1))
    mod_all = _all_gather(mod_part, "ag_mod")
    mod6 = lax.dynamic_index_in_dim(mod_all, me, axis=1, keepdims=False).reshape(6, D)

    ag_in = _Gather(w_in[0].astype(BF16), me, "w_in", mod_all)
    ag_o = _Gather(w_o[0].astype(BF16), me, "w_o", ag_in.token)
    ag_f1 = _Gather(w_ffn_in[0].astype(BF16), me, "w_ffn_in", ag_o.token)
    ag_f2 = _Gather(w_ffn_out[0].astype(BF16), me, "w_ffn_out", ag_f1.token)

    h1 = _ln_mod(x2, mod6 + ag_f2.token[0, 0])
    ids = lambda pairs: jnp.stack([4 * px + 2 * py + pc for px, py, pc in pairs]).astype(jnp.int32)
    others = [(1 - xi, yi), (xi, 1 - yi), (1 - xi, 1 - yi)]
    proj = lax.empty((T, w_in.shape[-1] * N_DEV), F32)
    proj = _mm_gathered(h1, ag_in.arrived(h1, (0,)), ids([(xi, yi, ci), (xi, yi, 1 - ci)]), proj, "in_proj_a")
    ag_in.arrived(proj, (1, 2, 3))
    proj = _mm_gathered(h1, ag_in.pass_on(proj, (0, 1, 2)), ids([(*ch, ci) for ch in others]), proj, "in_proj_b")
    wg_in = ag_in.passed(proj, 0)
    proj = _mm_gathered(h1, wg_in, ids([(*ch, 1 - ci) for ch in others]), proj, "in_proj_c")
    ag_o.arrived_from_chips(proj)
    mix_a = _attn_fwd(proj, gv, attn_norm_g, AW)
    wg_o = ag_o.passed_on(mix_a).reshape(MIX, D)
    mix_b, o_b, st_all = _hgrn_fwd(proj, lbv, gnorm_g, AW, RW)
    mixin = jnp.concatenate([mix_a, mix_b], axis=1)
    mix = _mm_nn(mixin, wg_o, "out_proj")
    ag_f1.arrived_from_chips(mix)
    x1, h2 = _mid_fwd(x2, mix, mod6, ln1_g, ln1_b)
    wg_f1 = ag_f1.passed_on(h2)
    gu, act = _mm_swiglu(h2, wg_f1)
    ag_f2.arrived_from_chips(act)
    wg_f2 = ag_f2.passed_on(act).reshape(F, D)
    ff = _mm_nn(act, wg_f2, "ffn_out")
    dff, dx1a, vec_a = _final(x1, ff, mod6, ln2_g, ln2_b, tgt)

    du = _mm_swiglu_bwd(dff, wg_f2, gu)
    rs_f2 = _ReduceScatter(_mm_tn_rows(dff, act, dff, F // N_DEV, "grad_w_ffn_out"), "w_ffn_out")
    tm = _tile(T, 512, 16)
    du_ij = lambda tm_, w, first: pl.BlockSpec((None, tm_, w), lambda i, p: (p // (half // 2), i + first, p % (half // 2)))
    du_j = lambda rows, ns: pl.BlockSpec((None, rows, ns), lambda j: (j // half, 0, j % half))
    dh2 = _mm_gathered_nt(rs_f2.token, du, du_ij, wg_f1, T, tm, "ffn_in_bwd")
    rs_f2.pair_done(core, chip, dh2)
    gw_f1 = _mm_tn_gathered(rs_f2.token, h2, du, du_j, wg_f1.shape[-1], "grad_w_ffn_in")
    rs_f1 = _ReduceScatter(gw_f1.reshape(2, half, D, -1), "w_ffn_in")
    dmix, dxa, vec_b = _mid_bwd(x2, mix, x1, dx1a, dh2, mod6 + rs_f1.token[0, 0], ln1_g)
    dmixin = _mm_nt(dmix, wg_o, "out_proj_bwd")
    rs_f1.pair_done(core, chip, dmixin)
    rs_o = _ReduceScatter(_mm_tn_rows(rs_f1.token, mixin, dmix, MIX // N_DEV, "grad_w_o"), "w_o")
    dq, dk, dv, dgv, dga = _attn_bwd(proj, dmixin, gv + rs_o.token[0, 0], attn_norm_g, AW)
    rs_o.pair_done(core, chip, dq)
    dqb, dfl, dib, dgb, dlb, dgn = _hgrn_bwd(proj, dmixin, o_b, st_all, lbv + rs_o.token[0, 0], gnorm_g, AW, RW)
    dproj = jnp.concatenate([dq, dk, dv, dqb, dfl, dib, dgb], axis=1)
    p_ij = lambda tm_, w, first: pl.BlockSpec((tm_, w), lambda i, p: (i + first, p))
    p_j = lambda rows, ns: pl.BlockSpec((rows, ns), lambda j: (0, j))
    gw_in = _mm_tn_gathered(rs_o.token, h1, dproj, p_j, wg_in.shape[-1], "grad_w_in")
    rs_in = _ReduceScatter(gw_in.reshape(2, half, D, -1), "w_in")
    n_tiles = T // tm
    dh1 = _mm_gathered_nt(rs_in.token, dproj, p_ij, wg_in, T, tm, "in_proj_bwd_a", 0, n_tiles // 2)
    rs_in.pair_done(core, chip, dh1)
    dh1 = _mm_gathered_nt(rs_in.token, dproj, p_ij, wg_in, T, tm, "in_proj_bwd_b", n_tiles // 2,
                          n_tiles - n_tiles // 2, dh1)
    grad_x, vec_c = _first_bwd(x2, dh1, dxa, mod6)

    dmod = jnp.concatenate([vec_c[1:2], vec_c[0:1], vec_b[4:5], vec_b[1:2], vec_b[0:1], vec_a[2:3]], axis=0)
    pieces = dict(b_ada=dmod, rel_bias=dgv, attn_norm_g=dga, lb_logits=dlb, gnorm_g=dgn, ln1_g=vec_b[2:3],
                  ln1_b=vec_b[3:4], ln2_g=vec_a[0:1], ln2_b=vec_a[1:2], loss=vec_a[3:4])
    widths = dict(b_ada=(1, 6 * D), rel_bias=(H, TAB), attn_norm_g=(1, AW), lb_logits=(1, RW), gnorm_g=(RH, LANE),
                  ln1_g=(1, D), ln1_b=(1, D), ln2_g=(1, D), ln2_b=(1, D), loss=(1, D))
    packed = jnp.concatenate([pieces[k].reshape(-1, LANE) for k in widths], axis=0)
    small_ag = _SmallGather(packed, me, "small")
    after, res_big = small_ag.token, {}
    for k, rs in (("w_ffn_out", rs_f2), ("w_ffn_in", rs_f1), ("w_o", rs_o), ("w_in", rs_in)):
        four = _adam_shard(rs.sums(after), W[k][0], M[k][0], V[k][0], "adam_" + k)
        res_big[k] = [a[None] for a in four]
        after = four[0]
    gathered = small_ag.done(after)
    parts, r0 = {}, 0
    for k, (rows, width) in widths.items():
        nr = rows * width // LANE
        parts[k] = gathered[:, r0:r0 + nr, :].reshape(N_DEV, rows, width)
        r0 += nr
    prep_small = lambda d, k: pad_rb(d[k]) if k == "rel_bias" else d[k]
    small = _small_update(parts, parts["loss"], lbv, [prep_small(W, k) for k in SMALL],
                          [prep_small(M, k) for k in SMALL], [prep_small(V, k) for k in SMALL], max_rel)
    loss = small[0].reshape(())
    res = {}
    for idx, k in enumerate(SMALL):
        four = small[1 + 4 * idx:5 + 4 * idx]
        if k == "rel_bias":
            four = [a[:, :RB][None] for a in four]
        res[k] = list(four)

    res.update(res_big)
    dmod_s = lax.dynamic_slice_in_dim(parts["b_ada"].reshape(N_DEV, 6 * D), me * ns_ada, ns_ada, axis=1)
    res["w_ada"] = [a[None] for a in _adam_ada(c_all, dmod_s, w_ada[0], m_w_ada[0], v_w_ada[0])]

    out = [loss, grad_x[None]]
    for field in range(4):
        out += [res[k][field] for k in ORDER]
    return tuple(out)
```

```python
import jax
import jax.numpy as jnp
from jax import lax
from jax.experimental import pallas as pl
from jax.experimental.pallas import tpu as pltpu

F32 = jnp.float32
BF16 = jnp.bfloat16
MESH = pl.DeviceIdType.MESH
HIGHEST = lax.Precision.HIGHEST

N_DEV = 8
CHUNK = 64
N_PAST = 8
QBLK = 4 * CHUNK
KPAD = N_PAST * CHUNK
WIN = KPAD + QBLK
TAB = 1024
ATTN_HEAD_DIM = 64
ATTN_HEADS_PER_STEP = 4
SUB = 32
ROWS = 8
LANE = 128
EPS = 1e-5
ALPHA = 2.0 ** 0.25
ADAM_LR, ADAM_B1, ADAM_B2, ADAM_EPS, ADAM_WD, ADAM_STEP = 0.001, 0.9, 0.999, 1e-08, 0.01, 10
NEG = -1e30
VMEM_LIMIT = 56 * 1024 * 1024


def _sds(shape, dtype):
    return jax.ShapeDtypeStruct(tuple(shape), dtype)


def _tile(n, pref, mult):
    best = None
    for t in range(mult, min(n, pref) + 1, mult):
        if n % t == 0:
            best = t
    return n if best is None else best


def _params(sem=None, big=False):
    kw = {}
    if sem is not None:
        kw["dimension_semantics"] = sem
    if big:
        kw["vmem_limit_bytes"] = VMEM_LIMIT
    return pltpu.CompilerParams(**kw)


def _sigmoid(v):
    return 1.0 / (1.0 + jnp.exp(-v))


def _dot(a, b, dims, precision=None):
    return lax.dot_general(a, b, (dims, ((), ())), preferred_element_type=F32, precision=precision)


NN = ((1,), (0,))
NT = ((1,), (1,))
TN = ((0,), (0,))


def _ln(v):
    mu = jnp.mean(v, axis=-1, keepdims=True)
    d = v - mu
    rstd = lax.rsqrt(jnp.mean(d * d, axis=-1, keepdims=True) + EPS)
    return d * rstd, rstd


def _ln_bwd(dxh, xh, rstd):
    return rstd * (dxh - jnp.mean(dxh, axis=-1, keepdims=True) - xh * jnp.mean(dxh * xh, axis=-1, keepdims=True))


def _colsum(v):
    return jnp.sum(v, axis=0, keepdims=True)


def _ln_mod(x2, mod6):
    T, D = x2.shape
    tm = _tile(T, 256, 8)

    def body(x_ref, mod_ref, o_ref):
        xh, _ = _ln(x_ref[...])
        o_ref[...] = (xh * (1.0 + mod_ref[1:2, :]) + mod_ref[0:1, :]).astype(BF16)

    return pl.pallas_call(
        body, grid=(T // tm,), name="ln_mod",
        in_specs=[pl.BlockSpec((tm, D), lambda i: (i, 0)), pl.BlockSpec((6, D), lambda i: (0, 0))],
        out_specs=pl.BlockSpec((tm, D), lambda i: (i, 0)),
        out_shape=_sds((T, D), BF16), compiler_params=_params(("parallel",)),
    )(x2, mod6)


def _mid_fwd(x2, mix, mod6, ln1_g, ln1_b):
    T, D = x2.shape
    tm = _tile(T, 256, 8)

    def body(x_ref, mix_ref, mod_ref, g_ref, b_ref, x1_ref, h2_ref):
        zh, _ = _ln(ALPHA * x_ref[...] + mod_ref[2:3, :] * mix_ref[...])
        x1 = zh * g_ref[...] + b_ref[...]
        x1_ref[...] = x1
        xh, _ = _ln(x1)
        h2_ref[...] = (xh * (1.0 + mod_ref[4:5, :]) + mod_ref[3:4, :]).astype(BF16)

    row = pl.BlockSpec((tm, D), lambda i: (i, 0))
    vec = pl.BlockSpec((1, D), lambda i: (0, 0))
    return pl.pallas_call(
        body, grid=(T // tm,), name="mid_fwd",
        in_specs=[row, row, pl.BlockSpec((6, D), lambda i: (0, 0)), vec, vec],
        out_specs=[row, row],
        out_shape=[_sds((T, D), F32), _sds((T, D), BF16)], compiler_params=_params(("parallel",)),
    )(x2, mix, mod6, ln1_g, ln1_b)


def _final(x1, ff, mod6, ln2_g, ln2_b, tgt):
    T, D = x1.shape
    tm = _tile(T, 256, 8)

    def body(x1_ref, ff_ref, mod_ref, g_ref, b_ref, t_ref, dff_ref, dx1_ref, vec_ref):
        @pl.when(pl.program_id(0) == 0)
        def _():
            vec_ref[...] = jnp.zeros_like(vec_ref)

        ff_v = ff_ref[...]
        gate2 = mod_ref[5:6, :]
        zh, rstd = _ln(ALPHA * x1_ref[...] + gate2 * ff_v)
        err = zh * g_ref[...] + b_ref[...] - t_ref[...]
        dy = err * (1.0 / D)
        dz = _ln_bwd(dy * g_ref[...], zh, rstd)
        dff_ref[...] = (gate2 * dz).astype(BF16)
        dx1_ref[...] = ALPHA * dz
        vec_ref[0:1, :] += _colsum(dy * zh)
        vec_ref[1:2, :] += _colsum(dy)
        vec_ref[2:3, :] += _colsum(dz * ff_v)
        vec_ref[3:4, :] += _colsum(err * err) * (0.5 / D)

    row = pl.BlockSpec((tm, D), lambda i: (i, 0))
    vec = pl.BlockSpec((1, D), lambda i: (0, 0))
    return pl.pallas_call(
        body, grid=(T // tm,), name="final_fwd_bwd",
        in_specs=[row, row, pl.BlockSpec((6, D), lambda i: (0, 0)), vec, vec, row],
        out_specs=[row, row, pl.BlockSpec((8, D), lambda i: (0, 0))],
        out_shape=[_sds((T, D), BF16), _sds((T, D), F32), _sds((8, D), F32)],
        compiler_params=_params(("arbitrary",)),
    )(x1, ff, mod6, ln2_g, ln2_b, tgt)


def _mid_bwd(x2, mix, x1, dx1a, dh2, mod6, ln1_g):
    T, D = x2.shape
    tm = _tile(T, 256, 8)

    def body(x_ref, mix_ref, x1_ref, dx1a_ref, dh2_ref, mod_ref, g_ref, dmix_ref, dxa_ref, vec_ref):
        @pl.when(pl.program_id(0) == 0)
        def _():
            vec_ref[...] = jnp.zeros_like(vec_ref)

        dh2 = dh2_ref[...]
        xh, rstd = _ln(x1_ref[...])
        dx1 = dx1a_ref[...] + _ln_bwd(dh2 * (1.0 + mod_ref[4:5, :]), xh, rstd)
        mix_v = mix_ref[...]
        gate1 = mod_ref[2:3, :]
        zh, rstdz = _ln(ALPHA * x_ref[...] + gate1 * mix_v)
        dz = _ln_bwd(dx1 * g_ref[...], zh, rstdz)
        dmix_ref[...] = (gate1 * dz).astype(BF16)
        dxa_ref[...] = ALPHA * dz
        vec_ref[0:1, :] += _colsum(dh2 * xh)
        vec_ref[1:2, :] += _colsum(dh2)
        vec_ref[2:3, :] += _colsum(dx1 * zh)
        vec_ref[3:4, :] += _colsum(dx1)
        vec_ref[4:5, :] += _colsum(dz * mix_v)

    row = pl.BlockSpec((tm, D), lambda i: (i, 0))
    vec = pl.BlockSpec((1, D), lambda i: (0, 0))
    return pl.pallas_call(
        body, grid=(T // tm,), name="mid_bwd",
        in_specs=[row, row, row, row, row, pl.BlockSpec((6, D), lambda i: (0, 0)), vec],
        out_specs=[row, row, pl.BlockSpec((8, D), lambda i: (0, 0))],
        out_shape=[_sds((T, D), BF16), _sds((T, D), F32), _sds((8, D), F32)],
        compiler_params=_params(("arbitrary",)),
    )(x2, mix, x1, dx1a, dh2, mod6, ln1_g)


def _first_bwd(x2, dh1, dxa, mod6):
    T, D = x2.shape
    tm = _tile(T, 256, 8)

    def body(x_ref, dh1_ref, dxa_ref, mod_ref, gx_ref, vec_ref):
        @pl.when(pl.program_id(0) == 0)
        def _():
            vec_ref[...] = jnp.zeros_like(vec_ref)

        dh1 = dh1_ref[...]
        xh, rstd = _ln(x_ref[...])
        gx_ref[...] = dxa_ref[...] + _ln_bwd(dh1 * (1.0 + mod_ref[1:2, :]), xh, rstd)
        vec_ref[0:1, :] += _colsum(dh1 * xh)
        vec_ref[1:2, :] += _colsum(dh1)

    row = pl.BlockSpec((tm, D), lambda i: (i, 0))
    return pl.pallas_call(
        body, grid=(T // tm,), name="first_bwd",
        in_specs=[row, row, row, pl.BlockSpec((6, D), lambda i: (0, 0))],
        out_specs=[row, pl.BlockSpec((8, D), lambda i: (0, 0))],
        out_shape=[_sds((T, D), F32), _sds((8, D), F32)],
        compiler_params=_params(("arbitrary",)),
    )(x2, dh1, dxa, mod6)


def _slot(j):
    return (j % 2) * 4 + j // 2


def _mm_gathered(a, wg, shards, out, name):
    M, K = a.shape
    _, _, ns = wg.shape
    tm = _tile(M, 512, 16)

    def body(shards_ref, a_ref, w_ref, prev_ref, o_ref):
        o_ref[...] = _dot(a_ref[...], w_ref[...], NN)

    return pl.pallas_call(
        body, name=name,
        grid_spec=pltpu.PrefetchScalarGridSpec(
            num_scalar_prefetch=1, grid=(shards.shape[0], M // tm),
            in_specs=[pl.BlockSpec((tm, K), lambda j, i, s: (i, 0)),
                      pl.BlockSpec((None, K, ns), lambda j, i, s: (s[j], 0, 0)), ORDER_ONLY],
            out_specs=pl.BlockSpec((tm, ns), lambda j, i, s: (i, s[j]))),
        out_shape=_sds((M, N_DEV * ns), F32), input_output_aliases={3: 0},
        compiler_params=_params(("parallel", "parallel"), big=True),
    )(shards, a, wg, out)


def _mm_nn(a, b, name):
    M, K = a.shape
    _, N = b.shape
    tm, tn = _tile(M, 512, 16), _tile(N, 1024, LANE)

    def body(a_ref, b_ref, o_ref):
        o_ref[...] = _dot(a_ref[...], b_ref[...], NN)

    return pl.pallas_call(
        body, grid=(N // tn, M // tm), name=name,
        in_specs=[pl.BlockSpec((tm, K), lambda j, i: (i, 0)), pl.BlockSpec((K, tn), lambda j, i: (0, j))],
        out_specs=pl.BlockSpec((tm, tn), lambda j, i: (i, j)),
        out_shape=_sds((M, N), F32), compiler_params=_params(("parallel", "parallel"), big=True),
    )(a, b)


def _mm_nt(a, b, name):
    M, K = a.shape
    N, _ = b.shape
    tm, tn = _tile(M, 512, 16), _tile(N, 1024, LANE)

    def body(a_ref, b_ref, o_ref):
        o_ref[...] = _dot(a_ref[...], b_ref[...], NT)

    return pl.pallas_call(
        body, grid=(M // tm, N // tn), name=name,
        in_specs=[pl.BlockSpec((tm, K), lambda i, j: (i, 0)), pl.BlockSpec((tn, K), lambda i, j: (j, 0))],
        out_specs=pl.BlockSpec((tm, tn), lambda i, j: (i, j)),
        out_shape=_sds((M, N), F32), compiler_params=_params(("parallel", "parallel"), big=True),
    )(a, b)


def _mm_swiglu(h2, wg):
    M, K = h2.shape
    _, _, ns = wg.shape
    half = N_DEV // 2
    tm = _tile(M, 256, 16)

    def body(a_ref, wgate_ref, wup_ref, gu_ref, act_ref):
        a = a_ref[...]
        g = _dot(a, wgate_ref[...], NN)
        u = _dot(a, wup_ref[...], NN)
        sg = _sigmoid(g)
        silu = g * sg
        gu_ref[0] = u * (sg * (1.0 + g * (1.0 - sg)))
        gu_ref[1] = silu
        act_ref[...] = (silu * u).astype(BF16)

    return pl.pallas_call(
        body, grid=(half, M // tm), name="ffn_in_swiglu",
        in_specs=[pl.BlockSpec((tm, K), lambda j, i: (i, 0)),
                  pl.BlockSpec((None, K, ns), lambda j, i: (j, 0, 0)),
                  pl.BlockSpec((None, K, ns), lambda j, i: (j + half, 0, 0))],
        out_specs=[pl.BlockSpec((2, tm, ns), lambda j, i: (0, i, j)), pl.BlockSpec((tm, ns), lambda j, i: (i, j))],
        out_shape=[_sds((2, M, half * ns), F32), _sds((M, half * ns), BF16)],
        compiler_params=_params(("parallel", "parallel"), big=True),
    )(h2, wg, wg)


def _mm_swiglu_bwd(dff, w2, gu):
    M, K = dff.shape
    F = w2.shape[0]
    tm, tn = _tile(M, 512, 16), _tile(F, 1408, LANE)

    def body(a_ref, b_ref, gu_ref, du_ref):
        da = _dot(a_ref[...], b_ref[...], NT)
        du_ref[0] = (da * gu_ref[0]).astype(BF16)
        du_ref[1] = (da * gu_ref[1]).astype(BF16)

    return pl.pallas_call(
        body, grid=(F // tn, M // tm), name="ffn_out_bwd_swiglu",
        in_specs=[pl.BlockSpec((tm, K), lambda j, i: (i, 0)), pl.BlockSpec((tn, K), lambda j, i: (j, 0)),
                  pl.BlockSpec((2, tm, tn), lambda j, i: (0, i, j))],
        out_specs=pl.BlockSpec((2, tm, tn), lambda j, i: (0, i, j)),
        out_shape=_sds((2, M, F), BF16), compiler_params=_params(("parallel", "parallel"), big=True),
    )(dff, w2, gu)


ORDER_ONLY = pl.BlockSpec(memory_space=pl.ANY)


def _mm_tn_rows(dep, a, b, rs, name):
    M, Ka = a.shape
    _, N = b.shape

    def body(_, a_ref, b_ref, o_ref):
        g = _dot(a_ref[...], b_ref[...], TN)
        o_ref[0, 0] = g[0:rs, :].astype(BF16)
        o_ref[1, 0] = g[rs:2 * rs, :].astype(BF16)

    return pl.pallas_call(
        body, grid=(N_DEV // 2,), name=name,
        in_specs=[ORDER_ONLY, pl.BlockSpec((M, 2 * rs), lambda ch: (0, ch)), pl.BlockSpec((M, N), lambda ch: (0, 0))],
        out_specs=pl.BlockSpec((2, 1, rs, N), lambda ch: (0, ch, 0, 0)),
        out_shape=_sds((2, N_DEV // 2, rs, N), BF16),
        compiler_params=_params(("parallel",), big=True),
    )(dep, a, b)


def _mm_gathered_nt(dep, a, a_spec, wg, M, tm, name, first=0, count=None, out=None):
    _, K, ns = wg.shape
    count = M // tm if count is None else count
    out = lax.empty((M, K), F32) if out is None else out

    def body(_, a_ref, w_ref, prev_ref, o_ref):
        @pl.when(pl.program_id(1) == 0)
        def _():
            o_ref[...] = jnp.zeros_like(o_ref)

        o_ref[...] += _dot(a_ref[:, 0:ns], w_ref[0], NT) + _dot(a_ref[:, ns:2 * ns], w_ref[1], NT)

    return pl.pallas_call(
        body, grid=(count, N_DEV // 2), name=name,
        in_specs=[ORDER_ONLY, a_spec(tm, 2 * ns, first), pl.BlockSpec((2, K, ns), lambda i, p: (p, 0, 0)), ORDER_ONLY],
        out_specs=pl.BlockSpec((tm, K), lambda i, j: (i + first, 0)),
        out_shape=_sds((M, K), F32), input_output_aliases={3: 0},
        compiler_params=_params(("parallel", "arbitrary"), big=True),
    )(dep, a, wg, out)


def _mm_tn_gathered(dep, h, a, a_spec, ns, name):
    M, K = h.shape

    def body(_, h_ref, a_ref, o_ref):
        o_ref[...] = _dot(h_ref[...], a_ref[...], TN).astype(BF16)

    return pl.pallas_call(
        body, grid=(N_DEV,), name=name,
        in_specs=[ORDER_ONLY, pl.BlockSpec((M, K), lambda j: (0, 0)), a_spec(M, ns)],
        out_specs=pl.BlockSpec((None, K, ns), lambda j: (_slot(j), 0, 0)),
        out_shape=_sds((N_DEV, K, ns), BF16),
        compiler_params=_params(("parallel",), big=True),
    )(dep, h, a)


def _bias_onehot(rbp, max_rel):
    r = lax.broadcasted_iota(jnp.int32, (rbp, TAB), 0)
    m = lax.broadcasted_iota(jnp.int32, (rbp, TAB), 1)
    dist = KPAD - jnp.where(m < WIN, m, m - TAB)
    return (r == jnp.clip(dist, -max_rel, max_rel) + max_rel).astype(F32)


def _attn_setup(i, hp, k_ref, v_ref, gv_ref, kpad, vpad, bias):
    ls = slice(i * ATTN_HEAD_DIM, (i + 1) * ATTN_HEAD_DIM)
    kpad[i][0:KPAD, :] = jnp.zeros((KPAD, ATTN_HEAD_DIM), BF16)
    vpad[i][0:KPAD, :] = jnp.zeros((KPAD, ATTN_HEAD_DIM), BF16)
    kpad[i][KPAD:, :] = k_ref[:, ls].astype(BF16)
    vpad[i][KPAD:, :] = v_ref[:, ls].astype(BF16)
    gvrow = gv_ref[pl.ds(hp * ATTN_HEADS_PER_STEP + i, 1), :]
    tab = pltpu.roll(jnp.broadcast_to(gvrow, (QBLK, TAB)), 0, 1, stride=1, stride_axis=0)
    row = lax.broadcasted_iota(jnp.int32, (QBLK, WIN), 0)
    col = lax.broadcasted_iota(jnp.int32, (QBLK, WIN), 1)
    first = jnp.bitwise_and(row, -CHUNK)
    seen = jnp.logical_and(col >= first, col < first + (N_PAST + 1) * CHUNK)
    bias[i][...] = jnp.where(seen, tab[:, 0:WIN], NEG)


def _attn_probs(b, q_ref, kpad, vpad, bias, col):
    pair = range(ATTN_HEADS_PER_STEP)
    ls = [slice(i * ATTN_HEAD_DIM, (i + 1) * ATTN_HEAD_DIM) for i in pair]
    r0 = pl.multiple_of(b * QBLK, QBLK)
    q = [q_ref[pl.ds(r0, QBLK), ls[i]].astype(BF16) for i in pair]
    kw = [kpad[i][pl.ds(r0, WIN), :] for i in pair]
    vw = [vpad[i][pl.ds(r0, WIN), :] for i in pair]
    s = [_dot(q[i], kw[i], NT) * (ATTN_HEAD_DIM ** -0.5) + bias[i][...] for i in pair]
    s = [jnp.where(col >= KPAD - r0, s[i], NEG) for i in pair]
    p = [jnp.exp(s[i] - jnp.max(s[i], axis=-1, keepdims=True)) for i in pair]
    pn = [p[i] / jnp.sum(p[i], axis=-1, keepdims=True) for i in pair]
    return r0, ls, q, kw, vw, pn


def _attn_fwd(proj, gv, ga, AW):
    T = proj.shape[0]
    AH = ATTN_HEADS_PER_STEP
    W = AH * ATTN_HEAD_DIM
    HP = AW // W

    def body(q_ref, k_ref, v_ref, gv_ref, ga_ref, o_ref, *scratch):
        kpad, vpad, bias = (scratch[k * AH:(k + 1) * AH] for k in range(3))
        hp = pl.program_id(0)
        for i in range(AH):
            _attn_setup(i, hp, k_ref, v_ref, gv_ref, kpad, vpad, bias)
        col = lax.broadcasted_iota(jnp.int32, (QBLK, WIN), 1)

        def block(b, carry):
            pair = range(AH)
            r0, ls, _, _, vw, pn = _attn_probs(b, q_ref, kpad, vpad, bias, col)
            o = [_dot(pn[i].astype(BF16), vw[i], NN) for i in pair]
            r = [lax.rsqrt(jnp.mean(o[i] * o[i], axis=-1, keepdims=True) + EPS) for i in pair]
            outs = [o[i] * r[i] * ga_ref[0:1, ls[i]] for i in pair]
            o_ref[pl.ds(r0, QBLK), :] = jnp.concatenate(outs, axis=1).astype(BF16)
            return carry

        lax.fori_loop(0, T // QBLK, block, 0)

    blk = lambda off: pl.BlockSpec((T, W), lambda hp: (0, off + hp))
    return pl.pallas_call(
        body, grid=(HP,), name="attn_fwd",
        in_specs=[blk(0), blk(HP), blk(2 * HP), pl.BlockSpec(gv.shape, lambda hp: (0, 0)),
                  pl.BlockSpec((1, W), lambda hp: (0, hp))],
        out_specs=pl.BlockSpec((T, W), lambda hp: (0, hp)),
        out_shape=_sds((T, AW), BF16),
        scratch_shapes=[pltpu.VMEM((T + KPAD, ATTN_HEAD_DIM), BF16)] * (2 * AH) + [pltpu.VMEM((QBLK, WIN), F32)] * AH,
        compiler_params=_params(("parallel",), big=True),
    )(proj, proj, proj, gv, ga)


def _attn_bwd(proj, dmixin, gv, ga, AW):
    T = proj.shape[0]
    AH = ATTN_HEADS_PER_STEP
    W = AH * ATTN_HEAD_DIM
    HP = AW // W
    scale = ATTN_HEAD_DIM ** -0.5

    def body(q_ref, k_ref, v_ref, dn_ref, gv_ref, ga_ref, dq_ref, dk_ref, dv_ref, dgv_ref, dga_ref, *scratch):
        kpad, vpad, dkacc, dvacc, bias, dbias = (scratch[k * AH:(k + 1) * AH] for k in range(6))
        hp = pl.program_id(0)
        for i in range(AH):
            _attn_setup(i, hp, k_ref, v_ref, gv_ref, kpad, vpad, bias)
            dkacc[i][...] = jnp.zeros_like(dkacc[i])
            dvacc[i][...] = jnp.zeros_like(dvacc[i])
            dbias[i][...] = jnp.zeros_like(dbias[i])
        dga_ref[...] = jnp.zeros_like(dga_ref)
        col = lax.broadcasted_iota(jnp.int32, (QBLK, WIN), 1)

        def block(b, carry):
            pair = range(AH)
            r0, lss, qs, kws, vws, pns = _attn_probs(b, q_ref, kpad, vpad, bias, col)
            pn_b = [pns[i].astype(BF16) for i in pair]
            o = [_dot(pn_b[i], vws[i], NN) for i in pair]
            r = [lax.rsqrt(jnp.mean(o[i] * o[i], axis=-1, keepdims=True) + EPS) for i in pair]
            dn = [dn_ref[pl.ds(r0, QBLK), lss[i]] for i in pair]
            for i in pair:
                dga_ref[i:i + 1, :] += _colsum(dn[i] * o[i] * r[i])
            a = [dn[i] * ga_ref[0:1, lss[i]] for i in pair]
            do_b = [(r[i] * (a[i] - o[i] * (r[i] * r[i]) * jnp.mean(a[i] * o[i], axis=-1, keepdims=True))).astype(BF16)
                    for i in pair]
            dp = [_dot(do_b[i], vws[i], NT) for i in pair]
            for i in pair:
                dvacc[i][pl.ds(r0, WIN), :] += _dot(pn_b[i], do_b[i], TN)
            ds = [pns[i] * (dp[i] - jnp.sum(pns[i] * dp[i], axis=-1, keepdims=True)) for i in pair]
            for i in pair:
                dbias[i][...] += ds[i]
            ds_b = [ds[i].astype(BF16) for i in pair]
            dq = [_dot(ds_b[i], kws[i], NN) * scale for i in pair]
            dq_ref[pl.ds(r0, QBLK), :] = jnp.concatenate(dq, axis=1).astype(BF16)
            for i in pair:
                dkacc[i][pl.ds(r0, WIN), :] += _dot(ds_b[i], qs[i], TN) * scale
            return carry

        lax.fori_loop(0, T // QBLK, block, 0)

        rr = lax.broadcasted_iota(jnp.int32, (QBLK, QBLK), 0)
        cc = lax.broadcasted_iota(jnp.int32, (QBLK, QBLK), 1)
        flip = (rr + cc == QBLK - 1).astype(BF16)
        for i in range(AH):
            ls = slice(i * ATTN_HEAD_DIM, (i + 1) * ATTN_HEAD_DIM)
            dk_ref[:, ls] = dkacc[i][KPAD:, :].astype(BF16)
            dv_ref[:, ls] = dvacc[i][KPAD:, :].astype(BF16)
            full = jnp.concatenate([dbias[i][...], jnp.zeros((QBLK, TAB - WIN), F32)], axis=1)
            hi = full.astype(BF16)
            lo = (full - hi.astype(F32)).astype(BF16)
            rev = _dot(flip, hi, NN) + _dot(flip, lo, NN)
            dgv_ref[i:i + 1, :] = _colsum(pltpu.roll(rev, TAB - (QBLK - 1), 1, stride=1, stride_axis=0))

    blk = lambda off: pl.BlockSpec((T, W), lambda hp: (0, off + hp))
    accs = lambda dt: [pltpu.VMEM((T + KPAD, ATTN_HEAD_DIM), dt)] * AH
    return pl.pallas_call(
        body, grid=(HP,), name="attn_bwd",
        in_specs=[blk(0), blk(HP), blk(2 * HP), blk(0), pl.BlockSpec(gv.shape, lambda hp: (0, 0)),
                  pl.BlockSpec((1, W), lambda hp: (0, hp))],
        out_specs=[blk(0), blk(0), blk(0), pl.BlockSpec((None, AH, TAB), lambda hp: (hp, 0, 0)),
                   pl.BlockSpec((None, AH, ATTN_HEAD_DIM), lambda hp: (hp, 0, 0))],
        out_shape=[_sds((T, AW), BF16), _sds((T, AW), BF16), _sds((T, AW), BF16),
                   _sds((HP, AH, TAB), F32), _sds((HP, AH, ATTN_HEAD_DIM), F32)],
        scratch_shapes=accs(BF16) + accs(BF16) + accs(F32) + accs(F32) + [pltpu.VMEM((QBLK, WIN), F32)] * (2 * AH),
        compiler_params=_params(("parallel",), big=True),
    )(proj, proj, proj, dmixin, gv, ga)


def _ltri():
    r = lax.broadcasted_iota(jnp.int32, (CHUNK, CHUNK), 0)
    c = lax.broadcasted_iota(jnp.int32, (CHUNK, CHUNK), 1)
    return (c <= r).astype(BF16)


def _tri_dot(tri, v, dims):
    hi = v.astype(BF16)
    lo = (v - hi.astype(F32)).astype(BF16)
    return _dot(tri, hi, dims) + _dot(tri, lo, dims)


HEADS_PER_STEP = 2


def _alternate(stages):
    live = list(stages)
    while live:
        for g in list(live):
            if next(g, StopIteration) is StopIteration:
                live.remove(g)


def _hgrn_gates(n, ls, q_ref, f_ref, lb_ref, ltri):
    r0 = pl.multiple_of(n * CHUNK, CHUNK)
    rows = pl.ds(r0, CHUNK)
    lb = lb_ref[:, ls]
    qb = q_ref[rows, ls]
    sg = _sigmoid(f_ref[rows, ls])
    f = lb + (1.0 - lb) * sg
    sq = _sigmoid(qb)
    b = _tri_dot(ltri, jnp.log(f), NN)
    return rows, lb, qb, sg, f, 1.0 - f, sq, qb * sq, b


def _hgrn_specs(T, RW, AW):
    HG = HEADS_PER_STEP
    W = HG * LANE
    base = 3 * AW // W
    blk_in = lambda off: pl.BlockSpec((T, W), lambda g: (0, base + off + g))
    col = pl.BlockSpec((T, W), lambda g: (0, g))
    return HG, W, RW // W, blk_in, col


def _hgrn_fwd(proj, lb, gn, AW, RW):
    T = proj.shape[0]
    RH, NC, NSUB = RW // LANE, T // CHUNK, CHUNK // SUB
    HG, W, NG, blk_in, col = _hgrn_specs(T, RW, AW)

    def body(q_ref, f_ref, i_ref, g_ref, lb_ref, gn_ref, mix_ref, o_ref, stall_ref, st_all, bs_all, kks_all, ics_all):
        st_all[...] = jnp.zeros_like(st_all)
        ltri = _ltri()
        rowi = lax.broadcasted_iota(jnp.int32, (SUB, 1), 0)

        def one_head(h, n):
            ls = slice(h * LANE, (h + 1) * LANE)
            st, bs, kks, ics = st_all.at[h], bs_all.at[h], kks_all.at[h], ics_all.at[h]
            rows, _, _, _, _, kk, _, qs, b = _hgrn_gates(n, ls, q_ref, f_ref, lb_ref, ltri)
            ic = i_ref[rows, ls]
            stv = st[...]
            stall_ref[h, n] = stv
            bs[...] = b
            kks[...] = kk
            ics[...] = ic
            yield
            o = _dot((qs * jnp.exp(b)).astype(BF16), stv.astype(BF16), NT)
            yield
            ic_b = ic.astype(BF16)
            pieces = []
            for blk in range(NSUB):
                s0 = blk * SUB
                bI, qI = b[s0:s0 + SUB], qs[s0:s0 + SUB]
                if blk == 0:
                    oI = jnp.zeros((SUB, LANE), F32)
                else:
                    ref = bs[s0 - 1:s0, :]
                    qt = (qI * jnp.exp(bI - ref)).astype(BF16)
                    kt = (kk[0:s0] * jnp.exp(ref - b[0:s0])).astype(BF16)
                    oI = _dot(_dot(qt, kt, NT).astype(BF16), ic_b[0:s0], NN)
                    yield
                acc = [oI[g * ROWS:(g + 1) * ROWS] for g in range(SUB // ROWS)]
                for s in range(SUB):
                    sr = s0 + s
                    g0 = s // ROWS
                    lo = g0 * ROWS
                    e = jnp.exp(jnp.minimum(bI[lo:] - bs[sr:sr + 1, :], 0.0))
                    a = jnp.sum(qI[lo:] * kks[sr:sr + 1, :] * e, axis=-1, keepdims=True)
                    add = jnp.where(rowi[lo:] >= s, a, 0.0) * ics[sr:sr + 1, :]
                    for g in range(g0, SUB // ROWS):
                        acc[g] = acc[g] + add[(g - g0) * ROWS:(g - g0 + 1) * ROWS]
                    yield
                pieces.extend(acc)
            o = o + jnp.concatenate(pieces, axis=0)
            bl = bs[CHUNK - 1:CHUNK, :]
            kd = (kk * jnp.exp(bl - b)).astype(BF16)
            st[...] = stv * jnp.exp(bl) + _dot(ic_b, kd, TN)
            yield
            o_ref[rows, ls] = o
            r = lax.rsqrt(jnp.mean(o * o, axis=-1, keepdims=True) + EPS)
            gb = g_ref[rows, ls]
            mix_ref[rows, ls] = (o * r * gn_ref[...] * (gb * _sigmoid(gb))).astype(BF16)

        def chunk(n, carry):
            _alternate([one_head(h, n) for h in range(HG)])
            return carry

        lax.fori_loop(0, NC, chunk, 0)

    tile = pltpu.VMEM((HG, CHUNK, LANE), F32)
    return pl.pallas_call(
        body, grid=(NG,), name="hgrn_fwd",
        in_specs=[blk_in(0), blk_in(NG), blk_in(2 * NG), blk_in(3 * NG), pl.BlockSpec((1, W), lambda g: (0, g)),
                  pl.BlockSpec((1, LANE), lambda g: (0, 0))],
        out_specs=[col, col, pl.BlockSpec((HG, NC, LANE, LANE), lambda g: (g, 0, 0, 0))],
        out_shape=[_sds((T, RW), BF16), _sds((T, RW), F32), _sds((RH, NC, LANE, LANE), F32)],
        scratch_shapes=[pltpu.VMEM((HG, LANE, LANE), F32), tile, tile, tile],
        compiler_params=_params(("parallel",), big=True),
    )(proj, proj, proj, proj, lb, gn)


def _hgrn_bwd(proj, dmixin, o_b, st_all, lb, gn, AW, RW):
    T = proj.shape[0]
    RH, NC, NSUB = RW // LANE, T // CHUNK, CHUNK // SUB
    HG, W, NG, blk_in, col = _hgrn_specs(T, RW, AW)

    def body(q_ref, f_ref, i_ref, g_ref, o_ref, dn_ref, stall_ref, lb_ref, gn_ref,
             dq_ref, df_ref, di_ref, dg_ref, dlb_ref, dgn_ref, dst_all, bs_all, qss_all, dos_all, p2_all, dic_all,
             p1_all):
        dst_all[...] = jnp.zeros_like(dst_all)
        dlb_ref[...] = jnp.zeros_like(dlb_ref)
        dgn_ref[...] = jnp.zeros_like(dgn_ref)
        ltri = _ltri()
        rowi = lax.broadcasted_iota(jnp.int32, (SUB, 1), 0)
        last = lax.broadcasted_iota(jnp.int32, (CHUNK, 1), 0) == CHUNK - 1

        def one_head(h, n):
            ls = slice(h * LANE, (h + 1) * LANE)
            dst, bs, qss, dos = dst_all.at[h], bs_all.at[h], qss_all.at[h], dos_all.at[h]
            p2, dic, p1s = p2_all.at[h], dic_all.at[h], p1_all.at[h]
            rows, lbv, qb, sg, f, kk, sq, qs, b = _hgrn_gates(n, ls, q_ref, f_ref, lb_ref, ltri)
            ic = i_ref[rows, ls]
            stv = stall_ref[h, n]
            dstv = dst[...]
            o = o_ref[rows, ls]
            dn = dn_ref[rows, ls]
            gb = g_ref[rows, ls]
            sgb = _sigmoid(gb)
            r = lax.rsqrt(jnp.mean(o * o, axis=-1, keepdims=True) + EPS)
            gnv = gn_ref[...]
            dg_ref[rows, ls] = (dn * (o * r * gnv) * (sgb * (1.0 + gb * (1.0 - sgb)))).astype(BF16)
            dy = dn * (gb * sgb)
            dgn_ref[h] += _colsum(dy * o * r)
            a_ = dy * gnv
            do = r * (a_ - o * (r * r) * jnp.mean(a_ * o, axis=-1, keepdims=True))
            do_b = do.astype(BF16)
            bs[...] = b
            qss[...] = qs
            dos[...] = do
            yield
            ic_b = ic.astype(BF16)
            eb = jnp.exp(b)
            bl = bs[CHUNK - 1:CHUNK, :]
            ebl = jnp.exp(bl)
            dec = jnp.exp(bl - b)
            kd = (kk * dec).astype(BF16)
            dst_b = dstv.astype(BF16)
            dqs = _dot(do_b, stv.astype(BF16), NN) * eb
            dkk2 = _dot(ic_b, dst_b, NN) * dec
            dic[...] = _dot(kd, dst_b, NT)
            dbl = ebl * _colsum(stv * dstv) + _colsum(kk * dkk2)
            dst[...] = dstv * ebl + _dot(do_b, (qs * eb).astype(BF16), TN)
            yield
            p2[...] = jnp.zeros_like(p2)
            p1_pieces = []
            for blk in range(NSUB):
                s0 = blk * SUB
                bI, qI, doI = b[s0:s0 + SUB], qs[s0:s0 + SUB], do[s0:s0 + SUB]
                if blk == 0:
                    p1 = jnp.zeros((SUB, LANE), F32)
                else:
                    ref = bs[s0 - 1:s0, :]
                    eq = jnp.exp(bI - ref)
                    ek = jnp.exp(ref - b[0:s0])
                    qt = (qI * eq).astype(BF16)
                    kt = (kk[0:s0] * ek).astype(BF16)
                    doI_b = doI.astype(BF16)
                    dic[0:s0, :] += _dot(_dot(qt, kt, NT).astype(BF16), doI_b, TN)
                    da = _dot(doI_b, ic_b[0:s0], NT).astype(BF16)
                    p1 = _dot(da, kt, NN) * eq
                    p2[0:s0, :] += _dot(da, qt, TN) * ek
                    yield
                p1_pieces.append(p1)
                kkI, icI = kk[s0:s0 + SUB], ic[s0:s0 + SUB]
                p2acc = [jnp.zeros((ROWS, LANE), F32) for _ in range(SUB // ROWS)]
                diacc = [jnp.zeros((ROWS, LANE), F32) for _ in range(SUB // ROWS)]
                for t in range(SUB):
                    tr = s0 + t
                    ng = t // ROWS + 1
                    hi = ng * ROWS
                    keep = rowi[:hi] <= t
                    do_t = dos[tr:tr + 1, :]
                    e = jnp.exp(jnp.minimum(bs[tr:tr + 1, :] - bI[:hi], 0.0))
                    qe = qss[tr:tr + 1, :] * e
                    a = jnp.where(keep, jnp.sum(kkI[:hi] * qe, axis=-1, keepdims=True), 0.0)
                    da = jnp.where(keep, jnp.sum(icI[:hi] * do_t, axis=-1, keepdims=True), 0.0)
                    dp2, ddi = da * qe, a * do_t
                    for g in range(ng):
                        p2acc[g] = p2acc[g] + dp2[g * ROWS:(g + 1) * ROWS]
                        diacc[g] = diacc[g] + ddi[g * ROWS:(g + 1) * ROWS]
                    p1s[tr:tr + 1, :] = _colsum(da * kkI[:hi] * e)
                    yield
                p2[s0:s0 + SUB, :] += jnp.concatenate(p2acc, axis=0)
                dic[s0:s0 + SUB, :] += jnp.concatenate(diacc, axis=0)
            dqs = dqs + jnp.concatenate(p1_pieces, axis=0) + p1s[...]
            dkk = dkk2 + p2[...]
            db = qs * dqs - kk * dkk + jnp.where(last, dbl, 0.0)
            dgl = _tri_dot(ltri, db, TN)
            yield
            dfv = dgl / f - dkk
            df_ref[rows, ls] = (dfv * (1.0 - lbv) * sg * (1.0 - sg)).astype(BF16)
            dlb_ref[:, ls] += _colsum(dfv * (1.0 - sg))
            dq_ref[rows, ls] = (dqs * (sq * (1.0 + qb * (1.0 - sq)))).astype(BF16)
            di_ref[rows, ls] = dic[...].astype(BF16)

        def chunk(k, carry):
            _alternate([one_head(h, NC - 1 - k) for h in range(HG)])
            return carry

        lax.fori_loop(0, NC, chunk, 0)

    tile = pltpu.VMEM((HG, CHUNK, LANE), F32)
    return pl.pallas_call(
        body, grid=(NG,), name="hgrn_bwd",
        in_specs=[blk_in(0), blk_in(NG), blk_in(2 * NG), blk_in(3 * NG), col,
                  pl.BlockSpec((T, W), lambda g: (0, AW // W + g)),
                  pl.BlockSpec((HG, NC, LANE, LANE), lambda g: (g, 0, 0, 0)),
                  pl.BlockSpec((1, W), lambda g: (0, g)), pl.BlockSpec((1, LANE), lambda g: (0, 0))],
        out_specs=[col, col, col, col, pl.BlockSpec((1, W), lambda g: (0, g)),
                   pl.BlockSpec((HG, 1, LANE), lambda g: (g, 0, 0))],
        out_shape=[_sds((T, RW), BF16)] * 4 + [_sds((1, RW), F32), _sds((RH, 1, LANE), F32)],
        scratch_shapes=[pltpu.VMEM((HG, LANE, LANE), F32), tile, tile, tile, tile, tile, tile],
        compiler_params=_params(("parallel",), big=True),
    )(proj, proj, proj, proj, o_b, dmixin, st_all, lb, gn)


def _prep(c, lb_logits, rb_pad, max_rel):
    D, RW = c.shape[-1], lb_logits.shape[-1]
    H, rbp = rb_pad.shape

    def body(c_ref, l_ref, rb_ref, cact_ref, lb_ref, gv_ref):
        cv = c_ref[...]
        cact_ref[...] = cv * _sigmoid(cv)
        lb_ref[...] = _sigmoid(l_ref[0:1, :] - l_ref[1:2, :])
        gv_ref[...] = _dot(rb_ref[...], _bias_onehot(rbp, max_rel), NN, HIGHEST)

    return pl.pallas_call(
        body, name="prep", out_shape=[_sds((1, D), F32), _sds((1, RW), F32), _sds((H, TAB), F32)],
    )(c, lb_logits, rb_pad)


def _mod_part(c_all, w_ada_s, b_ada_s):
    B, D = c_all.shape
    ns = w_ada_s.shape[1]
    tn = _tile(ns, 768, LANE)

    def body(c_ref, w_ref, b_ref, o_ref):
        o_ref[...] = _dot(c_ref[...], w_ref[...], NN) + b_ref[...]

    return pl.pallas_call(
        body, grid=(ns // tn,), name="mod_part",
        in_specs=[pl.BlockSpec((B, D), lambda j: (0, 0)), pl.BlockSpec((D, tn), lambda j: (0, j)),
                  pl.BlockSpec((1, tn), lambda j: (0, j))],
        out_specs=pl.BlockSpec((B, tn), lambda j: (0, j)),
        out_shape=_sds((B, ns), F32), compiler_params=_params(("parallel",)),
    )(c_all, w_ada_s, b_ada_s)


def _adam(w, g, m, v):
    m = ADAM_B1 * m + (1.0 - ADAM_B1) * g
    v = ADAM_B2 * v + (1.0 - ADAM_B2) * (g * g)
    m_hat = m * (1.0 / (1.0 - ADAM_B1 ** ADAM_STEP))
    v_hat = v * (1.0 / (1.0 - ADAM_B2 ** ADAM_STEP))
    return -ADAM_LR * (m_hat / (jnp.sqrt(v_hat) + ADAM_EPS) + ADAM_WD * w), m, v


def _adam_ada(c_all, dmod_s, w, m, v):
    B, D = c_all.shape
    ns = w.shape[1]
    tr, tn = _tile(D, 512, LANE), _tile(ns, 768, LANE)

    def body(c_ref, d_ref, w_ref, m_ref, v_ref, g_out, dw_out, m_out, v_out):
        g = _dot(c_ref[...], d_ref[...], TN)
        g_out[...] = g
        dw_out[...], m_out[...], v_out[...] = _adam(w_ref[...], g, m_ref[...], v_ref[...])

    big = pl.BlockSpec((tr, tn), lambda i, j: (i, j))
    return pl.pallas_call(
        body, grid=(D // tr, ns // tn), name="adam_w_ada",
        in_specs=[pl.BlockSpec((B, tr), lambda i, j: (0, i)), pl.BlockSpec((B, tn), lambda i, j: (0, j)),
                  big, big, big],
        out_specs=[big] * 4, out_shape=[_sds((D, ns), F32)] * 4,
        compiler_params=_params(("parallel", "parallel")),
    )(c_all, dmod_s, w, m, v)


def _adam_shard(parts, w, m, v, name):
    R, C = w.shape
    tr = _tile(R, 256, 16)

    def body(p_ref, w_ref, m_ref, v_ref, g_out, dw_out, m_out, v_out):
        g = p_ref[0].astype(F32)
        for k in range(1, N_DEV // 2):
            g = g + p_ref[k].astype(F32)
        g_out[...] = g
        dw_out[...], m_out[...], v_out[...] = _adam(w_ref[...], g, m_ref[...], v_ref[...])

    big = pl.BlockSpec((tr, C), lambda i: (i, 0))
    return pl.pallas_call(
        body, grid=(R // tr,), name=name,
        in_specs=[pl.BlockSpec((N_DEV // 2, tr, C), lambda i: (0, i, 0)), big, big, big],
        out_specs=[big] * 4, out_shape=[_sds((R, C), F32)] * 4,
        compiler_params=_params(("parallel",), big=True),
    )(parts, w, m, v)


def _pair_sum(g8, land, core, name):
    _, NCHIP, R, C = g8.shape
    tr = _tile(R, 1024, 16)

    def body(core_ref, g_ref, l_ref, o_ref):
        o_ref[...] = g_ref[...] + l_ref[...]

    return pl.pallas_call(
        body, name=name,
        grid_spec=pltpu.PrefetchScalarGridSpec(
            num_scalar_prefetch=1, grid=(NCHIP, R // tr),
            in_specs=[pl.BlockSpec((None, None, tr, C), lambda k, i, core_ref: (core_ref[0], k, i, 0)),
                      pl.BlockSpec((None, tr, C), lambda k, i, core_ref: (k, i, 0))],
            out_specs=pl.BlockSpec((None, tr, C), lambda k, i, core_ref: (k, i, 0))),
        out_shape=_sds((NCHIP, R, C), BF16), compiler_params=_params(("parallel", "parallel")),
    )(core, g8, land)


SMALL = ("b_ada", "rel_bias", "attn_norm_g", "lb_logits", "gnorm_g", "ln1_g", "ln1_b", "ln2_g", "ln2_b")


def _small_update(parts, loss_parts, lbv, ws, ms, vs, max_rel):
    n = len(SMALL)

    def body(*refs):
        part_refs = dict(zip(SMALL, refs[:n]))
        loss_in, lb_ref = refs[n], refs[n + 1]
        w_refs, m_refs, v_refs = refs[n + 2:2 * n + 2], refs[2 * n + 2:3 * n + 2], refs[3 * n + 2:4 * n + 2]
        outs = refs[4 * n + 2:]

        def total(ref):
            tot = ref[0]
            for k in range(1, N_DEV):
                tot = tot + ref[k]
            return tot

        outs[0][...] = jnp.sum(total(loss_in), axis=-1, keepdims=True)
        for idx, name in enumerate(SMALL):
            g = total(part_refs[name])
            if name == "rel_bias":
                g = _dot(g, _bias_onehot(w_refs[idx].shape[1], max_rel), NT, HIGHEST)
            elif name == "lb_logits":
                lb = lb_ref[...]
                sign = (1 - 2 * lax.broadcasted_iota(jnp.int32, (2, 1), 0)).astype(F32)
                g = sign * (g * lb * (1.0 - lb))
            elif name == "gnorm_g":
                g = _colsum(g)
            dw, mm, vv = _adam(w_refs[idx][...], g, m_refs[idx][...], v_refs[idx][...])
            outs[1 + 4 * idx][...] = g
            outs[2 + 4 * idx][...] = dw
            outs[3 + 4 * idx][...] = mm
            outs[4 + 4 * idx][...] = vv

    out_shape = [_sds((1, 1), F32)]
    for w in ws:
        out_shape += [_sds(w.shape, F32)] * 4
    return pl.pallas_call(body, name="small_update", out_shape=out_shape, compiler_params=_params(big=True))(
        *[parts[k] for k in SMALL], loss_parts, lbv, *ws, *ms, *vs)


def _place():
    x, y, c = lax.axis_index("x"), lax.axis_index("y"), lax.axis_index("c")
    return x, y, c, [(1 - x, y), (x, 1 - y), (1 - x, 1 - y)]


def _all_gather(shard, name):
    HBM = pl.BlockSpec(memory_space=pl.ANY)

    def body(x_ref, out_ref, send_sems, recv_sems, local_sem):
        x, y, c, chips = _place()
        me, sibling = (x, y, c), (x, y, 1 - c)

        def slot(px, py, pc):
            return out_ref.at[4 * px + 2 * py + pc]

        def copy(k, block, to, src=None):
            return pltpu.make_async_remote_copy(
                src_ref=slot(*block) if src is None else src, dst_ref=slot(*block),
                send_sem=send_sems.at[k], recv_sem=recv_sems.at[k], device_id=to, device_id_type=MESH)

        mine = pltpu.make_async_copy(x_ref, slot(*me), local_sem)
        mine.start()
        first = [copy(0, me, sibling, src=x_ref)]
        first += [copy(1 + j, me, (*chip, c), src=x_ref) for j, chip in enumerate(chips)]
        for cp in first:
            cp.start()
        passed = [copy(4 + j, (*chip, c), sibling) for j, chip in enumerate(chips)]
        for j, chip in enumerate(chips):
            copy(1 + j, (*chip, c), me).wait_recv()
            passed[j].start()
        copy(0, sibling, me).wait_recv()
        for j, chip in enumerate(chips):
            copy(4 + j, (*chip, 1 - c), me).wait_recv()
        for cp in first + passed:
            cp.wait_send()
        mine.wait()

    return pl.pallas_call(
        body, name=name, out_shape=_sds((N_DEV,) + shard.shape, shard.dtype),
        in_specs=[HBM], out_specs=HBM,
        scratch_shapes=[pltpu.SemaphoreType.DMA((7,)), pltpu.SemaphoreType.DMA((7,)), pltpu.SemaphoreType.DMA(())],
    )(shard)


SEM_SPEC = pl.BlockSpec(memory_space=pltpu.SEMAPHORE)
HBM_SPEC = pl.BlockSpec(memory_space=pltpu.HBM)
EFFECT = pltpu.SideEffectType.DATAFLOW_SIDE_EFFECTING


def _remote(src, dst, send_sems, recv_sems, k, dev):
    return pltpu.make_async_remote_copy(src_ref=src, dst_ref=dst, send_sem=send_sems.at[k], recv_sem=recv_sems.at[k],
                                        device_id=dev, device_id_type=MESH)


def _copy_start(name, bufs, plan, n, after, only=None):
    nb = len(bufs)

    def body(*refs):
        send_sems, recv_sems = refs[nb + 1], refs[nb + 2]
        for k, (src, dst, dev) in enumerate(plan(*refs[:nb])):
            if only is not None and k not in only:
                continue
            _remote(src, dst, send_sems, recv_sems, k, dev).start()
        refs[-1][...] = jnp.zeros_like(refs[-1])

    out = pl.pallas_call(
        body, name=name,
        out_shape=(pltpu.SemaphoreType.DMA((n,)), pltpu.SemaphoreType.DMA((n,)),
                   *[pltpu.HBM(b.shape, b.dtype) for b in bufs], _sds((8, LANE), F32)),
        in_specs=[HBM_SPEC] * nb + [ORDER_ONLY],
        out_specs=(SEM_SPEC, SEM_SPEC, *[HBM_SPEC] * nb, pl.BlockSpec(memory_space=pltpu.VMEM)),
        input_output_aliases={i: 2 + i for i in range(nb)},
        compiler_params=pltpu.CompilerParams(has_side_effects=EFFECT),
    )(*[pltpu.with_memory_space_constraint(b, pltpu.HBM) for b in bufs], after)
    return (out[0], out[1]), list(out[2:2 + nb]), out[-1]


def _copy_wait(name, sems, bufs, plan, after, only=None):
    nb = len(bufs)

    def body(*refs):
        send_sems, recv_sems = refs[nb], refs[nb + 1]
        for k, (src, dst, dev) in enumerate(plan(*refs[:nb])):
            if only is not None and k not in only:
                continue
            cp = _remote(src, dst, send_sems, recv_sems, k, dev)
            cp.wait_send()
            cp.wait_recv()

    out = pl.pallas_call(
        body, name=name, out_shape=tuple(pltpu.HBM(b.shape, b.dtype) for b in bufs),
        in_specs=[HBM_SPEC] * nb + [SEM_SPEC, SEM_SPEC, pl.BlockSpec(memory_space=pl.ANY)],
        out_specs=tuple([HBM_SPEC] * nb), input_output_aliases={i: i for i in range(nb)},
        compiler_params=pltpu.CompilerParams(has_side_effects=EFFECT),
    )(*bufs, sems[0], sems[1], after)
    return list(out)


def _ag_plan_chips(shard_ref, out_ref):
    x, y, c, chips = _place()
    mine = out_ref.at[4 * x + 2 * y + c]
    return [(shard_ref, mine, (x, y, 1 - c))] + [(shard_ref, mine, (*chip, c)) for chip in chips]


def _ag_plan_pass(out_ref):
    x, y, c, chips = _place()
    slots = [out_ref.at[4 * chip[0] + 2 * chip[1] + c] for chip in chips]
    return [(s, s, (x, y, 1 - c)) for s in slots]


def _rs_plan_pair(g_ref, land_ref):
    x, y, c, _ = _place()
    return [(g_ref.at[1 - c], land_ref, (x, y, 1 - c))]


def _rs_plan_chips(p_ref, land_ref):
    x, y, c, chips = _place()
    return [(p_ref.at[2 * chip[0] + chip[1]], land_ref.at[2 * x + y], (*chip, c)) for chip in chips]


class _Gather:
    def __init__(self, shard, me, tag, after):
        self.tag = tag
        out = lax.dynamic_update_slice(lax.empty((N_DEV,) + shard.shape, shard.dtype), shard[None],
                                       (me,) + (0,) * shard.ndim)
        self.sems, (self.shard, self.out), self.token = _copy_start(
            "ag_start_" + tag, [shard, out], _ag_plan_chips, 4, after)
        self.groups = []

    def arrived(self, after, copies):
        name = "ag_wait_%s_%s" % (self.tag, "".join(map(str, copies)))
        self.shard, self.out = _copy_wait(name, self.sems, [self.shard, self.out], _ag_plan_chips, after, copies)
        return self.out

    def pass_on(self, after, blocks):
        name = "ag_pass_%s_%s" % (self.tag, "".join(map(str, blocks)))
        sems, (self.out,), _ = _copy_start(name, [self.out], _ag_plan_pass, 3, after, blocks)
        self.groups.append((sems, blocks))
        return self.out

    def passed(self, after, group):
        sems, blocks = self.groups[group]
        name = "ag_pass_wait_%s_%s" % (self.tag, "".join(map(str, blocks)))
        self.out = _copy_wait(name, sems, [self.out], _ag_plan_pass, after, blocks)[0]
        return self.out

    def arrived_from_chips(self, after):
        self.arrived(after, (0, 1, 2, 3))
        return self.pass_on(after, (0, 1, 2))

    def passed_on(self, after):
        return self.passed(after, 0)


def _ag_plan_direct(src_ref, out_ref):
    x, y, c, chips = _place()
    mine = out_ref.at[4 * x + 2 * y + c]
    peers = [(x, y, 1 - c)] + [(*chip, pc) for chip in chips for pc in (c, 1 - c)]
    return [(src_ref, mine, peer) for peer in peers]


class _SmallGather:
    def __init__(self, block, me, tag):
        self.tag = tag
        out = lax.dynamic_update_slice(lax.empty((N_DEV,) + block.shape, block.dtype), block[None],
                                       (me,) + (0,) * block.ndim)
        self.sems, self.bufs, self.token = _copy_start(
            "ag_direct_start_" + tag, [block, out], _ag_plan_direct, N_DEV - 1, jnp.zeros((1,), F32))

    def done(self, after):
        return _copy_wait("ag_direct_wait_" + self.tag, self.sems, self.bufs, _ag_plan_direct, after)[1]


class _ReduceScatter:
    def __init__(self, g8, tag):
        self.tag = tag
        land = lax.empty(g8.shape[1:], g8.dtype)
        self.sems, self.bufs, self.token = _copy_start(
            "rs_pair_start_" + tag, [g8, land], _rs_plan_pair, 1, jnp.zeros((1,), F32))

    def pair_done(self, core, chip, after):
        g8, land = _copy_wait("rs_pair_wait_" + self.tag, self.sems, self.bufs, _rs_plan_pair, after)
        p4 = _pair_sum(g8, land, core, "rs_pair_sum_" + self.tag)
        own = lax.dynamic_slice_in_dim(p4, chip, 1, axis=0)
        land2 = lax.dynamic_update_slice(lax.empty(p4.shape, p4.dtype), own, (chip, 0, 0))
        self.sems, self.bufs, self.token = _copy_start(
            "rs_chips_start_" + self.tag, [p4, land2], _rs_plan_chips, 3, jnp.zeros((1,), F32))

    def sums(self, after):
        return _copy_wait("rs_chips_wait_" + self.tag, self.sems, self.bufs, _rs_plan_chips, after)[1]


ORDER = ("w_ada", "b_ada", "w_in", "rel_bias", "attn_norm_g", "lb_logits", "gnorm_g", "w_o", "ln1_g", "ln1_b",
         "w_ffn_in", "w_ffn_out", "ln2_g", "ln2_b")


def kernel(x, c, w_ada, b_ada, w_in, rel_bias, attn_norm_g, lb_logits, gnorm_g, w_o, ln1_g, ln1_b, w_ffn_in, w_ffn_out, ln2_g, ln2_b, loss_target, m_w_ada, m_b_ada, m_w_in, m_rel_bias, m_attn_norm_g, m_lb_logits, m_gnorm_g, m_w_o, m_ln1_g, m_ln1_b, m_w_ffn_in, m_w_ffn_out, m_ln2_g, m_ln2_b, v_w_ada, v_b_ada, v_w_in, v_rel_bias, v_attn_norm_g, v_lb_logits, v_gnorm_g, v_w_o, v_ln1_g, v_ln1_b, v_w_ffn_in, v_w_ffn_out, v_ln2_g, v_ln2_b):
    W = dict(w_ada=w_ada, b_ada=b_ada, w_in=w_in, rel_bias=rel_bias, attn_norm_g=attn_norm_g, lb_logits=lb_logits,
             gnorm_g=gnorm_g, w_o=w_o, ln1_g=ln1_g, ln1_b=ln1_b, w_ffn_in=w_ffn_in, w_ffn_out=w_ffn_out,
             ln2_g=ln2_g, ln2_b=ln2_b)
    M = dict(w_ada=m_w_ada, b_ada=m_b_ada, w_in=m_w_in, rel_bias=m_rel_bias, attn_norm_g=m_attn_norm_g,
             lb_logits=m_lb_logits, gnorm_g=m_gnorm_g, w_o=m_w_o, ln1_g=m_ln1_g, ln1_b=m_ln1_b,
             w_ffn_in=m_w_ffn_in, w_ffn_out=m_w_ffn_out, ln2_g=m_ln2_g, ln2_b=m_ln2_b)
    V = dict(w_ada=v_w_ada, b_ada=v_b_ada, w_in=v_w_in, rel_bias=v_rel_bias, attn_norm_g=v_attn_norm_g,
             lb_logits=v_lb_logits, gnorm_g=v_gnorm_g, w_o=v_w_o, ln1_g=v_ln1_g, ln1_b=v_ln1_b,
             w_ffn_in=v_w_ffn_in, w_ffn_out=v_w_ffn_out, ln2_g=v_ln2_g, ln2_b=v_ln2_b)

    x2, tgt = x[0], loss_target[0]
    T, D = x2.shape
    AW, RW = attn_norm_g.shape[-1], lb_logits.shape[-1]
    MIX = AW + RW
    H, RH = AW // ATTN_HEAD_DIM, RW // LANE
    RB = rel_bias.shape[-1]
    max_rel = (RB - 1) // 2
    rbp = -(-RB // LANE) * LANE
    F = w_ffn_out.shape[1] * N_DEV
    half = N_DEV // 2
    xi, yi, ci = lax.axis_index("x"), lax.axis_index("y"), lax.axis_index("c")
    me = 4 * xi + 2 * yi + ci
    core = jnp.reshape(ci, (1,)).astype(jnp.int32)
    pad_rb = lambda a: jnp.pad(a[0], ((0, 0), (0, rbp - RB)))

    chip = 2 * xi + yi

    c_act, lbv, gv = _prep(c, lb_logits, pad_rb(rel_bias), max_rel)
    c_all = _all_gather(c_act, "ag_c").reshape(N_DEV, D)
    ns_ada = w_ada.shape[-1]
    mod_part = _mod_part(c_all, w_ada[0], lax.dynamic_slice_in_dim(b_ada, me * ns_ada, ns_ada, axis=1))
    mod_all = _all_gather(mod_part, "ag_mod")
    mod6 = lax.dynamic_index_in_dim(mod_all, me, axis=1, keepdims=False).reshape(6, D)

    ag_in = _Gather(w_in[0].astype(BF16), me, "w_in", mod_all)
    ag_o = _Gather(w_o[0].astype(BF16), me, "w_o", ag_in.token)
    ag_f1 = _Gather(w_ffn_in[0].astype(BF16), me, "w_ffn_in", ag_o.token)
    ag_f2 = _Gather(w_ffn_out[0].astype(BF16), me, "w_ffn_out", ag_f1.token)

    h1 = _ln_mod(x2, mod6 + ag_f2.token[0, 0])
    ids = lambda pairs: jnp.stack([4 * px + 2 * py + pc for px, py, pc in pairs]).astype(jnp.int32)
    others = [(1 - xi, yi), (xi, 1 - yi), (1 - xi, 1 - yi)]
    proj = lax.empty((T, w_in.shape[-1] * N_DEV), F32)
    proj = _mm_gathered(h1, ag_in.arrived(h1, (0,)), ids([(xi, yi, ci), (xi, yi, 1 - ci)]), proj, "in_proj_a")
    ag_in.arrived(proj, (1, 2, 3))
    proj = _mm_gathered(h1, ag_in.pass_on(proj, (0, 1, 2)), ids([(*ch, ci) for ch in others]), proj, "in_proj_b")
    wg_in = ag_in.passed(proj, 0)
    proj = _mm_gathered(h1, wg_in, ids([(*ch, 1 - ci) for ch in others]), proj, "in_proj_c")
    ag_o.arrived_from_chips(proj)
    mix_a = _attn_fwd(proj, gv, attn_norm_g, AW)
    wg_o = ag_o.passed_on(mix_a).reshape(MIX, D)
    mix_b, o_b, st_all = _hgrn_fwd(proj, lbv, gnorm_g, AW, RW)
    mixin = jnp.concatenate([mix_a, mix_b], axis=1)
    mix = _mm_nn(mixin, wg_o, "out_proj")
    ag_f1.arrived_from_chips(mix)
    x1, h2 = _mid_fwd(x2, mix, mod6, ln1_g, ln1_b)
    wg_f1 = ag_f1.passed_on(h2)
    gu, act = _mm_swiglu(h2, wg_f1)
    ag_f2.arrived_from_chips(act)
    wg_f2 = ag_f2.passed_on(act).reshape(F, D)
    ff = _mm_nn(act, wg_f2, "ffn_out")
    dff, dx1a, vec_a = _final(x1, ff, mod6, ln2_g, ln2_b, tgt)

    du = _mm_swiglu_bwd(dff, wg_f2, gu)
    rs_f2 = _ReduceScatter(_mm_tn_rows(dff, act, dff, F // N_DEV, "grad_w_ffn_out"), "w_ffn_out")
    tm = _tile(T, 512, 16)
    du_ij = lambda tm_, w, first: pl.BlockSpec((None, tm_, w), lambda i, p: (p // (half // 2), i + first, p % (half // 2)))
    du_j = lambda rows, ns: pl.BlockSpec((None, rows, ns), lambda j: (j // half, 0, j % half))
    dh2 = _mm_gathered_nt(rs_f2.token, du, du_ij, wg_f1, T, tm, "ffn_in_bwd")
    rs_f2.pair_done(core, chip, dh2)
    gw_f1 = _mm_tn_gathered(rs_f2.token, h2, du, du_j, wg_f1.shape[-1], "grad_w_ffn_in")
    rs_f1 = _ReduceScatter(gw_f1.reshape(2, half, D, -1), "w_ffn_in")
    dmix, dxa, vec_b = _mid_bwd(x2, mix, x1, dx1a, dh2, mod6 + rs_f1.token[0, 0], ln1_g)
    dmixin = _mm_nt(dmix, wg_o, "out_proj_bwd")
    rs_f1.pair_done(core, chip, dmixin)
    rs_o = _ReduceScatter(_mm_tn_rows(rs_f1.token, mixin, dmix, MIX // N_DEV, "grad_w_o"), "w_o")
    dq, dk, dv, dgv, dga = _attn_bwd(proj, dmixin, gv + rs_o.token[0, 0], attn_norm_g, AW)
    rs_o.pair_done(core, chip, dq)
    dqb, dfl, dib, dgb, dlb, dgn = _hgrn_bwd(proj, dmixin, o_b, st_all, lbv + rs_o.token[0, 0], gnorm_g, AW, RW)
    dproj = jnp.concatenate([dq, dk, dv, dqb, dfl, dib, dgb], axis=1)
    p_ij = lambda tm_, w, first: pl.BlockSpec((tm_, w), lambda i, p: (i + first, p))
    p_j = lambda rows, ns: pl.BlockSpec((rows, ns), lambda j: (0, j))
    gw_in = _mm_tn_gathered(rs_o.token, h1, dproj, p_j, wg_in.shape[-1], "grad_w_in")
    rs_in = _ReduceScatter(gw_in.reshape(2, half, D, -1), "w_in")
    n_tiles = T // tm
    dh1 = _mm_gathered_nt(rs_in.token, dproj, p_ij, wg_in, T, tm, "in_proj_bwd_a", 0, n_tiles // 2)
    rs_in.pair_done(core, chip, dh1)
    dh1 = _mm_gathered_nt(rs_in.token, dproj, p_ij, wg_in, T, tm, "in_proj_bwd_b", n_tiles // 2,
                          n_tiles - n_tiles // 2, dh1)
    grad_x, vec_c = _first_bwd(x2, dh1, dxa, mod6)

    dmod = jnp.concatenate([vec_c[1:2], vec_c[0:1], vec_b[4:5], vec_b[1:2], vec_b[0:1], vec_a[2:3]], axis=0)
    pieces = dict(b_ada=dmod, rel_bias=dgv, attn_norm_g=dga, lb_logits=dlb, gnorm_g=dgn, ln1_g=vec_b[2:3],
                  ln1_b=vec_b[3:4], ln2_g=vec_a[0:1], ln2_b=vec_a[1:2], loss=vec_a[3:4])
    widths = dict(b_ada=(1, 6 * D), rel_bias=(H, TAB), attn_norm_g=(1, AW), lb_logits=(1, RW), gnorm_g=(RH, LANE),
                  ln1_g=(1, D), ln1_b=(1, D), ln2_g=(1, D), ln2_b=(1, D), loss=(1, D))
    packed = jnp.concatenate([pieces[k].reshape(-1, LANE) for k in widths], axis=0)
    small_ag = _SmallGather(packed, me, "small")
    after, res_big = small_ag.token, {}
    for k, rs in (("w_ffn_out", rs_f2), ("w_ffn_in", rs_f1), ("w_o", rs_o), ("w_in", rs_in)):
        four = _adam_shard(rs.sums(after), W[k][0], M[k][0], V[k][0], "adam_" + k)
        res_big[k] = [a[None] for a in four]
        after = four[0]
    gathered = small_ag.done(after)
    parts, r0 = {}, 0
    for k, (rows, width) in widths.items():
        nr = rows * width // LANE
        parts[k] = gathered[:, r0:r0 + nr, :].reshape(N_DEV, rows, width)
        r0 += nr
    prep_small = lambda d, k: pad_rb(d[k]) if k == "rel_bias" else d[k]
    small = _small_update(parts, parts["loss"], lbv, [prep_small(W, k) for k in SMALL],
                          [prep_small(M, k) for k in SMALL], [prep_small(V, k) for k in SMALL], max_rel)
    loss = small[0].reshape(())
    res = {}
    for idx, k in enumerate(SMALL):
        four = small[1 + 4 * idx:5 + 4 * idx]
        if k == "rel_bias":
            four = [a[:, :RB][None] for a in four]
        res[k] = list(four)

    res.update(res_big)
    dmod_s = lax.dynamic_slice_in_dim(parts["b_ada"].reshape(N_DEV, 6 * D), me * ns_ada, ns_ada, axis=1)
    res["w_ada"] = [a[None] for a in _adam_ada(c_all, dmod_s, w_ada[0], m_w_ada[0], v_w_ada[0])]

    out = [loss, grad_x[None]]
    for field in range(4):
        out += [res[k][field] for k in ORDER]
    return tuple(out)
```

```python
import jax
import jax.numpy as jnp
from jax import lax
from jax.experimental import pallas as pl
from jax.experimental.pallas import tpu as pltpu

F32 = jnp.float32
BF16 = jnp.bfloat16
MESH = pl.DeviceIdType.MESH
HIGHEST = lax.Precision.HIGHEST

N_DEV = 8
CHUNK = 64
N_PAST = 8
QBLK = 4 * CHUNK
KPAD = N_PAST * CHUNK
WIN = KPAD + QBLK
TAB = 1024
ATTN_HEAD_DIM = 64
ATTN_HEADS_PER_STEP = 4
SUB = 32
ROWS = 8
LANE = 128
EPS = 1e-5
ALPHA = 2.0 ** 0.25
ADAM_LR, ADAM_B1, ADAM_B2, ADAM_EPS, ADAM_WD, ADAM_STEP = 0.001, 0.9, 0.999, 1e-08, 0.01, 10
NEG = -1e30
VMEM_LIMIT = 56 * 1024 * 1024


def _sds(shape, dtype):
    return jax.ShapeDtypeStruct(tuple(shape), dtype)


def _tile(n, pref, mult):
    best = None
    for t in range(mult, min(n, pref) + 1, mult):
        if n % t == 0:
            best = t
    return n if best is None else best


def _params(sem=None, big=False):
    kw = {}
    if sem is not None:
        kw["dimension_semantics"] = sem
    if big:
        kw["vmem_limit_bytes"] = VMEM_LIMIT
    return pltpu.CompilerParams(**kw)


def _sigmoid(v):
    return 1.0 / (1.0 + jnp.exp(-v))


def _dot(a, b, dims, precision=None):
    return lax.dot_general(a, b, (dims, ((), ())), preferred_element_type=F32, precision=precision)


NN = ((1,), (0,))
NT = ((1,), (1,))
TN = ((0,), (0,))


def _ln(v):
    mu = jnp.mean(v, axis=-1, keepdims=True)
    d = v - mu
    rstd = lax.rsqrt(jnp.mean(d * d, axis=-1, keepdims=True) + EPS)
    return d * rstd, rstd


def _ln_bwd(dxh, xh, rstd):
    return rstd * (dxh - jnp.mean(dxh, axis=-1, keepdims=True) - xh * jnp.mean(dxh * xh, axis=-1, keepdims=True))


def _colsum(v):
    return jnp.sum(v, axis=0, keepdims=True)


def _ln_mod(x2, mod6):
    T, D = x2.shape
    tm = _tile(T, 256, 8)

    def body(x_ref, mod_ref, o_ref):
        xh, _ = _ln(x_ref[...])
        o_ref[...] = (xh * (1.0 + mod_ref[1:2, :]) + mod_ref[0:1, :]).astype(BF16)

    return pl.pallas_call(
        body, grid=(T // tm,), name="ln_mod",
        in_specs=[pl.BlockSpec((tm, D), lambda i: (i, 0)), pl.BlockSpec((6, D), lambda i: (0, 0))],
        out_specs=pl.BlockSpec((tm, D), lambda i: (i, 0)),
        out_shape=_sds((T, D), BF16), compiler_params=_params(("parallel",)),
    )(x2, mod6)


def _mid_fwd(x2, mix, mod6, ln1_g, ln1_b):
    T, D = x2.shape
    tm = _tile(T, 256, 8)

    def body(x_ref, mix_ref, mod_ref, g_ref, b_ref, x1_ref, h2_ref):
        zh, _ = _ln(ALPHA * x_ref[...] + mod_ref[2:3, :] * mix_ref[...])
        x1 = zh * g_ref[...] + b_ref[...]
        x1_ref[...] = x1
        xh, _ = _ln(x1)
        h2_ref[...] = (xh * (1.0 + mod_ref[4:5, :]) + mod_ref[3:4, :]).astype(BF16)

    row = pl.BlockSpec((tm, D), lambda i: (i, 0))
    vec = pl.BlockSpec((1, D), lambda i: (0, 0))
    return pl.pallas_call(
        body, grid=(T // tm,), name="mid_fwd",
        in_specs=[row, row, pl.BlockSpec((6, D), lambda i: (0, 0)), vec, vec],
        out_specs=[row, row],
        out_shape=[_sds((T, D), F32), _sds((T, D), BF16)], compiler_params=_params(("parallel",)),
    )(x2, mix, mod6, ln1_g, ln1_b)


def _final(x1, ff, mod6, ln2_g, ln2_b, tgt):
    T, D = x1.shape
    tm = _tile(T, 256, 8)

    def body(x1_ref, ff_ref, mod_ref, g_ref, b_ref, t_ref, dff_ref, dx1_ref, vec_ref):
        @pl.when(pl.program_id(0) == 0)
        def _():
            vec_ref[...] = jnp.zeros_like(vec_ref)

        ff_v = ff_ref[...]
        gate2 = mod_ref[5:6, :]
        zh, rstd = _ln(ALPHA * x1_ref[...] + gate2 * ff_v)
        err = zh * g_ref[...] + b_ref[...] - t_ref[...]
        dy = err * (1.0 / D)
        dz = _ln_bwd(dy * g_ref[...], zh, rstd)
        dff_ref[...] = (gate2 * dz).astype(BF16)
        dx1_ref[...] = ALPHA * dz
        vec_ref[0:1, :] += _colsum(dy * zh)
        vec_ref[1:2, :] += _colsum(dy)
        vec_ref[2:3, :] += _colsum(dz * ff_v)
        vec_ref[3:4, :] += _colsum(err * err) * (0.5 / D)

    row = pl.BlockSpec((tm, D), lambda i: (i, 0))
    vec = pl.BlockSpec((1, D), lambda i: (0, 0))
    return pl.pallas_call(
        body, grid=(T // tm,), name="final_fwd_bwd",
        in_specs=[row, row, pl.BlockSpec((6, D), lambda i: (0, 0)), vec, vec, row],
        out_specs=[row, row, pl.BlockSpec((8, D), lambda i: (0, 0))],
        out_shape=[_sds((T, D), BF16), _sds((T, D), F32), _sds((8, D), F32)],
        compiler_params=_params(("arbitrary",)),
    )(x1, ff, mod6, ln2_g, ln2_b, tgt)


def _mid_bwd(x2, mix, x1, dx1a, dh2, mod6, ln1_g):
    T, D = x2.shape
    tm = _tile(T, 256, 8)

    def body(x_ref, mix_ref, x1_ref, dx1a_ref, dh2_ref, mod_ref, g_ref, dmix_ref, dxa_ref, vec_ref):
        @pl.when(pl.program_id(0) == 0)
        def _():
            vec_ref[...] = jnp.zeros_like(vec_ref)

        dh2 = dh2_ref[...]
        xh, rstd = _ln(x1_ref[...])
        dx1 = dx1a_ref[...] + _ln_bwd(dh2 * (1.0 + mod_ref[4:5, :]), xh, rstd)
        mix_v = mix_ref[...]
        gate1 = mod_ref[2:3, :]
        zh, rstdz = _ln(ALPHA * x_ref[...] + gate1 * mix_v)
        dz = _ln_bwd(dx1 * g_ref[...], zh, rstdz)
        dmix_ref[...] = (gate1 * dz).astype(BF16)
        dxa_ref[...] = ALPHA * dz
        vec_ref[0:1, :] += _colsum(dh2 * xh)
        vec_ref[1:2, :] += _colsum(dh2)
        vec_ref[2:3, :] += _colsum(dx1 * zh)
        vec_ref[3:4, :] += _colsum(dx1)
        vec_ref[4:5, :] += _colsum(dz * mix_v)

    row = pl.BlockSpec((tm, D), lambda i: (i, 0))
    vec = pl.BlockSpec((1, D), lambda i: (0, 0))
    return pl.pallas_call(
        body, grid=(T // tm,), name="mid_bwd",
        in_specs=[row, row, row, row, row, pl.BlockSpec((6, D), lambda i: (0, 0)), vec],
        out_specs=[row, row, pl.BlockSpec((8, D), lambda i: (0, 0))],
        out_shape=[_sds((T, D), BF16), _sds((T, D), F32), _sds((8, D), F32)],
        compiler_params=_params(("arbitrary",)),
    )(x2, mix, x1, dx1a, dh2, mod6, ln1_g)


def _first_bwd(x2, dh1, dxa, mod6):
    T, D = x2.shape
    tm = _tile(T, 256, 8)

    def body(x_ref, dh1_ref, dxa_ref, mod_ref, gx_ref, vec_ref):
        @pl.when(pl.program_id(0) == 0)
        def _():
            vec_ref[...] = jnp.zeros_like(vec_ref)

        dh1 = dh1_ref[...]
        xh, rstd = _ln(x_ref[...])
        gx_ref[...] = dxa_ref[...] + _ln_bwd(dh1 * (1.0 + mod_ref[1:2, :]), xh, rstd)
        vec_ref[0:1, :] += _colsum(dh1 * xh)
        vec_ref[1:2, :] += _colsum(dh1)

    row = pl.BlockSpec((tm, D), lambda i: (i, 0))
    return pl.pallas_call(
        body, grid=(T // tm,), name="first_bwd",
        in_specs=[row, row, row, pl.BlockSpec((6, D), lambda i: (0, 0))],
        out_specs=[row, pl.BlockSpec((8, D), lambda i: (0, 0))],
        out_shape=[_sds((T, D), F32), _sds((8, D), F32)],
        compiler_params=_params(("arbitrary",)),
    )(x2, dh1, dxa, mod6)


def _slot(j):
    return (j % 2) * 4 + j // 2


def _mm_gathered(a, wg, shards, out, name):
    M, K = a.shape
    _, _, ns = wg.shape
    tm = _tile(M, 512, 16)

    def body(shards_ref, a_ref, w_ref, prev_ref, o_ref):
        o_ref[...] = _dot(a_ref[...], w_ref[...], NN)

    return pl.pallas_call(
        body, name=name,
        grid_spec=pltpu.PrefetchScalarGridSpec(
            num_scalar_prefetch=1, grid=(shards.shape[0], M // tm),
            in_specs=[pl.BlockSpec((tm, K), lambda j, i, s: (i, 0)),
                      pl.BlockSpec((None, K, ns), lambda j, i, s: (s[j], 0, 0)), ORDER_ONLY],
            out_specs=pl.BlockSpec((tm, ns), lambda j, i, s: (i, s[j]))),
        out_shape=_sds((M, N_DEV * ns), F32), input_output_aliases={3: 0},
        compiler_params=_params(("parallel", "parallel"), big=True),
    )(shards, a, wg, out)


def _mm_nn(a, b, name):
    M, K = a.shape
    _, N = b.shape
    tm, tn = _tile(M, 512, 16), _tile(N, 1024, LANE)

    def body(a_ref, b_ref, o_ref):
        o_ref[...] = _dot(a_ref[...], b_ref[...], NN)

    return pl.pallas_call(
        body, grid=(N // tn, M // tm), name=name,
        in_specs=[pl.BlockSpec((tm, K), lambda j, i: (i, 0)), pl.BlockSpec((K, tn), lambda j, i: (0, j))],
        out_specs=pl.BlockSpec((tm, tn), lambda j, i: (i, j)),
        out_shape=_sds((M, N), F32), compiler_params=_params(("parallel", "parallel"), big=True),
    )(a, b)


def _mm_nt(a, b, name):
    M, K = a.shape
    N, _ = b.shape
    tm, tn = _tile(M, 512, 16), _tile(N, 1024, LANE)

    def body(a_ref, b_ref, o_ref):
        o_ref[...] = _dot(a_ref[...], b_ref[...], NT)

    return pl.pallas_call(
        body, grid=(M // tm, N // tn), name=name,
        in_specs=[pl.BlockSpec((tm, K), lambda i, j: (i, 0)), pl.BlockSpec((tn, K), lambda i, j: (j, 0))],
        out_specs=pl.BlockSpec((tm, tn), lambda i, j: (i, j)),
        out_shape=_sds((M, N), F32), compiler_params=_params(("parallel", "parallel"), big=True),
    )(a, b)


def _mm_swiglu(h2, wg):
    M, K = h2.shape
    _, _, ns = wg.shape
    half = N_DEV // 2
    tm = _tile(M, 256, 16)

    def body(a_ref, wgate_ref, wup_ref, gu_ref, act_ref):
        a = a_ref[...]
        g = _dot(a, wgate_ref[...], NN)
        u = _dot(a, wup_ref[...], NN)
        sg = _sigmoid(g)
        silu = g * sg
        gu_ref[0] = u * (sg * (1.0 + g * (1.0 - sg)))
        gu_ref[1] = silu
        act_ref[...] = (silu * u).astype(BF16)

    return pl.pallas_call(
        body, grid=(half, M // tm), name="ffn_in_swiglu",
        in_specs=[pl.BlockSpec((tm, K), lambda j, i: (i, 0)),
                  pl.BlockSpec((None, K, ns), lambda j, i: (j, 0, 0)),
                  pl.BlockSpec((None, K, ns), lambda j, i: (j + half, 0, 0))],
        out_specs=[pl.BlockSpec((2, tm, ns), lambda j, i: (0, i, j)), pl.BlockSpec((tm, ns), lambda j, i: (i, j))],
        out_shape=[_sds((2, M, half * ns), F32), _sds((M, half * ns), BF16)],
        compiler_params=_params(("parallel", "parallel"), big=True),
    )(h2, wg, wg)


def _mm_swiglu_bwd(dff, w2, gu):
    M, K = dff.shape
    F = w2.shape[0]
    tm, tn = _tile(M, 512, 16), _tile(F, 1408, LANE)

    def body(a_ref, b_ref, gu_ref, du_ref):
        da = _dot(a_ref[...], b_ref[...], NT)
        du_ref[0] = (da * gu_ref[0]).astype(BF16)
        du_ref[1] = (da * gu_ref[1]).astype(BF16)

    return pl.pallas_call(
        body, grid=(F // tn, M // tm), name="ffn_out_bwd_swiglu",
        in_specs=[pl.BlockSpec((tm, K), lambda j, i: (i, 0)), pl.BlockSpec((tn, K), lambda j, i: (j, 0)),
                  pl.BlockSpec((2, tm, tn), lambda j, i: (0, i, j))],
        out_specs=pl.BlockSpec((2, tm, tn), lambda j, i: (0, i, j)),
        out_shape=_sds((2, M, F), BF16), compiler_params=_params(("parallel", "parallel"), big=True),
    )(dff, w2, gu)


ORDER_ONLY = pl.BlockSpec(memory_space=pl.ANY)


def _mm_tn_rows(dep, a, b, rs, name):
    M, Ka = a.shape
    _, N = b.shape

    def body(_, a_ref, b_ref, o_ref):
        g = _dot(a_ref[...], b_ref[...], TN)
        o_ref[0, 0] = g[0:rs, :].astype(BF16)
        o_ref[1, 0] = g[rs:2 * rs, :].astype(BF16)

    return pl.pallas_call(
        body, grid=(N_DEV // 2,), name=name,
        in_specs=[ORDER_ONLY, pl.BlockSpec((M, 2 * rs), lambda ch: (0, ch)), pl.BlockSpec((M, N), lambda ch: (0, 0))],
        out_specs=pl.BlockSpec((2, 1, rs, N), lambda ch: (0, ch, 0, 0)),
        out_shape=_sds((2, N_DEV // 2, rs, N), BF16),
        compiler_params=_params(("parallel",), big=True),
    )(dep, a, b)


def _mm_gathered_nt(dep, a, a_spec, wg, M, tm, name, first=0, count=None, out=None):
    _, K, ns = wg.shape
    count = M // tm if count is None else count
    out = lax.empty((M, K), F32) if out is None else out

    def body(_, a_ref, w_ref, prev_ref, o_ref):
        @pl.when(pl.program_id(1) == 0)
        def _():
            o_ref[...] = jnp.zeros_like(o_ref)

        o_ref[...] += _dot(a_ref[:, 0:ns], w_ref[0], NT) + _dot(a_ref[:, ns:2 * ns], w_ref[1], NT)

    return pl.pallas_call(
        body, grid=(count, N_DEV // 2), name=name,
        in_specs=[ORDER_ONLY, a_spec(tm, 2 * ns, first), pl.BlockSpec((2, K, ns), lambda i, p: (p, 0, 0)), ORDER_ONLY],
        out_specs=pl.BlockSpec((tm, K), lambda i, j: (i + first, 0)),
        out_shape=_sds((M, K), F32), input_output_aliases={3: 0},
        compiler_params=_params(("parallel", "arbitrary"), big=True),
    )(dep, a, wg, out)


def _mm_tn_gathered(dep, h, a, a_spec, ns, name):
    M, K = h.shape

    def body(_, h_ref, a_ref, o_ref):
        o_ref[...] = _dot(h_ref[...], a_ref[...], TN).astype(BF16)

    return pl.pallas_call(
        body, grid=(N_DEV,), name=name,
        in_specs=[ORDER_ONLY, pl.BlockSpec((M, K), lambda j: (0, 0)), a_spec(M, ns)],
        out_specs=pl.BlockSpec((None, K, ns), lambda j: (_slot(j), 0, 0)),
        out_shape=_sds((N_DEV, K, ns), BF16),
        compiler_params=_params(("parallel",), big=True),
    )(dep, h, a)


def _bias_onehot(rbp, max_rel):
    r = lax.broadcasted_iota(jnp.int32, (rbp, TAB), 0)
    m = lax.broadcasted_iota(jnp.int32, (rbp, TAB), 1)
    dist = KPAD - jnp.where(m < WIN, m, m - TAB)
    return (r == jnp.clip(dist, -max_rel, max_rel) + max_rel).astype(F32)


def _attn_setup(i, hp, k_ref, v_ref, gv_ref, kpad, vpad, bias):
    ls = slice(i * ATTN_HEAD_DIM, (i + 1) * ATTN_HEAD_DIM)
    kpad[i][0:KPAD, :] = jnp.zeros((KPAD, ATTN_HEAD_DIM), BF16)
    vpad[i][0:KPAD, :] = jnp.zeros((KPAD, ATTN_HEAD_DIM), BF16)
    kpad[i][KPAD:, :] = k_ref[:, ls].astype(BF16)
    vpad[i][KPAD:, :] = v_ref[:, ls].astype(BF16)
    gvrow = gv_ref[pl.ds(hp * ATTN_HEADS_PER_STEP + i, 1), :]
    tab = pltpu.roll(jnp.broadcast_to(gvrow, (QBLK, TAB)), 0, 1, stride=1, stride_axis=0)
    row = lax.broadcasted_iota(jnp.int32, (QBLK, WIN), 0)
    col = lax.broadcasted_iota(jnp.int32, (QBLK, WIN), 1)
    first = jnp.bitwise_and(row, -CHUNK)
    seen = jnp.logical_and(col >= first, col < first + (N_PAST + 1) * CHUNK)
    bias[i][...] = jnp.where(seen, tab[:, 0:WIN], NEG)


def _attn_probs(b, q_ref, kpad, vpad, bias, col):
    pair = range(ATTN_HEADS_PER_STEP)
    ls = [slice(i * ATTN_HEAD_DIM, (i + 1) * ATTN_HEAD_DIM) for i in pair]
    r0 = pl.multiple_of(b * QBLK, QBLK)
    q = [q_ref[pl.ds(r0, QBLK), ls[i]].astype(BF16) for i in pair]
    kw = [kpad[i][pl.ds(r0, WIN), :] for i in pair]
    vw = [vpad[i][pl.ds(r0, WIN), :] for i in pair]
    s = [_dot(q[i], kw[i], NT) * (ATTN_HEAD_DIM ** -0.5) + bias[i][...] for i in pair]
    s = [jnp.where(col >= KPAD - r0, s[i], NEG) for i in pair]
    p = [jnp.exp(s[i] - jnp.max(s[i], axis=-1, keepdims=True)) for i in pair]
    pn = [p[i] / jnp.sum(p[i], axis=-1, keepdims=True) for i in pair]
    return r0, ls, q, kw, vw, pn


def _attn_fwd(proj, gv, ga, AW):
    T = proj.shape[0]
    AH = ATTN_HEADS_PER_STEP
    W = AH * ATTN_HEAD_DIM
    HP = AW // W

    def body(q_ref, k_ref, v_ref, gv_ref, ga_ref, o_ref, *scratch):
        kpad, vpad, bias = (scratch[k * AH:(k + 1) * AH] for k in range(3))
        hp = pl.program_id(0)
        for i in range(AH):
            _attn_setup(i, hp, k_ref, v_ref, gv_ref, kpad, vpad, bias)
        col = lax.broadcasted_iota(jnp.int32, (QBLK, WIN), 1)

        def block(b, carry):
            pair = range(AH)
            r0, ls, _, _, vw, pn = _attn_probs(b, q_ref, kpad, vpad, bias, col)
            o = [_dot(pn[i].astype(BF16), vw[i], NN) for i in pair]
            r = [lax.rsqrt(jnp.mean(o[i] * o[i], axis=-1, keepdims=True) + EPS) for i in pair]
            outs = [o[i] * r[i] * ga_ref[0:1, ls[i]] for i in pair]
            o_ref[pl.ds(r0, QBLK), :] = jnp.concatenate(outs, axis=1).astype(BF16)
            return carry

        lax.fori_loop(0, T // QBLK, block, 0)

    blk = lambda off: pl.BlockSpec((T, W), lambda hp: (0, off + hp))
    return pl.pallas_call(
        body, grid=(HP,), name="attn_fwd",
        in_specs=[blk(0), blk(HP), blk(2 * HP), pl.BlockSpec(gv.shape, lambda hp: (0, 0)),
                  pl.BlockSpec((1, W), lambda hp: (0, hp))],
        out_specs=pl.BlockSpec((T, W), lambda hp: (0, hp)),
        out_shape=_sds((T, AW), BF16),
        scratch_shapes=[pltpu.VMEM((T + KPAD, ATTN_HEAD_DIM), BF16)] * (2 * AH) + [pltpu.VMEM((QBLK, WIN), F32)] * AH,
        compiler_params=_params(("parallel",), big=True),
    )(proj, proj, proj, gv, ga)


def _attn_bwd(proj, dmixin, gv, ga, AW):
    T = proj.shape[0]
    AH = ATTN_HEADS_PER_STEP
    W = AH * ATTN_HEAD_DIM
    HP = AW // W
    scale = ATTN_HEAD_DIM ** -0.5

    def body(q_ref, k_ref, v_ref, dn_ref, gv_ref, ga_ref, dq_ref, dk_ref, dv_ref, dgv_ref, dga_ref, *scratch):
        kpad, vpad, dkacc, dvacc, bias, dbias = (scratch[k * AH:(k + 1) * AH] for k in range(6))
        hp = pl.program_id(0)
        for i in range(AH):
            _attn_setup(i, hp, k_ref, v_ref, gv_ref, kpad, vpad, bias)
            dkacc[i][...] = jnp.zeros_like(dkacc[i])
            dvacc[i][...] = jnp.zeros_like(dvacc[i])
            dbias[i][...] = jnp.zeros_like(dbias[i])
        dga_ref[...] = jnp.zeros_like(dga_ref)
        col = lax.broadcasted_iota(jnp.int32, (QBLK, WIN), 1)

        def block(b, carry):
            pair = range(AH)
            r0, lss, qs, kws, vws, pns = _attn_probs(b, q_ref, kpad, vpad, bias, col)
            pn_b = [pns[i].astype(BF16) for i in pair]
            o = [_dot(pn_b[i], vws[i], NN) for i in pair]
            r = [lax.rsqrt(jnp.mean(o[i] * o[i], axis=-1, keepdims=True) + EPS) for i in pair]
            dn = [dn_ref[pl.ds(r0, QBLK), lss[i]] for i in pair]
            for i in pair:
                dga_ref[i:i + 1, :] += _colsum(dn[i] * o[i] * r[i])
            a = [dn[i] * ga_ref[0:1, lss[i]] for i in pair]
            do_b = [(r[i] * (a[i] - o[i] * (r[i] * r[i]) * jnp.mean(a[i] * o[i], axis=-1, keepdims=True))).astype(BF16)
                    for i in pair]
            dp = [_dot(do_b[i], vws[i], NT) for i in pair]
            for i in pair:
                dvacc[i][pl.ds(r0, WIN), :] += _dot(pn_b[i], do_b[i], TN)
            ds = [pns[i] * (dp[i] - jnp.sum(pns[i] * dp[i], axis=-1, keepdims=True)) for i in pair]
            for i in pair:
                dbias[i][...] += ds[i]
            ds_b = [ds[i].astype(BF16) for i in pair]
            dq = [_dot(ds_b[i], kws[i], NN) * scale for i in pair]
            dq_ref[pl.ds(r0, QBLK), :] = jnp.concatenate(dq, axis=1).astype(BF16)
            for i in pair:
                dkacc[i][pl.ds(r0, WIN), :] += _dot(ds_b[i], qs[i], TN) * scale
            return carry

        lax.fori_loop(0, T // QBLK, block, 0)

        rr = lax.broadcasted_iota(jnp.int32, (QBLK, QBLK), 0)
        cc = lax.broadcasted_iota(jnp.int32, (QBLK, QBLK), 1)
        flip = (rr + cc == QBLK - 1).astype(BF16)
        for i in range(AH):
            ls = slice(i * ATTN_HEAD_DIM, (i + 1) * ATTN_HEAD_DIM)
            dk_ref[:, ls] = dkacc[i][KPAD:, :].astype(BF16)
            dv_ref[:, ls] = dvacc[i][KPAD:, :].astype(BF16)
            full = jnp.concatenate([dbias[i][...], jnp.zeros((QBLK, TAB - WIN), F32)], axis=1)
            hi = full.astype(BF16)
            lo = (full - hi.astype(F32)).astype(BF16)
            rev = _dot(flip, hi, NN) + _dot(flip, lo, NN)
            dgv_ref[i:i + 1, :] = _colsum(pltpu.roll(rev, TAB - (QBLK - 1), 1, stride=1, stride_axis=0))

    blk = lambda off: pl.BlockSpec((T, W), lambda hp: (0, off + hp))
    accs = lambda dt: [pltpu.VMEM((T + KPAD, ATTN_HEAD_DIM), dt)] * AH
    return pl.pallas_call(
        body, grid=(HP,), name="attn_bwd",
        in_specs=[blk(0), blk(HP), blk(2 * HP), blk(0), pl.BlockSpec(gv.shape, lambda hp: (0, 0)),
                  pl.BlockSpec((1, W), lambda hp: (0, hp))],
        out_specs=[blk(0), blk(0), blk(0), pl.BlockSpec((None, AH, TAB), lambda hp: (hp, 0, 0)),
                   pl.BlockSpec((None, AH, ATTN_HEAD_DIM), lambda hp: (hp, 0, 0))],
        out_shape=[_sds((T, AW), BF16), _sds((T, AW), BF16), _sds((T, AW), BF16),
                   _sds((HP, AH, TAB), F32), _sds((HP, AH, ATTN_HEAD_DIM), F32)],
        scratch_shapes=accs(BF16) + accs(BF16) + accs(F32) + accs(F32) + [pltpu.VMEM((QBLK, WIN), F32)] * (2 * AH),
        compiler_params=_params(("parallel",), big=True),
    )(proj, proj, proj, dmixin, gv, ga)


def _ltri():
    r = lax.broadcasted_iota(jnp.int32, (CHUNK, CHUNK), 0)
    c = lax.broadcasted_iota(jnp.int32, (CHUNK, CHUNK), 1)
    return (c <= r).astype(BF16)


def _tri_dot(tri, v, dims):
    hi = v.astype(BF16)
    lo = (v - hi.astype(F32)).astype(BF16)
    return _dot(tri, hi, dims) + _dot(tri, lo, dims)


HEADS_PER_STEP = 2


def _alternate(stages):
    live = list(stages)
    while live:
        for g in list(live):
            if next(g, StopIteration) is StopIteration:
                live.remove(g)


def _hgrn_gates(n, ls, q_ref, f_ref, lb_ref, ltri):
    r0 = pl.multiple_of(n * CHUNK, CHUNK)
    rows = pl.ds(r0, CHUNK)
    lb = lb_ref[:, ls]
    qb = q_ref[rows, ls]
    sg = _sigmoid(f_ref[rows, ls])
    f = lb + (1.0 - lb) * sg
    sq = _sigmoid(qb)
    b = _tri_dot(ltri, jnp.log(f), NN)
    return rows, lb, qb, sg, f, 1.0 - f, sq, qb * sq, b


def _hgrn_specs(T, RW, AW):
    HG = HEADS_PER_STEP
    W = HG * LANE
    base = 3 * AW // W
    blk_in = lambda off: pl.BlockSpec((T, W), lambda g: (0, base + off + g))
    col = pl.BlockSpec((T, W), lambda g: (0, g))
    return HG, W, RW // W, blk_in, col


def _hgrn_fwd(proj, lb, gn, AW, RW):
    T = proj.shape[0]
    RH, NC, NSUB = RW // LANE, T // CHUNK, CHUNK // SUB
    HG, W, NG, blk_in, col = _hgrn_specs(T, RW, AW)

    def body(q_ref, f_ref, i_ref, g_ref, lb_ref, gn_ref, mix_ref, o_ref, stall_ref, st_all, bs_all, kks_all, ics_all):
        st_all[...] = jnp.zeros_like(st_all)
        ltri = _ltri()
        rowi = lax.broadcasted_iota(jnp.int32, (SUB, 1), 0)

        def one_head(h, n):
            ls = slice(h * LANE, (h + 1) * LANE)
            st, bs, kks, ics = st_all.at[h], bs_all.at[h], kks_all.at[h], ics_all.at[h]
            rows, _, _, _, _, kk, _, qs, b = _hgrn_gates(n, ls, q_ref, f_ref, lb_ref, ltri)
            ic = i_ref[rows, ls]
            stv = st[...]
            stall_ref[h, n] = stv
            bs[...] = b
            kks[...] = kk
            ics[...] = ic
            yield
            o = _dot((qs * jnp.exp(b)).astype(BF16), stv.astype(BF16), NT)
            yield
            ic_b = ic.astype(BF16)
            pieces = []
            for blk in range(NSUB):
                s0 = blk * SUB
                bI, qI = b[s0:s0 + SUB], qs[s0:s0 + SUB]
                if blk == 0:
                    oI = jnp.zeros((SUB, LANE), F32)
                else:
                    ref = bs[s0 - 1:s0, :]
                    qt = (qI * jnp.exp(bI - ref)).astype(BF16)
                    kt = (kk[0:s0] * jnp.exp(ref - b[0:s0])).astype(BF16)
                    oI = _dot(_dot(qt, kt, NT).astype(BF16), ic_b[0:s0], NN)
                    yield
                acc = [oI[g * ROWS:(g + 1) * ROWS] for g in range(SUB // ROWS)]
                for s in range(SUB):
                    sr = s0 + s
                    g0 = s // ROWS
                    lo = g0 * ROWS
                    e = jnp.exp(jnp.minimum(bI[lo:] - bs[sr:sr + 1, :], 0.0))
                    a = jnp.sum(qI[lo:] * kks[sr:sr + 1, :] * e, axis=-1, keepdims=True)
                    add = jnp.where(rowi[lo:] >= s, a, 0.0) * ics[sr:sr + 1, :]
                    for g in range(g0, SUB // ROWS):
                        acc[g] = acc[g] + add[(g - g0) * ROWS:(g - g0 + 1) * ROWS]
                    yield
                pieces.extend(acc)
            o = o + jnp.concatenate(pieces, axis=0)
            bl = bs[CHUNK - 1:CHUNK, :]
            kd = (kk * jnp.exp(bl - b)).astype(BF16)
            st[...] = stv * jnp.exp(bl) + _dot(ic_b, kd, TN)
            yield
            o_ref[rows, ls] = o
            r = lax.rsqrt(jnp.mean(o * o, axis=-1, keepdims=True) + EPS)
            gb = g_ref[rows, ls]
            mix_ref[rows, ls] = (o * r * gn_ref[...] * (gb * _sigmoid(gb))).astype(BF16)

        def chunk(n, carry):
            _alternate([one_head(h, n) for h in range(HG)])
            return carry

        lax.fori_loop(0, NC, chunk, 0)

    tile = pltpu.VMEM((HG, CHUNK, LANE), F32)
    return pl.pallas_call(
        body, grid=(NG,), name="hgrn_fwd",
        in_specs=[blk_in(0), blk_in(NG), blk_in(2 * NG), blk_in(3 * NG), pl.BlockSpec((1, W), lambda g: (0, g)),
                  pl.BlockSpec((1, LANE), lambda g: (0, 0))],
        out_specs=[col, col, pl.BlockSpec((HG, NC, LANE, LANE), lambda g: (g, 0, 0, 0))],
        out_shape=[_sds((T, RW), BF16), _sds((T, RW), F32), _sds((RH, NC, LANE, LANE), F32)],
        scratch_shapes=[pltpu.VMEM((HG, LANE, LANE), F32), tile, tile, tile],
        compiler_params=_params(("parallel",), big=True),
    )(proj, proj, proj, proj, lb, gn)


def _hgrn_bwd(proj, dmixin, o_b, st_all, lb, gn, AW, RW):
    T = proj.shape[0]
    RH, NC, NSUB = RW // LANE, T // CHUNK, CHUNK // SUB
    HG, W, NG, blk_in, col = _hgrn_specs(T, RW, AW)

    def body(q_ref, f_ref, i_ref, g_ref, o_ref, dn_ref, stall_ref, lb_ref, gn_ref,
             dq_ref, df_ref, di_ref, dg_ref, dlb_ref, dgn_ref, dst_all, bs_all, qss_all, dos_all, p2_all, dic_all,
             p1_all):
        dst_all[...] = jnp.zeros_like(dst_all)
        dlb_ref[...] = jnp.zeros_like(dlb_ref)
        dgn_ref[...] = jnp.zeros_like(dgn_ref)
        ltri = _ltri()
        rowi = lax.broadcasted_iota(jnp.int32, (SUB, 1), 0)
        last = lax.broadcasted_iota(jnp.int32, (CHUNK, 1), 0) == CHUNK - 1

        def one_head(h, n):
            ls = slice(h * LANE, (h + 1) * LANE)
            dst, bs, qss, dos = dst_all.at[h], bs_all.at[h], qss_all.at[h], dos_all.at[h]
            p2, dic, p1s = p2_all.at[h], dic_all.at[h], p1_all.at[h]
            rows, lbv, qb, sg, f, kk, sq, qs, b = _hgrn_gates(n, ls, q_ref, f_ref, lb_ref, ltri)
            ic = i_ref[rows, ls]
            stv = stall_ref[h, n]
            dstv = dst[...]
            o = o_ref[rows, ls]
            dn = dn_ref[rows, ls]
            gb = g_ref[rows, ls]
            sgb = _sigmoid(gb)
            r = lax.rsqrt(jnp.mean(o * o, axis=-1, keepdims=True) + EPS)
            gnv = gn_ref[...]
            dg_ref[rows, ls] = (dn * (o * r * gnv) * (sgb * (1.0 + gb * (1.0 - sgb)))).astype(BF16)
            dy = dn * (gb * sgb)
            dgn_ref[h] += _colsum(dy * o * r)
            a_ = dy * gnv
            do = r * (a_ - o * (r * r) * jnp.mean(a_ * o, axis=-1, keepdims=True))
            do_b = do.astype(BF16)
            bs[...] = b
            qss[...] = qs
            dos[...] = do
            yield
            ic_b = ic.astype(BF16)
            eb = jnp.exp(b)
            bl = bs[CHUNK - 1:CHUNK, :]
            ebl = jnp.exp(bl)
            dec = jnp.exp(bl - b)
            kd = (kk * dec).astype(BF16)
            dst_b = dstv.astype(BF16)
            dqs = _dot(do_b, stv.astype(BF16), NN) * eb
            dkk2 = _dot(ic_b, dst_b, NN) * dec
            dic[...] = _dot(kd, dst_b, NT)
            dbl = ebl * _colsum(stv * dstv) + _colsum(kk * dkk2)
            dst[...] = dstv * ebl + _dot(do_b, (qs * eb).astype(BF16), TN)
            yield
            p2[...] = jnp.zeros_like(p2)
            p1_pieces = []
            for blk in range(NSUB):
                s0 = blk * SUB
                bI, qI, doI = b[s0:s0 + SUB], qs[s0:s0 + SUB], do[s0:s0 + SUB]
                if blk == 0:
                    p1 = jnp.zeros((SUB, LANE), F32)
                else:
                    ref = bs[s0 - 1:s0, :]
                    eq = jnp.exp(bI - ref)
                    ek = jnp.exp(ref - b[0:s0])
                    qt = (qI * eq).astype(BF16)
                    kt = (kk[0:s0] * ek).astype(BF16)
                    doI_b = doI.astype(BF16)
                    dic[0:s0, :] += _dot(_dot(qt, kt, NT).astype(BF16), doI_b, TN)
                    da = _dot(doI_b, ic_b[0:s0], NT).astype(BF16)
                    p1 = _dot(da, kt, NN) * eq
                    p2[0:s0, :] += _dot(da, qt, TN) * ek
                    yield
                p1_pieces.append(p1)
                kkI, icI = kk[s0:s0 + SUB], ic[s0:s0 + SUB]
                p2acc = [jnp.zeros((ROWS, LANE), F32) for _ in range(SUB // ROWS)]
                diacc = [jnp.zeros((ROWS, LANE), F32) for _ in range(SUB // ROWS)]
                for t in range(SUB):
                    tr = s0 + t
                    ng = t // ROWS + 1
                    hi = ng * ROWS
                    keep = rowi[:hi] <= t
                    do_t = dos[tr:tr + 1, :]
                    e = jnp.exp(jnp.minimum(bs[tr:tr + 1, :] - bI[:hi], 0.0))
                    qe = qss[tr:tr + 1, :] * e
                    a = jnp.where(keep, jnp.sum(kkI[:hi] * qe, axis=-1, keepdims=True), 0.0)
                    da = jnp.where(keep, jnp.sum(icI[:hi] * do_t, axis=-1, keepdims=True), 0.0)
                    dp2, ddi = da * qe, a * do_t
                    for g in range(ng):
                        p2acc[g] = p2acc[g] + dp2[g * ROWS:(g + 1) * ROWS]
                        diacc[g] = diacc[g] + ddi[g * ROWS:(g + 1) * ROWS]
                    p1s[tr:tr + 1, :] = _colsum(da * kkI[:hi] * e)
                    yield
                p2[s0:s0 + SUB, :] += jnp.concatenate(p2acc, axis=0)
                dic[s0:s0 + SUB, :] += jnp.concatenate(diacc, axis=0)
            dqs = dqs + jnp.concatenate(p1_pieces, axis=0) + p1s[...]
            dkk = dkk2 + p2[...]
            db = qs * dqs - kk * dkk + jnp.where(last, dbl, 0.0)
            dgl = _tri_dot(ltri, db, TN)
            yield
            dfv = dgl / f - dkk
            df_ref[rows, ls] = (dfv * (1.0 - lbv) * sg * (1.0 - sg)).astype(BF16)
            dlb_ref[:, ls] += _colsum(dfv * (1.0 - sg))
            dq_ref[rows, ls] = (dqs * (sq * (1.0 + qb * (1.0 - sq)))).astype(BF16)
            di_ref[rows, ls] = dic[...].astype(BF16)

        def chunk(k, carry):
            _alternate([one_head(h, NC - 1 - k) for h in range(HG)])
            return carry

        lax.fori_loop(0, NC, chunk, 0)

    tile = pltpu.VMEM((HG, CHUNK, LANE), F32)
    return pl.pallas_call(
        body, grid=(NG,), name="hgrn_bwd",
        in_specs=[blk_in(0), blk_in(NG), blk_in(2 * NG), blk_in(3 * NG), col,
                  pl.BlockSpec((T, W), lambda g: (0, AW // W + g)),
                  pl.BlockSpec((HG, NC, LANE, LANE), lambda g: (g, 0, 0, 0)),
                  pl.BlockSpec((1, W), lambda g: (0, g)), pl.BlockSpec((1, LANE), lambda g: (0, 0))],
        out_specs=[col, col, col, col, pl.BlockSpec((1, W), lambda g: (0, g)),
                   pl.BlockSpec((HG, 1, LANE), lambda g: (g, 0, 0))],
        out_shape=[_sds((T, RW), BF16)] * 4 + [_sds((1, RW), F32), _sds((RH, 1, LANE), F32)],
        scratch_shapes=[pltpu.VMEM((HG, LANE, LANE), F32), tile, tile, tile, tile, tile, tile],
        compiler_params=_params(("parallel",), big=True),
    )(proj, proj, proj, proj, o_b, dmixin, st_all, lb, gn)


def _prep(c, lb_logits, rb_pad, max_rel, after):
    D, RW = c.shape[-1], lb_logits.shape[-1]
    H, rbp = rb_pad.shape

    def body(c_ref, l_ref, rb_ref, _, __, cact_ref, lb_ref, gv_ref):
        cv = c_ref[...]
        cact_ref[...] = cv * _sigmoid(cv)
        lb_ref[...] = _sigmoid(l_ref[0:1, :] - l_ref[1:2, :])
        gv_ref[...] = _dot(rb_ref[...], _bias_onehot(rbp, max_rel), NN, HIGHEST)

    vmem = pl.BlockSpec(memory_space=pltpu.VMEM)
    return pl.pallas_call(
        body, name="prep", in_specs=[vmem, vmem, vmem, ORDER_ONLY, ORDER_ONLY],
        out_shape=[_sds((1, D), F32), _sds((1, RW), F32), _sds((H, TAB), F32)],
    )(c, lb_logits, rb_pad, *after)


def _mod_part(c_all, w_ada_s, b_ada_s):
    B, D = c_all.shape
    ns = w_ada_s.shape[1]
    tn = _tile(ns, 768, LANE)

    def body(c_ref, w_ref, b_ref, o_ref):
        o_ref[...] = _dot(c_ref[...], w_ref[...], NN) + b_ref[...]

    return pl.pallas_call(
        body, grid=(ns // tn,), name="mod_part",
        in_specs=[pl.BlockSpec((B, D), lambda j: (0, 0)), pl.BlockSpec((D, tn), lambda j: (0, j)),
                  pl.BlockSpec((1, tn), lambda j: (0, j))],
        out_specs=pl.BlockSpec((B, tn), lambda j: (0, j)),
        out_shape=_sds((B, ns), F32), compiler_params=_params(("parallel",)),
    )(c_all, w_ada_s, b_ada_s)


def _adam(w, g, m, v):
    m = ADAM_B1 * m + (1.0 - ADAM_B1) * g
    v = ADAM_B2 * v + (1.0 - ADAM_B2) * (g * g)
    m_hat = m * (1.0 / (1.0 - ADAM_B1 ** ADAM_STEP))
    v_hat = v * (1.0 / (1.0 - ADAM_B2 ** ADAM_STEP))
    return -ADAM_LR * (m_hat / (jnp.sqrt(v_hat) + ADAM_EPS) + ADAM_WD * w), m, v


def _adam_ada(c_all, dmod_s, w, m, v):
    B, D = c_all.shape
    ns = w.shape[1]
    tr, tn = _tile(D, 512, LANE), _tile(ns, 768, LANE)

    def body(c_ref, d_ref, w_ref, m_ref, v_ref, g_out, dw_out, m_out, v_out):
        g = _dot(c_ref[...], d_ref[...], TN)
        g_out[...] = g
        dw_out[...], m_out[...], v_out[...] = _adam(w_ref[...], g, m_ref[...], v_ref[...])

    big = pl.BlockSpec((tr, tn), lambda i, j: (i, j))
    return pl.pallas_call(
        body, grid=(D // tr, ns // tn), name="adam_w_ada",
        in_specs=[pl.BlockSpec((B, tr), lambda i, j: (0, i)), pl.BlockSpec((B, tn), lambda i, j: (0, j)),
                  big, big, big],
        out_specs=[big] * 4, out_shape=[_sds((D, ns), F32)] * 4,
        compiler_params=_params(("parallel", "parallel")),
    )(c_all, dmod_s, w, m, v)


def _adam_shard(parts, w, m, v, name):
    R, C = w.shape
    tr = _tile(R, 256, 16)

    def body(p_ref, w_ref, m_ref, v_ref, g_out, dw_out, m_out, v_out):
        g = p_ref[0].astype(F32)
        for k in range(1, N_DEV // 2):
            g = g + p_ref[k].astype(F32)
        g_out[...] = g
        dw_out[...], m_out[...], v_out[...] = _adam(w_ref[...], g, m_ref[...], v_ref[...])

    big = pl.BlockSpec((tr, C), lambda i: (i, 0))
    return pl.pallas_call(
        body, grid=(R // tr,), name=name,
        in_specs=[pl.BlockSpec((N_DEV // 2, tr, C), lambda i: (0, i, 0)), big, big, big],
        out_specs=[big] * 4, out_shape=[_sds((R, C), F32)] * 4,
        compiler_params=_params(("parallel",), big=True),
    )(parts, w, m, v)


def _pair_sum(g8, land, core, name):
    _, NCHIP, R, C = g8.shape
    tr = _tile(R, 1024, 16)

    def body(core_ref, g_ref, l_ref, o_ref):
        o_ref[...] = g_ref[...] + l_ref[...]

    return pl.pallas_call(
        body, name=name,
        grid_spec=pltpu.PrefetchScalarGridSpec(
            num_scalar_prefetch=1, grid=(NCHIP, R // tr),
            in_specs=[pl.BlockSpec((None, None, tr, C), lambda k, i, core_ref: (core_ref[0], k, i, 0)),
                      pl.BlockSpec((None, tr, C), lambda k, i, core_ref: (k, i, 0))],
            out_specs=pl.BlockSpec((None, tr, C), lambda k, i, core_ref: (k, i, 0))),
        out_shape=_sds((NCHIP, R, C), BF16), compiler_params=_params(("parallel", "parallel")),
    )(core, g8, land)


SMALL = ("b_ada", "rel_bias", "attn_norm_g", "lb_logits", "gnorm_g", "ln1_g", "ln1_b", "ln2_g", "ln2_b")


def _small_update(parts, loss_parts, lbv, ws, ms, vs, max_rel):
    n = len(SMALL)

    def body(*refs):
        part_refs = dict(zip(SMALL, refs[:n]))
        loss_in, lb_ref = refs[n], refs[n + 1]
        w_refs, m_refs, v_refs = refs[n + 2:2 * n + 2], refs[2 * n + 2:3 * n + 2], refs[3 * n + 2:4 * n + 2]
        outs = refs[4 * n + 2:]

        def total(ref):
            tot = ref[0]
            for k in range(1, N_DEV):
                tot = tot + ref[k]
            return tot

        outs[0][...] = jnp.sum(total(loss_in), axis=-1, keepdims=True)
        for idx, name in enumerate(SMALL):
            g = total(part_refs[name])
            if name == "rel_bias":
                g = _dot(g, _bias_onehot(w_refs[idx].shape[1], max_rel), NT, HIGHEST)
            elif name == "lb_logits":
                lb = lb_ref[...]
                sign = (1 - 2 * lax.broadcasted_iota(jnp.int32, (2, 1), 0)).astype(F32)
                g = sign * (g * lb * (1.0 - lb))
            elif name == "gnorm_g":
                g = _colsum(g)
            dw, mm, vv = _adam(w_refs[idx][...], g, m_refs[idx][...], v_refs[idx][...])
            outs[1 + 4 * idx][...] = g
            outs[2 + 4 * idx][...] = dw
            outs[3 + 4 * idx][...] = mm
            outs[4 + 4 * idx][...] = vv

    out_shape = [_sds((1, 1), F32)]
    for w in ws:
        out_shape += [_sds(w.shape, F32)] * 4
    return pl.pallas_call(body, name="small_update", out_shape=out_shape, compiler_params=_params(big=True))(
        *[parts[k] for k in SMALL], loss_parts, lbv, *ws, *ms, *vs)


def _place():
    x, y, c = lax.axis_index("x"), lax.axis_index("y"), lax.axis_index("c")
    return x, y, c, [(1 - x, y), (x, 1 - y), (1 - x, 1 - y)]


def _all_gather(shard, name):
    HBM = pl.BlockSpec(memory_space=pl.ANY)

    def body(x_ref, out_ref, send_sems, recv_sems, local_sem):
        x, y, c, chips = _place()
        me, sibling = (x, y, c), (x, y, 1 - c)

        def slot(px, py, pc):
            return out_ref.at[4 * px + 2 * py + pc]

        def copy(k, block, to, src=None):
            return pltpu.make_async_remote_copy(
                src_ref=slot(*block) if src is None else src, dst_ref=slot(*block),
                send_sem=send_sems.at[k], recv_sem=recv_sems.at[k], device_id=to, device_id_type=MESH)

        mine = pltpu.make_async_copy(x_ref, slot(*me), local_sem)
        mine.start()
        first = [copy(0, me, sibling, src=x_ref)]
        first += [copy(1 + j, me, (*chip, c), src=x_ref) for j, chip in enumerate(chips)]
        for cp in first:
            cp.start()
        passed = [copy(4 + j, (*chip, c), sibling) for j, chip in enumerate(chips)]
        for j, chip in enumerate(chips):
            copy(1 + j, (*chip, c), me).wait_recv()
            passed[j].start()
        copy(0, sibling, me).wait_recv()
        for j, chip in enumerate(chips):
            copy(4 + j, (*chip, 1 - c), me).wait_recv()
        for cp in first + passed:
            cp.wait_send()
        mine.wait()

    return pl.pallas_call(
        body, name=name, out_shape=_sds((N_DEV,) + shard.shape, shard.dtype),
        in_specs=[HBM], out_specs=HBM,
        scratch_shapes=[pltpu.SemaphoreType.DMA((7,)), pltpu.SemaphoreType.DMA((7,)), pltpu.SemaphoreType.DMA(())],
    )(shard)


SEM_SPEC = pl.BlockSpec(memory_space=pltpu.SEMAPHORE)
HBM_SPEC = pl.BlockSpec(memory_space=pltpu.HBM)
EFFECT = pltpu.SideEffectType.DATAFLOW_SIDE_EFFECTING


def _remote(src, dst, send_sems, recv_sems, k, dev):
    return pltpu.make_async_remote_copy(src_ref=src, dst_ref=dst, send_sem=send_sems.at[k], recv_sem=recv_sems.at[k],
                                        device_id=dev, device_id_type=MESH)


def _copy_start(name, bufs, plan, n, after, only=None):
    nb = len(bufs)

    def body(*refs):
        send_sems, recv_sems = refs[nb + 1], refs[nb + 2]
        for k, (src, dst, dev) in enumerate(plan(*refs[:nb])):
            if only is not None and k not in only:
                continue
            _remote(src, dst, send_sems, recv_sems, k, dev).start()
        refs[-1][...] = jnp.zeros_like(refs[-1])

    out = pl.pallas_call(
        body, name=name,
        out_shape=(pltpu.SemaphoreType.DMA((n,)), pltpu.SemaphoreType.DMA((n,)),
                   *[pltpu.HBM(b.shape, b.dtype) for b in bufs], _sds((8, LANE), F32)),
        in_specs=[HBM_SPEC] * nb + [ORDER_ONLY],
        out_specs=(SEM_SPEC, SEM_SPEC, *[HBM_SPEC] * nb, pl.BlockSpec(memory_space=pltpu.VMEM)),
        input_output_aliases={i: 2 + i for i in range(nb)},
        compiler_params=pltpu.CompilerParams(has_side_effects=EFFECT),
    )(*[pltpu.with_memory_space_constraint(b, pltpu.HBM) for b in bufs], after)
    return (out[0], out[1]), list(out[2:2 + nb]), out[-1]


def _copy_wait(name, sems, bufs, plan, after, only=None):
    nb = len(bufs)

    def body(*refs):
        send_sems, recv_sems = refs[nb], refs[nb + 1]
        for k, (src, dst, dev) in enumerate(plan(*refs[:nb])):
            if only is not None and k not in only:
                continue
            cp = _remote(src, dst, send_sems, recv_sems, k, dev)
            cp.wait_send()
            cp.wait_recv()

    out = pl.pallas_call(
        body, name=name, out_shape=tuple(pltpu.HBM(b.shape, b.dtype) for b in bufs),
        in_specs=[HBM_SPEC] * nb + [SEM_SPEC, SEM_SPEC, pl.BlockSpec(memory_space=pl.ANY)],
        out_specs=tuple([HBM_SPEC] * nb), input_output_aliases={i: i for i in range(nb)},
        compiler_params=pltpu.CompilerParams(has_side_effects=EFFECT),
    )(*bufs, sems[0], sems[1], after)
    return list(out)


def _ag_plan_chips(shard_ref, out_ref):
    x, y, c, chips = _place()
    mine = out_ref.at[4 * x + 2 * y + c]
    return [(shard_ref, mine, (x, y, 1 - c))] + [(shard_ref, mine, (*chip, c)) for chip in chips]


def _ag_plan_pass(out_ref):
    x, y, c, chips = _place()
    slots = [out_ref.at[4 * chip[0] + 2 * chip[1] + c] for chip in chips]
    return [(s, s, (x, y, 1 - c)) for s in slots]


def _rs_plan_pair(g_ref, land_ref):
    x, y, c, _ = _place()
    return [(g_ref.at[1 - c], land_ref, (x, y, 1 - c))]


def _rs_plan_chips(p_ref, land_ref):
    x, y, c, chips = _place()
    return [(p_ref.at[2 * chip[0] + chip[1]], land_ref.at[2 * x + y], (*chip, c)) for chip in chips]


class _Gather:
    @staticmethod
    def landing(shard, me):
        return lax.dynamic_update_slice(lax.empty((N_DEV,) + shard.shape, shard.dtype), shard[None],
                                        (me,) + (0,) * shard.ndim)

    def __init__(self, shard, out, tag, after):
        self.tag = tag
        self.sems, (self.shard, self.out), self.token = _copy_start(
            "ag_start_" + tag, [shard, out], _ag_plan_chips, 4, after)
        self.groups = []

    def arrived(self, after, copies):
        name = "ag_wait_%s_%s" % (self.tag, "".join(map(str, copies)))
        self.shard, self.out = _copy_wait(name, self.sems, [self.shard, self.out], _ag_plan_chips, after, copies)
        return self.out

    def pass_on(self, after, blocks):
        name = "ag_pass_%s_%s" % (self.tag, "".join(map(str, blocks)))
        sems, (self.out,), _ = _copy_start(name, [self.out], _ag_plan_pass, 3, after, blocks)
        self.groups.append((sems, blocks))
        return self.out

    def passed(self, after, group):
        sems, blocks = self.groups[group]
        name = "ag_pass_wait_%s_%s" % (self.tag, "".join(map(str, blocks)))
        self.out = _copy_wait(name, sems, [self.out], _ag_plan_pass, after, blocks)[0]
        return self.out

    def arrived_from_chips(self, after):
        self.arrived(after, (0, 1, 2, 3))
        return self.pass_on(after, (0, 1, 2))

    def passed_on(self, after):
        return self.passed(after, 0)


def _ag_plan_direct(src_ref, out_ref):
    x, y, c, chips = _place()
    mine = out_ref.at[4 * x + 2 * y + c]
    peers = [(x, y, 1 - c)] + [(*chip, pc) for chip in chips for pc in (c, 1 - c)]
    return [(src_ref, mine, peer) for peer in peers]


class _SmallGather:
    def __init__(self, block, me, tag):
        self.tag = tag
        out = lax.dynamic_update_slice(lax.empty((N_DEV,) + block.shape, block.dtype), block[None],
                                       (me,) + (0,) * block.ndim)
        self.sems, self.bufs, self.token = _copy_start(
            "ag_direct_start_" + tag, [block, out], _ag_plan_direct, N_DEV - 1, jnp.zeros((1,), F32))

    def done(self, after):
        return _copy_wait("ag_direct_wait_" + self.tag, self.sems, self.bufs, _ag_plan_direct, after)[1]


class _ReduceScatter:
    def __init__(self, g8, tag):
        self.tag = tag
        land = lax.empty(g8.shape[1:], g8.dtype)
        self.sems, self.bufs, self.token = _copy_start(
            "rs_pair_start_" + tag, [g8, land], _rs_plan_pair, 1, jnp.zeros((1,), F32))

    def pair_done(self, core, chip, after):
        g8, land = _copy_wait("rs_pair_wait_" + self.tag, self.sems, self.bufs, _rs_plan_pair, after)
        p4 = _pair_sum(g8, land, core, "rs_pair_sum_" + self.tag)
        own = lax.dynamic_slice_in_dim(p4, chip, 1, axis=0)
        land2 = lax.dynamic_update_slice(lax.empty(p4.shape, p4.dtype), own, (chip, 0, 0))
        self.sems, self.bufs, self.token = _copy_start(
            "rs_chips_start_" + self.tag, [p4, land2], _rs_plan_chips, 3, jnp.zeros((1,), F32))

    def sums(self, after):
        return _copy_wait("rs_chips_wait_" + self.tag, self.sems, self.bufs, _rs_plan_chips, after)[1]


ORDER = ("w_ada", "b_ada", "w_in", "rel_bias", "attn_norm_g", "lb_logits", "gnorm_g", "w_o", "ln1_g", "ln1_b",
         "w_ffn_in", "w_ffn_out", "ln2_g", "ln2_b")


def kernel(x, c, w_ada, b_ada, w_in, rel_bias, attn_norm_g, lb_logits, gnorm_g, w_o, ln1_g, ln1_b, w_ffn_in, w_ffn_out, ln2_g, ln2_b, loss_target, m_w_ada, m_b_ada, m_w_in, m_rel_bias, m_attn_norm_g, m_lb_logits, m_gnorm_g, m_w_o, m_ln1_g, m_ln1_b, m_w_ffn_in, m_w_ffn_out, m_ln2_g, m_ln2_b, v_w_ada, v_b_ada, v_w_in, v_rel_bias, v_attn_norm_g, v_lb_logits, v_gnorm_g, v_w_o, v_ln1_g, v_ln1_b, v_w_ffn_in, v_w_ffn_out, v_ln2_g, v_ln2_b):
    W = dict(w_ada=w_ada, b_ada=b_ada, w_in=w_in, rel_bias=rel_bias, attn_norm_g=attn_norm_g, lb_logits=lb_logits,
             gnorm_g=gnorm_g, w_o=w_o, ln1_g=ln1_g, ln1_b=ln1_b, w_ffn_in=w_ffn_in, w_ffn_out=w_ffn_out,
             ln2_g=ln2_g, ln2_b=ln2_b)
    M = dict(w_ada=m_w_ada, b_ada=m_b_ada, w_in=m_w_in, rel_bias=m_rel_bias, attn_norm_g=m_attn_norm_g,
             lb_logits=m_lb_logits, gnorm_g=m_gnorm_g, w_o=m_w_o, ln1_g=m_ln1_g, ln1_b=m_ln1_b,
             w_ffn_in=m_w_ffn_in, w_ffn_out=m_w_ffn_out, ln2_g=m_ln2_g, ln2_b=m_ln2_b)
    V = dict(w_ada=v_w_ada, b_ada=v_b_ada, w_in=v_w_in, rel_bias=v_rel_bias, attn_norm_g=v_attn_norm_g,
             lb_logits=v_lb_logits, gnorm_g=v_gnorm_g, w_o=v_w_o, ln1_g=v_ln1_g, ln1_b=v_ln1_b,
             w_ffn_in=v_w_ffn_in, w_ffn_out=v_w_ffn_out, ln2_g=v_ln2_g, ln2_b=v_ln2_b)

    x2, tgt = x[0], loss_target[0]
    T, D = x2.shape
    AW, RW = attn_norm_g.shape[-1], lb_logits.shape[-1]
    MIX = AW + RW
    H, RH = AW // ATTN_HEAD_DIM, RW // LANE
    RB = rel_bias.shape[-1]
    max_rel = (RB - 1) // 2
    rbp = -(-RB // LANE) * LANE
    F = w_ffn_out.shape[1] * N_DEV
    half = N_DEV // 2
    xi, yi, ci = lax.axis_index("x"), lax.axis_index("y"), lax.axis_index("c")
    me = 4 * xi + 2 * yi + ci
    core = jnp.reshape(ci, (1,)).astype(jnp.int32)
    pad_rb = lambda a: jnp.pad(a[0], ((0, 0), (0, rbp - RB)))

    chip = 2 * xi + yi

    w_in_b = w_in[0].astype(BF16)
    w_in_land = _Gather.landing(w_in_b, me)
    c_act, lbv, gv = _prep(c, lb_logits, pad_rb(rel_bias), max_rel, (w_in_b, w_in_land))
    c_all = _all_gather(c_act, "ag_c").reshape(N_DEV, D)
    ns_ada = w_ada.shape[-1]
    mod_part = _mod_part(c_all, w_ada[0], lax.dynamic_slice_in_dim(b_ada, me * ns_ada, ns_ada, axis=1))
    mod_all = _all_gather(mod_part, "ag_mod")
    mod6 = lax.dynamic_index_in_dim(mod_all, me, axis=1, keepdims=False).reshape(6, D)

    bf = lambda w: w[0].astype(BF16)
    ag_in = _Gather(w_in_b, w_in_land, "w_in", mod_all)
    ag_o = _Gather(bf(w_o), _Gather.landing(bf(w_o), me), "w_o", ag_in.token)
    ag_f1 = _Gather(bf(w_ffn_in), _Gather.landing(bf(w_ffn_in), me), "w_ffn_in", ag_o.token)
    ag_f2 = _Gather(bf(w_ffn_out), _Gather.landing(bf(w_ffn_out), me), "w_ffn_out", ag_f1.token)

    h1 = _ln_mod(x2, mod6 + ag_f2.token[0, 0])
    ids = lambda pairs: jnp.stack([4 * px + 2 * py + pc for px, py, pc in pairs]).astype(jnp.int32)
    others = [(1 - xi, yi), (xi, 1 - yi), (1 - xi, 1 - yi)]
    proj = lax.empty((T, w_in.shape[-1] * N_DEV), F32)
    proj = _mm_gathered(h1, ag_in.arrived(h1, (0,)), ids([(xi, yi, ci), (xi, yi, 1 - ci)]), proj, "in_proj_a")
    ag_in.arrived(proj, (1, 2, 3))
    proj = _mm_gathered(h1, ag_in.pass_on(proj, (0, 1, 2)), ids([(*ch, ci) for ch in others]), proj, "in_proj_b")
    wg_in = ag_in.passed(proj, 0)
    proj = _mm_gathered(h1, wg_in, ids([(*ch, 1 - ci) for ch in others]), proj, "in_proj_c")
    ag_o.arrived_from_chips(proj)
    mix_a = _attn_fwd(proj, gv, attn_norm_g, AW)
    wg_o = ag_o.passed_on(mix_a).reshape(MIX, D)
    mix_b, o_b, st_all = _hgrn_fwd(proj, lbv, gnorm_g, AW, RW)
    mixin = jnp.concatenate([mix_a, mix_b], axis=1)
    mix = _mm_nn(mixin, wg_o, "out_proj")
    ag_f1.arrived_from_chips(mix)
    x1, h2 = _mid_fwd(x2, mix, mod6, ln1_g, ln1_b)
    wg_f1 = ag_f1.passed_on(h2)
    gu, act = _mm_swiglu(h2, wg_f1)
    ag_f2.arrived_from_chips(act)
    wg_f2 = ag_f2.passed_on(act).reshape(F, D)
    ff = _mm_nn(act, wg_f2, "ffn_out")
    dff, dx1a, vec_a = _final(x1, ff, mod6, ln2_g, ln2_b, tgt)

    du = _mm_swiglu_bwd(dff, wg_f2, gu)
    rs_f2 = _ReduceScatter(_mm_tn_rows(dff, act, dff, F // N_DEV, "grad_w_ffn_out"), "w_ffn_out")
    tm = _tile(T, 512, 16)
    du_ij = lambda tm_, w, first: pl.BlockSpec((None, tm_, w), lambda i, p: (p // (half // 2), i + first, p % (half // 2)))
    du_j = lambda rows, ns: pl.BlockSpec((None, rows, ns), lambda j: (j // half, 0, j % half))
    dh2 = _mm_gathered_nt(rs_f2.token, du, du_ij, wg_f1, T, tm, "ffn_in_bwd")
    rs_f2.pair_done(core, chip, dh2)
    gw_f1 = _mm_tn_gathered(rs_f2.token, h2, du, du_j, wg_f1.shape[-1], "grad_w_ffn_in")
    rs_f1 = _ReduceScatter(gw_f1.reshape(2, half, D, -1), "w_ffn_in")
    dmix, dxa, vec_b = _mid_bwd(x2, mix, x1, dx1a, dh2, mod6 + rs_f1.token[0, 0], ln1_g)
    dmixin = _mm_nt(dmix, wg_o, "out_proj_bwd")
    rs_f1.pair_done(core, chip, dmixin)
    rs_o = _ReduceScatter(_mm_tn_rows(rs_f1.token, mixin, dmix, MIX // N_DEV, "grad_w_o"), "w_o")
    dq, dk, dv, dgv, dga = _attn_bwd(proj, dmixin, gv + rs_o.token[0, 0], attn_norm_g, AW)
    rs_o.pair_done(core, chip, dq)
    dqb, dfl, dib, dgb, dlb, dgn = _hgrn_bwd(proj, dmixin, o_b, st_all, lbv + rs_o.token[0, 0], gnorm_g, AW, RW)
    dproj = jnp.concatenate([dq, dk, dv, dqb, dfl, dib, dgb], axis=1)
    p_ij = lambda tm_, w, first: pl.BlockSpec((tm_, w), lambda i, p: (i + first, p))
    p_j = lambda rows, ns: pl.BlockSpec((rows, ns), lambda j: (0, j))
    gw_in = _mm_tn_gathered(rs_o.token, h1, dproj, p_j, wg_in.shape[-1], "grad_w_in")
    rs_in = _ReduceScatter(gw_in.reshape(2, half, D, -1), "w_in")
    n_tiles = T // tm
    dh1 = _mm_gathered_nt(rs_in.token, dproj, p_ij, wg_in, T, tm, "in_proj_bwd_a", 0, n_tiles // 2)
    rs_in.pair_done(core, chip, dh1)
    dh1 = _mm_gathered_nt(rs_in.token, dproj, p_ij, wg_in, T, tm, "in_proj_bwd_b", n_tiles // 2,
                          n_tiles - n_tiles // 2, dh1)
    grad_x, vec_c = _first_bwd(x2, dh1, dxa, mod6)

    dmod = jnp.concatenate([vec_c[1:2], vec_c[0:1], vec_b[4:5], vec_b[1:2], vec_b[0:1], vec_a[2:3]], axis=0)
    pieces = dict(b_ada=dmod, rel_bias=dgv, attn_norm_g=dga, lb_logits=dlb, gnorm_g=dgn, ln1_g=vec_b[2:3],
                  ln1_b=vec_b[3:4], ln2_g=vec_a[0:1], ln2_b=vec_a[1:2], loss=vec_a[3:4])
    widths = dict(b_ada=(1, 6 * D), rel_bias=(H, TAB), attn_norm_g=(1, AW), lb_logits=(1, RW), gnorm_g=(RH, LANE),
                  ln1_g=(1, D), ln1_b=(1, D), ln2_g=(1, D), ln2_b=(1, D), loss=(1, D))
    packed = jnp.concatenate([pieces[k].reshape(-1, LANE) for k in widths], axis=0)
    small_ag = _SmallGather(packed, me, "small")
    after, res_big = small_ag.token, {}
    for k, rs in (("w_ffn_out", rs_f2), ("w_ffn_in", rs_f1), ("w_o", rs_o), ("w_in", rs_in)):
        four = _adam_shard(rs.sums(after), W[k][0], M[k][0], V[k][0], "adam_" + k)
        res_big[k] = [a[None] for a in four]
        after = four[0]
    gathered = small_ag.done(after)
    parts, r0 = {}, 0
    for k, (rows, width) in widths.items():
        nr = rows * width // LANE
        parts[k] = gathered[:, r0:r0 + nr, :].reshape(N_DEV, rows, width)
        r0 += nr
    prep_small = lambda d, k: pad_rb(d[k]) if k == "rel_bias" else d[k]
    small = _small_update(parts, parts["loss"], lbv, [prep_small(W, k) for k in SMALL],
                          [prep_small(M, k) for k in SMALL], [prep_small(V, k) for k in SMALL], max_rel)
    loss = small[0].reshape(())
    res = {}
    for idx, k in enumerate(SMALL):
        four = small[1 + 4 * idx:5 + 4 * idx]
        if k == "rel_bias":
            four = [a[:, :RB][None] for a in four]
        res[k] = list(four)

    res.update(res_big)
    dmod_s = lax.dynamic_slice_in_dim(parts["b_ada"].reshape(N_DEV, 6 * D), me * ns_ada, ns_ada, axis=1)
    res["w_ada"] = [a[None] for a in _adam_ada(c_all, dmod_s, w_ada[0], m_w_ada[0], v_w_ada[0])]

    out = [loss, grad_x[None]]
    for field in range(4):
        out += [res[k][field] for k in ORDER]
    return tuple(out)
```

```python
import jax
import jax.numpy as jnp
from jax import lax
from jax.experimental import pallas as pl
from jax.experimental.pallas import tpu as pltpu

F32 = jnp.float32
BF16 = jnp.bfloat16
MESH = pl.DeviceIdType.MESH
HIGHEST = lax.Precision.HIGHEST

N_DEV = 8
CHUNK = 64
N_PAST = 8
QBLK = 4 * CHUNK
KPAD = N_PAST * CHUNK
WIN = KPAD + QBLK
TAB = 1024
ATTN_HEAD_DIM = 64
ATTN_HEADS_PER_STEP = 4
SUB = 32
ROWS = 8
LANE = 128
EPS = 1e-5
ALPHA = 2.0 ** 0.25
ADAM_LR, ADAM_B1, ADAM_B2, ADAM_EPS, ADAM_WD, ADAM_STEP = 0.001, 0.9, 0.999, 1e-08, 0.01, 10
NEG = -1e30
ROW_TILE = 128
VMEM_LIMIT = 56 * 1024 * 1024


def _sds(shape, dtype):
    return jax.ShapeDtypeStruct(tuple(shape), dtype)


def _tile(n, pref, mult):
    best = None
    for t in range(mult, min(n, pref) + 1, mult):
        if n % t == 0:
            best = t
    return n if best is None else best


def _params(sem=None, big=False):
    kw = {}
    if sem is not None:
        kw["dimension_semantics"] = sem
    if big:
        kw["vmem_limit_bytes"] = VMEM_LIMIT
    return pltpu.CompilerParams(**kw)


def _sigmoid(v):
    return 1.0 / (1.0 + jnp.exp(-v))


def _dot(a, b, dims, precision=None):
    return lax.dot_general(a, b, (dims, ((), ())), preferred_element_type=F32, precision=precision)


NN = ((1,), (0,))
NT = ((1,), (1,))
TN = ((0,), (0,))


def _ln(v):
    mu = jnp.mean(v, axis=-1, keepdims=True)
    d = v - mu
    rstd = lax.rsqrt(jnp.mean(d * d, axis=-1, keepdims=True) + EPS)
    return d * rstd, rstd


def _ln_bwd(dxh, xh, rstd):
    return rstd * (dxh - jnp.mean(dxh, axis=-1, keepdims=True) - xh * jnp.mean(dxh * xh, axis=-1, keepdims=True))


def _colsum(v):
    return jnp.sum(v, axis=0, keepdims=True)


def _ln_mod(x2, mod6):
    T, D = x2.shape
    tm = _tile(T, ROW_TILE, 8)

    def body(x_ref, mod_ref, o_ref):
        xh, _ = _ln(x_ref[...])
        o_ref[...] = (xh * (1.0 + mod_ref[1:2, :]) + mod_ref[0:1, :]).astype(BF16)

    return pl.pallas_call(
        body, grid=(T // tm,), name="ln_mod",
        in_specs=[pl.BlockSpec((tm, D), lambda i: (i, 0)), pl.BlockSpec((6, D), lambda i: (0, 0))],
        out_specs=pl.BlockSpec((tm, D), lambda i: (i, 0)),
        out_shape=_sds((T, D), BF16), compiler_params=_params(("parallel",)),
    )(x2, mod6)


def _mid_fwd(x2, mix, mod6, ln1_g, ln1_b):
    T, D = x2.shape
    tm = _tile(T, ROW_TILE, 8)

    def body(x_ref, mix_ref, mod_ref, g_ref, b_ref, x1_ref, h2_ref):
        zh, _ = _ln(ALPHA * x_ref[...] + mod_ref[2:3, :] * mix_ref[...])
        x1 = zh * g_ref[...] + b_ref[...]
        x1_ref[...] = x1
        xh, _ = _ln(x1)
        h2_ref[...] = (xh * (1.0 + mod_ref[4:5, :]) + mod_ref[3:4, :]).astype(BF16)

    row = pl.BlockSpec((tm, D), lambda i: (i, 0))
    vec = pl.BlockSpec((1, D), lambda i: (0, 0))
    return pl.pallas_call(
        body, grid=(T // tm,), name="mid_fwd",
        in_specs=[row, row, pl.BlockSpec((6, D), lambda i: (0, 0)), vec, vec],
        out_specs=[row, row],
        out_shape=[_sds((T, D), F32), _sds((T, D), BF16)], compiler_params=_params(("parallel",)),
    )(x2, mix, mod6, ln1_g, ln1_b)


def _final(x1, ff, mod6, ln2_g, ln2_b, tgt):
    T, D = x1.shape
    tm = _tile(T, ROW_TILE, 8)

    def body(x1_ref, ff_ref, mod_ref, g_ref, b_ref, t_ref, dff_ref, dx1_ref, vec_ref):
        @pl.when(pl.program_id(0) == 0)
        def _():
            vec_ref[...] = jnp.zeros_like(vec_ref)

        ff_v = ff_ref[...]
        gate2 = mod_ref[5:6, :]
        zh, rstd = _ln(ALPHA * x1_ref[...] + gate2 * ff_v)
        err = zh * g_ref[...] + b_ref[...] - t_ref[...]
        dy = err * (1.0 / D)
        dz = _ln_bwd(dy * g_ref[...], zh, rstd)
        dff_ref[...] = (gate2 * dz).astype(BF16)
        dx1_ref[...] = ALPHA * dz
        vec_ref[0:1, :] += _colsum(dy * zh)
        vec_ref[1:2, :] += _colsum(dy)
        vec_ref[2:3, :] += _colsum(dz * ff_v)
        vec_ref[3:4, :] += _colsum(err * err) * (0.5 / D)

    row = pl.BlockSpec((tm, D), lambda i: (i, 0))
    vec = pl.BlockSpec((1, D), lambda i: (0, 0))
    return pl.pallas_call(
        body, grid=(T // tm,), name="final_fwd_bwd",
        in_specs=[row, row, pl.BlockSpec((6, D), lambda i: (0, 0)), vec, vec, row],
        out_specs=[row, row, pl.BlockSpec((8, D), lambda i: (0, 0))],
        out_shape=[_sds((T, D), BF16), _sds((T, D), F32), _sds((8, D), F32)],
        compiler_params=_params(("arbitrary",)),
    )(x1, ff, mod6, ln2_g, ln2_b, tgt)


def _mid_bwd(x2, mix, x1, dx1a, dh2, mod6, ln1_g):
    T, D = x2.shape
    tm = _tile(T, ROW_TILE, 8)

    def body(x_ref, mix_ref, x1_ref, dx1a_ref, dh2_ref, mod_ref, g_ref, dmix_ref, dxa_ref, vec_ref):
        @pl.when(pl.program_id(0) == 0)
        def _():
            vec_ref[...] = jnp.zeros_like(vec_ref)

        dh2 = dh2_ref[...]
        xh, rstd = _ln(x1_ref[...])
        dx1 = dx1a_ref[...] + _ln_bwd(dh2 * (1.0 + mod_ref[4:5, :]), xh, rstd)
        mix_v = mix_ref[...]
        gate1 = mod_ref[2:3, :]
        zh, rstdz = _ln(ALPHA * x_ref[...] + gate1 * mix_v)
        dz = _ln_bwd(dx1 * g_ref[...], zh, rstdz)
        dmix_ref[...] = (gate1 * dz).astype(BF16)
        dxa_ref[...] = ALPHA * dz
        vec_ref[0:1, :] += _colsum(dh2 * xh)
        vec_ref[1:2, :] += _colsum(dh2)
        vec_ref[2:3, :] += _colsum(dx1 * zh)
        vec_ref[3:4, :] += _colsum(dx1)
        vec_ref[4:5, :] += _colsum(dz * mix_v)

    row = pl.BlockSpec((tm, D), lambda i: (i, 0))
    vec = pl.BlockSpec((1, D), lambda i: (0, 0))
    return pl.pallas_call(
        body, grid=(T // tm,), name="mid_bwd",
        in_specs=[row, row, row, row, row, pl.BlockSpec((6, D), lambda i: (0, 0)), vec],
        out_specs=[row, row, pl.BlockSpec((8, D), lambda i: (0, 0))],
        out_shape=[_sds((T, D), BF16), _sds((T, D), F32), _sds((8, D), F32)],
        compiler_params=_params(("arbitrary",)),
    )(x2, mix, x1, dx1a, dh2, mod6, ln1_g)


def _first_bwd(x2, dh1, dxa, mod6):
    T, D = x2.shape
    tm = _tile(T, ROW_TILE, 8)

    def body(x_ref, dh1_ref, dxa_ref, mod_ref, gx_ref, vec_ref):
        @pl.when(pl.program_id(0) == 0)
        def _():
            vec_ref[...] = jnp.zeros_like(vec_ref)

        dh1 = dh1_ref[...]
        xh, rstd = _ln(x_ref[...])
        gx_ref[...] = dxa_ref[...] + _ln_bwd(dh1 * (1.0 + mod_ref[1:2, :]), xh, rstd)
        vec_ref[0:1, :] += _colsum(dh1 * xh)
        vec_ref[1:2, :] += _colsum(dh1)

    row = pl.BlockSpec((tm, D), lambda i: (i, 0))
    return pl.pallas_call(
        body, grid=(T // tm,), name="first_bwd",
        in_specs=[row, row, row, pl.BlockSpec((6, D), lambda i: (0, 0))],
        out_specs=[row, pl.BlockSpec((8, D), lambda i: (0, 0))],
        out_shape=[_sds((T, D), F32), _sds((8, D), F32)],
        compiler_params=_params(("arbitrary",)),
    )(x2, dh1, dxa, mod6)


def _slot(j):
    return (j % 2) * 4 + j // 2


def _mm_gathered(a, wg, shards, out, name):
    M, K = a.shape
    _, _, ns = wg.shape
    tm = _tile(M, 512, 16)

    def body(shards_ref, a_ref, w_ref, prev_ref, o_ref):
        o_ref[...] = _dot(a_ref[...], w_ref[...], NN)

    return pl.pallas_call(
        body, name=name,
        grid_spec=pltpu.PrefetchScalarGridSpec(
            num_scalar_prefetch=1, grid=(shards.shape[0], M // tm),
            in_specs=[pl.BlockSpec((tm, K), lambda j, i, s: (i, 0)),
                      pl.BlockSpec((None, K, ns), lambda j, i, s: (s[j], 0, 0)), ORDER_ONLY],
            out_specs=pl.BlockSpec((tm, ns), lambda j, i, s: (i, s[j]))),
        out_shape=_sds((M, N_DEV * ns), F32), input_output_aliases={3: 0},
        compiler_params=_params(("parallel", "parallel"), big=True),
    )(shards, a, wg, out)


def _mm_nn(a, b, name):
    M, K = a.shape
    _, N = b.shape
    tm, tn = _tile(M, 512, 16), _tile(N, 1024, LANE)

    def body(a_ref, b_ref, o_ref):
        o_ref[...] = _dot(a_ref[...], b_ref[...], NN)

    return pl.pallas_call(
        body, grid=(N // tn, M // tm), name=name,
        in_specs=[pl.BlockSpec((tm, K), lambda j, i: (i, 0)), pl.BlockSpec((K, tn), lambda j, i: (0, j))],
        out_specs=pl.BlockSpec((tm, tn), lambda j, i: (i, j)),
        out_shape=_sds((M, N), F32), compiler_params=_params(("parallel", "parallel"), big=True),
    )(a, b)


def _mm_nt(a, b, name):
    M, K = a.shape
    N, _ = b.shape
    tm, tn = _tile(M, 512, 16), _tile(N, 1024, LANE)

    def body(a_ref, b_ref, o_ref):
        o_ref[...] = _dot(a_ref[...], b_ref[...], NT)

    return pl.pallas_call(
        body, grid=(M // tm, N // tn), name=name,
        in_specs=[pl.BlockSpec((tm, K), lambda i, j: (i, 0)), pl.BlockSpec((tn, K), lambda i, j: (j, 0))],
        out_specs=pl.BlockSpec((tm, tn), lambda i, j: (i, j)),
        out_shape=_sds((M, N), F32), compiler_params=_params(("parallel", "parallel"), big=True),
    )(a, b)


def _mm_swiglu(h2, wg):
    M, K = h2.shape
    _, _, ns = wg.shape
    half = N_DEV // 2
    tm = _tile(M, 256, 16)

    def body(a_ref, wgate_ref, wup_ref, gu_ref, act_ref):
        a = a_ref[...]
        g = _dot(a, wgate_ref[...], NN)
        u = _dot(a, wup_ref[...], NN)
        sg = _sigmoid(g)
        silu = g * sg
        gu_ref[0] = u * (sg * (1.0 + g * (1.0 - sg)))
        gu_ref[1] = silu
        act_ref[...] = (silu * u).astype(BF16)

    return pl.pallas_call(
        body, grid=(half, M // tm), name="ffn_in_swiglu",
        in_specs=[pl.BlockSpec((tm, K), lambda j, i: (i, 0)),
                  pl.BlockSpec((None, K, ns), lambda j, i: (j, 0, 0)),
                  pl.BlockSpec((None, K, ns), lambda j, i: (j + half, 0, 0))],
        out_specs=[pl.BlockSpec((2, tm, ns), lambda j, i: (0, i, j)), pl.BlockSpec((tm, ns), lambda j, i: (i, j))],
        out_shape=[_sds((2, M, half * ns), F32), _sds((M, half * ns), BF16)],
        compiler_params=_params(("parallel", "parallel"), big=True),
    )(h2, wg, wg)


def _mm_swiglu_bwd(dff, w2, gu):
    M, K = dff.shape
    F = w2.shape[0]
    tm, tn = _tile(M, 512, 16), _tile(F, 1408, LANE)

    def body(a_ref, b_ref, gu_ref, du_ref):
        da = _dot(a_ref[...], b_ref[...], NT)
        du_ref[0] = (da * gu_ref[0]).astype(BF16)
        du_ref[1] = (da * gu_ref[1]).astype(BF16)

    return pl.pallas_call(
        body, grid=(F // tn, M // tm), name="ffn_out_bwd_swiglu",
        in_specs=[pl.BlockSpec((tm, K), lambda j, i: (i, 0)), pl.BlockSpec((tn, K), lambda j, i: (j, 0)),
                  pl.BlockSpec((2, tm, tn), lambda j, i: (0, i, j))],
        out_specs=pl.BlockSpec((2, tm, tn), lambda j, i: (0, i, j)),
        out_shape=_sds((2, M, F), BF16), compiler_params=_params(("parallel", "parallel"), big=True),
    )(dff, w2, gu)


ORDER_ONLY = pl.BlockSpec(memory_space=pl.ANY)


def _mm_tn_rows(dep, a, b, rs, name):
    M, Ka = a.shape
    _, N = b.shape

    def body(_, a_ref, b_ref, o_ref):
        g = _dot(a_ref[...], b_ref[...], TN)
        o_ref[0, 0] = g[0:rs, :].astype(BF16)
        o_ref[1, 0] = g[rs:2 * rs, :].astype(BF16)

    return pl.pallas_call(
        body, grid=(N_DEV // 2,), name=name,
        in_specs=[ORDER_ONLY, pl.BlockSpec((M, 2 * rs), lambda ch: (0, ch)), pl.BlockSpec((M, N), lambda ch: (0, 0))],
        out_specs=pl.BlockSpec((2, 1, rs, N), lambda ch: (0, ch, 0, 0)),
        out_shape=_sds((2, N_DEV // 2, rs, N), BF16),
        compiler_params=_params(("parallel",), big=True),
    )(dep, a, b)


def _mm_gathered_nt(dep, a, a_spec, wg, M, tm, name, first=0, count=None, out=None):
    _, K, ns = wg.shape
    count = M // tm if count is None else count
    out = lax.empty((M, K), F32) if out is None else out

    def body(_, a_ref, w_ref, prev_ref, o_ref):
        @pl.when(pl.program_id(1) == 0)
        def _():
            o_ref[...] = jnp.zeros_like(o_ref)

        o_ref[...] += _dot(a_ref[:, 0:ns], w_ref[0], NT) + _dot(a_ref[:, ns:2 * ns], w_ref[1], NT)

    return pl.pallas_call(
        body, grid=(count, N_DEV // 2), name=name,
        in_specs=[ORDER_ONLY, a_spec(tm, 2 * ns, first), pl.BlockSpec((2, K, ns), lambda i, p: (p, 0, 0)), ORDER_ONLY],
        out_specs=pl.BlockSpec((tm, K), lambda i, j: (i + first, 0)),
        out_shape=_sds((M, K), F32), input_output_aliases={3: 0},
        compiler_params=_params(("parallel", "arbitrary"), big=True),
    )(dep, a, wg, out)


def _mm_tn_gathered(dep, h, a, a_spec, ns, name):
    M, K = h.shape

    def body(_, h_ref, a_ref, o_ref):
        o_ref[...] = _dot(h_ref[...], a_ref[...], TN).astype(BF16)

    return pl.pallas_call(
        body, grid=(N_DEV,), name=name,
        in_specs=[ORDER_ONLY, pl.BlockSpec((M, K), lambda j: (0, 0)), a_spec(M, ns)],
        out_specs=pl.BlockSpec((None, K, ns), lambda j: (_slot(j), 0, 0)),
        out_shape=_sds((N_DEV, K, ns), BF16),
        compiler_params=_params(("parallel",), big=True),
    )(dep, h, a)


def _bias_onehot(rbp, max_rel):
    r = lax.broadcasted_iota(jnp.int32, (rbp, TAB), 0)
    m = lax.broadcasted_iota(jnp.int32, (rbp, TAB), 1)
    dist = KPAD - jnp.where(m < WIN, m, m - TAB)
    return (r == jnp.clip(dist, -max_rel, max_rel) + max_rel).astype(F32)


def _attn_setup(i, hp, k_ref, v_ref, gv_ref, kpad, vpad, bias):
    ls = slice(i * ATTN_HEAD_DIM, (i + 1) * ATTN_HEAD_DIM)
    kpad[i][0:KPAD, :] = jnp.zeros((KPAD, ATTN_HEAD_DIM), BF16)
    vpad[i][0:KPAD, :] = jnp.zeros((KPAD, ATTN_HEAD_DIM), BF16)
    kpad[i][KPAD:, :] = k_ref[:, ls].astype(BF16)
    vpad[i][KPAD:, :] = v_ref[:, ls].astype(BF16)
    gvrow = gv_ref[pl.ds(hp * ATTN_HEADS_PER_STEP + i, 1), :]
    tab = pltpu.roll(jnp.broadcast_to(gvrow, (QBLK, TAB)), 0, 1, stride=1, stride_axis=0)
    row = lax.broadcasted_iota(jnp.int32, (QBLK, WIN), 0)
    col = lax.broadcasted_iota(jnp.int32, (QBLK, WIN), 1)
    first = jnp.bitwise_and(row, -CHUNK)
    seen = jnp.logical_and(col >= first, col < first + (N_PAST + 1) * CHUNK)
    bias[i][...] = jnp.where(seen, tab[:, 0:WIN], NEG)


def _attn_probs(b, q_ref, kpad, vpad, bias, col):
    pair = range(ATTN_HEADS_PER_STEP)
    ls = [slice(i * ATTN_HEAD_DIM, (i + 1) * ATTN_HEAD_DIM) for i in pair]
    r0 = pl.multiple_of(b * QBLK, QBLK)
    q = [q_ref[pl.ds(r0, QBLK), ls[i]].astype(BF16) for i in pair]
    kw = [kpad[i][pl.ds(r0, WIN), :] for i in pair]
    vw = [vpad[i][pl.ds(r0, WIN), :] for i in pair]
    s = [_dot(q[i], kw[i], NT) * (ATTN_HEAD_DIM ** -0.5) + bias[i][...] for i in pair]
    s = [jnp.where(col >= KPAD - r0, s[i], NEG) for i in pair]
    p = [jnp.exp(s[i] - jnp.max(s[i], axis=-1, keepdims=True)) for i in pair]
    pn = [p[i] / jnp.sum(p[i], axis=-1, keepdims=True) for i in pair]
    return r0, ls, q, kw, vw, pn


def _attn_fwd(proj, gv, ga, AW):
    T = proj.shape[0]
    AH = ATTN_HEADS_PER_STEP
    W = AH * ATTN_HEAD_DIM
    HP = AW // W

    def body(q_ref, k_ref, v_ref, gv_ref, ga_ref, o_ref, *scratch):
        kpad, vpad, bias = (scratch[k * AH:(k + 1) * AH] for k in range(3))
        hp = pl.program_id(0)
        for i in range(AH):
            _attn_setup(i, hp, k_ref, v_ref, gv_ref, kpad, vpad, bias)
        col = lax.broadcasted_iota(jnp.int32, (QBLK, WIN), 1)

        def block(b, carry):
            pair = range(AH)
            r0, ls, _, _, vw, pn = _attn_probs(b, q_ref, kpad, vpad, bias, col)
            o = [_dot(pn[i].astype(BF16), vw[i], NN) for i in pair]
            r = [lax.rsqrt(jnp.mean(o[i] * o[i], axis=-1, keepdims=True) + EPS) for i in pair]
            outs = [o[i] * r[i] * ga_ref[0:1, ls[i]] for i in pair]
            o_ref[pl.ds(r0, QBLK), :] = jnp.concatenate(outs, axis=1).astype(BF16)
            return carry

        lax.fori_loop(0, T // QBLK, block, 0)

    blk = lambda off: pl.BlockSpec((T, W), lambda hp: (0, off + hp))
    return pl.pallas_call(
        body, grid=(HP,), name="attn_fwd",
        in_specs=[blk(0), blk(HP), blk(2 * HP), pl.BlockSpec(gv.shape, lambda hp: (0, 0)),
                  pl.BlockSpec((1, W), lambda hp: (0, hp))],
        out_specs=pl.BlockSpec((T, W), lambda hp: (0, hp)),
        out_shape=_sds((T, AW), BF16),
        scratch_shapes=[pltpu.VMEM((T + KPAD, ATTN_HEAD_DIM), BF16)] * (2 * AH) + [pltpu.VMEM((QBLK, WIN), F32)] * AH,
        compiler_params=_params(("parallel",), big=True),
    )(proj, proj, proj, gv, ga)


def _attn_bwd(proj, dmixin, gv, ga, AW):
    T = proj.shape[0]
    AH = ATTN_HEADS_PER_STEP
    W = AH * ATTN_HEAD_DIM
    HP = AW // W
    scale = ATTN_HEAD_DIM ** -0.5

    def body(q_ref, k_ref, v_ref, dn_ref, gv_ref, ga_ref, dq_ref, dk_ref, dv_ref, dgv_ref, dga_ref, *scratch):
        kpad, vpad, dkacc, dvacc, bias, dbias = (scratch[k * AH:(k + 1) * AH] for k in range(6))
        hp = pl.program_id(0)
        for i in range(AH):
            _attn_setup(i, hp, k_ref, v_ref, gv_ref, kpad, vpad, bias)
            dkacc[i][...] = jnp.zeros_like(dkacc[i])
            dvacc[i][...] = jnp.zeros_like(dvacc[i])
            dbias[i][...] = jnp.zeros_like(dbias[i])
        dga_ref[...] = jnp.zeros_like(dga_ref)
        col = lax.broadcasted_iota(jnp.int32, (QBLK, WIN), 1)

        def block(b, carry):
            pair = range(AH)
            r0, lss, qs, kws, vws, pns = _attn_probs(b, q_ref, kpad, vpad, bias, col)
            pn_b = [pns[i].astype(BF16) for i in pair]
            o = [_dot(pn_b[i], vws[i], NN) for i in pair]
            r = [lax.rsqrt(jnp.mean(o[i] * o[i], axis=-1, keepdims=True) + EPS) for i in pair]
            dn = [dn_ref[pl.ds(r0, QBLK), lss[i]] for i in pair]
            for i in pair:
                dga_ref[i:i + 1, :] += _colsum(dn[i] * o[i] * r[i])
            a = [dn[i] * ga_ref[0:1, lss[i]] for i in pair]
            do_b = [(r[i] * (a[i] - o[i] * (r[i] * r[i]) * jnp.mean(a[i] * o[i], axis=-1, keepdims=True))).astype(BF16)
                    for i in pair]
            dp = [_dot(do_b[i], vws[i], NT) for i in pair]
            for i in pair:
                dvacc[i][pl.ds(r0, WIN), :] += _dot(pn_b[i], do_b[i], TN)
            ds = [pns[i] * (dp[i] - jnp.sum(pns[i] * dp[i], axis=-1, keepdims=True)) for i in pair]
            for i in pair:
                dbias[i][...] += ds[i]
            ds_b = [ds[i].astype(BF16) for i in pair]
            dq = [_dot(ds_b[i], kws[i], NN) * scale for i in pair]
            dq_ref[pl.ds(r0, QBLK), :] = jnp.concatenate(dq, axis=1).astype(BF16)
            for i in pair:
                dkacc[i][pl.ds(r0, WIN), :] += _dot(ds_b[i], qs[i], TN) * scale
            return carry

        lax.fori_loop(0, T // QBLK, block, 0)

        rr = lax.broadcasted_iota(jnp.int32, (QBLK, QBLK), 0)
        cc = lax.broadcasted_iota(jnp.int32, (QBLK, QBLK), 1)
        flip = (rr + cc == QBLK - 1).astype(BF16)
        for i in range(AH):
            ls = slice(i * ATTN_HEAD_DIM, (i + 1) * ATTN_HEAD_DIM)
            dk_ref[:, ls] = dkacc[i][KPAD:, :].astype(BF16)
            dv_ref[:, ls] = dvacc[i][KPAD:, :].astype(BF16)
            full = jnp.concatenate([dbias[i][...], jnp.zeros((QBLK, TAB - WIN), F32)], axis=1)
            hi = full.astype(BF16)
            lo = (full - hi.astype(F32)).astype(BF16)
            rev = _dot(flip, hi, NN) + _dot(flip, lo, NN)
            dgv_ref[i:i + 1, :] = _colsum(pltpu.roll(rev, TAB - (QBLK - 1), 1, stride=1, stride_axis=0))

    blk = lambda off: pl.BlockSpec((T, W), lambda hp: (0, off + hp))
    accs = lambda dt: [pltpu.VMEM((T + KPAD, ATTN_HEAD_DIM), dt)] * AH
    return pl.pallas_call(
        body, grid=(HP,), name="attn_bwd",
        in_specs=[blk(0), blk(HP), blk(2 * HP), blk(0), pl.BlockSpec(gv.shape, lambda hp: (0, 0)),
                  pl.BlockSpec((1, W), lambda hp: (0, hp))],
        out_specs=[blk(0), blk(0), blk(0), pl.BlockSpec((None, AH, TAB), lambda hp: (hp, 0, 0)),
                   pl.BlockSpec((None, AH, ATTN_HEAD_DIM), lambda hp: (hp, 0, 0))],
        out_shape=[_sds((T, AW), BF16), _sds((T, AW), BF16), _sds((T, AW), BF16),
                   _sds((HP, AH, TAB), F32), _sds((HP, AH, ATTN_HEAD_DIM), F32)],
        scratch_shapes=accs(BF16) + accs(BF16) + accs(F32) + accs(F32) + [pltpu.VMEM((QBLK, WIN), F32)] * (2 * AH),
        compiler_params=_params(("parallel",), big=True),
    )(proj, proj, proj, dmixin, gv, ga)


def _ltri():
    r = lax.broadcasted_iota(jnp.int32, (CHUNK, CHUNK), 0)
    c = lax.broadcasted_iota(jnp.int32, (CHUNK, CHUNK), 1)
    return (c <= r).astype(BF16)


def _tri_dot(tri, v, dims):
    hi = v.astype(BF16)
    lo = (v - hi.astype(F32)).astype(BF16)
    return _dot(tri, hi, dims) + _dot(tri, lo, dims)


HEADS_PER_STEP = 2


def _alternate(stages):
    live = list(stages)
    while live:
        for g in list(live):
            if next(g, StopIteration) is StopIteration:
                live.remove(g)


def _hgrn_gates(n, ls, q_ref, f_ref, lb_ref, ltri):
    r0 = pl.multiple_of(n * CHUNK, CHUNK)
    rows = pl.ds(r0, CHUNK)
    lb = lb_ref[:, ls]
    qb = q_ref[rows, ls]
    sg = _sigmoid(f_ref[rows, ls])
    f = lb + (1.0 - lb) * sg
    sq = _sigmoid(qb)
    b = _tri_dot(ltri, jnp.log(f), NN)
    return rows, lb, qb, sg, f, 1.0 - f, sq, qb * sq, b


def _hgrn_specs(T, RW, AW):
    HG = HEADS_PER_STEP
    W = HG * LANE
    base = 3 * AW // W
    blk_in = lambda off: pl.BlockSpec((T, W), lambda g: (0, base + off + g))
    col = pl.BlockSpec((T, W), lambda g: (0, g))
    return HG, W, RW // W, blk_in, col


def _hgrn_fwd(proj, lb, gn, AW, RW):
    T = proj.shape[0]
    RH, NC, NSUB = RW // LANE, T // CHUNK, CHUNK // SUB
    HG, W, NG, blk_in, col = _hgrn_specs(T, RW, AW)

    def body(q_ref, f_ref, i_ref, g_ref, lb_ref, gn_ref, mix_ref, o_ref, stall_ref, st_all, bs_all, kks_all, ics_all):
        st_all[...] = jnp.zeros_like(st_all)
        ltri = _ltri()
        rowi = lax.broadcasted_iota(jnp.int32, (SUB, 1), 0)

        def one_head(h, n):
            ls = slice(h * LANE, (h + 1) * LANE)
            st, bs, kks, ics = st_all.at[h], bs_all.at[h], kks_all.at[h], ics_all.at[h]
            rows, _, _, _, _, kk, _, qs, b = _hgrn_gates(n, ls, q_ref, f_ref, lb_ref, ltri)
            ic = i_ref[rows, ls]
            stv = st[...]
            stall_ref[h, n] = stv
            bs[...] = b
            kks[...] = kk
            ics[...] = ic
            yield
            o = _dot((qs * jnp.exp(b)).astype(BF16), stv.astype(BF16), NT)
            yield
            ic_b = ic.astype(BF16)
            pieces = []
            for blk in range(NSUB):
                s0 = blk * SUB
                bI, qI = b[s0:s0 + SUB], qs[s0:s0 + SUB]
                if blk == 0:
                    oI = jnp.zeros((SUB, LANE), F32)
                else:
                    ref = bs[s0 - 1:s0, :]
                    qt = (qI * jnp.exp(bI - ref)).astype(BF16)
                    kt = (kk[0:s0] * jnp.exp(ref - b[0:s0])).astype(BF16)
                    oI = _dot(_dot(qt, kt, NT).astype(BF16), ic_b[0:s0], NN)
                    yield
                acc = [oI[g * ROWS:(g + 1) * ROWS] for g in range(SUB // ROWS)]
                for s in range(SUB):
                    sr = s0 + s
                    g0 = s // ROWS
                    lo = g0 * ROWS
                    e = jnp.exp(jnp.minimum(bI[lo:] - bs[sr:sr + 1, :], 0.0))
                    a = jnp.sum(qI[lo:] * kks[sr:sr + 1, :] * e, axis=-1, keepdims=True)
                    add = jnp.where(rowi[lo:] >= s, a, 0.0) * ics[sr:sr + 1, :]
                    for g in range(g0, SUB // ROWS):
                        acc[g] = acc[g] + add[(g - g0) * ROWS:(g - g0 + 1) * ROWS]
                    yield
                pieces.extend(acc)
            o = o + jnp.concatenate(pieces, axis=0)
            bl = bs[CHUNK - 1:CHUNK, :]
            kd = (kk * jnp.exp(bl - b)).astype(BF16)
            st[...] = stv * jnp.exp(bl) + _dot(ic_b, kd, TN)
            yield
            o_ref[rows, ls] = o
            r = lax.rsqrt(jnp.mean(o * o, axis=-1, keepdims=True) + EPS)
            gb = g_ref[rows, ls]
            mix_ref[rows, ls] = (o * r * gn_ref[...] * (gb * _sigmoid(gb))).astype(BF16)

        def chunk(n, carry):
            _alternate([one_head(h, n) for h in range(HG)])
            return carry

        lax.fori_loop(0, NC, chunk, 0)

    tile = pltpu.VMEM((HG, CHUNK, LANE), F32)
    return pl.pallas_call(
        body, grid=(NG,), name="hgrn_fwd",
        in_specs=[blk_in(0), blk_in(NG), blk_in(2 * NG), blk_in(3 * NG), pl.BlockSpec((1, W), lambda g: (0, g)),
                  pl.BlockSpec((1, LANE), lambda g: (0, 0))],
        out_specs=[col, col, pl.BlockSpec((HG, NC, LANE, LANE), lambda g: (g, 0, 0, 0))],
        out_shape=[_sds((T, RW), BF16), _sds((T, RW), F32), _sds((RH, NC, LANE, LANE), F32)],
        scratch_shapes=[pltpu.VMEM((HG, LANE, LANE), F32), tile, tile, tile],
        compiler_params=_params(("parallel",), big=True),
    )(proj, proj, proj, proj, lb, gn)


def _hgrn_bwd(proj, dmixin, o_b, st_all, lb, gn, AW, RW):
    T = proj.shape[0]
    RH, NC, NSUB = RW // LANE, T // CHUNK, CHUNK // SUB
    HG, W, NG, blk_in, col = _hgrn_specs(T, RW, AW)

    def body(q_ref, f_ref, i_ref, g_ref, o_ref, dn_ref, stall_ref, lb_ref, gn_ref,
             dq_ref, df_ref, di_ref, dg_ref, dlb_ref, dgn_ref, dst_all, bs_all, qss_all, dos_all, p2_all, dic_all,
             p1_all):
        dst_all[...] = jnp.zeros_like(dst_all)
        dlb_ref[...] = jnp.zeros_like(dlb_ref)
        dgn_ref[...] = jnp.zeros_like(dgn_ref)
        ltri = _ltri()
        rowi = lax.broadcasted_iota(jnp.int32, (SUB, 1), 0)
        last = lax.broadcasted_iota(jnp.int32, (CHUNK, 1), 0) == CHUNK - 1

        def one_head(h, n):
            ls = slice(h * LANE, (h + 1) * LANE)
            dst, bs, qss, dos = dst_all.at[h], bs_all.at[h], qss_all.at[h], dos_all.at[h]
            p2, dic, p1s = p2_all.at[h], dic_all.at[h], p1_all.at[h]
            rows, lbv, qb, sg, f, kk, sq, qs, b = _hgrn_gates(n, ls, q_ref, f_ref, lb_ref, ltri)
            ic = i_ref[rows, ls]
            stv = stall_ref[h, n]
            dstv = dst[...]
            o = o_ref[rows, ls]
            dn = dn_ref[rows, ls]
            gb = g_ref[rows, ls]
            sgb = _sigmoid(gb)
            r = lax.rsqrt(jnp.mean(o * o, axis=-1, keepdims=True) + EPS)
            gnv = gn_ref[...]
            dg_ref[rows, ls] = (dn * (o * r * gnv) * (sgb * (1.0 + gb * (1.0 - sgb)))).astype(BF16)
            dy = dn * (gb * sgb)
            dgn_ref[h] += _colsum(dy * o * r)
            a_ = dy * gnv
            do = r * (a_ - o * (r * r) * jnp.mean(a_ * o, axis=-1, keepdims=True))
            do_b = do.astype(BF16)
            bs[...] = b
            qss[...] = qs
            dos[...] = do
            yield
            ic_b = ic.astype(BF16)
            eb = jnp.exp(b)
            bl = bs[CHUNK - 1:CHUNK, :]
            ebl = jnp.exp(bl)
            dec = jnp.exp(bl - b)
            kd = (kk * dec).astype(BF16)
            dst_b = dstv.astype(BF16)
            dqs = _dot(do_b, stv.astype(BF16), NN) * eb
            dkk2 = _dot(ic_b, dst_b, NN) * dec
            dic[...] = _dot(kd, dst_b, NT)
            dbl = ebl * _colsum(stv * dstv) + _colsum(kk * dkk2)
            dst[...] = dstv * ebl + _dot(do_b, (qs * eb).astype(BF16), TN)
            yield
            p2[...] = jnp.zeros_like(p2)
            p1_pieces = []
            for blk in range(NSUB):
                s0 = blk * SUB
                bI, qI, doI = b[s0:s0 + SUB], qs[s0:s0 + SUB], do[s0:s0 + SUB]
                if blk == 0:
                    p1 = jnp.zeros((SUB, LANE), F32)
                else:
                    ref = bs[s0 - 1:s0, :]
                    eq = jnp.exp(bI - ref)
                    ek = jnp.exp(ref - b[0:s0])
                    qt = (qI * eq).astype(BF16)
                    kt = (kk[0:s0] * ek).astype(BF16)
                    doI_b = doI.astype(BF16)
                    dic[0:s0, :] += _dot(_dot(qt, kt, NT).astype(BF16), doI_b, TN)
                    da = _dot(doI_b, ic_b[0:s0], NT).astype(BF16)
                    p1 = _dot(da, kt, NN) * eq
                    p2[0:s0, :] += _dot(da, qt, TN) * ek
                    yield
                p1_pieces.append(p1)
                kkI, icI = kk[s0:s0 + SUB], ic[s0:s0 + SUB]
                p2acc = [jnp.zeros((ROWS, LANE), F32) for _ in range(SUB // ROWS)]
                diacc = [jnp.zeros((ROWS, LANE), F32) for _ in range(SUB // ROWS)]
                for t in range(SUB):
                    tr = s0 + t
                    ng = t // ROWS + 1
                    hi = ng * ROWS
                    keep = rowi[:hi] <= t
                    do_t = dos[tr:tr + 1, :]
                    e = jnp.exp(jnp.minimum(bs[tr:tr + 1, :] - bI[:hi], 0.0))
                    qe = qss[tr:tr + 1, :] * e
                    a = jnp.where(keep, jnp.sum(kkI[:hi] * qe, axis=-1, keepdims=True), 0.0)
                    da = jnp.where(keep, jnp.sum(icI[:hi] * do_t, axis=-1, keepdims=True), 0.0)
                    dp2, ddi = da * qe, a * do_t
                    for g in range(ng):
                        p2acc[g] = p2acc[g] + dp2[g * ROWS:(g + 1) * ROWS]
                        diacc[g] = diacc[g] + ddi[g * ROWS:(g + 1) * ROWS]
                    p1s[tr:tr + 1, :] = _colsum(da * kkI[:hi] * e)
                    yield
                p2[s0:s0 + SUB, :] += jnp.concatenate(p2acc, axis=0)
                dic[s0:s0 + SUB, :] += jnp.concatenate(diacc, axis=0)
            dqs = dqs + jnp.concatenate(p1_pieces, axis=0) + p1s[...]
            dkk = dkk2 + p2[...]
            db = qs * dqs - kk * dkk + jnp.where(last, dbl, 0.0)
            dgl = _tri_dot(ltri, db, TN)
            yield
            dfv = dgl / f - dkk
            df_ref[rows, ls] = (dfv * (1.0 - lbv) * sg * (1.0 - sg)).astype(BF16)
            dlb_ref[:, ls] += _colsum(dfv * (1.0 - sg))
            dq_ref[rows, ls] = (dqs * (sq * (1.0 + qb * (1.0 - sq)))).astype(BF16)
            di_ref[rows, ls] = dic[...].astype(BF16)

        def chunk(k, carry):
            _alternate([one_head(h, NC - 1 - k) for h in range(HG)])
            return carry

        lax.fori_loop(0, NC, chunk, 0)

    tile = pltpu.VMEM((HG, CHUNK, LANE), F32)
    return pl.pallas_call(
        body, grid=(NG,), name="hgrn_bwd",
        in_specs=[blk_in(0), blk_in(NG), blk_in(2 * NG), blk_in(3 * NG), col,
                  pl.BlockSpec((T, W), lambda g: (0, AW // W + g)),
                  pl.BlockSpec((HG, NC, LANE, LANE), lambda g: (g, 0, 0, 0)),
                  pl.BlockSpec((1, W), lambda g: (0, g)), pl.BlockSpec((1, LANE), lambda g: (0, 0))],
        out_specs=[col, col, col, col, pl.BlockSpec((1, W), lambda g: (0, g)),
                   pl.BlockSpec((HG, 1, LANE), lambda g: (g, 0, 0))],
        out_shape=[_sds((T, RW), BF16)] * 4 + [_sds((1, RW), F32), _sds((RH, 1, LANE), F32)],
        scratch_shapes=[pltpu.VMEM((HG, LANE, LANE), F32), tile, tile, tile, tile, tile, tile],
        compiler_params=_params(("parallel",), big=True),
    )(proj, proj, proj, proj, o_b, dmixin, st_all, lb, gn)


def _prep(c, lb_logits, rb_pad, max_rel, after):
    D, RW = c.shape[-1], lb_logits.shape[-1]
    H, rbp = rb_pad.shape

    def body(c_ref, l_ref, rb_ref, _, __, cact_ref, lb_ref, gv_ref):
        cv = c_ref[...]
        cact_ref[...] = cv * _sigmoid(cv)
        lb_ref[...] = _sigmoid(l_ref[0:1, :] - l_ref[1:2, :])
        gv_ref[...] = _dot(rb_ref[...], _bias_onehot(rbp, max_rel), NN, HIGHEST)

    vmem = pl.BlockSpec(memory_space=pltpu.VMEM)
    return pl.pallas_call(
        body, name="prep", in_specs=[vmem, vmem, vmem, ORDER_ONLY, ORDER_ONLY],
        out_shape=[_sds((1, D), F32), _sds((1, RW), F32), _sds((H, TAB), F32)],
    )(c, lb_logits, rb_pad, *after)


def _mod_part(c_all, w_ada_s, b_ada_s):
    B, D = c_all.shape
    ns = w_ada_s.shape[1]
    tn = _tile(ns, 768, LANE)

    def body(c_ref, w_ref, b_ref, o_ref):
        o_ref[...] = _dot(c_ref[...], w_ref[...], NN) + b_ref[...]

    return pl.pallas_call(
        body, grid=(ns // tn,), name="mod_part",
        in_specs=[pl.BlockSpec((B, D), lambda j: (0, 0)), pl.BlockSpec((D, tn), lambda j: (0, j)),
                  pl.BlockSpec((1, tn), lambda j: (0, j))],
        out_specs=pl.BlockSpec((B, tn), lambda j: (0, j)),
        out_shape=_sds((B, ns), F32), compiler_params=_params(("parallel",)),
    )(c_all, w_ada_s, b_ada_s)


def _adam(w, g, m, v):
    m = ADAM_B1 * m + (1.0 - ADAM_B1) * g
    v = ADAM_B2 * v + (1.0 - ADAM_B2) * (g * g)
    m_hat = m * (1.0 / (1.0 - ADAM_B1 ** ADAM_STEP))
    v_hat = v * (1.0 / (1.0 - ADAM_B2 ** ADAM_STEP))
    return -ADAM_LR * (m_hat / (jnp.sqrt(v_hat) + ADAM_EPS) + ADAM_WD * w), m, v


def _adam_ada(c_all, dmod_s, w, m, v):
    B, D = c_all.shape
    ns = w.shape[1]
    tr, tn = _tile(D, 512, LANE), _tile(ns, 768, LANE)

    def body(c_ref, d_ref, w_ref, m_ref, v_ref, g_out, dw_out, m_out, v_out):
        g = _dot(c_ref[...], d_ref[...], TN)
        g_out[...] = g
        dw_out[...], m_out[...], v_out[...] = _adam(w_ref[...], g, m_ref[...], v_ref[...])

    big = pl.BlockSpec((tr, tn), lambda i, j: (i, j))
    return pl.pallas_call(
        body, grid=(D // tr, ns // tn), name="adam_w_ada",
        in_specs=[pl.BlockSpec((B, tr), lambda i, j: (0, i)), pl.BlockSpec((B, tn), lambda i, j: (0, j)),
                  big, big, big],
        out_specs=[big] * 4, out_shape=[_sds((D, ns), F32)] * 4,
        compiler_params=_params(("parallel", "parallel")),
    )(c_all, dmod_s, w, m, v)


def _adam_shard(parts, w, m, v, name):
    R, C = w.shape
    tr = _tile(R, 256, 16)

    def body(p_ref, w_ref, m_ref, v_ref, g_out, dw_out, m_out, v_out):
        g = p_ref[0].astype(F32)
        for k in range(1, N_DEV // 2):
            g = g + p_ref[k].astype(F32)
        g_out[...] = g
        dw_out[...], m_out[...], v_out[...] = _adam(w_ref[...], g, m_ref[...], v_ref[...])

    big = pl.BlockSpec((tr, C), lambda i: (i, 0))
    return pl.pallas_call(
        body, grid=(R // tr,), name=name,
        in_specs=[pl.BlockSpec((N_DEV // 2, tr, C), lambda i: (0, i, 0)), big, big, big],
        out_specs=[big] * 4, out_shape=[_sds((R, C), F32)] * 4,
        compiler_params=_params(("parallel",), big=True),
    )(parts, w, m, v)


def _pair_sum(g8, land, core, name):
    _, NCHIP, R, C = g8.shape
    tr = _tile(R, 1024, 16)

    def body(core_ref, g_ref, l_ref, o_ref):
        o_ref[...] = g_ref[...] + l_ref[...]

    return pl.pallas_call(
        body, name=name,
        grid_spec=pltpu.PrefetchScalarGridSpec(
            num_scalar_prefetch=1, grid=(NCHIP, R // tr),
            in_specs=[pl.BlockSpec((None, None, tr, C), lambda k, i, core_ref: (core_ref[0], k, i, 0)),
                      pl.BlockSpec((None, tr, C), lambda k, i, core_ref: (k, i, 0))],
            out_specs=pl.BlockSpec((None, tr, C), lambda k, i, core_ref: (k, i, 0))),
        out_shape=_sds((NCHIP, R, C), BF16), compiler_params=_params(("parallel", "parallel")),
    )(core, g8, land)


SMALL = ("b_ada", "rel_bias", "attn_norm_g", "lb_logits", "gnorm_g", "ln1_g", "ln1_b", "ln2_g", "ln2_b")


def _small_update(parts, loss_parts, lbv, ws, ms, vs, max_rel):
    n = len(SMALL)

    def body(*refs):
        part_refs = dict(zip(SMALL, refs[:n]))
        loss_in, lb_ref = refs[n], refs[n + 1]
        w_refs, m_refs, v_refs = refs[n + 2:2 * n + 2], refs[2 * n + 2:3 * n + 2], refs[3 * n + 2:4 * n + 2]
        outs = refs[4 * n + 2:]

        def total(ref):
            tot = ref[0]
            for k in range(1, N_DEV):
                tot = tot + ref[k]
            return tot

        outs[0][...] = jnp.sum(total(loss_in), axis=-1, keepdims=True)
        for idx, name in enumerate(SMALL):
            g = total(part_refs[name])
            if name == "rel_bias":
                g = _dot(g, _bias_onehot(w_refs[idx].shape[1], max_rel), NT, HIGHEST)
            elif name == "lb_logits":
                lb = lb_ref[...]
                sign = (1 - 2 * lax.broadcasted_iota(jnp.int32, (2, 1), 0)).astype(F32)
                g = sign * (g * lb * (1.0 - lb))
            elif name == "gnorm_g":
                g = _colsum(g)
            dw, mm, vv = _adam(w_refs[idx][...], g, m_refs[idx][...], v_refs[idx][...])
            outs[1 + 4 * idx][...] = g
            outs[2 + 4 * idx][...] = dw
            outs[3 + 4 * idx][...] = mm
            outs[4 + 4 * idx][...] = vv

    out_shape = [_sds((1, 1), F32)]
    for w in ws:
        out_shape += [_sds(w.shape, F32)] * 4
    return pl.pallas_call(body, name="small_update", out_shape=out_shape, compiler_params=_params(big=True))(
        *[parts[k] for k in SMALL], loss_parts, lbv, *ws, *ms, *vs)


def _place():
    x, y, c = lax.axis_index("x"), lax.axis_index("y"), lax.axis_index("c")
    return x, y, c, [(1 - x, y), (x, 1 - y), (1 - x, 1 - y)]


def _all_gather(shard, name):
    HBM = pl.BlockSpec(memory_space=pl.ANY)

    def body(x_ref, out_ref, send_sems, recv_sems, local_sem):
        x, y, c, chips = _place()
        me, sibling = (x, y, c), (x, y, 1 - c)

        def slot(px, py, pc):
            return out_ref.at[4 * px + 2 * py + pc]

        def copy(k, block, to, src=None):
            return pltpu.make_async_remote_copy(
                src_ref=slot(*block) if src is None else src, dst_ref=slot(*block),
                send_sem=send_sems.at[k], recv_sem=recv_sems.at[k], device_id=to, device_id_type=MESH)

        mine = pltpu.make_async_copy(x_ref, slot(*me), local_sem)
        mine.start()
        first = [copy(0, me, sibling, src=x_ref)]
        first += [copy(1 + j, me, (*chip, c), src=x_ref) for j, chip in enumerate(chips)]
        for cp in first:
            cp.start()
        passed = [copy(4 + j, (*chip, c), sibling) for j, chip in enumerate(chips)]
        for j, chip in enumerate(chips):
            copy(1 + j, (*chip, c), me).wait_recv()
            passed[j].start()
        copy(0, sibling, me).wait_recv()
        for j, chip in enumerate(chips):
            copy(4 + j, (*chip, 1 - c), me).wait_recv()
        for cp in first + passed:
            cp.wait_send()
        mine.wait()

    return pl.pallas_call(
        body, name=name, out_shape=_sds((N_DEV,) + shard.shape, shard.dtype),
        in_specs=[HBM], out_specs=HBM,
        scratch_shapes=[pltpu.SemaphoreType.DMA((7,)), pltpu.SemaphoreType.DMA((7,)), pltpu.SemaphoreType.DMA(())],
    )(shard)


SEM_SPEC = pl.BlockSpec(memory_space=pltpu.SEMAPHORE)
HBM_SPEC = pl.BlockSpec(memory_space=pltpu.HBM)
EFFECT = pltpu.SideEffectType.DATAFLOW_SIDE_EFFECTING


def _remote(src, dst, send_sems, recv_sems, k, dev):
    return pltpu.make_async_remote_copy(src_ref=src, dst_ref=dst, send_sem=send_sems.at[k], recv_sem=recv_sems.at[k],
                                        device_id=dev, device_id_type=MESH)


def _copy_start(name, bufs, plan, n, after, only=None):
    nb = len(bufs)

    def body(*refs):
        send_sems, recv_sems = refs[nb + 1], refs[nb + 2]
        for k, (src, dst, dev) in enumerate(plan(*refs[:nb])):
            if only is not None and k not in only:
                continue
            _remote(src, dst, send_sems, recv_sems, k, dev).start()
        refs[-1][...] = jnp.zeros_like(refs[-1])

    out = pl.pallas_call(
        body, name=name,
        out_shape=(pltpu.SemaphoreType.DMA((n,)), pltpu.SemaphoreType.DMA((n,)),
                   *[pltpu.HBM(b.shape, b.dtype) for b in bufs], _sds((8, LANE), F32)),
        in_specs=[HBM_SPEC] * nb + [ORDER_ONLY],
        out_specs=(SEM_SPEC, SEM_SPEC, *[HBM_SPEC] * nb, pl.BlockSpec(memory_space=pltpu.VMEM)),
        input_output_aliases={i: 2 + i for i in range(nb)},
        compiler_params=pltpu.CompilerParams(has_side_effects=EFFECT),
    )(*[pltpu.with_memory_space_constraint(b, pltpu.HBM) for b in bufs], after)
    return (out[0], out[1]), list(out[2:2 + nb]), out[-1]


def _copy_wait(name, sems, bufs, plan, after, only=None):
    nb = len(bufs)

    def body(*refs):
        send_sems, recv_sems = refs[nb], refs[nb + 1]
        for k, (src, dst, dev) in enumerate(plan(*refs[:nb])):
            if only is not None and k not in only:
                continue
            cp = _remote(src, dst, send_sems, recv_sems, k, dev)
            cp.wait_send()
            cp.wait_recv()

    out = pl.pallas_call(
        body, name=name, out_shape=tuple(pltpu.HBM(b.shape, b.dtype) for b in bufs),
        in_specs=[HBM_SPEC] * nb + [SEM_SPEC, SEM_SPEC, pl.BlockSpec(memory_space=pl.ANY)],
        out_specs=tuple([HBM_SPEC] * nb), input_output_aliases={i: i for i in range(nb)},
        compiler_params=pltpu.CompilerParams(has_side_effects=EFFECT),
    )(*bufs, sems[0], sems[1], after)
    return list(out)


def _ag_plan_chips(shard_ref, out_ref):
    x, y, c, chips = _place()
    mine = out_ref.at[4 * x + 2 * y + c]
    return [(shard_ref, mine, (x, y, 1 - c))] + [(shard_ref, mine, (*chip, c)) for chip in chips]


def _ag_plan_pass(out_ref):
    x, y, c, chips = _place()
    slots = [out_ref.at[4 * chip[0] + 2 * chip[1] + c] for chip in chips]
    return [(s, s, (x, y, 1 - c)) for s in slots]


def _rs_plan_pair(g_ref, land_ref):
    x, y, c, _ = _place()
    return [(g_ref.at[1 - c], land_ref, (x, y, 1 - c))]


def _rs_plan_chips(p_ref, land_ref):
    x, y, c, chips = _place()
    return [(p_ref.at[2 * chip[0] + chip[1]], land_ref.at[2 * x + y], (*chip, c)) for chip in chips]


class _Gather:
    @staticmethod
    def landing(shard, me):
        return lax.dynamic_update_slice(lax.empty((N_DEV,) + shard.shape, shard.dtype), shard[None],
                                        (me,) + (0,) * shard.ndim)

    def __init__(self, shard, out, tag, after):
        self.tag = tag
        self.sems, (self.shard, self.out), self.token = _copy_start(
            "ag_start_" + tag, [shard, out], _ag_plan_chips, 4, after)
        self.groups = []

    def arrived(self, after, copies):
        name = "ag_wait_%s_%s" % (self.tag, "".join(map(str, copies)))
        self.shard, self.out = _copy_wait(name, self.sems, [self.shard, self.out], _ag_plan_chips, after, copies)
        return self.out

    def pass_on(self, after, blocks):
        name = "ag_pass_%s_%s" % (self.tag, "".join(map(str, blocks)))
        sems, (self.out,), _ = _copy_start(name, [self.out], _ag_plan_pass, 3, after, blocks)
        self.groups.append((sems, blocks))
        return self.out

    def passed(self, after, group):
        sems, blocks = self.groups[group]
        name = "ag_pass_wait_%s_%s" % (self.tag, "".join(map(str, blocks)))
        self.out = _copy_wait(name, sems, [self.out], _ag_plan_pass, after, blocks)[0]
        return self.out

    def arrived_from_chips(self, after):
        self.arrived(after, (0, 1, 2, 3))
        return self.pass_on(after, (0, 1, 2))

    def passed_on(self, after):
        return self.passed(after, 0)


def _ag_plan_direct(src_ref, out_ref):
    x, y, c, chips = _place()
    mine = out_ref.at[4 * x + 2 * y + c]
    peers = [(x, y, 1 - c)] + [(*chip, pc) for chip in chips for pc in (c, 1 - c)]
    return [(src_ref, mine, peer) for peer in peers]


class _SmallGather:
    def __init__(self, block, me, tag):
        self.tag = tag
        out = lax.dynamic_update_slice(lax.empty((N_DEV,) + block.shape, block.dtype), block[None],
                                       (me,) + (0,) * block.ndim)
        self.sems, self.bufs, self.token = _copy_start(
            "ag_direct_start_" + tag, [block, out], _ag_plan_direct, N_DEV - 1, jnp.zeros((1,), F32))

    def done(self, after):
        return _copy_wait("ag_direct_wait_" + self.tag, self.sems, self.bufs, _ag_plan_direct, after)[1]


class _ReduceScatter:
    def __init__(self, g8, tag):
        self.tag = tag
        land = lax.empty(g8.shape[1:], g8.dtype)
        self.sems, self.bufs, self.token = _copy_start(
            "rs_pair_start_" + tag, [g8, land], _rs_plan_pair, 1, jnp.zeros((1,), F32))

    def pair_done(self, core, chip, after):
        g8, land = _copy_wait("rs_pair_wait_" + self.tag, self.sems, self.bufs, _rs_plan_pair, after)
        p4 = _pair_sum(g8, land, core, "rs_pair_sum_" + self.tag)
        own = lax.dynamic_slice_in_dim(p4, chip, 1, axis=0)
        land2 = lax.dynamic_update_slice(lax.empty(p4.shape, p4.dtype), own, (chip, 0, 0))
        self.sems, self.bufs, self.token = _copy_start(
            "rs_chips_start_" + self.tag, [p4, land2], _rs_plan_chips, 3, jnp.zeros((1,), F32))

    def sums(self, after):
        return _copy_wait("rs_chips_wait_" + self.tag, self.sems, self.bufs, _rs_plan_chips, after)[1]


ORDER = ("w_ada", "b_ada", "w_in", "rel_bias", "attn_norm_g", "lb_logits", "gnorm_g", "w_o", "ln1_g", "ln1_b",
         "w_ffn_in", "w_ffn_out", "ln2_g", "ln2_b")


def kernel(x, c, w_ada, b_ada, w_in, rel_bias, attn_norm_g, lb_logits, gnorm_g, w_o, ln1_g, ln1_b, w_ffn_in, w_ffn_out, ln2_g, ln2_b, loss_target, m_w_ada, m_b_ada, m_w_in, m_rel_bias, m_attn_norm_g, m_lb_logits, m_gnorm_g, m_w_o, m_ln1_g, m_ln1_b, m_w_ffn_in, m_w_ffn_out, m_ln2_g, m_ln2_b, v_w_ada, v_b_ada, v_w_in, v_rel_bias, v_attn_norm_g, v_lb_logits, v_gnorm_g, v_w_o, v_ln1_g, v_ln1_b, v_w_ffn_in, v_w_ffn_out, v_ln2_g, v_ln2_b):
    W = dict(w_ada=w_ada, b_ada=b_ada, w_in=w_in, rel_bias=rel_bias, attn_norm_g=attn_norm_g, lb_logits=lb_logits,
             gnorm_g=gnorm_g, w_o=w_o, ln1_g=ln1_g, ln1_b=ln1_b, w_ffn_in=w_ffn_in, w_ffn_out=w_ffn_out,
             ln2_g=ln2_g, ln2_b=ln2_b)
    M = dict(w_ada=m_w_ada, b_ada=m_b_ada, w_in=m_w_in, rel_bias=m_rel_bias, attn_norm_g=m_attn_norm_g,
             lb_logits=m_lb_logits, gnorm_g=m_gnorm_g, w_o=m_w_o, ln1_g=m_ln1_g, ln1_b=m_ln1_b,
             w_ffn_in=m_w_ffn_in, w_ffn_out=m_w_ffn_out, ln2_g=m_ln2_g, ln2_b=m_ln2_b)
    V = dict(w_ada=v_w_ada, b_ada=v_b_ada, w_in=v_w_in, rel_bias=v_rel_bias, attn_norm_g=v_attn_norm_g,
             lb_logits=v_lb_logits, gnorm_g=v_gnorm_g, w_o=v_w_o, ln1_g=v_ln1_g, ln1_b=v_ln1_b,
             w_ffn_in=v_w_ffn_in, w_ffn_out=v_w_ffn_out, ln2_g=v_ln2_g, ln2_b=v_ln2_b)

    x2, tgt = x[0], loss_target[0]
    T, D = x2.shape
    AW, RW = attn_norm_g.shape[-1], lb_logits.shape[-1]
    MIX = AW + RW
    H, RH = AW // ATTN_HEAD_DIM, RW // LANE
    RB = rel_bias.shape[-1]
    max_rel = (RB - 1) // 2
    rbp = -(-RB // LANE) * LANE
    F = w_ffn_out.shape[1] * N_DEV
    half = N_DEV // 2
    xi, yi, ci = lax.axis_index("x"), lax.axis_index("y"), lax.axis_index("c")
    me = 4 * xi + 2 * yi + ci
    core = jnp.reshape(ci, (1,)).astype(jnp.int32)
    pad_rb = lambda a: jnp.pad(a[0], ((0, 0), (0, rbp - RB)))

    chip = 2 * xi + yi

    w_in_b = w_in[0].astype(BF16)
    w_in_land = _Gather.landing(w_in_b, me)
    c_act, lbv, gv = _prep(c, lb_logits, pad_rb(rel_bias), max_rel, (w_in_b, w_in_land))
    c_all = _all_gather(c_act, "ag_c").reshape(N_DEV, D)
    ns_ada = w_ada.shape[-1]
    mod_part = _mod_part(c_all, w_ada[0], lax.dynamic_slice_in_dim(b_ada, me * ns_ada, ns_ada, axis=1))
    mod_all = _all_gather(mod_part, "ag_mod")
    mod6 = lax.dynamic_index_in_dim(mod_all, me, axis=1, keepdims=False).reshape(6, D)

    bf = lambda w: w[0].astype(BF16)
    ag_in = _Gather(w_in_b, w_in_land, "w_in", mod_all)
    ag_o = _Gather(bf(w_o), _Gather.landing(bf(w_o), me), "w_o", ag_in.token)
    ag_f1 = _Gather(bf(w_ffn_in), _Gather.landing(bf(w_ffn_in), me), "w_ffn_in", ag_o.token)
    ag_f2 = _Gather(bf(w_ffn_out), _Gather.landing(bf(w_ffn_out), me), "w_ffn_out", ag_f1.token)

    h1 = _ln_mod(x2, mod6 + ag_f2.token[0, 0])
    ids = lambda pairs: jnp.stack([4 * px + 2 * py + pc for px, py, pc in pairs]).astype(jnp.int32)
    others = [(1 - xi, yi), (xi, 1 - yi), (1 - xi, 1 - yi)]
    proj = lax.empty((T, w_in.shape[-1] * N_DEV), F32)
    proj = _mm_gathered(h1, ag_in.arrived(h1, (0,)), ids([(xi, yi, ci), (xi, yi, 1 - ci)]), proj, "in_proj_a")
    ag_in.arrived(proj, (1, 2, 3))
    proj = _mm_gathered(h1, ag_in.pass_on(proj, (0, 1, 2)), ids([(*ch, ci) for ch in others]), proj, "in_proj_b")
    wg_in = ag_in.passed(proj, 0)
    proj = _mm_gathered(h1, wg_in, ids([(*ch, 1 - ci) for ch in others]), proj, "in_proj_c")
    ag_o.arrived_from_chips(proj)
    mix_a = _attn_fwd(proj, gv, attn_norm_g, AW)
    wg_o = ag_o.passed_on(mix_a).reshape(MIX, D)
    mix_b, o_b, st_all = _hgrn_fwd(proj, lbv, gnorm_g, AW, RW)
    mixin = jnp.concatenate([mix_a, mix_b], axis=1)
    mix = _mm_nn(mixin, wg_o, "out_proj")
    ag_f1.arrived_from_chips(mix)
    x1, h2 = _mid_fwd(x2, mix, mod6, ln1_g, ln1_b)
    wg_f1 = ag_f1.passed_on(h2)
    gu, act = _mm_swiglu(h2, wg_f1)
    ag_f2.arrived_from_chips(act)
    wg_f2 = ag_f2.passed_on(act).reshape(F, D)
    ff = _mm_nn(act, wg_f2, "ffn_out")
    dff, dx1a, vec_a = _final(x1, ff, mod6, ln2_g, ln2_b, tgt)

    du = _mm_swiglu_bwd(dff, wg_f2, gu)
    rs_f2 = _ReduceScatter(_mm_tn_rows(dff, act, dff, F // N_DEV, "grad_w_ffn_out"), "w_ffn_out")
    tm = _tile(T, 512, 16)
    du_ij = lambda tm_, w, first: pl.BlockSpec((None, tm_, w), lambda i, p: (p // (half // 2), i + first, p % (half // 2)))
    du_j = lambda rows, ns: pl.BlockSpec((None, rows, ns), lambda j: (j // half, 0, j % half))
    dh2 = _mm_gathered_nt(rs_f2.token, du, du_ij, wg_f1, T, tm, "ffn_in_bwd")
    rs_f2.pair_done(core, chip, dh2)
    gw_f1 = _mm_tn_gathered(rs_f2.token, h2, du, du_j, wg_f1.shape[-1], "grad_w_ffn_in")
    rs_f1 = _ReduceScatter(gw_f1.reshape(2, half, D, -1), "w_ffn_in")
    dmix, dxa, vec_b = _mid_bwd(x2, mix, x1, dx1a, dh2, mod6 + rs_f1.token[0, 0], ln1_g)
    dmixin = _mm_nt(dmix, wg_o, "out_proj_bwd")
    rs_f1.pair_done(core, chip, dmixin)
    rs_o = _ReduceScatter(_mm_tn_rows(rs_f1.token, mixin, dmix, MIX // N_DEV, "grad_w_o"), "w_o")
    dq, dk, dv, dgv, dga = _attn_bwd(proj, dmixin, gv + rs_o.token[0, 0], attn_norm_g, AW)
    rs_o.pair_done(core, chip, dq)
    dqb, dfl, dib, dgb, dlb, dgn = _hgrn_bwd(proj, dmixin, o_b, st_all, lbv + rs_o.token[0, 0], gnorm_g, AW, RW)
    dproj = jnp.concatenate([dq, dk, dv, dqb, dfl, dib, dgb], axis=1)
    p_ij = lambda tm_, w, first: pl.BlockSpec((tm_, w), lambda i, p: (i + first, p))
    p_j = lambda rows, ns: pl.BlockSpec((rows, ns), lambda j: (0, j))
    gw_in = _mm_tn_gathered(rs_o.token, h1, dproj, p_j, wg_in.shape[-1], "grad_w_in")
    rs_in = _ReduceScatter(gw_in.reshape(2, half, D, -1), "w_in")
    n_tiles = T // tm
    dh1 = _mm_gathered_nt(rs_in.token, dproj, p_ij, wg_in, T, tm, "in_proj_bwd_a", 0, n_tiles // 2)
    rs_in.pair_done(core, chip, dh1)
    dh1 = _mm_gathered_nt(rs_in.token, dproj, p_ij, wg_in, T, tm, "in_proj_bwd_b", n_tiles // 2,
                          n_tiles - n_tiles // 2, dh1)
    grad_x, vec_c = _first_bwd(x2, dh1, dxa, mod6)

    dmod = jnp.concatenate([vec_c[1:2], vec_c[0:1], vec_b[4:5], vec_b[1:2], vec_b[0:1], vec_a[2:3]], axis=0)
    pieces = dict(b_ada=dmod, rel_bias=dgv, attn_norm_g=dga, lb_logits=dlb, gnorm_g=dgn, ln1_g=vec_b[2:3],
                  ln1_b=vec_b[3:4], ln2_g=vec_a[0:1], ln2_b=vec_a[1:2], loss=vec_a[3:4])
    widths = dict(b_ada=(1, 6 * D), rel_bias=(H, TAB), attn_norm_g=(1, AW), lb_logits=(1, RW), gnorm_g=(RH, LANE),
                  ln1_g=(1, D), ln1_b=(1, D), ln2_g=(1, D), ln2_b=(1, D), loss=(1, D))
    packed = jnp.concatenate([pieces[k].reshape(-1, LANE) for k in widths], axis=0)
    small_ag = _SmallGather(packed, me, "small")
    after, res_big = small_ag.token, {}
    for k, rs in (("w_ffn_out", rs_f2), ("w_ffn_in", rs_f1), ("w_o", rs_o), ("w_in", rs_in)):
        four = _adam_shard(rs.sums(after), W[k][0], M[k][0], V[k][0], "adam_" + k)
        res_big[k] = [a[None] for a in four]
        after = four[0]
    gathered = small_ag.done(after)
    parts, r0 = {}, 0
    for k, (rows, width) in widths.items():
        nr = rows * width // LANE
        parts[k] = gathered[:, r0:r0 + nr, :].reshape(N_DEV, rows, width)
        r0 += nr
    prep_small = lambda d, k: pad_rb(d[k]) if k == "rel_bias" else d[k]
    small = _small_update(parts, parts["loss"], lbv, [prep_small(W, k) for k in SMALL],
                          [prep_small(M, k) for k in SMALL], [prep_small(V, k) for k in SMALL], max_rel)
    loss = small[0].reshape(())
    res = {}
    for idx, k in enumerate(SMALL):
        four = small[1 + 4 * idx:5 + 4 * idx]
        if k == "rel_bias":
            four = [a[:, :RB][None] for a in four]
        res[k] = list(four)

    res.update(res_big)
    dmod_s = lax.dynamic_slice_in_dim(parts["b_ada"].reshape(N_DEV, 6 * D), me * ns_ada, ns_ada, axis=1)
    res["w_ada"] = [a[None] for a in _adam_ada(c_all, dmod_s, w_ada[0], m_w_ada[0], v_w_ada[0])]

    out = [loss, grad_x[None]]
    for field in range(4):
        out += [res[k][field] for k in ORDER]
    return tuple(out)
```

```python
import jax
import jax.numpy as jnp
from jax import lax
from jax.experimental import pallas as pl
from jax.experimental.pallas import tpu as pltpu

F32 = jnp.float32
BF16 = jnp.bfloat16
MESH = pl.DeviceIdType.MESH
HIGHEST = lax.Precision.HIGHEST

N_DEV = 8
CHUNK = 64
N_PAST = 8
QBLK = 4 * CHUNK
KPAD = N_PAST * CHUNK
WIN = KPAD + QBLK
TAB = 1024
ATTN_HEAD_DIM = 64
ATTN_HEADS_PER_STEP = 4
SUB = 32
ROWS = 8
LANE = 128
EPS = 1e-5
ALPHA = 2.0 ** 0.25
ADAM_LR, ADAM_B1, ADAM_B2, ADAM_EPS, ADAM_WD, ADAM_STEP = 0.001, 0.9, 0.999, 1e-08, 0.01, 10
NEG = -1e30
ROW_TILE = 512
VMEM_LIMIT = 56 * 1024 * 1024


def _sds(shape, dtype):
    return jax.ShapeDtypeStruct(tuple(shape), dtype)


def _tile(n, pref, mult):
    best = None
    for t in range(mult, min(n, pref) + 1, mult):
        if n % t == 0:
            best = t
    return n if best is None else best


def _params(sem=None, big=False):
    kw = {}
    if sem is not None:
        kw["dimension_semantics"] = sem
    if big:
        kw["vmem_limit_bytes"] = VMEM_LIMIT
    return pltpu.CompilerParams(**kw)


def _sigmoid(v):
    return 1.0 / (1.0 + jnp.exp(-v))


def _dot(a, b, dims, precision=None):
    return lax.dot_general(a, b, (dims, ((), ())), preferred_element_type=F32, precision=precision)


NN = ((1,), (0,))
NT = ((1,), (1,))
TN = ((0,), (0,))


def _ln(v):
    mu = jnp.mean(v, axis=-1, keepdims=True)
    d = v - mu
    rstd = lax.rsqrt(jnp.mean(d * d, axis=-1, keepdims=True) + EPS)
    return d * rstd, rstd


def _ln_bwd(dxh, xh, rstd):
    return rstd * (dxh - jnp.mean(dxh, axis=-1, keepdims=True) - xh * jnp.mean(dxh * xh, axis=-1, keepdims=True))


def _colsum(v):
    return jnp.sum(v, axis=0, keepdims=True)


def _ln_mod(x2, mod6):
    T, D = x2.shape
    tm = _tile(T, ROW_TILE, 8)

    def body(x_ref, mod_ref, o_ref):
        xh, _ = _ln(x_ref[...])
        o_ref[...] = (xh * (1.0 + mod_ref[1:2, :]) + mod_ref[0:1, :]).astype(BF16)

    return pl.pallas_call(
        body, grid=(T // tm,), name="ln_mod",
        in_specs=[pl.BlockSpec((tm, D), lambda i: (i, 0)), pl.BlockSpec((6, D), lambda i: (0, 0))],
        out_specs=pl.BlockSpec((tm, D), lambda i: (i, 0)),
        out_shape=_sds((T, D), BF16), compiler_params=_params(("parallel",), big=True),
    )(x2, mod6)


def _mid_fwd(x2, mix, mod6, ln1_g, ln1_b):
    T, D = x2.shape
    tm = _tile(T, ROW_TILE, 8)

    def body(x_ref, mix_ref, mod_ref, g_ref, b_ref, x1_ref, h2_ref):
        zh, _ = _ln(ALPHA * x_ref[...] + mod_ref[2:3, :] * mix_ref[...])
        x1 = zh * g_ref[...] + b_ref[...]
        x1_ref[...] = x1
        xh, _ = _ln(x1)
        h2_ref[...] = (xh * (1.0 + mod_ref[4:5, :]) + mod_ref[3:4, :]).astype(BF16)

    row = pl.BlockSpec((tm, D), lambda i: (i, 0))
    vec = pl.BlockSpec((1, D), lambda i: (0, 0))
    return pl.pallas_call(
        body, grid=(T // tm,), name="mid_fwd",
        in_specs=[row, row, pl.BlockSpec((6, D), lambda i: (0, 0)), vec, vec],
        out_specs=[row, row],
        out_shape=[_sds((T, D), F32), _sds((T, D), BF16)], compiler_params=_params(("parallel",), big=True),
    )(x2, mix, mod6, ln1_g, ln1_b)


def _final(x1, ff, mod6, ln2_g, ln2_b, tgt):
    T, D = x1.shape
    tm = _tile(T, ROW_TILE, 8)

    def body(x1_ref, ff_ref, mod_ref, g_ref, b_ref, t_ref, dff_ref, dx1_ref, vec_ref):
        @pl.when(pl.program_id(0) == 0)
        def _():
            vec_ref[...] = jnp.zeros_like(vec_ref)

        ff_v = ff_ref[...]
        gate2 = mod_ref[5:6, :]
        zh, rstd = _ln(ALPHA * x1_ref[...] + gate2 * ff_v)
        err = zh * g_ref[...] + b_ref[...] - t_ref[...]
        dy = err * (1.0 / D)
        dz = _ln_bwd(dy * g_ref[...], zh, rstd)
        dff_ref[...] = (gate2 * dz).astype(BF16)
        dx1_ref[...] = ALPHA * dz
        vec_ref[0:1, :] += _colsum(dy * zh)
        vec_ref[1:2, :] += _colsum(dy)
        vec_ref[2:3, :] += _colsum(dz * ff_v)
        vec_ref[3:4, :] += _colsum(err * err) * (0.5 / D)

    row = pl.BlockSpec((tm, D), lambda i: (i, 0))
    vec = pl.BlockSpec((1, D), lambda i: (0, 0))
    return pl.pallas_call(
        body, grid=(T // tm,), name="final_fwd_bwd",
        in_specs=[row, row, pl.BlockSpec((6, D), lambda i: (0, 0)), vec, vec, row],
        out_specs=[row, row, pl.BlockSpec((8, D), lambda i: (0, 0))],
        out_shape=[_sds((T, D), BF16), _sds((T, D), F32), _sds((8, D), F32)],
        compiler_params=_params(("arbitrary",), big=True),
    )(x1, ff, mod6, ln2_g, ln2_b, tgt)


def _mid_bwd(x2, mix, x1, dx1a, dh2, mod6, ln1_g):
    T, D = x2.shape
    tm = _tile(T, ROW_TILE // 2, 8)

    def body(x_ref, mix_ref, x1_ref, dx1a_ref, dh2_ref, mod_ref, g_ref, dmix_ref, dxa_ref, vec_ref):
        @pl.when(pl.program_id(0) == 0)
        def _():
            vec_ref[...] = jnp.zeros_like(vec_ref)

        dh2 = dh2_ref[...]
        xh, rstd = _ln(x1_ref[...])
        dx1 = dx1a_ref[...] + _ln_bwd(dh2 * (1.0 + mod_ref[4:5, :]), xh, rstd)
        mix_v = mix_ref[...]
        gate1 = mod_ref[2:3, :]
        zh, rstdz = _ln(ALPHA * x_ref[...] + gate1 * mix_v)
        dz = _ln_bwd(dx1 * g_ref[...], zh, rstdz)
        dmix_ref[...] = (gate1 * dz).astype(BF16)
        dxa_ref[...] = ALPHA * dz
        vec_ref[0:1, :] += _colsum(dh2 * xh)
        vec_ref[1:2, :] += _colsum(dh2)
        vec_ref[2:3, :] += _colsum(dx1 * zh)
        vec_ref[3:4, :] += _colsum(dx1)
        vec_ref[4:5, :] += _colsum(dz * mix_v)

    row = pl.BlockSpec((tm, D), lambda i: (i, 0))
    vec = pl.BlockSpec((1, D), lambda i: (0, 0))
    return pl.pallas_call(
        body, grid=(T // tm,), name="mid_bwd",
        in_specs=[row, row, row, row, row, pl.BlockSpec((6, D), lambda i: (0, 0)), vec],
        out_specs=[row, row, pl.BlockSpec((8, D), lambda i: (0, 0))],
        out_shape=[_sds((T, D), BF16), _sds((T, D), F32), _sds((8, D), F32)],
        compiler_params=_params(("arbitrary",), big=True),
    )(x2, mix, x1, dx1a, dh2, mod6, ln1_g)


def _first_bwd(x2, dh1, dxa, mod6):
    T, D = x2.shape
    tm = _tile(T, ROW_TILE, 8)

    def body(x_ref, dh1_ref, dxa_ref, mod_ref, gx_ref, vec_ref):
        @pl.when(pl.program_id(0) == 0)
        def _():
            vec_ref[...] = jnp.zeros_like(vec_ref)

        dh1 = dh1_ref[...]
        xh, rstd = _ln(x_ref[...])
        gx_ref[...] = dxa_ref[...] + _ln_bwd(dh1 * (1.0 + mod_ref[1:2, :]), xh, rstd)
        vec_ref[0:1, :] += _colsum(dh1 * xh)
        vec_ref[1:2, :] += _colsum(dh1)

    row = pl.BlockSpec((tm, D), lambda i: (i, 0))
    return pl.pallas_call(
        body, grid=(T // tm,), name="first_bwd",
        in_specs=[row, row, row, pl.BlockSpec((6, D), lambda i: (0, 0))],
        out_specs=[row, pl.BlockSpec((8, D), lambda i: (0, 0))],
        out_shape=[_sds((T, D), F32), _sds((8, D), F32)],
        compiler_params=_params(("arbitrary",), big=True),
    )(x2, dh1, dxa, mod6)


def _slot(j):
    return (j % 2) * 4 + j // 2


def _mm_gathered(a, wg, shards, out, name):
    M, K = a.shape
    _, _, ns = wg.shape
    tm = _tile(M, 512, 16)

    def body(shards_ref, a_ref, w_ref, prev_ref, o_ref):
        o_ref[...] = _dot(a_ref[...], w_ref[...], NN)

    return pl.pallas_call(
        body, name=name,
        grid_spec=pltpu.PrefetchScalarGridSpec(
            num_scalar_prefetch=1, grid=(shards.shape[0], M // tm),
            in_specs=[pl.BlockSpec((tm, K), lambda j, i, s: (i, 0)),
                      pl.BlockSpec((None, K, ns), lambda j, i, s: (s[j], 0, 0)), ORDER_ONLY],
            out_specs=pl.BlockSpec((tm, ns), lambda j, i, s: (i, s[j]))),
        out_shape=_sds((M, N_DEV * ns), F32), input_output_aliases={3: 0},
        compiler_params=_params(("parallel", "parallel"), big=True),
    )(shards, a, wg, out)


def _mm_nn(a, b, name):
    M, K = a.shape
    _, N = b.shape
    tm, tn = _tile(M, 512, 16), _tile(N, 1024, LANE)

    def body(a_ref, b_ref, o_ref):
        o_ref[...] = _dot(a_ref[...], b_ref[...], NN)

    return pl.pallas_call(
        body, grid=(N // tn, M // tm), name=name,
        in_specs=[pl.BlockSpec((tm, K), lambda j, i: (i, 0)), pl.BlockSpec((K, tn), lambda j, i: (0, j))],
        out_specs=pl.BlockSpec((tm, tn), lambda j, i: (i, j)),
        out_shape=_sds((M, N), F32), compiler_params=_params(("parallel", "parallel"), big=True),
    )(a, b)


def _mm_nt(a, b, name):
    M, K = a.shape
    N, _ = b.shape
    tm, tn = _tile(M, 512, 16), _tile(N, 1024, LANE)

    def body(a_ref, b_ref, o_ref):
        o_ref[...] = _dot(a_ref[...], b_ref[...], NT)

    return pl.pallas_call(
        body, grid=(M // tm, N // tn), name=name,
        in_specs=[pl.BlockSpec((tm, K), lambda i, j: (i, 0)), pl.BlockSpec((tn, K), lambda i, j: (j, 0))],
        out_specs=pl.BlockSpec((tm, tn), lambda i, j: (i, j)),
        out_shape=_sds((M, N), F32), compiler_params=_params(("parallel", "parallel"), big=True),
    )(a, b)


def _mm_swiglu(h2, wg):
    M, K = h2.shape
    _, _, ns = wg.shape
    half = N_DEV // 2
    tm = _tile(M, 256, 16)

    def body(a_ref, wgate_ref, wup_ref, gu_ref, act_ref):
        a = a_ref[...]
        g = _dot(a, wgate_ref[...], NN)
        u = _dot(a, wup_ref[...], NN)
        sg = _sigmoid(g)
        silu = g * sg
        gu_ref[0] = u * (sg * (1.0 + g * (1.0 - sg)))
        gu_ref[1] = silu
        act_ref[...] = (silu * u).astype(BF16)

    return pl.pallas_call(
        body, grid=(half, M // tm), name="ffn_in_swiglu",
        in_specs=[pl.BlockSpec((tm, K), lambda j, i: (i, 0)),
                  pl.BlockSpec((None, K, ns), lambda j, i: (j, 0, 0)),
                  pl.BlockSpec((None, K, ns), lambda j, i: (j + half, 0, 0))],
        out_specs=[pl.BlockSpec((2, tm, ns), lambda j, i: (0, i, j)), pl.BlockSpec((tm, ns), lambda j, i: (i, j))],
        out_shape=[_sds((2, M, half * ns), F32), _sds((M, half * ns), BF16)],
        compiler_params=_params(("parallel", "parallel"), big=True),
    )(h2, wg, wg)


def _mm_swiglu_bwd(dff, w2, gu):
    M, K = dff.shape
    F = w2.shape[0]
    tm, tn = _tile(M, 512, 16), _tile(F, 1408, LANE)

    def body(a_ref, b_ref, gu_ref, du_ref):
        da = _dot(a_ref[...], b_ref[...], NT)
        du_ref[0] = (da * gu_ref[0]).astype(BF16)
        du_ref[1] = (da * gu_ref[1]).astype(BF16)

    return pl.pallas_call(
        body, grid=(F // tn, M // tm), name="ffn_out_bwd_swiglu",
        in_specs=[pl.BlockSpec((tm, K), lambda j, i: (i, 0)), pl.BlockSpec((tn, K), lambda j, i: (j, 0)),
                  pl.BlockSpec((2, tm, tn), lambda j, i: (0, i, j))],
        out_specs=pl.BlockSpec((2, tm, tn), lambda j, i: (0, i, j)),
        out_shape=_sds((2, M, F), BF16), compiler_params=_params(("parallel", "parallel"), big=True),
    )(dff, w2, gu)


ORDER_ONLY = pl.BlockSpec(memory_space=pl.ANY)


def _mm_tn_rows(dep, a, b, rs, name):
    M, Ka = a.shape
    _, N = b.shape

    def body(_, a_ref, b_ref, o_ref):
        g = _dot(a_ref[...], b_ref[...], TN)
        o_ref[0, 0] = g[0:rs, :].astype(BF16)
        o_ref[1, 0] = g[rs:2 * rs, :].astype(BF16)

    return pl.pallas_call(
        body, grid=(N_DEV // 2,), name=name,
        in_specs=[ORDER_ONLY, pl.BlockSpec((M, 2 * rs), lambda ch: (0, ch)), pl.BlockSpec((M, N), lambda ch: (0, 0))],
        out_specs=pl.BlockSpec((2, 1, rs, N), lambda ch: (0, ch, 0, 0)),
        out_shape=_sds((2, N_DEV // 2, rs, N), BF16),
        compiler_params=_params(("parallel",), big=True),
    )(dep, a, b)


def _mm_gathered_nt(dep, a, a_spec, wg, M, tm, name, first=0, count=None, out=None):
    _, K, ns = wg.shape
    count = M // tm if count is None else count
    out = lax.empty((M, K), F32) if out is None else out

    def body(_, a_ref, w_ref, prev_ref, o_ref):
        @pl.when(pl.program_id(1) == 0)
        def _():
            o_ref[...] = jnp.zeros_like(o_ref)

        o_ref[...] += _dot(a_ref[:, 0:ns], w_ref[0], NT) + _dot(a_ref[:, ns:2 * ns], w_ref[1], NT)

    return pl.pallas_call(
        body, grid=(count, N_DEV // 2), name=name,
        in_specs=[ORDER_ONLY, a_spec(tm, 2 * ns, first), pl.BlockSpec((2, K, ns), lambda i, p: (p, 0, 0)), ORDER_ONLY],
        out_specs=pl.BlockSpec((tm, K), lambda i, j: (i + first, 0)),
        out_shape=_sds((M, K), F32), input_output_aliases={3: 0},
        compiler_params=_params(("parallel", "arbitrary"), big=True),
    )(dep, a, wg, out)


def _mm_tn_gathered(dep, h, a, a_spec, ns, name):
    M, K = h.shape

    def body(_, h_ref, a_ref, o_ref):
        o_ref[...] = _dot(h_ref[...], a_ref[...], TN).astype(BF16)

    return pl.pallas_call(
        body, grid=(N_DEV,), name=name,
        in_specs=[ORDER_ONLY, pl.BlockSpec((M, K), lambda j: (0, 0)), a_spec(M, ns)],
        out_specs=pl.BlockSpec((None, K, ns), lambda j: (_slot(j), 0, 0)),
        out_shape=_sds((N_DEV, K, ns), BF16),
        compiler_params=_params(("parallel",), big=True),
    )(dep, h, a)


def _bias_onehot(rbp, max_rel):
    r = lax.broadcasted_iota(jnp.int32, (rbp, TAB), 0)
    m = lax.broadcasted_iota(jnp.int32, (rbp, TAB), 1)
    dist = KPAD - jnp.where(m < WIN, m, m - TAB)
    return (r == jnp.clip(dist, -max_rel, max_rel) + max_rel).astype(F32)


def _attn_setup(i, hp, k_ref, v_ref, gv_ref, kpad, vpad, bias):
    ls = slice(i * ATTN_HEAD_DIM, (i + 1) * ATTN_HEAD_DIM)
    kpad[i][0:KPAD, :] = jnp.zeros((KPAD, ATTN_HEAD_DIM), BF16)
    vpad[i][0:KPAD, :] = jnp.zeros((KPAD, ATTN_HEAD_DIM), BF16)
    kpad[i][KPAD:, :] = k_ref[:, ls].astype(BF16)
    vpad[i][KPAD:, :] = v_ref[:, ls].astype(BF16)
    gvrow = gv_ref[pl.ds(hp * ATTN_HEADS_PER_STEP + i, 1), :]
    tab = pltpu.roll(jnp.broadcast_to(gvrow, (QBLK, TAB)), 0, 1, stride=1, stride_axis=0)
    row = lax.broadcasted_iota(jnp.int32, (QBLK, WIN), 0)
    col = lax.broadcasted_iota(jnp.int32, (QBLK, WIN), 1)
    first = jnp.bitwise_and(row, -CHUNK)
    seen = jnp.logical_and(col >= first, col < first + (N_PAST + 1) * CHUNK)
    bias[i][...] = jnp.where(seen, tab[:, 0:WIN], NEG)


def _attn_probs(b, q_ref, kpad, vpad, bias, col):
    pair = range(ATTN_HEADS_PER_STEP)
    ls = [slice(i * ATTN_HEAD_DIM, (i + 1) * ATTN_HEAD_DIM) for i in pair]
    r0 = pl.multiple_of(b * QBLK, QBLK)
    q = [q_ref[pl.ds(r0, QBLK), ls[i]].astype(BF16) for i in pair]
    kw = [kpad[i][pl.ds(r0, WIN), :] for i in pair]
    vw = [vpad[i][pl.ds(r0, WIN), :] for i in pair]
    s = [_dot(q[i], kw[i], NT) * (ATTN_HEAD_DIM ** -0.5) + bias[i][...] for i in pair]
    s = [jnp.where(col >= KPAD - r0, s[i], NEG) for i in pair]
    p = [jnp.exp(s[i] - jnp.max(s[i], axis=-1, keepdims=True)) for i in pair]
    pn = [p[i] / jnp.sum(p[i], axis=-1, keepdims=True) for i in pair]
    return r0, ls, q, kw, vw, pn


def _attn_fwd(proj, gv, ga, AW):
    T = proj.shape[0]
    AH = ATTN_HEADS_PER_STEP
    W = AH * ATTN_HEAD_DIM
    HP = AW // W

    def body(q_ref, k_ref, v_ref, gv_ref, ga_ref, o_ref, *scratch):
        kpad, vpad, bias = (scratch[k * AH:(k + 1) * AH] for k in range(3))
        hp = pl.program_id(0)
        for i in range(AH):
            _attn_setup(i, hp, k_ref, v_ref, gv_ref, kpad, vpad, bias)
        col = lax.broadcasted_iota(jnp.int32, (QBLK, WIN), 1)

        def block(b, carry):
            pair = range(AH)
            r0, ls, _, _, vw, pn = _attn_probs(b, q_ref, kpad, vpad, bias, col)
            o = [_dot(pn[i].astype(BF16), vw[i], NN) for i in pair]
            r = [lax.rsqrt(jnp.mean(o[i] * o[i], axis=-1, keepdims=True) + EPS) for i in pair]
            outs = [o[i] * r[i] * ga_ref[0:1, ls[i]] for i in pair]
            o_ref[pl.ds(r0, QBLK), :] = jnp.concatenate(outs, axis=1).astype(BF16)
            return carry

        lax.fori_loop(0, T // QBLK, block, 0)

    blk = lambda off: pl.BlockSpec((T, W), lambda hp: (0, off + hp))
    return pl.pallas_call(
        body, grid=(HP,), name="attn_fwd",
        in_specs=[blk(0), blk(HP), blk(2 * HP), pl.BlockSpec(gv.shape, lambda hp: (0, 0)),
                  pl.BlockSpec((1, W), lambda hp: (0, hp))],
        out_specs=pl.BlockSpec((T, W), lambda hp: (0, hp)),
        out_shape=_sds((T, AW), BF16),
        scratch_shapes=[pltpu.VMEM((T + KPAD, ATTN_HEAD_DIM), BF16)] * (2 * AH) + [pltpu.VMEM((QBLK, WIN), F32)] * AH,
        compiler_params=_params(("parallel",), big=True),
    )(proj, proj, proj, gv, ga)


def _attn_bwd(proj, dmixin, gv, ga, AW):
    T = proj.shape[0]
    AH = ATTN_HEADS_PER_STEP
    W = AH * ATTN_HEAD_DIM
    HP = AW // W
    scale = ATTN_HEAD_DIM ** -0.5

    def body(q_ref, k_ref, v_ref, dn_ref, gv_ref, ga_ref, dq_ref, dk_ref, dv_ref, dgv_ref, dga_ref, *scratch):
        kpad, vpad, dkacc, dvacc, bias, dbias = (scratch[k * AH:(k + 1) * AH] for k in range(6))
        hp = pl.program_id(0)
        for i in range(AH):
            _attn_setup(i, hp, k_ref, v_ref, gv_ref, kpad, vpad, bias)
            dkacc[i][...] = jnp.zeros_like(dkacc[i])
            dvacc[i][...] = jnp.zeros_like(dvacc[i])
            dbias[i][...] = jnp.zeros_like(dbias[i])
        dga_ref[...] = jnp.zeros_like(dga_ref)
        col = lax.broadcasted_iota(jnp.int32, (QBLK, WIN), 1)

        def block(b, carry):
            pair = range(AH)
            r0, lss, qs, kws, vws, pns = _attn_probs(b, q_ref, kpad, vpad, bias, col)
            pn_b = [pns[i].astype(BF16) for i in pair]
            o = [_dot(pn_b[i], vws[i], NN) for i in pair]
            r = [lax.rsqrt(jnp.mean(o[i] * o[i], axis=-1, keepdims=True) + EPS) for i in pair]
            dn = [dn_ref[pl.ds(r0, QBLK), lss[i]] for i in pair]
            for i in pair:
                dga_ref[i:i + 1, :] += _colsum(dn[i] * o[i] * r[i])
            a = [dn[i] * ga_ref[0:1, lss[i]] for i in pair]
            do_b = [(r[i] * (a[i] - o[i] * (r[i] * r[i]) * jnp.mean(a[i] * o[i], axis=-1, keepdims=True))).astype(BF16)
                    for i in pair]
            dp = [_dot(do_b[i], vws[i], NT) for i in pair]
            for i in pair:
                dvacc[i][pl.ds(r0, WIN), :] += _dot(pn_b[i], do_b[i], TN)
            ds = [pns[i] * (dp[i] - jnp.sum(pns[i] * dp[i], axis=-1, keepdims=True)) for i in pair]
            for i in pair:
                dbias[i][...] += ds[i]
            ds_b = [ds[i].astype(BF16) for i in pair]
            dq = [_dot(ds_b[i], kws[i], NN) * scale for i in pair]
            dq_ref[pl.ds(r0, QBLK), :] = jnp.concatenate(dq, axis=1).astype(BF16)
            for i in pair:
                dkacc[i][pl.ds(r0, WIN), :] += _dot(ds_b[i], qs[i], TN) * scale
            return carry

        lax.fori_loop(0, T // QBLK, block, 0)

        rr = lax.broadcasted_iota(jnp.int32, (QBLK, QBLK), 0)
        cc = lax.broadcasted_iota(jnp.int32, (QBLK, QBLK), 1)
        flip = (rr + cc == QBLK - 1).astype(BF16)
        for i in range(AH):
            ls = slice(i * ATTN_HEAD_DIM, (i + 1) * ATTN_HEAD_DIM)
            dk_ref[:, ls] = dkacc[i][KPAD:, :].astype(BF16)
            dv_ref[:, ls] = dvacc[i][KPAD:, :].astype(BF16)
            full = jnp.concatenate([dbias[i][...], jnp.zeros((QBLK, TAB - WIN), F32)], axis=1)
            hi = full.astype(BF16)
            lo = (full - hi.astype(F32)).astype(BF16)
            rev = _dot(flip, hi, NN) + _dot(flip, lo, NN)
            dgv_ref[i:i + 1, :] = _colsum(pltpu.roll(rev, TAB - (QBLK - 1), 1, stride=1, stride_axis=0))

    blk = lambda off: pl.BlockSpec((T, W), lambda hp: (0, off + hp))
    accs = lambda dt: [pltpu.VMEM((T + KPAD, ATTN_HEAD_DIM), dt)] * AH
    return pl.pallas_call(
        body, grid=(HP,), name="attn_bwd",
        in_specs=[blk(0), blk(HP), blk(2 * HP), blk(0), pl.BlockSpec(gv.shape, lambda hp: (0, 0)),
                  pl.BlockSpec((1, W), lambda hp: (0, hp))],
        out_specs=[blk(0), blk(0), blk(0), pl.BlockSpec((None, AH, TAB), lambda hp: (hp, 0, 0)),
                   pl.BlockSpec((None, AH, ATTN_HEAD_DIM), lambda hp: (hp, 0, 0))],
        out_shape=[_sds((T, AW), BF16), _sds((T, AW), BF16), _sds((T, AW), BF16),
                   _sds((HP, AH, TAB), F32), _sds((HP, AH, ATTN_HEAD_DIM), F32)],
        scratch_shapes=accs(BF16) + accs(BF16) + accs(F32) + accs(F32) + [pltpu.VMEM((QBLK, WIN), F32)] * (2 * AH),
        compiler_params=_params(("parallel",), big=True),
    )(proj, proj, proj, dmixin, gv, ga)


def _ltri():
    r = lax.broadcasted_iota(jnp.int32, (CHUNK, CHUNK), 0)
    c = lax.broadcasted_iota(jnp.int32, (CHUNK, CHUNK), 1)
    return (c <= r).astype(BF16)


def _tri_dot(tri, v, dims):
    hi = v.astype(BF16)
    lo = (v - hi.astype(F32)).astype(BF16)
    return _dot(tri, hi, dims) + _dot(tri, lo, dims)


HEADS_PER_STEP = 2


def _alternate(stages):
    live = list(stages)
    while live:
        for g in list(live):
            if next(g, StopIteration) is StopIteration:
                live.remove(g)


def _hgrn_gates(n, ls, q_ref, f_ref, lb_ref, ltri):
    r0 = pl.multiple_of(n * CHUNK, CHUNK)
    rows = pl.ds(r0, CHUNK)
    lb = lb_ref[:, ls]
    qb = q_ref[rows, ls]
    sg = _sigmoid(f_ref[rows, ls])
    f = lb + (1.0 - lb) * sg
    sq = _sigmoid(qb)
    b = _tri_dot(ltri, jnp.log(f), NN)
    return rows, lb, qb, sg, f, 1.0 - f, sq, qb * sq, b


def _hgrn_specs(T, RW, AW):
    HG = HEADS_PER_STEP
    W = HG * LANE
    base = 3 * AW // W
    blk_in = lambda off: pl.BlockSpec((T, W), lambda g: (0, base + off + g))
    col = pl.BlockSpec((T, W), lambda g: (0, g))
    return HG, W, RW // W, blk_in, col


def _hgrn_fwd(proj, lb, gn, AW, RW):
    T = proj.shape[0]
    RH, NC, NSUB = RW // LANE, T // CHUNK, CHUNK // SUB
    HG, W, NG, blk_in, col = _hgrn_specs(T, RW, AW)

    def body(q_ref, f_ref, i_ref, g_ref, lb_ref, gn_ref, mix_ref, o_ref, stall_ref, st_all, bs_all, kks_all, ics_all):
        st_all[...] = jnp.zeros_like(st_all)
        ltri = _ltri()
        rowi = lax.broadcasted_iota(jnp.int32, (SUB, 1), 0)

        def one_head(h, n):
            ls = slice(h * LANE, (h + 1) * LANE)
            st, bs, kks, ics = st_all.at[h], bs_all.at[h], kks_all.at[h], ics_all.at[h]
            rows, _, _, _, _, kk, _, qs, b = _hgrn_gates(n, ls, q_ref, f_ref, lb_ref, ltri)
            ic = i_ref[rows, ls]
            stv = st[...]
            stall_ref[h, n] = stv
            bs[...] = b
            kks[...] = kk
            ics[...] = ic
            yield
            o = _dot((qs * jnp.exp(b)).astype(BF16), stv.astype(BF16), NT)
            yield
            ic_b = ic.astype(BF16)
            pieces = []
            for blk in range(NSUB):
                s0 = blk * SUB
                bI, qI = b[s0:s0 + SUB], qs[s0:s0 + SUB]
                if blk == 0:
                    oI = jnp.zeros((SUB, LANE), F32)
                else:
                    ref = bs[s0 - 1:s0, :]
                    qt = (qI * jnp.exp(bI - ref)).astype(BF16)
                    kt = (kk[0:s0] * jnp.exp(ref - b[0:s0])).astype(BF16)
                    oI = _dot(_dot(qt, kt, NT).astype(BF16), ic_b[0:s0], NN)
                    yield
                acc = [oI[g * ROWS:(g + 1) * ROWS] for g in range(SUB // ROWS)]
                for s in range(SUB):
                    sr = s0 + s
                    g0 = s // ROWS
                    lo = g0 * ROWS
                    e = jnp.exp(jnp.minimum(bI[lo:] - bs[sr:sr + 1, :], 0.0))
                    a = jnp.sum(qI[lo:] * kks[sr:sr + 1, :] * e, axis=-1, keepdims=True)
                    add = jnp.where(rowi[lo:] >= s, a, 0.0) * ics[sr:sr + 1, :]
                    for g in range(g0, SUB // ROWS):
                        acc[g] = acc[g] + add[(g - g0) * ROWS:(g - g0 + 1) * ROWS]
                    yield
                pieces.extend(acc)
            o = o + jnp.concatenate(pieces, axis=0)
            bl = bs[CHUNK - 1:CHUNK, :]
            kd = (kk * jnp.exp(bl - b)).astype(BF16)
            st[...] = stv * jnp.exp(bl) + _dot(ic_b, kd, TN)
            yield
            o_ref[rows, ls] = o
            r = lax.rsqrt(jnp.mean(o * o, axis=-1, keepdims=True) + EPS)
            gb = g_ref[rows, ls]
            mix_ref[rows, ls] = (o * r * gn_ref[...] * (gb * _sigmoid(gb))).astype(BF16)

        def chunk(n, carry):
            _alternate([one_head(h, n) for h in range(HG)])
            return carry

        lax.fori_loop(0, NC, chunk, 0)

    tile = pltpu.VMEM((HG, CHUNK, LANE), F32)
    return pl.pallas_call(
        body, grid=(NG,), name="hgrn_fwd",
        in_specs=[blk_in(0), blk_in(NG), blk_in(2 * NG), blk_in(3 * NG), pl.BlockSpec((1, W), lambda g: (0, g)),
                  pl.BlockSpec((1, LANE), lambda g: (0, 0))],
        out_specs=[col, col, pl.BlockSpec((HG, NC, LANE, LANE), lambda g: (g, 0, 0, 0))],
        out_shape=[_sds((T, RW), BF16), _sds((T, RW), F32), _sds((RH, NC, LANE, LANE), F32)],
        scratch_shapes=[pltpu.VMEM((HG, LANE, LANE), F32), tile, tile, tile],
        compiler_params=_params(("parallel",), big=True),
    )(proj, proj, proj, proj, lb, gn)


def _hgrn_bwd(proj, dmixin, o_b, st_all, lb, gn, AW, RW):
    T = proj.shape[0]
    RH, NC, NSUB = RW // LANE, T // CHUNK, CHUNK // SUB
    HG, W, NG, blk_in, col = _hgrn_specs(T, RW, AW)

    def body(q_ref, f_ref, i_ref, g_ref, o_ref, dn_ref, stall_ref, lb_ref, gn_ref,
             dq_ref, df_ref, di_ref, dg_ref, dlb_ref, dgn_ref, dst_all, bs_all, qss_all, dos_all, p2_all, dic_all,
             p1_all):
        dst_all[...] = jnp.zeros_like(dst_all)
        dlb_ref[...] = jnp.zeros_like(dlb_ref)
        dgn_ref[...] = jnp.zeros_like(dgn_ref)
        ltri = _ltri()
        rowi = lax.broadcasted_iota(jnp.int32, (SUB, 1), 0)
        last = lax.broadcasted_iota(jnp.int32, (CHUNK, 1), 0) == CHUNK - 1

        def one_head(h, n):
            ls = slice(h * LANE, (h + 1) * LANE)
            dst, bs, qss, dos = dst_all.at[h], bs_all.at[h], qss_all.at[h], dos_all.at[h]
            p2, dic, p1s = p2_all.at[h], dic_all.at[h], p1_all.at[h]
            rows, lbv, qb, sg, f, kk, sq, qs, b = _hgrn_gates(n, ls, q_ref, f_ref, lb_ref, ltri)
            ic = i_ref[rows, ls]
            stv = stall_ref[h, n]
            dstv = dst[...]
            o = o_ref[rows, ls]
            dn = dn_ref[rows, ls]
            gb = g_ref[rows, ls]
            sgb = _sigmoid(gb)
            r = lax.rsqrt(jnp.mean(o * o, axis=-1, keepdims=True) + EPS)
            gnv = gn_ref[...]
            dg_ref[rows, ls] = (dn * (o * r * gnv) * (sgb * (1.0 + gb * (1.0 - sgb)))).astype(BF16)
            dy = dn * (gb * sgb)
            dgn_ref[h] += _colsum(dy * o * r)
            a_ = dy * gnv
            do = r * (a_ - o * (r * r) * jnp.mean(a_ * o, axis=-1, keepdims=True))
            do_b = do.astype(BF16)
            bs[...] = b
            qss[...] = qs
            dos[...] = do
            yield
            ic_b = ic.astype(BF16)
            eb = jnp.exp(b)
            bl = bs[CHUNK - 1:CHUNK, :]
            ebl = jnp.exp(bl)
            dec = jnp.exp(bl - b)
            kd = (kk * dec).astype(BF16)
            dst_b = dstv.astype(BF16)
            dqs = _dot(do_b, stv.astype(BF16), NN) * eb
            dkk2 = _dot(ic_b, dst_b, NN) * dec
            dic[...] = _dot(kd, dst_b, NT)
            dbl = ebl * _colsum(stv * dstv) + _colsum(kk * dkk2)
            dst[...] = dstv * ebl + _dot(do_b, (qs * eb).astype(BF16), TN)
            yield
            p2[...] = jnp.zeros_like(p2)
            p1_pieces = []
            for blk in range(NSUB):
                s0 = blk * SUB
                bI, qI, doI = b[s0:s0 + SUB], qs[s0:s0 + SUB], do[s0:s0 + SUB]
                if blk == 0:
                    p1 = jnp.zeros((SUB, LANE), F32)
                else:
                    ref = bs[s0 - 1:s0, :]
                    eq = jnp.exp(bI - ref)
                    ek = jnp.exp(ref - b[0:s0])
                    qt = (qI * eq).astype(BF16)
                    kt = (kk[0:s0] * ek).astype(BF16)
                    doI_b = doI.astype(BF16)
                    dic[0:s0, :] += _dot(_dot(qt, kt, NT).astype(BF16), doI_b, TN)
                    da = _dot(doI_b, ic_b[0:s0], NT).astype(BF16)
                    p1 = _dot(da, kt, NN) * eq
                    p2[0:s0, :] += _dot(da, qt, TN) * ek
                    yield
                p1_pieces.append(p1)
                kkI, icI = kk[s0:s0 + SUB], ic[s0:s0 + SUB]
                p2acc = [jnp.zeros((ROWS, LANE), F32) for _ in range(SUB // ROWS)]
                diacc = [jnp.zeros((ROWS, LANE), F32) for _ in range(SUB // ROWS)]
                for t in range(SUB):
                    tr = s0 + t
                    ng = t // ROWS + 1
                    hi = ng * ROWS
                    keep = rowi[:hi] <= t
                    do_t = dos[tr:tr + 1, :]
                    e = jnp.exp(jnp.minimum(bs[tr:tr + 1, :] - bI[:hi], 0.0))
                    qe = qss[tr:tr + 1, :] * e
                    a = jnp.where(keep, jnp.sum(kkI[:hi] * qe, axis=-1, keepdims=True), 0.0)
                    da = jnp.where(keep, jnp.sum(icI[:hi] * do_t, axis=-1, keepdims=True), 0.0)
                    dp2, ddi = da * qe, a * do_t
                    for g in range(ng):
                        p2acc[g] = p2acc[g] + dp2[g * ROWS:(g + 1) * ROWS]
                        diacc[g] = diacc[g] + ddi[g * ROWS:(g + 1) * ROWS]
                    p1s[tr:tr + 1, :] = _colsum(da * kkI[:hi] * e)
                    yield
                p2[s0:s0 + SUB, :] += jnp.concatenate(p2acc, axis=0)
                dic[s0:s0 + SUB, :] += jnp.concatenate(diacc, axis=0)
            dqs = dqs + jnp.concatenate(p1_pieces, axis=0) + p1s[...]
            dkk = dkk2 + p2[...]
            db = qs * dqs - kk * dkk + jnp.where(last, dbl, 0.0)
            dgl = _tri_dot(ltri, db, TN)
            yield
            dfv = dgl / f - dkk
            df_ref[rows, ls] = (dfv * (1.0 - lbv) * sg * (1.0 - sg)).astype(BF16)
            dlb_ref[:, ls] += _colsum(dfv * (1.0 - sg))
            dq_ref[rows, ls] = (dqs * (sq * (1.0 + qb * (1.0 - sq)))).astype(BF16)
            di_ref[rows, ls] = dic[...].astype(BF16)

        def chunk(k, carry):
            _alternate([one_head(h, NC - 1 - k) for h in range(HG)])
            return carry

        lax.fori_loop(0, NC, chunk, 0)

    tile = pltpu.VMEM((HG, CHUNK, LANE), F32)
    return pl.pallas_call(
        body, grid=(NG,), name="hgrn_bwd",
        in_specs=[blk_in(0), blk_in(NG), blk_in(2 * NG), blk_in(3 * NG), col,
                  pl.BlockSpec((T, W), lambda g: (0, AW // W + g)),
                  pl.BlockSpec((HG, NC, LANE, LANE), lambda g: (g, 0, 0, 0)),
                  pl.BlockSpec((1, W), lambda g: (0, g)), pl.BlockSpec((1, LANE), lambda g: (0, 0))],
        out_specs=[col, col, col, col, pl.BlockSpec((1, W), lambda g: (0, g)),
                   pl.BlockSpec((HG, 1, LANE), lambda g: (g, 0, 0))],
        out_shape=[_sds((T, RW), BF16)] * 4 + [_sds((1, RW), F32), _sds((RH, 1, LANE), F32)],
        scratch_shapes=[pltpu.VMEM((HG, LANE, LANE), F32), tile, tile, tile, tile, tile, tile],
        compiler_params=_params(("parallel",), big=True),
    )(proj, proj, proj, proj, o_b, dmixin, st_all, lb, gn)


def _prep(c, lb_logits, rb_pad, max_rel, after):
    D, RW = c.shape[-1], lb_logits.shape[-1]
    H, rbp = rb_pad.shape

    def body(c_ref, l_ref, rb_ref, _, __, cact_ref, lb_ref, gv_ref):
        cv = c_ref[...]
        cact_ref[...] = cv * _sigmoid(cv)
        lb_ref[...] = _sigmoid(l_ref[0:1, :] - l_ref[1:2, :])
        gv_ref[...] = _dot(rb_ref[...], _bias_onehot(rbp, max_rel), NN, HIGHEST)

    vmem = pl.BlockSpec(memory_space=pltpu.VMEM)
    return pl.pallas_call(
        body, name="prep", in_specs=[vmem, vmem, vmem, ORDER_ONLY, ORDER_ONLY],
        out_shape=[_sds((1, D), F32), _sds((1, RW), F32), _sds((H, TAB), F32)],
    )(c, lb_logits, rb_pad, *after)


def _mod_part(c_all, w_ada_s, b_ada_s):
    B, D = c_all.shape
    ns = w_ada_s.shape[1]
    tn = _tile(ns, 768, LANE)

    def body(c_ref, w_ref, b_ref, o_ref):
        o_ref[...] = _dot(c_ref[...], w_ref[...], NN) + b_ref[...]

    return pl.pallas_call(
        body, grid=(ns // tn,), name="mod_part",
        in_specs=[pl.BlockSpec((B, D), lambda j: (0, 0)), pl.BlockSpec((D, tn), lambda j: (0, j)),
                  pl.BlockSpec((1, tn), lambda j: (0, j))],
        out_specs=pl.BlockSpec((B, tn), lambda j: (0, j)),
        out_shape=_sds((B, ns), F32), compiler_params=_params(("parallel",)),
    )(c_all, w_ada_s, b_ada_s)


def _adam(w, g, m, v):
    m = ADAM_B1 * m + (1.0 - ADAM_B1) * g
    v = ADAM_B2 * v + (1.0 - ADAM_B2) * (g * g)
    m_hat = m * (1.0 / (1.0 - ADAM_B1 ** ADAM_STEP))
    v_hat = v * (1.0 / (1.0 - ADAM_B2 ** ADAM_STEP))
    return -ADAM_LR * (m_hat / (jnp.sqrt(v_hat) + ADAM_EPS) + ADAM_WD * w), m, v


def _adam_ada(c_all, dmod_s, w, m, v):
    B, D = c_all.shape
    ns = w.shape[1]
    tr, tn = _tile(D, 512, LANE), _tile(ns, 768, LANE)

    def body(c_ref, d_ref, w_ref, m_ref, v_ref, g_out, dw_out, m_out, v_out):
        g = _dot(c_ref[...], d_ref[...], TN)
        g_out[...] = g
        dw_out[...], m_out[...], v_out[...] = _adam(w_ref[...], g, m_ref[...], v_ref[...])

    big = pl.BlockSpec((tr, tn), lambda i, j: (i, j))
    return pl.pallas_call(
        body, grid=(D // tr, ns // tn), name="adam_w_ada",
        in_specs=[pl.BlockSpec((B, tr), lambda i, j: (0, i)), pl.BlockSpec((B, tn), lambda i, j: (0, j)),
                  big, big, big],
        out_specs=[big] * 4, out_shape=[_sds((D, ns), F32)] * 4,
        compiler_params=_params(("parallel", "parallel")),
    )(c_all, dmod_s, w, m, v)


def _adam_shard(parts, w, m, v, name):
    R, C = w.shape
    tr = _tile(R, 256, 16)

    def body(p_ref, w_ref, m_ref, v_ref, g_out, dw_out, m_out, v_out):
        g = p_ref[0].astype(F32)
        for k in range(1, N_DEV // 2):
            g = g + p_ref[k].astype(F32)
        g_out[...] = g
        dw_out[...], m_out[...], v_out[...] = _adam(w_ref[...], g, m_ref[...], v_ref[...])

    big = pl.BlockSpec((tr, C), lambda i: (i, 0))
    return pl.pallas_call(
        body, grid=(R // tr,), name=name,
        in_specs=[pl.BlockSpec((N_DEV // 2, tr, C), lambda i: (0, i, 0)), big, big, big],
        out_specs=[big] * 4, out_shape=[_sds((R, C), F32)] * 4,
        compiler_params=_params(("parallel",), big=True),
    )(parts, w, m, v)


def _pair_sum(g8, land, core, name):
    _, NCHIP, R, C = g8.shape
    tr = _tile(R, 1024, 16)

    def body(core_ref, g_ref, l_ref, o_ref):
        o_ref[...] = g_ref[...] + l_ref[...]

    return pl.pallas_call(
        body, name=name,
        grid_spec=pltpu.PrefetchScalarGridSpec(
            num_scalar_prefetch=1, grid=(NCHIP, R // tr),
            in_specs=[pl.BlockSpec((None, None, tr, C), lambda k, i, core_ref: (core_ref[0], k, i, 0)),
                      pl.BlockSpec((None, tr, C), lambda k, i, core_ref: (k, i, 0))],
            out_specs=pl.BlockSpec((None, tr, C), lambda k, i, core_ref: (k, i, 0))),
        out_shape=_sds((NCHIP, R, C), BF16), compiler_params=_params(("parallel", "parallel")),
    )(core, g8, land)


SMALL = ("b_ada", "rel_bias", "attn_norm_g", "lb_logits", "gnorm_g", "ln1_g", "ln1_b", "ln2_g", "ln2_b")


def _small_update(parts, loss_parts, lbv, ws, ms, vs, max_rel):
    n = len(SMALL)

    def body(*refs):
        part_refs = dict(zip(SMALL, refs[:n]))
        loss_in, lb_ref = refs[n], refs[n + 1]
        w_refs, m_refs, v_refs = refs[n + 2:2 * n + 2], refs[2 * n + 2:3 * n + 2], refs[3 * n + 2:4 * n + 2]
        outs = refs[4 * n + 2:]

        def total(ref):
            tot = ref[0]
            for k in range(1, N_DEV):
                tot = tot + ref[k]
            return tot

        outs[0][...] = jnp.sum(total(loss_in), axis=-1, keepdims=True)
        for idx, name in enumerate(SMALL):
            g = total(part_refs[name])
            if name == "rel_bias":
                g = _dot(g, _bias_onehot(w_refs[idx].shape[1], max_rel), NT, HIGHEST)
            elif name == "lb_logits":
                lb = lb_ref[...]
                sign = (1 - 2 * lax.broadcasted_iota(jnp.int32, (2, 1), 0)).astype(F32)
                g = sign * (g * lb * (1.0 - lb))
            elif name == "gnorm_g":
                g = _colsum(g)
            dw, mm, vv = _adam(w_refs[idx][...], g, m_refs[idx][...], v_refs[idx][...])
            outs[1 + 4 * idx][...] = g
            outs[2 + 4 * idx][...] = dw
            outs[3 + 4 * idx][...] = mm
            outs[4 + 4 * idx][...] = vv

    out_shape = [_sds((1, 1), F32)]
    for w in ws:
        out_shape += [_sds(w.shape, F32)] * 4
    return pl.pallas_call(body, name="small_update", out_shape=out_shape, compiler_params=_params(big=True))(
        *[parts[k] for k in SMALL], loss_parts, lbv, *ws, *ms, *vs)


def _place():
    x, y, c = lax.axis_index("x"), lax.axis_index("y"), lax.axis_index("c")
    return x, y, c, [(1 - x, y), (x, 1 - y), (1 - x, 1 - y)]


def _all_gather(shard, name):
    HBM = pl.BlockSpec(memory_space=pl.ANY)

    def body(x_ref, out_ref, send_sems, recv_sems, local_sem):
        x, y, c, chips = _place()
        me, sibling = (x, y, c), (x, y, 1 - c)

        def slot(px, py, pc):
            return out_ref.at[4 * px + 2 * py + pc]

        def copy(k, block, to, src=None):
            return pltpu.make_async_remote_copy(
                src_ref=slot(*block) if src is None else src, dst_ref=slot(*block),
                send_sem=send_sems.at[k], recv_sem=recv_sems.at[k], device_id=to, device_id_type=MESH)

        mine = pltpu.make_async_copy(x_ref, slot(*me), local_sem)
        mine.start()
        first = [copy(0, me, sibling, src=x_ref)]
        first += [copy(1 + j, me, (*chip, c), src=x_ref) for j, chip in enumerate(chips)]
        for cp in first:
            cp.start()
        passed = [copy(4 + j, (*chip, c), sibling) for j, chip in enumerate(chips)]
        for j, chip in enumerate(chips):
            copy(1 + j, (*chip, c), me).wait_recv()
            passed[j].start()
        copy(0, sibling, me).wait_recv()
        for j, chip in enumerate(chips):
            copy(4 + j, (*chip, 1 - c), me).wait_recv()
        for cp in first + passed:
            cp.wait_send()
        mine.wait()

    return pl.pallas_call(
        body, name=name, out_shape=_sds((N_DEV,) + shard.shape, shard.dtype),
        in_specs=[HBM], out_specs=HBM,
        scratch_shapes=[pltpu.SemaphoreType.DMA((7,)), pltpu.SemaphoreType.DMA((7,)), pltpu.SemaphoreType.DMA(())],
    )(shard)


SEM_SPEC = pl.BlockSpec(memory_space=pltpu.SEMAPHORE)
HBM_SPEC = pl.BlockSpec(memory_space=pltpu.HBM)
EFFECT = pltpu.SideEffectType.DATAFLOW_SIDE_EFFECTING


def _remote(src, dst, send_sems, recv_sems, k, dev):
    return pltpu.make_async_remote_copy(src_ref=src, dst_ref=dst, send_sem=send_sems.at[k], recv_sem=recv_sems.at[k],
                                        device_id=dev, device_id_type=MESH)


def _copy_start(name, bufs, plan, n, after, only=None):
    nb = len(bufs)

    def body(*refs):
        send_sems, recv_sems = refs[nb + 1], refs[nb + 2]
        for k, (src, dst, dev) in enumerate(plan(*refs[:nb])):
            if only is not None and k not in only:
                continue
            _remote(src, dst, send_sems, recv_sems, k, dev).start()
        refs[-1][...] = jnp.zeros_like(refs[-1])

    out = pl.pallas_call(
        body, name=name,
        out_shape=(pltpu.SemaphoreType.DMA((n,)), pltpu.SemaphoreType.DMA((n,)),
                   *[pltpu.HBM(b.shape, b.dtype) for b in bufs], _sds((8, LANE), F32)),
        in_specs=[HBM_SPEC] * nb + [ORDER_ONLY],
        out_specs=(SEM_SPEC, SEM_SPEC, *[HBM_SPEC] * nb, pl.BlockSpec(memory_space=pltpu.VMEM)),
        input_output_aliases={i: 2 + i for i in range(nb)},
        compiler_params=pltpu.CompilerParams(has_side_effects=EFFECT),
    )(*[pltpu.with_memory_space_constraint(b, pltpu.HBM) for b in bufs], after)
    return (out[0], out[1]), list(out[2:2 + nb]), out[-1]


def _copy_wait(name, sems, bufs, plan, after, only=None):
    nb = len(bufs)

    def body(*refs):
        send_sems, recv_sems = refs[nb], refs[nb + 1]
        for k, (src, dst, dev) in enumerate(plan(*refs[:nb])):
            if only is not None and k not in only:
                continue
            cp = _remote(src, dst, send_sems, recv_sems, k, dev)
            cp.wait_send()
            cp.wait_recv()

    out = pl.pallas_call(
        body, name=name, out_shape=tuple(pltpu.HBM(b.shape, b.dtype) for b in bufs),
        in_specs=[HBM_SPEC] * nb + [SEM_SPEC, SEM_SPEC, pl.BlockSpec(memory_space=pl.ANY)],
        out_specs=tuple([HBM_SPEC] * nb), input_output_aliases={i: i for i in range(nb)},
        compiler_params=pltpu.CompilerParams(has_side_effects=EFFECT),
    )(*bufs, sems[0], sems[1], after)
    return list(out)


def _ag_plan_chips(shard_ref, out_ref):
    x, y, c, chips = _place()
    mine = out_ref.at[4 * x + 2 * y + c]
    return [(shard_ref, mine, (x, y, 1 - c))] + [(shard_ref, mine, (*chip, c)) for chip in chips]


def _ag_plan_pass(out_ref):
    x, y, c, chips = _place()
    slots = [out_ref.at[4 * chip[0] + 2 * chip[1] + c] for chip in chips]
    return [(s, s, (x, y, 1 - c)) for s in slots]


def _rs_plan_pair(g_ref, land_ref):
    x, y, c, _ = _place()
    return [(g_ref.at[1 - c], land_ref, (x, y, 1 - c))]


def _rs_plan_chips(p_ref, land_ref):
    x, y, c, chips = _place()
    return [(p_ref.at[2 * chip[0] + chip[1]], land_ref.at[2 * x + y], (*chip, c)) for chip in chips]


class _Gather:
    @staticmethod
    def landing(shard, me):
        return lax.dynamic_update_slice(lax.empty((N_DEV,) + shard.shape, shard.dtype), shard[None],
                                        (me,) + (0,) * shard.ndim)

    def __init__(self, shard, out, tag, after):
        self.tag = tag
        self.sems, (self.shard, self.out), self.token = _copy_start(
            "ag_start_" + tag, [shard, out], _ag_plan_chips, 4, after)
        self.groups = []

    def arrived(self, after, copies):
        name = "ag_wait_%s_%s" % (self.tag, "".join(map(str, copies)))
        self.shard, self.out = _copy_wait(name, self.sems, [self.shard, self.out], _ag_plan_chips, after, copies)
        return self.out

    def pass_on(self, after, blocks):
        name = "ag_pass_%s_%s" % (self.tag, "".join(map(str, blocks)))
        sems, (self.out,), _ = _copy_start(name, [self.out], _ag_plan_pass, 3, after, blocks)
        self.groups.append((sems, blocks))
        return self.out

    def passed(self, after, group):
        sems, blocks = self.groups[group]
        name = "ag_pass_wait_%s_%s" % (self.tag, "".join(map(str, blocks)))
        self.out = _copy_wait(name, sems, [self.out], _ag_plan_pass, after, blocks)[0]
        return self.out

    def arrived_from_chips(self, after):
        self.arrived(after, (0, 1, 2, 3))
        return self.pass_on(after, (0, 1, 2))

    def passed_on(self, after):
        return self.passed(after, 0)


def _ag_plan_direct(src_ref, out_ref):
    x, y, c, chips = _place()
    mine = out_ref.at[4 * x + 2 * y + c]
    peers = [(x, y, 1 - c)] + [(*chip, pc) for chip in chips for pc in (c, 1 - c)]
    return [(src_ref, mine, peer) for peer in peers]


class _SmallGather:
    def __init__(self, block, me, tag):
        self.tag = tag
        out = lax.dynamic_update_slice(lax.empty((N_DEV,) + block.shape, block.dtype), block[None],
                                       (me,) + (0,) * block.ndim)
        self.sems, self.bufs, self.token = _copy_start(
            "ag_direct_start_" + tag, [block, out], _ag_plan_direct, N_DEV - 1, jnp.zeros((1,), F32))

    def done(self, after):
        return _copy_wait("ag_direct_wait_" + self.tag, self.sems, self.bufs, _ag_plan_direct, after)[1]


class _ReduceScatter:
    def __init__(self, g8, tag):
        self.tag = tag
        land = lax.empty(g8.shape[1:], g8.dtype)
        self.sems, self.bufs, self.token = _copy_start(
            "rs_pair_start_" + tag, [g8, land], _rs_plan_pair, 1, jnp.zeros((1,), F32))

    def pair_done(self, core, chip, after):
        g8, land = _copy_wait("rs_pair_wait_" + self.tag, self.sems, self.bufs, _rs_plan_pair, after)
        p4 = _pair_sum(g8, land, core, "rs_pair_sum_" + self.tag)
        own = lax.dynamic_slice_in_dim(p4, chip, 1, axis=0)
        land2 = lax.dynamic_update_slice(lax.empty(p4.shape, p4.dtype), own, (chip, 0, 0))
        self.sems, self.bufs, self.token = _copy_start(
            "rs_chips_start_" + self.tag, [p4, land2], _rs_plan_chips, 3, jnp.zeros((1,), F32))

    def sums(self, after):
        return _copy_wait("rs_chips_wait_" + self.tag, self.sems, self.bufs, _rs_plan_chips, after)[1]


ORDER = ("w_ada", "b_ada", "w_in", "rel_bias", "attn_norm_g", "lb_logits", "gnorm_g", "w_o", "ln1_g", "ln1_b",
         "w_ffn_in", "w_ffn_out", "ln2_g", "ln2_b")


def kernel(x, c, w_ada, b_ada, w_in, rel_bias, attn_norm_g, lb_logits, gnorm_g, w_o, ln1_g, ln1_b, w_ffn_in, w_ffn_out, ln2_g, ln2_b, loss_target, m_w_ada, m_b_ada, m_w_in, m_rel_bias, m_attn_norm_g, m_lb_logits, m_gnorm_g, m_w_o, m_ln1_g, m_ln1_b, m_w_ffn_in, m_w_ffn_out, m_ln2_g, m_ln2_b, v_w_ada, v_b_ada, v_w_in, v_rel_bias, v_attn_norm_g, v_lb_logits, v_gnorm_g, v_w_o, v_ln1_g, v_ln1_b, v_w_ffn_in, v_w_ffn_out, v_ln2_g, v_ln2_b):
    W = dict(w_ada=w_ada, b_ada=b_ada, w_in=w_in, rel_bias=rel_bias, attn_norm_g=attn_norm_g, lb_logits=lb_logits,
             gnorm_g=gnorm_g, w_o=w_o, ln1_g=ln1_g, ln1_b=ln1_b, w_ffn_in=w_ffn_in, w_ffn_out=w_ffn_out,
             ln2_g=ln2_g, ln2_b=ln2_b)
    M = dict(w_ada=m_w_ada, b_ada=m_b_ada, w_in=m_w_in, rel_bias=m_rel_bias, attn_norm_g=m_attn_norm_g,
             lb_logits=m_lb_logits, gnorm_g=m_gnorm_g, w_o=m_w_o, ln1_g=m_ln1_g, ln1_b=m_ln1_b,
             w_ffn_in=m_w_ffn_in, w_ffn_out=m_w_ffn_out, ln2_g=m_ln2_g, ln2_b=m_ln2_b)
    V = dict(w_ada=v_w_ada, b_ada=v_b_ada, w_in=v_w_in, rel_bias=v_rel_bias, attn_norm_g=v_attn_norm_g,
             lb_logits=v_lb_logits, gnorm_g=v_gnorm_g, w_o=v_w_o, ln1_g=v_ln1_g, ln1_b=v_ln1_b,
             w_ffn_in=v_w_ffn_in, w_ffn_out=v_w_ffn_out, ln2_g=v_ln2_g, ln2_b=v_ln2_b)

    x2, tgt = x[0], loss_target[0]
    T, D = x2.shape
    AW, RW = attn_norm_g.shape[-1], lb_logits.shape[-1]
    MIX = AW + RW
    H, RH = AW // ATTN_HEAD_DIM, RW // LANE
    RB = rel_bias.shape[-1]
    max_rel = (RB - 1) // 2
    rbp = -(-RB // LANE) * LANE
    F = w_ffn_out.shape[1] * N_DEV
    half = N_DEV // 2
    xi, yi, ci = lax.axis_index("x"), lax.axis_index("y"), lax.axis_index("c")
    me = 4 * xi + 2 * yi + ci
    core = jnp.reshape(ci, (1,)).astype(jnp.int32)
    pad_rb = lambda a: jnp.pad(a[0], ((0, 0), (0, rbp - RB)))

    chip = 2 * xi + yi

    w_in_b = w_in[0].astype(BF16)
    w_in_land = _Gather.landing(w_in_b, me)
    c_act, lbv, gv = _prep(c, lb_logits, pad_rb(rel_bias), max_rel, (w_in_b, w_in_land))
    c_all = _all_gather(c_act, "ag_c").reshape(N_DEV, D)
    ns_ada = w_ada.shape[-1]
    mod_part = _mod_part(c_all, w_ada[0], lax.dynamic_slice_in_dim(b_ada, me * ns_ada, ns_ada, axis=1))
    mod_all = _all_gather(mod_part, "ag_mod")
    mod6 = lax.dynamic_index_in_dim(mod_all, me, axis=1, keepdims=False).reshape(6, D)

    bf = lambda w: w[0].astype(BF16)
    ag_in = _Gather(w_in_b, w_in_land, "w_in", mod_all)
    ag_o = _Gather(bf(w_o), _Gather.landing(bf(w_o), me), "w_o", ag_in.token)
    ag_f1 = _Gather(bf(w_ffn_in), _Gather.landing(bf(w_ffn_in), me), "w_ffn_in", ag_o.token)
    ag_f2 = _Gather(bf(w_ffn_out), _Gather.landing(bf(w_ffn_out), me), "w_ffn_out", ag_f1.token)

    h1 = _ln_mod(x2, mod6 + ag_f2.token[0, 0])
    ids = lambda pairs: jnp.stack([4 * px + 2 * py + pc for px, py, pc in pairs]).astype(jnp.int32)
    others = [(1 - xi, yi), (xi, 1 - yi), (1 - xi, 1 - yi)]
    proj = lax.empty((T, w_in.shape[-1] * N_DEV), F32)
    proj = _mm_gathered(h1, ag_in.arrived(h1, (0,)), ids([(xi, yi, ci), (xi, yi, 1 - ci)]), proj, "in_proj_a")
    ag_in.arrived(proj, (1, 2, 3))
    proj = _mm_gathered(h1, ag_in.pass_on(proj, (0, 1, 2)), ids([(*ch, ci) for ch in others]), proj, "in_proj_b")
    wg_in = ag_in.passed(proj, 0)
    proj = _mm_gathered(h1, wg_in, ids([(*ch, 1 - ci) for ch in others]), proj, "in_proj_c")
    ag_o.arrived_from_chips(proj)
    mix_a = _attn_fwd(proj, gv, attn_norm_g, AW)
    wg_o = ag_o.passed_on(mix_a).reshape(MIX, D)
    mix_b, o_b, st_all = _hgrn_fwd(proj, lbv, gnorm_g, AW, RW)
    mixin = jnp.concatenate([mix_a, mix_b], axis=1)
    mix = _mm_nn(mixin, wg_o, "out_proj")
    ag_f1.arrived_from_chips(mix)
    x1, h2 = _mid_fwd(x2, mix, mod6, ln1_g, ln1_b)
    wg_f1 = ag_f1.passed_on(h2)
    gu, act = _mm_swiglu(h2, wg_f1)
    ag_f2.arrived_from_chips(act)
    wg_f2 = ag_f2.passed_on(act).reshape(F, D)
    ff = _mm_nn(act, wg_f2, "ffn_out")
    dff, dx1a, vec_a = _final(x1, ff, mod6, ln2_g, ln2_b, tgt)

    du = _mm_swiglu_bwd(dff, wg_f2, gu)
    rs_f2 = _ReduceScatter(_mm_tn_rows(dff, act, dff, F // N_DEV, "grad_w_ffn_out"), "w_ffn_out")
    tm = _tile(T, 512, 16)
    du_ij = lambda tm_, w, first: pl.BlockSpec((None, tm_, w), lambda i, p: (p // (half // 2), i + first, p % (half // 2)))
    du_j = lambda rows, ns: pl.BlockSpec((None, rows, ns), lambda j: (j // half, 0, j % half))
    dh2 = _mm_gathered_nt(rs_f2.token, du, du_ij, wg_f1, T, tm, "ffn_in_bwd")
    rs_f2.pair_done(core, chip, dh2)
    gw_f1 = _mm_tn_gathered(rs_f2.token, h2, du, du_j, wg_f1.shape[-1], "grad_w_ffn_in")
    rs_f1 = _ReduceScatter(gw_f1.reshape(2, half, D, -1), "w_ffn_in")
    dmix, dxa, vec_b = _mid_bwd(x2, mix, x1, dx1a, dh2, mod6 + rs_f1.token[0, 0], ln1_g)
    dmixin = _mm_nt(dmix, wg_o, "out_proj_bwd")
    rs_f1.pair_done(core, chip, dmixin)
    rs_o = _ReduceScatter(_mm_tn_rows(rs_f1.token, mixin, dmix, MIX // N_DEV, "grad_w_o"), "w_o")
    dq, dk, dv, dgv, dga = _attn_bwd(proj, dmixin, gv + rs_o.token[0, 0], attn_norm_g, AW)
    rs_o.pair_done(core, chip, dq)
    dqb, dfl, dib, dgb, dlb, dgn = _hgrn_bwd(proj, dmixin, o_b, st_all, lbv + rs_o.token[0, 0], gnorm_g, AW, RW)
    dproj = jnp.concatenate([dq, dk, dv, dqb, dfl, dib, dgb], axis=1)
    p_ij = lambda tm_, w, first: pl.BlockSpec((tm_, w), lambda i, p: (i + first, p))
    p_j = lambda rows, ns: pl.BlockSpec((rows, ns), lambda j: (0, j))
    gw_in = _mm_tn_gathered(rs_o.token, h1, dproj, p_j, wg_in.shape[-1], "grad_w_in")
    rs_in = _ReduceScatter(gw_in.reshape(2, half, D, -1), "w_in")
    n_tiles = T // tm
    dh1 = _mm_gathered_nt(rs_in.token, dproj, p_ij, wg_in, T, tm, "in_proj_bwd_a", 0, n_tiles // 2)
    rs_in.pair_done(core, chip, dh1)
    dh1 = _mm_gathered_nt(rs_in.token, dproj, p_ij, wg_in, T, tm, "in_proj_bwd_b", n_tiles // 2,
                          n_tiles - n_tiles // 2, dh1)
    grad_x, vec_c = _first_bwd(x2, dh1, dxa, mod6)

    dmod = jnp.concatenate([vec_c[1:2], vec_c[0:1], vec_b[4:5], vec_b[1:2], vec_b[0:1], vec_a[2:3]], axis=0)
    pieces = dict(b_ada=dmod, rel_bias=dgv, attn_norm_g=dga, lb_logits=dlb, gnorm_g=dgn, ln1_g=vec_b[2:3],
                  ln1_b=vec_b[3:4], ln2_g=vec_a[0:1], ln2_b=vec_a[1:2], loss=vec_a[3:4])
    widths = dict(b_ada=(1, 6 * D), rel_bias=(H, TAB), attn_norm_g=(1, AW), lb_logits=(1, RW), gnorm_g=(RH, LANE),
                  ln1_g=(1, D), ln1_b=(1, D), ln2_g=(1, D), ln2_b=(1, D), loss=(1, D))
    packed = jnp.concatenate([pieces[k].reshape(-1, LANE) for k in widths], axis=0)
    small_ag = _SmallGather(packed, me, "small")
    after, res_big = small_ag.token, {}
    for k, rs in (("w_ffn_out", rs_f2), ("w_ffn_in", rs_f1), ("w_o", rs_o), ("w_in", rs_in)):
        four = _adam_shard(rs.sums(after), W[k][0], M[k][0], V[k][0], "adam_" + k)
        res_big[k] = [a[None] for a in four]
        after = four[0]
    gathered = small_ag.done(after)
    parts, r0 = {}, 0
    for k, (rows, width) in widths.items():
        nr = rows * width // LANE
        parts[k] = gathered[:, r0:r0 + nr, :].reshape(N_DEV, rows, width)
        r0 += nr
    prep_small = lambda d, k: pad_rb(d[k]) if k == "rel_bias" else d[k]
    small = _small_update(parts, parts["loss"], lbv, [prep_small(W, k) for k in SMALL],
                          [prep_small(M, k) for k in SMALL], [prep_small(V, k) for k in SMALL], max_rel)
    loss = small[0].reshape(())
    res = {}
    for idx, k in enumerate(SMALL):
        four = small[1 + 4 * idx:5 + 4 * idx]
        if k == "rel_bias":
            four = [a[:, :RB][None] for a in four]
        res[k] = list(four)

    res.update(res_big)
    dmod_s = lax.dynamic_slice_in_dim(parts["b_ada"].reshape(N_DEV, 6 * D), me * ns_ada, ns_ada, axis=1)
    res["w_ada"] = [a[None] for a in _adam_ada(c_all, dmod_s, w_ada[0], m_w_ada[0], v_w_ada[0])]

    out = [loss, grad_x[None]]
    for field in range(4):
        out += [res[k][field] for k in ORDER]
    return tuple(out)
```

```python
import jax
import jax.numpy as jnp
from jax import lax
from jax.experimental import pallas as pl
from jax.experimental.pallas import tpu as pltpu

F32 = jnp.float32
BF16 = jnp.bfloat16
MESH = pl.DeviceIdType.MESH
HIGHEST = lax.Precision.HIGHEST

N_DEV = 8
CHUNK = 64
N_PAST = 8
QBLK = 4 * CHUNK
KPAD = N_PAST * CHUNK
WIN = KPAD + QBLK
TAB = 1024
ATTN_HEAD_DIM = 64
ATTN_HEADS_PER_STEP = 4
SUB = 32
ROWS = 8
LANE = 128
EPS = 1e-5
ALPHA = 2.0 ** 0.25
ADAM_LR, ADAM_B1, ADAM_B2, ADAM_EPS, ADAM_WD, ADAM_STEP = 0.001, 0.9, 0.999, 1e-08, 0.01, 10
NEG = -1e30
VMEM_LIMIT = 56 * 1024 * 1024


def _sds(shape, dtype):
    return jax.ShapeDtypeStruct(tuple(shape), dtype)


def _tile(n, pref, mult):
    best = None
    for t in range(mult, min(n, pref) + 1, mult):
        if n % t == 0:
            best = t
    return n if best is None else best


def _params(sem=None, big=False):
    kw = {}
    if sem is not None:
        kw["dimension_semantics"] = sem
    if big:
        kw["vmem_limit_bytes"] = VMEM_LIMIT
    return pltpu.CompilerParams(**kw)


def _sigmoid(v):
    return 1.0 / (1.0 + jnp.exp(-v))


def _dot(a, b, dims, precision=None):
    return lax.dot_general(a, b, (dims, ((), ())), preferred_element_type=F32, precision=precision)


NN = ((1,), (0,))
NT = ((1,), (1,))
TN = ((0,), (0,))


def _ln(v):
    mu = jnp.mean(v, axis=-1, keepdims=True)
    d = v - mu
    rstd = lax.rsqrt(jnp.mean(d * d, axis=-1, keepdims=True) + EPS)
    return d * rstd, rstd


def _ln_bwd(dxh, xh, rstd):
    return rstd * (dxh - jnp.mean(dxh, axis=-1, keepdims=True) - xh * jnp.mean(dxh * xh, axis=-1, keepdims=True))


def _colsum(v):
    return jnp.sum(v, axis=0, keepdims=True)


def _ln_mod(x2, mod6):
    T, D = x2.shape
    tm = _tile(T, 256, 8)

    def body(x_ref, mod_ref, o_ref):
        xh, _ = _ln(x_ref[...])
        o_ref[...] = (xh * (1.0 + mod_ref[1:2, :]) + mod_ref[0:1, :]).astype(BF16)

    return pl.pallas_call(
        body, grid=(T // tm,), name="ln_mod",
        in_specs=[pl.BlockSpec((tm, D), lambda i: (i, 0)), pl.BlockSpec((6, D), lambda i: (0, 0))],
        out_specs=pl.BlockSpec((tm, D), lambda i: (i, 0)),
        out_shape=_sds((T, D), BF16), compiler_params=_params(("parallel",)),
    )(x2, mod6)


def _mid_fwd(x2, mix, mod6, ln1_g, ln1_b):
    T, D = x2.shape
    tm = _tile(T, 256, 8)

    def body(x_ref, mix_ref, mod_ref, g_ref, b_ref, x1_ref, h2_ref):
        zh, _ = _ln(ALPHA * x_ref[...] + mod_ref[2:3, :] * mix_ref[...])
        x1 = zh * g_ref[...] + b_ref[...]
        x1_ref[...] = x1
        xh, _ = _ln(x1)
        h2_ref[...] = (xh * (1.0 + mod_ref[4:5, :]) + mod_ref[3:4, :]).astype(BF16)

    row = pl.BlockSpec((tm, D), lambda i: (i, 0))
    vec = pl.BlockSpec((1, D), lambda i: (0, 0))
    return pl.pallas_call(
        body, grid=(T // tm,), name="mid_fwd",
        in_specs=[row, row, pl.BlockSpec((6, D), lambda i: (0, 0)), vec, vec],
        out_specs=[row, row],
        out_shape=[_sds((T, D), F32), _sds((T, D), BF16)], compiler_params=_params(("parallel",)),
    )(x2, mix, mod6, ln1_g, ln1_b)


def _final(x1, ff, mod6, ln2_g, ln2_b, tgt):
    T, D = x1.shape
    tm = _tile(T, 256, 8)

    def body(x1_ref, ff_ref, mod_ref, g_ref, b_ref, t_ref, dff_ref, dx1_ref, vec_ref):
        @pl.when(pl.program_id(0) == 0)
        def _():
            vec_ref[...] = jnp.zeros_like(vec_ref)

        ff_v = ff_ref[...]
        gate2 = mod_ref[5:6, :]
        zh, rstd = _ln(ALPHA * x1_ref[...] + gate2 * ff_v)
        err = zh * g_ref[...] + b_ref[...] - t_ref[...]
        dy = err * (1.0 / D)
        dz = _ln_bwd(dy * g_ref[...], zh, rstd)
        dff_ref[...] = (gate2 * dz).astype(BF16)
        dx1_ref[...] = ALPHA * dz
        vec_ref[0:1, :] += _colsum(dy * zh)
        vec_ref[1:2, :] += _colsum(dy)
        vec_ref[2:3, :] += _colsum(dz * ff_v)
        vec_ref[3:4, :] += _colsum(err * err) * (0.5 / D)

    row = pl.BlockSpec((tm, D), lambda i: (i, 0))
    vec = pl.BlockSpec((1, D), lambda i: (0, 0))
    return pl.pallas_call(
        body, grid=(T // tm,), name="final_fwd_bwd",
        in_specs=[row, row, pl.BlockSpec((6, D), lambda i: (0, 0)), vec, vec, row],
        out_specs=[row, row, pl.BlockSpec((8, D), lambda i: (0, 0))],
        out_shape=[_sds((T, D), BF16), _sds((T, D), F32), _sds((8, D), F32)],
        compiler_params=_params(("arbitrary",)),
    )(x1, ff, mod6, ln2_g, ln2_b, tgt)


def _mid_bwd(x2, mix, x1, dx1a, dh2, mod6, ln1_g):
    T, D = x2.shape
    tm = _tile(T, 256, 8)

    def body(x_ref, mix_ref, x1_ref, dx1a_ref, dh2_ref, mod_ref, g_ref, dmix_ref, dxa_ref, vec_ref):
        @pl.when(pl.program_id(0) == 0)
        def _():
            vec_ref[...] = jnp.zeros_like(vec_ref)

        dh2 = dh2_ref[...]
        xh, rstd = _ln(x1_ref[...])
        dx1 = dx1a_ref[...] + _ln_bwd(dh2 * (1.0 + mod_ref[4:5, :]), xh, rstd)
        mix_v = mix_ref[...]
        gate1 = mod_ref[2:3, :]
        zh, rstdz = _ln(ALPHA * x_ref[...] + gate1 * mix_v)
        dz = _ln_bwd(dx1 * g_ref[...], zh, rstdz)
        dmix_ref[...] = (gate1 * dz).astype(BF16)
        dxa_ref[...] = ALPHA * dz
        vec_ref[0:1, :] += _colsum(dh2 * xh)
        vec_ref[1:2, :] += _colsum(dh2)
        vec_ref[2:3, :] += _colsum(dx1 * zh)
        vec_ref[3:4, :] += _colsum(dx1)
        vec_ref[4:5, :] += _colsum(dz * mix_v)

    row = pl.BlockSpec((tm, D), lambda i: (i, 0))
    vec = pl.BlockSpec((1, D), lambda i: (0, 0))
    return pl.pallas_call(
        body, grid=(T // tm,), name="mid_bwd",
        in_specs=[row, row, row, row, row, pl.BlockSpec((6, D), lambda i: (0, 0)), vec],
        out_specs=[row, row, pl.BlockSpec((8, D), lambda i: (0, 0))],
        out_shape=[_sds((T, D), BF16), _sds((T, D), F32), _sds((8, D), F32)],
        compiler_params=_params(("arbitrary",)),
    )(x2, mix, x1, dx1a, dh2, mod6, ln1_g)


def _first_bwd(x2, dh1, dxa, mod6):
    T, D = x2.shape
    tm = _tile(T, 256, 8)

    def body(x_ref, dh1_ref, dxa_ref, mod_ref, gx_ref, vec_ref):
        @pl.when(pl.program_id(0) == 0)
        def _():
            vec_ref[...] = jnp.zeros_like(vec_ref)

        dh1 = dh1_ref[...]
        xh, rstd = _ln(x_ref[...])
        gx_ref[...] = dxa_ref[...] + _ln_bwd(dh1 * (1.0 + mod_ref[1:2, :]), xh, rstd)
        vec_ref[0:1, :] += _colsum(dh1 * xh)
        vec_ref[1:2, :] += _colsum(dh1)

    row = pl.BlockSpec((tm, D), lambda i: (i, 0))
    return pl.pallas_call(
        body, grid=(T // tm,), name="first_bwd",
        in_specs=[row, row, row, pl.BlockSpec((6, D), lambda i: (0, 0))],
        out_specs=[row, pl.BlockSpec((8, D), lambda i: (0, 0))],
        out_shape=[_sds((T, D), F32), _sds((8, D), F32)],
        compiler_params=_params(("arbitrary",)),
    )(x2, dh1, dxa, mod6)


def _slot(j):
    return (j % 2) * 4 + j // 2


def _mm_gathered(a, wg, shards, out, name):
    M, K = a.shape
    _, _, ns = wg.shape
    tm = _tile(M, 512, 16)

    def body(shards_ref, a_ref, w_ref, prev_ref, o_ref):
        o_ref[...] = _dot(a_ref[...], w_ref[...], NN)

    return pl.pallas_call(
        body, name=name,
        grid_spec=pltpu.PrefetchScalarGridSpec(
            num_scalar_prefetch=1, grid=(shards.shape[0], M // tm),
            in_specs=[pl.BlockSpec((tm, K), lambda j, i, s: (i, 0)),
                      pl.BlockSpec((None, K, ns), lambda j, i, s: (s[j], 0, 0)), ORDER_ONLY],
            out_specs=pl.BlockSpec((tm, ns), lambda j, i, s: (i, s[j]))),
        out_shape=_sds((M, N_DEV * ns), F32), input_output_aliases={3: 0},
        compiler_params=_params(("parallel", "parallel"), big=True),
    )(shards, a, wg, out)


def _mm_nn(a, b, name):
    M, K = a.shape
    _, N = b.shape
    tm, tn = _tile(M, 512, 16), _tile(N, 1024, LANE)

    def body(a_ref, b_ref, o_ref):
        o_ref[...] = _dot(a_ref[...], b_ref[...], NN)

    return pl.pallas_call(
        body, grid=(N // tn, M // tm), name=name,
        in_specs=[pl.BlockSpec((tm, K), lambda j, i: (i, 0)), pl.BlockSpec((K, tn), lambda j, i: (0, j))],
        out_specs=pl.BlockSpec((tm, tn), lambda j, i: (i, j)),
        out_shape=_sds((M, N), F32), compiler_params=_params(("parallel", "parallel"), big=True),
    )(a, b)


def _mm_nt(a, b, name):
    M, K = a.shape
    N, _ = b.shape
    tm, tn = _tile(M, 512, 16), _tile(N, 1024, LANE)

    def body(a_ref, b_ref, o_ref):
        o_ref[...] = _dot(a_ref[...], b_ref[...], NT)

    return pl.pallas_call(
        body, grid=(M // tm, N // tn), name=name,
        in_specs=[pl.BlockSpec((tm, K), lambda i, j: (i, 0)), pl.BlockSpec((tn, K), lambda i, j: (j, 0))],
        out_specs=pl.BlockSpec((tm, tn), lambda i, j: (i, j)),
        out_shape=_sds((M, N), F32), compiler_params=_params(("parallel", "parallel"), big=True),
    )(a, b)


def _mm_swiglu(h2, wg):
    M, K = h2.shape
    _, _, ns = wg.shape
    half = N_DEV // 2
    tm = _tile(M, 256, 16)

    def body(a_ref, wgate_ref, wup_ref, gu_ref, act_ref):
        a = a_ref[...]
        g = _dot(a, wgate_ref[...], NN)
        u = _dot(a, wup_ref[...], NN)
        sg = _sigmoid(g)
        silu = g * sg
        gu_ref[0] = u * (sg * (1.0 + g * (1.0 - sg)))
        gu_ref[1] = silu
        act_ref[...] = (silu * u).astype(BF16)

    return pl.pallas_call(
        body, grid=(half, M // tm), name="ffn_in_swiglu",
        in_specs=[pl.BlockSpec((tm, K), lambda j, i: (i, 0)),
                  pl.BlockSpec((None, K, ns), lambda j, i: (j, 0, 0)),
                  pl.BlockSpec((None, K, ns), lambda j, i: (j + half, 0, 0))],
        out_specs=[pl.BlockSpec((2, tm, ns), lambda j, i: (0, i, j)), pl.BlockSpec((tm, ns), lambda j, i: (i, j))],
        out_shape=[_sds((2, M, half * ns), F32), _sds((M, half * ns), BF16)],
        compiler_params=_params(("parallel", "parallel"), big=True),
    )(h2, wg, wg)


def _mm_swiglu_bwd(dff, w2, gu):
    M, K = dff.shape
    F = w2.shape[0]
    tm, tn = _tile(M, 512, 16), _tile(F, 1408, LANE)

    def body(a_ref, b_ref, gu_ref, du_ref):
        da = _dot(a_ref[...], b_ref[...], NT)
        du_ref[0] = (da * gu_ref[0]).astype(BF16)
        du_ref[1] = (da * gu_ref[1]).astype(BF16)

    return pl.pallas_call(
        body, grid=(F // tn, M // tm), name="ffn_out_bwd_swiglu",
        in_specs=[pl.BlockSpec((tm, K), lambda j, i: (i, 0)), pl.BlockSpec((tn, K), lambda j, i: (j, 0)),
                  pl.BlockSpec((2, tm, tn), lambda j, i: (0, i, j))],
        out_specs=pl.BlockSpec((2, tm, tn), lambda j, i: (0, i, j)),
        out_shape=_sds((2, M, F), BF16), compiler_params=_params(("parallel", "parallel"), big=True),
    )(dff, w2, gu)


ORDER_ONLY = pl.BlockSpec(memory_space=pl.ANY)


def _mm_tn_rows(dep, a, b, rs, name):
    M, Ka = a.shape
    _, N = b.shape

    def body(_, a_ref, b_ref, o_ref):
        g = _dot(a_ref[...], b_ref[...], TN)
        o_ref[0, 0] = g[0:rs, :].astype(BF16)
        o_ref[1, 0] = g[rs:2 * rs, :].astype(BF16)

    return pl.pallas_call(
        body, grid=(N_DEV // 2,), name=name,
        in_specs=[ORDER_ONLY, pl.BlockSpec((M, 2 * rs), lambda ch: (0, ch)), pl.BlockSpec((M, N), lambda ch: (0, 0))],
        out_specs=pl.BlockSpec((2, 1, rs, N), lambda ch: (0, ch, 0, 0)),
        out_shape=_sds((2, N_DEV // 2, rs, N), BF16),
        compiler_params=_params(("parallel",), big=True),
    )(dep, a, b)


def _mm_gathered_nt(dep, a, a_spec, wg, M, tm, name, first=0, count=None, out=None):
    _, K, ns = wg.shape
    count = M // tm if count is None else count
    out = lax.empty((M, K), F32) if out is None else out

    def body(_, a_ref, w_ref, prev_ref, o_ref):
        @pl.when(pl.program_id(1) == 0)
        def _():
            o_ref[...] = jnp.zeros_like(o_ref)

        o_ref[...] += _dot(a_ref[:, 0:ns], w_ref[0], NT) + _dot(a_ref[:, ns:2 * ns], w_ref[1], NT)

    return pl.pallas_call(
        body, grid=(count, N_DEV // 2), name=name,
        in_specs=[ORDER_ONLY, a_spec(tm, 2 * ns, first), pl.BlockSpec((2, K, ns), lambda i, p: (p, 0, 0)), ORDER_ONLY],
        out_specs=pl.BlockSpec((tm, K), lambda i, j: (i + first, 0)),
        out_shape=_sds((M, K), F32), input_output_aliases={3: 0},
        compiler_params=_params(("parallel", "arbitrary"), big=True),
    )(dep, a, wg, out)


def _mm_tn_gathered(dep, h, a, a_spec, ns, name):
    M, K = h.shape

    def body(_, h_ref, a_ref, o_ref):
        o_ref[...] = _dot(h_ref[...], a_ref[...], TN).astype(BF16)

    return pl.pallas_call(
        body, grid=(N_DEV,), name=name,
        in_specs=[ORDER_ONLY, pl.BlockSpec((M, K), lambda j: (0, 0)), a_spec(M, ns)],
        out_specs=pl.BlockSpec((None, K, ns), lambda j: (_slot(j), 0, 0)),
        out_shape=_sds((N_DEV, K, ns), BF16),
        compiler_params=_params(("parallel",), big=True),
    )(dep, h, a)


def _bias_onehot(rbp, max_rel):
    r = lax.broadcasted_iota(jnp.int32, (rbp, TAB), 0)
    m = lax.broadcasted_iota(jnp.int32, (rbp, TAB), 1)
    dist = KPAD - jnp.where(m < WIN, m, m - TAB)
    return (r == jnp.clip(dist, -max_rel, max_rel) + max_rel).astype(F32)


def _attn_setup(i, hp, k_ref, v_ref, gv_ref, kpad, vpad, bias):
    ls = slice(i * ATTN_HEAD_DIM, (i + 1) * ATTN_HEAD_DIM)
    kpad[i][0:KPAD, :] = jnp.zeros((KPAD, ATTN_HEAD_DIM), BF16)
    vpad[i][0:KPAD, :] = jnp.zeros((KPAD, ATTN_HEAD_DIM), BF16)
    kpad[i][KPAD:, :] = k_ref[:, ls].astype(BF16)
    vpad[i][KPAD:, :] = v_ref[:, ls].astype(BF16)
    gvrow = gv_ref[pl.ds(hp * ATTN_HEADS_PER_STEP + i, 1), :]
    tab = pltpu.roll(jnp.broadcast_to(gvrow, (QBLK, TAB)), 0, 1, stride=1, stride_axis=0)
    row = lax.broadcasted_iota(jnp.int32, (QBLK, WIN), 0)
    col = lax.broadcasted_iota(jnp.int32, (QBLK, WIN), 1)
    first = jnp.bitwise_and(row, -CHUNK)
    seen = jnp.logical_and(col >= first, col < first + (N_PAST + 1) * CHUNK)
    bias[i][...] = jnp.where(seen, tab[:, 0:WIN], NEG)


def _attn_probs(b, q_ref, kpad, vpad, bias, col):
    pair = range(ATTN_HEADS_PER_STEP)
    ls = [slice(i * ATTN_HEAD_DIM, (i + 1) * ATTN_HEAD_DIM) for i in pair]
    r0 = pl.multiple_of(b * QBLK, QBLK)
    q = [q_ref[pl.ds(r0, QBLK), ls[i]].astype(BF16) for i in pair]
    kw = [kpad[i][pl.ds(r0, WIN), :] for i in pair]
    vw = [vpad[i][pl.ds(r0, WIN), :] for i in pair]
    s = [_dot(q[i], kw[i], NT) * (ATTN_HEAD_DIM ** -0.5) + bias[i][...] for i in pair]
    s = [jnp.where(col >= KPAD - r0, s[i], NEG) for i in pair]
    p = [jnp.exp(s[i] - jnp.max(s[i], axis=-1, keepdims=True)) for i in pair]
    pn = [p[i] / jnp.sum(p[i], axis=-1, keepdims=True) for i in pair]
    return r0, ls, q, kw, vw, pn


def _attn_fwd(proj, gv, ga, AW):
    T = proj.shape[0]
    AH = ATTN_HEADS_PER_STEP
    W = AH * ATTN_HEAD_DIM
    HP = AW // W

    def body(q_ref, k_ref, v_ref, gv_ref, ga_ref, o_ref, *scratch):
        kpad, vpad, bias = (scratch[k * AH:(k + 1) * AH] for k in range(3))
        hp = pl.program_id(0)
        for i in range(AH):
            _attn_setup(i, hp, k_ref, v_ref, gv_ref, kpad, vpad, bias)
        col = lax.broadcasted_iota(jnp.int32, (QBLK, WIN), 1)

        def block(b, carry):
            pair = range(AH)
            r0, ls, _, _, vw, pn = _attn_probs(b, q_ref, kpad, vpad, bias, col)
            o = [_dot(pn[i].astype(BF16), vw[i], NN) for i in pair]
            r = [lax.rsqrt(jnp.mean(o[i] * o[i], axis=-1, keepdims=True) + EPS) for i in pair]
            outs = [o[i] * r[i] * ga_ref[0:1, ls[i]] for i in pair]
            o_ref[pl.ds(r0, QBLK), :] = jnp.concatenate(outs, axis=1).astype(BF16)
            return carry

        lax.fori_loop(0, T // QBLK, block, 0)

    blk = lambda off: pl.BlockSpec((T, W), lambda hp: (0, off + hp))
    return pl.pallas_call(
        body, grid=(HP,), name="attn_fwd",
        in_specs=[blk(0), blk(HP), blk(2 * HP), pl.BlockSpec(gv.shape, lambda hp: (0, 0)),
                  pl.BlockSpec((1, W), lambda hp: (0, hp))],
        out_specs=pl.BlockSpec((T, W), lambda hp: (0, hp)),
        out_shape=_sds((T, AW), BF16),
        scratch_shapes=[pltpu.VMEM((T + KPAD, ATTN_HEAD_DIM), BF16)] * (2 * AH) + [pltpu.VMEM((QBLK, WIN), F32)] * AH,
        compiler_params=_params(("parallel",), big=True),
    )(proj, proj, proj, gv, ga)


def _attn_bwd(proj, dmixin, gv, ga, AW):
    T = proj.shape[0]
    AH = ATTN_HEADS_PER_STEP
    W = AH * ATTN_HEAD_DIM
    HP = AW // W
    scale = ATTN_HEAD_DIM ** -0.5

    def body(q_ref, k_ref, v_ref, dn_ref, gv_ref, ga_ref, dq_ref, dk_ref, dv_ref, dgv_ref, dga_ref, *scratch):
        kpad, vpad, dkacc, dvacc, bias, dbias = (scratch[k * AH:(k + 1) * AH] for k in range(6))
        hp = pl.program_id(0)
        for i in range(AH):
            _attn_setup(i, hp, k_ref, v_ref, gv_ref, kpad, vpad, bias)
            dkacc[i][...] = jnp.zeros_like(dkacc[i])
            dvacc[i][...] = jnp.zeros_like(dvacc[i])
            dbias[i][...] = jnp.zeros_like(dbias[i])
        dga_ref[...] = jnp.zeros_like(dga_ref)
        col = lax.broadcasted_iota(jnp.int32, (QBLK, WIN), 1)

        def block(b, carry):
            pair = range(AH)
            r0, lss, qs, kws, vws, pns = _attn_probs(b, q_ref, kpad, vpad, bias, col)
            pn_b = [pns[i].astype(BF16) for i in pair]
            o = [_dot(pn_b[i], vws[i], NN) for i in pair]
            r = [lax.rsqrt(jnp.mean(o[i] * o[i], axis=-1, keepdims=True) + EPS) for i in pair]
            dn = [dn_ref[pl.ds(r0, QBLK), lss[i]] for i in pair]
            for i in pair:
                dga_ref[i:i + 1, :] += _colsum(dn[i] * o[i] * r[i])
            a = [dn[i] * ga_ref[0:1, lss[i]] for i in pair]
            do_b = [(r[i] * (a[i] - o[i] * (r[i] * r[i]) * jnp.mean(a[i] * o[i], axis=-1, keepdims=True))).astype(BF16)
                    for i in pair]
            dp = [_dot(do_b[i], vws[i], NT) for i in pair]
            for i in pair:
                dvacc[i][pl.ds(r0, WIN), :] += _dot(pn_b[i], do_b[i], TN)
            ds = [pns[i] * (dp[i] - jnp.sum(pns[i] * dp[i], axis=-1, keepdims=True)) for i in pair]
            for i in pair:
                dbias[i][...] += ds[i]
            ds_b = [ds[i].astype(BF16) for i in pair]
            dq = [_dot(ds_b[i], kws[i], NN) * scale for i in pair]
            dq_ref[pl.ds(r0, QBLK), :] = jnp.concatenate(dq, axis=1).astype(BF16)
            for i in pair:
                dkacc[i][pl.ds(r0, WIN), :] += _dot(ds_b[i], qs[i], TN) * scale
            return carry

        lax.fori_loop(0, T // QBLK, block, 0)

        rr = lax.broadcasted_iota(jnp.int32, (QBLK, QBLK), 0)
        cc = lax.broadcasted_iota(jnp.int32, (QBLK, QBLK), 1)
        flip = (rr + cc == QBLK - 1).astype(BF16)
        for i in range(AH):
            ls = slice(i * ATTN_HEAD_DIM, (i + 1) * ATTN_HEAD_DIM)
            dk_ref[:, ls] = dkacc[i][KPAD:, :].astype(BF16)
            dv_ref[:, ls] = dvacc[i][KPAD:, :].astype(BF16)
            full = jnp.concatenate([dbias[i][...], jnp.zeros((QBLK, TAB - WIN), F32)], axis=1)
            hi = full.astype(BF16)
            lo = (full - hi.astype(F32)).astype(BF16)
            rev = _dot(flip, hi, NN) + _dot(flip, lo, NN)
            dgv_ref[i:i + 1, :] = _colsum(pltpu.roll(rev, TAB - (QBLK - 1), 1, stride=1, stride_axis=0))

    blk = lambda off: pl.BlockSpec((T, W), lambda hp: (0, off + hp))
    accs = lambda dt: [pltpu.VMEM((T + KPAD, ATTN_HEAD_DIM), dt)] * AH
    return pl.pallas_call(
        body, grid=(HP,), name="attn_bwd",
        in_specs=[blk(0), blk(HP), blk(2 * HP), blk(0), pl.BlockSpec(gv.shape, lambda hp: (0, 0)),
                  pl.BlockSpec((1, W), lambda hp: (0, hp))],
        out_specs=[blk(0), blk(0), blk(0), pl.BlockSpec((None, AH, TAB), lambda hp: (hp, 0, 0)),
                   pl.BlockSpec((None, AH, ATTN_HEAD_DIM), lambda hp: (hp, 0, 0))],
        out_shape=[_sds((T, AW), BF16), _sds((T, AW), BF16), _sds((T, AW), BF16),
                   _sds((HP, AH, TAB), F32), _sds((HP, AH, ATTN_HEAD_DIM), F32)],
        scratch_shapes=accs(BF16) + accs(BF16) + accs(F32) + accs(F32) + [pltpu.VMEM((QBLK, WIN), F32)] * (2 * AH),
        compiler_params=_params(("parallel",), big=True),
    )(proj, proj, proj, dmixin, gv, ga)


def _ltri():
    r = lax.broadcasted_iota(jnp.int32, (CHUNK, CHUNK), 0)
    c = lax.broadcasted_iota(jnp.int32, (CHUNK, CHUNK), 1)
    return (c <= r).astype(BF16)


def _tri_dot(tri, v, dims):
    hi = v.astype(BF16)
    lo = (v - hi.astype(F32)).astype(BF16)
    return _dot(tri, hi, dims) + _dot(tri, lo, dims)


HEADS_PER_STEP = (4, 2)
REC_ROW_TILE = 512


def _alternate(stages):
    live = list(stages)
    while live:
        for g in list(live):
            if next(g, StopIteration) is StopIteration:
                live.remove(g)


def _hgrn_gates(n, ls, q_ref, f_ref, lb_ref, ltri):
    r0 = pl.multiple_of(n * CHUNK, CHUNK)
    rows = pl.ds(r0, CHUNK)
    lb = lb_ref[:, ls]
    qb = q_ref[rows, ls]
    sg = _sigmoid(f_ref[rows, ls])
    f = lb + (1.0 - lb) * sg
    sq = _sigmoid(qb)
    b = _tri_dot(ltri, jnp.log(f), NN)
    return rows, lb, qb, sg, f, 1.0 - f, sq, qb * sq, b


def _hgrn_specs(T, RW, AW, backward):
    HG = HEADS_PER_STEP[1 if backward else 0]
    W = HG * LANE
    TT = _tile(T, REC_ROW_TILE, CHUNK)
    n_row_tiles = T // TT
    base = 3 * AW // W
    row = (lambda t: n_row_tiles - 1 - t) if backward else (lambda t: t)
    blk_in = lambda off: pl.BlockSpec((TT, W), lambda g, t: (row(t), base + off + g))
    col = pl.BlockSpec((TT, W), lambda g, t: (row(t), g))
    states = pl.BlockSpec((HG, TT // CHUNK, LANE, LANE), lambda g, t: (g, row(t), 0, 0))
    return HG, W, RW // W, TT, n_row_tiles, blk_in, col, states


def _hgrn_fwd(proj, lb, gn, AW, RW):
    T = proj.shape[0]
    RH, NC, NSUB = RW // LANE, T // CHUNK, CHUNK // SUB
    HG, W, NG, TT, n_row_tiles, blk_in, col, states = _hgrn_specs(T, RW, AW, False)

    def body(q_ref, f_ref, i_ref, g_ref, lb_ref, gn_ref, mix_ref, o_ref, stall_ref, st_all, bs_all, kks_all, ics_all):
        @pl.when(pl.program_id(1) == 0)
        def _():
            st_all[...] = jnp.zeros_like(st_all)

        ltri = _ltri()
        rowi = lax.broadcasted_iota(jnp.int32, (SUB, 1), 0)

        def one_head(h, n):
            ls = slice(h * LANE, (h + 1) * LANE)
            st, bs, kks, ics = st_all.at[h], bs_all.at[h], kks_all.at[h], ics_all.at[h]
            rows, _, _, _, _, kk, _, qs, b = _hgrn_gates(n, ls, q_ref, f_ref, lb_ref, ltri)
            ic = i_ref[rows, ls]
            stv = st[...]
            stall_ref[h, n] = stv
            bs[...] = b
            kks[...] = kk
            ics[...] = ic
            yield
            o = _dot((qs * jnp.exp(b)).astype(BF16), stv.astype(BF16), NT)
            yield
            ic_b = ic.astype(BF16)
            pieces = []
            for blk in range(NSUB):
                s0 = blk * SUB
                bI, qI = b[s0:s0 + SUB], qs[s0:s0 + SUB]
                if blk == 0:
                    oI = jnp.zeros((SUB, LANE), F32)
                else:
                    ref = bs[s0 - 1:s0, :]
                    qt = (qI * jnp.exp(bI - ref)).astype(BF16)
                    kt = (kk[0:s0] * jnp.exp(ref - b[0:s0])).astype(BF16)
                    oI = _dot(_dot(qt, kt, NT).astype(BF16), ic_b[0:s0], NN)
                    yield
                acc = [oI[g * ROWS:(g + 1) * ROWS] for g in range(SUB // ROWS)]
                for s in range(SUB):
                    sr = s0 + s
                    g0 = s // ROWS
                    lo = g0 * ROWS
                    e = jnp.exp(jnp.minimum(bI[lo:] - bs[sr:sr + 1, :], 0.0))
                    a = jnp.sum(qI[lo:] * kks[sr:sr + 1, :] * e, axis=-1, keepdims=True)
                    add = jnp.where(rowi[lo:] >= s, a, 0.0) * ics[sr:sr + 1, :]
                    for g in range(g0, SUB // ROWS):
                        acc[g] = acc[g] + add[(g - g0) * ROWS:(g - g0 + 1) * ROWS]
                    yield
                pieces.extend(acc)
            o = o + jnp.concatenate(pieces, axis=0)
            bl = bs[CHUNK - 1:CHUNK, :]
            kd = (kk * jnp.exp(bl - b)).astype(BF16)
            st[...] = stv * jnp.exp(bl) + _dot(ic_b, kd, TN)
            yield
            o_ref[rows, ls] = o
            r = lax.rsqrt(jnp.mean(o * o, axis=-1, keepdims=True) + EPS)
            gb = g_ref[rows, ls]
            mix_ref[rows, ls] = (o * r * gn_ref[...] * (gb * _sigmoid(gb))).astype(BF16)

        def chunk(n, carry):
            _alternate([one_head(h, n) for h in range(HG)])
            return carry

        lax.fori_loop(0, TT // CHUNK, chunk, 0)

    tile = pltpu.VMEM((HG, CHUNK, LANE), F32)
    return pl.pallas_call(
        body, grid=(NG, n_row_tiles), name="hgrn_fwd",
        in_specs=[blk_in(0), blk_in(NG), blk_in(2 * NG), blk_in(3 * NG), pl.BlockSpec((1, W), lambda g, t: (0, g)),
                  pl.BlockSpec((1, LANE), lambda g, t: (0, 0))],
        out_specs=[col, col, states],
        out_shape=[_sds((T, RW), BF16), _sds((T, RW), F32), _sds((RH, NC, LANE, LANE), F32)],
        scratch_shapes=[pltpu.VMEM((HG, LANE, LANE), F32), tile, tile, tile],
        compiler_params=_params(("parallel", "arbitrary"), big=True),
    )(proj, proj, proj, proj, lb, gn)


def _hgrn_bwd(proj, dmixin, o_b, st_all, lb, gn, AW, RW):
    T = proj.shape[0]
    RH, NC, NSUB = RW // LANE, T // CHUNK, CHUNK // SUB
    HG, W, NG, TT, n_row_tiles, blk_in, col, states = _hgrn_specs(T, RW, AW, True)

    def body(q_ref, f_ref, i_ref, g_ref, o_ref, dn_ref, stall_ref, lb_ref, gn_ref,
             dq_ref, df_ref, di_ref, dg_ref, dlb_ref, dgn_ref, dst_all, bs_all, qss_all, dos_all, p2_all, dic_all,
             p1_all):
        @pl.when(pl.program_id(1) == 0)
        def _():
            dst_all[...] = jnp.zeros_like(dst_all)
            dlb_ref[...] = jnp.zeros_like(dlb_ref)
            dgn_ref[...] = jnp.zeros_like(dgn_ref)

        ltri = _ltri()
        rowi = lax.broadcasted_iota(jnp.int32, (SUB, 1), 0)
        last = lax.broadcasted_iota(jnp.int32, (CHUNK, 1), 0) == CHUNK - 1

        def one_head(h, n):
            ls = slice(h * LANE, (h + 1) * LANE)
            dst, bs, qss, dos = dst_all.at[h], bs_all.at[h], qss_all.at[h], dos_all.at[h]
            p2, dic, p1s = p2_all.at[h], dic_all.at[h], p1_all.at[h]
            rows, lbv, qb, sg, f, kk, sq, qs, b = _hgrn_gates(n, ls, q_ref, f_ref, lb_ref, ltri)
            ic = i_ref[rows, ls]
            stv = stall_ref[h, n]
            dstv = dst[...]
            o = o_ref[rows, ls]
            dn = dn_ref[rows, ls]
            gb = g_ref[rows, ls]
            sgb = _sigmoid(gb)
            r = lax.rsqrt(jnp.mean(o * o, axis=-1, keepdims=True) + EPS)
            gnv = gn_ref[...]
            dg_ref[rows, ls] = (dn * (o * r * gnv) * (sgb * (1.0 + gb * (1.0 - sgb)))).astype(BF16)
            dy = dn * (gb * sgb)
            dgn_ref[h] += _colsum(dy * o * r)
            a_ = dy * gnv
            do = r * (a_ - o * (r * r) * jnp.mean(a_ * o, axis=-1, keepdims=True))
            do_b = do.astype(BF16)
            bs[...] = b
            qss[...] = qs
            dos[...] = do
            yield
            ic_b = ic.astype(BF16)
            eb = jnp.exp(b)
            bl = bs[CHUNK - 1:CHUNK, :]
            ebl = jnp.exp(bl)
            dec = jnp.exp(bl - b)
            kd = (kk * dec).astype(BF16)
            dst_b = dstv.astype(BF16)
            dqs = _dot(do_b, stv.astype(BF16), NN) * eb
            dkk2 = _dot(ic_b, dst_b, NN) * dec
            dic[...] = _dot(kd, dst_b, NT)
            dbl = ebl * _colsum(stv * dstv) + _colsum(kk * dkk2)
            dst[...] = dstv * ebl + _dot(do_b, (qs * eb).astype(BF16), TN)
            yield
            p2[...] = jnp.zeros_like(p2)
            p1_pieces = []
            for blk in range(NSUB):
                s0 = blk * SUB
                bI, qI, doI = b[s0:s0 + SUB], qs[s0:s0 + SUB], do[s0:s0 + SUB]
                if blk == 0:
                    p1 = jnp.zeros((SUB, LANE), F32)
                else:
                    ref = bs[s0 - 1:s0, :]
                    eq = jnp.exp(bI - ref)
                    ek = jnp.exp(ref - b[0:s0])
                    qt = (qI * eq).astype(BF16)
                    kt = (kk[0:s0] * ek).astype(BF16)
                    doI_b = doI.astype(BF16)
                    dic[0:s0, :] += _dot(_dot(qt, kt, NT).astype(BF16), doI_b, TN)
                    da = _dot(doI_b, ic_b[0:s0], NT).astype(BF16)
                    p1 = _dot(da, kt, NN) * eq
                    p2[0:s0, :] += _dot(da, qt, TN) * ek
                    yield
                p1_pieces.append(p1)
                kkI, icI = kk[s0:s0 + SUB], ic[s0:s0 + SUB]
                p2acc = [jnp.zeros((ROWS, LANE), F32) for _ in range(SUB // ROWS)]
                diacc = [jnp.zeros((ROWS, LANE), F32) for _ in range(SUB // ROWS)]
                for t in range(SUB):
                    tr = s0 + t
                    ng = t // ROWS + 1
                    hi = ng * ROWS
                    keep = rowi[:hi] <= t
                    do_t = dos[tr:tr + 1, :]
                    e = jnp.exp(jnp.minimum(bs[tr:tr + 1, :] - bI[:hi], 0.0))
                    qe = qss[tr:tr + 1, :] * e
                    a = jnp.where(keep, jnp.sum(kkI[:hi] * qe, axis=-1, keepdims=True), 0.0)
                    da = jnp.where(keep, jnp.sum(icI[:hi] * do_t, axis=-1, keepdims=True), 0.0)
                    dp2, ddi = da * qe, a * do_t
                    for g in range(ng):
                        p2acc[g] = p2acc[g] + dp2[g * ROWS:(g + 1) * ROWS]
                        diacc[g] = diacc[g] + ddi[g * ROWS:(g + 1) * ROWS]
                    p1s[tr:tr + 1, :] = _colsum(da * kkI[:hi] * e)
                    yield
                p2[s0:s0 + SUB, :] += jnp.concatenate(p2acc, axis=0)
                dic[s0:s0 + SUB, :] += jnp.concatenate(diacc, axis=0)
            dqs = dqs + jnp.concatenate(p1_pieces, axis=0) + p1s[...]
            dkk = dkk2 + p2[...]
            db = qs * dqs - kk * dkk + jnp.where(last, dbl, 0.0)
            dgl = _tri_dot(ltri, db, TN)
            yield
            dfv = dgl / f - dkk
            df_ref[rows, ls] = (dfv * (1.0 - lbv) * sg * (1.0 - sg)).astype(BF16)
            dlb_ref[:, ls] += _colsum(dfv * (1.0 - sg))
            dq_ref[rows, ls] = (dqs * (sq * (1.0 + qb * (1.0 - sq)))).astype(BF16)
            di_ref[rows, ls] = dic[...].astype(BF16)

        def chunk(k, carry):
            _alternate([one_head(h, TT // CHUNK - 1 - k) for h in range(HG)])
            return carry

        lax.fori_loop(0, TT // CHUNK, chunk, 0)

    tile = pltpu.VMEM((HG, CHUNK, LANE), F32)
    return pl.pallas_call(
        body, grid=(NG, n_row_tiles), name="hgrn_bwd",
        in_specs=[blk_in(0), blk_in(NG), blk_in(2 * NG), blk_in(3 * NG), col,
                  pl.BlockSpec((TT, W), lambda g, t: (n_row_tiles - 1 - t, AW // W + g)), states,
                  pl.BlockSpec((1, W), lambda g, t: (0, g)), pl.BlockSpec((1, LANE), lambda g, t: (0, 0))],
        out_specs=[col, col, col, col, pl.BlockSpec((1, W), lambda g, t: (0, g)),
                   pl.BlockSpec((HG, 1, LANE), lambda g, t: (g, 0, 0))],
        out_shape=[_sds((T, RW), BF16)] * 4 + [_sds((1, RW), F32), _sds((RH, 1, LANE), F32)],
        scratch_shapes=[pltpu.VMEM((HG, LANE, LANE), F32), tile, tile, tile, tile, tile, tile],
        compiler_params=_params(("parallel", "arbitrary"), big=True),
    )(proj, proj, proj, proj, o_b, dmixin, st_all, lb, gn)


def _prep(c, lb_logits, rb_pad, max_rel, after):
    D, RW = c.shape[-1], lb_logits.shape[-1]
    H, rbp = rb_pad.shape

    def body(c_ref, l_ref, rb_ref, _, __, cact_ref, lb_ref, gv_ref):
        cv = c_ref[...]
        cact_ref[...] = cv * _sigmoid(cv)
        lb_ref[...] = _sigmoid(l_ref[0:1, :] - l_ref[1:2, :])
        gv_ref[...] = _dot(rb_ref[...], _bias_onehot(rbp, max_rel), NN, HIGHEST)

    vmem = pl.BlockSpec(memory_space=pltpu.VMEM)
    return pl.pallas_call(
        body, name="prep", in_specs=[vmem, vmem, vmem, ORDER_ONLY, ORDER_ONLY],
        out_shape=[_sds((1, D), F32), _sds((1, RW), F32), _sds((H, TAB), F32)],
    )(c, lb_logits, rb_pad, *after)


def _mod_part(c_all, w_ada_s, b_ada_s):
    B, D = c_all.shape
    ns = w_ada_s.shape[1]
    tn = _tile(ns, 768, LANE)

    def body(c_ref, w_ref, b_ref, o_ref):
        o_ref[...] = _dot(c_ref[...], w_ref[...], NN) + b_ref[...]

    return pl.pallas_call(
        body, grid=(ns // tn,), name="mod_part",
        in_specs=[pl.BlockSpec((B, D), lambda j: (0, 0)), pl.BlockSpec((D, tn), lambda j: (0, j)),
                  pl.BlockSpec((1, tn), lambda j: (0, j))],
        out_specs=pl.BlockSpec((B, tn), lambda j: (0, j)),
        out_shape=_sds((B, ns), F32), compiler_params=_params(("parallel",)),
    )(c_all, w_ada_s, b_ada_s)


def _adam(w, g, m, v):
    m = ADAM_B1 * m + (1.0 - ADAM_B1) * g
    v = ADAM_B2 * v + (1.0 - ADAM_B2) * (g * g)
    m_hat = m * (1.0 / (1.0 - ADAM_B1 ** ADAM_STEP))
    v_hat = v * (1.0 / (1.0 - ADAM_B2 ** ADAM_STEP))
    return -ADAM_LR * (m_hat / (jnp.sqrt(v_hat) + ADAM_EPS) + ADAM_WD * w), m, v


def _adam_ada(c_all, dmod_s, w, m, v):
    B, D = c_all.shape
    ns = w.shape[1]
    tr, tn = _tile(D, 512, LANE), _tile(ns, 768, LANE)

    def body(c_ref, d_ref, w_ref, m_ref, v_ref, g_out, dw_out, m_out, v_out):
        g = _dot(c_ref[...], d_ref[...], TN)
        g_out[...] = g
        dw_out[...], m_out[...], v_out[...] = _adam(w_ref[...], g, m_ref[...], v_ref[...])

    big = pl.BlockSpec((tr, tn), lambda i, j: (i, j))
    return pl.pallas_call(
        body, grid=(D // tr, ns // tn), name="adam_w_ada",
        in_specs=[pl.BlockSpec((B, tr), lambda i, j: (0, i)), pl.BlockSpec((B, tn), lambda i, j: (0, j)),
                  big, big, big],
        out_specs=[big] * 4, out_shape=[_sds((D, ns), F32)] * 4,
        compiler_params=_params(("parallel", "parallel")),
    )(c_all, dmod_s, w, m, v)


def _adam_shard(parts, w, m, v, name):
    R, C = w.shape
    tr = _tile(R, 256, 16)

    def body(p_ref, w_ref, m_ref, v_ref, g_out, dw_out, m_out, v_out):
        g = p_ref[0].astype(F32)
        for k in range(1, N_DEV // 2):
            g = g + p_ref[k].astype(F32)
        g_out[...] = g
        dw_out[...], m_out[...], v_out[...] = _adam(w_ref[...], g, m_ref[...], v_ref[...])

    big = pl.BlockSpec((tr, C), lambda i: (i, 0))
    return pl.pallas_call(
        body, grid=(R // tr,), name=name,
        in_specs=[pl.BlockSpec((N_DEV // 2, tr, C), lambda i: (0, i, 0)), big, big, big],
        out_specs=[big] * 4, out_shape=[_sds((R, C), F32)] * 4,
        compiler_params=_params(("parallel",), big=True),
    )(parts, w, m, v)


def _pair_sum(g8, land, core, name):
    _, NCHIP, R, C = g8.shape
    tr = _tile(R, 1024, 16)

    def body(core_ref, g_ref, l_ref, o_ref):
        o_ref[...] = g_ref[...] + l_ref[...]

    return pl.pallas_call(
        body, name=name,
        grid_spec=pltpu.PrefetchScalarGridSpec(
            num_scalar_prefetch=1, grid=(NCHIP, R // tr),
            in_specs=[pl.BlockSpec((None, None, tr, C), lambda k, i, core_ref: (core_ref[0], k, i, 0)),
                      pl.BlockSpec((None, tr, C), lambda k, i, core_ref: (k, i, 0))],
            out_specs=pl.BlockSpec((None, tr, C), lambda k, i, core_ref: (k, i, 0))),
        out_shape=_sds((NCHIP, R, C), BF16), compiler_params=_params(("parallel", "parallel")),
    )(core, g8, land)


SMALL = ("b_ada", "rel_bias", "attn_norm_g", "lb_logits", "gnorm_g", "ln1_g", "ln1_b", "ln2_g", "ln2_b")


def _small_update(parts, loss_parts, lbv, ws, ms, vs, max_rel):
    n = len(SMALL)

    def body(*refs):
        part_refs = dict(zip(SMALL, refs[:n]))
        loss_in, lb_ref = refs[n], refs[n + 1]
        w_refs, m_refs, v_refs = refs[n + 2:2 * n + 2], refs[2 * n + 2:3 * n + 2], refs[3 * n + 2:4 * n + 2]
        outs = refs[4 * n + 2:]

        def total(ref):
            tot = ref[0]
            for k in range(1, N_DEV):
                tot = tot + ref[k]
            return tot

        outs[0][...] = jnp.sum(total(loss_in), axis=-1, keepdims=True)
        for idx, name in enumerate(SMALL):
            g = total(part_refs[name])
            if name == "rel_bias":
                g = _dot(g, _bias_onehot(w_refs[idx].shape[1], max_rel), NT, HIGHEST)
            elif name == "lb_logits":
                lb = lb_ref[...]
                sign = (1 - 2 * lax.broadcasted_iota(jnp.int32, (2, 1), 0)).astype(F32)
                g = sign * (g * lb * (1.0 - lb))
            elif name == "gnorm_g":
                g = _colsum(g)
            dw, mm, vv = _adam(w_refs[idx][...], g, m_refs[idx][...], v_refs[idx][...])
            outs[1 + 4 * idx][...] = g
            outs[2 + 4 * idx][...] = dw
            outs[3 + 4 * idx][...] = mm
            outs[4 + 4 * idx][...] = vv

    out_shape = [_sds((1, 1), F32)]
    for w in ws:
        out_shape += [_sds(w.shape, F32)] * 4
    return pl.pallas_call(body, name="small_update", out_shape=out_shape, compiler_params=_params(big=True))(
        *[parts[k] for k in SMALL], loss_parts, lbv, *ws, *ms, *vs)


def _place():
    x, y, c = lax.axis_index("x"), lax.axis_index("y"), lax.axis_index("c")
    return x, y, c, [(1 - x, y), (x, 1 - y), (1 - x, 1 - y)]


def _all_gather(shard, name):
    HBM = pl.BlockSpec(memory_space=pl.ANY)

    def body(x_ref, out_ref, send_sems, recv_sems, local_sem):
        x, y, c, chips = _place()
        me, sibling = (x, y, c), (x, y, 1 - c)

        def slot(px, py, pc):
            return out_ref.at[4 * px + 2 * py + pc]

        def copy(k, block, to, src=None):
            return pltpu.make_async_remote_copy(
                src_ref=slot(*block) if src is None else src, dst_ref=slot(*block),
                send_sem=send_sems.at[k], recv_sem=recv_sems.at[k], device_id=to, device_id_type=MESH)

        mine = pltpu.make_async_copy(x_ref, slot(*me), local_sem)
        mine.start()
        first = [copy(0, me, sibling, src=x_ref)]
        first += [copy(1 + j, me, (*chip, c), src=x_ref) for j, chip in enumerate(chips)]
        for cp in first:
            cp.start()
        passed = [copy(4 + j, (*chip, c), sibling) for j, chip in enumerate(chips)]
        for j, chip in enumerate(chips):
            copy(1 + j, (*chip, c), me).wait_recv()
            passed[j].start()
        copy(0, sibling, me).wait_recv()
        for j, chip in enumerate(chips):
            copy(4 + j, (*chip, 1 - c), me).wait_recv()
        for cp in first + passed:
            cp.wait_send()
        mine.wait()

    return pl.pallas_call(
        body, name=name, out_shape=_sds((N_DEV,) + shard.shape, shard.dtype),
        in_specs=[HBM], out_specs=HBM,
        scratch_shapes=[pltpu.SemaphoreType.DMA((7,)), pltpu.SemaphoreType.DMA((7,)), pltpu.SemaphoreType.DMA(())],
    )(shard)


SEM_SPEC = pl.BlockSpec(memory_space=pltpu.SEMAPHORE)
HBM_SPEC = pl.BlockSpec(memory_space=pltpu.HBM)
EFFECT = pltpu.SideEffectType.DATAFLOW_SIDE_EFFECTING


def _remote(src, dst, send_sems, recv_sems, k, dev):
    return pltpu.make_async_remote_copy(src_ref=src, dst_ref=dst, send_sem=send_sems.at[k], recv_sem=recv_sems.at[k],
                                        device_id=dev, device_id_type=MESH)


def _copy_start(name, bufs, plan, n, after, only=None):
    nb = len(bufs)

    def body(*refs):
        send_sems, recv_sems = refs[nb + 1], refs[nb + 2]
        for k, (src, dst, dev) in enumerate(plan(*refs[:nb])):
            if only is not None and k not in only:
                continue
            _remote(src, dst, send_sems, recv_sems, k, dev).start()
        refs[-1][...] = jnp.zeros_like(refs[-1])

    out = pl.pallas_call(
        body, name=name,
        out_shape=(pltpu.SemaphoreType.DMA((n,)), pltpu.SemaphoreType.DMA((n,)),
                   *[pltpu.HBM(b.shape, b.dtype) for b in bufs], _sds((8, LANE), F32)),
        in_specs=[HBM_SPEC] * nb + [ORDER_ONLY],
        out_specs=(SEM_SPEC, SEM_SPEC, *[HBM_SPEC] * nb, pl.BlockSpec(memory_space=pltpu.VMEM)),
        input_output_aliases={i: 2 + i for i in range(nb)},
        compiler_params=pltpu.CompilerParams(has_side_effects=EFFECT),
    )(*[pltpu.with_memory_space_constraint(b, pltpu.HBM) for b in bufs], after)
    return (out[0], out[1]), list(out[2:2 + nb]), out[-1]


def _copy_wait(name, sems, bufs, plan, after, only=None):
    nb = len(bufs)

    def body(*refs):
        send_sems, recv_sems = refs[nb], refs[nb + 1]
        for k, (src, dst, dev) in enumerate(plan(*refs[:nb])):
            if only is not None and k not in only:
                continue
            cp = _remote(src, dst, send_sems, recv_sems, k, dev)
            cp.wait_send()
            cp.wait_recv()

    out = pl.pallas_call(
        body, name=name, out_shape=tuple(pltpu.HBM(b.shape, b.dtype) for b in bufs),
        in_specs=[HBM_SPEC] * nb + [SEM_SPEC, SEM_SPEC, pl.BlockSpec(memory_space=pl.ANY)],
        out_specs=tuple([HBM_SPEC] * nb), input_output_aliases={i: i for i in range(nb)},
        compiler_params=pltpu.CompilerParams(has_side_effects=EFFECT),
    )(*bufs, sems[0], sems[1], after)
    return list(out)


def _ag_plan_chips(shard_ref, out_ref):
    x, y, c, chips = _place()
    mine = out_ref.at[4 * x + 2 * y + c]
    return [(shard_ref, mine, (x, y, 1 - c))] + [(shard_ref, mine, (*chip, c)) for chip in chips]


def _ag_plan_pass(out_ref):
    x, y, c, chips = _place()
    slots = [out_ref.at[4 * chip[0] + 2 * chip[1] + c] for chip in chips]
    return [(s, s, (x, y, 1 - c)) for s in slots]


def _rs_plan_pair(g_ref, land_ref):
    x, y, c, _ = _place()
    return [(g_ref.at[1 - c], land_ref, (x, y, 1 - c))]


def _rs_plan_chips(p_ref, land_ref):
    x, y, c, chips = _place()
    return [(p_ref.at[2 * chip[0] + chip[1]], land_ref.at[2 * x + y], (*chip, c)) for chip in chips]


class _Gather:
    @staticmethod
    def landing(shard, me):
        return lax.dynamic_update_slice(lax.empty((N_DEV,) + shard.shape, shard.dtype), shard[None],
                                        (me,) + (0,) * shard.ndim)

    def __init__(self, shard, out, tag, after):
        self.tag = tag
        self.sems, (self.shard, self.out), self.token = _copy_start(
            "ag_start_" + tag, [shard, out], _ag_plan_chips, 4, after)
        self.groups = []

    def arrived(self, after, copies):
        name = "ag_wait_%s_%s" % (self.tag, "".join(map(str, copies)))
        self.shard, self.out = _copy_wait(name, self.sems, [self.shard, self.out], _ag_plan_chips, after, copies)
        return self.out

    def pass_on(self, after, blocks):
        name = "ag_pass_%s_%s" % (self.tag, "".join(map(str, blocks)))
        sems, (self.out,), _ = _copy_start(name, [self.out], _ag_plan_pass, 3, after, blocks)
        self.groups.append((sems, blocks))
        return self.out

    def passed(self, after, group):
        sems, blocks = self.groups[group]
        name = "ag_pass_wait_%s_%s" % (self.tag, "".join(map(str, blocks)))
        self.out = _copy_wait(name, sems, [self.out], _ag_plan_pass, after, blocks)[0]
        return self.out

    def arrived_from_chips(self, after):
        self.arrived(after, (0, 1, 2, 3))
        return self.pass_on(after, (0, 1, 2))

    def passed_on(self, after):
        return self.passed(after, 0)


def _ag_plan_direct(src_ref, out_ref):
    x, y, c, chips = _place()
    mine = out_ref.at[4 * x + 2 * y + c]
    peers = [(x, y, 1 - c)] + [(*chip, pc) for chip in chips for pc in (c, 1 - c)]
    return [(src_ref, mine, peer) for peer in peers]


class _SmallGather:
    def __init__(self, block, me, tag):
        self.tag = tag
        out = lax.dynamic_update_slice(lax.empty((N_DEV,) + block.shape, block.dtype), block[None],
                                       (me,) + (0,) * block.ndim)
        self.sems, self.bufs, self.token = _copy_start(
            "ag_direct_start_" + tag, [block, out], _ag_plan_direct, N_DEV - 1, jnp.zeros((1,), F32))

    def done(self, after):
        return _copy_wait("ag_direct_wait_" + self.tag, self.sems, self.bufs, _ag_plan_direct, after)[1]


class _ReduceScatter:
    def __init__(self, g8, tag):
        self.tag = tag
        land = lax.empty(g8.shape[1:], g8.dtype)
        self.sems, self.bufs, self.token = _copy_start(
            "rs_pair_start_" + tag, [g8, land], _rs_plan_pair, 1, jnp.zeros((1,), F32))

    def pair_done(self, core, chip, after):
        g8, land = _copy_wait("rs_pair_wait_" + self.tag, self.sems, self.bufs, _rs_plan_pair, after)
        p4 = _pair_sum(g8, land, core, "rs_pair_sum_" + self.tag)
        own = lax.dynamic_slice_in_dim(p4, chip, 1, axis=0)
        land2 = lax.dynamic_update_slice(lax.empty(p4.shape, p4.dtype), own, (chip, 0, 0))
        self.sems, self.bufs, self.token = _copy_start(
            "rs_chips_start_" + self.tag, [p4, land2], _rs_plan_chips, 3, jnp.zeros((1,), F32))

    def sums(self, after):
        return _copy_wait("rs_chips_wait_" + self.tag, self.sems, self.bufs, _rs_plan_chips, after)[1]


ORDER = ("w_ada", "b_ada", "w_in", "rel_bias", "attn_norm_g", "lb_logits", "gnorm_g", "w_o", "ln1_g", "ln1_b",
         "w_ffn_in", "w_ffn_out", "ln2_g", "ln2_b")


def kernel(x, c, w_ada, b_ada, w_in, rel_bias, attn_norm_g, lb_logits, gnorm_g, w_o, ln1_g, ln1_b, w_ffn_in, w_ffn_out, ln2_g, ln2_b, loss_target, m_w_ada, m_b_ada, m_w_in, m_rel_bias, m_attn_norm_g, m_lb_logits, m_gnorm_g, m_w_o, m_ln1_g, m_ln1_b, m_w_ffn_in, m_w_ffn_out, m_ln2_g, m_ln2_b, v_w_ada, v_b_ada, v_w_in, v_rel_bias, v_attn_norm_g, v_lb_logits, v_gnorm_g, v_w_o, v_ln1_g, v_ln1_b, v_w_ffn_in, v_w_ffn_out, v_ln2_g, v_ln2_b):
    W = dict(w_ada=w_ada, b_ada=b_ada, w_in=w_in, rel_bias=rel_bias, attn_norm_g=attn_norm_g, lb_logits=lb_logits,
             gnorm_g=gnorm_g, w_o=w_o, ln1_g=ln1_g, ln1_b=ln1_b, w_ffn_in=w_ffn_in, w_ffn_out=w_ffn_out,
             ln2_g=ln2_g, ln2_b=ln2_b)
    M = dict(w_ada=m_w_ada, b_ada=m_b_ada, w_in=m_w_in, rel_bias=m_rel_bias, attn_norm_g=m_attn_norm_g,
             lb_logits=m_lb_logits, gnorm_g=m_gnorm_g, w_o=m_w_o, ln1_g=m_ln1_g, ln1_b=m_ln1_b,
             w_ffn_in=m_w_ffn_in, w_ffn_out=m_w_ffn_out, ln2_g=m_ln2_g, ln2_b=m_ln2_b)
    V = dict(w_ada=v_w_ada, b_ada=v_b_ada, w_in=v_w_in, rel_bias=v_rel_bias, attn_norm_g=v_attn_norm_g,
             lb_logits=v_lb_logits, gnorm_g=v_gnorm_g, w_o=v_w_o, ln1_g=v_ln1_g, ln1_b=v_ln1_b,
             w_ffn_in=v_w_ffn_in, w_ffn_out=v_w_ffn_out, ln2_g=v_ln2_g, ln2_b=v_ln2_b)

    x2, tgt = x[0], loss_target[0]
    T, D = x2.shape
    AW, RW = attn_norm_g.shape[-1], lb_logits.shape[-1]
    MIX = AW + RW
    H, RH = AW // ATTN_HEAD_DIM, RW // LANE
    RB = rel_bias.shape[-1]
    max_rel = (RB - 1) // 2
    rbp = -(-RB // LANE) * LANE
    F = w_ffn_out.shape[1] * N_DEV
    half = N_DEV // 2
    xi, yi, ci = lax.axis_index("x"), lax.axis_index("y"), lax.axis_index("c")
    me = 4 * xi + 2 * yi + ci
    core = jnp.reshape(ci, (1,)).astype(jnp.int32)
    pad_rb = lambda a: jnp.pad(a[0], ((0, 0), (0, rbp - RB)))

    chip = 2 * xi + yi

    w_in_b = w_in[0].astype(BF16)
    w_in_land = _Gather.landing(w_in_b, me)
    c_act, lbv, gv = _prep(c, lb_logits, pad_rb(rel_bias), max_rel, (w_in_b, w_in_land))
    c_all = _all_gather(c_act, "ag_c").reshape(N_DEV, D)
    ns_ada = w_ada.shape[-1]
    mod_part = _mod_part(c_all, w_ada[0], lax.dynamic_slice_in_dim(b_ada, me * ns_ada, ns_ada, axis=1))
    mod_all = _all_gather(mod_part, "ag_mod")
    mod6 = lax.dynamic_index_in_dim(mod_all, me, axis=1, keepdims=False).reshape(6, D)

    bf = lambda w: w[0].astype(BF16)
    ag_in = _Gather(w_in_b, w_in_land, "w_in", mod_all)
    ag_o = _Gather(bf(w_o), _Gather.landing(bf(w_o), me), "w_o", ag_in.token)
    ag_f1 = _Gather(bf(w_ffn_in), _Gather.landing(bf(w_ffn_in), me), "w_ffn_in", ag_o.token)
    ag_f2 = _Gather(bf(w_ffn_out), _Gather.landing(bf(w_ffn_out), me), "w_ffn_out", ag_f1.token)

    h1 = _ln_mod(x2, mod6 + ag_f2.token[0, 0])
    ids = lambda pairs: jnp.stack([4 * px + 2 * py + pc for px, py, pc in pairs]).astype(jnp.int32)
    others = [(1 - xi, yi), (xi, 1 - yi), (1 - xi, 1 - yi)]
    proj = lax.empty((T, w_in.shape[-1] * N_DEV), F32)
    proj = _mm_gathered(h1, ag_in.arrived(h1, (0,)), ids([(xi, yi, ci), (xi, yi, 1 - ci)]), proj, "in_proj_a")
    ag_in.arrived(proj, (1, 2, 3))
    proj = _mm_gathered(h1, ag_in.pass_on(proj, (0, 1, 2)), ids([(*ch, ci) for ch in others]), proj, "in_proj_b")
    wg_in = ag_in.passed(proj, 0)
    proj = _mm_gathered(h1, wg_in, ids([(*ch, 1 - ci) for ch in others]), proj, "in_proj_c")
    ag_o.arrived_from_chips(proj)
    mix_a = _attn_fwd(proj, gv, attn_norm_g, AW)
    wg_o = ag_o.passed_on(mix_a).reshape(MIX, D)
    mix_b, o_b, st_all = _hgrn_fwd(proj, lbv, gnorm_g, AW, RW)
    mixin = jnp.concatenate([mix_a, mix_b], axis=1)
    mix = _mm_nn(mixin, wg_o, "out_proj")
    ag_f1.arrived_from_chips(mix)
    x1, h2 = _mid_fwd(x2, mix, mod6, ln1_g, ln1_b)
    wg_f1 = ag_f1.passed_on(h2)
    gu, act = _mm_swiglu(h2, wg_f1)
    ag_f2.arrived_from_chips(act)
    wg_f2 = ag_f2.passed_on(act).reshape(F, D)
    ff = _mm_nn(act, wg_f2, "ffn_out")
    dff, dx1a, vec_a = _final(x1, ff, mod6, ln2_g, ln2_b, tgt)

    du = _mm_swiglu_bwd(dff, wg_f2, gu)
    rs_f2 = _ReduceScatter(_mm_tn_rows(dff, act, dff, F // N_DEV, "grad_w_ffn_out"), "w_ffn_out")
    tm = _tile(T, 512, 16)
    du_ij = lambda tm_, w, first: pl.BlockSpec((None, tm_, w), lambda i, p: (p // (half // 2), i + first, p % (half // 2)))
    du_j = lambda rows, ns: pl.BlockSpec((None, rows, ns), lambda j: (j // half, 0, j % half))
    dh2 = _mm_gathered_nt(rs_f2.token, du, du_ij, wg_f1, T, tm, "ffn_in_bwd")
    rs_f2.pair_done(core, chip, dh2)
    gw_f1 = _mm_tn_gathered(rs_f2.token, h2, du, du_j, wg_f1.shape[-1], "grad_w_ffn_in")
    rs_f1 = _ReduceScatter(gw_f1.reshape(2, half, D, -1), "w_ffn_in")
    dmix, dxa, vec_b = _mid_bwd(x2, mix, x1, dx1a, dh2, mod6 + rs_f1.token[0, 0], ln1_g)
    dmixin = _mm_nt(dmix, wg_o, "out_proj_bwd")
    rs_f1.pair_done(core, chip, dmixin)
    rs_o = _ReduceScatter(_mm_tn_rows(rs_f1.token, mixin, dmix, MIX // N_DEV, "grad_w_o"), "w_o")
    dq, dk, dv, dgv, dga = _attn_bwd(proj, dmixin, gv + rs_o.token[0, 0], attn_norm_g, AW)
    rs_o.pair_done(core, chip, dq)
    dqb, dfl, dib, dgb, dlb, dgn = _hgrn_bwd(proj, dmixin, o_b, st_all, lbv + rs_o.token[0, 0], gnorm_g, AW, RW)
    dproj = jnp.concatenate([dq, dk, dv, dqb, dfl, dib, dgb], axis=1)
    p_ij = lambda tm_, w, first: pl.BlockSpec((tm_, w), lambda i, p: (i + first, p))
    p_j = lambda rows, ns: pl.BlockSpec((rows, ns), lambda j: (0, j))
    gw_in = _mm_tn_gathered(rs_o.token, h1, dproj, p_j, wg_in.shape[-1], "grad_w_in")
    rs_in = _ReduceScatter(gw_in.reshape(2, half, D, -1), "w_in")
    n_tiles = T // tm
    dh1 = _mm_gathered_nt(rs_in.token, dproj, p_ij, wg_in, T, tm, "in_proj_bwd_a", 0, n_tiles // 2)
    rs_in.pair_done(core, chip, dh1)
    dh1 = _mm_gathered_nt(rs_in.token, dproj, p_ij, wg_in, T, tm, "in_proj_bwd_b", n_tiles // 2,
                          n_tiles - n_tiles // 2, dh1)
    grad_x, vec_c = _first_bwd(x2, dh1, dxa, mod6)

    dmod = jnp.concatenate([vec_c[1:2], vec_c[0:1], vec_b[4:5], vec_b[1:2], vec_b[0:1], vec_a[2:3]], axis=0)
    pieces = dict(b_ada=dmod, rel_bias=dgv, attn_norm_g=dga, lb_logits=dlb, gnorm_g=dgn, ln1_g=vec_b[2:3],
                  ln1_b=vec_b[3:4], ln2_g=vec_a[0:1], ln2_b=vec_a[1:2], loss=vec_a[3:4])
    widths = dict(b_ada=(1, 6 * D), rel_bias=(H, TAB), attn_norm_g=(1, AW), lb_logits=(1, RW), gnorm_g=(RH, LANE),
                  ln1_g=(1, D), ln1_b=(1, D), ln2_g=(1, D), ln2_b=(1, D), loss=(1, D))
    packed = jnp.concatenate([pieces[k].reshape(-1, LANE) for k in widths], axis=0)
    small_ag = _SmallGather(packed, me, "small")
    after, res_big = small_ag.token, {}
    for k, rs in (("w_ffn_out", rs_f2), ("w_ffn_in", rs_f1), ("w_o", rs_o), ("w_in", rs_in)):
        four = _adam_shard(rs.sums(after), W[k][0], M[k][0], V[k][0], "adam_" + k)
        res_big[k] = [a[None] for a in four]
        after = four[0]
    gathered = small_ag.done(after)
    parts, r0 = {}, 0
    for k, (rows, width) in widths.items():
        nr = rows * width // LANE
        parts[k] = gathered[:, r0:r0 + nr, :].reshape(N_DEV, rows, width)
        r0 += nr
    prep_small = lambda d, k: pad_rb(d[k]) if k == "rel_bias" else d[k]
    small = _small_update(parts, parts["loss"], lbv, [prep_small(W, k) for k in SMALL],
                          [prep_small(M, k) for k in SMALL], [prep_small(V, k) for k in SMALL], max_rel)
    loss = small[0].reshape(())
    res = {}
    for idx, k in enumerate(SMALL):
        four = small[1 + 4 * idx:5 + 4 * idx]
        if k == "rel_bias":
            four = [a[:, :RB][None] for a in four]
        res[k] = list(four)

    res.update(res_big)
    dmod_s = lax.dynamic_slice_in_dim(parts["b_ada"].reshape(N_DEV, 6 * D), me * ns_ada, ns_ada, axis=1)
    res["w_ada"] = [a[None] for a in _adam_ada(c_all, dmod_s, w_ada[0], m_w_ada[0], v_w_ada[0])]

    out = [loss, grad_x[None]]
    for field in range(4):
        out += [res[k][field] for k in ORDER]
    return tuple(out)
```

```python
import jax
import jax.numpy as jnp
from jax import lax
from jax.experimental import pallas as pl
from jax.experimental.pallas import tpu as pltpu

F32 = jnp.float32
BF16 = jnp.bfloat16
MESH = pl.DeviceIdType.MESH
HIGHEST = lax.Precision.HIGHEST

N_DEV = 8
CHUNK = 64
N_PAST = 8
QBLK = 4 * CHUNK
KPAD = N_PAST * CHUNK
WIN = KPAD + QBLK
TAB = 1024
ATTN_HEAD_DIM = 64
ATTN_HEADS_PER_STEP = 4
SUB = 32
ROWS = 8
LANE = 128
EPS = 1e-5
ALPHA = 2.0 ** 0.25
ADAM_LR, ADAM_B1, ADAM_B2, ADAM_EPS, ADAM_WD, ADAM_STEP = 0.001, 0.9, 0.999, 1e-08, 0.01, 10
NEG = -1e30
VMEM_LIMIT = 56 * 1024 * 1024


def _sds(shape, dtype):
    return jax.ShapeDtypeStruct(tuple(shape), dtype)


def _tile(n, pref, mult):
    best = None
    for t in range(mult, min(n, pref) + 1, mult):
        if n % t == 0:
            best = t
    return n if best is None else best


def _params(sem=None, big=False):
    kw = {}
    if sem is not None:
        kw["dimension_semantics"] = sem
    if big:
        kw["vmem_limit_bytes"] = VMEM_LIMIT
    return pltpu.CompilerParams(**kw)


def _sigmoid(v):
    return 1.0 / (1.0 + jnp.exp(-v))


def _dot(a, b, dims, precision=None):
    return lax.dot_general(a, b, (dims, ((), ())), preferred_element_type=F32, precision=precision)


NN = ((1,), (0,))
NT = ((1,), (1,))
TN = ((0,), (0,))


def _ln(v):
    mu = jnp.mean(v, axis=-1, keepdims=True)
    d = v - mu
    rstd = lax.rsqrt(jnp.mean(d * d, axis=-1, keepdims=True) + EPS)
    return d * rstd, rstd


def _ln_bwd(dxh, xh, rstd):
    return rstd * (dxh - jnp.mean(dxh, axis=-1, keepdims=True) - xh * jnp.mean(dxh * xh, axis=-1, keepdims=True))


def _colsum(v):
    return jnp.sum(v, axis=0, keepdims=True)


def _ln_mod(x2, mod6):
    T, D = x2.shape
    tm = _tile(T, 256, 8)

    def body(x_ref, mod_ref, o_ref):
        xh, _ = _ln(x_ref[...])
        o_ref[...] = (xh * (1.0 + mod_ref[1:2, :]) + mod_ref[0:1, :]).astype(BF16)

    return pl.pallas_call(
        body, grid=(T // tm,), name="ln_mod",
        in_specs=[pl.BlockSpec((tm, D), lambda i: (i, 0)), pl.BlockSpec((6, D), lambda i: (0, 0))],
        out_specs=pl.BlockSpec((tm, D), lambda i: (i, 0)),
        out_shape=_sds((T, D), BF16), compiler_params=_params(("parallel",)),
    )(x2, mod6)


def _mid_fwd(x2, mix, mod6, ln1_g, ln1_b):
    T, D = x2.shape
    tm = _tile(T, 256, 8)

    def body(x_ref, mix_ref, mod_ref, g_ref, b_ref, x1_ref, h2_ref):
        zh, _ = _ln(ALPHA * x_ref[...] + mod_ref[2:3, :] * mix_ref[...])
        x1 = zh * g_ref[...] + b_ref[...]
        x1_ref[...] = x1
        xh, _ = _ln(x1)
        h2_ref[...] = (xh * (1.0 + mod_ref[4:5, :]) + mod_ref[3:4, :]).astype(BF16)

    row = pl.BlockSpec((tm, D), lambda i: (i, 0))
    vec = pl.BlockSpec((1, D), lambda i: (0, 0))
    return pl.pallas_call(
        body, grid=(T // tm,), name="mid_fwd",
        in_specs=[row, row, pl.BlockSpec((6, D), lambda i: (0, 0)), vec, vec],
        out_specs=[row, row],
        out_shape=[_sds((T, D), F32), _sds((T, D), BF16)], compiler_params=_params(("parallel",)),
    )(x2, mix, mod6, ln1_g, ln1_b)


def _final(x1, ff, mod6, ln2_g, ln2_b, tgt):
    T, D = x1.shape
    tm = _tile(T, 256, 8)

    def body(x1_ref, ff_ref, mod_ref, g_ref, b_ref, t_ref, dff_ref, dx1_ref, vec_ref):
        @pl.when(pl.program_id(0) == 0)
        def _():
            vec_ref[...] = jnp.zeros_like(vec_ref)

        ff_v = ff_ref[...]
        gate2 = mod_ref[5:6, :]
        zh, rstd = _ln(ALPHA * x1_ref[...] + gate2 * ff_v)
        err = zh * g_ref[...] + b_ref[...] - t_ref[...]
        dy = err * (1.0 / D)
        dz = _ln_bwd(dy * g_ref[...], zh, rstd)
        dff_ref[...] = (gate2 * dz).astype(BF16)
        dx1_ref[...] = ALPHA * dz
        vec_ref[0:1, :] += _colsum(dy * zh)
        vec_ref[1:2, :] += _colsum(dy)
        vec_ref[2:3, :] += _colsum(dz * ff_v)
        vec_ref[3:4, :] += _colsum(err * err) * (0.5 / D)

    row = pl.BlockSpec((tm, D), lambda i: (i, 0))
    vec = pl.BlockSpec((1, D), lambda i: (0, 0))
    return pl.pallas_call(
        body, grid=(T // tm,), name="final_fwd_bwd",
        in_specs=[row, row, pl.BlockSpec((6, D), lambda i: (0, 0)), vec, vec, row],
        out_specs=[row, row, pl.BlockSpec((8, D), lambda i: (0, 0))],
        out_shape=[_sds((T, D), BF16), _sds((T, D), F32), _sds((8, D), F32)],
        compiler_params=_params(("arbitrary",)),
    )(x1, ff, mod6, ln2_g, ln2_b, tgt)


def _mid_bwd(x2, mix, x1, dx1a, dh2, mod6, ln1_g):
    T, D = x2.shape
    tm = _tile(T, 256, 8)

    def body(x_ref, mix_ref, x1_ref, dx1a_ref, dh2_ref, mod_ref, g_ref, dmix_ref, dxa_ref, vec_ref):
        @pl.when(pl.program_id(0) == 0)
        def _():
            vec_ref[...] = jnp.zeros_like(vec_ref)

        dh2 = dh2_ref[...]
        xh, rstd = _ln(x1_ref[...])
        dx1 = dx1a_ref[...] + _ln_bwd(dh2 * (1.0 + mod_ref[4:5, :]), xh, rstd)
        mix_v = mix_ref[...]
        gate1 = mod_ref[2:3, :]
        zh, rstdz = _ln(ALPHA * x_ref[...] + gate1 * mix_v)
        dz = _ln_bwd(dx1 * g_ref[...], zh, rstdz)
        dmix_ref[...] = (gate1 * dz).astype(BF16)
        dxa_ref[...] = ALPHA * dz
        vec_ref[0:1, :] += _colsum(dh2 * xh)
        vec_ref[1:2, :] += _colsum(dh2)
        vec_ref[2:3, :] += _colsum(dx1 * zh)
        vec_ref[3:4, :] += _colsum(dx1)
        vec_ref[4:5, :] += _colsum(dz * mix_v)

    row = pl.BlockSpec((tm, D), lambda i: (i, 0))
    vec = pl.BlockSpec((1, D), lambda i: (0, 0))
    return pl.pallas_call(
        body, grid=(T // tm,), name="mid_bwd",
        in_specs=[row, row, row, row, row, pl.BlockSpec((6, D), lambda i: (0, 0)), vec],
        out_specs=[row, row, pl.BlockSpec((8, D), lambda i: (0, 0))],
        out_shape=[_sds((T, D), BF16), _sds((T, D), F32), _sds((8, D), F32)],
        compiler_params=_params(("arbitrary",)),
    )(x2, mix, x1, dx1a, dh2, mod6, ln1_g)


def _first_bwd(x2, dh1, dxa, mod6):
    T, D = x2.shape
    tm = _tile(T, 256, 8)

    def body(x_ref, dh1_ref, dxa_ref, mod_ref, gx_ref, vec_ref):
        @pl.when(pl.program_id(0) == 0)
        def _():
            vec_ref[...] = jnp.zeros_like(vec_ref)

        dh1 = dh1_ref[...]
        xh, rstd = _ln(x_ref[...])
        gx_ref[...] = dxa_ref[...] + _ln_bwd(dh1 * (1.0 + mod_ref[1:2, :]), xh, rstd)
        vec_ref[0:1, :] += _colsum(dh1 * xh)
        vec_ref[1:2, :] += _colsum(dh1)

    row = pl.BlockSpec((tm, D), lambda i: (i, 0))
    return pl.pallas_call(
        body, grid=(T // tm,), name="first_bwd",
        in_specs=[row, row, row, pl.BlockSpec((6, D), lambda i: (0, 0))],
        out_specs=[row, pl.BlockSpec((8, D), lambda i: (0, 0))],
        out_shape=[_sds((T, D), F32), _sds((8, D), F32)],
        compiler_params=_params(("arbitrary",)),
    )(x2, dh1, dxa, mod6)


def _slot(j):
    return (j % 2) * 4 + j // 2


def _mm_gathered(a, wg, shards, out, name):
    M, K = a.shape
    _, _, ns = wg.shape
    tm = _tile(M, 512, 16)

    def body(shards_ref, a_ref, w_ref, prev_ref, o_ref):
        o_ref[...] = _dot(a_ref[...], w_ref[...], NN)

    return pl.pallas_call(
        body, name=name,
        grid_spec=pltpu.PrefetchScalarGridSpec(
            num_scalar_prefetch=1, grid=(shards.shape[0], M // tm),
            in_specs=[pl.BlockSpec((tm, K), lambda j, i, s: (i, 0)),
                      pl.BlockSpec((None, K, ns), lambda j, i, s: (s[j], 0, 0)), ORDER_ONLY],
            out_specs=pl.BlockSpec((tm, ns), lambda j, i, s: (i, s[j]))),
        out_shape=_sds((M, N_DEV * ns), F32), input_output_aliases={3: 0},
        compiler_params=_params(("parallel", "parallel"), big=True),
    )(shards, a, wg, out)


def _mm_nn(a, b, name):
    M, K = a.shape
    _, N = b.shape
    tm, tn = _tile(M, 512, 16), _tile(N, 1024, LANE)

    def body(a_ref, b_ref, o_ref):
        o_ref[...] = _dot(a_ref[...], b_ref[...], NN)

    return pl.pallas_call(
        body, grid=(N // tn, M // tm), name=name,
        in_specs=[pl.BlockSpec((tm, K), lambda j, i: (i, 0)), pl.BlockSpec((K, tn), lambda j, i: (0, j))],
        out_specs=pl.BlockSpec((tm, tn), lambda j, i: (i, j)),
        out_shape=_sds((M, N), F32), compiler_params=_params(("parallel", "parallel"), big=True),
    )(a, b)


def _mm_nt(a, b, name):
    M, K = a.shape
    N, _ = b.shape
    tm, tn = _tile(M, 512, 16), _tile(N, 1024, LANE)

    def body(a_ref, b_ref, o_ref):
        o_ref[...] = _dot(a_ref[...], b_ref[...], NT)

    return pl.pallas_call(
        body, grid=(M // tm, N // tn), name=name,
        in_specs=[pl.BlockSpec((tm, K), lambda i, j: (i, 0)), pl.BlockSpec((tn, K), lambda i, j: (j, 0))],
        out_specs=pl.BlockSpec((tm, tn), lambda i, j: (i, j)),
        out_shape=_sds((M, N), F32), compiler_params=_params(("parallel", "parallel"), big=True),
    )(a, b)


def _mm_swiglu(h2, wg):
    M, K = h2.shape
    _, _, ns = wg.shape
    half = N_DEV // 2
    tm = _tile(M, 256, 16)

    def body(a_ref, wgate_ref, wup_ref, gu_ref, act_ref):
        a = a_ref[...]
        g = _dot(a, wgate_ref[...], NN)
        u = _dot(a, wup_ref[...], NN)
        sg = _sigmoid(g)
        silu = g * sg
        gu_ref[0] = u * (sg * (1.0 + g * (1.0 - sg)))
        gu_ref[1] = silu
        act_ref[...] = (silu * u).astype(BF16)

    return pl.pallas_call(
        body, grid=(half, M // tm), name="ffn_in_swiglu",
        in_specs=[pl.BlockSpec((tm, K), lambda j, i: (i, 0)),
                  pl.BlockSpec((None, K, ns), lambda j, i: (j, 0, 0)),
                  pl.BlockSpec((None, K, ns), lambda j, i: (j + half, 0, 0))],
        out_specs=[pl.BlockSpec((2, tm, ns), lambda j, i: (0, i, j)), pl.BlockSpec((tm, ns), lambda j, i: (i, j))],
        out_shape=[_sds((2, M, half * ns), F32), _sds((M, half * ns), BF16)],
        compiler_params=_params(("parallel", "parallel"), big=True),
    )(h2, wg, wg)


def _mm_swiglu_bwd(dff, w2, gu):
    M, K = dff.shape
    F = w2.shape[0]
    tm, tn = _tile(M, 512, 16), _tile(F, 1408, LANE)

    def body(a_ref, b_ref, gu_ref, du_ref):
        da = _dot(a_ref[...], b_ref[...], NT)
        du_ref[0] = (da * gu_ref[0]).astype(BF16)
        du_ref[1] = (da * gu_ref[1]).astype(BF16)

    return pl.pallas_call(
        body, grid=(F // tn, M // tm), name="ffn_out_bwd_swiglu",
        in_specs=[pl.BlockSpec((tm, K), lambda j, i: (i, 0)), pl.BlockSpec((tn, K), lambda j, i: (j, 0)),
                  pl.BlockSpec((2, tm, tn), lambda j, i: (0, i, j))],
        out_specs=pl.BlockSpec((2, tm, tn), lambda j, i: (0, i, j)),
        out_shape=_sds((2, M, F), BF16), compiler_params=_params(("parallel", "parallel"), big=True),
    )(dff, w2, gu)


ORDER_ONLY = pl.BlockSpec(memory_space=pl.ANY)


def _mm_tn_rows(dep, a, b, rs, name):
    M, Ka = a.shape
    _, N = b.shape

    def body(_, a_ref, b_ref, o_ref):
        g = _dot(a_ref[...], b_ref[...], TN)
        o_ref[0, 0] = g[0:rs, :].astype(BF16)
        o_ref[1, 0] = g[rs:2 * rs, :].astype(BF16)

    return pl.pallas_call(
        body, grid=(N_DEV // 2,), name=name,
        in_specs=[ORDER_ONLY, pl.BlockSpec((M, 2 * rs), lambda ch: (0, ch)), pl.BlockSpec((M, N), lambda ch: (0, 0))],
        out_specs=pl.BlockSpec((2, 1, rs, N), lambda ch: (0, ch, 0, 0)),
        out_shape=_sds((2, N_DEV // 2, rs, N), BF16),
        compiler_params=_params(("parallel",), big=True),
    )(dep, a, b)


def _mm_gathered_nt(dep, a, a_spec, wg, M, tm, name, first=0, count=None, out=None):
    _, K, ns = wg.shape
    count = M // tm if count is None else count
    out = lax.empty((M, K), F32) if out is None else out

    def body(_, a_ref, w_ref, prev_ref, o_ref):
        @pl.when(pl.program_id(1) == 0)
        def _():
            o_ref[...] = jnp.zeros_like(o_ref)

        o_ref[...] += _dot(a_ref[:, 0:ns], w_ref[0], NT) + _dot(a_ref[:, ns:2 * ns], w_ref[1], NT)

    return pl.pallas_call(
        body, grid=(count, N_DEV // 2), name=name,
        in_specs=[ORDER_ONLY, a_spec(tm, 2 * ns, first), pl.BlockSpec((2, K, ns), lambda i, p: (p, 0, 0)), ORDER_ONLY],
        out_specs=pl.BlockSpec((tm, K), lambda i, j: (i + first, 0)),
        out_shape=_sds((M, K), F32), input_output_aliases={3: 0},
        compiler_params=_params(("parallel", "arbitrary"), big=True),
    )(dep, a, wg, out)


def _mm_tn_gathered(dep, h, a, a_spec, ns, name):
    M, K = h.shape

    def body(_, h_ref, a_ref, o_ref):
        o_ref[...] = _dot(h_ref[...], a_ref[...], TN).astype(BF16)

    return pl.pallas_call(
        body, grid=(N_DEV,), name=name,
        in_specs=[ORDER_ONLY, pl.BlockSpec((M, K), lambda j: (0, 0)), a_spec(M, ns)],
        out_specs=pl.BlockSpec((None, K, ns), lambda j: (_slot(j), 0, 0)),
        out_shape=_sds((N_DEV, K, ns), BF16),
        compiler_params=_params(("parallel",), big=True),
    )(dep, h, a)


def _bias_onehot(rbp, max_rel):
    r = lax.broadcasted_iota(jnp.int32, (rbp, TAB), 0)
    m = lax.broadcasted_iota(jnp.int32, (rbp, TAB), 1)
    dist = KPAD - jnp.where(m < WIN, m, m - TAB)
    return (r == jnp.clip(dist, -max_rel, max_rel) + max_rel).astype(F32)


def _attn_setup(i, hp, k_ref, v_ref, gv_ref, kpad, vpad, bias):
    ls = slice(i * ATTN_HEAD_DIM, (i + 1) * ATTN_HEAD_DIM)
    kpad[i][0:KPAD, :] = jnp.zeros((KPAD, ATTN_HEAD_DIM), BF16)
    vpad[i][0:KPAD, :] = jnp.zeros((KPAD, ATTN_HEAD_DIM), BF16)
    kpad[i][KPAD:, :] = k_ref[:, ls].astype(BF16)
    vpad[i][KPAD:, :] = v_ref[:, ls].astype(BF16)
    gvrow = gv_ref[pl.ds(hp * ATTN_HEADS_PER_STEP + i, 1), :]
    tab = pltpu.roll(jnp.broadcast_to(gvrow, (QBLK, TAB)), 0, 1, stride=1, stride_axis=0)
    row = lax.broadcasted_iota(jnp.int32, (QBLK, WIN), 0)
    col = lax.broadcasted_iota(jnp.int32, (QBLK, WIN), 1)
    first = jnp.bitwise_and(row, -CHUNK)
    seen = jnp.logical_and(col >= first, col < first + (N_PAST + 1) * CHUNK)
    bias[i][...] = jnp.where(seen, tab[:, 0:WIN], NEG)


def _attn_probs(b, q_ref, kpad, vpad, bias, col):
    pair = range(ATTN_HEADS_PER_STEP)
    ls = [slice(i * ATTN_HEAD_DIM, (i + 1) * ATTN_HEAD_DIM) for i in pair]
    r0 = pl.multiple_of(b * QBLK, QBLK)
    q = [q_ref[pl.ds(r0, QBLK), ls[i]].astype(BF16) for i in pair]
    kw = [kpad[i][pl.ds(r0, WIN), :] for i in pair]
    vw = [vpad[i][pl.ds(r0, WIN), :] for i in pair]
    s = [_dot(q[i], kw[i], NT) * (ATTN_HEAD_DIM ** -0.5) + bias[i][...] for i in pair]
    s = [jnp.where(col >= KPAD - r0, s[i], NEG) for i in pair]
    p = [jnp.exp(s[i] - jnp.max(s[i], axis=-1, keepdims=True)) for i in pair]
    pn = [p[i] / jnp.sum(p[i], axis=-1, keepdims=True) for i in pair]
    return r0, ls, q, kw, vw, pn


def _attn_fwd(proj, gv, ga, AW):
    T = proj.shape[0]
    AH = ATTN_HEADS_PER_STEP
    W = AH * ATTN_HEAD_DIM
    HP = AW // W

    def body(q_ref, k_ref, v_ref, gv_ref, ga_ref, o_ref, *scratch):
        kpad, vpad, bias = (scratch[k * AH:(k + 1) * AH] for k in range(3))
        hp = pl.program_id(0)
        for i in range(AH):
            _attn_setup(i, hp, k_ref, v_ref, gv_ref, kpad, vpad, bias)
        col = lax.broadcasted_iota(jnp.int32, (QBLK, WIN), 1)

        def block(b, carry):
            pair = range(AH)
            r0, ls, _, _, vw, pn = _attn_probs(b, q_ref, kpad, vpad, bias, col)
            o = [_dot(pn[i].astype(BF16), vw[i], NN) for i in pair]
            r = [lax.rsqrt(jnp.mean(o[i] * o[i], axis=-1, keepdims=True) + EPS) for i in pair]
            outs = [o[i] * r[i] * ga_ref[0:1, ls[i]] for i in pair]
            o_ref[pl.ds(r0, QBLK), :] = jnp.concatenate(outs, axis=1).astype(BF16)
            return carry

        lax.fori_loop(0, T // QBLK, block, 0)

    blk = lambda off: pl.BlockSpec((T, W), lambda hp: (0, off + hp))
    return pl.pallas_call(
        body, grid=(HP,), name="attn_fwd",
        in_specs=[blk(0), blk(HP), blk(2 * HP), pl.BlockSpec(gv.shape, lambda hp: (0, 0)),
                  pl.BlockSpec((1, W), lambda hp: (0, hp))],
        out_specs=pl.BlockSpec((T, W), lambda hp: (0, hp)),
        out_shape=_sds((T, AW), BF16),
        scratch_shapes=[pltpu.VMEM((T + KPAD, ATTN_HEAD_DIM), BF16)] * (2 * AH) + [pltpu.VMEM((QBLK, WIN), F32)] * AH,
        compiler_params=_params(("parallel",), big=True),
    )(proj, proj, proj, gv, ga)


def _attn_bwd(proj, dmixin, gv, ga, AW):
    T = proj.shape[0]
    AH = ATTN_HEADS_PER_STEP
    W = AH * ATTN_HEAD_DIM
    HP = AW // W
    scale = ATTN_HEAD_DIM ** -0.5

    def body(q_ref, k_ref, v_ref, dn_ref, gv_ref, ga_ref, dq_ref, dk_ref, dv_ref, dgv_ref, dga_ref, *scratch):
        kpad, vpad, dkacc, dvacc, bias, dbias = (scratch[k * AH:(k + 1) * AH] for k in range(6))
        hp = pl.program_id(0)
        for i in range(AH):
            _attn_setup(i, hp, k_ref, v_ref, gv_ref, kpad, vpad, bias)
            dkacc[i][...] = jnp.zeros_like(dkacc[i])
            dvacc[i][...] = jnp.zeros_like(dvacc[i])
            dbias[i][...] = jnp.zeros_like(dbias[i])
        dga_ref[...] = jnp.zeros_like(dga_ref)
        col = lax.broadcasted_iota(jnp.int32, (QBLK, WIN), 1)

        def block(b, carry):
            pair = range(AH)
            r0, lss, qs, kws, vws, pns = _attn_probs(b, q_ref, kpad, vpad, bias, col)
            pn_b = [pns[i].astype(BF16) for i in pair]
            o = [_dot(pn_b[i], vws[i], NN) for i in pair]
            r = [lax.rsqrt(jnp.mean(o[i] * o[i], axis=-1, keepdims=True) + EPS) for i in pair]
            dn = [dn_ref[pl.ds(r0, QBLK), lss[i]] for i in pair]
            for i in pair:
                dga_ref[i:i + 1, :] += _colsum(dn[i] * o[i] * r[i])
            a = [dn[i] * ga_ref[0:1, lss[i]] for i in pair]
            do_b = [(r[i] * (a[i] - o[i] * (r[i] * r[i]) * jnp.mean(a[i] * o[i], axis=-1, keepdims=True))).astype(BF16)
                    for i in pair]
            dp = [_dot(do_b[i], vws[i], NT) for i in pair]
            for i in pair:
                dvacc[i][pl.ds(r0, WIN), :] += _dot(pn_b[i], do_b[i], TN)
            ds = [pns[i] * (dp[i] - jnp.sum(pns[i] * dp[i], axis=-1, keepdims=True)) for i in pair]
            for i in pair:
                dbias[i][...] += ds[i]
            ds_b = [ds[i].astype(BF16) for i in pair]
            dq = [_dot(ds_b[i], kws[i], NN) * scale for i in pair]
            dq_ref[pl.ds(r0, QBLK), :] = jnp.concatenate(dq, axis=1).astype(BF16)
            for i in pair:
                dkacc[i][pl.ds(r0, WIN), :] += _dot(ds_b[i], qs[i], TN) * scale
            return carry

        lax.fori_loop(0, T // QBLK, block, 0)

        rr = lax.broadcasted_iota(jnp.int32, (QBLK, QBLK), 0)
        cc = lax.broadcasted_iota(jnp.int32, (QBLK, QBLK), 1)
        flip = (rr + cc == QBLK - 1).astype(BF16)
        for i in range(AH):
            ls = slice(i * ATTN_HEAD_DIM, (i + 1) * ATTN_HEAD_DIM)
            dk_ref[:, ls] = dkacc[i][KPAD:, :].astype(BF16)
            dv_ref[:, ls] = dvacc[i][KPAD:, :].astype(BF16)
            full = jnp.concatenate([dbias[i][...], jnp.zeros((QBLK, TAB - WIN), F32)], axis=1)
            hi = full.astype(BF16)
            lo = (full - hi.astype(F32)).astype(BF16)
            rev = _dot(flip, hi, NN) + _dot(flip, lo, NN)
            dgv_ref[i:i + 1, :] = _colsum(pltpu.roll(rev, TAB - (QBLK - 1), 1, stride=1, stride_axis=0))

    blk = lambda off: pl.BlockSpec((T, W), lambda hp: (0, off + hp))
    accs = lambda dt: [pltpu.VMEM((T + KPAD, ATTN_HEAD_DIM), dt)] * AH
    return pl.pallas_call(
        body, grid=(HP,), name="attn_bwd",
        in_specs=[blk(0), blk(HP), blk(2 * HP), blk(0), pl.BlockSpec(gv.shape, lambda hp: (0, 0)),
                  pl.BlockSpec((1, W), lambda hp: (0, hp))],
        out_specs=[blk(0), blk(0), blk(0), pl.BlockSpec((None, AH, TAB), lambda hp: (hp, 0, 0)),
                   pl.BlockSpec((None, AH, ATTN_HEAD_DIM), lambda hp: (hp, 0, 0))],
        out_shape=[_sds((T, AW), BF16), _sds((T, AW), BF16), _sds((T, AW), BF16),
                   _sds((HP, AH, TAB), F32), _sds((HP, AH, ATTN_HEAD_DIM), F32)],
        scratch_shapes=accs(BF16) + accs(BF16) + accs(F32) + accs(F32) + [pltpu.VMEM((QBLK, WIN), F32)] * (2 * AH),
        compiler_params=_params(("parallel",), big=True),
    )(proj, proj, proj, dmixin, gv, ga)


def _ltri():
    r = lax.broadcasted_iota(jnp.int32, (CHUNK, CHUNK), 0)
    c = lax.broadcasted_iota(jnp.int32, (CHUNK, CHUNK), 1)
    return (c <= r).astype(BF16)


def _tri_dot(tri, v, dims):
    hi = v.astype(BF16)
    lo = (v - hi.astype(F32)).astype(BF16)
    return _dot(tri, hi, dims) + _dot(tri, lo, dims)


HEADS_PER_STEP = (8, 2)
REC_ROW_TILE = 512


def _alternate(stages):
    live = list(stages)
    while live:
        for g in list(live):
            if next(g, StopIteration) is StopIteration:
                live.remove(g)


def _hgrn_gates(n, ls, q_ref, f_ref, lb_ref, ltri):
    r0 = pl.multiple_of(n * CHUNK, CHUNK)
    rows = pl.ds(r0, CHUNK)
    lb = lb_ref[:, ls]
    qb = q_ref[rows, ls]
    sg = _sigmoid(f_ref[rows, ls])
    f = lb + (1.0 - lb) * sg
    sq = _sigmoid(qb)
    b = _tri_dot(ltri, jnp.log(f), NN)
    return rows, lb, qb, sg, f, 1.0 - f, sq, qb * sq, b


def _hgrn_specs(T, RW, AW, backward):
    HG = HEADS_PER_STEP[1 if backward else 0]
    W = HG * LANE
    TT = _tile(T, REC_ROW_TILE, CHUNK)
    n_row_tiles = T // TT
    base = 3 * AW // W
    row = (lambda t: n_row_tiles - 1 - t) if backward else (lambda t: t)
    blk_in = lambda off: pl.BlockSpec((TT, W), lambda g, t: (row(t), base + off + g))
    col = pl.BlockSpec((TT, W), lambda g, t: (row(t), g))
    states = pl.BlockSpec((HG, TT // CHUNK, LANE, LANE), lambda g, t: (g, row(t), 0, 0))
    return HG, W, RW // W, TT, n_row_tiles, blk_in, col, states


def _hgrn_fwd(proj, lb, gn, AW, RW):
    T = proj.shape[0]
    RH, NC, NSUB = RW // LANE, T // CHUNK, CHUNK // SUB
    HG, W, NG, TT, n_row_tiles, blk_in, col, states = _hgrn_specs(T, RW, AW, False)

    def body(q_ref, f_ref, i_ref, g_ref, lb_ref, gn_ref, mix_ref, o_ref, stall_ref, st_all, bs_all, kks_all, ics_all):
        @pl.when(pl.program_id(1) == 0)
        def _():
            st_all[...] = jnp.zeros_like(st_all)

        ltri = _ltri()
        rowi = lax.broadcasted_iota(jnp.int32, (SUB, 1), 0)

        def one_head(h, n):
            ls = slice(h * LANE, (h + 1) * LANE)
            st, bs, kks, ics = st_all.at[h], bs_all.at[h], kks_all.at[h], ics_all.at[h]
            rows, _, _, _, _, kk, _, qs, b = _hgrn_gates(n, ls, q_ref, f_ref, lb_ref, ltri)
            ic = i_ref[rows, ls]
            stv = st[...]
            stall_ref[h, n] = stv
            bs[...] = b
            kks[...] = kk
            ics[...] = ic
            yield
            o = _dot((qs * jnp.exp(b)).astype(BF16), stv.astype(BF16), NT)
            yield
            ic_b = ic.astype(BF16)
            pieces = []
            for blk in range(NSUB):
                s0 = blk * SUB
                bI, qI = b[s0:s0 + SUB], qs[s0:s0 + SUB]
                if blk == 0:
                    oI = jnp.zeros((SUB, LANE), F32)
                else:
                    ref = bs[s0 - 1:s0, :]
                    qt = (qI * jnp.exp(bI - ref)).astype(BF16)
                    kt = (kk[0:s0] * jnp.exp(ref - b[0:s0])).astype(BF16)
                    oI = _dot(_dot(qt, kt, NT).astype(BF16), ic_b[0:s0], NN)
                    yield
                acc = [oI[g * ROWS:(g + 1) * ROWS] for g in range(SUB // ROWS)]
                for s in range(SUB):
                    sr = s0 + s
                    g0 = s // ROWS
                    lo = g0 * ROWS
                    e = jnp.exp(jnp.minimum(bI[lo:] - bs[sr:sr + 1, :], 0.0))
                    a = jnp.sum(qI[lo:] * kks[sr:sr + 1, :] * e, axis=-1, keepdims=True)
                    add = jnp.where(rowi[lo:] >= s, a, 0.0) * ics[sr:sr + 1, :]
                    for g in range(g0, SUB // ROWS):
                        acc[g] = acc[g] + add[(g - g0) * ROWS:(g - g0 + 1) * ROWS]
                    yield
                pieces.extend(acc)
            o = o + jnp.concatenate(pieces, axis=0)
            bl = bs[CHUNK - 1:CHUNK, :]
            kd = (kk * jnp.exp(bl - b)).astype(BF16)
            st[...] = stv * jnp.exp(bl) + _dot(ic_b, kd, TN)
            yield
            o_ref[rows, ls] = o
            r = lax.rsqrt(jnp.mean(o * o, axis=-1, keepdims=True) + EPS)
            gb = g_ref[rows, ls]
            mix_ref[rows, ls] = (o * r * gn_ref[...] * (gb * _sigmoid(gb))).astype(BF16)

        def chunk(n, carry):
            _alternate([one_head(h, n) for h in range(HG)])
            return carry

        lax.fori_loop(0, TT // CHUNK, chunk, 0)

    tile = pltpu.VMEM((HG, CHUNK, LANE), F32)
    return pl.pallas_call(
        body, grid=(NG, n_row_tiles), name="hgrn_fwd",
        in_specs=[blk_in(0), blk_in(NG), blk_in(2 * NG), blk_in(3 * NG), pl.BlockSpec((1, W), lambda g, t: (0, g)),
                  pl.BlockSpec((1, LANE), lambda g, t: (0, 0))],
        out_specs=[col, col, states],
        out_shape=[_sds((T, RW), BF16), _sds((T, RW), F32), _sds((RH, NC, LANE, LANE), F32)],
        scratch_shapes=[pltpu.VMEM((HG, LANE, LANE), F32), tile, tile, tile],
        compiler_params=_params(("parallel", "arbitrary"), big=True),
    )(proj, proj, proj, proj, lb, gn)


def _hgrn_bwd(proj, dmixin, o_b, st_all, lb, gn, AW, RW):
    T = proj.shape[0]
    RH, NC, NSUB = RW // LANE, T // CHUNK, CHUNK // SUB
    HG, W, NG, TT, n_row_tiles, blk_in, col, states = _hgrn_specs(T, RW, AW, True)

    def body(q_ref, f_ref, i_ref, g_ref, o_ref, dn_ref, stall_ref, lb_ref, gn_ref,
             dq_ref, df_ref, di_ref, dg_ref, dlb_ref, dgn_ref, dst_all, bs_all, qss_all, dos_all, p2_all, dic_all,
             p1_all):
        @pl.when(pl.program_id(1) == 0)
        def _():
            dst_all[...] = jnp.zeros_like(dst_all)
            dlb_ref[...] = jnp.zeros_like(dlb_ref)
            dgn_ref[...] = jnp.zeros_like(dgn_ref)

        ltri = _ltri()
        rowi = lax.broadcasted_iota(jnp.int32, (SUB, 1), 0)
        last = lax.broadcasted_iota(jnp.int32, (CHUNK, 1), 0) == CHUNK - 1

        def one_head(h, n):
            ls = slice(h * LANE, (h + 1) * LANE)
            dst, bs, qss, dos = dst_all.at[h], bs_all.at[h], qss_all.at[h], dos_all.at[h]
            p2, dic, p1s = p2_all.at[h], dic_all.at[h], p1_all.at[h]
            rows, lbv, qb, sg, f, kk, sq, qs, b = _hgrn_gates(n, ls, q_ref, f_ref, lb_ref, ltri)
            ic = i_ref[rows, ls]
            stv = stall_ref[h, n]
            dstv = dst[...]
            o = o_ref[rows, ls]
            dn = dn_ref[rows, ls]
            gb = g_ref[rows, ls]
            sgb = _sigmoid(gb)
            r = lax.rsqrt(jnp.mean(o * o, axis=-1, keepdims=True) + EPS)
            gnv = gn_ref[...]
            dg_ref[rows, ls] = (dn * (o * r * gnv) * (sgb * (1.0 + gb * (1.0 - sgb)))).astype(BF16)
            dy = dn * (gb * sgb)
            dgn_ref[h] += _colsum(dy * o * r)
            a_ = dy * gnv
            do = r * (a_ - o * (r * r) * jnp.mean(a_ * o, axis=-1, keepdims=True))
            do_b = do.astype(BF16)
            bs[...] = b
            qss[...] = qs
            dos[...] = do
            yield
            ic_b = ic.astype(BF16)
            eb = jnp.exp(b)
            bl = bs[CHUNK - 1:CHUNK, :]
            ebl = jnp.exp(bl)
            dec = jnp.exp(bl - b)
            kd = (kk * dec).astype(BF16)
            dst_b = dstv.astype(BF16)
            dqs = _dot(do_b, stv.astype(BF16), NN) * eb
            dkk2 = _dot(ic_b, dst_b, NN) * dec
            dic[...] = _dot(kd, dst_b, NT)
            dbl = ebl * _colsum(stv * dstv) + _colsum(kk * dkk2)
            dst[...] = dstv * ebl + _dot(do_b, (qs * eb).astype(BF16), TN)
            yield
            p2[...] = jnp.zeros_like(p2)
            p1_pieces = []
            for blk in range(NSUB):
                s0 = blk * SUB
                bI, qI, doI = b[s0:s0 + SUB], qs[s0:s0 + SUB], do[s0:s0 + SUB]
                if blk == 0:
                    p1 = jnp.zeros((SUB, LANE), F32)
                else:
                    ref = bs[s0 - 1:s0, :]
                    eq = jnp.exp(bI - ref)
                    ek = jnp.exp(ref - b[0:s0])
                    qt = (qI * eq).astype(BF16)
                    kt = (kk[0:s0] * ek).astype(BF16)
                    doI_b = doI.astype(BF16)
                    dic[0:s0, :] += _dot(_dot(qt, kt, NT).astype(BF16), doI_b, TN)
                    da = _dot(doI_b, ic_b[0:s0], NT).astype(BF16)
                    p1 = _dot(da, kt, NN) * eq
                    p2[0:s0, :] += _dot(da, qt, TN) * ek
                    yield
                p1_pieces.append(p1)
                kkI, icI = kk[s0:s0 + SUB], ic[s0:s0 + SUB]
                p2acc = [jnp.zeros((ROWS, LANE), F32) for _ in range(SUB // ROWS)]
                diacc = [jnp.zeros((ROWS, LANE), F32) for _ in range(SUB // ROWS)]
                for t in range(SUB):
                    tr = s0 + t
                    ng = t // ROWS + 1
                    hi = ng * ROWS
                    keep = rowi[:hi] <= t
                    do_t = dos[tr:tr + 1, :]
                    e = jnp.exp(jnp.minimum(bs[tr:tr + 1, :] - bI[:hi], 0.0))
                    qe = qss[tr:tr + 1, :] * e
                    a = jnp.where(keep, jnp.sum(kkI[:hi] * qe, axis=-1, keepdims=True), 0.0)
                    da = jnp.where(keep, jnp.sum(icI[:hi] * do_t, axis=-1, keepdims=True), 0.0)
                    dp2, ddi = da * qe, a * do_t
                    for g in range(ng):
                        p2acc[g] = p2acc[g] + dp2[g * ROWS:(g + 1) * ROWS]
                        diacc[g] = diacc[g] + ddi[g * ROWS:(g + 1) * ROWS]
                    p1s[tr:tr + 1, :] = _colsum(da * kkI[:hi] * e)
                    yield
                p2[s0:s0 + SUB, :] += jnp.concatenate(p2acc, axis=0)
                dic[s0:s0 + SUB, :] += jnp.concatenate(diacc, axis=0)
            dqs = dqs + jnp.concatenate(p1_pieces, axis=0) + p1s[...]
            dkk = dkk2 + p2[...]
            db = qs * dqs - kk * dkk + jnp.where(last, dbl, 0.0)
            dgl = _tri_dot(ltri, db, TN)
            yield
            dfv = dgl / f - dkk
            df_ref[rows, ls] = (dfv * (1.0 - lbv) * sg * (1.0 - sg)).astype(BF16)
            dlb_ref[:, ls] += _colsum(dfv * (1.0 - sg))
            dq_ref[rows, ls] = (dqs * (sq * (1.0 + qb * (1.0 - sq)))).astype(BF16)
            di_ref[rows, ls] = dic[...].astype(BF16)

        def chunk(k, carry):
            _alternate([one_head(h, TT // CHUNK - 1 - k) for h in range(HG)])
            return carry

        lax.fori_loop(0, TT // CHUNK, chunk, 0)

    tile = pltpu.VMEM((HG, CHUNK, LANE), F32)
    return pl.pallas_call(
        body, grid=(NG, n_row_tiles), name="hgrn_bwd",
        in_specs=[blk_in(0), blk_in(NG), blk_in(2 * NG), blk_in(3 * NG), col,
                  pl.BlockSpec((TT, W), lambda g, t: (n_row_tiles - 1 - t, AW // W + g)), states,
                  pl.BlockSpec((1, W), lambda g, t: (0, g)), pl.BlockSpec((1, LANE), lambda g, t: (0, 0))],
        out_specs=[col, col, col, col, pl.BlockSpec((1, W), lambda g, t: (0, g)),
                   pl.BlockSpec((HG, 1, LANE), lambda g, t: (g, 0, 0))],
        out_shape=[_sds((T, RW), BF16)] * 4 + [_sds((1, RW), F32), _sds((RH, 1, LANE), F32)],
        scratch_shapes=[pltpu.VMEM((HG, LANE, LANE), F32), tile, tile, tile, tile, tile, tile],
        compiler_params=_params(("parallel", "arbitrary"), big=True),
    )(proj, proj, proj, proj, o_b, dmixin, st_all, lb, gn)


def _prep(c, lb_logits, rb_pad, max_rel, after):
    D, RW = c.shape[-1], lb_logits.shape[-1]
    H, rbp = rb_pad.shape

    def body(c_ref, l_ref, rb_ref, _, __, cact_ref, lb_ref, gv_ref):
        cv = c_ref[...]
        cact_ref[...] = cv * _sigmoid(cv)
        lb_ref[...] = _sigmoid(l_ref[0:1, :] - l_ref[1:2, :])
        gv_ref[...] = _dot(rb_ref[...], _bias_onehot(rbp, max_rel), NN, HIGHEST)

    vmem = pl.BlockSpec(memory_space=pltpu.VMEM)
    return pl.pallas_call(
        body, name="prep", in_specs=[vmem, vmem, vmem, ORDER_ONLY, ORDER_ONLY],
        out_shape=[_sds((1, D), F32), _sds((1, RW), F32), _sds((H, TAB), F32)],
    )(c, lb_logits, rb_pad, *after)


def _mod_part(c_all, w_ada_s, b_ada_s):
    B, D = c_all.shape
    ns = w_ada_s.shape[1]
    tn = _tile(ns, 768, LANE)

    def body(c_ref, w_ref, b_ref, o_ref):
        o_ref[...] = _dot(c_ref[...], w_ref[...], NN) + b_ref[...]

    return pl.pallas_call(
        body, grid=(ns // tn,), name="mod_part",
        in_specs=[pl.BlockSpec((B, D), lambda j: (0, 0)), pl.BlockSpec((D, tn), lambda j: (0, j)),
                  pl.BlockSpec((1, tn), lambda j: (0, j))],
        out_specs=pl.BlockSpec((B, tn), lambda j: (0, j)),
        out_shape=_sds((B, ns), F32), compiler_params=_params(("parallel",)),
    )(c_all, w_ada_s, b_ada_s)


def _adam(w, g, m, v):
    m = ADAM_B1 * m + (1.0 - ADAM_B1) * g
    v = ADAM_B2 * v + (1.0 - ADAM_B2) * (g * g)
    m_hat = m * (1.0 / (1.0 - ADAM_B1 ** ADAM_STEP))
    v_hat = v * (1.0 / (1.0 - ADAM_B2 ** ADAM_STEP))
    return -ADAM_LR * (m_hat / (jnp.sqrt(v_hat) + ADAM_EPS) + ADAM_WD * w), m, v


def _adam_ada(c_all, dmod_s, w, m, v):
    B, D = c_all.shape
    ns = w.shape[1]
    tr, tn = _tile(D, 512, LANE), _tile(ns, 768, LANE)

    def body(c_ref, d_ref, w_ref, m_ref, v_ref, g_out, dw_out, m_out, v_out):
        g = _dot(c_ref[...], d_ref[...], TN)
        g_out[...] = g
        dw_out[...], m_out[...], v_out[...] = _adam(w_ref[...], g, m_ref[...], v_ref[...])

    big = pl.BlockSpec((tr, tn), lambda i, j: (i, j))
    return pl.pallas_call(
        body, grid=(D // tr, ns // tn), name="adam_w_ada",
        in_specs=[pl.BlockSpec((B, tr), lambda i, j: (0, i)), pl.BlockSpec((B, tn), lambda i, j: (0, j)),
                  big, big, big],
        out_specs=[big] * 4, out_shape=[_sds((D, ns), F32)] * 4,
        compiler_params=_params(("parallel", "parallel")),
    )(c_all, dmod_s, w, m, v)


def _adam_shard(parts, w, m, v, name):
    R, C = w.shape
    tr = _tile(R, 256, 16)

    def body(p_ref, w_ref, m_ref, v_ref, g_out, dw_out, m_out, v_out):
        g = p_ref[0].astype(F32)
        for k in range(1, N_DEV // 2):
            g = g + p_ref[k].astype(F32)
        g_out[...] = g
        dw_out[...], m_out[...], v_out[...] = _adam(w_ref[...], g, m_ref[...], v_ref[...])

    big = pl.BlockSpec((tr, C), lambda i: (i, 0))
    return pl.pallas_call(
        body, grid=(R // tr,), name=name,
        in_specs=[pl.BlockSpec((N_DEV // 2, tr, C), lambda i: (0, i, 0)), big, big, big],
        out_specs=[big] * 4, out_shape=[_sds((R, C), F32)] * 4,
        compiler_params=_params(("parallel",), big=True),
    )(parts, w, m, v)


def _pair_sum(g8, land, core, name):
    _, NCHIP, R, C = g8.shape
    tr = _tile(R, 1024, 16)

    def body(core_ref, g_ref, l_ref, o_ref):
        o_ref[...] = g_ref[...] + l_ref[...]

    return pl.pallas_call(
        body, name=name,
        grid_spec=pltpu.PrefetchScalarGridSpec(
            num_scalar_prefetch=1, grid=(NCHIP, R // tr),
            in_specs=[pl.BlockSpec((None, None, tr, C), lambda k, i, core_ref: (core_ref[0], k, i, 0)),
                      pl.BlockSpec((None, tr, C), lambda k, i, core_ref: (k, i, 0))],
            out_specs=pl.BlockSpec((None, tr, C), lambda k, i, core_ref: (k, i, 0))),
        out_shape=_sds((NCHIP, R, C), BF16), compiler_params=_params(("parallel", "parallel")),
    )(core, g8, land)


SMALL = ("b_ada", "rel_bias", "attn_norm_g", "lb_logits", "gnorm_g", "ln1_g", "ln1_b", "ln2_g", "ln2_b")


def _small_update(parts, loss_parts, lbv, ws, ms, vs, max_rel):
    n = len(SMALL)

    def body(*refs):
        part_refs = dict(zip(SMALL, refs[:n]))
        loss_in, lb_ref = refs[n], refs[n + 1]
        w_refs, m_refs, v_refs = refs[n + 2:2 * n + 2], refs[2 * n + 2:3 * n + 2], refs[3 * n + 2:4 * n + 2]
        outs = refs[4 * n + 2:]

        def total(ref):
            tot = ref[0]
            for k in range(1, N_DEV):
                tot = tot + ref[k]
            return tot

        outs[0][...] = jnp.sum(total(loss_in), axis=-1, keepdims=True)
        for idx, name in enumerate(SMALL):
            g = total(part_refs[name])
            if name == "rel_bias":
                g = _dot(g, _bias_onehot(w_refs[idx].shape[1], max_rel), NT, HIGHEST)
            elif name == "lb_logits":
                lb = lb_ref[...]
                sign = (1 - 2 * lax.broadcasted_iota(jnp.int32, (2, 1), 0)).astype(F32)
                g = sign * (g * lb * (1.0 - lb))
            elif name == "gnorm_g":
                g = _colsum(g)
            dw, mm, vv = _adam(w_refs[idx][...], g, m_refs[idx][...], v_refs[idx][...])
            outs[1 + 4 * idx][...] = g
            outs[2 + 4 * idx][...] = dw
            outs[3 + 4 * idx][...] = mm
            outs[4 + 4 * idx][...] = vv

    out_shape = [_sds((1, 1), F32)]
    for w in ws:
        out_shape += [_sds(w.shape, F32)] * 4
    return pl.pallas_call(body, name="small_update", out_shape=out_shape, compiler_params=_params(big=True))(
        *[parts[k] for k in SMALL], loss_parts, lbv, *ws, *ms, *vs)


def _place():
    x, y, c = lax.axis_index("x"), lax.axis_index("y"), lax.axis_index("c")
    return x, y, c, [(1 - x, y), (x, 1 - y), (1 - x, 1 - y)]


def _all_gather(shard, name):
    HBM = pl.BlockSpec(memory_space=pl.ANY)

    def body(x_ref, out_ref, send_sems, recv_sems, local_sem):
        x, y, c, chips = _place()
        me, sibling = (x, y, c), (x, y, 1 - c)

        def slot(px, py, pc):
            return out_ref.at[4 * px + 2 * py + pc]

        def copy(k, block, to, src=None):
            return pltpu.make_async_remote_copy(
                src_ref=slot(*block) if src is None else src, dst_ref=slot(*block),
                send_sem=send_sems.at[k], recv_sem=recv_sems.at[k], device_id=to, device_id_type=MESH)

        mine = pltpu.make_async_copy(x_ref, slot(*me), local_sem)
        mine.start()
        first = [copy(0, me, sibling, src=x_ref)]
        first += [copy(1 + j, me, (*chip, c), src=x_ref) for j, chip in enumerate(chips)]
        for cp in first:
            cp.start()
        passed = [copy(4 + j, (*chip, c), sibling) for j, chip in enumerate(chips)]
        for j, chip in enumerate(chips):
            copy(1 + j, (*chip, c), me).wait_recv()
            passed[j].start()
        copy(0, sibling, me).wait_recv()
        for j, chip in enumerate(chips):
            copy(4 + j, (*chip, 1 - c), me).wait_recv()
        for cp in first + passed:
            cp.wait_send()
        mine.wait()

    return pl.pallas_call(
        body, name=name, out_shape=_sds((N_DEV,) + shard.shape, shard.dtype),
        in_specs=[HBM], out_specs=HBM,
        scratch_shapes=[pltpu.SemaphoreType.DMA((7,)), pltpu.SemaphoreType.DMA((7,)), pltpu.SemaphoreType.DMA(())],
    )(shard)


SEM_SPEC = pl.BlockSpec(memory_space=pltpu.SEMAPHORE)
HBM_SPEC = pl.BlockSpec(memory_space=pltpu.HBM)
EFFECT = pltpu.SideEffectType.DATAFLOW_SIDE_EFFECTING


def _remote(src, dst, send_sems, recv_sems, k, dev):
    return pltpu.make_async_remote_copy(src_ref=src, dst_ref=dst, send_sem=send_sems.at[k], recv_sem=recv_sems.at[k],
                                        device_id=dev, device_id_type=MESH)


def _copy_start(name, bufs, plan, n, after, only=None):
    nb = len(bufs)

    def body(*refs):
        send_sems, recv_sems = refs[nb + 1], refs[nb + 2]
        for k, (src, dst, dev) in enumerate(plan(*refs[:nb])):
            if only is not None and k not in only:
                continue
            _remote(src, dst, send_sems, recv_sems, k, dev).start()
        refs[-1][...] = jnp.zeros_like(refs[-1])

    out = pl.pallas_call(
        body, name=name,
        out_shape=(pltpu.SemaphoreType.DMA((n,)), pltpu.SemaphoreType.DMA((n,)),
                   *[pltpu.HBM(b.shape, b.dtype) for b in bufs], _sds((8, LANE), F32)),
        in_specs=[HBM_SPEC] * nb + [ORDER_ONLY],
        out_specs=(SEM_SPEC, SEM_SPEC, *[HBM_SPEC] * nb, pl.BlockSpec(memory_space=pltpu.VMEM)),
        input_output_aliases={i: 2 + i for i in range(nb)},
        compiler_params=pltpu.CompilerParams(has_side_effects=EFFECT),
    )(*[pltpu.with_memory_space_constraint(b, pltpu.HBM) for b in bufs], after)
    return (out[0], out[1]), list(out[2:2 + nb]), out[-1]


def _copy_wait(name, sems, bufs, plan, after, only=None):
    nb = len(bufs)

    def body(*refs):
        send_sems, recv_sems = refs[nb], refs[nb + 1]
        for k, (src, dst, dev) in enumerate(plan(*refs[:nb])):
            if only is not None and k not in only:
                continue
            cp = _remote(src, dst, send_sems, recv_sems, k, dev)
            cp.wait_send()
            cp.wait_recv()

    out = pl.pallas_call(
        body, name=name, out_shape=tuple(pltpu.HBM(b.shape, b.dtype) for b in bufs),
        in_specs=[HBM_SPEC] * nb + [SEM_SPEC, SEM_SPEC, pl.BlockSpec(memory_space=pl.ANY)],
        out_specs=tuple([HBM_SPEC] * nb), input_output_aliases={i: i for i in range(nb)},
        compiler_params=pltpu.CompilerParams(has_side_effects=EFFECT),
    )(*bufs, sems[0], sems[1], after)
    return list(out)


def _ag_plan_chips(shard_ref, out_ref):
    x, y, c, chips = _place()
    mine = out_ref.at[4 * x + 2 * y + c]
    return [(shard_ref, mine, (x, y, 1 - c))] + [(shard_ref, mine, (*chip, c)) for chip in chips]


def _ag_plan_pass(out_ref):
    x, y, c, chips = _place()
    slots = [out_ref.at[4 * chip[0] + 2 * chip[1] + c] for chip in chips]
    return [(s, s, (x, y, 1 - c)) for s in slots]


def _rs_plan_pair(g_ref, land_ref):
    x, y, c, _ = _place()
    return [(g_ref.at[1 - c], land_ref, (x, y, 1 - c))]


def _rs_plan_chips(p_ref, land_ref):
    x, y, c, chips = _place()
    return [(p_ref.at[2 * chip[0] + chip[1]], land_ref.at[2 * x + y], (*chip, c)) for chip in chips]


class _Gather:
    @staticmethod
    def landing(shard, me):
        return lax.dynamic_update_slice(lax.empty((N_DEV,) + shard.shape, shard.dtype), shard[None],
                                        (me,) + (0,) * shard.ndim)

    def __init__(self, shard, out, tag, after):
        self.tag = tag
        self.sems, (self.shard, self.out), self.token = _copy_start(
            "ag_start_" + tag, [shard, out], _ag_plan_chips, 4, after)
        self.groups = []

    def arrived(self, after, copies):
        name = "ag_wait_%s_%s" % (self.tag, "".join(map(str, copies)))
        self.shard, self.out = _copy_wait(name, self.sems, [self.shard, self.out], _ag_plan_chips, after, copies)
        return self.out

    def pass_on(self, after, blocks):
        name = "ag_pass_%s_%s" % (self.tag, "".join(map(str, blocks)))
        sems, (self.out,), _ = _copy_start(name, [self.out], _ag_plan_pass, 3, after, blocks)
        self.groups.append((sems, blocks))
        return self.out

    def passed(self, after, group):
        sems, blocks = self.groups[group]
        name = "ag_pass_wait_%s_%s" % (self.tag, "".join(map(str, blocks)))
        self.out = _copy_wait(name, sems, [self.out], _ag_plan_pass, after, blocks)[0]
        return self.out

    def arrived_from_chips(self, after):
        self.arrived(after, (0, 1, 2, 3))
        return self.pass_on(after, (0, 1, 2))

    def passed_on(self, after):
        return self.passed(after, 0)


def _ag_plan_direct(src_ref, out_ref):
    x, y, c, chips = _place()
    mine = out_ref.at[4 * x + 2 * y + c]
    peers = [(x, y, 1 - c)] + [(*chip, pc) for chip in chips for pc in (c, 1 - c)]
    return [(src_ref, mine, peer) for peer in peers]


class _SmallGather:
    def __init__(self, block, me, tag):
        self.tag = tag
        out = lax.dynamic_update_slice(lax.empty((N_DEV,) + block.shape, block.dtype), block[None],
                                       (me,) + (0,) * block.ndim)
        self.sems, self.bufs, self.token = _copy_start(
            "ag_direct_start_" + tag, [block, out], _ag_plan_direct, N_DEV - 1, jnp.zeros((1,), F32))

    def done(self, after):
        return _copy_wait("ag_direct_wait_" + self.tag, self.sems, self.bufs, _ag_plan_direct, after)[1]


class _ReduceScatter:
    def __init__(self, g8, tag):
        self.tag = tag
        land = lax.empty(g8.shape[1:], g8.dtype)
        self.sems, self.bufs, self.token = _copy_start(
            "rs_pair_start_" + tag, [g8, land], _rs_plan_pair, 1, jnp.zeros((1,), F32))

    def pair_done(self, core, chip, after):
        g8, land = _copy_wait("rs_pair_wait_" + self.tag, self.sems, self.bufs, _rs_plan_pair, after)
        p4 = _pair_sum(g8, land, core, "rs_pair_sum_" + self.tag)
        own = lax.dynamic_slice_in_dim(p4, chip, 1, axis=0)
        land2 = lax.dynamic_update_slice(lax.empty(p4.shape, p4.dtype), own, (chip, 0, 0))
        self.sems, self.bufs, self.token = _copy_start(
            "rs_chips_start_" + self.tag, [p4, land2], _rs_plan_chips, 3, jnp.zeros((1,), F32))

    def sums(self, after):
        return _copy_wait("rs_chips_wait_" + self.tag, self.sems, self.bufs, _rs_plan_chips, after)[1]


ORDER = ("w_ada", "b_ada", "w_in", "rel_bias", "attn_norm_g", "lb_logits", "gnorm_g", "w_o", "ln1_g", "ln1_b",
         "w_ffn_in", "w_ffn_out", "ln2_g", "ln2_b")


def kernel(x, c, w_ada, b_ada, w_in, rel_bias, attn_norm_g, lb_logits, gnorm_g, w_o, ln1_g, ln1_b, w_ffn_in, w_ffn_out, ln2_g, ln2_b, loss_target, m_w_ada, m_b_ada, m_w_in, m_rel_bias, m_attn_norm_g, m_lb_logits, m_gnorm_g, m_w_o, m_ln1_g, m_ln1_b, m_w_ffn_in, m_w_ffn_out, m_ln2_g, m_ln2_b, v_w_ada, v_b_ada, v_w_in, v_rel_bias, v_attn_norm_g, v_lb_logits, v_gnorm_g, v_w_o, v_ln1_g, v_ln1_b, v_w_ffn_in, v_w_ffn_out, v_ln2_g, v_ln2_b):
    W = dict(w_ada=w_ada, b_ada=b_ada, w_in=w_in, rel_bias=rel_bias, attn_norm_g=attn_norm_g, lb_logits=lb_logits,
             gnorm_g=gnorm_g, w_o=w_o, ln1_g=ln1_g, ln1_b=ln1_b, w_ffn_in=w_ffn_in, w_ffn_out=w_ffn_out,
             ln2_g=ln2_g, ln2_b=ln2_b)
    M = dict(w_ada=m_w_ada, b_ada=m_b_ada, w_in=m_w_in, rel_bias=m_rel_bias, attn_norm_g=m_attn_norm_g,
             lb_logits=m_lb_logits, gnorm_g=m_gnorm_g, w_o=m_w_o, ln1_g=m_ln1_g, ln1_b=m_ln1_b,
             w_ffn_in=m_w_ffn_in, w_ffn_out=m_w_ffn_out, ln2_g=m_ln2_g, ln2_b=m_ln2_b)
    V = dict(w_ada=v_w_ada, b_ada=v_b_ada, w_in=v_w_in, rel_bias=v_rel_bias, attn_norm_g=v_attn_norm_g,
             lb_logits=v_lb_logits, gnorm_g=v_gnorm_g, w_o=v_w_o, ln1_g=v_ln1_g, ln1_b=v_ln1_b,
             w_ffn_in=v_w_ffn_in, w_ffn_out=v_w_ffn_out, ln2_g=v_ln2_g, ln2_b=v_ln2_b)

    x2, tgt = x[0], loss_target[0]
    T, D = x2.shape
    AW, RW = attn_norm_g.shape[-1], lb_logits.shape[-1]
    MIX = AW + RW
    H, RH = AW // ATTN_HEAD_DIM, RW // LANE
    RB = rel_bias.shape[-1]
    max_rel = (RB - 1) // 2
    rbp = -(-RB // LANE) * LANE
    F = w_ffn_out.shape[1] * N_DEV
    half = N_DEV // 2
    xi, yi, ci = lax.axis_index("x"), lax.axis_index("y"), lax.axis_index("c")
    me = 4 * xi + 2 * yi + ci
    core = jnp.reshape(ci, (1,)).astype(jnp.int32)
    pad_rb = lambda a: jnp.pad(a[0], ((0, 0), (0, rbp - RB)))

    chip = 2 * xi + yi

    w_in_b = w_in[0].astype(BF16)
    w_in_land = _Gather.landing(w_in_b, me)
    c_act, lbv, gv = _prep(c, lb_logits, pad_rb(rel_bias), max_rel, (w_in_b, w_in_land))
    c_all = _all_gather(c_act, "ag_c").reshape(N_DEV, D)
    ns_ada = w_ada.shape[-1]
    mod_part = _mod_part(c_all, w_ada[0], lax.dynamic_slice_in_dim(b_ada, me * ns_ada, ns_ada, axis=1))
    mod_all = _all_gather(mod_part, "ag_mod")
    mod6 = lax.dynamic_index_in_dim(mod_all, me, axis=1, keepdims=False).reshape(6, D)

    bf = lambda w: w[0].astype(BF16)
    ag_in = _Gather(w_in_b, w_in_land, "w_in", mod_all)
    ag_o = _Gather(bf(w_o), _Gather.landing(bf(w_o), me), "w_o", ag_in.token)
    ag_f1 = _Gather(bf(w_ffn_in), _Gather.landing(bf(w_ffn_in), me), "w_ffn_in", ag_o.token)
    ag_f2 = _Gather(bf(w_ffn_out), _Gather.landing(bf(w_ffn_out), me), "w_ffn_out", ag_f1.token)

    h1 = _ln_mod(x2, mod6 + ag_f2.token[0, 0])
    ids = lambda pairs: jnp.stack([4 * px + 2 * py + pc for px, py, pc in pairs]).astype(jnp.int32)
    others = [(1 - xi, yi), (xi, 1 - yi), (1 - xi, 1 - yi)]
    proj = lax.empty((T, w_in.shape[-1] * N_DEV), F32)
    proj = _mm_gathered(h1, ag_in.arrived(h1, (0,)), ids([(xi, yi, ci), (xi, yi, 1 - ci)]), proj, "in_proj_a")
    ag_in.arrived(proj, (1, 2, 3))
    proj = _mm_gathered(h1, ag_in.pass_on(proj, (0, 1, 2)), ids([(*ch, ci) for ch in others]), proj, "in_proj_b")
    wg_in = ag_in.passed(proj, 0)
    proj = _mm_gathered(h1, wg_in, ids([(*ch, 1 - ci) for ch in others]), proj, "in_proj_c")
    ag_o.arrived_from_chips(proj)
    mix_a = _attn_fwd(proj, gv, attn_norm_g, AW)
    wg_o = ag_o.passed_on(mix_a).reshape(MIX, D)
    mix_b, o_b, st_all = _hgrn_fwd(proj, lbv, gnorm_g, AW, RW)
    mixin = jnp.concatenate([mix_a, mix_b], axis=1)
    mix = _mm_nn(mixin, wg_o, "out_proj")
    ag_f1.arrived_from_chips(mix)
    x1, h2 = _mid_fwd(x2, mix, mod6, ln1_g, ln1_b)
    wg_f1 = ag_f1.passed_on(h2)
    gu, act = _mm_swiglu(h2, wg_f1)
    ag_f2.arrived_from_chips(act)
    wg_f2 = ag_f2.passed_on(act).reshape(F, D)
    ff = _mm_nn(act, wg_f2, "ffn_out")
    dff, dx1a, vec_a = _final(x1, ff, mod6, ln2_g, ln2_b, tgt)

    du = _mm_swiglu_bwd(dff, wg_f2, gu)
    rs_f2 = _ReduceScatter(_mm_tn_rows(dff, act, dff, F // N_DEV, "grad_w_ffn_out"), "w_ffn_out")
    tm = _tile(T, 512, 16)
    du_ij = lambda tm_, w, first: pl.BlockSpec((None, tm_, w), lambda i, p: (p // (half // 2), i + first, p % (half // 2)))
    du_j = lambda rows, ns: pl.BlockSpec((None, rows, ns), lambda j: (j // half, 0, j % half))
    dh2 = _mm_gathered_nt(rs_f2.token, du, du_ij, wg_f1, T, tm, "ffn_in_bwd")
    rs_f2.pair_done(core, chip, dh2)
    gw_f1 = _mm_tn_gathered(rs_f2.token, h2, du, du_j, wg_f1.shape[-1], "grad_w_ffn_in")
    rs_f1 = _ReduceScatter(gw_f1.reshape(2, half, D, -1), "w_ffn_in")
    dmix, dxa, vec_b = _mid_bwd(x2, mix, x1, dx1a, dh2, mod6 + rs_f1.token[0, 0], ln1_g)
    dmixin = _mm_nt(dmix, wg_o, "out_proj_bwd")
    rs_f1.pair_done(core, chip, dmixin)
    rs_o = _ReduceScatter(_mm_tn_rows(rs_f1.token, mixin, dmix, MIX // N_DEV, "grad_w_o"), "w_o")
    dq, dk, dv, dgv, dga = _attn_bwd(proj, dmixin, gv + rs_o.token[0, 0], attn_norm_g, AW)
    rs_o.pair_done(core, chip, dq)
    dqb, dfl, dib, dgb, dlb, dgn = _hgrn_bwd(proj, dmixin, o_b, st_all, lbv + rs_o.token[0, 0], gnorm_g, AW, RW)
    dproj = jnp.concatenate([dq, dk, dv, dqb, dfl, dib, dgb], axis=1)
    p_ij = lambda tm_, w, first: pl.BlockSpec((tm_, w), lambda i, p: (i + first, p))
    p_j = lambda rows, ns: pl.BlockSpec((rows, ns), lambda j: (0, j))
    gw_in = _mm_tn_gathered(rs_o.token, h1, dproj, p_j, wg_in.shape[-1], "grad_w_in")
    rs_in = _ReduceScatter(gw_in.reshape(2, half, D, -1), "w_in")
    n_tiles = T // tm
    dh1 = _mm_gathered_nt(rs_in.token, dproj, p_ij, wg_in, T, tm, "in_proj_bwd_a", 0, n_tiles // 2)
    rs_in.pair_done(core, chip, dh1)
    dh1 = _mm_gathered_nt(rs_in.token, dproj, p_ij, wg_in, T, tm, "in_proj_bwd_b", n_tiles // 2,
                          n_tiles - n_tiles // 2, dh1)
    grad_x, vec_c = _first_bwd(x2, dh1, dxa, mod6)

    dmod = jnp.concatenate([vec_c[1:2], vec_c[0:1], vec_b[4:5], vec_b[1:2], vec_b[0:1], vec_a[2:3]], axis=0)
    pieces = dict(b_ada=dmod, rel_bias=dgv, attn_norm_g=dga, lb_logits=dlb, gnorm_g=dgn, ln1_g=vec_b[2:3],
                  ln1_b=vec_b[3:4], ln2_g=vec_a[0:1], ln2_b=vec_a[1:2], loss=vec_a[3:4])
    widths = dict(b_ada=(1, 6 * D), rel_bias=(H, TAB), attn_norm_g=(1, AW), lb_logits=(1, RW), gnorm_g=(RH, LANE),
                  ln1_g=(1, D), ln1_b=(1, D), ln2_g=(1, D), ln2_b=(1, D), loss=(1, D))
    packed = jnp.concatenate([pieces[k].reshape(-1, LANE) for k in widths], axis=0)
    small_ag = _SmallGather(packed, me, "small")
    after, res_big = small_ag.token, {}
    for k, rs in (("w_ffn_out", rs_f2), ("w_ffn_in", rs_f1), ("w_o", rs_o), ("w_in", rs_in)):
        four = _adam_shard(rs.sums(after), W[k][0], M[k][0], V[k][0], "adam_" + k)
        res_big[k] = [a[None] for a in four]
        after = four[0]
    gathered = small_ag.done(after)
    parts, r0 = {}, 0
    for k, (rows, width) in widths.items():
        nr = rows * width // LANE
        parts[k] = gathered[:, r0:r0 + nr, :].reshape(N_DEV, rows, width)
        r0 += nr
    prep_small = lambda d, k: pad_rb(d[k]) if k == "rel_bias" else d[k]
    small = _small_update(parts, parts["loss"], lbv, [prep_small(W, k) for k in SMALL],
                          [prep_small(M, k) for k in SMALL], [prep_small(V, k) for k in SMALL], max_rel)
    loss = small[0].reshape(())
    res = {}
    for idx, k in enumerate(SMALL):
        four = small[1 + 4 * idx:5 + 4 * idx]
        if k == "rel_bias":
            four = [a[:, :RB][None] for a in four]
        res[k] = list(four)

    res.update(res_big)
    dmod_s = lax.dynamic_slice_in_dim(parts["b_ada"].reshape(N_DEV, 6 * D), me * ns_ada, ns_ada, axis=1)
    res["w_ada"] = [a[None] for a in _adam_ada(c_all, dmod_s, w_ada[0], m_w_ada[0], v_w_ada[0])]

    out = [loss, grad_x[None]]
    for field in range(4):
        out += [res[k][field] for k in ORDER]
    return tuple(out)
```

```python
import jax
import jax.numpy as jnp
from jax import lax
from jax.experimental import pallas as pl
from jax.experimental.pallas import tpu as pltpu

F32 = jnp.float32
BF16 = jnp.bfloat16
MESH = pl.DeviceIdType.MESH
HIGHEST = lax.Precision.HIGHEST

N_DEV = 8
CHUNK = 64
N_PAST = 8
QBLK = 4 * CHUNK
KPAD = N_PAST * CHUNK
WIN = KPAD + QBLK
TAB = 1024
ATTN_HEAD_DIM = 64
ATTN_HEADS_PER_STEP = 4
SUB = 32
ROWS = 8
LANE = 128
EPS = 1e-5
ALPHA = 2.0 ** 0.25
ADAM_LR, ADAM_B1, ADAM_B2, ADAM_EPS, ADAM_WD, ADAM_STEP = 0.001, 0.9, 0.999, 1e-08, 0.01, 10
NEG = -1e30
VMEM_LIMIT = 56 * 1024 * 1024


def _sds(shape, dtype):
    return jax.ShapeDtypeStruct(tuple(shape), dtype)


def _tile(n, pref, mult):
    best = None
    for t in range(mult, min(n, pref) + 1, mult):
        if n % t == 0:
            best = t
    return n if best is None else best


def _params(sem=None, big=False):
    kw = {}
    if sem is not None:
        kw["dimension_semantics"] = sem
    if big:
        kw["vmem_limit_bytes"] = VMEM_LIMIT
    return pltpu.CompilerParams(**kw)


def _sigmoid(v):
    return 1.0 / (1.0 + jnp.exp(-v))


def _dot(a, b, dims, precision=None):
    return lax.dot_general(a, b, (dims, ((), ())), preferred_element_type=F32, precision=precision)


NN = ((1,), (0,))
NT = ((1,), (1,))
TN = ((0,), (0,))


def _ln(v):
    mu = jnp.mean(v, axis=-1, keepdims=True)
    d = v - mu
    rstd = lax.rsqrt(jnp.mean(d * d, axis=-1, keepdims=True) + EPS)
    return d * rstd, rstd


def _ln_bwd(dxh, xh, rstd):
    return rstd * (dxh - jnp.mean(dxh, axis=-1, keepdims=True) - xh * jnp.mean(dxh * xh, axis=-1, keepdims=True))


def _colsum(v):
    return jnp.sum(v, axis=0, keepdims=True)


def _ln_mod(x2, mod6):
    T, D = x2.shape
    tm = _tile(T, 256, 8)

    def body(x_ref, mod_ref, o_ref):
        xh, _ = _ln(x_ref[...])
        o_ref[...] = (xh * (1.0 + mod_ref[1:2, :]) + mod_ref[0:1, :]).astype(BF16)

    return pl.pallas_call(
        body, grid=(T // tm,), name="ln_mod",
        in_specs=[pl.BlockSpec((tm, D), lambda i: (i, 0)), pl.BlockSpec((6, D), lambda i: (0, 0))],
        out_specs=pl.BlockSpec((tm, D), lambda i: (i, 0)),
        out_shape=_sds((T, D), BF16), compiler_params=_params(("parallel",)),
    )(x2, mod6)


def _mid_fwd(x2, mix, mod6, ln1_g, ln1_b):
    T, D = x2.shape
    tm = _tile(T, 256, 8)

    def body(x_ref, mix_ref, mod_ref, g_ref, b_ref, x1_ref, h2_ref):
        zh, _ = _ln(ALPHA * x_ref[...] + mod_ref[2:3, :] * mix_ref[...])
        x1 = zh * g_ref[...] + b_ref[...]
        x1_ref[...] = x1
        xh, _ = _ln(x1)
        h2_ref[...] = (xh * (1.0 + mod_ref[4:5, :]) + mod_ref[3:4, :]).astype(BF16)

    row = pl.BlockSpec((tm, D), lambda i: (i, 0))
    vec = pl.BlockSpec((1, D), lambda i: (0, 0))
    return pl.pallas_call(
        body, grid=(T // tm,), name="mid_fwd",
        in_specs=[row, row, pl.BlockSpec((6, D), lambda i: (0, 0)), vec, vec],
        out_specs=[row, row],
        out_shape=[_sds((T, D), F32), _sds((T, D), BF16)], compiler_params=_params(("parallel",)),
    )(x2, mix, mod6, ln1_g, ln1_b)


def _final(x1, ff, mod6, ln2_g, ln2_b, tgt):
    T, D = x1.shape
    tm = _tile(T, 256, 8)

    def body(x1_ref, ff_ref, mod_ref, g_ref, b_ref, t_ref, dff_ref, dx1_ref, vec_ref):
        @pl.when(pl.program_id(0) == 0)
        def _():
            vec_ref[...] = jnp.zeros_like(vec_ref)

        ff_v = ff_ref[...]
        gate2 = mod_ref[5:6, :]
        zh, rstd = _ln(ALPHA * x1_ref[...] + gate2 * ff_v)
        err = zh * g_ref[...] + b_ref[...] - t_ref[...]
        dy = err * (1.0 / D)
        dz = _ln_bwd(dy * g_ref[...], zh, rstd)
        dff_ref[...] = (gate2 * dz).astype(BF16)
        dx1_ref[...] = ALPHA * dz
        vec_ref[0:1, :] += _colsum(dy * zh)
        vec_ref[1:2, :] += _colsum(dy)
        vec_ref[2:3, :] += _colsum(dz * ff_v)
        vec_ref[3:4, :] += _colsum(err * err) * (0.5 / D)

    row = pl.BlockSpec((tm, D), lambda i: (i, 0))
    vec = pl.BlockSpec((1, D), lambda i: (0, 0))
    return pl.pallas_call(
        body, grid=(T // tm,), name="final_fwd_bwd",
        in_specs=[row, row, pl.BlockSpec((6, D), lambda i: (0, 0)), vec, vec, row],
        out_specs=[row, row, pl.BlockSpec((8, D), lambda i: (0, 0))],
        out_shape=[_sds((T, D), BF16), _sds((T, D), F32), _sds((8, D), F32)],
        compiler_params=_params(("arbitrary",)),
    )(x1, ff, mod6, ln2_g, ln2_b, tgt)


def _mid_bwd(x2, mix, x1, dx1a, dh2, mod6, ln1_g):
    T, D = x2.shape
    tm = _tile(T, 256, 8)

    def body(x_ref, mix_ref, x1_ref, dx1a_ref, dh2_ref, mod_ref, g_ref, dmix_ref, dxa_ref, vec_ref):
        @pl.when(pl.program_id(0) == 0)
        def _():
            vec_ref[...] = jnp.zeros_like(vec_ref)

        dh2 = dh2_ref[...]
        xh, rstd = _ln(x1_ref[...])
        dx1 = dx1a_ref[...] + _ln_bwd(dh2 * (1.0 + mod_ref[4:5, :]), xh, rstd)
        mix_v = mix_ref[...]
        gate1 = mod_ref[2:3, :]
        zh, rstdz = _ln(ALPHA * x_ref[...] + gate1 * mix_v)
        dz = _ln_bwd(dx1 * g_ref[...], zh, rstdz)
        dmix_ref[...] = (gate1 * dz).astype(BF16)
        dxa_ref[...] = ALPHA * dz
        vec_ref[0:1, :] += _colsum(dh2 * xh)
        vec_ref[1:2, :] += _colsum(dh2)
        vec_ref[2:3, :] += _colsum(dx1 * zh)
        vec_ref[3:4, :] += _colsum(dx1)
        vec_ref[4:5, :] += _colsum(dz * mix_v)

    row = pl.BlockSpec((tm, D), lambda i: (i, 0))
    vec = pl.BlockSpec((1, D), lambda i: (0, 0))
    return pl.pallas_call(
        body, grid=(T // tm,), name="mid_bwd",
        in_specs=[row, row, row, row, row, pl.BlockSpec((6, D), lambda i: (0, 0)), vec],
        out_specs=[row, row, pl.BlockSpec((8, D), lambda i: (0, 0))],
        out_shape=[_sds((T, D), BF16), _sds((T, D), F32), _sds((8, D), F32)],
        compiler_params=_params(("arbitrary",)),
    )(x2, mix, x1, dx1a, dh2, mod6, ln1_g)


def _first_bwd(x2, dh1, dxa, mod6):
    T, D = x2.shape
    tm = _tile(T, 256, 8)

    def body(x_ref, dh1_ref, dxa_ref, mod_ref, gx_ref, vec_ref):
        @pl.when(pl.program_id(0) == 0)
        def _():
            vec_ref[...] = jnp.zeros_like(vec_ref)

        dh1 = dh1_ref[...]
        xh, rstd = _ln(x_ref[...])
        gx_ref[...] = dxa_ref[...] + _ln_bwd(dh1 * (1.0 + mod_ref[1:2, :]), xh, rstd)
        vec_ref[0:1, :] += _colsum(dh1 * xh)
        vec_ref[1:2, :] += _colsum(dh1)

    row = pl.BlockSpec((tm, D), lambda i: (i, 0))
    return pl.pallas_call(
        body, grid=(T // tm,), name="first_bwd",
        in_specs=[row, row, row, pl.BlockSpec((6, D), lambda i: (0, 0))],
        out_specs=[row, pl.BlockSpec((8, D), lambda i: (0, 0))],
        out_shape=[_sds((T, D), F32), _sds((8, D), F32)],
        compiler_params=_params(("arbitrary",)),
    )(x2, dh1, dxa, mod6)


def _slot(j):
    return (j % 2) * 4 + j // 2


def _mm_gathered(a, wg, shards, out, name):
    M, K = a.shape
    _, _, ns = wg.shape
    tm = _tile(M, 512, 16)

    def body(shards_ref, a_ref, w_ref, prev_ref, o_ref):
        o_ref[...] = _dot(a_ref[...], w_ref[...], NN)

    return pl.pallas_call(
        body, name=name,
        grid_spec=pltpu.PrefetchScalarGridSpec(
            num_scalar_prefetch=1, grid=(shards.shape[0], M // tm),
            in_specs=[pl.BlockSpec((tm, K), lambda j, i, s: (i, 0)),
                      pl.BlockSpec((None, K, ns), lambda j, i, s: (s[j], 0, 0)), ORDER_ONLY],
            out_specs=pl.BlockSpec((tm, ns), lambda j, i, s: (i, s[j]))),
        out_shape=_sds((M, N_DEV * ns), F32), input_output_aliases={3: 0},
        compiler_params=_params(("parallel", "parallel"), big=True),
    )(shards, a, wg, out)


def _mm_nn(a, b, name):
    M, K = a.shape
    _, N = b.shape
    tm, tn = _tile(M, 512, 16), _tile(N, 1024, LANE)

    def body(a_ref, b_ref, o_ref):
        o_ref[...] = _dot(a_ref[...], b_ref[...], NN)

    return pl.pallas_call(
        body, grid=(N // tn, M // tm), name=name,
        in_specs=[pl.BlockSpec((tm, K), lambda j, i: (i, 0)), pl.BlockSpec((K, tn), lambda j, i: (0, j))],
        out_specs=pl.BlockSpec((tm, tn), lambda j, i: (i, j)),
        out_shape=_sds((M, N), F32), compiler_params=_params(("parallel", "parallel"), big=True),
    )(a, b)


def _mm_nt(a, b, name):
    M, K = a.shape
    N, _ = b.shape
    tm, tn = _tile(M, 512, 16), _tile(N, 1024, LANE)

    def body(a_ref, b_ref, o_ref):
        o_ref[...] = _dot(a_ref[...], b_ref[...], NT)

    return pl.pallas_call(
        body, grid=(M // tm, N // tn), name=name,
        in_specs=[pl.BlockSpec((tm, K), lambda i, j: (i, 0)), pl.BlockSpec((tn, K), lambda i, j: (j, 0))],
        out_specs=pl.BlockSpec((tm, tn), lambda i, j: (i, j)),
        out_shape=_sds((M, N), F32), compiler_params=_params(("parallel", "parallel"), big=True),
    )(a, b)


def _mm_swiglu(h2, wg):
    M, K = h2.shape
    _, _, ns = wg.shape
    half = N_DEV // 2
    tm = _tile(M, 256, 16)

    def body(a_ref, wgate_ref, wup_ref, gu_ref, act_ref):
        a = a_ref[...]
        g = _dot(a, wgate_ref[...], NN)
        u = _dot(a, wup_ref[...], NN)
        sg = _sigmoid(g)
        silu = g * sg
        gu_ref[0] = u * (sg * (1.0 + g * (1.0 - sg)))
        gu_ref[1] = silu
        act_ref[...] = (silu * u).astype(BF16)

    return pl.pallas_call(
        body, grid=(half, M // tm), name="ffn_in_swiglu",
        in_specs=[pl.BlockSpec((tm, K), lambda j, i: (i, 0)),
                  pl.BlockSpec((None, K, ns), lambda j, i: (j, 0, 0)),
                  pl.BlockSpec((None, K, ns), lambda j, i: (j + half, 0, 0))],
        out_specs=[pl.BlockSpec((2, tm, ns), lambda j, i: (0, i, j)), pl.BlockSpec((tm, ns), lambda j, i: (i, j))],
        out_shape=[_sds((2, M, half * ns), F32), _sds((M, half * ns), BF16)],
        compiler_params=_params(("parallel", "parallel"), big=True),
    )(h2, wg, wg)


def _mm_swiglu_bwd(dff, w2, gu):
    M, K = dff.shape
    F = w2.shape[0]
    tm, tn = _tile(M, 512, 16), _tile(F, 1408, LANE)

    def body(a_ref, b_ref, gu_ref, du_ref):
        da = _dot(a_ref[...], b_ref[...], NT)
        du_ref[0] = (da * gu_ref[0]).astype(BF16)
        du_ref[1] = (da * gu_ref[1]).astype(BF16)

    return pl.pallas_call(
        body, grid=(F // tn, M // tm), name="ffn_out_bwd_swiglu",
        in_specs=[pl.BlockSpec((tm, K), lambda j, i: (i, 0)), pl.BlockSpec((tn, K), lambda j, i: (j, 0)),
                  pl.BlockSpec((2, tm, tn), lambda j, i: (0, i, j))],
        out_specs=pl.BlockSpec((2, tm, tn), lambda j, i: (0, i, j)),
        out_shape=_sds((2, M, F), BF16), compiler_params=_params(("parallel", "parallel"), big=True),
    )(dff, w2, gu)


ORDER_ONLY = pl.BlockSpec(memory_space=pl.ANY)


def _mm_tn_rows(dep, a, b, rs, name):
    M, Ka = a.shape
    _, N = b.shape

    def body(_, a_ref, b_ref, o_ref):
        g = _dot(a_ref[...], b_ref[...], TN)
        o_ref[0, 0] = g[0:rs, :].astype(BF16)
        o_ref[1, 0] = g[rs:2 * rs, :].astype(BF16)

    return pl.pallas_call(
        body, grid=(N_DEV // 2,), name=name,
        in_specs=[ORDER_ONLY, pl.BlockSpec((M, 2 * rs), lambda ch: (0, ch)), pl.BlockSpec((M, N), lambda ch: (0, 0))],
        out_specs=pl.BlockSpec((2, 1, rs, N), lambda ch: (0, ch, 0, 0)),
        out_shape=_sds((2, N_DEV // 2, rs, N), BF16),
        compiler_params=_params(("parallel",), big=True),
    )(dep, a, b)


def _mm_gathered_nt(dep, a, a_spec, wg, M, tm, name, first=0, count=None, out=None):
    _, K, ns = wg.shape
    count = M // tm if count is None else count
    out = lax.empty((M, K), F32) if out is None else out

    def body(_, a_ref, w_ref, prev_ref, o_ref):
        @pl.when(pl.program_id(1) == 0)
        def _():
            o_ref[...] = jnp.zeros_like(o_ref)

        o_ref[...] += _dot(a_ref[:, 0:ns], w_ref[0], NT) + _dot(a_ref[:, ns:2 * ns], w_ref[1], NT)

    return pl.pallas_call(
        body, grid=(count, N_DEV // 2), name=name,
        in_specs=[ORDER_ONLY, a_spec(tm, 2 * ns, first), pl.BlockSpec((2, K, ns), lambda i, p: (p, 0, 0)), ORDER_ONLY],
        out_specs=pl.BlockSpec((tm, K), lambda i, j: (i + first, 0)),
        out_shape=_sds((M, K), F32), input_output_aliases={3: 0},
        compiler_params=_params(("parallel", "arbitrary"), big=True),
    )(dep, a, wg, out)


def _mm_tn_gathered(dep, h, a, a_spec, ns, name):
    M, K = h.shape

    def body(_, h_ref, a_ref, o_ref):
        o_ref[...] = _dot(h_ref[...], a_ref[...], TN).astype(BF16)

    return pl.pallas_call(
        body, grid=(N_DEV,), name=name,
        in_specs=[ORDER_ONLY, pl.BlockSpec((M, K), lambda j: (0, 0)), a_spec(M, ns)],
        out_specs=pl.BlockSpec((None, K, ns), lambda j: (_slot(j), 0, 0)),
        out_shape=_sds((N_DEV, K, ns), BF16),
        compiler_params=_params(("parallel",), big=True),
    )(dep, h, a)


def _bias_onehot(rbp, max_rel):
    r = lax.broadcasted_iota(jnp.int32, (rbp, TAB), 0)
    m = lax.broadcasted_iota(jnp.int32, (rbp, TAB), 1)
    dist = KPAD - jnp.where(m < WIN, m, m - TAB)
    return (r == jnp.clip(dist, -max_rel, max_rel) + max_rel).astype(F32)


def _attn_setup(i, hp, k_ref, v_ref, gv_ref, kpad, vpad, bias):
    ls = slice(i * ATTN_HEAD_DIM, (i + 1) * ATTN_HEAD_DIM)
    kpad[i][0:KPAD, :] = jnp.zeros((KPAD, ATTN_HEAD_DIM), BF16)
    vpad[i][0:KPAD, :] = jnp.zeros((KPAD, ATTN_HEAD_DIM), BF16)
    kpad[i][KPAD:, :] = k_ref[:, ls].astype(BF16)
    vpad[i][KPAD:, :] = v_ref[:, ls].astype(BF16)
    gvrow = gv_ref[pl.ds(hp * ATTN_HEADS_PER_STEP + i, 1), :]
    tab = pltpu.roll(jnp.broadcast_to(gvrow, (QBLK, TAB)), 0, 1, stride=1, stride_axis=0)
    row = lax.broadcasted_iota(jnp.int32, (QBLK, WIN), 0)
    col = lax.broadcasted_iota(jnp.int32, (QBLK, WIN), 1)
    first = jnp.bitwise_and(row, -CHUNK)
    seen = jnp.logical_and(col >= first, col < first + (N_PAST + 1) * CHUNK)
    bias[i][...] = jnp.where(seen, tab[:, 0:WIN], NEG)


def _attn_probs(b, q_ref, kpad, vpad, bias, col):
    pair = range(ATTN_HEADS_PER_STEP)
    ls = [slice(i * ATTN_HEAD_DIM, (i + 1) * ATTN_HEAD_DIM) for i in pair]
    r0 = pl.multiple_of(b * QBLK, QBLK)
    q = [q_ref[pl.ds(r0, QBLK), ls[i]].astype(BF16) for i in pair]
    kw = [kpad[i][pl.ds(r0, WIN), :] for i in pair]
    vw = [vpad[i][pl.ds(r0, WIN), :] for i in pair]
    s = [_dot(q[i], kw[i], NT) * (ATTN_HEAD_DIM ** -0.5) + bias[i][...] for i in pair]
    s = [jnp.where(col >= KPAD - r0, s[i], NEG) for i in pair]
    p = [jnp.exp(s[i] - jnp.max(s[i], axis=-1, keepdims=True)) for i in pair]
    pn = [p[i] / jnp.sum(p[i], axis=-1, keepdims=True) for i in pair]
    return r0, ls, q, kw, vw, pn


def _attn_fwd(proj, gv, ga, AW, mixin):
    T = proj.shape[0]
    AH = ATTN_HEADS_PER_STEP
    W = AH * ATTN_HEAD_DIM
    HP = AW // W

    def body(q_ref, k_ref, v_ref, gv_ref, ga_ref, prev_ref, o_ref, *scratch):
        kpad, vpad, bias = (scratch[k * AH:(k + 1) * AH] for k in range(3))
        hp = pl.program_id(0)
        for i in range(AH):
            _attn_setup(i, hp, k_ref, v_ref, gv_ref, kpad, vpad, bias)
        col = lax.broadcasted_iota(jnp.int32, (QBLK, WIN), 1)

        def block(b, carry):
            pair = range(AH)
            r0, ls, _, _, vw, pn = _attn_probs(b, q_ref, kpad, vpad, bias, col)
            o = [_dot(pn[i].astype(BF16), vw[i], NN) for i in pair]
            r = [lax.rsqrt(jnp.mean(o[i] * o[i], axis=-1, keepdims=True) + EPS) for i in pair]
            outs = [o[i] * r[i] * ga_ref[0:1, ls[i]] for i in pair]
            o_ref[pl.ds(r0, QBLK), :] = jnp.concatenate(outs, axis=1).astype(BF16)
            return carry

        lax.fori_loop(0, T // QBLK, block, 0)

    blk = lambda off: pl.BlockSpec((T, W), lambda hp: (0, off + hp))
    return pl.pallas_call(
        body, grid=(HP,), name="attn_fwd",
        in_specs=[blk(0), blk(HP), blk(2 * HP), pl.BlockSpec(gv.shape, lambda hp: (0, 0)),
                  pl.BlockSpec((1, W), lambda hp: (0, hp)), ORDER_ONLY],
        out_specs=pl.BlockSpec((T, W), lambda hp: (0, hp)),
        out_shape=_sds(mixin.shape, BF16), input_output_aliases={5: 0},
        scratch_shapes=[pltpu.VMEM((T + KPAD, ATTN_HEAD_DIM), BF16)] * (2 * AH) + [pltpu.VMEM((QBLK, WIN), F32)] * AH,
        compiler_params=_params(("parallel",), big=True),
    )(proj, proj, proj, gv, ga, mixin)


def _attn_bwd(proj, dmixin, gv, ga, AW):
    T = proj.shape[0]
    AH = ATTN_HEADS_PER_STEP
    W = AH * ATTN_HEAD_DIM
    HP = AW // W
    scale = ATTN_HEAD_DIM ** -0.5

    def body(q_ref, k_ref, v_ref, dn_ref, gv_ref, ga_ref, dq_ref, dk_ref, dv_ref, dgv_ref, dga_ref, *scratch):
        kpad, vpad, dkacc, dvacc, bias, dbias = (scratch[k * AH:(k + 1) * AH] for k in range(6))
        hp = pl.program_id(0)
        for i in range(AH):
            _attn_setup(i, hp, k_ref, v_ref, gv_ref, kpad, vpad, bias)
            dkacc[i][...] = jnp.zeros_like(dkacc[i])
            dvacc[i][...] = jnp.zeros_like(dvacc[i])
            dbias[i][...] = jnp.zeros_like(dbias[i])
        dga_ref[...] = jnp.zeros_like(dga_ref)
        col = lax.broadcasted_iota(jnp.int32, (QBLK, WIN), 1)

        def block(b, carry):
            pair = range(AH)
            r0, lss, qs, kws, vws, pns = _attn_probs(b, q_ref, kpad, vpad, bias, col)
            pn_b = [pns[i].astype(BF16) for i in pair]
            o = [_dot(pn_b[i], vws[i], NN) for i in pair]
            r = [lax.rsqrt(jnp.mean(o[i] * o[i], axis=-1, keepdims=True) + EPS) for i in pair]
            dn = [dn_ref[pl.ds(r0, QBLK), lss[i]] for i in pair]
            for i in pair:
                dga_ref[i:i + 1, :] += _colsum(dn[i] * o[i] * r[i])
            a = [dn[i] * ga_ref[0:1, lss[i]] for i in pair]
            do_b = [(r[i] * (a[i] - o[i] * (r[i] * r[i]) * jnp.mean(a[i] * o[i], axis=-1, keepdims=True))).astype(BF16)
                    for i in pair]
            dp = [_dot(do_b[i], vws[i], NT) for i in pair]
            for i in pair:
                dvacc[i][pl.ds(r0, WIN), :] += _dot(pn_b[i], do_b[i], TN)
            ds = [pns[i] * (dp[i] - jnp.sum(pns[i] * dp[i], axis=-1, keepdims=True)) for i in pair]
            for i in pair:
                dbias[i][...] += ds[i]
            ds_b = [ds[i].astype(BF16) for i in pair]
            dq = [_dot(ds_b[i], kws[i], NN) * scale for i in pair]
            dq_ref[pl.ds(r0, QBLK), :] = jnp.concatenate(dq, axis=1).astype(BF16)
            for i in pair:
                dkacc[i][pl.ds(r0, WIN), :] += _dot(ds_b[i], qs[i], TN) * scale
            return carry

        lax.fori_loop(0, T // QBLK, block, 0)

        rr = lax.broadcasted_iota(jnp.int32, (QBLK, QBLK), 0)
        cc = lax.broadcasted_iota(jnp.int32, (QBLK, QBLK), 1)
        flip = (rr + cc == QBLK - 1).astype(BF16)
        for i in range(AH):
            ls = slice(i * ATTN_HEAD_DIM, (i + 1) * ATTN_HEAD_DIM)
            dk_ref[:, ls] = dkacc[i][KPAD:, :].astype(BF16)
            dv_ref[:, ls] = dvacc[i][KPAD:, :].astype(BF16)
            full = jnp.concatenate([dbias[i][...], jnp.zeros((QBLK, TAB - WIN), F32)], axis=1)
            hi = full.astype(BF16)
            lo = (full - hi.astype(F32)).astype(BF16)
            rev = _dot(flip, hi, NN) + _dot(flip, lo, NN)
            dgv_ref[i:i + 1, :] = _colsum(pltpu.roll(rev, TAB - (QBLK - 1), 1, stride=1, stride_axis=0))

    blk = lambda off: pl.BlockSpec((T, W), lambda hp: (0, off + hp))
    accs = lambda dt: [pltpu.VMEM((T + KPAD, ATTN_HEAD_DIM), dt)] * AH
    return pl.pallas_call(
        body, grid=(HP,), name="attn_bwd",
        in_specs=[blk(0), blk(HP), blk(2 * HP), blk(0), pl.BlockSpec(gv.shape, lambda hp: (0, 0)),
                  pl.BlockSpec((1, W), lambda hp: (0, hp))],
        out_specs=[blk(0), blk(0), blk(0), pl.BlockSpec((None, AH, TAB), lambda hp: (hp, 0, 0)),
                   pl.BlockSpec((None, AH, ATTN_HEAD_DIM), lambda hp: (hp, 0, 0))],
        out_shape=[_sds((T, AW), BF16), _sds((T, AW), BF16), _sds((T, AW), BF16),
                   _sds((HP, AH, TAB), F32), _sds((HP, AH, ATTN_HEAD_DIM), F32)],
        scratch_shapes=accs(BF16) + accs(BF16) + accs(F32) + accs(F32) + [pltpu.VMEM((QBLK, WIN), F32)] * (2 * AH),
        compiler_params=_params(("parallel",), big=True),
    )(proj, proj, proj, dmixin, gv, ga)


def _ltri():
    r = lax.broadcasted_iota(jnp.int32, (CHUNK, CHUNK), 0)
    c = lax.broadcasted_iota(jnp.int32, (CHUNK, CHUNK), 1)
    return (c <= r).astype(BF16)


def _tri_dot(tri, v, dims):
    hi = v.astype(BF16)
    lo = (v - hi.astype(F32)).astype(BF16)
    return _dot(tri, hi, dims) + _dot(tri, lo, dims)


HEADS_PER_STEP = (8, 2)
REC_ROW_TILE = 512


def _alternate(stages):
    live = list(stages)
    while live:
        for g in list(live):
            if next(g, StopIteration) is StopIteration:
                live.remove(g)


def _hgrn_gates(n, ls, q_ref, f_ref, lb_ref, ltri):
    r0 = pl.multiple_of(n * CHUNK, CHUNK)
    rows = pl.ds(r0, CHUNK)
    lb = lb_ref[:, ls]
    qb = q_ref[rows, ls]
    sg = _sigmoid(f_ref[rows, ls])
    f = lb + (1.0 - lb) * sg
    sq = _sigmoid(qb)
    b = _tri_dot(ltri, jnp.log(f), NN)
    return rows, lb, qb, sg, f, 1.0 - f, sq, qb * sq, b


def _hgrn_specs(T, RW, AW, backward):
    HG = HEADS_PER_STEP[1 if backward else 0]
    W = HG * LANE
    TT = _tile(T, REC_ROW_TILE, CHUNK)
    n_row_tiles = T // TT
    base = 3 * AW // W
    row = (lambda t: n_row_tiles - 1 - t) if backward else (lambda t: t)
    blk_in = lambda off: pl.BlockSpec((TT, W), lambda g, t: (row(t), base + off + g))
    col = pl.BlockSpec((TT, W), lambda g, t: (row(t), g))
    states = pl.BlockSpec((HG, TT // CHUNK, LANE, LANE), lambda g, t: (g, row(t), 0, 0))
    return HG, W, RW // W, TT, n_row_tiles, blk_in, col, states


def _hgrn_fwd(proj, lb, gn, AW, RW, mixin):
    T = proj.shape[0]
    RH, NC, NSUB = RW // LANE, T // CHUNK, CHUNK // SUB
    HG, W, NG, TT, n_row_tiles, blk_in, col, states = _hgrn_specs(T, RW, AW, False)
    mix_col = pl.BlockSpec((TT, W), lambda g, t: (t, AW // W + g))

    def body(q_ref, f_ref, i_ref, g_ref, lb_ref, gn_ref, prev_ref, mix_ref, o_ref, stall_ref,
             st_all, bs_all, kks_all, ics_all):
        @pl.when(pl.program_id(1) == 0)
        def _():
            st_all[...] = jnp.zeros_like(st_all)

        ltri = _ltri()
        rowi = lax.broadcasted_iota(jnp.int32, (SUB, 1), 0)

        def one_head(h, n):
            ls = slice(h * LANE, (h + 1) * LANE)
            st, bs, kks, ics = st_all.at[h], bs_all.at[h], kks_all.at[h], ics_all.at[h]
            rows, _, _, _, _, kk, _, qs, b = _hgrn_gates(n, ls, q_ref, f_ref, lb_ref, ltri)
            ic = i_ref[rows, ls]
            stv = st[...]
            stall_ref[h, n] = stv
            bs[...] = b
            kks[...] = kk
            ics[...] = ic
            yield
            o = _dot((qs * jnp.exp(b)).astype(BF16), stv.astype(BF16), NT)
            yield
            ic_b = ic.astype(BF16)
            pieces = []
            for blk in range(NSUB):
                s0 = blk * SUB
                bI, qI = b[s0:s0 + SUB], qs[s0:s0 + SUB]
                if blk == 0:
                    oI = jnp.zeros((SUB, LANE), F32)
                else:
                    ref = bs[s0 - 1:s0, :]
                    qt = (qI * jnp.exp(bI - ref)).astype(BF16)
                    kt = (kk[0:s0] * jnp.exp(ref - b[0:s0])).astype(BF16)
                    oI = _dot(_dot(qt, kt, NT).astype(BF16), ic_b[0:s0], NN)
                    yield
                acc = [oI[g * ROWS:(g + 1) * ROWS] for g in range(SUB // ROWS)]
                for s in range(SUB):
                    sr = s0 + s
                    g0 = s // ROWS
                    lo = g0 * ROWS
                    e = jnp.exp(jnp.minimum(bI[lo:] - bs[sr:sr + 1, :], 0.0))
                    a = jnp.sum(qI[lo:] * kks[sr:sr + 1, :] * e, axis=-1, keepdims=True)
                    add = jnp.where(rowi[lo:] >= s, a, 0.0) * ics[sr:sr + 1, :]
                    for g in range(g0, SUB // ROWS):
                        acc[g] = acc[g] + add[(g - g0) * ROWS:(g - g0 + 1) * ROWS]
                    yield
                pieces.extend(acc)
            o = o + jnp.concatenate(pieces, axis=0)
            bl = bs[CHUNK - 1:CHUNK, :]
            kd = (kk * jnp.exp(bl - b)).astype(BF16)
            st[...] = stv * jnp.exp(bl) + _dot(ic_b, kd, TN)
            yield
            o_ref[rows, ls] = o
            r = lax.rsqrt(jnp.mean(o * o, axis=-1, keepdims=True) + EPS)
            gb = g_ref[rows, ls]
            mix_ref[rows, ls] = (o * r * gn_ref[...] * (gb * _sigmoid(gb))).astype(BF16)

        def chunk(n, carry):
            _alternate([one_head(h, n) for h in range(HG)])
            return carry

        lax.fori_loop(0, TT // CHUNK, chunk, 0)

    tile = pltpu.VMEM((HG, CHUNK, LANE), F32)
    return pl.pallas_call(
        body, grid=(NG, n_row_tiles), name="hgrn_fwd",
        in_specs=[blk_in(0), blk_in(NG), blk_in(2 * NG), blk_in(3 * NG), pl.BlockSpec((1, W), lambda g, t: (0, g)),
                  pl.BlockSpec((1, LANE), lambda g, t: (0, 0)), ORDER_ONLY],
        out_specs=[mix_col, col, states],
        out_shape=[_sds(mixin.shape, BF16), _sds((T, RW), F32), _sds((RH, NC, LANE, LANE), F32)],
        input_output_aliases={6: 0},
        scratch_shapes=[pltpu.VMEM((HG, LANE, LANE), F32), tile, tile, tile],
        compiler_params=_params(("parallel", "arbitrary"), big=True),
    )(proj, proj, proj, proj, lb, gn, mixin)


def _hgrn_bwd(proj, dmixin, o_b, st_all, lb, gn, AW, RW):
    T = proj.shape[0]
    RH, NC, NSUB = RW // LANE, T // CHUNK, CHUNK // SUB
    HG, W, NG, TT, n_row_tiles, blk_in, col, states = _hgrn_specs(T, RW, AW, True)

    def body(q_ref, f_ref, i_ref, g_ref, o_ref, dn_ref, stall_ref, lb_ref, gn_ref,
             dq_ref, df_ref, di_ref, dg_ref, dlb_ref, dgn_ref, dst_all, bs_all, qss_all, dos_all, p2_all, dic_all,
             p1_all):
        @pl.when(pl.program_id(1) == 0)
        def _():
            dst_all[...] = jnp.zeros_like(dst_all)
            dlb_ref[...] = jnp.zeros_like(dlb_ref)
            dgn_ref[...] = jnp.zeros_like(dgn_ref)

        ltri = _ltri()
        rowi = lax.broadcasted_iota(jnp.int32, (SUB, 1), 0)
        last = lax.broadcasted_iota(jnp.int32, (CHUNK, 1), 0) == CHUNK - 1

        def one_head(h, n):
            ls = slice(h * LANE, (h + 1) * LANE)
            dst, bs, qss, dos = dst_all.at[h], bs_all.at[h], qss_all.at[h], dos_all.at[h]
            p2, dic, p1s = p2_all.at[h], dic_all.at[h], p1_all.at[h]
            rows, lbv, qb, sg, f, kk, sq, qs, b = _hgrn_gates(n, ls, q_ref, f_ref, lb_ref, ltri)
            ic = i_ref[rows, ls]
            stv = stall_ref[h, n]
            dstv = dst[...]
            o = o_ref[rows, ls]
            dn = dn_ref[rows, ls]
            gb = g_ref[rows, ls]
            sgb = _sigmoid(gb)
            r = lax.rsqrt(jnp.mean(o * o, axis=-1, keepdims=True) + EPS)
            gnv = gn_ref[...]
            dg_ref[rows, ls] = (dn * (o * r * gnv) * (sgb * (1.0 + gb * (1.0 - sgb)))).astype(BF16)
            dy = dn * (gb * sgb)
            dgn_ref[h] += _colsum(dy * o * r)
            a_ = dy * gnv
            do = r * (a_ - o * (r * r) * jnp.mean(a_ * o, axis=-1, keepdims=True))
            do_b = do.astype(BF16)
            bs[...] = b
            qss[...] = qs
            dos[...] = do
            yield
            ic_b = ic.astype(BF16)
            eb = jnp.exp(b)
            bl = bs[CHUNK - 1:CHUNK, :]
            ebl = jnp.exp(bl)
            dec = jnp.exp(bl - b)
            kd = (kk * dec).astype(BF16)
            dst_b = dstv.astype(BF16)
            dqs = _dot(do_b, stv.astype(BF16), NN) * eb
            dkk2 = _dot(ic_b, dst_b, NN) * dec
            dic[...] = _dot(kd, dst_b, NT)
            dbl = ebl * _colsum(stv * dstv) + _colsum(kk * dkk2)
            dst[...] = dstv * ebl + _dot(do_b, (qs * eb).astype(BF16), TN)
            yield
            p2[...] = jnp.zeros_like(p2)
            p1_pieces = []
            for blk in range(NSUB):
                s0 = blk * SUB
                bI, qI, doI = b[s0:s0 + SUB], qs[s0:s0 + SUB], do[s0:s0 + SUB]
                if blk == 0:
                    p1 = jnp.zeros((SUB, LANE), F32)
                else:
                    ref = bs[s0 - 1:s0, :]
                    eq = jnp.exp(bI - ref)
                    ek = jnp.exp(ref - b[0:s0])
                    qt = (qI * eq).astype(BF16)
                    kt = (kk[0:s0] * ek).astype(BF16)
                    doI_b = doI.astype(BF16)
                    dic[0:s0, :] += _dot(_dot(qt, kt, NT).astype(BF16), doI_b, TN)
                    da = _dot(doI_b, ic_b[0:s0], NT).astype(BF16)
                    p1 = _dot(da, kt, NN) * eq
                    p2[0:s0, :] += _dot(da, qt, TN) * ek
                    yield
                p1_pieces.append(p1)
                kkI, icI = kk[s0:s0 + SUB], ic[s0:s0 + SUB]
                p2acc = [jnp.zeros((ROWS, LANE), F32) for _ in range(SUB // ROWS)]
                diacc = [jnp.zeros((ROWS, LANE), F32) for _ in range(SUB // ROWS)]
                for t in range(SUB):
                    tr = s0 + t
                    ng = t // ROWS + 1
                    hi = ng * ROWS
                    keep = rowi[:hi] <= t
                    do_t = dos[tr:tr + 1, :]
                    e = jnp.exp(jnp.minimum(bs[tr:tr + 1, :] - bI[:hi], 0.0))
                    qe = qss[tr:tr + 1, :] * e
                    a = jnp.where(keep, jnp.sum(kkI[:hi] * qe, axis=-1, keepdims=True), 0.0)
                    da = jnp.where(keep, jnp.sum(icI[:hi] * do_t, axis=-1, keepdims=True), 0.0)
                    dp2, ddi = da * qe, a * do_t
                    for g in range(ng):
                        p2acc[g] = p2acc[g] + dp2[g * ROWS:(g + 1) * ROWS]
                        diacc[g] = diacc[g] + ddi[g * ROWS:(g + 1) * ROWS]
                    p1s[tr:tr + 1, :] = _colsum(da * kkI[:hi] * e)
                    yield
                p2[s0:s0 + SUB, :] += jnp.concatenate(p2acc, axis=0)
                dic[s0:s0 + SUB, :] += jnp.concatenate(diacc, axis=0)
            dqs = dqs + jnp.concatenate(p1_pieces, axis=0) + p1s[...]
            dkk = dkk2 + p2[...]
            db = qs * dqs - kk * dkk + jnp.where(last, dbl, 0.0)
            dgl = _tri_dot(ltri, db, TN)
            yield
            dfv = dgl / f - dkk
            df_ref[rows, ls] = (dfv * (1.0 - lbv) * sg * (1.0 - sg)).astype(BF16)
            dlb_ref[:, ls] += _colsum(dfv * (1.0 - sg))
            dq_ref[rows, ls] = (dqs * (sq * (1.0 + qb * (1.0 - sq)))).astype(BF16)
            di_ref[rows, ls] = dic[...].astype(BF16)

        def chunk(k, carry):
            _alternate([one_head(h, TT // CHUNK - 1 - k) for h in range(HG)])
            return carry

        lax.fori_loop(0, TT // CHUNK, chunk, 0)

    tile = pltpu.VMEM((HG, CHUNK, LANE), F32)
    return pl.pallas_call(
        body, grid=(NG, n_row_tiles), name="hgrn_bwd",
        in_specs=[blk_in(0), blk_in(NG), blk_in(2 * NG), blk_in(3 * NG), col,
                  pl.BlockSpec((TT, W), lambda g, t: (n_row_tiles - 1 - t, AW // W + g)), states,
                  pl.BlockSpec((1, W), lambda g, t: (0, g)), pl.BlockSpec((1, LANE), lambda g, t: (0, 0))],
        out_specs=[col, col, col, col, pl.BlockSpec((1, W), lambda g, t: (0, g)),
                   pl.BlockSpec((HG, 1, LANE), lambda g, t: (g, 0, 0))],
        out_shape=[_sds((T, RW), BF16)] * 4 + [_sds((1, RW), F32), _sds((RH, 1, LANE), F32)],
        scratch_shapes=[pltpu.VMEM((HG, LANE, LANE), F32), tile, tile, tile, tile, tile, tile],
        compiler_params=_params(("parallel", "arbitrary"), big=True),
    )(proj, proj, proj, proj, o_b, dmixin, st_all, lb, gn)


def _prep(c, lb_logits, rb_pad, max_rel, after):
    D, RW = c.shape[-1], lb_logits.shape[-1]
    H, rbp = rb_pad.shape

    def body(c_ref, l_ref, rb_ref, _, __, cact_ref, lb_ref, gv_ref):
        cv = c_ref[...]
        cact_ref[...] = cv * _sigmoid(cv)
        lb_ref[...] = _sigmoid(l_ref[0:1, :] - l_ref[1:2, :])
        gv_ref[...] = _dot(rb_ref[...], _bias_onehot(rbp, max_rel), NN, HIGHEST)

    vmem = pl.BlockSpec(memory_space=pltpu.VMEM)
    return pl.pallas_call(
        body, name="prep", in_specs=[vmem, vmem, vmem, ORDER_ONLY, ORDER_ONLY],
        out_shape=[_sds((1, D), F32), _sds((1, RW), F32), _sds((H, TAB), F32)],
    )(c, lb_logits, rb_pad, *after)


def _mod_part(c_all, w_ada_s, b_ada_s):
    B, D = c_all.shape
    ns = w_ada_s.shape[1]
    tn = _tile(ns, 768, LANE)

    def body(c_ref, w_ref, b_ref, o_ref):
        o_ref[...] = _dot(c_ref[...], w_ref[...], NN) + b_ref[...]

    return pl.pallas_call(
        body, grid=(ns // tn,), name="mod_part",
        in_specs=[pl.BlockSpec((B, D), lambda j: (0, 0)), pl.BlockSpec((D, tn), lambda j: (0, j)),
                  pl.BlockSpec((1, tn), lambda j: (0, j))],
        out_specs=pl.BlockSpec((B, tn), lambda j: (0, j)),
        out_shape=_sds((B, ns), F32), compiler_params=_params(("parallel",)),
    )(c_all, w_ada_s, b_ada_s)


def _adam(w, g, m, v):
    m = ADAM_B1 * m + (1.0 - ADAM_B1) * g
    v = ADAM_B2 * v + (1.0 - ADAM_B2) * (g * g)
    m_hat = m * (1.0 / (1.0 - ADAM_B1 ** ADAM_STEP))
    v_hat = v * (1.0 / (1.0 - ADAM_B2 ** ADAM_STEP))
    return -ADAM_LR * (m_hat / (jnp.sqrt(v_hat) + ADAM_EPS) + ADAM_WD * w), m, v


def _adam_ada(c_all, dmod_s, w, m, v):
    B, D = c_all.shape
    ns = w.shape[1]
    tr, tn = _tile(D, 512, LANE), _tile(ns, 768, LANE)

    def body(c_ref, d_ref, w_ref, m_ref, v_ref, g_out, dw_out, m_out, v_out):
        g = _dot(c_ref[...], d_ref[...], TN)
        g_out[...] = g
        dw_out[...], m_out[...], v_out[...] = _adam(w_ref[...], g, m_ref[...], v_ref[...])

    big = pl.BlockSpec((tr, tn), lambda i, j: (i, j))
    return pl.pallas_call(
        body, grid=(D // tr, ns // tn), name="adam_w_ada",
        in_specs=[pl.BlockSpec((B, tr), lambda i, j: (0, i)), pl.BlockSpec((B, tn), lambda i, j: (0, j)),
                  big, big, big],
        out_specs=[big] * 4, out_shape=[_sds((D, ns), F32)] * 4,
        compiler_params=_params(("parallel", "parallel")),
    )(c_all, dmod_s, w, m, v)


def _adam_shard(parts, w, m, v, name):
    R, C = w.shape
    tr = _tile(R, 256, 16)

    def body(p_ref, w_ref, m_ref, v_ref, g_out, dw_out, m_out, v_out):
        g = p_ref[0].astype(F32)
        for k in range(1, N_DEV // 2):
            g = g + p_ref[k].astype(F32)
        g_out[...] = g
        dw_out[...], m_out[...], v_out[...] = _adam(w_ref[...], g, m_ref[...], v_ref[...])

    big = pl.BlockSpec((tr, C), lambda i: (i, 0))
    return pl.pallas_call(
        body, grid=(R // tr,), name=name,
        in_specs=[pl.BlockSpec((N_DEV // 2, tr, C), lambda i: (0, i, 0)), big, big, big],
        out_specs=[big] * 4, out_shape=[_sds((R, C), F32)] * 4,
        compiler_params=_params(("parallel",), big=True),
    )(parts, w, m, v)


def _pair_sum(g8, land, core, name):
    _, NCHIP, R, C = g8.shape
    tr = _tile(R, 1024, 16)

    def body(core_ref, g_ref, l_ref, o_ref):
        o_ref[...] = g_ref[...] + l_ref[...]

    return pl.pallas_call(
        body, name=name,
        grid_spec=pltpu.PrefetchScalarGridSpec(
            num_scalar_prefetch=1, grid=(NCHIP, R // tr),
            in_specs=[pl.BlockSpec((None, None, tr, C), lambda k, i, core_ref: (core_ref[0], k, i, 0)),
                      pl.BlockSpec((None, tr, C), lambda k, i, core_ref: (k, i, 0))],
            out_specs=pl.BlockSpec((None, tr, C), lambda k, i, core_ref: (k, i, 0))),
        out_shape=_sds((NCHIP, R, C), BF16), compiler_params=_params(("parallel", "parallel")),
    )(core, g8, land)


SMALL = ("b_ada", "rel_bias", "attn_norm_g", "lb_logits", "gnorm_g", "ln1_g", "ln1_b", "ln2_g", "ln2_b")


def _small_update(parts, loss_parts, lbv, ws, ms, vs, max_rel):
    n = len(SMALL)

    def body(*refs):
        part_refs = dict(zip(SMALL, refs[:n]))
        loss_in, lb_ref = refs[n], refs[n + 1]
        w_refs, m_refs, v_refs = refs[n + 2:2 * n + 2], refs[2 * n + 2:3 * n + 2], refs[3 * n + 2:4 * n + 2]
        outs = refs[4 * n + 2:]

        def total(ref):
            tot = ref[0]
            for k in range(1, N_DEV):
                tot = tot + ref[k]
            return tot

        outs[0][...] = jnp.sum(total(loss_in), axis=-1, keepdims=True)
        for idx, name in enumerate(SMALL):
            g = total(part_refs[name])
            if name == "rel_bias":
                g = _dot(g, _bias_onehot(w_refs[idx].shape[1], max_rel), NT, HIGHEST)
            elif name == "lb_logits":
                lb = lb_ref[...]
                sign = (1 - 2 * lax.broadcasted_iota(jnp.int32, (2, 1), 0)).astype(F32)
                g = sign * (g * lb * (1.0 - lb))
            elif name == "gnorm_g":
                g = _colsum(g)
            dw, mm, vv = _adam(w_refs[idx][...], g, m_refs[idx][...], v_refs[idx][...])
            outs[1 + 4 * idx][...] = g
            outs[2 + 4 * idx][...] = dw
            outs[3 + 4 * idx][...] = mm
            outs[4 + 4 * idx][...] = vv

    out_shape = [_sds((1, 1), F32)]
    for w in ws:
        out_shape += [_sds(w.shape, F32)] * 4
    return pl.pallas_call(body, name="small_update", out_shape=out_shape, compiler_params=_params(big=True))(
        *[parts[k] for k in SMALL], loss_parts, lbv, *ws, *ms, *vs)


def _place():
    x, y, c = lax.axis_index("x"), lax.axis_index("y"), lax.axis_index("c")
    return x, y, c, [(1 - x, y), (x, 1 - y), (1 - x, 1 - y)]


def _all_gather(shard, name):
    HBM = pl.BlockSpec(memory_space=pl.ANY)

    def body(x_ref, out_ref, send_sems, recv_sems, local_sem):
        x, y, c, chips = _place()
        me, sibling = (x, y, c), (x, y, 1 - c)

        def slot(px, py, pc):
            return out_ref.at[4 * px + 2 * py + pc]

        def copy(k, block, to, src=None):
            return pltpu.make_async_remote_copy(
                src_ref=slot(*block) if src is None else src, dst_ref=slot(*block),
                send_sem=send_sems.at[k], recv_sem=recv_sems.at[k], device_id=to, device_id_type=MESH)

        mine = pltpu.make_async_copy(x_ref, slot(*me), local_sem)
        mine.start()
        first = [copy(0, me, sibling, src=x_ref)]
        first += [copy(1 + j, me, (*chip, c), src=x_ref) for j, chip in enumerate(chips)]
        for cp in first:
            cp.start()
        passed = [copy(4 + j, (*chip, c), sibling) for j, chip in enumerate(chips)]
        for j, chip in enumerate(chips):
            copy(1 + j, (*chip, c), me).wait_recv()
            passed[j].start()
        copy(0, sibling, me).wait_recv()
        for j, chip in enumerate(chips):
            copy(4 + j, (*chip, 1 - c), me).wait_recv()
        for cp in first + passed:
            cp.wait_send()
        mine.wait()

    return pl.pallas_call(
        body, name=name, out_shape=_sds((N_DEV,) + shard.shape, shard.dtype),
        in_specs=[HBM], out_specs=HBM,
        scratch_shapes=[pltpu.SemaphoreType.DMA((7,)), pltpu.SemaphoreType.DMA((7,)), pltpu.SemaphoreType.DMA(())],
    )(shard)


SEM_SPEC = pl.BlockSpec(memory_space=pltpu.SEMAPHORE)
HBM_SPEC = pl.BlockSpec(memory_space=pltpu.HBM)
EFFECT = pltpu.SideEffectType.DATAFLOW_SIDE_EFFECTING


def _remote(src, dst, send_sems, recv_sems, k, dev):
    return pltpu.make_async_remote_copy(src_ref=src, dst_ref=dst, send_sem=send_sems.at[k], recv_sem=recv_sems.at[k],
                                        device_id=dev, device_id_type=MESH)


def _copy_start(name, bufs, plan, n, after, only=None):
    nb = len(bufs)

    def body(*refs):
        send_sems, recv_sems = refs[nb + 1], refs[nb + 2]
        for k, (src, dst, dev) in enumerate(plan(*refs[:nb])):
            if only is not None and k not in only:
                continue
            _remote(src, dst, send_sems, recv_sems, k, dev).start()
        refs[-1][...] = jnp.zeros_like(refs[-1])

    out = pl.pallas_call(
        body, name=name,
        out_shape=(pltpu.SemaphoreType.DMA((n,)), pltpu.SemaphoreType.DMA((n,)),
                   *[pltpu.HBM(b.shape, b.dtype) for b in bufs], _sds((8, LANE), F32)),
        in_specs=[HBM_SPEC] * nb + [ORDER_ONLY],
        out_specs=(SEM_SPEC, SEM_SPEC, *[HBM_SPEC] * nb, pl.BlockSpec(memory_space=pltpu.VMEM)),
        input_output_aliases={i: 2 + i for i in range(nb)},
        compiler_params=pltpu.CompilerParams(has_side_effects=EFFECT),
    )(*[pltpu.with_memory_space_constraint(b, pltpu.HBM) for b in bufs], after)
    return (out[0], out[1]), list(out[2:2 + nb]), out[-1]


def _copy_wait(name, sems, bufs, plan, after, only=None):
    nb = len(bufs)

    def body(*refs):
        send_sems, recv_sems = refs[nb], refs[nb + 1]
        for k, (src, dst, dev) in enumerate(plan(*refs[:nb])):
            if only is not None and k not in only:
                continue
            cp = _remote(src, dst, send_sems, recv_sems, k, dev)
            cp.wait_send()
            cp.wait_recv()

    out = pl.pallas_call(
        body, name=name, out_shape=tuple(pltpu.HBM(b.shape, b.dtype) for b in bufs),
        in_specs=[HBM_SPEC] * nb + [SEM_SPEC, SEM_SPEC, pl.BlockSpec(memory_space=pl.ANY)],
        out_specs=tuple([HBM_SPEC] * nb), input_output_aliases={i: i for i in range(nb)},
        compiler_params=pltpu.CompilerParams(has_side_effects=EFFECT),
    )(*bufs, sems[0], sems[1], after)
    return list(out)


def _ag_plan_chips(shard_ref, out_ref):
    x, y, c, chips = _place()
    mine = out_ref.at[4 * x + 2 * y + c]
    return [(shard_ref, mine, (x, y, 1 - c))] + [(shard_ref, mine, (*chip, c)) for chip in chips]


def _ag_plan_pass(out_ref):
    x, y, c, chips = _place()
    slots = [out_ref.at[4 * chip[0] + 2 * chip[1] + c] for chip in chips]
    return [(s, s, (x, y, 1 - c)) for s in slots]


def _rs_plan_pair(g_ref, land_ref):
    x, y, c, _ = _place()
    return [(g_ref.at[1 - c], land_ref, (x, y, 1 - c))]


def _rs_plan_chips(p_ref, land_ref):
    x, y, c, chips = _place()
    return [(p_ref.at[2 * chip[0] + chip[1]], land_ref.at[2 * x + y], (*chip, c)) for chip in chips]


class _Gather:
    @staticmethod
    def landing(shard, me):
        return lax.dynamic_update_slice(lax.empty((N_DEV,) + shard.shape, shard.dtype), shard[None],
                                        (me,) + (0,) * shard.ndim)

    def __init__(self, shard, out, tag, after):
        self.tag = tag
        self.sems, (self.shard, self.out), self.token = _copy_start(
            "ag_start_" + tag, [shard, out], _ag_plan_chips, 4, after)
        self.groups = []

    def arrived(self, after, copies):
        name = "ag_wait_%s_%s" % (self.tag, "".join(map(str, copies)))
        self.shard, self.out = _copy_wait(name, self.sems, [self.shard, self.out], _ag_plan_chips, after, copies)
        return self.out

    def pass_on(self, after, blocks):
        name = "ag_pass_%s_%s" % (self.tag, "".join(map(str, blocks)))
        sems, (self.out,), _ = _copy_start(name, [self.out], _ag_plan_pass, 3, after, blocks)
        self.groups.append((sems, blocks))
        return self.out

    def passed(self, after, group):
        sems, blocks = self.groups[group]
        name = "ag_pass_wait_%s_%s" % (self.tag, "".join(map(str, blocks)))
        self.out = _copy_wait(name, sems, [self.out], _ag_plan_pass, after, blocks)[0]
        return self.out

    def arrived_from_chips(self, after):
        self.arrived(after, (0, 1, 2, 3))
        return self.pass_on(after, (0, 1, 2))

    def passed_on(self, after):
        return self.passed(after, 0)


def _ag_plan_direct(src_ref, out_ref):
    x, y, c, chips = _place()
    mine = out_ref.at[4 * x + 2 * y + c]
    peers = [(x, y, 1 - c)] + [(*chip, pc) for chip in chips for pc in (c, 1 - c)]
    return [(src_ref, mine, peer) for peer in peers]


class _SmallGather:
    def __init__(self, block, me, tag):
        self.tag = tag
        out = lax.dynamic_update_slice(lax.empty((N_DEV,) + block.shape, block.dtype), block[None],
                                       (me,) + (0,) * block.ndim)
        self.sems, self.bufs, self.token = _copy_start(
            "ag_direct_start_" + tag, [block, out], _ag_plan_direct, N_DEV - 1, jnp.zeros((1,), F32))

    def done(self, after):
        return _copy_wait("ag_direct_wait_" + self.tag, self.sems, self.bufs, _ag_plan_direct, after)[1]


class _ReduceScatter:
    def __init__(self, g8, tag):
        self.tag = tag
        land = lax.empty(g8.shape[1:], g8.dtype)
        self.sems, self.bufs, self.token = _copy_start(
            "rs_pair_start_" + tag, [g8, land], _rs_plan_pair, 1, jnp.zeros((1,), F32))

    def pair_done(self, core, chip, after):
        g8, land = _copy_wait("rs_pair_wait_" + self.tag, self.sems, self.bufs, _rs_plan_pair, after)
        p4 = _pair_sum(g8, land, core, "rs_pair_sum_" + self.tag)
        own = lax.dynamic_slice_in_dim(p4, chip, 1, axis=0)
        land2 = lax.dynamic_update_slice(lax.empty(p4.shape, p4.dtype), own, (chip, 0, 0))
        self.sems, self.bufs, self.token = _copy_start(
            "rs_chips_start_" + self.tag, [p4, land2], _rs_plan_chips, 3, jnp.zeros((1,), F32))

    def sums(self, after):
        return _copy_wait("rs_chips_wait_" + self.tag, self.sems, self.bufs, _rs_plan_chips, after)[1]


ORDER = ("w_ada", "b_ada", "w_in", "rel_bias", "attn_norm_g", "lb_logits", "gnorm_g", "w_o", "ln1_g", "ln1_b",
         "w_ffn_in", "w_ffn_out", "ln2_g", "ln2_b")


def kernel(x, c, w_ada, b_ada, w_in, rel_bias, attn_norm_g, lb_logits, gnorm_g, w_o, ln1_g, ln1_b, w_ffn_in, w_ffn_out, ln2_g, ln2_b, loss_target, m_w_ada, m_b_ada, m_w_in, m_rel_bias, m_attn_norm_g, m_lb_logits, m_gnorm_g, m_w_o, m_ln1_g, m_ln1_b, m_w_ffn_in, m_w_ffn_out, m_ln2_g, m_ln2_b, v_w_ada, v_b_ada, v_w_in, v_rel_bias, v_attn_norm_g, v_lb_logits, v_gnorm_g, v_w_o, v_ln1_g, v_ln1_b, v_w_ffn_in, v_w_ffn_out, v_ln2_g, v_ln2_b):
    W = dict(w_ada=w_ada, b_ada=b_ada, w_in=w_in, rel_bias=rel_bias, attn_norm_g=attn_norm_g, lb_logits=lb_logits,
             gnorm_g=gnorm_g, w_o=w_o, ln1_g=ln1_g, ln1_b=ln1_b, w_ffn_in=w_ffn_in, w_ffn_out=w_ffn_out,
             ln2_g=ln2_g, ln2_b=ln2_b)
    M = dict(w_ada=m_w_ada, b_ada=m_b_ada, w_in=m_w_in, rel_bias=m_rel_bias, attn_norm_g=m_attn_norm_g,
             lb_logits=m_lb_logits, gnorm_g=m_gnorm_g, w_o=m_w_o, ln1_g=m_ln1_g, ln1_b=m_ln1_b,
             w_ffn_in=m_w_ffn_in, w_ffn_out=m_w_ffn_out, ln2_g=m_ln2_g, ln2_b=m_ln2_b)
    V = dict(w_ada=v_w_ada, b_ada=v_b_ada, w_in=v_w_in, rel_bias=v_rel_bias, attn_norm_g=v_attn_norm_g,
             lb_logits=v_lb_logits, gnorm_g=v_gnorm_g, w_o=v_w_o, ln1_g=v_ln1_g, ln1_b=v_ln1_b,
             w_ffn_in=v_w_ffn_in, w_ffn_out=v_w_ffn_out, ln2_g=v_ln2_g, ln2_b=v_ln2_b)

    x2, tgt = x[0], loss_target[0]
    T, D = x2.shape
    AW, RW = attn_norm_g.shape[-1], lb_logits.shape[-1]
    MIX = AW + RW
    H, RH = AW // ATTN_HEAD_DIM, RW // LANE
    RB = rel_bias.shape[-1]
    max_rel = (RB - 1) // 2
    rbp = -(-RB // LANE) * LANE
    F = w_ffn_out.shape[1] * N_DEV
    half = N_DEV // 2
    xi, yi, ci = lax.axis_index("x"), lax.axis_index("y"), lax.axis_index("c")
    me = 4 * xi + 2 * yi + ci
    core = jnp.reshape(ci, (1,)).astype(jnp.int32)
    pad_rb = lambda a: jnp.pad(a[0], ((0, 0), (0, rbp - RB)))

    chip = 2 * xi + yi

    w_in_b = w_in[0].astype(BF16)
    w_in_land = _Gather.landing(w_in_b, me)
    c_act, lbv, gv = _prep(c, lb_logits, pad_rb(rel_bias), max_rel, (w_in_b, w_in_land))
    c_all = _all_gather(c_act, "ag_c").reshape(N_DEV, D)
    ns_ada = w_ada.shape[-1]
    mod_part = _mod_part(c_all, w_ada[0], lax.dynamic_slice_in_dim(b_ada, me * ns_ada, ns_ada, axis=1))
    mod_all = _all_gather(mod_part, "ag_mod")
    mod6 = lax.dynamic_index_in_dim(mod_all, me, axis=1, keepdims=False).reshape(6, D)

    bf = lambda w: w[0].astype(BF16)
    ag_in = _Gather(w_in_b, w_in_land, "w_in", mod_all)
    ag_o = _Gather(bf(w_o), _Gather.landing(bf(w_o), me), "w_o", ag_in.token)
    ag_f1 = _Gather(bf(w_ffn_in), _Gather.landing(bf(w_ffn_in), me), "w_ffn_in", ag_o.token)
    ag_f2 = _Gather(bf(w_ffn_out), _Gather.landing(bf(w_ffn_out), me), "w_ffn_out", ag_f1.token)

    h1 = _ln_mod(x2, mod6 + ag_f2.token[0, 0])
    ids = lambda pairs: jnp.stack([4 * px + 2 * py + pc for px, py, pc in pairs]).astype(jnp.int32)
    others = [(1 - xi, yi), (xi, 1 - yi), (1 - xi, 1 - yi)]
    proj = lax.empty((T, w_in.shape[-1] * N_DEV), F32)
    proj = _mm_gathered(h1, ag_in.arrived(h1, (0,)), ids([(xi, yi, ci), (xi, yi, 1 - ci)]), proj, "in_proj_a")
    ag_in.arrived(proj, (1, 2, 3))
    proj = _mm_gathered(h1, ag_in.pass_on(proj, (0, 1, 2)), ids([(*ch, ci) for ch in others]), proj, "in_proj_b")
    wg_in = ag_in.passed(proj, 0)
    proj = _mm_gathered(h1, wg_in, ids([(*ch, 1 - ci) for ch in others]), proj, "in_proj_c")
    ag_o.arrived_from_chips(proj)
    mixin = _attn_fwd(proj, gv, attn_norm_g, AW, lax.empty((T, MIX), BF16))
    wg_o = ag_o.passed_on(mixin).reshape(MIX, D)
    mixin, o_b, st_all = _hgrn_fwd(proj, lbv, gnorm_g, AW, RW, mixin)
    mix = _mm_nn(mixin, wg_o, "out_proj")
    ag_f1.arrived_from_chips(mix)
    x1, h2 = _mid_fwd(x2, mix, mod6, ln1_g, ln1_b)
    wg_f1 = ag_f1.passed_on(h2)
    gu, act = _mm_swiglu(h2, wg_f1)
    ag_f2.arrived_from_chips(act)
    wg_f2 = ag_f2.passed_on(act).reshape(F, D)
    ff = _mm_nn(act, wg_f2, "ffn_out")
    dff, dx1a, vec_a = _final(x1, ff, mod6, ln2_g, ln2_b, tgt)

    du = _mm_swiglu_bwd(dff, wg_f2, gu)
    rs_f2 = _ReduceScatter(_mm_tn_rows(dff, act, dff, F // N_DEV, "grad_w_ffn_out"), "w_ffn_out")
    tm = _tile(T, 512, 16)
    du_ij = lambda tm_, w, first: pl.BlockSpec((None, tm_, w), lambda i, p: (p // (half // 2), i + first, p % (half // 2)))
    du_j = lambda rows, ns: pl.BlockSpec((None, rows, ns), lambda j: (j // half, 0, j % half))
    dh2 = _mm_gathered_nt(rs_f2.token, du, du_ij, wg_f1, T, tm, "ffn_in_bwd")
    rs_f2.pair_done(core, chip, dh2)
    gw_f1 = _mm_tn_gathered(rs_f2.token, h2, du, du_j, wg_f1.shape[-1], "grad_w_ffn_in")
    rs_f1 = _ReduceScatter(gw_f1.reshape(2, half, D, -1), "w_ffn_in")
    dmix, dxa, vec_b = _mid_bwd(x2, mix, x1, dx1a, dh2, mod6 + rs_f1.token[0, 0], ln1_g)
    dmixin = _mm_nt(dmix, wg_o, "out_proj_bwd")
    rs_f1.pair_done(core, chip, dmixin)
    rs_o = _ReduceScatter(_mm_tn_rows(rs_f1.token, mixin, dmix, MIX // N_DEV, "grad_w_o"), "w_o")
    dq, dk, dv, dgv, dga = _attn_bwd(proj, dmixin, gv + rs_o.token[0, 0], attn_norm_g, AW)
    rs_o.pair_done(core, chip, dq)
    dqb, dfl, dib, dgb, dlb, dgn = _hgrn_bwd(proj, dmixin, o_b, st_all, lbv + rs_o.token[0, 0], gnorm_g, AW, RW)
    dproj = jnp.concatenate([dq, dk, dv, dqb, dfl, dib, dgb], axis=1)
    p_ij = lambda tm_, w, first: pl.BlockSpec((tm_, w), lambda i, p: (i + first, p))
    p_j = lambda rows, ns: pl.BlockSpec((rows, ns), lambda j: (0, j))
    gw_in = _mm_tn_gathered(rs_o.token, h1, dproj, p_j, wg_in.shape[-1], "grad_w_in")
    rs_in = _ReduceScatter(gw_in.reshape(2, half, D, -1), "w_in")
    n_tiles = T // tm
    dh1 = _mm_gathered_nt(rs_in.token, dproj, p_ij, wg_in, T, tm, "in_proj_bwd_a", 0, n_tiles // 2)
    rs_in.pair_done(core, chip, dh1)
    dh1 = _mm_gathered_nt(rs_in.token, dproj, p_ij, wg_in, T, tm, "in_proj_bwd_b", n_tiles // 2,
                          n_tiles - n_tiles // 2, dh1)
    grad_x, vec_c = _first_bwd(x2, dh1, dxa, mod6)

    dmod = jnp.concatenate([vec_c[1:2], vec_c[0:1], vec_b[4:5], vec_b[1:2], vec_b[0:1], vec_a[2:3]], axis=0)
    pieces = dict(b_ada=dmod, rel_bias=dgv, attn_norm_g=dga, lb_logits=dlb, gnorm_g=dgn, ln1_g=vec_b[2:3],
                  ln1_b=vec_b[3:4], ln2_g=vec_a[0:1], ln2_b=vec_a[1:2], loss=vec_a[3:4])
    widths = dict(b_ada=(1, 6 * D), rel_bias=(H, TAB), attn_norm_g=(1, AW), lb_logits=(1, RW), gnorm_g=(RH, LANE),
                  ln1_g=(1, D), ln1_b=(1, D), ln2_g=(1, D), ln2_b=(1, D), loss=(1, D))
    packed = jnp.concatenate([pieces[k].reshape(-1, LANE) for k in widths], axis=0)
    small_ag = _SmallGather(packed, me, "small")
    after, res_big = small_ag.token, {}
    for k, rs in (("w_ffn_out", rs_f2), ("w_ffn_in", rs_f1), ("w_o", rs_o), ("w_in", rs_in)):
        four = _adam_shard(rs.sums(after), W[k][0], M[k][0], V[k][0], "adam_" + k)
        res_big[k] = [a[None] for a in four]
        after = four[0]
    gathered = small_ag.done(after)
    parts, r0 = {}, 0
    for k, (rows, width) in widths.items():
        nr = rows * width // LANE
        parts[k] = gathered[:, r0:r0 + nr, :].reshape(N_DEV, rows, width)
        r0 += nr
    prep_small = lambda d, k: pad_rb(d[k]) if k == "rel_bias" else d[k]
    small = _small_update(parts, parts["loss"], lbv, [prep_small(W, k) for k in SMALL],
                          [prep_small(M, k) for k in SMALL], [prep_small(V, k) for k in SMALL], max_rel)
    loss = small[0].reshape(())
    res = {}
    for idx, k in enumerate(SMALL):
        four = small[1 + 4 * idx:5 + 4 * idx]
        if k == "rel_bias":
            four = [a[:, :RB][None] for a in four]
        res[k] = list(four)

    res.update(res_big)
    dmod_s = lax.dynamic_slice_in_dim(parts["b_ada"].reshape(N_DEV, 6 * D), me * ns_ada, ns_ada, axis=1)
    res["w_ada"] = [a[None] for a in _adam_ada(c_all, dmod_s, w_ada[0], m_w_ada[0], v_w_ada[0])]

    out = [loss, grad_x[None]]
    for field in range(4):
        out += [res[k][field] for k in ORDER]
    return tuple(out)
```

```python
import jax
import jax.numpy as jnp
from jax import lax
from jax.experimental import pallas as pl
from jax.experimental.pallas import tpu as pltpu

F32 = jnp.float32
BF16 = jnp.bfloat16
MESH = pl.DeviceIdType.MESH
HIGHEST = lax.Precision.HIGHEST

N_DEV = 8
CHUNK = 64
N_PAST = 8
QBLK = 4 * CHUNK
KPAD = N_PAST * CHUNK
WIN = KPAD + QBLK
TAB = 1024
ATTN_HEAD_DIM = 64
ATTN_HEADS_PER_STEP = 4
SUB = 32
ROWS = 8
LANE = 128
EPS = 1e-5
ALPHA = 2.0 ** 0.25
ADAM_LR, ADAM_B1, ADAM_B2, ADAM_EPS, ADAM_WD, ADAM_STEP = 0.001, 0.9, 0.999, 1e-08, 0.01, 10
NEG = -1e30
VMEM_LIMIT = 56 * 1024 * 1024


def _sds(shape, dtype):
    return jax.ShapeDtypeStruct(tuple(shape), dtype)


def _tile(n, pref, mult):
    best = None
    for t in range(mult, min(n, pref) + 1, mult):
        if n % t == 0:
            best = t
    return n if best is None else best


def _params(sem=None, big=False):
    kw = {}
    if sem is not None:
        kw["dimension_semantics"] = sem
    if big:
        kw["vmem_limit_bytes"] = VMEM_LIMIT
    return pltpu.CompilerParams(**kw)


def _sigmoid(v):
    return 1.0 / (1.0 + jnp.exp(-v))


def _dot(a, b, dims, precision=None):
    return lax.dot_general(a, b, (dims, ((), ())), preferred_element_type=F32, precision=precision)


NN = ((1,), (0,))
NT = ((1,), (1,))
TN = ((0,), (0,))


def _ln(v):
    mu = jnp.mean(v, axis=-1, keepdims=True)
    d = v - mu
    rstd = lax.rsqrt(jnp.mean(d * d, axis=-1, keepdims=True) + EPS)
    return d * rstd, rstd


def _ln_bwd(dxh, xh, rstd):
    return rstd * (dxh - jnp.mean(dxh, axis=-1, keepdims=True) - xh * jnp.mean(dxh * xh, axis=-1, keepdims=True))


def _colsum(v):
    return jnp.sum(v, axis=0, keepdims=True)


def _ln_mod(x2, mod6):
    T, D = x2.shape
    tm = _tile(T, 256, 8)

    def body(x_ref, mod_ref, o_ref):
        xh, _ = _ln(x_ref[...])
        o_ref[...] = (xh * (1.0 + mod_ref[1:2, :]) + mod_ref[0:1, :]).astype(BF16)

    return pl.pallas_call(
        body, grid=(T // tm,), name="ln_mod",
        in_specs=[pl.BlockSpec((tm, D), lambda i: (i, 0)), pl.BlockSpec((6, D), lambda i: (0, 0))],
        out_specs=pl.BlockSpec((tm, D), lambda i: (i, 0)),
        out_shape=_sds((T, D), BF16), compiler_params=_params(("parallel",)),
    )(x2, mod6)


def _mid_fwd(x2, mix, mod6, ln1_g, ln1_b):
    T, D = x2.shape
    tm = _tile(T, 256, 8)

    def body(x_ref, mix_ref, mod_ref, g_ref, b_ref, x1_ref, h2_ref):
        zh, _ = _ln(ALPHA * x_ref[...] + mod_ref[2:3, :] * mix_ref[...])
        x1 = zh * g_ref[...] + b_ref[...]
        x1_ref[...] = x1
        xh, _ = _ln(x1)
        h2_ref[...] = (xh * (1.0 + mod_ref[4:5, :]) + mod_ref[3:4, :]).astype(BF16)

    row = pl.BlockSpec((tm, D), lambda i: (i, 0))
    vec = pl.BlockSpec((1, D), lambda i: (0, 0))
    return pl.pallas_call(
        body, grid=(T // tm,), name="mid_fwd",
        in_specs=[row, row, pl.BlockSpec((6, D), lambda i: (0, 0)), vec, vec],
        out_specs=[row, row],
        out_shape=[_sds((T, D), F32), _sds((T, D), BF16)], compiler_params=_params(("parallel",)),
    )(x2, mix, mod6, ln1_g, ln1_b)


def _final(x1, ff, mod6, ln2_g, ln2_b, tgt):
    T, D = x1.shape
    tm = _tile(T, 256, 8)

    def body(x1_ref, ff_ref, mod_ref, g_ref, b_ref, t_ref, dff_ref, dx1_ref, vec_ref):
        @pl.when(pl.program_id(0) == 0)
        def _():
            vec_ref[...] = jnp.zeros_like(vec_ref)

        ff_v = ff_ref[...]
        gate2 = mod_ref[5:6, :]
        zh, rstd = _ln(ALPHA * x1_ref[...] + gate2 * ff_v)
        err = zh * g_ref[...] + b_ref[...] - t_ref[...]
        dy = err * (1.0 / D)
        dz = _ln_bwd(dy * g_ref[...], zh, rstd)
        dff_ref[...] = (gate2 * dz).astype(BF16)
        dx1_ref[...] = ALPHA * dz
        vec_ref[0:1, :] += _colsum(dy * zh)
        vec_ref[1:2, :] += _colsum(dy)
        vec_ref[2:3, :] += _colsum(dz * ff_v)
        vec_ref[3:4, :] += _colsum(err * err) * (0.5 / D)

    row = pl.BlockSpec((tm, D), lambda i: (i, 0))
    vec = pl.BlockSpec((1, D), lambda i: (0, 0))
    return pl.pallas_call(
        body, grid=(T // tm,), name="final_fwd_bwd",
        in_specs=[row, row, pl.BlockSpec((6, D), lambda i: (0, 0)), vec, vec, row],
        out_specs=[row, row, pl.BlockSpec((8, D), lambda i: (0, 0))],
        out_shape=[_sds((T, D), BF16), _sds((T, D), F32), _sds((8, D), F32)],
        compiler_params=_params(("arbitrary",)),
    )(x1, ff, mod6, ln2_g, ln2_b, tgt)


def _mid_bwd(x2, mix, x1, dx1a, dh2, mod6, ln1_g):
    T, D = x2.shape
    tm = _tile(T, 256, 8)

    def body(x_ref, mix_ref, x1_ref, dx1a_ref, dh2_ref, mod_ref, g_ref, dmix_ref, dxa_ref, vec_ref):
        @pl.when(pl.program_id(0) == 0)
        def _():
            vec_ref[...] = jnp.zeros_like(vec_ref)

        dh2 = dh2_ref[...]
        xh, rstd = _ln(x1_ref[...])
        dx1 = dx1a_ref[...] + _ln_bwd(dh2 * (1.0 + mod_ref[4:5, :]), xh, rstd)
        mix_v = mix_ref[...]
        gate1 = mod_ref[2:3, :]
        zh, rstdz = _ln(ALPHA * x_ref[...] + gate1 * mix_v)
        dz = _ln_bwd(dx1 * g_ref[...], zh, rstdz)
        dmix_ref[...] = (gate1 * dz).astype(BF16)
        dxa_ref[...] = ALPHA * dz
        vec_ref[0:1, :] += _colsum(dh2 * xh)
        vec_ref[1:2, :] += _colsum(dh2)
        vec_ref[2:3, :] += _colsum(dx1 * zh)
        vec_ref[3:4, :] += _colsum(dx1)
        vec_ref[4:5, :] += _colsum(dz * mix_v)

    row = pl.BlockSpec((tm, D), lambda i: (i, 0))
    vec = pl.BlockSpec((1, D), lambda i: (0, 0))
    return pl.pallas_call(
        body, grid=(T // tm,), name="mid_bwd",
        in_specs=[row, row, row, row, row, pl.BlockSpec((6, D), lambda i: (0, 0)), vec],
        out_specs=[row, row, pl.BlockSpec((8, D), lambda i: (0, 0))],
        out_shape=[_sds((T, D), BF16), _sds((T, D), F32), _sds((8, D), F32)],
        compiler_params=_params(("arbitrary",)),
    )(x2, mix, x1, dx1a, dh2, mod6, ln1_g)


def _first_bwd(x2, dh1, dxa, mod6):
    T, D = x2.shape
    tm = _tile(T, 256, 8)

    def body(x_ref, dh1_ref, dxa_ref, mod_ref, gx_ref, vec_ref):
        @pl.when(pl.program_id(0) == 0)
        def _():
            vec_ref[...] = jnp.zeros_like(vec_ref)

        dh1 = dh1_ref[...]
        xh, rstd = _ln(x_ref[...])
        gx_ref[...] = dxa_ref[...] + _ln_bwd(dh1 * (1.0 + mod_ref[1:2, :]), xh, rstd)
        vec_ref[0:1, :] += _colsum(dh1 * xh)
        vec_ref[1:2, :] += _colsum(dh1)

    row = pl.BlockSpec((tm, D), lambda i: (i, 0))
    return pl.pallas_call(
        body, grid=(T // tm,), name="first_bwd",
        in_specs=[row, row, row, pl.BlockSpec((6, D), lambda i: (0, 0))],
        out_specs=[row, pl.BlockSpec((8, D), lambda i: (0, 0))],
        out_shape=[_sds((T, D), F32), _sds((8, D), F32)],
        compiler_params=_params(("arbitrary",)),
    )(x2, dh1, dxa, mod6)


def _slot(j):
    return (j % 2) * 4 + j // 2


def _mm_gathered(a, wg, shards, out, name):
    M, K = a.shape
    _, _, ns = wg.shape
    tm = _tile(M, 512, 16)

    def body(shards_ref, a_ref, w_ref, prev_ref, o_ref):
        o_ref[...] = _dot(a_ref[...], w_ref[...], NN)

    return pl.pallas_call(
        body, name=name,
        grid_spec=pltpu.PrefetchScalarGridSpec(
            num_scalar_prefetch=1, grid=(shards.shape[0], M // tm),
            in_specs=[pl.BlockSpec((tm, K), lambda j, i, s: (i, 0)),
                      pl.BlockSpec((None, K, ns), lambda j, i, s: (s[j], 0, 0)), ORDER_ONLY],
            out_specs=pl.BlockSpec((tm, ns), lambda j, i, s: (i, s[j]))),
        out_shape=_sds((M, N_DEV * ns), F32), input_output_aliases={3: 0},
        compiler_params=_params(("parallel", "parallel"), big=True),
    )(shards, a, wg, out)


def _mm_nn(a, b, name):
    M, K = a.shape
    _, N = b.shape
    tm, tn = _tile(M, 512, 16), _tile(N, 1024, LANE)

    def body(a_ref, b_ref, o_ref):
        o_ref[...] = _dot(a_ref[...], b_ref[...], NN)

    return pl.pallas_call(
        body, grid=(N // tn, M // tm), name=name,
        in_specs=[pl.BlockSpec((tm, K), lambda j, i: (i, 0)), pl.BlockSpec((K, tn), lambda j, i: (0, j))],
        out_specs=pl.BlockSpec((tm, tn), lambda j, i: (i, j)),
        out_shape=_sds((M, N), F32), compiler_params=_params(("parallel", "parallel"), big=True),
    )(a, b)


def _mm_nt(a, b, name):
    M, K = a.shape
    N, _ = b.shape
    tm, tn = _tile(M, 512, 16), _tile(N, 1024, LANE)

    def body(a_ref, b_ref, o_ref):
        o_ref[...] = _dot(a_ref[...], b_ref[...], NT)

    return pl.pallas_call(
        body, grid=(M // tm, N // tn), name=name,
        in_specs=[pl.BlockSpec((tm, K), lambda i, j: (i, 0)), pl.BlockSpec((tn, K), lambda i, j: (j, 0))],
        out_specs=pl.BlockSpec((tm, tn), lambda i, j: (i, j)),
        out_shape=_sds((M, N), F32), compiler_params=_params(("parallel", "parallel"), big=True),
    )(a, b)


def _mm_swiglu(h2, wg):
    M, K = h2.shape
    _, _, ns = wg.shape
    half = N_DEV // 2
    tm = _tile(M, 256, 16)

    def body(a_ref, wgate_ref, wup_ref, gu_ref, act_ref):
        a = a_ref[...]
        g = _dot(a, wgate_ref[...], NN)
        u = _dot(a, wup_ref[...], NN)
        sg = _sigmoid(g)
        silu = g * sg
        gu_ref[0] = u * (sg * (1.0 + g * (1.0 - sg)))
        gu_ref[1] = silu
        act_ref[...] = (silu * u).astype(BF16)

    return pl.pallas_call(
        body, grid=(half, M // tm), name="ffn_in_swiglu",
        in_specs=[pl.BlockSpec((tm, K), lambda j, i: (i, 0)),
                  pl.BlockSpec((None, K, ns), lambda j, i: (j, 0, 0)),
                  pl.BlockSpec((None, K, ns), lambda j, i: (j + half, 0, 0))],
        out_specs=[pl.BlockSpec((2, tm, ns), lambda j, i: (0, i, j)), pl.BlockSpec((tm, ns), lambda j, i: (i, j))],
        out_shape=[_sds((2, M, half * ns), F32), _sds((M, half * ns), BF16)],
        compiler_params=_params(("parallel", "parallel"), big=True),
    )(h2, wg, wg)


def _mm_swiglu_bwd(dff, w2, gu):
    M, K = dff.shape
    F = w2.shape[0]
    tm, tn = _tile(M, 512, 16), _tile(F, 1408, LANE)

    def body(a_ref, b_ref, gu_ref, du_ref):
        da = _dot(a_ref[...], b_ref[...], NT)
        du_ref[0] = (da * gu_ref[0]).astype(BF16)
        du_ref[1] = (da * gu_ref[1]).astype(BF16)

    return pl.pallas_call(
        body, grid=(F // tn, M // tm), name="ffn_out_bwd_swiglu",
        in_specs=[pl.BlockSpec((tm, K), lambda j, i: (i, 0)), pl.BlockSpec((tn, K), lambda j, i: (j, 0)),
                  pl.BlockSpec((2, tm, tn), lambda j, i: (0, i, j))],
        out_specs=pl.BlockSpec((2, tm, tn), lambda j, i: (0, i, j)),
        out_shape=_sds((2, M, F), BF16), compiler_params=_params(("parallel", "parallel"), big=True),
    )(dff, w2, gu)


ORDER_ONLY = pl.BlockSpec(memory_space=pl.ANY)


def _mm_tn_rows(dep, a, b, rs, name):
    M, Ka = a.shape
    _, N = b.shape

    def body(_, a_ref, b_ref, o_ref):
        g = _dot(a_ref[...], b_ref[...], TN)
        o_ref[0, 0] = g[0:rs, :].astype(BF16)
        o_ref[1, 0] = g[rs:2 * rs, :].astype(BF16)

    return pl.pallas_call(
        body, grid=(N_DEV // 2,), name=name,
        in_specs=[ORDER_ONLY, pl.BlockSpec((M, 2 * rs), lambda ch: (0, ch)), pl.BlockSpec((M, N), lambda ch: (0, 0))],
        out_specs=pl.BlockSpec((2, 1, rs, N), lambda ch: (0, ch, 0, 0)),
        out_shape=_sds((2, N_DEV // 2, rs, N), BF16),
        compiler_params=_params(("parallel",), big=True),
    )(dep, a, b)


def _mm_gathered_nt(dep, a, a_spec, wg, M, tm, name, first=0, count=None, out=None):
    _, K, ns = wg.shape
    count = M // tm if count is None else count
    out = lax.empty((M, K), F32) if out is None else out

    def body(_, a_ref, w_ref, prev_ref, o_ref):
        @pl.when(pl.program_id(1) == 0)
        def _():
            o_ref[...] = jnp.zeros_like(o_ref)

        o_ref[...] += _dot(a_ref[:, 0:ns], w_ref[0], NT) + _dot(a_ref[:, ns:2 * ns], w_ref[1], NT)

    return pl.pallas_call(
        body, grid=(count, N_DEV // 2), name=name,
        in_specs=[ORDER_ONLY, a_spec(tm, 2 * ns, first), pl.BlockSpec((2, K, ns), lambda i, p: (p, 0, 0)), ORDER_ONLY],
        out_specs=pl.BlockSpec((tm, K), lambda i, j: (i + first, 0)),
        out_shape=_sds((M, K), F32), input_output_aliases={3: 0},
        compiler_params=_params(("parallel", "arbitrary"), big=True),
    )(dep, a, wg, out)


def _mm_tn_gathered(dep, h, a, a_spec, ns, name):
    M, K = h.shape

    def body(_, h_ref, a_ref, o_ref):
        o_ref[...] = _dot(h_ref[...], a_ref[...], TN).astype(BF16)

    return pl.pallas_call(
        body, grid=(N_DEV,), name=name,
        in_specs=[ORDER_ONLY, pl.BlockSpec((M, K), lambda j: (0, 0)), a_spec(M, ns)],
        out_specs=pl.BlockSpec((None, K, ns), lambda j: (_slot(j), 0, 0)),
        out_shape=_sds((N_DEV, K, ns), BF16),
        compiler_params=_params(("parallel",), big=True),
    )(dep, h, a)


def _bias_onehot(rbp, max_rel):
    r = lax.broadcasted_iota(jnp.int32, (rbp, TAB), 0)
    m = lax.broadcasted_iota(jnp.int32, (rbp, TAB), 1)
    dist = KPAD - jnp.where(m < WIN, m, m - TAB)
    return (r == jnp.clip(dist, -max_rel, max_rel) + max_rel).astype(F32)


def _attn_setup(i, hp, k_ref, v_ref, gv_ref, kpad, vpad, bias):
    ls = slice(i * ATTN_HEAD_DIM, (i + 1) * ATTN_HEAD_DIM)
    kpad[i][0:KPAD, :] = jnp.zeros((KPAD, ATTN_HEAD_DIM), BF16)
    vpad[i][0:KPAD, :] = jnp.zeros((KPAD, ATTN_HEAD_DIM), BF16)
    kpad[i][KPAD:, :] = k_ref[:, ls].astype(BF16)
    vpad[i][KPAD:, :] = v_ref[:, ls].astype(BF16)
    gvrow = gv_ref[pl.ds(hp * ATTN_HEADS_PER_STEP + i, 1), :]
    tab = pltpu.roll(jnp.broadcast_to(gvrow, (QBLK, TAB)), 0, 1, stride=1, stride_axis=0)
    row = lax.broadcasted_iota(jnp.int32, (QBLK, WIN), 0)
    col = lax.broadcasted_iota(jnp.int32, (QBLK, WIN), 1)
    first = jnp.bitwise_and(row, -CHUNK)
    seen = jnp.logical_and(col >= first, col < first + (N_PAST + 1) * CHUNK)
    bias[i][...] = jnp.where(seen, tab[:, 0:WIN], NEG)


def _attn_probs(b, q_ref, kpad, vpad, bias, col):
    pair = range(ATTN_HEADS_PER_STEP)
    ls = [slice(i * ATTN_HEAD_DIM, (i + 1) * ATTN_HEAD_DIM) for i in pair]
    r0 = pl.multiple_of(b * QBLK, QBLK)
    q = [q_ref[pl.ds(r0, QBLK), ls[i]].astype(BF16) for i in pair]
    kw = [kpad[i][pl.ds(r0, WIN), :] for i in pair]
    vw = [vpad[i][pl.ds(r0, WIN), :] for i in pair]
    s = [_dot(q[i], kw[i], NT) * (ATTN_HEAD_DIM ** -0.5) + bias[i][...] for i in pair]
    s = [jnp.where(col >= KPAD - r0, s[i], NEG) for i in pair]
    p = [jnp.exp(s[i] - jnp.max(s[i], axis=-1, keepdims=True)) for i in pair]
    pn = [p[i] / jnp.sum(p[i], axis=-1, keepdims=True) for i in pair]
    return r0, ls, q, kw, vw, pn


def _attn_fwd(proj, gv, ga, AW):
    T = proj.shape[0]
    AH = ATTN_HEADS_PER_STEP
    W = AH * ATTN_HEAD_DIM
    HP = AW // W

    def body(q_ref, k_ref, v_ref, gv_ref, ga_ref, o_ref, *scratch):
        kpad, vpad, bias = (scratch[k * AH:(k + 1) * AH] for k in range(3))
        hp = pl.program_id(0)
        for i in range(AH):
            _attn_setup(i, hp, k_ref, v_ref, gv_ref, kpad, vpad, bias)
        col = lax.broadcasted_iota(jnp.int32, (QBLK, WIN), 1)

        def block(b, carry):
            pair = range(AH)
            r0, ls, _, _, vw, pn = _attn_probs(b, q_ref, kpad, vpad, bias, col)
            o = [_dot(pn[i].astype(BF16), vw[i], NN) for i in pair]
            r = [lax.rsqrt(jnp.mean(o[i] * o[i], axis=-1, keepdims=True) + EPS) for i in pair]
            outs = [o[i] * r[i] * ga_ref[0:1, ls[i]] for i in pair]
            o_ref[pl.ds(r0, QBLK), :] = jnp.concatenate(outs, axis=1).astype(BF16)
            return carry

        lax.fori_loop(0, T // QBLK, block, 0)

    blk = lambda off: pl.BlockSpec((T, W), lambda hp: (0, off + hp))
    return pl.pallas_call(
        body, grid=(HP,), name="attn_fwd",
        in_specs=[blk(0), blk(HP), blk(2 * HP), pl.BlockSpec(gv.shape, lambda hp: (0, 0)),
                  pl.BlockSpec((1, W), lambda hp: (0, hp))],
        out_specs=pl.BlockSpec((T, W), lambda hp: (0, hp)),
        out_shape=_sds((T, AW), BF16),
        scratch_shapes=[pltpu.VMEM((T + KPAD, ATTN_HEAD_DIM), BF16)] * (2 * AH) + [pltpu.VMEM((QBLK, WIN), F32)] * AH,
        compiler_params=_params(("parallel",), big=True),
    )(proj, proj, proj, gv, ga)


def _attn_bwd(proj, dmixin, gv, ga, AW):
    T = proj.shape[0]
    AH = ATTN_HEADS_PER_STEP
    W = AH * ATTN_HEAD_DIM
    HP = AW // W
    scale = ATTN_HEAD_DIM ** -0.5

    def body(q_ref, k_ref, v_ref, dn_ref, gv_ref, ga_ref, dq_ref, dk_ref, dv_ref, dgv_ref, dga_ref, *scratch):
        kpad, vpad, dkacc, dvacc, bias, dbias = (scratch[k * AH:(k + 1) * AH] for k in range(6))
        hp = pl.program_id(0)
        for i in range(AH):
            _attn_setup(i, hp, k_ref, v_ref, gv_ref, kpad, vpad, bias)
            dkacc[i][...] = jnp.zeros_like(dkacc[i])
            dvacc[i][...] = jnp.zeros_like(dvacc[i])
            dbias[i][...] = jnp.zeros_like(dbias[i])
        dga_ref[...] = jnp.zeros_like(dga_ref)
        col = lax.broadcasted_iota(jnp.int32, (QBLK, WIN), 1)

        def block(b, carry):
            pair = range(AH)
            r0, lss, qs, kws, vws, pns = _attn_probs(b, q_ref, kpad, vpad, bias, col)
            pn_b = [pns[i].astype(BF16) for i in pair]
            o = [_dot(pn_b[i], vws[i], NN) for i in pair]
            r = [lax.rsqrt(jnp.mean(o[i] * o[i], axis=-1, keepdims=True) + EPS) for i in pair]
            dn = [dn_ref[pl.ds(r0, QBLK), lss[i]] for i in pair]
            for i in pair:
                dga_ref[i:i + 1, :] += _colsum(dn[i] * o[i] * r[i])
            a = [dn[i] * ga_ref[0:1, lss[i]] for i in pair]
            do_b = [(r[i] * (a[i] - o[i] * (r[i] * r[i]) * jnp.mean(a[i] * o[i], axis=-1, keepdims=True))).astype(BF16)
                    for i in pair]
            dp = [_dot(do_b[i], vws[i], NT) for i in pair]
            for i in pair:
                dvacc[i][pl.ds(r0, WIN), :] += _dot(pn_b[i], do_b[i], TN)
            ds = [pns[i] * (dp[i] - jnp.sum(pns[i] * dp[i], axis=-1, keepdims=True)) for i in pair]
            for i in pair:
                dbias[i][...] += ds[i]
            ds_b = [ds[i].astype(BF16) for i in pair]
            dq = [_dot(ds_b[i], kws[i], NN) * scale for i in pair]
            dq_ref[pl.ds(r0, QBLK), :] = jnp.concatenate(dq, axis=1).astype(BF16)
            for i in pair:
                dkacc[i][pl.ds(r0, WIN), :] += _dot(ds_b[i], qs[i], TN) * scale
            return carry

        lax.fori_loop(0, T // QBLK, block, 0)

        rr = lax.broadcasted_iota(jnp.int32, (QBLK, QBLK), 0)
        cc = lax.broadcasted_iota(jnp.int32, (QBLK, QBLK), 1)
        flip = (rr + cc == QBLK - 1).astype(BF16)
        for i in range(AH):
            ls = slice(i * ATTN_HEAD_DIM, (i + 1) * ATTN_HEAD_DIM)
            dk_ref[:, ls] = dkacc[i][KPAD:, :].astype(BF16)
            dv_ref[:, ls] = dvacc[i][KPAD:, :].astype(BF16)
            full = jnp.concatenate([dbias[i][...], jnp.zeros((QBLK, TAB - WIN), F32)], axis=1)
            hi = full.astype(BF16)
            lo = (full - hi.astype(F32)).astype(BF16)
            rev = _dot(flip, hi, NN) + _dot(flip, lo, NN)
            dgv_ref[i:i + 1, :] = _colsum(pltpu.roll(rev, TAB - (QBLK - 1), 1, stride=1, stride_axis=0))

    blk = lambda off: pl.BlockSpec((T, W), lambda hp: (0, off + hp))
    accs = lambda dt: [pltpu.VMEM((T + KPAD, ATTN_HEAD_DIM), dt)] * AH
    return pl.pallas_call(
        body, grid=(HP,), name="attn_bwd",
        in_specs=[blk(0), blk(HP), blk(2 * HP), blk(0), pl.BlockSpec(gv.shape, lambda hp: (0, 0)),
                  pl.BlockSpec((1, W), lambda hp: (0, hp))],
        out_specs=[blk(0), blk(0), blk(0), pl.BlockSpec((None, AH, TAB), lambda hp: (hp, 0, 0)),
                   pl.BlockSpec((None, AH, ATTN_HEAD_DIM), lambda hp: (hp, 0, 0))],
        out_shape=[_sds((T, AW), BF16), _sds((T, AW), BF16), _sds((T, AW), BF16),
                   _sds((HP, AH, TAB), F32), _sds((HP, AH, ATTN_HEAD_DIM), F32)],
        scratch_shapes=accs(BF16) + accs(BF16) + accs(F32) + accs(F32) + [pltpu.VMEM((QBLK, WIN), F32)] * (2 * AH),
        compiler_params=_params(("parallel",), big=True),
    )(proj, proj, proj, dmixin, gv, ga)


def _ltri():
    r = lax.broadcasted_iota(jnp.int32, (CHUNK, CHUNK), 0)
    c = lax.broadcasted_iota(jnp.int32, (CHUNK, CHUNK), 1)
    return (c <= r).astype(BF16)


def _tri_dot(tri, v, dims):
    hi = v.astype(BF16)
    lo = (v - hi.astype(F32)).astype(BF16)
    return _dot(tri, hi, dims) + _dot(tri, lo, dims)


HEADS_PER_STEP = (8, 2)
REC_ROW_TILE = 512


def _alternate(stages):
    live = list(stages)
    while live:
        for g in list(live):
            if next(g, StopIteration) is StopIteration:
                live.remove(g)


def _hgrn_gates(n, ls, q_ref, f_ref, lb_ref, ltri):
    r0 = pl.multiple_of(n * CHUNK, CHUNK)
    rows = pl.ds(r0, CHUNK)
    lb = lb_ref[:, ls]
    qb = q_ref[rows, ls]
    sg = _sigmoid(f_ref[rows, ls])
    f = lb + (1.0 - lb) * sg
    sq = _sigmoid(qb)
    b = _tri_dot(ltri, jnp.log(f), NN)
    return rows, lb, qb, sg, f, 1.0 - f, sq, qb * sq, b


def _hgrn_specs(T, RW, AW, backward):
    HG = HEADS_PER_STEP[1 if backward else 0]
    W = HG * LANE
    TT = _tile(T, REC_ROW_TILE, CHUNK)
    n_row_tiles = T // TT
    base = 3 * AW // W
    row = (lambda t: n_row_tiles - 1 - t) if backward else (lambda t: t)
    blk_in = lambda off: pl.BlockSpec((TT, W), lambda g, t: (row(t), base + off + g))
    col = pl.BlockSpec((TT, W), lambda g, t: (row(t), g))
    states = pl.BlockSpec((HG, TT // CHUNK, LANE, LANE), lambda g, t: (g, row(t), 0, 0))
    return HG, W, RW // W, TT, n_row_tiles, blk_in, col, states


def _hgrn_fwd(proj, lb, gn, AW, RW):
    T = proj.shape[0]
    RH, NC, NSUB = RW // LANE, T // CHUNK, CHUNK // SUB
    HG, W, NG, TT, n_row_tiles, blk_in, col, states = _hgrn_specs(T, RW, AW, False)

    def body(q_ref, f_ref, i_ref, g_ref, lb_ref, gn_ref, mix_ref, o_ref, stall_ref, st_all, bs_all, kks_all, ics_all):
        @pl.when(pl.program_id(1) == 0)
        def _():
            st_all[...] = jnp.zeros_like(st_all)

        ltri = _ltri()
        rowi = lax.broadcasted_iota(jnp.int32, (SUB, 1), 0)

        def one_head(h, n):
            ls = slice(h * LANE, (h + 1) * LANE)
            st, bs, kks, ics = st_all.at[h], bs_all.at[h], kks_all.at[h], ics_all.at[h]
            rows, _, _, _, _, kk, _, qs, b = _hgrn_gates(n, ls, q_ref, f_ref, lb_ref, ltri)
            ic = i_ref[rows, ls]
            stv = st[...]
            stall_ref[h, n] = stv
            bs[...] = b
            kks[...] = kk
            ics[...] = ic
            yield
            o = _dot((qs * jnp.exp(b)).astype(BF16), stv.astype(BF16), NT)
            yield
            ic_b = ic.astype(BF16)
            pieces = []
            for blk in range(NSUB):
                s0 = blk * SUB
                bI, qI = b[s0:s0 + SUB], qs[s0:s0 + SUB]
                if blk == 0:
                    oI = jnp.zeros((SUB, LANE), F32)
                else:
                    ref = bs[s0 - 1:s0, :]
                    qt = (qI * jnp.exp(bI - ref)).astype(BF16)
                    kt = (kk[0:s0] * jnp.exp(ref - b[0:s0])).astype(BF16)
                    oI = _dot(_dot(qt, kt, NT).astype(BF16), ic_b[0:s0], NN)
                    yield
                acc = [oI[g * ROWS:(g + 1) * ROWS] for g in range(SUB // ROWS)]
                for s in range(SUB):
                    sr = s0 + s
                    g0 = s // ROWS
                    lo = g0 * ROWS
                    e = jnp.exp(jnp.minimum(bI[lo:] - bs[sr:sr + 1, :], 0.0))
                    a = jnp.sum(qI[lo:] * kks[sr:sr + 1, :] * e, axis=-1, keepdims=True)
                    add = jnp.where(rowi[lo:] >= s, a, 0.0) * ics[sr:sr + 1, :]
                    for g in range(g0, SUB // ROWS):
                        acc[g] = acc[g] + add[(g - g0) * ROWS:(g - g0 + 1) * ROWS]
                    yield
                pieces.extend(acc)
            o = o + jnp.concatenate(pieces, axis=0)
            bl = bs[CHUNK - 1:CHUNK, :]
            kd = (kk * jnp.exp(bl - b)).astype(BF16)
            st[...] = stv * jnp.exp(bl) + _dot(ic_b, kd, TN)
            yield
            o_ref[rows, ls] = o
            r = lax.rsqrt(jnp.mean(o * o, axis=-1, keepdims=True) + EPS)
            gb = g_ref[rows, ls]
            mix_ref[rows, ls] = (o * r * gn_ref[...] * (gb * _sigmoid(gb))).astype(BF16)

        def chunk(n, carry):
            _alternate([one_head(h, n) for h in range(HG)])
            return carry

        lax.fori_loop(0, TT // CHUNK, chunk, 0)

    tile = pltpu.VMEM((HG, CHUNK, LANE), F32)
    return pl.pallas_call(
        body, grid=(NG, n_row_tiles), name="hgrn_fwd",
        in_specs=[blk_in(0), blk_in(NG), blk_in(2 * NG), blk_in(3 * NG), pl.BlockSpec((1, W), lambda g, t: (0, g)),
                  pl.BlockSpec((1, LANE), lambda g, t: (0, 0))],
        out_specs=[col, col, states],
        out_shape=[_sds((T, RW), BF16), _sds((T, RW), F32), _sds((RH, NC, LANE, LANE), F32)],
        scratch_shapes=[pltpu.VMEM((HG, LANE, LANE), F32), tile, tile, tile],
        compiler_params=_params(("parallel", "arbitrary"), big=True),
    )(proj, proj, proj, proj, lb, gn)


def _hgrn_bwd(proj, dmixin, o_b, st_all, lb, gn, AW, RW):
    T = proj.shape[0]
    RH, NC, NSUB = RW // LANE, T // CHUNK, CHUNK // SUB
    HG, W, NG, TT, n_row_tiles, blk_in, col, states = _hgrn_specs(T, RW, AW, True)

    def body(q_ref, f_ref, i_ref, g_ref, o_ref, dn_ref, stall_ref, lb_ref, gn_ref,
             dq_ref, df_ref, di_ref, dg_ref, dlb_ref, dgn_ref, dst_all, bs_all, qss_all, dos_all, p2_all, dic_all,
             p1_all):
        @pl.when(pl.program_id(1) == 0)
        def _():
            dst_all[...] = jnp.zeros_like(dst_all)
            dlb_ref[...] = jnp.zeros_like(dlb_ref)
            dgn_ref[...] = jnp.zeros_like(dgn_ref)

        ltri = _ltri()
        rowi = lax.broadcasted_iota(jnp.int32, (SUB, 1), 0)
        last = lax.broadcasted_iota(jnp.int32, (CHUNK, 1), 0) == CHUNK - 1

        def one_head(h, n):
            ls = slice(h * LANE, (h + 1) * LANE)
            dst, bs, qss, dos = dst_all.at[h], bs_all.at[h], qss_all.at[h], dos_all.at[h]
            p2, dic, p1s = p2_all.at[h], dic_all.at[h], p1_all.at[h]
            rows, lbv, qb, sg, f, kk, sq, qs, b = _hgrn_gates(n, ls, q_ref, f_ref, lb_ref, ltri)
            ic = i_ref[rows, ls]
            stv = stall_ref[h, n]
            dstv = dst[...]
            o = o_ref[rows, ls]
            dn = dn_ref[rows, ls]
            gb = g_ref[rows, ls]
            sgb = _sigmoid(gb)
            r = lax.rsqrt(jnp.mean(o * o, axis=-1, keepdims=True) + EPS)
            gnv = gn_ref[...]
            dg_ref[rows, ls] = (dn * (o * r * gnv) * (sgb * (1.0 + gb * (1.0 - sgb)))).astype(BF16)
            dy = dn * (gb * sgb)
            dgn_ref[h] += _colsum(dy * o * r)
            a_ = dy * gnv
            do = r * (a_ - o * (r * r) * jnp.mean(a_ * o, axis=-1, keepdims=True))
            do_b = do.astype(BF16)
            bs[...] = b
            qss[...] = qs
            dos[...] = do
            yield
            ic_b = ic.astype(BF16)
            eb = jnp.exp(b)
            bl = bs[CHUNK - 1:CHUNK, :]
            ebl = jnp.exp(bl)
            dec = jnp.exp(bl - b)
            kd = (kk * dec).astype(BF16)
            dst_b = dstv.astype(BF16)
            dqs = _dot(do_b, stv.astype(BF16), NN) * eb
            dkk2 = _dot(ic_b, dst_b, NN) * dec
            dic[...] = _dot(kd, dst_b, NT)
            dbl = ebl * _colsum(stv * dstv) + _colsum(kk * dkk2)
            dst[...] = dstv * ebl + _dot(do_b, (qs * eb).astype(BF16), TN)
            yield
            p2[...] = jnp.zeros_like(p2)
            p1_pieces = []
            for blk in range(NSUB):
                s0 = blk * SUB
                bI, qI, doI = b[s0:s0 + SUB], qs[s0:s0 + SUB], do[s0:s0 + SUB]
                if blk == 0:
                    p1 = jnp.zeros((SUB, LANE), F32)
                else:
                    ref = bs[s0 - 1:s0, :]
                    eq = jnp.exp(bI - ref)
                    ek = jnp.exp(ref - b[0:s0])
                    qt = (qI * eq).astype(BF16)
                    kt = (kk[0:s0] * ek).astype(BF16)
                    doI_b = doI.astype(BF16)
                    dic[0:s0, :] += _dot(_dot(qt, kt, NT).astype(BF16), doI_b, TN)
                    da = _dot(doI_b, ic_b[0:s0], NT).astype(BF16)
                    p1 = _dot(da, kt, NN) * eq
                    p2[0:s0, :] += _dot(da, qt, TN) * ek
                    yield
                p1_pieces.append(p1)
                kkI, icI = kk[s0:s0 + SUB], ic[s0:s0 + SUB]
                p2acc = [jnp.zeros((ROWS, LANE), F32) for _ in range(SUB // ROWS)]
                diacc = [jnp.zeros((ROWS, LANE), F32) for _ in range(SUB // ROWS)]
                for t in range(SUB):
                    tr = s0 + t
                    ng = t // ROWS + 1
                    hi = ng * ROWS
                    keep = rowi[:hi] <= t
                    do_t = dos[tr:tr + 1, :]
                    e = jnp.exp(jnp.minimum(bs[tr:tr + 1, :] - bI[:hi], 0.0))
                    qe = qss[tr:tr + 1, :] * e
                    a = jnp.where(keep, jnp.sum(kkI[:hi] * qe, axis=-1, keepdims=True), 0.0)
                    da = jnp.where(keep, jnp.sum(icI[:hi] * do_t, axis=-1, keepdims=True), 0.0)
                    dp2, ddi = da * qe, a * do_t
                    for g in range(ng):
                        p2acc[g] = p2acc[g] + dp2[g * ROWS:(g + 1) * ROWS]
                        diacc[g] = diacc[g] + ddi[g * ROWS:(g + 1) * ROWS]
                    p1s[tr:tr + 1, :] = _colsum(da * kkI[:hi] * e)
                    yield
                p2[s0:s0 + SUB, :] += jnp.concatenate(p2acc, axis=0)
                dic[s0:s0 + SUB, :] += jnp.concatenate(diacc, axis=0)
            dqs = dqs + jnp.concatenate(p1_pieces, axis=0) + p1s[...]
            dkk = dkk2 + p2[...]
            db = qs * dqs - kk * dkk + jnp.where(last, dbl, 0.0)
            dgl = _tri_dot(ltri, db, TN)
            yield
            dfv = dgl / f - dkk
            df_ref[rows, ls] = (dfv * (1.0 - lbv) * sg * (1.0 - sg)).astype(BF16)
            dlb_ref[:, ls] += _colsum(dfv * (1.0 - sg))
            dq_ref[rows, ls] = (dqs * (sq * (1.0 + qb * (1.0 - sq)))).astype(BF16)
            di_ref[rows, ls] = dic[...].astype(BF16)

        def chunk(k, carry):
            _alternate([one_head(h, TT // CHUNK - 1 - k) for h in range(HG)])
            return carry

        lax.fori_loop(0, TT // CHUNK, chunk, 0)

    tile = pltpu.VMEM((HG, CHUNK, LANE), F32)
    return pl.pallas_call(
        body, grid=(NG, n_row_tiles), name="hgrn_bwd",
        in_specs=[blk_in(0), blk_in(NG), blk_in(2 * NG), blk_in(3 * NG), col,
                  pl.BlockSpec((TT, W), lambda g, t: (n_row_tiles - 1 - t, AW // W + g)), states,
                  pl.BlockSpec((1, W), lambda g, t: (0, g)), pl.BlockSpec((1, LANE), lambda g, t: (0, 0))],
        out_specs=[col, col, col, col, pl.BlockSpec((1, W), lambda g, t: (0, g)),
                   pl.BlockSpec((HG, 1, LANE), lambda g, t: (g, 0, 0))],
        out_shape=[_sds((T, RW), BF16)] * 4 + [_sds((1, RW), F32), _sds((RH, 1, LANE), F32)],
        scratch_shapes=[pltpu.VMEM((HG, LANE, LANE), F32), tile, tile, tile, tile, tile, tile],
        compiler_params=_params(("parallel", "arbitrary"), big=True),
    )(proj, proj, proj, proj, o_b, dmixin, st_all, lb, gn)


def _prep(c, lb_logits, rb_pad, max_rel, after):
    D, RW = c.shape[-1], lb_logits.shape[-1]
    H, rbp = rb_pad.shape

    def body(c_ref, l_ref, rb_ref, _, __, cact_ref, lb_ref, gv_ref):
        cv = c_ref[...]
        cact_ref[...] = cv * _sigmoid(cv)
        lb_ref[...] = _sigmoid(l_ref[0:1, :] - l_ref[1:2, :])
        gv_ref[...] = _dot(rb_ref[...], _bias_onehot(rbp, max_rel), NN, HIGHEST)

    vmem = pl.BlockSpec(memory_space=pltpu.VMEM)
    return pl.pallas_call(
        body, name="prep", in_specs=[vmem, vmem, vmem, ORDER_ONLY, ORDER_ONLY],
        out_shape=[_sds((1, D), F32), _sds((1, RW), F32), _sds((H, TAB), F32)],
    )(c, lb_logits, rb_pad, *after)


def _mod_part(c_all, w_ada_s, b_ada_s):
    B, D = c_all.shape
    ns = w_ada_s.shape[1]
    tn = _tile(ns, 768, LANE)

    def body(c_ref, w_ref, b_ref, o_ref):
        o_ref[...] = _dot(c_ref[...], w_ref[...], NN) + b_ref[...]

    return pl.pallas_call(
        body, grid=(ns // tn,), name="mod_part",
        in_specs=[pl.BlockSpec((B, D), lambda j: (0, 0)), pl.BlockSpec((D, tn), lambda j: (0, j)),
                  pl.BlockSpec((1, tn), lambda j: (0, j))],
        out_specs=pl.BlockSpec((B, tn), lambda j: (0, j)),
        out_shape=_sds((B, ns), F32), compiler_params=_params(("parallel",)),
    )(c_all, w_ada_s, b_ada_s)


def _adam(w, g, m, v):
    m = ADAM_B1 * m + (1.0 - ADAM_B1) * g
    v = ADAM_B2 * v + (1.0 - ADAM_B2) * (g * g)
    m_hat = m * (1.0 / (1.0 - ADAM_B1 ** ADAM_STEP))
    v_hat = v * (1.0 / (1.0 - ADAM_B2 ** ADAM_STEP))
    return -ADAM_LR * (m_hat / (jnp.sqrt(v_hat) + ADAM_EPS) + ADAM_WD * w), m, v


def _adam_ada(c_all, dmod_s, w, m, v):
    B, D = c_all.shape
    ns = w.shape[1]
    tr, tn = _tile(D, 512, LANE), _tile(ns, 768, LANE)

    def body(c_ref, d_ref, w_ref, m_ref, v_ref, g_out, dw_out, m_out, v_out):
        g = _dot(c_ref[...], d_ref[...], TN)
        g_out[...] = g
        dw_out[...], m_out[...], v_out[...] = _adam(w_ref[...], g, m_ref[...], v_ref[...])

    big = pl.BlockSpec((tr, tn), lambda i, j: (i, j))
    return pl.pallas_call(
        body, grid=(D // tr, ns // tn), name="adam_w_ada",
        in_specs=[pl.BlockSpec((B, tr), lambda i, j: (0, i)), pl.BlockSpec((B, tn), lambda i, j: (0, j)),
                  big, big, big],
        out_specs=[big] * 4, out_shape=[_sds((D, ns), F32)] * 4,
        compiler_params=_params(("parallel", "parallel")),
    )(c_all, dmod_s, w, m, v)


def _adam_shard(parts, w, m, v, name):
    R, C = w.shape
    tr = _tile(R, 256, 16)

    def body(p_ref, w_ref, m_ref, v_ref, g_out, dw_out, m_out, v_out):
        g = p_ref[0].astype(F32)
        for k in range(1, N_DEV // 2):
            g = g + p_ref[k].astype(F32)
        g_out[...] = g
        dw_out[...], m_out[...], v_out[...] = _adam(w_ref[...], g, m_ref[...], v_ref[...])

    big = pl.BlockSpec((tr, C), lambda i: (i, 0))
    return pl.pallas_call(
        body, grid=(R // tr,), name=name,
        in_specs=[pl.BlockSpec((N_DEV // 2, tr, C), lambda i: (0, i, 0)), big, big, big],
        out_specs=[big] * 4, out_shape=[_sds((R, C), F32)] * 4,
        compiler_params=_params(("parallel",), big=True),
    )(parts, w, m, v)


def _pair_sum(g8, land, core, name):
    _, NCHIP, R, C = g8.shape
    tr = _tile(R, 1024, 16)

    def body(core_ref, g_ref, l_ref, o_ref):
        o_ref[...] = g_ref[...] + l_ref[...]

    return pl.pallas_call(
        body, name=name,
        grid_spec=pltpu.PrefetchScalarGridSpec(
            num_scalar_prefetch=1, grid=(NCHIP, R // tr),
            in_specs=[pl.BlockSpec((None, None, tr, C), lambda k, i, core_ref: (core_ref[0], k, i, 0)),
                      pl.BlockSpec((None, tr, C), lambda k, i, core_ref: (k, i, 0))],
            out_specs=pl.BlockSpec((None, tr, C), lambda k, i, core_ref: (k, i, 0))),
        out_shape=_sds((NCHIP, R, C), BF16), compiler_params=_params(("parallel", "parallel")),
    )(core, g8, land)


SMALL = ("b_ada", "rel_bias", "attn_norm_g", "lb_logits", "gnorm_g", "ln1_g", "ln1_b", "ln2_g", "ln2_b")


def _small_update(parts, loss_parts, lbv, ws, ms, vs, max_rel):
    n = len(SMALL)

    def body(*refs):
        part_refs = dict(zip(SMALL, refs[:n]))
        loss_in, lb_ref = refs[n], refs[n + 1]
        w_refs, m_refs, v_refs = refs[n + 2:2 * n + 2], refs[2 * n + 2:3 * n + 2], refs[3 * n + 2:4 * n + 2]
        outs = refs[4 * n + 2:]

        def total(ref):
            tot = ref[0]
            for k in range(1, N_DEV):
                tot = tot + ref[k]
            return tot

        outs[0][...] = jnp.sum(total(loss_in), axis=-1, keepdims=True)
        for idx, name in enumerate(SMALL):
            g = total(part_refs[name])
            if name == "rel_bias":
                g = _dot(g, _bias_onehot(w_refs[idx].shape[1], max_rel), NT, HIGHEST)
            elif name == "lb_logits":
                lb = lb_ref[...]
                sign = (1 - 2 * lax.broadcasted_iota(jnp.int32, (2, 1), 0)).astype(F32)
                g = sign * (g * lb * (1.0 - lb))
            elif name == "gnorm_g":
                g = _colsum(g)
            dw, mm, vv = _adam(w_refs[idx][...], g, m_refs[idx][...], v_refs[idx][...])
            outs[1 + 4 * idx][...] = g
            outs[2 + 4 * idx][...] = dw
            outs[3 + 4 * idx][...] = mm
            outs[4 + 4 * idx][...] = vv

    out_shape = [_sds((1, 1), F32)]
    for w in ws:
        out_shape += [_sds(w.shape, F32)] * 4
    return pl.pallas_call(body, name="small_update", out_shape=out_shape, compiler_params=_params(big=True))(
        *[parts[k] for k in SMALL], loss_parts, lbv, *ws, *ms, *vs)


def _place():
    x, y, c = lax.axis_index("x"), lax.axis_index("y"), lax.axis_index("c")
    return x, y, c, [(1 - x, y), (x, 1 - y), (1 - x, 1 - y)]


def _all_gather(shard, name):
    HBM = pl.BlockSpec(memory_space=pl.ANY)

    def body(x_ref, out_ref, send_sems, recv_sems, local_sem):
        x, y, c, chips = _place()
        me, sibling = (x, y, c), (x, y, 1 - c)

        def slot(px, py, pc):
            return out_ref.at[4 * px + 2 * py + pc]

        def copy(k, block, to, src=None):
            return pltpu.make_async_remote_copy(
                src_ref=slot(*block) if src is None else src, dst_ref=slot(*block),
                send_sem=send_sems.at[k], recv_sem=recv_sems.at[k], device_id=to, device_id_type=MESH)

        mine = pltpu.make_async_copy(x_ref, slot(*me), local_sem)
        mine.start()
        first = [copy(0, me, sibling, src=x_ref)]
        first += [copy(1 + j, me, (*chip, c), src=x_ref) for j, chip in enumerate(chips)]
        for cp in first:
            cp.start()
        passed = [copy(4 + j, (*chip, c), sibling) for j, chip in enumerate(chips)]
        for j, chip in enumerate(chips):
            copy(1 + j, (*chip, c), me).wait_recv()
            passed[j].start()
        copy(0, sibling, me).wait_recv()
        for j, chip in enumerate(chips):
            copy(4 + j, (*chip, 1 - c), me).wait_recv()
        for cp in first + passed:
            cp.wait_send()
        mine.wait()

    return pl.pallas_call(
        body, name=name, out_shape=_sds((N_DEV,) + shard.shape, shard.dtype),
        in_specs=[HBM], out_specs=HBM,
        scratch_shapes=[pltpu.SemaphoreType.DMA((7,)), pltpu.SemaphoreType.DMA((7,)), pltpu.SemaphoreType.DMA(())],
    )(shard)


SEM_SPEC = pl.BlockSpec(memory_space=pltpu.SEMAPHORE)
HBM_SPEC = pl.BlockSpec(memory_space=pltpu.HBM)
EFFECT = pltpu.SideEffectType.DATAFLOW_SIDE_EFFECTING


def _remote(src, dst, send_sems, recv_sems, k, dev):
    return pltpu.make_async_remote_copy(src_ref=src, dst_ref=dst, send_sem=send_sems.at[k], recv_sem=recv_sems.at[k],
                                        device_id=dev, device_id_type=MESH)


def _copy_start(name, bufs, plan, n, after, only=None):
    nb = len(bufs)

    def body(*refs):
        send_sems, recv_sems = refs[nb + 1], refs[nb + 2]
        for k, (src, dst, dev) in enumerate(plan(*refs[:nb])):
            if only is not None and k not in only:
                continue
            _remote(src, dst, send_sems, recv_sems, k, dev).start()
        refs[-1][...] = jnp.zeros_like(refs[-1])

    out = pl.pallas_call(
        body, name=name,
        out_shape=(pltpu.SemaphoreType.DMA((n,)), pltpu.SemaphoreType.DMA((n,)),
                   *[pltpu.HBM(b.shape, b.dtype) for b in bufs], _sds((8, LANE), F32)),
        in_specs=[HBM_SPEC] * nb + [ORDER_ONLY],
        out_specs=(SEM_SPEC, SEM_SPEC, *[HBM_SPEC] * nb, pl.BlockSpec(memory_space=pltpu.VMEM)),
        input_output_aliases={i: 2 + i for i in range(nb)},
        compiler_params=pltpu.CompilerParams(has_side_effects=EFFECT),
    )(*[pltpu.with_memory_space_constraint(b, pltpu.HBM) for b in bufs], after)
    return (out[0], out[1]), list(out[2:2 + nb]), out[-1]


def _copy_wait(name, sems, bufs, plan, after, only=None):
    nb = len(bufs)

    def body(*refs):
        send_sems, recv_sems = refs[nb], refs[nb + 1]
        for k, (src, dst, dev) in enumerate(plan(*refs[:nb])):
            if only is not None and k not in only:
                continue
            cp = _remote(src, dst, send_sems, recv_sems, k, dev)
            cp.wait_send()
            cp.wait_recv()

    out = pl.pallas_call(
        body, name=name, out_shape=tuple(pltpu.HBM(b.shape, b.dtype) for b in bufs),
        in_specs=[HBM_SPEC] * nb + [SEM_SPEC, SEM_SPEC, pl.BlockSpec(memory_space=pl.ANY)],
        out_specs=tuple([HBM_SPEC] * nb), input_output_aliases={i: i for i in range(nb)},
        compiler_params=pltpu.CompilerParams(has_side_effects=EFFECT),
    )(*bufs, sems[0], sems[1], after)
    return list(out)


def _ag_plan_chips(shard_ref, out_ref):
    x, y, c, chips = _place()
    mine = out_ref.at[4 * x + 2 * y + c]
    return [(shard_ref, mine, (x, y, 1 - c))] + [(shard_ref, mine, (*chip, c)) for chip in chips]


def _ag_plan_pass(out_ref):
    x, y, c, chips = _place()
    slots = [out_ref.at[4 * chip[0] + 2 * chip[1] + c] for chip in chips]
    return [(s, s, (x, y, 1 - c)) for s in slots]


def _rs_plan_pair(g_ref, land_ref):
    x, y, c, _ = _place()
    return [(g_ref.at[1 - c], land_ref, (x, y, 1 - c))]


def _rs_plan_chips(p_ref, land_ref):
    x, y, c, chips = _place()
    return [(p_ref.at[2 * chip[0] + chip[1]], land_ref.at[2 * x + y], (*chip, c)) for chip in chips]


class _Gather:
    @staticmethod
    def landing(shard, me):
        return lax.dynamic_update_slice(lax.empty((N_DEV,) + shard.shape, shard.dtype), shard[None],
                                        (me,) + (0,) * shard.ndim)

    def __init__(self, shard, out, tag, after):
        self.tag = tag
        self.sems, (self.shard, self.out), self.token = _copy_start(
            "ag_start_" + tag, [shard, out], _ag_plan_chips, 4, after)
        self.groups = []

    def arrived(self, after, copies):
        name = "ag_wait_%s_%s" % (self.tag, "".join(map(str, copies)))
        self.shard, self.out = _copy_wait(name, self.sems, [self.shard, self.out], _ag_plan_chips, after, copies)
        return self.out

    def pass_on(self, after, blocks):
        name = "ag_pass_%s_%s" % (self.tag, "".join(map(str, blocks)))
        sems, (self.out,), _ = _copy_start(name, [self.out], _ag_plan_pass, 3, after, blocks)
        self.groups.append((sems, blocks))
        return self.out

    def passed(self, after, group):
        sems, blocks = self.groups[group]
        name = "ag_pass_wait_%s_%s" % (self.tag, "".join(map(str, blocks)))
        self.out = _copy_wait(name, sems, [self.out], _ag_plan_pass, after, blocks)[0]
        return self.out

    def arrived_from_chips(self, after):
        self.arrived(after, (0, 1, 2, 3))
        return self.pass_on(after, (0, 1, 2))

    def passed_on(self, after):
        return self.passed(after, 0)


def _ag_plan_direct(src_ref, out_ref):
    x, y, c, chips = _place()
    mine = out_ref.at[4 * x + 2 * y + c]
    peers = [(x, y, 1 - c)] + [(*chip, pc) for chip in chips for pc in (c, 1 - c)]
    return [(src_ref, mine, peer) for peer in peers]


class _SmallGather:
    def __init__(self, block, me, tag):
        self.tag = tag
        out = lax.dynamic_update_slice(lax.empty((N_DEV,) + block.shape, block.dtype), block[None],
                                       (me,) + (0,) * block.ndim)
        self.sems, self.bufs, self.token = _copy_start(
            "ag_direct_start_" + tag, [block, out], _ag_plan_direct, N_DEV - 1, jnp.zeros((1,), F32))

    def done(self, after):
        return _copy_wait("ag_direct_wait_" + self.tag, self.sems, self.bufs, _ag_plan_direct, after)[1]


class _ReduceScatter:
    def __init__(self, g8, tag):
        self.tag = tag
        land = lax.empty(g8.shape[1:], g8.dtype)
        self.sems, self.bufs, self.token = _copy_start(
            "rs_pair_start_" + tag, [g8, land], _rs_plan_pair, 1, jnp.zeros((1,), F32))

    def pair_done(self, core, chip, after):
        g8, land = _copy_wait("rs_pair_wait_" + self.tag, self.sems, self.bufs, _rs_plan_pair, after)
        p4 = _pair_sum(g8, land, core, "rs_pair_sum_" + self.tag)
        own = lax.dynamic_slice_in_dim(p4, chip, 1, axis=0)
        land2 = lax.dynamic_update_slice(lax.empty(p4.shape, p4.dtype), own, (chip, 0, 0))
        self.sems, self.bufs, self.token = _copy_start(
            "rs_chips_start_" + self.tag, [p4, land2], _rs_plan_chips, 3, jnp.zeros((1,), F32))

    def sums(self, after):
        return _copy_wait("rs_chips_wait_" + self.tag, self.sems, self.bufs, _rs_plan_chips, after)[1]


ORDER = ("w_ada", "b_ada", "w_in", "rel_bias", "attn_norm_g", "lb_logits", "gnorm_g", "w_o", "ln1_g", "ln1_b",
         "w_ffn_in", "w_ffn_out", "ln2_g", "ln2_b")


def kernel(x, c, w_ada, b_ada, w_in, rel_bias, attn_norm_g, lb_logits, gnorm_g, w_o, ln1_g, ln1_b, w_ffn_in, w_ffn_out, ln2_g, ln2_b, loss_target, m_w_ada, m_b_ada, m_w_in, m_rel_bias, m_attn_norm_g, m_lb_logits, m_gnorm_g, m_w_o, m_ln1_g, m_ln1_b, m_w_ffn_in, m_w_ffn_out, m_ln2_g, m_ln2_b, v_w_ada, v_b_ada, v_w_in, v_rel_bias, v_attn_norm_g, v_lb_logits, v_gnorm_g, v_w_o, v_ln1_g, v_ln1_b, v_w_ffn_in, v_w_ffn_out, v_ln2_g, v_ln2_b):
    W = dict(w_ada=w_ada, b_ada=b_ada, w_in=w_in, rel_bias=rel_bias, attn_norm_g=attn_norm_g, lb_logits=lb_logits,
             gnorm_g=gnorm_g, w_o=w_o, ln1_g=ln1_g, ln1_b=ln1_b, w_ffn_in=w_ffn_in, w_ffn_out=w_ffn_out,
             ln2_g=ln2_g, ln2_b=ln2_b)
    M = dict(w_ada=m_w_ada, b_ada=m_b_ada, w_in=m_w_in, rel_bias=m_rel_bias, attn_norm_g=m_attn_norm_g,
             lb_logits=m_lb_logits, gnorm_g=m_gnorm_g, w_o=m_w_o, ln1_g=m_ln1_g, ln1_b=m_ln1_b,
             w_ffn_in=m_w_ffn_in, w_ffn_out=m_w_ffn_out, ln2_g=m_ln2_g, ln2_b=m_ln2_b)
    V = dict(w_ada=v_w_ada, b_ada=v_b_ada, w_in=v_w_in, rel_bias=v_rel_bias, attn_norm_g=v_attn_norm_g,
             lb_logits=v_lb_logits, gnorm_g=v_gnorm_g, w_o=v_w_o, ln1_g=v_ln1_g, ln1_b=v_ln1_b,
             w_ffn_in=v_w_ffn_in, w_ffn_out=v_w_ffn_out, ln2_g=v_ln2_g, ln2_b=v_ln2_b)

    x2, tgt = x[0], loss_target[0]
    T, D = x2.shape
    AW, RW = attn_norm_g.shape[-1], lb_logits.shape[-1]
    MIX = AW + RW
    H, RH = AW // ATTN_HEAD_DIM, RW // LANE
    RB = rel_bias.shape[-1]
    max_rel = (RB - 1) // 2
    rbp = -(-RB // LANE) * LANE
    F = w_ffn_out.shape[1] * N_DEV
    half = N_DEV // 2
    xi, yi, ci = lax.axis_index("x"), lax.axis_index("y"), lax.axis_index("c")
    me = 4 * xi + 2 * yi + ci
    core = jnp.reshape(ci, (1,)).astype(jnp.int32)
    pad_rb = lambda a: jnp.pad(a[0], ((0, 0), (0, rbp - RB)))

    chip = 2 * xi + yi

    w_in_b = w_in[0].astype(BF16)
    w_in_land = _Gather.landing(w_in_b, me)
    c_act, lbv, gv = _prep(c, lb_logits, pad_rb(rel_bias), max_rel, (w_in_b, w_in_land))
    c_all = _all_gather(c_act, "ag_c").reshape(N_DEV, D)
    ns_ada = w_ada.shape[-1]
    mod_part = _mod_part(c_all, w_ada[0], lax.dynamic_slice_in_dim(b_ada, me * ns_ada, ns_ada, axis=1))
    mod_all = _all_gather(mod_part, "ag_mod")
    mod6 = lax.dynamic_index_in_dim(mod_all, me, axis=1, keepdims=False).reshape(6, D)

    bf = lambda w: w[0].astype(BF16)
    ag_in = _Gather(w_in_b, w_in_land, "w_in", mod_all)
    ag_o = _Gather(bf(w_o), _Gather.landing(bf(w_o), me), "w_o", ag_in.token)
    ag_f1 = _Gather(bf(w_ffn_in), _Gather.landing(bf(w_ffn_in), me), "w_ffn_in", ag_o.token)
    ag_f2 = _Gather(bf(w_ffn_out), _Gather.landing(bf(w_ffn_out), me), "w_ffn_out", ag_f1.token)

    h1 = _ln_mod(x2, mod6 + ag_f2.token[0, 0])
    ids = lambda pairs: jnp.stack([4 * px + 2 * py + pc for px, py, pc in pairs]).astype(jnp.int32)
    others = [(1 - xi, yi), (xi, 1 - yi), (1 - xi, 1 - yi)]
    proj = lax.empty((T, w_in.shape[-1] * N_DEV), F32)
    proj = _mm_gathered(h1, ag_in.arrived(h1, (0,)), ids([(xi, yi, ci), (xi, yi, 1 - ci)]), proj, "in_proj_a")
    ag_in.arrived(proj, (1, 2, 3))
    proj = _mm_gathered(h1, ag_in.pass_on(proj, (0, 1, 2)), ids([(*ch, ci) for ch in others]), proj, "in_proj_b")
    wg_in = ag_in.passed(proj, 0)
    proj = _mm_gathered(h1, wg_in, ids([(*ch, 1 - ci) for ch in others]), proj, "in_proj_c")
    ag_o.arrived_from_chips(proj)
    mix_a = _attn_fwd(proj, gv, attn_norm_g, AW)
    wg_o = ag_o.passed_on(mix_a).reshape(MIX, D)
    mix_b, o_b, st_all = _hgrn_fwd(proj, lbv, gnorm_g, AW, RW)
    mixin = jnp.concatenate([mix_a, mix_b], axis=1)
    mix = _mm_nn(mixin, wg_o, "out_proj")
    ag_f1.arrived_from_chips(mix)
    x1, h2 = _mid_fwd(x2, mix, mod6, ln1_g, ln1_b)
    wg_f1 = ag_f1.passed_on(h2)
    gu, act = _mm_swiglu(h2, wg_f1)
    ag_f2.arrived_from_chips(act)
    wg_f2 = ag_f2.passed_on(act).reshape(F, D)
    ff = _mm_nn(act, wg_f2, "ffn_out")
    dff, dx1a, vec_a = _final(x1, ff, mod6, ln2_g, ln2_b, tgt)

    du = _mm_swiglu_bwd(dff, wg_f2, gu)
    rs_f2 = _ReduceScatter(_mm_tn_rows(dff, act, dff, F // N_DEV, "grad_w_ffn_out"), "w_ffn_out")
    tm = _tile(T, 512, 16)
    du_ij = lambda tm_, w, first: pl.BlockSpec((None, tm_, w), lambda i, p: (p // (half // 2), i + first, p % (half // 2)))
    du_j = lambda rows, ns: pl.BlockSpec((None, rows, ns), lambda j: (j // half, 0, j % half))
    dh2 = _mm_gathered_nt(rs_f2.token, du, du_ij, wg_f1, T, tm, "ffn_in_bwd")
    rs_f2.pair_done(core, chip, dh2)
    gw_f1 = _mm_tn_gathered(rs_f2.token, h2, du, du_j, wg_f1.shape[-1], "grad_w_ffn_in")
    rs_f1 = _ReduceScatter(gw_f1.reshape(2, half, D, -1), "w_ffn_in")
    dmix, dxa, vec_b = _mid_bwd(x2, mix, x1, dx1a, dh2, mod6 + rs_f1.token[0, 0], ln1_g)
    dmixin = _mm_nt(dmix, wg_o, "out_proj_bwd")
    rs_f1.pair_done(core, chip, dmixin)
    rs_o = _ReduceScatter(_mm_tn_rows(rs_f1.token, mixin, dmix, MIX // N_DEV, "grad_w_o"), "w_o")
    dq, dk, dv, dgv, dga = _attn_bwd(proj, dmixin, gv + rs_o.token[0, 0], attn_norm_g, AW)
    rs_o.pair_done(core, chip, dq)
    dqb, dfl, dib, dgb, dlb, dgn = _hgrn_bwd(proj, dmixin, o_b, st_all, lbv + rs_o.token[0, 0], gnorm_g, AW, RW)
    dproj = jnp.concatenate([dq, dk, dv, dqb, dfl, dib, dgb], axis=1)
    p_ij = lambda tm_, w, first: pl.BlockSpec((tm_, w), lambda i, p: (i + first, p))
    p_j = lambda rows, ns: pl.BlockSpec((rows, ns), lambda j: (0, j))
    gw_in = _mm_tn_gathered(rs_o.token, h1, dproj, p_j, wg_in.shape[-1], "grad_w_in")
    rs_in = _ReduceScatter(gw_in.reshape(2, half, D, -1), "w_in")
    n_tiles = T // tm
    n_first = max(1, n_tiles // 4)
    dh1 = _mm_gathered_nt(rs_in.token, dproj, p_ij, wg_in, T, tm, "in_proj_bwd_a", 0, n_first)
    rs_in.pair_done(core, chip, dh1)
    dh1 = _mm_gathered_nt(rs_in.token, dproj, p_ij, wg_in, T, tm, "in_proj_bwd_b", n_first, n_tiles - n_first, dh1)
    grad_x, vec_c = _first_bwd(x2, dh1, dxa, mod6)

    dmod = jnp.concatenate([vec_c[1:2], vec_c[0:1], vec_b[4:5], vec_b[1:2], vec_b[0:1], vec_a[2:3]], axis=0)
    pieces = dict(b_ada=dmod, rel_bias=dgv, attn_norm_g=dga, lb_logits=dlb, gnorm_g=dgn, ln1_g=vec_b[2:3],
                  ln1_b=vec_b[3:4], ln2_g=vec_a[0:1], ln2_b=vec_a[1:2], loss=vec_a[3:4])
    widths = dict(b_ada=(1, 6 * D), rel_bias=(H, TAB), attn_norm_g=(1, AW), lb_logits=(1, RW), gnorm_g=(RH, LANE),
                  ln1_g=(1, D), ln1_b=(1, D), ln2_g=(1, D), ln2_b=(1, D), loss=(1, D))
    packed = jnp.concatenate([pieces[k].reshape(-1, LANE) for k in widths], axis=0)
    small_ag = _SmallGather(packed, me, "small")
    after, res_big = small_ag.token, {}
    for k, rs in (("w_ffn_out", rs_f2), ("w_ffn_in", rs_f1), ("w_o", rs_o), ("w_in", rs_in)):
        four = _adam_shard(rs.sums(after), W[k][0], M[k][0], V[k][0], "adam_" + k)
        res_big[k] = [a[None] for a in four]
        after = four[0]
    gathered = small_ag.done(after)
    parts, r0 = {}, 0
    for k, (rows, width) in widths.items():
        nr = rows * width // LANE
        parts[k] = gathered[:, r0:r0 + nr, :].reshape(N_DEV, rows, width)
        r0 += nr
    prep_small = lambda d, k: pad_rb(d[k]) if k == "rel_bias" else d[k]
    small = _small_update(parts, parts["loss"], lbv, [prep_small(W, k) for k in SMALL],
                          [prep_small(M, k) for k in SMALL], [prep_small(V, k) for k in SMALL], max_rel)
    loss = small[0].reshape(())
    res = {}
    for idx, k in enumerate(SMALL):
        four = small[1 + 4 * idx:5 + 4 * idx]
        if k == "rel_bias":
            four = [a[:, :RB][None] for a in four]
        res[k] = list(four)

    res.update(res_big)
    dmod_s = lax.dynamic_slice_in_dim(parts["b_ada"].reshape(N_DEV, 6 * D), me * ns_ada, ns_ada, axis=1)
    res["w_ada"] = [a[None] for a in _adam_ada(c_all, dmod_s, w_ada[0], m_w_ada[0], v_w_ada[0])]

    out = [loss, grad_x[None]]
    for field in range(4):
        out += [res[k][field] for k in ORDER]
    return tuple(out)
```

```python
import jax
import jax.numpy as jnp
from jax import lax
from jax.experimental import pallas as pl
from jax.experimental.pallas import tpu as pltpu

F32 = jnp.float32
BF16 = jnp.bfloat16
MESH = pl.DeviceIdType.MESH
HIGHEST = lax.Precision.HIGHEST

N_DEV = 8
CHUNK = 64
N_PAST = 8
QBLK = 4 * CHUNK
KPAD = N_PAST * CHUNK
WIN = KPAD + QBLK
TAB = 1024
ATTN_HEAD_DIM = 64
ATTN_HEADS_PER_STEP = 4
SUB = 32
ROWS = 8
LANE = 128
EPS = 1e-5
ALPHA = 2.0 ** 0.25
ADAM_LR, ADAM_B1, ADAM_B2, ADAM_EPS, ADAM_WD, ADAM_STEP = 0.001, 0.9, 0.999, 1e-08, 0.01, 10
NEG = -1e30
VMEM_LIMIT = 56 * 1024 * 1024


def _sds(shape, dtype):
    return jax.ShapeDtypeStruct(tuple(shape), dtype)


def _tile(n, pref, mult):
    best = None
    for t in range(mult, min(n, pref) + 1, mult):
        if n % t == 0:
            best = t
    return n if best is None else best


def _params(sem=None, big=False):
    kw = {}
    if sem is not None:
        kw["dimension_semantics"] = sem
    if big:
        kw["vmem_limit_bytes"] = VMEM_LIMIT
    return pltpu.CompilerParams(**kw)


def _sigmoid(v):
    return 1.0 / (1.0 + jnp.exp(-v))


def _dot(a, b, dims, precision=None):
    return lax.dot_general(a, b, (dims, ((), ())), preferred_element_type=F32, precision=precision)


NN = ((1,), (0,))
NT = ((1,), (1,))
TN = ((0,), (0,))


def _ln(v):
    mu = jnp.mean(v, axis=-1, keepdims=True)
    d = v - mu
    rstd = lax.rsqrt(jnp.mean(d * d, axis=-1, keepdims=True) + EPS)
    return d * rstd, rstd


def _ln_bwd(dxh, xh, rstd):
    return rstd * (dxh - jnp.mean(dxh, axis=-1, keepdims=True) - xh * jnp.mean(dxh * xh, axis=-1, keepdims=True))


def _colsum(v):
    return jnp.sum(v, axis=0, keepdims=True)


def _ln_mod(x2, mod6):
    T, D = x2.shape
    tm = _tile(T, 256, 8)

    def body(x_ref, mod_ref, o_ref):
        xh, _ = _ln(x_ref[...])
        o_ref[...] = (xh * (1.0 + mod_ref[1:2, :]) + mod_ref[0:1, :]).astype(BF16)

    return pl.pallas_call(
        body, grid=(T // tm,), name="ln_mod",
        in_specs=[pl.BlockSpec((tm, D), lambda i: (i, 0)), pl.BlockSpec((6, D), lambda i: (0, 0))],
        out_specs=pl.BlockSpec((tm, D), lambda i: (i, 0)),
        out_shape=_sds((T, D), BF16), compiler_params=_params(("parallel",)),
    )(x2, mod6)


def _mid_fwd(x2, mix, mod6, ln1_g, ln1_b):
    T, D = x2.shape
    tm = _tile(T, 256, 8)

    def body(x_ref, mix_ref, mod_ref, g_ref, b_ref, x1_ref, h2_ref):
        zh, _ = _ln(ALPHA * x_ref[...] + mod_ref[2:3, :] * mix_ref[...])
        x1 = zh * g_ref[...] + b_ref[...]
        x1_ref[...] = x1
        xh, _ = _ln(x1)
        h2_ref[...] = (xh * (1.0 + mod_ref[4:5, :]) + mod_ref[3:4, :]).astype(BF16)

    row = pl.BlockSpec((tm, D), lambda i: (i, 0))
    vec = pl.BlockSpec((1, D), lambda i: (0, 0))
    return pl.pallas_call(
        body, grid=(T // tm,), name="mid_fwd",
        in_specs=[row, row, pl.BlockSpec((6, D), lambda i: (0, 0)), vec, vec],
        out_specs=[row, row],
        out_shape=[_sds((T, D), F32), _sds((T, D), BF16)], compiler_params=_params(("parallel",)),
    )(x2, mix, mod6, ln1_g, ln1_b)


def _final(x1, ff, mod6, ln2_g, ln2_b, tgt):
    T, D = x1.shape
    tm = _tile(T, 256, 8)

    def body(x1_ref, ff_ref, mod_ref, g_ref, b_ref, t_ref, dff_ref, dx1_ref, vec_ref):
        @pl.when(pl.program_id(0) == 0)
        def _():
            vec_ref[...] = jnp.zeros_like(vec_ref)

        ff_v = ff_ref[...]
        gate2 = mod_ref[5:6, :]
        zh, rstd = _ln(ALPHA * x1_ref[...] + gate2 * ff_v)
        err = zh * g_ref[...] + b_ref[...] - t_ref[...]
        dy = err * (1.0 / D)
        dz = _ln_bwd(dy * g_ref[...], zh, rstd)
        dff_ref[...] = (gate2 * dz).astype(BF16)
        dx1_ref[...] = ALPHA * dz
        vec_ref[0:1, :] += _colsum(dy * zh)
        vec_ref[1:2, :] += _colsum(dy)
        vec_ref[2:3, :] += _colsum(dz * ff_v)
        vec_ref[3:4, :] += _colsum(err * err) * (0.5 / D)

    row = pl.BlockSpec((tm, D), lambda i: (i, 0))
    vec = pl.BlockSpec((1, D), lambda i: (0, 0))
    return pl.pallas_call(
        body, grid=(T // tm,), name="final_fwd_bwd",
        in_specs=[row, row, pl.BlockSpec((6, D), lambda i: (0, 0)), vec, vec, row],
        out_specs=[row, row, pl.BlockSpec((8, D), lambda i: (0, 0))],
        out_shape=[_sds((T, D), BF16), _sds((T, D), F32), _sds((8, D), F32)],
        compiler_params=_params(("arbitrary",)),
    )(x1, ff, mod6, ln2_g, ln2_b, tgt)


def _mid_bwd(x2, mix, x1, dx1a, dh2, mod6, ln1_g):
    T, D = x2.shape
    tm = _tile(T, 256, 8)

    def body(x_ref, mix_ref, x1_ref, dx1a_ref, dh2_ref, mod_ref, g_ref, dmix_ref, dxa_ref, vec_ref):
        @pl.when(pl.program_id(0) == 0)
        def _():
            vec_ref[...] = jnp.zeros_like(vec_ref)

        dh2 = dh2_ref[...]
        xh, rstd = _ln(x1_ref[...])
        dx1 = dx1a_ref[...] + _ln_bwd(dh2 * (1.0 + mod_ref[4:5, :]), xh, rstd)
        mix_v = mix_ref[...]
        gate1 = mod_ref[2:3, :]
        zh, rstdz = _ln(ALPHA * x_ref[...] + gate1 * mix_v)
        dz = _ln_bwd(dx1 * g_ref[...], zh, rstdz)
        dmix_ref[...] = (gate1 * dz).astype(BF16)
        dxa_ref[...] = ALPHA * dz
        vec_ref[0:1, :] += _colsum(dh2 * xh)
        vec_ref[1:2, :] += _colsum(dh2)
        vec_ref[2:3, :] += _colsum(dx1 * zh)
        vec_ref[3:4, :] += _colsum(dx1)
        vec_ref[4:5, :] += _colsum(dz * mix_v)

    row = pl.BlockSpec((tm, D), lambda i: (i, 0))
    vec = pl.BlockSpec((1, D), lambda i: (0, 0))
    return pl.pallas_call(
        body, grid=(T // tm,), name="mid_bwd",
        in_specs=[row, row, row, row, row, pl.BlockSpec((6, D), lambda i: (0, 0)), vec],
        out_specs=[row, row, pl.BlockSpec((8, D), lambda i: (0, 0))],
        out_shape=[_sds((T, D), BF16), _sds((T, D), F32), _sds((8, D), F32)],
        compiler_params=_params(("arbitrary",)),
    )(x2, mix, x1, dx1a, dh2, mod6, ln1_g)


def _first_bwd(x2, dh1, dxa, mod6):
    T, D = x2.shape
    tm = _tile(T, 256, 8)

    def body(x_ref, dh1_ref, dxa_ref, mod_ref, gx_ref, vec_ref):
        @pl.when(pl.program_id(0) == 0)
        def _():
            vec_ref[...] = jnp.zeros_like(vec_ref)

        dh1 = dh1_ref[...]
        xh, rstd = _ln(x_ref[...])
        gx_ref[...] = dxa_ref[...] + _ln_bwd(dh1 * (1.0 + mod_ref[1:2, :]), xh, rstd)
        vec_ref[0:1, :] += _colsum(dh1 * xh)
        vec_ref[1:2, :] += _colsum(dh1)

    row = pl.BlockSpec((tm, D), lambda i: (i, 0))
    return pl.pallas_call(
        body, grid=(T // tm,), name="first_bwd",
        in_specs=[row, row, row, pl.BlockSpec((6, D), lambda i: (0, 0))],
        out_specs=[row, pl.BlockSpec((8, D), lambda i: (0, 0))],
        out_shape=[_sds((T, D), F32), _sds((8, D), F32)],
        compiler_params=_params(("arbitrary",)),
    )(x2, dh1, dxa, mod6)


def _slot(j):
    return (j % 2) * 4 + j // 2


def _mm_gathered(a, wg, shards, out, name):
    M, K = a.shape
    _, _, ns = wg.shape
    tm = _tile(M, 512, 16)

    def body(shards_ref, a_ref, w_ref, prev_ref, o_ref):
        o_ref[...] = _dot(a_ref[...], w_ref[...], NN)

    return pl.pallas_call(
        body, name=name,
        grid_spec=pltpu.PrefetchScalarGridSpec(
            num_scalar_prefetch=1, grid=(shards.shape[0], M // tm),
            in_specs=[pl.BlockSpec((tm, K), lambda j, i, s: (i, 0)),
                      pl.BlockSpec((None, K, ns), lambda j, i, s: (s[j], 0, 0)), ORDER_ONLY],
            out_specs=pl.BlockSpec((tm, ns), lambda j, i, s: (i, s[j]))),
        out_shape=_sds((M, N_DEV * ns), F32), input_output_aliases={3: 0},
        compiler_params=_params(("parallel", "parallel"), big=True),
    )(shards, a, wg, out)


def _mm_nn(a, b, name):
    M, K = a.shape
    _, N = b.shape
    tm, tn = _tile(M, 512, 16), _tile(N, 1024, LANE)

    def body(a_ref, b_ref, o_ref):
        o_ref[...] = _dot(a_ref[...], b_ref[...], NN)

    return pl.pallas_call(
        body, grid=(N // tn, M // tm), name=name,
        in_specs=[pl.BlockSpec((tm, K), lambda j, i: (i, 0)), pl.BlockSpec((K, tn), lambda j, i: (0, j))],
        out_specs=pl.BlockSpec((tm, tn), lambda j, i: (i, j)),
        out_shape=_sds((M, N), F32), compiler_params=_params(("parallel", "parallel"), big=True),
    )(a, b)


def _mm_nt(a, b, name):
    M, K = a.shape
    N, _ = b.shape
    tm, tn = _tile(M, 512, 16), _tile(N, 1024, LANE)

    def body(a_ref, b_ref, o_ref):
        o_ref[...] = _dot(a_ref[...], b_ref[...], NT)

    return pl.pallas_call(
        body, grid=(M // tm, N // tn), name=name,
        in_specs=[pl.BlockSpec((tm, K), lambda i, j: (i, 0)), pl.BlockSpec((tn, K), lambda i, j: (j, 0))],
        out_specs=pl.BlockSpec((tm, tn), lambda i, j: (i, j)),
        out_shape=_sds((M, N), F32), compiler_params=_params(("parallel", "parallel"), big=True),
    )(a, b)


def _mm_swiglu(h2, wg, first, count, outs, dep, name):
    M, K = h2.shape
    _, _, ns = wg.shape
    half = N_DEV // 2
    tm = _tile(M, 256, 16)

    def body(a_ref, wgate_ref, wup_ref, prev_gu_ref, prev_act_ref, dep_ref, gu_ref, act_ref):
        a = a_ref[...]
        g = _dot(a, wgate_ref[...], NN)
        u = _dot(a, wup_ref[...], NN)
        sg = _sigmoid(g)
        silu = g * sg
        gu_ref[0] = u * (sg * (1.0 + g * (1.0 - sg)))
        gu_ref[1] = silu
        act_ref[...] = (silu * u).astype(BF16)

    return pl.pallas_call(
        body, grid=(count, M // tm), name=name,
        in_specs=[pl.BlockSpec((tm, K), lambda j, i: (i, 0)),
                  pl.BlockSpec((None, K, ns), lambda j, i: (j + first, 0, 0)),
                  pl.BlockSpec((None, K, ns), lambda j, i: (j + first + half, 0, 0))] + [ORDER_ONLY] * 3,
        out_specs=[pl.BlockSpec((2, tm, ns), lambda j, i: (0, i, j + first)),
                   pl.BlockSpec((tm, ns), lambda j, i: (i, j + first))],
        out_shape=[_sds((2, M, half * ns), F32), _sds((M, half * ns), BF16)],
        input_output_aliases={3: 0, 4: 1},
        compiler_params=_params(("parallel", "parallel"), big=True),
    )(h2, wg, wg, *outs, dep)


def _mm_swiglu_bwd(dff, w2, gu):
    M, K = dff.shape
    F = w2.shape[0]
    tm, tn = _tile(M, 512, 16), _tile(F, 1408, LANE)

    def body(a_ref, b_ref, gu_ref, du_ref):
        da = _dot(a_ref[...], b_ref[...], NT)
        du_ref[0] = (da * gu_ref[0]).astype(BF16)
        du_ref[1] = (da * gu_ref[1]).astype(BF16)

    return pl.pallas_call(
        body, grid=(F // tn, M // tm), name="ffn_out_bwd_swiglu",
        in_specs=[pl.BlockSpec((tm, K), lambda j, i: (i, 0)), pl.BlockSpec((tn, K), lambda j, i: (j, 0)),
                  pl.BlockSpec((2, tm, tn), lambda j, i: (0, i, j))],
        out_specs=pl.BlockSpec((2, tm, tn), lambda j, i: (0, i, j)),
        out_shape=_sds((2, M, F), BF16), compiler_params=_params(("parallel", "parallel"), big=True),
    )(dff, w2, gu)


ORDER_ONLY = pl.BlockSpec(memory_space=pl.ANY)


def _mm_tn_rows(dep, a, b, rs, name):
    M, Ka = a.shape
    _, N = b.shape

    def body(_, a_ref, b_ref, o_ref):
        g = _dot(a_ref[...], b_ref[...], TN)
        o_ref[0, 0] = g[0:rs, :].astype(BF16)
        o_ref[1, 0] = g[rs:2 * rs, :].astype(BF16)

    return pl.pallas_call(
        body, grid=(N_DEV // 2,), name=name,
        in_specs=[ORDER_ONLY, pl.BlockSpec((M, 2 * rs), lambda ch: (0, ch)), pl.BlockSpec((M, N), lambda ch: (0, 0))],
        out_specs=pl.BlockSpec((2, 1, rs, N), lambda ch: (0, ch, 0, 0)),
        out_shape=_sds((2, N_DEV // 2, rs, N), BF16),
        compiler_params=_params(("parallel",), big=True),
    )(dep, a, b)


def _mm_gathered_nt(dep, a, a_spec, wg, M, tm, name, first=0, count=None, out=None):
    _, K, ns = wg.shape
    count = M // tm if count is None else count
    out = lax.empty((M, K), F32) if out is None else out

    def body(_, a_ref, w_ref, prev_ref, o_ref):
        @pl.when(pl.program_id(1) == 0)
        def _():
            o_ref[...] = jnp.zeros_like(o_ref)

        o_ref[...] += _dot(a_ref[:, 0:ns], w_ref[0], NT) + _dot(a_ref[:, ns:2 * ns], w_ref[1], NT)

    return pl.pallas_call(
        body, grid=(count, N_DEV // 2), name=name,
        in_specs=[ORDER_ONLY, a_spec(tm, 2 * ns, first), pl.BlockSpec((2, K, ns), lambda i, p: (p, 0, 0)), ORDER_ONLY],
        out_specs=pl.BlockSpec((tm, K), lambda i, j: (i + first, 0)),
        out_shape=_sds((M, K), F32), input_output_aliases={3: 0},
        compiler_params=_params(("parallel", "arbitrary"), big=True),
    )(dep, a, wg, out)


def _mm_tn_gathered(dep, h, a, a_spec, ns, name):
    M, K = h.shape

    def body(_, h_ref, a_ref, o_ref):
        o_ref[...] = _dot(h_ref[...], a_ref[...], TN).astype(BF16)

    return pl.pallas_call(
        body, grid=(N_DEV,), name=name,
        in_specs=[ORDER_ONLY, pl.BlockSpec((M, K), lambda j: (0, 0)), a_spec(M, ns)],
        out_specs=pl.BlockSpec((None, K, ns), lambda j: (_slot(j), 0, 0)),
        out_shape=_sds((N_DEV, K, ns), BF16),
        compiler_params=_params(("parallel",), big=True),
    )(dep, h, a)


def _bias_onehot(rbp, max_rel):
    r = lax.broadcasted_iota(jnp.int32, (rbp, TAB), 0)
    m = lax.broadcasted_iota(jnp.int32, (rbp, TAB), 1)
    dist = KPAD - jnp.where(m < WIN, m, m - TAB)
    return (r == jnp.clip(dist, -max_rel, max_rel) + max_rel).astype(F32)


def _attn_setup(i, hp, k_ref, v_ref, gv_ref, kpad, vpad, bias):
    ls = slice(i * ATTN_HEAD_DIM, (i + 1) * ATTN_HEAD_DIM)
    kpad[i][0:KPAD, :] = jnp.zeros((KPAD, ATTN_HEAD_DIM), BF16)
    vpad[i][0:KPAD, :] = jnp.zeros((KPAD, ATTN_HEAD_DIM), BF16)
    kpad[i][KPAD:, :] = k_ref[:, ls].astype(BF16)
    vpad[i][KPAD:, :] = v_ref[:, ls].astype(BF16)
    gvrow = gv_ref[pl.ds(hp * ATTN_HEADS_PER_STEP + i, 1), :]
    tab = pltpu.roll(jnp.broadcast_to(gvrow, (QBLK, TAB)), 0, 1, stride=1, stride_axis=0)
    row = lax.broadcasted_iota(jnp.int32, (QBLK, WIN), 0)
    col = lax.broadcasted_iota(jnp.int32, (QBLK, WIN), 1)
    first = jnp.bitwise_and(row, -CHUNK)
    seen = jnp.logical_and(col >= first, col < first + (N_PAST + 1) * CHUNK)
    bias[i][...] = jnp.where(seen, tab[:, 0:WIN], NEG)


def _attn_probs(b, q_ref, kpad, vpad, bias, col):
    pair = range(ATTN_HEADS_PER_STEP)
    ls = [slice(i * ATTN_HEAD_DIM, (i + 1) * ATTN_HEAD_DIM) for i in pair]
    r0 = pl.multiple_of(b * QBLK, QBLK)
    q = [q_ref[pl.ds(r0, QBLK), ls[i]].astype(BF16) for i in pair]
    kw = [kpad[i][pl.ds(r0, WIN), :] for i in pair]
    vw = [vpad[i][pl.ds(r0, WIN), :] for i in pair]
    s = [_dot(q[i], kw[i], NT) * (ATTN_HEAD_DIM ** -0.5) + bias[i][...] for i in pair]
    s = [jnp.where(col >= KPAD - r0, s[i], NEG) for i in pair]
    p = [jnp.exp(s[i] - jnp.max(s[i], axis=-1, keepdims=True)) for i in pair]
    pn = [p[i] / jnp.sum(p[i], axis=-1, keepdims=True) for i in pair]
    return r0, ls, q, kw, vw, pn


def _attn_fwd(proj, gv, ga, AW):
    T = proj.shape[0]
    AH = ATTN_HEADS_PER_STEP
    W = AH * ATTN_HEAD_DIM
    HP = AW // W

    def body(q_ref, k_ref, v_ref, gv_ref, ga_ref, o_ref, *scratch):
        kpad, vpad, bias = (scratch[k * AH:(k + 1) * AH] for k in range(3))
        hp = pl.program_id(0)
        for i in range(AH):
            _attn_setup(i, hp, k_ref, v_ref, gv_ref, kpad, vpad, bias)
        col = lax.broadcasted_iota(jnp.int32, (QBLK, WIN), 1)

        def block(b, carry):
            pair = range(AH)
            r0, ls, _, _, vw, pn = _attn_probs(b, q_ref, kpad, vpad, bias, col)
            o = [_dot(pn[i].astype(BF16), vw[i], NN) for i in pair]
            r = [lax.rsqrt(jnp.mean(o[i] * o[i], axis=-1, keepdims=True) + EPS) for i in pair]
            outs = [o[i] * r[i] * ga_ref[0:1, ls[i]] for i in pair]
            o_ref[pl.ds(r0, QBLK), :] = jnp.concatenate(outs, axis=1).astype(BF16)
            return carry

        lax.fori_loop(0, T // QBLK, block, 0)

    blk = lambda off: pl.BlockSpec((T, W), lambda hp: (0, off + hp))
    return pl.pallas_call(
        body, grid=(HP,), name="attn_fwd",
        in_specs=[blk(0), blk(HP), blk(2 * HP), pl.BlockSpec(gv.shape, lambda hp: (0, 0)),
                  pl.BlockSpec((1, W), lambda hp: (0, hp))],
        out_specs=pl.BlockSpec((T, W), lambda hp: (0, hp)),
        out_shape=_sds((T, AW), BF16),
        scratch_shapes=[pltpu.VMEM((T + KPAD, ATTN_HEAD_DIM), BF16)] * (2 * AH) + [pltpu.VMEM((QBLK, WIN), F32)] * AH,
        compiler_params=_params(("parallel",), big=True),
    )(proj, proj, proj, gv, ga)


def _attn_bwd(proj, dmixin, gv, ga, AW):
    T = proj.shape[0]
    AH = ATTN_HEADS_PER_STEP
    W = AH * ATTN_HEAD_DIM
    HP = AW // W
    scale = ATTN_HEAD_DIM ** -0.5

    def body(q_ref, k_ref, v_ref, dn_ref, gv_ref, ga_ref, dq_ref, dk_ref, dv_ref, dgv_ref, dga_ref, *scratch):
        kpad, vpad, dkacc, dvacc, bias, dbias = (scratch[k * AH:(k + 1) * AH] for k in range(6))
        hp = pl.program_id(0)
        for i in range(AH):
            _attn_setup(i, hp, k_ref, v_ref, gv_ref, kpad, vpad, bias)
            dkacc[i][...] = jnp.zeros_like(dkacc[i])
            dvacc[i][...] = jnp.zeros_like(dvacc[i])
            dbias[i][...] = jnp.zeros_like(dbias[i])
        dga_ref[...] = jnp.zeros_like(dga_ref)
        col = lax.broadcasted_iota(jnp.int32, (QBLK, WIN), 1)

        def block(b, carry):
            pair = range(AH)
            r0, lss, qs, kws, vws, pns = _attn_probs(b, q_ref, kpad, vpad, bias, col)
            pn_b = [pns[i].astype(BF16) for i in pair]
            o = [_dot(pn_b[i], vws[i], NN) for i in pair]
            r = [lax.rsqrt(jnp.mean(o[i] * o[i], axis=-1, keepdims=True) + EPS) for i in pair]
            dn = [dn_ref[pl.ds(r0, QBLK), lss[i]] for i in pair]
            for i in pair:
                dga_ref[i:i + 1, :] += _colsum(dn[i] * o[i] * r[i])
            a = [dn[i] * ga_ref[0:1, lss[i]] for i in pair]
            do_b = [(r[i] * (a[i] - o[i] * (r[i] * r[i]) * jnp.mean(a[i] * o[i], axis=-1, keepdims=True))).astype(BF16)
                    for i in pair]
            dp = [_dot(do_b[i], vws[i], NT) for i in pair]
            for i in pair:
                dvacc[i][pl.ds(r0, WIN), :] += _dot(pn_b[i], do_b[i], TN)
            ds = [pns[i] * (dp[i] - jnp.sum(pns[i] * dp[i], axis=-1, keepdims=True)) for i in pair]
            for i in pair:
                dbias[i][...] += ds[i]
            ds_b = [ds[i].astype(BF16) for i in pair]
            dq = [_dot(ds_b[i], kws[i], NN) * scale for i in pair]
            dq_ref[pl.ds(r0, QBLK), :] = jnp.concatenate(dq, axis=1).astype(BF16)
            for i in pair:
                dkacc[i][pl.ds(r0, WIN), :] += _dot(ds_b[i], qs[i], TN) * scale
            return carry

        lax.fori_loop(0, T // QBLK, block, 0)

        rr = lax.broadcasted_iota(jnp.int32, (QBLK, QBLK), 0)
        cc = lax.broadcasted_iota(jnp.int32, (QBLK, QBLK), 1)
        flip = (rr + cc == QBLK - 1).astype(BF16)
        for i in range(AH):
            ls = slice(i * ATTN_HEAD_DIM, (i + 1) * ATTN_HEAD_DIM)
            dk_ref[:, ls] = dkacc[i][KPAD:, :].astype(BF16)
            dv_ref[:, ls] = dvacc[i][KPAD:, :].astype(BF16)
            full = jnp.concatenate([dbias[i][...], jnp.zeros((QBLK, TAB - WIN), F32)], axis=1)
            hi = full.astype(BF16)
            lo = (full - hi.astype(F32)).astype(BF16)
            rev = _dot(flip, hi, NN) + _dot(flip, lo, NN)
            dgv_ref[i:i + 1, :] = _colsum(pltpu.roll(rev, TAB - (QBLK - 1), 1, stride=1, stride_axis=0))

    blk = lambda off: pl.BlockSpec((T, W), lambda hp: (0, off + hp))
    accs = lambda dt: [pltpu.VMEM((T + KPAD, ATTN_HEAD_DIM), dt)] * AH
    return pl.pallas_call(
        body, grid=(HP,), name="attn_bwd",
        in_specs=[blk(0), blk(HP), blk(2 * HP), blk(0), pl.BlockSpec(gv.shape, lambda hp: (0, 0)),
                  pl.BlockSpec((1, W), lambda hp: (0, hp))],
        out_specs=[blk(0), blk(0), blk(0), pl.BlockSpec((None, AH, TAB), lambda hp: (hp, 0, 0)),
                   pl.BlockSpec((None, AH, ATTN_HEAD_DIM), lambda hp: (hp, 0, 0))],
        out_shape=[_sds((T, AW), BF16), _sds((T, AW), BF16), _sds((T, AW), BF16),
                   _sds((HP, AH, TAB), F32), _sds((HP, AH, ATTN_HEAD_DIM), F32)],
        scratch_shapes=accs(BF16) + accs(BF16) + accs(F32) + accs(F32) + [pltpu.VMEM((QBLK, WIN), F32)] * (2 * AH),
        compiler_params=_params(("parallel",), big=True),
    )(proj, proj, proj, dmixin, gv, ga)


def _ltri():
    r = lax.broadcasted_iota(jnp.int32, (CHUNK, CHUNK), 0)
    c = lax.broadcasted_iota(jnp.int32, (CHUNK, CHUNK), 1)
    return (c <= r).astype(BF16)


def _tri_dot(tri, v, dims):
    hi = v.astype(BF16)
    lo = (v - hi.astype(F32)).astype(BF16)
    return _dot(tri, hi, dims) + _dot(tri, lo, dims)


HEADS_PER_STEP = (8, 2)
REC_ROW_TILE = 512


def _alternate(stages):
    live = list(stages)
    while live:
        for g in list(live):
            if next(g, StopIteration) is StopIteration:
                live.remove(g)


def _hgrn_gates(n, ls, q_ref, f_ref, lb_ref, ltri):
    r0 = pl.multiple_of(n * CHUNK, CHUNK)
    rows = pl.ds(r0, CHUNK)
    lb = lb_ref[:, ls]
    qb = q_ref[rows, ls]
    sg = _sigmoid(f_ref[rows, ls])
    f = lb + (1.0 - lb) * sg
    sq = _sigmoid(qb)
    b = _tri_dot(ltri, jnp.log(f), NN)
    return rows, lb, qb, sg, f, 1.0 - f, sq, qb * sq, b


def _hgrn_specs(T, RW, AW, backward):
    HG = HEADS_PER_STEP[1 if backward else 0]
    W = HG * LANE
    TT = _tile(T, REC_ROW_TILE, CHUNK)
    n_row_tiles = T // TT
    base = 3 * AW // W
    row = (lambda t: n_row_tiles - 1 - t) if backward else (lambda t: t)
    blk_in = lambda off: pl.BlockSpec((TT, W), lambda g, t: (row(t), base + off + g))
    col = pl.BlockSpec((TT, W), lambda g, t: (row(t), g))
    states = pl.BlockSpec((HG, TT // CHUNK, LANE, LANE), lambda g, t: (g, row(t), 0, 0))
    return HG, W, RW // W, TT, n_row_tiles, blk_in, col, states


def _hgrn_fwd(proj, lb, gn, AW, RW):
    T = proj.shape[0]
    RH, NC, NSUB = RW // LANE, T // CHUNK, CHUNK // SUB
    HG, W, NG, TT, n_row_tiles, blk_in, col, states = _hgrn_specs(T, RW, AW, False)

    def body(q_ref, f_ref, i_ref, g_ref, lb_ref, gn_ref, mix_ref, o_ref, stall_ref, st_all, bs_all, kks_all, ics_all):
        @pl.when(pl.program_id(1) == 0)
        def _():
            st_all[...] = jnp.zeros_like(st_all)

        ltri = _ltri()
        rowi = lax.broadcasted_iota(jnp.int32, (SUB, 1), 0)

        def one_head(h, n):
            ls = slice(h * LANE, (h + 1) * LANE)
            st, bs, kks, ics = st_all.at[h], bs_all.at[h], kks_all.at[h], ics_all.at[h]
            rows, _, _, _, _, kk, _, qs, b = _hgrn_gates(n, ls, q_ref, f_ref, lb_ref, ltri)
            ic = i_ref[rows, ls]
            stv = st[...]
            stall_ref[h, n] = stv
            bs[...] = b
            kks[...] = kk
            ics[...] = ic
            yield
            o = _dot((qs * jnp.exp(b)).astype(BF16), stv.astype(BF16), NT)
            yield
            ic_b = ic.astype(BF16)
            pieces = []
            for blk in range(NSUB):
                s0 = blk * SUB
                bI, qI = b[s0:s0 + SUB], qs[s0:s0 + SUB]
                if blk == 0:
                    oI = jnp.zeros((SUB, LANE), F32)
                else:
                    ref = bs[s0 - 1:s0, :]
                    qt = (qI * jnp.exp(bI - ref)).astype(BF16)
                    kt = (kk[0:s0] * jnp.exp(ref - b[0:s0])).astype(BF16)
                    oI = _dot(_dot(qt, kt, NT).astype(BF16), ic_b[0:s0], NN)
                    yield
                acc = [oI[g * ROWS:(g + 1) * ROWS] for g in range(SUB // ROWS)]
                for s in range(SUB):
                    sr = s0 + s
                    g0 = s // ROWS
                    lo = g0 * ROWS
                    e = jnp.exp(jnp.minimum(bI[lo:] - bs[sr:sr + 1, :], 0.0))
                    a = jnp.sum(qI[lo:] * kks[sr:sr + 1, :] * e, axis=-1, keepdims=True)
                    add = jnp.where(rowi[lo:] >= s, a, 0.0) * ics[sr:sr + 1, :]
                    for g in range(g0, SUB // ROWS):
                        acc[g] = acc[g] + add[(g - g0) * ROWS:(g - g0 + 1) * ROWS]
                    yield
                pieces.extend(acc)
            o = o + jnp.concatenate(pieces, axis=0)
            bl = bs[CHUNK - 1:CHUNK, :]
            kd = (kk * jnp.exp(bl - b)).astype(BF16)
            st[...] = stv * jnp.exp(bl) + _dot(ic_b, kd, TN)
            yield
            o_ref[rows, ls] = o
            r = lax.rsqrt(jnp.mean(o * o, axis=-1, keepdims=True) + EPS)
            gb = g_ref[rows, ls]
            mix_ref[rows, ls] = (o * r * gn_ref[...] * (gb * _sigmoid(gb))).astype(BF16)

        def chunk(n, carry):
            _alternate([one_head(h, n) for h in range(HG)])
            return carry

        lax.fori_loop(0, TT // CHUNK, chunk, 0)

    tile = pltpu.VMEM((HG, CHUNK, LANE), F32)
    return pl.pallas_call(
        body, grid=(NG, n_row_tiles), name="hgrn_fwd",
        in_specs=[blk_in(0), blk_in(NG), blk_in(2 * NG), blk_in(3 * NG), pl.BlockSpec((1, W), lambda g, t: (0, g)),
                  pl.BlockSpec((1, LANE), lambda g, t: (0, 0))],
        out_specs=[col, col, states],
        out_shape=[_sds((T, RW), BF16), _sds((T, RW), F32), _sds((RH, NC, LANE, LANE), F32)],
        scratch_shapes=[pltpu.VMEM((HG, LANE, LANE), F32), tile, tile, tile],
        compiler_params=_params(("parallel", "arbitrary"), big=True),
    )(proj, proj, proj, proj, lb, gn)


def _hgrn_bwd(proj, dmixin, o_b, st_all, lb, gn, AW, RW):
    T = proj.shape[0]
    RH, NC, NSUB = RW // LANE, T // CHUNK, CHUNK // SUB
    HG, W, NG, TT, n_row_tiles, blk_in, col, states = _hgrn_specs(T, RW, AW, True)

    def body(q_ref, f_ref, i_ref, g_ref, o_ref, dn_ref, stall_ref, lb_ref, gn_ref,
             dq_ref, df_ref, di_ref, dg_ref, dlb_ref, dgn_ref, dst_all, bs_all, qss_all, dos_all, p2_all, dic_all,
             p1_all):
        @pl.when(pl.program_id(1) == 0)
        def _():
            dst_all[...] = jnp.zeros_like(dst_all)
            dlb_ref[...] = jnp.zeros_like(dlb_ref)
            dgn_ref[...] = jnp.zeros_like(dgn_ref)

        ltri = _ltri()
        rowi = lax.broadcasted_iota(jnp.int32, (SUB, 1), 0)
        last = lax.broadcasted_iota(jnp.int32, (CHUNK, 1), 0) == CHUNK - 1

        def one_head(h, n):
            ls = slice(h * LANE, (h + 1) * LANE)
            dst, bs, qss, dos = dst_all.at[h], bs_all.at[h], qss_all.at[h], dos_all.at[h]
            p2, dic, p1s = p2_all.at[h], dic_all.at[h], p1_all.at[h]
            rows, lbv, qb, sg, f, kk, sq, qs, b = _hgrn_gates(n, ls, q_ref, f_ref, lb_ref, ltri)
            ic = i_ref[rows, ls]
            stv = stall_ref[h, n]
            dstv = dst[...]
            o = o_ref[rows, ls]
            dn = dn_ref[rows, ls]
            gb = g_ref[rows, ls]
            sgb = _sigmoid(gb)
            r = lax.rsqrt(jnp.mean(o * o, axis=-1, keepdims=True) + EPS)
            gnv = gn_ref[...]
            dg_ref[rows, ls] = (dn * (o * r * gnv) * (sgb * (1.0 + gb * (1.0 - sgb)))).astype(BF16)
            dy = dn * (gb * sgb)
            dgn_ref[h] += _colsum(dy * o * r)
            a_ = dy * gnv
            do = r * (a_ - o * (r * r) * jnp.mean(a_ * o, axis=-1, keepdims=True))
            do_b = do.astype(BF16)
            bs[...] = b
            qss[...] = qs
            dos[...] = do
            yield
            ic_b = ic.astype(BF16)
            eb = jnp.exp(b)
            bl = bs[CHUNK - 1:CHUNK, :]
            ebl = jnp.exp(bl)
            dec = jnp.exp(bl - b)
            kd = (kk * dec).astype(BF16)
            dst_b = dstv.astype(BF16)
            dqs = _dot(do_b, stv.astype(BF16), NN) * eb
            dkk2 = _dot(ic_b, dst_b, NN) * dec
            dic[...] = _dot(kd, dst_b, NT)
            dbl = ebl * _colsum(stv * dstv) + _colsum(kk * dkk2)
            dst[...] = dstv * ebl + _dot(do_b, (qs * eb).astype(BF16), TN)
            yield
            p2[...] = jnp.zeros_like(p2)
            p1_pieces = []
            for blk in range(NSUB):
                s0 = blk * SUB
                bI, qI, doI = b[s0:s0 + SUB], qs[s0:s0 + SUB], do[s0:s0 + SUB]
                if blk == 0:
                    p1 = jnp.zeros((SUB, LANE), F32)
                else:
                    ref = bs[s0 - 1:s0, :]
                    eq = jnp.exp(bI - ref)
                    ek = jnp.exp(ref - b[0:s0])
                    qt = (qI * eq).astype(BF16)
                    kt = (kk[0:s0] * ek).astype(BF16)
                    doI_b = doI.astype(BF16)
                    dic[0:s0, :] += _dot(_dot(qt, kt, NT).astype(BF16), doI_b, TN)
                    da = _dot(doI_b, ic_b[0:s0], NT).astype(BF16)
                    p1 = _dot(da, kt, NN) * eq
                    p2[0:s0, :] += _dot(da, qt, TN) * ek
                    yield
                p1_pieces.append(p1)
                kkI, icI = kk[s0:s0 + SUB], ic[s0:s0 + SUB]
                p2acc = [jnp.zeros((ROWS, LANE), F32) for _ in range(SUB // ROWS)]
                diacc = [jnp.zeros((ROWS, LANE), F32) for _ in range(SUB // ROWS)]
                for t in range(SUB):
                    tr = s0 + t
                    ng = t // ROWS + 1
                    hi = ng * ROWS
                    keep = rowi[:hi] <= t
                    do_t = dos[tr:tr + 1, :]
                    e = jnp.exp(jnp.minimum(bs[tr:tr + 1, :] - bI[:hi], 0.0))
                    qe = qss[tr:tr + 1, :] * e
                    a = jnp.where(keep, jnp.sum(kkI[:hi] * qe, axis=-1, keepdims=True), 0.0)
                    da = jnp.where(keep, jnp.sum(icI[:hi] * do_t, axis=-1, keepdims=True), 0.0)
                    dp2, ddi = da * qe, a * do_t
                    for g in range(ng):
                        p2acc[g] = p2acc[g] + dp2[g * ROWS:(g + 1) * ROWS]
                        diacc[g] = diacc[g] + ddi[g * ROWS:(g + 1) * ROWS]
                    p1s[tr:tr + 1, :] = _colsum(da * kkI[:hi] * e)
                    yield
                p2[s0:s0 + SUB, :] += jnp.concatenate(p2acc, axis=0)
                dic[s0:s0 + SUB, :] += jnp.concatenate(diacc, axis=0)
            dqs = dqs + jnp.concatenate(p1_pieces, axis=0) + p1s[...]
            dkk = dkk2 + p2[...]
            db = qs * dqs - kk * dkk + jnp.where(last, dbl, 0.0)
            dgl = _tri_dot(ltri, db, TN)
            yield
            dfv = dgl / f - dkk
            df_ref[rows, ls] = (dfv * (1.0 - lbv) * sg * (1.0 - sg)).astype(BF16)
            dlb_ref[:, ls] += _colsum(dfv * (1.0 - sg))
            dq_ref[rows, ls] = (dqs * (sq * (1.0 + qb * (1.0 - sq)))).astype(BF16)
            di_ref[rows, ls] = dic[...].astype(BF16)

        def chunk(k, carry):
            _alternate([one_head(h, TT // CHUNK - 1 - k) for h in range(HG)])
            return carry

        lax.fori_loop(0, TT // CHUNK, chunk, 0)

    tile = pltpu.VMEM((HG, CHUNK, LANE), F32)
    return pl.pallas_call(
        body, grid=(NG, n_row_tiles), name="hgrn_bwd",
        in_specs=[blk_in(0), blk_in(NG), blk_in(2 * NG), blk_in(3 * NG), col,
                  pl.BlockSpec((TT, W), lambda g, t: (n_row_tiles - 1 - t, AW // W + g)), states,
                  pl.BlockSpec((1, W), lambda g, t: (0, g)), pl.BlockSpec((1, LANE), lambda g, t: (0, 0))],
        out_specs=[col, col, col, col, pl.BlockSpec((1, W), lambda g, t: (0, g)),
                   pl.BlockSpec((HG, 1, LANE), lambda g, t: (g, 0, 0))],
        out_shape=[_sds((T, RW), BF16)] * 4 + [_sds((1, RW), F32), _sds((RH, 1, LANE), F32)],
        scratch_shapes=[pltpu.VMEM((HG, LANE, LANE), F32), tile, tile, tile, tile, tile, tile],
        compiler_params=_params(("parallel", "arbitrary"), big=True),
    )(proj, proj, proj, proj, o_b, dmixin, st_all, lb, gn)


def _prep(c, lb_logits, rb_pad, max_rel, after):
    D, RW = c.shape[-1], lb_logits.shape[-1]
    H, rbp = rb_pad.shape

    def body(c_ref, l_ref, rb_ref, _, __, cact_ref, lb_ref, gv_ref):
        cv = c_ref[...]
        cact_ref[...] = cv * _sigmoid(cv)
        lb_ref[...] = _sigmoid(l_ref[0:1, :] - l_ref[1:2, :])
        gv_ref[...] = _dot(rb_ref[...], _bias_onehot(rbp, max_rel), NN, HIGHEST)

    vmem = pl.BlockSpec(memory_space=pltpu.VMEM)
    return pl.pallas_call(
        body, name="prep", in_specs=[vmem, vmem, vmem, ORDER_ONLY, ORDER_ONLY],
        out_shape=[_sds((1, D), F32), _sds((1, RW), F32), _sds((H, TAB), F32)],
    )(c, lb_logits, rb_pad, *after)


def _mod_part(c_all, w_ada_s, b_ada_s):
    B, D = c_all.shape
    ns = w_ada_s.shape[1]
    tn = _tile(ns, 768, LANE)

    def body(c_ref, w_ref, b_ref, o_ref):
        o_ref[...] = _dot(c_ref[...], w_ref[...], NN) + b_ref[...]

    return pl.pallas_call(
        body, grid=(ns // tn,), name="mod_part",
        in_specs=[pl.BlockSpec((B, D), lambda j: (0, 0)), pl.BlockSpec((D, tn), lambda j: (0, j)),
                  pl.BlockSpec((1, tn), lambda j: (0, j))],
        out_specs=pl.BlockSpec((B, tn), lambda j: (0, j)),
        out_shape=_sds((B, ns), F32), compiler_params=_params(("parallel",)),
    )(c_all, w_ada_s, b_ada_s)


def _adam(w, g, m, v):
    m = ADAM_B1 * m + (1.0 - ADAM_B1) * g
    v = ADAM_B2 * v + (1.0 - ADAM_B2) * (g * g)
    m_hat = m * (1.0 / (1.0 - ADAM_B1 ** ADAM_STEP))
    v_hat = v * (1.0 / (1.0 - ADAM_B2 ** ADAM_STEP))
    return -ADAM_LR * (m_hat / (jnp.sqrt(v_hat) + ADAM_EPS) + ADAM_WD * w), m, v


def _adam_ada(c_all, dmod_s, w, m, v):
    B, D = c_all.shape
    ns = w.shape[1]
    tr, tn = _tile(D, 512, LANE), _tile(ns, 768, LANE)

    def body(c_ref, d_ref, w_ref, m_ref, v_ref, g_out, dw_out, m_out, v_out):
        g = _dot(c_ref[...], d_ref[...], TN)
        g_out[...] = g
        dw_out[...], m_out[...], v_out[...] = _adam(w_ref[...], g, m_ref[...], v_ref[...])

    big = pl.BlockSpec((tr, tn), lambda i, j: (i, j))
    return pl.pallas_call(
        body, grid=(D // tr, ns // tn), name="adam_w_ada",
        in_specs=[pl.BlockSpec((B, tr), lambda i, j: (0, i)), pl.BlockSpec((B, tn), lambda i, j: (0, j)),
                  big, big, big],
        out_specs=[big] * 4, out_shape=[_sds((D, ns), F32)] * 4,
        compiler_params=_params(("parallel", "parallel")),
    )(c_all, dmod_s, w, m, v)


def _adam_shard(parts, w, m, v, name):
    R, C = w.shape
    tr = _tile(R, 256, 16)

    def body(p_ref, w_ref, m_ref, v_ref, g_out, dw_out, m_out, v_out):
        g = p_ref[0].astype(F32)
        for k in range(1, N_DEV // 2):
            g = g + p_ref[k].astype(F32)
        g_out[...] = g
        dw_out[...], m_out[...], v_out[...] = _adam(w_ref[...], g, m_ref[...], v_ref[...])

    big = pl.BlockSpec((tr, C), lambda i: (i, 0))
    return pl.pallas_call(
        body, grid=(R // tr,), name=name,
        in_specs=[pl.BlockSpec((N_DEV // 2, tr, C), lambda i: (0, i, 0)), big, big, big],
        out_specs=[big] * 4, out_shape=[_sds((R, C), F32)] * 4,
        compiler_params=_params(("parallel",), big=True),
    )(parts, w, m, v)


def _pair_sum(g8, land, core, name):
    _, NCHIP, R, C = g8.shape
    tr = _tile(R, 1024, 16)

    def body(core_ref, g_ref, l_ref, o_ref):
        o_ref[...] = g_ref[...] + l_ref[...]

    return pl.pallas_call(
        body, name=name,
        grid_spec=pltpu.PrefetchScalarGridSpec(
            num_scalar_prefetch=1, grid=(NCHIP, R // tr),
            in_specs=[pl.BlockSpec((None, None, tr, C), lambda k, i, core_ref: (core_ref[0], k, i, 0)),
                      pl.BlockSpec((None, tr, C), lambda k, i, core_ref: (k, i, 0))],
            out_specs=pl.BlockSpec((None, tr, C), lambda k, i, core_ref: (k, i, 0))),
        out_shape=_sds((NCHIP, R, C), BF16), compiler_params=_params(("parallel", "parallel")),
    )(core, g8, land)


SMALL = ("b_ada", "rel_bias", "attn_norm_g", "lb_logits", "gnorm_g", "ln1_g", "ln1_b", "ln2_g", "ln2_b")


def _small_update(parts, loss_parts, lbv, ws, ms, vs, max_rel):
    n = len(SMALL)

    def body(*refs):
        part_refs = dict(zip(SMALL, refs[:n]))
        loss_in, lb_ref = refs[n], refs[n + 1]
        w_refs, m_refs, v_refs = refs[n + 2:2 * n + 2], refs[2 * n + 2:3 * n + 2], refs[3 * n + 2:4 * n + 2]
        outs = refs[4 * n + 2:]

        def total(ref):
            tot = ref[0]
            for k in range(1, N_DEV):
                tot = tot + ref[k]
            return tot

        outs[0][...] = jnp.sum(total(loss_in), axis=-1, keepdims=True)
        for idx, name in enumerate(SMALL):
            g = total(part_refs[name])
            if name == "rel_bias":
                g = _dot(g, _bias_onehot(w_refs[idx].shape[1], max_rel), NT, HIGHEST)
            elif name == "lb_logits":
                lb = lb_ref[...]
                sign = (1 - 2 * lax.broadcasted_iota(jnp.int32, (2, 1), 0)).astype(F32)
                g = sign * (g * lb * (1.0 - lb))
            elif name == "gnorm_g":
                g = _colsum(g)
            dw, mm, vv = _adam(w_refs[idx][...], g, m_refs[idx][...], v_refs[idx][...])
            outs[1 + 4 * idx][...] = g
            outs[2 + 4 * idx][...] = dw
            outs[3 + 4 * idx][...] = mm
            outs[4 + 4 * idx][...] = vv

    out_shape = [_sds((1, 1), F32)]
    for w in ws:
        out_shape += [_sds(w.shape, F32)] * 4
    return pl.pallas_call(body, name="small_update", out_shape=out_shape, compiler_params=_params(big=True))(
        *[parts[k] for k in SMALL], loss_parts, lbv, *ws, *ms, *vs)


def _place():
    x, y, c = lax.axis_index("x"), lax.axis_index("y"), lax.axis_index("c")
    return x, y, c, [(1 - x, y), (x, 1 - y), (1 - x, 1 - y)]


def _all_gather(shard, name):
    HBM = pl.BlockSpec(memory_space=pl.ANY)

    def body(x_ref, out_ref, send_sems, recv_sems, local_sem):
        x, y, c, chips = _place()
        me, sibling = (x, y, c), (x, y, 1 - c)

        def slot(px, py, pc):
            return out_ref.at[4 * px + 2 * py + pc]

        def copy(k, block, to, src=None):
            return pltpu.make_async_remote_copy(
                src_ref=slot(*block) if src is None else src, dst_ref=slot(*block),
                send_sem=send_sems.at[k], recv_sem=recv_sems.at[k], device_id=to, device_id_type=MESH)

        mine = pltpu.make_async_copy(x_ref, slot(*me), local_sem)
        mine.start()
        first = [copy(0, me, sibling, src=x_ref)]
        first += [copy(1 + j, me, (*chip, c), src=x_ref) for j, chip in enumerate(chips)]
        for cp in first:
            cp.start()
        passed = [copy(4 + j, (*chip, c), sibling) for j, chip in enumerate(chips)]
        for j, chip in enumerate(chips):
            copy(1 + j, (*chip, c), me).wait_recv()
            passed[j].start()
        copy(0, sibling, me).wait_recv()
        for j, chip in enumerate(chips):
            copy(4 + j, (*chip, 1 - c), me).wait_recv()
        for cp in first + passed:
            cp.wait_send()
        mine.wait()

    return pl.pallas_call(
        body, name=name, out_shape=_sds((N_DEV,) + shard.shape, shard.dtype),
        in_specs=[HBM], out_specs=HBM,
        scratch_shapes=[pltpu.SemaphoreType.DMA((7,)), pltpu.SemaphoreType.DMA((7,)), pltpu.SemaphoreType.DMA(())],
    )(shard)


SEM_SPEC = pl.BlockSpec(memory_space=pltpu.SEMAPHORE)
HBM_SPEC = pl.BlockSpec(memory_space=pltpu.HBM)
EFFECT = pltpu.SideEffectType.DATAFLOW_SIDE_EFFECTING


def _remote(src, dst, send_sems, recv_sems, k, dev):
    return pltpu.make_async_remote_copy(src_ref=src, dst_ref=dst, send_sem=send_sems.at[k], recv_sem=recv_sems.at[k],
                                        device_id=dev, device_id_type=MESH)


def _copy_start(name, bufs, plan, n, after, only=None):
    nb = len(bufs)

    def body(*refs):
        send_sems, recv_sems = refs[nb + 1], refs[nb + 2]
        for k, (src, dst, dev) in enumerate(plan(*refs[:nb])):
            if only is not None and k not in only:
                continue
            _remote(src, dst, send_sems, recv_sems, k, dev).start()
        refs[-1][...] = jnp.zeros_like(refs[-1])

    out = pl.pallas_call(
        body, name=name,
        out_shape=(pltpu.SemaphoreType.DMA((n,)), pltpu.SemaphoreType.DMA((n,)),
                   *[pltpu.HBM(b.shape, b.dtype) for b in bufs], _sds((8, LANE), F32)),
        in_specs=[HBM_SPEC] * nb + [ORDER_ONLY],
        out_specs=(SEM_SPEC, SEM_SPEC, *[HBM_SPEC] * nb, pl.BlockSpec(memory_space=pltpu.VMEM)),
        input_output_aliases={i: 2 + i for i in range(nb)},
        compiler_params=pltpu.CompilerParams(has_side_effects=EFFECT),
    )(*[pltpu.with_memory_space_constraint(b, pltpu.HBM) for b in bufs], after)
    return (out[0], out[1]), list(out[2:2 + nb]), out[-1]


def _copy_wait(name, sems, bufs, plan, after, only=None):
    nb = len(bufs)

    def body(*refs):
        send_sems, recv_sems = refs[nb], refs[nb + 1]
        for k, (src, dst, dev) in enumerate(plan(*refs[:nb])):
            if only is not None and k not in only:
                continue
            cp = _remote(src, dst, send_sems, recv_sems, k, dev)
            cp.wait_send()
            cp.wait_recv()

    out = pl.pallas_call(
        body, name=name, out_shape=tuple(pltpu.HBM(b.shape, b.dtype) for b in bufs),
        in_specs=[HBM_SPEC] * nb + [SEM_SPEC, SEM_SPEC, pl.BlockSpec(memory_space=pl.ANY)],
        out_specs=tuple([HBM_SPEC] * nb), input_output_aliases={i: i for i in range(nb)},
        compiler_params=pltpu.CompilerParams(has_side_effects=EFFECT),
    )(*bufs, sems[0], sems[1], after)
    return list(out)


def _ag_plan_chips(shard_ref, out_ref):
    x, y, c, chips = _place()
    mine = out_ref.at[4 * x + 2 * y + c]
    return [(shard_ref, mine, (x, y, 1 - c))] + [(shard_ref, mine, (*chip, c)) for chip in chips]


def _ag_plan_pass(out_ref):
    x, y, c, chips = _place()
    slots = [out_ref.at[4 * chip[0] + 2 * chip[1] + c] for chip in chips]
    return [(s, s, (x, y, 1 - c)) for s in slots]


def _rs_plan_pair(g_ref, land_ref):
    x, y, c, _ = _place()
    return [(g_ref.at[1 - c], land_ref, (x, y, 1 - c))]


def _rs_plan_chips(p_ref, land_ref):
    x, y, c, chips = _place()
    return [(p_ref.at[2 * chip[0] + chip[1]], land_ref.at[2 * x + y], (*chip, c)) for chip in chips]


class _Gather:
    @staticmethod
    def landing(shard, me):
        return lax.dynamic_update_slice(lax.empty((N_DEV,) + shard.shape, shard.dtype), shard[None],
                                        (me,) + (0,) * shard.ndim)

    def __init__(self, shard, out, tag, after):
        self.tag = tag
        self.sems, (self.shard, self.out), self.token = _copy_start(
            "ag_start_" + tag, [shard, out], _ag_plan_chips, 4, after)
        self.groups = []

    def arrived(self, after, copies):
        name = "ag_wait_%s_%s" % (self.tag, "".join(map(str, copies)))
        self.shard, self.out = _copy_wait(name, self.sems, [self.shard, self.out], _ag_plan_chips, after, copies)
        return self.out

    def pass_on(self, after, blocks):
        name = "ag_pass_%s_%s" % (self.tag, "".join(map(str, blocks)))
        sems, (self.out,), _ = _copy_start(name, [self.out], _ag_plan_pass, 3, after, blocks)
        self.groups.append((sems, blocks))
        return self.out

    def passed(self, after, group):
        sems, blocks = self.groups[group]
        name = "ag_pass_wait_%s_%s" % (self.tag, "".join(map(str, blocks)))
        self.out = _copy_wait(name, sems, [self.out], _ag_plan_pass, after, blocks)[0]
        return self.out

    def arrived_from_chips(self, after):
        self.arrived(after, (0, 1, 2, 3))
        return self.pass_on(after, (0, 1, 2))

    def passed_on(self, after):
        return self.passed(after, 0)


def _ag_plan_direct(src_ref, out_ref):
    x, y, c, chips = _place()
    mine = out_ref.at[4 * x + 2 * y + c]
    peers = [(x, y, 1 - c)] + [(*chip, pc) for chip in chips for pc in (c, 1 - c)]
    return [(src_ref, mine, peer) for peer in peers]


class _SmallGather:
    def __init__(self, block, me, tag):
        self.tag = tag
        out = lax.dynamic_update_slice(lax.empty((N_DEV,) + block.shape, block.dtype), block[None],
                                       (me,) + (0,) * block.ndim)
        self.sems, self.bufs, self.token = _copy_start(
            "ag_direct_start_" + tag, [block, out], _ag_plan_direct, N_DEV - 1, jnp.zeros((1,), F32))

    def done(self, after):
        return _copy_wait("ag_direct_wait_" + self.tag, self.sems, self.bufs, _ag_plan_direct, after)[1]


class _ReduceScatter:
    def __init__(self, g8, tag):
        self.tag = tag
        land = lax.empty(g8.shape[1:], g8.dtype)
        self.sems, self.bufs, self.token = _copy_start(
            "rs_pair_start_" + tag, [g8, land], _rs_plan_pair, 1, jnp.zeros((1,), F32))

    def pair_done(self, core, chip, after):
        g8, land = _copy_wait("rs_pair_wait_" + self.tag, self.sems, self.bufs, _rs_plan_pair, after)
        p4 = _pair_sum(g8, land, core, "rs_pair_sum_" + self.tag)
        own = lax.dynamic_slice_in_dim(p4, chip, 1, axis=0)
        land2 = lax.dynamic_update_slice(lax.empty(p4.shape, p4.dtype), own, (chip, 0, 0))
        self.sems, self.bufs, self.token = _copy_start(
            "rs_chips_start_" + self.tag, [p4, land2], _rs_plan_chips, 3, jnp.zeros((1,), F32))

    def sums(self, after):
        return _copy_wait("rs_chips_wait_" + self.tag, self.sems, self.bufs, _rs_plan_chips, after)[1]


ORDER = ("w_ada", "b_ada", "w_in", "rel_bias", "attn_norm_g", "lb_logits", "gnorm_g", "w_o", "ln1_g", "ln1_b",
         "w_ffn_in", "w_ffn_out", "ln2_g", "ln2_b")


def kernel(x, c, w_ada, b_ada, w_in, rel_bias, attn_norm_g, lb_logits, gnorm_g, w_o, ln1_g, ln1_b, w_ffn_in, w_ffn_out, ln2_g, ln2_b, loss_target, m_w_ada, m_b_ada, m_w_in, m_rel_bias, m_attn_norm_g, m_lb_logits, m_gnorm_g, m_w_o, m_ln1_g, m_ln1_b, m_w_ffn_in, m_w_ffn_out, m_ln2_g, m_ln2_b, v_w_ada, v_b_ada, v_w_in, v_rel_bias, v_attn_norm_g, v_lb_logits, v_gnorm_g, v_w_o, v_ln1_g, v_ln1_b, v_w_ffn_in, v_w_ffn_out, v_ln2_g, v_ln2_b):
    W = dict(w_ada=w_ada, b_ada=b_ada, w_in=w_in, rel_bias=rel_bias, attn_norm_g=attn_norm_g, lb_logits=lb_logits,
             gnorm_g=gnorm_g, w_o=w_o, ln1_g=ln1_g, ln1_b=ln1_b, w_ffn_in=w_ffn_in, w_ffn_out=w_ffn_out,
             ln2_g=ln2_g, ln2_b=ln2_b)
    M = dict(w_ada=m_w_ada, b_ada=m_b_ada, w_in=m_w_in, rel_bias=m_rel_bias, attn_norm_g=m_attn_norm_g,
             lb_logits=m_lb_logits, gnorm_g=m_gnorm_g, w_o=m_w_o, ln1_g=m_ln1_g, ln1_b=m_ln1_b,
             w_ffn_in=m_w_ffn_in, w_ffn_out=m_w_ffn_out, ln2_g=m_ln2_g, ln2_b=m_ln2_b)
    V = dict(w_ada=v_w_ada, b_ada=v_b_ada, w_in=v_w_in, rel_bias=v_rel_bias, attn_norm_g=v_attn_norm_g,
             lb_logits=v_lb_logits, gnorm_g=v_gnorm_g, w_o=v_w_o, ln1_g=v_ln1_g, ln1_b=v_ln1_b,
             w_ffn_in=v_w_ffn_in, w_ffn_out=v_w_ffn_out, ln2_g=v_ln2_g, ln2_b=v_ln2_b)

    x2, tgt = x[0], loss_target[0]
    T, D = x2.shape
    AW, RW = attn_norm_g.shape[-1], lb_logits.shape[-1]
    MIX = AW + RW
    H, RH = AW // ATTN_HEAD_DIM, RW // LANE
    RB = rel_bias.shape[-1]
    max_rel = (RB - 1) // 2
    rbp = -(-RB // LANE) * LANE
    F = w_ffn_out.shape[1] * N_DEV
    half = N_DEV // 2
    xi, yi, ci = lax.axis_index("x"), lax.axis_index("y"), lax.axis_index("c")
    me = 4 * xi + 2 * yi + ci
    core = jnp.reshape(ci, (1,)).astype(jnp.int32)
    pad_rb = lambda a: jnp.pad(a[0], ((0, 0), (0, rbp - RB)))

    chip = 2 * xi + yi

    w_in_b = w_in[0].astype(BF16)
    w_in_land = _Gather.landing(w_in_b, me)
    c_act, lbv, gv = _prep(c, lb_logits, pad_rb(rel_bias), max_rel, (w_in_b, w_in_land))
    c_all = _all_gather(c_act, "ag_c").reshape(N_DEV, D)
    ns_ada = w_ada.shape[-1]
    mod_part = _mod_part(c_all, w_ada[0], lax.dynamic_slice_in_dim(b_ada, me * ns_ada, ns_ada, axis=1))
    mod_all = _all_gather(mod_part, "ag_mod")
    mod6 = lax.dynamic_index_in_dim(mod_all, me, axis=1, keepdims=False).reshape(6, D)

    bf = lambda w: w[0].astype(BF16)
    ag_in = _Gather(w_in_b, w_in_land, "w_in", mod_all)
    ag_o = _Gather(bf(w_o), _Gather.landing(bf(w_o), me), "w_o", ag_in.token)
    ag_f1 = _Gather(bf(w_ffn_in), _Gather.landing(bf(w_ffn_in), me), "w_ffn_in", ag_o.token)
    ag_f2 = _Gather(bf(w_ffn_out), _Gather.landing(bf(w_ffn_out), me), "w_ffn_out", ag_f1.token)

    h1 = _ln_mod(x2, mod6 + ag_f2.token[0, 0])
    ids = lambda pairs: jnp.stack([4 * px + 2 * py + pc for px, py, pc in pairs]).astype(jnp.int32)
    others = [(1 - xi, yi), (xi, 1 - yi), (1 - xi, 1 - yi)]
    proj = lax.empty((T, w_in.shape[-1] * N_DEV), F32)
    proj = _mm_gathered(h1, ag_in.arrived(h1, (0,)), ids([(xi, yi, ci), (xi, yi, 1 - ci)]), proj, "in_proj_a")
    ag_in.arrived(proj, (1, 2, 3))
    proj = _mm_gathered(h1, ag_in.pass_on(proj, (0, 1, 2)), ids([(*ch, ci) for ch in others]), proj, "in_proj_b")
    wg_in = ag_in.passed(proj, 0)
    proj = _mm_gathered(h1, wg_in, ids([(*ch, 1 - ci) for ch in others]), proj, "in_proj_c")
    ag_o.arrived_from_chips(proj)
    mix_a = _attn_fwd(proj, gv, attn_norm_g, AW)
    wg_o = ag_o.passed_on(mix_a).reshape(MIX, D)
    mix_b, o_b, st_all = _hgrn_fwd(proj, lbv, gnorm_g, AW, RW)
    mixin = jnp.concatenate([mix_a, mix_b], axis=1)
    mix = _mm_nn(mixin, wg_o, "out_proj")
    ag_f1.arrived_from_chips(mix)
    x1, h2 = _mid_fwd(x2, mix, mod6, ln1_g, ln1_b)
    wg_f1 = ag_f1.passed_on(h2)
    outs = (lax.empty((2, T, F), F32), lax.empty((T, F), BF16))
    outs = _mm_swiglu(h2, wg_f1, 0, half - 1, outs, ag_f1.token, "ffn_in_swiglu_a")
    passing = ag_f2.arrived_from_chips(outs[1])
    gu, act = _mm_swiglu(h2, wg_f1, half - 1, 1, outs, passing, "ffn_in_swiglu_b")
    wg_f2 = ag_f2.passed_on(act).reshape(F, D)
    ff = _mm_nn(act, wg_f2, "ffn_out")
    dff, dx1a, vec_a = _final(x1, ff, mod6, ln2_g, ln2_b, tgt)

    du = _mm_swiglu_bwd(dff, wg_f2, gu)
    rs_f2 = _ReduceScatter(_mm_tn_rows(dff, act, dff, F // N_DEV, "grad_w_ffn_out"), "w_ffn_out")
    tm = _tile(T, 512, 16)
    du_ij = lambda tm_, w, first: pl.BlockSpec((None, tm_, w), lambda i, p: (p // (half // 2), i + first, p % (half // 2)))
    du_j = lambda rows, ns: pl.BlockSpec((None, rows, ns), lambda j: (j // half, 0, j % half))
    dh2 = _mm_gathered_nt(rs_f2.token, du, du_ij, wg_f1, T, tm, "ffn_in_bwd")
    rs_f2.pair_done(core, chip, dh2)
    gw_f1 = _mm_tn_gathered(rs_f2.token, h2, du, du_j, wg_f1.shape[-1], "grad_w_ffn_in")
    rs_f1 = _ReduceScatter(gw_f1.reshape(2, half, D, -1), "w_ffn_in")
    dmix, dxa, vec_b = _mid_bwd(x2, mix, x1, dx1a, dh2, mod6 + rs_f1.token[0, 0], ln1_g)
    dmixin = _mm_nt(dmix, wg_o, "out_proj_bwd")
    rs_f1.pair_done(core, chip, dmixin)
    rs_o = _ReduceScatter(_mm_tn_rows(rs_f1.token, mixin, dmix, MIX // N_DEV, "grad_w_o"), "w_o")
    dq, dk, dv, dgv, dga = _attn_bwd(proj, dmixin, gv + rs_o.token[0, 0], attn_norm_g, AW)
    rs_o.pair_done(core, chip, dq)
    dqb, dfl, dib, dgb, dlb, dgn = _hgrn_bwd(proj, dmixin, o_b, st_all, lbv + rs_o.token[0, 0], gnorm_g, AW, RW)
    dproj = jnp.concatenate([dq, dk, dv, dqb, dfl, dib, dgb], axis=1)
    p_ij = lambda tm_, w, first: pl.BlockSpec((tm_, w), lambda i, p: (i + first, p))
    p_j = lambda rows, ns: pl.BlockSpec((rows, ns), lambda j: (0, j))
    gw_in = _mm_tn_gathered(rs_o.token, h1, dproj, p_j, wg_in.shape[-1], "grad_w_in")
    rs_in = _ReduceScatter(gw_in.reshape(2, half, D, -1), "w_in")
    n_tiles = T // tm
    n_first = max(1, n_tiles // 4)
    dh1 = _mm_gathered_nt(rs_in.token, dproj, p_ij, wg_in, T, tm, "in_proj_bwd_a", 0, n_first)
    rs_in.pair_done(core, chip, dh1)
    dh1 = _mm_gathered_nt(rs_in.token, dproj, p_ij, wg_in, T, tm, "in_proj_bwd_b", n_first, n_tiles - n_first, dh1)
    grad_x, vec_c = _first_bwd(x2, dh1, dxa, mod6)

    dmod = jnp.concatenate([vec_c[1:2], vec_c[0:1], vec_b[4:5], vec_b[1:2], vec_b[0:1], vec_a[2:3]], axis=0)
    pieces = dict(b_ada=dmod, rel_bias=dgv, attn_norm_g=dga, lb_logits=dlb, gnorm_g=dgn, ln1_g=vec_b[2:3],
                  ln1_b=vec_b[3:4], ln2_g=vec_a[0:1], ln2_b=vec_a[1:2], loss=vec_a[3:4])
    widths = dict(b_ada=(1, 6 * D), rel_bias=(H, TAB), attn_norm_g=(1, AW), lb_logits=(1, RW), gnorm_g=(RH, LANE),
                  ln1_g=(1, D), ln1_b=(1, D), ln2_g=(1, D), ln2_b=(1, D), loss=(1, D))
    packed = jnp.concatenate([pieces[k].reshape(-1, LANE) for k in widths], axis=0)
    small_ag = _SmallGather(packed, me, "small")
    after, res_big = small_ag.token, {}
    for k, rs in (("w_ffn_out", rs_f2), ("w_ffn_in", rs_f1), ("w_o", rs_o), ("w_in", rs_in)):
        four = _adam_shard(rs.sums(after), W[k][0], M[k][0], V[k][0], "adam_" + k)
        res_big[k] = [a[None] for a in four]
        after = four[0]
    gathered = small_ag.done(after)
    parts, r0 = {}, 0
    for k, (rows, width) in widths.items():
        nr = rows * width // LANE
        parts[k] = gathered[:, r0:r0 + nr, :].reshape(N_DEV, rows, width)
        r0 += nr
    prep_small = lambda d, k: pad_rb(d[k]) if k == "rel_bias" else d[k]
    small = _small_update(parts, parts["loss"], lbv, [prep_small(W, k) for k in SMALL],
                          [prep_small(M, k) for k in SMALL], [prep_small(V, k) for k in SMALL], max_rel)
    loss = small[0].reshape(())
    res = {}
    for idx, k in enumerate(SMALL):
        four = small[1 + 4 * idx:5 + 4 * idx]
        if k == "rel_bias":
            four = [a[:, :RB][None] for a in four]
        res[k] = list(four)

    res.update(res_big)
    dmod_s = lax.dynamic_slice_in_dim(parts["b_ada"].reshape(N_DEV, 6 * D), me * ns_ada, ns_ada, axis=1)
    res["w_ada"] = [a[None] for a in _adam_ada(c_all, dmod_s, w_ada[0], m_w_ada[0], v_w_ada[0])]

    out = [loss, grad_x[None]]
    for field in range(4):
        out += [res[k][field] for k in ORDER]
    return tuple(out)
```

```python
import jax
import jax.numpy as jnp
from jax import lax
from jax.experimental import pallas as pl
from jax.experimental.pallas import tpu as pltpu

F32 = jnp.float32
BF16 = jnp.bfloat16
MESH = pl.DeviceIdType.MESH
HIGHEST = lax.Precision.HIGHEST

N_DEV = 8
CHUNK = 64
N_PAST = 8
QBLK = 4 * CHUNK
KPAD = N_PAST * CHUNK
WIN = KPAD + QBLK
TAB = 1024
ATTN_HEAD_DIM = 64
ATTN_HEADS_PER_STEP = 4
SUB = 32
ROWS = 8
LANE = 128
EPS = 1e-5
ALPHA = 2.0 ** 0.25
ADAM_LR, ADAM_B1, ADAM_B2, ADAM_EPS, ADAM_WD, ADAM_STEP = 0.001, 0.9, 0.999, 1e-08, 0.01, 10
NEG = -1e30
VMEM_LIMIT = 56 * 1024 * 1024


def _sds(shape, dtype):
    return jax.ShapeDtypeStruct(tuple(shape), dtype)


def _tile(n, pref, mult):
    best = None
    for t in range(mult, min(n, pref) + 1, mult):
        if n % t == 0:
            best = t
    return n if best is None else best


def _params(sem=None, big=False):
    kw = {}
    if sem is not None:
        kw["dimension_semantics"] = sem
    if big:
        kw["vmem_limit_bytes"] = VMEM_LIMIT
    return pltpu.CompilerParams(**kw)


def _sigmoid(v):
    return 1.0 / (1.0 + jnp.exp(-v))


def _dot(a, b, dims, precision=None):
    return lax.dot_general(a, b, (dims, ((), ())), preferred_element_type=F32, precision=precision)


NN = ((1,), (0,))
NT = ((1,), (1,))
TN = ((0,), (0,))


def _ln(v):
    mu = jnp.mean(v, axis=-1, keepdims=True)
    d = v - mu
    rstd = lax.rsqrt(jnp.mean(d * d, axis=-1, keepdims=True) + EPS)
    return d * rstd, rstd


def _ln_bwd(dxh, xh, rstd):
    return rstd * (dxh - jnp.mean(dxh, axis=-1, keepdims=True) - xh * jnp.mean(dxh * xh, axis=-1, keepdims=True))


def _colsum(v):
    return jnp.sum(v, axis=0, keepdims=True)


def _ln_mod(x2, mod6):
    T, D = x2.shape
    tm = _tile(T, 256, 8)

    def body(x_ref, mod_ref, o_ref):
        xh, _ = _ln(x_ref[...])
        o_ref[...] = (xh * (1.0 + mod_ref[1:2, :]) + mod_ref[0:1, :]).astype(BF16)

    return pl.pallas_call(
        body, grid=(T // tm,), name="ln_mod",
        in_specs=[pl.BlockSpec((tm, D), lambda i: (i, 0)), pl.BlockSpec((6, D), lambda i: (0, 0))],
        out_specs=pl.BlockSpec((tm, D), lambda i: (i, 0)),
        out_shape=_sds((T, D), BF16), compiler_params=_params(("parallel",)),
    )(x2, mod6)


def _mid_fwd(x2, mix, mod6, ln1_g, ln1_b):
    T, D = x2.shape
    tm = _tile(T, 256, 8)

    def body(x_ref, mix_ref, mod_ref, g_ref, b_ref, x1_ref, h2_ref):
        zh, _ = _ln(ALPHA * x_ref[...] + mod_ref[2:3, :] * mix_ref[...])
        x1 = zh * g_ref[...] + b_ref[...]
        x1_ref[...] = x1
        xh, _ = _ln(x1)
        h2_ref[...] = (xh * (1.0 + mod_ref[4:5, :]) + mod_ref[3:4, :]).astype(BF16)

    row = pl.BlockSpec((tm, D), lambda i: (i, 0))
    vec = pl.BlockSpec((1, D), lambda i: (0, 0))
    return pl.pallas_call(
        body, grid=(T // tm,), name="mid_fwd",
        in_specs=[row, row, pl.BlockSpec((6, D), lambda i: (0, 0)), vec, vec],
        out_specs=[row, row],
        out_shape=[_sds((T, D), F32), _sds((T, D), BF16)], compiler_params=_params(("parallel",)),
    )(x2, mix, mod6, ln1_g, ln1_b)


def _final(x1, ff, mod6, ln2_g, ln2_b, tgt):
    T, D = x1.shape
    tm = _tile(T, 256, 8)

    def body(x1_ref, ff_ref, mod_ref, g_ref, b_ref, t_ref, dff_ref, dx1_ref, vec_ref):
        @pl.when(pl.program_id(0) == 0)
        def _():
            vec_ref[...] = jnp.zeros_like(vec_ref)

        ff_v = ff_ref[...]
        gate2 = mod_ref[5:6, :]
        zh, rstd = _ln(ALPHA * x1_ref[...] + gate2 * ff_v)
        err = zh * g_ref[...] + b_ref[...] - t_ref[...]
        dy = err * (1.0 / D)
        dz = _ln_bwd(dy * g_ref[...], zh, rstd)
        dff_ref[...] = (gate2 * dz).astype(BF16)
        dx1_ref[...] = ALPHA * dz
        vec_ref[0:1, :] += _colsum(dy * zh)
        vec_ref[1:2, :] += _colsum(dy)
        vec_ref[2:3, :] += _colsum(dz * ff_v)
        vec_ref[3:4, :] += _colsum(err * err) * (0.5 / D)

    row = pl.BlockSpec((tm, D), lambda i: (i, 0))
    vec = pl.BlockSpec((1, D), lambda i: (0, 0))
    return pl.pallas_call(
        body, grid=(T // tm,), name="final_fwd_bwd",
        in_specs=[row, row, pl.BlockSpec((6, D), lambda i: (0, 0)), vec, vec, row],
        out_specs=[row, row, pl.BlockSpec((8, D), lambda i: (0, 0))],
        out_shape=[_sds((T, D), BF16), _sds((T, D), F32), _sds((8, D), F32)],
        compiler_params=_params(("arbitrary",)),
    )(x1, ff, mod6, ln2_g, ln2_b, tgt)


def _mid_bwd(x2, mix, x1, dx1a, dh2, mod6, ln1_g):
    T, D = x2.shape
    tm = _tile(T, 256, 8)

    def body(x_ref, mix_ref, x1_ref, dx1a_ref, dh2_ref, mod_ref, g_ref, dmix_ref, dxa_ref, vec_ref):
        @pl.when(pl.program_id(0) == 0)
        def _():
            vec_ref[...] = jnp.zeros_like(vec_ref)

        dh2 = dh2_ref[...]
        xh, rstd = _ln(x1_ref[...])
        dx1 = dx1a_ref[...] + _ln_bwd(dh2 * (1.0 + mod_ref[4:5, :]), xh, rstd)
        mix_v = mix_ref[...]
        gate1 = mod_ref[2:3, :]
        zh, rstdz = _ln(ALPHA * x_ref[...] + gate1 * mix_v)
        dz = _ln_bwd(dx1 * g_ref[...], zh, rstdz)
        dmix_ref[...] = (gate1 * dz).astype(BF16)
        dxa_ref[...] = ALPHA * dz
        vec_ref[0:1, :] += _colsum(dh2 * xh)
        vec_ref[1:2, :] += _colsum(dh2)
        vec_ref[2:3, :] += _colsum(dx1 * zh)
        vec_ref[3:4, :] += _colsum(dx1)
        vec_ref[4:5, :] += _colsum(dz * mix_v)

    row = pl.BlockSpec((tm, D), lambda i: (i, 0))
    vec = pl.BlockSpec((1, D), lambda i: (0, 0))
    return pl.pallas_call(
        body, grid=(T // tm,), name="mid_bwd",
        in_specs=[row, row, row, row, row, pl.BlockSpec((6, D), lambda i: (0, 0)), vec],
        out_specs=[row, row, pl.BlockSpec((8, D), lambda i: (0, 0))],
        out_shape=[_sds((T, D), BF16), _sds((T, D), F32), _sds((8, D), F32)],
        compiler_params=_params(("arbitrary",)),
    )(x2, mix, x1, dx1a, dh2, mod6, ln1_g)


def _first_bwd(x2, dh1, dxa, mod6):
    T, D = x2.shape
    tm = _tile(T, 256, 8)

    def body(x_ref, dh1_ref, dxa_ref, mod_ref, gx_ref, vec_ref):
        @pl.when(pl.program_id(0) == 0)
        def _():
            vec_ref[...] = jnp.zeros_like(vec_ref)

        dh1 = dh1_ref[...]
        xh, rstd = _ln(x_ref[...])
        gx_ref[...] = dxa_ref[...] + _ln_bwd(dh1 * (1.0 + mod_ref[1:2, :]), xh, rstd)
        vec_ref[0:1, :] += _colsum(dh1 * xh)
        vec_ref[1:2, :] += _colsum(dh1)

    row = pl.BlockSpec((tm, D), lambda i: (i, 0))
    return pl.pallas_call(
        body, grid=(T // tm,), name="first_bwd",
        in_specs=[row, row, row, pl.BlockSpec((6, D), lambda i: (0, 0))],
        out_specs=[row, pl.BlockSpec((8, D), lambda i: (0, 0))],
        out_shape=[_sds((T, D), F32), _sds((8, D), F32)],
        compiler_params=_params(("arbitrary",)),
    )(x2, dh1, dxa, mod6)


def _slot(j):
    return (j % 2) * 4 + j // 2


def _mm_gathered(a, wg, shards, out, name):
    M, K = a.shape
    _, _, ns = wg.shape
    tm = _tile(M, 512, 16)

    def body(shards_ref, a_ref, w_ref, prev_ref, o_ref):
        o_ref[...] = _dot(a_ref[...], w_ref[...], NN)

    return pl.pallas_call(
        body, name=name,
        grid_spec=pltpu.PrefetchScalarGridSpec(
            num_scalar_prefetch=1, grid=(shards.shape[0], M // tm),
            in_specs=[pl.BlockSpec((tm, K), lambda j, i, s: (i, 0)),
                      pl.BlockSpec((None, K, ns), lambda j, i, s: (s[j], 0, 0)), ORDER_ONLY],
            out_specs=pl.BlockSpec((tm, ns), lambda j, i, s: (i, s[j]))),
        out_shape=_sds((M, N_DEV * ns), F32), input_output_aliases={3: 0},
        compiler_params=_params(("parallel", "parallel"), big=True),
    )(shards, a, wg, out)


def _mm_nn(a, b, name):
    M, K = a.shape
    _, N = b.shape
    tm, tn = _tile(M, 512, 16), _tile(N, 1024, LANE)

    def body(a_ref, b_ref, o_ref):
        o_ref[...] = _dot(a_ref[...], b_ref[...], NN)

    return pl.pallas_call(
        body, grid=(N // tn, M // tm), name=name,
        in_specs=[pl.BlockSpec((tm, K), lambda j, i: (i, 0)), pl.BlockSpec((K, tn), lambda j, i: (0, j))],
        out_specs=pl.BlockSpec((tm, tn), lambda j, i: (i, j)),
        out_shape=_sds((M, N), F32), compiler_params=_params(("parallel", "parallel"), big=True),
    )(a, b)


def _mm_nt(a, b, name):
    M, K = a.shape
    N, _ = b.shape
    tm, tn = _tile(M, 512, 16), _tile(N, 1024, LANE)

    def body(a_ref, b_ref, o_ref):
        o_ref[...] = _dot(a_ref[...], b_ref[...], NT)

    return pl.pallas_call(
        body, grid=(M // tm, N // tn), name=name,
        in_specs=[pl.BlockSpec((tm, K), lambda i, j: (i, 0)), pl.BlockSpec((tn, K), lambda i, j: (j, 0))],
        out_specs=pl.BlockSpec((tm, tn), lambda i, j: (i, j)),
        out_shape=_sds((M, N), F32), compiler_params=_params(("parallel", "parallel"), big=True),
    )(a, b)


def _mm_swiglu(h2, wg, first, count, outs, dep, name):
    M, K = h2.shape
    _, _, ns = wg.shape
    half = N_DEV // 2
    tm = _tile(M, 256, 16)

    def body(a_ref, wgate_ref, wup_ref, prev_gu_ref, prev_act_ref, dep_ref, gu_ref, act_ref):
        a = a_ref[...]
        g = _dot(a, wgate_ref[...], NN)
        u = _dot(a, wup_ref[...], NN)
        sg = _sigmoid(g)
        silu = g * sg
        gu_ref[0] = u * (sg * (1.0 + g * (1.0 - sg)))
        gu_ref[1] = silu
        act_ref[...] = (silu * u).astype(BF16)

    return pl.pallas_call(
        body, grid=(count, M // tm), name=name,
        in_specs=[pl.BlockSpec((tm, K), lambda j, i: (i, 0)),
                  pl.BlockSpec((None, K, ns), lambda j, i: (j + first, 0, 0)),
                  pl.BlockSpec((None, K, ns), lambda j, i: (j + first + half, 0, 0))] + [ORDER_ONLY] * 3,
        out_specs=[pl.BlockSpec((2, tm, ns), lambda j, i: (0, i, j + first)),
                   pl.BlockSpec((tm, ns), lambda j, i: (i, j + first))],
        out_shape=[_sds((2, M, half * ns), F32), _sds((M, half * ns), BF16)],
        input_output_aliases={3: 0, 4: 1},
        compiler_params=_params(("parallel", "parallel"), big=True),
    )(h2, wg, wg, *outs, dep)


def _mm_swiglu_bwd(dff, w2, gu):
    M, K = dff.shape
    F = w2.shape[0]
    tm, tn = _tile(M, 512, 16), _tile(F, 1408, LANE)

    def body(a_ref, b_ref, gu_ref, du_ref):
        da = _dot(a_ref[...], b_ref[...], NT)
        du_ref[0] = (da * gu_ref[0]).astype(BF16)
        du_ref[1] = (da * gu_ref[1]).astype(BF16)

    return pl.pallas_call(
        body, grid=(F // tn, M // tm), name="ffn_out_bwd_swiglu",
        in_specs=[pl.BlockSpec((tm, K), lambda j, i: (i, 0)), pl.BlockSpec((tn, K), lambda j, i: (j, 0)),
                  pl.BlockSpec((2, tm, tn), lambda j, i: (0, i, j))],
        out_specs=pl.BlockSpec((2, tm, tn), lambda j, i: (0, i, j)),
        out_shape=_sds((2, M, F), BF16), compiler_params=_params(("parallel", "parallel"), big=True),
    )(dff, w2, gu)


ORDER_ONLY = pl.BlockSpec(memory_space=pl.ANY)


def _mm_tn_rows(dep, a, b, rs, name):
    M, Ka = a.shape
    _, N = b.shape

    def body(_, a_ref, b_ref, o_ref):
        g = _dot(a_ref[...], b_ref[...], TN)
        o_ref[0, 0] = g[0:rs, :].astype(BF16)
        o_ref[1, 0] = g[rs:2 * rs, :].astype(BF16)

    return pl.pallas_call(
        body, grid=(N_DEV // 2,), name=name,
        in_specs=[ORDER_ONLY, pl.BlockSpec((M, 2 * rs), lambda ch: (0, ch)), pl.BlockSpec((M, N), lambda ch: (0, 0))],
        out_specs=pl.BlockSpec((2, 1, rs, N), lambda ch: (0, ch, 0, 0)),
        out_shape=_sds((2, N_DEV // 2, rs, N), BF16),
        compiler_params=_params(("parallel",), big=True),
    )(dep, a, b)


def _mm_gathered_nt(dep, a, a_spec, wg, M, tm, name, first=0, count=None, out=None):
    _, K, ns = wg.shape
    count = M // tm if count is None else count
    out = lax.empty((M, K), F32) if out is None else out

    def body(_, a_ref, w_ref, prev_ref, o_ref):
        @pl.when(pl.program_id(1) == 0)
        def _():
            o_ref[...] = jnp.zeros_like(o_ref)

        o_ref[...] += _dot(a_ref[:, 0:ns], w_ref[0], NT) + _dot(a_ref[:, ns:2 * ns], w_ref[1], NT)

    return pl.pallas_call(
        body, grid=(count, N_DEV // 2), name=name,
        in_specs=[ORDER_ONLY, a_spec(tm, 2 * ns, first), pl.BlockSpec((2, K, ns), lambda i, p: (p, 0, 0)), ORDER_ONLY],
        out_specs=pl.BlockSpec((tm, K), lambda i, j: (i + first, 0)),
        out_shape=_sds((M, K), F32), input_output_aliases={3: 0},
        compiler_params=_params(("parallel", "arbitrary"), big=True),
    )(dep, a, wg, out)


def _mm_tn_gathered(dep, h, a, a_spec, ns, name):
    M, K = h.shape

    def body(_, h_ref, a_ref, o_ref):
        o_ref[...] = _dot(h_ref[...], a_ref[...], TN).astype(BF16)

    return pl.pallas_call(
        body, grid=(N_DEV,), name=name,
        in_specs=[ORDER_ONLY, pl.BlockSpec((M, K), lambda j: (0, 0)), a_spec(M, ns)],
        out_specs=pl.BlockSpec((None, K, ns), lambda j: (_slot(j), 0, 0)),
        out_shape=_sds((N_DEV, K, ns), BF16),
        compiler_params=_params(("parallel",), big=True),
    )(dep, h, a)


def _bias_onehot(rbp, max_rel):
    r = lax.broadcasted_iota(jnp.int32, (rbp, TAB), 0)
    m = lax.broadcasted_iota(jnp.int32, (rbp, TAB), 1)
    dist = KPAD - jnp.where(m < WIN, m, m - TAB)
    return (r == jnp.clip(dist, -max_rel, max_rel) + max_rel).astype(F32)


def _attn_setup(i, hp, k_ref, v_ref, gv_ref, kpad, vpad, bias):
    ls = slice(i * ATTN_HEAD_DIM, (i + 1) * ATTN_HEAD_DIM)
    kpad[i][0:KPAD, :] = jnp.zeros((KPAD, ATTN_HEAD_DIM), BF16)
    vpad[i][0:KPAD, :] = jnp.zeros((KPAD, ATTN_HEAD_DIM), BF16)
    kpad[i][KPAD:, :] = k_ref[:, ls].astype(BF16)
    vpad[i][KPAD:, :] = v_ref[:, ls].astype(BF16)
    gvrow = gv_ref[pl.ds(hp * ATTN_HEADS_PER_STEP + i, 1), :]
    tab = pltpu.roll(jnp.broadcast_to(gvrow, (QBLK, TAB)), 0, 1, stride=1, stride_axis=0)
    row = lax.broadcasted_iota(jnp.int32, (QBLK, WIN), 0)
    col = lax.broadcasted_iota(jnp.int32, (QBLK, WIN), 1)
    first = jnp.bitwise_and(row, -CHUNK)
    seen = jnp.logical_and(col >= first, col < first + (N_PAST + 1) * CHUNK)
    bias[i][...] = jnp.where(seen, tab[:, 0:WIN], NEG)


def _attn_probs(b, q_ref, kpad, vpad, bias, col):
    pair = range(ATTN_HEADS_PER_STEP)
    ls = [slice(i * ATTN_HEAD_DIM, (i + 1) * ATTN_HEAD_DIM) for i in pair]
    r0 = pl.multiple_of(b * QBLK, QBLK)
    q = [q_ref[pl.ds(r0, QBLK), ls[i]].astype(BF16) for i in pair]
    kw = [kpad[i][pl.ds(r0, WIN), :] for i in pair]
    vw = [vpad[i][pl.ds(r0, WIN), :] for i in pair]
    s = [_dot(q[i], kw[i], NT) * (ATTN_HEAD_DIM ** -0.5) + bias[i][...] for i in pair]
    s = [jnp.where(col >= KPAD - r0, s[i], NEG) for i in pair]
    p = [jnp.exp(s[i] - jnp.max(s[i], axis=-1, keepdims=True)) for i in pair]
    pn = [p[i] / jnp.sum(p[i], axis=-1, keepdims=True) for i in pair]
    return r0, ls, q, kw, vw, pn


def _attn_fwd(proj, gv, ga, AW):
    T = proj.shape[0]
    AH = ATTN_HEADS_PER_STEP
    W = AH * ATTN_HEAD_DIM
    HP = AW // W

    def body(q_ref, k_ref, v_ref, gv_ref, ga_ref, o_ref, *scratch):
        kpad, vpad, bias = (scratch[k * AH:(k + 1) * AH] for k in range(3))
        hp = pl.program_id(0)
        for i in range(AH):
            _attn_setup(i, hp, k_ref, v_ref, gv_ref, kpad, vpad, bias)
        col = lax.broadcasted_iota(jnp.int32, (QBLK, WIN), 1)

        def block(b, carry):
            pair = range(AH)
            r0, ls, _, _, vw, pn = _attn_probs(b, q_ref, kpad, vpad, bias, col)
            o = [_dot(pn[i].astype(BF16), vw[i], NN) for i in pair]
            r = [lax.rsqrt(jnp.mean(o[i] * o[i], axis=-1, keepdims=True) + EPS) for i in pair]
            outs = [o[i] * r[i] * ga_ref[0:1, ls[i]] for i in pair]
            o_ref[pl.ds(r0, QBLK), :] = jnp.concatenate(outs, axis=1).astype(BF16)
            return carry

        lax.fori_loop(0, T // QBLK, block, 0)

    blk = lambda off: pl.BlockSpec((T, W), lambda hp: (0, off + hp))
    return pl.pallas_call(
        body, grid=(HP,), name="attn_fwd",
        in_specs=[blk(0), blk(HP), blk(2 * HP), pl.BlockSpec(gv.shape, lambda hp: (0, 0)),
                  pl.BlockSpec((1, W), lambda hp: (0, hp))],
        out_specs=pl.BlockSpec((T, W), lambda hp: (0, hp)),
        out_shape=_sds((T, AW), BF16),
        scratch_shapes=[pltpu.VMEM((T + KPAD, ATTN_HEAD_DIM), BF16)] * (2 * AH) + [pltpu.VMEM((QBLK, WIN), F32)] * AH,
        compiler_params=_params(("parallel",), big=True),
    )(proj, proj, proj, gv, ga)


def _attn_bwd(proj, dmixin, gv, ga, AW):
    T = proj.shape[0]
    AH = ATTN_HEADS_PER_STEP
    W = AH * ATTN_HEAD_DIM
    HP = AW // W
    scale = ATTN_HEAD_DIM ** -0.5

    def body(q_ref, k_ref, v_ref, dn_ref, gv_ref, ga_ref, dq_ref, dk_ref, dv_ref, dgv_ref, dga_ref, *scratch):
        kpad, vpad, dkacc, dvacc, bias, dbias = (scratch[k * AH:(k + 1) * AH] for k in range(6))
        hp = pl.program_id(0)
        for i in range(AH):
            _attn_setup(i, hp, k_ref, v_ref, gv_ref, kpad, vpad, bias)
            dkacc[i][...] = jnp.zeros_like(dkacc[i])
            dvacc[i][...] = jnp.zeros_like(dvacc[i])
            dbias[i][...] = jnp.zeros_like(dbias[i])
        dga_ref[...] = jnp.zeros_like(dga_ref)
        col = lax.broadcasted_iota(jnp.int32, (QBLK, WIN), 1)

        def block(b, carry):
            pair = range(AH)
            r0, lss, qs, kws, vws, pns = _attn_probs(b, q_ref, kpad, vpad, bias, col)
            pn_b = [pns[i].astype(BF16) for i in pair]
            o = [_dot(pn_b[i], vws[i], NN) for i in pair]
            r = [lax.rsqrt(jnp.mean(o[i] * o[i], axis=-1, keepdims=True) + EPS) for i in pair]
            dn = [dn_ref[pl.ds(r0, QBLK), lss[i]] for i in pair]
            for i in pair:
                dga_ref[i:i + 1, :] += _colsum(dn[i] * o[i] * r[i])
            a = [dn[i] * ga_ref[0:1, lss[i]] for i in pair]
            do_b = [(r[i] * (a[i] - o[i] * (r[i] * r[i]) * jnp.mean(a[i] * o[i], axis=-1, keepdims=True))).astype(BF16)
                    for i in pair]
            dp = [_dot(do_b[i], vws[i], NT) for i in pair]
            for i in pair:
                dvacc[i][pl.ds(r0, WIN), :] += _dot(pn_b[i], do_b[i], TN)
            ds = [pns[i] * (dp[i] - jnp.sum(pns[i] * dp[i], axis=-1, keepdims=True)) for i in pair]
            for i in pair:
                dbias[i][...] += ds[i]
            ds_b = [ds[i].astype(BF16) for i in pair]
            dq = [_dot(ds_b[i], kws[i], NN) * scale for i in pair]
            dq_ref[pl.ds(r0, QBLK), :] = jnp.concatenate(dq, axis=1).astype(BF16)
            for i in pair:
                dkacc[i][pl.ds(r0, WIN), :] += _dot(ds_b[i], qs[i], TN) * scale
            return carry

        lax.fori_loop(0, T // QBLK, block, 0)

        rr = lax.broadcasted_iota(jnp.int32, (QBLK, QBLK), 0)
        cc = lax.broadcasted_iota(jnp.int32, (QBLK, QBLK), 1)
        flip = (rr + cc == QBLK - 1).astype(BF16)
        for i in range(AH):
            ls = slice(i * ATTN_HEAD_DIM, (i + 1) * ATTN_HEAD_DIM)
            dk_ref[:, ls] = dkacc[i][KPAD:, :].astype(BF16)
            dv_ref[:, ls] = dvacc[i][KPAD:, :].astype(BF16)
            full = jnp.concatenate([dbias[i][...], jnp.zeros((QBLK, TAB - WIN), F32)], axis=1)
            hi = full.astype(BF16)
            lo = (full - hi.astype(F32)).astype(BF16)
            rev = _dot(flip, hi, NN) + _dot(flip, lo, NN)
            dgv_ref[i:i + 1, :] = _colsum(pltpu.roll(rev, TAB - (QBLK - 1), 1, stride=1, stride_axis=0))

    blk = lambda off: pl.BlockSpec((T, W), lambda hp: (0, off + hp))
    accs = lambda dt: [pltpu.VMEM((T + KPAD, ATTN_HEAD_DIM), dt)] * AH
    return pl.pallas_call(
        body, grid=(HP,), name="attn_bwd",
        in_specs=[blk(0), blk(HP), blk(2 * HP), blk(0), pl.BlockSpec(gv.shape, lambda hp: (0, 0)),
                  pl.BlockSpec((1, W), lambda hp: (0, hp))],
        out_specs=[blk(0), blk(0), blk(0), pl.BlockSpec((None, AH, TAB), lambda hp: (hp, 0, 0)),
                   pl.BlockSpec((None, AH, ATTN_HEAD_DIM), lambda hp: (hp, 0, 0))],
        out_shape=[_sds((T, AW), BF16), _sds((T, AW), BF16), _sds((T, AW), BF16),
                   _sds((HP, AH, TAB), F32), _sds((HP, AH, ATTN_HEAD_DIM), F32)],
        scratch_shapes=accs(BF16) + accs(BF16) + accs(F32) + accs(F32) + [pltpu.VMEM((QBLK, WIN), F32)] * (2 * AH),
        compiler_params=_params(("parallel",), big=True),
    )(proj, proj, proj, dmixin, gv, ga)


def _ltri():
    r = lax.broadcasted_iota(jnp.int32, (CHUNK, CHUNK), 0)
    c = lax.broadcasted_iota(jnp.int32, (CHUNK, CHUNK), 1)
    return (c <= r).astype(BF16)


def _tri_dot(tri, v, dims):
    hi = v.astype(BF16)
    lo = (v - hi.astype(F32)).astype(BF16)
    return _dot(tri, hi, dims) + _dot(tri, lo, dims)


HEADS_PER_STEP = (8, 2)
REC_ROW_TILE = 512


def _alternate(stages):
    live = list(stages)
    while live:
        for g in list(live):
            if next(g, StopIteration) is StopIteration:
                live.remove(g)


def _hgrn_gates(n, ls, q_ref, f_ref, lb_ref, ltri):
    r0 = pl.multiple_of(n * CHUNK, CHUNK)
    rows = pl.ds(r0, CHUNK)
    lb = lb_ref[:, ls]
    qb = q_ref[rows, ls]
    sg = _sigmoid(f_ref[rows, ls])
    f = lb + (1.0 - lb) * sg
    sq = _sigmoid(qb)
    b = _tri_dot(ltri, jnp.log(f), NN)
    return rows, lb, qb, sg, f, 1.0 - f, sq, qb * sq, b


def _hgrn_specs(T, RW, AW, backward):
    HG = HEADS_PER_STEP[1 if backward else 0]
    W = HG * LANE
    TT = _tile(T, REC_ROW_TILE, CHUNK)
    n_row_tiles = T // TT
    base = 3 * AW // W
    row = (lambda t: n_row_tiles - 1 - t) if backward else (lambda t: t)
    blk_in = lambda off: pl.BlockSpec((TT, W), lambda g, t: (row(t), base + off + g))
    col = pl.BlockSpec((TT, W), lambda g, t: (row(t), g))
    states = pl.BlockSpec((HG, TT // CHUNK, LANE, LANE), lambda g, t: (g, row(t), 0, 0))
    return HG, W, RW // W, TT, n_row_tiles, blk_in, col, states


def _hgrn_fwd(proj, lb, gn, AW, RW):
    T = proj.shape[0]
    RH, NC, NSUB = RW // LANE, T // CHUNK, CHUNK // SUB
    HG, W, NG, TT, n_row_tiles, blk_in, col, states = _hgrn_specs(T, RW, AW, False)

    def body(q_ref, f_ref, i_ref, g_ref, lb_ref, gn_ref, mix_ref, o_ref, stall_ref, st_all, bs_all, kks_all, ics_all):
        @pl.when(pl.program_id(1) == 0)
        def _():
            st_all[...] = jnp.zeros_like(st_all)

        ltri = _ltri()
        rowi = lax.broadcasted_iota(jnp.int32, (SUB, 1), 0)

        def one_head(h, n):
            ls = slice(h * LANE, (h + 1) * LANE)
            st, bs, kks, ics = st_all.at[h], bs_all.at[h], kks_all.at[h], ics_all.at[h]
            rows, _, _, _, _, kk, _, qs, b = _hgrn_gates(n, ls, q_ref, f_ref, lb_ref, ltri)
            ic = i_ref[rows, ls]
            stv = st[...]
            stall_ref[h, n] = stv
            bs[...] = b
            kks[...] = kk
            ics[...] = ic
            yield
            o = _dot((qs * jnp.exp(b)).astype(BF16), stv.astype(BF16), NT)
            yield
            ic_b = ic.astype(BF16)
            pieces = []
            for blk in range(NSUB):
                s0 = blk * SUB
                bI, qI = b[s0:s0 + SUB], qs[s0:s0 + SUB]
                if blk == 0:
                    oI = jnp.zeros((SUB, LANE), F32)
                else:
                    ref = bs[s0 - 1:s0, :]
                    qt = (qI * jnp.exp(bI - ref)).astype(BF16)
                    kt = (kk[0:s0] * jnp.exp(ref - b[0:s0])).astype(BF16)
                    oI = _dot(_dot(qt, kt, NT).astype(BF16), ic_b[0:s0], NN)
                    yield
                acc = [oI[g * ROWS:(g + 1) * ROWS] for g in range(SUB // ROWS)]
                for s in range(SUB):
                    sr = s0 + s
                    g0 = s // ROWS
                    lo = g0 * ROWS
                    e = jnp.exp(jnp.minimum(bI[lo:] - bs[sr:sr + 1, :], 0.0))
                    a = jnp.sum(qI[lo:] * kks[sr:sr + 1, :] * e, axis=-1, keepdims=True)
                    add = jnp.where(rowi[lo:] >= s, a, 0.0) * ics[sr:sr + 1, :]
                    for g in range(g0, SUB // ROWS):
                        acc[g] = acc[g] + add[(g - g0) * ROWS:(g - g0 + 1) * ROWS]
                    yield
                pieces.extend(acc)
            o = o + jnp.concatenate(pieces, axis=0)
            bl = bs[CHUNK - 1:CHUNK, :]
            kd = (kk * jnp.exp(bl - b)).astype(BF16)
            st[...] = stv * jnp.exp(bl) + _dot(ic_b, kd, TN)
            yield
            o_ref[rows, ls] = o
            r = lax.rsqrt(jnp.mean(o * o, axis=-1, keepdims=True) + EPS)
            gb = g_ref[rows, ls]
            mix_ref[rows, ls] = (o * r * gn_ref[...] * (gb * _sigmoid(gb))).astype(BF16)

        def chunk(n, carry):
            _alternate([one_head(h, n) for h in range(HG)])
            return carry

        lax.fori_loop(0, TT // CHUNK, chunk, 0)

    tile = pltpu.VMEM((HG, CHUNK, LANE), F32)
    return pl.pallas_call(
        body, grid=(NG, n_row_tiles), name="hgrn_fwd",
        in_specs=[blk_in(0), blk_in(NG), blk_in(2 * NG), blk_in(3 * NG), pl.BlockSpec((1, W), lambda g, t: (0, g)),
                  pl.BlockSpec((1, LANE), lambda g, t: (0, 0))],
        out_specs=[col, col, states],
        out_shape=[_sds((T, RW), BF16), _sds((T, RW), F32), _sds((RH, NC, LANE, LANE), F32)],
        scratch_shapes=[pltpu.VMEM((HG, LANE, LANE), F32), tile, tile, tile],
        compiler_params=_params(("parallel", "arbitrary"), big=True),
    )(proj, proj, proj, proj, lb, gn)


def _hgrn_bwd(proj, dmixin, o_b, st_all, lb, gn, AW, RW):
    T = proj.shape[0]
    RH, NC, NSUB = RW // LANE, T // CHUNK, CHUNK // SUB
    HG, W, NG, TT, n_row_tiles, blk_in, col, states = _hgrn_specs(T, RW, AW, True)

    def body(q_ref, f_ref, i_ref, g_ref, o_ref, dn_ref, stall_ref, lb_ref, gn_ref,
             dq_ref, df_ref, di_ref, dg_ref, dlb_ref, dgn_ref, dst_all, bs_all, qss_all, dos_all, p2_all, dic_all,
             p1_all):
        @pl.when(pl.program_id(1) == 0)
        def _():
            dst_all[...] = jnp.zeros_like(dst_all)
            dlb_ref[...] = jnp.zeros_like(dlb_ref)
            dgn_ref[...] = jnp.zeros_like(dgn_ref)

        ltri = _ltri()
        rowi = lax.broadcasted_iota(jnp.int32, (SUB, 1), 0)
        last = lax.broadcasted_iota(jnp.int32, (CHUNK, 1), 0) == CHUNK - 1

        def one_head(h, n):
            ls = slice(h * LANE, (h + 1) * LANE)
            dst, bs, qss, dos = dst_all.at[h], bs_all.at[h], qss_all.at[h], dos_all.at[h]
            p2, dic, p1s = p2_all.at[h], dic_all.at[h], p1_all.at[h]
            rows, lbv, qb, sg, f, kk, sq, qs, b = _hgrn_gates(n, ls, q_ref, f_ref, lb_ref, ltri)
            ic = i_ref[rows, ls]
            stv = stall_ref[h, n]
            dstv = dst[...]
            o = o_ref[rows, ls]
            dn = dn_ref[rows, ls]
            gb = g_ref[rows, ls]
            sgb = _sigmoid(gb)
            r = lax.rsqrt(jnp.mean(o * o, axis=-1, keepdims=True) + EPS)
            gnv = gn_ref[...]
            dg_ref[rows, ls] = (dn * (o * r * gnv) * (sgb * (1.0 + gb * (1.0 - sgb)))).astype(BF16)
            dy = dn * (gb * sgb)
            dgn_ref[h] += _colsum(dy * o * r)
            a_ = dy * gnv
            do = r * (a_ - o * (r * r) * jnp.mean(a_ * o, axis=-1, keepdims=True))
            do_b = do.astype(BF16)
            bs[...] = b
            qss[...] = qs
            dos[...] = do
            yield
            ic_b = ic.astype(BF16)
            eb = jnp.exp(b)
            bl = bs[CHUNK - 1:CHUNK, :]
            ebl = jnp.exp(bl)
            dec = jnp.exp(bl - b)
            kd = (kk * dec).astype(BF16)
            dst_b = dstv.astype(BF16)
            dqs = _dot(do_b, stv.astype(BF16), NN) * eb
            dkk2 = _dot(ic_b, dst_b, NN) * dec
            dic[...] = _dot(kd, dst_b, NT)
            dbl = ebl * _colsum(stv * dstv) + _colsum(kk * dkk2)
            dst[...] = dstv * ebl + _dot(do_b, (qs * eb).astype(BF16), TN)
            yield
            p2[...] = jnp.zeros_like(p2)
            p1_pieces = []
            for blk in range(NSUB):
                s0 = blk * SUB
                bI, qI, doI = b[s0:s0 + SUB], qs[s0:s0 + SUB], do[s0:s0 + SUB]
                if blk == 0:
                    p1 = jnp.zeros((SUB, LANE), F32)
                else:
                    ref = bs[s0 - 1:s0, :]
                    eq = jnp.exp(bI - ref)
                    ek = jnp.exp(ref - b[0:s0])
                    qt = (qI * eq).astype(BF16)
                    kt = (kk[0:s0] * ek).astype(BF16)
                    doI_b = doI.astype(BF16)
                    dic[0:s0, :] += _dot(_dot(qt, kt, NT).astype(BF16), doI_b, TN)
                    da = _dot(doI_b, ic_b[0:s0], NT).astype(BF16)
                    p1 = _dot(da, kt, NN) * eq
                    p2[0:s0, :] += _dot(da, qt, TN) * ek
                    yield
                p1_pieces.append(p1)
                kkI, icI = kk[s0:s0 + SUB], ic[s0:s0 + SUB]
                p2acc = [jnp.zeros((ROWS, LANE), F32) for _ in range(SUB // ROWS)]
                diacc = [jnp.zeros((ROWS, LANE), F32) for _ in range(SUB // ROWS)]
                for t in range(SUB):
                    tr = s0 + t
                    ng = t // ROWS + 1
                    hi = ng * ROWS
                    keep = rowi[:hi] <= t
                    do_t = dos[tr:tr + 1, :]
                    e = jnp.exp(jnp.minimum(bs[tr:tr + 1, :] - bI[:hi], 0.0))
                    qe = qss[tr:tr + 1, :] * e
                    a = jnp.where(keep, jnp.sum(kkI[:hi] * qe, axis=-1, keepdims=True), 0.0)
                    da = jnp.where(keep, jnp.sum(icI[:hi] * do_t, axis=-1, keepdims=True), 0.0)
                    dp2, ddi = da * qe, a * do_t
                    for g in range(ng):
                        p2acc[g] = p2acc[g] + dp2[g * ROWS:(g + 1) * ROWS]
                        diacc[g] = diacc[g] + ddi[g * ROWS:(g + 1) * ROWS]
                    p1s[tr:tr + 1, :] = _colsum(da * kkI[:hi] * e)
                    yield
                p2[s0:s0 + SUB, :] += jnp.concatenate(p2acc, axis=0)
                dic[s0:s0 + SUB, :] += jnp.concatenate(diacc, axis=0)
            dqs = dqs + jnp.concatenate(p1_pieces, axis=0) + p1s[...]
            dkk = dkk2 + p2[...]
            db = qs * dqs - kk * dkk + jnp.where(last, dbl, 0.0)
            dgl = _tri_dot(ltri, db, TN)
            yield
            dfv = dgl / f - dkk
            df_ref[rows, ls] = (dfv * (1.0 - lbv) * sg * (1.0 - sg)).astype(BF16)
            dlb_ref[:, ls] += _colsum(dfv * (1.0 - sg))
            dq_ref[rows, ls] = (dqs * (sq * (1.0 + qb * (1.0 - sq)))).astype(BF16)
            di_ref[rows, ls] = dic[...].astype(BF16)

        def chunk(k, carry):
            _alternate([one_head(h, TT // CHUNK - 1 - k) for h in range(HG)])
            return carry

        lax.fori_loop(0, TT // CHUNK, chunk, 0)

    tile = pltpu.VMEM((HG, CHUNK, LANE), F32)
    return pl.pallas_call(
        body, grid=(NG, n_row_tiles), name="hgrn_bwd",
        in_specs=[blk_in(0), blk_in(NG), blk_in(2 * NG), blk_in(3 * NG), col,
                  pl.BlockSpec((TT, W), lambda g, t: (n_row_tiles - 1 - t, AW // W + g)), states,
                  pl.BlockSpec((1, W), lambda g, t: (0, g)), pl.BlockSpec((1, LANE), lambda g, t: (0, 0))],
        out_specs=[col, col, col, col, pl.BlockSpec((1, W), lambda g, t: (0, g)),
                   pl.BlockSpec((HG, 1, LANE), lambda g, t: (g, 0, 0))],
        out_shape=[_sds((T, RW), BF16)] * 4 + [_sds((1, RW), F32), _sds((RH, 1, LANE), F32)],
        scratch_shapes=[pltpu.VMEM((HG, LANE, LANE), F32), tile, tile, tile, tile, tile, tile],
        compiler_params=_params(("parallel", "arbitrary"), big=True),
    )(proj, proj, proj, proj, o_b, dmixin, st_all, lb, gn)


def _prep(c, lb_logits, rb_pad, max_rel, after):
    D, RW = c.shape[-1], lb_logits.shape[-1]
    H, rbp = rb_pad.shape

    def body(c_ref, l_ref, rb_ref, _, __, cact_ref, lb_ref, gv_ref):
        cv = c_ref[...]
        cact_ref[...] = cv * _sigmoid(cv)
        lb_ref[...] = _sigmoid(l_ref[0:1, :] - l_ref[1:2, :])
        gv_ref[...] = _dot(rb_ref[...], _bias_onehot(rbp, max_rel), NN, HIGHEST)

    vmem = pl.BlockSpec(memory_space=pltpu.VMEM)
    return pl.pallas_call(
        body, name="prep", in_specs=[vmem, vmem, vmem, ORDER_ONLY, ORDER_ONLY],
        out_shape=[_sds((1, D), F32), _sds((1, RW), F32), _sds((H, TAB), F32)],
    )(c, lb_logits, rb_pad, *after)


def _mod_part(c_all, w_ada_s, b_ada_s):
    B, D = c_all.shape
    ns = w_ada_s.shape[1]
    tn = _tile(ns, 768, LANE)

    def body(c_ref, w_ref, b_ref, o_ref):
        o_ref[...] = _dot(c_ref[...], w_ref[...], NN) + b_ref[...]

    return pl.pallas_call(
        body, grid=(ns // tn,), name="mod_part",
        in_specs=[pl.BlockSpec((B, D), lambda j: (0, 0)), pl.BlockSpec((D, tn), lambda j: (0, j)),
                  pl.BlockSpec((1, tn), lambda j: (0, j))],
        out_specs=pl.BlockSpec((B, tn), lambda j: (0, j)),
        out_shape=_sds((B, ns), F32), compiler_params=_params(("parallel",)),
    )(c_all, w_ada_s, b_ada_s)


def _adam(w, g, m, v):
    m = ADAM_B1 * m + (1.0 - ADAM_B1) * g
    v = ADAM_B2 * v + (1.0 - ADAM_B2) * (g * g)
    m_hat = m * (1.0 / (1.0 - ADAM_B1 ** ADAM_STEP))
    v_hat = v * (1.0 / (1.0 - ADAM_B2 ** ADAM_STEP))
    return -ADAM_LR * (m_hat / (jnp.sqrt(v_hat) + ADAM_EPS) + ADAM_WD * w), m, v


def _adam_ada(c_all, dmod_s, w, m, v):
    B, D = c_all.shape
    ns = w.shape[1]
    tr, tn = _tile(D, 512, LANE), _tile(ns, 768, LANE)

    def body(c_ref, d_ref, w_ref, m_ref, v_ref, g_out, dw_out, m_out, v_out):
        g = _dot(c_ref[...], d_ref[...], TN)
        g_out[...] = g
        dw_out[...], m_out[...], v_out[...] = _adam(w_ref[...], g, m_ref[...], v_ref[...])

    big = pl.BlockSpec((tr, tn), lambda i, j: (i, j))
    return pl.pallas_call(
        body, grid=(D // tr, ns // tn), name="adam_w_ada",
        in_specs=[pl.BlockSpec((B, tr), lambda i, j: (0, i)), pl.BlockSpec((B, tn), lambda i, j: (0, j)),
                  big, big, big],
        out_specs=[big] * 4, out_shape=[_sds((D, ns), F32)] * 4,
        compiler_params=_params(("parallel", "parallel")),
    )(c_all, dmod_s, w, m, v)


def _adam_shard(parts, w, m, v, name):
    R, C = w.shape
    tr = _tile(R, 256, 16)

    def body(p_ref, w_ref, m_ref, v_ref, g_out, dw_out, m_out, v_out):
        g = p_ref[0].astype(F32)
        for k in range(1, N_DEV // 2):
            g = g + p_ref[k].astype(F32)
        g_out[...] = g
        dw_out[...], m_out[...], v_out[...] = _adam(w_ref[...], g, m_ref[...], v_ref[...])

    big = pl.BlockSpec((tr, C), lambda i: (i, 0))
    return pl.pallas_call(
        body, grid=(R // tr,), name=name,
        in_specs=[pl.BlockSpec((N_DEV // 2, tr, C), lambda i: (0, i, 0)), big, big, big],
        out_specs=[big] * 4, out_shape=[_sds((R, C), F32)] * 4,
        compiler_params=_params(("parallel",), big=True),
    )(parts, w, m, v)


def _pair_sum(g8, land, core, name):
    _, NCHIP, R, C = g8.shape
    tr = _tile(R, 1024, 16)

    def body(core_ref, g_ref, l_ref, o_ref):
        o_ref[...] = g_ref[...] + l_ref[...]

    return pl.pallas_call(
        body, name=name,
        grid_spec=pltpu.PrefetchScalarGridSpec(
            num_scalar_prefetch=1, grid=(NCHIP, R // tr),
            in_specs=[pl.BlockSpec((None, None, tr, C), lambda k, i, core_ref: (core_ref[0], k, i, 0)),
                      pl.BlockSpec((None, tr, C), lambda k, i, core_ref: (k, i, 0))],
            out_specs=pl.BlockSpec((None, tr, C), lambda k, i, core_ref: (k, i, 0))),
        out_shape=_sds((NCHIP, R, C), BF16), compiler_params=_params(("parallel", "parallel")),
    )(core, g8, land)


SMALL = ("b_ada", "rel_bias", "attn_norm_g", "lb_logits", "gnorm_g", "ln1_g", "ln1_b", "ln2_g", "ln2_b")


def _small_update(parts, loss_parts, lbv, ws, ms, vs, max_rel):
    n = len(SMALL)

    def body(*refs):
        part_refs = dict(zip(SMALL, refs[:n]))
        loss_in, lb_ref = refs[n], refs[n + 1]
        w_refs, m_refs, v_refs = refs[n + 2:2 * n + 2], refs[2 * n + 2:3 * n + 2], refs[3 * n + 2:4 * n + 2]
        outs = refs[4 * n + 2:]

        def total(ref):
            tot = ref[0]
            for k in range(1, N_DEV):
                tot = tot + ref[k]
            return tot

        outs[0][...] = jnp.sum(total(loss_in), axis=-1, keepdims=True)
        for idx, name in enumerate(SMALL):
            g = total(part_refs[name])
            if name == "rel_bias":
                g = _dot(g, _bias_onehot(w_refs[idx].shape[1], max_rel), NT, HIGHEST)
            elif name == "lb_logits":
                lb = lb_ref[...]
                sign = (1 - 2 * lax.broadcasted_iota(jnp.int32, (2, 1), 0)).astype(F32)
                g = sign * (g * lb * (1.0 - lb))
            elif name == "gnorm_g":
                g = _colsum(g)
            dw, mm, vv = _adam(w_refs[idx][...], g, m_refs[idx][...], v_refs[idx][...])
            outs[1 + 4 * idx][...] = g
            outs[2 + 4 * idx][...] = dw
            outs[3 + 4 * idx][...] = mm
            outs[4 + 4 * idx][...] = vv

    out_shape = [_sds((1, 1), F32)]
    for w in ws:
        out_shape += [_sds(w.shape, F32)] * 4
    return pl.pallas_call(body, name="small_update", out_shape=out_shape, compiler_params=_params(big=True))(
        *[parts[k] for k in SMALL], loss_parts, lbv, *ws, *ms, *vs)


def _place():
    x, y, c = lax.axis_index("x"), lax.axis_index("y"), lax.axis_index("c")
    return x, y, c, [(1 - x, y), (x, 1 - y), (1 - x, 1 - y)]


def _all_gather(shard, name):
    HBM = pl.BlockSpec(memory_space=pl.ANY)

    def body(x_ref, out_ref, send_sems, recv_sems, local_sem):
        x, y, c, chips = _place()
        me, sibling = (x, y, c), (x, y, 1 - c)

        def slot(px, py, pc):
            return out_ref.at[4 * px + 2 * py + pc]

        def copy(k, block, to, src=None):
            return pltpu.make_async_remote_copy(
                src_ref=slot(*block) if src is None else src, dst_ref=slot(*block),
                send_sem=send_sems.at[k], recv_sem=recv_sems.at[k], device_id=to, device_id_type=MESH)

        mine = pltpu.make_async_copy(x_ref, slot(*me), local_sem)
        mine.start()
        first = [copy(0, me, sibling, src=x_ref)]
        first += [copy(1 + j, me, (*chip, c), src=x_ref) for j, chip in enumerate(chips)]
        for cp in first:
            cp.start()
        passed = [copy(4 + j, (*chip, c), sibling) for j, chip in enumerate(chips)]
        for j, chip in enumerate(chips):
            copy(1 + j, (*chip, c), me).wait_recv()
            passed[j].start()
        copy(0, sibling, me).wait_recv()
        for j, chip in enumerate(chips):
            copy(4 + j, (*chip, 1 - c), me).wait_recv()
        for cp in first + passed:
            cp.wait_send()
        mine.wait()

    return pl.pallas_call(
        body, name=name, out_shape=_sds((N_DEV,) + shard.shape, shard.dtype),
        in_specs=[HBM], out_specs=HBM,
        scratch_shapes=[pltpu.SemaphoreType.DMA((7,)), pltpu.SemaphoreType.DMA((7,)), pltpu.SemaphoreType.DMA(())],
    )(shard)


SEM_SPEC = pl.BlockSpec(memory_space=pltpu.SEMAPHORE)
HBM_SPEC = pl.BlockSpec(memory_space=pltpu.HBM)
EFFECT = pltpu.SideEffectType.DATAFLOW_SIDE_EFFECTING


def _remote(src, dst, send_sems, recv_sems, k, dev):
    return pltpu.make_async_remote_copy(src_ref=src, dst_ref=dst, send_sem=send_sems.at[k], recv_sem=recv_sems.at[k],
                                        device_id=dev, device_id_type=MESH)


def _copy_start(name, bufs, plan, n, after, only=None):
    nb = len(bufs)

    def body(*refs):
        send_sems, recv_sems = refs[nb + 1], refs[nb + 2]
        for k, (src, dst, dev) in enumerate(plan(*refs[:nb])):
            if only is not None and k not in only:
                continue
            _remote(src, dst, send_sems, recv_sems, k, dev).start()
        refs[-1][...] = jnp.zeros_like(refs[-1])

    out = pl.pallas_call(
        body, name=name,
        out_shape=(pltpu.SemaphoreType.DMA((n,)), pltpu.SemaphoreType.DMA((n,)),
                   *[pltpu.HBM(b.shape, b.dtype) for b in bufs], _sds((8, LANE), F32)),
        in_specs=[HBM_SPEC] * nb + [ORDER_ONLY],
        out_specs=(SEM_SPEC, SEM_SPEC, *[HBM_SPEC] * nb, pl.BlockSpec(memory_space=pltpu.VMEM)),
        input_output_aliases={i: 2 + i for i in range(nb)},
        compiler_params=pltpu.CompilerParams(has_side_effects=EFFECT),
    )(*[pltpu.with_memory_space_constraint(b, pltpu.HBM) for b in bufs], after)
    return (out[0], out[1]), list(out[2:2 + nb]), out[-1]


def _copy_wait(name, sems, bufs, plan, after, only=None):
    nb = len(bufs)

    def body(*refs):
        send_sems, recv_sems = refs[nb], refs[nb + 1]
        for k, (src, dst, dev) in enumerate(plan(*refs[:nb])):
            if only is not None and k not in only:
                continue
            cp = _remote(src, dst, send_sems, recv_sems, k, dev)
            cp.wait_send()
            cp.wait_recv()

    out = pl.pallas_call(
        body, name=name, out_shape=tuple(pltpu.HBM(b.shape, b.dtype) for b in bufs),
        in_specs=[HBM_SPEC] * nb + [SEM_SPEC, SEM_SPEC, pl.BlockSpec(memory_space=pl.ANY)],
        out_specs=tuple([HBM_SPEC] * nb), input_output_aliases={i: i for i in range(nb)},
        compiler_params=pltpu.CompilerParams(has_side_effects=EFFECT),
    )(*bufs, sems[0], sems[1], after)
    return list(out)


def _ag_plan_chips(shard_ref, out_ref):
    x, y, c, chips = _place()
    mine = out_ref.at[4 * x + 2 * y + c]
    return [(shard_ref, mine, (x, y, 1 - c))] + [(shard_ref, mine, (*chip, c)) for chip in chips]


def _ag_plan_pass(out_ref):
    x, y, c, chips = _place()
    slots = [out_ref.at[4 * chip[0] + 2 * chip[1] + c] for chip in chips]
    return [(s, s, (x, y, 1 - c)) for s in slots]


def _rs_plan_pair(g_ref, land_ref):
    x, y, c, _ = _place()
    return [(g_ref.at[1 - c], land_ref, (x, y, 1 - c))]


def _rs_plan_chips(p_ref, land_ref):
    x, y, c, chips = _place()
    return [(p_ref.at[2 * chip[0] + chip[1]], land_ref.at[2 * x + y], (*chip, c)) for chip in chips]


class _Gather:
    @staticmethod
    def landing(shard, me):
        return lax.dynamic_update_slice(lax.empty((N_DEV,) + shard.shape, shard.dtype), shard[None],
                                        (me,) + (0,) * shard.ndim)

    def __init__(self, shard, out, tag, after):
        self.tag = tag
        self.sems, (self.shard, self.out), self.token = _copy_start(
            "ag_start_" + tag, [shard, out], _ag_plan_chips, 4, after)
        self.groups = []

    def arrived(self, after, copies):
        name = "ag_wait_%s_%s" % (self.tag, "".join(map(str, copies)))
        self.shard, self.out = _copy_wait(name, self.sems, [self.shard, self.out], _ag_plan_chips, after, copies)
        return self.out

    def pass_on(self, after, blocks):
        name = "ag_pass_%s_%s" % (self.tag, "".join(map(str, blocks)))
        sems, (self.out,), self.pass_token = _copy_start(name, [self.out], _ag_plan_pass, 3, after, blocks)
        self.groups.append((sems, blocks))
        return self.out

    def passed(self, after, group):
        sems, blocks = self.groups[group]
        name = "ag_pass_wait_%s_%s" % (self.tag, "".join(map(str, blocks)))
        self.out = _copy_wait(name, sems, [self.out], _ag_plan_pass, after, blocks)[0]
        return self.out

    def arrived_from_chips(self, after):
        self.arrived(after, (0, 1, 2, 3))
        return self.pass_on(after, (0, 1, 2))

    def passed_on(self, after):
        return self.passed(after, 0)


def _ag_plan_direct(src_ref, out_ref):
    x, y, c, chips = _place()
    mine = out_ref.at[4 * x + 2 * y + c]
    peers = [(x, y, 1 - c)] + [(*chip, pc) for chip in chips for pc in (c, 1 - c)]
    return [(src_ref, mine, peer) for peer in peers]


class _SmallGather:
    def __init__(self, block, me, tag):
        self.tag = tag
        out = lax.dynamic_update_slice(lax.empty((N_DEV,) + block.shape, block.dtype), block[None],
                                       (me,) + (0,) * block.ndim)
        self.sems, self.bufs, self.token = _copy_start(
            "ag_direct_start_" + tag, [block, out], _ag_plan_direct, N_DEV - 1, jnp.zeros((1,), F32))

    def done(self, after):
        return _copy_wait("ag_direct_wait_" + self.tag, self.sems, self.bufs, _ag_plan_direct, after)[1]


class _ReduceScatter:
    def __init__(self, g8, tag):
        self.tag = tag
        land = lax.empty(g8.shape[1:], g8.dtype)
        self.sems, self.bufs, self.token = _copy_start(
            "rs_pair_start_" + tag, [g8, land], _rs_plan_pair, 1, jnp.zeros((1,), F32))

    def pair_done(self, core, chip, after):
        g8, land = _copy_wait("rs_pair_wait_" + self.tag, self.sems, self.bufs, _rs_plan_pair, after)
        p4 = _pair_sum(g8, land, core, "rs_pair_sum_" + self.tag)
        own = lax.dynamic_slice_in_dim(p4, chip, 1, axis=0)
        land2 = lax.dynamic_update_slice(lax.empty(p4.shape, p4.dtype), own, (chip, 0, 0))
        self.sems, self.bufs, self.token = _copy_start(
            "rs_chips_start_" + self.tag, [p4, land2], _rs_plan_chips, 3, jnp.zeros((1,), F32))

    def sums(self, after):
        return _copy_wait("rs_chips_wait_" + self.tag, self.sems, self.bufs, _rs_plan_chips, after)[1]


ORDER = ("w_ada", "b_ada", "w_in", "rel_bias", "attn_norm_g", "lb_logits", "gnorm_g", "w_o", "ln1_g", "ln1_b",
         "w_ffn_in", "w_ffn_out", "ln2_g", "ln2_b")


def kernel(x, c, w_ada, b_ada, w_in, rel_bias, attn_norm_g, lb_logits, gnorm_g, w_o, ln1_g, ln1_b, w_ffn_in, w_ffn_out, ln2_g, ln2_b, loss_target, m_w_ada, m_b_ada, m_w_in, m_rel_bias, m_attn_norm_g, m_lb_logits, m_gnorm_g, m_w_o, m_ln1_g, m_ln1_b, m_w_ffn_in, m_w_ffn_out, m_ln2_g, m_ln2_b, v_w_ada, v_b_ada, v_w_in, v_rel_bias, v_attn_norm_g, v_lb_logits, v_gnorm_g, v_w_o, v_ln1_g, v_ln1_b, v_w_ffn_in, v_w_ffn_out, v_ln2_g, v_ln2_b):
    W = dict(w_ada=w_ada, b_ada=b_ada, w_in=w_in, rel_bias=rel_bias, attn_norm_g=attn_norm_g, lb_logits=lb_logits,
             gnorm_g=gnorm_g, w_o=w_o, ln1_g=ln1_g, ln1_b=ln1_b, w_ffn_in=w_ffn_in, w_ffn_out=w_ffn_out,
             ln2_g=ln2_g, ln2_b=ln2_b)
    M = dict(w_ada=m_w_ada, b_ada=m_b_ada, w_in=m_w_in, rel_bias=m_rel_bias, attn_norm_g=m_attn_norm_g,
             lb_logits=m_lb_logits, gnorm_g=m_gnorm_g, w_o=m_w_o, ln1_g=m_ln1_g, ln1_b=m_ln1_b,
             w_ffn_in=m_w_ffn_in, w_ffn_out=m_w_ffn_out, ln2_g=m_ln2_g, ln2_b=m_ln2_b)
    V = dict(w_ada=v_w_ada, b_ada=v_b_ada, w_in=v_w_in, rel_bias=v_rel_bias, attn_norm_g=v_attn_norm_g,
             lb_logits=v_lb_logits, gnorm_g=v_gnorm_g, w_o=v_w_o, ln1_g=v_ln1_g, ln1_b=v_ln1_b,
             w_ffn_in=v_w_ffn_in, w_ffn_out=v_w_ffn_out, ln2_g=v_ln2_g, ln2_b=v_ln2_b)

    x2, tgt = x[0], loss_target[0]
    T, D = x2.shape
    AW, RW = attn_norm_g.shape[-1], lb_logits.shape[-1]
    MIX = AW + RW
    H, RH = AW // ATTN_HEAD_DIM, RW // LANE
    RB = rel_bias.shape[-1]
    max_rel = (RB - 1) // 2
    rbp = -(-RB // LANE) * LANE
    F = w_ffn_out.shape[1] * N_DEV
    half = N_DEV // 2
    xi, yi, ci = lax.axis_index("x"), lax.axis_index("y"), lax.axis_index("c")
    me = 4 * xi + 2 * yi + ci
    core = jnp.reshape(ci, (1,)).astype(jnp.int32)
    pad_rb = lambda a: jnp.pad(a[0], ((0, 0), (0, rbp - RB)))

    chip = 2 * xi + yi

    w_in_b = w_in[0].astype(BF16)
    w_in_land = _Gather.landing(w_in_b, me)
    c_act, lbv, gv = _prep(c, lb_logits, pad_rb(rel_bias), max_rel, (w_in_b, w_in_land))
    c_all = _all_gather(c_act, "ag_c").reshape(N_DEV, D)
    ns_ada = w_ada.shape[-1]
    mod_part = _mod_part(c_all, w_ada[0], lax.dynamic_slice_in_dim(b_ada, me * ns_ada, ns_ada, axis=1))
    mod_all = _all_gather(mod_part, "ag_mod")
    mod6 = lax.dynamic_index_in_dim(mod_all, me, axis=1, keepdims=False).reshape(6, D)

    bf = lambda w: w[0].astype(BF16)
    ag_in = _Gather(w_in_b, w_in_land, "w_in", mod_all)
    ag_o = _Gather(bf(w_o), _Gather.landing(bf(w_o), me), "w_o", ag_in.token)
    ag_f1 = _Gather(bf(w_ffn_in), _Gather.landing(bf(w_ffn_in), me), "w_ffn_in", ag_o.token)
    ag_f2 = _Gather(bf(w_ffn_out), _Gather.landing(bf(w_ffn_out), me), "w_ffn_out", ag_f1.token)

    h1 = _ln_mod(x2, mod6 + ag_f2.token[0, 0])
    ids = lambda pairs: jnp.stack([4 * px + 2 * py + pc for px, py, pc in pairs]).astype(jnp.int32)
    others = [(1 - xi, yi), (xi, 1 - yi), (1 - xi, 1 - yi)]
    proj = lax.empty((T, w_in.shape[-1] * N_DEV), F32)
    proj = _mm_gathered(h1, ag_in.arrived(h1, (0,)), ids([(xi, yi, ci), (xi, yi, 1 - ci)]), proj, "in_proj_a")
    ag_in.arrived(proj, (1, 2, 3))
    proj = _mm_gathered(h1, ag_in.pass_on(proj, (0, 1, 2)), ids([(*ch, ci) for ch in others]), proj, "in_proj_b")
    wg_in = ag_in.passed(proj, 0)
    proj = _mm_gathered(h1, wg_in, ids([(*ch, 1 - ci) for ch in others]), proj, "in_proj_c")
    ag_o.arrived_from_chips(proj)
    mix_a = _attn_fwd(proj, gv, attn_norm_g, AW)
    wg_o = ag_o.passed_on(mix_a).reshape(MIX, D)
    mix_b, o_b, st_all = _hgrn_fwd(proj, lbv, gnorm_g, AW, RW)
    mixin = jnp.concatenate([mix_a, mix_b], axis=1)
    mix = _mm_nn(mixin, wg_o, "out_proj")
    ag_f1.arrived_from_chips(mix)
    x1, h2 = _mid_fwd(x2, mix, mod6 + ag_f1.pass_token[0, 0], ln1_g, ln1_b)
    wg_f1 = ag_f1.passed_on(h2)
    outs = (lax.empty((2, T, F), F32), lax.empty((T, F), BF16))
    outs = _mm_swiglu(h2, wg_f1, 0, half - 1, outs, ag_f1.token, "ffn_in_swiglu_a")
    passing = ag_f2.arrived_from_chips(outs[1])
    gu, act = _mm_swiglu(h2, wg_f1, half - 1, 1, outs, passing, "ffn_in_swiglu_b")
    wg_f2 = ag_f2.passed_on(act).reshape(F, D)
    ff = _mm_nn(act, wg_f2, "ffn_out")
    dff, dx1a, vec_a = _final(x1, ff, mod6, ln2_g, ln2_b, tgt)

    du = _mm_swiglu_bwd(dff, wg_f2, gu)
    rs_f2 = _ReduceScatter(_mm_tn_rows(dff, act, dff, F // N_DEV, "grad_w_ffn_out"), "w_ffn_out")
    tm = _tile(T, 512, 16)
    du_ij = lambda tm_, w, first: pl.BlockSpec((None, tm_, w), lambda i, p: (p // (half // 2), i + first, p % (half // 2)))
    du_j = lambda rows, ns: pl.BlockSpec((None, rows, ns), lambda j: (j // half, 0, j % half))
    dh2 = _mm_gathered_nt(rs_f2.token, du, du_ij, wg_f1, T, tm, "ffn_in_bwd")
    rs_f2.pair_done(core, chip, dh2)
    gw_f1 = _mm_tn_gathered(rs_f2.token, h2, du, du_j, wg_f1.shape[-1], "grad_w_ffn_in")
    rs_f1 = _ReduceScatter(gw_f1.reshape(2, half, D, -1), "w_ffn_in")
    dmix, dxa, vec_b = _mid_bwd(x2, mix, x1, dx1a, dh2, mod6 + rs_f1.token[0, 0], ln1_g)
    dmixin = _mm_nt(dmix, wg_o, "out_proj_bwd")
    rs_f1.pair_done(core, chip, dmixin)
    rs_o = _ReduceScatter(_mm_tn_rows(rs_f1.token, mixin, dmix, MIX // N_DEV, "grad_w_o"), "w_o")
    dq, dk, dv, dgv, dga = _attn_bwd(proj, dmixin, gv + rs_o.token[0, 0], attn_norm_g, AW)
    rs_o.pair_done(core, chip, dq)
    dqb, dfl, dib, dgb, dlb, dgn = _hgrn_bwd(proj, dmixin, o_b, st_all, lbv + rs_o.token[0, 0], gnorm_g, AW, RW)
    dproj = jnp.concatenate([dq, dk, dv, dqb, dfl, dib, dgb], axis=1)
    p_ij = lambda tm_, w, first: pl.BlockSpec((tm_, w), lambda i, p: (i + first, p))
    p_j = lambda rows, ns: pl.BlockSpec((rows, ns), lambda j: (0, j))
    gw_in = _mm_tn_gathered(rs_o.token, h1, dproj, p_j, wg_in.shape[-1], "grad_w_in")
    rs_in = _ReduceScatter(gw_in.reshape(2, half, D, -1), "w_in")
    n_tiles = T // tm
    n_first = max(1, n_tiles // 4)
    dh1 = _mm_gathered_nt(rs_in.token, dproj, p_ij, wg_in, T, tm, "in_proj_bwd_a", 0, n_first)
    rs_in.pair_done(core, chip, dh1)
    dh1 = _mm_gathered_nt(rs_in.token, dproj, p_ij, wg_in, T, tm, "in_proj_bwd_b", n_first, n_tiles - n_first, dh1)
    grad_x, vec_c = _first_bwd(x2, dh1, dxa, mod6)

    dmod = jnp.concatenate([vec_c[1:2], vec_c[0:1], vec_b[4:5], vec_b[1:2], vec_b[0:1], vec_a[2:3]], axis=0)
    pieces = dict(b_ada=dmod, rel_bias=dgv, attn_norm_g=dga, lb_logits=dlb, gnorm_g=dgn, ln1_g=vec_b[2:3],
                  ln1_b=vec_b[3:4], ln2_g=vec_a[0:1], ln2_b=vec_a[1:2], loss=vec_a[3:4])
    widths = dict(b_ada=(1, 6 * D), rel_bias=(H, TAB), attn_norm_g=(1, AW), lb_logits=(1, RW), gnorm_g=(RH, LANE),
                  ln1_g=(1, D), ln1_b=(1, D), ln2_g=(1, D), ln2_b=(1, D), loss=(1, D))
    packed = jnp.concatenate([pieces[k].reshape(-1, LANE) for k in widths], axis=0)
    small_ag = _SmallGather(packed, me, "small")
    after, res_big = small_ag.token, {}
    for k, rs in (("w_ffn_out", rs_f2), ("w_ffn_in", rs_f1), ("w_o", rs_o), ("w_in", rs_in)):
        four = _adam_shard(rs.sums(after), W[k][0], M[k][0], V[k][0], "adam_" + k)
        res_big[k] = [a[None] for a in four]
        after = four[0]
    gathered = small_ag.done(after)
    parts, r0 = {}, 0
    for k, (rows, width) in widths.items():
        nr = rows * width // LANE
        parts[k] = gathered[:, r0:r0 + nr, :].reshape(N_DEV, rows, width)
        r0 += nr
    prep_small = lambda d, k: pad_rb(d[k]) if k == "rel_bias" else d[k]
    small = _small_update(parts, parts["loss"], lbv, [prep_small(W, k) for k in SMALL],
                          [prep_small(M, k) for k in SMALL], [prep_small(V, k) for k in SMALL], max_rel)
    loss = small[0].reshape(())
    res = {}
    for idx, k in enumerate(SMALL):
        four = small[1 + 4 * idx:5 + 4 * idx]
        if k == "rel_bias":
            four = [a[:, :RB][None] for a in four]
        res[k] = list(four)

    res.update(res_big)
    dmod_s = lax.dynamic_slice_in_dim(parts["b_ada"].reshape(N_DEV, 6 * D), me * ns_ada, ns_ada, axis=1)
    res["w_ada"] = [a[None] for a in _adam_ada(c_all, dmod_s, w_ada[0], m_w_ada[0], v_w_ada[0])]

    out = [loss, grad_x[None]]
    for field in range(4):
        out += [res[k][field] for k in ORDER]
    return tuple(out)
```
